```python
import jax, jax.numpy as jnp
from jax import lax
import numpy as np

D_MODEL = 1024
BATCH = 8
SEQ = 2048
DEPTH = 2

CHUNK = 64
HEAD_DIM = 64
EPS = 1e-6
NEG_INF = -1e30

A_HEADS = 8
A_KV_HEADS = 2
A_WINDOW = 128
A_PREV_CHUNKS = A_WINDOW // CHUNK
A_BAND_CHUNKS = A_PREV_CHUNKS + 1

B_HEADS = 8
B_BLOCK = 128
FORGET_BIAS_INIT = 3.0

C_HEADS = 8
C_PREV_CHUNKS = 8
C_BAND_CHUNKS = C_PREV_CHUNKS + 1
REL_CLIP = 128
N_REL = 2 * REL_CLIP + 1

N_BRANCH = 3
BRANCH_WIDTH = 8 * HEAD_DIM
FFN_HIDDEN = ((-(-8 * D_MODEL // 3)) + 255) // 256 * 256

IN_SPLIT_SIZES = (
    A_HEADS * HEAD_DIM, A_KV_HEADS * HEAD_DIM, A_KV_HEADS * HEAD_DIM,
    B_HEADS * HEAD_DIM, B_HEADS * HEAD_DIM, B_HEADS * HEAD_DIM, B_HEADS,
    C_HEADS * HEAD_DIM, C_HEADS * HEAD_DIM, C_HEADS * HEAD_DIM,
    N_BRANCH * D_MODEL,
)
N_IN_COLS = sum(IN_SPLIT_SIZES)

kernel_name = "chunk_causal_hybrid_swa_fox_relpos_adaln"


def rms_norm(x, g):
    xf = x.astype(jnp.float32)
    y = xf * lax.rsqrt(jnp.mean(xf * xf, axis=-1, keepdims=True) + EPS)
    return (y * g.astype(jnp.float32)).astype(x.dtype)


def modulate(h, shift, scale):
    return h * (1.0 + scale[:, None, :]) + shift[:, None, :]


def alibi_slopes(n_heads):
    return jnp.exp2(-8.0 * jnp.arange(1, n_heads + 1, dtype=jnp.float32) / n_heads)


def sliding_window_sink_attention(q, k, v, sinks):
    b, s, _, dh = q.shape
    nc = s // CHUNK
    g = A_HEADS // A_KV_HEADS
    band_len = A_BAND_CHUNKS * CHUNK
    qc = q.reshape(b, nc, CHUNK, A_KV_HEADS, g, dh)

    def band(t):
        tp = jnp.pad(t, ((0, 0), (A_PREV_CHUNKS * CHUNK, 0), (0, 0), (0, 0)))
        tp = tp.reshape(b, nc + A_PREV_CHUNKS, CHUNK, A_KV_HEADS, dh)
        return jnp.concatenate([tp[:, j:j + nc] for j in range(A_BAND_CHUNKS)], axis=2)

    kb, vb = band(k), band(v)
    scores = jnp.einsum('bnqkgd,bnskd->bnkgqs', qc, kb).astype(jnp.float32) * (dh ** -0.5)
    qi = jnp.arange(CHUNK)
    si = jnp.arange(band_len)
    dist = A_PREV_CHUNKS * CHUNK + qi[:, None] - si[None, :]
    alibi = -alibi_slopes(A_HEADS).reshape(A_KV_HEADS, g, 1, 1) * jnp.abs(dist).astype(jnp.float32)
    key_chunk = jnp.arange(nc)[:, None] - A_PREV_CHUNKS + si[None, :] // CHUNK
    valid = (key_chunk >= 0)[None, :, None, None, None, :]
    scores = jnp.where(valid, scores + alibi, NEG_INF)
    sink_col = jnp.broadcast_to(sinks.astype(jnp.float32).reshape(1, 1, A_KV_HEADS, g, 1, 1),
                                scores.shape[:-1] + (1,))
    probs = jax.nn.softmax(jnp.concatenate([scores, sink_col], axis=-1), axis=-1)[..., :-1]
    out = jnp.einsum('bnkgqs,bnskd->bnqkgd', probs.astype(v.dtype), vb)
    return out.reshape(b, s, A_HEADS * dh)


def forgetting_attention(q, k, v, f_logit):
    b, s, h, dh = q.shape
    log_f = jax.nn.log_sigmoid(f_logit.astype(jnp.float32))
    cum = lax.cumsum(log_f, axis=1).transpose(0, 2, 1)
    kpos = jnp.arange(s)
    scale = dh ** -0.5

    def block(i):
        start = i * B_BLOCK
        qb = lax.dynamic_slice_in_dim(q, start, B_BLOCK, axis=1)
        cq = lax.dynamic_slice_in_dim(cum, start, B_BLOCK, axis=2)
        sc = jnp.einsum('bqhd,bshd->bhqs', qb, k).astype(jnp.float32) * scale
        sc = sc + cq[..., :, None] - cum[:, :, None, :]
        qpos = start + jnp.arange(B_BLOCK)
        sc = jnp.where(kpos[None, :] <= qpos[:, None], sc, NEG_INF)
        p = jax.nn.softmax(sc, axis=-1)
        return jnp.einsum('bhqs,bshd->bqhd', p.astype(v.dtype), v)

    out = lax.map(block, jnp.arange(s // B_BLOCK))
    return out.transpose(1, 0, 2, 3, 4).reshape(b, s, h * dh)


def chunked_relpos_attention(q, k, v, rel_table):
    b, s, h, dh = q.shape
    nc = s // CHUNK
    band_len = C_BAND_CHUNKS * CHUNK
    pad = C_PREV_CHUNKS * CHUNK
    kp = jnp.pad(k, ((0, 0), (pad, 0), (0, 0), (0, 0)))
    vp = jnp.pad(v, ((0, 0), (pad, 0), (0, 0), (0, 0)))
    qi = jnp.arange(CHUNK)
    si = jnp.arange(band_len)
    dist = pad + qi[:, None] - si[None, :]
    rel_idx = jnp.clip(dist, -REL_CLIP, REL_CLIP) + REL_CLIP
    bias = rel_table[:, rel_idx].astype(jnp.float32)
    scale = dh ** -0.5

    def chunk(n):
        qc = lax.dynamic_slice_in_dim(q, n * CHUNK, CHUNK, axis=1)
        kc = lax.dynamic_slice_in_dim(kp, n * CHUNK, band_len, axis=1)
        vc = lax.dynamic_slice_in_dim(vp, n * CHUNK, band_len, axis=1)
        sc = jnp.einsum('bqhd,bshd->bhqs', qc, kc).astype(jnp.float32) * scale + bias
        valid = (n * CHUNK - pad + si) >= 0
        sc = jnp.where(valid[None, None, None, :], sc, NEG_INF)
        p = jax.nn.softmax(sc, axis=-1)
        return jnp.einsum('bhqs,bshd->bqhd', p.astype(vc.dtype), vc)

    out = lax.map(chunk, jnp.arange(nc))
    return out.transpose(1, 0, 2, 3, 4).reshape(b, s, h * dh)


def hybrid_mixer(h, w_in, b_forget, sinks, rel_table, w_branch, w_out):
    b, s, _ = h.shape
    proj = jnp.einsum('bsd,de->bse', h, w_in)
    split_points = [int(p) for p in np.cumsum(IN_SPLIT_SIZES)[:-1]]
    qa, ka, va, qb, kb, vb, fb, qc, kc, vc, gates = jnp.split(proj, split_points, axis=-1)
    heads = lambda t, n: t.reshape(b, s, n, HEAD_DIM)
    o_a = sliding_window_sink_attention(heads(qa, A_HEADS), heads(ka, A_KV_HEADS),
                                        heads(va, A_KV_HEADS), sinks)
    o_b = forgetting_attention(heads(qb, B_HEADS), heads(kb, B_HEADS), heads(vb, B_HEADS),
                               fb + b_forget)
    o_c = chunked_relpos_attention(heads(qc, C_HEADS), heads(kc, C_HEADS), heads(vc, C_HEADS),
                                   rel_table)
    branches = jnp.stack([o_a, o_b, o_c], axis=2)
    y = jnp.einsum('bskw,kwd->bskd', branches, w_branch)
    g = jax.nn.sigmoid(gates.reshape(b, s, N_BRANCH, D_MODEL))
    merged = jnp.sum(g * y, axis=2)
    return jnp.einsum('bsd,de->bse', merged, w_out)


def swiglu(h, w_ffn_in, w_ffn_out):
    u = jnp.einsum('bsd,df->bsf', h, w_ffn_in)
    gate, up = jnp.split(u, 2, axis=-1)
    return jnp.einsum('bsf,fd->bsd', jax.nn.silu(gate) * up, w_ffn_out)


def _fwd_setup_inputs(seed: int = 0) -> dict:
    key = jax.random.key(seed)
    ks = jax.random.split(key, 16)
    f32 = jnp.float32
    nrm = lambda k, shape, sd: jax.random.normal(k, shape, f32) * sd
    return {
        "x": nrm(ks[0], (BATCH, SEQ, D_MODEL), 1.0),
        "c": nrm(ks[1], (BATCH, D_MODEL), 1.0),
        "norm_mix_g": 1.0 + nrm(ks[2], (DEPTH, D_MODEL), 0.02),
        "norm_ffn_g": 1.0 + nrm(ks[3], (DEPTH, D_MODEL), 0.02),
        "w_ada": nrm(ks[4], (DEPTH, D_MODEL, 6 * D_MODEL), 0.5 * D_MODEL ** -0.5),
        "b_ada": nrm(ks[5], (DEPTH, 6 * D_MODEL), 0.02),
        "w_in": nrm(ks[6], (DEPTH, D_MODEL, N_IN_COLS), D_MODEL ** -0.5),
        "b_forget": FORGET_BIAS_INIT + nrm(ks[7], (DEPTH, B_HEADS), 0.5),
        "sinks": nrm(ks[8], (DEPTH, A_HEADS), 0.5),
        "rel_bias": nrm(ks[9], (DEPTH, C_HEADS, N_REL), 0.1),
        "w_branch": nrm(ks[10], (DEPTH, N_BRANCH, BRANCH_WIDTH, D_MODEL), BRANCH_WIDTH ** -0.5),
        "w_out": nrm(ks[11], (DEPTH, D_MODEL, D_MODEL), D_MODEL ** -0.5),
        "w_ffn_in": nrm(ks[12], (DEPTH, D_MODEL, 2 * FFN_HIDDEN), D_MODEL ** -0.5),
        "w_ffn_out": nrm(ks[13], (DEPTH, FFN_HIDDEN, D_MODEL), FFN_HIDDEN ** -0.5),
        "final_norm_g": 1.0 + nrm(ks[14], (D_MODEL,), 0.02),
    }


def _fwd_reference(x, c, norm_mix_g, norm_ffn_g, w_ada, b_ada, w_in, b_forget, sinks, rel_bias,
              w_branch, w_out, w_ffn_in, w_ffn_out, final_norm_g):
    cond = jax.nn.silu(c)
    for l in range(DEPTH):
        mod = jnp.einsum('bd,de->be', cond, w_ada[l]) + b_ada[l]
        sh_m, sc_m, g_m, sh_f, sc_f, g_f = jnp.split(mod, 6, axis=-1)
        h = modulate(rms_norm(x, norm_mix_g[l]), sh_m, sc_m)
        x = x + g_m[:, None, :] * hybrid_mixer(h, w_in[l], b_forget[l], sinks[l], rel_bias[l],
                                               w_branch[l], w_out[l])
        h = modulate(rms_norm(x, norm_ffn_g[l]), sh_f, sc_f)
        x = x + g_f[:, None, :] * swiglu(h, w_ffn_in[l], w_ffn_out[l])
    return rms_norm(x, final_norm_g)


import jax as _jax
import jax.numpy as _jnp

TWIN_FORMAT = 'train_step'
FWD_PARAMS = ['x', 'c', 'norm_mix_g', 'norm_ffn_g', 'w_ada', 'b_ada', 'w_in', 'b_forget', 'sinks', 'rel_bias', 'w_branch', 'w_out', 'w_ffn_in', 'w_ffn_out', 'final_norm_g']
TWIN_WEIGHTS = ['norm_mix_g', 'norm_ffn_g', 'w_ada', 'b_ada', 'w_in', 'b_forget', 'sinks', 'rel_bias', 'w_branch', 'w_out', 'w_ffn_in', 'w_ffn_out', 'final_norm_g']
TWIN_DIFF_INPUT = 'x'
TWIN_INPUTS = ['x', 'c', 'norm_mix_g', 'norm_ffn_g', 'w_ada', 'b_ada', 'w_in', 'b_forget', 'sinks', 'rel_bias', 'w_branch', 'w_out', 'w_ffn_in', 'w_ffn_out', 'final_norm_g', 'loss_target', 'm_norm_mix_g', 'm_norm_ffn_g', 'm_w_ada', 'm_b_ada', 'm_w_in', 'm_b_forget', 'm_sinks', 'm_rel_bias', 'm_w_branch', 'm_w_out', 'm_w_ffn_in', 'm_w_ffn_out', 'm_final_norm_g', 'v_norm_mix_g', 'v_norm_ffn_g', 'v_w_ada', 'v_b_ada', 'v_w_in', 'v_b_forget', 'v_sinks', 'v_rel_bias', 'v_w_branch', 'v_w_out', 'v_w_ffn_in', 'v_w_ffn_out', 'v_final_norm_g']
TWIN_OUTPUTS = ['loss', 'grad_x', 'grad_norm_mix_g', 'grad_norm_ffn_g', 'grad_w_ada', 'grad_b_ada', 'grad_w_in', 'grad_b_forget', 'grad_sinks', 'grad_rel_bias', 'grad_w_branch', 'grad_w_out', 'grad_w_ffn_in', 'grad_w_ffn_out', 'grad_final_norm_g', 'delta_norm_mix_g', 'delta_norm_ffn_g', 'delta_w_ada', 'delta_b_ada', 'delta_w_in', 'delta_b_forget', 'delta_sinks', 'delta_rel_bias', 'delta_w_branch', 'delta_w_out', 'delta_w_ffn_in', 'delta_w_ffn_out', 'delta_final_norm_g', 'new_m_norm_mix_g', 'new_m_norm_ffn_g', 'new_m_w_ada', 'new_m_b_ada', 'new_m_w_in', 'new_m_b_forget', 'new_m_sinks', 'new_m_rel_bias', 'new_m_w_branch', 'new_m_w_out', 'new_m_w_ffn_in', 'new_m_w_ffn_out', 'new_m_final_norm_g', 'new_v_norm_mix_g', 'new_v_norm_ffn_g', 'new_v_w_ada', 'new_v_b_ada', 'new_v_w_in', 'new_v_b_forget', 'new_v_sinks', 'new_v_rel_bias', 'new_v_w_branch', 'new_v_w_out', 'new_v_w_ffn_in', 'new_v_w_ffn_out', 'new_v_final_norm_g']
TWIN_LEAF_KINDS = {'loss': 'loss', 'grad_x': 'grad_x', 'grad_norm_mix_g': 'grad_w', 'grad_norm_ffn_g': 'grad_w', 'grad_w_ada': 'grad_w', 'grad_b_ada': 'grad_w', 'grad_w_in': 'grad_w', 'grad_b_forget': 'grad_w', 'grad_sinks': 'grad_w', 'grad_rel_bias': 'grad_w', 'grad_w_branch': 'grad_w', 'grad_w_out': 'grad_w', 'grad_w_ffn_in': 'grad_w', 'grad_w_ffn_out': 'grad_w', 'grad_final_norm_g': 'grad_w', 'delta_norm_mix_g': 'delta_w', 'delta_norm_ffn_g': 'delta_w', 'delta_w_ada': 'delta_w', 'delta_b_ada': 'delta_w', 'delta_w_in': 'delta_w', 'delta_b_forget': 'delta_w', 'delta_sinks': 'delta_w', 'delta_rel_bias': 'delta_w', 'delta_w_branch': 'delta_w', 'delta_w_out': 'delta_w', 'delta_w_ffn_in': 'delta_w', 'delta_w_ffn_out': 'delta_w', 'delta_final_norm_g': 'delta_w', 'new_m_norm_mix_g': 'new_m', 'new_m_norm_ffn_g': 'new_m', 'new_m_w_ada': 'new_m', 'new_m_b_ada': 'new_m', 'new_m_w_in': 'new_m', 'new_m_b_forget': 'new_m', 'new_m_sinks': 'new_m', 'new_m_rel_bias': 'new_m', 'new_m_w_branch': 'new_m', 'new_m_w_out': 'new_m', 'new_m_w_ffn_in': 'new_m', 'new_m_w_ffn_out': 'new_m', 'new_m_final_norm_g': 'new_m', 'new_v_norm_mix_g': 'new_v', 'new_v_norm_ffn_g': 'new_v', 'new_v_w_ada': 'new_v', 'new_v_b_ada': 'new_v', 'new_v_w_in': 'new_v', 'new_v_b_forget': 'new_v', 'new_v_sinks': 'new_v', 'new_v_rel_bias': 'new_v', 'new_v_w_branch': 'new_v', 'new_v_w_out': 'new_v', 'new_v_w_ffn_in': 'new_v', 'new_v_w_ffn_out': 'new_v', 'new_v_final_norm_g': 'new_v'}


def _forward(args):
    return _fwd_reference(*[args[k] for k in FWD_PARAMS])


def _output_shape():
    out = _jax.eval_shape(lambda: _forward(_fwd_setup_inputs(0)))
    return out.shape, out.dtype

N_MICROBATCH = 1
ADAM_LR = 0.001
ADAM_B1 = 0.9
ADAM_B2 = 0.999
ADAM_EPS = 1e-08
ADAM_WD = 0.01
ADAM_STEP = 10
PER_EXAMPLE_BATCH_AXIS = {'x': 0, 'c': 0, 'loss_target': 0}
SHARED_INPUTS = []
_WEIGHT_DTYPES = {'norm_mix_g': _jnp.float32, 'norm_ffn_g': _jnp.float32, 'w_ada': _jnp.float32, 'b_ada': _jnp.float32, 'w_in': _jnp.float32, 'b_forget': _jnp.float32, 'sinks': _jnp.float32, 'rel_bias': _jnp.float32, 'w_branch': _jnp.float32, 'w_out': _jnp.float32, 'w_ffn_in': _jnp.float32, 'w_ffn_out': _jnp.float32, 'final_norm_g': _jnp.float32}
MOMENT_SCALE = {'norm_mix_g': 1.890725e-02, 'norm_ffn_g': 3.813381e-02, 'w_ada': 3.143873e-02, 'b_ada': 5.140445e-02, 'w_in': 8.628820e-03, 'b_forget': 5.975715e-02, 'sinks': 6.373468e-03, 'rel_bias': 2.364863e-03, 'w_branch': 9.266292e-03, 'w_out': 1.590017e-02, 'w_ffn_in': 1.670360e-02, 'w_ffn_out': 2.727315e-02, 'final_norm_g': 1.601454e+01}


def _to_microbatches(a, axis):
    t = _jnp.moveaxis(a, axis, 0)
    t = t.reshape((N_MICROBATCH, t.shape[0] // N_MICROBATCH) + t.shape[1:])
    return _jnp.moveaxis(t, 1, axis + 1)


def setup_inputs(seed: int = 0) -> dict:
    inp = _fwd_setup_inputs(seed)
    key = _jax.random.fold_in(_jax.random.key(seed), 7919)
    shape, _ = _output_shape()
    out = dict(inp)
    out["loss_target"] = _jax.random.normal(_jax.random.fold_in(key, 0), shape, _jnp.float32)
    for i, name in enumerate(TWIN_WEIGHTS):
        w = inp[name].astype(_jnp.float32)
        if MOMENT_SCALE is None:
            s = _jnp.sqrt(_jnp.mean(_jnp.square(w)) + 1e-30)
        else:
            s = MOMENT_SCALE[name]
        km, kv = _jax.random.split(_jax.random.fold_in(key, i + 1))
        out[name] = w
        out["m_" + name] = s * _jax.random.normal(km, w.shape, _jnp.float32)
        out["v_" + name] = (s * s) * _jax.random.uniform(kv, w.shape, _jnp.float32, 0.5, 1.5)
    if N_MICROBATCH > 1:
        for name, axis in PER_EXAMPLE_BATCH_AXIS.items():
            out[name] = _to_microbatches(out[name], axis)
    return {'x': out['x'], 'c': out['c'], 'norm_mix_g': out['norm_mix_g'], 'norm_ffn_g': out['norm_ffn_g'], 'w_ada': out['w_ada'], 'b_ada': out['b_ada'], 'w_in': out['w_in'], 'b_forget': out['b_forget'], 'sinks': out['sinks'], 'rel_bias': out['rel_bias'], 'w_branch': out['w_branch'], 'w_out': out['w_out'], 'w_ffn_in': out['w_ffn_in'], 'w_ffn_out': out['w_ffn_out'], 'final_norm_g': out['final_norm_g'], 'loss_target': out['loss_target'], 'm_norm_mix_g': out['m_norm_mix_g'], 'm_norm_ffn_g': out['m_norm_ffn_g'], 'm_w_ada': out['m_w_ada'], 'm_b_ada': out['m_b_ada'], 'm_w_in': out['m_w_in'], 'm_b_forget': out['m_b_forget'], 'm_sinks': out['m_sinks'], 'm_rel_bias': out['m_rel_bias'], 'm_w_branch': out['m_w_branch'], 'm_w_out': out['m_w_out'], 'm_w_ffn_in': out['m_w_ffn_in'], 'm_w_ffn_out': out['m_w_ffn_out'], 'm_final_norm_g': out['m_final_norm_g'], 'v_norm_mix_g': out['v_norm_mix_g'], 'v_norm_ffn_g': out['v_norm_ffn_g'], 'v_w_ada': out['v_w_ada'], 'v_b_ada': out['v_b_ada'], 'v_w_in': out['v_w_in'], 'v_b_forget': out['v_b_forget'], 'v_sinks': out['v_sinks'], 'v_rel_bias': out['v_rel_bias'], 'v_w_branch': out['v_w_branch'], 'v_w_out': out['v_w_out'], 'v_w_ffn_in': out['v_w_ffn_in'], 'v_w_ffn_out': out['v_w_ffn_out'], 'v_final_norm_g': out['v_final_norm_g']}


def _loss(weights, diff, rest, loss_target):
    with _jax.named_scope("forward"):
        args = {**rest, TWIN_DIFF_INPUT: diff, **{k: w.astype(_WEIGHT_DTYPES[k]) for k, w in weights.items()}}
        y = _forward(args)
    with _jax.named_scope("loss_head"):
        err = _jnp.square(y.astype(_jnp.float32) - loss_target)
        return 0.5 * _jnp.sum(_jnp.mean(err, axis=-1)) if err.ndim else 0.5 * err


def _adamw(w, g, m, v):
    m = ADAM_B1 * m + (1.0 - ADAM_B1) * g
    v = ADAM_B2 * v + (1.0 - ADAM_B2) * _jnp.square(g)
    m_hat = m / (1.0 - ADAM_B1 ** ADAM_STEP)
    v_hat = v / (1.0 - ADAM_B2 ** ADAM_STEP)
    delta = -ADAM_LR * (m_hat / (_jnp.sqrt(v_hat) + ADAM_EPS) + ADAM_WD * w)
    return delta, m, v


def reference(x, c, norm_mix_g, norm_ffn_g, w_ada, b_ada, w_in, b_forget, sinks, rel_bias, w_branch, w_out, w_ffn_in, w_ffn_out, final_norm_g, loss_target, m_norm_mix_g, m_norm_ffn_g, m_w_ada, m_b_ada, m_w_in, m_b_forget, m_sinks, m_rel_bias, m_w_branch, m_w_out, m_w_ffn_in, m_w_ffn_out, m_final_norm_g, v_norm_mix_g, v_norm_ffn_g, v_w_ada, v_b_ada, v_w_in, v_b_forget, v_sinks, v_rel_bias, v_w_branch, v_w_out, v_w_ffn_in, v_w_ffn_out, v_final_norm_g):
    given = dict(x=x, c=c, norm_mix_g=norm_mix_g, norm_ffn_g=norm_ffn_g, w_ada=w_ada, b_ada=b_ada, w_in=w_in, b_forget=b_forget, sinks=sinks, rel_bias=rel_bias, w_branch=w_branch, w_out=w_out, w_ffn_in=w_ffn_in, w_ffn_out=w_ffn_out, final_norm_g=final_norm_g, loss_target=loss_target, m_norm_mix_g=m_norm_mix_g, m_norm_ffn_g=m_norm_ffn_g, m_w_ada=m_w_ada, m_b_ada=m_b_ada, m_w_in=m_w_in, m_b_forget=m_b_forget, m_sinks=m_sinks, m_rel_bias=m_rel_bias, m_w_branch=m_w_branch, m_w_out=m_w_out, m_w_ffn_in=m_w_ffn_in, m_w_ffn_out=m_w_ffn_out, m_final_norm_g=m_final_norm_g, v_norm_mix_g=v_norm_mix_g, v_norm_ffn_g=v_norm_ffn_g, v_w_ada=v_w_ada, v_b_ada=v_b_ada, v_w_in=v_w_in, v_b_forget=v_b_forget, v_sinks=v_sinks, v_rel_bias=v_rel_bias, v_w_branch=v_w_branch, v_w_out=v_w_out, v_w_ffn_in=v_w_ffn_in, v_w_ffn_out=v_w_ffn_out, v_final_norm_g=v_final_norm_g)
    weights = {n: given[n] for n in TWIN_WEIGHTS}
    shared = {n: given[n] for n in SHARED_INPUTS}
    per_example = {n: given[n] for n in ['x', 'c']}
    grad_fn = _jax.value_and_grad(_loss, argnums=(0, 1))

    def one_microbatch(ex, loss_target):
        ex = dict(ex)
        diff = ex.pop(TWIN_DIFF_INPUT)
        return grad_fn(weights, diff, {**shared, **ex}, loss_target)

    if N_MICROBATCH == 1:
        loss, (grad_w, grad_x) = one_microbatch(per_example, given["loss_target"])
    else:
        def body(carry, xs):
            loss_sum, grad_sum = carry
            l_k, (gw_k, gx_k) = one_microbatch(xs[0], xs[1])
            with _jax.named_scope("update"):
                return (loss_sum + l_k, _jax.tree.map(_jnp.add, grad_sum, gw_k)), gx_k

        init = (_jnp.zeros((), _jnp.float32), _jax.tree.map(_jnp.zeros_like, weights))
        (loss, grad_w), grad_x = _jax.lax.scan(body, init, (per_example, given["loss_target"]))
    with _jax.named_scope("update"):
        delta_w, new_m, new_v = {}, {}, {}
        for n in TWIN_WEIGHTS:
            delta_w[n], new_m[n], new_v[n] = _adamw(weights[n], grad_w[n], given["m_" + n], given["v_" + n])
    return (loss, grad_x, *[grad_w[n] for n in TWIN_WEIGHTS], *[delta_w[n] for n in TWIN_WEIGHTS],
            *[new_m[n] for n in TWIN_WEIGHTS], *[new_v[n] for n in TWIN_WEIGHTS])
```

```python
import functools

import numpy as np
import jax
import jax.numpy as jnp
from jax import lax
from jax.experimental import pallas as pl
from jax.experimental.pallas import tpu as pltpu

F32 = jnp.float32
BF16 = jnp.bfloat16
SDS = jax.ShapeDtypeStruct

D_MODEL = 1024
DEPTH = 2
CHUNK = 64
HEAD_DIM = 64
EPS = 1e-6
NEG_INF = -1e30
N_HEADS = 8
A_KV_HEADS = 2
A_PREV = 2
C_PREV = 8
REL_CLIP = 128
N_REL = 2 * REL_CLIP + 1
N_REL_PAD = 384
BRANCH_W = 512
FFN_H = 2816
B_BLOCK = 128
QKV_COLS = 4096
GF_COLS = 3200
N_IN_COLS = 6920
LANE = 128
VMEM_LIMIT = 48 * 1024 * 1024

ADAM_LR = 0.001
ADAM_B1 = 0.9
ADAM_B2 = 0.999
ADAM_EPS = 1e-08
ADAM_WD = 0.01
ADAM_STEP = 10

MESH = pl.DeviceIdType.MESH
ANY = pl.BlockSpec(memory_space=pl.ANY)
VMEM_SPEC = pl.BlockSpec(memory_space=pltpu.VMEM)


def _cparams(sem=None):
    return pltpu.CompilerParams(dimension_semantics=sem, vmem_limit_bytes=VMEM_LIMIT)


def _blk(n, cap):
    if n <= cap:
        return n
    best = None
    for m in range(LANE, cap + 1, LANE):
        if n % m == 0:
            best = m
    assert best is not None, (n, cap)
    return best


def _sigmoid(x):
    return 1.0 / (1.0 + jnp.exp(-x))


def _mm(a, b, *, mode, out_dtype, name, groups=1, cap_m=512, cap_n=1024, cap_k=1408):
    G = groups
    if mode == "nn":
        M, K, N = a.shape[0], a.shape[1] // G, b.shape[1]
        assert b.shape[0] == G * K
    elif mode == "nt":
        M, K, N = a.shape[0], a.shape[1] // G, b.shape[0] // G
        assert b.shape[1] == K
    else:
        K, M, N = a.shape[0], a.shape[1] // G, b.shape[1] // G
        assert b.shape[0] == K
    bm, bn, bk = _blk(M, cap_m), _blk(N, cap_n), _blk(K, cap_k)
    nm, nn, nk = M // bm, N // bn, K // bk
    if mode == "nn":
        a_spec = pl.BlockSpec((bm, bk), lambda g, i, j, k: (i, g * nk + k))
        b_spec = pl.BlockSpec((bk, bn), lambda g, i, j, k: (g * nk + k, j))
        o_spec = pl.BlockSpec((bm, bn), lambda g, i, j, k: (i, g * nn + j))
        dims = (((1,), (0,)), ((), ()))
        out_shape = (M, G * N)
    elif mode == "nt":
        a_spec = pl.BlockSpec((bm, bk), lambda g, i, j, k: (i, g * nk + k))
        b_spec = pl.BlockSpec((bn, bk), lambda g, i, j, k: (g * nn + j, k))
        o_spec = pl.BlockSpec((bm, bn), lambda g, i, j, k: (i, g * nn + j))
        dims = (((1,), (1,)), ((), ()))
        out_shape = (M, G * N)
    else:
        a_spec = pl.BlockSpec((bk, bm), lambda g, i, j, k: (k, g * nm + i))
        b_spec = pl.BlockSpec((bk, bn), lambda g, i, j, k: (k, g * nn + j))
        o_spec = pl.BlockSpec((bm, bn), lambda g, i, j, k: (g * nm + i, j))
        dims = (((0,), (0,)), ((), ()))
        out_shape = (G * M, N)

    def body(a_ref, b_ref, o_ref, acc_ref):
        k = pl.program_id(3)

        @pl.when(k == 0)
        def _():
            acc_ref[...] = jnp.zeros_like(acc_ref)

        acc_ref[...] += lax.dot_general(a_ref[...].astype(BF16), b_ref[...].astype(BF16), dims,
                                        preferred_element_type=F32)

        @pl.when(k == nk - 1)
        def _():
            o_ref[...] = acc_ref[...].astype(o_ref.dtype)

    return pl.pallas_call(
        body, grid=(G, nm, nn, nk), in_specs=[a_spec, b_spec], out_specs=o_spec,
        out_shape=SDS(out_shape, out_dtype), scratch_shapes=[pltpu.VMEM((bm, bn), F32)],
        compiler_params=_cparams(("parallel", "parallel", "parallel", "arbitrary")), name=name,
    )(a, b)


def _rows(tm, n, col=0):
    return pl.BlockSpec((tm, n), lambda i: (i, col))


def _vec(n):
    return pl.BlockSpec((1, n), lambda i: (0, 0))


def _tm(S):
    return min(S, 256)


def _norm_mod_fwd(x, g, sc, sh, name):
    S, Dm = x.shape
    tm = _tm(S)

    def body(x_ref, g_ref, sc_ref, sh_ref, h_ref):
        xv = x_ref[...]
        r = lax.rsqrt(jnp.mean(xv * xv, axis=-1, keepdims=True) + EPS)
        h_ref[...] = ((xv * r) * g_ref[...] * (1.0 + sc_ref[...]) + sh_ref[...]).astype(h_ref.dtype)

    return pl.pallas_call(
        body, grid=(S // tm,), in_specs=[_rows(tm, Dm), _vec(Dm), _vec(Dm), _vec(Dm)],
        out_specs=_rows(tm, Dm), out_shape=SDS((S, Dm), BF16),
        compiler_params=_cparams(("parallel",)), name=name)(x, g, sc, sh)


def _norm_mod_bwd(x, dh_list, dres, g, sc, name):
    S, Dm = x.shape
    tm = _tm(S)
    nh = len(dh_list)

    def body(*refs):
        x_ref = refs[0]
        dh_refs = refs[1:1 + nh]
        dres_ref, g_ref, sc_ref, dx_ref, dsc_ref, dsh_ref, dg_ref = refs[1 + nh:]
        i = pl.program_id(0)

        @pl.when(i == 0)
        def _():
            dsc_ref[...] = jnp.zeros_like(dsc_ref)
            dsh_ref[...] = jnp.zeros_like(dsh_ref)
            dg_ref[...] = jnp.zeros_like(dg_ref)

        xv = x_ref[...]
        dh = dh_refs[0][...]
        for r_ in dh_refs[1:]:
            dh = dh + r_[...]
        gv = g_ref[...]
        r = lax.rsqrt(jnp.mean(xv * xv, axis=-1, keepdims=True) + EPS)
        xn = xv * r
        xg = xn * gv
        dsh_ref[...] += jnp.sum(dh, axis=0, keepdims=True)
        dsc_ref[...] += jnp.sum(dh * xg, axis=0, keepdims=True)
        dxg = dh * (1.0 + sc_ref[...])
        dg_ref[...] += jnp.sum(dxg * xn, axis=0, keepdims=True)
        dxn = dxg * gv
        dx_ref[...] = dres_ref[...] + r * (dxn - xn * jnp.mean(dxn * xn, axis=-1, keepdims=True))

    return pl.pallas_call(
        body, grid=(S // tm,),
        in_specs=[_rows(tm, Dm)] * (2 + nh) + [_vec(Dm), _vec(Dm)],
        out_specs=[_rows(tm, Dm), _vec(Dm), _vec(Dm), _vec(Dm)],
        out_shape=[SDS((S, Dm), F32), SDS((1, Dm), F32), SDS((1, Dm), F32), SDS((1, Dm), F32)],
        compiler_params=_cparams(("arbitrary",)), name=name)(x, *dh_list, dres, g, sc)


def _resid_fwd(x, val, g, name):
    S, Dm = x.shape
    tm = _tm(S)

    def body(x_ref, v_ref, g_ref, o_ref):
        o_ref[...] = x_ref[...] + g_ref[...] * v_ref[...]

    return pl.pallas_call(
        body, grid=(S // tm,), in_specs=[_rows(tm, Dm), _rows(tm, Dm), _vec(Dm)],
        out_specs=_rows(tm, Dm), out_shape=SDS((S, Dm), F32),
        compiler_params=_cparams(("parallel",)), name=name)(x, val, g)


def _resid_bwd(dx, val, g, name):
    S, Dm = dx.shape
    tm = _tm(S)

    def body(dx_ref, v_ref, g_ref, dg_ref, dv_ref):
        @pl.when(pl.program_id(0) == 0)
        def _():
            dg_ref[...] = jnp.zeros_like(dg_ref)

        dxv = dx_ref[...]
        dg_ref[...] += jnp.sum(dxv * v_ref[...], axis=0, keepdims=True)
        dv_ref[...] = (dxv * g_ref[...]).astype(dv_ref.dtype)

    return pl.pallas_call(
        body, grid=(S // tm,), in_specs=[_rows(tm, Dm), _rows(tm, Dm), _vec(Dm)],
        out_specs=[_vec(Dm), _rows(tm, Dm)], out_shape=[SDS((1, Dm), F32), SDS((S, Dm), BF16)],
        compiler_params=_cparams(("arbitrary",)), name=name)(dx, val, g)


def _merge_fwd(y, gf, name):
    S = y.shape[0]
    tm = _tm(S)
    W = 3 * D_MODEL

    def body(y_ref, g_ref, o_ref):
        acc = None
        for k in range(3):
            sl = slice(k * D_MODEL, (k + 1) * D_MODEL)
            t = _sigmoid(g_ref[:, sl]) * y_ref[:, sl]
            acc = t if acc is None else acc + t
        o_ref[...] = acc.astype(o_ref.dtype)

    return pl.pallas_call(
        body, grid=(S // tm,), in_specs=[_rows(tm, W), _rows(tm, W)],
        out_specs=_rows(tm, D_MODEL), out_shape=SDS((S, D_MODEL), BF16),
        compiler_params=_cparams(("parallel",)), name=name)(y, gf)


def _merge_bwd(dm, y, gf, name):
    S = y.shape[0]
    tm = _tm(S)
    W = 3 * D_MODEL

    def body(dm_ref, y_ref, g_ref, dy_ref, dg_ref):
        dmv = dm_ref[...]
        for k in range(3):
            sl = slice(k * D_MODEL, (k + 1) * D_MODEL)
            sg = _sigmoid(g_ref[:, sl])
            dy_ref[:, sl] = (dmv * sg).astype(dy_ref.dtype)
            dg_ref[:, sl] = (dmv * y_ref[:, sl] * (sg * (1.0 - sg))).astype(dg_ref.dtype)

    return pl.pallas_call(
        body, grid=(S // tm,), in_specs=[_rows(tm, D_MODEL), _rows(tm, W), _rows(tm, W)],
        out_specs=[_rows(tm, W), _rows(tm, W)], out_shape=[SDS((S, W), BF16), SDS((S, W), BF16)],
        compiler_params=_cparams(("parallel",)), name=name)(dm, y, gf)


def _swiglu_fwd(u, name):
    S = u.shape[0]
    tm = _tm(S)

    def body(g_ref, u_ref, a_ref):
        gv = g_ref[...]
        a_ref[...] = (gv * _sigmoid(gv) * u_ref[...]).astype(a_ref.dtype)

    return pl.pallas_call(
        body, grid=(S // tm,), in_specs=[_rows(tm, FFN_H, 0), _rows(tm, FFN_H, 1)],
        out_specs=_rows(tm, FFN_H), out_shape=SDS((S, FFN_H), BF16),
        compiler_params=_cparams(("parallel",)), name=name)(u, u)


def _swiglu_bwd(da, u, name):
    S = u.shape[0]
    tm = _tm(S)

    def body(da_ref, g_ref, u_ref, du_ref):
        dav = da_ref[...]
        gv = g_ref[...]
        sg = _sigmoid(gv)
        du_ref[:, 0:FFN_H] = (dav * u_ref[...] * (sg * (1.0 + gv * (1.0 - sg)))).astype(du_ref.dtype)
        du_ref[:, FFN_H:2 * FFN_H] = (dav * (gv * sg)).astype(du_ref.dtype)

    return pl.pallas_call(
        body, grid=(S // tm,), in_specs=[_rows(tm, FFN_H), _rows(tm, FFN_H, 0), _rows(tm, FFN_H, 1)],
        out_specs=_rows(tm, 2 * FFN_H), out_shape=SDS((S, 2 * FFN_H), BF16),
        compiler_params=_cparams(("parallel",)), name=name)(da, u, u)


def _final_loss(x, g, target, name):
    S, Dm = x.shape
    tm = _tm(S)

    def body(x_ref, g_ref, t_ref, loss_ref, dx_ref, dg_ref):
        @pl.when(pl.program_id(0) == 0)
        def _():
            loss_ref[...] = jnp.zeros_like(loss_ref)
            dg_ref[...] = jnp.zeros_like(dg_ref)

        xv = x_ref[...]
        gv = g_ref[...]
        r = lax.rsqrt(jnp.mean(xv * xv, axis=-1, keepdims=True) + EPS)
        xn = xv * r
        err = xn * gv - t_ref[...]
        row = jnp.mean(err * err, axis=-1, keepdims=True)
        loss_ref[...] += 0.5 * jnp.sum(row, axis=0, keepdims=True)
        dy = err * (1.0 / Dm)
        dg_ref[...] += jnp.sum(dy * xn, axis=0, keepdims=True)
        dxn = dy * gv
        dx_ref[...] = r * (dxn - xn * jnp.mean(dxn * xn, axis=-1, keepdims=True))

    return pl.pallas_call(
        body, grid=(S // tm,), in_specs=[_rows(tm, Dm), _vec(Dm), _rows(tm, Dm)],
        out_specs=[pl.BlockSpec((1, 1), lambda i: (0, 0)), _rows(tm, Dm), _vec(Dm)],
        out_shape=[SDS((1, 1), F32), SDS((S, Dm), F32), SDS((1, Dm), F32)],
        compiler_params=_cparams(("arbitrary",)), name=name)(x, g, target)


def _band_softmax(qg, kg, bias, sink, valid):
    s = lax.dot_general(qg, kg, (((1,), (1,)), ((), ())), preferred_element_type=F32)
    s = jnp.where(valid, s + bias, NEG_INF)
    m = jnp.maximum(jnp.max(s, axis=-1, keepdims=True), sink)
    e = jnp.exp(s - m)
    es = jnp.exp(sink - m)
    l = jnp.sum(e, axis=-1, keepdims=True) + es
    return e / l, es / l


def _band_attn_fwd(q, k, v, bias, sink, *, G, P, kvoff, name):
    S = q.shape[0]
    ng = q.shape[1] // (G * HEAD_DIM)
    band = (P + 1) * CHUNK
    pad = P * CHUNK
    nc = S // CHUNK

    def body(q_ref, k_ref, v_ref, b_ref, s_ref, o_ref, kp, vp):
        kp[0:pad, :] = jnp.zeros((pad, LANE), BF16)
        vp[0:pad, :] = jnp.zeros((pad, LANE), BF16)
        kp[pad:pad + S, :] = k_ref[...]
        vp[pad:pad + S, :] = v_ref[...]
        col = lax.broadcasted_iota(jnp.int32, (CHUNK, band), 1)

        def step(n, carry):
            r = pl.multiple_of(n * CHUNK, CHUNK)
            qn = q_ref[pl.ds(r, CHUNK), :]
            kb = kp[pl.ds(r, band), :]
            vb = vp[pl.ds(r, band), :]
            valid = col >= (P - n) * CHUNK
            for g in range(G):
                ko = kvoff(g) * HEAD_DIM
                qg = qn[:, g * HEAD_DIM:(g + 1) * HEAD_DIM] * 0.125
                p, _ = _band_softmax(qg, kb[:, ko:ko + HEAD_DIM], b_ref[g], s_ref[g, 0:1, 0:1], valid)
                og = jnp.dot(p.astype(BF16), vb[:, ko:ko + HEAD_DIM], preferred_element_type=F32)
                o_ref[pl.ds(r, CHUNK), g * HEAD_DIM:(g + 1) * HEAD_DIM] = og.astype(o_ref.dtype)
            return carry

        lax.fori_loop(0, nc, step, 0)

    GW = G * HEAD_DIM
    return pl.pallas_call(
        body, grid=(ng,),
        in_specs=[pl.BlockSpec((S, GW), lambda i: (0, i)), pl.BlockSpec((S, LANE), lambda i: (0, i)),
                  pl.BlockSpec((S, LANE), lambda i: (0, i)),
                  pl.BlockSpec((G, CHUNK, band), lambda i: (i, 0, 0)),
                  pl.BlockSpec((G, 8, LANE), lambda i: (i, 0, 0))],
        out_specs=pl.BlockSpec((S, GW), lambda i: (0, i)),
        out_shape=SDS((S, ng * GW), BF16),
        scratch_shapes=[pltpu.VMEM((S + pad, LANE), BF16), pltpu.VMEM((S + pad, LANE), BF16)],
        compiler_params=_cparams(("parallel",)), name=name)(q, k, v, bias, sink)


def _band_attn_bwd(q, k, v, bias, sink, do, *, G, P, kvoff, name):
    S = q.shape[0]
    ng = q.shape[1] // (G * HEAD_DIM)
    band = (P + 1) * CHUNK
    pad = P * CHUNK
    nc = S // CHUNK
    TN = (((0,), (0,)), ((), ()))

    def body(q_ref, k_ref, v_ref, b_ref, s_ref, do_ref, dq_ref, dk_ref, dv_ref, db_ref, dsk_ref,
             kp, vp, dkp, dvp):
        kp[0:pad, :] = jnp.zeros((pad, LANE), BF16)
        vp[0:pad, :] = jnp.zeros((pad, LANE), BF16)
        kp[pad:pad + S, :] = k_ref[...]
        vp[pad:pad + S, :] = v_ref[...]
        dkp[...] = jnp.zeros_like(dkp)
        dvp[...] = jnp.zeros_like(dvp)
        db_ref[...] = jnp.zeros_like(db_ref)
        col = lax.broadcasted_iota(jnp.int32, (CHUNK, band), 1)

        def step(n, dsink):
            r = pl.multiple_of(n * CHUNK, CHUNK)
            qn = q_ref[pl.ds(r, CHUNK), :]
            don = do_ref[pl.ds(r, CHUNK), :]
            kb = kp[pl.ds(r, band), :]
            vb = vp[pl.ds(r, band), :]
            valid = col >= (P - n) * CHUNK
            new = []
            for g in range(G):
                ko = kvoff(g) * HEAD_DIM
                lanes = slice(g * HEAD_DIM, (g + 1) * HEAD_DIM)
                qg = qn[:, lanes] * 0.125
                kg = kb[:, ko:ko + HEAD_DIM]
                dog = don[:, lanes]
                p, ps = _band_softmax(qg, kg, b_ref[g], s_ref[g, 0:1, 0:1], valid)
                dp = lax.dot_general(dog, vb[:, ko:ko + HEAD_DIM], (((1,), (1,)), ((), ())),
                                     preferred_element_type=F32)
                delta = jnp.sum(p * dp, axis=-1, keepdims=True)
                ds = p * (dp - delta)
                new.append(dsink[g] - jnp.sum(ps * delta, axis=0, keepdims=True))
                db_ref[g] += ds
                dsb = ds.astype(BF16)
                dq = jnp.dot(dsb, kg, preferred_element_type=F32) * 0.125
                dq_ref[pl.ds(r, CHUNK), lanes] = dq.astype(dq_ref.dtype)
                dkp[pl.ds(r, band), ko:ko + HEAD_DIM] += lax.dot_general(
                    dsb, qg, TN, preferred_element_type=F32)
                dvp[pl.ds(r, band), ko:ko + HEAD_DIM] += lax.dot_general(
                    p.astype(BF16), dog, TN, preferred_element_type=F32)
            return tuple(new)

        dsink = lax.fori_loop(0, nc, step, tuple(jnp.zeros((1, 1), F32) for _ in range(G)))
        for g in range(G):
            dsk_ref[g] = jnp.broadcast_to(dsink[g], (8, LANE))
        dk_ref[...] = dkp[pad:pad + S, :].astype(dk_ref.dtype)
        dv_ref[...] = dvp[pad:pad + S, :].astype(dv_ref.dtype)

    GW = G * HEAD_DIM
    qs = pl.BlockSpec((S, GW), lambda i: (0, i))
    ks = pl.BlockSpec((S, LANE), lambda i: (0, i))
    bs = pl.BlockSpec((G, CHUNK, band), lambda i: (i, 0, 0))
    ss = pl.BlockSpec((G, 8, LANE), lambda i: (i, 0, 0))
    return pl.pallas_call(
        body, grid=(ng,), in_specs=[qs, ks, ks, bs, ss, qs],
        out_specs=[qs, ks, ks, bs, ss],
        out_shape=[SDS((S, ng * GW), BF16), SDS((S, ng * LANE), BF16), SDS((S, ng * LANE), BF16),
                   SDS((ng * G, CHUNK, band), F32), SDS((ng * G, 8, LANE), F32)],
        scratch_shapes=[pltpu.VMEM((S + pad, LANE), BF16), pltpu.VMEM((S + pad, LANE), BF16),
                        pltpu.VMEM((S + pad, LANE), F32), pltpu.VMEM((S + pad, LANE), F32)],
        compiler_params=_cparams(("parallel",)), name=name)(q, k, v, bias, sink, do)


def _fox_logits(qg, kj, cq, ck, r, c, row, col):
    s = lax.dot_general(qg, kj, (((1,), (1,)), ((), ())), preferred_element_type=F32)
    s = s + cq - ck
    return jnp.where(c + col <= r + row, s, NEG_INF)


def _fox_fwd(q, k, v, cc, cr, name):
    S = q.shape[0]
    npair = q.shape[1] // LANE
    nq = S // B_BLOCK
    BB = B_BLOCK

    def body(q_ref, k_ref, v_ref, cc_ref, cr_ref, o_ref, lse_ref):
        row = lax.broadcasted_iota(jnp.int32, (BB, BB), 0)
        col = lax.broadcasted_iota(jnp.int32, (BB, BB), 1)

        def qstep(i, carry):
            r = pl.multiple_of(i * BB, BB)
            for g in range(2):
                lanes = slice(g * HEAD_DIM, (g + 1) * HEAD_DIM)
                qg = q_ref[pl.ds(r, BB), lanes] * 0.125
                cq = cc_ref[g, pl.ds(r, BB), :]

                def kstep(j, st):
                    m, l, acc = st
                    c = pl.multiple_of(j * BB, BB)
                    s = _fox_logits(qg, k_ref[pl.ds(c, BB), lanes], cq, cr_ref[g, :, pl.ds(c, BB)],
                                    r, c, row, col)
                    mn = jnp.maximum(m, jnp.max(s, axis=-1, keepdims=True))
                    al = jnp.exp(m - mn)
                    e = jnp.exp(s - mn)
                    l = al * l + jnp.sum(e, axis=-1, keepdims=True)
                    acc = al * acc + jnp.dot(e.astype(BF16), v_ref[pl.ds(c, BB), lanes],
                                             preferred_element_type=F32)
                    return mn, l, acc

                m, l, acc = lax.fori_loop(
                    0, i + 1, kstep,
                    (jnp.full((BB, 1), NEG_INF, F32), jnp.zeros((BB, 1), F32), jnp.zeros((BB, HEAD_DIM), F32)))
                o_ref[pl.ds(r, BB), lanes] = (acc / l).astype(o_ref.dtype)
                lse_ref[g, pl.ds(r, BB), :] = m + jnp.log(l)
            return carry

        lax.fori_loop(0, nq, qstep, 0)

    blk = pl.BlockSpec((S, LANE), lambda i: (0, i))
    ccs = pl.BlockSpec((2, S, 1), lambda i: (i, 0, 0))
    crs = pl.BlockSpec((2, 1, S), lambda i: (i, 0, 0))
    return pl.pallas_call(
        body, grid=(npair,), in_specs=[blk, blk, blk, ccs, crs], out_specs=[blk, ccs],
        out_shape=[SDS((S, npair * LANE), BF16), SDS((2 * npair, S, 1), F32)],
        compiler_params=_cparams(("parallel",)), name=name)(q, k, v, cc, cr)


def _fox_bwd(q, k, v, cc, cr, o, do, lse, name):
    S = q.shape[0]
    npair = q.shape[1] // LANE
    nq = S // B_BLOCK
    BB = B_BLOCK
    TN = (((0,), (0,)), ((), ()))

    def body(q_ref, k_ref, v_ref, cc_ref, cr_ref, o_ref, do_ref, lse_ref,
             dq_ref, dk_ref, dv_ref, dcr_ref, dcc_ref, dka, dva):
        dka[...] = jnp.zeros_like(dka)
        dva[...] = jnp.zeros_like(dva)
        dcr_ref[...] = jnp.zeros_like(dcr_ref)
        row = lax.broadcasted_iota(jnp.int32, (BB, BB), 0)
        col = lax.broadcasted_iota(jnp.int32, (BB, BB), 1)

        def qstep(i, carry):
            r = pl.multiple_of(i * BB, BB)
            for g in range(2):
                lanes = slice(g * HEAD_DIM, (g + 1) * HEAD_DIM)
                qg = q_ref[pl.ds(r, BB), lanes] * 0.125
                dog = do_ref[pl.ds(r, BB), lanes]
                delta = jnp.sum(dog.astype(F32) * o_ref[pl.ds(r, BB), lanes].astype(F32),
                                axis=-1, keepdims=True)
                cq = cc_ref[g, pl.ds(r, BB), :]
                lse_q = lse_ref[g, pl.ds(r, BB), :]

                def kstep(j, st):
                    dq, rs = st
                    c = pl.multiple_of(j * BB, BB)
                    kj = k_ref[pl.ds(c, BB), lanes]
                    s = _fox_logits(qg, kj, cq, cr_ref[g, :, pl.ds(c, BB)], r, c, row, col)
                    p = jnp.exp(s - lse_q)
                    dp = lax.dot_general(dog, v_ref[pl.ds(c, BB), lanes], (((1,), (1,)), ((), ())),
                                         preferred_element_type=F32)
                    ds = p * (dp - delta)
                    dcr_ref[g, :, pl.ds(c, BB)] -= jnp.sum(ds, axis=0, keepdims=True)
                    dsb = ds.astype(BF16)
                    dka[pl.ds(c, BB), lanes] += lax.dot_general(dsb, qg, TN, preferred_element_type=F32)
                    dva[pl.ds(c, BB), lanes] += lax.dot_general(p.astype(BF16), dog, TN,
                                                               preferred_element_type=F32)
                    return (dq + jnp.dot(dsb, kj, preferred_element_type=F32),
                            rs + jnp.sum(ds, axis=-1, keepdims=True))

                dq, rs = lax.fori_loop(0, i + 1, kstep,
                                       (jnp.zeros((BB, HEAD_DIM), F32), jnp.zeros((BB, 1), F32)))
                dq_ref[pl.ds(r, BB), lanes] = (dq * 0.125).astype(dq_ref.dtype)
                dcc_ref[g, pl.ds(r, BB), :] = rs
            return carry

        lax.fori_loop(0, nq, qstep, 0)
        dk_ref[...] = dka[...].astype(dk_ref.dtype)
        dv_ref[...] = dva[...].astype(dv_ref.dtype)

    blk = pl.BlockSpec((S, LANE), lambda i: (0, i))
    ccs = pl.BlockSpec((2, S, 1), lambda i: (i, 0, 0))
    crs = pl.BlockSpec((2, 1, S), lambda i: (i, 0, 0))
    return pl.pallas_call(
        body, grid=(npair,), in_specs=[blk, blk, blk, ccs, crs, blk, blk, ccs],
        out_specs=[blk, blk, blk, crs, ccs],
        out_shape=[SDS((S, npair * LANE), BF16)] * 3 + [SDS((2 * npair, 1, S), F32), SDS((2 * npair, S, 1), F32)],
        scratch_shapes=[pltpu.VMEM((S, LANE), F32), pltpu.VMEM((S, LANE), F32)],
        compiler_params=_cparams(("parallel",)), name=name)(q, k, v, cc, cr, o, do, lse)


def _split3(x):
    hi = x.astype(BF16)
    r1 = x - hi.astype(F32)
    mid = r1.astype(BF16)
    lo = (r1 - mid.astype(F32)).astype(BF16)
    return hi, mid, lo


def _tri_dot(tri, x):
    hi, mid, lo = _split3(x)
    return (jnp.dot(tri, hi, preferred_element_type=F32) + jnp.dot(tri, mid, preferred_element_type=F32)
            + jnp.dot(tri, lo, preferred_element_type=F32))


def _fox_cum(gf, bfo, name):
    S = gf.shape[0]
    nb = S // LANE
    fcol = (GF_COLS - LANE) // LANE

    def body(f_ref, b_ref, cum_ref):
        row = lax.broadcasted_iota(jnp.int32, (LANE, LANE), 0)
        col = lax.broadcasted_iota(jnp.int32, (LANE, LANE), 1)
        tri = jnp.where(row >= col, 1.0, 0.0).astype(BF16)
        carry = jnp.zeros((1, LANE), F32)
        for t in range(nb):
            xl = f_ref[t * LANE:(t + 1) * LANE, :] + b_ref[...]
            lf = jnp.minimum(xl, 0.0) - jnp.log(1.0 + jnp.exp(-jnp.abs(xl)))
            cblk = _tri_dot(tri, lf) + carry
            cum_ref[t * LANE:(t + 1) * LANE, :] = cblk
            carry = cblk[LANE - 1:LANE, :]

    return pl.pallas_call(
        body, grid=(1,), in_specs=[pl.BlockSpec((S, LANE), lambda i: (0, fcol)), _vec(LANE)],
        out_specs=pl.BlockSpec((S, LANE), lambda i: (0, 0)), out_shape=SDS((S, LANE), F32),
        compiler_params=_cparams(("arbitrary",)), name=name)(gf, bfo)


def _fox_cum_bwd(gf, bfo, dcum, name):
    S = gf.shape[0]
    nb = S // LANE
    fcol = (GF_COLS - LANE) // LANE

    def body(f_ref, b_ref, dc_ref, df_ref, db_ref):
        row = lax.broadcasted_iota(jnp.int32, (LANE, LANE), 0)
        col = lax.broadcasted_iota(jnp.int32, (LANE, LANE), 1)
        tri = jnp.where(row <= col, 1.0, 0.0).astype(BF16)
        carry = jnp.zeros((1, LANE), F32)
        tot = jnp.zeros((1, LANE), F32)
        for t in range(nb - 1, -1, -1):
            rows = slice(t * LANE, (t + 1) * LANE)
            dlf = _tri_dot(tri, dc_ref[rows, :]) + carry
            carry = dlf[0:1, :]
            xl = f_ref[rows, :] + b_ref[...]
            dfl = dlf * (1.0 / (1.0 + jnp.exp(xl)))
            df_ref[rows, :] = dfl.astype(df_ref.dtype)
            tot = tot + jnp.sum(dfl, axis=0, keepdims=True)
        db_ref[...] = tot

    return pl.pallas_call(
        body, grid=(1,),
        in_specs=[pl.BlockSpec((S, LANE), lambda i: (0, fcol)), _vec(LANE), pl.BlockSpec((S, LANE), lambda i: (0, 0))],
        out_specs=[pl.BlockSpec((S, LANE), lambda i: (0, 0)), _vec(LANE)],
        out_shape=[SDS((S, LANE), BF16), SDS((1, LANE), F32)],
        compiler_params=_cparams(("arbitrary",)), name=name)(gf, bfo, dcum)


def _rel_onehot(qi, band):
    r = lax.broadcasted_iota(jnp.int32, (N_REL_PAD, band), 0)
    j = lax.broadcasted_iota(jnp.int32, (N_REL_PAD, band), 1)
    idx = jnp.clip(C_PREV * CHUNK + qi - j, -REL_CLIP, REL_CLIP) + REL_CLIP
    return jnp.where(r == idx, 1.0, 0.0).astype(BF16)


def _rel_expand(rel, name):
    band = (C_PREV + 1) * CHUNK

    def body(rel_ref, o_ref):
        oh = _rel_onehot(pl.program_id(0), band)
        o_ref[0] = _tri_dot_rhs(rel_ref[...], oh)

    return pl.pallas_call(
        body, grid=(CHUNK,), in_specs=[pl.BlockSpec((N_HEADS, N_REL_PAD), lambda i: (0, 0))],
        out_specs=pl.BlockSpec((1, N_HEADS, band), lambda i: (i, 0, 0)),
        out_shape=SDS((CHUNK, N_HEADS, band), F32),
        compiler_params=_cparams(("parallel",)), name=name)(rel)


def _tri_dot_rhs(x, oh):
    hi, mid, lo = _split3(x)
    return (jnp.dot(hi, oh, preferred_element_type=F32) + jnp.dot(mid, oh, preferred_element_type=F32)
            + jnp.dot(lo, oh, preferred_element_type=F32))


def _rel_reduce(dbias, name):
    band = (C_PREV + 1) * CHUNK
    NT = (((1,), (1,)), ((), ()))

    def body(d_ref, o_ref):
        @pl.when(pl.program_id(0) == 0)
        def _():
            o_ref[...] = jnp.zeros_like(o_ref)

        oh = _rel_onehot(pl.program_id(0), band)
        hi, mid, lo = _split3(d_ref[0])
        o_ref[...] += (lax.dot_general(hi, oh, NT, preferred_element_type=F32)
                       + lax.dot_general(mid, oh, NT, preferred_element_type=F32)
                       + lax.dot_general(lo, oh, NT, preferred_element_type=F32))

    return pl.pallas_call(
        body, grid=(CHUNK,), in_specs=[pl.BlockSpec((1, N_HEADS, band), lambda i: (i, 0, 0))],
        out_specs=pl.BlockSpec((N_HEADS, N_REL_PAD), lambda i: (0, 0)),
        out_shape=SDS((N_HEADS, N_REL_PAD), F32),
        compiler_params=_cparams(("arbitrary",)), name=name)(dbias)


def _alibi_table():
    qi = np.arange(CHUNK)[:, None]
    j = np.arange((A_PREV + 1) * CHUNK)[None, :]
    dist = np.abs(A_PREV * CHUNK + qi - j).astype(np.float32)
    slopes = np.exp2(-8.0 * np.arange(1, N_HEADS + 1, dtype=np.float32) / N_HEADS).astype(np.float32)
    return jnp.asarray(-slopes[:, None, None] * dist[None])


def _ada_fwd(c_all, w, b, name):
    n = w.shape[2]

    def body(c_ref, w_ref, b_ref, o_ref):
        cv = c_ref[...]
        cond = (cv * _sigmoid(cv)).astype(BF16)
        o_ref[0] = jnp.dot(cond, w_ref[0].astype(BF16), preferred_element_type=F32) + b_ref[0]

    return pl.pallas_call(
        body, grid=(DEPTH,),
        in_specs=[pl.BlockSpec((16, D_MODEL), lambda l: (0, 0)), pl.BlockSpec((1, D_MODEL, n), lambda l: (l, 0, 0)),
                  pl.BlockSpec((1, 1, n), lambda l: (l, 0, 0))],
        out_specs=pl.BlockSpec((1, 16, n), lambda l: (l, 0, 0)), out_shape=SDS((DEPTH, 16, n), F32),
        compiler_params=_cparams(("parallel",)), name=name)(c_all, w, b)


def _ada_bwd(c_t, dmod, name):
    n = dmod.shape[2]
    bn = _blk(n, 512)
    tr = 256

    def body(c_ref, d_ref, o_ref):
        cv = c_ref[...]
        cond = (cv * _sigmoid(cv)).astype(BF16).astype(F32)
        dm = d_ref[0].astype(BF16).astype(F32)
        acc = cond[:, 0:1] * dm[0:1, :]
        for b_ in range(1, 8):
            acc = acc + cond[:, b_:b_ + 1] * dm[b_:b_ + 1, :]
        o_ref[0] = acc

    return pl.pallas_call(
        body, grid=(DEPTH, D_MODEL // tr, n // bn),
        in_specs=[pl.BlockSpec((tr, 8), lambda l, i, j: (i, 0)), pl.BlockSpec((1, 8, bn), lambda l, i, j: (l, 0, j))],
        out_specs=pl.BlockSpec((1, tr, bn), lambda l, i, j: (l, i, j)), out_shape=SDS((DEPTH, D_MODEL, n), F32),
        compiler_params=_cparams(("parallel", "parallel", "parallel")), name=name)(c_t, dmod)


def _adamw(w, m, v, parts, name, tr=None):
    R, C = w.shape
    P = parts.shape[0]
    if tr is None:
        tr = _blk_rows(R, max(16, (1 << 18) // C))
    c1 = 1.0 - ADAM_B1 ** ADAM_STEP
    c2 = 1.0 - ADAM_B2 ** ADAM_STEP

    def body(w_ref, m_ref, v_ref, p_ref, g_ref, d_ref, nm_ref, nv_ref):
        g = p_ref[0].astype(F32)
        for k in range(1, P):
            g = g + p_ref[k].astype(F32)
        mn = ADAM_B1 * m_ref[...] + (1.0 - ADAM_B1) * g
        vn = ADAM_B2 * v_ref[...] + (1.0 - ADAM_B2) * (g * g)
        m_hat = mn / c1
        v_hat = vn / c2
        g_ref[...] = g
        nm_ref[...] = mn
        nv_ref[...] = vn
        d_ref[...] = -ADAM_LR * (m_hat / (jnp.sqrt(v_hat) + ADAM_EPS) + ADAM_WD * w_ref[...])

    rs = pl.BlockSpec((tr, C), lambda i: (i, 0))
    return pl.pallas_call(
        body, grid=(R // tr,), in_specs=[rs, rs, rs, pl.BlockSpec((P, tr, C), lambda i: (0, i, 0))],
        out_specs=[rs, rs, rs, rs], out_shape=[SDS((R, C), F32)] * 4,
        compiler_params=_cparams(("parallel",)), name=name)(w, m, v, parts)


def _blk_rows(R, cap):
    if R <= cap:
        return R
    best = None
    for t in range(16, cap + 1, 16):
        if R % t == 0:
            best = t
    assert best is not None, (R, cap)
    return best


def _add_cast(a, b, name):
    Q, R, C = a.shape
    tr = _blk_rows(R, max(16, (1 << 19) // C))

    def body(a_ref, b_ref, o_ref):
        o_ref[...] = (a_ref[...] + b_ref[...]).astype(o_ref.dtype)

    bs = pl.BlockSpec((1, tr, C), lambda q, i: (q, i, 0))
    return pl.pallas_call(
        body, grid=(Q, R // tr), in_specs=[bs, bs], out_specs=bs, out_shape=SDS((Q, R, C), BF16),
        compiler_params=_cparams(("parallel", "parallel")), name=name)(a, b)


def _coords():
    return lax.axis_index("x"), lax.axis_index("y"), lax.axis_index("c")


def _flip(v, bit):
    return 1 - v if bit else v


def _all_gather8(v, name):
    R = v.shape[0]

    def body(v_ref, o_ref, send_sems, recv_sems):
        x, y, c = _coords()
        me = 4 * x + 2 * y + c
        o_ref[me] = v_ref[...]
        copies = []
        for k in range(1, 8):
            peer = (_flip(x, k & 4), _flip(y, k & 2), _flip(c, k & 1))
            cp = pltpu.make_async_remote_copy(
                src_ref=v_ref, dst_ref=o_ref.at[me], send_sem=send_sems.at[k - 1], recv_sem=recv_sems.at[k - 1],
                device_id=peer, device_id_type=MESH)
            cp.start()
            copies.append(cp)
        for cp in copies:
            cp.wait_recv()
        for cp in copies:
            cp.wait_send()

    return pl.pallas_call(
        body, in_specs=[VMEM_SPEC], out_specs=VMEM_SPEC, out_shape=SDS((8, R, LANE), v.dtype),
        scratch_shapes=[pltpu.SemaphoreType.DMA((7,)), pltpu.SemaphoreType.DMA((7,))],
        compiler_params=pltpu.CompilerParams(vmem_limit_bytes=VMEM_LIMIT), name=name)(v)


def _sibling_swap(arrs, name):
    n = len(arrs)

    def body(*refs):
        in_refs, out_refs = refs[:n], refs[n:2 * n]
        send_sems, recv_sems = refs[2 * n:]
        x, y, c = _coords()
        copies = []
        for a in range(n):
            cp = pltpu.make_async_remote_copy(
                src_ref=in_refs[a].at[1 - c], dst_ref=out_refs[a], send_sem=send_sems.at[a],
                recv_sem=recv_sems.at[a], device_id=(x, y, 1 - c), device_id_type=MESH)
            cp.start()
            copies.append(cp)
        for cp in copies:
            cp.wait_recv()
        for cp in copies:
            cp.wait_send()

    return pl.pallas_call(
        body, in_specs=[ANY] * n, out_specs=[ANY] * n,
        out_shape=[SDS(a.shape[1:], a.dtype) for a in arrs],
        scratch_shapes=[pltpu.SemaphoreType.DMA((n,)), pltpu.SemaphoreType.DMA((n,))],
        name=name)(*arrs)


def _chip_exchange(arrs, *, reduce, name):
    n = len(arrs)

    def body(*refs):
        in_refs, out_refs = refs[:n], refs[n:2 * n]
        ici_send, ici_recv, d2d_send, d2d_recv, loc_sem = refs[2 * n:]
        x, y, c = _coords()
        p = 2 * x + y
        local, first, fwd = [], [], []
        for a in range(n):
            R = out_refs[a].shape[1] // 2
            half = pl.ds(pl.multiple_of(c * R, 16), R)
            if reduce:
                lc = pltpu.make_async_copy(in_refs[a].at[p], out_refs[a].at[p, half], loc_sem.at[a])
            else:
                lc = pltpu.make_async_copy(in_refs[a], out_refs[a].at[p], loc_sem.at[a])
            lc.start()
            local.append(lc)
            for k in range(1, 4):
                qx, qy = _flip(x, k & 2), _flip(y, k & 1)
                src = in_refs[a].at[2 * qx + qy] if reduce else in_refs[a].at[half]
                cp = pltpu.make_async_remote_copy(
                    src_ref=src, dst_ref=out_refs[a].at[p, half], send_sem=ici_send.at[a, k - 1],
                    recv_sem=ici_recv.at[a, k - 1], device_id=(qx, qy, c), device_id_type=MESH)
                cp.start()
                first.append(cp)
        for a in range(n):
            R = out_refs[a].shape[1] // 2
            half = pl.ds(pl.multiple_of(c * R, 16), R)
            for k in range(0 if reduce else 1, 4):
                qx, qy = _flip(x, k & 2), _flip(y, k & 1)
                slot = out_refs[a].at[2 * qx + qy, half]
                if k == 0:
                    local[a].wait()
                else:
                    first[a * 3 + k - 1].wait_recv()
                cp = pltpu.make_async_remote_copy(
                    src_ref=slot, dst_ref=slot, send_sem=d2d_send.at[a, k], recv_sem=d2d_recv.at[a, k],
                    device_id=(x, y, 1 - c), device_id_type=MESH)
                cp.start()
                fwd.append(cp)
        for cp in fwd:
            cp.wait_recv()
        for cp in first + fwd:
            cp.wait_send()
        if not reduce:
            for lc in local:
                lc.wait()

    if reduce:
        out_shape = [SDS((4, 2 * a.shape[1], a.shape[2]), a.dtype) for a in arrs]
    else:
        out_shape = [SDS((4,) + a.shape, a.dtype) for a in arrs]
    return pl.pallas_call(
        body, in_specs=[ANY] * n, out_specs=[ANY] * n, out_shape=out_shape,
        scratch_shapes=[pltpu.SemaphoreType.DMA((n, 3)), pltpu.SemaphoreType.DMA((n, 3)),
                        pltpu.SemaphoreType.DMA((n, 4)), pltpu.SemaphoreType.DMA((n, 4)),
                        pltpu.SemaphoreType.DMA((n,))],
        name=name)(*arrs)


_IN_SIZES = (512, 128, 128, 512, 512, 512, 8, 512, 512, 512, 3072)
_IN_OFF = tuple(int(v) for v in np.cumsum((0,) + _IN_SIZES))


def _pack_w_in(w):
    seg = [w[:, _IN_OFF[i]:_IN_OFF[i + 1]] for i in range(11)]
    qa, ka, va, qb, kb, vb, fb, qc, kc, vc, gates = seg
    z = jnp.zeros((w.shape[0], HEAD_DIM), w.dtype)
    wqkv = jnp.concatenate([qa, ka[:, :64], z, ka[:, 64:], z, va[:, :64], z, va[:, 64:], z,
                            qb, kb, vb, qc, kc, vc], axis=1)
    wgf = jnp.concatenate([gates, fb, jnp.zeros((w.shape[0], LANE - 8), w.dtype)], axis=1)
    return wqkv, wgf


def _unpack_w_in(dqkv, dgf):
    qa = dqkv[:, 0:512]
    ka = jnp.concatenate([dqkv[:, 512:576], dqkv[:, 640:704]], axis=1)
    va = jnp.concatenate([dqkv[:, 768:832], dqkv[:, 896:960]], axis=1)
    rest = dqkv[:, 1024:QKV_COLS]
    return jnp.concatenate([qa, ka, va, rest[:, 0:1536], dgf[:, 3072:3080], rest[:, 1536:3072], dgf[:, 0:3072]],
                           axis=1)


def _pad_rows(a, rows):
    return jnp.pad(a, ((0, rows - a.shape[0]), (0, 0)))


def _small_pack(parts):
    flat = jnp.concatenate([p.reshape(-1) for p in parts])
    n = flat.shape[0]
    rows = -(-n // LANE)
    rows = -(-rows // 8) * 8
    return jnp.pad(flat, (0, rows * LANE - n)).reshape(rows, LANE)


def _small_unpack(block, shapes):
    flat = block.reshape(-1)
    out, off = [], 0
    for s in shapes:
        n = int(np.prod(s))
        out.append(flat[off:off + n].reshape(s))
        off += n
    return out


def _kv_same(g):
    return 0


def _kv_own(g):
    return g


def _layer_fwd(x, mod, p, l):
    sh_m, sc_m, g_m, sh_f, sc_f, g_f = mod
    nm = "l%d_" % l
    h1 = _norm_mod_fwd(x, p["norm_mix_g"], sc_m, sh_m, nm + "norm_mix_fwd")
    qkv = _mm(h1, p["wqkv"], mode="nn", out_dtype=BF16, name=nm + "proj_qkv")
    gf = _mm(h1, p["wgf"], mode="nn", out_dtype=F32, name=nm + "proj_gf", cap_n=640)
    qa, ka, va = qkv[:, 0:512], qkv[:, 512:768], qkv[:, 768:1024]
    o_a = _band_attn_fwd(qa, ka, va, p["alibi"], p["sink_tab"], G=4, P=A_PREV, kvoff=_kv_same, name=nm + "attn_a_fwd")
    qb, kb, vb = qkv[:, 1024:1536], qkv[:, 1536:2048], qkv[:, 2048:2560]
    cum = _fox_cum(gf, p["b_forget_pad"], nm + "fox_cum")
    cum_t = cum[:, :N_HEADS].T
    cc, cr = cum_t[:, :, None], cum_t[:, None, :]
    o_b, lse_b = _fox_fwd(qb, kb, vb, cc, cr, nm + "attn_b_fwd")
    qc, kc, vc = qkv[:, 2560:3072], qkv[:, 3072:3584], qkv[:, 3584:4096]
    o_c = _band_attn_fwd(qc, kc, vc, p["rel_tab"], p["no_sink"], G=2, P=C_PREV, kvoff=_kv_own, name=nm + "attn_c_fwd")
    o = jnp.concatenate([o_a, o_b, o_c], axis=1)
    y = _mm(o, p["wb"], mode="nn", out_dtype=F32, groups=3, name=nm + "branch")
    merged = _merge_fwd(y, gf, nm + "merge_fwd")
    mix = _mm(merged, p["wout"], mode="nn", out_dtype=F32, name=nm + "out_proj")
    x1 = _resid_fwd(x, mix, g_m, nm + "resid_mix")
    h2 = _norm_mod_fwd(x1, p["norm_ffn_g"], sc_f, sh_f, nm + "norm_ffn_fwd")
    u = _mm(h2, p["wfi"], mode="nn", out_dtype=F32, name=nm + "ffn_in", cap_n=1408)
    a = _swiglu_fwd(u, nm + "swiglu_fwd")
    f = _mm(a, p["wfo"], mode="nn", out_dtype=F32, name=nm + "ffn_out")
    x2 = _resid_fwd(x1, f, g_f, nm + "resid_ffn")
    saved = dict(x=x, h1=h1, qkv=qkv, gf=gf, cc=cc, cr=cr, o_b=o_b, lse_b=lse_b, o=o, y=y, merged=merged,
                 mix=mix, x1=x1, h2=h2, u=u, a=a, f=f)
    return x2, saved


def _layer_bwd(dx2, mod, p, s, l):
    sh_m, sc_m, g_m, sh_f, sc_f, g_f = mod
    nm = "l%d_" % l
    dg_f, df = _resid_bwd(dx2, s["f"], g_f, nm + "resid_ffn_bwd")
    da = _mm(df, p["wfo"], mode="nt", out_dtype=F32, name=nm + "ffn_out_dx", cap_n=1408)
    d_wfo = _mm(s["a"], df, mode="tn", out_dtype=F32, name=nm + "ffn_out_dw", cap_m=1408, cap_k=512)
    du = _swiglu_bwd(da, s["u"], nm + "swiglu_bwd")
    dh2 = _mm(du, p["wfi"], mode="nt", out_dtype=F32, name=nm + "ffn_in_dx")
    d_wfi = _mm(s["h2"], du, mode="tn", out_dtype=F32, name=nm + "ffn_in_dw", cap_n=1408, cap_k=512)
    dx1, dsc_f, dsh_f, dgn_f = _norm_mod_bwd(s["x1"], [dh2], dx2, p["norm_ffn_g"], sc_f, nm + "norm_ffn_bwd")
    dg_m, dmix = _resid_bwd(dx1, s["mix"], g_m, nm + "resid_mix_bwd")
    dmerged = _mm(dmix, p["wout"], mode="nt", out_dtype=F32, name=nm + "out_proj_dx")
    d_wout = _mm(s["merged"], dmix, mode="tn", out_dtype=F32, name=nm + "out_proj_dw", cap_k=512)
    dy, dgates = _merge_bwd(dmerged, s["y"], s["gf"], nm + "merge_bwd")
    do = _mm(dy, p["wb"], mode="nt", out_dtype=BF16, groups=3, name=nm + "branch_dx")
    d_wb = _mm(s["o"], dy, mode="tn", out_dtype=F32, groups=3, name=nm + "branch_dw", cap_k=512)
    qkv = s["qkv"]
    qa, ka, va = qkv[:, 0:512], qkv[:, 512:768], qkv[:, 768:1024]
    dqa, dka, dva, _, dsink = _band_attn_bwd(qa, ka, va, p["alibi"], p["sink_tab"], do[:, 0:512], G=4, P=A_PREV,
                                             kvoff=_kv_same, name=nm + "attn_a_bwd")
    qb, kb, vb = qkv[:, 1024:1536], qkv[:, 1536:2048], qkv[:, 2048:2560]
    dqb, dkb, dvb, dcr, dcc = _fox_bwd(qb, kb, vb, s["cc"], s["cr"], s["o_b"], do[:, 512:1024], s["lse_b"],
                                       nm + "attn_b_bwd")
    dcum = jnp.pad((dcr[:, 0, :] + dcc[:, :, 0]).T, ((0, 0), (0, LANE - N_HEADS)))
    dfb, db_forget = _fox_cum_bwd(s["gf"], p["b_forget_pad"], dcum, nm + "fox_cum_bwd")
    qc, kc, vc = qkv[:, 2560:3072], qkv[:, 3072:3584], qkv[:, 3584:4096]
    dqc, dkc, dvc, dbias_c, _ = _band_attn_bwd(qc, kc, vc, p["rel_tab"], p["no_sink"], do[:, 1024:1536], G=2,
                                               P=C_PREV, kvoff=_kv_own, name=nm + "attn_c_bwd")
    d_rel = _rel_reduce(jnp.transpose(dbias_c, (1, 0, 2)), nm + "rel_reduce")[:, :N_REL]
    dqkv = jnp.concatenate([dqa, dka, dva, dqb, dkb, dvb, dqc, dkc, dvc], axis=1)
    dgf = jnp.concatenate([dgates, dfb], axis=1)
    dh1a = _mm(dqkv, p["wqkv"], mode="nt", out_dtype=F32, name=nm + "proj_qkv_dx", cap_k=1024)
    dh1b = _mm(dgf, p["wgf"], mode="nt", out_dtype=F32, name=nm + "proj_gf_dx", cap_k=640)
    d_wqkv = _mm(s["h1"], dqkv, mode="tn", out_dtype=F32, name=nm + "proj_qkv_dw", cap_k=512)
    d_wgf = _mm(s["h1"], dgf, mode="tn", out_dtype=F32, name=nm + "proj_gf_dw", cap_n=640, cap_k=512)
    dx, dsc_m, dsh_m, dgn_m = _norm_mod_bwd(s["x"], [dh1a, dh1b], dx1, p["norm_mix_g"], sc_m, nm + "norm_mix_bwd")
    d_mod = jnp.concatenate([dsh_m, dsc_m, dg_m, dsh_f, dsc_f, dg_f], axis=1)[0]
    grads = dict(w_in=_unpack_w_in(d_wqkv, d_wgf), w_branch=d_wb, w_out=d_wout, w_ffn_in=d_wfi, w_ffn_out=d_wfo,
                 norm_mix_g=dgn_m[0], norm_ffn_g=dgn_f[0], b_forget=db_forget[0, :N_HEADS],
                 sinks=dsink[:, 0, 0], rel_bias=d_rel, d_mod=d_mod)
    return dx, grads


def kernel(x, c, norm_mix_g, norm_ffn_g, w_ada, b_ada, w_in, b_forget, sinks, rel_bias, w_branch, w_out, w_ffn_in, w_ffn_out, final_norm_g, loss_target, m_norm_mix_g, m_norm_ffn_g, m_w_ada, m_b_ada, m_w_in, m_b_forget, m_sinks, m_rel_bias, m_w_branch, m_w_out, m_w_ffn_in, m_w_ffn_out, m_final_norm_g, v_norm_mix_g, v_norm_ffn_g, v_w_ada, v_b_ada, v_w_in, v_b_forget, v_sinks, v_rel_bias, v_w_branch, v_w_out, v_w_ffn_in, v_w_ffn_out, v_final_norm_g):
    xi, yi, ci = _coords()
    chip = 2 * xi + yi
    dev = 2 * chip + ci
    xs = x[0]
    S = xs.shape[0]
    n_ada = w_ada.shape[2]

    big_names = ("w_in", "w_branch", "w_out", "w_ffn_in", "w_ffn_out")
    big_w = dict(w_in=w_in, w_branch=w_branch, w_out=w_out, w_ffn_in=w_ffn_in, w_ffn_out=w_ffn_out)
    big_m = dict(w_in=m_w_in, w_branch=m_w_branch, w_out=m_w_out, w_ffn_in=m_w_ffn_in, w_ffn_out=m_w_ffn_out)
    big_v = dict(w_in=v_w_in, w_branch=v_w_branch, w_out=v_w_out, w_ffn_in=v_w_ffn_in, w_ffn_out=v_w_ffn_out)
    flat2 = lambda a: a.reshape(-1, a.shape[-1])
    shards = [flat2(big_w[n]).astype(BF16) for n in big_names]
    gw_in, gw_branch, gw_out, gw_ffn_in, gw_ffn_out = _chip_exchange(shards, reduce=False, name="weights_all_gather")
    cin = w_in.shape[2]
    cbr = w_branch.shape[3]
    rout = w_out.shape[1]
    cfi = w_ffn_in.shape[2]
    rfo = w_ffn_out.shape[1]
    w_in_full = gw_in.reshape(4, DEPTH, D_MODEL, cin).transpose(1, 2, 0, 3).reshape(DEPTH, D_MODEL, 4 * cin)
    w_branch_full = gw_branch.reshape(4, DEPTH, 3, BRANCH_W, cbr).transpose(1, 2, 3, 0, 4).reshape(
        DEPTH, 3 * BRANCH_W, 4 * cbr)
    w_out_full = gw_out.reshape(4, DEPTH, rout, D_MODEL).transpose(1, 0, 2, 3).reshape(DEPTH, 4 * rout, D_MODEL)
    w_ffn_in_full = gw_ffn_in.reshape(4, DEPTH, D_MODEL, cfi).transpose(1, 2, 0, 3).reshape(DEPTH, D_MODEL, 4 * cfi)
    w_ffn_out_full = gw_ffn_out.reshape(4, DEPTH, rfo, D_MODEL).transpose(1, 0, 2, 3).reshape(DEPTH, 4 * rfo, D_MODEL)

    c_all = _all_gather8(c.reshape(8, LANE), "gather_c").reshape(8, D_MODEL)
    b_sh = lax.dynamic_slice_in_dim(b_ada, chip * n_ada, n_ada, axis=1)[:, None, :]
    mod_sh = _ada_fwd(_pad_rows(c_all, 16), w_ada, b_sh, "ada_fwd")[:, :8, :]
    mod_all = _all_gather8(mod_sh.reshape(-1, LANE), "gather_mod").reshape(8, DEPTH, 8, n_ada)
    mod_mine = lax.dynamic_index_in_dim(mod_all[0::2], dev, axis=2, keepdims=False)
    mod = mod_mine.transpose(1, 0, 2).reshape(DEPTH, 6, D_MODEL)

    alibi = _alibi_table()
    no_sink = jnp.full((N_HEADS, 8, LANE), NEG_INF, F32)
    params = []
    for l in range(DEPTH):
        wqkv, wgf = _pack_w_in(w_in_full[l])
        rel_tab = _rel_expand(jnp.pad(rel_bias[l], ((0, 0), (0, N_REL_PAD - N_REL))), "l%d_rel_expand" % l)
        params.append(dict(
            wqkv=wqkv, wgf=wgf, wb=w_branch_full[l], wout=w_out_full[l], wfi=w_ffn_in_full[l], wfo=w_ffn_out_full[l],
            norm_mix_g=norm_mix_g[l][None], norm_ffn_g=norm_ffn_g[l][None],
            b_forget_pad=jnp.pad(b_forget[l], (0, LANE - N_HEADS))[None],
            sink_tab=jnp.broadcast_to(sinks[l][:, None, None], (N_HEADS, 8, LANE)),
            no_sink=no_sink, alibi=alibi, rel_tab=jnp.transpose(rel_tab, (1, 0, 2))))
    mods = [[mod[l, k][None] for k in range(6)] for l in range(DEPTH)]
    h = xs
    saved = []
    for l in range(DEPTH):
        h, s = _layer_fwd(h, mods[l], params[l], l)
        saved.append(s)
    loss_dev, dh, d_final = _final_loss(h, final_norm_g[None], loss_target[0], "final_loss")
    grads = [None] * DEPTH
    for l in reversed(range(DEPTH)):
        dh, grads[l] = _layer_bwd(dh, mods[l], params[l], saved[l], l)
    grad_x = dh[None]
    loss = lax.psum(loss_dev[0, 0], ("x", "y", "c"))

    def by_quarter(name, g):
        if name == "w_in":
            return g.reshape(DEPTH, D_MODEL, 4, cin).transpose(0, 2, 1, 3)
        if name == "w_branch":
            return g.reshape(DEPTH, 3 * BRANCH_W, 4, cbr).transpose(0, 2, 1, 3)
        if name == "w_out":
            return g.reshape(DEPTH, 4, rout, D_MODEL)
        if name == "w_ffn_in":
            return g.reshape(DEPTH, D_MODEL, 4, cfi).transpose(0, 2, 1, 3)
        return g.reshape(DEPTH, 4, rfo, D_MODEL)

    full = [by_quarter(n, jnp.stack([grads[l][n] for l in range(DEPTH)])) for n in big_names]
    theirs = _sibling_swap(full, "grads_sibling_swap")
    mine = [lax.dynamic_index_in_dim(g, ci, axis=0, keepdims=False) for g in full]
    chip_sum = [_add_cast(a, b, "grads_chip_sum_%s" % n) for n, a, b in zip(big_names, mine, theirs)]
    parts = _chip_exchange(chip_sum, reduce=True, name="grads_reduce_scatter")
    big_out = {}
    for n, pt in zip(big_names, parts):
        shp = big_w[n].shape
        res = _adamw(flat2(big_w[n]), flat2(big_m[n]), flat2(big_v[n]), pt, "adamw_" + n)
        big_out[n] = [r.reshape(shp) for r in res]

    small_names = ("norm_mix_g", "norm_ffn_g", "b_ada", "b_forget", "sinks", "rel_bias", "final_norm_g")
    small_w = dict(norm_mix_g=norm_mix_g, norm_ffn_g=norm_ffn_g, b_ada=b_ada, b_forget=b_forget, sinks=sinks,
                   rel_bias=rel_bias, final_norm_g=final_norm_g)
    small_m = dict(norm_mix_g=m_norm_mix_g, norm_ffn_g=m_norm_ffn_g, b_ada=m_b_ada, b_forget=m_b_forget,
                   sinks=m_sinks, rel_bias=m_rel_bias, final_norm_g=m_final_norm_g)
    small_v = dict(norm_mix_g=v_norm_mix_g, norm_ffn_g=v_norm_ffn_g, b_ada=v_b_ada, b_forget=v_b_forget,
                   sinks=v_sinks, rel_bias=v_rel_bias, final_norm_g=v_final_norm_g)
    small_g = dict(
        norm_mix_g=jnp.stack([grads[l]["norm_mix_g"] for l in range(DEPTH)]),
        norm_ffn_g=jnp.stack([grads[l]["norm_ffn_g"] for l in range(DEPTH)]),
        b_ada=jnp.stack([grads[l]["d_mod"] for l in range(DEPTH)]),
        b_forget=jnp.stack([grads[l]["b_forget"] for l in range(DEPTH)]),
        sinks=jnp.stack([grads[l]["sinks"] for l in range(DEPTH)]),
        rel_bias=jnp.stack([grads[l]["rel_bias"] for l in range(DEPTH)]),
        final_norm_g=d_final[0])
    shapes = [small_w[n].shape for n in small_names]
    g_all = _all_gather8(_small_pack([small_g[n] for n in small_names]), "gather_small_grads")
    res = _adamw(_small_pack([small_w[n] for n in small_names]), _small_pack([small_m[n] for n in small_names]),
                 _small_pack([small_v[n] for n in small_names]), g_all, "adamw_small")
    small_out = {n: [] for n in small_names}
    for r in res:
        for n, a in zip(small_names, _small_unpack(r, shapes)):
            small_out[n].append(a)
    off_b = sum(int(np.prod(s)) for s in shapes[:2])
    n_mod = DEPTH * 6 * D_MODEL
    dmod_all = g_all.reshape(8, -1)[:, off_b:off_b + n_mod].reshape(8, DEPTH, 6 * D_MODEL)
    dmod_sh = lax.dynamic_slice_in_dim(dmod_all, chip * n_ada, n_ada, axis=2).transpose(1, 0, 2)
    g_ada = _ada_bwd(c_all.T, dmod_sh, "ada_bwd")
    res = _adamw(flat2(w_ada), flat2(m_w_ada), flat2(v_w_ada), flat2(g_ada)[None], "adamw_w_ada")
    ada_out = [r.reshape(w_ada.shape) for r in res]

    order = ("norm_mix_g", "norm_ffn_g", "w_ada", "b_ada", "w_in", "b_forget", "sinks", "rel_bias", "w_branch",
             "w_out", "w_ffn_in", "w_ffn_out", "final_norm_g")

    def pick(n, k):
        if n == "w_ada":
            return ada_out[k]
        if n in big_out:
            return big_out[n][k]
        return small_out[n][k]

    outs = [loss, grad_x]
    for k in range(4):
        outs += [pick(n, k) for n in order]
    return tuple(outs)
```

```python
import functools

import numpy as np
import jax
import jax.numpy as jnp
from jax import lax
from jax.experimental import pallas as pl
from jax.experimental.pallas import tpu as pltpu

F32 = jnp.float32
BF16 = jnp.bfloat16
SDS = jax.ShapeDtypeStruct

D_MODEL = 1024
DEPTH = 2
CHUNK = 64
HEAD_DIM = 64
EPS = 1e-6
NEG_INF = -1e30
N_HEADS = 8
A_KV_HEADS = 2
A_PREV = 2
C_PREV = 8
REL_CLIP = 128
N_REL = 2 * REL_CLIP + 1
N_REL_PAD = 384
BRANCH_W = 512
FFN_H = 2816
FOX_BQ = 256
FOX_BK = 512
BAND_UNROLL_FWD = 4
BAND_UNROLL_BWD = 2
QKV_COLS = 4096
GF_COLS = 3200
N_IN_COLS = 6920
LANE = 128
VMEM_LIMIT = 48 * 1024 * 1024

ADAM_LR = 0.001
ADAM_B1 = 0.9
ADAM_B2 = 0.999
ADAM_EPS = 1e-08
ADAM_WD = 0.01
ADAM_STEP = 10

MESH = pl.DeviceIdType.MESH
ANY = pl.BlockSpec(memory_space=pl.ANY)
VMEM_SPEC = pl.BlockSpec(memory_space=pltpu.VMEM)


def _cparams(sem=None):
    return pltpu.CompilerParams(dimension_semantics=sem, vmem_limit_bytes=VMEM_LIMIT)


def _blk(n, cap):
    if n <= cap:
        return n
    best = None
    for m in range(LANE, cap + 1, LANE):
        if n % m == 0:
            best = m
    assert best is not None, (n, cap)
    return best


def _sigmoid(x):
    return 1.0 / (1.0 + jnp.exp(-x))


def _mm(a, b, *, mode, out_dtype, name, groups=1, cap_m=512, cap_n=1024, cap_k=1408):
    G = groups
    if mode == "nn":
        M, K, N = a.shape[0], a.shape[1] // G, b.shape[1]
        assert b.shape[0] == G * K
    elif mode == "nt":
        M, K, N = a.shape[0], a.shape[1] // G, b.shape[0] // G
        assert b.shape[1] == K
    else:
        K, M, N = a.shape[0], a.shape[1] // G, b.shape[1] // G
        assert b.shape[0] == K
    bm, bn, bk = _blk(M, cap_m), _blk(N, cap_n), _blk(K, cap_k)
    nm, nn, nk = M // bm, N // bn, K // bk
    if mode == "nn":
        a_spec = pl.BlockSpec((bm, bk), lambda g, i, j, k: (i, g * nk + k))
        b_spec = pl.BlockSpec((bk, bn), lambda g, i, j, k: (g * nk + k, j))
        o_spec = pl.BlockSpec((bm, bn), lambda g, i, j, k: (i, g * nn + j))
        dims = (((1,), (0,)), ((), ()))
        out_shape = (M, G * N)
    elif mode == "nt":
        a_spec = pl.BlockSpec((bm, bk), lambda g, i, j, k: (i, g * nk + k))
        b_spec = pl.BlockSpec((bn, bk), lambda g, i, j, k: (g * nn + j, k))
        o_spec = pl.BlockSpec((bm, bn), lambda g, i, j, k: (i, g * nn + j))
        dims = (((1,), (1,)), ((), ()))
        out_shape = (M, G * N)
    else:
        a_spec = pl.BlockSpec((bk, bm), lambda g, i, j, k: (k, g * nm + i))
        b_spec = pl.BlockSpec((bk, bn), lambda g, i, j, k: (k, g * nn + j))
        o_spec = pl.BlockSpec((bm, bn), lambda g, i, j, k: (g * nm + i, j))
        dims = (((0,), (0,)), ((), ()))
        out_shape = (G * M, N)

    def body(a_ref, b_ref, o_ref, acc_ref):
        k = pl.program_id(3)

        @pl.when(k == 0)
        def _():
            acc_ref[...] = jnp.zeros_like(acc_ref)

        acc_ref[...] += lax.dot_general(a_ref[...].astype(BF16), b_ref[...].astype(BF16), dims,
                                        preferred_element_type=F32)

        @pl.when(k == nk - 1)
        def _():
            o_ref[...] = acc_ref[...].astype(o_ref.dtype)

    return pl.pallas_call(
        body, grid=(G, nm, nn, nk), in_specs=[a_spec, b_spec], out_specs=o_spec,
        out_shape=SDS(out_shape, out_dtype), scratch_shapes=[pltpu.VMEM((bm, bn), F32)],
        compiler_params=_cparams(("parallel", "parallel", "parallel", "arbitrary")), name=name,
    )(a, b)


def _rows(tm, n, col=0):
    return pl.BlockSpec((tm, n), lambda i: (i, col))


def _vec(n):
    return pl.BlockSpec((1, n), lambda i: (0, 0))


def _tm(S):
    return min(S, 256)


def _norm_mod_fwd(x, g, sc, sh, name):
    S, Dm = x.shape
    tm = _tm(S)

    def body(x_ref, g_ref, sc_ref, sh_ref, h_ref):
        xv = x_ref[...]
        r = lax.rsqrt(jnp.mean(xv * xv, axis=-1, keepdims=True) + EPS)
        h_ref[...] = ((xv * r) * g_ref[...] * (1.0 + sc_ref[...]) + sh_ref[...]).astype(h_ref.dtype)

    return pl.pallas_call(
        body, grid=(S // tm,), in_specs=[_rows(tm, Dm), _vec(Dm), _vec(Dm), _vec(Dm)],
        out_specs=_rows(tm, Dm), out_shape=SDS((S, Dm), BF16),
        compiler_params=_cparams(("parallel",)), name=name)(x, g, sc, sh)


def _norm_mod_bwd(x, dh_list, dres, g, sc, name):
    S, Dm = x.shape
    tm = _tm(S)
    nh = len(dh_list)

    def body(*refs):
        x_ref = refs[0]
        dh_refs = refs[1:1 + nh]
        dres_ref, g_ref, sc_ref, dx_ref, dsc_ref, dsh_ref, dg_ref = refs[1 + nh:]
        i = pl.program_id(0)

        @pl.when(i == 0)
        def _():
            dsc_ref[...] = jnp.zeros_like(dsc_ref)
            dsh_ref[...] = jnp.zeros_like(dsh_ref)
            dg_ref[...] = jnp.zeros_like(dg_ref)

        xv = x_ref[...]
        dh = dh_refs[0][...]
        for r_ in dh_refs[1:]:
            dh = dh + r_[...]
        gv = g_ref[...]
        r = lax.rsqrt(jnp.mean(xv * xv, axis=-1, keepdims=True) + EPS)
        xn = xv * r
        xg = xn * gv
        dsh_ref[...] += jnp.sum(dh, axis=0, keepdims=True)
        dsc_ref[...] += jnp.sum(dh * xg, axis=0, keepdims=True)
        dxg = dh * (1.0 + sc_ref[...])
        dg_ref[...] += jnp.sum(dxg * xn, axis=0, keepdims=True)
        dxn = dxg * gv
        dx_ref[...] = dres_ref[...] + r * (dxn - xn * jnp.mean(dxn * xn, axis=-1, keepdims=True))

    return pl.pallas_call(
        body, grid=(S // tm,),
        in_specs=[_rows(tm, Dm)] * (2 + nh) + [_vec(Dm), _vec(Dm)],
        out_specs=[_rows(tm, Dm), _vec(Dm), _vec(Dm), _vec(Dm)],
        out_shape=[SDS((S, Dm), F32), SDS((1, Dm), F32), SDS((1, Dm), F32), SDS((1, Dm), F32)],
        compiler_params=_cparams(("arbitrary",)), name=name)(x, *dh_list, dres, g, sc)


def _resid_fwd(x, val, g, name):
    S, Dm = x.shape
    tm = _tm(S)

    def body(x_ref, v_ref, g_ref, o_ref):
        o_ref[...] = x_ref[...] + g_ref[...] * v_ref[...]

    return pl.pallas_call(
        body, grid=(S // tm,), in_specs=[_rows(tm, Dm), _rows(tm, Dm), _vec(Dm)],
        out_specs=_rows(tm, Dm), out_shape=SDS((S, Dm), F32),
        compiler_params=_cparams(("parallel",)), name=name)(x, val, g)


def _resid_bwd(dx, val, g, name):
    S, Dm = dx.shape
    tm = _tm(S)

    def body(dx_ref, v_ref, g_ref, dg_ref, dv_ref):
        @pl.when(pl.program_id(0) == 0)
        def _():
            dg_ref[...] = jnp.zeros_like(dg_ref)

        dxv = dx_ref[...]
        dg_ref[...] += jnp.sum(dxv * v_ref[...], axis=0, keepdims=True)
        dv_ref[...] = (dxv * g_ref[...]).astype(dv_ref.dtype)

    return pl.pallas_call(
        body, grid=(S // tm,), in_specs=[_rows(tm, Dm), _rows(tm, Dm), _vec(Dm)],
        out_specs=[_vec(Dm), _rows(tm, Dm)], out_shape=[SDS((1, Dm), F32), SDS((S, Dm), BF16)],
        compiler_params=_cparams(("arbitrary",)), name=name)(dx, val, g)


def _merge_fwd(y, gf, name):
    S = y.shape[0]
    tm = _tm(S)
    W = 3 * D_MODEL

    def body(y_ref, g_ref, o_ref):
        acc = None
        for k in range(3):
            sl = slice(k * D_MODEL, (k + 1) * D_MODEL)
            t = _sigmoid(g_ref[:, sl]) * y_ref[:, sl]
            acc = t if acc is None else acc + t
        o_ref[...] = acc.astype(o_ref.dtype)

    return pl.pallas_call(
        body, grid=(S // tm,), in_specs=[_rows(tm, W), _rows(tm, W)],
        out_specs=_rows(tm, D_MODEL), out_shape=SDS((S, D_MODEL), BF16),
        compiler_params=_cparams(("parallel",)), name=name)(y, gf)


def _merge_bwd(dm, y, gf, name):
    S = y.shape[0]
    tm = _tm(S)
    W = 3 * D_MODEL

    def body(dm_ref, y_ref, g_ref, dy_ref, dg_ref):
        dmv = dm_ref[...]
        for k in range(3):
            sl = slice(k * D_MODEL, (k + 1) * D_MODEL)
            sg = _sigmoid(g_ref[:, sl])
            dy_ref[:, sl] = (dmv * sg).astype(dy_ref.dtype)
            dg_ref[:, sl] = (dmv * y_ref[:, sl] * (sg * (1.0 - sg))).astype(dg_ref.dtype)

    return pl.pallas_call(
        body, grid=(S // tm,), in_specs=[_rows(tm, D_MODEL), _rows(tm, W), _rows(tm, W)],
        out_specs=[_rows(tm, W), _rows(tm, W)], out_shape=[SDS((S, W), BF16), SDS((S, W), BF16)],
        compiler_params=_cparams(("parallel",)), name=name)(dm, y, gf)


def _swiglu_fwd(u, name):
    S = u.shape[0]
    tm = _tm(S)

    def body(g_ref, u_ref, a_ref):
        gv = g_ref[...]
        a_ref[...] = (gv * _sigmoid(gv) * u_ref[...]).astype(a_ref.dtype)

    return pl.pallas_call(
        body, grid=(S // tm,), in_specs=[_rows(tm, FFN_H, 0), _rows(tm, FFN_H, 1)],
        out_specs=_rows(tm, FFN_H), out_shape=SDS((S, FFN_H), BF16),
        compiler_params=_cparams(("parallel",)), name=name)(u, u)


def _swiglu_bwd(da, u, name):
    S = u.shape[0]
    tm = _tm(S)

    def body(da_ref, g_ref, u_ref, du_ref):
        dav = da_ref[...]
        gv = g_ref[...]
        sg = _sigmoid(gv)
        du_ref[:, 0:FFN_H] = (dav * u_ref[...] * (sg * (1.0 + gv * (1.0 - sg)))).astype(du_ref.dtype)
        du_ref[:, FFN_H:2 * FFN_H] = (dav * (gv * sg)).astype(du_ref.dtype)

    return pl.pallas_call(
        body, grid=(S // tm,), in_specs=[_rows(tm, FFN_H), _rows(tm, FFN_H, 0), _rows(tm, FFN_H, 1)],
        out_specs=_rows(tm, 2 * FFN_H), out_shape=SDS((S, 2 * FFN_H), BF16),
        compiler_params=_cparams(("parallel",)), name=name)(da, u, u)


def _final_loss(x, g, target, name):
    S, Dm = x.shape
    tm = _tm(S)

    def body(x_ref, g_ref, t_ref, loss_ref, dx_ref, dg_ref):
        @pl.when(pl.program_id(0) == 0)
        def _():
            loss_ref[...] = jnp.zeros_like(loss_ref)
            dg_ref[...] = jnp.zeros_like(dg_ref)

        xv = x_ref[...]
        gv = g_ref[...]
        r = lax.rsqrt(jnp.mean(xv * xv, axis=-1, keepdims=True) + EPS)
        xn = xv * r
        err = xn * gv - t_ref[...]
        row = jnp.mean(err * err, axis=-1, keepdims=True)
        loss_ref[...] += 0.5 * jnp.sum(row, axis=0, keepdims=True)
        dy = err * (1.0 / Dm)
        dg_ref[...] += jnp.sum(dy * xn, axis=0, keepdims=True)
        dxn = dy * gv
        dx_ref[...] = r * (dxn - xn * jnp.mean(dxn * xn, axis=-1, keepdims=True))

    return pl.pallas_call(
        body, grid=(S // tm,), in_specs=[_rows(tm, Dm), _vec(Dm), _rows(tm, Dm)],
        out_specs=[pl.BlockSpec((1, 1), lambda i: (0, 0)), _rows(tm, Dm), _vec(Dm)],
        out_shape=[SDS((1, 1), F32), SDS((S, Dm), F32), SDS((1, Dm), F32)],
        compiler_params=_cparams(("arbitrary",)), name=name)(x, g, target)


def _band_softmax(qg, kg, bias, sink, valid):
    s = lax.dot_general(qg, kg, (((1,), (1,)), ((), ())), preferred_element_type=F32)
    s = jnp.where(valid, s + bias, NEG_INF)
    m = jnp.maximum(jnp.max(s, axis=-1, keepdims=True), sink)
    e = jnp.exp(s - m)
    es = jnp.exp(sink - m)
    l = jnp.sum(e, axis=-1, keepdims=True) + es
    return e / l, es / l


def _band_attn_fwd(q, k, v, bias, sink, *, G, P, kvoff, name):
    S = q.shape[0]
    ng = q.shape[1] // (G * HEAD_DIM)
    band = (P + 1) * CHUNK
    pad = P * CHUNK
    nc = S // CHUNK

    def body(q_ref, k_ref, v_ref, b_ref, s_ref, o_ref, kp, vp):
        kp[0:pad, :] = jnp.zeros((pad, LANE), BF16)
        vp[0:pad, :] = jnp.zeros((pad, LANE), BF16)
        kp[pad:pad + S, :] = k_ref[...]
        vp[pad:pad + S, :] = v_ref[...]
        col = lax.broadcasted_iota(jnp.int32, (CHUNK, band), 1)

        def step(n, carry):
            r = pl.multiple_of(n * CHUNK, CHUNK)
            qn = q_ref[pl.ds(r, CHUNK), :]
            kb = kp[pl.ds(r, band), :]
            vb = vp[pl.ds(r, band), :]
            valid = col >= (P - n) * CHUNK
            for g in range(G):
                ko = kvoff(g) * HEAD_DIM
                qg = qn[:, g * HEAD_DIM:(g + 1) * HEAD_DIM] * 0.125
                p, _ = _band_softmax(qg, kb[:, ko:ko + HEAD_DIM], b_ref[g], s_ref[g, 0:1, 0:1], valid)
                og = jnp.dot(p.astype(BF16), vb[:, ko:ko + HEAD_DIM], preferred_element_type=F32)
                o_ref[pl.ds(r, CHUNK), g * HEAD_DIM:(g + 1) * HEAD_DIM] = og.astype(o_ref.dtype)
            return carry

        lax.fori_loop(0, nc, step, 0, unroll=min(BAND_UNROLL_FWD, nc))

    GW = G * HEAD_DIM
    return pl.pallas_call(
        body, grid=(ng,),
        in_specs=[pl.BlockSpec((S, GW), lambda i: (0, i)), pl.BlockSpec((S, LANE), lambda i: (0, i)),
                  pl.BlockSpec((S, LANE), lambda i: (0, i)),
                  pl.BlockSpec((G, CHUNK, band), lambda i: (i, 0, 0)),
                  pl.BlockSpec((G, 8, LANE), lambda i: (i, 0, 0))],
        out_specs=pl.BlockSpec((S, GW), lambda i: (0, i)),
        out_shape=SDS((S, ng * GW), BF16),
        scratch_shapes=[pltpu.VMEM((S + pad, LANE), BF16), pltpu.VMEM((S + pad, LANE), BF16)],
        compiler_params=_cparams(("parallel",)), name=name)(q, k, v, bias, sink)


def _band_attn_bwd(q, k, v, bias, sink, do, *, G, P, kvoff, name):
    S = q.shape[0]
    ng = q.shape[1] // (G * HEAD_DIM)
    band = (P + 1) * CHUNK
    pad = P * CHUNK
    nc = S // CHUNK
    TN = (((0,), (0,)), ((), ()))

    def body(q_ref, k_ref, v_ref, b_ref, s_ref, do_ref, dq_ref, dk_ref, dv_ref, db_ref, dsk_ref,
             kp, vp, dkp, dvp):
        kp[0:pad, :] = jnp.zeros((pad, LANE), BF16)
        vp[0:pad, :] = jnp.zeros((pad, LANE), BF16)
        kp[pad:pad + S, :] = k_ref[...]
        vp[pad:pad + S, :] = v_ref[...]
        dkp[...] = jnp.zeros_like(dkp)
        dvp[...] = jnp.zeros_like(dvp)
        db_ref[...] = jnp.zeros_like(db_ref)
        col = lax.broadcasted_iota(jnp.int32, (CHUNK, band), 1)

        def step(n, dsink):
            r = pl.multiple_of(n * CHUNK, CHUNK)
            qn = q_ref[pl.ds(r, CHUNK), :]
            don = do_ref[pl.ds(r, CHUNK), :]
            kb = kp[pl.ds(r, band), :]
            vb = vp[pl.ds(r, band), :]
            valid = col >= (P - n) * CHUNK
            new = []
            for g in range(G):
                ko = kvoff(g) * HEAD_DIM
                lanes = slice(g * HEAD_DIM, (g + 1) * HEAD_DIM)
                qg = qn[:, lanes] * 0.125
                kg = kb[:, ko:ko + HEAD_DIM]
                dog = don[:, lanes]
                p, ps = _band_softmax(qg, kg, b_ref[g], s_ref[g, 0:1, 0:1], valid)
                dp = lax.dot_general(dog, vb[:, ko:ko + HEAD_DIM], (((1,), (1,)), ((), ())),
                                     preferred_element_type=F32)
                delta = jnp.sum(p * dp, axis=-1, keepdims=True)
                ds = p * (dp - delta)
                new.append(dsink[g] - jnp.sum(ps * delta, axis=0, keepdims=True))
                db_ref[g] += ds
                dsb = ds.astype(BF16)
                dq = jnp.dot(dsb, kg, preferred_element_type=F32) * 0.125
                dq_ref[pl.ds(r, CHUNK), lanes] = dq.astype(dq_ref.dtype)
                dkp[pl.ds(r, band), ko:ko + HEAD_DIM] += lax.dot_general(
                    dsb, qg, TN, preferred_element_type=F32)
                dvp[pl.ds(r, band), ko:ko + HEAD_DIM] += lax.dot_general(
                    p.astype(BF16), dog, TN, preferred_element_type=F32)
            return tuple(new)

        dsink = lax.fori_loop(0, nc, step, tuple(jnp.zeros((1, 1), F32) for _ in range(G)),
                              unroll=min(BAND_UNROLL_BWD, nc))
        for g in range(G):
            dsk_ref[g] = jnp.broadcast_to(dsink[g], (8, LANE))
        dk_ref[...] = dkp[pad:pad + S, :].astype(dk_ref.dtype)
        dv_ref[...] = dvp[pad:pad + S, :].astype(dv_ref.dtype)

    GW = G * HEAD_DIM
    qs = pl.BlockSpec((S, GW), lambda i: (0, i))
    ks = pl.BlockSpec((S, LANE), lambda i: (0, i))
    bs = pl.BlockSpec((G, CHUNK, band), lambda i: (i, 0, 0))
    ss = pl.BlockSpec((G, 8, LANE), lambda i: (i, 0, 0))
    return pl.pallas_call(
        body, grid=(ng,), in_specs=[qs, ks, ks, bs, ss, qs],
        out_specs=[qs, ks, ks, bs, ss],
        out_shape=[SDS((S, ng * GW), BF16), SDS((S, ng * LANE), BF16), SDS((S, ng * LANE), BF16),
                   SDS((ng * G, CHUNK, band), F32), SDS((ng * G, 8, LANE), F32)],
        scratch_shapes=[pltpu.VMEM((S + pad, LANE), BF16), pltpu.VMEM((S + pad, LANE), BF16),
                        pltpu.VMEM((S + pad, LANE), F32), pltpu.VMEM((S + pad, LANE), F32)],
        compiler_params=_cparams(("parallel",)), name=name)(q, k, v, bias, sink, do)


def _fox_logits(qg, kj, cq, ck, r, c, row, col):
    s = lax.dot_general(qg, kj, (((1,), (1,)), ((), ())), preferred_element_type=F32)
    s = s + cq - ck
    return jnp.where(c + col <= r + row, s, NEG_INF)


def _fox_fwd(q, k, v, cc, cr, name):
    S = q.shape[0]
    npair = q.shape[1] // LANE
    BQ, BK = min(FOX_BQ, S), min(FOX_BK, S)
    nq = S // BQ
    heads = [slice(g * HEAD_DIM, (g + 1) * HEAD_DIM) for g in range(2)]

    def body(q_ref, k_ref, v_ref, cc_ref, cr_ref, o_ref, lse_ref):
        row = lax.broadcasted_iota(jnp.int32, (BQ, BK), 0)
        col = lax.broadcasted_iota(jnp.int32, (BQ, BK), 1)

        def qstep(i, carry):
            r = pl.multiple_of(i * BQ, BQ)
            qs = [q_ref[pl.ds(r, BQ), hl] * 0.125 for hl in heads]
            cqs = [cc_ref[g, pl.ds(r, BQ), :] for g in range(2)]

            def kstep(j, st):
                c = pl.multiple_of(j * BK, BK)
                new = []
                for g, hl in enumerate(heads):
                    m, l, acc = st[g]
                    s = _fox_logits(qs[g], k_ref[pl.ds(c, BK), hl], cqs[g], cr_ref[g, :, pl.ds(c, BK)],
                                    r, c, row, col)
                    mn = jnp.maximum(m, jnp.max(s, axis=-1, keepdims=True))
                    al = jnp.exp(m - mn)
                    e = jnp.exp(s - mn)
                    l = al * l + jnp.sum(e, axis=-1, keepdims=True)
                    acc = al * acc + jnp.dot(e.astype(BF16), v_ref[pl.ds(c, BK), hl],
                                             preferred_element_type=F32)
                    new.append((mn, l, acc))
                return tuple(new)

            init = (jnp.full((BQ, 1), NEG_INF, F32), jnp.zeros((BQ, 1), F32), jnp.zeros((BQ, HEAD_DIM), F32))
            st = lax.fori_loop(0, (r + BQ + BK - 1) // BK, kstep, (init, init))
            for g, hl in enumerate(heads):
                m, l, acc = st[g]
                o_ref[pl.ds(r, BQ), hl] = (acc / l).astype(o_ref.dtype)
                lse_ref[g, pl.ds(r, BQ), :] = m + jnp.log(l)
            return carry

        lax.fori_loop(0, nq, qstep, 0)

    blk = pl.BlockSpec((S, LANE), lambda i: (0, i))
    ccs = pl.BlockSpec((2, S, 1), lambda i: (i, 0, 0))
    crs = pl.BlockSpec((2, 1, S), lambda i: (i, 0, 0))
    return pl.pallas_call(
        body, grid=(npair,), in_specs=[blk, blk, blk, ccs, crs], out_specs=[blk, ccs],
        out_shape=[SDS((S, npair * LANE), BF16), SDS((2 * npair, S, 1), F32)],
        compiler_params=_cparams(("parallel",)), name=name)(q, k, v, cc, cr)


def _fox_bwd(q, k, v, cc, cr, o, do, lse, name):
    S = q.shape[0]
    npair = q.shape[1] // LANE
    BQ, BK = min(FOX_BQ, S), min(FOX_BK, S)
    nq = S // BQ
    heads = [slice(g * HEAD_DIM, (g + 1) * HEAD_DIM) for g in range(2)]
    TN = (((0,), (0,)), ((), ()))

    def body(q_ref, k_ref, v_ref, cc_ref, cr_ref, o_ref, do_ref, lse_ref,
             dq_ref, dk_ref, dv_ref, dcr_ref, dcc_ref, dka, dva):
        dka[...] = jnp.zeros_like(dka)
        dva[...] = jnp.zeros_like(dva)
        dcr_ref[...] = jnp.zeros_like(dcr_ref)
        row = lax.broadcasted_iota(jnp.int32, (BQ, BK), 0)
        col = lax.broadcasted_iota(jnp.int32, (BQ, BK), 1)

        def qstep(i, carry):
            r = pl.multiple_of(i * BQ, BQ)
            qs = [q_ref[pl.ds(r, BQ), hl] * 0.125 for hl in heads]
            dos = [do_ref[pl.ds(r, BQ), hl] for hl in heads]
            deltas = [jnp.sum(dos[g].astype(F32) * o_ref[pl.ds(r, BQ), hl].astype(F32), axis=-1, keepdims=True)
                      for g, hl in enumerate(heads)]
            cqs = [cc_ref[g, pl.ds(r, BQ), :] for g in range(2)]
            lses = [lse_ref[g, pl.ds(r, BQ), :] for g in range(2)]

            def kstep(j, st):
                c = pl.multiple_of(j * BK, BK)
                new = []
                for g, hl in enumerate(heads):
                    dq, rs = st[g]
                    kj = k_ref[pl.ds(c, BK), hl]
                    s = _fox_logits(qs[g], kj, cqs[g], cr_ref[g, :, pl.ds(c, BK)], r, c, row, col)
                    p = jnp.exp(s - lses[g])
                    dp = lax.dot_general(dos[g], v_ref[pl.ds(c, BK), hl], (((1,), (1,)), ((), ())),
                                         preferred_element_type=F32)
                    ds = p * (dp - deltas[g])
                    dcr_ref[g, :, pl.ds(c, BK)] -= jnp.sum(ds, axis=0, keepdims=True)
                    dsb = ds.astype(BF16)
                    dka[pl.ds(c, BK), hl] += lax.dot_general(dsb, qs[g], TN, preferred_element_type=F32)
                    dva[pl.ds(c, BK), hl] += lax.dot_general(p.astype(BF16), dos[g], TN,
                                                            preferred_element_type=F32)
                    new.append((dq + jnp.dot(dsb, kj, preferred_element_type=F32),
                                rs + jnp.sum(ds, axis=-1, keepdims=True)))
                return tuple(new)

            init = (jnp.zeros((BQ, HEAD_DIM), F32), jnp.zeros((BQ, 1), F32))
            st = lax.fori_loop(0, (r + BQ + BK - 1) // BK, kstep, (init, init))
            for g, hl in enumerate(heads):
                dq_ref[pl.ds(r, BQ), hl] = (st[g][0] * 0.125).astype(dq_ref.dtype)
                dcc_ref[g, pl.ds(r, BQ), :] = st[g][1]
            return carry

        lax.fori_loop(0, nq, qstep, 0)
        dk_ref[...] = dka[...].astype(dk_ref.dtype)
        dv_ref[...] = dva[...].astype(dv_ref.dtype)

    blk = pl.BlockSpec((S, LANE), lambda i: (0, i))
    ccs = pl.BlockSpec((2, S, 1), lambda i: (i, 0, 0))
    crs = pl.BlockSpec((2, 1, S), lambda i: (i, 0, 0))
    return pl.pallas_call(
        body, grid=(npair,), in_specs=[blk, blk, blk, ccs, crs, blk, blk, ccs],
        out_specs=[blk, blk, blk, crs, ccs],
        out_shape=[SDS((S, npair * LANE), BF16)] * 3 + [SDS((2 * npair, 1, S), F32), SDS((2 * npair, S, 1), F32)],
        scratch_shapes=[pltpu.VMEM((S, LANE), F32), pltpu.VMEM((S, LANE), F32)],
        compiler_params=_cparams(("parallel",)), name=name)(q, k, v, cc, cr, o, do, lse)


def _split3(x):
    hi = x.astype(BF16)
    r1 = x - hi.astype(F32)
    mid = r1.astype(BF16)
    lo = (r1 - mid.astype(F32)).astype(BF16)
    return hi, mid, lo


def _tri_dot(tri, x):
    hi, mid, lo = _split3(x)
    return (jnp.dot(tri, hi, preferred_element_type=F32) + jnp.dot(tri, mid, preferred_element_type=F32)
            + jnp.dot(tri, lo, preferred_element_type=F32))


def _fox_cum(gf, bfo, name):
    S = gf.shape[0]
    nb = S // LANE
    fcol = (GF_COLS - LANE) // LANE

    def body(f_ref, b_ref, cum_ref):
        row = lax.broadcasted_iota(jnp.int32, (LANE, LANE), 0)
        col = lax.broadcasted_iota(jnp.int32, (LANE, LANE), 1)
        tri = jnp.where(row >= col, 1.0, 0.0).astype(BF16)
        carry = jnp.zeros((1, LANE), F32)
        for t in range(nb):
            xl = f_ref[t * LANE:(t + 1) * LANE, :] + b_ref[...]
            lf = jnp.minimum(xl, 0.0) - jnp.log(1.0 + jnp.exp(-jnp.abs(xl)))
            cblk = _tri_dot(tri, lf) + carry
            cum_ref[t * LANE:(t + 1) * LANE, :] = cblk
            carry = cblk[LANE - 1:LANE, :]

    return pl.pallas_call(
        body, grid=(1,), in_specs=[pl.BlockSpec((S, LANE), lambda i: (0, fcol)), _vec(LANE)],
        out_specs=pl.BlockSpec((S, LANE), lambda i: (0, 0)), out_shape=SDS((S, LANE), F32),
        compiler_params=_cparams(("arbitrary",)), name=name)(gf, bfo)


def _fox_cum_bwd(gf, bfo, dcum, name):
    S = gf.shape[0]
    nb = S // LANE
    fcol = (GF_COLS - LANE) // LANE

    def body(f_ref, b_ref, dc_ref, df_ref, db_ref):
        row = lax.broadcasted_iota(jnp.int32, (LANE, LANE), 0)
        col = lax.broadcasted_iota(jnp.int32, (LANE, LANE), 1)
        tri = jnp.where(row <= col, 1.0, 0.0).astype(BF16)
        carry = jnp.zeros((1, LANE), F32)
        tot = jnp.zeros((1, LANE), F32)
        for t in range(nb - 1, -1, -1):
            rows = slice(t * LANE, (t + 1) * LANE)
            dlf = _tri_dot(tri, dc_ref[rows, :]) + carry
            carry = dlf[0:1, :]
            xl = f_ref[rows, :] + b_ref[...]
            dfl = dlf * (1.0 / (1.0 + jnp.exp(xl)))
            df_ref[rows, :] = dfl.astype(df_ref.dtype)
            tot = tot + jnp.sum(dfl, axis=0, keepdims=True)
        db_ref[...] = tot

    return pl.pallas_call(
        body, grid=(1,),
        in_specs=[pl.BlockSpec((S, LANE), lambda i: (0, fcol)), _vec(LANE), pl.BlockSpec((S, LANE), lambda i: (0, 0))],
        out_specs=[pl.BlockSpec((S, LANE), lambda i: (0, 0)), _vec(LANE)],
        out_shape=[SDS((S, LANE), BF16), SDS((1, LANE), F32)],
        compiler_params=_cparams(("arbitrary",)), name=name)(gf, bfo, dcum)


def _rel_onehot(qi, band):
    r = lax.broadcasted_iota(jnp.int32, (N_REL_PAD, band), 0)
    j = lax.broadcasted_iota(jnp.int32, (N_REL_PAD, band), 1)
    idx = jnp.clip(C_PREV * CHUNK + qi - j, -REL_CLIP, REL_CLIP) + REL_CLIP
    return jnp.where(r == idx, 1.0, 0.0).astype(BF16)


def _rel_expand(rel, name):
    band = (C_PREV + 1) * CHUNK

    def body(rel_ref, o_ref):
        oh = _rel_onehot(pl.program_id(0), band)
        o_ref[0] = _tri_dot_rhs(rel_ref[...], oh)

    return pl.pallas_call(
        body, grid=(CHUNK,), in_specs=[pl.BlockSpec((N_HEADS, N_REL_PAD), lambda i: (0, 0))],
        out_specs=pl.BlockSpec((1, N_HEADS, band), lambda i: (i, 0, 0)),
        out_shape=SDS((CHUNK, N_HEADS, band), F32),
        compiler_params=_cparams(("parallel",)), name=name)(rel)


def _tri_dot_rhs(x, oh):
    hi, mid, lo = _split3(x)
    return (jnp.dot(hi, oh, preferred_element_type=F32) + jnp.dot(mid, oh, preferred_element_type=F32)
            + jnp.dot(lo, oh, preferred_element_type=F32))


def _rel_reduce(dbias, name):
    band = (C_PREV + 1) * CHUNK
    NT = (((1,), (1,)), ((), ()))

    def body(d_ref, o_ref):
        @pl.when(pl.program_id(0) == 0)
        def _():
            o_ref[...] = jnp.zeros_like(o_ref)

        oh = _rel_onehot(pl.program_id(0), band)
        hi, mid, lo = _split3(d_ref[0])
        o_ref[...] += (lax.dot_general(hi, oh, NT, preferred_element_type=F32)
                       + lax.dot_general(mid, oh, NT, preferred_element_type=F32)
                       + lax.dot_general(lo, oh, NT, preferred_element_type=F32))

    return pl.pallas_call(
        body, grid=(CHUNK,), in_specs=[pl.BlockSpec((1, N_HEADS, band), lambda i: (i, 0, 0))],
        out_specs=pl.BlockSpec((N_HEADS, N_REL_PAD), lambda i: (0, 0)),
        out_shape=SDS((N_HEADS, N_REL_PAD), F32),
        compiler_params=_cparams(("arbitrary",)), name=name)(dbias)


def _alibi_table():
    qi = np.arange(CHUNK)[:, None]
    j = np.arange((A_PREV + 1) * CHUNK)[None, :]
    dist = np.abs(A_PREV * CHUNK + qi - j).astype(np.float32)
    slopes = np.exp2(-8.0 * np.arange(1, N_HEADS + 1, dtype=np.float32) / N_HEADS).astype(np.float32)
    return jnp.asarray(-slopes[:, None, None] * dist[None])


def _ada_fwd(c_all, w, b, name):
    n = w.shape[2]

    def body(c_ref, w_ref, b_ref, o_ref):
        cv = c_ref[...]
        cond = (cv * _sigmoid(cv)).astype(BF16)
        o_ref[0] = jnp.dot(cond, w_ref[0].astype(BF16), preferred_element_type=F32) + b_ref[0]

    return pl.pallas_call(
        body, grid=(DEPTH,),
        in_specs=[pl.BlockSpec((16, D_MODEL), lambda l: (0, 0)), pl.BlockSpec((1, D_MODEL, n), lambda l: (l, 0, 0)),
                  pl.BlockSpec((1, 1, n), lambda l: (l, 0, 0))],
        out_specs=pl.BlockSpec((1, 16, n), lambda l: (l, 0, 0)), out_shape=SDS((DEPTH, 16, n), F32),
        compiler_params=_cparams(("parallel",)), name=name)(c_all, w, b)


def _ada_bwd(c_t, dmod, name):
    n = dmod.shape[2]
    bn = _blk(n, 512)
    tr = 256

    def body(c_ref, d_ref, o_ref):
        cv = c_ref[...]
        cond = (cv * _sigmoid(cv)).astype(BF16).astype(F32)
        dm = d_ref[0].astype(BF16).astype(F32)
        acc = cond[:, 0:1] * dm[0:1, :]
        for b_ in range(1, 8):
            acc = acc + cond[:, b_:b_ + 1] * dm[b_:b_ + 1, :]
        o_ref[0] = acc

    return pl.pallas_call(
        body, grid=(DEPTH, D_MODEL // tr, n // bn),
        in_specs=[pl.BlockSpec((tr, 8), lambda l, i, j: (i, 0)), pl.BlockSpec((1, 8, bn), lambda l, i, j: (l, 0, j))],
        out_specs=pl.BlockSpec((1, tr, bn), lambda l, i, j: (l, i, j)), out_shape=SDS((DEPTH, D_MODEL, n), F32),
        compiler_params=_cparams(("parallel", "parallel", "parallel")), name=name)(c_t, dmod)


def _adamw(w, m, v, parts, name, tr=None):
    R, C = w.shape
    P = parts.shape[0]
    if tr is None:
        tr = _blk_rows(R, max(16, (1 << 18) // C))
    c1 = 1.0 - ADAM_B1 ** ADAM_STEP
    c2 = 1.0 - ADAM_B2 ** ADAM_STEP

    def body(w_ref, m_ref, v_ref, p_ref, g_ref, d_ref, nm_ref, nv_ref):
        g = p_ref[0].astype(F32)
        for k in range(1, P):
            g = g + p_ref[k].astype(F32)
        mn = ADAM_B1 * m_ref[...] + (1.0 - ADAM_B1) * g
        vn = ADAM_B2 * v_ref[...] + (1.0 - ADAM_B2) * (g * g)
        m_hat = mn / c1
        v_hat = vn / c2
        g_ref[...] = g
        nm_ref[...] = mn
        nv_ref[...] = vn
        d_ref[...] = -ADAM_LR * (m_hat / (jnp.sqrt(v_hat) + ADAM_EPS) + ADAM_WD * w_ref[...])

    rs = pl.BlockSpec((tr, C), lambda i: (i, 0))
    return pl.pallas_call(
        body, grid=(R // tr,), in_specs=[rs, rs, rs, pl.BlockSpec((P, tr, C), lambda i: (0, i, 0))],
        out_specs=[rs, rs, rs, rs], out_shape=[SDS((R, C), F32)] * 4,
        compiler_params=_cparams(("parallel",)), name=name)(w, m, v, parts)


def _blk_rows(R, cap):
    if R <= cap:
        return R
    best = None
    for t in range(16, cap + 1, 16):
        if R % t == 0:
            best = t
    assert best is not None, (R, cap)
    return best


def _add_cast(a, b, name):
    Q, R, C = a.shape
    tr = _blk_rows(R, max(16, (1 << 19) // C))

    def body(a_ref, b_ref, o_ref):
        o_ref[...] = (a_ref[...] + b_ref[...]).astype(o_ref.dtype)

    bs = pl.BlockSpec((1, tr, C), lambda q, i: (q, i, 0))
    return pl.pallas_call(
        body, grid=(Q, R // tr), in_specs=[bs, bs], out_specs=bs, out_shape=SDS((Q, R, C), BF16),
        compiler_params=_cparams(("parallel", "parallel")), name=name)(a, b)


def _coords():
    return lax.axis_index("x"), lax.axis_index("y"), lax.axis_index("c")


def _flip(v, bit):
    return 1 - v if bit else v


def _all_gather8(v, name):
    R = v.shape[0]

    def body(v_ref, o_ref, send_sems, recv_sems):
        x, y, c = _coords()
        me = 4 * x + 2 * y + c
        o_ref[me] = v_ref[...]
        copies = []
        for k in range(1, 8):
            peer = (_flip(x, k & 4), _flip(y, k & 2), _flip(c, k & 1))
            cp = pltpu.make_async_remote_copy(
                src_ref=v_ref, dst_ref=o_ref.at[me], send_sem=send_sems.at[k - 1], recv_sem=recv_sems.at[k - 1],
                device_id=peer, device_id_type=MESH)
            cp.start()
            copies.append(cp)
        for cp in copies:
            cp.wait_recv()
        for cp in copies:
            cp.wait_send()

    return pl.pallas_call(
        body, in_specs=[VMEM_SPEC], out_specs=VMEM_SPEC, out_shape=SDS((8, R, LANE), v.dtype),
        scratch_shapes=[pltpu.SemaphoreType.DMA((7,)), pltpu.SemaphoreType.DMA((7,))],
        compiler_params=pltpu.CompilerParams(vmem_limit_bytes=VMEM_LIMIT), name=name)(v)


def _sibling_swap(arrs, name):
    n = len(arrs)

    def body(*refs):
        in_refs, out_refs = refs[:n], refs[n:2 * n]
        send_sems, recv_sems = refs[2 * n:]
        x, y, c = _coords()
        copies = []
        for a in range(n):
            cp = pltpu.make_async_remote_copy(
                src_ref=in_refs[a].at[1 - c], dst_ref=out_refs[a], send_sem=send_sems.at[a],
                recv_sem=recv_sems.at[a], device_id=(x, y, 1 - c), device_id_type=MESH)
            cp.start()
            copies.append(cp)
        for cp in copies:
            cp.wait_recv()
        for cp in copies:
            cp.wait_send()

    return pl.pallas_call(
        body, in_specs=[ANY] * n, out_specs=[ANY] * n,
        out_shape=[SDS(a.shape[1:], a.dtype) for a in arrs],
        scratch_shapes=[pltpu.SemaphoreType.DMA((n,)), pltpu.SemaphoreType.DMA((n,))],
        name=name)(*arrs)


def _chip_exchange(arrs, *, reduce, name):
    n = len(arrs)

    def body(*refs):
        in_refs, out_refs = refs[:n], refs[n:2 * n]
        ici_send, ici_recv, d2d_send, d2d_recv, loc_sem = refs[2 * n:]
        x, y, c = _coords()
        p = 2 * x + y
        local, first, fwd = [], [], []
        for a in range(n):
            R = out_refs[a].shape[1] // 2
            half = pl.ds(pl.multiple_of(c * R, 16), R)
            if reduce:
                lc = pltpu.make_async_copy(in_refs[a].at[p], out_refs[a].at[p, half], loc_sem.at[a])
            else:
                lc = pltpu.make_async_copy(in_refs[a], out_refs[a].at[p], loc_sem.at[a])
            lc.start()
            local.append(lc)
            for k in range(1, 4):
                qx, qy = _flip(x, k & 2), _flip(y, k & 1)
                src = in_refs[a].at[2 * qx + qy] if reduce else in_refs[a].at[half]
                cp = pltpu.make_async_remote_copy(
                    src_ref=src, dst_ref=out_refs[a].at[p, half], send_sem=ici_send.at[a, k - 1],
                    recv_sem=ici_recv.at[a, k - 1], device_id=(qx, qy, c), device_id_type=MESH)
                cp.start()
                first.append(cp)
        for a in range(n):
            R = out_refs[a].shape[1] // 2
            half = pl.ds(pl.multiple_of(c * R, 16), R)
            for k in range(0 if reduce else 1, 4):
                qx, qy = _flip(x, k & 2), _flip(y, k & 1)
                slot = out_refs[a].at[2 * qx + qy, half]
                if k == 0:
                    local[a].wait()
                else:
                    first[a * 3 + k - 1].wait_recv()
                cp = pltpu.make_async_remote_copy(
                    src_ref=slot, dst_ref=slot, send_sem=d2d_send.at[a, k], recv_sem=d2d_recv.at[a, k],
                    device_id=(x, y, 1 - c), device_id_type=MESH)
                cp.start()
                fwd.append(cp)
        for cp in fwd:
            cp.wait_recv()
        for cp in first + fwd:
            cp.wait_send()
        if not reduce:
            for lc in local:
                lc.wait()

    if reduce:
        out_shape = [SDS((4, 2 * a.shape[1], a.shape[2]), a.dtype) for a in arrs]
    else:
        out_shape = [SDS((4,) + a.shape, a.dtype) for a in arrs]
    return pl.pallas_call(
        body, in_specs=[ANY] * n, out_specs=[ANY] * n, out_shape=out_shape,
        scratch_shapes=[pltpu.SemaphoreType.DMA((n, 3)), pltpu.SemaphoreType.DMA((n, 3)),
                        pltpu.SemaphoreType.DMA((n, 4)), pltpu.SemaphoreType.DMA((n, 4)),
                        pltpu.SemaphoreType.DMA((n,))],
        name=name)(*arrs)


_IN_SIZES = (512, 128, 128, 512, 512, 512, 8, 512, 512, 512, 3072)
_IN_OFF = tuple(int(v) for v in np.cumsum((0,) + _IN_SIZES))


def _pack_w_in(w):
    seg = [w[:, _IN_OFF[i]:_IN_OFF[i + 1]] for i in range(11)]
    qa, ka, va, qb, kb, vb, fb, qc, kc, vc, gates = seg
    z = jnp.zeros((w.shape[0], HEAD_DIM), w.dtype)
    wqkv = jnp.concatenate([qa, ka[:, :64], z, ka[:, 64:], z, va[:, :64], z, va[:, 64:], z,
                            qb, kb, vb, qc, kc, vc], axis=1)
    wgf = jnp.concatenate([gates, fb, jnp.zeros((w.shape[0], LANE - 8), w.dtype)], axis=1)
    return wqkv, wgf


def _unpack_w_in(dqkv, dgf):
    qa = dqkv[:, 0:512]
    ka = jnp.concatenate([dqkv[:, 512:576], dqkv[:, 640:704]], axis=1)
    va = jnp.concatenate([dqkv[:, 768:832], dqkv[:, 896:960]], axis=1)
    rest = dqkv[:, 1024:QKV_COLS]
    return jnp.concatenate([qa, ka, va, rest[:, 0:1536], dgf[:, 3072:3080], rest[:, 1536:3072], dgf[:, 0:3072]],
                           axis=1)


def _pad_rows(a, rows):
    return jnp.pad(a, ((0, rows - a.shape[0]), (0, 0)))


def _small_pack(parts):
    flat = jnp.concatenate([p.reshape(-1) for p in parts])
    n = flat.shape[0]
    rows = -(-n // LANE)
    rows = -(-rows // 8) * 8
    return jnp.pad(flat, (0, rows * LANE - n)).reshape(rows, LANE)


def _small_unpack(block, shapes):
    flat = block.reshape(-1)
    out, off = [], 0
    for s in shapes:
        n = int(np.prod(s))
        out.append(flat[off:off + n].reshape(s))
        off += n
    return out


def _kv_same(g):
    return 0


def _kv_own(g):
    return g


def _layer_fwd(x, mod, p, l):
    sh_m, sc_m, g_m, sh_f, sc_f, g_f = mod
    nm = "l%d_" % l
    h1 = _norm_mod_fwd(x, p["norm_mix_g"], sc_m, sh_m, nm + "norm_mix_fwd")
    qkv = _mm(h1, p["wqkv"], mode="nn", out_dtype=BF16, name=nm + "proj_qkv")
    gf = _mm(h1, p["wgf"], mode="nn", out_dtype=F32, name=nm + "proj_gf", cap_n=640)
    qa, ka, va = qkv[:, 0:512], qkv[:, 512:768], qkv[:, 768:1024]
    o_a = _band_attn_fwd(qa, ka, va, p["alibi"], p["sink_tab"], G=4, P=A_PREV, kvoff=_kv_same, name=nm + "attn_a_fwd")
    qb, kb, vb = qkv[:, 1024:1536], qkv[:, 1536:2048], qkv[:, 2048:2560]
    cum = _fox_cum(gf, p["b_forget_pad"], nm + "fox_cum")
    cum_t = cum[:, :N_HEADS].T
    cc, cr = cum_t[:, :, None], cum_t[:, None, :]
    o_b, lse_b = _fox_fwd(qb, kb, vb, cc, cr, nm + "attn_b_fwd")
    qc, kc, vc = qkv[:, 2560:3072], qkv[:, 3072:3584], qkv[:, 3584:4096]
    o_c = _band_attn_fwd(qc, kc, vc, p["rel_tab"], p["no_sink"], G=2, P=C_PREV, kvoff=_kv_own, name=nm + "attn_c_fwd")
    o = jnp.concatenate([o_a, o_b, o_c], axis=1)
    y = _mm(o, p["wb"], mode="nn", out_dtype=F32, groups=3, name=nm + "branch")
    merged = _merge_fwd(y, gf, nm + "merge_fwd")
    mix = _mm(merged, p["wout"], mode="nn", out_dtype=F32, name=nm + "out_proj")
    x1 = _resid_fwd(x, mix, g_m, nm + "resid_mix")
    h2 = _norm_mod_fwd(x1, p["norm_ffn_g"], sc_f, sh_f, nm + "norm_ffn_fwd")
    u = _mm(h2, p["wfi"], mode="nn", out_dtype=F32, name=nm + "ffn_in", cap_n=1408)
    a = _swiglu_fwd(u, nm + "swiglu_fwd")
    f = _mm(a, p["wfo"], mode="nn", out_dtype=F32, name=nm + "ffn_out")
    x2 = _resid_fwd(x1, f, g_f, nm + "resid_ffn")
    saved = dict(x=x, h1=h1, qkv=qkv, gf=gf, cc=cc, cr=cr, o_b=o_b, lse_b=lse_b, o=o, y=y, merged=merged,
                 mix=mix, x1=x1, h2=h2, u=u, a=a, f=f)
    return x2, saved


def _layer_bwd(dx2, mod, p, s, l):
    sh_m, sc_m, g_m, sh_f, sc_f, g_f = mod
    nm = "l%d_" % l
    dg_f, df = _resid_bwd(dx2, s["f"], g_f, nm + "resid_ffn_bwd")
    da = _mm(df, p["wfo"], mode="nt", out_dtype=F32, name=nm + "ffn_out_dx", cap_n=1408)
    d_wfo = _mm(s["a"], df, mode="tn", out_dtype=F32, name=nm + "ffn_out_dw", cap_m=1408, cap_k=512)
    du = _swiglu_bwd(da, s["u"], nm + "swiglu_bwd")
    dh2 = _mm(du, p["wfi"], mode="nt", out_dtype=F32, name=nm + "ffn_in_dx")
    d_wfi = _mm(s["h2"], du, mode="tn", out_dtype=F32, name=nm + "ffn_in_dw", cap_n=1408, cap_k=512)
    dx1, dsc_f, dsh_f, dgn_f = _norm_mod_bwd(s["x1"], [dh2], dx2, p["norm_ffn_g"], sc_f, nm + "norm_ffn_bwd")
    dg_m, dmix = _resid_bwd(dx1, s["mix"], g_m, nm + "resid_mix_bwd")
    dmerged = _mm(dmix, p["wout"], mode="nt", out_dtype=F32, name=nm + "out_proj_dx")
    d_wout = _mm(s["merged"], dmix, mode="tn", out_dtype=F32, name=nm + "out_proj_dw", cap_k=512)
    dy, dgates = _merge_bwd(dmerged, s["y"], s["gf"], nm + "merge_bwd")
    do = _mm(dy, p["wb"], mode="nt", out_dtype=BF16, groups=3, name=nm + "branch_dx")
    d_wb = _mm(s["o"], dy, mode="tn", out_dtype=F32, groups=3, name=nm + "branch_dw", cap_k=512)
    qkv = s["qkv"]
    qa, ka, va = qkv[:, 0:512], qkv[:, 512:768], qkv[:, 768:1024]
    dqa, dka, dva, _, dsink = _band_attn_bwd(qa, ka, va, p["alibi"], p["sink_tab"], do[:, 0:512], G=4, P=A_PREV,
                                             kvoff=_kv_same, name=nm + "attn_a_bwd")
    qb, kb, vb = qkv[:, 1024:1536], qkv[:, 1536:2048], qkv[:, 2048:2560]
    dqb, dkb, dvb, dcr, dcc = _fox_bwd(qb, kb, vb, s["cc"], s["cr"], s["o_b"], do[:, 512:1024], s["lse_b"],
                                       nm + "attn_b_bwd")
    dcum = jnp.pad((dcr[:, 0, :] + dcc[:, :, 0]).T, ((0, 0), (0, LANE - N_HEADS)))
    dfb, db_forget = _fox_cum_bwd(s["gf"], p["b_forget_pad"], dcum, nm + "fox_cum_bwd")
    qc, kc, vc = qkv[:, 2560:3072], qkv[:, 3072:3584], qkv[:, 3584:4096]
    dqc, dkc, dvc, dbias_c, _ = _band_attn_bwd(qc, kc, vc, p["rel_tab"], p["no_sink"], do[:, 1024:1536], G=2,
                                               P=C_PREV, kvoff=_kv_own, name=nm + "attn_c_bwd")
    d_rel = _rel_reduce(jnp.transpose(dbias_c, (1, 0, 2)), nm + "rel_reduce")[:, :N_REL]
    dqkv = jnp.concatenate([dqa, dka, dva, dqb, dkb, dvb, dqc, dkc, dvc], axis=1)
    dgf = jnp.concatenate([dgates, dfb], axis=1)
    dh1a = _mm(dqkv, p["wqkv"], mode="nt", out_dtype=F32, name=nm + "proj_qkv_dx", cap_k=1024)
    dh1b = _mm(dgf, p["wgf"], mode="nt", out_dtype=F32, name=nm + "proj_gf_dx", cap_k=640)
    d_wqkv = _mm(s["h1"], dqkv, mode="tn", out_dtype=F32, name=nm + "proj_qkv_dw", cap_k=512)
    d_wgf = _mm(s["h1"], dgf, mode="tn", out_dtype=F32, name=nm + "proj_gf_dw", cap_n=640, cap_k=512)
    dx, dsc_m, dsh_m, dgn_m = _norm_mod_bwd(s["x"], [dh1a, dh1b], dx1, p["norm_mix_g"], sc_m, nm + "norm_mix_bwd")
    d_mod = jnp.concatenate([dsh_m, dsc_m, dg_m, dsh_f, dsc_f, dg_f], axis=1)[0]
    grads = dict(w_in=_unpack_w_in(d_wqkv, d_wgf), w_branch=d_wb, w_out=d_wout, w_ffn_in=d_wfi, w_ffn_out=d_wfo,
                 norm_mix_g=dgn_m[0], norm_ffn_g=dgn_f[0], b_forget=db_forget[0, :N_HEADS],
                 sinks=dsink[:, 0, 0], rel_bias=d_rel, d_mod=d_mod)
    return dx, grads


def kernel(x, c, norm_mix_g, norm_ffn_g, w_ada, b_ada, w_in, b_forget, sinks, rel_bias, w_branch, w_out, w_ffn_in, w_ffn_out, final_norm_g, loss_target, m_norm_mix_g, m_norm_ffn_g, m_w_ada, m_b_ada, m_w_in, m_b_forget, m_sinks, m_rel_bias, m_w_branch, m_w_out, m_w_ffn_in, m_w_ffn_out, m_final_norm_g, v_norm_mix_g, v_norm_ffn_g, v_w_ada, v_b_ada, v_w_in, v_b_forget, v_sinks, v_rel_bias, v_w_branch, v_w_out, v_w_ffn_in, v_w_ffn_out, v_final_norm_g):
    xi, yi, ci = _coords()
    chip = 2 * xi + yi
    dev = 2 * chip + ci
    xs = x[0]
    S = xs.shape[0]
    n_ada = w_ada.shape[2]

    big_names = ("w_in", "w_branch", "w_out", "w_ffn_in", "w_ffn_out")
    big_w = dict(w_in=w_in, w_branch=w_branch, w_out=w_out, w_ffn_in=w_ffn_in, w_ffn_out=w_ffn_out)
    big_m = dict(w_in=m_w_in, w_branch=m_w_branch, w_out=m_w_out, w_ffn_in=m_w_ffn_in, w_ffn_out=m_w_ffn_out)
    big_v = dict(w_in=v_w_in, w_branch=v_w_branch, w_out=v_w_out, w_ffn_in=v_w_ffn_in, w_ffn_out=v_w_ffn_out)
    flat2 = lambda a: a.reshape(-1, a.shape[-1])
    shards = [flat2(big_w[n]).astype(BF16) for n in big_names]
    gw_in, gw_branch, gw_out, gw_ffn_in, gw_ffn_out = _chip_exchange(shards, reduce=False, name="weights_all_gather")
    cin = w_in.shape[2]
    cbr = w_branch.shape[3]
    rout = w_out.shape[1]
    cfi = w_ffn_in.shape[2]
    rfo = w_ffn_out.shape[1]
    w_in_full = gw_in.reshape(4, DEPTH, D_MODEL, cin).transpose(1, 2, 0, 3).reshape(DEPTH, D_MODEL, 4 * cin)
    w_branch_full = gw_branch.reshape(4, DEPTH, 3, BRANCH_W, cbr).transpose(1, 2, 3, 0, 4).reshape(
        DEPTH, 3 * BRANCH_W, 4 * cbr)
    w_out_full = gw_out.reshape(4, DEPTH, rout, D_MODEL).transpose(1, 0, 2, 3).reshape(DEPTH, 4 * rout, D_MODEL)
    w_ffn_in_full = gw_ffn_in.reshape(4, DEPTH, D_MODEL, cfi).transpose(1, 2, 0, 3).reshape(DEPTH, D_MODEL, 4 * cfi)
    w_ffn_out_full = gw_ffn_out.reshape(4, DEPTH, rfo, D_MODEL).transpose(1, 0, 2, 3).reshape(DEPTH, 4 * rfo, D_MODEL)

    c_all = _all_gather8(c.reshape(8, LANE), "gather_c").reshape(8, D_MODEL)
    b_sh = lax.dynamic_slice_in_dim(b_ada, chip * n_ada, n_ada, axis=1)[:, None, :]
    mod_sh = _ada_fwd(_pad_rows(c_all, 16), w_ada, b_sh, "ada_fwd")[:, :8, :]
    mod_all = _all_gather8(mod_sh.reshape(-1, LANE), "gather_mod").reshape(8, DEPTH, 8, n_ada)
    mod_mine = lax.dynamic_index_in_dim(mod_all[0::2], dev, axis=2, keepdims=False)
    mod = mod_mine.transpose(1, 0, 2).reshape(DEPTH, 6, D_MODEL)

    alibi = _alibi_table()
    no_sink = jnp.full((N_HEADS, 8, LANE), NEG_INF, F32)
    params = []
    for l in range(DEPTH):
        wqkv, wgf = _pack_w_in(w_in_full[l])
        rel_tab = _rel_expand(jnp.pad(rel_bias[l], ((0, 0), (0, N_REL_PAD - N_REL))), "l%d_rel_expand" % l)
        params.append(dict(
            wqkv=wqkv, wgf=wgf, wb=w_branch_full[l], wout=w_out_full[l], wfi=w_ffn_in_full[l], wfo=w_ffn_out_full[l],
            norm_mix_g=norm_mix_g[l][None], norm_ffn_g=norm_ffn_g[l][None],
            b_forget_pad=jnp.pad(b_forget[l], (0, LANE - N_HEADS))[None],
            sink_tab=jnp.broadcast_to(sinks[l][:, None, None], (N_HEADS, 8, LANE)),
            no_sink=no_sink, alibi=alibi, rel_tab=jnp.transpose(rel_tab, (1, 0, 2))))
    mods = [[mod[l, k][None] for k in range(6)] for l in range(DEPTH)]
    h = xs
    saved = []
    for l in range(DEPTH):
        h, s = _layer_fwd(h, mods[l], params[l], l)
        saved.append(s)
    loss_dev, dh, d_final = _final_loss(h, final_norm_g[None], loss_target[0], "final_loss")
    grads = [None] * DEPTH
    for l in reversed(range(DEPTH)):
        dh, grads[l] = _layer_bwd(dh, mods[l], params[l], saved[l], l)
    grad_x = dh[None]
    loss = lax.psum(loss_dev[0, 0], ("x", "y", "c"))

    def by_quarter(name, g):
        if name == "w_in":
            return g.reshape(DEPTH, D_MODEL, 4, cin).transpose(0, 2, 1, 3)
        if name == "w_branch":
            return g.reshape(DEPTH, 3 * BRANCH_W, 4, cbr).transpose(0, 2, 1, 3)
        if name == "w_out":
            return g.reshape(DEPTH, 4, rout, D_MODEL)
        if name == "w_ffn_in":
            return g.reshape(DEPTH, D_MODEL, 4, cfi).transpose(0, 2, 1, 3)
        return g.reshape(DEPTH, 4, rfo, D_MODEL)

    full = [by_quarter(n, jnp.stack([grads[l][n] for l in range(DEPTH)])) for n in big_names]
    theirs = _sibling_swap(full, "grads_sibling_swap")
    mine = [lax.dynamic_index_in_dim(g, ci, axis=0, keepdims=False) for g in full]
    chip_sum = [_add_cast(a, b, "grads_chip_sum_%s" % n) for n, a, b in zip(big_names, mine, theirs)]
    parts = _chip_exchange(chip_sum, reduce=True, name="grads_reduce_scatter")
    big_out = {}
    for n, pt in zip(big_names, parts):
        shp = big_w[n].shape
        res = _adamw(flat2(big_w[n]), flat2(big_m[n]), flat2(big_v[n]), pt, "adamw_" + n)
        big_out[n] = [r.reshape(shp) for r in res]

    small_names = ("norm_mix_g", "norm_ffn_g", "b_ada", "b_forget", "sinks", "rel_bias", "final_norm_g")
    small_w = dict(norm_mix_g=norm_mix_g, norm_ffn_g=norm_ffn_g, b_ada=b_ada, b_forget=b_forget, sinks=sinks,
                   rel_bias=rel_bias, final_norm_g=final_norm_g)
    small_m = dict(norm_mix_g=m_norm_mix_g, norm_ffn_g=m_norm_ffn_g, b_ada=m_b_ada, b_forget=m_b_forget,
                   sinks=m_sinks, rel_bias=m_rel_bias, final_norm_g=m_final_norm_g)
    small_v = dict(norm_mix_g=v_norm_mix_g, norm_ffn_g=v_norm_ffn_g, b_ada=v_b_ada, b_forget=v_b_forget,
                   sinks=v_sinks, rel_bias=v_rel_bias, final_norm_g=v_final_norm_g)
    small_g = dict(
        norm_mix_g=jnp.stack([grads[l]["norm_mix_g"] for l in range(DEPTH)]),
        norm_ffn_g=jnp.stack([grads[l]["norm_ffn_g"] for l in range(DEPTH)]),
        b_ada=jnp.stack([grads[l]["d_mod"] for l in range(DEPTH)]),
        b_forget=jnp.stack([grads[l]["b_forget"] for l in range(DEPTH)]),
        sinks=jnp.stack([grads[l]["sinks"] for l in range(DEPTH)]),
        rel_bias=jnp.stack([grads[l]["rel_bias"] for l in range(DEPTH)]),
        final_norm_g=d_final[0])
    shapes = [small_w[n].shape for n in small_names]
    g_all = _all_gather8(_small_pack([small_g[n] for n in small_names]), "gather_small_grads")
    res = _adamw(_small_pack([small_w[n] for n in small_names]), _small_pack([small_m[n] for n in small_names]),
                 _small_pack([small_v[n] for n in small_names]), g_all, "adamw_small")
    small_out = {n: [] for n in small_names}
    for r in res:
        for n, a in zip(small_names, _small_unpack(r, shapes)):
            small_out[n].append(a)
    off_b = sum(int(np.prod(s)) for s in shapes[:2])
    n_mod = DEPTH * 6 * D_MODEL
    dmod_all = g_all.reshape(8, -1)[:, off_b:off_b + n_mod].reshape(8, DEPTH, 6 * D_MODEL)
    dmod_sh = lax.dynamic_slice_in_dim(dmod_all, chip * n_ada, n_ada, axis=2).transpose(1, 0, 2)
    g_ada = _ada_bwd(c_all.T, dmod_sh, "ada_bwd")
    res = _adamw(flat2(w_ada), flat2(m_w_ada), flat2(v_w_ada), flat2(g_ada)[None], "adamw_w_ada")
    ada_out = [r.reshape(w_ada.shape) for r in res]

    order = ("norm_mix_g", "norm_ffn_g", "w_ada", "b_ada", "w_in", "b_forget", "sinks", "rel_bias", "w_branch",
             "w_out", "w_ffn_in", "w_ffn_out", "final_norm_g")

    def pick(n, k):
        if n == "w_ada":
            return ada_out[k]
        if n in big_out:
            return big_out[n][k]
        return small_out[n][k]

    outs = [loss, grad_x]
    for k in range(4):
        outs += [pick(n, k) for n in order]
    return tuple(outs)
```

```python
import functools

import numpy as np
import jax
import jax.numpy as jnp
from jax import lax
from jax.experimental import pallas as pl
from jax.experimental.pallas import tpu as pltpu

F32 = jnp.float32
BF16 = jnp.bfloat16
SDS = jax.ShapeDtypeStruct

D_MODEL = 1024
DEPTH = 2
CHUNK = 64
HEAD_DIM = 64
EPS = 1e-6
NEG_INF = -1e30
N_HEADS = 8
A_KV_HEADS = 2
A_PREV = 2
C_PREV = 8
REL_CLIP = 128
N_REL = 2 * REL_CLIP + 1
N_REL_PAD = 384
BRANCH_W = 512
FFN_H = 2816
FOX_BQ = 256
FOX_BK = 512
BAND_UNROLL_FWD = 4
BAND_UNROLL_BWD = 2
QKV_COLS = 3840
GF_COLS = 3200
N_IN_COLS = 6920
LANE = 128
VMEM_LIMIT = 48 * 1024 * 1024

ADAM_LR = 0.001
ADAM_B1 = 0.9
ADAM_B2 = 0.999
ADAM_EPS = 1e-08
ADAM_WD = 0.01
ADAM_STEP = 10

MESH = pl.DeviceIdType.MESH
ANY = pl.BlockSpec(memory_space=pl.ANY)
VMEM_SPEC = pl.BlockSpec(memory_space=pltpu.VMEM)


def _cparams(sem=None):
    return pltpu.CompilerParams(dimension_semantics=sem, vmem_limit_bytes=VMEM_LIMIT)


def _blk(n, cap):
    if n <= cap:
        return n
    best = None
    for m in range(LANE, cap + 1, LANE):
        if n % m == 0:
            best = m
    assert best is not None, (n, cap)
    return best


def _sigmoid(x):
    return 1.0 / (1.0 + jnp.exp(-x))


def _mm(a, b, *, mode, out_dtype, name, groups=1, cap_m=512, cap_n=1024, cap_k=1408):
    G = groups
    if mode == "nn":
        M, K, N = a.shape[0], a.shape[1] // G, b.shape[1]
        assert b.shape[0] == G * K
    elif mode == "nt":
        M, K, N = a.shape[0], a.shape[1] // G, b.shape[0] // G
        assert b.shape[1] == K
    else:
        K, M, N = a.shape[0], a.shape[1] // G, b.shape[1] // G
        assert b.shape[0] == K
    bm, bn, bk = _blk(M, cap_m), _blk(N, cap_n), _blk(K, cap_k)
    nm, nn, nk = M // bm, N // bn, K // bk
    if mode == "nn":
        a_spec = pl.BlockSpec((bm, bk), lambda g, i, j, k: (i, g * nk + k))
        b_spec = pl.BlockSpec((bk, bn), lambda g, i, j, k: (g * nk + k, j))
        o_spec = pl.BlockSpec((bm, bn), lambda g, i, j, k: (i, g * nn + j))
        dims = (((1,), (0,)), ((), ()))
        out_shape = (M, G * N)
    elif mode == "nt":
        a_spec = pl.BlockSpec((bm, bk), lambda g, i, j, k: (i, g * nk + k))
        b_spec = pl.BlockSpec((bn, bk), lambda g, i, j, k: (g * nn + j, k))
        o_spec = pl.BlockSpec((bm, bn), lambda g, i, j, k: (i, g * nn + j))
        dims = (((1,), (1,)), ((), ()))
        out_shape = (M, G * N)
    else:
        a_spec = pl.BlockSpec((bk, bm), lambda g, i, j, k: (k, g * nm + i))
        b_spec = pl.BlockSpec((bk, bn), lambda g, i, j, k: (k, g * nn + j))
        o_spec = pl.BlockSpec((bm, bn), lambda g, i, j, k: (g * nm + i, j))
        dims = (((0,), (0,)), ((), ()))
        out_shape = (G * M, N)

    def body(a_ref, b_ref, o_ref, acc_ref):
        k = pl.program_id(3)

        @pl.when(k == 0)
        def _():
            acc_ref[...] = jnp.zeros_like(acc_ref)

        acc_ref[...] += lax.dot_general(a_ref[...].astype(BF16), b_ref[...].astype(BF16), dims,
                                        preferred_element_type=F32)

        @pl.when(k == nk - 1)
        def _():
            o_ref[...] = acc_ref[...].astype(o_ref.dtype)

    return pl.pallas_call(
        body, grid=(G, nm, nn, nk), in_specs=[a_spec, b_spec], out_specs=o_spec,
        out_shape=SDS(out_shape, out_dtype), scratch_shapes=[pltpu.VMEM((bm, bn), F32)],
        compiler_params=_cparams(("parallel", "parallel", "parallel", "arbitrary")), name=name,
    )(a, b)


def _rows(tm, n, col=0):
    return pl.BlockSpec((tm, n), lambda i: (i, col))


def _vec(n):
    return pl.BlockSpec((1, n), lambda i: (0, 0))


def _tm(S):
    return min(S, 256)


def _norm_mod_fwd(x, g, sc, sh, name):
    S, Dm = x.shape
    tm = _tm(S)

    def body(x_ref, g_ref, sc_ref, sh_ref, h_ref):
        xv = x_ref[...]
        r = lax.rsqrt(jnp.mean(xv * xv, axis=-1, keepdims=True) + EPS)
        h_ref[...] = ((xv * r) * g_ref[...] * (1.0 + sc_ref[...]) + sh_ref[...]).astype(h_ref.dtype)

    return pl.pallas_call(
        body, grid=(S // tm,), in_specs=[_rows(tm, Dm), _vec(Dm), _vec(Dm), _vec(Dm)],
        out_specs=_rows(tm, Dm), out_shape=SDS((S, Dm), BF16),
        compiler_params=_cparams(("parallel",)), name=name)(x, g, sc, sh)


def _norm_mod_bwd(x, dh_list, dres, g, sc, name):
    S, Dm = x.shape
    tm = _tm(S)
    nh = len(dh_list)

    def body(*refs):
        x_ref = refs[0]
        dh_refs = refs[1:1 + nh]
        dres_ref, g_ref, sc_ref, dx_ref, dsc_ref, dsh_ref, dg_ref = refs[1 + nh:]
        i = pl.program_id(0)

        @pl.when(i == 0)
        def _():
            dsc_ref[...] = jnp.zeros_like(dsc_ref)
            dsh_ref[...] = jnp.zeros_like(dsh_ref)
            dg_ref[...] = jnp.zeros_like(dg_ref)

        xv = x_ref[...]
        dh = dh_refs[0][...]
        for r_ in dh_refs[1:]:
            dh = dh + r_[...]
        gv = g_ref[...]
        r = lax.rsqrt(jnp.mean(xv * xv, axis=-1, keepdims=True) + EPS)
        xn = xv * r
        xg = xn * gv
        dsh_ref[...] += jnp.sum(dh, axis=0, keepdims=True)
        dsc_ref[...] += jnp.sum(dh * xg, axis=0, keepdims=True)
        dxg = dh * (1.0 + sc_ref[...])
        dg_ref[...] += jnp.sum(dxg * xn, axis=0, keepdims=True)
        dxn = dxg * gv
        dx_ref[...] = dres_ref[...] + r * (dxn - xn * jnp.mean(dxn * xn, axis=-1, keepdims=True))

    return pl.pallas_call(
        body, grid=(S // tm,),
        in_specs=[_rows(tm, Dm)] * (2 + nh) + [_vec(Dm), _vec(Dm)],
        out_specs=[_rows(tm, Dm), _vec(Dm), _vec(Dm), _vec(Dm)],
        out_shape=[SDS((S, Dm), F32), SDS((1, Dm), F32), SDS((1, Dm), F32), SDS((1, Dm), F32)],
        compiler_params=_cparams(("arbitrary",)), name=name)(x, *dh_list, dres, g, sc)


def _resid_fwd(x, val, g, name):
    S, Dm = x.shape
    tm = _tm(S)

    def body(x_ref, v_ref, g_ref, o_ref):
        o_ref[...] = x_ref[...] + g_ref[...] * v_ref[...]

    return pl.pallas_call(
        body, grid=(S // tm,), in_specs=[_rows(tm, Dm), _rows(tm, Dm), _vec(Dm)],
        out_specs=_rows(tm, Dm), out_shape=SDS((S, Dm), F32),
        compiler_params=_cparams(("parallel",)), name=name)(x, val, g)


def _resid_bwd(dx, val, g, name):
    S, Dm = dx.shape
    tm = _tm(S)

    def body(dx_ref, v_ref, g_ref, dg_ref, dv_ref):
        @pl.when(pl.program_id(0) == 0)
        def _():
            dg_ref[...] = jnp.zeros_like(dg_ref)

        dxv = dx_ref[...]
        dg_ref[...] += jnp.sum(dxv * v_ref[...], axis=0, keepdims=True)
        dv_ref[...] = (dxv * g_ref[...]).astype(dv_ref.dtype)

    return pl.pallas_call(
        body, grid=(S // tm,), in_specs=[_rows(tm, Dm), _rows(tm, Dm), _vec(Dm)],
        out_specs=[_vec(Dm), _rows(tm, Dm)], out_shape=[SDS((1, Dm), F32), SDS((S, Dm), BF16)],
        compiler_params=_cparams(("arbitrary",)), name=name)(dx, val, g)


def _merge_fwd(y, gf, name):
    S = y.shape[0]
    tm = _tm(S)
    W = 3 * D_MODEL

    def body(y_ref, g_ref, o_ref):
        acc = None
        for k in range(3):
            sl = slice(k * D_MODEL, (k + 1) * D_MODEL)
            t = _sigmoid(g_ref[:, sl]) * y_ref[:, sl]
            acc = t if acc is None else acc + t
        o_ref[...] = acc.astype(o_ref.dtype)

    return pl.pallas_call(
        body, grid=(S // tm,), in_specs=[_rows(tm, W), _rows(tm, W)],
        out_specs=_rows(tm, D_MODEL), out_shape=SDS((S, D_MODEL), BF16),
        compiler_params=_cparams(("parallel",)), name=name)(y, gf)


def _merge_bwd(dm, y, gf, name):
    S = y.shape[0]
    tm = _tm(S)
    W = 3 * D_MODEL

    def body(dm_ref, y_ref, g_ref, dy_ref, dg_ref):
        dmv = dm_ref[...]
        for k in range(3):
            sl = slice(k * D_MODEL, (k + 1) * D_MODEL)
            sg = _sigmoid(g_ref[:, sl])
            dy_ref[:, sl] = (dmv * sg).astype(dy_ref.dtype)
            dg_ref[:, sl] = (dmv * y_ref[:, sl] * (sg * (1.0 - sg))).astype(dg_ref.dtype)

    return pl.pallas_call(
        body, grid=(S // tm,), in_specs=[_rows(tm, D_MODEL), _rows(tm, W), _rows(tm, W)],
        out_specs=[_rows(tm, W), _rows(tm, W)], out_shape=[SDS((S, W), BF16), SDS((S, W), BF16)],
        compiler_params=_cparams(("parallel",)), name=name)(dm, y, gf)


def _swiglu_fwd(u, name):
    S = u.shape[0]
    tm = _tm(S)

    def body(g_ref, u_ref, a_ref):
        gv = g_ref[...]
        a_ref[...] = (gv * _sigmoid(gv) * u_ref[...]).astype(a_ref.dtype)

    return pl.pallas_call(
        body, grid=(S // tm,), in_specs=[_rows(tm, FFN_H, 0), _rows(tm, FFN_H, 1)],
        out_specs=_rows(tm, FFN_H), out_shape=SDS((S, FFN_H), BF16),
        compiler_params=_cparams(("parallel",)), name=name)(u, u)


def _swiglu_bwd(da, u, name):
    S = u.shape[0]
    tm = _tm(S)

    def body(da_ref, g_ref, u_ref, du_ref):
        dav = da_ref[...]
        gv = g_ref[...]
        sg = _sigmoid(gv)
        du_ref[:, 0:FFN_H] = (dav * u_ref[...] * (sg * (1.0 + gv * (1.0 - sg)))).astype(du_ref.dtype)
        du_ref[:, FFN_H:2 * FFN_H] = (dav * (gv * sg)).astype(du_ref.dtype)

    return pl.pallas_call(
        body, grid=(S // tm,), in_specs=[_rows(tm, FFN_H), _rows(tm, FFN_H, 0), _rows(tm, FFN_H, 1)],
        out_specs=_rows(tm, 2 * FFN_H), out_shape=SDS((S, 2 * FFN_H), BF16),
        compiler_params=_cparams(("parallel",)), name=name)(da, u, u)


def _final_loss(x, g, target, name):
    S, Dm = x.shape
    tm = _tm(S)

    def body(x_ref, g_ref, t_ref, loss_ref, dx_ref, dg_ref):
        @pl.when(pl.program_id(0) == 0)
        def _():
            loss_ref[...] = jnp.zeros_like(loss_ref)
            dg_ref[...] = jnp.zeros_like(dg_ref)

        xv = x_ref[...]
        gv = g_ref[...]
        r = lax.rsqrt(jnp.mean(xv * xv, axis=-1, keepdims=True) + EPS)
        xn = xv * r
        err = xn * gv - t_ref[...]
        row = jnp.mean(err * err, axis=-1, keepdims=True)
        loss_ref[...] += 0.5 * jnp.sum(row, axis=0, keepdims=True)
        dy = err * (1.0 / Dm)
        dg_ref[...] += jnp.sum(dy * xn, axis=0, keepdims=True)
        dxn = dy * gv
        dx_ref[...] = r * (dxn - xn * jnp.mean(dxn * xn, axis=-1, keepdims=True))

    return pl.pallas_call(
        body, grid=(S // tm,), in_specs=[_rows(tm, Dm), _vec(Dm), _rows(tm, Dm)],
        out_specs=[pl.BlockSpec((1, 1), lambda i: (0, 0)), _rows(tm, Dm), _vec(Dm)],
        out_shape=[SDS((1, 1), F32), SDS((S, Dm), F32), SDS((1, Dm), F32)],
        compiler_params=_cparams(("arbitrary",)), name=name)(x, g, target)


def _band_softmax(qg, kg, bias, sink, valid):
    s = lax.dot_general(qg, kg, (((1,), (1,)), ((), ())), preferred_element_type=F32)
    s = jnp.where(valid, s + bias, NEG_INF)
    m = jnp.maximum(jnp.max(s, axis=-1, keepdims=True), sink)
    e = jnp.exp(s - m)
    es = jnp.exp(sink - m)
    l = jnp.sum(e, axis=-1, keepdims=True) + es
    return e / l, es / l


def _band_attn_fwd(q, k, v, bias, sink, *, G, P, kvoff, name):
    S = q.shape[0]
    ng = q.shape[1] // (G * HEAD_DIM)
    band = (P + 1) * CHUNK
    pad = P * CHUNK
    nc = S // CHUNK

    def body(q_ref, k_ref, v_ref, b_ref, s_ref, o_ref, kp, vp):
        kp[0:pad, :] = jnp.zeros((pad, LANE), BF16)
        vp[0:pad, :] = jnp.zeros((pad, LANE), BF16)
        kp[pad:pad + S, :] = k_ref[...]
        vp[pad:pad + S, :] = v_ref[...]
        col = lax.broadcasted_iota(jnp.int32, (CHUNK, band), 1)

        def step(n, carry):
            r = pl.multiple_of(n * CHUNK, CHUNK)
            qn = q_ref[pl.ds(r, CHUNK), :]
            kb = kp[pl.ds(r, band), :]
            vb = vp[pl.ds(r, band), :]
            valid = col >= (P - n) * CHUNK
            for g in range(G):
                ko = kvoff(g) * HEAD_DIM
                qg = qn[:, g * HEAD_DIM:(g + 1) * HEAD_DIM] * 0.125
                p, _ = _band_softmax(qg, kb[:, ko:ko + HEAD_DIM], b_ref[g], s_ref[g, 0:1, 0:1], valid)
                og = jnp.dot(p.astype(BF16), vb[:, ko:ko + HEAD_DIM], preferred_element_type=F32)
                o_ref[pl.ds(r, CHUNK), g * HEAD_DIM:(g + 1) * HEAD_DIM] = og.astype(o_ref.dtype)
            return carry

        lax.fori_loop(0, nc, step, 0, unroll=min(BAND_UNROLL_FWD, nc))

    GW = G * HEAD_DIM
    return pl.pallas_call(
        body, grid=(ng,),
        in_specs=[pl.BlockSpec((S, GW), lambda i: (0, i)), pl.BlockSpec((S, LANE), lambda i: (0, i)),
                  pl.BlockSpec((S, LANE), lambda i: (0, i)),
                  pl.BlockSpec((G, CHUNK, band), lambda i: (i, 0, 0)),
                  pl.BlockSpec((G, 8, LANE), lambda i: (i, 0, 0))],
        out_specs=pl.BlockSpec((S, GW), lambda i: (0, i)),
        out_shape=SDS((S, ng * GW), BF16),
        scratch_shapes=[pltpu.VMEM((S + pad, LANE), BF16), pltpu.VMEM((S + pad, LANE), BF16)],
        compiler_params=_cparams(("parallel",)), name=name)(q, k, v, bias, sink)


def _band_attn_bwd(q, k, v, bias, sink, do, *, G, P, kvoff, name):
    S = q.shape[0]
    ng = q.shape[1] // (G * HEAD_DIM)
    band = (P + 1) * CHUNK
    pad = P * CHUNK
    nc = S // CHUNK
    TN = (((0,), (0,)), ((), ()))

    def body(q_ref, k_ref, v_ref, b_ref, s_ref, do_ref, dq_ref, dk_ref, dv_ref, db_ref, dsk_ref,
             kp, vp, dkp, dvp):
        kp[0:pad, :] = jnp.zeros((pad, LANE), BF16)
        vp[0:pad, :] = jnp.zeros((pad, LANE), BF16)
        kp[pad:pad + S, :] = k_ref[...]
        vp[pad:pad + S, :] = v_ref[...]
        dkp[...] = jnp.zeros_like(dkp)
        dvp[...] = jnp.zeros_like(dvp)
        db_ref[...] = jnp.zeros_like(db_ref)
        col = lax.broadcasted_iota(jnp.int32, (CHUNK, band), 1)

        def step(n, dsink):
            r = pl.multiple_of(n * CHUNK, CHUNK)
            qn = q_ref[pl.ds(r, CHUNK), :]
            don = do_ref[pl.ds(r, CHUNK), :]
            kb = kp[pl.ds(r, band), :]
            vb = vp[pl.ds(r, band), :]
            valid = col >= (P - n) * CHUNK
            new = []
            for g in range(G):
                ko = kvoff(g) * HEAD_DIM
                lanes = slice(g * HEAD_DIM, (g + 1) * HEAD_DIM)
                qg = qn[:, lanes] * 0.125
                kg = kb[:, ko:ko + HEAD_DIM]
                dog = don[:, lanes]
                p, ps = _band_softmax(qg, kg, b_ref[g], s_ref[g, 0:1, 0:1], valid)
                dp = lax.dot_general(dog, vb[:, ko:ko + HEAD_DIM], (((1,), (1,)), ((), ())),
                                     preferred_element_type=F32)
                delta = jnp.sum(p * dp, axis=-1, keepdims=True)
                ds = p * (dp - delta)
                new.append(dsink[g] - jnp.sum(ps * delta, axis=0, keepdims=True))
                db_ref[g] += ds
                dsb = ds.astype(BF16)
                dq = jnp.dot(dsb, kg, preferred_element_type=F32) * 0.125
                dq_ref[pl.ds(r, CHUNK), lanes] = dq.astype(dq_ref.dtype)
                dkp[pl.ds(r, band), ko:ko + HEAD_DIM] += lax.dot_general(
                    dsb, qg, TN, preferred_element_type=F32)
                dvp[pl.ds(r, band), ko:ko + HEAD_DIM] += lax.dot_general(
                    p.astype(BF16), dog, TN, preferred_element_type=F32)
            return tuple(new)

        dsink = lax.fori_loop(0, nc, step, tuple(jnp.zeros((1, 1), F32) for _ in range(G)),
                              unroll=min(BAND_UNROLL_BWD, nc))
        for g in range(G):
            dsk_ref[g] = jnp.broadcast_to(dsink[g], (8, LANE))
        dk_ref[...] = dkp[pad:pad + S, :].astype(dk_ref.dtype)
        dv_ref[...] = dvp[pad:pad + S, :].astype(dv_ref.dtype)

    GW = G * HEAD_DIM
    qs = pl.BlockSpec((S, GW), lambda i: (0, i))
    ks = pl.BlockSpec((S, LANE), lambda i: (0, i))
    bs = pl.BlockSpec((G, CHUNK, band), lambda i: (i, 0, 0))
    ss = pl.BlockSpec((G, 8, LANE), lambda i: (i, 0, 0))
    return pl.pallas_call(
        body, grid=(ng,), in_specs=[qs, ks, ks, bs, ss, qs],
        out_specs=[qs, ks, ks, bs, ss],
        out_shape=[SDS((S, ng * GW), BF16), SDS((S, ng * LANE), BF16), SDS((S, ng * LANE), BF16),
                   SDS((ng * G, CHUNK, band), F32), SDS((ng * G, 8, LANE), F32)],
        scratch_shapes=[pltpu.VMEM((S + pad, LANE), BF16), pltpu.VMEM((S + pad, LANE), BF16),
                        pltpu.VMEM((S + pad, LANE), F32), pltpu.VMEM((S + pad, LANE), F32)],
        compiler_params=_cparams(("parallel",)), name=name)(q, k, v, bias, sink, do)


PAIR = 2 * CHUNK


def _bandT_softmax(kg, qTg, bias, sink, valid):
    s = jnp.dot(kg, qTg, preferred_element_type=F32)
    s = jnp.where(valid, s + bias, NEG_INF)
    m = jnp.maximum(jnp.max(s, axis=0, keepdims=True), sink)
    e = jnp.exp(s - m)
    es = jnp.exp(sink - m)
    inv = 1.0 / (jnp.sum(e, axis=0, keepdims=True) + es)
    return e * inv, es * inv


def _pad_copy_rows(dst, src, pad, S):
    dst[:, 0:pad, :] = jnp.zeros((dst.shape[0], pad, dst.shape[2]), dst.dtype)
    dst[:, pad:pad + S, :] = src[...]


def _pad_copy_lanes(dst, src, pad, S):
    dst[:, 0:pad] = jnp.zeros((dst.shape[0], pad), dst.dtype)
    dst[:, pad:pad + S] = src[...]


def _bandT_fwd(qT, k_h, vT, bias, sink, *, GQ, GK, P, kvoff, name):
    S = qT.shape[1]
    ng = qT.shape[0] // (GQ * HEAD_DIM)
    BU = (P + 2) * CHUNK
    pad = P * CHUNK
    npair = S // PAIR

    def body(qT_ref, k_ref, vT_ref, b_ref, s_ref, oT_ref, kp, vTp):
        _pad_copy_rows(kp, k_ref, pad, S)
        _pad_copy_lanes(vTp, vT_ref, pad, S)
        rowi = lax.broadcasted_iota(jnp.int32, (BU, PAIR), 0)

        def step(n2, carry):
            r = pl.multiple_of(n2 * PAIR, PAIR)
            valid = rowi >= (P - 2 * n2) * CHUNK
            for g in range(GQ):
                kv = kvoff(g)
                hs = slice(g * HEAD_DIM, (g + 1) * HEAD_DIM)
                kvs = slice(kv * HEAD_DIM, (kv + 1) * HEAD_DIM)
                qTg = qT_ref[hs, pl.ds(r, PAIR)] * 0.125
                p, _ = _bandT_softmax(kp[kv, pl.ds(r, BU), :], qTg, b_ref[g], s_ref[g, 0:1, :], valid)
                oTg = jnp.dot(vTp[kvs, pl.ds(r, BU)], p.astype(BF16), preferred_element_type=F32)
                oT_ref[hs, pl.ds(r, PAIR)] = oTg.astype(oT_ref.dtype)
            return carry

        lax.fori_loop(0, npair, step, 0, unroll=min(2, npair))

    return pl.pallas_call(
        body, grid=(ng,),
        in_specs=[pl.BlockSpec((GQ * HEAD_DIM, S), lambda i: (i, 0)),
                  pl.BlockSpec((GK, S, HEAD_DIM), lambda i: (i, 0, 0)),
                  pl.BlockSpec((GK * HEAD_DIM, S), lambda i: (i, 0)),
                  pl.BlockSpec((GQ, BU, PAIR), lambda i: (i, 0, 0)),
                  pl.BlockSpec((GQ, 8, LANE), lambda i: (i, 0, 0))],
        out_specs=pl.BlockSpec((GQ * HEAD_DIM, S), lambda i: (i, 0)),
        out_shape=SDS((ng * GQ * HEAD_DIM, S), BF16),
        scratch_shapes=[pltpu.VMEM((GK, S + pad, HEAD_DIM), BF16), pltpu.VMEM((GK * HEAD_DIM, S + pad), BF16)],
        compiler_params=_cparams(("parallel",)), name=name)(qT, k_h, vT, bias, sink)


def _bandT_bwd(qT, q_h, k_h, kT, v_h, doT, do_h, bias, sink, *, GQ, GK, P, kvoff, name):
    S = qT.shape[1]
    ng = qT.shape[0] // (GQ * HEAD_DIM)
    BU = (P + 2) * CHUNK
    pad = P * CHUNK
    npair = S // PAIR

    def body(qT_ref, q_ref, k_ref, kT_ref, v_ref, doT_ref, do_ref, b_ref, s_ref,
             dqT_ref, dk_ref, dv_ref, db_ref, dsk_ref, kp, kTp, vp, dkp, dvp):
        _pad_copy_rows(kp, k_ref, pad, S)
        _pad_copy_rows(vp, v_ref, pad, S)
        _pad_copy_lanes(kTp, kT_ref, pad, S)
        dkp[...] = jnp.zeros_like(dkp)
        dvp[...] = jnp.zeros_like(dvp)
        db_ref[...] = jnp.zeros_like(db_ref)
        rowi = lax.broadcasted_iota(jnp.int32, (BU, PAIR), 0)

        def step(n2, dsink):
            r = pl.multiple_of(n2 * PAIR, PAIR)
            valid = rowi >= (P - 2 * n2) * CHUNK
            new = []
            for g in range(GQ):
                kv = kvoff(g)
                hs = slice(g * HEAD_DIM, (g + 1) * HEAD_DIM)
                kvs = slice(kv * HEAD_DIM, (kv + 1) * HEAD_DIM)
                qTg = qT_ref[hs, pl.ds(r, PAIR)] * 0.125
                p, ps = _bandT_softmax(kp[kv, pl.ds(r, BU), :], qTg, b_ref[g], s_ref[g, 0:1, :], valid)
                dp = jnp.dot(vp[kv, pl.ds(r, BU), :], doT_ref[hs, pl.ds(r, PAIR)], preferred_element_type=F32)
                delta = jnp.sum(p * dp, axis=0, keepdims=True)
                ds = p * (dp - delta)
                new.append(dsink[g] - ps * delta)
                db_ref[g] += ds
                dsb = ds.astype(BF16)
                dq = jnp.dot(kTp[kvs, pl.ds(r, BU)], dsb, preferred_element_type=F32) * 0.125
                dqT_ref[hs, pl.ds(r, PAIR)] = dq.astype(dqT_ref.dtype)
                dkp[kv, pl.ds(r, BU), :] += jnp.dot(dsb, q_ref[g, pl.ds(r, PAIR), :] * 0.125,
                                                    preferred_element_type=F32)
                dvp[kv, pl.ds(r, BU), :] += jnp.dot(p.astype(BF16), do_ref[g, pl.ds(r, PAIR), :],
                                                    preferred_element_type=F32)
            return tuple(new)

        dsink = lax.fori_loop(0, npair, step, tuple(jnp.zeros((1, PAIR), F32) for _ in range(GQ)))
        for g in range(GQ):
            dsk_ref[g] = jnp.broadcast_to(jnp.sum(dsink[g], axis=1, keepdims=True), (8, LANE))
        dk_ref[...] = dkp[:, pad:pad + S, :].astype(dk_ref.dtype)
        dv_ref[...] = dvp[:, pad:pad + S, :].astype(dv_ref.dtype)

    qTs = pl.BlockSpec((GQ * HEAD_DIM, S), lambda i: (i, 0))
    qhs = pl.BlockSpec((GQ, S, HEAD_DIM), lambda i: (i, 0, 0))
    khs = pl.BlockSpec((GK, S, HEAD_DIM), lambda i: (i, 0, 0))
    kTs = pl.BlockSpec((GK * HEAD_DIM, S), lambda i: (i, 0))
    bs = pl.BlockSpec((GQ, BU, PAIR), lambda i: (i, 0, 0))
    ss = pl.BlockSpec((GQ, 8, LANE), lambda i: (i, 0, 0))
    nkv = ng * GK
    return pl.pallas_call(
        body, grid=(ng,), in_specs=[qTs, qhs, khs, kTs, khs, qTs, qhs, bs, ss],
        out_specs=[qTs, khs, khs, bs, ss],
        out_shape=[SDS((ng * GQ * HEAD_DIM, S), BF16), SDS((nkv, S, HEAD_DIM), BF16), SDS((nkv, S, HEAD_DIM), BF16),
                   SDS((ng * GQ, BU, PAIR), F32), SDS((ng * GQ, 8, LANE), F32)],
        scratch_shapes=[pltpu.VMEM((GK, S + pad, HEAD_DIM), BF16), pltpu.VMEM((GK * HEAD_DIM, S + pad), BF16),
                        pltpu.VMEM((GK, S + pad, HEAD_DIM), BF16),
                        pltpu.VMEM((GK, S + pad, HEAD_DIM), F32), pltpu.VMEM((GK, S + pad, HEAD_DIM), F32)],
        compiler_params=_cparams(("parallel",)), name=name)(qT, q_h, k_h, kT, v_h, doT, do_h, bias, sink)


def _pair_table(tab):
    t = jnp.transpose(tab, (0, 2, 1))
    lo = jnp.pad(t, ((0, 0), (0, CHUNK), (0, 0)), constant_values=NEG_INF)
    hi = jnp.pad(t, ((0, 0), (CHUNK, 0), (0, 0)), constant_values=NEG_INF)
    return jnp.concatenate([lo, hi], axis=2)


def _unpair_table(d):
    band = d.shape[1] - CHUNK
    return jnp.transpose(d[:, 0:band, 0:CHUNK] + d[:, CHUNK:CHUNK + band, CHUNK:PAIR], (0, 2, 1))


def _heads(a, n):
    return jnp.transpose(a.reshape(a.shape[0], n, HEAD_DIM), (1, 0, 2))


def _unheads(a):
    return jnp.transpose(a, (1, 0, 2)).reshape(a.shape[1], a.shape[0] * HEAD_DIM)


def _fox_logits(qg, kj, cq, ck, r, c, row, col):
    s = lax.dot_general(qg, kj, (((1,), (1,)), ((), ())), preferred_element_type=F32)
    s = s + cq - ck
    return jnp.where(c + col <= r + row, s, NEG_INF)


def _fox_fwd(q, k, v, cc, cr, name):
    S = q.shape[0]
    npair = q.shape[1] // LANE
    BQ, BK = min(FOX_BQ, S), min(FOX_BK, S)
    nq = S // BQ
    heads = [slice(g * HEAD_DIM, (g + 1) * HEAD_DIM) for g in range(2)]

    def body(q_ref, k_ref, v_ref, cc_ref, cr_ref, o_ref, lse_ref):
        row = lax.broadcasted_iota(jnp.int32, (BQ, BK), 0)
        col = lax.broadcasted_iota(jnp.int32, (BQ, BK), 1)

        def qstep(i, carry):
            r = pl.multiple_of(i * BQ, BQ)
            qs = [q_ref[pl.ds(r, BQ), hl] * 0.125 for hl in heads]
            cqs = [cc_ref[g, pl.ds(r, BQ), :] for g in range(2)]

            def kstep(j, st):
                c = pl.multiple_of(j * BK, BK)
                new = []
                for g, hl in enumerate(heads):
                    m, l, acc = st[g]
                    s = _fox_logits(qs[g], k_ref[pl.ds(c, BK), hl], cqs[g], cr_ref[g, :, pl.ds(c, BK)],
                                    r, c, row, col)
                    mn = jnp.maximum(m, jnp.max(s, axis=-1, keepdims=True))
                    al = jnp.exp(m - mn)
                    e = jnp.exp(s - mn)
                    l = al * l + jnp.sum(e, axis=-1, keepdims=True)
                    acc = al * acc + jnp.dot(e.astype(BF16), v_ref[pl.ds(c, BK), hl],
                                             preferred_element_type=F32)
                    new.append((mn, l, acc))
                return tuple(new)

            init = (jnp.full((BQ, 1), NEG_INF, F32), jnp.zeros((BQ, 1), F32), jnp.zeros((BQ, HEAD_DIM), F32))
            st = lax.fori_loop(0, (r + BQ + BK - 1) // BK, kstep, (init, init))
            for g, hl in enumerate(heads):
                m, l, acc = st[g]
                o_ref[pl.ds(r, BQ), hl] = (acc / l).astype(o_ref.dtype)
                lse_ref[g, pl.ds(r, BQ), :] = m + jnp.log(l)
            return carry

        lax.fori_loop(0, nq, qstep, 0)

    blk = pl.BlockSpec((S, LANE), lambda i: (0, i))
    ccs = pl.BlockSpec((2, S, 1), lambda i: (i, 0, 0))
    crs = pl.BlockSpec((2, 1, S), lambda i: (i, 0, 0))
    return pl.pallas_call(
        body, grid=(npair,), in_specs=[blk, blk, blk, ccs, crs], out_specs=[blk, ccs],
        out_shape=[SDS((S, npair * LANE), BF16), SDS((2 * npair, S, 1), F32)],
        compiler_params=_cparams(("parallel",)), name=name)(q, k, v, cc, cr)


def _fox_bwd(q, k, v, cc, cr, o, do, lse, name):
    S = q.shape[0]
    npair = q.shape[1] // LANE
    BQ, BK = min(FOX_BQ, S), min(FOX_BK, S)
    nq = S // BQ
    heads = [slice(g * HEAD_DIM, (g + 1) * HEAD_DIM) for g in range(2)]
    TN = (((0,), (0,)), ((), ()))

    def body(q_ref, k_ref, v_ref, cc_ref, cr_ref, o_ref, do_ref, lse_ref,
             dq_ref, dk_ref, dv_ref, dcr_ref, dcc_ref, dka, dva):
        dka[...] = jnp.zeros_like(dka)
        dva[...] = jnp.zeros_like(dva)
        dcr_ref[...] = jnp.zeros_like(dcr_ref)
        row = lax.broadcasted_iota(jnp.int32, (BQ, BK), 0)
        col = lax.broadcasted_iota(jnp.int32, (BQ, BK), 1)

        def qstep(i, carry):
            r = pl.multiple_of(i * BQ, BQ)
            qs = [q_ref[pl.ds(r, BQ), hl] * 0.125 for hl in heads]
            dos = [do_ref[pl.ds(r, BQ), hl] for hl in heads]
            deltas = [jnp.sum(dos[g].astype(F32) * o_ref[pl.ds(r, BQ), hl].astype(F32), axis=-1, keepdims=True)
                      for g, hl in enumerate(heads)]
            cqs = [cc_ref[g, pl.ds(r, BQ), :] for g in range(2)]
            lses = [lse_ref[g, pl.ds(r, BQ), :] for g in range(2)]

            def kstep(j, st):
                c = pl.multiple_of(j * BK, BK)
                new = []
                for g, hl in enumerate(heads):
                    dq, rs = st[g]
                    kj = k_ref[pl.ds(c, BK), hl]
                    s = _fox_logits(qs[g], kj, cqs[g], cr_ref[g, :, pl.ds(c, BK)], r, c, row, col)
                    p = jnp.exp(s - lses[g])
                    dp = lax.dot_general(dos[g], v_ref[pl.ds(c, BK), hl], (((1,), (1,)), ((), ())),
                                         preferred_element_type=F32)
                    ds = p * (dp - deltas[g])
                    dcr_ref[g, :, pl.ds(c, BK)] -= jnp.sum(ds, axis=0, keepdims=True)
                    dsb = ds.astype(BF16)
                    dka[pl.ds(c, BK), hl] += lax.dot_general(dsb, qs[g], TN, preferred_element_type=F32)
                    dva[pl.ds(c, BK), hl] += lax.dot_general(p.astype(BF16), dos[g], TN,
                                                            preferred_element_type=F32)
                    new.append((dq + jnp.dot(dsb, kj, preferred_element_type=F32),
                                rs + jnp.sum(ds, axis=-1, keepdims=True)))
                return tuple(new)

            init = (jnp.zeros((BQ, HEAD_DIM), F32), jnp.zeros((BQ, 1), F32))
            st = lax.fori_loop(0, (r + BQ + BK - 1) // BK, kstep, (init, init))
            for g, hl in enumerate(heads):
                dq_ref[pl.ds(r, BQ), hl] = (st[g][0] * 0.125).astype(dq_ref.dtype)
                dcc_ref[g, pl.ds(r, BQ), :] = st[g][1]
            return carry

        lax.fori_loop(0, nq, qstep, 0)
        dk_ref[...] = dka[...].astype(dk_ref.dtype)
        dv_ref[...] = dva[...].astype(dv_ref.dtype)

    blk = pl.BlockSpec((S, LANE), lambda i: (0, i))
    ccs = pl.BlockSpec((2, S, 1), lambda i: (i, 0, 0))
    crs = pl.BlockSpec((2, 1, S), lambda i: (i, 0, 0))
    return pl.pallas_call(
        body, grid=(npair,), in_specs=[blk, blk, blk, ccs, crs, blk, blk, ccs],
        out_specs=[blk, blk, blk, crs, ccs],
        out_shape=[SDS((S, npair * LANE), BF16)] * 3 + [SDS((2 * npair, 1, S), F32), SDS((2 * npair, S, 1), F32)],
        scratch_shapes=[pltpu.VMEM((S, LANE), F32), pltpu.VMEM((S, LANE), F32)],
        compiler_params=_cparams(("parallel",)), name=name)(q, k, v, cc, cr, o, do, lse)


def _split3(x):
    hi = x.astype(BF16)
    r1 = x - hi.astype(F32)
    mid = r1.astype(BF16)
    lo = (r1 - mid.astype(F32)).astype(BF16)
    return hi, mid, lo


def _tri_dot(tri, x):
    hi, mid, lo = _split3(x)
    return (jnp.dot(tri, hi, preferred_element_type=F32) + jnp.dot(tri, mid, preferred_element_type=F32)
            + jnp.dot(tri, lo, preferred_element_type=F32))


def _fox_cum(gf, bfo, name):
    S = gf.shape[0]
    nb = S // LANE
    fcol = (GF_COLS - LANE) // LANE

    def body(f_ref, b_ref, cum_ref):
        row = lax.broadcasted_iota(jnp.int32, (LANE, LANE), 0)
        col = lax.broadcasted_iota(jnp.int32, (LANE, LANE), 1)
        tri = jnp.where(row >= col, 1.0, 0.0).astype(BF16)
        carry = jnp.zeros((1, LANE), F32)
        for t in range(nb):
            xl = f_ref[t * LANE:(t + 1) * LANE, :] + b_ref[...]
            lf = jnp.minimum(xl, 0.0) - jnp.log(1.0 + jnp.exp(-jnp.abs(xl)))
            cblk = _tri_dot(tri, lf) + carry
            cum_ref[t * LANE:(t + 1) * LANE, :] = cblk
            carry = cblk[LANE - 1:LANE, :]

    return pl.pallas_call(
        body, grid=(1,), in_specs=[pl.BlockSpec((S, LANE), lambda i: (0, fcol)), _vec(LANE)],
        out_specs=pl.BlockSpec((S, LANE), lambda i: (0, 0)), out_shape=SDS((S, LANE), F32),
        compiler_params=_cparams(("arbitrary",)), name=name)(gf, bfo)


def _fox_cum_bwd(gf, bfo, dcum, name):
    S = gf.shape[0]
    nb = S // LANE
    fcol = (GF_COLS - LANE) // LANE

    def body(f_ref, b_ref, dc_ref, df_ref, db_ref):
        row = lax.broadcasted_iota(jnp.int32, (LANE, LANE), 0)
        col = lax.broadcasted_iota(jnp.int32, (LANE, LANE), 1)
        tri = jnp.where(row <= col, 1.0, 0.0).astype(BF16)
        carry = jnp.zeros((1, LANE), F32)
        tot = jnp.zeros((1, LANE), F32)
        for t in range(nb - 1, -1, -1):
            rows = slice(t * LANE, (t + 1) * LANE)
            dlf = _tri_dot(tri, dc_ref[rows, :]) + carry
            carry = dlf[0:1, :]
            xl = f_ref[rows, :] + b_ref[...]
            dfl = dlf * (1.0 / (1.0 + jnp.exp(xl)))
            df_ref[rows, :] = dfl.astype(df_ref.dtype)
            tot = tot + jnp.sum(dfl, axis=0, keepdims=True)
        db_ref[...] = tot

    return pl.pallas_call(
        body, grid=(1,),
        in_specs=[pl.BlockSpec((S, LANE), lambda i: (0, fcol)), _vec(LANE), pl.BlockSpec((S, LANE), lambda i: (0, 0))],
        out_specs=[pl.BlockSpec((S, LANE), lambda i: (0, 0)), _vec(LANE)],
        out_shape=[SDS((S, LANE), BF16), SDS((1, LANE), F32)],
        compiler_params=_cparams(("arbitrary",)), name=name)(gf, bfo, dcum)


def _rel_onehot(qi, band):
    r = lax.broadcasted_iota(jnp.int32, (N_REL_PAD, band), 0)
    j = lax.broadcasted_iota(jnp.int32, (N_REL_PAD, band), 1)
    idx = jnp.clip(C_PREV * CHUNK + qi - j, -REL_CLIP, REL_CLIP) + REL_CLIP
    return jnp.where(r == idx, 1.0, 0.0).astype(BF16)


def _rel_expand(rel, name):
    band = (C_PREV + 1) * CHUNK

    def body(rel_ref, o_ref):
        oh = _rel_onehot(pl.program_id(0), band)
        o_ref[0] = _tri_dot_rhs(rel_ref[...], oh)

    return pl.pallas_call(
        body, grid=(CHUNK,), in_specs=[pl.BlockSpec((N_HEADS, N_REL_PAD), lambda i: (0, 0))],
        out_specs=pl.BlockSpec((1, N_HEADS, band), lambda i: (i, 0, 0)),
        out_shape=SDS((CHUNK, N_HEADS, band), F32),
        compiler_params=_cparams(("parallel",)), name=name)(rel)


def _tri_dot_rhs(x, oh):
    hi, mid, lo = _split3(x)
    return (jnp.dot(hi, oh, preferred_element_type=F32) + jnp.dot(mid, oh, preferred_element_type=F32)
            + jnp.dot(lo, oh, preferred_element_type=F32))


def _rel_reduce(dbias, name):
    band = (C_PREV + 1) * CHUNK
    NT = (((1,), (1,)), ((), ()))

    def body(d_ref, o_ref):
        @pl.when(pl.program_id(0) == 0)
        def _():
            o_ref[...] = jnp.zeros_like(o_ref)

        oh = _rel_onehot(pl.program_id(0), band)
        hi, mid, lo = _split3(d_ref[0])
        o_ref[...] += (lax.dot_general(hi, oh, NT, preferred_element_type=F32)
                       + lax.dot_general(mid, oh, NT, preferred_element_type=F32)
                       + lax.dot_general(lo, oh, NT, preferred_element_type=F32))

    return pl.pallas_call(
        body, grid=(CHUNK,), in_specs=[pl.BlockSpec((1, N_HEADS, band), lambda i: (i, 0, 0))],
        out_specs=pl.BlockSpec((N_HEADS, N_REL_PAD), lambda i: (0, 0)),
        out_shape=SDS((N_HEADS, N_REL_PAD), F32),
        compiler_params=_cparams(("arbitrary",)), name=name)(dbias)


def _alibi_table():
    qi = np.arange(CHUNK)[:, None]
    j = np.arange((A_PREV + 1) * CHUNK)[None, :]
    dist = np.abs(A_PREV * CHUNK + qi - j).astype(np.float32)
    slopes = np.exp2(-8.0 * np.arange(1, N_HEADS + 1, dtype=np.float32) / N_HEADS).astype(np.float32)
    return jnp.asarray(-slopes[:, None, None] * dist[None])


def _ada_fwd(c_all, w, b, name):
    n = w.shape[2]

    def body(c_ref, w_ref, b_ref, o_ref):
        cv = c_ref[...]
        cond = (cv * _sigmoid(cv)).astype(BF16)
        o_ref[0] = jnp.dot(cond, w_ref[0].astype(BF16), preferred_element_type=F32) + b_ref[0]

    return pl.pallas_call(
        body, grid=(DEPTH,),
        in_specs=[pl.BlockSpec((16, D_MODEL), lambda l: (0, 0)), pl.BlockSpec((1, D_MODEL, n), lambda l: (l, 0, 0)),
                  pl.BlockSpec((1, 1, n), lambda l: (l, 0, 0))],
        out_specs=pl.BlockSpec((1, 16, n), lambda l: (l, 0, 0)), out_shape=SDS((DEPTH, 16, n), F32),
        compiler_params=_cparams(("parallel",)), name=name)(c_all, w, b)


def _ada_bwd(c_t, dmod, name):
    n = dmod.shape[2]
    bn = _blk(n, 512)
    tr = 256

    def body(c_ref, d_ref, o_ref):
        cv = c_ref[...]
        cond = (cv * _sigmoid(cv)).astype(BF16).astype(F32)
        dm = d_ref[0].astype(BF16).astype(F32)
        acc = cond[:, 0:1] * dm[0:1, :]
        for b_ in range(1, 8):
            acc = acc + cond[:, b_:b_ + 1] * dm[b_:b_ + 1, :]
        o_ref[0] = acc

    return pl.pallas_call(
        body, grid=(DEPTH, D_MODEL // tr, n // bn),
        in_specs=[pl.BlockSpec((tr, 8), lambda l, i, j: (i, 0)), pl.BlockSpec((1, 8, bn), lambda l, i, j: (l, 0, j))],
        out_specs=pl.BlockSpec((1, tr, bn), lambda l, i, j: (l, i, j)), out_shape=SDS((DEPTH, D_MODEL, n), F32),
        compiler_params=_cparams(("parallel", "parallel", "parallel")), name=name)(c_t, dmod)


def _adamw(w, m, v, parts, name, tr=None):
    R, C = w.shape
    P = parts.shape[0]
    if tr is None:
        tr = _blk_rows(R, max(16, (1 << 18) // C))
    c1 = 1.0 - ADAM_B1 ** ADAM_STEP
    c2 = 1.0 - ADAM_B2 ** ADAM_STEP

    def body(w_ref, m_ref, v_ref, p_ref, g_ref, d_ref, nm_ref, nv_ref):
        g = p_ref[0].astype(F32)
        for k in range(1, P):
            g = g + p_ref[k].astype(F32)
        mn = ADAM_B1 * m_ref[...] + (1.0 - ADAM_B1) * g
        vn = ADAM_B2 * v_ref[...] + (1.0 - ADAM_B2) * (g * g)
        m_hat = mn / c1
        v_hat = vn / c2
        g_ref[...] = g
        nm_ref[...] = mn
        nv_ref[...] = vn
        d_ref[...] = -ADAM_LR * (m_hat / (jnp.sqrt(v_hat) + ADAM_EPS) + ADAM_WD * w_ref[...])

    rs = pl.BlockSpec((tr, C), lambda i: (i, 0))
    return pl.pallas_call(
        body, grid=(R // tr,), in_specs=[rs, rs, rs, pl.BlockSpec((P, tr, C), lambda i: (0, i, 0))],
        out_specs=[rs, rs, rs, rs], out_shape=[SDS((R, C), F32)] * 4,
        compiler_params=_cparams(("parallel",)), name=name)(w, m, v, parts)


def _blk_rows(R, cap):
    if R <= cap:
        return R
    best = None
    for t in range(16, cap + 1, 16):
        if R % t == 0:
            best = t
    assert best is not None, (R, cap)
    return best


def _add_cast(a, b, name):
    Q, R, C = a.shape
    tr = _blk_rows(R, max(16, (1 << 19) // C))

    def body(a_ref, b_ref, o_ref):
        o_ref[...] = (a_ref[...] + b_ref[...]).astype(o_ref.dtype)

    bs = pl.BlockSpec((1, tr, C), lambda q, i: (q, i, 0))
    return pl.pallas_call(
        body, grid=(Q, R // tr), in_specs=[bs, bs], out_specs=bs, out_shape=SDS((Q, R, C), BF16),
        compiler_params=_cparams(("parallel", "parallel")), name=name)(a, b)


def _coords():
    return lax.axis_index("x"), lax.axis_index("y"), lax.axis_index("c")


def _flip(v, bit):
    return 1 - v if bit else v


def _all_gather8(v, name):
    R = v.shape[0]

    def body(v_ref, o_ref, send_sems, recv_sems):
        x, y, c = _coords()
        me = 4 * x + 2 * y + c
        o_ref[me] = v_ref[...]
        copies = []
        for k in range(1, 8):
            peer = (_flip(x, k & 4), _flip(y, k & 2), _flip(c, k & 1))
            cp = pltpu.make_async_remote_copy(
                src_ref=v_ref, dst_ref=o_ref.at[me], send_sem=send_sems.at[k - 1], recv_sem=recv_sems.at[k - 1],
                device_id=peer, device_id_type=MESH)
            cp.start()
            copies.append(cp)
        for cp in copies:
            cp.wait_recv()
        for cp in copies:
            cp.wait_send()

    return pl.pallas_call(
        body, in_specs=[VMEM_SPEC], out_specs=VMEM_SPEC, out_shape=SDS((8, R, LANE), v.dtype),
        scratch_shapes=[pltpu.SemaphoreType.DMA((7,)), pltpu.SemaphoreType.DMA((7,))],
        compiler_params=pltpu.CompilerParams(vmem_limit_bytes=VMEM_LIMIT), name=name)(v)


def _sibling_swap(arrs, name):
    n = len(arrs)

    def body(*refs):
        in_refs, out_refs = refs[:n], refs[n:2 * n]
        send_sems, recv_sems = refs[2 * n:]
        x, y, c = _coords()
        copies = []
        for a in range(n):
            cp = pltpu.make_async_remote_copy(
                src_ref=in_refs[a].at[1 - c], dst_ref=out_refs[a], send_sem=send_sems.at[a],
                recv_sem=recv_sems.at[a], device_id=(x, y, 1 - c), device_id_type=MESH)
            cp.start()
            copies.append(cp)
        for cp in copies:
            cp.wait_recv()
        for cp in copies:
            cp.wait_send()

    return pl.pallas_call(
        body, in_specs=[ANY] * n, out_specs=[ANY] * n,
        out_shape=[SDS(a.shape[1:], a.dtype) for a in arrs],
        scratch_shapes=[pltpu.SemaphoreType.DMA((n,)), pltpu.SemaphoreType.DMA((n,))],
        name=name)(*arrs)


def _chip_exchange(arrs, *, reduce, name):
    n = len(arrs)

    def body(*refs):
        in_refs, out_refs = refs[:n], refs[n:2 * n]
        ici_send, ici_recv, d2d_send, d2d_recv, loc_sem = refs[2 * n:]
        x, y, c = _coords()
        p = 2 * x + y
        local, first, fwd = [], [], []
        for a in range(n):
            R = out_refs[a].shape[1] // 2
            half = pl.ds(pl.multiple_of(c * R, 16), R)
            if reduce:
                lc = pltpu.make_async_copy(in_refs[a].at[p], out_refs[a].at[p, half], loc_sem.at[a])
            else:
                lc = pltpu.make_async_copy(in_refs[a], out_refs[a].at[p], loc_sem.at[a])
            lc.start()
            local.append(lc)
            for k in range(1, 4):
                qx, qy = _flip(x, k & 2), _flip(y, k & 1)
                src = in_refs[a].at[2 * qx + qy] if reduce else in_refs[a].at[half]
                cp = pltpu.make_async_remote_copy(
                    src_ref=src, dst_ref=out_refs[a].at[p, half], send_sem=ici_send.at[a, k - 1],
                    recv_sem=ici_recv.at[a, k - 1], device_id=(qx, qy, c), device_id_type=MESH)
                cp.start()
                first.append(cp)
        for a in range(n):
            R = out_refs[a].shape[1] // 2
            half = pl.ds(pl.multiple_of(c * R, 16), R)
            for k in range(0 if reduce else 1, 4):
                qx, qy = _flip(x, k & 2), _flip(y, k & 1)
                slot = out_refs[a].at[2 * qx + qy, half]
                if k == 0:
                    local[a].wait()
                else:
                    first[a * 3 + k - 1].wait_recv()
                cp = pltpu.make_async_remote_copy(
                    src_ref=slot, dst_ref=slot, send_sem=d2d_send.at[a, k], recv_sem=d2d_recv.at[a, k],
                    device_id=(x, y, 1 - c), device_id_type=MESH)
                cp.start()
                fwd.append(cp)
        for cp in fwd:
            cp.wait_recv()
        for cp in first + fwd:
            cp.wait_send()
        if not reduce:
            for lc in local:
                lc.wait()

    if reduce:
        out_shape = [SDS((4, 2 * a.shape[1], a.shape[2]), a.dtype) for a in arrs]
    else:
        out_shape = [SDS((4,) + a.shape, a.dtype) for a in arrs]
    return pl.pallas_call(
        body, in_specs=[ANY] * n, out_specs=[ANY] * n, out_shape=out_shape,
        scratch_shapes=[pltpu.SemaphoreType.DMA((n, 3)), pltpu.SemaphoreType.DMA((n, 3)),
                        pltpu.SemaphoreType.DMA((n, 4)), pltpu.SemaphoreType.DMA((n, 4)),
                        pltpu.SemaphoreType.DMA((n,))],
        name=name)(*arrs)


_IN_SIZES = (512, 128, 128, 512, 512, 512, 8, 512, 512, 512, 3072)
_IN_OFF = tuple(int(v) for v in np.cumsum((0,) + _IN_SIZES))


def _pack_w_in(w):
    fb0, fb1 = _IN_OFF[6], _IN_OFF[7]
    wqkv = jnp.concatenate([w[:, :fb0], w[:, fb1:_IN_OFF[10]]], axis=1)
    wgf = jnp.concatenate([w[:, _IN_OFF[10]:], w[:, fb0:fb1], jnp.zeros((w.shape[0], LANE - 8), w.dtype)], axis=1)
    return wqkv, wgf


def _unpack_w_in(dqkv, dgf):
    fb0 = _IN_OFF[6]
    return jnp.concatenate([dqkv[:, :fb0], dgf[:, 3072:3080], dqkv[:, fb0:], dgf[:, 0:3072]], axis=1)


def _pad_rows(a, rows):
    return jnp.pad(a, ((0, rows - a.shape[0]), (0, 0)))


def _small_pack(parts):
    flat = jnp.concatenate([p.reshape(-1) for p in parts])
    n = flat.shape[0]
    rows = -(-n // LANE)
    rows = -(-rows // 8) * 8
    return jnp.pad(flat, (0, rows * LANE - n)).reshape(rows, LANE)


def _small_unpack(block, shapes):
    flat = block.reshape(-1)
    out, off = [], 0
    for s in shapes:
        n = int(np.prod(s))
        out.append(flat[off:off + n].reshape(s))
        off += n
    return out


def _kv_same(g):
    return 0


def _kv_own(g):
    return g


def _layer_fwd(x, mod, p, l):
    sh_m, sc_m, g_m, sh_f, sc_f, g_f = mod
    nm = "l%d_" % l
    h1 = _norm_mod_fwd(x, p["norm_mix_g"], sc_m, sh_m, nm + "norm_mix_fwd")
    qkv = _mm(h1, p["wqkv"], mode="nn", out_dtype=BF16, name=nm + "proj_qkv")
    gf = _mm(h1, p["wgf"], mode="nn", out_dtype=F32, name=nm + "proj_gf", cap_n=640)
    qkv_t = qkv.T
    o_a_t = _bandT_fwd(qkv_t[0:512], _heads(qkv[:, 512:640], A_KV_HEADS), qkv_t[640:768], p["alibi"],
                       p["sink_tab"], GQ=4, GK=1, P=A_PREV, kvoff=_kv_same, name=nm + "attn_a_fwd")
    qb, kb, vb = qkv[:, 768:1280], qkv[:, 1280:1792], qkv[:, 1792:2304]
    cum = _fox_cum(gf, p["b_forget_pad"], nm + "fox_cum")
    cum_t = cum[:, :N_HEADS].T
    cc, cr = cum_t[:, :, None], cum_t[:, None, :]
    o_b, lse_b = _fox_fwd(qb, kb, vb, cc, cr, nm + "attn_b_fwd")
    o_c_t = _bandT_fwd(qkv_t[2304:2816], _heads(qkv[:, 2816:3328], N_HEADS), qkv_t[3328:3840], p["rel_tab"],
                       p["no_sink"], GQ=2, GK=2, P=C_PREV, kvoff=_kv_own, name=nm + "attn_c_fwd")
    o = jnp.concatenate([o_a_t.T, o_b, o_c_t.T], axis=1)
    y = _mm(o, p["wb"], mode="nn", out_dtype=F32, groups=3, name=nm + "branch")
    merged = _merge_fwd(y, gf, nm + "merge_fwd")
    mix = _mm(merged, p["wout"], mode="nn", out_dtype=F32, name=nm + "out_proj")
    x1 = _resid_fwd(x, mix, g_m, nm + "resid_mix")
    h2 = _norm_mod_fwd(x1, p["norm_ffn_g"], sc_f, sh_f, nm + "norm_ffn_fwd")
    u = _mm(h2, p["wfi"], mode="nn", out_dtype=F32, name=nm + "ffn_in", cap_n=1408)
    a = _swiglu_fwd(u, nm + "swiglu_fwd")
    f = _mm(a, p["wfo"], mode="nn", out_dtype=F32, name=nm + "ffn_out")
    x2 = _resid_fwd(x1, f, g_f, nm + "resid_ffn")
    saved = dict(x=x, h1=h1, qkv=qkv, qkv_t=qkv_t, gf=gf, cc=cc, cr=cr, o_b=o_b, lse_b=lse_b, o=o, y=y, merged=merged,
                 mix=mix, x1=x1, h2=h2, u=u, a=a, f=f)
    return x2, saved


def _layer_bwd(dx2, mod, p, s, l):
    sh_m, sc_m, g_m, sh_f, sc_f, g_f = mod
    nm = "l%d_" % l
    dg_f, df = _resid_bwd(dx2, s["f"], g_f, nm + "resid_ffn_bwd")
    da = _mm(df, p["wfo"], mode="nt", out_dtype=F32, name=nm + "ffn_out_dx", cap_n=1408)
    d_wfo = _mm(s["a"], df, mode="tn", out_dtype=F32, name=nm + "ffn_out_dw", cap_m=1408, cap_k=512)
    du = _swiglu_bwd(da, s["u"], nm + "swiglu_bwd")
    dh2 = _mm(du, p["wfi"], mode="nt", out_dtype=F32, name=nm + "ffn_in_dx")
    d_wfi = _mm(s["h2"], du, mode="tn", out_dtype=F32, name=nm + "ffn_in_dw", cap_n=1408, cap_k=512)
    dx1, dsc_f, dsh_f, dgn_f = _norm_mod_bwd(s["x1"], [dh2], dx2, p["norm_ffn_g"], sc_f, nm + "norm_ffn_bwd")
    dg_m, dmix = _resid_bwd(dx1, s["mix"], g_m, nm + "resid_mix_bwd")
    dmerged = _mm(dmix, p["wout"], mode="nt", out_dtype=F32, name=nm + "out_proj_dx")
    d_wout = _mm(s["merged"], dmix, mode="tn", out_dtype=F32, name=nm + "out_proj_dw", cap_k=512)
    dy, dgates = _merge_bwd(dmerged, s["y"], s["gf"], nm + "merge_bwd")
    do = _mm(dy, p["wb"], mode="nt", out_dtype=BF16, groups=3, name=nm + "branch_dx")
    d_wb = _mm(s["o"], dy, mode="tn", out_dtype=F32, groups=3, name=nm + "branch_dw", cap_k=512)
    qkv, qkv_t = s["qkv"], s["qkv_t"]
    do_t = do.T
    dqa_t, dka_h, dva_h, _, dsink = _bandT_bwd(
        qkv_t[0:512], _heads(qkv[:, 0:512], N_HEADS), _heads(qkv[:, 512:640], A_KV_HEADS), qkv_t[512:640],
        _heads(qkv[:, 640:768], A_KV_HEADS), do_t[0:512], _heads(do[:, 0:512], N_HEADS), p["alibi"], p["sink_tab"],
        GQ=4, GK=1, P=A_PREV, kvoff=_kv_same, name=nm + "attn_a_bwd")
    qb, kb, vb = qkv[:, 768:1280], qkv[:, 1280:1792], qkv[:, 1792:2304]
    dqb, dkb, dvb, dcr, dcc = _fox_bwd(qb, kb, vb, s["cc"], s["cr"], s["o_b"], do[:, 512:1024], s["lse_b"],
                                       nm + "attn_b_bwd")
    dcum = jnp.pad((dcr[:, 0, :] + dcc[:, :, 0]).T, ((0, 0), (0, LANE - N_HEADS)))
    dfb, db_forget = _fox_cum_bwd(s["gf"], p["b_forget_pad"], dcum, nm + "fox_cum_bwd")
    dqc_t, dkc_h, dvc_h, dbias_c, _ = _bandT_bwd(
        qkv_t[2304:2816], _heads(qkv[:, 2304:2816], N_HEADS), _heads(qkv[:, 2816:3328], N_HEADS), qkv_t[2816:3328],
        _heads(qkv[:, 3328:3840], N_HEADS), do_t[1024:1536], _heads(do[:, 1024:1536], N_HEADS), p["rel_tab"],
        p["no_sink"], GQ=2, GK=2, P=C_PREV, kvoff=_kv_own, name=nm + "attn_c_bwd")
    d_rel = _rel_reduce(jnp.transpose(_unpair_table(dbias_c), (1, 0, 2)), nm + "rel_reduce")[:, :N_REL]
    dqkv = jnp.concatenate([dqa_t.T, _unheads(dka_h), _unheads(dva_h), dqb, dkb, dvb,
                            dqc_t.T, _unheads(dkc_h), _unheads(dvc_h)], axis=1)
    dgf = jnp.concatenate([dgates, dfb], axis=1)
    dh1a = _mm(dqkv, p["wqkv"], mode="nt", out_dtype=F32, name=nm + "proj_qkv_dx", cap_k=1024)
    dh1b = _mm(dgf, p["wgf"], mode="nt", out_dtype=F32, name=nm + "proj_gf_dx", cap_k=640)
    d_wqkv = _mm(s["h1"], dqkv, mode="tn", out_dtype=F32, name=nm + "proj_qkv_dw", cap_k=512)
    d_wgf = _mm(s["h1"], dgf, mode="tn", out_dtype=F32, name=nm + "proj_gf_dw", cap_n=640, cap_k=512)
    dx, dsc_m, dsh_m, dgn_m = _norm_mod_bwd(s["x"], [dh1a, dh1b], dx1, p["norm_mix_g"], sc_m, nm + "norm_mix_bwd")
    d_mod = jnp.concatenate([dsh_m, dsc_m, dg_m, dsh_f, dsc_f, dg_f], axis=1)[0]
    grads = dict(w_in=_unpack_w_in(d_wqkv, d_wgf), w_branch=d_wb, w_out=d_wout, w_ffn_in=d_wfi, w_ffn_out=d_wfo,
                 norm_mix_g=dgn_m[0], norm_ffn_g=dgn_f[0], b_forget=db_forget[0, :N_HEADS],
                 sinks=dsink[:, 0, 0], rel_bias=d_rel, d_mod=d_mod)
    return dx, grads


def kernel(x, c, norm_mix_g, norm_ffn_g, w_ada, b_ada, w_in, b_forget, sinks, rel_bias, w_branch, w_out, w_ffn_in, w_ffn_out, final_norm_g, loss_target, m_norm_mix_g, m_norm_ffn_g, m_w_ada, m_b_ada, m_w_in, m_b_forget, m_sinks, m_rel_bias, m_w_branch, m_w_out, m_w_ffn_in, m_w_ffn_out, m_final_norm_g, v_norm_mix_g, v_norm_ffn_g, v_w_ada, v_b_ada, v_w_in, v_b_forget, v_sinks, v_rel_bias, v_w_branch, v_w_out, v_w_ffn_in, v_w_ffn_out, v_final_norm_g):
    xi, yi, ci = _coords()
    chip = 2 * xi + yi
    dev = 2 * chip + ci
    xs = x[0]
    S = xs.shape[0]
    n_ada = w_ada.shape[2]

    big_names = ("w_in", "w_branch", "w_out", "w_ffn_in", "w_ffn_out")
    big_w = dict(w_in=w_in, w_branch=w_branch, w_out=w_out, w_ffn_in=w_ffn_in, w_ffn_out=w_ffn_out)
    big_m = dict(w_in=m_w_in, w_branch=m_w_branch, w_out=m_w_out, w_ffn_in=m_w_ffn_in, w_ffn_out=m_w_ffn_out)
    big_v = dict(w_in=v_w_in, w_branch=v_w_branch, w_out=v_w_out, w_ffn_in=v_w_ffn_in, w_ffn_out=v_w_ffn_out)
    flat2 = lambda a: a.reshape(-1, a.shape[-1])
    shards = [flat2(big_w[n]).astype(BF16) for n in big_names]
    gw_in, gw_branch, gw_out, gw_ffn_in, gw_ffn_out = _chip_exchange(shards, reduce=False, name="weights_all_gather")
    cin = w_in.shape[2]
    cbr = w_branch.shape[3]
    rout = w_out.shape[1]
    cfi = w_ffn_in.shape[2]
    rfo = w_ffn_out.shape[1]
    w_in_full = gw_in.reshape(4, DEPTH, D_MODEL, cin).transpose(1, 2, 0, 3).reshape(DEPTH, D_MODEL, 4 * cin)
    w_branch_full = gw_branch.reshape(4, DEPTH, 3, BRANCH_W, cbr).transpose(1, 2, 3, 0, 4).reshape(
        DEPTH, 3 * BRANCH_W, 4 * cbr)
    w_out_full = gw_out.reshape(4, DEPTH, rout, D_MODEL).transpose(1, 0, 2, 3).reshape(DEPTH, 4 * rout, D_MODEL)
    w_ffn_in_full = gw_ffn_in.reshape(4, DEPTH, D_MODEL, cfi).transpose(1, 2, 0, 3).reshape(DEPTH, D_MODEL, 4 * cfi)
    w_ffn_out_full = gw_ffn_out.reshape(4, DEPTH, rfo, D_MODEL).transpose(1, 0, 2, 3).reshape(DEPTH, 4 * rfo, D_MODEL)

    c_all = _all_gather8(c.reshape(8, LANE), "gather_c").reshape(8, D_MODEL)
    b_sh = lax.dynamic_slice_in_dim(b_ada, chip * n_ada, n_ada, axis=1)[:, None, :]
    mod_sh = _ada_fwd(_pad_rows(c_all, 16), w_ada, b_sh, "ada_fwd")[:, :8, :]
    mod_all = _all_gather8(mod_sh.reshape(-1, LANE), "gather_mod").reshape(8, DEPTH, 8, n_ada)
    mod_mine = lax.dynamic_index_in_dim(mod_all[0::2], dev, axis=2, keepdims=False)
    mod = mod_mine.transpose(1, 0, 2).reshape(DEPTH, 6, D_MODEL)

    alibi = _pair_table(_alibi_table())
    no_sink = jnp.full((N_HEADS, 8, LANE), NEG_INF, F32)
    params = []
    for l in range(DEPTH):
        wqkv, wgf = _pack_w_in(w_in_full[l])
        rel_tab = _rel_expand(jnp.pad(rel_bias[l], ((0, 0), (0, N_REL_PAD - N_REL))), "l%d_rel_expand" % l)
        params.append(dict(
            wqkv=wqkv, wgf=wgf, wb=w_branch_full[l], wout=w_out_full[l], wfi=w_ffn_in_full[l], wfo=w_ffn_out_full[l],
            norm_mix_g=norm_mix_g[l][None], norm_ffn_g=norm_ffn_g[l][None],
            b_forget_pad=jnp.pad(b_forget[l], (0, LANE - N_HEADS))[None],
            sink_tab=jnp.broadcast_to(sinks[l][:, None, None], (N_HEADS, 8, LANE)),
            no_sink=no_sink, alibi=alibi, rel_tab=_pair_table(jnp.transpose(rel_tab, (1, 0, 2)))))
    mods = [[mod[l, k][None] for k in range(6)] for l in range(DEPTH)]
    h = xs
    saved = []
    for l in range(DEPTH):
        h, s = _layer_fwd(h, mods[l], params[l], l)
        saved.append(s)
    loss_dev, dh, d_final = _final_loss(h, final_norm_g[None], loss_target[0], "final_loss")
    grads = [None] * DEPTH
    for l in reversed(range(DEPTH)):
        dh, grads[l] = _layer_bwd(dh, mods[l], params[l], saved[l], l)
    grad_x = dh[None]
    loss = lax.psum(loss_dev[0, 0], ("x", "y", "c"))

    def by_quarter(name, g):
        if name == "w_in":
            return g.reshape(DEPTH, D_MODEL, 4, cin).transpose(0, 2, 1, 3)
        if name == "w_branch":
            return g.reshape(DEPTH, 3 * BRANCH_W, 4, cbr).transpose(0, 2, 1, 3)
        if name == "w_out":
            return g.reshape(DEPTH, 4, rout, D_MODEL)
        if name == "w_ffn_in":
            return g.reshape(DEPTH, D_MODEL, 4, cfi).transpose(0, 2, 1, 3)
        return g.reshape(DEPTH, 4, rfo, D_MODEL)

    full = [by_quarter(n, jnp.stack([grads[l][n] for l in range(DEPTH)])) for n in big_names]
    theirs = _sibling_swap(full, "grads_sibling_swap")
    mine = [lax.dynamic_index_in_dim(g, ci, axis=0, keepdims=False) for g in full]
    chip_sum = [_add_cast(a, b, "grads_chip_sum_%s" % n) for n, a, b in zip(big_names, mine, theirs)]
    parts = _chip_exchange(chip_sum, reduce=True, name="grads_reduce_scatter")
    big_out = {}
    for n, pt in zip(big_names, parts):
        shp = big_w[n].shape
        res = _adamw(flat2(big_w[n]), flat2(big_m[n]), flat2(big_v[n]), pt, "adamw_" + n)
        big_out[n] = [r.reshape(shp) for r in res]

    small_names = ("norm_mix_g", "norm_ffn_g", "b_ada", "b_forget", "sinks", "rel_bias", "final_norm_g")
    small_w = dict(norm_mix_g=norm_mix_g, norm_ffn_g=norm_ffn_g, b_ada=b_ada, b_forget=b_forget, sinks=sinks,
                   rel_bias=rel_bias, final_norm_g=final_norm_g)
    small_m = dict(norm_mix_g=m_norm_mix_g, norm_ffn_g=m_norm_ffn_g, b_ada=m_b_ada, b_forget=m_b_forget,
                   sinks=m_sinks, rel_bias=m_rel_bias, final_norm_g=m_final_norm_g)
    small_v = dict(norm_mix_g=v_norm_mix_g, norm_ffn_g=v_norm_ffn_g, b_ada=v_b_ada, b_forget=v_b_forget,
                   sinks=v_sinks, rel_bias=v_rel_bias, final_norm_g=v_final_norm_g)
    small_g = dict(
        norm_mix_g=jnp.stack([grads[l]["norm_mix_g"] for l in range(DEPTH)]),
        norm_ffn_g=jnp.stack([grads[l]["norm_ffn_g"] for l in range(DEPTH)]),
        b_ada=jnp.stack([grads[l]["d_mod"] for l in range(DEPTH)]),
        b_forget=jnp.stack([grads[l]["b_forget"] for l in range(DEPTH)]),
        sinks=jnp.stack([grads[l]["sinks"] for l in range(DEPTH)]),
        rel_bias=jnp.stack([grads[l]["rel_bias"] for l in range(DEPTH)]),
        final_norm_g=d_final[0])
    shapes = [small_w[n].shape for n in small_names]
    g_all = _all_gather8(_small_pack([small_g[n] for n in small_names]), "gather_small_grads")
    res = _adamw(_small_pack([small_w[n] for n in small_names]), _small_pack([small_m[n] for n in small_names]),
                 _small_pack([small_v[n] for n in small_names]), g_all, "adamw_small")
    small_out = {n: [] for n in small_names}
    for r in res:
        for n, a in zip(small_names, _small_unpack(r, shapes)):
            small_out[n].append(a)
    off_b = sum(int(np.prod(s)) for s in shapes[:2])
    n_mod = DEPTH * 6 * D_MODEL
    dmod_all = g_all.reshape(8, -1)[:, off_b:off_b + n_mod].reshape(8, DEPTH, 6 * D_MODEL)
    dmod_sh = lax.dynamic_slice_in_dim(dmod_all, chip * n_ada, n_ada, axis=2).transpose(1, 0, 2)
    g_ada = _ada_bwd(c_all.T, dmod_sh, "ada_bwd")
    res = _adamw(flat2(w_ada), flat2(m_w_ada), flat2(v_w_ada), flat2(g_ada)[None], "adamw_w_ada")
    ada_out = [r.reshape(w_ada.shape) for r in res]

    order = ("norm_mix_g", "norm_ffn_g", "w_ada", "b_ada", "w_in", "b_forget", "sinks", "rel_bias", "w_branch",
             "w_out", "w_ffn_in", "w_ffn_out", "final_norm_g")

    def pick(n, k):
        if n == "w_ada":
            return ada_out[k]
        if n in big_out:
            return big_out[n][k]
        return small_out[n][k]

    outs = [loss, grad_x]
    for k in range(4):
        outs += [pick(n, k) for n in order]
    return tuple(outs)
```

```python
import functools

import numpy as np
import jax
import jax.numpy as jnp
from jax import lax
from jax.experimental import pallas as pl
from jax.experimental.pallas import tpu as pltpu

F32 = jnp.float32
BF16 = jnp.bfloat16
SDS = jax.ShapeDtypeStruct

D_MODEL = 1024
DEPTH = 2
CHUNK = 64
HEAD_DIM = 64
EPS = 1e-6
NEG_INF = -1e30
N_HEADS = 8
A_KV_HEADS = 2
A_PREV = 2
C_PREV = 8
REL_CLIP = 128
N_REL = 2 * REL_CLIP + 1
N_REL_PAD = 384
BRANCH_W = 512
FFN_H = 2816
FOX_BQ = 256
FOX_BK = 512
BAND_UNROLL_FWD = 4
BAND_UNROLL_BWD = 2
QKV_COLS = 3840
GF_COLS = 3200
N_IN_COLS = 6920
LANE = 128
VMEM_LIMIT = 48 * 1024 * 1024

ADAM_LR = 0.001
ADAM_B1 = 0.9
ADAM_B2 = 0.999
ADAM_EPS = 1e-08
ADAM_WD = 0.01
ADAM_STEP = 10

MESH = pl.DeviceIdType.MESH
ANY = pl.BlockSpec(memory_space=pl.ANY)
VMEM_SPEC = pl.BlockSpec(memory_space=pltpu.VMEM)


def _cparams(sem=None):
    return pltpu.CompilerParams(dimension_semantics=sem, vmem_limit_bytes=VMEM_LIMIT)


def _blk(n, cap):
    if n <= cap:
        return n
    best = None
    for m in range(LANE, cap + 1, LANE):
        if n % m == 0:
            best = m
    assert best is not None, (n, cap)
    return best


def _sigmoid(x):
    return 1.0 / (1.0 + jnp.exp(-x))


def _mm(a, b, *, mode, out_dtype, name, groups=1, cap_m=512, cap_n=1024, cap_k=1408, col_quarters=False):
    G = groups
    assert not col_quarters or mode == "tn"
    if mode == "nn":
        M, K, N = a.shape[0], a.shape[1] // G, b.shape[1]
        assert b.shape[0] == G * K
    elif mode == "nt":
        M, K, N = a.shape[0], a.shape[1] // G, b.shape[0] // G
        assert b.shape[1] == K
    else:
        K, M, N = a.shape[0], a.shape[1] // G, b.shape[1] // G
        assert b.shape[0] == K
    bm, bn, bk = _blk(M, cap_m), _blk(N // 4 if col_quarters else N, cap_n), _blk(K, cap_k)
    nm, nn, nk = M // bm, N // bn, K // bk
    if mode == "nn":
        a_spec = pl.BlockSpec((bm, bk), lambda g, i, j, k: (i, g * nk + k))
        b_spec = pl.BlockSpec((bk, bn), lambda g, i, j, k: (g * nk + k, j))
        o_spec = pl.BlockSpec((bm, bn), lambda g, i, j, k: (i, g * nn + j))
        dims = (((1,), (0,)), ((), ()))
        out_shape = (M, G * N)
    elif mode == "nt":
        a_spec = pl.BlockSpec((bm, bk), lambda g, i, j, k: (i, g * nk + k))
        b_spec = pl.BlockSpec((bn, bk), lambda g, i, j, k: (g * nn + j, k))
        o_spec = pl.BlockSpec((bm, bn), lambda g, i, j, k: (i, g * nn + j))
        dims = (((1,), (1,)), ((), ()))
        out_shape = (M, G * N)
    else:
        a_spec = pl.BlockSpec((bk, bm), lambda g, i, j, k: (k, g * nm + i))
        b_spec = pl.BlockSpec((bk, bn), lambda g, i, j, k: (k, g * nn + j))
        dims = (((0,), (0,)), ((), ()))
        if col_quarters:
            nq = nn // 4
            o_spec = pl.BlockSpec((1, bm, bn), lambda g, i, j, k: (j // nq, g * nm + i, j % nq))
            out_shape = (4, G * M, N // 4)
        else:
            o_spec = pl.BlockSpec((bm, bn), lambda g, i, j, k: (g * nm + i, j))
            out_shape = (G * M, N)

    def body(a_ref, b_ref, o_ref, acc_ref):
        k = pl.program_id(3)

        @pl.when(k == 0)
        def _():
            acc_ref[...] = jnp.zeros_like(acc_ref)

        acc_ref[...] += lax.dot_general(a_ref[...].astype(BF16), b_ref[...].astype(BF16), dims,
                                        preferred_element_type=F32)

        @pl.when(k == nk - 1)
        def _():
            o_ref[...] = acc_ref[...].astype(o_ref.dtype).reshape(o_ref.shape)

    return pl.pallas_call(
        body, grid=(G, nm, nn, nk), in_specs=[a_spec, b_spec], out_specs=o_spec,
        out_shape=SDS(out_shape, out_dtype), scratch_shapes=[pltpu.VMEM((bm, bn), F32)],
        compiler_params=_cparams(("parallel", "parallel", "parallel", "arbitrary")), name=name,
    )(a, b)


def _rows(tm, n, col=0):
    return pl.BlockSpec((tm, n), lambda i: (i, col))


def _vec(n):
    return pl.BlockSpec((1, n), lambda i: (0, 0))


def _tm(S):
    return min(S, 256)


def _norm_mod_fwd(x, g, sc, sh, name):
    S, Dm = x.shape
    tm = _tm(S)

    def body(x_ref, g_ref, sc_ref, sh_ref, h_ref):
        xv = x_ref[...]
        r = lax.rsqrt(jnp.mean(xv * xv, axis=-1, keepdims=True) + EPS)
        h_ref[...] = ((xv * r) * g_ref[...] * (1.0 + sc_ref[...]) + sh_ref[...]).astype(h_ref.dtype)

    return pl.pallas_call(
        body, grid=(S // tm,), in_specs=[_rows(tm, Dm), _vec(Dm), _vec(Dm), _vec(Dm)],
        out_specs=_rows(tm, Dm), out_shape=SDS((S, Dm), BF16),
        compiler_params=_cparams(("parallel",)), name=name)(x, g, sc, sh)


def _norm_mod_bwd(x, dh_list, dres, g, sc, name):
    S, Dm = x.shape
    tm = _tm(S)
    nh = len(dh_list)

    def body(*refs):
        x_ref = refs[0]
        dh_refs = refs[1:1 + nh]
        dres_ref, g_ref, sc_ref, dx_ref, dsc_ref, dsh_ref, dg_ref = refs[1 + nh:]
        i = pl.program_id(0)

        @pl.when(i == 0)
        def _():
            dsc_ref[...] = jnp.zeros_like(dsc_ref)
            dsh_ref[...] = jnp.zeros_like(dsh_ref)
            dg_ref[...] = jnp.zeros_like(dg_ref)

        xv = x_ref[...]
        dh = dh_refs[0][...]
        for r_ in dh_refs[1:]:
            dh = dh + r_[...]
        gv = g_ref[...]
        r = lax.rsqrt(jnp.mean(xv * xv, axis=-1, keepdims=True) + EPS)
        xn = xv * r
        xg = xn * gv
        dsh_ref[...] += jnp.sum(dh, axis=0, keepdims=True)
        dsc_ref[...] += jnp.sum(dh * xg, axis=0, keepdims=True)
        dxg = dh * (1.0 + sc_ref[...])
        dg_ref[...] += jnp.sum(dxg * xn, axis=0, keepdims=True)
        dxn = dxg * gv
        dx_ref[...] = dres_ref[...] + r * (dxn - xn * jnp.mean(dxn * xn, axis=-1, keepdims=True))

    return pl.pallas_call(
        body, grid=(S // tm,),
        in_specs=[_rows(tm, Dm)] * (2 + nh) + [_vec(Dm), _vec(Dm)],
        out_specs=[_rows(tm, Dm), _vec(Dm), _vec(Dm), _vec(Dm)],
        out_shape=[SDS((S, Dm), F32), SDS((1, Dm), F32), SDS((1, Dm), F32), SDS((1, Dm), F32)],
        compiler_params=_cparams(("arbitrary",)), name=name)(x, *dh_list, dres, g, sc)


def _resid_fwd(x, val, g, name):
    S, Dm = x.shape
    tm = _tm(S)

    def body(x_ref, v_ref, g_ref, o_ref):
        o_ref[...] = x_ref[...] + g_ref[...] * v_ref[...]

    return pl.pallas_call(
        body, grid=(S // tm,), in_specs=[_rows(tm, Dm), _rows(tm, Dm), _vec(Dm)],
        out_specs=_rows(tm, Dm), out_shape=SDS((S, Dm), F32),
        compiler_params=_cparams(("parallel",)), name=name)(x, val, g)


def _resid_bwd(dx, val, g, name):
    S, Dm = dx.shape
    tm = _tm(S)

    def body(dx_ref, v_ref, g_ref, dg_ref, dv_ref):
        @pl.when(pl.program_id(0) == 0)
        def _():
            dg_ref[...] = jnp.zeros_like(dg_ref)

        dxv = dx_ref[...]
        dg_ref[...] += jnp.sum(dxv * v_ref[...], axis=0, keepdims=True)
        dv_ref[...] = (dxv * g_ref[...]).astype(dv_ref.dtype)

    return pl.pallas_call(
        body, grid=(S // tm,), in_specs=[_rows(tm, Dm), _rows(tm, Dm), _vec(Dm)],
        out_specs=[_vec(Dm), _rows(tm, Dm)], out_shape=[SDS((1, Dm), F32), SDS((S, Dm), BF16)],
        compiler_params=_cparams(("arbitrary",)), name=name)(dx, val, g)


def _merge_fwd(y, gf, name):
    S = y.shape[0]
    tm = _tm(S)
    W = 3 * D_MODEL

    def body(y_ref, g_ref, o_ref):
        acc = None
        for k in range(3):
            sl = slice(k * D_MODEL, (k + 1) * D_MODEL)
            t = _sigmoid(g_ref[:, sl]) * y_ref[:, sl]
            acc = t if acc is None else acc + t
        o_ref[...] = acc.astype(o_ref.dtype)

    return pl.pallas_call(
        body, grid=(S // tm,), in_specs=[_rows(tm, W), _rows(tm, W)],
        out_specs=_rows(tm, D_MODEL), out_shape=SDS((S, D_MODEL), BF16),
        compiler_params=_cparams(("parallel",)), name=name)(y, gf)


def _merge_bwd(dm, y, gf, name):
    S = y.shape[0]
    tm = _tm(S)
    W = 3 * D_MODEL

    def body(dm_ref, y_ref, g_ref, dy_ref, dg_ref):
        dmv = dm_ref[...]
        for k in range(3):
            sl = slice(k * D_MODEL, (k + 1) * D_MODEL)
            sg = _sigmoid(g_ref[:, sl])
            dy_ref[:, sl] = (dmv * sg).astype(dy_ref.dtype)
            dg_ref[:, sl] = (dmv * y_ref[:, sl] * (sg * (1.0 - sg))).astype(dg_ref.dtype)

    return pl.pallas_call(
        body, grid=(S // tm,), in_specs=[_rows(tm, D_MODEL), _rows(tm, W), _rows(tm, W)],
        out_specs=[_rows(tm, W), _rows(tm, W)], out_shape=[SDS((S, W), BF16), SDS((S, W), BF16)],
        compiler_params=_cparams(("parallel",)), name=name)(dm, y, gf)


def _swiglu_fwd(u, name):
    S = u.shape[0]
    tm = _tm(S)

    def body(g_ref, u_ref, a_ref):
        gv = g_ref[...]
        a_ref[...] = (gv * _sigmoid(gv) * u_ref[...]).astype(a_ref.dtype)

    return pl.pallas_call(
        body, grid=(S // tm,), in_specs=[_rows(tm, FFN_H, 0), _rows(tm, FFN_H, 1)],
        out_specs=_rows(tm, FFN_H), out_shape=SDS((S, FFN_H), BF16),
        compiler_params=_cparams(("parallel",)), name=name)(u, u)


def _swiglu_bwd(da, u, name):
    S = u.shape[0]
    tm = _tm(S)

    def body(da_ref, g_ref, u_ref, du_ref):
        dav = da_ref[...]
        gv = g_ref[...]
        sg = _sigmoid(gv)
        du_ref[:, 0:FFN_H] = (dav * u_ref[...] * (sg * (1.0 + gv * (1.0 - sg)))).astype(du_ref.dtype)
        du_ref[:, FFN_H:2 * FFN_H] = (dav * (gv * sg)).astype(du_ref.dtype)

    return pl.pallas_call(
        body, grid=(S // tm,), in_specs=[_rows(tm, FFN_H), _rows(tm, FFN_H, 0), _rows(tm, FFN_H, 1)],
        out_specs=_rows(tm, 2 * FFN_H), out_shape=SDS((S, 2 * FFN_H), BF16),
        compiler_params=_cparams(("parallel",)), name=name)(da, u, u)


def _final_loss(x, g, target, name):
    S, Dm = x.shape
    tm = _tm(S)

    def body(x_ref, g_ref, t_ref, loss_ref, dx_ref, dg_ref):
        @pl.when(pl.program_id(0) == 0)
        def _():
            loss_ref[...] = jnp.zeros_like(loss_ref)
            dg_ref[...] = jnp.zeros_like(dg_ref)

        xv = x_ref[...]
        gv = g_ref[...]
        r = lax.rsqrt(jnp.mean(xv * xv, axis=-1, keepdims=True) + EPS)
        xn = xv * r
        err = xn * gv - t_ref[...]
        row = jnp.mean(err * err, axis=-1, keepdims=True)
        loss_ref[...] += 0.5 * jnp.sum(row, axis=0, keepdims=True)
        dy = err * (1.0 / Dm)
        dg_ref[...] += jnp.sum(dy * xn, axis=0, keepdims=True)
        dxn = dy * gv
        dx_ref[...] = r * (dxn - xn * jnp.mean(dxn * xn, axis=-1, keepdims=True))

    return pl.pallas_call(
        body, grid=(S // tm,), in_specs=[_rows(tm, Dm), _vec(Dm), _rows(tm, Dm)],
        out_specs=[pl.BlockSpec((1, 1), lambda i: (0, 0)), _rows(tm, Dm), _vec(Dm)],
        out_shape=[SDS((1, 1), F32), SDS((S, Dm), F32), SDS((1, Dm), F32)],
        compiler_params=_cparams(("arbitrary",)), name=name)(x, g, target)


def _band_softmax(qg, kg, bias, sink, valid):
    s = lax.dot_general(qg, kg, (((1,), (1,)), ((), ())), preferred_element_type=F32)
    s = jnp.where(valid, s + bias, NEG_INF)
    m = jnp.maximum(jnp.max(s, axis=-1, keepdims=True), sink)
    e = jnp.exp(s - m)
    es = jnp.exp(sink - m)
    l = jnp.sum(e, axis=-1, keepdims=True) + es
    return e / l, es / l


def _band_attn_fwd(q, k, v, bias, sink, *, G, P, kvoff, name):
    S = q.shape[0]
    ng = q.shape[1] // (G * HEAD_DIM)
    band = (P + 1) * CHUNK
    pad = P * CHUNK
    nc = S // CHUNK

    def body(q_ref, k_ref, v_ref, b_ref, s_ref, o_ref, kp, vp):
        kp[0:pad, :] = jnp.zeros((pad, LANE), BF16)
        vp[0:pad, :] = jnp.zeros((pad, LANE), BF16)
        kp[pad:pad + S, :] = k_ref[...]
        vp[pad:pad + S, :] = v_ref[...]
        col = lax.broadcasted_iota(jnp.int32, (CHUNK, band), 1)

        def step(n, carry):
            r = pl.multiple_of(n * CHUNK, CHUNK)
            qn = q_ref[pl.ds(r, CHUNK), :]
            kb = kp[pl.ds(r, band), :]
            vb = vp[pl.ds(r, band), :]
            valid = col >= (P - n) * CHUNK
            for g in range(G):
                ko = kvoff(g) * HEAD_DIM
                qg = qn[:, g * HEAD_DIM:(g + 1) * HEAD_DIM] * 0.125
                p, _ = _band_softmax(qg, kb[:, ko:ko + HEAD_DIM], b_ref[g], s_ref[g, 0:1, 0:1], valid)
                og = jnp.dot(p.astype(BF16), vb[:, ko:ko + HEAD_DIM], preferred_element_type=F32)
                o_ref[pl.ds(r, CHUNK), g * HEAD_DIM:(g + 1) * HEAD_DIM] = og.astype(o_ref.dtype)
            return carry

        lax.fori_loop(0, nc, step, 0, unroll=min(BAND_UNROLL_FWD, nc))

    GW = G * HEAD_DIM
    return pl.pallas_call(
        body, grid=(ng,),
        in_specs=[pl.BlockSpec((S, GW), lambda i: (0, i)), pl.BlockSpec((S, LANE), lambda i: (0, i)),
                  pl.BlockSpec((S, LANE), lambda i: (0, i)),
                  pl.BlockSpec((G, CHUNK, band), lambda i: (i, 0, 0)),
                  pl.BlockSpec((G, 8, LANE), lambda i: (i, 0, 0))],
        out_specs=pl.BlockSpec((S, GW), lambda i: (0, i)),
        out_shape=SDS((S, ng * GW), BF16),
        scratch_shapes=[pltpu.VMEM((S + pad, LANE), BF16), pltpu.VMEM((S + pad, LANE), BF16)],
        compiler_params=_cparams(("parallel",)), name=name)(q, k, v, bias, sink)


def _band_attn_bwd(q, k, v, bias, sink, do, *, G, P, kvoff, name):
    S = q.shape[0]
    ng = q.shape[1] // (G * HEAD_DIM)
    band = (P + 1) * CHUNK
    pad = P * CHUNK
    nc = S // CHUNK
    TN = (((0,), (0,)), ((), ()))

    def body(q_ref, k_ref, v_ref, b_ref, s_ref, do_ref, dq_ref, dk_ref, dv_ref, db_ref, dsk_ref,
             kp, vp, dkp, dvp):
        kp[0:pad, :] = jnp.zeros((pad, LANE), BF16)
        vp[0:pad, :] = jnp.zeros((pad, LANE), BF16)
        kp[pad:pad + S, :] = k_ref[...]
        vp[pad:pad + S, :] = v_ref[...]
        dkp[...] = jnp.zeros_like(dkp)
        dvp[...] = jnp.zeros_like(dvp)
        db_ref[...] = jnp.zeros_like(db_ref)
        col = lax.broadcasted_iota(jnp.int32, (CHUNK, band), 1)

        def step(n, dsink):
            r = pl.multiple_of(n * CHUNK, CHUNK)
            qn = q_ref[pl.ds(r, CHUNK), :]
            don = do_ref[pl.ds(r, CHUNK), :]
            kb = kp[pl.ds(r, band), :]
            vb = vp[pl.ds(r, band), :]
            valid = col >= (P - n) * CHUNK
            new = []
            for g in range(G):
                ko = kvoff(g) * HEAD_DIM
                lanes = slice(g * HEAD_DIM, (g + 1) * HEAD_DIM)
                qg = qn[:, lanes] * 0.125
                kg = kb[:, ko:ko + HEAD_DIM]
                dog = don[:, lanes]
                p, ps = _band_softmax(qg, kg, b_ref[g], s_ref[g, 0:1, 0:1], valid)
                dp = lax.dot_general(dog, vb[:, ko:ko + HEAD_DIM], (((1,), (1,)), ((), ())),
                                     preferred_element_type=F32)
                delta = jnp.sum(p * dp, axis=-1, keepdims=True)
                ds = p * (dp - delta)
                new.append(dsink[g] - jnp.sum(ps * delta, axis=0, keepdims=True))
                db_ref[g] += ds
                dsb = ds.astype(BF16)
                dq = jnp.dot(dsb, kg, preferred_element_type=F32) * 0.125
                dq_ref[pl.ds(r, CHUNK), lanes] = dq.astype(dq_ref.dtype)
                dkp[pl.ds(r, band), ko:ko + HEAD_DIM] += lax.dot_general(
                    dsb, qg, TN, preferred_element_type=F32)
                dvp[pl.ds(r, band), ko:ko + HEAD_DIM] += lax.dot_general(
                    p.astype(BF16), dog, TN, preferred_element_type=F32)
            return tuple(new)

        dsink = lax.fori_loop(0, nc, step, tuple(jnp.zeros((1, 1), F32) for _ in range(G)),
                              unroll=min(BAND_UNROLL_BWD, nc))
        for g in range(G):
            dsk_ref[g] = jnp.broadcast_to(dsink[g], (8, LANE))
        dk_ref[...] = dkp[pad:pad + S, :].astype(dk_ref.dtype)
        dv_ref[...] = dvp[pad:pad + S, :].astype(dv_ref.dtype)

    GW = G * HEAD_DIM
    qs = pl.BlockSpec((S, GW), lambda i: (0, i))
    ks = pl.BlockSpec((S, LANE), lambda i: (0, i))
    bs = pl.BlockSpec((G, CHUNK, band), lambda i: (i, 0, 0))
    ss = pl.BlockSpec((G, 8, LANE), lambda i: (i, 0, 0))
    return pl.pallas_call(
        body, grid=(ng,), in_specs=[qs, ks, ks, bs, ss, qs],
        out_specs=[qs, ks, ks, bs, ss],
        out_shape=[SDS((S, ng * GW), BF16), SDS((S, ng * LANE), BF16), SDS((S, ng * LANE), BF16),
                   SDS((ng * G, CHUNK, band), F32), SDS((ng * G, 8, LANE), F32)],
        scratch_shapes=[pltpu.VMEM((S + pad, LANE), BF16), pltpu.VMEM((S + pad, LANE), BF16),
                        pltpu.VMEM((S + pad, LANE), F32), pltpu.VMEM((S + pad, LANE), F32)],
        compiler_params=_cparams(("parallel",)), name=name)(q, k, v, bias, sink, do)


PAIR = 2 * CHUNK


def _bandT_softmax(kg, qTg, bias, sink, valid):
    s = jnp.dot(kg, qTg, preferred_element_type=F32)
    s = jnp.where(valid, s + bias, NEG_INF)
    m = jnp.maximum(jnp.max(s, axis=0, keepdims=True), sink)
    e = jnp.exp(s - m)
    es = jnp.exp(sink - m)
    inv = 1.0 / (jnp.sum(e, axis=0, keepdims=True) + es)
    return e * inv, es * inv


def _pad_copy_rows(dst, src, pad, S):
    dst[:, 0:pad, :] = jnp.zeros((dst.shape[0], pad, dst.shape[2]), dst.dtype)
    dst[:, pad:pad + S, :] = src[...]


def _pad_copy_lanes(dst, src, pad, S):
    dst[:, 0:pad] = jnp.zeros((dst.shape[0], pad), dst.dtype)
    dst[:, pad:pad + S] = src[...]


def _bandT_fwd(qT, k_h, vT, bias, sink, *, GQ, GK, P, kvoff, name):
    S = qT.shape[1]
    ng = qT.shape[0] // (GQ * HEAD_DIM)
    BU = (P + 2) * CHUNK
    pad = P * CHUNK
    npair = S // PAIR

    def body(qT_ref, k_ref, vT_ref, b_ref, s_ref, oT_ref, kp, vTp):
        _pad_copy_rows(kp, k_ref, pad, S)
        _pad_copy_lanes(vTp, vT_ref, pad, S)
        rowi = lax.broadcasted_iota(jnp.int32, (BU, PAIR), 0)

        def step(n2, carry):
            r = pl.multiple_of(n2 * PAIR, PAIR)
            valid = rowi >= (P - 2 * n2) * CHUNK
            for g in range(GQ):
                kv = kvoff(g)
                hs = slice(g * HEAD_DIM, (g + 1) * HEAD_DIM)
                kvs = slice(kv * HEAD_DIM, (kv + 1) * HEAD_DIM)
                qTg = qT_ref[hs, pl.ds(r, PAIR)] * 0.125
                p, _ = _bandT_softmax(kp[kv, pl.ds(r, BU), :], qTg, b_ref[g], s_ref[g, 0:1, :], valid)
                oTg = jnp.dot(vTp[kvs, pl.ds(r, BU)], p.astype(BF16), preferred_element_type=F32)
                oT_ref[hs, pl.ds(r, PAIR)] = oTg.astype(oT_ref.dtype)
            return carry

        lax.fori_loop(0, npair, step, 0, unroll=min(2, npair))

    return pl.pallas_call(
        body, grid=(ng,),
        in_specs=[pl.BlockSpec((GQ * HEAD_DIM, S), lambda i: (i, 0)),
                  pl.BlockSpec((GK, S, HEAD_DIM), lambda i: (i, 0, 0)),
                  pl.BlockSpec((GK * HEAD_DIM, S), lambda i: (i, 0)),
                  pl.BlockSpec((GQ, BU, PAIR), lambda i: (i, 0, 0)),
                  pl.BlockSpec((GQ, 8, LANE), lambda i: (i, 0, 0))],
        out_specs=pl.BlockSpec((GQ * HEAD_DIM, S), lambda i: (i, 0)),
        out_shape=SDS((ng * GQ * HEAD_DIM, S), BF16),
        scratch_shapes=[pltpu.VMEM((GK, S + pad, HEAD_DIM), BF16), pltpu.VMEM((GK * HEAD_DIM, S + pad), BF16)],
        compiler_params=_cparams(("parallel",)), name=name)(qT, k_h, vT, bias, sink)


def _bandT_bwd(qT, q_h, k_h, kT, v_h, doT, do_h, bias, sink, *, GQ, GK, P, kvoff, name):
    S = qT.shape[1]
    ng = qT.shape[0] // (GQ * HEAD_DIM)
    BU = (P + 2) * CHUNK
    pad = P * CHUNK
    npair = S // PAIR

    def body(qT_ref, q_ref, k_ref, kT_ref, v_ref, doT_ref, do_ref, b_ref, s_ref,
             dqT_ref, dk_ref, dv_ref, db_ref, dsk_ref, kp, kTp, vp, dkp, dvp):
        _pad_copy_rows(kp, k_ref, pad, S)
        _pad_copy_rows(vp, v_ref, pad, S)
        _pad_copy_lanes(kTp, kT_ref, pad, S)
        dkp[...] = jnp.zeros_like(dkp)
        dvp[...] = jnp.zeros_like(dvp)
        db_ref[...] = jnp.zeros_like(db_ref)
        rowi = lax.broadcasted_iota(jnp.int32, (BU, PAIR), 0)

        def step(n2, dsink):
            r = pl.multiple_of(n2 * PAIR, PAIR)
            valid = rowi >= (P - 2 * n2) * CHUNK
            new = []
            for g in range(GQ):
                kv = kvoff(g)
                hs = slice(g * HEAD_DIM, (g + 1) * HEAD_DIM)
                kvs = slice(kv * HEAD_DIM, (kv + 1) * HEAD_DIM)
                qTg = qT_ref[hs, pl.ds(r, PAIR)] * 0.125
                p, ps = _bandT_softmax(kp[kv, pl.ds(r, BU), :], qTg, b_ref[g], s_ref[g, 0:1, :], valid)
                dp = jnp.dot(vp[kv, pl.ds(r, BU), :], doT_ref[hs, pl.ds(r, PAIR)], preferred_element_type=F32)
                delta = jnp.sum(p * dp, axis=0, keepdims=True)
                ds = p * (dp - delta)
                new.append(dsink[g] - ps * delta)
                db_ref[g] += ds
                dsb = ds.astype(BF16)
                dq = jnp.dot(kTp[kvs, pl.ds(r, BU)], dsb, preferred_element_type=F32) * 0.125
                dqT_ref[hs, pl.ds(r, PAIR)] = dq.astype(dqT_ref.dtype)
                dkp[kv, pl.ds(r, BU), :] += jnp.dot(dsb, q_ref[g, pl.ds(r, PAIR), :] * 0.125,
                                                    preferred_element_type=F32)
                dvp[kv, pl.ds(r, BU), :] += jnp.dot(p.astype(BF16), do_ref[g, pl.ds(r, PAIR), :],
                                                    preferred_element_type=F32)
            return tuple(new)

        dsink = lax.fori_loop(0, npair, step, tuple(jnp.zeros((1, PAIR), F32) for _ in range(GQ)))
        for g in range(GQ):
            dsk_ref[g] = jnp.broadcast_to(jnp.sum(dsink[g], axis=1, keepdims=True), (8, LANE))
        dk_ref[...] = dkp[:, pad:pad + S, :].astype(dk_ref.dtype)
        dv_ref[...] = dvp[:, pad:pad + S, :].astype(dv_ref.dtype)

    qTs = pl.BlockSpec((GQ * HEAD_DIM, S), lambda i: (i, 0))
    qhs = pl.BlockSpec((GQ, S, HEAD_DIM), lambda i: (i, 0, 0))
    khs = pl.BlockSpec((GK, S, HEAD_DIM), lambda i: (i, 0, 0))
    kTs = pl.BlockSpec((GK * HEAD_DIM, S), lambda i: (i, 0))
    bs = pl.BlockSpec((GQ, BU, PAIR), lambda i: (i, 0, 0))
    ss = pl.BlockSpec((GQ, 8, LANE), lambda i: (i, 0, 0))
    nkv = ng * GK
    return pl.pallas_call(
        body, grid=(ng,), in_specs=[qTs, qhs, khs, kTs, khs, qTs, qhs, bs, ss],
        out_specs=[qTs, khs, khs, bs, ss],
        out_shape=[SDS((ng * GQ * HEAD_DIM, S), BF16), SDS((nkv, S, HEAD_DIM), BF16), SDS((nkv, S, HEAD_DIM), BF16),
                   SDS((ng * GQ, BU, PAIR), F32), SDS((ng * GQ, 8, LANE), F32)],
        scratch_shapes=[pltpu.VMEM((GK, S + pad, HEAD_DIM), BF16), pltpu.VMEM((GK * HEAD_DIM, S + pad), BF16),
                        pltpu.VMEM((GK, S + pad, HEAD_DIM), BF16),
                        pltpu.VMEM((GK, S + pad, HEAD_DIM), F32), pltpu.VMEM((GK, S + pad, HEAD_DIM), F32)],
        compiler_params=_cparams(("parallel",)), name=name)(qT, q_h, k_h, kT, v_h, doT, do_h, bias, sink)


def _pair_table(tab):
    t = jnp.transpose(tab, (0, 2, 1))
    lo = jnp.pad(t, ((0, 0), (0, CHUNK), (0, 0)), constant_values=NEG_INF)
    hi = jnp.pad(t, ((0, 0), (CHUNK, 0), (0, 0)), constant_values=NEG_INF)
    return jnp.concatenate([lo, hi], axis=2)


def _unpair_table(d):
    band = d.shape[1] - CHUNK
    return jnp.transpose(d[:, 0:band, 0:CHUNK] + d[:, CHUNK:CHUNK + band, CHUNK:PAIR], (0, 2, 1))


def _heads(a, n):
    return jnp.transpose(a.reshape(a.shape[0], n, HEAD_DIM), (1, 0, 2))


def _unheads(a):
    return jnp.transpose(a, (1, 0, 2)).reshape(a.shape[1], a.shape[0] * HEAD_DIM)


def _fox_logits(qg, kj, cq, ck, r, c, row, col):
    s = lax.dot_general(qg, kj, (((1,), (1,)), ((), ())), preferred_element_type=F32)
    s = s + cq - ck
    return jnp.where(c + col <= r + row, s, NEG_INF)


def _fox_fwd(q, k, v, cc, cr, name):
    S = q.shape[0]
    npair = q.shape[1] // LANE
    BQ, BK = min(FOX_BQ, S), min(FOX_BK, S)
    nq = S // BQ
    heads = [slice(g * HEAD_DIM, (g + 1) * HEAD_DIM) for g in range(2)]

    def body(q_ref, k_ref, v_ref, cc_ref, cr_ref, o_ref, lse_ref):
        row = lax.broadcasted_iota(jnp.int32, (BQ, BK), 0)
        col = lax.broadcasted_iota(jnp.int32, (BQ, BK), 1)

        def qstep(i, carry):
            r = pl.multiple_of(i * BQ, BQ)
            qs = [q_ref[pl.ds(r, BQ), hl] * 0.125 for hl in heads]
            cqs = [cc_ref[g, pl.ds(r, BQ), :] for g in range(2)]

            def kstep(j, st):
                c = pl.multiple_of(j * BK, BK)
                new = []
                for g, hl in enumerate(heads):
                    m, l, acc = st[g]
                    s = _fox_logits(qs[g], k_ref[pl.ds(c, BK), hl], cqs[g], cr_ref[g, :, pl.ds(c, BK)],
                                    r, c, row, col)
                    mn = jnp.maximum(m, jnp.max(s, axis=-1, keepdims=True))
                    al = jnp.exp(m - mn)
                    e = jnp.exp(s - mn)
                    l = al * l + jnp.sum(e, axis=-1, keepdims=True)
                    acc = al * acc + jnp.dot(e.astype(BF16), v_ref[pl.ds(c, BK), hl],
                                             preferred_element_type=F32)
                    new.append((mn, l, acc))
                return tuple(new)

            init = (jnp.full((BQ, 1), NEG_INF, F32), jnp.zeros((BQ, 1), F32), jnp.zeros((BQ, HEAD_DIM), F32))
            st = lax.fori_loop(0, (r + BQ + BK - 1) // BK, kstep, (init, init))
            for g, hl in enumerate(heads):
                m, l, acc = st[g]
                o_ref[pl.ds(r, BQ), hl] = (acc / l).astype(o_ref.dtype)
                lse_ref[g, pl.ds(r, BQ), :] = m + jnp.log(l)
            return carry

        lax.fori_loop(0, nq, qstep, 0)

    blk = pl.BlockSpec((S, LANE), lambda i: (0, i))
    ccs = pl.BlockSpec((2, S, 1), lambda i: (i, 0, 0))
    crs = pl.BlockSpec((2, 1, S), lambda i: (i, 0, 0))
    return pl.pallas_call(
        body, grid=(npair,), in_specs=[blk, blk, blk, ccs, crs], out_specs=[blk, ccs],
        out_shape=[SDS((S, npair * LANE), BF16), SDS((2 * npair, S, 1), F32)],
        compiler_params=_cparams(("parallel",)), name=name)(q, k, v, cc, cr)


def _fox_bwd(q, k, v, cc, cr, o, do, lse, name):
    S = q.shape[0]
    npair = q.shape[1] // LANE
    BQ, BK = min(FOX_BQ, S), min(FOX_BK, S)
    nq = S // BQ
    heads = [slice(g * HEAD_DIM, (g + 1) * HEAD_DIM) for g in range(2)]
    TN = (((0,), (0,)), ((), ()))

    def body(q_ref, k_ref, v_ref, cc_ref, cr_ref, o_ref, do_ref, lse_ref,
             dq_ref, dk_ref, dv_ref, dcr_ref, dcc_ref, dka, dva):
        dka[...] = jnp.zeros_like(dka)
        dva[...] = jnp.zeros_like(dva)
        dcr_ref[...] = jnp.zeros_like(dcr_ref)
        row = lax.broadcasted_iota(jnp.int32, (BQ, BK), 0)
        col = lax.broadcasted_iota(jnp.int32, (BQ, BK), 1)

        def qstep(i, carry):
            r = pl.multiple_of(i * BQ, BQ)
            qs = [q_ref[pl.ds(r, BQ), hl] * 0.125 for hl in heads]
            dos = [do_ref[pl.ds(r, BQ), hl] for hl in heads]
            deltas = [jnp.sum(dos[g].astype(F32) * o_ref[pl.ds(r, BQ), hl].astype(F32), axis=-1, keepdims=True)
                      for g, hl in enumerate(heads)]
            cqs = [cc_ref[g, pl.ds(r, BQ), :] for g in range(2)]
            lses = [lse_ref[g, pl.ds(r, BQ), :] for g in range(2)]

            def kstep(j, st):
                c = pl.multiple_of(j * BK, BK)
                new = []
                for g, hl in enumerate(heads):
                    dq, rs = st[g]
                    kj = k_ref[pl.ds(c, BK), hl]
                    s = _fox_logits(qs[g], kj, cqs[g], cr_ref[g, :, pl.ds(c, BK)], r, c, row, col)
                    p = jnp.exp(s - lses[g])
                    dp = lax.dot_general(dos[g], v_ref[pl.ds(c, BK), hl], (((1,), (1,)), ((), ())),
                                         preferred_element_type=F32)
                    ds = p * (dp - deltas[g])
                    dcr_ref[g, :, pl.ds(c, BK)] -= jnp.sum(ds, axis=0, keepdims=True)
                    dsb = ds.astype(BF16)
                    dka[pl.ds(c, BK), hl] += lax.dot_general(dsb, qs[g], TN, preferred_element_type=F32)
                    dva[pl.ds(c, BK), hl] += lax.dot_general(p.astype(BF16), dos[g], TN,
                                                            preferred_element_type=F32)
                    new.append((dq + jnp.dot(dsb, kj, preferred_element_type=F32),
                                rs + jnp.sum(ds, axis=-1, keepdims=True)))
                return tuple(new)

            init = (jnp.zeros((BQ, HEAD_DIM), F32), jnp.zeros((BQ, 1), F32))
            st = lax.fori_loop(0, (r + BQ + BK - 1) // BK, kstep, (init, init))
            for g, hl in enumerate(heads):
                dq_ref[pl.ds(r, BQ), hl] = (st[g][0] * 0.125).astype(dq_ref.dtype)
                dcc_ref[g, pl.ds(r, BQ), :] = st[g][1]
            return carry

        lax.fori_loop(0, nq, qstep, 0)
        dk_ref[...] = dka[...].astype(dk_ref.dtype)
        dv_ref[...] = dva[...].astype(dv_ref.dtype)

    blk = pl.BlockSpec((S, LANE), lambda i: (0, i))
    ccs = pl.BlockSpec((2, S, 1), lambda i: (i, 0, 0))
    crs = pl.BlockSpec((2, 1, S), lambda i: (i, 0, 0))
    return pl.pallas_call(
        body, grid=(npair,), in_specs=[blk, blk, blk, ccs, crs, blk, blk, ccs],
        out_specs=[blk, blk, blk, crs, ccs],
        out_shape=[SDS((S, npair * LANE), BF16)] * 3 + [SDS((2 * npair, 1, S), F32), SDS((2 * npair, S, 1), F32)],
        scratch_shapes=[pltpu.VMEM((S, LANE), F32), pltpu.VMEM((S, LANE), F32)],
        compiler_params=_cparams(("parallel",)), name=name)(q, k, v, cc, cr, o, do, lse)


def _foxT_logits(kj, qTg, cq, ck, r, c, rowi, coli):
    s = jnp.dot(kj, qTg, preferred_element_type=F32)
    s = s + cq - ck
    return jnp.where(c + rowi <= r + coli, s, NEG_INF)


def _foxT_fwd(qT, k_h, vT, ck, cq, name):
    S = qT.shape[1]
    npair = qT.shape[0] // LANE
    BQ, BK = min(FOX_BQ, S), min(FOX_BK, S)
    nq = S // BQ
    heads = [slice(g * HEAD_DIM, (g + 1) * HEAD_DIM) for g in range(2)]

    def body(qT_ref, k_ref, vT_ref, ck_ref, cq_ref, oT_ref, lse_ref):
        rowi = lax.broadcasted_iota(jnp.int32, (BK, BQ), 0)
        coli = lax.broadcasted_iota(jnp.int32, (BK, BQ), 1)

        def qstep(i, carry):
            r = pl.multiple_of(i * BQ, BQ)
            qs = [qT_ref[hs, pl.ds(r, BQ)] * 0.125 for hs in heads]
            cqs = [cq_ref[g, :, pl.ds(r, BQ)] for g in range(2)]

            def kstep(j, st):
                c = pl.multiple_of(j * BK, BK)
                new = []
                for g, hs in enumerate(heads):
                    m, l, acc = st[g]
                    s = _foxT_logits(k_ref[g, pl.ds(c, BK), :], qs[g], cqs[g], ck_ref[g, pl.ds(c, BK), :],
                                     r, c, rowi, coli)
                    mn = jnp.maximum(m, jnp.max(s, axis=0, keepdims=True))
                    al = jnp.exp(m - mn)
                    e = jnp.exp(s - mn)
                    l = al * l + jnp.sum(e, axis=0, keepdims=True)
                    acc = al * acc + jnp.dot(vT_ref[hs, pl.ds(c, BK)], e.astype(BF16), preferred_element_type=F32)
                    new.append((mn, l, acc))
                return tuple(new)

            init = (jnp.full((1, BQ), NEG_INF, F32), jnp.zeros((1, BQ), F32), jnp.zeros((HEAD_DIM, BQ), F32))
            st = lax.fori_loop(0, (r + BQ + BK - 1) // BK, kstep, (init, init))
            for g, hs in enumerate(heads):
                m, l, acc = st[g]
                oT_ref[hs, pl.ds(r, BQ)] = (acc * (1.0 / l)).astype(oT_ref.dtype)
                lse_ref[g, :, pl.ds(r, BQ)] = m + jnp.log(l)
            return carry

        lax.fori_loop(0, nq, qstep, 0)

    fT = pl.BlockSpec((LANE, S), lambda i: (i, 0))
    hm = pl.BlockSpec((2, S, HEAD_DIM), lambda i: (i, 0, 0))
    col = pl.BlockSpec((2, S, 1), lambda i: (i, 0, 0))
    rw = pl.BlockSpec((2, 1, S), lambda i: (i, 0, 0))
    return pl.pallas_call(
        body, grid=(npair,), in_specs=[fT, hm, fT, col, rw], out_specs=[fT, rw],
        out_shape=[SDS((npair * LANE, S), BF16), SDS((2 * npair, 1, S), F32)],
        compiler_params=_cparams(("parallel",)), name=name)(qT, k_h, vT, ck, cq)


def _foxT_bwd(qT, q_aug, k_h, kT, v_h, ck, cq, oT, doT, do_h, lse, name):
    S = qT.shape[1]
    npair = qT.shape[0] // LANE
    BQ, BK = min(FOX_BQ, S), min(FOX_BK, S)
    nq = S // BQ
    heads = [slice(g * HEAD_DIM, (g + 1) * HEAD_DIM) for g in range(2)]

    def body(qT_ref, qa_ref, k_ref, kT_ref, v_ref, ck_ref, cq_ref, oT_ref, doT_ref, do_ref, lse_ref,
             dqT_ref, dk_ref, dv_ref, dck_ref, dcq_ref, dka, dva):
        dka[...] = jnp.zeros_like(dka)
        dva[...] = jnp.zeros_like(dva)
        rowi = lax.broadcasted_iota(jnp.int32, (BK, BQ), 0)
        coli = lax.broadcasted_iota(jnp.int32, (BK, BQ), 1)

        def qstep(i, carry):
            r = pl.multiple_of(i * BQ, BQ)
            qs = [qT_ref[hs, pl.ds(r, BQ)] * 0.125 for hs in heads]
            dos = [doT_ref[hs, pl.ds(r, BQ)] for hs in heads]
            deltas = [jnp.sum(dos[g].astype(F32) * oT_ref[hs, pl.ds(r, BQ)].astype(F32), axis=0, keepdims=True)
                      for g, hs in enumerate(heads)]
            cqs = [cq_ref[g, :, pl.ds(r, BQ)] for g in range(2)]
            lses = [lse_ref[g, :, pl.ds(r, BQ)] for g in range(2)]

            def kstep(j, st):
                c = pl.multiple_of(j * BK, BK)
                new = []
                for g, hs in enumerate(heads):
                    dq, rs = st[g]
                    s = _foxT_logits(k_ref[g, pl.ds(c, BK), :], qs[g], cqs[g], ck_ref[g, pl.ds(c, BK), :],
                                     r, c, rowi, coli)
                    p = jnp.exp(s - lses[g])
                    dp = jnp.dot(v_ref[g, pl.ds(c, BK), :], dos[g], preferred_element_type=F32)
                    ds = p * (dp - deltas[g])
                    dsb = ds.astype(BF16)
                    dka[g, pl.ds(c, BK), :] += jnp.dot(dsb, qa_ref[g, pl.ds(r, BQ), :], preferred_element_type=F32)
                    dva[g, pl.ds(c, BK), :] += jnp.dot(p.astype(BF16), do_ref[g, pl.ds(r, BQ), :],
                                                      preferred_element_type=F32)
                    new.append((dq + jnp.dot(kT_ref[hs, pl.ds(c, BK)], dsb, preferred_element_type=F32),
                                rs + jnp.sum(dsb.astype(F32), axis=0, keepdims=True)))
                return tuple(new)

            init = (jnp.zeros((HEAD_DIM, BQ), F32), jnp.zeros((1, BQ), F32))
            st = lax.fori_loop(0, (r + BQ + BK - 1) // BK, kstep, (init, init))
            for g, hs in enumerate(heads):
                dqT_ref[hs, pl.ds(r, BQ)] = (st[g][0] * 0.125).astype(dqT_ref.dtype)
                dcq_ref[g, :, pl.ds(r, BQ)] = st[g][1]
            return carry

        lax.fori_loop(0, nq, qstep, 0)
        dk_ref[...] = dka[:, :, 0:HEAD_DIM].astype(dk_ref.dtype)
        dck_ref[...] = -dka[:, :, HEAD_DIM:HEAD_DIM + 1]
        dv_ref[...] = dva[...].astype(dv_ref.dtype)

    fT = pl.BlockSpec((LANE, S), lambda i: (i, 0))
    hm = pl.BlockSpec((2, S, HEAD_DIM), lambda i: (i, 0, 0))
    hma = pl.BlockSpec((2, S, LANE), lambda i: (i, 0, 0))
    col = pl.BlockSpec((2, S, 1), lambda i: (i, 0, 0))
    rw = pl.BlockSpec((2, 1, S), lambda i: (i, 0, 0))
    nh = 2 * npair
    return pl.pallas_call(
        body, grid=(npair,), in_specs=[fT, hma, hm, fT, hm, col, rw, fT, fT, hm, rw],
        out_specs=[fT, hm, hm, col, rw],
        out_shape=[SDS((npair * LANE, S), BF16), SDS((nh, S, HEAD_DIM), BF16), SDS((nh, S, HEAD_DIM), BF16),
                   SDS((nh, S, 1), F32), SDS((nh, 1, S), F32)],
        scratch_shapes=[pltpu.VMEM((2, S, LANE), F32), pltpu.VMEM((2, S, HEAD_DIM), F32)],
        compiler_params=_cparams(("parallel",)), name=name)(qT, q_aug, k_h, kT, v_h, ck, cq, oT, doT, do_h, lse)


def _split3(x):
    hi = x.astype(BF16)
    r1 = x - hi.astype(F32)
    mid = r1.astype(BF16)
    lo = (r1 - mid.astype(F32)).astype(BF16)
    return hi, mid, lo


def _tri_dot(tri, x):
    hi, mid, lo = _split3(x)
    return (jnp.dot(tri, hi, preferred_element_type=F32) + jnp.dot(tri, mid, preferred_element_type=F32)
            + jnp.dot(tri, lo, preferred_element_type=F32))


def _fox_cum(gf, bfo, name):
    S = gf.shape[0]
    nb = S // LANE
    fcol = (GF_COLS - LANE) // LANE

    def body(f_ref, b_ref, cum_ref):
        row = lax.broadcasted_iota(jnp.int32, (LANE, LANE), 0)
        col = lax.broadcasted_iota(jnp.int32, (LANE, LANE), 1)
        tri = jnp.where(row >= col, 1.0, 0.0).astype(BF16)
        carry = jnp.zeros((1, LANE), F32)
        for t in range(nb):
            xl = f_ref[t * LANE:(t + 1) * LANE, :] + b_ref[...]
            lf = jnp.minimum(xl, 0.0) - jnp.log(1.0 + jnp.exp(-jnp.abs(xl)))
            cblk = _tri_dot(tri, lf) + carry
            cum_ref[t * LANE:(t + 1) * LANE, :] = cblk
            carry = cblk[LANE - 1:LANE, :]

    return pl.pallas_call(
        body, grid=(1,), in_specs=[pl.BlockSpec((S, LANE), lambda i: (0, fcol)), _vec(LANE)],
        out_specs=pl.BlockSpec((S, LANE), lambda i: (0, 0)), out_shape=SDS((S, LANE), F32),
        compiler_params=_cparams(("arbitrary",)), name=name)(gf, bfo)


def _fox_cum_bwd(gf, bfo, dcum, name):
    S = gf.shape[0]
    nb = S // LANE
    fcol = (GF_COLS - LANE) // LANE

    def body(f_ref, b_ref, dc_ref, df_ref, db_ref):
        row = lax.broadcasted_iota(jnp.int32, (LANE, LANE), 0)
        col = lax.broadcasted_iota(jnp.int32, (LANE, LANE), 1)
        tri = jnp.where(row <= col, 1.0, 0.0).astype(BF16)
        carry = jnp.zeros((1, LANE), F32)
        tot = jnp.zeros((1, LANE), F32)
        for t in range(nb - 1, -1, -1):
            rows = slice(t * LANE, (t + 1) * LANE)
            dlf = _tri_dot(tri, dc_ref[rows, :]) + carry
            carry = dlf[0:1, :]
            xl = f_ref[rows, :] + b_ref[...]
            dfl = dlf * (1.0 / (1.0 + jnp.exp(xl)))
            df_ref[rows, :] = dfl.astype(df_ref.dtype)
            tot = tot + jnp.sum(dfl, axis=0, keepdims=True)
        db_ref[...] = tot

    return pl.pallas_call(
        body, grid=(1,),
        in_specs=[pl.BlockSpec((S, LANE), lambda i: (0, fcol)), _vec(LANE), pl.BlockSpec((S, LANE), lambda i: (0, 0))],
        out_specs=[pl.BlockSpec((S, LANE), lambda i: (0, 0)), _vec(LANE)],
        out_shape=[SDS((S, LANE), BF16), SDS((1, LANE), F32)],
        compiler_params=_cparams(("arbitrary",)), name=name)(gf, bfo, dcum)


def _rel_onehot(qi, band):
    r = lax.broadcasted_iota(jnp.int32, (N_REL_PAD, band), 0)
    j = lax.broadcasted_iota(jnp.int32, (N_REL_PAD, band), 1)
    idx = jnp.clip(C_PREV * CHUNK + qi - j, -REL_CLIP, REL_CLIP) + REL_CLIP
    return jnp.where(r == idx, 1.0, 0.0).astype(BF16)


def _rel_expand(rel, name):
    band = (C_PREV + 1) * CHUNK

    def body(rel_ref, o_ref):
        hi, mid, lo = _split3(rel_ref[...])

        def row(qi, carry):
            oh = _rel_onehot(qi, band)
            o_ref[qi] = (jnp.dot(hi, oh, preferred_element_type=F32) + jnp.dot(mid, oh, preferred_element_type=F32)
                         + jnp.dot(lo, oh, preferred_element_type=F32))
            return carry

        lax.fori_loop(0, CHUNK, row, 0, unroll=2)

    return pl.pallas_call(
        body, grid=(1,), in_specs=[pl.BlockSpec((N_HEADS, N_REL_PAD), lambda i: (0, 0))],
        out_specs=pl.BlockSpec((CHUNK, N_HEADS, band), lambda i: (0, 0, 0)),
        out_shape=SDS((CHUNK, N_HEADS, band), F32),
        compiler_params=_cparams(("arbitrary",)), name=name)(rel)


def _tri_dot_rhs(x, oh):
    hi, mid, lo = _split3(x)
    return (jnp.dot(hi, oh, preferred_element_type=F32) + jnp.dot(mid, oh, preferred_element_type=F32)
            + jnp.dot(lo, oh, preferred_element_type=F32))


def _rel_reduce(dbias, name):
    band = (C_PREV + 1) * CHUNK
    NT = (((1,), (1,)), ((), ()))

    def body(d_ref, o_ref):
        def row(qi, acc):
            oh = _rel_onehot(qi, band)
            hi, mid, lo = _split3(d_ref[qi])
            return acc + (lax.dot_general(hi, oh, NT, preferred_element_type=F32)
                          + lax.dot_general(mid, oh, NT, preferred_element_type=F32)
                          + lax.dot_general(lo, oh, NT, preferred_element_type=F32))

        o_ref[...] = lax.fori_loop(0, CHUNK, row, jnp.zeros((N_HEADS, N_REL_PAD), F32), unroll=2)

    return pl.pallas_call(
        body, grid=(1,), in_specs=[pl.BlockSpec((CHUNK, N_HEADS, band), lambda i: (0, 0, 0))],
        out_specs=pl.BlockSpec((N_HEADS, N_REL_PAD), lambda i: (0, 0)),
        out_shape=SDS((N_HEADS, N_REL_PAD), F32),
        compiler_params=_cparams(("arbitrary",)), name=name)(dbias)


def _alibi_table():
    qi = np.arange(CHUNK)[:, None]
    j = np.arange((A_PREV + 1) * CHUNK)[None, :]
    dist = np.abs(A_PREV * CHUNK + qi - j).astype(np.float32)
    slopes = np.exp2(-8.0 * np.arange(1, N_HEADS + 1, dtype=np.float32) / N_HEADS).astype(np.float32)
    return jnp.asarray(-slopes[:, None, None] * dist[None])


def _ada_fwd(c_all, w, b, name):
    n = w.shape[2]

    def body(c_ref, w_ref, b_ref, o_ref):
        cv = c_ref[...]
        cond = (cv * _sigmoid(cv)).astype(BF16)
        o_ref[0] = jnp.dot(cond, w_ref[0].astype(BF16), preferred_element_type=F32) + b_ref[0]

    return pl.pallas_call(
        body, grid=(DEPTH,),
        in_specs=[pl.BlockSpec((16, D_MODEL), lambda l: (0, 0)), pl.BlockSpec((1, D_MODEL, n), lambda l: (l, 0, 0)),
                  pl.BlockSpec((1, 1, n), lambda l: (l, 0, 0))],
        out_specs=pl.BlockSpec((1, 16, n), lambda l: (l, 0, 0)), out_shape=SDS((DEPTH, 16, n), F32),
        compiler_params=_cparams(("parallel",)), name=name)(c_all, w, b)


def _ada_bwd(c_t, dmod, name):
    n = dmod.shape[2]
    bn = _blk(n, 512)
    tr = 256

    def body(c_ref, d_ref, o_ref):
        cv = c_ref[...]
        cond = (cv * _sigmoid(cv)).astype(BF16).astype(F32)
        dm = d_ref[0].astype(BF16).astype(F32)
        acc = cond[:, 0:1] * dm[0:1, :]
        for b_ in range(1, 8):
            acc = acc + cond[:, b_:b_ + 1] * dm[b_:b_ + 1, :]
        o_ref[0] = acc

    return pl.pallas_call(
        body, grid=(DEPTH, D_MODEL // tr, n // bn),
        in_specs=[pl.BlockSpec((tr, 8), lambda l, i, j: (i, 0)), pl.BlockSpec((1, 8, bn), lambda l, i, j: (l, 0, j))],
        out_specs=pl.BlockSpec((1, tr, bn), lambda l, i, j: (l, i, j)), out_shape=SDS((DEPTH, D_MODEL, n), F32),
        compiler_params=_cparams(("parallel", "parallel", "parallel")), name=name)(c_t, dmod)


def _adamw(w, m, v, parts, name):
    L, R, C = w.shape
    P = parts.shape[0]
    tr = _blk_rows(R, max(16, (1 << 18) // C))
    nr = R // tr
    c1 = 1.0 - ADAM_B1 ** ADAM_STEP
    c2 = 1.0 - ADAM_B2 ** ADAM_STEP

    def body(w_ref, m_ref, v_ref, p_ref, g_ref, d_ref, nm_ref, nv_ref):
        g = p_ref[0].astype(F32)
        for k in range(1, P):
            g = g + p_ref[k].astype(F32)
        mn = ADAM_B1 * m_ref[0] + (1.0 - ADAM_B1) * g
        vn = ADAM_B2 * v_ref[0] + (1.0 - ADAM_B2) * (g * g)
        m_hat = mn / c1
        v_hat = vn / c2
        g_ref[0] = g
        nm_ref[0] = mn
        nv_ref[0] = vn
        d_ref[0] = -ADAM_LR * (m_hat / (jnp.sqrt(v_hat) + ADAM_EPS) + ADAM_WD * w_ref[0])

    rs = pl.BlockSpec((1, tr, C), lambda l, i: (l, i, 0))
    return pl.pallas_call(
        body, grid=(L, nr), in_specs=[rs, rs, rs, pl.BlockSpec((P, tr, C), lambda l, i: (0, l * nr + i, 0))],
        out_specs=[rs, rs, rs, rs], out_shape=[SDS((L, R, C), F32)] * 4,
        compiler_params=_cparams(("parallel", "parallel")), name=name)(w, m, v, parts)


def _blk_rows(R, cap):
    if R <= cap:
        return R
    best = None
    for t in range(16, cap + 1, 16):
        if R % t == 0:
            best = t
    assert best is not None, (R, cap)
    return best


def _add_cast(a0, a1, b, name):
    Q, R, C = b.shape
    tr = _blk_rows(R, max(16, (1 << 19) // C))

    def body(a0_ref, a1_ref, b_ref, o_ref):
        c = lax.axis_index("c")

        @pl.when(c == 0)
        def _():
            o_ref[...] = (a0_ref[...] + b_ref[...]).astype(o_ref.dtype)

        @pl.when(c == 1)
        def _():
            o_ref[...] = (a1_ref[...] + b_ref[...]).astype(o_ref.dtype)

    bs = pl.BlockSpec((1, tr, C), lambda q, i: (q, i, 0))
    return pl.pallas_call(
        body, grid=(Q, R // tr), in_specs=[bs, bs, bs], out_specs=bs, out_shape=SDS((Q, R, C), BF16),
        compiler_params=_cparams(("parallel", "parallel")), name=name)(a0, a1, b)


def _coords():
    return lax.axis_index("x"), lax.axis_index("y"), lax.axis_index("c")


def _flip(v, bit):
    return 1 - v if bit else v


def _all_gather8(v, name):
    R = v.shape[0]

    def body(v_ref, o_ref, send_sems, recv_sems):
        x, y, c = _coords()
        me = 4 * x + 2 * y + c
        o_ref[me] = v_ref[...]
        copies = []
        for k in range(1, 8):
            peer = (_flip(x, k & 4), _flip(y, k & 2), _flip(c, k & 1))
            cp = pltpu.make_async_remote_copy(
                src_ref=v_ref, dst_ref=o_ref.at[me], send_sem=send_sems.at[k - 1], recv_sem=recv_sems.at[k - 1],
                device_id=peer, device_id_type=MESH)
            cp.start()
            copies.append(cp)
        for cp in copies:
            cp.wait_recv()
        for cp in copies:
            cp.wait_send()

    return pl.pallas_call(
        body, in_specs=[VMEM_SPEC], out_specs=VMEM_SPEC, out_shape=SDS((8, R, LANE), v.dtype),
        scratch_shapes=[pltpu.SemaphoreType.DMA((7,)), pltpu.SemaphoreType.DMA((7,))],
        compiler_params=pltpu.CompilerParams(vmem_limit_bytes=VMEM_LIMIT), name=name)(v)


def _sibling_swap(arrs0, arrs1, name):
    n = len(arrs0)

    def body(*refs):
        in0, in1, out_refs = refs[:n], refs[n:2 * n], refs[2 * n:3 * n]
        send_sems, recv_sems = refs[3 * n:]
        x, y, c = _coords()

        def swap(srcs):
            copies = [pltpu.make_async_remote_copy(
                src_ref=srcs[a], dst_ref=out_refs[a], send_sem=send_sems.at[a], recv_sem=recv_sems.at[a],
                device_id=(x, y, 1 - c), device_id_type=MESH) for a in range(n)]
            for cp in copies:
                cp.start()
            for cp in copies:
                cp.wait_recv()
            for cp in copies:
                cp.wait_send()

        @pl.when(c == 0)
        def _():
            swap(in1)

        @pl.when(c == 1)
        def _():
            swap(in0)

    return pl.pallas_call(
        body, in_specs=[ANY] * (2 * n), out_specs=[ANY] * n,
        out_shape=[SDS(a.shape, a.dtype) for a in arrs0],
        scratch_shapes=[pltpu.SemaphoreType.DMA((n,)), pltpu.SemaphoreType.DMA((n,))],
        name=name)(*arrs0, *arrs1)


def _chip_exchange(arrs, *, reduce, name):
    n = len(arrs)

    def body(*refs):
        in_refs, out_refs = refs[:n], refs[n:2 * n]
        ici_send, ici_recv, d2d_send, d2d_recv, loc_sem = refs[2 * n:]
        x, y, c = _coords()
        p = 2 * x + y
        local, first, fwd = [], [], []
        for a in range(n):
            R = out_refs[a].shape[1] // 2
            half = pl.ds(pl.multiple_of(c * R, 16), R)
            if reduce:
                lc = pltpu.make_async_copy(in_refs[a].at[p], out_refs[a].at[p, half], loc_sem.at[a])
            else:
                lc = pltpu.make_async_copy(in_refs[a], out_refs[a].at[p], loc_sem.at[a])
            lc.start()
            local.append(lc)
            for k in range(1, 4):
                qx, qy = _flip(x, k & 2), _flip(y, k & 1)
                src = in_refs[a].at[2 * qx + qy] if reduce else in_refs[a].at[half]
                cp = pltpu.make_async_remote_copy(
                    src_ref=src, dst_ref=out_refs[a].at[p, half], send_sem=ici_send.at[a, k - 1],
                    recv_sem=ici_recv.at[a, k - 1], device_id=(qx, qy, c), device_id_type=MESH)
                cp.start()
                first.append(cp)
        for a in range(n):
            R = out_refs[a].shape[1] // 2
            half = pl.ds(pl.multiple_of(c * R, 16), R)
            for k in range(0 if reduce else 1, 4):
                qx, qy = _flip(x, k & 2), _flip(y, k & 1)
                slot = out_refs[a].at[2 * qx + qy, half]
                if k == 0:
                    local[a].wait()
                else:
                    first[a * 3 + k - 1].wait_recv()
                cp = pltpu.make_async_remote_copy(
                    src_ref=slot, dst_ref=slot, send_sem=d2d_send.at[a, k], recv_sem=d2d_recv.at[a, k],
                    device_id=(x, y, 1 - c), device_id_type=MESH)
                cp.start()
                fwd.append(cp)
        for cp in fwd:
            cp.wait_recv()
        for cp in first + fwd:
            cp.wait_send()
        if not reduce:
            for lc in local:
                lc.wait()

    if reduce:
        out_shape = [SDS((4, 2 * a.shape[1], a.shape[2]), a.dtype) for a in arrs]
    else:
        out_shape = [SDS((4,) + a.shape, a.dtype) for a in arrs]
    return pl.pallas_call(
        body, in_specs=[ANY] * n, out_specs=[ANY] * n, out_shape=out_shape,
        scratch_shapes=[pltpu.SemaphoreType.DMA((n, 3)), pltpu.SemaphoreType.DMA((n, 3)),
                        pltpu.SemaphoreType.DMA((n, 4)), pltpu.SemaphoreType.DMA((n, 4)),
                        pltpu.SemaphoreType.DMA((n,))],
        name=name)(*arrs)


_IN_SIZES = (512, 128, 128, 512, 512, 512, 8, 512, 512, 512, 3072)
_IN_OFF = tuple(int(v) for v in np.cumsum((0,) + _IN_SIZES))
_IN_Q = N_IN_COLS // 4


def _pack_w_in(w):
    def cols(lo, hi):
        out = []
        while lo < hi:
            q, off = divmod(lo, _IN_Q)
            n = min(hi - lo, _IN_Q - off)
            out.append(w[q, :, off:off + n])
            lo += n
        return out

    fb0, fb1, g0 = _IN_OFF[6], _IN_OFF[7], _IN_OFF[10]
    wqkv = jnp.concatenate(cols(0, fb0) + cols(fb1, g0), axis=1)
    wgf = jnp.concatenate(cols(g0, N_IN_COLS) + cols(fb0, fb1) + [jnp.zeros((w.shape[1], LANE - 8), w.dtype)], axis=1)
    return wqkv, wgf


def _unpack_w_in(dqkv, dgf):
    fb0, fb1, g0 = _IN_OFF[6], _IN_OFF[7], _IN_OFF[10]

    def cols(lo, hi):
        out = []
        while lo < hi:
            if lo < fb0:
                n = min(hi, fb0) - lo
                out.append(dqkv[:, lo:lo + n])
            elif lo < fb1:
                n = min(hi, fb1) - lo
                out.append(dgf[:, 3072 + lo - fb0:3072 + lo - fb0 + n])
            elif lo < g0:
                n = min(hi, g0) - lo
                out.append(dqkv[:, lo - 8:lo - 8 + n])
            else:
                n = hi - lo
                out.append(dgf[:, lo - g0:lo - g0 + n])
            lo += n
        return out

    return jnp.stack([jnp.concatenate(cols(q * _IN_Q, (q + 1) * _IN_Q), axis=1) for q in range(4)])


def _pad_rows(a, rows):
    return jnp.pad(a, ((0, rows - a.shape[0]), (0, 0)))


def _small_pack(parts):
    flat = jnp.concatenate([p.reshape(-1) for p in parts])
    n = flat.shape[0]
    rows = -(-n // LANE)
    rows = -(-rows // 8) * 8
    return jnp.pad(flat, (0, rows * LANE - n)).reshape(rows, LANE)


def _small_unpack(block, shapes):
    flat = block.reshape(-1)
    out, off = [], 0
    for s in shapes:
        n = int(np.prod(s))
        out.append(flat[off:off + n].reshape(s))
        off += n
    return out


def _kv_same(g):
    return 0


def _kv_own(g):
    return g


def _layer_fwd(x, mod, p, l):
    sh_m, sc_m, g_m, sh_f, sc_f, g_f = mod
    nm = "l%d_" % l
    h1 = _norm_mod_fwd(x, p["norm_mix_g"], sc_m, sh_m, nm + "norm_mix_fwd")
    qkv = _mm(h1, p["wqkv"], mode="nn", out_dtype=BF16, name=nm + "proj_qkv")
    gf = _mm(h1, p["wgf"], mode="nn", out_dtype=F32, name=nm + "proj_gf", cap_n=640)
    qkv_t = qkv.T
    o_a_t = _bandT_fwd(qkv_t[0:512], _heads(qkv[:, 512:640], A_KV_HEADS), qkv_t[640:768], p["alibi"],
                       p["sink_tab"], GQ=4, GK=1, P=A_PREV, kvoff=_kv_same, name=nm + "attn_a_fwd")
    cum = _fox_cum(gf, p["b_forget_pad"], nm + "fox_cum")
    cum_t = cum[:, :N_HEADS].T
    cc, cr = cum_t[:, :, None], cum_t[:, None, :]
    o_b_t, lse_b = _foxT_fwd(qkv_t[768:1280], _heads(qkv[:, 1280:1792], N_HEADS), qkv_t[1792:2304], cc, cr,
                             nm + "attn_b_fwd")
    o_c_t = _bandT_fwd(qkv_t[2304:2816], _heads(qkv[:, 2816:3328], N_HEADS), qkv_t[3328:3840], p["rel_tab"],
                       p["no_sink"], GQ=2, GK=2, P=C_PREV, kvoff=_kv_own, name=nm + "attn_c_fwd")
    o = jnp.concatenate([o_a_t, o_b_t, o_c_t], axis=0).T
    y = _mm(o, p["wb"], mode="nn", out_dtype=F32, groups=3, name=nm + "branch")
    merged = _merge_fwd(y, gf, nm + "merge_fwd")
    mix = _mm(merged, p["wout"], mode="nn", out_dtype=F32, name=nm + "out_proj")
    x1 = _resid_fwd(x, mix, g_m, nm + "resid_mix")
    h2 = _norm_mod_fwd(x1, p["norm_ffn_g"], sc_f, sh_f, nm + "norm_ffn_fwd")
    u = _mm(h2, p["wfi"], mode="nn", out_dtype=F32, name=nm + "ffn_in", cap_n=1408)
    a = _swiglu_fwd(u, nm + "swiglu_fwd")
    f = _mm(a, p["wfo"], mode="nn", out_dtype=F32, name=nm + "ffn_out")
    x2 = _resid_fwd(x1, f, g_f, nm + "resid_ffn")
    saved = dict(x=x, h1=h1, qkv=qkv, qkv_t=qkv_t, gf=gf, cc=cc, cr=cr, o_b_t=o_b_t, lse_b=lse_b, o=o, y=y, merged=merged,
                 mix=mix, x1=x1, h2=h2, u=u, a=a, f=f)
    return x2, saved


def _layer_bwd(dx2, mod, p, s, l):
    sh_m, sc_m, g_m, sh_f, sc_f, g_f = mod
    nm = "l%d_" % l
    dg_f, df = _resid_bwd(dx2, s["f"], g_f, nm + "resid_ffn_bwd")
    da = _mm(df, p["wfo"], mode="nt", out_dtype=F32, name=nm + "ffn_out_dx", cap_n=1408)
    d_wfo = _mm(s["a"], df, mode="tn", out_dtype=F32, name=nm + "ffn_out_dw", cap_m=1408, cap_k=512)
    du = _swiglu_bwd(da, s["u"], nm + "swiglu_bwd")
    dh2 = _mm(du, p["wfi"], mode="nt", out_dtype=F32, name=nm + "ffn_in_dx")
    d_wfi = _mm(s["h2"], du, mode="tn", out_dtype=F32, name=nm + "ffn_in_dw", cap_n=1408, cap_k=512,
                col_quarters=True)
    dx1, dsc_f, dsh_f, dgn_f = _norm_mod_bwd(s["x1"], [dh2], dx2, p["norm_ffn_g"], sc_f, nm + "norm_ffn_bwd")
    dg_m, dmix = _resid_bwd(dx1, s["mix"], g_m, nm + "resid_mix_bwd")
    dmerged = _mm(dmix, p["wout"], mode="nt", out_dtype=F32, name=nm + "out_proj_dx")
    d_wout = _mm(s["merged"], dmix, mode="tn", out_dtype=F32, name=nm + "out_proj_dw", cap_k=512)
    dy, dgates = _merge_bwd(dmerged, s["y"], s["gf"], nm + "merge_bwd")
    do = _mm(dy, p["wb"], mode="nt", out_dtype=BF16, groups=3, name=nm + "branch_dx")
    d_wb = _mm(s["o"], dy, mode="tn", out_dtype=F32, groups=3, name=nm + "branch_dw", cap_k=512,
               col_quarters=True)
    qkv, qkv_t = s["qkv"], s["qkv_t"]
    do_t = do.T
    dqa_t, dka_h, dva_h, _, dsink = _bandT_bwd(
        qkv_t[0:512], _heads(qkv[:, 0:512], N_HEADS), _heads(qkv[:, 512:640], A_KV_HEADS), qkv_t[512:640],
        _heads(qkv[:, 640:768], A_KV_HEADS), do_t[0:512], _heads(do[:, 0:512], N_HEADS), p["alibi"], p["sink_tab"],
        GQ=4, GK=1, P=A_PREV, kvoff=_kv_same, name=nm + "attn_a_bwd")
    qb_h = _heads(qkv[:, 768:1280], N_HEADS)
    q_aug = jnp.concatenate([qb_h * 0.125, jnp.ones(qb_h.shape[:2] + (1,), BF16),
                             jnp.zeros(qb_h.shape[:2] + (LANE - HEAD_DIM - 1,), BF16)], axis=2)
    dqb_t, dkb_h, dvb_h, dck, dcq = _foxT_bwd(
        qkv_t[768:1280], q_aug, _heads(qkv[:, 1280:1792], N_HEADS), qkv_t[1280:1792],
        _heads(qkv[:, 1792:2304], N_HEADS), s["cc"], s["cr"], s["o_b_t"], do_t[512:1024],
        _heads(do[:, 512:1024], N_HEADS), s["lse_b"], nm + "attn_b_bwd")
    dcum = jnp.pad((dck[:, :, 0] + dcq[:, 0, :]).T, ((0, 0), (0, LANE - N_HEADS)))
    dfb, db_forget = _fox_cum_bwd(s["gf"], p["b_forget_pad"], dcum, nm + "fox_cum_bwd")
    dqc_t, dkc_h, dvc_h, dbias_c, _ = _bandT_bwd(
        qkv_t[2304:2816], _heads(qkv[:, 2304:2816], N_HEADS), _heads(qkv[:, 2816:3328], N_HEADS), qkv_t[2816:3328],
        _heads(qkv[:, 3328:3840], N_HEADS), do_t[1024:1536], _heads(do[:, 1024:1536], N_HEADS), p["rel_tab"],
        p["no_sink"], GQ=2, GK=2, P=C_PREV, kvoff=_kv_own, name=nm + "attn_c_bwd")
    d_rel = _rel_reduce(jnp.transpose(_unpair_table(dbias_c), (1, 0, 2)), nm + "rel_reduce")[:, :N_REL]
    dqkv = jnp.concatenate([dqa_t.T, _unheads(dka_h), _unheads(dva_h), dqb_t.T, _unheads(dkb_h), _unheads(dvb_h),
                            dqc_t.T, _unheads(dkc_h), _unheads(dvc_h)], axis=1)
    dgf = jnp.concatenate([dgates, dfb], axis=1)
    dh1a = _mm(dqkv, p["wqkv"], mode="nt", out_dtype=F32, name=nm + "proj_qkv_dx", cap_k=1024)
    dh1b = _mm(dgf, p["wgf"], mode="nt", out_dtype=F32, name=nm + "proj_gf_dx", cap_k=640)
    d_wqkv = _mm(s["h1"], dqkv, mode="tn", out_dtype=F32, name=nm + "proj_qkv_dw", cap_k=512)
    d_wgf = _mm(s["h1"], dgf, mode="tn", out_dtype=F32, name=nm + "proj_gf_dw", cap_n=640, cap_k=512)
    dx, dsc_m, dsh_m, dgn_m = _norm_mod_bwd(s["x"], [dh1a, dh1b], dx1, p["norm_mix_g"], sc_m, nm + "norm_mix_bwd")
    d_mod = jnp.concatenate([dsh_m, dsc_m, dg_m, dsh_f, dsc_f, dg_f], axis=1)[0]
    grads = dict(w_in=_unpack_w_in(d_wqkv, d_wgf), w_branch=d_wb, w_out=d_wout.reshape(4, -1, D_MODEL),
                 w_ffn_in=d_wfi, w_ffn_out=d_wfo.reshape(4, -1, D_MODEL),
                 norm_mix_g=dgn_m[0], norm_ffn_g=dgn_f[0], b_forget=db_forget[0, :N_HEADS],
                 sinks=dsink[:, 0, 0], rel_bias=d_rel, d_mod=d_mod)
    return dx, grads


def kernel(x, c, norm_mix_g, norm_ffn_g, w_ada, b_ada, w_in, b_forget, sinks, rel_bias, w_branch, w_out, w_ffn_in, w_ffn_out, final_norm_g, loss_target, m_norm_mix_g, m_norm_ffn_g, m_w_ada, m_b_ada, m_w_in, m_b_forget, m_sinks, m_rel_bias, m_w_branch, m_w_out, m_w_ffn_in, m_w_ffn_out, m_final_norm_g, v_norm_mix_g, v_norm_ffn_g, v_w_ada, v_b_ada, v_w_in, v_b_forget, v_sinks, v_rel_bias, v_w_branch, v_w_out, v_w_ffn_in, v_w_ffn_out, v_final_norm_g):
    xi, yi, ci = _coords()
    chip = 2 * xi + yi
    dev = 2 * chip + ci
    xs = x[0]
    S = xs.shape[0]
    n_ada = w_ada.shape[2]

    big_names = ("w_in", "w_branch", "w_out", "w_ffn_in", "w_ffn_out")
    big_w = dict(w_in=w_in, w_branch=w_branch, w_out=w_out, w_ffn_in=w_ffn_in, w_ffn_out=w_ffn_out)
    big_m = dict(w_in=m_w_in, w_branch=m_w_branch, w_out=m_w_out, w_ffn_in=m_w_ffn_in, w_ffn_out=m_w_ffn_out)
    big_v = dict(w_in=v_w_in, w_branch=v_w_branch, w_out=v_w_out, w_ffn_in=v_w_ffn_in, w_ffn_out=v_w_ffn_out)
    flat2 = lambda a: a.reshape(-1, a.shape[-1])
    shards = [flat2(big_w[n]).astype(BF16) for n in big_names]
    gw_in, gw_branch, gw_out, gw_ffn_in, gw_ffn_out = _chip_exchange(shards, reduce=False, name="weights_all_gather")
    cin = w_in.shape[2]
    cbr = w_branch.shape[3]
    rout = w_out.shape[1]
    cfi = w_ffn_in.shape[2]
    rfo = w_ffn_out.shape[1]
    w_branch_full = gw_branch.reshape(4, DEPTH, 3, BRANCH_W, cbr).transpose(1, 2, 3, 0, 4).reshape(
        DEPTH, 3 * BRANCH_W, 4 * cbr)
    w_out_full = gw_out.reshape(4, DEPTH, rout, D_MODEL).transpose(1, 0, 2, 3).reshape(DEPTH, 4 * rout, D_MODEL)
    w_ffn_in_full = gw_ffn_in.reshape(4, DEPTH, D_MODEL, cfi).transpose(1, 2, 0, 3).reshape(DEPTH, D_MODEL, 4 * cfi)
    w_ffn_out_full = gw_ffn_out.reshape(4, DEPTH, rfo, D_MODEL).transpose(1, 0, 2, 3).reshape(DEPTH, 4 * rfo, D_MODEL)

    c_all = _all_gather8(c.reshape(8, LANE), "gather_c").reshape(8, D_MODEL)
    b_sh = lax.dynamic_slice_in_dim(b_ada, chip * n_ada, n_ada, axis=1)[:, None, :]
    mod_sh = _ada_fwd(_pad_rows(c_all, 16), w_ada, b_sh, "ada_fwd")[:, :8, :]
    mod_all = _all_gather8(mod_sh.reshape(-1, LANE), "gather_mod").reshape(8, DEPTH, 8, n_ada)
    mod_mine = lax.dynamic_index_in_dim(mod_all[0::2], dev, axis=2, keepdims=False)
    mod = mod_mine.transpose(1, 0, 2).reshape(DEPTH, 6, D_MODEL)

    alibi = _pair_table(_alibi_table())
    no_sink = jnp.full((N_HEADS, 8, LANE), NEG_INF, F32)
    params = []
    for l in range(DEPTH):
        wqkv, wgf = _pack_w_in(gw_in[:, l * D_MODEL:(l + 1) * D_MODEL, :])
        rel_tab = _rel_expand(jnp.pad(rel_bias[l], ((0, 0), (0, N_REL_PAD - N_REL))), "l%d_rel_expand" % l)
        params.append(dict(
            wqkv=wqkv, wgf=wgf, wb=w_branch_full[l], wout=w_out_full[l], wfi=w_ffn_in_full[l], wfo=w_ffn_out_full[l],
            norm_mix_g=norm_mix_g[l][None], norm_ffn_g=norm_ffn_g[l][None],
            b_forget_pad=jnp.pad(b_forget[l], (0, LANE - N_HEADS))[None],
            sink_tab=jnp.broadcast_to(sinks[l][:, None, None], (N_HEADS, 8, LANE)),
            no_sink=no_sink, alibi=alibi, rel_tab=_pair_table(jnp.transpose(rel_tab, (1, 0, 2)))))
    mods = [[mod[l, k][None] for k in range(6)] for l in range(DEPTH)]
    h = xs
    saved = []
    for l in range(DEPTH):
        h, s = _layer_fwd(h, mods[l], params[l], l)
        saved.append(s)
    loss_dev, dh, d_final = _final_loss(h, final_norm_g[None], loss_target[0], "final_loss")
    grads = [None] * DEPTH
    for l in reversed(range(DEPTH)):
        dh, grads[l] = _layer_bwd(dh, mods[l], params[l], saved[l], l)
    grad_x = dh[None]
    loss = lax.psum(loss_dev[0, 0], ("x", "y", "c"))

    g0 = [grads[0][n] for n in big_names]
    g1 = [grads[1][n] for n in big_names]
    theirs = _sibling_swap(g0, g1, "grads_sibling_swap")
    chip_sum = [_add_cast(a0, a1, b, "grads_chip_sum_%s" % n) for n, a0, a1, b in zip(big_names, g0, g1, theirs)]
    parts = _chip_exchange(chip_sum, reduce=True, name="grads_reduce_scatter")
    big_out = {}
    for n, pt in zip(big_names, parts):
        shp = big_w[n].shape
        as3 = lambda a: a.reshape(shp[0], -1, shp[-1])
        res = _adamw(as3(big_w[n]), as3(big_m[n]), as3(big_v[n]), pt, "adamw_" + n)
        big_out[n] = [r.reshape(shp) for r in res]

    small_names = ("norm_mix_g", "norm_ffn_g", "b_ada", "b_forget", "sinks", "rel_bias", "final_norm_g")
    small_w = dict(norm_mix_g=norm_mix_g, norm_ffn_g=norm_ffn_g, b_ada=b_ada, b_forget=b_forget, sinks=sinks,
                   rel_bias=rel_bias, final_norm_g=final_norm_g)
    small_m = dict(norm_mix_g=m_norm_mix_g, norm_ffn_g=m_norm_ffn_g, b_ada=m_b_ada, b_forget=m_b_forget,
                   sinks=m_sinks, rel_bias=m_rel_bias, final_norm_g=m_final_norm_g)
    small_v = dict(norm_mix_g=v_norm_mix_g, norm_ffn_g=v_norm_ffn_g, b_ada=v_b_ada, b_forget=v_b_forget,
                   sinks=v_sinks, rel_bias=v_rel_bias, final_norm_g=v_final_norm_g)
    small_g = dict(
        norm_mix_g=jnp.stack([grads[l]["norm_mix_g"] for l in range(DEPTH)]),
        norm_ffn_g=jnp.stack([grads[l]["norm_ffn_g"] for l in range(DEPTH)]),
        b_ada=jnp.stack([grads[l]["d_mod"] for l in range(DEPTH)]),
        b_forget=jnp.stack([grads[l]["b_forget"] for l in range(DEPTH)]),
        sinks=jnp.stack([grads[l]["sinks"] for l in range(DEPTH)]),
        rel_bias=jnp.stack([grads[l]["rel_bias"] for l in range(DEPTH)]),
        final_norm_g=d_final[0])
    shapes = [small_w[n].shape for n in small_names]
    g_all = _all_gather8(_small_pack([small_g[n] for n in small_names]), "gather_small_grads")
    res = _adamw(_small_pack([small_w[n] for n in small_names])[None], _small_pack([small_m[n] for n in small_names])[None],
                 _small_pack([small_v[n] for n in small_names])[None], g_all, "adamw_small")
    small_out = {n: [] for n in small_names}
    for r in res:
        for n, a in zip(small_names, _small_unpack(r[0], shapes)):
            small_out[n].append(a)
    off_b = sum(int(np.prod(s)) for s in shapes[:2])
    n_mod = DEPTH * 6 * D_MODEL
    dmod_all = g_all.reshape(8, -1)[:, off_b:off_b + n_mod].reshape(8, DEPTH, 6 * D_MODEL)
    dmod_sh = lax.dynamic_slice_in_dim(dmod_all, chip * n_ada, n_ada, axis=2).transpose(1, 0, 2)
    g_ada = _ada_bwd(c_all.T, dmod_sh, "ada_bwd")
    ada_out = _adamw(w_ada, m_w_ada, v_w_ada, flat2(g_ada)[None], "adamw_w_ada")

    order = ("norm_mix_g", "norm_ffn_g", "w_ada", "b_ada", "w_in", "b_forget", "sinks", "rel_bias", "w_branch",
             "w_out", "w_ffn_in", "w_ffn_out", "final_norm_g")

    def pick(n, k):
        if n == "w_ada":
            return ada_out[k]
        if n in big_out:
            return big_out[n][k]
        return small_out[n][k]

    outs = [loss, grad_x]
    for k in range(4):
        outs += [pick(n, k) for n in order]
    return tuple(outs)
```

```python
import functools

import numpy as np
import jax
import jax.numpy as jnp
from jax import lax
from jax.experimental import pallas as pl
from jax.experimental.pallas import tpu as pltpu

F32 = jnp.float32
BF16 = jnp.bfloat16
SDS = jax.ShapeDtypeStruct

D_MODEL = 1024
DEPTH = 2
CHUNK = 64
HEAD_DIM = 64
EPS = 1e-6
NEG_INF = -1e30
N_HEADS = 8
A_KV_HEADS = 2
A_PREV = 2
C_PREV = 8
REL_CLIP = 128
N_REL = 2 * REL_CLIP + 1
N_REL_PAD = 384
BRANCH_W = 512
FFN_H = 2816
FOX_BQ = 256
FOX_BK = 512
BAND_UNROLL_FWD = 4
BAND_UNROLL_BWD = 2
QKV_COLS = 3840
GF_COLS = 3200
N_IN_COLS = 6920
LANE = 128
VMEM_LIMIT = 48 * 1024 * 1024

ADAM_LR = 0.001
ADAM_B1 = 0.9
ADAM_B2 = 0.999
ADAM_EPS = 1e-08
ADAM_WD = 0.01
ADAM_STEP = 10

MESH = pl.DeviceIdType.MESH
ANY = pl.BlockSpec(memory_space=pl.ANY)
VMEM_SPEC = pl.BlockSpec(memory_space=pltpu.VMEM)


def _cparams(sem=None):
    return pltpu.CompilerParams(dimension_semantics=sem, vmem_limit_bytes=VMEM_LIMIT)


def _blk(n, cap):
    if n <= cap:
        return n
    best = None
    for m in range(LANE, cap + 1, LANE):
        if n % m == 0:
            best = m
    assert best is not None, (n, cap)
    return best


def _sigmoid(x):
    return 1.0 / (1.0 + jnp.exp(-x))


def _mm(a, b, *, mode, out_dtype, name, groups=1, cap_m=2048, cap_n=1024, cap_k=1408, col_quarters=False):
    G = groups
    assert not col_quarters or mode == "tn"
    if mode == "nn":
        M, K, N = a.shape[0], a.shape[1] // G, b.shape[1]
        assert b.shape[0] == G * K
    elif mode == "nt":
        M, K, N = a.shape[0], a.shape[1] // G, b.shape[0] // G
        assert b.shape[1] == K
    else:
        K, M, N = a.shape[0], a.shape[1] // G, b.shape[1] // G
        assert b.shape[0] == K
    bm, bn, bk = _blk(M, cap_m), _blk(N // 4 if col_quarters else N, cap_n), _blk(K, cap_k)
    nm, nn, nk = M // bm, N // bn, K // bk
    if mode == "nn":
        a_spec = pl.BlockSpec((bm, bk), lambda g, i, j, k: (i, g * nk + k))
        b_spec = pl.BlockSpec((bk, bn), lambda g, i, j, k: (g * nk + k, j))
        o_spec = pl.BlockSpec((bm, bn), lambda g, i, j, k: (i, g * nn + j))
        dims = (((1,), (0,)), ((), ()))
        out_shape = (M, G * N)
    elif mode == "nt":
        a_spec = pl.BlockSpec((bm, bk), lambda g, i, j, k: (i, g * nk + k))
        b_spec = pl.BlockSpec((bn, bk), lambda g, i, j, k: (g * nn + j, k))
        o_spec = pl.BlockSpec((bm, bn), lambda g, i, j, k: (i, g * nn + j))
        dims = (((1,), (1,)), ((), ()))
        out_shape = (M, G * N)
    else:
        a_spec = pl.BlockSpec((bk, bm), lambda g, i, j, k: (k, g * nm + i))
        b_spec = pl.BlockSpec((bk, bn), lambda g, i, j, k: (k, g * nn + j))
        dims = (((0,), (0,)), ((), ()))
        if col_quarters:
            nq = nn // 4
            o_spec = pl.BlockSpec((1, bm, bn), lambda g, i, j, k: (j // nq, g * nm + i, j % nq))
            out_shape = (4, G * M, N // 4)
        else:
            o_spec = pl.BlockSpec((bm, bn), lambda g, i, j, k: (g * nm + i, j))
            out_shape = (G * M, N)

    def product(a_ref, b_ref):
        return lax.dot_general(a_ref[...].astype(BF16), b_ref[...].astype(BF16), dims, preferred_element_type=F32)

    def body_one(a_ref, b_ref, o_ref):
        o_ref[...] = product(a_ref, b_ref).astype(o_ref.dtype).reshape(o_ref.shape)

    def body_acc(a_ref, b_ref, o_ref, acc_ref):
        k = pl.program_id(3)

        @pl.when(k == 0)
        def _():
            acc_ref[...] = jnp.zeros_like(acc_ref)

        acc_ref[...] += product(a_ref, b_ref)

        @pl.when(k == nk - 1)
        def _():
            o_ref[...] = acc_ref[...].astype(o_ref.dtype).reshape(o_ref.shape)

    return pl.pallas_call(
        body_one if nk == 1 else body_acc, grid=(G, nm, nn, nk), in_specs=[a_spec, b_spec], out_specs=o_spec,
        out_shape=SDS(out_shape, out_dtype), scratch_shapes=[] if nk == 1 else [pltpu.VMEM((bm, bn), F32)],
        compiler_params=_cparams(("parallel", "parallel", "parallel", "arbitrary")), name=name,
    )(a, b)


def _rows(tm, n, col=0):
    return pl.BlockSpec((tm, n), lambda i: (i, col))


def _vec(n):
    return pl.BlockSpec((1, n), lambda i: (0, 0))


def _tm(S):
    return min(S, 256)


def _norm_mod_fwd(x, g, sc, sh, name):
    S, Dm = x.shape
    tm = _tm(S)

    def body(x_ref, g_ref, sc_ref, sh_ref, h_ref):
        xv = x_ref[...]
        r = lax.rsqrt(jnp.mean(xv * xv, axis=-1, keepdims=True) + EPS)
        h_ref[...] = ((xv * r) * g_ref[...] * (1.0 + sc_ref[...]) + sh_ref[...]).astype(h_ref.dtype)

    return pl.pallas_call(
        body, grid=(S // tm,), in_specs=[_rows(tm, Dm), _vec(Dm), _vec(Dm), _vec(Dm)],
        out_specs=_rows(tm, Dm), out_shape=SDS((S, Dm), BF16),
        compiler_params=_cparams(("parallel",)), name=name)(x, g, sc, sh)


def _norm_mod_bwd(x, dh_list, dres, g, sc, name):
    S, Dm = x.shape
    tm = _tm(S)
    nh = len(dh_list)

    def body(*refs):
        x_ref = refs[0]
        dh_refs = refs[1:1 + nh]
        dres_ref, g_ref, sc_ref, dx_ref, dsc_ref, dsh_ref, dg_ref = refs[1 + nh:]
        i = pl.program_id(0)

        @pl.when(i == 0)
        def _():
            dsc_ref[...] = jnp.zeros_like(dsc_ref)
            dsh_ref[...] = jnp.zeros_like(dsh_ref)
            dg_ref[...] = jnp.zeros_like(dg_ref)

        xv = x_ref[...]
        dh = dh_refs[0][...]
        for r_ in dh_refs[1:]:
            dh = dh + r_[...]
        gv = g_ref[...]
        r = lax.rsqrt(jnp.mean(xv * xv, axis=-1, keepdims=True) + EPS)
        xn = xv * r
        xg = xn * gv
        dsh_ref[...] += jnp.sum(dh, axis=0, keepdims=True)
        dsc_ref[...] += jnp.sum(dh * xg, axis=0, keepdims=True)
        dxg = dh * (1.0 + sc_ref[...])
        dg_ref[...] += jnp.sum(dxg * xn, axis=0, keepdims=True)
        dxn = dxg * gv
        dx_ref[...] = dres_ref[...] + r * (dxn - xn * jnp.mean(dxn * xn, axis=-1, keepdims=True))

    return pl.pallas_call(
        body, grid=(S // tm,),
        in_specs=[_rows(tm, Dm)] * (2 + nh) + [_vec(Dm), _vec(Dm)],
        out_specs=[_rows(tm, Dm), _vec(Dm), _vec(Dm), _vec(Dm)],
        out_shape=[SDS((S, Dm), F32), SDS((1, Dm), F32), SDS((1, Dm), F32), SDS((1, Dm), F32)],
        compiler_params=_cparams(("arbitrary",)), name=name)(x, *dh_list, dres, g, sc)


def _resid_fwd(x, val, g, name):
    S, Dm = x.shape
    tm = _tm(S)

    def body(x_ref, v_ref, g_ref, o_ref):
        o_ref[...] = x_ref[...] + g_ref[...] * v_ref[...]

    return pl.pallas_call(
        body, grid=(S // tm,), in_specs=[_rows(tm, Dm), _rows(tm, Dm), _vec(Dm)],
        out_specs=_rows(tm, Dm), out_shape=SDS((S, Dm), F32),
        compiler_params=_cparams(("parallel",)), name=name)(x, val, g)


def _resid_bwd(dx, val, g, name):
    S, Dm = dx.shape
    tm = _tm(S)

    def body(dx_ref, v_ref, g_ref, dg_ref, dv_ref):
        @pl.when(pl.program_id(0) == 0)
        def _():
            dg_ref[...] = jnp.zeros_like(dg_ref)

        dxv = dx_ref[...]
        dg_ref[...] += jnp.sum(dxv * v_ref[...], axis=0, keepdims=True)
        dv_ref[...] = (dxv * g_ref[...]).astype(dv_ref.dtype)

    return pl.pallas_call(
        body, grid=(S // tm,), in_specs=[_rows(tm, Dm), _rows(tm, Dm), _vec(Dm)],
        out_specs=[_vec(Dm), _rows(tm, Dm)], out_shape=[SDS((1, Dm), F32), SDS((S, Dm), BF16)],
        compiler_params=_cparams(("arbitrary",)), name=name)(dx, val, g)


def _merge_fwd(y, gf, name):
    S = y.shape[0]
    tm = _tm(S)
    W = 3 * D_MODEL

    def body(y_ref, g_ref, o_ref):
        acc = None
        for k in range(3):
            sl = slice(k * D_MODEL, (k + 1) * D_MODEL)
            t = _sigmoid(g_ref[:, sl]) * y_ref[:, sl]
            acc = t if acc is None else acc + t
        o_ref[...] = acc.astype(o_ref.dtype)

    return pl.pallas_call(
        body, grid=(S // tm,), in_specs=[_rows(tm, W), _rows(tm, W)],
        out_specs=_rows(tm, D_MODEL), out_shape=SDS((S, D_MODEL), BF16),
        compiler_params=_cparams(("parallel",)), name=name)(y, gf)


def _merge_bwd(dm, y, gf, name):
    S = y.shape[0]
    tm = _tm(S)
    W = 3 * D_MODEL

    def body(dm_ref, y_ref, g_ref, dy_ref, dg_ref):
        dmv = dm_ref[...]
        for k in range(3):
            sl = slice(k * D_MODEL, (k + 1) * D_MODEL)
            sg = _sigmoid(g_ref[:, sl])
            dy_ref[:, sl] = (dmv * sg).astype(dy_ref.dtype)
            dg_ref[:, sl] = (dmv * y_ref[:, sl] * (sg * (1.0 - sg))).astype(dg_ref.dtype)

    return pl.pallas_call(
        body, grid=(S // tm,), in_specs=[_rows(tm, D_MODEL), _rows(tm, W), _rows(tm, W)],
        out_specs=[_rows(tm, W), _rows(tm, W)], out_shape=[SDS((S, W), BF16), SDS((S, W), BF16)],
        compiler_params=_cparams(("parallel",)), name=name)(dm, y, gf)


def _swiglu_fwd(u, name):
    S = u.shape[0]
    tm = _tm(S)

    def body(g_ref, u_ref, a_ref):
        gv = g_ref[...]
        a_ref[...] = (gv * _sigmoid(gv) * u_ref[...]).astype(a_ref.dtype)

    return pl.pallas_call(
        body, grid=(S // tm,), in_specs=[_rows(tm, FFN_H, 0), _rows(tm, FFN_H, 1)],
        out_specs=_rows(tm, FFN_H), out_shape=SDS((S, FFN_H), BF16),
        compiler_params=_cparams(("parallel",)), name=name)(u, u)


def _swiglu_bwd(da, u, name):
    S = u.shape[0]
    tm = _tm(S)

    def body(da_ref, g_ref, u_ref, du_ref):
        dav = da_ref[...]
        gv = g_ref[...]
        sg = _sigmoid(gv)
        du_ref[:, 0:FFN_H] = (dav * u_ref[...] * (sg * (1.0 + gv * (1.0 - sg)))).astype(du_ref.dtype)
        du_ref[:, FFN_H:2 * FFN_H] = (dav * (gv * sg)).astype(du_ref.dtype)

    return pl.pallas_call(
        body, grid=(S // tm,), in_specs=[_rows(tm, FFN_H), _rows(tm, FFN_H, 0), _rows(tm, FFN_H, 1)],
        out_specs=_rows(tm, 2 * FFN_H), out_shape=SDS((S, 2 * FFN_H), BF16),
        compiler_params=_cparams(("parallel",)), name=name)(da, u, u)


def _final_loss(x, g, target, name):
    S, Dm = x.shape
    tm = _tm(S)

    def body(x_ref, g_ref, t_ref, loss_ref, dx_ref, dg_ref):
        @pl.when(pl.program_id(0) == 0)
        def _():
            loss_ref[...] = jnp.zeros_like(loss_ref)
            dg_ref[...] = jnp.zeros_like(dg_ref)

        xv = x_ref[...]
        gv = g_ref[...]
        r = lax.rsqrt(jnp.mean(xv * xv, axis=-1, keepdims=True) + EPS)
        xn = xv * r
        err = xn * gv - t_ref[...]
        row = jnp.mean(err * err, axis=-1, keepdims=True)
        loss_ref[...] += 0.5 * jnp.sum(row, axis=0, keepdims=True)
        dy = err * (1.0 / Dm)
        dg_ref[...] += jnp.sum(dy * xn, axis=0, keepdims=True)
        dxn = dy * gv
        dx_ref[...] = r * (dxn - xn * jnp.mean(dxn * xn, axis=-1, keepdims=True))

    return pl.pallas_call(
        body, grid=(S // tm,), in_specs=[_rows(tm, Dm), _vec(Dm), _rows(tm, Dm)],
        out_specs=[pl.BlockSpec((1, 1), lambda i: (0, 0)), _rows(tm, Dm), _vec(Dm)],
        out_shape=[SDS((1, 1), F32), SDS((S, Dm), F32), SDS((1, Dm), F32)],
        compiler_params=_cparams(("arbitrary",)), name=name)(x, g, target)


def _band_softmax(qg, kg, bias, sink, valid):
    s = lax.dot_general(qg, kg, (((1,), (1,)), ((), ())), preferred_element_type=F32)
    s = jnp.where(valid, s + bias, NEG_INF)
    m = jnp.maximum(jnp.max(s, axis=-1, keepdims=True), sink)
    e = jnp.exp(s - m)
    es = jnp.exp(sink - m)
    l = jnp.sum(e, axis=-1, keepdims=True) + es
    return e / l, es / l


def _band_attn_fwd(q, k, v, bias, sink, *, G, P, kvoff, name):
    S = q.shape[0]
    ng = q.shape[1] // (G * HEAD_DIM)
    band = (P + 1) * CHUNK
    pad = P * CHUNK
    nc = S // CHUNK

    def body(q_ref, k_ref, v_ref, b_ref, s_ref, o_ref, kp, vp):
        kp[0:pad, :] = jnp.zeros((pad, LANE), BF16)
        vp[0:pad, :] = jnp.zeros((pad, LANE), BF16)
        kp[pad:pad + S, :] = k_ref[...]
        vp[pad:pad + S, :] = v_ref[...]
        col = lax.broadcasted_iota(jnp.int32, (CHUNK, band), 1)

        def step(n, carry):
            r = pl.multiple_of(n * CHUNK, CHUNK)
            qn = q_ref[pl.ds(r, CHUNK), :]
            kb = kp[pl.ds(r, band), :]
            vb = vp[pl.ds(r, band), :]
            valid = col >= (P - n) * CHUNK
            for g in range(G):
                ko = kvoff(g) * HEAD_DIM
                qg = qn[:, g * HEAD_DIM:(g + 1) * HEAD_DIM] * 0.125
                p, _ = _band_softmax(qg, kb[:, ko:ko + HEAD_DIM], b_ref[g], s_ref[g, 0:1, 0:1], valid)
                og = jnp.dot(p.astype(BF16), vb[:, ko:ko + HEAD_DIM], preferred_element_type=F32)
                o_ref[pl.ds(r, CHUNK), g * HEAD_DIM:(g + 1) * HEAD_DIM] = og.astype(o_ref.dtype)
            return carry

        lax.fori_loop(0, nc, step, 0, unroll=min(BAND_UNROLL_FWD, nc))

    GW = G * HEAD_DIM
    return pl.pallas_call(
        body, grid=(ng,),
        in_specs=[pl.BlockSpec((S, GW), lambda i: (0, i)), pl.BlockSpec((S, LANE), lambda i: (0, i)),
                  pl.BlockSpec((S, LANE), lambda i: (0, i)),
                  pl.BlockSpec((G, CHUNK, band), lambda i: (i, 0, 0)),
                  pl.BlockSpec((G, 8, LANE), lambda i: (i, 0, 0))],
        out_specs=pl.BlockSpec((S, GW), lambda i: (0, i)),
        out_shape=SDS((S, ng * GW), BF16),
        scratch_shapes=[pltpu.VMEM((S + pad, LANE), BF16), pltpu.VMEM((S + pad, LANE), BF16)],
        compiler_params=_cparams(("parallel",)), name=name)(q, k, v, bias, sink)


def _band_attn_bwd(q, k, v, bias, sink, do, *, G, P, kvoff, name):
    S = q.shape[0]
    ng = q.shape[1] // (G * HEAD_DIM)
    band = (P + 1) * CHUNK
    pad = P * CHUNK
    nc = S // CHUNK
    TN = (((0,), (0,)), ((), ()))

    def body(q_ref, k_ref, v_ref, b_ref, s_ref, do_ref, dq_ref, dk_ref, dv_ref, db_ref, dsk_ref,
             kp, vp, dkp, dvp):
        kp[0:pad, :] = jnp.zeros((pad, LANE), BF16)
        vp[0:pad, :] = jnp.zeros((pad, LANE), BF16)
        kp[pad:pad + S, :] = k_ref[...]
        vp[pad:pad + S, :] = v_ref[...]
        dkp[...] = jnp.zeros_like(dkp)
        dvp[...] = jnp.zeros_like(dvp)
        db_ref[...] = jnp.zeros_like(db_ref)
        col = lax.broadcasted_iota(jnp.int32, (CHUNK, band), 1)

        def step(n, dsink):
            r = pl.multiple_of(n * CHUNK, CHUNK)
            qn = q_ref[pl.ds(r, CHUNK), :]
            don = do_ref[pl.ds(r, CHUNK), :]
            kb = kp[pl.ds(r, band), :]
            vb = vp[pl.ds(r, band), :]
            valid = col >= (P - n) * CHUNK
            new = []
            for g in range(G):
                ko = kvoff(g) * HEAD_DIM
                lanes = slice(g * HEAD_DIM, (g + 1) * HEAD_DIM)
                qg = qn[:, lanes] * 0.125
                kg = kb[:, ko:ko + HEAD_DIM]
                dog = don[:, lanes]
                p, ps = _band_softmax(qg, kg, b_ref[g], s_ref[g, 0:1, 0:1], valid)
                dp = lax.dot_general(dog, vb[:, ko:ko + HEAD_DIM], (((1,), (1,)), ((), ())),
                                     preferred_element_type=F32)
                delta = jnp.sum(p * dp, axis=-1, keepdims=True)
                ds = p * (dp - delta)
                new.append(dsink[g] - jnp.sum(ps * delta, axis=0, keepdims=True))
                db_ref[g] += ds
                dsb = ds.astype(BF16)
                dq = jnp.dot(dsb, kg, preferred_element_type=F32) * 0.125
                dq_ref[pl.ds(r, CHUNK), lanes] = dq.astype(dq_ref.dtype)
                dkp[pl.ds(r, band), ko:ko + HEAD_DIM] += lax.dot_general(
                    dsb, qg, TN, preferred_element_type=F32)
                dvp[pl.ds(r, band), ko:ko + HEAD_DIM] += lax.dot_general(
                    p.astype(BF16), dog, TN, preferred_element_type=F32)
            return tuple(new)

        dsink = lax.fori_loop(0, nc, step, tuple(jnp.zeros((1, 1), F32) for _ in range(G)),
                              unroll=min(BAND_UNROLL_BWD, nc))
        for g in range(G):
            dsk_ref[g] = jnp.broadcast_to(dsink[g], (8, LANE))
        dk_ref[...] = dkp[pad:pad + S, :].astype(dk_ref.dtype)
        dv_ref[...] = dvp[pad:pad + S, :].astype(dv_ref.dtype)

    GW = G * HEAD_DIM
    qs = pl.BlockSpec((S, GW), lambda i: (0, i))
    ks = pl.BlockSpec((S, LANE), lambda i: (0, i))
    bs = pl.BlockSpec((G, CHUNK, band), lambda i: (i, 0, 0))
    ss = pl.BlockSpec((G, 8, LANE), lambda i: (i, 0, 0))
    return pl.pallas_call(
        body, grid=(ng,), in_specs=[qs, ks, ks, bs, ss, qs],
        out_specs=[qs, ks, ks, bs, ss],
        out_shape=[SDS((S, ng * GW), BF16), SDS((S, ng * LANE), BF16), SDS((S, ng * LANE), BF16),
                   SDS((ng * G, CHUNK, band), F32), SDS((ng * G, 8, LANE), F32)],
        scratch_shapes=[pltpu.VMEM((S + pad, LANE), BF16), pltpu.VMEM((S + pad, LANE), BF16),
                        pltpu.VMEM((S + pad, LANE), F32), pltpu.VMEM((S + pad, LANE), F32)],
        compiler_params=_cparams(("parallel",)), name=name)(q, k, v, bias, sink, do)


PAIR = 2 * CHUNK


def _bandT_softmax(kg, qTg, bias, sink, valid):
    s = jnp.dot(kg, qTg, preferred_element_type=F32)
    s = jnp.where(valid, s + bias, NEG_INF)
    m = jnp.maximum(jnp.max(s, axis=0, keepdims=True), sink)
    e = jnp.exp(s - m)
    es = jnp.exp(sink - m)
    inv = 1.0 / (jnp.sum(e, axis=0, keepdims=True) + es)
    return e * inv, es * inv


def _pad_copy_rows(dst, src, pad, S):
    dst[:, 0:pad, :] = jnp.zeros((dst.shape[0], pad, dst.shape[2]), dst.dtype)
    dst[:, pad:pad + S, :] = src[...]


def _pad_copy_lanes(dst, src, pad, S):
    dst[:, 0:pad] = jnp.zeros((dst.shape[0], pad), dst.dtype)
    dst[:, pad:pad + S] = src[...]


def _bandT_fwd(qT, k_h, vT, bias, sink, *, GQ, GK, P, kvoff, name):
    S = qT.shape[1]
    ng = qT.shape[0] // (GQ * HEAD_DIM)
    BU = (P + 2) * CHUNK
    pad = P * CHUNK
    npair = S // PAIR

    def body(qT_ref, k_ref, vT_ref, b_ref, s_ref, oT_ref, kp, vTp):
        _pad_copy_rows(kp, k_ref, pad, S)
        _pad_copy_lanes(vTp, vT_ref, pad, S)
        rowi = lax.broadcasted_iota(jnp.int32, (BU, PAIR), 0)

        def step(n2, carry):
            r = pl.multiple_of(n2 * PAIR, PAIR)
            valid = rowi >= (P - 2 * n2) * CHUNK
            for g in range(GQ):
                kv = kvoff(g)
                hs = slice(g * HEAD_DIM, (g + 1) * HEAD_DIM)
                kvs = slice(kv * HEAD_DIM, (kv + 1) * HEAD_DIM)
                qTg = qT_ref[hs, pl.ds(r, PAIR)] * 0.125
                p, _ = _bandT_softmax(kp[kv, pl.ds(r, BU), :], qTg, b_ref[g], s_ref[g, 0:1, :], valid)
                oTg = jnp.dot(vTp[kvs, pl.ds(r, BU)], p.astype(BF16), preferred_element_type=F32)
                oT_ref[hs, pl.ds(r, PAIR)] = oTg.astype(oT_ref.dtype)
            return carry

        lax.fori_loop(0, npair, step, 0, unroll=min(2, npair))

    return pl.pallas_call(
        body, grid=(ng,),
        in_specs=[pl.BlockSpec((GQ * HEAD_DIM, S), lambda i: (i, 0)),
                  pl.BlockSpec((GK, S, HEAD_DIM), lambda i: (i, 0, 0)),
                  pl.BlockSpec((GK * HEAD_DIM, S), lambda i: (i, 0)),
                  pl.BlockSpec((GQ, BU, PAIR), lambda i: (i, 0, 0)),
                  pl.BlockSpec((GQ, 8, LANE), lambda i: (i, 0, 0))],
        out_specs=pl.BlockSpec((GQ * HEAD_DIM, S), lambda i: (i, 0)),
        out_shape=SDS((ng * GQ * HEAD_DIM, S), BF16),
        scratch_shapes=[pltpu.VMEM((GK, S + pad, HEAD_DIM), BF16), pltpu.VMEM((GK * HEAD_DIM, S + pad), BF16)],
        compiler_params=_cparams(("parallel",)), name=name)(qT, k_h, vT, bias, sink)


def _bandT_bwd(qT, q_h, k_h, kT, v_h, doT, do_h, bias, sink, *, GQ, GK, P, kvoff, name):
    S = qT.shape[1]
    ng = qT.shape[0] // (GQ * HEAD_DIM)
    BU = (P + 2) * CHUNK
    pad = P * CHUNK
    npair = S // PAIR

    def body(qT_ref, q_ref, k_ref, kT_ref, v_ref, doT_ref, do_ref, b_ref, s_ref,
             dqT_ref, dk_ref, dv_ref, db_ref, dsk_ref, kp, kTp, vp, dkp, dvp):
        _pad_copy_rows(kp, k_ref, pad, S)
        _pad_copy_rows(vp, v_ref, pad, S)
        _pad_copy_lanes(kTp, kT_ref, pad, S)
        dkp[...] = jnp.zeros_like(dkp)
        dvp[...] = jnp.zeros_like(dvp)
        db_ref[...] = jnp.zeros_like(db_ref)
        rowi = lax.broadcasted_iota(jnp.int32, (BU, PAIR), 0)

        def step(n2, dsink):
            r = pl.multiple_of(n2 * PAIR, PAIR)
            valid = rowi >= (P - 2 * n2) * CHUNK
            new = []
            for g in range(GQ):
                kv = kvoff(g)
                hs = slice(g * HEAD_DIM, (g + 1) * HEAD_DIM)
                kvs = slice(kv * HEAD_DIM, (kv + 1) * HEAD_DIM)
                qTg = qT_ref[hs, pl.ds(r, PAIR)] * 0.125
                p, ps = _bandT_softmax(kp[kv, pl.ds(r, BU), :], qTg, b_ref[g], s_ref[g, 0:1, :], valid)
                dp = jnp.dot(vp[kv, pl.ds(r, BU), :], doT_ref[hs, pl.ds(r, PAIR)], preferred_element_type=F32)
                delta = jnp.sum(p * dp, axis=0, keepdims=True)
                ds = p * (dp - delta)
                new.append(dsink[g] - ps * delta)
                db_ref[g] += ds
                dsb = ds.astype(BF16)
                dq = jnp.dot(kTp[kvs, pl.ds(r, BU)], dsb, preferred_element_type=F32) * 0.125
                dqT_ref[hs, pl.ds(r, PAIR)] = dq.astype(dqT_ref.dtype)
                dkp[kv, pl.ds(r, BU), :] += jnp.dot(dsb, q_ref[g, pl.ds(r, PAIR), :] * 0.125,
                                                    preferred_element_type=F32)
                dvp[kv, pl.ds(r, BU), :] += jnp.dot(p.astype(BF16), do_ref[g, pl.ds(r, PAIR), :],
                                                    preferred_element_type=F32)
            return tuple(new)

        dsink = lax.fori_loop(0, npair, step, tuple(jnp.zeros((1, PAIR), F32) for _ in range(GQ)))
        for g in range(GQ):
            dsk_ref[g] = jnp.broadcast_to(jnp.sum(dsink[g], axis=1, keepdims=True), (8, LANE))
        dk_ref[...] = dkp[:, pad:pad + S, :].astype(dk_ref.dtype)
        dv_ref[...] = dvp[:, pad:pad + S, :].astype(dv_ref.dtype)

    qTs = pl.BlockSpec((GQ * HEAD_DIM, S), lambda i: (i, 0))
    qhs = pl.BlockSpec((GQ, S, HEAD_DIM), lambda i: (i, 0, 0))
    khs = pl.BlockSpec((GK, S, HEAD_DIM), lambda i: (i, 0, 0))
    kTs = pl.BlockSpec((GK * HEAD_DIM, S), lambda i: (i, 0))
    bs = pl.BlockSpec((GQ, BU, PAIR), lambda i: (i, 0, 0))
    ss = pl.BlockSpec((GQ, 8, LANE), lambda i: (i, 0, 0))
    nkv = ng * GK
    return pl.pallas_call(
        body, grid=(ng,), in_specs=[qTs, qhs, khs, kTs, khs, qTs, qhs, bs, ss],
        out_specs=[qTs, khs, khs, bs, ss],
        out_shape=[SDS((ng * GQ * HEAD_DIM, S), BF16), SDS((nkv, S, HEAD_DIM), BF16), SDS((nkv, S, HEAD_DIM), BF16),
                   SDS((ng * GQ, BU, PAIR), F32), SDS((ng * GQ, 8, LANE), F32)],
        scratch_shapes=[pltpu.VMEM((GK, S + pad, HEAD_DIM), BF16), pltpu.VMEM((GK * HEAD_DIM, S + pad), BF16),
                        pltpu.VMEM((GK, S + pad, HEAD_DIM), BF16),
                        pltpu.VMEM((GK, S + pad, HEAD_DIM), F32), pltpu.VMEM((GK, S + pad, HEAD_DIM), F32)],
        compiler_params=_cparams(("parallel",)), name=name)(qT, q_h, k_h, kT, v_h, doT, do_h, bias, sink)


def _pair_table(tab):
    t = jnp.transpose(tab, (0, 2, 1))
    lo = jnp.pad(t, ((0, 0), (0, CHUNK), (0, 0)), constant_values=NEG_INF)
    hi = jnp.pad(t, ((0, 0), (CHUNK, 0), (0, 0)), constant_values=NEG_INF)
    return jnp.concatenate([lo, hi], axis=2)


def _unpair_table(d):
    band = d.shape[1] - CHUNK
    return jnp.transpose(d[:, 0:band, 0:CHUNK] + d[:, CHUNK:CHUNK + band, CHUNK:PAIR], (0, 2, 1))


def _heads(a, n):
    return jnp.transpose(a.reshape(a.shape[0], n, HEAD_DIM), (1, 0, 2))


def _unheads(a):
    return jnp.transpose(a, (1, 0, 2)).reshape(a.shape[1], a.shape[0] * HEAD_DIM)


def _fox_logits(qg, kj, cq, ck, r, c, row, col):
    s = lax.dot_general(qg, kj, (((1,), (1,)), ((), ())), preferred_element_type=F32)
    s = s + cq - ck
    return jnp.where(c + col <= r + row, s, NEG_INF)


def _fox_fwd(q, k, v, cc, cr, name):
    S = q.shape[0]
    npair = q.shape[1] // LANE
    BQ, BK = min(FOX_BQ, S), min(FOX_BK, S)
    nq = S // BQ
    heads = [slice(g * HEAD_DIM, (g + 1) * HEAD_DIM) for g in range(2)]

    def body(q_ref, k_ref, v_ref, cc_ref, cr_ref, o_ref, lse_ref):
        row = lax.broadcasted_iota(jnp.int32, (BQ, BK), 0)
        col = lax.broadcasted_iota(jnp.int32, (BQ, BK), 1)

        def qstep(i, carry):
            r = pl.multiple_of(i * BQ, BQ)
            qs = [q_ref[pl.ds(r, BQ), hl] * 0.125 for hl in heads]
            cqs = [cc_ref[g, pl.ds(r, BQ), :] for g in range(2)]

            def kstep(j, st):
                c = pl.multiple_of(j * BK, BK)
                new = []
                for g, hl in enumerate(heads):
                    m, l, acc = st[g]
                    s = _fox_logits(qs[g], k_ref[pl.ds(c, BK), hl], cqs[g], cr_ref[g, :, pl.ds(c, BK)],
                                    r, c, row, col)
                    mn = jnp.maximum(m, jnp.max(s, axis=-1, keepdims=True))
                    al = jnp.exp(m - mn)
                    e = jnp.exp(s - mn)
                    l = al * l + jnp.sum(e, axis=-1, keepdims=True)
                    acc = al * acc + jnp.dot(e.astype(BF16), v_ref[pl.ds(c, BK), hl],
                                             preferred_element_type=F32)
                    new.append((mn, l, acc))
                return tuple(new)

            init = (jnp.full((BQ, 1), NEG_INF, F32), jnp.zeros((BQ, 1), F32), jnp.zeros((BQ, HEAD_DIM), F32))
            st = lax.fori_loop(0, (r + BQ + BK - 1) // BK, kstep, (init, init))
            for g, hl in enumerate(heads):
                m, l, acc = st[g]
                o_ref[pl.ds(r, BQ), hl] = (acc / l).astype(o_ref.dtype)
                lse_ref[g, pl.ds(r, BQ), :] = m + jnp.log(l)
            return carry

        lax.fori_loop(0, nq, qstep, 0)

    blk = pl.BlockSpec((S, LANE), lambda i: (0, i))
    ccs = pl.BlockSpec((2, S, 1), lambda i: (i, 0, 0))
    crs = pl.BlockSpec((2, 1, S), lambda i: (i, 0, 0))
    return pl.pallas_call(
        body, grid=(npair,), in_specs=[blk, blk, blk, ccs, crs], out_specs=[blk, ccs],
        out_shape=[SDS((S, npair * LANE), BF16), SDS((2 * npair, S, 1), F32)],
        compiler_params=_cparams(("parallel",)), name=name)(q, k, v, cc, cr)


def _fox_bwd(q, k, v, cc, cr, o, do, lse, name):
    S = q.shape[0]
    npair = q.shape[1] // LANE
    BQ, BK = min(FOX_BQ, S), min(FOX_BK, S)
    nq = S // BQ
    heads = [slice(g * HEAD_DIM, (g + 1) * HEAD_DIM) for g in range(2)]
    TN = (((0,), (0,)), ((), ()))

    def body(q_ref, k_ref, v_ref, cc_ref, cr_ref, o_ref, do_ref, lse_ref,
             dq_ref, dk_ref, dv_ref, dcr_ref, dcc_ref, dka, dva):
        dka[...] = jnp.zeros_like(dka)
        dva[...] = jnp.zeros_like(dva)
        dcr_ref[...] = jnp.zeros_like(dcr_ref)
        row = lax.broadcasted_iota(jnp.int32, (BQ, BK), 0)
        col = lax.broadcasted_iota(jnp.int32, (BQ, BK), 1)

        def qstep(i, carry):
            r = pl.multiple_of(i * BQ, BQ)
            qs = [q_ref[pl.ds(r, BQ), hl] * 0.125 for hl in heads]
            dos = [do_ref[pl.ds(r, BQ), hl] for hl in heads]
            deltas = [jnp.sum(dos[g].astype(F32) * o_ref[pl.ds(r, BQ), hl].astype(F32), axis=-1, keepdims=True)
                      for g, hl in enumerate(heads)]
            cqs = [cc_ref[g, pl.ds(r, BQ), :] for g in range(2)]
            lses = [lse_ref[g, pl.ds(r, BQ), :] for g in range(2)]

            def kstep(j, st):
                c = pl.multiple_of(j * BK, BK)
                new = []
                for g, hl in enumerate(heads):
                    dq, rs = st[g]
                    kj = k_ref[pl.ds(c, BK), hl]
                    s = _fox_logits(qs[g], kj, cqs[g], cr_ref[g, :, pl.ds(c, BK)], r, c, row, col)
                    p = jnp.exp(s - lses[g])
                    dp = lax.dot_general(dos[g], v_ref[pl.ds(c, BK), hl], (((1,), (1,)), ((), ())),
                                         preferred_element_type=F32)
                    ds = p * (dp - deltas[g])
                    dcr_ref[g, :, pl.ds(c, BK)] -= jnp.sum(ds, axis=0, keepdims=True)
                    dsb = ds.astype(BF16)
                    dka[pl.ds(c, BK), hl] += lax.dot_general(dsb, qs[g], TN, preferred_element_type=F32)
                    dva[pl.ds(c, BK), hl] += lax.dot_general(p.astype(BF16), dos[g], TN,
                                                            preferred_element_type=F32)
                    new.append((dq + jnp.dot(dsb, kj, preferred_element_type=F32),
                                rs + jnp.sum(ds, axis=-1, keepdims=True)))
                return tuple(new)

            init = (jnp.zeros((BQ, HEAD_DIM), F32), jnp.zeros((BQ, 1), F32))
            st = lax.fori_loop(0, (r + BQ + BK - 1) // BK, kstep, (init, init))
            for g, hl in enumerate(heads):
                dq_ref[pl.ds(r, BQ), hl] = (st[g][0] * 0.125).astype(dq_ref.dtype)
                dcc_ref[g, pl.ds(r, BQ), :] = st[g][1]
            return carry

        lax.fori_loop(0, nq, qstep, 0)
        dk_ref[...] = dka[...].astype(dk_ref.dtype)
        dv_ref[...] = dva[...].astype(dv_ref.dtype)

    blk = pl.BlockSpec((S, LANE), lambda i: (0, i))
    ccs = pl.BlockSpec((2, S, 1), lambda i: (i, 0, 0))
    crs = pl.BlockSpec((2, 1, S), lambda i: (i, 0, 0))
    return pl.pallas_call(
        body, grid=(npair,), in_specs=[blk, blk, blk, ccs, crs, blk, blk, ccs],
        out_specs=[blk, blk, blk, crs, ccs],
        out_shape=[SDS((S, npair * LANE), BF16)] * 3 + [SDS((2 * npair, 1, S), F32), SDS((2 * npair, S, 1), F32)],
        scratch_shapes=[pltpu.VMEM((S, LANE), F32), pltpu.VMEM((S, LANE), F32)],
        compiler_params=_cparams(("parallel",)), name=name)(q, k, v, cc, cr, o, do, lse)


def _foxT_logits(kj, qTg, cq, ck, r, c, rowi, coli):
    s = jnp.dot(kj, qTg, preferred_element_type=F32)
    s = s + cq - ck
    return jnp.where(c + rowi <= r + coli, s, NEG_INF)


def _foxT_fwd(qT, k_h, vT, ck, cq, name):
    S = qT.shape[1]
    npair = qT.shape[0] // LANE
    BQ, BK = min(FOX_BQ, S), min(FOX_BK, S)
    nq = S // BQ
    heads = [slice(g * HEAD_DIM, (g + 1) * HEAD_DIM) for g in range(2)]

    def body(qT_ref, k_ref, vT_ref, ck_ref, cq_ref, oT_ref, lse_ref):
        rowi = lax.broadcasted_iota(jnp.int32, (BK, BQ), 0)
        coli = lax.broadcasted_iota(jnp.int32, (BK, BQ), 1)

        def qstep(i, carry):
            r = pl.multiple_of(i * BQ, BQ)
            qs = [qT_ref[hs, pl.ds(r, BQ)] * 0.125 for hs in heads]
            cqs = [cq_ref[g, :, pl.ds(r, BQ)] for g in range(2)]

            def kstep(j, st):
                c = pl.multiple_of(j * BK, BK)
                new = []
                for g, hs in enumerate(heads):
                    m, l, acc = st[g]
                    s = _foxT_logits(k_ref[g, pl.ds(c, BK), :], qs[g], cqs[g], ck_ref[g, pl.ds(c, BK), :],
                                     r, c, rowi, coli)
                    mn = jnp.maximum(m, jnp.max(s, axis=0, keepdims=True))
                    al = jnp.exp(m - mn)
                    e = jnp.exp(s - mn)
                    l = al * l + jnp.sum(e, axis=0, keepdims=True)
                    acc = al * acc + jnp.dot(vT_ref[hs, pl.ds(c, BK)], e.astype(BF16), preferred_element_type=F32)
                    new.append((mn, l, acc))
                return tuple(new)

            init = (jnp.full((1, BQ), NEG_INF, F32), jnp.zeros((1, BQ), F32), jnp.zeros((HEAD_DIM, BQ), F32))
            st = lax.fori_loop(0, (r + BQ + BK - 1) // BK, kstep, (init, init))
            for g, hs in enumerate(heads):
                m, l, acc = st[g]
                oT_ref[hs, pl.ds(r, BQ)] = (acc * (1.0 / l)).astype(oT_ref.dtype)
                lse_ref[g, :, pl.ds(r, BQ)] = m + jnp.log(l)
            return carry

        lax.fori_loop(0, nq, qstep, 0)

    fT = pl.BlockSpec((LANE, S), lambda i: (i, 0))
    hm = pl.BlockSpec((2, S, HEAD_DIM), lambda i: (i, 0, 0))
    col = pl.BlockSpec((2, S, 1), lambda i: (i, 0, 0))
    rw = pl.BlockSpec((2, 1, S), lambda i: (i, 0, 0))
    return pl.pallas_call(
        body, grid=(npair,), in_specs=[fT, hm, fT, col, rw], out_specs=[fT, rw],
        out_shape=[SDS((npair * LANE, S), BF16), SDS((2 * npair, 1, S), F32)],
        compiler_params=_cparams(("parallel",)), name=name)(qT, k_h, vT, ck, cq)


def _foxT_bwd(qT, q_aug, k_h, kT, v_h, ck, cq, oT, doT, do_h, lse, name):
    S = qT.shape[1]
    npair = qT.shape[0] // LANE
    BQ, BK = min(FOX_BQ, S), min(FOX_BK, S)
    nq = S // BQ
    heads = [slice(g * HEAD_DIM, (g + 1) * HEAD_DIM) for g in range(2)]

    def body(qT_ref, qa_ref, k_ref, kT_ref, v_ref, ck_ref, cq_ref, oT_ref, doT_ref, do_ref, lse_ref,
             dqT_ref, dk_ref, dv_ref, dck_ref, dcq_ref, dka, dva):
        dka[...] = jnp.zeros_like(dka)
        dva[...] = jnp.zeros_like(dva)
        rowi = lax.broadcasted_iota(jnp.int32, (BK, BQ), 0)
        coli = lax.broadcasted_iota(jnp.int32, (BK, BQ), 1)

        def qstep(i, carry):
            r = pl.multiple_of(i * BQ, BQ)
            qs = [qT_ref[hs, pl.ds(r, BQ)] * 0.125 for hs in heads]
            dos = [doT_ref[hs, pl.ds(r, BQ)] for hs in heads]
            deltas = [jnp.sum(dos[g].astype(F32) * oT_ref[hs, pl.ds(r, BQ)].astype(F32), axis=0, keepdims=True)
                      for g, hs in enumerate(heads)]
            cqs = [cq_ref[g, :, pl.ds(r, BQ)] for g in range(2)]
            lses = [lse_ref[g, :, pl.ds(r, BQ)] for g in range(2)]

            def kstep(j, st):
                c = pl.multiple_of(j * BK, BK)
                new = []
                for g, hs in enumerate(heads):
                    dq, rs = st[g]
                    s = _foxT_logits(k_ref[g, pl.ds(c, BK), :], qs[g], cqs[g], ck_ref[g, pl.ds(c, BK), :],
                                     r, c, rowi, coli)
                    p = jnp.exp(s - lses[g])
                    dp = jnp.dot(v_ref[g, pl.ds(c, BK), :], dos[g], preferred_element_type=F32)
                    ds = p * (dp - deltas[g])
                    dsb = ds.astype(BF16)
                    dka[g, pl.ds(c, BK), :] += jnp.dot(dsb, qa_ref[g, pl.ds(r, BQ), :], preferred_element_type=F32)
                    dva[g, pl.ds(c, BK), :] += jnp.dot(p.astype(BF16), do_ref[g, pl.ds(r, BQ), :],
                                                      preferred_element_type=F32)
                    new.append((dq + jnp.dot(kT_ref[hs, pl.ds(c, BK)], dsb, preferred_element_type=F32),
                                rs + jnp.sum(dsb.astype(F32), axis=0, keepdims=True)))
                return tuple(new)

            init = (jnp.zeros((HEAD_DIM, BQ), F32), jnp.zeros((1, BQ), F32))
            st = lax.fori_loop(0, (r + BQ + BK - 1) // BK, kstep, (init, init))
            for g, hs in enumerate(heads):
                dqT_ref[hs, pl.ds(r, BQ)] = (st[g][0] * 0.125).astype(dqT_ref.dtype)
                dcq_ref[g, :, pl.ds(r, BQ)] = st[g][1]
            return carry

        lax.fori_loop(0, nq, qstep, 0)
        dk_ref[...] = dka[:, :, 0:HEAD_DIM].astype(dk_ref.dtype)
        dck_ref[...] = -dka[:, :, HEAD_DIM:HEAD_DIM + 1]
        dv_ref[...] = dva[...].astype(dv_ref.dtype)

    fT = pl.BlockSpec((LANE, S), lambda i: (i, 0))
    hm = pl.BlockSpec((2, S, HEAD_DIM), lambda i: (i, 0, 0))
    hma = pl.BlockSpec((2, S, LANE), lambda i: (i, 0, 0))
    col = pl.BlockSpec((2, S, 1), lambda i: (i, 0, 0))
    rw = pl.BlockSpec((2, 1, S), lambda i: (i, 0, 0))
    nh = 2 * npair
    return pl.pallas_call(
        body, grid=(npair,), in_specs=[fT, hma, hm, fT, hm, col, rw, fT, fT, hm, rw],
        out_specs=[fT, hm, hm, col, rw],
        out_shape=[SDS((npair * LANE, S), BF16), SDS((nh, S, HEAD_DIM), BF16), SDS((nh, S, HEAD_DIM), BF16),
                   SDS((nh, S, 1), F32), SDS((nh, 1, S), F32)],
        scratch_shapes=[pltpu.VMEM((2, S, LANE), F32), pltpu.VMEM((2, S, HEAD_DIM), F32)],
        compiler_params=_cparams(("parallel",)), name=name)(qT, q_aug, k_h, kT, v_h, ck, cq, oT, doT, do_h, lse)


def _split3(x):
    hi = x.astype(BF16)
    r1 = x - hi.astype(F32)
    mid = r1.astype(BF16)
    lo = (r1 - mid.astype(F32)).astype(BF16)
    return hi, mid, lo


def _tri_dot(tri, x):
    hi, mid, lo = _split3(x)
    return (jnp.dot(tri, hi, preferred_element_type=F32) + jnp.dot(tri, mid, preferred_element_type=F32)
            + jnp.dot(tri, lo, preferred_element_type=F32))


def _fox_cum(gf, bfo, name):
    S = gf.shape[0]
    nb = S // LANE
    fcol = (GF_COLS - LANE) // LANE

    def body(f_ref, b_ref, cum_ref):
        row = lax.broadcasted_iota(jnp.int32, (LANE, LANE), 0)
        col = lax.broadcasted_iota(jnp.int32, (LANE, LANE), 1)
        tri = jnp.where(row >= col, 1.0, 0.0).astype(BF16)
        carry = jnp.zeros((1, LANE), F32)
        for t in range(nb):
            xl = f_ref[t * LANE:(t + 1) * LANE, :] + b_ref[...]
            lf = jnp.minimum(xl, 0.0) - jnp.log(1.0 + jnp.exp(-jnp.abs(xl)))
            cblk = _tri_dot(tri, lf) + carry
            cum_ref[t * LANE:(t + 1) * LANE, :] = cblk
            carry = cblk[LANE - 1:LANE, :]

    return pl.pallas_call(
        body, grid=(1,), in_specs=[pl.BlockSpec((S, LANE), lambda i: (0, fcol)), _vec(LANE)],
        out_specs=pl.BlockSpec((S, LANE), lambda i: (0, 0)), out_shape=SDS((S, LANE), F32),
        compiler_params=_cparams(("arbitrary",)), name=name)(gf, bfo)


def _fox_cum_bwd(gf, bfo, dcum, name):
    S = gf.shape[0]
    nb = S // LANE
    fcol = (GF_COLS - LANE) // LANE

    def body(f_ref, b_ref, dc_ref, df_ref, db_ref):
        row = lax.broadcasted_iota(jnp.int32, (LANE, LANE), 0)
        col = lax.broadcasted_iota(jnp.int32, (LANE, LANE), 1)
        tri = jnp.where(row <= col, 1.0, 0.0).astype(BF16)
        carry = jnp.zeros((1, LANE), F32)
        tot = jnp.zeros((1, LANE), F32)
        for t in range(nb - 1, -1, -1):
            rows = slice(t * LANE, (t + 1) * LANE)
            dlf = _tri_dot(tri, dc_ref[rows, :]) + carry
            carry = dlf[0:1, :]
            xl = f_ref[rows, :] + b_ref[...]
            dfl = dlf * (1.0 / (1.0 + jnp.exp(xl)))
            df_ref[rows, :] = dfl.astype(df_ref.dtype)
            tot = tot + jnp.sum(dfl, axis=0, keepdims=True)
        db_ref[...] = tot

    return pl.pallas_call(
        body, grid=(1,),
        in_specs=[pl.BlockSpec((S, LANE), lambda i: (0, fcol)), _vec(LANE), pl.BlockSpec((S, LANE), lambda i: (0, 0))],
        out_specs=[pl.BlockSpec((S, LANE), lambda i: (0, 0)), _vec(LANE)],
        out_shape=[SDS((S, LANE), BF16), SDS((1, LANE), F32)],
        compiler_params=_cparams(("arbitrary",)), name=name)(gf, bfo, dcum)


def _rel_onehot(qi, band):
    r = lax.broadcasted_iota(jnp.int32, (N_REL_PAD, band), 0)
    j = lax.broadcasted_iota(jnp.int32, (N_REL_PAD, band), 1)
    idx = jnp.clip(C_PREV * CHUNK + qi - j, -REL_CLIP, REL_CLIP) + REL_CLIP
    return jnp.where(r == idx, 1.0, 0.0).astype(BF16)


def _rel_expand(rel, name):
    band = (C_PREV + 1) * CHUNK

    def body(rel_ref, o_ref):
        hi, mid, lo = _split3(rel_ref[...])

        def row(qi, carry):
            oh = _rel_onehot(qi, band)
            o_ref[qi] = (jnp.dot(hi, oh, preferred_element_type=F32) + jnp.dot(mid, oh, preferred_element_type=F32)
                         + jnp.dot(lo, oh, preferred_element_type=F32))
            return carry

        lax.fori_loop(0, CHUNK, row, 0, unroll=2)

    return pl.pallas_call(
        body, grid=(1,), in_specs=[pl.BlockSpec((N_HEADS, N_REL_PAD), lambda i: (0, 0))],
        out_specs=pl.BlockSpec((CHUNK, N_HEADS, band), lambda i: (0, 0, 0)),
        out_shape=SDS((CHUNK, N_HEADS, band), F32),
        compiler_params=_cparams(("arbitrary",)), name=name)(rel)


def _tri_dot_rhs(x, oh):
    hi, mid, lo = _split3(x)
    return (jnp.dot(hi, oh, preferred_element_type=F32) + jnp.dot(mid, oh, preferred_element_type=F32)
            + jnp.dot(lo, oh, preferred_element_type=F32))


def _rel_reduce(dbias, name):
    band = (C_PREV + 1) * CHUNK
    NT = (((1,), (1,)), ((), ()))

    def body(d_ref, o_ref):
        def row(qi, acc):
            oh = _rel_onehot(qi, band)
            hi, mid, lo = _split3(d_ref[qi])
            return acc + (lax.dot_general(hi, oh, NT, preferred_element_type=F32)
                          + lax.dot_general(mid, oh, NT, preferred_element_type=F32)
                          + lax.dot_general(lo, oh, NT, preferred_element_type=F32))

        o_ref[...] = lax.fori_loop(0, CHUNK, row, jnp.zeros((N_HEADS, N_REL_PAD), F32), unroll=2)

    return pl.pallas_call(
        body, grid=(1,), in_specs=[pl.BlockSpec((CHUNK, N_HEADS, band), lambda i: (0, 0, 0))],
        out_specs=pl.BlockSpec((N_HEADS, N_REL_PAD), lambda i: (0, 0)),
        out_shape=SDS((N_HEADS, N_REL_PAD), F32),
        compiler_params=_cparams(("arbitrary",)), name=name)(dbias)


def _alibi_table():
    qi = np.arange(CHUNK)[:, None]
    j = np.arange((A_PREV + 1) * CHUNK)[None, :]
    dist = np.abs(A_PREV * CHUNK + qi - j).astype(np.float32)
    slopes = np.exp2(-8.0 * np.arange(1, N_HEADS + 1, dtype=np.float32) / N_HEADS).astype(np.float32)
    return jnp.asarray(-slopes[:, None, None] * dist[None])


def _ada_fwd(c_all, w, b, name):
    n = w.shape[2]

    def body(c_ref, w_ref, b_ref, o_ref):
        cv = c_ref[...]
        cond = (cv * _sigmoid(cv)).astype(BF16)
        o_ref[0] = jnp.dot(cond, w_ref[0].astype(BF16), preferred_element_type=F32) + b_ref[0]

    return pl.pallas_call(
        body, grid=(DEPTH,),
        in_specs=[pl.BlockSpec((16, D_MODEL), lambda l: (0, 0)), pl.BlockSpec((1, D_MODEL, n), lambda l: (l, 0, 0)),
                  pl.BlockSpec((1, 1, n), lambda l: (l, 0, 0))],
        out_specs=pl.BlockSpec((1, 16, n), lambda l: (l, 0, 0)), out_shape=SDS((DEPTH, 16, n), F32),
        compiler_params=_cparams(("parallel",)), name=name)(c_all, w, b)


def _ada_bwd(c_t, dmod, name):
    n = dmod.shape[2]
    bn = _blk(n, 512)
    tr = 256

    def body(c_ref, d_ref, o_ref):
        cv = c_ref[...]
        cond = (cv * _sigmoid(cv)).astype(BF16).astype(F32)
        dm = d_ref[0].astype(BF16).astype(F32)
        acc = cond[:, 0:1] * dm[0:1, :]
        for b_ in range(1, 8):
            acc = acc + cond[:, b_:b_ + 1] * dm[b_:b_ + 1, :]
        o_ref[0] = acc

    return pl.pallas_call(
        body, grid=(DEPTH, D_MODEL // tr, n // bn),
        in_specs=[pl.BlockSpec((tr, 8), lambda l, i, j: (i, 0)), pl.BlockSpec((1, 8, bn), lambda l, i, j: (l, 0, j))],
        out_specs=pl.BlockSpec((1, tr, bn), lambda l, i, j: (l, i, j)), out_shape=SDS((DEPTH, D_MODEL, n), F32),
        compiler_params=_cparams(("parallel", "parallel", "parallel")), name=name)(c_t, dmod)


def _adamw(w, m, v, parts, name):
    L, R, C = w.shape
    P = parts.shape[0]
    tr = _blk_rows(R, max(16, (1 << 18) // C))
    nr = R // tr
    c1 = 1.0 - ADAM_B1 ** ADAM_STEP
    c2 = 1.0 - ADAM_B2 ** ADAM_STEP

    def body(w_ref, m_ref, v_ref, p_ref, g_ref, d_ref, nm_ref, nv_ref):
        g = p_ref[0].astype(F32)
        for k in range(1, P):
            g = g + p_ref[k].astype(F32)
        mn = ADAM_B1 * m_ref[0] + (1.0 - ADAM_B1) * g
        vn = ADAM_B2 * v_ref[0] + (1.0 - ADAM_B2) * (g * g)
        m_hat = mn / c1
        v_hat = vn / c2
        g_ref[0] = g
        nm_ref[0] = mn
        nv_ref[0] = vn
        d_ref[0] = -ADAM_LR * (m_hat / (jnp.sqrt(v_hat) + ADAM_EPS) + ADAM_WD * w_ref[0])

    rs = pl.BlockSpec((1, tr, C), lambda l, i: (l, i, 0))
    return pl.pallas_call(
        body, grid=(L, nr), in_specs=[rs, rs, rs, pl.BlockSpec((P, tr, C), lambda l, i: (0, l * nr + i, 0))],
        out_specs=[rs, rs, rs, rs], out_shape=[SDS((L, R, C), F32)] * 4,
        compiler_params=_cparams(("parallel", "parallel")), name=name)(w, m, v, parts)


def _blk_rows(R, cap):
    if R <= cap:
        return R
    best = None
    for t in range(16, cap + 1, 16):
        if R % t == 0:
            best = t
    assert best is not None, (R, cap)
    return best


def _add_cast(a0, a1, b, name):
    Q, R, C = b.shape
    tr = _blk_rows(R, max(16, (1 << 19) // C))

    def body(a0_ref, a1_ref, b_ref, o_ref):
        c = lax.axis_index("c")

        @pl.when(c == 0)
        def _():
            o_ref[...] = (a0_ref[...] + b_ref[...]).astype(o_ref.dtype)

        @pl.when(c == 1)
        def _():
            o_ref[...] = (a1_ref[...] + b_ref[...]).astype(o_ref.dtype)

    bs = pl.BlockSpec((1, tr, C), lambda q, i: (q, i, 0))
    return pl.pallas_call(
        body, grid=(Q, R // tr), in_specs=[bs, bs, bs], out_specs=bs, out_shape=SDS((Q, R, C), BF16),
        compiler_params=_cparams(("parallel", "parallel")), name=name)(a0, a1, b)


def _coords():
    return lax.axis_index("x"), lax.axis_index("y"), lax.axis_index("c")


def _flip(v, bit):
    return 1 - v if bit else v


def _all_gather8(v, name):
    R = v.shape[0]

    def body(v_ref, o_ref, send_sems, recv_sems):
        x, y, c = _coords()
        me = 4 * x + 2 * y + c
        o_ref[me] = v_ref[...]
        copies = []
        for k in range(1, 8):
            peer = (_flip(x, k & 4), _flip(y, k & 2), _flip(c, k & 1))
            cp = pltpu.make_async_remote_copy(
                src_ref=v_ref, dst_ref=o_ref.at[me], send_sem=send_sems.at[k - 1], recv_sem=recv_sems.at[k - 1],
                device_id=peer, device_id_type=MESH)
            cp.start()
            copies.append(cp)
        for cp in copies:
            cp.wait_recv()
        for cp in copies:
            cp.wait_send()

    return pl.pallas_call(
        body, in_specs=[VMEM_SPEC], out_specs=VMEM_SPEC, out_shape=SDS((8, R, LANE), v.dtype),
        scratch_shapes=[pltpu.SemaphoreType.DMA((7,)), pltpu.SemaphoreType.DMA((7,))],
        compiler_params=pltpu.CompilerParams(vmem_limit_bytes=VMEM_LIMIT), name=name)(v)


def _sibling_swap(arrs0, arrs1, name):
    n = len(arrs0)

    def body(*refs):
        in0, in1, out_refs = refs[:n], refs[n:2 * n], refs[2 * n:3 * n]
        send_sems, recv_sems = refs[3 * n:]
        x, y, c = _coords()

        def swap(srcs):
            copies = [pltpu.make_async_remote_copy(
                src_ref=srcs[a], dst_ref=out_refs[a], send_sem=send_sems.at[a], recv_sem=recv_sems.at[a],
                device_id=(x, y, 1 - c), device_id_type=MESH) for a in range(n)]
            for cp in copies:
                cp.start()
            for cp in copies:
                cp.wait_recv()
            for cp in copies:
                cp.wait_send()

        @pl.when(c == 0)
        def _():
            swap(in1)

        @pl.when(c == 1)
        def _():
            swap(in0)

    return pl.pallas_call(
        body, in_specs=[ANY] * (2 * n), out_specs=[ANY] * n,
        out_shape=[SDS(a.shape, a.dtype) for a in arrs0],
        scratch_shapes=[pltpu.SemaphoreType.DMA((n,)), pltpu.SemaphoreType.DMA((n,))],
        name=name)(*arrs0, *arrs1)


def _chip_exchange(arrs, *, reduce, name):
    n = len(arrs)

    def body(*refs):
        in_refs, out_refs = refs[:n], refs[n:2 * n]
        ici_send, ici_recv, d2d_send, d2d_recv, loc_sem = refs[2 * n:]
        x, y, c = _coords()
        p = 2 * x + y
        local, first, fwd = [], [], []
        for a in range(n):
            R = out_refs[a].shape[1] // 2
            half = pl.ds(pl.multiple_of(c * R, 16), R)
            if reduce:
                lc = pltpu.make_async_copy(in_refs[a].at[p], out_refs[a].at[p, half], loc_sem.at[a])
            else:
                lc = pltpu.make_async_copy(in_refs[a], out_refs[a].at[p], loc_sem.at[a])
            lc.start()
            local.append(lc)
            for k in range(1, 4):
                qx, qy = _flip(x, k & 2), _flip(y, k & 1)
                src = in_refs[a].at[2 * qx + qy] if reduce else in_refs[a].at[half]
                cp = pltpu.make_async_remote_copy(
                    src_ref=src, dst_ref=out_refs[a].at[p, half], send_sem=ici_send.at[a, k - 1],
                    recv_sem=ici_recv.at[a, k - 1], device_id=(qx, qy, c), device_id_type=MESH)
                cp.start()
                first.append(cp)
        for a in range(n):
            R = out_refs[a].shape[1] // 2
            half = pl.ds(pl.multiple_of(c * R, 16), R)
            for k in range(0 if reduce else 1, 4):
                qx, qy = _flip(x, k & 2), _flip(y, k & 1)
                slot = out_refs[a].at[2 * qx + qy, half]
                if k == 0:
                    local[a].wait()
                else:
                    first[a * 3 + k - 1].wait_recv()
                cp = pltpu.make_async_remote_copy(
                    src_ref=slot, dst_ref=slot, send_sem=d2d_send.at[a, k], recv_sem=d2d_recv.at[a, k],
                    device_id=(x, y, 1 - c), device_id_type=MESH)
                cp.start()
                fwd.append(cp)
        for cp in fwd:
            cp.wait_recv()
        for cp in first + fwd:
            cp.wait_send()
        if not reduce:
            for lc in local:
                lc.wait()

    if reduce:
        out_shape = [SDS((4, 2 * a.shape[1], a.shape[2]), a.dtype) for a in arrs]
    else:
        out_shape = [SDS((4,) + a.shape, a.dtype) for a in arrs]
    return pl.pallas_call(
        body, in_specs=[ANY] * n, out_specs=[ANY] * n, out_shape=out_shape,
        scratch_shapes=[pltpu.SemaphoreType.DMA((n, 3)), pltpu.SemaphoreType.DMA((n, 3)),
                        pltpu.SemaphoreType.DMA((n, 4)), pltpu.SemaphoreType.DMA((n, 4)),
                        pltpu.SemaphoreType.DMA((n,))],
        name=name)(*arrs)


_IN_SIZES = (512, 128, 128, 512, 512, 512, 8, 512, 512, 512, 3072)
_IN_OFF = tuple(int(v) for v in np.cumsum((0,) + _IN_SIZES))
_IN_Q = N_IN_COLS // 4


def _pack_w_in(w):
    def cols(lo, hi):
        out = []
        while lo < hi:
            q, off = divmod(lo, _IN_Q)
            n = min(hi - lo, _IN_Q - off)
            out.append(w[q, :, off:off + n])
            lo += n
        return out

    fb0, fb1, g0 = _IN_OFF[6], _IN_OFF[7], _IN_OFF[10]
    wqkv = jnp.concatenate(cols(0, fb0) + cols(fb1, g0), axis=1)
    wgf = jnp.concatenate(cols(g0, N_IN_COLS) + cols(fb0, fb1) + [jnp.zeros((w.shape[1], LANE - 8), w.dtype)], axis=1)
    return wqkv, wgf


def _unpack_w_in(dqkv, dgf):
    fb0, fb1, g0 = _IN_OFF[6], _IN_OFF[7], _IN_OFF[10]

    def cols(lo, hi):
        out = []
        while lo < hi:
            if lo < fb0:
                n = min(hi, fb0) - lo
                out.append(dqkv[:, lo:lo + n])
            elif lo < fb1:
                n = min(hi, fb1) - lo
                out.append(dgf[:, 3072 + lo - fb0:3072 + lo - fb0 + n])
            elif lo < g0:
                n = min(hi, g0) - lo
                out.append(dqkv[:, lo - 8:lo - 8 + n])
            else:
                n = hi - lo
                out.append(dgf[:, lo - g0:lo - g0 + n])
            lo += n
        return out

    return jnp.stack([jnp.concatenate(cols(q * _IN_Q, (q + 1) * _IN_Q), axis=1) for q in range(4)])


def _pad_rows(a, rows):
    return jnp.pad(a, ((0, rows - a.shape[0]), (0, 0)))


def _small_pack(parts):
    flat = jnp.concatenate([p.reshape(-1) for p in parts])
    n = flat.shape[0]
    rows = -(-n // LANE)
    rows = -(-rows // 8) * 8
    return jnp.pad(flat, (0, rows * LANE - n)).reshape(rows, LANE)


def _small_unpack(block, shapes):
    flat = block.reshape(-1)
    out, off = [], 0
    for s in shapes:
        n = int(np.prod(s))
        out.append(flat[off:off + n].reshape(s))
        off += n
    return out


def _kv_same(g):
    return 0


def _kv_own(g):
    return g


def _layer_fwd(x, mod, p, l):
    sh_m, sc_m, g_m, sh_f, sc_f, g_f = mod
    nm = "l%d_" % l
    h1 = _norm_mod_fwd(x, p["norm_mix_g"], sc_m, sh_m, nm + "norm_mix_fwd")
    qkv = _mm(h1, p["wqkv"], mode="nn", out_dtype=BF16, name=nm + "proj_qkv")
    gf = _mm(h1, p["wgf"], mode="nn", out_dtype=F32, name=nm + "proj_gf", cap_n=640)
    qkv_t = qkv.T
    o_a_t = _bandT_fwd(qkv_t[0:512], _heads(qkv[:, 512:640], A_KV_HEADS), qkv_t[640:768], p["alibi"],
                       p["sink_tab"], GQ=4, GK=1, P=A_PREV, kvoff=_kv_same, name=nm + "attn_a_fwd")
    cum = _fox_cum(gf, p["b_forget_pad"], nm + "fox_cum")
    cum_t = cum[:, :N_HEADS].T
    cc, cr = cum_t[:, :, None], cum_t[:, None, :]
    o_b_t, lse_b = _foxT_fwd(qkv_t[768:1280], _heads(qkv[:, 1280:1792], N_HEADS), qkv_t[1792:2304], cc, cr,
                             nm + "attn_b_fwd")
    o_c_t = _bandT_fwd(qkv_t[2304:2816], _heads(qkv[:, 2816:3328], N_HEADS), qkv_t[3328:3840], p["rel_tab"],
                       p["no_sink"], GQ=2, GK=2, P=C_PREV, kvoff=_kv_own, name=nm + "attn_c_fwd")
    o = jnp.concatenate([o_a_t, o_b_t, o_c_t], axis=0).T
    y = _mm(o, p["wb"], mode="nn", out_dtype=F32, groups=3, name=nm + "branch")
    merged = _merge_fwd(y, gf, nm + "merge_fwd")
    mix = _mm(merged, p["wout"], mode="nn", out_dtype=F32, name=nm + "out_proj")
    x1 = _resid_fwd(x, mix, g_m, nm + "resid_mix")
    h2 = _norm_mod_fwd(x1, p["norm_ffn_g"], sc_f, sh_f, nm + "norm_ffn_fwd")
    u = _mm(h2, p["wfi"], mode="nn", out_dtype=F32, name=nm + "ffn_in", cap_n=512)
    a = _swiglu_fwd(u, nm + "swiglu_fwd")
    f = _mm(a, p["wfo"], mode="nn", out_dtype=F32, name=nm + "ffn_out", cap_m=1024)
    x2 = _resid_fwd(x1, f, g_f, nm + "resid_ffn")
    saved = dict(x=x, h1=h1, qkv=qkv, qkv_t=qkv_t, gf=gf, cc=cc, cr=cr, o_b_t=o_b_t, lse_b=lse_b, o=o, y=y, merged=merged,
                 mix=mix, x1=x1, h2=h2, u=u, a=a, f=f)
    return x2, saved


def _layer_bwd(dx2, mod, p, s, l):
    sh_m, sc_m, g_m, sh_f, sc_f, g_f = mod
    nm = "l%d_" % l
    dg_f, df = _resid_bwd(dx2, s["f"], g_f, nm + "resid_ffn_bwd")
    da = _mm(df, p["wfo"], mode="nt", out_dtype=F32, name=nm + "ffn_out_dx", cap_m=1024, cap_n=1408)
    d_wfo = _mm(s["a"], df, mode="tn", out_dtype=F32, name=nm + "ffn_out_dw", cap_m=1408, cap_k=2048)
    du = _swiglu_bwd(da, s["u"], nm + "swiglu_bwd")
    dh2 = _mm(du, p["wfi"], mode="nt", out_dtype=F32, name=nm + "ffn_in_dx", cap_m=1024)
    d_wfi = _mm(s["h2"], du, mode="tn", out_dtype=F32, name=nm + "ffn_in_dw", cap_m=1024, cap_n=1408, cap_k=2048,
                col_quarters=True)
    dx1, dsc_f, dsh_f, dgn_f = _norm_mod_bwd(s["x1"], [dh2], dx2, p["norm_ffn_g"], sc_f, nm + "norm_ffn_bwd")
    dg_m, dmix = _resid_bwd(dx1, s["mix"], g_m, nm + "resid_mix_bwd")
    dmerged = _mm(dmix, p["wout"], mode="nt", out_dtype=F32, name=nm + "out_proj_dx")
    d_wout = _mm(s["merged"], dmix, mode="tn", out_dtype=F32, name=nm + "out_proj_dw", cap_m=1024, cap_k=2048)
    dy, dgates = _merge_bwd(dmerged, s["y"], s["gf"], nm + "merge_bwd")
    do = _mm(dy, p["wb"], mode="nt", out_dtype=BF16, groups=3, name=nm + "branch_dx")
    d_wb = _mm(s["o"], dy, mode="tn", out_dtype=F32, groups=3, name=nm + "branch_dw", cap_k=2048,
               col_quarters=True)
    qkv, qkv_t = s["qkv"], s["qkv_t"]
    do_t = do.T
    dqa_t, dka_h, dva_h, _, dsink = _bandT_bwd(
        qkv_t[0:512], _heads(qkv[:, 0:512], N_HEADS), _heads(qkv[:, 512:640], A_KV_HEADS), qkv_t[512:640],
        _heads(qkv[:, 640:768], A_KV_HEADS), do_t[0:512], _heads(do[:, 0:512], N_HEADS), p["alibi"], p["sink_tab"],
        GQ=4, GK=1, P=A_PREV, kvoff=_kv_same, name=nm + "attn_a_bwd")
    qb_h = _heads(qkv[:, 768:1280], N_HEADS)
    q_aug = jnp.concatenate([qb_h * 0.125, jnp.ones(qb_h.shape[:2] + (1,), BF16),
                             jnp.zeros(qb_h.shape[:2] + (LANE - HEAD_DIM - 1,), BF16)], axis=2)
    dqb_t, dkb_h, dvb_h, dck, dcq = _foxT_bwd(
        qkv_t[768:1280], q_aug, _heads(qkv[:, 1280:1792], N_HEADS), qkv_t[1280:1792],
        _heads(qkv[:, 1792:2304], N_HEADS), s["cc"], s["cr"], s["o_b_t"], do_t[512:1024],
        _heads(do[:, 512:1024], N_HEADS), s["lse_b"], nm + "attn_b_bwd")
    dcum = jnp.pad((dck[:, :, 0] + dcq[:, 0, :]).T, ((0, 0), (0, LANE - N_HEADS)))
    dfb, db_forget = _fox_cum_bwd(s["gf"], p["b_forget_pad"], dcum, nm + "fox_cum_bwd")
    dqc_t, dkc_h, dvc_h, dbias_c, _ = _bandT_bwd(
        qkv_t[2304:2816], _heads(qkv[:, 2304:2816], N_HEADS), _heads(qkv[:, 2816:3328], N_HEADS), qkv_t[2816:3328],
        _heads(qkv[:, 3328:3840], N_HEADS), do_t[1024:1536], _heads(do[:, 1024:1536], N_HEADS), p["rel_tab"],
        p["no_sink"], GQ=2, GK=2, P=C_PREV, kvoff=_kv_own, name=nm + "attn_c_bwd")
    d_rel = _rel_reduce(jnp.transpose(_unpair_table(dbias_c), (1, 0, 2)), nm + "rel_reduce")[:, :N_REL]
    dqkv = jnp.concatenate([dqa_t.T, _unheads(dka_h), _unheads(dva_h), dqb_t.T, _unheads(dkb_h), _unheads(dvb_h),
                            dqc_t.T, _unheads(dkc_h), _unheads(dvc_h)], axis=1)
    dgf = jnp.concatenate([dgates, dfb], axis=1)
    dh1a = _mm(dqkv, p["wqkv"], mode="nt", out_dtype=F32, name=nm + "proj_qkv_dx", cap_k=1024)
    dh1b = _mm(dgf, p["wgf"], mode="nt", out_dtype=F32, name=nm + "proj_gf_dx", cap_k=640)
    d_wqkv = _mm(s["h1"], dqkv, mode="tn", out_dtype=F32, name=nm + "proj_qkv_dw", cap_m=1024, cap_k=2048)
    d_wgf = _mm(s["h1"], dgf, mode="tn", out_dtype=F32, name=nm + "proj_gf_dw", cap_m=1024, cap_n=640, cap_k=2048)
    dx, dsc_m, dsh_m, dgn_m = _norm_mod_bwd(s["x"], [dh1a, dh1b], dx1, p["norm_mix_g"], sc_m, nm + "norm_mix_bwd")
    d_mod = jnp.concatenate([dsh_m, dsc_m, dg_m, dsh_f, dsc_f, dg_f], axis=1)[0]
    grads = dict(w_in=_unpack_w_in(d_wqkv, d_wgf), w_branch=d_wb, w_out=d_wout.reshape(4, -1, D_MODEL),
                 w_ffn_in=d_wfi, w_ffn_out=d_wfo.reshape(4, -1, D_MODEL),
                 norm_mix_g=dgn_m[0], norm_ffn_g=dgn_f[0], b_forget=db_forget[0, :N_HEADS],
                 sinks=dsink[:, 0, 0], rel_bias=d_rel, d_mod=d_mod)
    return dx, grads


def kernel(x, c, norm_mix_g, norm_ffn_g, w_ada, b_ada, w_in, b_forget, sinks, rel_bias, w_branch, w_out, w_ffn_in, w_ffn_out, final_norm_g, loss_target, m_norm_mix_g, m_norm_ffn_g, m_w_ada, m_b_ada, m_w_in, m_b_forget, m_sinks, m_rel_bias, m_w_branch, m_w_out, m_w_ffn_in, m_w_ffn_out, m_final_norm_g, v_norm_mix_g, v_norm_ffn_g, v_w_ada, v_b_ada, v_w_in, v_b_forget, v_sinks, v_rel_bias, v_w_branch, v_w_out, v_w_ffn_in, v_w_ffn_out, v_final_norm_g):
    xi, yi, ci = _coords()
    chip = 2 * xi + yi
    dev = 2 * chip + ci
    xs = x[0]
    S = xs.shape[0]
    n_ada = w_ada.shape[2]

    big_names = ("w_in", "w_branch", "w_out", "w_ffn_in", "w_ffn_out")
    big_w = dict(w_in=w_in, w_branch=w_branch, w_out=w_out, w_ffn_in=w_ffn_in, w_ffn_out=w_ffn_out)
    big_m = dict(w_in=m_w_in, w_branch=m_w_branch, w_out=m_w_out, w_ffn_in=m_w_ffn_in, w_ffn_out=m_w_ffn_out)
    big_v = dict(w_in=v_w_in, w_branch=v_w_branch, w_out=v_w_out, w_ffn_in=v_w_ffn_in, w_ffn_out=v_w_ffn_out)
    flat2 = lambda a: a.reshape(-1, a.shape[-1])
    shards = [flat2(big_w[n]).astype(BF16) for n in big_names]
    gw_in, gw_branch, gw_out, gw_ffn_in, gw_ffn_out = _chip_exchange(shards, reduce=False, name="weights_all_gather")
    cin = w_in.shape[2]
    cbr = w_branch.shape[3]
    rout = w_out.shape[1]
    cfi = w_ffn_in.shape[2]
    rfo = w_ffn_out.shape[1]
    w_branch_full = gw_branch.reshape(4, DEPTH, 3, BRANCH_W, cbr).transpose(1, 2, 3, 0, 4).reshape(
        DEPTH, 3 * BRANCH_W, 4 * cbr)
    w_out_full = gw_out.reshape(4, DEPTH, rout, D_MODEL).transpose(1, 0, 2, 3).reshape(DEPTH, 4 * rout, D_MODEL)
    w_ffn_in_full = gw_ffn_in.reshape(4, DEPTH, D_MODEL, cfi).transpose(1, 2, 0, 3).reshape(DEPTH, D_MODEL, 4 * cfi)
    w_ffn_out_full = gw_ffn_out.reshape(4, DEPTH, rfo, D_MODEL).transpose(1, 0, 2, 3).reshape(DEPTH, 4 * rfo, D_MODEL)

    c_all = _all_gather8(c.reshape(8, LANE), "gather_c").reshape(8, D_MODEL)
    b_sh = lax.dynamic_slice_in_dim(b_ada, chip * n_ada, n_ada, axis=1)[:, None, :]
    mod_sh = _ada_fwd(_pad_rows(c_all, 16), w_ada, b_sh, "ada_fwd")[:, :8, :]
    mod_all = _all_gather8(mod_sh.reshape(-1, LANE), "gather_mod").reshape(8, DEPTH, 8, n_ada)
    mod_mine = lax.dynamic_index_in_dim(mod_all[0::2], dev, axis=2, keepdims=False)
    mod = mod_mine.transpose(1, 0, 2).reshape(DEPTH, 6, D_MODEL)

    alibi = _pair_table(_alibi_table())
    no_sink = jnp.full((N_HEADS, 8, LANE), NEG_INF, F32)
    params = []
    for l in range(DEPTH):
        wqkv, wgf = _pack_w_in(gw_in[:, l * D_MODEL:(l + 1) * D_MODEL, :])
        rel_tab = _rel_expand(jnp.pad(rel_bias[l], ((0, 0), (0, N_REL_PAD - N_REL))), "l%d_rel_expand" % l)
        params.append(dict(
            wqkv=wqkv, wgf=wgf, wb=w_branch_full[l], wout=w_out_full[l], wfi=w_ffn_in_full[l], wfo=w_ffn_out_full[l],
            norm_mix_g=norm_mix_g[l][None], norm_ffn_g=norm_ffn_g[l][None],
            b_forget_pad=jnp.pad(b_forget[l], (0, LANE - N_HEADS))[None],
            sink_tab=jnp.broadcast_to(sinks[l][:, None, None], (N_HEADS, 8, LANE)),
            no_sink=no_sink, alibi=alibi, rel_tab=_pair_table(jnp.transpose(rel_tab, (1, 0, 2)))))
    mods = [[mod[l, k][None] for k in range(6)] for l in range(DEPTH)]
    h = xs
    saved = []
    for l in range(DEPTH):
        h, s = _layer_fwd(h, mods[l], params[l], l)
        saved.append(s)
    loss_dev, dh, d_final = _final_loss(h, final_norm_g[None], loss_target[0], "final_loss")
    grads = [None] * DEPTH
    for l in reversed(range(DEPTH)):
        dh, grads[l] = _layer_bwd(dh, mods[l], params[l], saved[l], l)
    grad_x = dh[None]
    loss = lax.psum(loss_dev[0, 0], ("x", "y", "c"))

    g0 = [grads[0][n] for n in big_names]
    g1 = [grads[1][n] for n in big_names]
    theirs = _sibling_swap(g0, g1, "grads_sibling_swap")
    chip_sum = [_add_cast(a0, a1, b, "grads_chip_sum_%s" % n) for n, a0, a1, b in zip(big_names, g0, g1, theirs)]
    parts = _chip_exchange(chip_sum, reduce=True, name="grads_reduce_scatter")
    big_out = {}
    for n, pt in zip(big_names, parts):
        shp = big_w[n].shape
        as3 = lambda a: a.reshape(shp[0], -1, shp[-1])
        res = _adamw(as3(big_w[n]), as3(big_m[n]), as3(big_v[n]), pt, "adamw_" + n)
        big_out[n] = [r.reshape(shp) for r in res]

    small_names = ("norm_mix_g", "norm_ffn_g", "b_ada", "b_forget", "sinks", "rel_bias", "final_norm_g")
    small_w = dict(norm_mix_g=norm_mix_g, norm_ffn_g=norm_ffn_g, b_ada=b_ada, b_forget=b_forget, sinks=sinks,
                   rel_bias=rel_bias, final_norm_g=final_norm_g)
    small_m = dict(norm_mix_g=m_norm_mix_g, norm_ffn_g=m_norm_ffn_g, b_ada=m_b_ada, b_forget=m_b_forget,
                   sinks=m_sinks, rel_bias=m_rel_bias, final_norm_g=m_final_norm_g)
    small_v = dict(norm_mix_g=v_norm_mix_g, norm_ffn_g=v_norm_ffn_g, b_ada=v_b_ada, b_forget=v_b_forget,
                   sinks=v_sinks, rel_bias=v_rel_bias, final_norm_g=v_final_norm_g)
    small_g = dict(
        norm_mix_g=jnp.stack([grads[l]["norm_mix_g"] for l in range(DEPTH)]),
        norm_ffn_g=jnp.stack([grads[l]["norm_ffn_g"] for l in range(DEPTH)]),
        b_ada=jnp.stack([grads[l]["d_mod"] for l in range(DEPTH)]),
        b_forget=jnp.stack([grads[l]["b_forget"] for l in range(DEPTH)]),
        sinks=jnp.stack([grads[l]["sinks"] for l in range(DEPTH)]),
        rel_bias=jnp.stack([grads[l]["rel_bias"] for l in range(DEPTH)]),
        final_norm_g=d_final[0])
    shapes = [small_w[n].shape for n in small_names]
    g_all = _all_gather8(_small_pack([small_g[n] for n in small_names]), "gather_small_grads")
    res = _adamw(_small_pack([small_w[n] for n in small_names])[None], _small_pack([small_m[n] for n in small_names])[None],
                 _small_pack([small_v[n] for n in small_names])[None], g_all, "adamw_small")
    small_out = {n: [] for n in small_names}
    for r in res:
        for n, a in zip(small_names, _small_unpack(r[0], shapes)):
            small_out[n].append(a)
    off_b = sum(int(np.prod(s)) for s in shapes[:2])
    n_mod = DEPTH * 6 * D_MODEL
    dmod_all = g_all.reshape(8, -1)[:, off_b:off_b + n_mod].reshape(8, DEPTH, 6 * D_MODEL)
    dmod_sh = lax.dynamic_slice_in_dim(dmod_all, chip * n_ada, n_ada, axis=2).transpose(1, 0, 2)
    g_ada = _ada_bwd(c_all.T, dmod_sh, "ada_bwd")
    ada_out = _adamw(w_ada, m_w_ada, v_w_ada, flat2(g_ada)[None], "adamw_w_ada")

    order = ("norm_mix_g", "norm_ffn_g", "w_ada", "b_ada", "w_in", "b_forget", "sinks", "rel_bias", "w_branch",
             "w_out", "w_ffn_in", "w_ffn_out", "final_norm_g")

    def pick(n, k):
        if n == "w_ada":
            return ada_out[k]
        if n in big_out:
            return big_out[n][k]
        return small_out[n][k]

    outs = [loss, grad_x]
    for k in range(4):
        outs += [pick(n, k) for n in order]
    return tuple(outs)
```

```python
import functools

import numpy as np
import jax
import jax.numpy as jnp
from jax import lax
from jax.experimental import pallas as pl
from jax.experimental.pallas import tpu as pltpu

F32 = jnp.float32
BF16 = jnp.bfloat16
SDS = jax.ShapeDtypeStruct

D_MODEL = 1024
DEPTH = 2
CHUNK = 64
HEAD_DIM = 64
EPS = 1e-6
NEG_INF = -1e30
N_HEADS = 8
A_KV_HEADS = 2
A_PREV = 2
C_PREV = 8
REL_CLIP = 128
N_REL = 2 * REL_CLIP + 1
N_REL_PAD = 384
BRANCH_W = 512
FFN_H = 2816
FOX_BQ = 256
FOX_BK = 512
BAND_UNROLL_FWD = 4
BAND_UNROLL_BWD = 2
QKV_COLS = 3840
GF_COLS = 3200
N_IN_COLS = 6920
LANE = 128
VMEM_LIMIT = 48 * 1024 * 1024

ADAM_LR = 0.001
ADAM_B1 = 0.9
ADAM_B2 = 0.999
ADAM_EPS = 1e-08
ADAM_WD = 0.01
ADAM_STEP = 10

MESH = pl.DeviceIdType.MESH
ANY = pl.BlockSpec(memory_space=pl.ANY)
VMEM_SPEC = pl.BlockSpec(memory_space=pltpu.VMEM)


def _cparams(sem=None):
    return pltpu.CompilerParams(dimension_semantics=sem, vmem_limit_bytes=VMEM_LIMIT)


def _blk(n, cap):
    if n <= cap:
        return n
    best = None
    for m in range(LANE, cap + 1, LANE):
        if n % m == 0:
            best = m
    assert best is not None, (n, cap)
    return best


def _sigmoid(x):
    return 1.0 / (1.0 + jnp.exp(-x))


def _mm(a, b, *, mode, out_dtype, name, groups=1, cap_m=2048, cap_n=1024, cap_k=1408, col_quarters=False):
    G = groups
    assert not col_quarters or mode == "tn"
    if mode == "nn":
        M, K, N = a.shape[0], a.shape[1] // G, b.shape[1]
        assert b.shape[0] == G * K
    elif mode == "nt":
        M, K, N = a.shape[0], a.shape[1] // G, b.shape[0] // G
        assert b.shape[1] == K
    else:
        K, M, N = a.shape[0], a.shape[1] // G, b.shape[1] // G
        assert b.shape[0] == K
    bm, bn, bk = _blk(M, cap_m), _blk(N // 4 if col_quarters else N, cap_n), _blk(K, cap_k)
    nm, nn, nk = M // bm, N // bn, K // bk
    if mode == "nn":
        a_spec = pl.BlockSpec((bm, bk), lambda g, i, j, k: (i, g * nk + k))
        b_spec = pl.BlockSpec((bk, bn), lambda g, i, j, k: (g * nk + k, j))
        o_spec = pl.BlockSpec((bm, bn), lambda g, i, j, k: (i, g * nn + j))
        dims = (((1,), (0,)), ((), ()))
        out_shape = (M, G * N)
    elif mode == "nt":
        a_spec = pl.BlockSpec((bm, bk), lambda g, i, j, k: (i, g * nk + k))
        b_spec = pl.BlockSpec((bn, bk), lambda g, i, j, k: (g * nn + j, k))
        o_spec = pl.BlockSpec((bm, bn), lambda g, i, j, k: (i, g * nn + j))
        dims = (((1,), (1,)), ((), ()))
        out_shape = (M, G * N)
    else:
        a_spec = pl.BlockSpec((bk, bm), lambda g, i, j, k: (k, g * nm + i))
        b_spec = pl.BlockSpec((bk, bn), lambda g, i, j, k: (k, g * nn + j))
        dims = (((0,), (0,)), ((), ()))
        if col_quarters:
            nq = nn // 4
            o_spec = pl.BlockSpec((1, bm, bn), lambda g, i, j, k: (j // nq, g * nm + i, j % nq))
            out_shape = (4, G * M, N // 4)
        else:
            o_spec = pl.BlockSpec((bm, bn), lambda g, i, j, k: (g * nm + i, j))
            out_shape = (G * M, N)

    def product(a_ref, b_ref):
        return lax.dot_general(a_ref[...].astype(BF16), b_ref[...].astype(BF16), dims, preferred_element_type=F32)

    def body_one(a_ref, b_ref, o_ref):
        o_ref[...] = product(a_ref, b_ref).astype(o_ref.dtype).reshape(o_ref.shape)

    def body_acc(a_ref, b_ref, o_ref, acc_ref):
        k = pl.program_id(3)

        @pl.when(k == 0)
        def _():
            acc_ref[...] = jnp.zeros_like(acc_ref)

        acc_ref[...] += product(a_ref, b_ref)

        @pl.when(k == nk - 1)
        def _():
            o_ref[...] = acc_ref[...].astype(o_ref.dtype).reshape(o_ref.shape)

    return pl.pallas_call(
        body_one if nk == 1 else body_acc, grid=(G, nm, nn, nk), in_specs=[a_spec, b_spec], out_specs=o_spec,
        out_shape=SDS(out_shape, out_dtype), scratch_shapes=[] if nk == 1 else [pltpu.VMEM((bm, bn), F32)],
        compiler_params=_cparams(("parallel", "parallel", "parallel", "arbitrary")), name=name,
    )(a, b)


def _rows(tm, n, col=0):
    return pl.BlockSpec((tm, n), lambda i: (i, col))


def _vec(n):
    return pl.BlockSpec((1, n), lambda i: (0, 0))


def _tm(S):
    return min(S, 256)


def _norm_mod_fwd(x, g, sc, sh, name):
    S, Dm = x.shape
    tm = _tm(S)

    def body(x_ref, g_ref, sc_ref, sh_ref, h_ref):
        xv = x_ref[...]
        r = lax.rsqrt(jnp.mean(xv * xv, axis=-1, keepdims=True) + EPS)
        h_ref[...] = ((xv * r) * g_ref[...] * (1.0 + sc_ref[...]) + sh_ref[...]).astype(h_ref.dtype)

    return pl.pallas_call(
        body, grid=(S // tm,), in_specs=[_rows(tm, Dm), _vec(Dm), _vec(Dm), _vec(Dm)],
        out_specs=_rows(tm, Dm), out_shape=SDS((S, Dm), BF16),
        compiler_params=_cparams(("parallel",)), name=name)(x, g, sc, sh)


def _norm_mod_bwd(x, dh_list, dres, g, sc, name):
    S, Dm = x.shape
    tm = _tm(S)
    nh = len(dh_list)

    def body(*refs):
        x_ref = refs[0]
        dh_refs = refs[1:1 + nh]
        dres_ref, g_ref, sc_ref, dx_ref, dsc_ref, dsh_ref, dg_ref = refs[1 + nh:]
        i = pl.program_id(0)

        @pl.when(i == 0)
        def _():
            dsc_ref[...] = jnp.zeros_like(dsc_ref)
            dsh_ref[...] = jnp.zeros_like(dsh_ref)
            dg_ref[...] = jnp.zeros_like(dg_ref)

        xv = x_ref[...]
        dh = dh_refs[0][...]
        for r_ in dh_refs[1:]:
            dh = dh + r_[...]
        gv = g_ref[...]
        r = lax.rsqrt(jnp.mean(xv * xv, axis=-1, keepdims=True) + EPS)
        xn = xv * r
        xg = xn * gv
        dsh_ref[...] += jnp.sum(dh, axis=0, keepdims=True)
        dsc_ref[...] += jnp.sum(dh * xg, axis=0, keepdims=True)
        dxg = dh * (1.0 + sc_ref[...])
        dg_ref[...] += jnp.sum(dxg * xn, axis=0, keepdims=True)
        dxn = dxg * gv
        dx_ref[...] = dres_ref[...] + r * (dxn - xn * jnp.mean(dxn * xn, axis=-1, keepdims=True))

    return pl.pallas_call(
        body, grid=(S // tm,),
        in_specs=[_rows(tm, Dm)] * (2 + nh) + [_vec(Dm), _vec(Dm)],
        out_specs=[_rows(tm, Dm), _vec(Dm), _vec(Dm), _vec(Dm)],
        out_shape=[SDS((S, Dm), F32), SDS((1, Dm), F32), SDS((1, Dm), F32), SDS((1, Dm), F32)],
        compiler_params=_cparams(("arbitrary",)), name=name)(x, *dh_list, dres, g, sc)


def _resid_fwd(x, val, g, name):
    S, Dm = x.shape
    tm = _tm(S)

    def body(x_ref, v_ref, g_ref, o_ref):
        o_ref[...] = x_ref[...] + g_ref[...] * v_ref[...]

    return pl.pallas_call(
        body, grid=(S // tm,), in_specs=[_rows(tm, Dm), _rows(tm, Dm), _vec(Dm)],
        out_specs=_rows(tm, Dm), out_shape=SDS((S, Dm), F32),
        compiler_params=_cparams(("parallel",)), name=name)(x, val, g)


def _resid_bwd(dx, val, g, name):
    S, Dm = dx.shape
    tm = _tm(S)

    def body(dx_ref, v_ref, g_ref, dg_ref, dv_ref):
        @pl.when(pl.program_id(0) == 0)
        def _():
            dg_ref[...] = jnp.zeros_like(dg_ref)

        dxv = dx_ref[...]
        dg_ref[...] += jnp.sum(dxv * v_ref[...], axis=0, keepdims=True)
        dv_ref[...] = (dxv * g_ref[...]).astype(dv_ref.dtype)

    return pl.pallas_call(
        body, grid=(S // tm,), in_specs=[_rows(tm, Dm), _rows(tm, Dm), _vec(Dm)],
        out_specs=[_vec(Dm), _rows(tm, Dm)], out_shape=[SDS((1, Dm), F32), SDS((S, Dm), BF16)],
        compiler_params=_cparams(("arbitrary",)), name=name)(dx, val, g)


def _merge_fwd(y, gf, name):
    S = y.shape[0]
    tm = _tm(S)
    W = 3 * D_MODEL

    def body(y_ref, g_ref, o_ref):
        acc = None
        for k in range(3):
            sl = slice(k * D_MODEL, (k + 1) * D_MODEL)
            t = _sigmoid(g_ref[:, sl]) * y_ref[:, sl]
            acc = t if acc is None else acc + t
        o_ref[...] = acc.astype(o_ref.dtype)

    return pl.pallas_call(
        body, grid=(S // tm,), in_specs=[_rows(tm, W), _rows(tm, W)],
        out_specs=_rows(tm, D_MODEL), out_shape=SDS((S, D_MODEL), BF16),
        compiler_params=_cparams(("parallel",)), name=name)(y, gf)


def _merge_bwd(dm, y, gf, name):
    S = y.shape[0]
    tm = _tm(S)
    W = 3 * D_MODEL

    def body(dm_ref, y_ref, g_ref, dy_ref, dg_ref):
        dmv = dm_ref[...]
        for k in range(3):
            sl = slice(k * D_MODEL, (k + 1) * D_MODEL)
            sg = _sigmoid(g_ref[:, sl])
            dy_ref[:, sl] = (dmv * sg).astype(dy_ref.dtype)
            dg_ref[:, sl] = (dmv * y_ref[:, sl] * (sg * (1.0 - sg))).astype(dg_ref.dtype)

    return pl.pallas_call(
        body, grid=(S // tm,), in_specs=[_rows(tm, D_MODEL), _rows(tm, W), _rows(tm, W)],
        out_specs=[_rows(tm, W), _rows(tm, W)], out_shape=[SDS((S, W), BF16), SDS((S, W), BF16)],
        compiler_params=_cparams(("parallel",)), name=name)(dm, y, gf)


def _swiglu_fwd(u, name):
    S = u.shape[0]
    tm = _tm(S)

    def body(g_ref, u_ref, a_ref):
        gv = g_ref[...]
        a_ref[...] = (gv * _sigmoid(gv) * u_ref[...]).astype(a_ref.dtype)

    return pl.pallas_call(
        body, grid=(S // tm,), in_specs=[_rows(tm, FFN_H, 0), _rows(tm, FFN_H, 1)],
        out_specs=_rows(tm, FFN_H), out_shape=SDS((S, FFN_H), BF16),
        compiler_params=_cparams(("parallel",)), name=name)(u, u)


def _swiglu_bwd(da, u, name):
    S = u.shape[0]
    tm = _tm(S)

    def body(da_ref, g_ref, u_ref, du_ref):
        dav = da_ref[...]
        gv = g_ref[...]
        sg = _sigmoid(gv)
        du_ref[:, 0:FFN_H] = (dav * u_ref[...] * (sg * (1.0 + gv * (1.0 - sg)))).astype(du_ref.dtype)
        du_ref[:, FFN_H:2 * FFN_H] = (dav * (gv * sg)).astype(du_ref.dtype)

    return pl.pallas_call(
        body, grid=(S // tm,), in_specs=[_rows(tm, FFN_H), _rows(tm, FFN_H, 0), _rows(tm, FFN_H, 1)],
        out_specs=_rows(tm, 2 * FFN_H), out_shape=SDS((S, 2 * FFN_H), BF16),
        compiler_params=_cparams(("parallel",)), name=name)(da, u, u)


def _final_loss(x, g, target, name):
    S, Dm = x.shape
    tm = _tm(S)

    def body(x_ref, g_ref, t_ref, loss_ref, dx_ref, dg_ref):
        @pl.when(pl.program_id(0) == 0)
        def _():
            loss_ref[...] = jnp.zeros_like(loss_ref)
            dg_ref[...] = jnp.zeros_like(dg_ref)

        xv = x_ref[...]
        gv = g_ref[...]
        r = lax.rsqrt(jnp.mean(xv * xv, axis=-1, keepdims=True) + EPS)
        xn = xv * r
        err = xn * gv - t_ref[...]
        row = jnp.mean(err * err, axis=-1, keepdims=True)
        loss_ref[...] += 0.5 * jnp.sum(row, axis=0, keepdims=True)
        dy = err * (1.0 / Dm)
        dg_ref[...] += jnp.sum(dy * xn, axis=0, keepdims=True)
        dxn = dy * gv
        dx_ref[...] = r * (dxn - xn * jnp.mean(dxn * xn, axis=-1, keepdims=True))

    return pl.pallas_call(
        body, grid=(S // tm,), in_specs=[_rows(tm, Dm), _vec(Dm), _rows(tm, Dm)],
        out_specs=[pl.BlockSpec((1, 1), lambda i: (0, 0)), _rows(tm, Dm), _vec(Dm)],
        out_shape=[SDS((1, 1), F32), SDS((S, Dm), F32), SDS((1, Dm), F32)],
        compiler_params=_cparams(("arbitrary",)), name=name)(x, g, target)


def _band_softmax(qg, kg, bias, sink, valid):
    s = lax.dot_general(qg, kg, (((1,), (1,)), ((), ())), preferred_element_type=F32)
    s = jnp.where(valid, s + bias, NEG_INF)
    m = jnp.maximum(jnp.max(s, axis=-1, keepdims=True), sink)
    e = jnp.exp(s - m)
    es = jnp.exp(sink - m)
    l = jnp.sum(e, axis=-1, keepdims=True) + es
    return e / l, es / l


def _band_attn_fwd(q, k, v, bias, sink, *, G, P, kvoff, name):
    S = q.shape[0]
    ng = q.shape[1] // (G * HEAD_DIM)
    band = (P + 1) * CHUNK
    pad = P * CHUNK
    nc = S // CHUNK

    def body(q_ref, k_ref, v_ref, b_ref, s_ref, o_ref, kp, vp):
        kp[0:pad, :] = jnp.zeros((pad, LANE), BF16)
        vp[0:pad, :] = jnp.zeros((pad, LANE), BF16)
        kp[pad:pad + S, :] = k_ref[...]
        vp[pad:pad + S, :] = v_ref[...]
        col = lax.broadcasted_iota(jnp.int32, (CHUNK, band), 1)

        def step(n, carry):
            r = pl.multiple_of(n * CHUNK, CHUNK)
            qn = q_ref[pl.ds(r, CHUNK), :]
            kb = kp[pl.ds(r, band), :]
            vb = vp[pl.ds(r, band), :]
            valid = col >= (P - n) * CHUNK
            for g in range(G):
                ko = kvoff(g) * HEAD_DIM
                qg = qn[:, g * HEAD_DIM:(g + 1) * HEAD_DIM] * 0.125
                p, _ = _band_softmax(qg, kb[:, ko:ko + HEAD_DIM], b_ref[g], s_ref[g, 0:1, 0:1], valid)
                og = jnp.dot(p.astype(BF16), vb[:, ko:ko + HEAD_DIM], preferred_element_type=F32)
                o_ref[pl.ds(r, CHUNK), g * HEAD_DIM:(g + 1) * HEAD_DIM] = og.astype(o_ref.dtype)
            return carry

        lax.fori_loop(0, nc, step, 0, unroll=min(BAND_UNROLL_FWD, nc))

    GW = G * HEAD_DIM
    return pl.pallas_call(
        body, grid=(ng,),
        in_specs=[pl.BlockSpec((S, GW), lambda i: (0, i)), pl.BlockSpec((S, LANE), lambda i: (0, i)),
                  pl.BlockSpec((S, LANE), lambda i: (0, i)),
                  pl.BlockSpec((G, CHUNK, band), lambda i: (i, 0, 0)),
                  pl.BlockSpec((G, 8, LANE), lambda i: (i, 0, 0))],
        out_specs=pl.BlockSpec((S, GW), lambda i: (0, i)),
        out_shape=SDS((S, ng * GW), BF16),
        scratch_shapes=[pltpu.VMEM((S + pad, LANE), BF16), pltpu.VMEM((S + pad, LANE), BF16)],
        compiler_params=_cparams(("parallel",)), name=name)(q, k, v, bias, sink)


def _band_attn_bwd(q, k, v, bias, sink, do, *, G, P, kvoff, name):
    S = q.shape[0]
    ng = q.shape[1] // (G * HEAD_DIM)
    band = (P + 1) * CHUNK
    pad = P * CHUNK
    nc = S // CHUNK
    TN = (((0,), (0,)), ((), ()))

    def body(q_ref, k_ref, v_ref, b_ref, s_ref, do_ref, dq_ref, dk_ref, dv_ref, db_ref, dsk_ref,
             kp, vp, dkp, dvp):
        kp[0:pad, :] = jnp.zeros((pad, LANE), BF16)
        vp[0:pad, :] = jnp.zeros((pad, LANE), BF16)
        kp[pad:pad + S, :] = k_ref[...]
        vp[pad:pad + S, :] = v_ref[...]
        dkp[...] = jnp.zeros_like(dkp)
        dvp[...] = jnp.zeros_like(dvp)
        db_ref[...] = jnp.zeros_like(db_ref)
        col = lax.broadcasted_iota(jnp.int32, (CHUNK, band), 1)

        def step(n, dsink):
            r = pl.multiple_of(n * CHUNK, CHUNK)
            qn = q_ref[pl.ds(r, CHUNK), :]
            don = do_ref[pl.ds(r, CHUNK), :]
            kb = kp[pl.ds(r, band), :]
            vb = vp[pl.ds(r, band), :]
            valid = col >= (P - n) * CHUNK
            new = []
            for g in range(G):
                ko = kvoff(g) * HEAD_DIM
                lanes = slice(g * HEAD_DIM, (g + 1) * HEAD_DIM)
                qg = qn[:, lanes] * 0.125
                kg = kb[:, ko:ko + HEAD_DIM]
                dog = don[:, lanes]
                p, ps = _band_softmax(qg, kg, b_ref[g], s_ref[g, 0:1, 0:1], valid)
                dp = lax.dot_general(dog, vb[:, ko:ko + HEAD_DIM], (((1,), (1,)), ((), ())),
                                     preferred_element_type=F32)
                delta = jnp.sum(p * dp, axis=-1, keepdims=True)
                ds = p * (dp - delta)
                new.append(dsink[g] - jnp.sum(ps * delta, axis=0, keepdims=True))
                db_ref[g] += ds
                dsb = ds.astype(BF16)
                dq = jnp.dot(dsb, kg, preferred_element_type=F32) * 0.125
                dq_ref[pl.ds(r, CHUNK), lanes] = dq.astype(dq_ref.dtype)
                dkp[pl.ds(r, band), ko:ko + HEAD_DIM] += lax.dot_general(
                    dsb, qg, TN, preferred_element_type=F32)
                dvp[pl.ds(r, band), ko:ko + HEAD_DIM] += lax.dot_general(
                    p.astype(BF16), dog, TN, preferred_element_type=F32)
            return tuple(new)

        dsink = lax.fori_loop(0, nc, step, tuple(jnp.zeros((1, 1), F32) for _ in range(G)),
                              unroll=min(BAND_UNROLL_BWD, nc))
        for g in range(G):
            dsk_ref[g] = jnp.broadcast_to(dsink[g], (8, LANE))
        dk_ref[...] = dkp[pad:pad + S, :].astype(dk_ref.dtype)
        dv_ref[...] = dvp[pad:pad + S, :].astype(dv_ref.dtype)

    GW = G * HEAD_DIM
    qs = pl.BlockSpec((S, GW), lambda i: (0, i))
    ks = pl.BlockSpec((S, LANE), lambda i: (0, i))
    bs = pl.BlockSpec((G, CHUNK, band), lambda i: (i, 0, 0))
    ss = pl.BlockSpec((G, 8, LANE), lambda i: (i, 0, 0))
    return pl.pallas_call(
        body, grid=(ng,), in_specs=[qs, ks, ks, bs, ss, qs],
        out_specs=[qs, ks, ks, bs, ss],
        out_shape=[SDS((S, ng * GW), BF16), SDS((S, ng * LANE), BF16), SDS((S, ng * LANE), BF16),
                   SDS((ng * G, CHUNK, band), F32), SDS((ng * G, 8, LANE), F32)],
        scratch_shapes=[pltpu.VMEM((S + pad, LANE), BF16), pltpu.VMEM((S + pad, LANE), BF16),
                        pltpu.VMEM((S + pad, LANE), F32), pltpu.VMEM((S + pad, LANE), F32)],
        compiler_params=_cparams(("parallel",)), name=name)(q, k, v, bias, sink, do)


PAIR = 2 * CHUNK


def _bandT_softmax(kg, qTg, bias, sink, valid):
    s = jnp.dot(kg, qTg, preferred_element_type=F32)
    s = jnp.where(valid, s + bias, NEG_INF)
    m = jnp.maximum(jnp.max(s, axis=0, keepdims=True), sink)
    e = jnp.exp(s - m)
    es = jnp.exp(sink - m)
    inv = 1.0 / (jnp.sum(e, axis=0, keepdims=True) + es)
    return e * inv, es * inv


def _pad_copy_rows(dst, src, pad, S):
    dst[:, 0:pad, :] = jnp.zeros((dst.shape[0], pad, dst.shape[2]), dst.dtype)
    dst[:, pad:pad + S, :] = src[...]


def _pad_copy_lanes(dst, src, pad, S):
    dst[:, 0:pad] = jnp.zeros((dst.shape[0], pad), dst.dtype)
    dst[:, pad:pad + S] = src[...]


def _bandT_fwd(qT, k_h, vT, bias, sink, *, GQ, GK, P, kvoff, name, comm=None):
    S = qT.shape[1]
    ng = qT.shape[0] // (GQ * HEAD_DIM)
    BU = (P + 2) * CHUNK
    pad = P * CHUNK
    npair = S // PAIR

    def body(qT_ref, k_ref, vT_ref, b_ref, s_ref, oT_ref, kp, vTp):
        _pad_copy_rows(kp, k_ref, pad, S)
        _pad_copy_lanes(vTp, vT_ref, pad, S)
        rowi = lax.broadcasted_iota(jnp.int32, (BU, PAIR), 0)

        def step(n2, carry):
            r = pl.multiple_of(n2 * PAIR, PAIR)
            valid = rowi >= (P - 2 * n2) * CHUNK
            for g in range(GQ):
                kv = kvoff(g)
                hs = slice(g * HEAD_DIM, (g + 1) * HEAD_DIM)
                kvs = slice(kv * HEAD_DIM, (kv + 1) * HEAD_DIM)
                qTg = qT_ref[hs, pl.ds(r, PAIR)] * 0.125
                p, _ = _bandT_softmax(kp[kv, pl.ds(r, BU), :], qTg, b_ref[g], s_ref[g, 0:1, :], valid)
                oTg = jnp.dot(vTp[kvs, pl.ds(r, BU)], p.astype(BF16), preferred_element_type=F32)
                oT_ref[hs, pl.ds(r, PAIR)] = oTg.astype(oT_ref.dtype)
            return carry

        lax.fori_loop(0, npair, step, 0, unroll=min(2, npair))

    res, got = _call_hosting(
        body, comm=comm, grid=(ng,),
        in_specs=[pl.BlockSpec((GQ * HEAD_DIM, S), lambda i: (i, 0)),
                  pl.BlockSpec((GK, S, HEAD_DIM), lambda i: (i, 0, 0)),
                  pl.BlockSpec((GK * HEAD_DIM, S), lambda i: (i, 0)),
                  pl.BlockSpec((GQ, BU, PAIR), lambda i: (i, 0, 0)),
                  pl.BlockSpec((GQ, 8, LANE), lambda i: (i, 0, 0))],
        out_specs=[pl.BlockSpec((GQ * HEAD_DIM, S), lambda i: (i, 0))],
        out_shape=[SDS((ng * GQ * HEAD_DIM, S), BF16)],
        scratch_shapes=[pltpu.VMEM((GK, S + pad, HEAD_DIM), BF16), pltpu.VMEM((GK * HEAD_DIM, S + pad), BF16)],
        name=name, args=(qT, k_h, vT, bias, sink))
    return res[0], got


def _bandT_bwd(qT, q_h, k_h, kT, v_h, doT, do_h, bias, sink, *, GQ, GK, P, kvoff, name, comm=None):
    S = qT.shape[1]
    ng = qT.shape[0] // (GQ * HEAD_DIM)
    BU = (P + 2) * CHUNK
    pad = P * CHUNK
    npair = S // PAIR

    def body(qT_ref, q_ref, k_ref, kT_ref, v_ref, doT_ref, do_ref, b_ref, s_ref,
             dqT_ref, dk_ref, dv_ref, db_ref, dsk_ref, kp, kTp, vp, dkp, dvp):
        _pad_copy_rows(kp, k_ref, pad, S)
        _pad_copy_rows(vp, v_ref, pad, S)
        _pad_copy_lanes(kTp, kT_ref, pad, S)
        dkp[...] = jnp.zeros_like(dkp)
        dvp[...] = jnp.zeros_like(dvp)
        db_ref[...] = jnp.zeros_like(db_ref)
        rowi = lax.broadcasted_iota(jnp.int32, (BU, PAIR), 0)

        def step(n2, dsink):
            r = pl.multiple_of(n2 * PAIR, PAIR)
            valid = rowi >= (P - 2 * n2) * CHUNK
            new = []
            for g in range(GQ):
                kv = kvoff(g)
                hs = slice(g * HEAD_DIM, (g + 1) * HEAD_DIM)
                kvs = slice(kv * HEAD_DIM, (kv + 1) * HEAD_DIM)
                qTg = qT_ref[hs, pl.ds(r, PAIR)] * 0.125
                p, ps = _bandT_softmax(kp[kv, pl.ds(r, BU), :], qTg, b_ref[g], s_ref[g, 0:1, :], valid)
                dp = jnp.dot(vp[kv, pl.ds(r, BU), :], doT_ref[hs, pl.ds(r, PAIR)], preferred_element_type=F32)
                delta = jnp.sum(p * dp, axis=0, keepdims=True)
                ds = p * (dp - delta)
                new.append(dsink[g] - ps * delta)
                db_ref[g] += ds
                dsb = ds.astype(BF16)
                dq = jnp.dot(kTp[kvs, pl.ds(r, BU)], dsb, preferred_element_type=F32) * 0.125
                dqT_ref[hs, pl.ds(r, PAIR)] = dq.astype(dqT_ref.dtype)
                dkp[kv, pl.ds(r, BU), :] += jnp.dot(dsb, q_ref[g, pl.ds(r, PAIR), :] * 0.125,
                                                    preferred_element_type=F32)
                dvp[kv, pl.ds(r, BU), :] += jnp.dot(p.astype(BF16), do_ref[g, pl.ds(r, PAIR), :],
                                                    preferred_element_type=F32)
            return tuple(new)

        dsink = lax.fori_loop(0, npair, step, tuple(jnp.zeros((1, PAIR), F32) for _ in range(GQ)))
        for g in range(GQ):
            dsk_ref[g] = jnp.broadcast_to(jnp.sum(dsink[g], axis=1, keepdims=True), (8, LANE))
        dk_ref[...] = dkp[:, pad:pad + S, :].astype(dk_ref.dtype)
        dv_ref[...] = dvp[:, pad:pad + S, :].astype(dv_ref.dtype)

    qTs = pl.BlockSpec((GQ * HEAD_DIM, S), lambda i: (i, 0))
    qhs = pl.BlockSpec((GQ, S, HEAD_DIM), lambda i: (i, 0, 0))
    khs = pl.BlockSpec((GK, S, HEAD_DIM), lambda i: (i, 0, 0))
    kTs = pl.BlockSpec((GK * HEAD_DIM, S), lambda i: (i, 0))
    bs = pl.BlockSpec((GQ, BU, PAIR), lambda i: (i, 0, 0))
    ss = pl.BlockSpec((GQ, 8, LANE), lambda i: (i, 0, 0))
    nkv = ng * GK
    return _call_hosting(
        body, comm=comm, grid=(ng,), in_specs=[qTs, qhs, khs, kTs, khs, qTs, qhs, bs, ss],
        out_specs=[qTs, khs, khs, bs, ss],
        out_shape=[SDS((ng * GQ * HEAD_DIM, S), BF16), SDS((nkv, S, HEAD_DIM), BF16), SDS((nkv, S, HEAD_DIM), BF16),
                   SDS((ng * GQ, BU, PAIR), F32), SDS((ng * GQ, 8, LANE), F32)],
        scratch_shapes=[pltpu.VMEM((GK, S + pad, HEAD_DIM), BF16), pltpu.VMEM((GK * HEAD_DIM, S + pad), BF16),
                        pltpu.VMEM((GK, S + pad, HEAD_DIM), BF16),
                        pltpu.VMEM((GK, S + pad, HEAD_DIM), F32), pltpu.VMEM((GK, S + pad, HEAD_DIM), F32)],
        name=name, args=(qT, q_h, k_h, kT, v_h, doT, do_h, bias, sink))


def _pair_table(tab):
    t = jnp.transpose(tab, (0, 2, 1))
    lo = jnp.pad(t, ((0, 0), (0, CHUNK), (0, 0)), constant_values=NEG_INF)
    hi = jnp.pad(t, ((0, 0), (CHUNK, 0), (0, 0)), constant_values=NEG_INF)
    return jnp.concatenate([lo, hi], axis=2)


def _unpair_table(d):
    band = d.shape[1] - CHUNK
    return jnp.transpose(d[:, 0:band, 0:CHUNK] + d[:, CHUNK:CHUNK + band, CHUNK:PAIR], (0, 2, 1))


def _heads(a, n):
    return jnp.transpose(a.reshape(a.shape[0], n, HEAD_DIM), (1, 0, 2))


def _unheads(a):
    return jnp.transpose(a, (1, 0, 2)).reshape(a.shape[1], a.shape[0] * HEAD_DIM)


def _fox_logits(qg, kj, cq, ck, r, c, row, col):
    s = lax.dot_general(qg, kj, (((1,), (1,)), ((), ())), preferred_element_type=F32)
    s = s + cq - ck
    return jnp.where(c + col <= r + row, s, NEG_INF)


def _fox_fwd(q, k, v, cc, cr, name):
    S = q.shape[0]
    npair = q.shape[1] // LANE
    BQ, BK = min(FOX_BQ, S), min(FOX_BK, S)
    nq = S // BQ
    heads = [slice(g * HEAD_DIM, (g + 1) * HEAD_DIM) for g in range(2)]

    def body(q_ref, k_ref, v_ref, cc_ref, cr_ref, o_ref, lse_ref):
        row = lax.broadcasted_iota(jnp.int32, (BQ, BK), 0)
        col = lax.broadcasted_iota(jnp.int32, (BQ, BK), 1)

        def qstep(i, carry):
            r = pl.multiple_of(i * BQ, BQ)
            qs = [q_ref[pl.ds(r, BQ), hl] * 0.125 for hl in heads]
            cqs = [cc_ref[g, pl.ds(r, BQ), :] for g in range(2)]

            def kstep(j, st):
                c = pl.multiple_of(j * BK, BK)
                new = []
                for g, hl in enumerate(heads):
                    m, l, acc = st[g]
                    s = _fox_logits(qs[g], k_ref[pl.ds(c, BK), hl], cqs[g], cr_ref[g, :, pl.ds(c, BK)],
                                    r, c, row, col)
                    mn = jnp.maximum(m, jnp.max(s, axis=-1, keepdims=True))
                    al = jnp.exp(m - mn)
                    e = jnp.exp(s - mn)
                    l = al * l + jnp.sum(e, axis=-1, keepdims=True)
                    acc = al * acc + jnp.dot(e.astype(BF16), v_ref[pl.ds(c, BK), hl],
                                             preferred_element_type=F32)
                    new.append((mn, l, acc))
                return tuple(new)

            init = (jnp.full((BQ, 1), NEG_INF, F32), jnp.zeros((BQ, 1), F32), jnp.zeros((BQ, HEAD_DIM), F32))
            st = lax.fori_loop(0, (r + BQ + BK - 1) // BK, kstep, (init, init))
            for g, hl in enumerate(heads):
                m, l, acc = st[g]
                o_ref[pl.ds(r, BQ), hl] = (acc / l).astype(o_ref.dtype)
                lse_ref[g, pl.ds(r, BQ), :] = m + jnp.log(l)
            return carry

        lax.fori_loop(0, nq, qstep, 0)

    blk = pl.BlockSpec((S, LANE), lambda i: (0, i))
    ccs = pl.BlockSpec((2, S, 1), lambda i: (i, 0, 0))
    crs = pl.BlockSpec((2, 1, S), lambda i: (i, 0, 0))
    return pl.pallas_call(
        body, grid=(npair,), in_specs=[blk, blk, blk, ccs, crs], out_specs=[blk, ccs],
        out_shape=[SDS((S, npair * LANE), BF16), SDS((2 * npair, S, 1), F32)],
        compiler_params=_cparams(("parallel",)), name=name)(q, k, v, cc, cr)


def _fox_bwd(q, k, v, cc, cr, o, do, lse, name):
    S = q.shape[0]
    npair = q.shape[1] // LANE
    BQ, BK = min(FOX_BQ, S), min(FOX_BK, S)
    nq = S // BQ
    heads = [slice(g * HEAD_DIM, (g + 1) * HEAD_DIM) for g in range(2)]
    TN = (((0,), (0,)), ((), ()))

    def body(q_ref, k_ref, v_ref, cc_ref, cr_ref, o_ref, do_ref, lse_ref,
             dq_ref, dk_ref, dv_ref, dcr_ref, dcc_ref, dka, dva):
        dka[...] = jnp.zeros_like(dka)
        dva[...] = jnp.zeros_like(dva)
        dcr_ref[...] = jnp.zeros_like(dcr_ref)
        row = lax.broadcasted_iota(jnp.int32, (BQ, BK), 0)
        col = lax.broadcasted_iota(jnp.int32, (BQ, BK), 1)

        def qstep(i, carry):
            r = pl.multiple_of(i * BQ, BQ)
            qs = [q_ref[pl.ds(r, BQ), hl] * 0.125 for hl in heads]
            dos = [do_ref[pl.ds(r, BQ), hl] for hl in heads]
            deltas = [jnp.sum(dos[g].astype(F32) * o_ref[pl.ds(r, BQ), hl].astype(F32), axis=-1, keepdims=True)
                      for g, hl in enumerate(heads)]
            cqs = [cc_ref[g, pl.ds(r, BQ), :] for g in range(2)]
            lses = [lse_ref[g, pl.ds(r, BQ), :] for g in range(2)]

            def kstep(j, st):
                c = pl.multiple_of(j * BK, BK)
                new = []
                for g, hl in enumerate(heads):
                    dq, rs = st[g]
                    kj = k_ref[pl.ds(c, BK), hl]
                    s = _fox_logits(qs[g], kj, cqs[g], cr_ref[g, :, pl.ds(c, BK)], r, c, row, col)
                    p = jnp.exp(s - lses[g])
                    dp = lax.dot_general(dos[g], v_ref[pl.ds(c, BK), hl], (((1,), (1,)), ((), ())),
                                         preferred_element_type=F32)
                    ds = p * (dp - deltas[g])
                    dcr_ref[g, :, pl.ds(c, BK)] -= jnp.sum(ds, axis=0, keepdims=True)
                    dsb = ds.astype(BF16)
                    dka[pl.ds(c, BK), hl] += lax.dot_general(dsb, qs[g], TN, preferred_element_type=F32)
                    dva[pl.ds(c, BK), hl] += lax.dot_general(p.astype(BF16), dos[g], TN,
                                                            preferred_element_type=F32)
                    new.append((dq + jnp.dot(dsb, kj, preferred_element_type=F32),
                                rs + jnp.sum(ds, axis=-1, keepdims=True)))
                return tuple(new)

            init = (jnp.zeros((BQ, HEAD_DIM), F32), jnp.zeros((BQ, 1), F32))
            st = lax.fori_loop(0, (r + BQ + BK - 1) // BK, kstep, (init, init))
            for g, hl in enumerate(heads):
                dq_ref[pl.ds(r, BQ), hl] = (st[g][0] * 0.125).astype(dq_ref.dtype)
                dcc_ref[g, pl.ds(r, BQ), :] = st[g][1]
            return carry

        lax.fori_loop(0, nq, qstep, 0)
        dk_ref[...] = dka[...].astype(dk_ref.dtype)
        dv_ref[...] = dva[...].astype(dv_ref.dtype)

    blk = pl.BlockSpec((S, LANE), lambda i: (0, i))
    ccs = pl.BlockSpec((2, S, 1), lambda i: (i, 0, 0))
    crs = pl.BlockSpec((2, 1, S), lambda i: (i, 0, 0))
    return pl.pallas_call(
        body, grid=(npair,), in_specs=[blk, blk, blk, ccs, crs, blk, blk, ccs],
        out_specs=[blk, blk, blk, crs, ccs],
        out_shape=[SDS((S, npair * LANE), BF16)] * 3 + [SDS((2 * npair, 1, S), F32), SDS((2 * npair, S, 1), F32)],
        scratch_shapes=[pltpu.VMEM((S, LANE), F32), pltpu.VMEM((S, LANE), F32)],
        compiler_params=_cparams(("parallel",)), name=name)(q, k, v, cc, cr, o, do, lse)


def _foxT_logits(kj, qTg, cq, ck, r, c, rowi, coli):
    s = jnp.dot(kj, qTg, preferred_element_type=F32)
    s = s + cq - ck
    return jnp.where(c + rowi <= r + coli, s, NEG_INF)


def _foxT_fwd(qT, k_h, vT, ck, cq, name, comm=None):
    S = qT.shape[1]
    npair = qT.shape[0] // LANE
    BQ, BK = min(FOX_BQ, S), min(FOX_BK, S)
    nq = S // BQ
    heads = [slice(g * HEAD_DIM, (g + 1) * HEAD_DIM) for g in range(2)]

    def body(qT_ref, k_ref, vT_ref, ck_ref, cq_ref, oT_ref, lse_ref):
        rowi = lax.broadcasted_iota(jnp.int32, (BK, BQ), 0)
        coli = lax.broadcasted_iota(jnp.int32, (BK, BQ), 1)

        def qstep(i, carry):
            r = pl.multiple_of(i * BQ, BQ)
            qs = [qT_ref[hs, pl.ds(r, BQ)] * 0.125 for hs in heads]
            cqs = [cq_ref[g, :, pl.ds(r, BQ)] for g in range(2)]

            def kstep(j, st):
                c = pl.multiple_of(j * BK, BK)
                new = []
                for g, hs in enumerate(heads):
                    m, l, acc = st[g]
                    s = _foxT_logits(k_ref[g, pl.ds(c, BK), :], qs[g], cqs[g], ck_ref[g, pl.ds(c, BK), :],
                                     r, c, rowi, coli)
                    mn = jnp.maximum(m, jnp.max(s, axis=0, keepdims=True))
                    al = jnp.exp(m - mn)
                    e = jnp.exp(s - mn)
                    l = al * l + jnp.sum(e, axis=0, keepdims=True)
                    acc = al * acc + jnp.dot(vT_ref[hs, pl.ds(c, BK)], e.astype(BF16), preferred_element_type=F32)
                    new.append((mn, l, acc))
                return tuple(new)

            init = (jnp.full((1, BQ), NEG_INF, F32), jnp.zeros((1, BQ), F32), jnp.zeros((HEAD_DIM, BQ), F32))
            st = lax.fori_loop(0, (r + BQ + BK - 1) // BK, kstep, (init, init))
            for g, hs in enumerate(heads):
                m, l, acc = st[g]
                oT_ref[hs, pl.ds(r, BQ)] = (acc * (1.0 / l)).astype(oT_ref.dtype)
                lse_ref[g, :, pl.ds(r, BQ)] = m + jnp.log(l)
            return carry

        lax.fori_loop(0, nq, qstep, 0)

    fT = pl.BlockSpec((LANE, S), lambda i: (i, 0))
    hm = pl.BlockSpec((2, S, HEAD_DIM), lambda i: (i, 0, 0))
    col = pl.BlockSpec((2, S, 1), lambda i: (i, 0, 0))
    rw = pl.BlockSpec((2, 1, S), lambda i: (i, 0, 0))
    return _call_hosting(
        body, comm=comm, grid=(npair,), in_specs=[fT, hm, fT, col, rw], out_specs=[fT, rw],
        out_shape=[SDS((npair * LANE, S), BF16), SDS((2 * npair, 1, S), F32)], scratch_shapes=[],
        name=name, args=(qT, k_h, vT, ck, cq))


def _foxT_bwd(qT, q_aug, k_h, kT, v_h, ck, cq, oT, doT, do_h, lse, name, comm=None):
    S = qT.shape[1]
    npair = qT.shape[0] // LANE
    BQ, BK = min(FOX_BQ, S), min(FOX_BK, S)
    nq = S // BQ
    heads = [slice(g * HEAD_DIM, (g + 1) * HEAD_DIM) for g in range(2)]

    def body(qT_ref, qa_ref, k_ref, kT_ref, v_ref, ck_ref, cq_ref, oT_ref, doT_ref, do_ref, lse_ref,
             dqT_ref, dk_ref, dv_ref, dck_ref, dcq_ref, dka, dva):
        dka[...] = jnp.zeros_like(dka)
        dva[...] = jnp.zeros_like(dva)
        rowi = lax.broadcasted_iota(jnp.int32, (BK, BQ), 0)
        coli = lax.broadcasted_iota(jnp.int32, (BK, BQ), 1)

        def qstep(i, carry):
            r = pl.multiple_of(i * BQ, BQ)
            qs = [qT_ref[hs, pl.ds(r, BQ)] * 0.125 for hs in heads]
            dos = [doT_ref[hs, pl.ds(r, BQ)] for hs in heads]
            deltas = [jnp.sum(dos[g].astype(F32) * oT_ref[hs, pl.ds(r, BQ)].astype(F32), axis=0, keepdims=True)
                      for g, hs in enumerate(heads)]
            cqs = [cq_ref[g, :, pl.ds(r, BQ)] for g in range(2)]
            lses = [lse_ref[g, :, pl.ds(r, BQ)] for g in range(2)]

            def kstep(j, st):
                c = pl.multiple_of(j * BK, BK)
                new = []
                for g, hs in enumerate(heads):
                    dq, rs = st[g]
                    s = _foxT_logits(k_ref[g, pl.ds(c, BK), :], qs[g], cqs[g], ck_ref[g, pl.ds(c, BK), :],
                                     r, c, rowi, coli)
                    p = jnp.exp(s - lses[g])
                    dp = jnp.dot(v_ref[g, pl.ds(c, BK), :], dos[g], preferred_element_type=F32)
                    ds = p * (dp - deltas[g])
                    dsb = ds.astype(BF16)
                    dka[g, pl.ds(c, BK), :] += jnp.dot(dsb, qa_ref[g, pl.ds(r, BQ), :], preferred_element_type=F32)
                    dva[g, pl.ds(c, BK), :] += jnp.dot(p.astype(BF16), do_ref[g, pl.ds(r, BQ), :],
                                                      preferred_element_type=F32)
                    new.append((dq + jnp.dot(kT_ref[hs, pl.ds(c, BK)], dsb, preferred_element_type=F32),
                                rs + jnp.sum(dsb.astype(F32), axis=0, keepdims=True)))
                return tuple(new)

            init = (jnp.zeros((HEAD_DIM, BQ), F32), jnp.zeros((1, BQ), F32))
            st = lax.fori_loop(0, (r + BQ + BK - 1) // BK, kstep, (init, init))
            for g, hs in enumerate(heads):
                dqT_ref[hs, pl.ds(r, BQ)] = (st[g][0] * 0.125).astype(dqT_ref.dtype)
                dcq_ref[g, :, pl.ds(r, BQ)] = st[g][1]
            return carry

        lax.fori_loop(0, nq, qstep, 0)
        dk_ref[...] = dka[:, :, 0:HEAD_DIM].astype(dk_ref.dtype)
        dck_ref[...] = -dka[:, :, HEAD_DIM:HEAD_DIM + 1]
        dv_ref[...] = dva[...].astype(dv_ref.dtype)

    fT = pl.BlockSpec((LANE, S), lambda i: (i, 0))
    hm = pl.BlockSpec((2, S, HEAD_DIM), lambda i: (i, 0, 0))
    hma = pl.BlockSpec((2, S, LANE), lambda i: (i, 0, 0))
    col = pl.BlockSpec((2, S, 1), lambda i: (i, 0, 0))
    rw = pl.BlockSpec((2, 1, S), lambda i: (i, 0, 0))
    nh = 2 * npair
    return _call_hosting(
        body, comm=comm, grid=(npair,), in_specs=[fT, hma, hm, fT, hm, col, rw, fT, fT, hm, rw],
        out_specs=[fT, hm, hm, col, rw],
        out_shape=[SDS((npair * LANE, S), BF16), SDS((nh, S, HEAD_DIM), BF16), SDS((nh, S, HEAD_DIM), BF16),
                   SDS((nh, S, 1), F32), SDS((nh, 1, S), F32)],
        scratch_shapes=[pltpu.VMEM((2, S, LANE), F32), pltpu.VMEM((2, S, HEAD_DIM), F32)],
        name=name, args=(qT, q_aug, k_h, kT, v_h, ck, cq, oT, doT, do_h, lse))


def _split3(x):
    hi = x.astype(BF16)
    r1 = x - hi.astype(F32)
    mid = r1.astype(BF16)
    lo = (r1 - mid.astype(F32)).astype(BF16)
    return hi, mid, lo


def _tri_dot(tri, x):
    hi, mid, lo = _split3(x)
    return (jnp.dot(tri, hi, preferred_element_type=F32) + jnp.dot(tri, mid, preferred_element_type=F32)
            + jnp.dot(tri, lo, preferred_element_type=F32))


def _fox_cum(gf, bfo, name):
    S = gf.shape[0]
    nb = S // LANE
    fcol = (GF_COLS - LANE) // LANE

    def body(f_ref, b_ref, cum_ref):
        row = lax.broadcasted_iota(jnp.int32, (LANE, LANE), 0)
        col = lax.broadcasted_iota(jnp.int32, (LANE, LANE), 1)
        tri = jnp.where(row >= col, 1.0, 0.0).astype(BF16)
        carry = jnp.zeros((1, LANE), F32)
        for t in range(nb):
            xl = f_ref[t * LANE:(t + 1) * LANE, :] + b_ref[...]
            lf = jnp.minimum(xl, 0.0) - jnp.log(1.0 + jnp.exp(-jnp.abs(xl)))
            cblk = _tri_dot(tri, lf) + carry
            cum_ref[t * LANE:(t + 1) * LANE, :] = cblk
            carry = cblk[LANE - 1:LANE, :]

    return pl.pallas_call(
        body, grid=(1,), in_specs=[pl.BlockSpec((S, LANE), lambda i: (0, fcol)), _vec(LANE)],
        out_specs=pl.BlockSpec((S, LANE), lambda i: (0, 0)), out_shape=SDS((S, LANE), F32),
        compiler_params=_cparams(("arbitrary",)), name=name)(gf, bfo)


def _fox_cum_bwd(gf, bfo, dcum, name):
    S = gf.shape[0]
    nb = S // LANE
    fcol = (GF_COLS - LANE) // LANE

    def body(f_ref, b_ref, dc_ref, df_ref, db_ref):
        row = lax.broadcasted_iota(jnp.int32, (LANE, LANE), 0)
        col = lax.broadcasted_iota(jnp.int32, (LANE, LANE), 1)
        tri = jnp.where(row <= col, 1.0, 0.0).astype(BF16)
        carry = jnp.zeros((1, LANE), F32)
        tot = jnp.zeros((1, LANE), F32)
        for t in range(nb - 1, -1, -1):
            rows = slice(t * LANE, (t + 1) * LANE)
            dlf = _tri_dot(tri, dc_ref[rows, :]) + carry
            carry = dlf[0:1, :]
            xl = f_ref[rows, :] + b_ref[...]
            dfl = dlf * (1.0 / (1.0 + jnp.exp(xl)))
            df_ref[rows, :] = dfl.astype(df_ref.dtype)
            tot = tot + jnp.sum(dfl, axis=0, keepdims=True)
        db_ref[...] = tot

    return pl.pallas_call(
        body, grid=(1,),
        in_specs=[pl.BlockSpec((S, LANE), lambda i: (0, fcol)), _vec(LANE), pl.BlockSpec((S, LANE), lambda i: (0, 0))],
        out_specs=[pl.BlockSpec((S, LANE), lambda i: (0, 0)), _vec(LANE)],
        out_shape=[SDS((S, LANE), BF16), SDS((1, LANE), F32)],
        compiler_params=_cparams(("arbitrary",)), name=name)(gf, bfo, dcum)


def _rel_onehot(qi, band):
    r = lax.broadcasted_iota(jnp.int32, (N_REL_PAD, band), 0)
    j = lax.broadcasted_iota(jnp.int32, (N_REL_PAD, band), 1)
    idx = jnp.clip(C_PREV * CHUNK + qi - j, -REL_CLIP, REL_CLIP) + REL_CLIP
    return jnp.where(r == idx, 1.0, 0.0).astype(BF16)


def _rel_expand(rel, name):
    band = (C_PREV + 1) * CHUNK

    def body(rel_ref, o_ref):
        hi, mid, lo = _split3(rel_ref[...])

        def row(qi, carry):
            oh = _rel_onehot(qi, band)
            o_ref[qi] = (jnp.dot(hi, oh, preferred_element_type=F32) + jnp.dot(mid, oh, preferred_element_type=F32)
                         + jnp.dot(lo, oh, preferred_element_type=F32))
            return carry

        lax.fori_loop(0, CHUNK, row, 0, unroll=2)

    return pl.pallas_call(
        body, grid=(1,), in_specs=[pl.BlockSpec((N_HEADS, N_REL_PAD), lambda i: (0, 0))],
        out_specs=pl.BlockSpec((CHUNK, N_HEADS, band), lambda i: (0, 0, 0)),
        out_shape=SDS((CHUNK, N_HEADS, band), F32),
        compiler_params=_cparams(("arbitrary",)), name=name)(rel)


def _tri_dot_rhs(x, oh):
    hi, mid, lo = _split3(x)
    return (jnp.dot(hi, oh, preferred_element_type=F32) + jnp.dot(mid, oh, preferred_element_type=F32)
            + jnp.dot(lo, oh, preferred_element_type=F32))


def _rel_reduce(dbias, name):
    band = (C_PREV + 1) * CHUNK
    NT = (((1,), (1,)), ((), ()))

    def body(d_ref, o_ref):
        def row(qi, acc):
            oh = _rel_onehot(qi, band)
            hi, mid, lo = _split3(d_ref[qi])
            return acc + (lax.dot_general(hi, oh, NT, preferred_element_type=F32)
                          + lax.dot_general(mid, oh, NT, preferred_element_type=F32)
                          + lax.dot_general(lo, oh, NT, preferred_element_type=F32))

        o_ref[...] = lax.fori_loop(0, CHUNK, row, jnp.zeros((N_HEADS, N_REL_PAD), F32), unroll=2)

    return pl.pallas_call(
        body, grid=(1,), in_specs=[pl.BlockSpec((CHUNK, N_HEADS, band), lambda i: (0, 0, 0))],
        out_specs=pl.BlockSpec((N_HEADS, N_REL_PAD), lambda i: (0, 0)),
        out_shape=SDS((N_HEADS, N_REL_PAD), F32),
        compiler_params=_cparams(("arbitrary",)), name=name)(dbias)


def _alibi_table():
    qi = np.arange(CHUNK)[:, None]
    j = np.arange((A_PREV + 1) * CHUNK)[None, :]
    dist = np.abs(A_PREV * CHUNK + qi - j).astype(np.float32)
    slopes = np.exp2(-8.0 * np.arange(1, N_HEADS + 1, dtype=np.float32) / N_HEADS).astype(np.float32)
    return jnp.asarray(-slopes[:, None, None] * dist[None])


def _ada_fwd(c_all, w, b, name):
    n = w.shape[2]

    def body(c_ref, w_ref, b_ref, o_ref):
        cv = c_ref[...]
        cond = (cv * _sigmoid(cv)).astype(BF16)
        o_ref[0] = jnp.dot(cond, w_ref[0].astype(BF16), preferred_element_type=F32) + b_ref[0]

    return pl.pallas_call(
        body, grid=(DEPTH,),
        in_specs=[pl.BlockSpec((16, D_MODEL), lambda l: (0, 0)), pl.BlockSpec((1, D_MODEL, n), lambda l: (l, 0, 0)),
                  pl.BlockSpec((1, 1, n), lambda l: (l, 0, 0))],
        out_specs=pl.BlockSpec((1, 16, n), lambda l: (l, 0, 0)), out_shape=SDS((DEPTH, 16, n), F32),
        compiler_params=_cparams(("parallel",)), name=name)(c_all, w, b)


def _ada_bwd(c_t, dmod, name):
    n = dmod.shape[2]
    bn = _blk(n, 512)
    tr = 256

    def body(c_ref, d_ref, o_ref):
        cv = c_ref[...]
        cond = (cv * _sigmoid(cv)).astype(BF16).astype(F32)
        dm = d_ref[0].astype(BF16).astype(F32)
        acc = cond[:, 0:1] * dm[0:1, :]
        for b_ in range(1, 8):
            acc = acc + cond[:, b_:b_ + 1] * dm[b_:b_ + 1, :]
        o_ref[0] = acc

    return pl.pallas_call(
        body, grid=(DEPTH, D_MODEL // tr, n // bn),
        in_specs=[pl.BlockSpec((tr, 8), lambda l, i, j: (i, 0)), pl.BlockSpec((1, 8, bn), lambda l, i, j: (l, 0, j))],
        out_specs=pl.BlockSpec((1, tr, bn), lambda l, i, j: (l, i, j)), out_shape=SDS((DEPTH, D_MODEL, n), F32),
        compiler_params=_cparams(("parallel", "parallel", "parallel")), name=name)(c_t, dmod)


def _adamw(w, m, v, parts, name):
    L, R, C = w.shape
    per_layer = isinstance(parts, (list, tuple))
    plist = list(parts) if per_layer else [parts]
    P = plist[0].shape[0]
    tr = _blk_rows(R, max(16, (1 << 18) // C))
    nr = R // tr
    c1 = 1.0 - ADAM_B1 ** ADAM_STEP
    c2 = 1.0 - ADAM_B2 ** ADAM_STEP

    def total(p_ref):
        g = p_ref[0].astype(F32)
        for k in range(1, P):
            g = g + p_ref[k].astype(F32)
        return g

    def body(w_ref, m_ref, v_ref, *rest):
        p_refs, (g_ref, d_ref, nm_ref, nv_ref) = rest[:len(plist)], rest[len(plist):]
        g = total(p_refs[0])
        for k in range(1, len(plist)):
            g = jnp.where(pl.program_id(0) == k, total(p_refs[k]), g)
        mn = ADAM_B1 * m_ref[0] + (1.0 - ADAM_B1) * g
        vn = ADAM_B2 * v_ref[0] + (1.0 - ADAM_B2) * (g * g)
        m_hat = mn / c1
        v_hat = vn / c2
        g_ref[0] = g
        nm_ref[0] = mn
        nv_ref[0] = vn
        d_ref[0] = -ADAM_LR * (m_hat / (jnp.sqrt(v_hat) + ADAM_EPS) + ADAM_WD * w_ref[0])

    rs = pl.BlockSpec((1, tr, C), lambda l, i: (l, i, 0))
    if per_layer:
        pspecs = [pl.BlockSpec((P, tr, C), functools.partial(lambda l, i, k: (0, jnp.where(l == k, i, 0), 0), k=k))
                  for k in range(L)]
    else:
        pspecs = [pl.BlockSpec((P, tr, C), lambda l, i: (0, l * nr + i, 0))]
    return pl.pallas_call(
        body, grid=(L, nr), in_specs=[rs, rs, rs] + pspecs,
        out_specs=[rs, rs, rs, rs], out_shape=[SDS((L, R, C), F32)] * 4,
        compiler_params=_cparams(("parallel", "parallel")), name=name)(w, m, v, *plist)


def _blk_rows(R, cap):
    if R <= cap:
        return R
    best = None
    for t in range(16, cap + 1, 16):
        if R % t == 0:
            best = t
    assert best is not None, (R, cap)
    return best


def _add_cast(a0, a1, b, name):
    Q, R, C = b.shape
    tr = _blk_rows(R, max(16, (1 << 19) // C))

    def body(a0_ref, a1_ref, b_ref, o_ref):
        c = lax.axis_index("c")

        @pl.when(c == 0)
        def _():
            o_ref[...] = (a0_ref[...] + b_ref[...]).astype(o_ref.dtype)

        @pl.when(c == 1)
        def _():
            o_ref[...] = (a1_ref[...] + b_ref[...]).astype(o_ref.dtype)

    bs = pl.BlockSpec((1, tr, C), lambda q, i: (q, i, 0))
    return pl.pallas_call(
        body, grid=(Q, R // tr), in_specs=[bs, bs, bs], out_specs=bs, out_shape=SDS((Q, R, C), BF16),
        compiler_params=_cparams(("parallel", "parallel")), name=name)(a0, a1, b)


def _coords():
    return lax.axis_index("x"), lax.axis_index("y"), lax.axis_index("c")


def _flip(v, bit):
    return 1 - v if bit else v


def _all_gather8(v, name):
    R = v.shape[0]

    def body(v_ref, o_ref, send_sems, recv_sems):
        x, y, c = _coords()
        me = 4 * x + 2 * y + c
        o_ref[me] = v_ref[...]
        copies = []
        for k in range(1, 8):
            peer = (_flip(x, k & 4), _flip(y, k & 2), _flip(c, k & 1))
            cp = pltpu.make_async_remote_copy(
                src_ref=v_ref, dst_ref=o_ref.at[me], send_sem=send_sems.at[k - 1], recv_sem=recv_sems.at[k - 1],
                device_id=peer, device_id_type=MESH)
            cp.start()
            copies.append(cp)
        for cp in copies:
            cp.wait_recv()
        for cp in copies:
            cp.wait_send()

    return pl.pallas_call(
        body, in_specs=[VMEM_SPEC], out_specs=VMEM_SPEC, out_shape=SDS((8, R, LANE), v.dtype),
        scratch_shapes=[pltpu.SemaphoreType.DMA((7,)), pltpu.SemaphoreType.DMA((7,))],
        compiler_params=pltpu.CompilerParams(vmem_limit_bytes=VMEM_LIMIT), name=name)(v)


def _sibling_swap(arrs0, arrs1, name):
    n = len(arrs0)

    def body(*refs):
        in0, in1, out_refs = refs[:n], refs[n:2 * n], refs[2 * n:3 * n]
        send_sems, recv_sems = refs[3 * n:]
        x, y, c = _coords()

        def swap(srcs):
            copies = [pltpu.make_async_remote_copy(
                src_ref=srcs[a], dst_ref=out_refs[a], send_sem=send_sems.at[a], recv_sem=recv_sems.at[a],
                device_id=(x, y, 1 - c), device_id_type=MESH) for a in range(n)]
            for cp in copies:
                cp.start()
            for cp in copies:
                cp.wait_recv()
            for cp in copies:
                cp.wait_send()

        @pl.when(c == 0)
        def _():
            swap(in1)

        @pl.when(c == 1)
        def _():
            swap(in0)

    return pl.pallas_call(
        body, in_specs=[ANY] * (2 * n), out_specs=[ANY] * n,
        out_shape=[SDS(a.shape, a.dtype) for a in arrs0],
        scratch_shapes=[pltpu.SemaphoreType.DMA((n,)), pltpu.SemaphoreType.DMA((n,))],
        name=name)(*arrs0, *arrs1)


def _chip_exchange(arrs, *, reduce, name):
    n = len(arrs)

    def body(*refs):
        in_refs, out_refs = refs[:n], refs[n:2 * n]
        ici_send, ici_recv, d2d_send, d2d_recv, loc_sem = refs[2 * n:]
        x, y, c = _coords()
        p = 2 * x + y
        local, first, fwd = [], [], []
        for a in range(n):
            R = out_refs[a].shape[1] // 2
            half = pl.ds(pl.multiple_of(c * R, 16), R)
            if reduce:
                lc = pltpu.make_async_copy(in_refs[a].at[p], out_refs[a].at[p, half], loc_sem.at[a])
            else:
                lc = pltpu.make_async_copy(in_refs[a], out_refs[a].at[p], loc_sem.at[a])
            lc.start()
            local.append(lc)
            for k in range(1, 4):
                qx, qy = _flip(x, k & 2), _flip(y, k & 1)
                src = in_refs[a].at[2 * qx + qy] if reduce else in_refs[a].at[half]
                cp = pltpu.make_async_remote_copy(
                    src_ref=src, dst_ref=out_refs[a].at[p, half], send_sem=ici_send.at[a, k - 1],
                    recv_sem=ici_recv.at[a, k - 1], device_id=(qx, qy, c), device_id_type=MESH)
                cp.start()
                first.append(cp)
        for a in range(n):
            R = out_refs[a].shape[1] // 2
            half = pl.ds(pl.multiple_of(c * R, 16), R)
            for k in range(0 if reduce else 1, 4):
                qx, qy = _flip(x, k & 2), _flip(y, k & 1)
                slot = out_refs[a].at[2 * qx + qy, half]
                if k == 0:
                    local[a].wait()
                else:
                    first[a * 3 + k - 1].wait_recv()
                cp = pltpu.make_async_remote_copy(
                    src_ref=slot, dst_ref=slot, send_sem=d2d_send.at[a, k], recv_sem=d2d_recv.at[a, k],
                    device_id=(x, y, 1 - c), device_id_type=MESH)
                cp.start()
                fwd.append(cp)
        for cp in fwd:
            cp.wait_recv()
        for cp in first + fwd:
            cp.wait_send()
        if not reduce:
            for lc in local:
                lc.wait()

    if reduce:
        out_shape = [SDS((4, 2 * a.shape[1], a.shape[2]), a.dtype) for a in arrs]
    else:
        out_shape = [SDS((4,) + a.shape, a.dtype) for a in arrs]
    return pl.pallas_call(
        body, in_specs=[ANY] * n, out_specs=[ANY] * n, out_shape=out_shape,
        scratch_shapes=[pltpu.SemaphoreType.DMA((n, 3)), pltpu.SemaphoreType.DMA((n, 3)),
                        pltpu.SemaphoreType.DMA((n, 4)), pltpu.SemaphoreType.DMA((n, 4)),
                        pltpu.SemaphoreType.DMA((n,))],
        name=name)(*arrs)


class _LayerExchange:
    def __init__(self, srcs, lay, reduce):
        self.srcs, self.lay, self.reduce = list(srcs), lay, reduce
        self.n = len(self.srcs)
        if reduce:
            self.out_shapes = [SDS(a.shape, a.dtype) for a in self.srcs]
        else:
            self.out_shapes = [SDS((4,) + a.shape, a.dtype) for a in self.srcs]
        self.sem_shapes = [pltpu.SemaphoreType.DMA((self.n, 3)), pltpu.SemaphoreType.DMA((self.n, 3)),
                           pltpu.SemaphoreType.DMA((self.n,))]

    def _copies(self, src_refs, dst_refs, sems):
        ici_send, ici_recv, loc_sem = sems
        x, y, c = _coords()
        p = 2 * x + y
        local, remote = [], []
        for a in range(self.n):
            src_own = src_refs[a].at[p] if self.reduce else src_refs[a]
            local.append(pltpu.make_async_copy(src_own, dst_refs[a].at[p], loc_sem.at[a]))
            for k in range(1, 4):
                qx, qy = _flip(x, k & 2), _flip(y, k & 1)
                src = src_refs[a].at[2 * qx + qy] if self.reduce else src_refs[a]
                remote.append(pltpu.make_async_remote_copy(
                    src_ref=src, dst_ref=dst_refs[a].at[p], send_sem=ici_send.at[a, k - 1],
                    recv_sem=ici_recv.at[a, k - 1], device_id=(qx, qy, self.lay), device_id_type=MESH))
        return c, local, remote

    def start(self, src_refs, dst_refs, sems):
        c, local, remote = self._copies(src_refs, dst_refs, sems)
        if self.reduce:
            @pl.when(c == self.lay)
            def _():
                for cp in local + remote:
                    cp.start()
        else:
            for cp in local:
                cp.start()

            @pl.when(c == self.lay)
            def _():
                for cp in remote:
                    cp.start()

    def finish(self, src_refs, dst_refs, sems):
        c, local, remote = self._copies(src_refs, dst_refs, sems)
        if self.reduce:
            @pl.when(c == self.lay)
            def _():
                for cp in remote:
                    cp.wait_recv()
                for cp in remote:
                    cp.wait_send()
                for cp in local:
                    cp.wait()
        else:
            @pl.when(c == self.lay)
            def _():
                for cp in remote:
                    cp.wait_recv()
                for cp in remote:
                    cp.wait_send()

            for cp in local:
                cp.wait()

    def run(self, name):
        n = self.n

        def body(*refs):
            src_refs, dst_refs, sems = refs[:n], refs[n:2 * n], refs[2 * n:]
            self.start(src_refs, dst_refs, sems)
            self.finish(src_refs, dst_refs, sems)

        return pl.pallas_call(body, in_specs=[ANY] * n, out_specs=[ANY] * n, out_shape=self.out_shapes,
                              scratch_shapes=self.sem_shapes, name=name)(*self.srcs)


def _call_hosting(body, *, comm, grid, in_specs, out_specs, out_shape, scratch_shapes, name, args):
    n_in, n_out, n_scr = len(args), len(out_shape), len(scratch_shapes)
    if comm is None:
        res = pl.pallas_call(body, grid=grid, in_specs=in_specs, out_specs=out_specs, out_shape=out_shape,
                             scratch_shapes=scratch_shapes, compiler_params=_cparams(("parallel",)), name=name)(*args)
        return list(res), None
    k = comm.n
    last = grid[0] - 1

    def hosted(*refs):
        ins, cin = refs[:n_in], refs[n_in:n_in + k]
        outs = refs[n_in + k:n_in + k + n_out]
        cout = refs[n_in + k + n_out:n_in + 2 * k + n_out]
        scr = refs[n_in + 2 * k + n_out:n_in + 2 * k + n_out + n_scr]
        sems = refs[n_in + 2 * k + n_out + n_scr:]
        step = pl.program_id(0)

        @pl.when(step == 0)
        def _():
            comm.start(cin, cout, sems)

        body(*ins, *outs, *scr)

        @pl.when(step == last)
        def _():
            comm.finish(cin, cout, sems)

    res = pl.pallas_call(
        hosted, grid=grid, in_specs=list(in_specs) + [ANY] * k, out_specs=list(out_specs) + [ANY] * k,
        out_shape=list(out_shape) + comm.out_shapes, scratch_shapes=list(scratch_shapes) + comm.sem_shapes,
        compiler_params=_cparams(("arbitrary",)), name=name)(*args, *comm.srcs)
    return list(res[:n_out]), list(res[n_out:])


def _layer_handoff(bufs, lays, slots, name):
    flat = [b for group in bufs for b in group]
    n = len(flat)
    ns = len(slots)

    def body(*refs):
        out_refs = refs[n:2 * n]
        send_sems, recv_sems = refs[2 * n:]
        x, y, c = _coords()
        i = 0
        for group, lay in zip(bufs, lays):
            copies = []
            for _b in group:
                for j, k in enumerate(slots):
                    slot = out_refs[i].at[2 * _flip(x, k & 2) + _flip(y, k & 1)]
                    copies.append(pltpu.make_async_remote_copy(
                        src_ref=slot, dst_ref=slot, send_sem=send_sems.at[i, j], recv_sem=recv_sems.at[i, j],
                        device_id=(x, y, 1 - c), device_id_type=MESH))
                i += 1

            @pl.when(c == lay)
            def _(copies=copies):
                for cp in copies:
                    cp.start()
                for cp in copies:
                    cp.wait_send()

            @pl.when(c != lay)
            def _(copies=copies):
                for cp in copies:
                    cp.wait_recv()

    return pl.pallas_call(
        body, in_specs=[ANY] * n, out_specs=[ANY] * n, out_shape=[SDS(b.shape, b.dtype) for b in flat],
        input_output_aliases={i: i for i in range(n)},
        scratch_shapes=[pltpu.SemaphoreType.DMA((n, ns)), pltpu.SemaphoreType.DMA((n, ns))], name=name)(*flat)


def _sibling_send(arrs, src_core, name):
    n = len(arrs)

    def body(*refs):
        in_refs, out_refs = refs[:n], refs[n:2 * n]
        send_sems, recv_sems = refs[2 * n:]
        x, y, c = _coords()
        copies = [pltpu.make_async_remote_copy(
            src_ref=in_refs[a], dst_ref=out_refs[a], send_sem=send_sems.at[a], recv_sem=recv_sems.at[a],
            device_id=(x, y, 1 - c), device_id_type=MESH) for a in range(n)]

        @pl.when(c == src_core)
        def _():
            for cp in copies:
                cp.start()
            for cp in copies:
                cp.wait_send()

        @pl.when(c != src_core)
        def _():
            for cp in copies:
                cp.wait_recv()

    return pl.pallas_call(
        body, in_specs=[ANY] * n, out_specs=[ANY] * n, out_shape=[SDS(a.shape, a.dtype) for a in arrs],
        scratch_shapes=[pltpu.SemaphoreType.DMA((n,)), pltpu.SemaphoreType.DMA((n,))], name=name)(*arrs)


def _add_cast_on(a, b, lay, name):
    Q, R, C = b.shape
    tr = _blk_rows(R, max(16, (1 << 19) // C))

    def body(a_ref, b_ref, o_ref):
        @pl.when(lax.axis_index("c") == lay)
        def _():
            o_ref[...] = (a_ref[...] + b_ref[...]).astype(o_ref.dtype)

    bs = pl.BlockSpec((1, tr, C), lambda q, i: (q, i, 0))
    return pl.pallas_call(
        body, grid=(Q, R // tr), in_specs=[bs, bs], out_specs=bs, out_shape=SDS((Q, R, C), BF16),
        compiler_params=_cparams(("parallel", "parallel")), name=name)(a, b)


_IN_SIZES = (512, 128, 128, 512, 512, 512, 8, 512, 512, 512, 3072)
_IN_OFF = tuple(int(v) for v in np.cumsum((0,) + _IN_SIZES))
_IN_Q = N_IN_COLS // 4


def _pack_w_in(w):
    def cols(lo, hi):
        out = []
        while lo < hi:
            q, off = divmod(lo, _IN_Q)
            n = min(hi - lo, _IN_Q - off)
            out.append(w[q, :, off:off + n])
            lo += n
        return out

    fb0, fb1, g0 = _IN_OFF[6], _IN_OFF[7], _IN_OFF[10]
    wqkv = jnp.concatenate(cols(0, fb0) + cols(fb1, g0), axis=1)
    wgf = jnp.concatenate(cols(g0, N_IN_COLS) + cols(fb0, fb1) + [jnp.zeros((w.shape[1], LANE - 8), w.dtype)], axis=1)
    return wqkv, wgf


def _unpack_w_in(dqkv, dgf):
    fb0, fb1, g0 = _IN_OFF[6], _IN_OFF[7], _IN_OFF[10]

    def cols(lo, hi):
        out = []
        while lo < hi:
            if lo < fb0:
                n = min(hi, fb0) - lo
                out.append(dqkv[:, lo:lo + n])
            elif lo < fb1:
                n = min(hi, fb1) - lo
                out.append(dgf[:, 3072 + lo - fb0:3072 + lo - fb0 + n])
            elif lo < g0:
                n = min(hi, g0) - lo
                out.append(dqkv[:, lo - 8:lo - 8 + n])
            else:
                n = hi - lo
                out.append(dgf[:, lo - g0:lo - g0 + n])
            lo += n
        return out

    return jnp.stack([jnp.concatenate(cols(q * _IN_Q, (q + 1) * _IN_Q), axis=1) for q in range(4)])


def _pad_rows(a, rows):
    return jnp.pad(a, ((0, rows - a.shape[0]), (0, 0)))


def _small_pack(parts):
    flat = jnp.concatenate([p.reshape(-1) for p in parts])
    n = flat.shape[0]
    rows = -(-n // LANE)
    rows = -(-rows // 8) * 8
    return jnp.pad(flat, (0, rows * LANE - n)).reshape(rows, LANE)


def _small_unpack(block, shapes):
    flat = block.reshape(-1)
    out, off = [], 0
    for s in shapes:
        n = int(np.prod(s))
        out.append(flat[off:off + n].reshape(s))
        off += n
    return out


def _kv_same(g):
    return 0


def _kv_own(g):
    return g


def _layer_fwd(x, mod, p, l, comms):
    sh_m, sc_m, g_m, sh_f, sc_f, g_f = mod
    nm = "l%d_" % l
    h1 = _norm_mod_fwd(x, p["norm_mix_g"], sc_m, sh_m, nm + "norm_mix_fwd")
    qkv = _mm(h1, p["wqkv"], mode="nn", out_dtype=BF16, name=nm + "proj_qkv")
    gf = _mm(h1, p["wgf"], mode="nn", out_dtype=F32, name=nm + "proj_gf", cap_n=640)
    qkv_t = qkv.T
    o_a_t, got_a = _bandT_fwd(qkv_t[0:512], _heads(qkv[:, 512:640], A_KV_HEADS), qkv_t[640:768], p["alibi"],
                              p["sink_tab"], GQ=4, GK=1, P=A_PREV, kvoff=_kv_same, name=nm + "attn_a_fwd",
                              comm=comms[0])
    cum = _fox_cum(gf, p["b_forget_pad"], nm + "fox_cum")
    cum_t = cum[:, :N_HEADS].T
    cc, cr = cum_t[:, :, None], cum_t[:, None, :]
    (o_b_t, lse_b), got_b = _foxT_fwd(qkv_t[768:1280], _heads(qkv[:, 1280:1792], N_HEADS), qkv_t[1792:2304], cc, cr,
                                      nm + "attn_b_fwd", comm=comms[1])
    o_c_t, got_c = _bandT_fwd(qkv_t[2304:2816], _heads(qkv[:, 2816:3328], N_HEADS), qkv_t[3328:3840], p["rel_tab"],
                              p["no_sink"], GQ=2, GK=2, P=C_PREV, kvoff=_kv_own, name=nm + "attn_c_fwd",
                              comm=comms[2])
    o = jnp.concatenate([o_a_t, o_b_t, o_c_t], axis=0).T
    y = _mm(o, p["wb"], mode="nn", out_dtype=F32, groups=3, name=nm + "branch")
    merged = _merge_fwd(y, gf, nm + "merge_fwd")
    mix = _mm(merged, p["wout"], mode="nn", out_dtype=F32, name=nm + "out_proj")
    x1 = _resid_fwd(x, mix, g_m, nm + "resid_mix")
    h2 = _norm_mod_fwd(x1, p["norm_ffn_g"], sc_f, sh_f, nm + "norm_ffn_fwd")
    u = _mm(h2, p["wfi"], mode="nn", out_dtype=F32, name=nm + "ffn_in", cap_n=512)
    a = _swiglu_fwd(u, nm + "swiglu_fwd")
    f = _mm(a, p["wfo"], mode="nn", out_dtype=F32, name=nm + "ffn_out", cap_m=1024)
    x2 = _resid_fwd(x1, f, g_f, nm + "resid_ffn")
    saved = dict(x=x, h1=h1, qkv=qkv, qkv_t=qkv_t, gf=gf, cc=cc, cr=cr, o_b_t=o_b_t, lse_b=lse_b, o=o, y=y, merged=merged,
                 mix=mix, x1=x1, h2=h2, u=u, a=a, f=f)
    return x2, saved, (got_a, got_b, got_c)


def _layer_bwd(dx2, mod, p, s, l, comms):
    sh_m, sc_m, g_m, sh_f, sc_f, g_f = mod
    nm = "l%d_" % l
    dg_f, df = _resid_bwd(dx2, s["f"], g_f, nm + "resid_ffn_bwd")
    da = _mm(df, p["wfo"], mode="nt", out_dtype=F32, name=nm + "ffn_out_dx", cap_m=1024, cap_n=1408)
    d_wfo = _mm(s["a"], df, mode="tn", out_dtype=F32, name=nm + "ffn_out_dw", cap_m=1408, cap_k=2048)
    du = _swiglu_bwd(da, s["u"], nm + "swiglu_bwd")
    dh2 = _mm(du, p["wfi"], mode="nt", out_dtype=F32, name=nm + "ffn_in_dx", cap_m=1024)
    d_wfi = _mm(s["h2"], du, mode="tn", out_dtype=F32, name=nm + "ffn_in_dw", cap_m=1024, cap_n=1408, cap_k=2048,
                col_quarters=True)
    dx1, dsc_f, dsh_f, dgn_f = _norm_mod_bwd(s["x1"], [dh2], dx2, p["norm_ffn_g"], sc_f, nm + "norm_ffn_bwd")
    dg_m, dmix = _resid_bwd(dx1, s["mix"], g_m, nm + "resid_mix_bwd")
    dmerged = _mm(dmix, p["wout"], mode="nt", out_dtype=F32, name=nm + "out_proj_dx")
    d_wout = _mm(s["merged"], dmix, mode="tn", out_dtype=F32, name=nm + "out_proj_dw", cap_m=1024, cap_k=2048)
    dy, dgates = _merge_bwd(dmerged, s["y"], s["gf"], nm + "merge_bwd")
    do = _mm(dy, p["wb"], mode="nt", out_dtype=BF16, groups=3, name=nm + "branch_dx")
    d_wb = _mm(s["o"], dy, mode="tn", out_dtype=F32, groups=3, name=nm + "branch_dw", cap_k=2048,
               col_quarters=True)
    qkv, qkv_t = s["qkv"], s["qkv_t"]
    do_t = do.T
    (dqa_t, dka_h, dva_h, _, dsink), got_a = _bandT_bwd(
        qkv_t[0:512], _heads(qkv[:, 0:512], N_HEADS), _heads(qkv[:, 512:640], A_KV_HEADS), qkv_t[512:640],
        _heads(qkv[:, 640:768], A_KV_HEADS), do_t[0:512], _heads(do[:, 0:512], N_HEADS), p["alibi"], p["sink_tab"],
        GQ=4, GK=1, P=A_PREV, kvoff=_kv_same, name=nm + "attn_a_bwd", comm=comms[0])
    qb_h = _heads(qkv[:, 768:1280], N_HEADS)
    q_aug = jnp.concatenate([qb_h * 0.125, jnp.ones(qb_h.shape[:2] + (1,), BF16),
                             jnp.zeros(qb_h.shape[:2] + (LANE - HEAD_DIM - 1,), BF16)], axis=2)
    (dqb_t, dkb_h, dvb_h, dck, dcq), got_b = _foxT_bwd(
        qkv_t[768:1280], q_aug, _heads(qkv[:, 1280:1792], N_HEADS), qkv_t[1280:1792],
        _heads(qkv[:, 1792:2304], N_HEADS), s["cc"], s["cr"], s["o_b_t"], do_t[512:1024],
        _heads(do[:, 512:1024], N_HEADS), s["lse_b"], nm + "attn_b_bwd", comm=comms[1])
    dcum = jnp.pad((dck[:, :, 0] + dcq[:, 0, :]).T, ((0, 0), (0, LANE - N_HEADS)))
    dfb, db_forget = _fox_cum_bwd(s["gf"], p["b_forget_pad"], dcum, nm + "fox_cum_bwd")
    (dqc_t, dkc_h, dvc_h, dbias_c, _), got_c = _bandT_bwd(
        qkv_t[2304:2816], _heads(qkv[:, 2304:2816], N_HEADS), _heads(qkv[:, 2816:3328], N_HEADS), qkv_t[2816:3328],
        _heads(qkv[:, 3328:3840], N_HEADS), do_t[1024:1536], _heads(do[:, 1024:1536], N_HEADS), p["rel_tab"],
        p["no_sink"], GQ=2, GK=2, P=C_PREV, kvoff=_kv_own, name=nm + "attn_c_bwd", comm=comms[2])
    d_rel = _rel_reduce(jnp.transpose(_unpair_table(dbias_c), (1, 0, 2)), nm + "rel_reduce")[:, :N_REL]
    dqkv = jnp.concatenate([dqa_t.T, _unheads(dka_h), _unheads(dva_h), dqb_t.T, _unheads(dkb_h), _unheads(dvb_h),
                            dqc_t.T, _unheads(dkc_h), _unheads(dvc_h)], axis=1)
    dgf = jnp.concatenate([dgates, dfb], axis=1)
    dh1a = _mm(dqkv, p["wqkv"], mode="nt", out_dtype=F32, name=nm + "proj_qkv_dx", cap_k=1024)
    dh1b = _mm(dgf, p["wgf"], mode="nt", out_dtype=F32, name=nm + "proj_gf_dx", cap_k=640)
    d_wqkv = _mm(s["h1"], dqkv, mode="tn", out_dtype=F32, name=nm + "proj_qkv_dw", cap_m=1024, cap_k=2048)
    d_wgf = _mm(s["h1"], dgf, mode="tn", out_dtype=F32, name=nm + "proj_gf_dw", cap_m=1024, cap_n=640, cap_k=2048)
    dx, dsc_m, dsh_m, dgn_m = _norm_mod_bwd(s["x"], [dh1a, dh1b], dx1, p["norm_mix_g"], sc_m, nm + "norm_mix_bwd")
    d_mod = jnp.concatenate([dsh_m, dsc_m, dg_m, dsh_f, dsc_f, dg_f], axis=1)[0]
    grads = dict(w_in=_unpack_w_in(d_wqkv, d_wgf), w_branch=d_wb, w_out=d_wout.reshape(4, -1, D_MODEL),
                 w_ffn_in=d_wfi, w_ffn_out=d_wfo.reshape(4, -1, D_MODEL),
                 norm_mix_g=dgn_m[0], norm_ffn_g=dgn_f[0], b_forget=db_forget[0, :N_HEADS],
                 sinks=dsink[:, 0, 0], rel_bias=d_rel, d_mod=d_mod)
    return dx, grads, (got_a, got_b, got_c)


def kernel(x, c, norm_mix_g, norm_ffn_g, w_ada, b_ada, w_in, b_forget, sinks, rel_bias, w_branch, w_out, w_ffn_in, w_ffn_out, final_norm_g, loss_target, m_norm_mix_g, m_norm_ffn_g, m_w_ada, m_b_ada, m_w_in, m_b_forget, m_sinks, m_rel_bias, m_w_branch, m_w_out, m_w_ffn_in, m_w_ffn_out, m_final_norm_g, v_norm_mix_g, v_norm_ffn_g, v_w_ada, v_b_ada, v_w_in, v_b_forget, v_sinks, v_rel_bias, v_w_branch, v_w_out, v_w_ffn_in, v_w_ffn_out, v_final_norm_g):
    xi, yi, ci = _coords()
    chip = 2 * xi + yi
    dev = 2 * chip + ci
    xs = x[0]
    S = xs.shape[0]
    n_ada = w_ada.shape[2]

    big_names = ("w_in", "w_branch", "w_out", "w_ffn_in", "w_ffn_out")
    big_w = dict(w_in=w_in, w_branch=w_branch, w_out=w_out, w_ffn_in=w_ffn_in, w_ffn_out=w_ffn_out)
    big_m = dict(w_in=m_w_in, w_branch=m_w_branch, w_out=m_w_out, w_ffn_in=m_w_ffn_in, w_ffn_out=m_w_ffn_out)
    big_v = dict(w_in=v_w_in, w_branch=v_w_branch, w_out=v_w_out, w_ffn_in=v_w_ffn_in, w_ffn_out=v_w_ffn_out)
    flat2 = lambda a: a.reshape(-1, a.shape[-1])
    shards = [[flat2(big_w[n][l]).astype(BF16) for n in big_names] for l in range(DEPTH)]
    gw0 = _LayerExchange(shards[0], 0, reduce=False).run("weights_gather_l0")
    gw = [_layer_handoff([gw0], [0], (1, 2, 3), "weights_handoff_l0"), None]
    host_w = ((1, 2), (0, 4), (3,))
    host_g = ((1, 2, 4), (0,), (3,))

    def hosted(arrs, split, reduce):
        return tuple(_LayerExchange([arrs[i] for i in idx], 1, reduce) for idx in split)

    def unsplit(got, split):
        out = [None] * len(big_names)
        for res, idx in zip(got, split):
            for r, i in zip(res, idx):
                out[i] = r
        return out

    c_all = _all_gather8(c.reshape(8, LANE), "gather_c").reshape(8, D_MODEL)
    b_sh = lax.dynamic_slice_in_dim(b_ada, chip * n_ada, n_ada, axis=1)[:, None, :]
    mod_sh = _ada_fwd(_pad_rows(c_all, 16), w_ada, b_sh, "ada_fwd")[:, :8, :]
    mod_all = _all_gather8(mod_sh.reshape(-1, LANE), "gather_mod").reshape(8, DEPTH, 8, n_ada)
    mod_mine = lax.dynamic_index_in_dim(mod_all[0::2], dev, axis=2, keepdims=False)
    mod = mod_mine.transpose(1, 0, 2).reshape(DEPTH, 6, D_MODEL)

    alibi = _pair_table(_alibi_table())
    no_sink = jnp.full((N_HEADS, 8, LANE), NEG_INF, F32)
    def make_params(l, g):
        wqkv, wgf = _pack_w_in(g[0])
        rel_tab = _rel_expand(jnp.pad(rel_bias[l], ((0, 0), (0, N_REL_PAD - N_REL))), "l%d_rel_expand" % l)
        return dict(
            wqkv=wqkv, wgf=wgf, wb=jnp.transpose(g[1], (1, 0, 2)).reshape(3 * BRANCH_W, D_MODEL),
            wout=g[2].reshape(D_MODEL, D_MODEL), wfi=jnp.transpose(g[3], (1, 0, 2)).reshape(D_MODEL, 2 * FFN_H),
            wfo=g[4].reshape(FFN_H, D_MODEL),
            norm_mix_g=norm_mix_g[l][None], norm_ffn_g=norm_ffn_g[l][None],
            b_forget_pad=jnp.pad(b_forget[l], (0, LANE - N_HEADS))[None],
            sink_tab=jnp.broadcast_to(sinks[l][:, None, None], (N_HEADS, 8, LANE)),
            no_sink=no_sink, alibi=alibi, rel_tab=_pair_table(jnp.transpose(rel_tab, (1, 0, 2))))

    mods = [[mod[l, k][None] for k in range(6)] for l in range(DEPTH)]
    none3 = (None, None, None)
    params, saved = [None] * DEPTH, [None] * DEPTH
    params[0] = make_params(0, gw[0])
    h, saved[0], got = _layer_fwd(xs, mods[0], params[0], 0, hosted(shards[1], host_w, False))
    gw[1] = _layer_handoff([unsplit(got, host_w)], [1], (1, 2, 3), "weights_handoff_l1")
    params[1] = make_params(1, gw[1])
    h, saved[1], _ = _layer_fwd(h, mods[1], params[1], 1, none3)
    loss_dev, dh, d_final = _final_loss(h, final_norm_g[None], loss_target[0], "final_loss")
    grads = [None] * DEPTH
    dh, grads[1], _ = _layer_bwd(dh, mods[1], params[1], saved[1], 1, none3)

    def chip_sums(l):
        mine = [grads[l][n] for n in big_names]
        theirs = _sibling_send(mine, 1 - l, "grads_send_l%d" % l)
        return [_add_cast_on(a, b, l, "grads_chip_sum_l%d_%s" % (l, n)) for n, a, b in zip(big_names, mine, theirs)]

    dh, grads[0], got = _layer_bwd(dh, mods[0], params[0], saved[0], 0, hosted(chip_sums(1), host_g, True))
    grad_x = dh[None]
    loss = lax.psum(loss_dev[0, 0], ("x", "y", "c"))
    parts1 = unsplit(got, host_g)
    parts0 = _LayerExchange(chip_sums(0), 0, reduce=True).run("grads_reduce_l0")
    both = _layer_handoff([parts0, parts1], [0, 1], (0, 1, 2, 3), "grads_handoff")
    parts0, parts1 = both[:len(big_names)], both[len(big_names):]
    big_out = {}
    for n, p0, p1 in zip(big_names, parts0, parts1):
        shp = big_w[n].shape
        as3 = lambda a: a.reshape(shp[0], -1, shp[-1])
        res = _adamw(as3(big_w[n]), as3(big_m[n]), as3(big_v[n]), [p0, p1], "adamw_" + n)
        big_out[n] = [r.reshape(shp) for r in res]

    small_names = ("norm_mix_g", "norm_ffn_g", "b_ada", "b_forget", "sinks", "rel_bias", "final_norm_g")
    small_w = dict(norm_mix_g=norm_mix_g, norm_ffn_g=norm_ffn_g, b_ada=b_ada, b_forget=b_forget, sinks=sinks,
                   rel_bias=rel_bias, final_norm_g=final_norm_g)
    small_m = dict(norm_mix_g=m_norm_mix_g, norm_ffn_g=m_norm_ffn_g, b_ada=m_b_ada, b_forget=m_b_forget,
                   sinks=m_sinks, rel_bias=m_rel_bias, final_norm_g=m_final_norm_g)
    small_v = dict(norm_mix_g=v_norm_mix_g, norm_ffn_g=v_norm_ffn_g, b_ada=v_b_ada, b_forget=v_b_forget,
                   sinks=v_sinks, rel_bias=v_rel_bias, final_norm_g=v_final_norm_g)
    small_g = dict(
        norm_mix_g=jnp.stack([grads[l]["norm_mix_g"] for l in range(DEPTH)]),
        norm_ffn_g=jnp.stack([grads[l]["norm_ffn_g"] for l in range(DEPTH)]),
        b_ada=jnp.stack([grads[l]["d_mod"] for l in range(DEPTH)]),
        b_forget=jnp.stack([grads[l]["b_forget"] for l in range(DEPTH)]),
        sinks=jnp.stack([grads[l]["sinks"] for l in range(DEPTH)]),
        rel_bias=jnp.stack([grads[l]["rel_bias"] for l in range(DEPTH)]),
        final_norm_g=d_final[0])
    shapes = [small_w[n].shape for n in small_names]
    g_all = _all_gather8(_small_pack([small_g[n] for n in small_names]), "gather_small_grads")
    res = _adamw(_small_pack([small_w[n] for n in small_names])[None], _small_pack([small_m[n] for n in small_names])[None],
                 _small_pack([small_v[n] for n in small_names])[None], g_all, "adamw_small")
    small_out = {n: [] for n in small_names}
    for r in res:
        for n, a in zip(small_names, _small_unpack(r[0], shapes)):
            small_out[n].append(a)
    off_b = sum(int(np.prod(s)) for s in shapes[:2])
    n_mod = DEPTH * 6 * D_MODEL
    dmod_all = g_all.reshape(8, -1)[:, off_b:off_b + n_mod].reshape(8, DEPTH, 6 * D_MODEL)
    dmod_sh = lax.dynamic_slice_in_dim(dmod_all, chip * n_ada, n_ada, axis=2).transpose(1, 0, 2)
    g_ada = _ada_bwd(c_all.T, dmod_sh, "ada_bwd")
    ada_out = _adamw(w_ada, m_w_ada, v_w_ada, flat2(g_ada)[None], "adamw_w_ada")

    order = ("norm_mix_g", "norm_ffn_g", "w_ada", "b_ada", "w_in", "b_forget", "sinks", "rel_bias", "w_branch",
             "w_out", "w_ffn_in", "w_ffn_out", "final_norm_g")

    def pick(n, k):
        if n == "w_ada":
            return ada_out[k]
        if n in big_out:
            return big_out[n][k]
        return small_out[n][k]

    outs = [loss, grad_x]
    for k in range(4):
        outs += [pick(n, k) for n in order]
    return tuple(outs)
```

```python
import functools

import numpy as np
import jax
import jax.numpy as jnp
from jax import lax
from jax.experimental import pallas as pl
from jax.experimental.pallas import tpu as pltpu

F32 = jnp.float32
BF16 = jnp.bfloat16
SDS = jax.ShapeDtypeStruct

D_MODEL = 1024
DEPTH = 2
CHUNK = 64
HEAD_DIM = 64
EPS = 1e-6
NEG_INF = -1e30
N_HEADS = 8
A_KV_HEADS = 2
A_PREV = 2
C_PREV = 8
REL_CLIP = 128
N_REL = 2 * REL_CLIP + 1
N_REL_PAD = 384
BRANCH_W = 512
FFN_H = 2816
FOX_BQ = 256
FOX_BK = 512
BAND_UNROLL_FWD = 4
BAND_UNROLL_BWD = 2
QKV_COLS = 3840
GF_COLS = 3200
N_IN_COLS = 6920
LANE = 128
VMEM_LIMIT = 48 * 1024 * 1024

ADAM_LR = 0.001
ADAM_B1 = 0.9
ADAM_B2 = 0.999
ADAM_EPS = 1e-08
ADAM_WD = 0.01
ADAM_STEP = 10

MESH = pl.DeviceIdType.MESH
ANY = pl.BlockSpec(memory_space=pl.ANY)
VMEM_SPEC = pl.BlockSpec(memory_space=pltpu.VMEM)


def _cparams(sem=None):
    return pltpu.CompilerParams(dimension_semantics=sem, vmem_limit_bytes=VMEM_LIMIT)


def _blk(n, cap):
    if n <= cap:
        return n
    best = None
    for m in range(LANE, cap + 1, LANE):
        if n % m == 0:
            best = m
    assert best is not None, (n, cap)
    return best


def _sigmoid(x):
    return 1.0 / (1.0 + jnp.exp(-x))


def _mm(a, b, *, mode, out_dtype, name, groups=1, cap_m=2048, cap_n=1024, cap_k=1408, col_quarters=False,
        comm=None):
    G = groups
    assert not col_quarters or mode == "tn"
    if mode == "nn":
        M, K, N = a.shape[0], a.shape[1] // G, b.shape[1]
        assert b.shape[0] == G * K
    elif mode == "nt":
        M, K, N = a.shape[0], a.shape[1] // G, b.shape[0] // G
        assert b.shape[1] == K
    else:
        K, M, N = a.shape[0], a.shape[1] // G, b.shape[1] // G
        assert b.shape[0] == K
    bm, bn, bk = _blk(M, cap_m), _blk(N // 4 if col_quarters else N, cap_n), _blk(K, cap_k)
    nm, nn, nk = M // bm, N // bn, K // bk
    if mode == "nn":
        a_spec = pl.BlockSpec((bm, bk), lambda g, i, j, k: (i, g * nk + k))
        b_spec = pl.BlockSpec((bk, bn), lambda g, i, j, k: (g * nk + k, j))
        o_spec = pl.BlockSpec((bm, bn), lambda g, i, j, k: (i, g * nn + j))
        dims = (((1,), (0,)), ((), ()))
        out_shape = (M, G * N)
    elif mode == "nt":
        a_spec = pl.BlockSpec((bm, bk), lambda g, i, j, k: (i, g * nk + k))
        b_spec = pl.BlockSpec((bn, bk), lambda g, i, j, k: (g * nn + j, k))
        o_spec = pl.BlockSpec((bm, bn), lambda g, i, j, k: (i, g * nn + j))
        dims = (((1,), (1,)), ((), ()))
        out_shape = (M, G * N)
    else:
        a_spec = pl.BlockSpec((bk, bm), lambda g, i, j, k: (k, g * nm + i))
        b_spec = pl.BlockSpec((bk, bn), lambda g, i, j, k: (k, g * nn + j))
        dims = (((0,), (0,)), ((), ()))
        if col_quarters:
            nq = nn // 4
            o_spec = pl.BlockSpec((1, bm, bn), lambda g, i, j, k: (j // nq, g * nm + i, j % nq))
            out_shape = (4, G * M, N // 4)
        else:
            o_spec = pl.BlockSpec((bm, bn), lambda g, i, j, k: (g * nm + i, j))
            out_shape = (G * M, N)

    def product(a_ref, b_ref):
        return lax.dot_general(a_ref[...].astype(BF16), b_ref[...].astype(BF16), dims, preferred_element_type=F32)

    def body_one(a_ref, b_ref, o_ref):
        o_ref[...] = product(a_ref, b_ref).astype(o_ref.dtype).reshape(o_ref.shape)

    def body_acc(a_ref, b_ref, o_ref, acc_ref):
        k = pl.program_id(3)

        @pl.when(k == 0)
        def _():
            acc_ref[...] = jnp.zeros_like(acc_ref)

        acc_ref[...] += product(a_ref, b_ref)

        @pl.when(k == nk - 1)
        def _():
            o_ref[...] = acc_ref[...].astype(o_ref.dtype).reshape(o_ref.shape)

    res, got = _call_hosting(
        body_one if nk == 1 else body_acc, comm=comm, grid=(G, nm, nn, nk), in_specs=[a_spec, b_spec],
        out_specs=[o_spec], out_shape=[SDS(out_shape, out_dtype)],
        scratch_shapes=[] if nk == 1 else [pltpu.VMEM((bm, bn), F32)], name=name, args=(a, b),
        semantics=("parallel", "parallel", "parallel", "arbitrary"))
    return res[0] if comm is None else (res[0], got)


def _rows(tm, n, col=0):
    return pl.BlockSpec((tm, n), lambda i: (i, col))


def _vec(n):
    return pl.BlockSpec((1, n), lambda i: (0, 0))


def _tm(S):
    return min(S, 256)


def _norm_mod_fwd(x, g, sc, sh, name):
    S, Dm = x.shape
    tm = _tm(S)

    def body(x_ref, g_ref, sc_ref, sh_ref, h_ref):
        xv = x_ref[...]
        r = lax.rsqrt(jnp.mean(xv * xv, axis=-1, keepdims=True) + EPS)
        h_ref[...] = ((xv * r) * g_ref[...] * (1.0 + sc_ref[...]) + sh_ref[...]).astype(h_ref.dtype)

    return pl.pallas_call(
        body, grid=(S // tm,), in_specs=[_rows(tm, Dm), _vec(Dm), _vec(Dm), _vec(Dm)],
        out_specs=_rows(tm, Dm), out_shape=SDS((S, Dm), BF16),
        compiler_params=_cparams(("parallel",)), name=name)(x, g, sc, sh)


def _norm_mod_bwd(x, dh_list, dres, g, sc, name):
    S, Dm = x.shape
    tm = _tm(S)
    nh = len(dh_list)

    def body(*refs):
        x_ref = refs[0]
        dh_refs = refs[1:1 + nh]
        dres_ref, g_ref, sc_ref, dx_ref, dsc_ref, dsh_ref, dg_ref = refs[1 + nh:]
        i = pl.program_id(0)

        @pl.when(i == 0)
        def _():
            dsc_ref[...] = jnp.zeros_like(dsc_ref)
            dsh_ref[...] = jnp.zeros_like(dsh_ref)
            dg_ref[...] = jnp.zeros_like(dg_ref)

        xv = x_ref[...]
        dh = dh_refs[0][...]
        for r_ in dh_refs[1:]:
            dh = dh + r_[...]
        gv = g_ref[...]
        r = lax.rsqrt(jnp.mean(xv * xv, axis=-1, keepdims=True) + EPS)
        xn = xv * r
        xg = xn * gv
        dsh_ref[...] += jnp.sum(dh, axis=0, keepdims=True)
        dsc_ref[...] += jnp.sum(dh * xg, axis=0, keepdims=True)
        dxg = dh * (1.0 + sc_ref[...])
        dg_ref[...] += jnp.sum(dxg * xn, axis=0, keepdims=True)
        dxn = dxg * gv
        dx_ref[...] = dres_ref[...] + r * (dxn - xn * jnp.mean(dxn * xn, axis=-1, keepdims=True))

    return pl.pallas_call(
        body, grid=(S // tm,),
        in_specs=[_rows(tm, Dm)] * (2 + nh) + [_vec(Dm), _vec(Dm)],
        out_specs=[_rows(tm, Dm), _vec(Dm), _vec(Dm), _vec(Dm)],
        out_shape=[SDS((S, Dm), F32), SDS((1, Dm), F32), SDS((1, Dm), F32), SDS((1, Dm), F32)],
        compiler_params=_cparams(("arbitrary",)), name=name)(x, *dh_list, dres, g, sc)


def _resid_fwd(x, val, g, name):
    S, Dm = x.shape
    tm = _tm(S)

    def body(x_ref, v_ref, g_ref, o_ref):
        o_ref[...] = x_ref[...] + g_ref[...] * v_ref[...]

    return pl.pallas_call(
        body, grid=(S // tm,), in_specs=[_rows(tm, Dm), _rows(tm, Dm), _vec(Dm)],
        out_specs=_rows(tm, Dm), out_shape=SDS((S, Dm), F32),
        compiler_params=_cparams(("parallel",)), name=name)(x, val, g)


def _resid_bwd(dx, val, g, name):
    S, Dm = dx.shape
    tm = _tm(S)

    def body(dx_ref, v_ref, g_ref, dg_ref, dv_ref):
        @pl.when(pl.program_id(0) == 0)
        def _():
            dg_ref[...] = jnp.zeros_like(dg_ref)

        dxv = dx_ref[...]
        dg_ref[...] += jnp.sum(dxv * v_ref[...], axis=0, keepdims=True)
        dv_ref[...] = (dxv * g_ref[...]).astype(dv_ref.dtype)

    return pl.pallas_call(
        body, grid=(S // tm,), in_specs=[_rows(tm, Dm), _rows(tm, Dm), _vec(Dm)],
        out_specs=[_vec(Dm), _rows(tm, Dm)], out_shape=[SDS((1, Dm), F32), SDS((S, Dm), BF16)],
        compiler_params=_cparams(("arbitrary",)), name=name)(dx, val, g)


def _merge_fwd(y, gf, name):
    S = y.shape[0]
    tm = _tm(S)
    W = 3 * D_MODEL

    def body(y_ref, g_ref, o_ref):
        acc = None
        for k in range(3):
            sl = slice(k * D_MODEL, (k + 1) * D_MODEL)
            t = _sigmoid(g_ref[:, sl]) * y_ref[:, sl]
            acc = t if acc is None else acc + t
        o_ref[...] = acc.astype(o_ref.dtype)

    return pl.pallas_call(
        body, grid=(S // tm,), in_specs=[_rows(tm, W), _rows(tm, W)],
        out_specs=_rows(tm, D_MODEL), out_shape=SDS((S, D_MODEL), BF16),
        compiler_params=_cparams(("parallel",)), name=name)(y, gf)


def _merge_bwd(dm, y, gf, name):
    S = y.shape[0]
    tm = _tm(S)
    W = 3 * D_MODEL

    def body(dm_ref, y_ref, g_ref, dy_ref, dg_ref):
        dmv = dm_ref[...]
        for k in range(3):
            sl = slice(k * D_MODEL, (k + 1) * D_MODEL)
            sg = _sigmoid(g_ref[:, sl])
            dy_ref[:, sl] = (dmv * sg).astype(dy_ref.dtype)
            dg_ref[:, sl] = (dmv * y_ref[:, sl] * (sg * (1.0 - sg))).astype(dg_ref.dtype)

    return pl.pallas_call(
        body, grid=(S // tm,), in_specs=[_rows(tm, D_MODEL), _rows(tm, W), _rows(tm, W)],
        out_specs=[_rows(tm, W), _rows(tm, W)], out_shape=[SDS((S, W), BF16), SDS((S, W), BF16)],
        compiler_params=_cparams(("parallel",)), name=name)(dm, y, gf)


def _swiglu_fwd(u, name):
    S = u.shape[0]
    tm = _tm(S)

    def body(g_ref, u_ref, a_ref):
        gv = g_ref[...]
        a_ref[...] = (gv * _sigmoid(gv) * u_ref[...]).astype(a_ref.dtype)

    return pl.pallas_call(
        body, grid=(S // tm,), in_specs=[_rows(tm, FFN_H, 0), _rows(tm, FFN_H, 1)],
        out_specs=_rows(tm, FFN_H), out_shape=SDS((S, FFN_H), BF16),
        compiler_params=_cparams(("parallel",)), name=name)(u, u)


def _swiglu_bwd(da, u, name):
    S = u.shape[0]
    tm = _tm(S)

    def body(da_ref, g_ref, u_ref, du_ref):
        dav = da_ref[...]
        gv = g_ref[...]
        sg = _sigmoid(gv)
        du_ref[:, 0:FFN_H] = (dav * u_ref[...] * (sg * (1.0 + gv * (1.0 - sg)))).astype(du_ref.dtype)
        du_ref[:, FFN_H:2 * FFN_H] = (dav * (gv * sg)).astype(du_ref.dtype)

    return pl.pallas_call(
        body, grid=(S // tm,), in_specs=[_rows(tm, FFN_H), _rows(tm, FFN_H, 0), _rows(tm, FFN_H, 1)],
        out_specs=_rows(tm, 2 * FFN_H), out_shape=SDS((S, 2 * FFN_H), BF16),
        compiler_params=_cparams(("parallel",)), name=name)(da, u, u)


def _final_loss(x, g, target, name):
    S, Dm = x.shape
    tm = _tm(S)

    def body(x_ref, g_ref, t_ref, loss_ref, dx_ref, dg_ref):
        @pl.when(pl.program_id(0) == 0)
        def _():
            loss_ref[...] = jnp.zeros_like(loss_ref)
            dg_ref[...] = jnp.zeros_like(dg_ref)

        xv = x_ref[...]
        gv = g_ref[...]
        r = lax.rsqrt(jnp.mean(xv * xv, axis=-1, keepdims=True) + EPS)
        xn = xv * r
        err = xn * gv - t_ref[...]
        row = jnp.mean(err * err, axis=-1, keepdims=True)
        loss_ref[...] += 0.5 * jnp.sum(row, axis=0, keepdims=True)
        dy = err * (1.0 / Dm)
        dg_ref[...] += jnp.sum(dy * xn, axis=0, keepdims=True)
        dxn = dy * gv
        dx_ref[...] = r * (dxn - xn * jnp.mean(dxn * xn, axis=-1, keepdims=True))

    return pl.pallas_call(
        body, grid=(S // tm,), in_specs=[_rows(tm, Dm), _vec(Dm), _rows(tm, Dm)],
        out_specs=[pl.BlockSpec((1, 1), lambda i: (0, 0)), _rows(tm, Dm), _vec(Dm)],
        out_shape=[SDS((1, 1), F32), SDS((S, Dm), F32), SDS((1, Dm), F32)],
        compiler_params=_cparams(("arbitrary",)), name=name)(x, g, target)


def _band_softmax(qg, kg, bias, sink, valid):
    s = lax.dot_general(qg, kg, (((1,), (1,)), ((), ())), preferred_element_type=F32)
    s = jnp.where(valid, s + bias, NEG_INF)
    m = jnp.maximum(jnp.max(s, axis=-1, keepdims=True), sink)
    e = jnp.exp(s - m)
    es = jnp.exp(sink - m)
    l = jnp.sum(e, axis=-1, keepdims=True) + es
    return e / l, es / l


def _band_attn_fwd(q, k, v, bias, sink, *, G, P, kvoff, name):
    S = q.shape[0]
    ng = q.shape[1] // (G * HEAD_DIM)
    band = (P + 1) * CHUNK
    pad = P * CHUNK
    nc = S // CHUNK

    def body(q_ref, k_ref, v_ref, b_ref, s_ref, o_ref, kp, vp):
        kp[0:pad, :] = jnp.zeros((pad, LANE), BF16)
        vp[0:pad, :] = jnp.zeros((pad, LANE), BF16)
        kp[pad:pad + S, :] = k_ref[...]
        vp[pad:pad + S, :] = v_ref[...]
        col = lax.broadcasted_iota(jnp.int32, (CHUNK, band), 1)

        def step(n, carry):
            r = pl.multiple_of(n * CHUNK, CHUNK)
            qn = q_ref[pl.ds(r, CHUNK), :]
            kb = kp[pl.ds(r, band), :]
            vb = vp[pl.ds(r, band), :]
            valid = col >= (P - n) * CHUNK
            for g in range(G):
                ko = kvoff(g) * HEAD_DIM
                qg = qn[:, g * HEAD_DIM:(g + 1) * HEAD_DIM] * 0.125
                p, _ = _band_softmax(qg, kb[:, ko:ko + HEAD_DIM], b_ref[g], s_ref[g, 0:1, 0:1], valid)
                og = jnp.dot(p.astype(BF16), vb[:, ko:ko + HEAD_DIM], preferred_element_type=F32)
                o_ref[pl.ds(r, CHUNK), g * HEAD_DIM:(g + 1) * HEAD_DIM] = og.astype(o_ref.dtype)
            return carry

        lax.fori_loop(0, nc, step, 0, unroll=min(BAND_UNROLL_FWD, nc))

    GW = G * HEAD_DIM
    return pl.pallas_call(
        body, grid=(ng,),
        in_specs=[pl.BlockSpec((S, GW), lambda i: (0, i)), pl.BlockSpec((S, LANE), lambda i: (0, i)),
                  pl.BlockSpec((S, LANE), lambda i: (0, i)),
                  pl.BlockSpec((G, CHUNK, band), lambda i: (i, 0, 0)),
                  pl.BlockSpec((G, 8, LANE), lambda i: (i, 0, 0))],
        out_specs=pl.BlockSpec((S, GW), lambda i: (0, i)),
        out_shape=SDS((S, ng * GW), BF16),
        scratch_shapes=[pltpu.VMEM((S + pad, LANE), BF16), pltpu.VMEM((S + pad, LANE), BF16)],
        compiler_params=_cparams(("parallel",)), name=name)(q, k, v, bias, sink)


def _band_attn_bwd(q, k, v, bias, sink, do, *, G, P, kvoff, name):
    S = q.shape[0]
    ng = q.shape[1] // (G * HEAD_DIM)
    band = (P + 1) * CHUNK
    pad = P * CHUNK
    nc = S // CHUNK
    TN = (((0,), (0,)), ((), ()))

    def body(q_ref, k_ref, v_ref, b_ref, s_ref, do_ref, dq_ref, dk_ref, dv_ref, db_ref, dsk_ref,
             kp, vp, dkp, dvp):
        kp[0:pad, :] = jnp.zeros((pad, LANE), BF16)
        vp[0:pad, :] = jnp.zeros((pad, LANE), BF16)
        kp[pad:pad + S, :] = k_ref[...]
        vp[pad:pad + S, :] = v_ref[...]
        dkp[...] = jnp.zeros_like(dkp)
        dvp[...] = jnp.zeros_like(dvp)
        db_ref[...] = jnp.zeros_like(db_ref)
        col = lax.broadcasted_iota(jnp.int32, (CHUNK, band), 1)

        def step(n, dsink):
            r = pl.multiple_of(n * CHUNK, CHUNK)
            qn = q_ref[pl.ds(r, CHUNK), :]
            don = do_ref[pl.ds(r, CHUNK), :]
            kb = kp[pl.ds(r, band), :]
            vb = vp[pl.ds(r, band), :]
            valid = col >= (P - n) * CHUNK
            new = []
            for g in range(G):
                ko = kvoff(g) * HEAD_DIM
                lanes = slice(g * HEAD_DIM, (g + 1) * HEAD_DIM)
                qg = qn[:, lanes] * 0.125
                kg = kb[:, ko:ko + HEAD_DIM]
                dog = don[:, lanes]
                p, ps = _band_softmax(qg, kg, b_ref[g], s_ref[g, 0:1, 0:1], valid)
                dp = lax.dot_general(dog, vb[:, ko:ko + HEAD_DIM], (((1,), (1,)), ((), ())),
                                     preferred_element_type=F32)
                delta = jnp.sum(p * dp, axis=-1, keepdims=True)
                ds = p * (dp - delta)
                new.append(dsink[g] - jnp.sum(ps * delta, axis=0, keepdims=True))
                db_ref[g] += ds
                dsb = ds.astype(BF16)
                dq = jnp.dot(dsb, kg, preferred_element_type=F32) * 0.125
                dq_ref[pl.ds(r, CHUNK), lanes] = dq.astype(dq_ref.dtype)
                dkp[pl.ds(r, band), ko:ko + HEAD_DIM] += lax.dot_general(
                    dsb, qg, TN, preferred_element_type=F32)
                dvp[pl.ds(r, band), ko:ko + HEAD_DIM] += lax.dot_general(
                    p.astype(BF16), dog, TN, preferred_element_type=F32)
            return tuple(new)

        dsink = lax.fori_loop(0, nc, step, tuple(jnp.zeros((1, 1), F32) for _ in range(G)),
                              unroll=min(BAND_UNROLL_BWD, nc))
        for g in range(G):
            dsk_ref[g] = jnp.broadcast_to(dsink[g], (8, LANE))
        dk_ref[...] = dkp[pad:pad + S, :].astype(dk_ref.dtype)
        dv_ref[...] = dvp[pad:pad + S, :].astype(dv_ref.dtype)

    GW = G * HEAD_DIM
    qs = pl.BlockSpec((S, GW), lambda i: (0, i))
    ks = pl.BlockSpec((S, LANE), lambda i: (0, i))
    bs = pl.BlockSpec((G, CHUNK, band), lambda i: (i, 0, 0))
    ss = pl.BlockSpec((G, 8, LANE), lambda i: (i, 0, 0))
    return pl.pallas_call(
        body, grid=(ng,), in_specs=[qs, ks, ks, bs, ss, qs],
        out_specs=[qs, ks, ks, bs, ss],
        out_shape=[SDS((S, ng * GW), BF16), SDS((S, ng * LANE), BF16), SDS((S, ng * LANE), BF16),
                   SDS((ng * G, CHUNK, band), F32), SDS((ng * G, 8, LANE), F32)],
        scratch_shapes=[pltpu.VMEM((S + pad, LANE), BF16), pltpu.VMEM((S + pad, LANE), BF16),
                        pltpu.VMEM((S + pad, LANE), F32), pltpu.VMEM((S + pad, LANE), F32)],
        compiler_params=_cparams(("parallel",)), name=name)(q, k, v, bias, sink, do)


PAIR = 2 * CHUNK


def _bandT_softmax(kg, qTg, bias, sink, valid):
    s = jnp.dot(kg, qTg, preferred_element_type=F32)
    s = jnp.where(valid, s + bias, NEG_INF)
    m = jnp.maximum(jnp.max(s, axis=0, keepdims=True), sink)
    e = jnp.exp(s - m)
    es = jnp.exp(sink - m)
    inv = 1.0 / (jnp.sum(e, axis=0, keepdims=True) + es)
    return e * inv, es * inv


def _pad_copy_rows(dst, src, pad, S):
    dst[:, 0:pad, :] = jnp.zeros((dst.shape[0], pad, dst.shape[2]), dst.dtype)
    dst[:, pad:pad + S, :] = src[...]


def _pad_copy_lanes(dst, src, pad, S):
    dst[:, 0:pad] = jnp.zeros((dst.shape[0], pad), dst.dtype)
    dst[:, pad:pad + S] = src[...]


def _bandT_fwd(qT, k_h, vT, bias, sink, *, GQ, GK, P, kvoff, name, comm=None):
    S = qT.shape[1]
    ng = qT.shape[0] // (GQ * HEAD_DIM)
    BU = (P + 2) * CHUNK
    pad = P * CHUNK
    npair = S // PAIR

    def body(qT_ref, k_ref, vT_ref, b_ref, s_ref, oT_ref, kp, vTp):
        _pad_copy_rows(kp, k_ref, pad, S)
        _pad_copy_lanes(vTp, vT_ref, pad, S)
        rowi = lax.broadcasted_iota(jnp.int32, (BU, PAIR), 0)

        def step(n2, carry):
            r = pl.multiple_of(n2 * PAIR, PAIR)
            valid = rowi >= (P - 2 * n2) * CHUNK
            for g in range(GQ):
                kv = kvoff(g)
                hs = slice(g * HEAD_DIM, (g + 1) * HEAD_DIM)
                kvs = slice(kv * HEAD_DIM, (kv + 1) * HEAD_DIM)
                qTg = qT_ref[hs, pl.ds(r, PAIR)] * 0.125
                p, _ = _bandT_softmax(kp[kv, pl.ds(r, BU), :], qTg, b_ref[g], s_ref[g, 0:1, :], valid)
                oTg = jnp.dot(vTp[kvs, pl.ds(r, BU)], p.astype(BF16), preferred_element_type=F32)
                oT_ref[hs, pl.ds(r, PAIR)] = oTg.astype(oT_ref.dtype)
            return carry

        lax.fori_loop(0, npair, step, 0, unroll=min(2, npair))

    res, got = _call_hosting(
        body, comm=comm, grid=(ng,),
        in_specs=[pl.BlockSpec((GQ * HEAD_DIM, S), lambda i: (i, 0)),
                  pl.BlockSpec((GK, S, HEAD_DIM), lambda i: (i, 0, 0)),
                  pl.BlockSpec((GK * HEAD_DIM, S), lambda i: (i, 0)),
                  pl.BlockSpec((GQ, BU, PAIR), lambda i: (i, 0, 0)),
                  pl.BlockSpec((GQ, 8, LANE), lambda i: (i, 0, 0))],
        out_specs=[pl.BlockSpec((GQ * HEAD_DIM, S), lambda i: (i, 0))],
        out_shape=[SDS((ng * GQ * HEAD_DIM, S), BF16)],
        scratch_shapes=[pltpu.VMEM((GK, S + pad, HEAD_DIM), BF16), pltpu.VMEM((GK * HEAD_DIM, S + pad), BF16)],
        name=name, args=(qT, k_h, vT, bias, sink))
    return res[0], got


def _bandT_bwd(qT, q_h, k_h, kT, v_h, doT, do_h, bias, sink, *, GQ, GK, P, kvoff, name, comm=None):
    S = qT.shape[1]
    ng = qT.shape[0] // (GQ * HEAD_DIM)
    BU = (P + 2) * CHUNK
    pad = P * CHUNK
    npair = S // PAIR

    def body(qT_ref, q_ref, k_ref, kT_ref, v_ref, doT_ref, do_ref, b_ref, s_ref,
             dqT_ref, dk_ref, dv_ref, db_ref, dsk_ref, kp, kTp, vp, dkp, dvp):
        _pad_copy_rows(kp, k_ref, pad, S)
        _pad_copy_rows(vp, v_ref, pad, S)
        _pad_copy_lanes(kTp, kT_ref, pad, S)
        dkp[...] = jnp.zeros_like(dkp)
        dvp[...] = jnp.zeros_like(dvp)
        db_ref[...] = jnp.zeros_like(db_ref)
        rowi = lax.broadcasted_iota(jnp.int32, (BU, PAIR), 0)

        def step(n2, dsink):
            r = pl.multiple_of(n2 * PAIR, PAIR)
            valid = rowi >= (P - 2 * n2) * CHUNK
            new = []
            for g in range(GQ):
                kv = kvoff(g)
                hs = slice(g * HEAD_DIM, (g + 1) * HEAD_DIM)
                kvs = slice(kv * HEAD_DIM, (kv + 1) * HEAD_DIM)
                qTg = qT_ref[hs, pl.ds(r, PAIR)] * 0.125
                p, ps = _bandT_softmax(kp[kv, pl.ds(r, BU), :], qTg, b_ref[g], s_ref[g, 0:1, :], valid)
                dp = jnp.dot(vp[kv, pl.ds(r, BU), :], doT_ref[hs, pl.ds(r, PAIR)], preferred_element_type=F32)
                delta = jnp.sum(p * dp, axis=0, keepdims=True)
                ds = p * (dp - delta)
                new.append(dsink[g] - ps * delta)
                db_ref[g] += ds
                dsb = ds.astype(BF16)
                dq = jnp.dot(kTp[kvs, pl.ds(r, BU)], dsb, preferred_element_type=F32) * 0.125
                dqT_ref[hs, pl.ds(r, PAIR)] = dq.astype(dqT_ref.dtype)
                dkp[kv, pl.ds(r, BU), :] += jnp.dot(dsb, q_ref[g, pl.ds(r, PAIR), :] * 0.125,
                                                    preferred_element_type=F32)
                dvp[kv, pl.ds(r, BU), :] += jnp.dot(p.astype(BF16), do_ref[g, pl.ds(r, PAIR), :],
                                                    preferred_element_type=F32)
            return tuple(new)

        dsink = lax.fori_loop(0, npair, step, tuple(jnp.zeros((1, PAIR), F32) for _ in range(GQ)))
        for g in range(GQ):
            dsk_ref[g] = jnp.broadcast_to(jnp.sum(dsink[g], axis=1, keepdims=True), (8, LANE))
        dk_ref[...] = dkp[:, pad:pad + S, :].astype(dk_ref.dtype)
        dv_ref[...] = dvp[:, pad:pad + S, :].astype(dv_ref.dtype)

    qTs = pl.BlockSpec((GQ * HEAD_DIM, S), lambda i: (i, 0))
    qhs = pl.BlockSpec((GQ, S, HEAD_DIM), lambda i: (i, 0, 0))
    khs = pl.BlockSpec((GK, S, HEAD_DIM), lambda i: (i, 0, 0))
    kTs = pl.BlockSpec((GK * HEAD_DIM, S), lambda i: (i, 0))
    bs = pl.BlockSpec((GQ, BU, PAIR), lambda i: (i, 0, 0))
    ss = pl.BlockSpec((GQ, 8, LANE), lambda i: (i, 0, 0))
    nkv = ng * GK
    return _call_hosting(
        body, comm=comm, grid=(ng,), in_specs=[qTs, qhs, khs, kTs, khs, qTs, qhs, bs, ss],
        out_specs=[qTs, khs, khs, bs, ss],
        out_shape=[SDS((ng * GQ * HEAD_DIM, S), BF16), SDS((nkv, S, HEAD_DIM), BF16), SDS((nkv, S, HEAD_DIM), BF16),
                   SDS((ng * GQ, BU, PAIR), F32), SDS((ng * GQ, 8, LANE), F32)],
        scratch_shapes=[pltpu.VMEM((GK, S + pad, HEAD_DIM), BF16), pltpu.VMEM((GK * HEAD_DIM, S + pad), BF16),
                        pltpu.VMEM((GK, S + pad, HEAD_DIM), BF16),
                        pltpu.VMEM((GK, S + pad, HEAD_DIM), F32), pltpu.VMEM((GK, S + pad, HEAD_DIM), F32)],
        name=name, args=(qT, q_h, k_h, kT, v_h, doT, do_h, bias, sink))


def _pair_table(tab):
    t = jnp.transpose(tab, (0, 2, 1))
    lo = jnp.pad(t, ((0, 0), (0, CHUNK), (0, 0)), constant_values=NEG_INF)
    hi = jnp.pad(t, ((0, 0), (CHUNK, 0), (0, 0)), constant_values=NEG_INF)
    return jnp.concatenate([lo, hi], axis=2)


def _unpair_table(d):
    band = d.shape[1] - CHUNK
    return jnp.transpose(d[:, 0:band, 0:CHUNK] + d[:, CHUNK:CHUNK + band, CHUNK:PAIR], (0, 2, 1))


def _heads(a, n):
    return jnp.transpose(a.reshape(a.shape[0], n, HEAD_DIM), (1, 0, 2))


def _unheads(a):
    return jnp.transpose(a, (1, 0, 2)).reshape(a.shape[1], a.shape[0] * HEAD_DIM)


def _fox_logits(qg, kj, cq, ck, r, c, row, col):
    s = lax.dot_general(qg, kj, (((1,), (1,)), ((), ())), preferred_element_type=F32)
    s = s + cq - ck
    return jnp.where(c + col <= r + row, s, NEG_INF)


def _fox_fwd(q, k, v, cc, cr, name):
    S = q.shape[0]
    npair = q.shape[1] // LANE
    BQ, BK = min(FOX_BQ, S), min(FOX_BK, S)
    nq = S // BQ
    heads = [slice(g * HEAD_DIM, (g + 1) * HEAD_DIM) for g in range(2)]

    def body(q_ref, k_ref, v_ref, cc_ref, cr_ref, o_ref, lse_ref):
        row = lax.broadcasted_iota(jnp.int32, (BQ, BK), 0)
        col = lax.broadcasted_iota(jnp.int32, (BQ, BK), 1)

        def qstep(i, carry):
            r = pl.multiple_of(i * BQ, BQ)
            qs = [q_ref[pl.ds(r, BQ), hl] * 0.125 for hl in heads]
            cqs = [cc_ref[g, pl.ds(r, BQ), :] for g in range(2)]

            def kstep(j, st):
                c = pl.multiple_of(j * BK, BK)
                new = []
                for g, hl in enumerate(heads):
                    m, l, acc = st[g]
                    s = _fox_logits(qs[g], k_ref[pl.ds(c, BK), hl], cqs[g], cr_ref[g, :, pl.ds(c, BK)],
                                    r, c, row, col)
                    mn = jnp.maximum(m, jnp.max(s, axis=-1, keepdims=True))
                    al = jnp.exp(m - mn)
                    e = jnp.exp(s - mn)
                    l = al * l + jnp.sum(e, axis=-1, keepdims=True)
                    acc = al * acc + jnp.dot(e.astype(BF16), v_ref[pl.ds(c, BK), hl],
                                             preferred_element_type=F32)
                    new.append((mn, l, acc))
                return tuple(new)

            init = (jnp.full((BQ, 1), NEG_INF, F32), jnp.zeros((BQ, 1), F32), jnp.zeros((BQ, HEAD_DIM), F32))
            st = lax.fori_loop(0, (r + BQ + BK - 1) // BK, kstep, (init, init))
            for g, hl in enumerate(heads):
                m, l, acc = st[g]
                o_ref[pl.ds(r, BQ), hl] = (acc / l).astype(o_ref.dtype)
                lse_ref[g, pl.ds(r, BQ), :] = m + jnp.log(l)
            return carry

        lax.fori_loop(0, nq, qstep, 0)

    blk = pl.BlockSpec((S, LANE), lambda i: (0, i))
    ccs = pl.BlockSpec((2, S, 1), lambda i: (i, 0, 0))
    crs = pl.BlockSpec((2, 1, S), lambda i: (i, 0, 0))
    return pl.pallas_call(
        body, grid=(npair,), in_specs=[blk, blk, blk, ccs, crs], out_specs=[blk, ccs],
        out_shape=[SDS((S, npair * LANE), BF16), SDS((2 * npair, S, 1), F32)],
        compiler_params=_cparams(("parallel",)), name=name)(q, k, v, cc, cr)


def _fox_bwd(q, k, v, cc, cr, o, do, lse, name):
    S = q.shape[0]
    npair = q.shape[1] // LANE
    BQ, BK = min(FOX_BQ, S), min(FOX_BK, S)
    nq = S // BQ
    heads = [slice(g * HEAD_DIM, (g + 1) * HEAD_DIM) for g in range(2)]
    TN = (((0,), (0,)), ((), ()))

    def body(q_ref, k_ref, v_ref, cc_ref, cr_ref, o_ref, do_ref, lse_ref,
             dq_ref, dk_ref, dv_ref, dcr_ref, dcc_ref, dka, dva):
        dka[...] = jnp.zeros_like(dka)
        dva[...] = jnp.zeros_like(dva)
        dcr_ref[...] = jnp.zeros_like(dcr_ref)
        row = lax.broadcasted_iota(jnp.int32, (BQ, BK), 0)
        col = lax.broadcasted_iota(jnp.int32, (BQ, BK), 1)

        def qstep(i, carry):
            r = pl.multiple_of(i * BQ, BQ)
            qs = [q_ref[pl.ds(r, BQ), hl] * 0.125 for hl in heads]
            dos = [do_ref[pl.ds(r, BQ), hl] for hl in heads]
            deltas = [jnp.sum(dos[g].astype(F32) * o_ref[pl.ds(r, BQ), hl].astype(F32), axis=-1, keepdims=True)
                      for g, hl in enumerate(heads)]
            cqs = [cc_ref[g, pl.ds(r, BQ), :] for g in range(2)]
            lses = [lse_ref[g, pl.ds(r, BQ), :] for g in range(2)]

            def kstep(j, st):
                c = pl.multiple_of(j * BK, BK)
                new = []
                for g, hl in enumerate(heads):
                    dq, rs = st[g]
                    kj = k_ref[pl.ds(c, BK), hl]
                    s = _fox_logits(qs[g], kj, cqs[g], cr_ref[g, :, pl.ds(c, BK)], r, c, row, col)
                    p = jnp.exp(s - lses[g])
                    dp = lax.dot_general(dos[g], v_ref[pl.ds(c, BK), hl], (((1,), (1,)), ((), ())),
                                         preferred_element_type=F32)
                    ds = p * (dp - deltas[g])
                    dcr_ref[g, :, pl.ds(c, BK)] -= jnp.sum(ds, axis=0, keepdims=True)
                    dsb = ds.astype(BF16)
                    dka[pl.ds(c, BK), hl] += lax.dot_general(dsb, qs[g], TN, preferred_element_type=F32)
                    dva[pl.ds(c, BK), hl] += lax.dot_general(p.astype(BF16), dos[g], TN,
                                                            preferred_element_type=F32)
                    new.append((dq + jnp.dot(dsb, kj, preferred_element_type=F32),
                                rs + jnp.sum(ds, axis=-1, keepdims=True)))
                return tuple(new)

            init = (jnp.zeros((BQ, HEAD_DIM), F32), jnp.zeros((BQ, 1), F32))
            st = lax.fori_loop(0, (r + BQ + BK - 1) // BK, kstep, (init, init))
            for g, hl in enumerate(heads):
                dq_ref[pl.ds(r, BQ), hl] = (st[g][0] * 0.125).astype(dq_ref.dtype)
                dcc_ref[g, pl.ds(r, BQ), :] = st[g][1]
            return carry

        lax.fori_loop(0, nq, qstep, 0)
        dk_ref[...] = dka[...].astype(dk_ref.dtype)
        dv_ref[...] = dva[...].astype(dv_ref.dtype)

    blk = pl.BlockSpec((S, LANE), lambda i: (0, i))
    ccs = pl.BlockSpec((2, S, 1), lambda i: (i, 0, 0))
    crs = pl.BlockSpec((2, 1, S), lambda i: (i, 0, 0))
    return pl.pallas_call(
        body, grid=(npair,), in_specs=[blk, blk, blk, ccs, crs, blk, blk, ccs],
        out_specs=[blk, blk, blk, crs, ccs],
        out_shape=[SDS((S, npair * LANE), BF16)] * 3 + [SDS((2 * npair, 1, S), F32), SDS((2 * npair, S, 1), F32)],
        scratch_shapes=[pltpu.VMEM((S, LANE), F32), pltpu.VMEM((S, LANE), F32)],
        compiler_params=_cparams(("parallel",)), name=name)(q, k, v, cc, cr, o, do, lse)


def _foxT_logits(kj, qTg, cq, ck, r, c, rowi, coli):
    s = jnp.dot(kj, qTg, preferred_element_type=F32)
    s = s + cq - ck
    return jnp.where(c + rowi <= r + coli, s, NEG_INF)


def _foxT_fwd(qT, k_h, vT, ck, cq, name, comm=None):
    S = qT.shape[1]
    npair = qT.shape[0] // LANE
    BQ, BK = min(FOX_BQ, S), min(FOX_BK, S)
    nq = S // BQ
    heads = [slice(g * HEAD_DIM, (g + 1) * HEAD_DIM) for g in range(2)]

    def body(qT_ref, k_ref, vT_ref, ck_ref, cq_ref, oT_ref, lse_ref):
        rowi = lax.broadcasted_iota(jnp.int32, (BK, BQ), 0)
        coli = lax.broadcasted_iota(jnp.int32, (BK, BQ), 1)

        def qstep(i, carry):
            r = pl.multiple_of(i * BQ, BQ)
            qs = [qT_ref[hs, pl.ds(r, BQ)] * 0.125 for hs in heads]
            cqs = [cq_ref[g, :, pl.ds(r, BQ)] for g in range(2)]

            def kstep(j, st):
                c = pl.multiple_of(j * BK, BK)
                new = []
                for g, hs in enumerate(heads):
                    m, l, acc = st[g]
                    s = _foxT_logits(k_ref[g, pl.ds(c, BK), :], qs[g], cqs[g], ck_ref[g, pl.ds(c, BK), :],
                                     r, c, rowi, coli)
                    mn = jnp.maximum(m, jnp.max(s, axis=0, keepdims=True))
                    al = jnp.exp(m - mn)
                    e = jnp.exp(s - mn)
                    l = al * l + jnp.sum(e, axis=0, keepdims=True)
                    acc = al * acc + jnp.dot(vT_ref[hs, pl.ds(c, BK)], e.astype(BF16), preferred_element_type=F32)
                    new.append((mn, l, acc))
                return tuple(new)

            init = (jnp.full((1, BQ), NEG_INF, F32), jnp.zeros((1, BQ), F32), jnp.zeros((HEAD_DIM, BQ), F32))
            st = lax.fori_loop(0, (r + BQ + BK - 1) // BK, kstep, (init, init))
            for g, hs in enumerate(heads):
                m, l, acc = st[g]
                oT_ref[hs, pl.ds(r, BQ)] = (acc * (1.0 / l)).astype(oT_ref.dtype)
                lse_ref[g, :, pl.ds(r, BQ)] = m + jnp.log(l)
            return carry

        lax.fori_loop(0, nq, qstep, 0)

    fT = pl.BlockSpec((LANE, S), lambda i: (i, 0))
    hm = pl.BlockSpec((2, S, HEAD_DIM), lambda i: (i, 0, 0))
    col = pl.BlockSpec((2, S, 1), lambda i: (i, 0, 0))
    rw = pl.BlockSpec((2, 1, S), lambda i: (i, 0, 0))
    return _call_hosting(
        body, comm=comm, grid=(npair,), in_specs=[fT, hm, fT, col, rw], out_specs=[fT, rw],
        out_shape=[SDS((npair * LANE, S), BF16), SDS((2 * npair, 1, S), F32)], scratch_shapes=[],
        name=name, args=(qT, k_h, vT, ck, cq))


def _foxT_bwd(qT, q_aug, k_h, kT, v_h, ck, cq, oT, doT, do_h, lse, name, comm=None):
    S = qT.shape[1]
    npair = qT.shape[0] // LANE
    BQ, BK = min(FOX_BQ, S), min(FOX_BK, S)
    nq = S // BQ
    heads = [slice(g * HEAD_DIM, (g + 1) * HEAD_DIM) for g in range(2)]

    def body(qT_ref, qa_ref, k_ref, kT_ref, v_ref, ck_ref, cq_ref, oT_ref, doT_ref, do_ref, lse_ref,
             dqT_ref, dk_ref, dv_ref, dck_ref, dcq_ref, dka, dva):
        dka[...] = jnp.zeros_like(dka)
        dva[...] = jnp.zeros_like(dva)
        rowi = lax.broadcasted_iota(jnp.int32, (BK, BQ), 0)
        coli = lax.broadcasted_iota(jnp.int32, (BK, BQ), 1)

        def qstep(i, carry):
            r = pl.multiple_of(i * BQ, BQ)
            qs = [qT_ref[hs, pl.ds(r, BQ)] * 0.125 for hs in heads]
            dos = [doT_ref[hs, pl.ds(r, BQ)] for hs in heads]
            deltas = [jnp.sum(dos[g].astype(F32) * oT_ref[hs, pl.ds(r, BQ)].astype(F32), axis=0, keepdims=True)
                      for g, hs in enumerate(heads)]
            cqs = [cq_ref[g, :, pl.ds(r, BQ)] for g in range(2)]
            lses = [lse_ref[g, :, pl.ds(r, BQ)] for g in range(2)]

            def kstep(j, st):
                c = pl.multiple_of(j * BK, BK)
                new = []
                for g, hs in enumerate(heads):
                    dq, rs = st[g]
                    s = _foxT_logits(k_ref[g, pl.ds(c, BK), :], qs[g], cqs[g], ck_ref[g, pl.ds(c, BK), :],
                                     r, c, rowi, coli)
                    p = jnp.exp(s - lses[g])
                    dp = jnp.dot(v_ref[g, pl.ds(c, BK), :], dos[g], preferred_element_type=F32)
                    ds = p * (dp - deltas[g])
                    dsb = ds.astype(BF16)
                    dka[g, pl.ds(c, BK), :] += jnp.dot(dsb, qa_ref[g, pl.ds(r, BQ), :], preferred_element_type=F32)
                    dva[g, pl.ds(c, BK), :] += jnp.dot(p.astype(BF16), do_ref[g, pl.ds(r, BQ), :],
                                                      preferred_element_type=F32)
                    new.append((dq + jnp.dot(kT_ref[hs, pl.ds(c, BK)], dsb, preferred_element_type=F32),
                                rs + jnp.sum(dsb.astype(F32), axis=0, keepdims=True)))
                return tuple(new)

            init = (jnp.zeros((HEAD_DIM, BQ), F32), jnp.zeros((1, BQ), F32))
            st = lax.fori_loop(0, (r + BQ + BK - 1) // BK, kstep, (init, init))
            for g, hs in enumerate(heads):
                dqT_ref[hs, pl.ds(r, BQ)] = (st[g][0] * 0.125).astype(dqT_ref.dtype)
                dcq_ref[g, :, pl.ds(r, BQ)] = st[g][1]
            return carry

        lax.fori_loop(0, nq, qstep, 0)
        dk_ref[...] = dka[:, :, 0:HEAD_DIM].astype(dk_ref.dtype)
        dck_ref[...] = -dka[:, :, HEAD_DIM:HEAD_DIM + 1]
        dv_ref[...] = dva[...].astype(dv_ref.dtype)

    fT = pl.BlockSpec((LANE, S), lambda i: (i, 0))
    hm = pl.BlockSpec((2, S, HEAD_DIM), lambda i: (i, 0, 0))
    hma = pl.BlockSpec((2, S, LANE), lambda i: (i, 0, 0))
    col = pl.BlockSpec((2, S, 1), lambda i: (i, 0, 0))
    rw = pl.BlockSpec((2, 1, S), lambda i: (i, 0, 0))
    nh = 2 * npair
    return _call_hosting(
        body, comm=comm, grid=(npair,), in_specs=[fT, hma, hm, fT, hm, col, rw, fT, fT, hm, rw],
        out_specs=[fT, hm, hm, col, rw],
        out_shape=[SDS((npair * LANE, S), BF16), SDS((nh, S, HEAD_DIM), BF16), SDS((nh, S, HEAD_DIM), BF16),
                   SDS((nh, S, 1), F32), SDS((nh, 1, S), F32)],
        scratch_shapes=[pltpu.VMEM((2, S, LANE), F32), pltpu.VMEM((2, S, HEAD_DIM), F32)],
        name=name, args=(qT, q_aug, k_h, kT, v_h, ck, cq, oT, doT, do_h, lse))


def _split3(x):
    hi = x.astype(BF16)
    r1 = x - hi.astype(F32)
    mid = r1.astype(BF16)
    lo = (r1 - mid.astype(F32)).astype(BF16)
    return hi, mid, lo


def _tri_dot(tri, x):
    hi, mid, lo = _split3(x)
    return (jnp.dot(tri, hi, preferred_element_type=F32) + jnp.dot(tri, mid, preferred_element_type=F32)
            + jnp.dot(tri, lo, preferred_element_type=F32))


def _fox_cum(gf, bfo, name):
    S = gf.shape[0]
    nb = S // LANE
    fcol = (GF_COLS - LANE) // LANE

    def body(f_ref, b_ref, cum_ref):
        row = lax.broadcasted_iota(jnp.int32, (LANE, LANE), 0)
        col = lax.broadcasted_iota(jnp.int32, (LANE, LANE), 1)
        tri = jnp.where(row >= col, 1.0, 0.0).astype(BF16)
        carry = jnp.zeros((1, LANE), F32)
        for t in range(nb):
            xl = f_ref[t * LANE:(t + 1) * LANE, :] + b_ref[...]
            lf = jnp.minimum(xl, 0.0) - jnp.log(1.0 + jnp.exp(-jnp.abs(xl)))
            cblk = _tri_dot(tri, lf) + carry
            cum_ref[t * LANE:(t + 1) * LANE, :] = cblk
            carry = cblk[LANE - 1:LANE, :]

    return pl.pallas_call(
        body, grid=(1,), in_specs=[pl.BlockSpec((S, LANE), lambda i: (0, fcol)), _vec(LANE)],
        out_specs=pl.BlockSpec((S, LANE), lambda i: (0, 0)), out_shape=SDS((S, LANE), F32),
        compiler_params=_cparams(("arbitrary",)), name=name)(gf, bfo)


def _fox_cum_bwd(gf, bfo, dcum, name):
    S = gf.shape[0]
    nb = S // LANE
    fcol = (GF_COLS - LANE) // LANE

    def body(f_ref, b_ref, dc_ref, df_ref, db_ref):
        row = lax.broadcasted_iota(jnp.int32, (LANE, LANE), 0)
        col = lax.broadcasted_iota(jnp.int32, (LANE, LANE), 1)
        tri = jnp.where(row <= col, 1.0, 0.0).astype(BF16)
        carry = jnp.zeros((1, LANE), F32)
        tot = jnp.zeros((1, LANE), F32)
        for t in range(nb - 1, -1, -1):
            rows = slice(t * LANE, (t + 1) * LANE)
            dlf = _tri_dot(tri, dc_ref[rows, :]) + carry
            carry = dlf[0:1, :]
            xl = f_ref[rows, :] + b_ref[...]
            dfl = dlf * (1.0 / (1.0 + jnp.exp(xl)))
            df_ref[rows, :] = dfl.astype(df_ref.dtype)
            tot = tot + jnp.sum(dfl, axis=0, keepdims=True)
        db_ref[...] = tot

    return pl.pallas_call(
        body, grid=(1,),
        in_specs=[pl.BlockSpec((S, LANE), lambda i: (0, fcol)), _vec(LANE), pl.BlockSpec((S, LANE), lambda i: (0, 0))],
        out_specs=[pl.BlockSpec((S, LANE), lambda i: (0, 0)), _vec(LANE)],
        out_shape=[SDS((S, LANE), BF16), SDS((1, LANE), F32)],
        compiler_params=_cparams(("arbitrary",)), name=name)(gf, bfo, dcum)


def _rel_onehot(qi, band):
    r = lax.broadcasted_iota(jnp.int32, (N_REL_PAD, band), 0)
    j = lax.broadcasted_iota(jnp.int32, (N_REL_PAD, band), 1)
    idx = jnp.clip(C_PREV * CHUNK + qi - j, -REL_CLIP, REL_CLIP) + REL_CLIP
    return jnp.where(r == idx, 1.0, 0.0).astype(BF16)


def _rel_expand(rel, name):
    band = (C_PREV + 1) * CHUNK

    def body(rel_ref, o_ref):
        hi, mid, lo = _split3(rel_ref[...])

        def row(qi, carry):
            oh = _rel_onehot(qi, band)
            o_ref[qi] = (jnp.dot(hi, oh, preferred_element_type=F32) + jnp.dot(mid, oh, preferred_element_type=F32)
                         + jnp.dot(lo, oh, preferred_element_type=F32))
            return carry

        lax.fori_loop(0, CHUNK, row, 0, unroll=2)

    return pl.pallas_call(
        body, grid=(1,), in_specs=[pl.BlockSpec((N_HEADS, N_REL_PAD), lambda i: (0, 0))],
        out_specs=pl.BlockSpec((CHUNK, N_HEADS, band), lambda i: (0, 0, 0)),
        out_shape=SDS((CHUNK, N_HEADS, band), F32),
        compiler_params=_cparams(("arbitrary",)), name=name)(rel)


def _tri_dot_rhs(x, oh):
    hi, mid, lo = _split3(x)
    return (jnp.dot(hi, oh, preferred_element_type=F32) + jnp.dot(mid, oh, preferred_element_type=F32)
            + jnp.dot(lo, oh, preferred_element_type=F32))


def _rel_reduce(dbias, name):
    band = (C_PREV + 1) * CHUNK
    NT = (((1,), (1,)), ((), ()))

    def body(d_ref, o_ref):
        def row(qi, acc):
            oh = _rel_onehot(qi, band)
            hi, mid, lo = _split3(d_ref[qi])
            return acc + (lax.dot_general(hi, oh, NT, preferred_element_type=F32)
                          + lax.dot_general(mid, oh, NT, preferred_element_type=F32)
                          + lax.dot_general(lo, oh, NT, preferred_element_type=F32))

        o_ref[...] = lax.fori_loop(0, CHUNK, row, jnp.zeros((N_HEADS, N_REL_PAD), F32), unroll=2)

    return pl.pallas_call(
        body, grid=(1,), in_specs=[pl.BlockSpec((CHUNK, N_HEADS, band), lambda i: (0, 0, 0))],
        out_specs=pl.BlockSpec((N_HEADS, N_REL_PAD), lambda i: (0, 0)),
        out_shape=SDS((N_HEADS, N_REL_PAD), F32),
        compiler_params=_cparams(("arbitrary",)), name=name)(dbias)


def _alibi_table():
    qi = np.arange(CHUNK)[:, None]
    j = np.arange((A_PREV + 1) * CHUNK)[None, :]
    dist = np.abs(A_PREV * CHUNK + qi - j).astype(np.float32)
    slopes = np.exp2(-8.0 * np.arange(1, N_HEADS + 1, dtype=np.float32) / N_HEADS).astype(np.float32)
    return jnp.asarray(-slopes[:, None, None] * dist[None])


def _ada_fwd(c_all, w, b, name):
    n = w.shape[2]

    def body(c_ref, w_ref, b_ref, o_ref):
        cv = c_ref[...]
        cond = (cv * _sigmoid(cv)).astype(BF16)
        o_ref[0] = jnp.dot(cond, w_ref[0].astype(BF16), preferred_element_type=F32) + b_ref[0]

    return pl.pallas_call(
        body, grid=(DEPTH,),
        in_specs=[pl.BlockSpec((16, D_MODEL), lambda l: (0, 0)), pl.BlockSpec((1, D_MODEL, n), lambda l: (l, 0, 0)),
                  pl.BlockSpec((1, 1, n), lambda l: (l, 0, 0))],
        out_specs=pl.BlockSpec((1, 16, n), lambda l: (l, 0, 0)), out_shape=SDS((DEPTH, 16, n), F32),
        compiler_params=_cparams(("parallel",)), name=name)(c_all, w, b)


def _ada_bwd(c_t, dmod, name):
    n = dmod.shape[2]
    bn = _blk(n, 512)
    tr = 256

    def body(c_ref, d_ref, o_ref):
        cv = c_ref[...]
        cond = (cv * _sigmoid(cv)).astype(BF16).astype(F32)
        dm = d_ref[0].astype(BF16).astype(F32)
        acc = cond[:, 0:1] * dm[0:1, :]
        for b_ in range(1, 8):
            acc = acc + cond[:, b_:b_ + 1] * dm[b_:b_ + 1, :]
        o_ref[0] = acc

    return pl.pallas_call(
        body, grid=(DEPTH, D_MODEL // tr, n // bn),
        in_specs=[pl.BlockSpec((tr, 8), lambda l, i, j: (i, 0)), pl.BlockSpec((1, 8, bn), lambda l, i, j: (l, 0, j))],
        out_specs=pl.BlockSpec((1, tr, bn), lambda l, i, j: (l, i, j)), out_shape=SDS((DEPTH, D_MODEL, n), F32),
        compiler_params=_cparams(("parallel", "parallel", "parallel")), name=name)(c_t, dmod)


def _adamw(w, m, v, parts, name):
    L, R, C = w.shape
    per_layer = isinstance(parts, (list, tuple))
    plist = list(parts) if per_layer else [parts]
    P = plist[0].shape[0]
    tr = _blk_rows(R, max(16, (1 << 18) // C))
    nr = R // tr
    c1 = 1.0 - ADAM_B1 ** ADAM_STEP
    c2 = 1.0 - ADAM_B2 ** ADAM_STEP

    def total(p_ref):
        g = p_ref[0].astype(F32)
        for k in range(1, P):
            g = g + p_ref[k].astype(F32)
        return g

    def body(w_ref, m_ref, v_ref, *rest):
        p_refs, (g_ref, d_ref, nm_ref, nv_ref) = rest[:len(plist)], rest[len(plist):]
        g = total(p_refs[0])
        for k in range(1, len(plist)):
            g = jnp.where(pl.program_id(0) == k, total(p_refs[k]), g)
        mn = ADAM_B1 * m_ref[0] + (1.0 - ADAM_B1) * g
        vn = ADAM_B2 * v_ref[0] + (1.0 - ADAM_B2) * (g * g)
        m_hat = mn / c1
        v_hat = vn / c2
        g_ref[0] = g
        nm_ref[0] = mn
        nv_ref[0] = vn
        d_ref[0] = -ADAM_LR * (m_hat / (jnp.sqrt(v_hat) + ADAM_EPS) + ADAM_WD * w_ref[0])

    rs = pl.BlockSpec((1, tr, C), lambda l, i: (l, i, 0))
    if per_layer:
        pspecs = [pl.BlockSpec((P, tr, C), functools.partial(lambda l, i, k: (0, jnp.where(l == k, i, 0), 0), k=k))
                  for k in range(L)]
    else:
        pspecs = [pl.BlockSpec((P, tr, C), lambda l, i: (0, l * nr + i, 0))]
    return pl.pallas_call(
        body, grid=(L, nr), in_specs=[rs, rs, rs] + pspecs,
        out_specs=[rs, rs, rs, rs], out_shape=[SDS((L, R, C), F32)] * 4,
        compiler_params=_cparams(("parallel", "parallel")), name=name)(w, m, v, *plist)


def _blk_rows(R, cap):
    if R <= cap:
        return R
    best = None
    for t in range(16, cap + 1, 16):
        if R % t == 0:
            best = t
    assert best is not None, (R, cap)
    return best


def _add_cast_rows(g, t, name):
    Q, R, C = g.shape
    half = R // 2
    tr = _blk_rows(half, max(16, (1 << 19) // C))
    nb = half // tr

    def body(lo_ref, hi_ref, t_ref, o_ref):
        c = lax.axis_index("c")

        @pl.when(c == 0)
        def _():
            o_ref[...] = (lo_ref[...] + t_ref[...]).astype(o_ref.dtype)

        @pl.when(c == 1)
        def _():
            o_ref[...] = (hi_ref[...] + t_ref[...]).astype(o_ref.dtype)

    bs = pl.BlockSpec((1, tr, C), lambda q, i: (q, i, 0))
    hi = pl.BlockSpec((1, tr, C), lambda q, i: (q, nb + i, 0))
    return pl.pallas_call(
        body, grid=(Q, nb), in_specs=[bs, hi, bs], out_specs=bs, out_shape=SDS((Q, half, C), BF16),
        compiler_params=_cparams(("parallel", "parallel")), name=name)(g, g, t)


def _coords():
    return lax.axis_index("x"), lax.axis_index("y"), lax.axis_index("c")


def _flip(v, bit):
    return 1 - v if bit else v


def _all_gather8(v, name):
    R = v.shape[0]

    def body(v_ref, o_ref, send_sems, recv_sems):
        x, y, c = _coords()
        me = 4 * x + 2 * y + c
        o_ref[me] = v_ref[...]
        copies = []
        for k in range(1, 8):
            peer = (_flip(x, k & 4), _flip(y, k & 2), _flip(c, k & 1))
            cp = pltpu.make_async_remote_copy(
                src_ref=v_ref, dst_ref=o_ref.at[me], send_sem=send_sems.at[k - 1], recv_sem=recv_sems.at[k - 1],
                device_id=peer, device_id_type=MESH)
            cp.start()
            copies.append(cp)
        for cp in copies:
            cp.wait_recv()
        for cp in copies:
            cp.wait_send()

    return pl.pallas_call(
        body, in_specs=[VMEM_SPEC], out_specs=VMEM_SPEC, out_shape=SDS((8, R, LANE), v.dtype),
        scratch_shapes=[pltpu.SemaphoreType.DMA((7,)), pltpu.SemaphoreType.DMA((7,))],
        compiler_params=pltpu.CompilerParams(vmem_limit_bytes=VMEM_LIMIT), name=name)(v)


def _sibling_swap_rows(arrs, name):
    n = len(arrs)

    def body(*refs):
        in_refs, out_refs = refs[:n], refs[n:2 * n]
        send_sems, recv_sems = refs[2 * n:]
        x, y, c = _coords()
        copies = []
        for a in range(n):
            Q, R = in_refs[a].shape[0], in_refs[a].shape[1]
            half = R // 2
            src = in_refs[a].at[pl.ds(0, Q), pl.ds(pl.multiple_of((1 - c) * half, 16), half)]
            cp = pltpu.make_async_remote_copy(
                src_ref=src, dst_ref=out_refs[a], send_sem=send_sems.at[a], recv_sem=recv_sems.at[a],
                device_id=(x, y, 1 - c), device_id_type=MESH)
            cp.start()
            copies.append(cp)
        for cp in copies:
            cp.wait_recv()
        for cp in copies:
            cp.wait_send()

    return pl.pallas_call(
        body, in_specs=[ANY] * n, out_specs=[ANY] * n,
        out_shape=[SDS((a.shape[0], a.shape[1] // 2, a.shape[2]), a.dtype) for a in arrs],
        scratch_shapes=[pltpu.SemaphoreType.DMA((n,)), pltpu.SemaphoreType.DMA((n,))],
        name=name)(*arrs)


def _chip_exchange(arrs, *, reduce, name):
    n = len(arrs)

    def body(*refs):
        in_refs, out_refs = refs[:n], refs[n:2 * n]
        ici_send, ici_recv, d2d_send, d2d_recv, loc_sem = refs[2 * n:]
        x, y, c = _coords()
        p = 2 * x + y
        local, first, fwd = [], [], []
        for a in range(n):
            R = out_refs[a].shape[1] // 2
            half = pl.ds(pl.multiple_of(c * R, 16), R)
            if reduce:
                lc = pltpu.make_async_copy(in_refs[a].at[p], out_refs[a].at[p, half], loc_sem.at[a])
            else:
                lc = pltpu.make_async_copy(in_refs[a], out_refs[a].at[p], loc_sem.at[a])
            lc.start()
            local.append(lc)
            for k in range(1, 4):
                qx, qy = _flip(x, k & 2), _flip(y, k & 1)
                src = in_refs[a].at[2 * qx + qy] if reduce else in_refs[a].at[half]
                cp = pltpu.make_async_remote_copy(
                    src_ref=src, dst_ref=out_refs[a].at[p, half], send_sem=ici_send.at[a, k - 1],
                    recv_sem=ici_recv.at[a, k - 1], device_id=(qx, qy, c), device_id_type=MESH)
                cp.start()
                first.append(cp)
        for a in range(n):
            R = out_refs[a].shape[1] // 2
            half = pl.ds(pl.multiple_of(c * R, 16), R)
            for k in range(0 if reduce else 1, 4):
                qx, qy = _flip(x, k & 2), _flip(y, k & 1)
                slot = out_refs[a].at[2 * qx + qy, half]
                if k == 0:
                    local[a].wait()
                else:
                    first[a * 3 + k - 1].wait_recv()
                cp = pltpu.make_async_remote_copy(
                    src_ref=slot, dst_ref=slot, send_sem=d2d_send.at[a, k], recv_sem=d2d_recv.at[a, k],
                    device_id=(x, y, 1 - c), device_id_type=MESH)
                cp.start()
                fwd.append(cp)
        for cp in fwd:
            cp.wait_recv()
        for cp in first + fwd:
            cp.wait_send()
        if not reduce:
            for lc in local:
                lc.wait()

    if reduce:
        out_shape = [SDS((4, 2 * a.shape[1], a.shape[2]), a.dtype) for a in arrs]
    else:
        out_shape = [SDS((4,) + a.shape, a.dtype) for a in arrs]
    return pl.pallas_call(
        body, in_specs=[ANY] * n, out_specs=[ANY] * n, out_shape=out_shape,
        scratch_shapes=[pltpu.SemaphoreType.DMA((n, 3)), pltpu.SemaphoreType.DMA((n, 3)),
                        pltpu.SemaphoreType.DMA((n, 4)), pltpu.SemaphoreType.DMA((n, 4)),
                        pltpu.SemaphoreType.DMA((n,))],
        name=name)(*arrs)


class _LayerExchange:
    aliased = False

    def __init__(self, srcs, lay, reduce):
        self.srcs, self.lay, self.reduce = list(srcs), lay, reduce
        self.n = len(self.srcs)
        if reduce:
            self.out_shapes = [SDS(a.shape, a.dtype) for a in self.srcs]
        else:
            self.out_shapes = [SDS((4,) + a.shape, a.dtype) for a in self.srcs]
        self.sem_shapes = [pltpu.SemaphoreType.DMA((self.n, 3)), pltpu.SemaphoreType.DMA((self.n, 3)),
                           pltpu.SemaphoreType.DMA((self.n,))]

    def _copies(self, src_refs, dst_refs, sems):
        ici_send, ici_recv, loc_sem = sems
        x, y, c = _coords()
        p = 2 * x + y
        local, remote = [], []
        for a in range(self.n):
            src_own = src_refs[a].at[p] if self.reduce else src_refs[a]
            local.append(pltpu.make_async_copy(src_own, dst_refs[a].at[p], loc_sem.at[a]))
            for k in range(1, 4):
                qx, qy = _flip(x, k & 2), _flip(y, k & 1)
                src = src_refs[a].at[2 * qx + qy] if self.reduce else src_refs[a]
                remote.append(pltpu.make_async_remote_copy(
                    src_ref=src, dst_ref=dst_refs[a].at[p], send_sem=ici_send.at[a, k - 1],
                    recv_sem=ici_recv.at[a, k - 1], device_id=(qx, qy, self.lay), device_id_type=MESH))
        return c, local, remote

    def start(self, src_refs, dst_refs, sems):
        c, local, remote = self._copies(src_refs, dst_refs, sems)
        if self.reduce:
            @pl.when(c == self.lay)
            def _():
                for cp in local + remote:
                    cp.start()
        else:
            for cp in local:
                cp.start()

            @pl.when(c == self.lay)
            def _():
                for cp in remote:
                    cp.start()

    def finish(self, src_refs, dst_refs, sems):
        c, local, remote = self._copies(src_refs, dst_refs, sems)
        if self.reduce:
            @pl.when(c == self.lay)
            def _():
                for cp in remote:
                    cp.wait_recv()
                for cp in remote:
                    cp.wait_send()
                for cp in local:
                    cp.wait()
        else:
            @pl.when(c == self.lay)
            def _():
                for cp in remote:
                    cp.wait_recv()
                for cp in remote:
                    cp.wait_send()

            for cp in local:
                cp.wait()

    def run(self, name):
        n = self.n

        def body(*refs):
            src_refs, dst_refs, sems = refs[:n], refs[n:2 * n], refs[2 * n:]
            self.start(src_refs, dst_refs, sems)
            self.finish(src_refs, dst_refs, sems)

        return pl.pallas_call(body, in_specs=[ANY] * n, out_specs=[ANY] * n, out_shape=self.out_shapes,
                              scratch_shapes=self.sem_shapes, name=name)(*self.srcs)


def _call_hosting(body, *, comm, grid, in_specs, out_specs, out_shape, scratch_shapes, name, args, semantics=None):
    n_in, n_out, n_scr = len(args), len(out_shape), len(scratch_shapes)
    if comm is None:
        sem = semantics if semantics is not None else ("parallel",) * len(grid)
        res = pl.pallas_call(body, grid=grid, in_specs=in_specs, out_specs=out_specs, out_shape=out_shape,
                             scratch_shapes=scratch_shapes, compiler_params=_cparams(sem), name=name)(*args)
        return list(res), None
    k = comm.n

    def hosted(*refs):
        ins, cin = refs[:n_in], refs[n_in:n_in + k]
        outs = refs[n_in + k:n_in + k + n_out]
        cout = refs[n_in + k + n_out:n_in + 2 * k + n_out]
        scr = refs[n_in + 2 * k + n_out:n_in + 2 * k + n_out + n_scr]
        sems = refs[n_in + 2 * k + n_out + n_scr:]
        first = pl.program_id(0) == 0
        last = pl.program_id(0) == grid[0] - 1
        for d in range(1, len(grid)):
            first = jnp.logical_and(first, pl.program_id(d) == 0)
            last = jnp.logical_and(last, pl.program_id(d) == grid[d] - 1)

        @pl.when(first)
        def _():
            comm.start(cin, cout, sems)

        body(*ins, *outs, *scr)

        @pl.when(last)
        def _():
            comm.finish(cin, cout, sems)

    aliases = {n_in + j: n_out + j for j in range(k)} if comm.aliased else {}
    res = pl.pallas_call(
        hosted, grid=grid, in_specs=list(in_specs) + [ANY] * k, out_specs=list(out_specs) + [ANY] * k,
        out_shape=list(out_shape) + comm.out_shapes, scratch_shapes=list(scratch_shapes) + comm.sem_shapes,
        input_output_aliases=aliases, compiler_params=_cparams(("arbitrary",) * len(grid)),
        name=name)(*args, *comm.srcs)
    return list(res[:n_out]), list(res[n_out:])


class _SiblingSend:
    aliased = False

    def __init__(self, srcs, src_core):
        self.srcs, self.src_core, self.n = list(srcs), src_core, len(srcs)
        self.out_shapes = [SDS(a.shape, a.dtype) for a in self.srcs]
        self.sem_shapes = [pltpu.SemaphoreType.DMA((self.n,)), pltpu.SemaphoreType.DMA((self.n,))]

    def _copies(self, src_refs, dst_refs, sems):
        x, y, c = _coords()
        return c, [pltpu.make_async_remote_copy(
            src_ref=src_refs[a], dst_ref=dst_refs[a], send_sem=sems[0].at[a], recv_sem=sems[1].at[a],
            device_id=(x, y, 1 - c), device_id_type=MESH) for a in range(self.n)]

    def start(self, src_refs, dst_refs, sems):
        c, copies = self._copies(src_refs, dst_refs, sems)

        @pl.when(c == self.src_core)
        def _():
            for cp in copies:
                cp.start()

    def finish(self, src_refs, dst_refs, sems):
        c, copies = self._copies(src_refs, dst_refs, sems)

        @pl.when(c == self.src_core)
        def _():
            for cp in copies:
                cp.wait_send()

        @pl.when(c != self.src_core)
        def _():
            for cp in copies:
                cp.wait_recv()


class _Handoff:
    aliased = True

    def __init__(self, srcs, lay, slots):
        self.srcs, self.lay, self.slots, self.n = list(srcs), lay, tuple(slots), len(srcs)
        self.out_shapes = [SDS(a.shape, a.dtype) for a in self.srcs]
        ns = len(self.slots)
        self.sem_shapes = [pltpu.SemaphoreType.DMA((self.n, ns)), pltpu.SemaphoreType.DMA((self.n, ns))]

    def _copies(self, dst_refs, sems):
        x, y, c = _coords()
        copies = []
        for a in range(self.n):
            for j, k in enumerate(self.slots):
                slot = dst_refs[a].at[2 * _flip(x, k & 2) + _flip(y, k & 1)]
                copies.append(pltpu.make_async_remote_copy(
                    src_ref=slot, dst_ref=slot, send_sem=sems[0].at[a, j], recv_sem=sems[1].at[a, j],
                    device_id=(x, y, 1 - c), device_id_type=MESH))
        return c, copies

    def start(self, src_refs, dst_refs, sems):
        c, copies = self._copies(dst_refs, sems)

        @pl.when(c == self.lay)
        def _():
            for cp in copies:
                cp.start()

    def finish(self, src_refs, dst_refs, sems):
        c, copies = self._copies(dst_refs, sems)

        @pl.when(c == self.lay)
        def _():
            for cp in copies:
                cp.wait_send()

        @pl.when(c != self.lay)
        def _():
            for cp in copies:
                cp.wait_recv()


def _layer_handoff(bufs, lays, slots, name):
    flat = [b for group in bufs for b in group]
    n = len(flat)
    ns = len(slots)

    def body(*refs):
        out_refs = refs[n:2 * n]
        send_sems, recv_sems = refs[2 * n:]
        x, y, c = _coords()
        i = 0
        for group, lay in zip(bufs, lays):
            copies = []
            for _b in group:
                for j, k in enumerate(slots):
                    slot = out_refs[i].at[2 * _flip(x, k & 2) + _flip(y, k & 1)]
                    copies.append(pltpu.make_async_remote_copy(
                        src_ref=slot, dst_ref=slot, send_sem=send_sems.at[i, j], recv_sem=recv_sems.at[i, j],
                        device_id=(x, y, 1 - c), device_id_type=MESH))
                i += 1

            @pl.when(c == lay)
            def _(copies=copies):
                for cp in copies:
                    cp.start()
                for cp in copies:
                    cp.wait_send()

            @pl.when(c != lay)
            def _(copies=copies):
                for cp in copies:
                    cp.wait_recv()

    return pl.pallas_call(
        body, in_specs=[ANY] * n, out_specs=[ANY] * n, out_shape=[SDS(b.shape, b.dtype) for b in flat],
        input_output_aliases={i: i for i in range(n)},
        scratch_shapes=[pltpu.SemaphoreType.DMA((n, ns)), pltpu.SemaphoreType.DMA((n, ns))], name=name)(*flat)


def _sibling_send(arrs, src_core, name):
    n = len(arrs)

    def body(*refs):
        in_refs, out_refs = refs[:n], refs[n:2 * n]
        send_sems, recv_sems = refs[2 * n:]
        x, y, c = _coords()
        copies = [pltpu.make_async_remote_copy(
            src_ref=in_refs[a], dst_ref=out_refs[a], send_sem=send_sems.at[a], recv_sem=recv_sems.at[a],
            device_id=(x, y, 1 - c), device_id_type=MESH) for a in range(n)]

        @pl.when(c == src_core)
        def _():
            for cp in copies:
                cp.start()
            for cp in copies:
                cp.wait_send()

        @pl.when(c != src_core)
        def _():
            for cp in copies:
                cp.wait_recv()

    return pl.pallas_call(
        body, in_specs=[ANY] * n, out_specs=[ANY] * n, out_shape=[SDS(a.shape, a.dtype) for a in arrs],
        scratch_shapes=[pltpu.SemaphoreType.DMA((n,)), pltpu.SemaphoreType.DMA((n,))], name=name)(*arrs)


def _add_cast_on(a, b, lay, name):
    Q, R, C = b.shape
    tr = _blk_rows(R, max(16, (1 << 19) // C))

    def body(a_ref, b_ref, o_ref):
        @pl.when(lax.axis_index("c") == lay)
        def _():
            o_ref[...] = (a_ref[...] + b_ref[...]).astype(o_ref.dtype)

    bs = pl.BlockSpec((1, tr, C), lambda q, i: (q, i, 0))
    return pl.pallas_call(
        body, grid=(Q, R // tr), in_specs=[bs, bs], out_specs=bs, out_shape=SDS((Q, R, C), BF16),
        compiler_params=_cparams(("parallel", "parallel")), name=name)(a, b)


_IN_SIZES = (512, 128, 128, 512, 512, 512, 8, 512, 512, 512, 3072)
_IN_OFF = tuple(int(v) for v in np.cumsum((0,) + _IN_SIZES))
_IN_Q = N_IN_COLS // 4


def _pack_w_in(w):
    def cols(lo, hi):
        out = []
        while lo < hi:
            q, off = divmod(lo, _IN_Q)
            n = min(hi - lo, _IN_Q - off)
            out.append(w[q, :, off:off + n])
            lo += n
        return out

    fb0, fb1, g0 = _IN_OFF[6], _IN_OFF[7], _IN_OFF[10]
    wqkv = jnp.concatenate(cols(0, fb0) + cols(fb1, g0), axis=1)
    wgf = jnp.concatenate(cols(g0, N_IN_COLS) + cols(fb0, fb1) + [jnp.zeros((w.shape[1], LANE - 8), w.dtype)], axis=1)
    return wqkv, wgf


def _unpack_w_in(dqkv, dgf):
    fb0, fb1, g0 = _IN_OFF[6], _IN_OFF[7], _IN_OFF[10]

    def cols(lo, hi):
        out = []
        while lo < hi:
            if lo < fb0:
                n = min(hi, fb0) - lo
                out.append(dqkv[:, lo:lo + n])
            elif lo < fb1:
                n = min(hi, fb1) - lo
                out.append(dgf[:, 3072 + lo - fb0:3072 + lo - fb0 + n])
            elif lo < g0:
                n = min(hi, g0) - lo
                out.append(dqkv[:, lo - 8:lo - 8 + n])
            else:
                n = hi - lo
                out.append(dgf[:, lo - g0:lo - g0 + n])
            lo += n
        return out

    return jnp.stack([jnp.concatenate(cols(q * _IN_Q, (q + 1) * _IN_Q), axis=1) for q in range(4)])


def _pad_rows(a, rows):
    return jnp.pad(a, ((0, rows - a.shape[0]), (0, 0)))


def _small_pack(parts):
    flat = jnp.concatenate([p.reshape(-1) for p in parts])
    n = flat.shape[0]
    rows = -(-n // LANE)
    rows = -(-rows // 8) * 8
    return jnp.pad(flat, (0, rows * LANE - n)).reshape(rows, LANE)


def _small_unpack(block, shapes):
    flat = block.reshape(-1)
    out, off = [], 0
    for s in shapes:
        n = int(np.prod(s))
        out.append(flat[off:off + n].reshape(s))
        off += n
    return out


def _kv_same(g):
    return 0


def _kv_own(g):
    return g


_mm_plain = _mm


def _layer_fwd(x, mod, p, l, comms):
    sh_m, sc_m, g_m, sh_f, sc_f, g_f = mod
    nm = "l%d_" % l
    h1 = _norm_mod_fwd(x, p["norm_mix_g"], sc_m, sh_m, nm + "norm_mix_fwd")
    qkv = _mm(h1, p["wqkv"], mode="nn", out_dtype=BF16, name=nm + "proj_qkv")
    gf = _mm(h1, p["wgf"], mode="nn", out_dtype=F32, name=nm + "proj_gf", cap_n=640)
    qkv_t = qkv.T
    o_a_t, got_a = _bandT_fwd(qkv_t[0:512], _heads(qkv[:, 512:640], A_KV_HEADS), qkv_t[640:768], p["alibi"],
                              p["sink_tab"], GQ=4, GK=1, P=A_PREV, kvoff=_kv_same, name=nm + "attn_a_fwd",
                              comm=comms[0])
    cum = _fox_cum(gf, p["b_forget_pad"], nm + "fox_cum")
    cum_t = cum[:, :N_HEADS].T
    cc, cr = cum_t[:, :, None], cum_t[:, None, :]
    (o_b_t, lse_b), got_b = _foxT_fwd(qkv_t[768:1280], _heads(qkv[:, 1280:1792], N_HEADS), qkv_t[1792:2304], cc, cr,
                                      nm + "attn_b_fwd", comm=comms[1])
    o_c_t, got_c = _bandT_fwd(qkv_t[2304:2816], _heads(qkv[:, 2816:3328], N_HEADS), qkv_t[3328:3840], p["rel_tab"],
                              p["no_sink"], GQ=2, GK=2, P=C_PREV, kvoff=_kv_own, name=nm + "attn_c_fwd",
                              comm=comms[2])
    o = jnp.concatenate([o_a_t, o_b_t, o_c_t], axis=0).T
    y = _mm(o, p["wb"], mode="nn", out_dtype=F32, groups=3, name=nm + "branch")
    merged = _merge_fwd(y, gf, nm + "merge_fwd")
    mix = _mm(merged, p["wout"], mode="nn", out_dtype=F32, name=nm + "out_proj")
    x1 = _resid_fwd(x, mix, g_m, nm + "resid_mix")
    h2 = _norm_mod_fwd(x1, p["norm_ffn_g"], sc_f, sh_f, nm + "norm_ffn_fwd")
    u = _mm(h2, p["wfi"], mode="nn", out_dtype=F32, name=nm + "ffn_in", cap_n=512)
    a = _swiglu_fwd(u, nm + "swiglu_fwd")
    f = _mm(a, p["wfo"], mode="nn", out_dtype=F32, name=nm + "ffn_out", cap_m=1024)
    x2 = _resid_fwd(x1, f, g_f, nm + "resid_ffn")
    saved = dict(x=x, h1=h1, qkv=qkv, qkv_t=qkv_t, gf=gf, cc=cc, cr=cr, o_b_t=o_b_t, lse_b=lse_b, o=o, y=y, merged=merged,
                 mix=mix, x1=x1, h2=h2, u=u, a=a, f=f)
    return x2, saved, (got_a, got_b, got_c)


def _layer_bwd(dx2, mod, p, s, l, ride=None):
    sh_m, sc_m, g_m, sh_f, sc_f, g_f = mod
    nm = "l%d_" % l

    def _mm(a, b, *, name, **kw):
        comm = ride.comm_for(name) if ride is not None else None
        if comm is None:
            return _mm_plain(a, b, name=nm + name, **kw)
        out, got = _mm_plain(a, b, name=nm + name, comm=comm, **kw)
        ride.done(name, got)
        return out

    dg_f, df = _resid_bwd(dx2, s["f"], g_f, nm + "resid_ffn_bwd")
    da = _mm(df, p["wfo"], mode="nt", out_dtype=F32, name="ffn_out_dx", cap_m=1024, cap_n=1408)
    d_wfo = _mm(s["a"], df, mode="tn", out_dtype=F32, name="ffn_out_dw", cap_m=1408, cap_k=2048)
    du = _swiglu_bwd(da, s["u"], nm + "swiglu_bwd")
    dh2 = _mm(du, p["wfi"], mode="nt", out_dtype=F32, name="ffn_in_dx", cap_m=1024)
    d_wfi = _mm(s["h2"], du, mode="tn", out_dtype=F32, name="ffn_in_dw", cap_m=1024, cap_n=1408, cap_k=2048,
                col_quarters=True)
    dx1, dsc_f, dsh_f, dgn_f = _norm_mod_bwd(s["x1"], [dh2], dx2, p["norm_ffn_g"], sc_f, nm + "norm_ffn_bwd")
    dg_m, dmix = _resid_bwd(dx1, s["mix"], g_m, nm + "resid_mix_bwd")
    dmerged = _mm(dmix, p["wout"], mode="nt", out_dtype=F32, name="out_proj_dx")
    d_wout = _mm(s["merged"], dmix, mode="tn", out_dtype=F32, name="out_proj_dw", cap_m=1024, cap_k=2048)
    dy, dgates = _merge_bwd(dmerged, s["y"], s["gf"], nm + "merge_bwd")
    do = _mm(dy, p["wb"], mode="nt", out_dtype=BF16, groups=3, name="branch_dx")
    d_wb = _mm(s["o"], dy, mode="tn", out_dtype=F32, groups=3, name="branch_dw", cap_k=2048,
               col_quarters=True)
    comms = ride.exchanges() if ride is not None else (None, None, None)
    qkv, qkv_t = s["qkv"], s["qkv_t"]
    do_t = do.T
    (dqa_t, dka_h, dva_h, _, dsink), got_a = _bandT_bwd(
        qkv_t[0:512], _heads(qkv[:, 0:512], N_HEADS), _heads(qkv[:, 512:640], A_KV_HEADS), qkv_t[512:640],
        _heads(qkv[:, 640:768], A_KV_HEADS), do_t[0:512], _heads(do[:, 0:512], N_HEADS), p["alibi"], p["sink_tab"],
        GQ=4, GK=1, P=A_PREV, kvoff=_kv_same, name=nm + "attn_a_bwd", comm=comms[0])
    qb_h = _heads(qkv[:, 768:1280], N_HEADS)
    q_aug = jnp.concatenate([qb_h * 0.125, jnp.ones(qb_h.shape[:2] + (1,), BF16),
                             jnp.zeros(qb_h.shape[:2] + (LANE - HEAD_DIM - 1,), BF16)], axis=2)
    (dqb_t, dkb_h, dvb_h, dck, dcq), got_b = _foxT_bwd(
        qkv_t[768:1280], q_aug, _heads(qkv[:, 1280:1792], N_HEADS), qkv_t[1280:1792],
        _heads(qkv[:, 1792:2304], N_HEADS), s["cc"], s["cr"], s["o_b_t"], do_t[512:1024],
        _heads(do[:, 512:1024], N_HEADS), s["lse_b"], nm + "attn_b_bwd", comm=comms[1])
    dcum = jnp.pad((dck[:, :, 0] + dcq[:, 0, :]).T, ((0, 0), (0, LANE - N_HEADS)))
    dfb, db_forget = _fox_cum_bwd(s["gf"], p["b_forget_pad"], dcum, nm + "fox_cum_bwd")
    (dqc_t, dkc_h, dvc_h, dbias_c, _), got_c = _bandT_bwd(
        qkv_t[2304:2816], _heads(qkv[:, 2304:2816], N_HEADS), _heads(qkv[:, 2816:3328], N_HEADS), qkv_t[2816:3328],
        _heads(qkv[:, 3328:3840], N_HEADS), do_t[1024:1536], _heads(do[:, 1024:1536], N_HEADS), p["rel_tab"],
        p["no_sink"], GQ=2, GK=2, P=C_PREV, kvoff=_kv_own, name=nm + "attn_c_bwd", comm=comms[2])
    d_rel = _rel_reduce(jnp.transpose(_unpair_table(dbias_c), (1, 0, 2)), nm + "rel_reduce")[:, :N_REL]
    dqkv = jnp.concatenate([dqa_t.T, _unheads(dka_h), _unheads(dva_h), dqb_t.T, _unheads(dkb_h), _unheads(dvb_h),
                            dqc_t.T, _unheads(dkc_h), _unheads(dvc_h)], axis=1)
    dgf = jnp.concatenate([dgates, dfb], axis=1)
    if ride is not None:
        ride.exchanged((got_a, got_b, got_c))
    dh1a = _mm(dqkv, p["wqkv"], mode="nt", out_dtype=F32, name="proj_qkv_dx", cap_k=1024)
    dh1b = _mm(dgf, p["wgf"], mode="nt", out_dtype=F32, name="proj_gf_dx", cap_k=640)
    d_wqkv = _mm(s["h1"], dqkv, mode="tn", out_dtype=F32, name="proj_qkv_dw", cap_m=1024, cap_k=2048)
    d_wgf = _mm(s["h1"], dgf, mode="tn", out_dtype=F32, name="proj_gf_dw", cap_m=1024, cap_n=640, cap_k=2048)
    dx, dsc_m, dsh_m, dgn_m = _norm_mod_bwd(s["x"], [dh1a, dh1b], dx1, p["norm_mix_g"], sc_m, nm + "norm_mix_bwd")
    d_mod = jnp.concatenate([dsh_m, dsc_m, dg_m, dsh_f, dsc_f, dg_f], axis=1)[0]
    grads = dict(w_in=_unpack_w_in(d_wqkv, d_wgf), w_branch=d_wb, w_out=d_wout.reshape(4, -1, D_MODEL),
                 w_ffn_in=d_wfi, w_ffn_out=d_wfo.reshape(4, -1, D_MODEL),
                 norm_mix_g=dgn_m[0], norm_ffn_g=dgn_f[0], b_forget=db_forget[0, :N_HEADS],
                 sinks=dsink[:, 0, 0], rel_bias=d_rel, d_mod=d_mod)
    return dx, grads


def kernel(x, c, norm_mix_g, norm_ffn_g, w_ada, b_ada, w_in, b_forget, sinks, rel_bias, w_branch, w_out, w_ffn_in, w_ffn_out, final_norm_g, loss_target, m_norm_mix_g, m_norm_ffn_g, m_w_ada, m_b_ada, m_w_in, m_b_forget, m_sinks, m_rel_bias, m_w_branch, m_w_out, m_w_ffn_in, m_w_ffn_out, m_final_norm_g, v_norm_mix_g, v_norm_ffn_g, v_w_ada, v_b_ada, v_w_in, v_b_forget, v_sinks, v_rel_bias, v_w_branch, v_w_out, v_w_ffn_in, v_w_ffn_out, v_final_norm_g):
    xi, yi, ci = _coords()
    chip = 2 * xi + yi
    dev = 2 * chip + ci
    xs = x[0]
    S = xs.shape[0]
    n_ada = w_ada.shape[2]

    big_names = ("w_in", "w_branch", "w_out", "w_ffn_in", "w_ffn_out")
    big_w = dict(w_in=w_in, w_branch=w_branch, w_out=w_out, w_ffn_in=w_ffn_in, w_ffn_out=w_ffn_out)
    big_m = dict(w_in=m_w_in, w_branch=m_w_branch, w_out=m_w_out, w_ffn_in=m_w_ffn_in, w_ffn_out=m_w_ffn_out)
    big_v = dict(w_in=v_w_in, w_branch=v_w_branch, w_out=v_w_out, w_ffn_in=v_w_ffn_in, w_ffn_out=v_w_ffn_out)
    flat2 = lambda a: a.reshape(-1, a.shape[-1])
    shards = [[flat2(big_w[n][l]).astype(BF16) for n in big_names] for l in range(DEPTH)]
    gw = [_chip_exchange(shards[0], reduce=False, name="weights_gather_l0"), None]
    host_w = ((1, 2), (0, 4), (3,))
    host_g = ((1, 2, 4), (0,), (3,))

    def hosted(arrs, split, reduce):
        return tuple(_LayerExchange([arrs[i] for i in idx], 1, reduce) for idx in split)

    def unsplit(got, split):
        out = [None] * len(big_names)
        for res, idx in zip(got, split):
            for r, i in zip(res, idx):
                out[i] = r
        return out

    c_all = _all_gather8(c.reshape(8, LANE), "gather_c").reshape(8, D_MODEL)
    b_sh = lax.dynamic_slice_in_dim(b_ada, chip * n_ada, n_ada, axis=1)[:, None, :]
    mod_sh = _ada_fwd(_pad_rows(c_all, 16), w_ada, b_sh, "ada_fwd")[:, :8, :]
    mod_all = _all_gather8(mod_sh.reshape(-1, LANE), "gather_mod").reshape(8, DEPTH, 8, n_ada)
    mod_mine = lax.dynamic_index_in_dim(mod_all[0::2], dev, axis=2, keepdims=False)
    mod = mod_mine.transpose(1, 0, 2).reshape(DEPTH, 6, D_MODEL)

    alibi = _pair_table(_alibi_table())
    no_sink = jnp.full((N_HEADS, 8, LANE), NEG_INF, F32)
    def make_params(l, g):
        wqkv, wgf = _pack_w_in(g[0])
        rel_tab = _rel_expand(jnp.pad(rel_bias[l], ((0, 0), (0, N_REL_PAD - N_REL))), "l%d_rel_expand" % l)
        return dict(
            wqkv=wqkv, wgf=wgf, wb=jnp.transpose(g[1], (1, 0, 2)).reshape(3 * BRANCH_W, D_MODEL),
            wout=g[2].reshape(D_MODEL, D_MODEL), wfi=jnp.transpose(g[3], (1, 0, 2)).reshape(D_MODEL, 2 * FFN_H),
            wfo=g[4].reshape(FFN_H, D_MODEL),
            norm_mix_g=norm_mix_g[l][None], norm_ffn_g=norm_ffn_g[l][None],
            b_forget_pad=jnp.pad(b_forget[l], (0, LANE - N_HEADS))[None],
            sink_tab=jnp.broadcast_to(sinks[l][:, None, None], (N_HEADS, 8, LANE)),
            no_sink=no_sink, alibi=alibi, rel_tab=_pair_table(jnp.transpose(rel_tab, (1, 0, 2))))

    mods = [[mod[l, k][None] for k in range(6)] for l in range(DEPTH)]
    none3 = (None, None, None)
    params, saved = [None] * DEPTH, [None] * DEPTH
    params[0] = make_params(0, gw[0])
    h, saved[0], got = _layer_fwd(xs, mods[0], params[0], 0, hosted(shards[1], host_w, False))
    gw[1] = _layer_handoff([unsplit(got, host_w)], [1], (1, 2, 3), "weights_handoff_l1")
    params[1] = make_params(1, gw[1])
    h, saved[1], _ = _layer_fwd(h, mods[1], params[1], 1, none3)
    loss_dev, dh, d_final = _final_loss(h, final_norm_g[None], loss_target[0], "final_loss")
    grads = [None] * DEPTH
    dh, grads[1] = _layer_bwd(dh, mods[1], params[1], saved[1], 1)

    class Layer1Ride:
        sends = {"ffn_out_dx": (4,), "ffn_out_dw": (1, 2), "ffn_in_dx": (3,), "ffn_in_dw": (0,)}
        hands = {"proj_qkv_dx": (0,), "proj_gf_dx": (3,), "proj_qkv_dw": (4,), "proj_gf_dw": (1, 2)}

        def __init__(self, g):
            self.g, self.t = g, [None] * len(g)
            self.parts, self.final = [None] * len(g), [None] * len(g)

        def comm_for(self, name):
            if name in self.sends:
                return _SiblingSend([self.g[i] for i in self.sends[name]], 0)
            if name in self.hands:
                return _Handoff([self.parts[i] for i in self.hands[name]], 1, (0, 1, 2, 3))
            return None

        def done(self, name, got):
            idx, dst = (self.sends[name], self.t) if name in self.sends else (self.hands[name], self.final)
            for i, r in zip(idx, got):
                dst[i] = r

        def exchanges(self):
            sums = [_add_cast_on(a, b, 1, "grads_chip_sum_l1_" + n) for n, a, b in zip(big_names, self.g, self.t)]
            return hosted(sums, host_g, True)

        def exchanged(self, got):
            self.parts = unsplit(got, host_g)

    ride = Layer1Ride([grads[1][n] for n in big_names])
    dh, grads[0] = _layer_bwd(dh, mods[0], params[0], saved[0], 0, ride)
    grad_x = dh[None]
    loss = lax.psum(loss_dev[0, 0], ("x", "y", "c"))
    parts1 = ride.final
    g0 = [grads[0][n] for n in big_names]
    t0 = _sibling_swap_rows(g0, "grads_swap_l0")
    sums0 = [_add_cast_rows(a, b, "grads_chip_sum_l0_" + n) for n, a, b in zip(big_names, g0, t0)]
    parts0 = _chip_exchange(sums0, reduce=True, name="grads_reduce_l0")
    big_out = {}
    for n, p0, p1 in zip(big_names, parts0, parts1):
        shp = big_w[n].shape
        as3 = lambda a: a.reshape(shp[0], -1, shp[-1])
        res = _adamw(as3(big_w[n]), as3(big_m[n]), as3(big_v[n]), [p0, p1], "adamw_" + n)
        big_out[n] = [r.reshape(shp) for r in res]

    small_names = ("norm_mix_g", "norm_ffn_g", "b_ada", "b_forget", "sinks", "rel_bias", "final_norm_g")
    small_w = dict(norm_mix_g=norm_mix_g, norm_ffn_g=norm_ffn_g, b_ada=b_ada, b_forget=b_forget, sinks=sinks,
                   rel_bias=rel_bias, final_norm_g=final_norm_g)
    small_m = dict(norm_mix_g=m_norm_mix_g, norm_ffn_g=m_norm_ffn_g, b_ada=m_b_ada, b_forget=m_b_forget,
                   sinks=m_sinks, rel_bias=m_rel_bias, final_norm_g=m_final_norm_g)
    small_v = dict(norm_mix_g=v_norm_mix_g, norm_ffn_g=v_norm_ffn_g, b_ada=v_b_ada, b_forget=v_b_forget,
                   sinks=v_sinks, rel_bias=v_rel_bias, final_norm_g=v_final_norm_g)
    small_g = dict(
        norm_mix_g=jnp.stack([grads[l]["norm_mix_g"] for l in range(DEPTH)]),
        norm_ffn_g=jnp.stack([grads[l]["norm_ffn_g"] for l in range(DEPTH)]),
        b_ada=jnp.stack([grads[l]["d_mod"] for l in range(DEPTH)]),
        b_forget=jnp.stack([grads[l]["b_forget"] for l in range(DEPTH)]),
        sinks=jnp.stack([grads[l]["sinks"] for l in range(DEPTH)]),
        rel_bias=jnp.stack([grads[l]["rel_bias"] for l in range(DEPTH)]),
        final_norm_g=d_final[0])
    shapes = [small_w[n].shape for n in small_names]
    g_all = _all_gather8(_small_pack([small_g[n] for n in small_names]), "gather_small_grads")
    res = _adamw(_small_pack([small_w[n] for n in small_names])[None], _small_pack([small_m[n] for n in small_names])[None],
                 _small_pack([small_v[n] for n in small_names])[None], g_all, "adamw_small")
    small_out = {n: [] for n in small_names}
    for r in res:
        for n, a in zip(small_names, _small_unpack(r[0], shapes)):
            small_out[n].append(a)
    off_b = sum(int(np.prod(s)) for s in shapes[:2])
    n_mod = DEPTH * 6 * D_MODEL
    dmod_all = g_all.reshape(8, -1)[:, off_b:off_b + n_mod].reshape(8, DEPTH, 6 * D_MODEL)
    dmod_sh = lax.dynamic_slice_in_dim(dmod_all, chip * n_ada, n_ada, axis=2).transpose(1, 0, 2)
    g_ada = _ada_bwd(c_all.T, dmod_sh, "ada_bwd")
    ada_out = _adamw(w_ada, m_w_ada, v_w_ada, flat2(g_ada)[None], "adamw_w_ada")

    order = ("norm_mix_g", "norm_ffn_g", "w_ada", "b_ada", "w_in", "b_forget", "sinks", "rel_bias", "w_branch",
             "w_out", "w_ffn_in", "w_ffn_out", "final_norm_g")

    def pick(n, k):
        if n == "w_ada":
            return ada_out[k]
        if n in big_out:
            return big_out[n][k]
        return small_out[n][k]

    outs = [loss, grad_x]
    for k in range(4):
        outs += [pick(n, k) for n in order]
    return tuple(outs)
```

```python
import functools

import numpy as np
import jax
import jax.numpy as jnp
from jax import lax
from jax.experimental import pallas as pl
from jax.experimental.pallas import tpu as pltpu

F32 = jnp.float32
BF16 = jnp.bfloat16
SDS = jax.ShapeDtypeStruct

D_MODEL = 1024
DEPTH = 2
CHUNK = 64
HEAD_DIM = 64
EPS = 1e-6
NEG_INF = -1e30
N_HEADS = 8
A_KV_HEADS = 2
A_PREV = 2
C_PREV = 8
REL_CLIP = 128
N_REL = 2 * REL_CLIP + 1
N_REL_PAD = 384
BRANCH_W = 512
FFN_H = 2816
FOX_BQ = 256
FOX_BK = 512
BAND_UNROLL_FWD = 4
BAND_UNROLL_BWD = 2
QKV_COLS = 3840
GF_COLS = 3200
N_IN_COLS = 6920
LANE = 128
VMEM_LIMIT = 48 * 1024 * 1024

ADAM_LR = 0.001
ADAM_B1 = 0.9
ADAM_B2 = 0.999
ADAM_EPS = 1e-08
ADAM_WD = 0.01
ADAM_STEP = 10

MESH = pl.DeviceIdType.MESH
ANY = pl.BlockSpec(memory_space=pl.ANY)
VMEM_SPEC = pl.BlockSpec(memory_space=pltpu.VMEM)


def _cparams(sem=None):
    return pltpu.CompilerParams(dimension_semantics=sem, vmem_limit_bytes=VMEM_LIMIT)


def _blk(n, cap):
    if n <= cap:
        return n
    best = None
    for m in range(LANE, cap + 1, LANE):
        if n % m == 0:
            best = m
    assert best is not None, (n, cap)
    return best


def _sigmoid(x):
    return 1.0 / (1.0 + jnp.exp(-x))


def _mm(a, b, *, mode, out_dtype, name, groups=1, cap_m=2048, cap_n=1024, cap_k=1408, col_quarters=False,
        comm=None):
    G = groups
    assert not col_quarters or mode == "tn"
    if mode == "nn":
        M, K, N = a.shape[0], a.shape[1] // G, b.shape[1]
        assert b.shape[0] == G * K
    elif mode == "nt":
        M, K, N = a.shape[0], a.shape[1] // G, b.shape[0] // G
        assert b.shape[1] == K
    else:
        K, M, N = a.shape[0], a.shape[1] // G, b.shape[1] // G
        assert b.shape[0] == K
    bm, bn, bk = _blk(M, cap_m), _blk(N // 4 if col_quarters else N, cap_n), _blk(K, cap_k)
    nm, nn, nk = M // bm, N // bn, K // bk
    if mode == "nn":
        a_spec = pl.BlockSpec((bm, bk), lambda g, i, j, k: (i, g * nk + k))
        b_spec = pl.BlockSpec((bk, bn), lambda g, i, j, k: (g * nk + k, j))
        o_spec = pl.BlockSpec((bm, bn), lambda g, i, j, k: (i, g * nn + j))
        dims = (((1,), (0,)), ((), ()))
        out_shape = (M, G * N)
    elif mode == "nt":
        a_spec = pl.BlockSpec((bm, bk), lambda g, i, j, k: (i, g * nk + k))
        b_spec = pl.BlockSpec((bn, bk), lambda g, i, j, k: (g * nn + j, k))
        o_spec = pl.BlockSpec((bm, bn), lambda g, i, j, k: (i, g * nn + j))
        dims = (((1,), (1,)), ((), ()))
        out_shape = (M, G * N)
    else:
        a_spec = pl.BlockSpec((bk, bm), lambda g, i, j, k: (k, g * nm + i))
        b_spec = pl.BlockSpec((bk, bn), lambda g, i, j, k: (k, g * nn + j))
        dims = (((0,), (0,)), ((), ()))
        if col_quarters:
            nq = nn // 4
            o_spec = pl.BlockSpec((1, bm, bn), lambda g, i, j, k: (j // nq, g * nm + i, j % nq))
            out_shape = (4, G * M, N // 4)
        else:
            o_spec = pl.BlockSpec((bm, bn), lambda g, i, j, k: (g * nm + i, j))
            out_shape = (G * M, N)

    def product(a_ref, b_ref):
        return lax.dot_general(a_ref[...].astype(BF16), b_ref[...].astype(BF16), dims, preferred_element_type=F32)

    def body_one(a_ref, b_ref, o_ref):
        o_ref[...] = product(a_ref, b_ref).astype(o_ref.dtype).reshape(o_ref.shape)

    def body_acc(a_ref, b_ref, o_ref, acc_ref):
        k = pl.program_id(3)

        @pl.when(k == 0)
        def _():
            acc_ref[...] = jnp.zeros_like(acc_ref)

        acc_ref[...] += product(a_ref, b_ref)

        @pl.when(k == nk - 1)
        def _():
            o_ref[...] = acc_ref[...].astype(o_ref.dtype).reshape(o_ref.shape)

    res, got = _call_hosting(
        body_one if nk == 1 else body_acc, comm=comm, grid=(G, nm, nn, nk), in_specs=[a_spec, b_spec],
        out_specs=[o_spec], out_shape=[SDS(out_shape, out_dtype)],
        scratch_shapes=[] if nk == 1 else [pltpu.VMEM((bm, bn), F32)], name=name, args=(a, b),
        semantics=("parallel", "parallel", "parallel", "arbitrary"))
    return res[0] if comm is None else (res[0], got)


def _rows(tm, n, col=0):
    return pl.BlockSpec((tm, n), lambda i: (i, col))


def _vec(n):
    return pl.BlockSpec((1, n), lambda i: (0, 0))


def _tm(S):
    return min(S, 256)


def _norm_mod_fwd(x, g, sc, sh, name):
    S, Dm = x.shape
    tm = _tm(S)

    def body(x_ref, g_ref, sc_ref, sh_ref, h_ref):
        xv = x_ref[...]
        r = lax.rsqrt(jnp.mean(xv * xv, axis=-1, keepdims=True) + EPS)
        h_ref[...] = ((xv * r) * g_ref[...] * (1.0 + sc_ref[...]) + sh_ref[...]).astype(h_ref.dtype)

    return pl.pallas_call(
        body, grid=(S // tm,), in_specs=[_rows(tm, Dm), _vec(Dm), _vec(Dm), _vec(Dm)],
        out_specs=_rows(tm, Dm), out_shape=SDS((S, Dm), BF16),
        compiler_params=_cparams(("parallel",)), name=name)(x, g, sc, sh)


def _norm_mod_bwd(x, dh_list, dres, g, sc, name):
    S, Dm = x.shape
    tm = _tm(S)
    nh = len(dh_list)

    def body(*refs):
        x_ref = refs[0]
        dh_refs = refs[1:1 + nh]
        dres_ref, g_ref, sc_ref, dx_ref, dsc_ref, dsh_ref, dg_ref = refs[1 + nh:]
        i = pl.program_id(0)

        @pl.when(i == 0)
        def _():
            dsc_ref[...] = jnp.zeros_like(dsc_ref)
            dsh_ref[...] = jnp.zeros_like(dsh_ref)
            dg_ref[...] = jnp.zeros_like(dg_ref)

        xv = x_ref[...]
        dh = dh_refs[0][...]
        for r_ in dh_refs[1:]:
            dh = dh + r_[...]
        gv = g_ref[...]
        r = lax.rsqrt(jnp.mean(xv * xv, axis=-1, keepdims=True) + EPS)
        xn = xv * r
        xg = xn * gv
        dsh_ref[...] += jnp.sum(dh, axis=0, keepdims=True)
        dsc_ref[...] += jnp.sum(dh * xg, axis=0, keepdims=True)
        dxg = dh * (1.0 + sc_ref[...])
        dg_ref[...] += jnp.sum(dxg * xn, axis=0, keepdims=True)
        dxn = dxg * gv
        dx_ref[...] = dres_ref[...] + r * (dxn - xn * jnp.mean(dxn * xn, axis=-1, keepdims=True))

    return pl.pallas_call(
        body, grid=(S // tm,),
        in_specs=[_rows(tm, Dm)] * (2 + nh) + [_vec(Dm), _vec(Dm)],
        out_specs=[_rows(tm, Dm), _vec(Dm), _vec(Dm), _vec(Dm)],
        out_shape=[SDS((S, Dm), F32), SDS((1, Dm), F32), SDS((1, Dm), F32), SDS((1, Dm), F32)],
        compiler_params=_cparams(("arbitrary",)), name=name)(x, *dh_list, dres, g, sc)


def _resid_fwd(x, val, g, name):
    S, Dm = x.shape
    tm = _tm(S)

    def body(x_ref, v_ref, g_ref, o_ref):
        o_ref[...] = x_ref[...] + g_ref[...] * v_ref[...]

    return pl.pallas_call(
        body, grid=(S // tm,), in_specs=[_rows(tm, Dm), _rows(tm, Dm), _vec(Dm)],
        out_specs=_rows(tm, Dm), out_shape=SDS((S, Dm), F32),
        compiler_params=_cparams(("parallel",)), name=name)(x, val, g)


def _resid_bwd(dx, val, g, name):
    S, Dm = dx.shape
    tm = _tm(S)

    def body(dx_ref, v_ref, g_ref, dg_ref, dv_ref):
        @pl.when(pl.program_id(0) == 0)
        def _():
            dg_ref[...] = jnp.zeros_like(dg_ref)

        dxv = dx_ref[...]
        dg_ref[...] += jnp.sum(dxv * v_ref[...], axis=0, keepdims=True)
        dv_ref[...] = (dxv * g_ref[...]).astype(dv_ref.dtype)

    return pl.pallas_call(
        body, grid=(S // tm,), in_specs=[_rows(tm, Dm), _rows(tm, Dm), _vec(Dm)],
        out_specs=[_vec(Dm), _rows(tm, Dm)], out_shape=[SDS((1, Dm), F32), SDS((S, Dm), BF16)],
        compiler_params=_cparams(("arbitrary",)), name=name)(dx, val, g)


def _merge_fwd(y, gf, name):
    S = y.shape[0]
    tm = _tm(S)
    W = 3 * D_MODEL

    def body(y_ref, g_ref, o_ref):
        acc = None
        for k in range(3):
            sl = slice(k * D_MODEL, (k + 1) * D_MODEL)
            t = _sigmoid(g_ref[:, sl]) * y_ref[:, sl]
            acc = t if acc is None else acc + t
        o_ref[...] = acc.astype(o_ref.dtype)

    return pl.pallas_call(
        body, grid=(S // tm,), in_specs=[_rows(tm, W), _rows(tm, W)],
        out_specs=_rows(tm, D_MODEL), out_shape=SDS((S, D_MODEL), BF16),
        compiler_params=_cparams(("parallel",)), name=name)(y, gf)


def _merge_bwd(dm, y, gf, name):
    S = y.shape[0]
    tm = _tm(S)
    W = 3 * D_MODEL

    def body(dm_ref, y_ref, g_ref, dy_ref, dg_ref):
        dmv = dm_ref[...]
        for k in range(3):
            sl = slice(k * D_MODEL, (k + 1) * D_MODEL)
            sg = _sigmoid(g_ref[:, sl])
            dy_ref[:, sl] = (dmv * sg).astype(dy_ref.dtype)
            dg_ref[:, sl] = (dmv * y_ref[:, sl] * (sg * (1.0 - sg))).astype(dg_ref.dtype)

    return pl.pallas_call(
        body, grid=(S // tm,), in_specs=[_rows(tm, D_MODEL), _rows(tm, W), _rows(tm, W)],
        out_specs=[_rows(tm, W), _rows(tm, W)], out_shape=[SDS((S, W), BF16), SDS((S, W), BF16)],
        compiler_params=_cparams(("parallel",)), name=name)(dm, y, gf)


def _swiglu_fwd(u, name):
    S = u.shape[0]
    tm = _tm(S)

    def body(g_ref, u_ref, a_ref):
        gv = g_ref[...]
        a_ref[...] = (gv * _sigmoid(gv) * u_ref[...]).astype(a_ref.dtype)

    return pl.pallas_call(
        body, grid=(S // tm,), in_specs=[_rows(tm, FFN_H, 0), _rows(tm, FFN_H, 1)],
        out_specs=_rows(tm, FFN_H), out_shape=SDS((S, FFN_H), BF16),
        compiler_params=_cparams(("parallel",)), name=name)(u, u)


def _swiglu_bwd(da, u, name):
    S = u.shape[0]
    tm = _tm(S)

    def body(da_ref, g_ref, u_ref, du_ref):
        dav = da_ref[...]
        gv = g_ref[...]
        sg = _sigmoid(gv)
        du_ref[:, 0:FFN_H] = (dav * u_ref[...] * (sg * (1.0 + gv * (1.0 - sg)))).astype(du_ref.dtype)
        du_ref[:, FFN_H:2 * FFN_H] = (dav * (gv * sg)).astype(du_ref.dtype)

    return pl.pallas_call(
        body, grid=(S // tm,), in_specs=[_rows(tm, FFN_H), _rows(tm, FFN_H, 0), _rows(tm, FFN_H, 1)],
        out_specs=_rows(tm, 2 * FFN_H), out_shape=SDS((S, 2 * FFN_H), BF16),
        compiler_params=_cparams(("parallel",)), name=name)(da, u, u)


def _final_loss(x, g, target, name):
    S, Dm = x.shape
    tm = _tm(S)

    def body(x_ref, g_ref, t_ref, loss_ref, dx_ref, dg_ref):
        @pl.when(pl.program_id(0) == 0)
        def _():
            loss_ref[...] = jnp.zeros_like(loss_ref)
            dg_ref[...] = jnp.zeros_like(dg_ref)

        xv = x_ref[...]
        gv = g_ref[...]
        r = lax.rsqrt(jnp.mean(xv * xv, axis=-1, keepdims=True) + EPS)
        xn = xv * r
        err = xn * gv - t_ref[...]
        row = jnp.mean(err * err, axis=-1, keepdims=True)
        loss_ref[...] += 0.5 * jnp.sum(row, axis=0, keepdims=True)
        dy = err * (1.0 / Dm)
        dg_ref[...] += jnp.sum(dy * xn, axis=0, keepdims=True)
        dxn = dy * gv
        dx_ref[...] = r * (dxn - xn * jnp.mean(dxn * xn, axis=-1, keepdims=True))

    return pl.pallas_call(
        body, grid=(S // tm,), in_specs=[_rows(tm, Dm), _vec(Dm), _rows(tm, Dm)],
        out_specs=[pl.BlockSpec((1, 1), lambda i: (0, 0)), _rows(tm, Dm), _vec(Dm)],
        out_shape=[SDS((1, 1), F32), SDS((S, Dm), F32), SDS((1, Dm), F32)],
        compiler_params=_cparams(("arbitrary",)), name=name)(x, g, target)


def _band_softmax(qg, kg, bias, sink, valid):
    s = lax.dot_general(qg, kg, (((1,), (1,)), ((), ())), preferred_element_type=F32)
    s = jnp.where(valid, s + bias, NEG_INF)
    m = jnp.maximum(jnp.max(s, axis=-1, keepdims=True), sink)
    e = jnp.exp(s - m)
    es = jnp.exp(sink - m)
    l = jnp.sum(e, axis=-1, keepdims=True) + es
    return e / l, es / l


def _band_attn_fwd(q, k, v, bias, sink, *, G, P, kvoff, name):
    S = q.shape[0]
    ng = q.shape[1] // (G * HEAD_DIM)
    band = (P + 1) * CHUNK
    pad = P * CHUNK
    nc = S // CHUNK

    def body(q_ref, k_ref, v_ref, b_ref, s_ref, o_ref, kp, vp):
        kp[0:pad, :] = jnp.zeros((pad, LANE), BF16)
        vp[0:pad, :] = jnp.zeros((pad, LANE), BF16)
        kp[pad:pad + S, :] = k_ref[...]
        vp[pad:pad + S, :] = v_ref[...]
        col = lax.broadcasted_iota(jnp.int32, (CHUNK, band), 1)

        def step(n, carry):
            r = pl.multiple_of(n * CHUNK, CHUNK)
            qn = q_ref[pl.ds(r, CHUNK), :]
            kb = kp[pl.ds(r, band), :]
            vb = vp[pl.ds(r, band), :]
            valid = col >= (P - n) * CHUNK
            for g in range(G):
                ko = kvoff(g) * HEAD_DIM
                qg = qn[:, g * HEAD_DIM:(g + 1) * HEAD_DIM] * 0.125
                p, _ = _band_softmax(qg, kb[:, ko:ko + HEAD_DIM], b_ref[g], s_ref[g, 0:1, 0:1], valid)
                og = jnp.dot(p.astype(BF16), vb[:, ko:ko + HEAD_DIM], preferred_element_type=F32)
                o_ref[pl.ds(r, CHUNK), g * HEAD_DIM:(g + 1) * HEAD_DIM] = og.astype(o_ref.dtype)
            return carry

        lax.fori_loop(0, nc, step, 0, unroll=min(BAND_UNROLL_FWD, nc))

    GW = G * HEAD_DIM
    return pl.pallas_call(
        body, grid=(ng,),
        in_specs=[pl.BlockSpec((S, GW), lambda i: (0, i)), pl.BlockSpec((S, LANE), lambda i: (0, i)),
                  pl.BlockSpec((S, LANE), lambda i: (0, i)),
                  pl.BlockSpec((G, CHUNK, band), lambda i: (i, 0, 0)),
                  pl.BlockSpec((G, 8, LANE), lambda i: (i, 0, 0))],
        out_specs=pl.BlockSpec((S, GW), lambda i: (0, i)),
        out_shape=SDS((S, ng * GW), BF16),
        scratch_shapes=[pltpu.VMEM((S + pad, LANE), BF16), pltpu.VMEM((S + pad, LANE), BF16)],
        compiler_params=_cparams(("parallel",)), name=name)(q, k, v, bias, sink)


def _band_attn_bwd(q, k, v, bias, sink, do, *, G, P, kvoff, name):
    S = q.shape[0]
    ng = q.shape[1] // (G * HEAD_DIM)
    band = (P + 1) * CHUNK
    pad = P * CHUNK
    nc = S // CHUNK
    TN = (((0,), (0,)), ((), ()))

    def body(q_ref, k_ref, v_ref, b_ref, s_ref, do_ref, dq_ref, dk_ref, dv_ref, db_ref, dsk_ref,
             kp, vp, dkp, dvp):
        kp[0:pad, :] = jnp.zeros((pad, LANE), BF16)
        vp[0:pad, :] = jnp.zeros((pad, LANE), BF16)
        kp[pad:pad + S, :] = k_ref[...]
        vp[pad:pad + S, :] = v_ref[...]
        dkp[...] = jnp.zeros_like(dkp)
        dvp[...] = jnp.zeros_like(dvp)
        db_ref[...] = jnp.zeros_like(db_ref)
        col = lax.broadcasted_iota(jnp.int32, (CHUNK, band), 1)

        def step(n, dsink):
            r = pl.multiple_of(n * CHUNK, CHUNK)
            qn = q_ref[pl.ds(r, CHUNK), :]
            don = do_ref[pl.ds(r, CHUNK), :]
            kb = kp[pl.ds(r, band), :]
            vb = vp[pl.ds(r, band), :]
            valid = col >= (P - n) * CHUNK
            new = []
            for g in range(G):
                ko = kvoff(g) * HEAD_DIM
                lanes = slice(g * HEAD_DIM, (g + 1) * HEAD_DIM)
                qg = qn[:, lanes] * 0.125
                kg = kb[:, ko:ko + HEAD_DIM]
                dog = don[:, lanes]
                p, ps = _band_softmax(qg, kg, b_ref[g], s_ref[g, 0:1, 0:1], valid)
                dp = lax.dot_general(dog, vb[:, ko:ko + HEAD_DIM], (((1,), (1,)), ((), ())),
                                     preferred_element_type=F32)
                delta = jnp.sum(p * dp, axis=-1, keepdims=True)
                ds = p * (dp - delta)
                new.append(dsink[g] - jnp.sum(ps * delta, axis=0, keepdims=True))
                db_ref[g] += ds
                dsb = ds.astype(BF16)
                dq = jnp.dot(dsb, kg, preferred_element_type=F32) * 0.125
                dq_ref[pl.ds(r, CHUNK), lanes] = dq.astype(dq_ref.dtype)
                dkp[pl.ds(r, band), ko:ko + HEAD_DIM] += lax.dot_general(
                    dsb, qg, TN, preferred_element_type=F32)
                dvp[pl.ds(r, band), ko:ko + HEAD_DIM] += lax.dot_general(
                    p.astype(BF16), dog, TN, preferred_element_type=F32)
            return tuple(new)

        dsink = lax.fori_loop(0, nc, step, tuple(jnp.zeros((1, 1), F32) for _ in range(G)),
                              unroll=min(BAND_UNROLL_BWD, nc))
        for g in range(G):
            dsk_ref[g] = jnp.broadcast_to(dsink[g], (8, LANE))
        dk_ref[...] = dkp[pad:pad + S, :].astype(dk_ref.dtype)
        dv_ref[...] = dvp[pad:pad + S, :].astype(dv_ref.dtype)

    GW = G * HEAD_DIM
    qs = pl.BlockSpec((S, GW), lambda i: (0, i))
    ks = pl.BlockSpec((S, LANE), lambda i: (0, i))
    bs = pl.BlockSpec((G, CHUNK, band), lambda i: (i, 0, 0))
    ss = pl.BlockSpec((G, 8, LANE), lambda i: (i, 0, 0))
    return pl.pallas_call(
        body, grid=(ng,), in_specs=[qs, ks, ks, bs, ss, qs],
        out_specs=[qs, ks, ks, bs, ss],
        out_shape=[SDS((S, ng * GW), BF16), SDS((S, ng * LANE), BF16), SDS((S, ng * LANE), BF16),
                   SDS((ng * G, CHUNK, band), F32), SDS((ng * G, 8, LANE), F32)],
        scratch_shapes=[pltpu.VMEM((S + pad, LANE), BF16), pltpu.VMEM((S + pad, LANE), BF16),
                        pltpu.VMEM((S + pad, LANE), F32), pltpu.VMEM((S + pad, LANE), F32)],
        compiler_params=_cparams(("parallel",)), name=name)(q, k, v, bias, sink, do)


PAIR = 2 * CHUNK


def _bandT_softmax(kg, qTg, bias, sink, valid):
    s = jnp.dot(kg, qTg, preferred_element_type=F32)
    s = jnp.where(valid, s + bias, NEG_INF)
    m = jnp.maximum(jnp.max(s, axis=0, keepdims=True), sink)
    e = jnp.exp(s - m)
    es = jnp.exp(sink - m)
    inv = 1.0 / (jnp.sum(e, axis=0, keepdims=True) + es)
    return e * inv, es * inv


def _pad_copy_rows(dst, src, pad, S):
    dst[:, 0:pad, :] = jnp.zeros((dst.shape[0], pad, dst.shape[2]), dst.dtype)
    dst[:, pad:pad + S, :] = src[...]


def _pad_copy_lanes(dst, src, pad, S):
    dst[:, 0:pad] = jnp.zeros((dst.shape[0], pad), dst.dtype)
    dst[:, pad:pad + S] = src[...]


def _bandT_fwd(qT, k_h, vT, bias, sink, *, GQ, GK, P, kvoff, name, comm=None):
    S = qT.shape[1]
    ng = qT.shape[0] // (GQ * HEAD_DIM)
    BU = (P + 2) * CHUNK
    pad = P * CHUNK
    npair = S // PAIR

    def body(qT_ref, k_ref, vT_ref, b_ref, s_ref, oT_ref, kp, vTp):
        _pad_copy_rows(kp, k_ref, pad, S)
        _pad_copy_lanes(vTp, vT_ref, pad, S)
        rowi = lax.broadcasted_iota(jnp.int32, (BU, PAIR), 0)

        def step(n2, carry):
            r = pl.multiple_of(n2 * PAIR, PAIR)
            valid = rowi >= (P - 2 * n2) * CHUNK
            for g in range(GQ):
                kv = kvoff(g)
                hs = slice(g * HEAD_DIM, (g + 1) * HEAD_DIM)
                kvs = slice(kv * HEAD_DIM, (kv + 1) * HEAD_DIM)
                qTg = qT_ref[hs, pl.ds(r, PAIR)] * 0.125
                p, _ = _bandT_softmax(kp[kv, pl.ds(r, BU), :], qTg, b_ref[g], s_ref[g, 0:1, :], valid)
                oTg = jnp.dot(vTp[kvs, pl.ds(r, BU)], p.astype(BF16), preferred_element_type=F32)
                oT_ref[hs, pl.ds(r, PAIR)] = oTg.astype(oT_ref.dtype)
            return carry

        lax.fori_loop(0, npair, step, 0, unroll=min(2, npair))

    res, got = _call_hosting(
        body, comm=comm, grid=(ng,),
        in_specs=[pl.BlockSpec((GQ * HEAD_DIM, S), lambda i: (i, 0)),
                  pl.BlockSpec((GK, S, HEAD_DIM), lambda i: (i, 0, 0)),
                  pl.BlockSpec((GK * HEAD_DIM, S), lambda i: (i, 0)),
                  pl.BlockSpec((GQ, BU, PAIR), lambda i: (i, 0, 0)),
                  pl.BlockSpec((GQ, 8, LANE), lambda i: (i, 0, 0))],
        out_specs=[pl.BlockSpec((GQ * HEAD_DIM, S), lambda i: (i, 0))],
        out_shape=[SDS((ng * GQ * HEAD_DIM, S), BF16)],
        scratch_shapes=[pltpu.VMEM((GK, S + pad, HEAD_DIM), BF16), pltpu.VMEM((GK * HEAD_DIM, S + pad), BF16)],
        name=name, args=(qT, k_h, vT, bias, sink))
    return res[0], got


def _bandT_bwd(qT, q_h, k_h, kT, v_h, doT, do_h, bias, sink, *, GQ, GK, P, kvoff, name, comm=None):
    S = qT.shape[1]
    ng = qT.shape[0] // (GQ * HEAD_DIM)
    BU = (P + 2) * CHUNK
    pad = P * CHUNK
    npair = S // PAIR

    def body(qT_ref, q_ref, k_ref, kT_ref, v_ref, doT_ref, do_ref, b_ref, s_ref,
             dqT_ref, dk_ref, dv_ref, db_ref, dsk_ref, kp, kTp, vp, dkp, dvp):
        _pad_copy_rows(kp, k_ref, pad, S)
        _pad_copy_rows(vp, v_ref, pad, S)
        _pad_copy_lanes(kTp, kT_ref, pad, S)
        dkp[...] = jnp.zeros_like(dkp)
        dvp[...] = jnp.zeros_like(dvp)
        db_ref[...] = jnp.zeros_like(db_ref)
        rowi = lax.broadcasted_iota(jnp.int32, (BU, PAIR), 0)

        def step(n2, dsink):
            r = pl.multiple_of(n2 * PAIR, PAIR)
            valid = rowi >= (P - 2 * n2) * CHUNK
            new = []
            for g in range(GQ):
                kv = kvoff(g)
                hs = slice(g * HEAD_DIM, (g + 1) * HEAD_DIM)
                kvs = slice(kv * HEAD_DIM, (kv + 1) * HEAD_DIM)
                qTg = qT_ref[hs, pl.ds(r, PAIR)] * 0.125
                p, ps = _bandT_softmax(kp[kv, pl.ds(r, BU), :], qTg, b_ref[g], s_ref[g, 0:1, :], valid)
                dp = jnp.dot(vp[kv, pl.ds(r, BU), :], doT_ref[hs, pl.ds(r, PAIR)], preferred_element_type=F32)
                delta = jnp.sum(p * dp, axis=0, keepdims=True)
                ds = p * (dp - delta)
                new.append(dsink[g] - ps * delta)
                db_ref[g] += ds
                dsb = ds.astype(BF16)
                dq = jnp.dot(kTp[kvs, pl.ds(r, BU)], dsb, preferred_element_type=F32) * 0.125
                dqT_ref[hs, pl.ds(r, PAIR)] = dq.astype(dqT_ref.dtype)
                dkp[kv, pl.ds(r, BU), :] += jnp.dot(dsb, q_ref[g, pl.ds(r, PAIR), :] * 0.125,
                                                    preferred_element_type=F32)
                dvp[kv, pl.ds(r, BU), :] += jnp.dot(p.astype(BF16), do_ref[g, pl.ds(r, PAIR), :],
                                                    preferred_element_type=F32)
            return tuple(new)

        dsink = lax.fori_loop(0, npair, step, tuple(jnp.zeros((1, PAIR), F32) for _ in range(GQ)))
        for g in range(GQ):
            dsk_ref[g] = jnp.broadcast_to(jnp.sum(dsink[g], axis=1, keepdims=True), (8, LANE))
        dk_ref[...] = dkp[:, pad:pad + S, :].astype(dk_ref.dtype)
        dv_ref[...] = dvp[:, pad:pad + S, :].astype(dv_ref.dtype)

    qTs = pl.BlockSpec((GQ * HEAD_DIM, S), lambda i: (i, 0))
    qhs = pl.BlockSpec((GQ, S, HEAD_DIM), lambda i: (i, 0, 0))
    khs = pl.BlockSpec((GK, S, HEAD_DIM), lambda i: (i, 0, 0))
    kTs = pl.BlockSpec((GK * HEAD_DIM, S), lambda i: (i, 0))
    bs = pl.BlockSpec((GQ, BU, PAIR), lambda i: (i, 0, 0))
    ss = pl.BlockSpec((GQ, 8, LANE), lambda i: (i, 0, 0))
    nkv = ng * GK
    return _call_hosting(
        body, comm=comm, grid=(ng,), in_specs=[qTs, qhs, khs, kTs, khs, qTs, qhs, bs, ss],
        out_specs=[qTs, khs, khs, bs, ss],
        out_shape=[SDS((ng * GQ * HEAD_DIM, S), BF16), SDS((nkv, S, HEAD_DIM), BF16), SDS((nkv, S, HEAD_DIM), BF16),
                   SDS((ng * GQ, BU, PAIR), F32), SDS((ng * GQ, 8, LANE), F32)],
        scratch_shapes=[pltpu.VMEM((GK, S + pad, HEAD_DIM), BF16), pltpu.VMEM((GK * HEAD_DIM, S + pad), BF16),
                        pltpu.VMEM((GK, S + pad, HEAD_DIM), BF16),
                        pltpu.VMEM((GK, S + pad, HEAD_DIM), F32), pltpu.VMEM((GK, S + pad, HEAD_DIM), F32)],
        name=name, args=(qT, q_h, k_h, kT, v_h, doT, do_h, bias, sink))


def _pair_table(tab):
    t = jnp.transpose(tab, (0, 2, 1))
    lo = jnp.pad(t, ((0, 0), (0, CHUNK), (0, 0)), constant_values=NEG_INF)
    hi = jnp.pad(t, ((0, 0), (CHUNK, 0), (0, 0)), constant_values=NEG_INF)
    return jnp.concatenate([lo, hi], axis=2)


def _unpair_table(d):
    band = d.shape[1] - CHUNK
    return jnp.transpose(d[:, 0:band, 0:CHUNK] + d[:, CHUNK:CHUNK + band, CHUNK:PAIR], (0, 2, 1))


def _heads(a, n):
    return jnp.transpose(a.reshape(a.shape[0], n, HEAD_DIM), (1, 0, 2))


def _unheads(a):
    return jnp.transpose(a, (1, 0, 2)).reshape(a.shape[1], a.shape[0] * HEAD_DIM)


def _fox_logits(qg, kj, cq, ck, r, c, row, col):
    s = lax.dot_general(qg, kj, (((1,), (1,)), ((), ())), preferred_element_type=F32)
    s = s + cq - ck
    return jnp.where(c + col <= r + row, s, NEG_INF)


def _fox_fwd(q, k, v, cc, cr, name):
    S = q.shape[0]
    npair = q.shape[1] // LANE
    BQ, BK = min(FOX_BQ, S), min(FOX_BK, S)
    nq = S // BQ
    heads = [slice(g * HEAD_DIM, (g + 1) * HEAD_DIM) for g in range(2)]

    def body(q_ref, k_ref, v_ref, cc_ref, cr_ref, o_ref, lse_ref):
        row = lax.broadcasted_iota(jnp.int32, (BQ, BK), 0)
        col = lax.broadcasted_iota(jnp.int32, (BQ, BK), 1)

        def qstep(i, carry):
            r = pl.multiple_of(i * BQ, BQ)
            qs = [q_ref[pl.ds(r, BQ), hl] * 0.125 for hl in heads]
            cqs = [cc_ref[g, pl.ds(r, BQ), :] for g in range(2)]

            def kstep(j, st):
                c = pl.multiple_of(j * BK, BK)
                new = []
                for g, hl in enumerate(heads):
                    m, l, acc = st[g]
                    s = _fox_logits(qs[g], k_ref[pl.ds(c, BK), hl], cqs[g], cr_ref[g, :, pl.ds(c, BK)],
                                    r, c, row, col)
                    mn = jnp.maximum(m, jnp.max(s, axis=-1, keepdims=True))
                    al = jnp.exp(m - mn)
                    e = jnp.exp(s - mn)
                    l = al * l + jnp.sum(e, axis=-1, keepdims=True)
                    acc = al * acc + jnp.dot(e.astype(BF16), v_ref[pl.ds(c, BK), hl],
                                             preferred_element_type=F32)
                    new.append((mn, l, acc))
                return tuple(new)

            init = (jnp.full((BQ, 1), NEG_INF, F32), jnp.zeros((BQ, 1), F32), jnp.zeros((BQ, HEAD_DIM), F32))
            st = lax.fori_loop(0, (r + BQ + BK - 1) // BK, kstep, (init, init))
            for g, hl in enumerate(heads):
                m, l, acc = st[g]
                o_ref[pl.ds(r, BQ), hl] = (acc / l).astype(o_ref.dtype)
                lse_ref[g, pl.ds(r, BQ), :] = m + jnp.log(l)
            return carry

        lax.fori_loop(0, nq, qstep, 0)

    blk = pl.BlockSpec((S, LANE), lambda i: (0, i))
    ccs = pl.BlockSpec((2, S, 1), lambda i: (i, 0, 0))
    crs = pl.BlockSpec((2, 1, S), lambda i: (i, 0, 0))
    return pl.pallas_call(
        body, grid=(npair,), in_specs=[blk, blk, blk, ccs, crs], out_specs=[blk, ccs],
        out_shape=[SDS((S, npair * LANE), BF16), SDS((2 * npair, S, 1), F32)],
        compiler_params=_cparams(("parallel",)), name=name)(q, k, v, cc, cr)


def _fox_bwd(q, k, v, cc, cr, o, do, lse, name):
    S = q.shape[0]
    npair = q.shape[1] // LANE
    BQ, BK = min(FOX_BQ, S), min(FOX_BK, S)
    nq = S // BQ
    heads = [slice(g * HEAD_DIM, (g + 1) * HEAD_DIM) for g in range(2)]
    TN = (((0,), (0,)), ((), ()))

    def body(q_ref, k_ref, v_ref, cc_ref, cr_ref, o_ref, do_ref, lse_ref,
             dq_ref, dk_ref, dv_ref, dcr_ref, dcc_ref, dka, dva):
        dka[...] = jnp.zeros_like(dka)
        dva[...] = jnp.zeros_like(dva)
        dcr_ref[...] = jnp.zeros_like(dcr_ref)
        row = lax.broadcasted_iota(jnp.int32, (BQ, BK), 0)
        col = lax.broadcasted_iota(jnp.int32, (BQ, BK), 1)

        def qstep(i, carry):
            r = pl.multiple_of(i * BQ, BQ)
            qs = [q_ref[pl.ds(r, BQ), hl] * 0.125 for hl in heads]
            dos = [do_ref[pl.ds(r, BQ), hl] for hl in heads]
            deltas = [jnp.sum(dos[g].astype(F32) * o_ref[pl.ds(r, BQ), hl].astype(F32), axis=-1, keepdims=True)
                      for g, hl in enumerate(heads)]
            cqs = [cc_ref[g, pl.ds(r, BQ), :] for g in range(2)]
            lses = [lse_ref[g, pl.ds(r, BQ), :] for g in range(2)]

            def kstep(j, st):
                c = pl.multiple_of(j * BK, BK)
                new = []
                for g, hl in enumerate(heads):
                    dq, rs = st[g]
                    kj = k_ref[pl.ds(c, BK), hl]
                    s = _fox_logits(qs[g], kj, cqs[g], cr_ref[g, :, pl.ds(c, BK)], r, c, row, col)
                    p = jnp.exp(s - lses[g])
                    dp = lax.dot_general(dos[g], v_ref[pl.ds(c, BK), hl], (((1,), (1,)), ((), ())),
                                         preferred_element_type=F32)
                    ds = p * (dp - deltas[g])
                    dcr_ref[g, :, pl.ds(c, BK)] -= jnp.sum(ds, axis=0, keepdims=True)
                    dsb = ds.astype(BF16)
                    dka[pl.ds(c, BK), hl] += lax.dot_general(dsb, qs[g], TN, preferred_element_type=F32)
                    dva[pl.ds(c, BK), hl] += lax.dot_general(p.astype(BF16), dos[g], TN,
                                                            preferred_element_type=F32)
                    new.append((dq + jnp.dot(dsb, kj, preferred_element_type=F32),
                                rs + jnp.sum(ds, axis=-1, keepdims=True)))
                return tuple(new)

            init = (jnp.zeros((BQ, HEAD_DIM), F32), jnp.zeros((BQ, 1), F32))
            st = lax.fori_loop(0, (r + BQ + BK - 1) // BK, kstep, (init, init))
            for g, hl in enumerate(heads):
                dq_ref[pl.ds(r, BQ), hl] = (st[g][0] * 0.125).astype(dq_ref.dtype)
                dcc_ref[g, pl.ds(r, BQ), :] = st[g][1]
            return carry

        lax.fori_loop(0, nq, qstep, 0)
        dk_ref[...] = dka[...].astype(dk_ref.dtype)
        dv_ref[...] = dva[...].astype(dv_ref.dtype)

    blk = pl.BlockSpec((S, LANE), lambda i: (0, i))
    ccs = pl.BlockSpec((2, S, 1), lambda i: (i, 0, 0))
    crs = pl.BlockSpec((2, 1, S), lambda i: (i, 0, 0))
    return pl.pallas_call(
        body, grid=(npair,), in_specs=[blk, blk, blk, ccs, crs, blk, blk, ccs],
        out_specs=[blk, blk, blk, crs, ccs],
        out_shape=[SDS((S, npair * LANE), BF16)] * 3 + [SDS((2 * npair, 1, S), F32), SDS((2 * npair, S, 1), F32)],
        scratch_shapes=[pltpu.VMEM((S, LANE), F32), pltpu.VMEM((S, LANE), F32)],
        compiler_params=_cparams(("parallel",)), name=name)(q, k, v, cc, cr, o, do, lse)


def _foxT_logits(kj, qTg, cq, ck, r, c, rowi, coli):
    s = jnp.dot(kj, qTg, preferred_element_type=F32)
    s = s + cq - ck
    return jnp.where(c + rowi <= r + coli, s, NEG_INF)


def _foxT_fwd(qT, k_h, vT, ck, cq, name, comm=None):
    S = qT.shape[1]
    npair = qT.shape[0] // LANE
    BQ, BK = min(FOX_BQ, S), min(FOX_BK, S)
    nq = S // BQ
    heads = [slice(g * HEAD_DIM, (g + 1) * HEAD_DIM) for g in range(2)]

    def body(qT_ref, k_ref, vT_ref, ck_ref, cq_ref, oT_ref, lse_ref):
        rowi = lax.broadcasted_iota(jnp.int32, (BK, BQ), 0)
        coli = lax.broadcasted_iota(jnp.int32, (BK, BQ), 1)

        def qstep(i, carry):
            r = pl.multiple_of(i * BQ, BQ)
            qs = [qT_ref[hs, pl.ds(r, BQ)] * 0.125 for hs in heads]
            cqs = [cq_ref[g, :, pl.ds(r, BQ)] for g in range(2)]

            def kstep(j, st):
                c = pl.multiple_of(j * BK, BK)
                new = []
                for g, hs in enumerate(heads):
                    m, l, acc = st[g]
                    s = _foxT_logits(k_ref[g, pl.ds(c, BK), :], qs[g], cqs[g], ck_ref[g, pl.ds(c, BK), :],
                                     r, c, rowi, coli)
                    mn = jnp.maximum(m, jnp.max(s, axis=0, keepdims=True))
                    al = jnp.exp(m - mn)
                    e = jnp.exp(s - mn)
                    l = al * l + jnp.sum(e, axis=0, keepdims=True)
                    acc = al * acc + jnp.dot(vT_ref[hs, pl.ds(c, BK)], e.astype(BF16), preferred_element_type=F32)
                    new.append((mn, l, acc))
                return tuple(new)

            init = (jnp.full((1, BQ), NEG_INF, F32), jnp.zeros((1, BQ), F32), jnp.zeros((HEAD_DIM, BQ), F32))
            st = lax.fori_loop(0, (r + BQ + BK - 1) // BK, kstep, (init, init))
            for g, hs in enumerate(heads):
                m, l, acc = st[g]
                oT_ref[hs, pl.ds(r, BQ)] = (acc * (1.0 / l)).astype(oT_ref.dtype)
                lse_ref[g, :, pl.ds(r, BQ)] = m + jnp.log(l)
            return carry

        lax.fori_loop(0, nq, qstep, 0)

    fT = pl.BlockSpec((LANE, S), lambda i: (i, 0))
    hm = pl.BlockSpec((2, S, HEAD_DIM), lambda i: (i, 0, 0))
    col = pl.BlockSpec((2, S, 1), lambda i: (i, 0, 0))
    rw = pl.BlockSpec((2, 1, S), lambda i: (i, 0, 0))
    return _call_hosting(
        body, comm=comm, grid=(npair,), in_specs=[fT, hm, fT, col, rw], out_specs=[fT, rw],
        out_shape=[SDS((npair * LANE, S), BF16), SDS((2 * npair, 1, S), F32)], scratch_shapes=[],
        name=name, args=(qT, k_h, vT, ck, cq))


def _foxT_bwd(qT, q_aug, k_h, kT, v_h, ck, cq, oT, doT, do_h, lse, name, comm=None):
    S = qT.shape[1]
    npair = qT.shape[0] // LANE
    BQ, BK = min(FOX_BQ, S), min(FOX_BK, S)
    nq = S // BQ
    heads = [slice(g * HEAD_DIM, (g + 1) * HEAD_DIM) for g in range(2)]

    def body(qT_ref, qa_ref, k_ref, kT_ref, v_ref, ck_ref, cq_ref, oT_ref, doT_ref, do_ref, lse_ref,
             dqT_ref, dk_ref, dv_ref, dck_ref, dcq_ref, dka, dva):
        dka[...] = jnp.zeros_like(dka)
        dva[...] = jnp.zeros_like(dva)
        rowi = lax.broadcasted_iota(jnp.int32, (BK, BQ), 0)
        coli = lax.broadcasted_iota(jnp.int32, (BK, BQ), 1)

        def qstep(i, carry):
            r = pl.multiple_of(i * BQ, BQ)
            qs = [qT_ref[hs, pl.ds(r, BQ)] * 0.125 for hs in heads]
            dos = [doT_ref[hs, pl.ds(r, BQ)] for hs in heads]
            deltas = [jnp.sum(dos[g].astype(F32) * oT_ref[hs, pl.ds(r, BQ)].astype(F32), axis=0, keepdims=True)
                      for g, hs in enumerate(heads)]
            cqs = [cq_ref[g, :, pl.ds(r, BQ)] for g in range(2)]
            lses = [lse_ref[g, :, pl.ds(r, BQ)] for g in range(2)]

            def kstep(j, st):
                c = pl.multiple_of(j * BK, BK)
                new = []
                for g, hs in enumerate(heads):
                    dq, rs = st[g]
                    s = _foxT_logits(k_ref[g, pl.ds(c, BK), :], qs[g], cqs[g], ck_ref[g, pl.ds(c, BK), :],
                                     r, c, rowi, coli)
                    p = jnp.exp(s - lses[g])
                    dp = jnp.dot(v_ref[g, pl.ds(c, BK), :], dos[g], preferred_element_type=F32)
                    ds = p * (dp - deltas[g])
                    dsb = ds.astype(BF16)
                    dka[g, pl.ds(c, BK), :] += jnp.dot(dsb, qa_ref[g, pl.ds(r, BQ), :], preferred_element_type=F32)
                    dva[g, pl.ds(c, BK), :] += jnp.dot(p.astype(BF16), do_ref[g, pl.ds(r, BQ), :],
                                                      preferred_element_type=F32)
                    new.append((dq + jnp.dot(kT_ref[hs, pl.ds(c, BK)], dsb, preferred_element_type=F32),
                                rs + jnp.sum(dsb.astype(F32), axis=0, keepdims=True)))
                return tuple(new)

            init = (jnp.zeros((HEAD_DIM, BQ), F32), jnp.zeros((1, BQ), F32))
            st = lax.fori_loop(0, (r + BQ + BK - 1) // BK, kstep, (init, init))
            for g, hs in enumerate(heads):
                dqT_ref[hs, pl.ds(r, BQ)] = (st[g][0] * 0.125).astype(dqT_ref.dtype)
                dcq_ref[g, :, pl.ds(r, BQ)] = st[g][1]
            return carry

        lax.fori_loop(0, nq, qstep, 0)
        dk_ref[...] = dka[:, :, 0:HEAD_DIM].astype(dk_ref.dtype)
        dck_ref[...] = -dka[:, :, HEAD_DIM:HEAD_DIM + 1]
        dv_ref[...] = dva[...].astype(dv_ref.dtype)

    fT = pl.BlockSpec((LANE, S), lambda i: (i, 0))
    hm = pl.BlockSpec((2, S, HEAD_DIM), lambda i: (i, 0, 0))
    hma = pl.BlockSpec((2, S, LANE), lambda i: (i, 0, 0))
    col = pl.BlockSpec((2, S, 1), lambda i: (i, 0, 0))
    rw = pl.BlockSpec((2, 1, S), lambda i: (i, 0, 0))
    nh = 2 * npair
    return _call_hosting(
        body, comm=comm, grid=(npair,), in_specs=[fT, hma, hm, fT, hm, col, rw, fT, fT, hm, rw],
        out_specs=[fT, hm, hm, col, rw],
        out_shape=[SDS((npair * LANE, S), BF16), SDS((nh, S, HEAD_DIM), BF16), SDS((nh, S, HEAD_DIM), BF16),
                   SDS((nh, S, 1), F32), SDS((nh, 1, S), F32)],
        scratch_shapes=[pltpu.VMEM((2, S, LANE), F32), pltpu.VMEM((2, S, HEAD_DIM), F32)],
        name=name, args=(qT, q_aug, k_h, kT, v_h, ck, cq, oT, doT, do_h, lse))


def _split3(x):
    hi = x.astype(BF16)
    r1 = x - hi.astype(F32)
    mid = r1.astype(BF16)
    lo = (r1 - mid.astype(F32)).astype(BF16)
    return hi, mid, lo


def _tri_dot(tri, x):
    hi, mid, lo = _split3(x)
    return (jnp.dot(tri, hi, preferred_element_type=F32) + jnp.dot(tri, mid, preferred_element_type=F32)
            + jnp.dot(tri, lo, preferred_element_type=F32))


def _fox_cum(gf, bfo, name):
    S = gf.shape[0]
    nb = S // LANE
    fcol = (GF_COLS - LANE) // LANE

    def body(f_ref, b_ref, cum_ref):
        row = lax.broadcasted_iota(jnp.int32, (LANE, LANE), 0)
        col = lax.broadcasted_iota(jnp.int32, (LANE, LANE), 1)
        tri = jnp.where(row >= col, 1.0, 0.0).astype(BF16)
        carry = jnp.zeros((1, LANE), F32)
        for t in range(nb):
            xl = f_ref[t * LANE:(t + 1) * LANE, :] + b_ref[...]
            lf = jnp.minimum(xl, 0.0) - jnp.log(1.0 + jnp.exp(-jnp.abs(xl)))
            cblk = _tri_dot(tri, lf) + carry
            cum_ref[t * LANE:(t + 1) * LANE, :] = cblk
            carry = cblk[LANE - 1:LANE, :]

    return pl.pallas_call(
        body, grid=(1,), in_specs=[pl.BlockSpec((S, LANE), lambda i: (0, fcol)), _vec(LANE)],
        out_specs=pl.BlockSpec((S, LANE), lambda i: (0, 0)), out_shape=SDS((S, LANE), F32),
        compiler_params=_cparams(("arbitrary",)), name=name)(gf, bfo)


def _fox_cum_bwd(gf, bfo, dcum, name):
    S = gf.shape[0]
    nb = S // LANE
    fcol = (GF_COLS - LANE) // LANE

    def body(f_ref, b_ref, dc_ref, df_ref, db_ref):
        row = lax.broadcasted_iota(jnp.int32, (LANE, LANE), 0)
        col = lax.broadcasted_iota(jnp.int32, (LANE, LANE), 1)
        tri = jnp.where(row <= col, 1.0, 0.0).astype(BF16)
        carry = jnp.zeros((1, LANE), F32)
        tot = jnp.zeros((1, LANE), F32)
        for t in range(nb - 1, -1, -1):
            rows = slice(t * LANE, (t + 1) * LANE)
            dlf = _tri_dot(tri, dc_ref[rows, :]) + carry
            carry = dlf[0:1, :]
            xl = f_ref[rows, :] + b_ref[...]
            dfl = dlf * (1.0 / (1.0 + jnp.exp(xl)))
            df_ref[rows, :] = dfl.astype(df_ref.dtype)
            tot = tot + jnp.sum(dfl, axis=0, keepdims=True)
        db_ref[...] = tot

    return pl.pallas_call(
        body, grid=(1,),
        in_specs=[pl.BlockSpec((S, LANE), lambda i: (0, fcol)), _vec(LANE), pl.BlockSpec((S, LANE), lambda i: (0, 0))],
        out_specs=[pl.BlockSpec((S, LANE), lambda i: (0, 0)), _vec(LANE)],
        out_shape=[SDS((S, LANE), BF16), SDS((1, LANE), F32)],
        compiler_params=_cparams(("arbitrary",)), name=name)(gf, bfo, dcum)


def _rel_onehot(qi, band):
    r = lax.broadcasted_iota(jnp.int32, (N_REL_PAD, band), 0)
    j = lax.broadcasted_iota(jnp.int32, (N_REL_PAD, band), 1)
    idx = jnp.clip(C_PREV * CHUNK + qi - j, -REL_CLIP, REL_CLIP) + REL_CLIP
    return jnp.where(r == idx, 1.0, 0.0).astype(BF16)


def _rel_expand(rel, name):
    band = (C_PREV + 1) * CHUNK

    def body(rel_ref, o_ref):
        hi, mid, lo = _split3(rel_ref[...])

        def row(qi, carry):
            oh = _rel_onehot(qi, band)
            o_ref[qi] = (jnp.dot(hi, oh, preferred_element_type=F32) + jnp.dot(mid, oh, preferred_element_type=F32)
                         + jnp.dot(lo, oh, preferred_element_type=F32))
            return carry

        lax.fori_loop(0, CHUNK, row, 0, unroll=2)

    return pl.pallas_call(
        body, grid=(1,), in_specs=[pl.BlockSpec((N_HEADS, N_REL_PAD), lambda i: (0, 0))],
        out_specs=pl.BlockSpec((CHUNK, N_HEADS, band), lambda i: (0, 0, 0)),
        out_shape=SDS((CHUNK, N_HEADS, band), F32),
        compiler_params=_cparams(("arbitrary",)), name=name)(rel)


def _tri_dot_rhs(x, oh):
    hi, mid, lo = _split3(x)
    return (jnp.dot(hi, oh, preferred_element_type=F32) + jnp.dot(mid, oh, preferred_element_type=F32)
            + jnp.dot(lo, oh, preferred_element_type=F32))


def _rel_reduce(dbias, name):
    band = (C_PREV + 1) * CHUNK
    NT = (((1,), (1,)), ((), ()))

    def body(d_ref, o_ref):
        def row(qi, acc):
            oh = _rel_onehot(qi, band)
            hi, mid, lo = _split3(d_ref[qi])
            return acc + (lax.dot_general(hi, oh, NT, preferred_element_type=F32)
                          + lax.dot_general(mid, oh, NT, preferred_element_type=F32)
                          + lax.dot_general(lo, oh, NT, preferred_element_type=F32))

        o_ref[...] = lax.fori_loop(0, CHUNK, row, jnp.zeros((N_HEADS, N_REL_PAD), F32), unroll=2)

    return pl.pallas_call(
        body, grid=(1,), in_specs=[pl.BlockSpec((CHUNK, N_HEADS, band), lambda i: (0, 0, 0))],
        out_specs=pl.BlockSpec((N_HEADS, N_REL_PAD), lambda i: (0, 0)),
        out_shape=SDS((N_HEADS, N_REL_PAD), F32),
        compiler_params=_cparams(("arbitrary",)), name=name)(dbias)


def _alibi_table():
    qi = np.arange(CHUNK)[:, None]
    j = np.arange((A_PREV + 1) * CHUNK)[None, :]
    dist = np.abs(A_PREV * CHUNK + qi - j).astype(np.float32)
    slopes = np.exp2(-8.0 * np.arange(1, N_HEADS + 1, dtype=np.float32) / N_HEADS).astype(np.float32)
    return jnp.asarray(-slopes[:, None, None] * dist[None])


def _ada_fwd(c_all, w, b, name):
    n = w.shape[2]

    def body(c_ref, w_ref, b_ref, o_ref):
        cv = c_ref[...]
        cond = (cv * _sigmoid(cv)).astype(BF16)
        o_ref[0] = jnp.dot(cond, w_ref[0].astype(BF16), preferred_element_type=F32) + b_ref[0]

    return pl.pallas_call(
        body, grid=(DEPTH,),
        in_specs=[pl.BlockSpec((16, D_MODEL), lambda l: (0, 0)), pl.BlockSpec((1, D_MODEL, n), lambda l: (l, 0, 0)),
                  pl.BlockSpec((1, 1, n), lambda l: (l, 0, 0))],
        out_specs=pl.BlockSpec((1, 16, n), lambda l: (l, 0, 0)), out_shape=SDS((DEPTH, 16, n), F32),
        compiler_params=_cparams(("parallel",)), name=name)(c_all, w, b)


def _ada_bwd(c_t, dmod, name):
    n = dmod.shape[2]
    bn = _blk(n, 512)
    tr = 256

    def body(c_ref, d_ref, o_ref):
        cv = c_ref[...]
        cond = (cv * _sigmoid(cv)).astype(BF16).astype(F32)
        dm = d_ref[0].astype(BF16).astype(F32)
        acc = cond[:, 0:1] * dm[0:1, :]
        for b_ in range(1, 8):
            acc = acc + cond[:, b_:b_ + 1] * dm[b_:b_ + 1, :]
        o_ref[0] = acc

    return pl.pallas_call(
        body, grid=(DEPTH, D_MODEL // tr, n // bn),
        in_specs=[pl.BlockSpec((tr, 8), lambda l, i, j: (i, 0)), pl.BlockSpec((1, 8, bn), lambda l, i, j: (l, 0, j))],
        out_specs=pl.BlockSpec((1, tr, bn), lambda l, i, j: (l, i, j)), out_shape=SDS((DEPTH, D_MODEL, n), F32),
        compiler_params=_cparams(("parallel", "parallel", "parallel")), name=name)(c_t, dmod)


def _adamw(w, m, v, parts, name):
    L, R, C = w.shape
    per_layer = isinstance(parts, (list, tuple))
    plist = list(parts) if per_layer else [parts]
    P = plist[0].shape[0]
    tr = _blk_rows(R, max(16, (1 << 18) // C))
    nr = R // tr
    c1 = 1.0 - ADAM_B1 ** ADAM_STEP
    c2 = 1.0 - ADAM_B2 ** ADAM_STEP

    def total(p_ref):
        g = p_ref[0].astype(F32)
        for k in range(1, P):
            g = g + p_ref[k].astype(F32)
        return g

    def body(w_ref, m_ref, v_ref, *rest):
        p_refs, (g_ref, d_ref, nm_ref, nv_ref) = rest[:len(plist)], rest[len(plist):]
        g = total(p_refs[0])
        for k in range(1, len(plist)):
            g = jnp.where(pl.program_id(0) == k, total(p_refs[k]), g)
        mn = ADAM_B1 * m_ref[0] + (1.0 - ADAM_B1) * g
        vn = ADAM_B2 * v_ref[0] + (1.0 - ADAM_B2) * (g * g)
        m_hat = mn / c1
        v_hat = vn / c2
        g_ref[0] = g
        nm_ref[0] = mn
        nv_ref[0] = vn
        d_ref[0] = -ADAM_LR * (m_hat / (jnp.sqrt(v_hat) + ADAM_EPS) + ADAM_WD * w_ref[0])

    rs = pl.BlockSpec((1, tr, C), lambda l, i: (l, i, 0))
    if per_layer:
        pspecs = [pl.BlockSpec((P, tr, C), functools.partial(lambda l, i, k: (0, jnp.where(l == k, i, 0), 0), k=k))
                  for k in range(L)]
    else:
        pspecs = [pl.BlockSpec((P, tr, C), lambda l, i: (0, l * nr + i, 0))]
    return pl.pallas_call(
        body, grid=(L, nr), in_specs=[rs, rs, rs] + pspecs,
        out_specs=[rs, rs, rs, rs], out_shape=[SDS((L, R, C), F32)] * 4,
        compiler_params=_cparams(("parallel", "parallel")), name=name)(w, m, v, *plist)


def _blk_rows(R, cap):
    if R <= cap:
        return R
    best = None
    for t in range(16, cap + 1, 16):
        if R % t == 0:
            best = t
    assert best is not None, (R, cap)
    return best


def _add_cast_rows(g, t, name):
    Q, R, C = g.shape
    half = R // 2
    tr = _blk_rows(half, max(16, (1 << 19) // C))
    nb = half // tr

    def body(lo_ref, hi_ref, t_ref, o_ref):
        c = lax.axis_index("c")

        @pl.when(c == 0)
        def _():
            o_ref[...] = (lo_ref[...] + t_ref[...]).astype(o_ref.dtype)

        @pl.when(c == 1)
        def _():
            o_ref[...] = (hi_ref[...] + t_ref[...]).astype(o_ref.dtype)

    bs = pl.BlockSpec((1, tr, C), lambda q, i: (q, i, 0))
    hi = pl.BlockSpec((1, tr, C), lambda q, i: (q, nb + i, 0))
    return pl.pallas_call(
        body, grid=(Q, nb), in_specs=[bs, hi, bs], out_specs=bs, out_shape=SDS((Q, half, C), BF16),
        compiler_params=_cparams(("parallel", "parallel")), name=name)(g, g, t)


def _coords():
    return lax.axis_index("x"), lax.axis_index("y"), lax.axis_index("c")


def _flip(v, bit):
    return 1 - v if bit else v


def _all_gather8(v, name):
    R = v.shape[0]

    def body(v_ref, o_ref, send_sems, recv_sems):
        x, y, c = _coords()
        me = 4 * x + 2 * y + c
        o_ref[me] = v_ref[...]
        copies = []
        for k in range(1, 8):
            peer = (_flip(x, k & 4), _flip(y, k & 2), _flip(c, k & 1))
            cp = pltpu.make_async_remote_copy(
                src_ref=v_ref, dst_ref=o_ref.at[me], send_sem=send_sems.at[k - 1], recv_sem=recv_sems.at[k - 1],
                device_id=peer, device_id_type=MESH)
            cp.start()
            copies.append(cp)
        for cp in copies:
            cp.wait_recv()
        for cp in copies:
            cp.wait_send()

    return pl.pallas_call(
        body, in_specs=[VMEM_SPEC], out_specs=VMEM_SPEC, out_shape=SDS((8, R, LANE), v.dtype),
        scratch_shapes=[pltpu.SemaphoreType.DMA((7,)), pltpu.SemaphoreType.DMA((7,))],
        compiler_params=pltpu.CompilerParams(vmem_limit_bytes=VMEM_LIMIT), name=name)(v)


def _sibling_swap_rows(arrs, name):
    n = len(arrs)

    def body(*refs):
        in_refs, out_refs = refs[:n], refs[n:2 * n]
        send_sems, recv_sems = refs[2 * n:]
        x, y, c = _coords()
        copies = []
        for a in range(n):
            Q, R = in_refs[a].shape[0], in_refs[a].shape[1]
            half = R // 2
            src = in_refs[a].at[pl.ds(0, Q), pl.ds(pl.multiple_of((1 - c) * half, 16), half)]
            cp = pltpu.make_async_remote_copy(
                src_ref=src, dst_ref=out_refs[a], send_sem=send_sems.at[a], recv_sem=recv_sems.at[a],
                device_id=(x, y, 1 - c), device_id_type=MESH)
            cp.start()
            copies.append(cp)
        for cp in copies:
            cp.wait_recv()
        for cp in copies:
            cp.wait_send()

    return pl.pallas_call(
        body, in_specs=[ANY] * n, out_specs=[ANY] * n,
        out_shape=[SDS((a.shape[0], a.shape[1] // 2, a.shape[2]), a.dtype) for a in arrs],
        scratch_shapes=[pltpu.SemaphoreType.DMA((n,)), pltpu.SemaphoreType.DMA((n,))],
        name=name)(*arrs)


def _chip_exchange(arrs, *, reduce, name):
    n = len(arrs)

    def body(*refs):
        in_refs, out_refs = refs[:n], refs[n:2 * n]
        ici_send, ici_recv, d2d_send, d2d_recv, loc_sem = refs[2 * n:]
        x, y, c = _coords()
        p = 2 * x + y
        local, first, fwd = [], [], []
        for a in range(n):
            R = out_refs[a].shape[1] // 2
            half = pl.ds(pl.multiple_of(c * R, 16), R)
            if reduce:
                lc = pltpu.make_async_copy(in_refs[a].at[p], out_refs[a].at[p, half], loc_sem.at[a])
            else:
                lc = pltpu.make_async_copy(in_refs[a], out_refs[a].at[p], loc_sem.at[a])
            lc.start()
            local.append(lc)
            for k in range(1, 4):
                qx, qy = _flip(x, k & 2), _flip(y, k & 1)
                src = in_refs[a].at[2 * qx + qy] if reduce else in_refs[a].at[half]
                cp = pltpu.make_async_remote_copy(
                    src_ref=src, dst_ref=out_refs[a].at[p, half], send_sem=ici_send.at[a, k - 1],
                    recv_sem=ici_recv.at[a, k - 1], device_id=(qx, qy, c), device_id_type=MESH)
                cp.start()
                first.append(cp)
        for a in range(n):
            R = out_refs[a].shape[1] // 2
            half = pl.ds(pl.multiple_of(c * R, 16), R)
            for k in range(0 if reduce else 1, 4):
                qx, qy = _flip(x, k & 2), _flip(y, k & 1)
                slot = out_refs[a].at[2 * qx + qy, half]
                if k == 0:
                    local[a].wait()
                else:
                    first[a * 3 + k - 1].wait_recv()
                cp = pltpu.make_async_remote_copy(
                    src_ref=slot, dst_ref=slot, send_sem=d2d_send.at[a, k], recv_sem=d2d_recv.at[a, k],
                    device_id=(x, y, 1 - c), device_id_type=MESH)
                cp.start()
                fwd.append(cp)
        for cp in fwd:
            cp.wait_recv()
        for cp in first + fwd:
            cp.wait_send()
        if not reduce:
            for lc in local:
                lc.wait()

    if reduce:
        out_shape = [SDS((4, 2 * a.shape[1], a.shape[2]), a.dtype) for a in arrs]
    else:
        out_shape = [SDS((4,) + a.shape, a.dtype) for a in arrs]
    return pl.pallas_call(
        body, in_specs=[ANY] * n, out_specs=[ANY] * n, out_shape=out_shape,
        scratch_shapes=[pltpu.SemaphoreType.DMA((n, 3)), pltpu.SemaphoreType.DMA((n, 3)),
                        pltpu.SemaphoreType.DMA((n, 4)), pltpu.SemaphoreType.DMA((n, 4)),
                        pltpu.SemaphoreType.DMA((n,))],
        name=name)(*arrs)


class _LayerExchange:
    aliased = False

    def __init__(self, srcs, lay, reduce):
        self.srcs, self.lay, self.reduce = list(srcs), lay, reduce
        self.n = len(self.srcs)
        if reduce:
            self.out_shapes = [SDS(a.shape, a.dtype) for a in self.srcs]
        else:
            self.out_shapes = [SDS((4,) + a.shape, a.dtype) for a in self.srcs]
        self.sem_shapes = [pltpu.SemaphoreType.DMA((self.n, 3)), pltpu.SemaphoreType.DMA((self.n, 3)),
                           pltpu.SemaphoreType.DMA((self.n,))]

    def _copies(self, src_refs, dst_refs, sems):
        ici_send, ici_recv, loc_sem = sems
        x, y, c = _coords()
        p = 2 * x + y
        local, remote = [], []
        for a in range(self.n):
            src_own = src_refs[a].at[p] if self.reduce else src_refs[a]
            local.append(pltpu.make_async_copy(src_own, dst_refs[a].at[p], loc_sem.at[a]))
            for k in range(1, 4):
                qx, qy = _flip(x, k & 2), _flip(y, k & 1)
                src = src_refs[a].at[2 * qx + qy] if self.reduce else src_refs[a]
                remote.append(pltpu.make_async_remote_copy(
                    src_ref=src, dst_ref=dst_refs[a].at[p], send_sem=ici_send.at[a, k - 1],
                    recv_sem=ici_recv.at[a, k - 1], device_id=(qx, qy, self.lay), device_id_type=MESH))
        return c, local, remote

    def start(self, src_refs, dst_refs, sems):
        c, local, remote = self._copies(src_refs, dst_refs, sems)
        if self.reduce:
            @pl.when(c == self.lay)
            def _():
                for cp in local + remote:
                    cp.start()
        else:
            for cp in local:
                cp.start()

            @pl.when(c == self.lay)
            def _():
                for cp in remote:
                    cp.start()

    def finish(self, src_refs, dst_refs, sems):
        c, local, remote = self._copies(src_refs, dst_refs, sems)
        if self.reduce:
            @pl.when(c == self.lay)
            def _():
                for cp in remote:
                    cp.wait_recv()
                for cp in remote:
                    cp.wait_send()
                for cp in local:
                    cp.wait()
        else:
            @pl.when(c == self.lay)
            def _():
                for cp in remote:
                    cp.wait_recv()
                for cp in remote:
                    cp.wait_send()

            for cp in local:
                cp.wait()

    def run(self, name):
        n = self.n

        def body(*refs):
            src_refs, dst_refs, sems = refs[:n], refs[n:2 * n], refs[2 * n:]
            self.start(src_refs, dst_refs, sems)
            self.finish(src_refs, dst_refs, sems)

        return pl.pallas_call(body, in_specs=[ANY] * n, out_specs=[ANY] * n, out_shape=self.out_shapes,
                              scratch_shapes=self.sem_shapes, name=name)(*self.srcs)


def _call_hosting(body, *, comm, grid, in_specs, out_specs, out_shape, scratch_shapes, name, args, semantics=None):
    n_in, n_out, n_scr = len(args), len(out_shape), len(scratch_shapes)
    if comm is None:
        sem = semantics if semantics is not None else ("parallel",) * len(grid)
        res = pl.pallas_call(body, grid=grid, in_specs=in_specs, out_specs=out_specs, out_shape=out_shape,
                             scratch_shapes=scratch_shapes, compiler_params=_cparams(sem), name=name)(*args)
        return list(res), None
    k = comm.n

    def hosted(*refs):
        ins, cin = refs[:n_in], refs[n_in:n_in + k]
        outs = refs[n_in + k:n_in + k + n_out]
        cout = refs[n_in + k + n_out:n_in + 2 * k + n_out]
        scr = refs[n_in + 2 * k + n_out:n_in + 2 * k + n_out + n_scr]
        sems = refs[n_in + 2 * k + n_out + n_scr:]
        first = pl.program_id(0) == 0
        last = pl.program_id(0) == grid[0] - 1
        for d in range(1, len(grid)):
            first = jnp.logical_and(first, pl.program_id(d) == 0)
            last = jnp.logical_and(last, pl.program_id(d) == grid[d] - 1)

        @pl.when(first)
        def _():
            comm.start(cin, cout, sems)

        body(*ins, *outs, *scr)

        @pl.when(last)
        def _():
            comm.finish(cin, cout, sems)

    aliases = {n_in + j: n_out + j for j in range(k)} if comm.aliased else {}
    res = pl.pallas_call(
        hosted, grid=grid, in_specs=list(in_specs) + [ANY] * k, out_specs=list(out_specs) + [ANY] * k,
        out_shape=list(out_shape) + comm.out_shapes, scratch_shapes=list(scratch_shapes) + comm.sem_shapes,
        input_output_aliases=aliases, compiler_params=_cparams(("arbitrary",) * len(grid)),
        name=name)(*args, *comm.srcs)
    return list(res[:n_out]), list(res[n_out:])


class _RowHalfGather:
    aliased = False

    def __init__(self, srcs):
        self.srcs, self.n = list(srcs), len(srcs)
        self.out_shapes = [SDS((4,) + a.shape, a.dtype) for a in self.srcs]
        n = self.n
        self.sem_shapes = [pltpu.SemaphoreType.DMA((n, 3)), pltpu.SemaphoreType.DMA((n, 3)),
                           pltpu.SemaphoreType.DMA((n, 3)), pltpu.SemaphoreType.DMA((n, 3)),
                           pltpu.SemaphoreType.DMA((n,))]

    def _copies(self, src_refs, dst_refs, sems):
        ici_send, ici_recv, d2d_send, d2d_recv, loc_sem = sems
        x, y, c = _coords()
        p = 2 * x + y
        local, first, fwd = [], [], []
        for a in range(self.n):
            R = src_refs[a].shape[0] // 2
            half = pl.ds(pl.multiple_of(c * R, 16), R)
            local.append(pltpu.make_async_copy(src_refs[a], dst_refs[a].at[p], loc_sem.at[a]))
            for k in range(1, 4):
                qx, qy = _flip(x, k & 2), _flip(y, k & 1)
                first.append(pltpu.make_async_remote_copy(
                    src_ref=src_refs[a].at[half], dst_ref=dst_refs[a].at[p, half], send_sem=ici_send.at[a, k - 1],
                    recv_sem=ici_recv.at[a, k - 1], device_id=(qx, qy, c), device_id_type=MESH))
                slot = dst_refs[a].at[2 * qx + qy, half]
                fwd.append(pltpu.make_async_remote_copy(
                    src_ref=slot, dst_ref=slot, send_sem=d2d_send.at[a, k - 1], recv_sem=d2d_recv.at[a, k - 1],
                    device_id=(x, y, 1 - c), device_id_type=MESH))
        return local, first, fwd

    def start(self, src_refs, dst_refs, sems):
        local, first, _ = self._copies(src_refs, dst_refs, sems)
        for cp in local + first:
            cp.start()

    def finish(self, src_refs, dst_refs, sems):
        local, first, fwd = self._copies(src_refs, dst_refs, sems)
        for got, on in zip(first, fwd):
            got.wait_recv()
            on.start()
        for cp in fwd:
            cp.wait_recv()
        for cp in first + fwd:
            cp.wait_send()
        for cp in local:
            cp.wait()

    def run(self, name):
        n = self.n

        def body(*refs):
            src_refs, dst_refs, sems = refs[:n], refs[n:2 * n], refs[2 * n:]
            self.start(src_refs, dst_refs, sems)
            self.finish(src_refs, dst_refs, sems)

        return pl.pallas_call(body, in_specs=[ANY] * n, out_specs=[ANY] * n, out_shape=self.out_shapes,
                              scratch_shapes=self.sem_shapes, name=name)(*self.srcs)


class _SiblingSend:
    aliased = False

    def __init__(self, srcs, src_core):
        self.srcs, self.src_core, self.n = list(srcs), src_core, len(srcs)
        self.out_shapes = [SDS(a.shape, a.dtype) for a in self.srcs]
        self.sem_shapes = [pltpu.SemaphoreType.DMA((self.n,)), pltpu.SemaphoreType.DMA((self.n,))]

    def _copies(self, src_refs, dst_refs, sems):
        x, y, c = _coords()
        return c, [pltpu.make_async_remote_copy(
            src_ref=src_refs[a], dst_ref=dst_refs[a], send_sem=sems[0].at[a], recv_sem=sems[1].at[a],
            device_id=(x, y, 1 - c), device_id_type=MESH) for a in range(self.n)]

    def start(self, src_refs, dst_refs, sems):
        c, copies = self._copies(src_refs, dst_refs, sems)

        @pl.when(c == self.src_core)
        def _():
            for cp in copies:
                cp.start()

    def finish(self, src_refs, dst_refs, sems):
        c, copies = self._copies(src_refs, dst_refs, sems)

        @pl.when(c == self.src_core)
        def _():
            for cp in copies:
                cp.wait_send()

        @pl.when(c != self.src_core)
        def _():
            for cp in copies:
                cp.wait_recv()


class _Handoff:
    aliased = True

    def __init__(self, srcs, lay, slots):
        self.srcs, self.lay, self.slots, self.n = list(srcs), lay, tuple(slots), len(srcs)
        self.out_shapes = [SDS(a.shape, a.dtype) for a in self.srcs]
        ns = len(self.slots)
        self.sem_shapes = [pltpu.SemaphoreType.DMA((self.n, ns)), pltpu.SemaphoreType.DMA((self.n, ns))]

    def _copies(self, dst_refs, sems):
        x, y, c = _coords()
        copies = []
        for a in range(self.n):
            for j, k in enumerate(self.slots):
                slot = dst_refs[a].at[2 * _flip(x, k & 2) + _flip(y, k & 1)]
                copies.append(pltpu.make_async_remote_copy(
                    src_ref=slot, dst_ref=slot, send_sem=sems[0].at[a, j], recv_sem=sems[1].at[a, j],
                    device_id=(x, y, 1 - c), device_id_type=MESH))
        return c, copies

    def start(self, src_refs, dst_refs, sems):
        c, copies = self._copies(dst_refs, sems)

        @pl.when(c == self.lay)
        def _():
            for cp in copies:
                cp.start()

    def finish(self, src_refs, dst_refs, sems):
        c, copies = self._copies(dst_refs, sems)

        @pl.when(c == self.lay)
        def _():
            for cp in copies:
                cp.wait_send()

        @pl.when(c != self.lay)
        def _():
            for cp in copies:
                cp.wait_recv()


def _layer_handoff(bufs, lays, slots, name):
    flat = [b for group in bufs for b in group]
    n = len(flat)
    ns = len(slots)

    def body(*refs):
        out_refs = refs[n:2 * n]
        send_sems, recv_sems = refs[2 * n:]
        x, y, c = _coords()
        i = 0
        for group, lay in zip(bufs, lays):
            copies = []
            for _b in group:
                for j, k in enumerate(slots):
                    slot = out_refs[i].at[2 * _flip(x, k & 2) + _flip(y, k & 1)]
                    copies.append(pltpu.make_async_remote_copy(
                        src_ref=slot, dst_ref=slot, send_sem=send_sems.at[i, j], recv_sem=recv_sems.at[i, j],
                        device_id=(x, y, 1 - c), device_id_type=MESH))
                i += 1

            @pl.when(c == lay)
            def _(copies=copies):
                for cp in copies:
                    cp.start()
                for cp in copies:
                    cp.wait_send()

            @pl.when(c != lay)
            def _(copies=copies):
                for cp in copies:
                    cp.wait_recv()

    return pl.pallas_call(
        body, in_specs=[ANY] * n, out_specs=[ANY] * n, out_shape=[SDS(b.shape, b.dtype) for b in flat],
        input_output_aliases={i: i for i in range(n)},
        scratch_shapes=[pltpu.SemaphoreType.DMA((n, ns)), pltpu.SemaphoreType.DMA((n, ns))], name=name)(*flat)


def _sibling_send(arrs, src_core, name):
    n = len(arrs)

    def body(*refs):
        in_refs, out_refs = refs[:n], refs[n:2 * n]
        send_sems, recv_sems = refs[2 * n:]
        x, y, c = _coords()
        copies = [pltpu.make_async_remote_copy(
            src_ref=in_refs[a], dst_ref=out_refs[a], send_sem=send_sems.at[a], recv_sem=recv_sems.at[a],
            device_id=(x, y, 1 - c), device_id_type=MESH) for a in range(n)]

        @pl.when(c == src_core)
        def _():
            for cp in copies:
                cp.start()
            for cp in copies:
                cp.wait_send()

        @pl.when(c != src_core)
        def _():
            for cp in copies:
                cp.wait_recv()

    return pl.pallas_call(
        body, in_specs=[ANY] * n, out_specs=[ANY] * n, out_shape=[SDS(a.shape, a.dtype) for a in arrs],
        scratch_shapes=[pltpu.SemaphoreType.DMA((n,)), pltpu.SemaphoreType.DMA((n,))], name=name)(*arrs)


def _add_cast_on(a, b, lay, name):
    Q, R, C = b.shape
    tr = _blk_rows(R, max(16, (1 << 19) // C))

    def body(a_ref, b_ref, o_ref):
        @pl.when(lax.axis_index("c") == lay)
        def _():
            o_ref[...] = (a_ref[...] + b_ref[...]).astype(o_ref.dtype)

    bs = pl.BlockSpec((1, tr, C), lambda q, i: (q, i, 0))
    return pl.pallas_call(
        body, grid=(Q, R // tr), in_specs=[bs, bs], out_specs=bs, out_shape=SDS((Q, R, C), BF16),
        compiler_params=_cparams(("parallel", "parallel")), name=name)(a, b)


_IN_SIZES = (512, 128, 128, 512, 512, 512, 8, 512, 512, 512, 3072)
_IN_OFF = tuple(int(v) for v in np.cumsum((0,) + _IN_SIZES))
_IN_Q = N_IN_COLS // 4


def _pack_w_in(w):
    def cols(lo, hi):
        out = []
        while lo < hi:
            q, off = divmod(lo, _IN_Q)
            n = min(hi - lo, _IN_Q - off)
            out.append(w[q, :, off:off + n])
            lo += n
        return out

    fb0, fb1, g0 = _IN_OFF[6], _IN_OFF[7], _IN_OFF[10]
    wqkv = jnp.concatenate(cols(0, fb0) + cols(fb1, g0), axis=1)
    wgf = jnp.concatenate(cols(g0, N_IN_COLS) + cols(fb0, fb1) + [jnp.zeros((w.shape[1], LANE - 8), w.dtype)], axis=1)
    return wqkv, wgf


def _unpack_w_in(dqkv, dgf):
    fb0, fb1, g0 = _IN_OFF[6], _IN_OFF[7], _IN_OFF[10]

    def cols(lo, hi):
        out = []
        while lo < hi:
            if lo < fb0:
                n = min(hi, fb0) - lo
                out.append(dqkv[:, lo:lo + n])
            elif lo < fb1:
                n = min(hi, fb1) - lo
                out.append(dgf[:, 3072 + lo - fb0:3072 + lo - fb0 + n])
            elif lo < g0:
                n = min(hi, g0) - lo
                out.append(dqkv[:, lo - 8:lo - 8 + n])
            else:
                n = hi - lo
                out.append(dgf[:, lo - g0:lo - g0 + n])
            lo += n
        return out

    return jnp.stack([jnp.concatenate(cols(q * _IN_Q, (q + 1) * _IN_Q), axis=1) for q in range(4)])


def _pad_rows(a, rows):
    return jnp.pad(a, ((0, rows - a.shape[0]), (0, 0)))


def _small_pack(parts):
    flat = jnp.concatenate([p.reshape(-1) for p in parts])
    n = flat.shape[0]
    rows = -(-n // LANE)
    rows = -(-rows // 8) * 8
    return jnp.pad(flat, (0, rows * LANE - n)).reshape(rows, LANE)


def _small_unpack(block, shapes):
    flat = block.reshape(-1)
    out, off = [], 0
    for s in shapes:
        n = int(np.prod(s))
        out.append(flat[off:off + n].reshape(s))
        off += n
    return out


def _kv_same(g):
    return 0


def _kv_own(g):
    return g


_mm_plain = _mm


def _mm_hosting(a, b, *, comm, **kw):
    if comm is None:
        return _mm(a, b, **kw), None
    return _mm(a, b, comm=comm, **kw)


def _layer_fwd(x, mod, p, l, ride):
    sh_m, sc_m, g_m, sh_f, sc_f, g_f = mod
    nm = "l%d_" % l

    def carried(name, run):
        res, got = run(ride.comm_for(name))
        if got is not None:
            ride.done(name, got)
        return res

    h1 = _norm_mod_fwd(x, p["norm_mix_g"], sc_m, sh_m, nm + "norm_mix_fwd")
    qkv = carried("proj_qkv", lambda cm: _mm_hosting(h1, p["wqkv"], mode="nn", out_dtype=BF16,
                                                     name=nm + "proj_qkv", comm=cm))
    gf = _mm(h1, p["wgf"], mode="nn", out_dtype=F32, name=nm + "proj_gf", cap_n=640)
    qkv_t = qkv.T
    o_a_t = carried("attn_a", lambda cm: _bandT_fwd(
        qkv_t[0:512], _heads(qkv[:, 512:640], A_KV_HEADS), qkv_t[640:768], p["alibi"], p["sink_tab"],
        GQ=4, GK=1, P=A_PREV, kvoff=_kv_same, name=nm + "attn_a_fwd", comm=cm))
    cum = _fox_cum(gf, p["b_forget_pad"], nm + "fox_cum")
    cum_t = cum[:, :N_HEADS].T
    cc, cr = cum_t[:, :, None], cum_t[:, None, :]
    o_b_t, lse_b = carried("attn_b", lambda cm: _foxT_fwd(
        qkv_t[768:1280], _heads(qkv[:, 1280:1792], N_HEADS), qkv_t[1792:2304], cc, cr, nm + "attn_b_fwd", comm=cm))
    o_c_t = carried("attn_c", lambda cm: _bandT_fwd(
        qkv_t[2304:2816], _heads(qkv[:, 2816:3328], N_HEADS), qkv_t[3328:3840], p["rel_tab"], p["no_sink"],
        GQ=2, GK=2, P=C_PREV, kvoff=_kv_own, name=nm + "attn_c_fwd", comm=cm))
    p = dict(p, **ride.late_weights())
    o = jnp.concatenate([o_a_t, o_b_t, o_c_t], axis=0).T
    y = _mm(o, p["wb"], mode="nn", out_dtype=F32, groups=3, name=nm + "branch")
    merged = _merge_fwd(y, gf, nm + "merge_fwd")
    mix = _mm(merged, p["wout"], mode="nn", out_dtype=F32, name=nm + "out_proj")
    x1 = _resid_fwd(x, mix, g_m, nm + "resid_mix")
    h2 = _norm_mod_fwd(x1, p["norm_ffn_g"], sc_f, sh_f, nm + "norm_ffn_fwd")
    u = _mm(h2, p["wfi"], mode="nn", out_dtype=F32, name=nm + "ffn_in", cap_n=512)
    a = _swiglu_fwd(u, nm + "swiglu_fwd")
    f = _mm(a, p["wfo"], mode="nn", out_dtype=F32, name=nm + "ffn_out", cap_m=1024)
    x2 = _resid_fwd(x1, f, g_f, nm + "resid_ffn")
    saved = dict(x=x, h1=h1, qkv=qkv, qkv_t=qkv_t, gf=gf, cc=cc, cr=cr, o_b_t=o_b_t, lse_b=lse_b, o=o, y=y, merged=merged,
                 mix=mix, x1=x1, h2=h2, u=u, a=a, f=f)
    return x2, saved, p


def _layer_bwd(dx2, mod, p, s, l, ride=None):
    sh_m, sc_m, g_m, sh_f, sc_f, g_f = mod
    nm = "l%d_" % l

    def _mm(a, b, *, name, **kw):
        comm = ride.comm_for(name) if ride is not None else None
        if comm is None:
            return _mm_plain(a, b, name=nm + name, **kw)
        out, got = _mm_plain(a, b, name=nm + name, comm=comm, **kw)
        ride.done(name, got)
        return out

    dg_f, df = _resid_bwd(dx2, s["f"], g_f, nm + "resid_ffn_bwd")
    da = _mm(df, p["wfo"], mode="nt", out_dtype=F32, name="ffn_out_dx", cap_m=1024, cap_n=1408)
    d_wfo = _mm(s["a"], df, mode="tn", out_dtype=F32, name="ffn_out_dw", cap_m=1408, cap_k=2048)
    du = _swiglu_bwd(da, s["u"], nm + "swiglu_bwd")
    dh2 = _mm(du, p["wfi"], mode="nt", out_dtype=F32, name="ffn_in_dx", cap_m=1024)
    d_wfi = _mm(s["h2"], du, mode="tn", out_dtype=F32, name="ffn_in_dw", cap_m=1024, cap_n=1408, cap_k=2048,
                col_quarters=True)
    dx1, dsc_f, dsh_f, dgn_f = _norm_mod_bwd(s["x1"], [dh2], dx2, p["norm_ffn_g"], sc_f, nm + "norm_ffn_bwd")
    dg_m, dmix = _resid_bwd(dx1, s["mix"], g_m, nm + "resid_mix_bwd")
    dmerged = _mm(dmix, p["wout"], mode="nt", out_dtype=F32, name="out_proj_dx")
    d_wout = _mm(s["merged"], dmix, mode="tn", out_dtype=F32, name="out_proj_dw", cap_m=1024, cap_k=2048)
    dy, dgates = _merge_bwd(dmerged, s["y"], s["gf"], nm + "merge_bwd")
    do = _mm(dy, p["wb"], mode="nt", out_dtype=BF16, groups=3, name="branch_dx")
    d_wb = _mm(s["o"], dy, mode="tn", out_dtype=F32, groups=3, name="branch_dw", cap_k=2048,
               col_quarters=True)
    comms = ride.exchanges() if ride is not None else (None, None, None)
    qkv, qkv_t = s["qkv"], s["qkv_t"]
    do_t = do.T
    (dqa_t, dka_h, dva_h, _, dsink), got_a = _bandT_bwd(
        qkv_t[0:512], _heads(qkv[:, 0:512], N_HEADS), _heads(qkv[:, 512:640], A_KV_HEADS), qkv_t[512:640],
        _heads(qkv[:, 640:768], A_KV_HEADS), do_t[0:512], _heads(do[:, 0:512], N_HEADS), p["alibi"], p["sink_tab"],
        GQ=4, GK=1, P=A_PREV, kvoff=_kv_same, name=nm + "attn_a_bwd", comm=comms[0])
    qb_h = _heads(qkv[:, 768:1280], N_HEADS)
    q_aug = jnp.concatenate([qb_h * 0.125, jnp.ones(qb_h.shape[:2] + (1,), BF16),
                             jnp.zeros(qb_h.shape[:2] + (LANE - HEAD_DIM - 1,), BF16)], axis=2)
    (dqb_t, dkb_h, dvb_h, dck, dcq), got_b = _foxT_bwd(
        qkv_t[768:1280], q_aug, _heads(qkv[:, 1280:1792], N_HEADS), qkv_t[1280:1792],
        _heads(qkv[:, 1792:2304], N_HEADS), s["cc"], s["cr"], s["o_b_t"], do_t[512:1024],
        _heads(do[:, 512:1024], N_HEADS), s["lse_b"], nm + "attn_b_bwd", comm=comms[1])
    dcum = jnp.pad((dck[:, :, 0] + dcq[:, 0, :]).T, ((0, 0), (0, LANE - N_HEADS)))
    dfb, db_forget = _fox_cum_bwd(s["gf"], p["b_forget_pad"], dcum, nm + "fox_cum_bwd")
    (dqc_t, dkc_h, dvc_h, dbias_c, _), got_c = _bandT_bwd(
        qkv_t[2304:2816], _heads(qkv[:, 2304:2816], N_HEADS), _heads(qkv[:, 2816:3328], N_HEADS), qkv_t[2816:3328],
        _heads(qkv[:, 3328:3840], N_HEADS), do_t[1024:1536], _heads(do[:, 1024:1536], N_HEADS), p["rel_tab"],
        p["no_sink"], GQ=2, GK=2, P=C_PREV, kvoff=_kv_own, name=nm + "attn_c_bwd", comm=comms[2])
    d_rel = _rel_reduce(jnp.transpose(_unpair_table(dbias_c), (1, 0, 2)), nm + "rel_reduce")[:, :N_REL]
    dqkv = jnp.concatenate([dqa_t.T, _unheads(dka_h), _unheads(dva_h), dqb_t.T, _unheads(dkb_h), _unheads(dvb_h),
                            dqc_t.T, _unheads(dkc_h), _unheads(dvc_h)], axis=1)
    dgf = jnp.concatenate([dgates, dfb], axis=1)
    if ride is not None:
        ride.exchanged((got_a, got_b, got_c))
    dh1a = _mm(dqkv, p["wqkv"], mode="nt", out_dtype=F32, name="proj_qkv_dx", cap_k=1024)
    dh1b = _mm(dgf, p["wgf"], mode="nt", out_dtype=F32, name="proj_gf_dx", cap_k=640)
    d_wqkv = _mm(s["h1"], dqkv, mode="tn", out_dtype=F32, name="proj_qkv_dw", cap_m=1024, cap_k=2048)
    d_wgf = _mm(s["h1"], dgf, mode="tn", out_dtype=F32, name="proj_gf_dw", cap_m=1024, cap_n=640, cap_k=2048)
    dx, dsc_m, dsh_m, dgn_m = _norm_mod_bwd(s["x"], [dh1a, dh1b], dx1, p["norm_mix_g"], sc_m, nm + "norm_mix_bwd")
    d_mod = jnp.concatenate([dsh_m, dsc_m, dg_m, dsh_f, dsc_f, dg_f], axis=1)[0]
    grads = dict(w_in=_unpack_w_in(d_wqkv, d_wgf), w_branch=d_wb, w_out=d_wout.reshape(4, -1, D_MODEL),
                 w_ffn_in=d_wfi, w_ffn_out=d_wfo.reshape(4, -1, D_MODEL),
                 norm_mix_g=dgn_m[0], norm_ffn_g=dgn_f[0], b_forget=db_forget[0, :N_HEADS],
                 sinks=dsink[:, 0, 0], rel_bias=d_rel, d_mod=d_mod)
    return dx, grads


def kernel(x, c, norm_mix_g, norm_ffn_g, w_ada, b_ada, w_in, b_forget, sinks, rel_bias, w_branch, w_out, w_ffn_in, w_ffn_out, final_norm_g, loss_target, m_norm_mix_g, m_norm_ffn_g, m_w_ada, m_b_ada, m_w_in, m_b_forget, m_sinks, m_rel_bias, m_w_branch, m_w_out, m_w_ffn_in, m_w_ffn_out, m_final_norm_g, v_norm_mix_g, v_norm_ffn_g, v_w_ada, v_b_ada, v_w_in, v_b_forget, v_sinks, v_rel_bias, v_w_branch, v_w_out, v_w_ffn_in, v_w_ffn_out, v_final_norm_g):
    xi, yi, ci = _coords()
    chip = 2 * xi + yi
    dev = 2 * chip + ci
    xs = x[0]
    S = xs.shape[0]
    n_ada = w_ada.shape[2]

    big_names = ("w_in", "w_branch", "w_out", "w_ffn_in", "w_ffn_out")
    big_w = dict(w_in=w_in, w_branch=w_branch, w_out=w_out, w_ffn_in=w_ffn_in, w_ffn_out=w_ffn_out)
    big_m = dict(w_in=m_w_in, w_branch=m_w_branch, w_out=m_w_out, w_ffn_in=m_w_ffn_in, w_ffn_out=m_w_ffn_out)
    big_v = dict(w_in=v_w_in, w_branch=v_w_branch, w_out=v_w_out, w_ffn_in=v_w_ffn_in, w_ffn_out=v_w_ffn_out)
    flat2 = lambda a: a.reshape(-1, a.shape[-1])
    shards = [[flat2(big_w[n][l]).astype(BF16) for n in big_names] for l in range(DEPTH)]
    gw = [[None] * len(big_names) for _ in range(DEPTH)]
    gw[0][0] = _RowHalfGather([shards[0][0]]).run("weights_gather_w_in_l0")[0]
    host_g = ((1, 2, 4), (0,), (3,))

    class WeightRide:
        def __init__(self, l, plan):
            self.l, self.plan = l, plan

        def comm_for(self, name):
            if name not in self.plan:
                return None
            lay, idx = self.plan[name]
            return _RowHalfGather([shards[lay][i] for i in idx])

        def done(self, name, got):
            lay, idx = self.plan[name]
            for i, r in zip(idx, got):
                gw[lay][i] = r

        def late_weights(self):
            g = gw[self.l]
            return dict(wb=jnp.transpose(g[1], (1, 0, 2)).reshape(3 * BRANCH_W, D_MODEL),
                        wout=g[2].reshape(D_MODEL, D_MODEL),
                        wfi=jnp.transpose(g[3], (1, 0, 2)).reshape(D_MODEL, 2 * FFN_H),
                        wfo=g[4].reshape(FFN_H, D_MODEL))

    weight_plan = [
        {"proj_qkv": (0, (1, 2)), "attn_a": (0, (4,)), "attn_b": (0, (3,)), "attn_c": (1, (0,))},
        {"attn_a": (1, (1, 2)), "attn_b": (1, (3,)), "attn_c": (1, (4,))}]

    def hosted(arrs, split, reduce):
        return tuple(_LayerExchange([arrs[i] for i in idx], 1, reduce) for idx in split)

    def unsplit(got, split):
        out = [None] * len(big_names)
        for res, idx in zip(got, split):
            for r, i in zip(res, idx):
                out[i] = r
        return out

    c_all = _all_gather8(c.reshape(8, LANE), "gather_c").reshape(8, D_MODEL)
    b_sh = lax.dynamic_slice_in_dim(b_ada, chip * n_ada, n_ada, axis=1)[:, None, :]
    mod_sh = _ada_fwd(_pad_rows(c_all, 16), w_ada, b_sh, "ada_fwd")[:, :8, :]
    mod_all = _all_gather8(mod_sh.reshape(-1, LANE), "gather_mod").reshape(8, DEPTH, 8, n_ada)
    mod_mine = lax.dynamic_index_in_dim(mod_all[0::2], dev, axis=2, keepdims=False)
    mod = mod_mine.transpose(1, 0, 2).reshape(DEPTH, 6, D_MODEL)

    alibi = _pair_table(_alibi_table())
    no_sink = jnp.full((N_HEADS, 8, LANE), NEG_INF, F32)
    def make_params(l):
        wqkv, wgf = _pack_w_in(gw[l][0])
        rel_tab = _rel_expand(jnp.pad(rel_bias[l], ((0, 0), (0, N_REL_PAD - N_REL))), "l%d_rel_expand" % l)
        return dict(
            wqkv=wqkv, wgf=wgf, norm_mix_g=norm_mix_g[l][None], norm_ffn_g=norm_ffn_g[l][None],
            b_forget_pad=jnp.pad(b_forget[l], (0, LANE - N_HEADS))[None],
            sink_tab=jnp.broadcast_to(sinks[l][:, None, None], (N_HEADS, 8, LANE)),
            no_sink=no_sink, alibi=alibi, rel_tab=_pair_table(jnp.transpose(rel_tab, (1, 0, 2))))

    mods = [[mod[l, k][None] for k in range(6)] for l in range(DEPTH)]
    params, saved = [None] * DEPTH, [None] * DEPTH
    h = xs
    for l in range(DEPTH):
        h, saved[l], params[l] = _layer_fwd(h, mods[l], make_params(l), l, WeightRide(l, weight_plan[l]))
    loss_dev, dh, d_final = _final_loss(h, final_norm_g[None], loss_target[0], "final_loss")
    grads = [None] * DEPTH
    dh, grads[1] = _layer_bwd(dh, mods[1], params[1], saved[1], 1)

    class Layer1Ride:
        sends = {"ffn_out_dx": (4,), "ffn_out_dw": (1, 2), "ffn_in_dx": (3,), "ffn_in_dw": (0,)}
        hands = {"proj_qkv_dx": (0,), "proj_gf_dx": (3,), "proj_qkv_dw": (4,), "proj_gf_dw": (1, 2)}

        def __init__(self, g):
            self.g, self.t = g, [None] * len(g)
            self.parts, self.final = [None] * len(g), [None] * len(g)

        def comm_for(self, name):
            if name in self.sends:
                return _SiblingSend([self.g[i] for i in self.sends[name]], 0)
            if name in self.hands:
                return _Handoff([self.parts[i] for i in self.hands[name]], 1, (0, 1, 2, 3))
            return None

        def done(self, name, got):
            idx, dst = (self.sends[name], self.t) if name in self.sends else (self.hands[name], self.final)
            for i, r in zip(idx, got):
                dst[i] = r

        def exchanges(self):
            sums = [_add_cast_on(a, b, 1, "grads_chip_sum_l1_" + n) for n, a, b in zip(big_names, self.g, self.t)]
            return hosted(sums, host_g, True)

        def exchanged(self, got):
            self.parts = unsplit(got, host_g)

    ride = Layer1Ride([grads[1][n] for n in big_names])
    dh, grads[0] = _layer_bwd(dh, mods[0], params[0], saved[0], 0, ride)
    grad_x = dh[None]
    loss = lax.psum(loss_dev[0, 0], ("x", "y", "c"))
    parts1 = ride.final
    g0 = [grads[0][n] for n in big_names]
    t0 = _sibling_swap_rows(g0, "grads_swap_l0")
    sums0 = [_add_cast_rows(a, b, "grads_chip_sum_l0_" + n) for n, a, b in zip(big_names, g0, t0)]
    parts0 = _chip_exchange(sums0, reduce=True, name="grads_reduce_l0")
    big_out = {}
    for n, p0, p1 in zip(big_names, parts0, parts1):
        shp = big_w[n].shape
        as3 = lambda a: a.reshape(shp[0], -1, shp[-1])
        res = _adamw(as3(big_w[n]), as3(big_m[n]), as3(big_v[n]), [p0, p1], "adamw_" + n)
        big_out[n] = [r.reshape(shp) for r in res]

    small_names = ("norm_mix_g", "norm_ffn_g", "b_ada", "b_forget", "sinks", "rel_bias", "final_norm_g")
    small_w = dict(norm_mix_g=norm_mix_g, norm_ffn_g=norm_ffn_g, b_ada=b_ada, b_forget=b_forget, sinks=sinks,
                   rel_bias=rel_bias, final_norm_g=final_norm_g)
    small_m = dict(norm_mix_g=m_norm_mix_g, norm_ffn_g=m_norm_ffn_g, b_ada=m_b_ada, b_forget=m_b_forget,
                   sinks=m_sinks, rel_bias=m_rel_bias, final_norm_g=m_final_norm_g)
    small_v = dict(norm_mix_g=v_norm_mix_g, norm_ffn_g=v_norm_ffn_g, b_ada=v_b_ada, b_forget=v_b_forget,
                   sinks=v_sinks, rel_bias=v_rel_bias, final_norm_g=v_final_norm_g)
    small_g = dict(
        norm_mix_g=jnp.stack([grads[l]["norm_mix_g"] for l in range(DEPTH)]),
        norm_ffn_g=jnp.stack([grads[l]["norm_ffn_g"] for l in range(DEPTH)]),
        b_ada=jnp.stack([grads[l]["d_mod"] for l in range(DEPTH)]),
        b_forget=jnp.stack([grads[l]["b_forget"] for l in range(DEPTH)]),
        sinks=jnp.stack([grads[l]["sinks"] for l in range(DEPTH)]),
        rel_bias=jnp.stack([grads[l]["rel_bias"] for l in range(DEPTH)]),
        final_norm_g=d_final[0])
    shapes = [small_w[n].shape for n in small_names]
    g_all = _all_gather8(_small_pack([small_g[n] for n in small_names]), "gather_small_grads")
    res = _adamw(_small_pack([small_w[n] for n in small_names])[None], _small_pack([small_m[n] for n in small_names])[None],
                 _small_pack([small_v[n] for n in small_names])[None], g_all, "adamw_small")
    small_out = {n: [] for n in small_names}
    for r in res:
        for n, a in zip(small_names, _small_unpack(r[0], shapes)):
            small_out[n].append(a)
    off_b = sum(int(np.prod(s)) for s in shapes[:2])
    n_mod = DEPTH * 6 * D_MODEL
    dmod_all = g_all.reshape(8, -1)[:, off_b:off_b + n_mod].reshape(8, DEPTH, 6 * D_MODEL)
    dmod_sh = lax.dynamic_slice_in_dim(dmod_all, chip * n_ada, n_ada, axis=2).transpose(1, 0, 2)
    g_ada = _ada_bwd(c_all.T, dmod_sh, "ada_bwd")
    ada_out = _adamw(w_ada, m_w_ada, v_w_ada, flat2(g_ada)[None], "adamw_w_ada")

    order = ("norm_mix_g", "norm_ffn_g", "w_ada", "b_ada", "w_in", "b_forget", "sinks", "rel_bias", "w_branch",
             "w_out", "w_ffn_in", "w_ffn_out", "final_norm_g")

    def pick(n, k):
        if n == "w_ada":
            return ada_out[k]
        if n in big_out:
            return big_out[n][k]
        return small_out[n][k]

    outs = [loss, grad_x]
    for k in range(4):
        outs += [pick(n, k) for n in order]
    return tuple(outs)
```

```python
import functools

import numpy as np
import jax
import jax.numpy as jnp
from jax import lax
from jax.experimental import pallas as pl
from jax.experimental.pallas import tpu as pltpu

F32 = jnp.float32
BF16 = jnp.bfloat16
SDS = jax.ShapeDtypeStruct

D_MODEL = 1024
DEPTH = 2
CHUNK = 64
HEAD_DIM = 64
EPS = 1e-6
NEG_INF = -1e30
N_HEADS = 8
A_KV_HEADS = 2
A_PREV = 2
C_PREV = 8
REL_CLIP = 128
N_REL = 2 * REL_CLIP + 1
N_REL_PAD = 384
BRANCH_W = 512
FFN_H = 2816
FOX_BQ = 256
FOX_BK = 512
BAND_UNROLL_FWD = 4
BAND_UNROLL_BWD = 2
QKV_COLS = 3840
GF_COLS = 3200
N_IN_COLS = 6920
LANE = 128
VMEM_LIMIT = 48 * 1024 * 1024

ADAM_LR = 0.001
ADAM_B1 = 0.9
ADAM_B2 = 0.999
ADAM_EPS = 1e-08
ADAM_WD = 0.01
ADAM_STEP = 10

MESH = pl.DeviceIdType.MESH
ANY = pl.BlockSpec(memory_space=pl.ANY)
VMEM_SPEC = pl.BlockSpec(memory_space=pltpu.VMEM)


def _cparams(sem=None):
    return pltpu.CompilerParams(dimension_semantics=sem, vmem_limit_bytes=VMEM_LIMIT)


def _blk(n, cap):
    if n <= cap:
        return n
    best = None
    for m in range(LANE, cap + 1, LANE):
        if n % m == 0:
            best = m
    assert best is not None, (n, cap)
    return best


def _sigmoid(x):
    return 1.0 / (1.0 + jnp.exp(-x))


def _mm(a, b, *, mode, out_dtype, name, groups=1, cap_m=2048, cap_n=1024, cap_k=1408, col_quarters=False,
        comm=None):
    G = groups
    assert not col_quarters or mode == "tn"
    if mode == "nn":
        M, K, N = a.shape[0], a.shape[1] // G, b.shape[1]
        assert b.shape[0] == G * K
    elif mode == "nt":
        M, K, N = a.shape[0], a.shape[1] // G, b.shape[0] // G
        assert b.shape[1] == K
    else:
        K, M, N = a.shape[0], a.shape[1] // G, b.shape[1] // G
        assert b.shape[0] == K
    bm, bn, bk = _blk(M, cap_m), _blk(N // 4 if col_quarters else N, cap_n), _blk(K, cap_k)
    nm, nn, nk = M // bm, N // bn, K // bk
    if mode == "nn":
        a_spec = pl.BlockSpec((bm, bk), lambda g, i, j, k: (i, g * nk + k))
        b_spec = pl.BlockSpec((bk, bn), lambda g, i, j, k: (g * nk + k, j))
        o_spec = pl.BlockSpec((bm, bn), lambda g, i, j, k: (i, g * nn + j))
        dims = (((1,), (0,)), ((), ()))
        out_shape = (M, G * N)
    elif mode == "nt":
        a_spec = pl.BlockSpec((bm, bk), lambda g, i, j, k: (i, g * nk + k))
        b_spec = pl.BlockSpec((bn, bk), lambda g, i, j, k: (g * nn + j, k))
        o_spec = pl.BlockSpec((bm, bn), lambda g, i, j, k: (i, g * nn + j))
        dims = (((1,), (1,)), ((), ()))
        out_shape = (M, G * N)
    else:
        a_spec = pl.BlockSpec((bk, bm), lambda g, i, j, k: (k, g * nm + i))
        b_spec = pl.BlockSpec((bk, bn), lambda g, i, j, k: (k, g * nn + j))
        dims = (((0,), (0,)), ((), ()))
        if col_quarters:
            nq = nn // 4
            o_spec = pl.BlockSpec((1, bm, bn), lambda g, i, j, k: (j // nq, g * nm + i, j % nq))
            out_shape = (4, G * M, N // 4)
        else:
            o_spec = pl.BlockSpec((bm, bn), lambda g, i, j, k: (g * nm + i, j))
            out_shape = (G * M, N)

    def product(a_ref, b_ref):
        return lax.dot_general(a_ref[...].astype(BF16), b_ref[...].astype(BF16), dims, preferred_element_type=F32)

    def body_one(a_ref, b_ref, o_ref):
        o_ref[...] = product(a_ref, b_ref).astype(o_ref.dtype).reshape(o_ref.shape)

    def body_acc(a_ref, b_ref, o_ref, acc_ref):
        k = pl.program_id(3)

        @pl.when(k == 0)
        def _():
            acc_ref[...] = jnp.zeros_like(acc_ref)

        acc_ref[...] += product(a_ref, b_ref)

        @pl.when(k == nk - 1)
        def _():
            o_ref[...] = acc_ref[...].astype(o_ref.dtype).reshape(o_ref.shape)

    res, got = _call_hosting(
        body_one if nk == 1 else body_acc, comm=comm, grid=(G, nm, nn, nk), in_specs=[a_spec, b_spec],
        out_specs=[o_spec], out_shape=[SDS(out_shape, out_dtype)],
        scratch_shapes=[] if nk == 1 else [pltpu.VMEM((bm, bn), F32)], name=name, args=(a, b),
        semantics=("parallel", "parallel", "parallel", "arbitrary"))
    return res[0] if comm is None else (res[0], got)


def _rows(tm, n, col=0):
    return pl.BlockSpec((tm, n), lambda i: (i, col))


def _vec(n):
    return pl.BlockSpec((1, n), lambda i: (0, 0))


def _tm(S):
    return min(S, 256)


def _norm_mod_fwd(x, g, sc, sh, name):
    S, Dm = x.shape
    tm = _tm(S)

    def body(x_ref, g_ref, sc_ref, sh_ref, h_ref):
        xv = x_ref[...]
        r = lax.rsqrt(jnp.mean(xv * xv, axis=-1, keepdims=True) + EPS)
        h_ref[...] = ((xv * r) * g_ref[...] * (1.0 + sc_ref[...]) + sh_ref[...]).astype(h_ref.dtype)

    return pl.pallas_call(
        body, grid=(S // tm,), in_specs=[_rows(tm, Dm), _vec(Dm), _vec(Dm), _vec(Dm)],
        out_specs=_rows(tm, Dm), out_shape=SDS((S, Dm), BF16),
        compiler_params=_cparams(("parallel",)), name=name)(x, g, sc, sh)


def _norm_mod_bwd(x, dh_list, dres, g, sc, name):
    S, Dm = x.shape
    tm = _tm(S)
    nh = len(dh_list)

    def body(*refs):
        x_ref = refs[0]
        dh_refs = refs[1:1 + nh]
        dres_ref, g_ref, sc_ref, dx_ref, dsc_ref, dsh_ref, dg_ref = refs[1 + nh:]
        i = pl.program_id(0)

        @pl.when(i == 0)
        def _():
            dsc_ref[...] = jnp.zeros_like(dsc_ref)
            dsh_ref[...] = jnp.zeros_like(dsh_ref)
            dg_ref[...] = jnp.zeros_like(dg_ref)

        xv = x_ref[...]
        dh = dh_refs[0][...]
        for r_ in dh_refs[1:]:
            dh = dh + r_[...]
        gv = g_ref[...]
        r = lax.rsqrt(jnp.mean(xv * xv, axis=-1, keepdims=True) + EPS)
        xn = xv * r
        xg = xn * gv
        dsh_ref[...] += jnp.sum(dh, axis=0, keepdims=True)
        dsc_ref[...] += jnp.sum(dh * xg, axis=0, keepdims=True)
        dxg = dh * (1.0 + sc_ref[...])
        dg_ref[...] += jnp.sum(dxg * xn, axis=0, keepdims=True)
        dxn = dxg * gv
        dx_ref[...] = dres_ref[...] + r * (dxn - xn * jnp.mean(dxn * xn, axis=-1, keepdims=True))

    return pl.pallas_call(
        body, grid=(S // tm,),
        in_specs=[_rows(tm, Dm)] * (2 + nh) + [_vec(Dm), _vec(Dm)],
        out_specs=[_rows(tm, Dm), _vec(Dm), _vec(Dm), _vec(Dm)],
        out_shape=[SDS((S, Dm), F32), SDS((1, Dm), F32), SDS((1, Dm), F32), SDS((1, Dm), F32)],
        compiler_params=_cparams(("arbitrary",)), name=name)(x, *dh_list, dres, g, sc)


def _resid_fwd(x, val, g, name):
    S, Dm = x.shape
    tm = _tm(S)

    def body(x_ref, v_ref, g_ref, o_ref):
        o_ref[...] = x_ref[...] + g_ref[...] * v_ref[...]

    return pl.pallas_call(
        body, grid=(S // tm,), in_specs=[_rows(tm, Dm), _rows(tm, Dm), _vec(Dm)],
        out_specs=_rows(tm, Dm), out_shape=SDS((S, Dm), F32),
        compiler_params=_cparams(("parallel",)), name=name)(x, val, g)


def _resid_bwd(dx, val, g, name):
    S, Dm = dx.shape
    tm = _tm(S)

    def body(dx_ref, v_ref, g_ref, dg_ref, dv_ref):
        @pl.when(pl.program_id(0) == 0)
        def _():
            dg_ref[...] = jnp.zeros_like(dg_ref)

        dxv = dx_ref[...]
        dg_ref[...] += jnp.sum(dxv * v_ref[...], axis=0, keepdims=True)
        dv_ref[...] = (dxv * g_ref[...]).astype(dv_ref.dtype)

    return pl.pallas_call(
        body, grid=(S // tm,), in_specs=[_rows(tm, Dm), _rows(tm, Dm), _vec(Dm)],
        out_specs=[_vec(Dm), _rows(tm, Dm)], out_shape=[SDS((1, Dm), F32), SDS((S, Dm), BF16)],
        compiler_params=_cparams(("arbitrary",)), name=name)(dx, val, g)


def _merge_fwd(y, gf, name):
    S = y.shape[0]
    tm = _tm(S)
    W = 3 * D_MODEL

    def body(y_ref, g_ref, o_ref):
        acc = None
        for k in range(3):
            sl = slice(k * D_MODEL, (k + 1) * D_MODEL)
            t = _sigmoid(g_ref[:, sl]) * y_ref[:, sl]
            acc = t if acc is None else acc + t
        o_ref[...] = acc.astype(o_ref.dtype)

    return pl.pallas_call(
        body, grid=(S // tm,), in_specs=[_rows(tm, W), _rows(tm, W)],
        out_specs=_rows(tm, D_MODEL), out_shape=SDS((S, D_MODEL), BF16),
        compiler_params=_cparams(("parallel",)), name=name)(y, gf)


def _merge_bwd(dm, y, gf, name):
    S = y.shape[0]
    tm = _tm(S)
    W = 3 * D_MODEL

    def body(dm_ref, y_ref, g_ref, dy_ref, dg_ref):
        dmv = dm_ref[...]
        for k in range(3):
            sl = slice(k * D_MODEL, (k + 1) * D_MODEL)
            sg = _sigmoid(g_ref[:, sl])
            dy_ref[:, sl] = (dmv * sg).astype(dy_ref.dtype)
            dg_ref[:, sl] = (dmv * y_ref[:, sl] * (sg * (1.0 - sg))).astype(dg_ref.dtype)

    return pl.pallas_call(
        body, grid=(S // tm,), in_specs=[_rows(tm, D_MODEL), _rows(tm, W), _rows(tm, W)],
        out_specs=[_rows(tm, W), _rows(tm, W)], out_shape=[SDS((S, W), BF16), SDS((S, W), BF16)],
        compiler_params=_cparams(("parallel",)), name=name)(dm, y, gf)


def _swiglu_fwd(u, name):
    S = u.shape[0]
    tm = _tm(S)

    def body(g_ref, u_ref, a_ref):
        gv = g_ref[...]
        a_ref[...] = (gv * _sigmoid(gv) * u_ref[...]).astype(a_ref.dtype)

    return pl.pallas_call(
        body, grid=(S // tm,), in_specs=[_rows(tm, FFN_H, 0), _rows(tm, FFN_H, 1)],
        out_specs=_rows(tm, FFN_H), out_shape=SDS((S, FFN_H), BF16),
        compiler_params=_cparams(("parallel",)), name=name)(u, u)


def _swiglu_bwd(da, u, name):
    S = u.shape[0]
    tm = _tm(S)

    def body(da_ref, g_ref, u_ref, du_ref):
        dav = da_ref[...]
        gv = g_ref[...]
        sg = _sigmoid(gv)
        du_ref[:, 0:FFN_H] = (dav * u_ref[...] * (sg * (1.0 + gv * (1.0 - sg)))).astype(du_ref.dtype)
        du_ref[:, FFN_H:2 * FFN_H] = (dav * (gv * sg)).astype(du_ref.dtype)

    return pl.pallas_call(
        body, grid=(S // tm,), in_specs=[_rows(tm, FFN_H), _rows(tm, FFN_H, 0), _rows(tm, FFN_H, 1)],
        out_specs=_rows(tm, 2 * FFN_H), out_shape=SDS((S, 2 * FFN_H), BF16),
        compiler_params=_cparams(("parallel",)), name=name)(da, u, u)


def _final_loss(x, g, target, name):
    S, Dm = x.shape
    tm = _tm(S)

    def body(x_ref, g_ref, t_ref, loss_ref, dx_ref, dg_ref):
        @pl.when(pl.program_id(0) == 0)
        def _():
            loss_ref[...] = jnp.zeros_like(loss_ref)
            dg_ref[...] = jnp.zeros_like(dg_ref)

        xv = x_ref[...]
        gv = g_ref[...]
        r = lax.rsqrt(jnp.mean(xv * xv, axis=-1, keepdims=True) + EPS)
        xn = xv * r
        err = xn * gv - t_ref[...]
        row = jnp.mean(err * err, axis=-1, keepdims=True)
        loss_ref[...] += 0.5 * jnp.sum(row, axis=0, keepdims=True)
        dy = err * (1.0 / Dm)
        dg_ref[...] += jnp.sum(dy * xn, axis=0, keepdims=True)
        dxn = dy * gv
        dx_ref[...] = r * (dxn - xn * jnp.mean(dxn * xn, axis=-1, keepdims=True))

    return pl.pallas_call(
        body, grid=(S // tm,), in_specs=[_rows(tm, Dm), _vec(Dm), _rows(tm, Dm)],
        out_specs=[pl.BlockSpec((1, 1), lambda i: (0, 0)), _rows(tm, Dm), _vec(Dm)],
        out_shape=[SDS((1, 1), F32), SDS((S, Dm), F32), SDS((1, Dm), F32)],
        compiler_params=_cparams(("arbitrary",)), name=name)(x, g, target)


def _band_softmax(qg, kg, bias, sink, valid):
    s = lax.dot_general(qg, kg, (((1,), (1,)), ((), ())), preferred_element_type=F32)
    s = jnp.where(valid, s + bias, NEG_INF)
    m = jnp.maximum(jnp.max(s, axis=-1, keepdims=True), sink)
    e = jnp.exp(s - m)
    es = jnp.exp(sink - m)
    l = jnp.sum(e, axis=-1, keepdims=True) + es
    return e / l, es / l


def _band_attn_fwd(q, k, v, bias, sink, *, G, P, kvoff, name):
    S = q.shape[0]
    ng = q.shape[1] // (G * HEAD_DIM)
    band = (P + 1) * CHUNK
    pad = P * CHUNK
    nc = S // CHUNK

    def body(q_ref, k_ref, v_ref, b_ref, s_ref, o_ref, kp, vp):
        kp[0:pad, :] = jnp.zeros((pad, LANE), BF16)
        vp[0:pad, :] = jnp.zeros((pad, LANE), BF16)
        kp[pad:pad + S, :] = k_ref[...]
        vp[pad:pad + S, :] = v_ref[...]
        col = lax.broadcasted_iota(jnp.int32, (CHUNK, band), 1)

        def step(n, carry):
            r = pl.multiple_of(n * CHUNK, CHUNK)
            qn = q_ref[pl.ds(r, CHUNK), :]
            kb = kp[pl.ds(r, band), :]
            vb = vp[pl.ds(r, band), :]
            valid = col >= (P - n) * CHUNK
            for g in range(G):
                ko = kvoff(g) * HEAD_DIM
                qg = qn[:, g * HEAD_DIM:(g + 1) * HEAD_DIM] * 0.125
                p, _ = _band_softmax(qg, kb[:, ko:ko + HEAD_DIM], b_ref[g], s_ref[g, 0:1, 0:1], valid)
                og = jnp.dot(p.astype(BF16), vb[:, ko:ko + HEAD_DIM], preferred_element_type=F32)
                o_ref[pl.ds(r, CHUNK), g * HEAD_DIM:(g + 1) * HEAD_DIM] = og.astype(o_ref.dtype)
            return carry

        lax.fori_loop(0, nc, step, 0, unroll=min(BAND_UNROLL_FWD, nc))

    GW = G * HEAD_DIM
    return pl.pallas_call(
        body, grid=(ng,),
        in_specs=[pl.BlockSpec((S, GW), lambda i: (0, i)), pl.BlockSpec((S, LANE), lambda i: (0, i)),
                  pl.BlockSpec((S, LANE), lambda i: (0, i)),
                  pl.BlockSpec((G, CHUNK, band), lambda i: (i, 0, 0)),
                  pl.BlockSpec((G, 8, LANE), lambda i: (i, 0, 0))],
        out_specs=pl.BlockSpec((S, GW), lambda i: (0, i)),
        out_shape=SDS((S, ng * GW), BF16),
        scratch_shapes=[pltpu.VMEM((S + pad, LANE), BF16), pltpu.VMEM((S + pad, LANE), BF16)],
        compiler_params=_cparams(("parallel",)), name=name)(q, k, v, bias, sink)


def _band_attn_bwd(q, k, v, bias, sink, do, *, G, P, kvoff, name):
    S = q.shape[0]
    ng = q.shape[1] // (G * HEAD_DIM)
    band = (P + 1) * CHUNK
    pad = P * CHUNK
    nc = S // CHUNK
    TN = (((0,), (0,)), ((), ()))

    def body(q_ref, k_ref, v_ref, b_ref, s_ref, do_ref, dq_ref, dk_ref, dv_ref, db_ref, dsk_ref,
             kp, vp, dkp, dvp):
        kp[0:pad, :] = jnp.zeros((pad, LANE), BF16)
        vp[0:pad, :] = jnp.zeros((pad, LANE), BF16)
        kp[pad:pad + S, :] = k_ref[...]
        vp[pad:pad + S, :] = v_ref[...]
        dkp[...] = jnp.zeros_like(dkp)
        dvp[...] = jnp.zeros_like(dvp)
        db_ref[...] = jnp.zeros_like(db_ref)
        col = lax.broadcasted_iota(jnp.int32, (CHUNK, band), 1)

        def step(n, dsink):
            r = pl.multiple_of(n * CHUNK, CHUNK)
            qn = q_ref[pl.ds(r, CHUNK), :]
            don = do_ref[pl.ds(r, CHUNK), :]
            kb = kp[pl.ds(r, band), :]
            vb = vp[pl.ds(r, band), :]
            valid = col >= (P - n) * CHUNK
            new = []
            for g in range(G):
                ko = kvoff(g) * HEAD_DIM
                lanes = slice(g * HEAD_DIM, (g + 1) * HEAD_DIM)
                qg = qn[:, lanes] * 0.125
                kg = kb[:, ko:ko + HEAD_DIM]
                dog = don[:, lanes]
                p, ps = _band_softmax(qg, kg, b_ref[g], s_ref[g, 0:1, 0:1], valid)
                dp = lax.dot_general(dog, vb[:, ko:ko + HEAD_DIM], (((1,), (1,)), ((), ())),
                                     preferred_element_type=F32)
                delta = jnp.sum(p * dp, axis=-1, keepdims=True)
                ds = p * (dp - delta)
                new.append(dsink[g] - jnp.sum(ps * delta, axis=0, keepdims=True))
                db_ref[g] += ds
                dsb = ds.astype(BF16)
                dq = jnp.dot(dsb, kg, preferred_element_type=F32) * 0.125
                dq_ref[pl.ds(r, CHUNK), lanes] = dq.astype(dq_ref.dtype)
                dkp[pl.ds(r, band), ko:ko + HEAD_DIM] += lax.dot_general(
                    dsb, qg, TN, preferred_element_type=F32)
                dvp[pl.ds(r, band), ko:ko + HEAD_DIM] += lax.dot_general(
                    p.astype(BF16), dog, TN, preferred_element_type=F32)
            return tuple(new)

        dsink = lax.fori_loop(0, nc, step, tuple(jnp.zeros((1, 1), F32) for _ in range(G)),
                              unroll=min(BAND_UNROLL_BWD, nc))
        for g in range(G):
            dsk_ref[g] = jnp.broadcast_to(dsink[g], (8, LANE))
        dk_ref[...] = dkp[pad:pad + S, :].astype(dk_ref.dtype)
        dv_ref[...] = dvp[pad:pad + S, :].astype(dv_ref.dtype)

    GW = G * HEAD_DIM
    qs = pl.BlockSpec((S, GW), lambda i: (0, i))
    ks = pl.BlockSpec((S, LANE), lambda i: (0, i))
    bs = pl.BlockSpec((G, CHUNK, band), lambda i: (i, 0, 0))
    ss = pl.BlockSpec((G, 8, LANE), lambda i: (i, 0, 0))
    return pl.pallas_call(
        body, grid=(ng,), in_specs=[qs, ks, ks, bs, ss, qs],
        out_specs=[qs, ks, ks, bs, ss],
        out_shape=[SDS((S, ng * GW), BF16), SDS((S, ng * LANE), BF16), SDS((S, ng * LANE), BF16),
                   SDS((ng * G, CHUNK, band), F32), SDS((ng * G, 8, LANE), F32)],
        scratch_shapes=[pltpu.VMEM((S + pad, LANE), BF16), pltpu.VMEM((S + pad, LANE), BF16),
                        pltpu.VMEM((S + pad, LANE), F32), pltpu.VMEM((S + pad, LANE), F32)],
        compiler_params=_cparams(("parallel",)), name=name)(q, k, v, bias, sink, do)


PAIR = 2 * CHUNK


def _bandT_softmax(kg, qTg, bias, sink, valid):
    s = jnp.dot(kg, qTg, preferred_element_type=F32)
    s = jnp.where(valid, s + bias, NEG_INF)
    m = jnp.maximum(jnp.max(s, axis=0, keepdims=True), sink)
    e = jnp.exp(s - m)
    es = jnp.exp(sink - m)
    inv = 1.0 / (jnp.sum(e, axis=0, keepdims=True) + es)
    return e * inv, es * inv


def _pad_copy_rows(dst, src, pad, S):
    dst[:, 0:pad, :] = jnp.zeros((dst.shape[0], pad, dst.shape[2]), dst.dtype)
    dst[:, pad:pad + S, :] = src[...]


def _pad_copy_lanes(dst, src, pad, S):
    dst[:, 0:pad] = jnp.zeros((dst.shape[0], pad), dst.dtype)
    dst[:, pad:pad + S] = src[...]


def _bandT_fwd(qT, k_h, vT, bias, sink, *, GQ, GK, P, kvoff, name, comm=None):
    S = qT.shape[1]
    ng = qT.shape[0] // (GQ * HEAD_DIM)
    BU = (P + 2) * CHUNK
    pad = P * CHUNK
    npair = S // PAIR

    def body(qT_ref, k_ref, vT_ref, b_ref, s_ref, oT_ref, kp, vTp):
        _pad_copy_rows(kp, k_ref, pad, S)
        _pad_copy_lanes(vTp, vT_ref, pad, S)
        rowi = lax.broadcasted_iota(jnp.int32, (BU, PAIR), 0)

        def step(n2, carry):
            r = pl.multiple_of(n2 * PAIR, PAIR)
            valid = rowi >= (P - 2 * n2) * CHUNK
            for g in range(GQ):
                kv = kvoff(g)
                hs = slice(g * HEAD_DIM, (g + 1) * HEAD_DIM)
                kvs = slice(kv * HEAD_DIM, (kv + 1) * HEAD_DIM)
                qTg = qT_ref[hs, pl.ds(r, PAIR)] * 0.125
                p, _ = _bandT_softmax(kp[kv, pl.ds(r, BU), :], qTg, b_ref[g], s_ref[g, 0:1, :], valid)
                oTg = jnp.dot(vTp[kvs, pl.ds(r, BU)], p.astype(BF16), preferred_element_type=F32)
                oT_ref[hs, pl.ds(r, PAIR)] = oTg.astype(oT_ref.dtype)
            return carry

        lax.fori_loop(0, npair, step, 0, unroll=min(2, npair))

    res, got = _call_hosting(
        body, comm=comm, grid=(ng,),
        in_specs=[pl.BlockSpec((GQ * HEAD_DIM, S), lambda i: (i, 0)),
                  pl.BlockSpec((GK, S, HEAD_DIM), lambda i: (i, 0, 0)),
                  pl.BlockSpec((GK * HEAD_DIM, S), lambda i: (i, 0)),
                  pl.BlockSpec((GQ, BU, PAIR), lambda i: (i, 0, 0)),
                  pl.BlockSpec((GQ, 8, LANE), lambda i: (i, 0, 0))],
        out_specs=[pl.BlockSpec((GQ * HEAD_DIM, S), lambda i: (i, 0))],
        out_shape=[SDS((ng * GQ * HEAD_DIM, S), BF16)],
        scratch_shapes=[pltpu.VMEM((GK, S + pad, HEAD_DIM), BF16), pltpu.VMEM((GK * HEAD_DIM, S + pad), BF16)],
        name=name, args=(qT, k_h, vT, bias, sink))
    return res[0], got


def _bandT_bwd(qT, q_h, k_h, kT, v_h, doT, do_h, bias, sink, *, GQ, GK, P, kvoff, name, comm=None):
    S = qT.shape[1]
    ng = qT.shape[0] // (GQ * HEAD_DIM)
    BU = (P + 2) * CHUNK
    pad = P * CHUNK
    npair = S // PAIR

    def body(qT_ref, q_ref, k_ref, kT_ref, v_ref, doT_ref, do_ref, b_ref, s_ref,
             dqT_ref, dk_ref, dv_ref, db_ref, dsk_ref, kp, kTp, vp, dkp, dvp):
        _pad_copy_rows(kp, k_ref, pad, S)
        _pad_copy_rows(vp, v_ref, pad, S)
        _pad_copy_lanes(kTp, kT_ref, pad, S)
        dkp[...] = jnp.zeros_like(dkp)
        dvp[...] = jnp.zeros_like(dvp)
        db_ref[...] = jnp.zeros_like(db_ref)
        rowi = lax.broadcasted_iota(jnp.int32, (BU, PAIR), 0)

        def step(n2, dsink):
            r = pl.multiple_of(n2 * PAIR, PAIR)
            valid = rowi >= (P - 2 * n2) * CHUNK
            new = []
            for g in range(GQ):
                kv = kvoff(g)
                hs = slice(g * HEAD_DIM, (g + 1) * HEAD_DIM)
                kvs = slice(kv * HEAD_DIM, (kv + 1) * HEAD_DIM)
                qTg = qT_ref[hs, pl.ds(r, PAIR)] * 0.125
                p, ps = _bandT_softmax(kp[kv, pl.ds(r, BU), :], qTg, b_ref[g], s_ref[g, 0:1, :], valid)
                dp = jnp.dot(vp[kv, pl.ds(r, BU), :], doT_ref[hs, pl.ds(r, PAIR)], preferred_element_type=F32)
                delta = jnp.sum(p * dp, axis=0, keepdims=True)
                ds = p * (dp - delta)
                new.append(dsink[g] - ps * delta)
                db_ref[g] += ds
                dsb = ds.astype(BF16)
                dq = jnp.dot(kTp[kvs, pl.ds(r, BU)], dsb, preferred_element_type=F32) * 0.125
                dqT_ref[hs, pl.ds(r, PAIR)] = dq.astype(dqT_ref.dtype)
                dkp[kv, pl.ds(r, BU), :] += jnp.dot(dsb, q_ref[g, pl.ds(r, PAIR), :] * 0.125,
                                                    preferred_element_type=F32)
                dvp[kv, pl.ds(r, BU), :] += jnp.dot(p.astype(BF16), do_ref[g, pl.ds(r, PAIR), :],
                                                    preferred_element_type=F32)
            return tuple(new)

        dsink = lax.fori_loop(0, npair, step, tuple(jnp.zeros((1, PAIR), F32) for _ in range(GQ)))
        for g in range(GQ):
            dsk_ref[g] = jnp.broadcast_to(jnp.sum(dsink[g], axis=1, keepdims=True), (8, LANE))
        dk_ref[...] = dkp[:, pad:pad + S, :].astype(dk_ref.dtype)
        dv_ref[...] = dvp[:, pad:pad + S, :].astype(dv_ref.dtype)

    qTs = pl.BlockSpec((GQ * HEAD_DIM, S), lambda i: (i, 0))
    qhs = pl.BlockSpec((GQ, S, HEAD_DIM), lambda i: (i, 0, 0))
    khs = pl.BlockSpec((GK, S, HEAD_DIM), lambda i: (i, 0, 0))
    kTs = pl.BlockSpec((GK * HEAD_DIM, S), lambda i: (i, 0))
    bs = pl.BlockSpec((GQ, BU, PAIR), lambda i: (i, 0, 0))
    ss = pl.BlockSpec((GQ, 8, LANE), lambda i: (i, 0, 0))
    nkv = ng * GK
    return _call_hosting(
        body, comm=comm, grid=(ng,), in_specs=[qTs, qhs, khs, kTs, khs, qTs, qhs, bs, ss],
        out_specs=[qTs, khs, khs, bs, ss],
        out_shape=[SDS((ng * GQ * HEAD_DIM, S), BF16), SDS((nkv, S, HEAD_DIM), BF16), SDS((nkv, S, HEAD_DIM), BF16),
                   SDS((ng * GQ, BU, PAIR), F32), SDS((ng * GQ, 8, LANE), F32)],
        scratch_shapes=[pltpu.VMEM((GK, S + pad, HEAD_DIM), BF16), pltpu.VMEM((GK * HEAD_DIM, S + pad), BF16),
                        pltpu.VMEM((GK, S + pad, HEAD_DIM), BF16),
                        pltpu.VMEM((GK, S + pad, HEAD_DIM), F32), pltpu.VMEM((GK, S + pad, HEAD_DIM), F32)],
        name=name, args=(qT, q_h, k_h, kT, v_h, doT, do_h, bias, sink))


def _pair_table(tab):
    t = jnp.transpose(tab, (0, 2, 1))
    lo = jnp.pad(t, ((0, 0), (0, CHUNK), (0, 0)), constant_values=NEG_INF)
    hi = jnp.pad(t, ((0, 0), (CHUNK, 0), (0, 0)), constant_values=NEG_INF)
    return jnp.concatenate([lo, hi], axis=2)


def _unpair_table(d):
    band = d.shape[1] - CHUNK
    return jnp.transpose(d[:, 0:band, 0:CHUNK] + d[:, CHUNK:CHUNK + band, CHUNK:PAIR], (0, 2, 1))


def _heads(a, n):
    return jnp.transpose(a.reshape(a.shape[0], n, HEAD_DIM), (1, 0, 2))


def _unheads(a):
    return jnp.transpose(a, (1, 0, 2)).reshape(a.shape[1], a.shape[0] * HEAD_DIM)


def _fox_logits(qg, kj, cq, ck, r, c, row, col):
    s = lax.dot_general(qg, kj, (((1,), (1,)), ((), ())), preferred_element_type=F32)
    s = s + cq - ck
    return jnp.where(c + col <= r + row, s, NEG_INF)


def _fox_fwd(q, k, v, cc, cr, name):
    S = q.shape[0]
    npair = q.shape[1] // LANE
    BQ, BK = min(FOX_BQ, S), min(FOX_BK, S)
    nq = S // BQ
    heads = [slice(g * HEAD_DIM, (g + 1) * HEAD_DIM) for g in range(2)]

    def body(q_ref, k_ref, v_ref, cc_ref, cr_ref, o_ref, lse_ref):
        row = lax.broadcasted_iota(jnp.int32, (BQ, BK), 0)
        col = lax.broadcasted_iota(jnp.int32, (BQ, BK), 1)

        def qstep(i, carry):
            r = pl.multiple_of(i * BQ, BQ)
            qs = [q_ref[pl.ds(r, BQ), hl] * 0.125 for hl in heads]
            cqs = [cc_ref[g, pl.ds(r, BQ), :] for g in range(2)]

            def kstep(j, st):
                c = pl.multiple_of(j * BK, BK)
                new = []
                for g, hl in enumerate(heads):
                    m, l, acc = st[g]
                    s = _fox_logits(qs[g], k_ref[pl.ds(c, BK), hl], cqs[g], cr_ref[g, :, pl.ds(c, BK)],
                                    r, c, row, col)
                    mn = jnp.maximum(m, jnp.max(s, axis=-1, keepdims=True))
                    al = jnp.exp(m - mn)
                    e = jnp.exp(s - mn)
                    l = al * l + jnp.sum(e, axis=-1, keepdims=True)
                    acc = al * acc + jnp.dot(e.astype(BF16), v_ref[pl.ds(c, BK), hl],
                                             preferred_element_type=F32)
                    new.append((mn, l, acc))
                return tuple(new)

            init = (jnp.full((BQ, 1), NEG_INF, F32), jnp.zeros((BQ, 1), F32), jnp.zeros((BQ, HEAD_DIM), F32))
            st = lax.fori_loop(0, (r + BQ + BK - 1) // BK, kstep, (init, init))
            for g, hl in enumerate(heads):
                m, l, acc = st[g]
                o_ref[pl.ds(r, BQ), hl] = (acc / l).astype(o_ref.dtype)
                lse_ref[g, pl.ds(r, BQ), :] = m + jnp.log(l)
            return carry

        lax.fori_loop(0, nq, qstep, 0)

    blk = pl.BlockSpec((S, LANE), lambda i: (0, i))
    ccs = pl.BlockSpec((2, S, 1), lambda i: (i, 0, 0))
    crs = pl.BlockSpec((2, 1, S), lambda i: (i, 0, 0))
    return pl.pallas_call(
        body, grid=(npair,), in_specs=[blk, blk, blk, ccs, crs], out_specs=[blk, ccs],
        out_shape=[SDS((S, npair * LANE), BF16), SDS((2 * npair, S, 1), F32)],
        compiler_params=_cparams(("parallel",)), name=name)(q, k, v, cc, cr)


def _fox_bwd(q, k, v, cc, cr, o, do, lse, name):
    S = q.shape[0]
    npair = q.shape[1] // LANE
    BQ, BK = min(FOX_BQ, S), min(FOX_BK, S)
    nq = S // BQ
    heads = [slice(g * HEAD_DIM, (g + 1) * HEAD_DIM) for g in range(2)]
    TN = (((0,), (0,)), ((), ()))

    def body(q_ref, k_ref, v_ref, cc_ref, cr_ref, o_ref, do_ref, lse_ref,
             dq_ref, dk_ref, dv_ref, dcr_ref, dcc_ref, dka, dva):
        dka[...] = jnp.zeros_like(dka)
        dva[...] = jnp.zeros_like(dva)
        dcr_ref[...] = jnp.zeros_like(dcr_ref)
        row = lax.broadcasted_iota(jnp.int32, (BQ, BK), 0)
        col = lax.broadcasted_iota(jnp.int32, (BQ, BK), 1)

        def qstep(i, carry):
            r = pl.multiple_of(i * BQ, BQ)
            qs = [q_ref[pl.ds(r, BQ), hl] * 0.125 for hl in heads]
            dos = [do_ref[pl.ds(r, BQ), hl] for hl in heads]
            deltas = [jnp.sum(dos[g].astype(F32) * o_ref[pl.ds(r, BQ), hl].astype(F32), axis=-1, keepdims=True)
                      for g, hl in enumerate(heads)]
            cqs = [cc_ref[g, pl.ds(r, BQ), :] for g in range(2)]
            lses = [lse_ref[g, pl.ds(r, BQ), :] for g in range(2)]

            def kstep(j, st):
                c = pl.multiple_of(j * BK, BK)
                new = []
                for g, hl in enumerate(heads):
                    dq, rs = st[g]
                    kj = k_ref[pl.ds(c, BK), hl]
                    s = _fox_logits(qs[g], kj, cqs[g], cr_ref[g, :, pl.ds(c, BK)], r, c, row, col)
                    p = jnp.exp(s - lses[g])
                    dp = lax.dot_general(dos[g], v_ref[pl.ds(c, BK), hl], (((1,), (1,)), ((), ())),
                                         preferred_element_type=F32)
                    ds = p * (dp - deltas[g])
                    dcr_ref[g, :, pl.ds(c, BK)] -= jnp.sum(ds, axis=0, keepdims=True)
                    dsb = ds.astype(BF16)
                    dka[pl.ds(c, BK), hl] += lax.dot_general(dsb, qs[g], TN, preferred_element_type=F32)
                    dva[pl.ds(c, BK), hl] += lax.dot_general(p.astype(BF16), dos[g], TN,
                                                            preferred_element_type=F32)
                    new.append((dq + jnp.dot(dsb, kj, preferred_element_type=F32),
                                rs + jnp.sum(ds, axis=-1, keepdims=True)))
                return tuple(new)

            init = (jnp.zeros((BQ, HEAD_DIM), F32), jnp.zeros((BQ, 1), F32))
            st = lax.fori_loop(0, (r + BQ + BK - 1) // BK, kstep, (init, init))
            for g, hl in enumerate(heads):
                dq_ref[pl.ds(r, BQ), hl] = (st[g][0] * 0.125).astype(dq_ref.dtype)
                dcc_ref[g, pl.ds(r, BQ), :] = st[g][1]
            return carry

        lax.fori_loop(0, nq, qstep, 0)
        dk_ref[...] = dka[...].astype(dk_ref.dtype)
        dv_ref[...] = dva[...].astype(dv_ref.dtype)

    blk = pl.BlockSpec((S, LANE), lambda i: (0, i))
    ccs = pl.BlockSpec((2, S, 1), lambda i: (i, 0, 0))
    crs = pl.BlockSpec((2, 1, S), lambda i: (i, 0, 0))
    return pl.pallas_call(
        body, grid=(npair,), in_specs=[blk, blk, blk, ccs, crs, blk, blk, ccs],
        out_specs=[blk, blk, blk, crs, ccs],
        out_shape=[SDS((S, npair * LANE), BF16)] * 3 + [SDS((2 * npair, 1, S), F32), SDS((2 * npair, S, 1), F32)],
        scratch_shapes=[pltpu.VMEM((S, LANE), F32), pltpu.VMEM((S, LANE), F32)],
        compiler_params=_cparams(("parallel",)), name=name)(q, k, v, cc, cr, o, do, lse)


def _foxT_logits(kj, qTg, cq, ck, r, c, rowi, coli):
    s = jnp.dot(kj, qTg, preferred_element_type=F32)
    s = s + cq - ck
    return jnp.where(c + rowi <= r + coli, s, NEG_INF)


def _foxT_fwd(qT, k_h, vT, ck, cq, name, comm=None):
    S = qT.shape[1]
    npair = qT.shape[0] // LANE
    BQ, BK = min(FOX_BQ, S), min(FOX_BK, S)
    nq = S // BQ
    heads = [slice(g * HEAD_DIM, (g + 1) * HEAD_DIM) for g in range(2)]

    def body(qT_ref, k_ref, vT_ref, ck_ref, cq_ref, oT_ref, lse_ref):
        rowi = lax.broadcasted_iota(jnp.int32, (BK, BQ), 0)
        coli = lax.broadcasted_iota(jnp.int32, (BK, BQ), 1)

        def qstep(i, carry):
            r = pl.multiple_of(i * BQ, BQ)
            qs = [qT_ref[hs, pl.ds(r, BQ)] * 0.125 for hs in heads]
            cqs = [cq_ref[g, :, pl.ds(r, BQ)] for g in range(2)]

            def kstep(j, st):
                c = pl.multiple_of(j * BK, BK)
                new = []
                for g, hs in enumerate(heads):
                    m, l, acc = st[g]
                    s = _foxT_logits(k_ref[g, pl.ds(c, BK), :], qs[g], cqs[g], ck_ref[g, pl.ds(c, BK), :],
                                     r, c, rowi, coli)
                    mn = jnp.maximum(m, jnp.max(s, axis=0, keepdims=True))
                    al = jnp.exp(m - mn)
                    e = jnp.exp(s - mn)
                    l = al * l + jnp.sum(e, axis=0, keepdims=True)
                    acc = al * acc + jnp.dot(vT_ref[hs, pl.ds(c, BK)], e.astype(BF16), preferred_element_type=F32)
                    new.append((mn, l, acc))
                return tuple(new)

            init = (jnp.full((1, BQ), NEG_INF, F32), jnp.zeros((1, BQ), F32), jnp.zeros((HEAD_DIM, BQ), F32))
            st = lax.fori_loop(0, (r + BQ + BK - 1) // BK, kstep, (init, init))
            for g, hs in enumerate(heads):
                m, l, acc = st[g]
                oT_ref[hs, pl.ds(r, BQ)] = (acc * (1.0 / l)).astype(oT_ref.dtype)
                lse_ref[g, :, pl.ds(r, BQ)] = m + jnp.log(l)
            return carry

        lax.fori_loop(0, nq, qstep, 0)

    fT = pl.BlockSpec((LANE, S), lambda i: (i, 0))
    hm = pl.BlockSpec((2, S, HEAD_DIM), lambda i: (i, 0, 0))
    col = pl.BlockSpec((2, S, 1), lambda i: (i, 0, 0))
    rw = pl.BlockSpec((2, 1, S), lambda i: (i, 0, 0))
    return _call_hosting(
        body, comm=comm, grid=(npair,), in_specs=[fT, hm, fT, col, rw], out_specs=[fT, rw],
        out_shape=[SDS((npair * LANE, S), BF16), SDS((2 * npair, 1, S), F32)], scratch_shapes=[],
        name=name, args=(qT, k_h, vT, ck, cq))


def _foxT_bwd(qT, q_h, k_h, kT, v_h, ck, cq, oT, doT, do_h, lse, name, comm=None):
    S = qT.shape[1]
    npair = qT.shape[0] // LANE
    BQ, BK = min(FOX_BQ, S), min(FOX_BK, S)
    nq = S // BQ
    heads = [slice(g * HEAD_DIM, (g + 1) * HEAD_DIM) for g in range(2)]

    def body(qT_ref, q_ref, k_ref, kT_ref, v_ref, ck_ref, cq_ref, oT_ref, doT_ref, do_ref, lse_ref,
             dqT_ref, dk_ref, dv_ref, dck_ref, dcq_ref, dka, dva, qa_ref):
        qa_ref[:, :, 0:HEAD_DIM] = q_ref[...] * 0.125
        qa_ref[:, :, HEAD_DIM:LANE] = jnp.ones((2, S, LANE - HEAD_DIM), BF16)
        dka[...] = jnp.zeros_like(dka)
        dva[...] = jnp.zeros_like(dva)
        rowi = lax.broadcasted_iota(jnp.int32, (BK, BQ), 0)
        coli = lax.broadcasted_iota(jnp.int32, (BK, BQ), 1)

        def qstep(i, carry):
            r = pl.multiple_of(i * BQ, BQ)
            qs = [qT_ref[hs, pl.ds(r, BQ)] * 0.125 for hs in heads]
            dos = [doT_ref[hs, pl.ds(r, BQ)] for hs in heads]
            deltas = [jnp.sum(dos[g].astype(F32) * oT_ref[hs, pl.ds(r, BQ)].astype(F32), axis=0, keepdims=True)
                      for g, hs in enumerate(heads)]
            cqs = [cq_ref[g, :, pl.ds(r, BQ)] for g in range(2)]
            lses = [lse_ref[g, :, pl.ds(r, BQ)] for g in range(2)]

            def kstep(j, st):
                c = pl.multiple_of(j * BK, BK)
                new = []
                for g, hs in enumerate(heads):
                    dq, rs = st[g]
                    s = _foxT_logits(k_ref[g, pl.ds(c, BK), :], qs[g], cqs[g], ck_ref[g, pl.ds(c, BK), :],
                                     r, c, rowi, coli)
                    p = jnp.exp(s - lses[g])
                    dp = jnp.dot(v_ref[g, pl.ds(c, BK), :], dos[g], preferred_element_type=F32)
                    ds = p * (dp - deltas[g])
                    dsb = ds.astype(BF16)
                    dka[g, pl.ds(c, BK), :] += jnp.dot(dsb, qa_ref[g, pl.ds(r, BQ), :], preferred_element_type=F32)
                    dva[g, pl.ds(c, BK), :] += jnp.dot(p.astype(BF16), do_ref[g, pl.ds(r, BQ), :],
                                                      preferred_element_type=F32)
                    new.append((dq + jnp.dot(kT_ref[hs, pl.ds(c, BK)], dsb, preferred_element_type=F32),
                                rs + jnp.sum(dsb.astype(F32), axis=0, keepdims=True)))
                return tuple(new)

            init = (jnp.zeros((HEAD_DIM, BQ), F32), jnp.zeros((1, BQ), F32))
            st = lax.fori_loop(0, (r + BQ + BK - 1) // BK, kstep, (init, init))
            for g, hs in enumerate(heads):
                dqT_ref[hs, pl.ds(r, BQ)] = (st[g][0] * 0.125).astype(dqT_ref.dtype)
                dcq_ref[g, :, pl.ds(r, BQ)] = st[g][1]
            return carry

        lax.fori_loop(0, nq, qstep, 0)
        dk_ref[...] = dka[:, :, 0:HEAD_DIM].astype(dk_ref.dtype)
        dck_ref[...] = -dka[:, :, HEAD_DIM:HEAD_DIM + 1]
        dv_ref[...] = dva[...].astype(dv_ref.dtype)

    fT = pl.BlockSpec((LANE, S), lambda i: (i, 0))
    hm = pl.BlockSpec((2, S, HEAD_DIM), lambda i: (i, 0, 0))
    hma = pl.BlockSpec((2, S, LANE), lambda i: (i, 0, 0))
    col = pl.BlockSpec((2, S, 1), lambda i: (i, 0, 0))
    rw = pl.BlockSpec((2, 1, S), lambda i: (i, 0, 0))
    nh = 2 * npair
    return _call_hosting(
        body, comm=comm, grid=(npair,), in_specs=[fT, hm, hm, fT, hm, col, rw, fT, fT, hm, rw],
        out_specs=[fT, hm, hm, col, rw],
        out_shape=[SDS((npair * LANE, S), BF16), SDS((nh, S, HEAD_DIM), BF16), SDS((nh, S, HEAD_DIM), BF16),
                   SDS((nh, S, 1), F32), SDS((nh, 1, S), F32)],
        scratch_shapes=[pltpu.VMEM((2, S, LANE), F32), pltpu.VMEM((2, S, HEAD_DIM), F32),
                        pltpu.VMEM((2, S, LANE), BF16)],
        name=name, args=(qT, q_h, k_h, kT, v_h, ck, cq, oT, doT, do_h, lse))


def _split3(x):
    hi = x.astype(BF16)
    r1 = x - hi.astype(F32)
    mid = r1.astype(BF16)
    lo = (r1 - mid.astype(F32)).astype(BF16)
    return hi, mid, lo


def _tri_dot(tri, x):
    hi, mid, lo = _split3(x)
    return (jnp.dot(tri, hi, preferred_element_type=F32) + jnp.dot(tri, mid, preferred_element_type=F32)
            + jnp.dot(tri, lo, preferred_element_type=F32))


def _fox_cum(gf, bfo, name):
    S = gf.shape[0]
    nb = S // LANE
    fcol = (GF_COLS - LANE) // LANE

    def body(f_ref, b_ref, cum_ref):
        row = lax.broadcasted_iota(jnp.int32, (LANE, LANE), 0)
        col = lax.broadcasted_iota(jnp.int32, (LANE, LANE), 1)
        tri = jnp.where(row >= col, 1.0, 0.0).astype(BF16)
        carry = jnp.zeros((1, LANE), F32)
        for t in range(nb):
            xl = f_ref[t * LANE:(t + 1) * LANE, :] + b_ref[...]
            lf = jnp.minimum(xl, 0.0) - jnp.log(1.0 + jnp.exp(-jnp.abs(xl)))
            cblk = _tri_dot(tri, lf) + carry
            cum_ref[t * LANE:(t + 1) * LANE, :] = cblk
            carry = cblk[LANE - 1:LANE, :]

    return pl.pallas_call(
        body, grid=(1,), in_specs=[pl.BlockSpec((S, LANE), lambda i: (0, fcol)), _vec(LANE)],
        out_specs=pl.BlockSpec((S, LANE), lambda i: (0, 0)), out_shape=SDS((S, LANE), F32),
        compiler_params=_cparams(("arbitrary",)), name=name)(gf, bfo)


def _fox_cum_bwd(gf, bfo, dcum, name):
    S = gf.shape[0]
    nb = S // LANE
    fcol = (GF_COLS - LANE) // LANE

    def body(f_ref, b_ref, dc_ref, df_ref, db_ref):
        row = lax.broadcasted_iota(jnp.int32, (LANE, LANE), 0)
        col = lax.broadcasted_iota(jnp.int32, (LANE, LANE), 1)
        tri = jnp.where(row <= col, 1.0, 0.0).astype(BF16)
        carry = jnp.zeros((1, LANE), F32)
        tot = jnp.zeros((1, LANE), F32)
        for t in range(nb - 1, -1, -1):
            rows = slice(t * LANE, (t + 1) * LANE)
            dlf = _tri_dot(tri, dc_ref[rows, :]) + carry
            carry = dlf[0:1, :]
            xl = f_ref[rows, :] + b_ref[...]
            dfl = dlf * (1.0 / (1.0 + jnp.exp(xl)))
            df_ref[rows, :] = dfl.astype(df_ref.dtype)
            tot = tot + jnp.sum(dfl, axis=0, keepdims=True)
        db_ref[...] = tot

    return pl.pallas_call(
        body, grid=(1,),
        in_specs=[pl.BlockSpec((S, LANE), lambda i: (0, fcol)), _vec(LANE), pl.BlockSpec((S, LANE), lambda i: (0, 0))],
        out_specs=[pl.BlockSpec((S, LANE), lambda i: (0, 0)), _vec(LANE)],
        out_shape=[SDS((S, LANE), BF16), SDS((1, LANE), F32)],
        compiler_params=_cparams(("arbitrary",)), name=name)(gf, bfo, dcum)


REL_FAR = C_PREV * CHUNK - REL_CLIP


def _rel_onehot(qi, band):
    w = band - REL_FAR
    r = lax.broadcasted_iota(jnp.int32, (N_REL_PAD, w), 0)
    j = lax.broadcasted_iota(jnp.int32, (N_REL_PAD, w), 1) + REL_FAR
    idx = jnp.clip(C_PREV * CHUNK + qi - j, -REL_CLIP, REL_CLIP) + REL_CLIP
    return jnp.where(r == idx, 1.0, 0.0).astype(BF16)


def _rel_expand(rel, name):
    band = (C_PREV + 1) * CHUNK

    def body(rel_ref, o_ref):
        hi, mid, lo = _split3(rel_ref[...])
        far = jnp.broadcast_to(rel_ref[:, 2 * REL_CLIP:2 * REL_CLIP + 1], (N_HEADS, REL_FAR))

        def row(qi, carry):
            oh = _rel_onehot(qi, band)
            o_ref[qi, :, 0:REL_FAR] = far
            o_ref[qi, :, REL_FAR:band] = (jnp.dot(hi, oh, preferred_element_type=F32)
                                          + jnp.dot(mid, oh, preferred_element_type=F32)
                                          + jnp.dot(lo, oh, preferred_element_type=F32))
            return carry

        lax.fori_loop(0, CHUNK, row, 0, unroll=2)

    return pl.pallas_call(
        body, grid=(1,), in_specs=[pl.BlockSpec((N_HEADS, N_REL_PAD), lambda i: (0, 0))],
        out_specs=pl.BlockSpec((CHUNK, N_HEADS, band), lambda i: (0, 0, 0)),
        out_shape=SDS((CHUNK, N_HEADS, band), F32),
        compiler_params=_cparams(("arbitrary",)), name=name)(rel)


def _tri_dot_rhs(x, oh):
    hi, mid, lo = _split3(x)
    return (jnp.dot(hi, oh, preferred_element_type=F32) + jnp.dot(mid, oh, preferred_element_type=F32)
            + jnp.dot(lo, oh, preferred_element_type=F32))


def _rel_reduce(dbias, name):
    band = (C_PREV + 1) * CHUNK
    NT = (((1,), (1,)), ((), ()))

    def body(d_ref, o_ref):
        def row(qi, st):
            acc, far = st
            oh = _rel_onehot(qi, band)
            hi, mid, lo = _split3(d_ref[qi, :, REL_FAR:band])
            acc = acc + (lax.dot_general(hi, oh, NT, preferred_element_type=F32)
                         + lax.dot_general(mid, oh, NT, preferred_element_type=F32)
                         + lax.dot_general(lo, oh, NT, preferred_element_type=F32))
            return acc, far + jnp.sum(d_ref[qi, :, 0:REL_FAR], axis=-1, keepdims=True)

        acc, far = lax.fori_loop(0, CHUNK, row, (jnp.zeros((N_HEADS, N_REL_PAD), F32), jnp.zeros((N_HEADS, 1), F32)),
                                 unroll=2)
        col = lax.broadcasted_iota(jnp.int32, (N_HEADS, N_REL_PAD), 1)
        o_ref[...] = acc + jnp.where(col == 2 * REL_CLIP, far, 0.0)

    return pl.pallas_call(
        body, grid=(1,), in_specs=[pl.BlockSpec((CHUNK, N_HEADS, band), lambda i: (0, 0, 0))],
        out_specs=pl.BlockSpec((N_HEADS, N_REL_PAD), lambda i: (0, 0)),
        out_shape=SDS((N_HEADS, N_REL_PAD), F32),
        compiler_params=_cparams(("arbitrary",)), name=name)(dbias)


def _alibi_table():
    qi = np.arange(CHUNK)[:, None]
    j = np.arange((A_PREV + 1) * CHUNK)[None, :]
    dist = np.abs(A_PREV * CHUNK + qi - j).astype(np.float32)
    slopes = np.exp2(-8.0 * np.arange(1, N_HEADS + 1, dtype=np.float32) / N_HEADS).astype(np.float32)
    return jnp.asarray(-slopes[:, None, None] * dist[None])


def _ada_fwd(c_all, w, b, name):
    n = w.shape[2]

    def body(c_ref, w_ref, b_ref, o_ref):
        cv = c_ref[...]
        cond = (cv * _sigmoid(cv)).astype(BF16)
        o_ref[0] = jnp.dot(cond, w_ref[0].astype(BF16), preferred_element_type=F32) + b_ref[0]

    return pl.pallas_call(
        body, grid=(DEPTH,),
        in_specs=[pl.BlockSpec((16, D_MODEL), lambda l: (0, 0)), pl.BlockSpec((1, D_MODEL, n), lambda l: (l, 0, 0)),
                  pl.BlockSpec((1, 1, n), lambda l: (l, 0, 0))],
        out_specs=pl.BlockSpec((1, 16, n), lambda l: (l, 0, 0)), out_shape=SDS((DEPTH, 16, n), F32),
        compiler_params=_cparams(("parallel",)), name=name)(c_all, w, b)


def _ada_bwd(c_t, dmod, name):
    n = dmod.shape[2]
    bn = _blk(n, 512)
    tr = 256

    def body(c_ref, d_ref, o_ref):
        cv = c_ref[...]
        cond = (cv * _sigmoid(cv)).astype(BF16).astype(F32)
        dm = d_ref[0].astype(BF16).astype(F32)
        acc = cond[:, 0:1] * dm[0:1, :]
        for b_ in range(1, 8):
            acc = acc + cond[:, b_:b_ + 1] * dm[b_:b_ + 1, :]
        o_ref[0] = acc

    return pl.pallas_call(
        body, grid=(DEPTH, D_MODEL // tr, n // bn),
        in_specs=[pl.BlockSpec((tr, 8), lambda l, i, j: (i, 0)), pl.BlockSpec((1, 8, bn), lambda l, i, j: (l, 0, j))],
        out_specs=pl.BlockSpec((1, tr, bn), lambda l, i, j: (l, i, j)), out_shape=SDS((DEPTH, D_MODEL, n), F32),
        compiler_params=_cparams(("parallel", "parallel", "parallel")), name=name)(c_t, dmod)


def _adamw(w, m, v, parts, name):
    L, R, C = w.shape
    per_layer = isinstance(parts, (list, tuple))
    plist = list(parts) if per_layer else [parts]
    P = plist[0].shape[0]
    tr = _blk_rows(R, max(16, (1 << 18) // C))
    nr = R // tr
    c1 = 1.0 - ADAM_B1 ** ADAM_STEP
    c2 = 1.0 - ADAM_B2 ** ADAM_STEP

    def total(p_ref):
        g = p_ref[0].astype(F32)
        for k in range(1, P):
            g = g + p_ref[k].astype(F32)
        return g

    def body(w_ref, m_ref, v_ref, *rest):
        p_refs, (g_ref, d_ref, nm_ref, nv_ref) = rest[:len(plist)], rest[len(plist):]
        g = total(p_refs[0])
        for k in range(1, len(plist)):
            g = jnp.where(pl.program_id(0) == k, total(p_refs[k]), g)
        mn = ADAM_B1 * m_ref[0] + (1.0 - ADAM_B1) * g
        vn = ADAM_B2 * v_ref[0] + (1.0 - ADAM_B2) * (g * g)
        m_hat = mn / c1
        v_hat = vn / c2
        g_ref[0] = g
        nm_ref[0] = mn
        nv_ref[0] = vn
        d_ref[0] = -ADAM_LR * (m_hat / (jnp.sqrt(v_hat) + ADAM_EPS) + ADAM_WD * w_ref[0])

    rs = pl.BlockSpec((1, tr, C), lambda l, i: (l, i, 0))
    if per_layer:
        pspecs = [pl.BlockSpec((P, tr, C), functools.partial(lambda l, i, k: (0, jnp.where(l == k, i, 0), 0), k=k))
                  for k in range(L)]
    else:
        pspecs = [pl.BlockSpec((P, tr, C), lambda l, i: (0, l * nr + i, 0))]
    return pl.pallas_call(
        body, grid=(L, nr), in_specs=[rs, rs, rs] + pspecs,
        out_specs=[rs, rs, rs, rs], out_shape=[SDS((L, R, C), F32)] * 4,
        compiler_params=_cparams(("parallel", "parallel")), name=name)(w, m, v, *plist)


def _blk_rows(R, cap):
    if R <= cap:
        return R
    best = None
    for t in range(16, cap + 1, 16):
        if R % t == 0:
            best = t
    assert best is not None, (R, cap)
    return best


def _add_cast_rows(g, t, name):
    Q, R, C = g.shape
    half = R // 2
    tr = _blk_rows(half, max(16, (1 << 19) // C))
    nb = half // tr

    def body(lo_ref, hi_ref, t_ref, o_ref):
        c = lax.axis_index("c")

        @pl.when(c == 0)
        def _():
            o_ref[...] = (lo_ref[...] + t_ref[...]).astype(o_ref.dtype)

        @pl.when(c == 1)
        def _():
            o_ref[...] = (hi_ref[...] + t_ref[...]).astype(o_ref.dtype)

    bs = pl.BlockSpec((1, tr, C), lambda q, i: (q, i, 0))
    hi = pl.BlockSpec((1, tr, C), lambda q, i: (q, nb + i, 0))
    return pl.pallas_call(
        body, grid=(Q, nb), in_specs=[bs, hi, bs], out_specs=bs, out_shape=SDS((Q, half, C), BF16),
        compiler_params=_cparams(("parallel", "parallel")), name=name)(g, g, t)


def _coords():
    return lax.axis_index("x"), lax.axis_index("y"), lax.axis_index("c")


def _flip(v, bit):
    return 1 - v if bit else v


def _all_gather8(v, name):
    R = v.shape[0]

    def body(v_ref, o_ref, send_sems, recv_sems):
        x, y, c = _coords()
        me = 4 * x + 2 * y + c
        o_ref[me] = v_ref[...]
        copies = []
        for k in range(1, 8):
            peer = (_flip(x, k & 4), _flip(y, k & 2), _flip(c, k & 1))
            cp = pltpu.make_async_remote_copy(
                src_ref=v_ref, dst_ref=o_ref.at[me], send_sem=send_sems.at[k - 1], recv_sem=recv_sems.at[k - 1],
                device_id=peer, device_id_type=MESH)
            cp.start()
            copies.append(cp)
        for cp in copies:
            cp.wait_recv()
        for cp in copies:
            cp.wait_send()

    return pl.pallas_call(
        body, in_specs=[VMEM_SPEC], out_specs=VMEM_SPEC, out_shape=SDS((8, R, LANE), v.dtype),
        scratch_shapes=[pltpu.SemaphoreType.DMA((7,)), pltpu.SemaphoreType.DMA((7,))],
        compiler_params=pltpu.CompilerParams(vmem_limit_bytes=VMEM_LIMIT), name=name)(v)


def _sibling_swap_rows(arrs, name):
    n = len(arrs)

    def body(*refs):
        in_refs, out_refs = refs[:n], refs[n:2 * n]
        send_sems, recv_sems = refs[2 * n:]
        x, y, c = _coords()
        copies = []
        for a in range(n):
            Q, R = in_refs[a].shape[0], in_refs[a].shape[1]
            half = R // 2
            src = in_refs[a].at[pl.ds(0, Q), pl.ds(pl.multiple_of((1 - c) * half, 16), half)]
            cp = pltpu.make_async_remote_copy(
                src_ref=src, dst_ref=out_refs[a], send_sem=send_sems.at[a], recv_sem=recv_sems.at[a],
                device_id=(x, y, 1 - c), device_id_type=MESH)
            cp.start()
            copies.append(cp)
        for cp in copies:
            cp.wait_recv()
        for cp in copies:
            cp.wait_send()

    return pl.pallas_call(
        body, in_specs=[ANY] * n, out_specs=[ANY] * n,
        out_shape=[SDS((a.shape[0], a.shape[1] // 2, a.shape[2]), a.dtype) for a in arrs],
        scratch_shapes=[pltpu.SemaphoreType.DMA((n,)), pltpu.SemaphoreType.DMA((n,))],
        name=name)(*arrs)


def _chip_exchange(arrs, *, reduce, name):
    n = len(arrs)

    def body(*refs):
        in_refs, out_refs = refs[:n], refs[n:2 * n]
        ici_send, ici_recv, d2d_send, d2d_recv, loc_sem = refs[2 * n:]
        x, y, c = _coords()
        p = 2 * x + y
        local, first, fwd = [], [], []
        for a in range(n):
            R = out_refs[a].shape[1] // 2
            half = pl.ds(pl.multiple_of(c * R, 16), R)
            if reduce:
                lc = pltpu.make_async_copy(in_refs[a].at[p], out_refs[a].at[p, half], loc_sem.at[a])
            else:
                lc = pltpu.make_async_copy(in_refs[a], out_refs[a].at[p], loc_sem.at[a])
            lc.start()
            local.append(lc)
            for k in range(1, 4):
                qx, qy = _flip(x, k & 2), _flip(y, k & 1)
                src = in_refs[a].at[2 * qx + qy] if reduce else in_refs[a].at[half]
                cp = pltpu.make_async_remote_copy(
                    src_ref=src, dst_ref=out_refs[a].at[p, half], send_sem=ici_send.at[a, k - 1],
                    recv_sem=ici_recv.at[a, k - 1], device_id=(qx, qy, c), device_id_type=MESH)
                cp.start()
                first.append(cp)
        for a in range(n):
            R = out_refs[a].shape[1] // 2
            half = pl.ds(pl.multiple_of(c * R, 16), R)
            for k in range(0 if reduce else 1, 4):
                qx, qy = _flip(x, k & 2), _flip(y, k & 1)
                slot = out_refs[a].at[2 * qx + qy, half]
                if k == 0:
                    local[a].wait()
                else:
                    first[a * 3 + k - 1].wait_recv()
                cp = pltpu.make_async_remote_copy(
                    src_ref=slot, dst_ref=slot, send_sem=d2d_send.at[a, k], recv_sem=d2d_recv.at[a, k],
                    device_id=(x, y, 1 - c), device_id_type=MESH)
                cp.start()
                fwd.append(cp)
        for cp in fwd:
            cp.wait_recv()
        for cp in first + fwd:
            cp.wait_send()
        if not reduce:
            for lc in local:
                lc.wait()

    if reduce:
        out_shape = [SDS((4, 2 * a.shape[1], a.shape[2]), a.dtype) for a in arrs]
    else:
        out_shape = [SDS((4,) + a.shape, a.dtype) for a in arrs]
    return pl.pallas_call(
        body, in_specs=[ANY] * n, out_specs=[ANY] * n, out_shape=out_shape,
        scratch_shapes=[pltpu.SemaphoreType.DMA((n, 3)), pltpu.SemaphoreType.DMA((n, 3)),
                        pltpu.SemaphoreType.DMA((n, 4)), pltpu.SemaphoreType.DMA((n, 4)),
                        pltpu.SemaphoreType.DMA((n,))],
        name=name)(*arrs)


class _LayerExchange:
    aliased = False

    def __init__(self, srcs, lay, reduce):
        self.srcs, self.lay, self.reduce = list(srcs), lay, reduce
        self.n = len(self.srcs)
        if reduce:
            self.out_shapes = [SDS(a.shape, a.dtype) for a in self.srcs]
        else:
            self.out_shapes = [SDS((4,) + a.shape, a.dtype) for a in self.srcs]
        self.sem_shapes = [pltpu.SemaphoreType.DMA((self.n, 3)), pltpu.SemaphoreType.DMA((self.n, 3)),
                           pltpu.SemaphoreType.DMA((self.n,))]

    def _copies(self, src_refs, dst_refs, sems):
        ici_send, ici_recv, loc_sem = sems
        x, y, c = _coords()
        p = 2 * x + y
        local, remote = [], []
        for a in range(self.n):
            src_own = src_refs[a].at[p] if self.reduce else src_refs[a]
            local.append(pltpu.make_async_copy(src_own, dst_refs[a].at[p], loc_sem.at[a]))
            for k in range(1, 4):
                qx, qy = _flip(x, k & 2), _flip(y, k & 1)
                src = src_refs[a].at[2 * qx + qy] if self.reduce else src_refs[a]
                remote.append(pltpu.make_async_remote_copy(
                    src_ref=src, dst_ref=dst_refs[a].at[p], send_sem=ici_send.at[a, k - 1],
                    recv_sem=ici_recv.at[a, k - 1], device_id=(qx, qy, self.lay), device_id_type=MESH))
        return c, local, remote

    def start(self, src_refs, dst_refs, sems):
        c, local, remote = self._copies(src_refs, dst_refs, sems)
        if self.reduce:
            @pl.when(c == self.lay)
            def _():
                for cp in local + remote:
                    cp.start()
        else:
            for cp in local:
                cp.start()

            @pl.when(c == self.lay)
            def _():
                for cp in remote:
                    cp.start()

    def finish(self, src_refs, dst_refs, sems):
        c, local, remote = self._copies(src_refs, dst_refs, sems)
        if self.reduce:
            @pl.when(c == self.lay)
            def _():
                for cp in remote:
                    cp.wait_recv()
                for cp in remote:
                    cp.wait_send()
                for cp in local:
                    cp.wait()
        else:
            @pl.when(c == self.lay)
            def _():
                for cp in remote:
                    cp.wait_recv()
                for cp in remote:
                    cp.wait_send()

            for cp in local:
                cp.wait()

    def run(self, name):
        n = self.n

        def body(*refs):
            src_refs, dst_refs, sems = refs[:n], refs[n:2 * n], refs[2 * n:]
            self.start(src_refs, dst_refs, sems)
            self.finish(src_refs, dst_refs, sems)

        return pl.pallas_call(body, in_specs=[ANY] * n, out_specs=[ANY] * n, out_shape=self.out_shapes,
                              scratch_shapes=self.sem_shapes, name=name)(*self.srcs)


def _call_hosting(body, *, comm, grid, in_specs, out_specs, out_shape, scratch_shapes, name, args, semantics=None):
    n_in, n_out, n_scr = len(args), len(out_shape), len(scratch_shapes)
    if comm is None:
        sem = semantics if semantics is not None else ("parallel",) * len(grid)
        res = pl.pallas_call(body, grid=grid, in_specs=in_specs, out_specs=out_specs, out_shape=out_shape,
                             scratch_shapes=scratch_shapes, compiler_params=_cparams(sem), name=name)(*args)
        return list(res), None
    k = comm.n

    def hosted(*refs):
        ins, cin = refs[:n_in], refs[n_in:n_in + k]
        outs = refs[n_in + k:n_in + k + n_out]
        cout = refs[n_in + k + n_out:n_in + 2 * k + n_out]
        scr = refs[n_in + 2 * k + n_out:n_in + 2 * k + n_out + n_scr]
        sems = refs[n_in + 2 * k + n_out + n_scr:]
        first = pl.program_id(0) == 0
        last = pl.program_id(0) == grid[0] - 1
        for d in range(1, len(grid)):
            first = jnp.logical_and(first, pl.program_id(d) == 0)
            last = jnp.logical_and(last, pl.program_id(d) == grid[d] - 1)

        @pl.when(first)
        def _():
            comm.start(cin, cout, sems)

        body(*ins, *outs, *scr)

        @pl.when(last)
        def _():
            comm.finish(cin, cout, sems)

    aliases = {n_in + j: n_out + j for j in range(k)} if comm.aliased else {}
    res = pl.pallas_call(
        hosted, grid=grid, in_specs=list(in_specs) + [ANY] * k, out_specs=list(out_specs) + [ANY] * k,
        out_shape=list(out_shape) + comm.out_shapes, scratch_shapes=list(scratch_shapes) + comm.sem_shapes,
        input_output_aliases=aliases, compiler_params=_cparams(("arbitrary",) * len(grid)),
        name=name)(*args, *comm.srcs)
    return list(res[:n_out]), list(res[n_out:])


class _RowHalfGather:
    aliased = False

    def __init__(self, srcs):
        self.srcs, self.n = list(srcs), len(srcs)
        self.out_shapes = [SDS((4,) + a.shape, a.dtype) for a in self.srcs]
        n = self.n
        self.sem_shapes = [pltpu.SemaphoreType.DMA((n, 3)), pltpu.SemaphoreType.DMA((n, 3)),
                           pltpu.SemaphoreType.DMA((n, 3)), pltpu.SemaphoreType.DMA((n, 3)),
                           pltpu.SemaphoreType.DMA((n,))]

    def _copies(self, src_refs, dst_refs, sems):
        ici_send, ici_recv, d2d_send, d2d_recv, loc_sem = sems
        x, y, c = _coords()
        p = 2 * x + y
        local, first, fwd = [], [], []
        for a in range(self.n):
            R = src_refs[a].shape[0] // 2
            half = pl.ds(pl.multiple_of(c * R, 16), R)
            local.append(pltpu.make_async_copy(src_refs[a], dst_refs[a].at[p], loc_sem.at[a]))
            for k in range(1, 4):
                qx, qy = _flip(x, k & 2), _flip(y, k & 1)
                first.append(pltpu.make_async_remote_copy(
                    src_ref=src_refs[a].at[half], dst_ref=dst_refs[a].at[p, half], send_sem=ici_send.at[a, k - 1],
                    recv_sem=ici_recv.at[a, k - 1], device_id=(qx, qy, c), device_id_type=MESH))
                slot = dst_refs[a].at[2 * qx + qy, half]
                fwd.append(pltpu.make_async_remote_copy(
                    src_ref=slot, dst_ref=slot, send_sem=d2d_send.at[a, k - 1], recv_sem=d2d_recv.at[a, k - 1],
                    device_id=(x, y, 1 - c), device_id_type=MESH))
        return local, first, fwd

    def start(self, src_refs, dst_refs, sems):
        local, first, _ = self._copies(src_refs, dst_refs, sems)
        for cp in local + first:
            cp.start()

    def finish(self, src_refs, dst_refs, sems):
        local, first, fwd = self._copies(src_refs, dst_refs, sems)
        for got, on in zip(first, fwd):
            got.wait_recv()
            on.start()
        for cp in fwd:
            cp.wait_recv()
        for cp in first + fwd:
            cp.wait_send()
        for cp in local:
            cp.wait()

    def run(self, name):
        n = self.n

        def body(*refs):
            src_refs, dst_refs, sems = refs[:n], refs[n:2 * n], refs[2 * n:]
            self.start(src_refs, dst_refs, sems)
            self.finish(src_refs, dst_refs, sems)

        return pl.pallas_call(body, in_specs=[ANY] * n, out_specs=[ANY] * n, out_shape=self.out_shapes,
                              scratch_shapes=self.sem_shapes, name=name)(*self.srcs)


class _SiblingSend:
    aliased = False

    def __init__(self, srcs, src_core):
        self.srcs, self.src_core, self.n = list(srcs), src_core, len(srcs)
        self.out_shapes = [SDS(a.shape, a.dtype) for a in self.srcs]
        self.sem_shapes = [pltpu.SemaphoreType.DMA((self.n,)), pltpu.SemaphoreType.DMA((self.n,))]

    def _copies(self, src_refs, dst_refs, sems):
        x, y, c = _coords()
        return c, [pltpu.make_async_remote_copy(
            src_ref=src_refs[a], dst_ref=dst_refs[a], send_sem=sems[0].at[a], recv_sem=sems[1].at[a],
            device_id=(x, y, 1 - c), device_id_type=MESH) for a in range(self.n)]

    def start(self, src_refs, dst_refs, sems):
        c, copies = self._copies(src_refs, dst_refs, sems)

        @pl.when(c == self.src_core)
        def _():
            for cp in copies:
                cp.start()

    def finish(self, src_refs, dst_refs, sems):
        c, copies = self._copies(src_refs, dst_refs, sems)

        @pl.when(c == self.src_core)
        def _():
            for cp in copies:
                cp.wait_send()

        @pl.when(c != self.src_core)
        def _():
            for cp in copies:
                cp.wait_recv()


class _Handoff:
    aliased = True

    def __init__(self, srcs, lay, slots):
        self.srcs, self.lay, self.slots, self.n = list(srcs), lay, tuple(slots), len(srcs)
        self.out_shapes = [SDS(a.shape, a.dtype) for a in self.srcs]
        ns = len(self.slots)
        self.sem_shapes = [pltpu.SemaphoreType.DMA((self.n, ns)), pltpu.SemaphoreType.DMA((self.n, ns))]

    def _copies(self, dst_refs, sems):
        x, y, c = _coords()
        copies = []
        for a in range(self.n):
            for j, k in enumerate(self.slots):
                slot = dst_refs[a].at[2 * _flip(x, k & 2) + _flip(y, k & 1)]
                copies.append(pltpu.make_async_remote_copy(
                    src_ref=slot, dst_ref=slot, send_sem=sems[0].at[a, j], recv_sem=sems[1].at[a, j],
                    device_id=(x, y, 1 - c), device_id_type=MESH))
        return c, copies

    def start(self, src_refs, dst_refs, sems):
        c, copies = self._copies(dst_refs, sems)

        @pl.when(c == self.lay)
        def _():
            for cp in copies:
                cp.start()

    def finish(self, src_refs, dst_refs, sems):
        c, copies = self._copies(dst_refs, sems)

        @pl.when(c == self.lay)
        def _():
            for cp in copies:
                cp.wait_send()

        @pl.when(c != self.lay)
        def _():
            for cp in copies:
                cp.wait_recv()


def _layer_handoff(bufs, lays, slots, name):
    flat = [b for group in bufs for b in group]
    n = len(flat)
    ns = len(slots)

    def body(*refs):
        out_refs = refs[n:2 * n]
        send_sems, recv_sems = refs[2 * n:]
        x, y, c = _coords()
        i = 0
        for group, lay in zip(bufs, lays):
            copies = []
            for _b in group:
                for j, k in enumerate(slots):
                    slot = out_refs[i].at[2 * _flip(x, k & 2) + _flip(y, k & 1)]
                    copies.append(pltpu.make_async_remote_copy(
                        src_ref=slot, dst_ref=slot, send_sem=send_sems.at[i, j], recv_sem=recv_sems.at[i, j],
                        device_id=(x, y, 1 - c), device_id_type=MESH))
                i += 1

            @pl.when(c == lay)
            def _(copies=copies):
                for cp in copies:
                    cp.start()
                for cp in copies:
                    cp.wait_send()

            @pl.when(c != lay)
            def _(copies=copies):
                for cp in copies:
                    cp.wait_recv()

    return pl.pallas_call(
        body, in_specs=[ANY] * n, out_specs=[ANY] * n, out_shape=[SDS(b.shape, b.dtype) for b in flat],
        input_output_aliases={i: i for i in range(n)},
        scratch_shapes=[pltpu.SemaphoreType.DMA((n, ns)), pltpu.SemaphoreType.DMA((n, ns))], name=name)(*flat)


def _sibling_send(arrs, src_core, name):
    n = len(arrs)

    def body(*refs):
        in_refs, out_refs = refs[:n], refs[n:2 * n]
        send_sems, recv_sems = refs[2 * n:]
        x, y, c = _coords()
        copies = [pltpu.make_async_remote_copy(
            src_ref=in_refs[a], dst_ref=out_refs[a], send_sem=send_sems.at[a], recv_sem=recv_sems.at[a],
            device_id=(x, y, 1 - c), device_id_type=MESH) for a in range(n)]

        @pl.when(c == src_core)
        def _():
            for cp in copies:
                cp.start()
            for cp in copies:
                cp.wait_send()

        @pl.when(c != src_core)
        def _():
            for cp in copies:
                cp.wait_recv()

    return pl.pallas_call(
        body, in_specs=[ANY] * n, out_specs=[ANY] * n, out_shape=[SDS(a.shape, a.dtype) for a in arrs],
        scratch_shapes=[pltpu.SemaphoreType.DMA((n,)), pltpu.SemaphoreType.DMA((n,))], name=name)(*arrs)


def _add_cast_on(a, b, lay, name):
    Q, R, C = b.shape
    tr = _blk_rows(R, max(16, (1 << 19) // C))

    def body(a_ref, b_ref, o_ref):
        @pl.when(lax.axis_index("c") == lay)
        def _():
            o_ref[...] = (a_ref[...] + b_ref[...]).astype(o_ref.dtype)

    bs = pl.BlockSpec((1, tr, C), lambda q, i: (q, i, 0))
    return pl.pallas_call(
        body, grid=(Q, R // tr), in_specs=[bs, bs], out_specs=bs, out_shape=SDS((Q, R, C), BF16),
        compiler_params=_cparams(("parallel", "parallel")), name=name)(a, b)


_IN_SIZES = (512, 128, 128, 512, 512, 512, 8, 512, 512, 512, 3072)
_IN_OFF = tuple(int(v) for v in np.cumsum((0,) + _IN_SIZES))
_IN_Q = N_IN_COLS // 4


def _pack_w_in(w):
    def cols(lo, hi):
        out = []
        while lo < hi:
            q, off = divmod(lo, _IN_Q)
            n = min(hi - lo, _IN_Q - off)
            out.append(w[q, :, off:off + n])
            lo += n
        return out

    fb0, fb1, g0 = _IN_OFF[6], _IN_OFF[7], _IN_OFF[10]
    wqkv = jnp.concatenate(cols(0, fb0) + cols(fb1, g0), axis=1)
    wgf = jnp.concatenate(cols(g0, N_IN_COLS) + cols(fb0, fb1) + [jnp.zeros((w.shape[1], LANE - 8), w.dtype)], axis=1)
    return wqkv, wgf


def _unpack_w_in(dqkv, dgf):
    fb0, fb1, g0 = _IN_OFF[6], _IN_OFF[7], _IN_OFF[10]

    def cols(lo, hi):
        out = []
        while lo < hi:
            if lo < fb0:
                n = min(hi, fb0) - lo
                out.append(dqkv[:, lo:lo + n])
            elif lo < fb1:
                n = min(hi, fb1) - lo
                out.append(dgf[:, 3072 + lo - fb0:3072 + lo - fb0 + n])
            elif lo < g0:
                n = min(hi, g0) - lo
                out.append(dqkv[:, lo - 8:lo - 8 + n])
            else:
                n = hi - lo
                out.append(dgf[:, lo - g0:lo - g0 + n])
            lo += n
        return out

    return jnp.stack([jnp.concatenate(cols(q * _IN_Q, (q + 1) * _IN_Q), axis=1) for q in range(4)])


def _pad_rows(a, rows):
    return jnp.pad(a, ((0, rows - a.shape[0]), (0, 0)))


def _small_pack(parts):
    flat = jnp.concatenate([p.reshape(-1) for p in parts])
    n = flat.shape[0]
    rows = -(-n // LANE)
    rows = -(-rows // 8) * 8
    return jnp.pad(flat, (0, rows * LANE - n)).reshape(rows, LANE)


def _small_unpack(block, shapes):
    flat = block.reshape(-1)
    out, off = [], 0
    for s in shapes:
        n = int(np.prod(s))
        out.append(flat[off:off + n].reshape(s))
        off += n
    return out


def _kv_same(g):
    return 0


def _kv_own(g):
    return g


_mm_plain = _mm


def _mm_hosting(a, b, *, comm, **kw):
    if comm is None:
        return _mm(a, b, **kw), None
    return _mm(a, b, comm=comm, **kw)


def _layer_fwd(x, mod, p, l, ride):
    sh_m, sc_m, g_m, sh_f, sc_f, g_f = mod
    nm = "l%d_" % l

    def carried(name, run):
        res, got = run(ride.comm_for(name))
        if got is not None:
            ride.done(name, got)
        return res

    h1 = _norm_mod_fwd(x, p["norm_mix_g"], sc_m, sh_m, nm + "norm_mix_fwd")
    qkv = carried("proj_qkv", lambda cm: _mm_hosting(h1, p["wqkv"], mode="nn", out_dtype=BF16,
                                                     name=nm + "proj_qkv", comm=cm))
    gf = _mm(h1, p["wgf"], mode="nn", out_dtype=F32, name=nm + "proj_gf", cap_n=640)
    qkv_t = qkv.T
    o_a_t = carried("attn_a", lambda cm: _bandT_fwd(
        qkv_t[0:512], _heads(qkv[:, 512:640], A_KV_HEADS), qkv_t[640:768], p["alibi"], p["sink_tab"],
        GQ=4, GK=1, P=A_PREV, kvoff=_kv_same, name=nm + "attn_a_fwd", comm=cm))
    cum = _fox_cum(gf, p["b_forget_pad"], nm + "fox_cum")
    cum_t = cum[:, :N_HEADS].T
    cc, cr = cum_t[:, :, None], cum_t[:, None, :]
    o_b_t, lse_b = carried("attn_b", lambda cm: _foxT_fwd(
        qkv_t[768:1280], _heads(qkv[:, 1280:1792], N_HEADS), qkv_t[1792:2304], cc, cr, nm + "attn_b_fwd", comm=cm))
    o_c_t = carried("attn_c", lambda cm: _bandT_fwd(
        qkv_t[2304:2816], _heads(qkv[:, 2816:3328], N_HEADS), qkv_t[3328:3840], p["rel_tab"], p["no_sink"],
        GQ=2, GK=2, P=C_PREV, kvoff=_kv_own, name=nm + "attn_c_fwd", comm=cm))
    p = dict(p, **ride.late_weights())
    o = jnp.concatenate([o_a_t, o_b_t, o_c_t], axis=0).T
    y = _mm(o, p["wb"], mode="nn", out_dtype=F32, groups=3, name=nm + "branch")
    merged = _merge_fwd(y, gf, nm + "merge_fwd")
    mix = _mm(merged, p["wout"], mode="nn", out_dtype=F32, name=nm + "out_proj")
    x1 = _resid_fwd(x, mix, g_m, nm + "resid_mix")
    h2 = _norm_mod_fwd(x1, p["norm_ffn_g"], sc_f, sh_f, nm + "norm_ffn_fwd")
    u = _mm(h2, p["wfi"], mode="nn", out_dtype=F32, name=nm + "ffn_in", cap_n=512)
    a = _swiglu_fwd(u, nm + "swiglu_fwd")
    f = _mm(a, p["wfo"], mode="nn", out_dtype=F32, name=nm + "ffn_out", cap_m=1024)
    x2 = _resid_fwd(x1, f, g_f, nm + "resid_ffn")
    saved = dict(x=x, h1=h1, qkv=qkv, qkv_t=qkv_t, gf=gf, cc=cc, cr=cr, o_b_t=o_b_t, lse_b=lse_b, o=o, y=y, merged=merged,
                 mix=mix, x1=x1, h2=h2, u=u, a=a, f=f)
    return x2, saved, p


def _layer_bwd(dx2, mod, p, s, l, ride=None):
    sh_m, sc_m, g_m, sh_f, sc_f, g_f = mod
    nm = "l%d_" % l

    def _mm(a, b, *, name, **kw):
        comm = ride.comm_for(name) if ride is not None else None
        if comm is None:
            return _mm_plain(a, b, name=nm + name, **kw)
        out, got = _mm_plain(a, b, name=nm + name, comm=comm, **kw)
        ride.done(name, got)
        return out

    dg_f, df = _resid_bwd(dx2, s["f"], g_f, nm + "resid_ffn_bwd")
    da = _mm(df, p["wfo"], mode="nt", out_dtype=F32, name="ffn_out_dx", cap_m=1024, cap_n=1408)
    d_wfo = _mm(s["a"], df, mode="tn", out_dtype=F32, name="ffn_out_dw", cap_m=1408, cap_k=2048)
    du = _swiglu_bwd(da, s["u"], nm + "swiglu_bwd")
    dh2 = _mm(du, p["wfi"], mode="nt", out_dtype=F32, name="ffn_in_dx", cap_m=1024)
    d_wfi = _mm(s["h2"], du, mode="tn", out_dtype=F32, name="ffn_in_dw", cap_m=1024, cap_n=1408, cap_k=2048,
                col_quarters=True)
    dx1, dsc_f, dsh_f, dgn_f = _norm_mod_bwd(s["x1"], [dh2], dx2, p["norm_ffn_g"], sc_f, nm + "norm_ffn_bwd")
    dg_m, dmix = _resid_bwd(dx1, s["mix"], g_m, nm + "resid_mix_bwd")
    dmerged = _mm(dmix, p["wout"], mode="nt", out_dtype=F32, name="out_proj_dx")
    d_wout = _mm(s["merged"], dmix, mode="tn", out_dtype=F32, name="out_proj_dw", cap_m=1024, cap_k=2048)
    dy, dgates = _merge_bwd(dmerged, s["y"], s["gf"], nm + "merge_bwd")
    do = _mm(dy, p["wb"], mode="nt", out_dtype=BF16, groups=3, name="branch_dx")
    d_wb = _mm(s["o"], dy, mode="tn", out_dtype=F32, groups=3, name="branch_dw", cap_k=2048,
               col_quarters=True)
    comms = ride.exchanges() if ride is not None else (None, None, None)
    qkv, qkv_t = s["qkv"], s["qkv_t"]
    do_t = do.T
    (dqa_t, dka_h, dva_h, _, dsink), got_a = _bandT_bwd(
        qkv_t[0:512], _heads(qkv[:, 0:512], N_HEADS), _heads(qkv[:, 512:640], A_KV_HEADS), qkv_t[512:640],
        _heads(qkv[:, 640:768], A_KV_HEADS), do_t[0:512], _heads(do[:, 0:512], N_HEADS), p["alibi"], p["sink_tab"],
        GQ=4, GK=1, P=A_PREV, kvoff=_kv_same, name=nm + "attn_a_bwd", comm=comms[0])
    (dqb_t, dkb_h, dvb_h, dck, dcq), got_b = _foxT_bwd(
        qkv_t[768:1280], _heads(qkv[:, 768:1280], N_HEADS), _heads(qkv[:, 1280:1792], N_HEADS), qkv_t[1280:1792],
        _heads(qkv[:, 1792:2304], N_HEADS), s["cc"], s["cr"], s["o_b_t"], do_t[512:1024],
        _heads(do[:, 512:1024], N_HEADS), s["lse_b"], nm + "attn_b_bwd", comm=comms[1])
    dcum = jnp.pad((dck[:, :, 0] + dcq[:, 0, :]).T, ((0, 0), (0, LANE - N_HEADS)))
    dfb, db_forget = _fox_cum_bwd(s["gf"], p["b_forget_pad"], dcum, nm + "fox_cum_bwd")
    (dqc_t, dkc_h, dvc_h, dbias_c, _), got_c = _bandT_bwd(
        qkv_t[2304:2816], _heads(qkv[:, 2304:2816], N_HEADS), _heads(qkv[:, 2816:3328], N_HEADS), qkv_t[2816:3328],
        _heads(qkv[:, 3328:3840], N_HEADS), do_t[1024:1536], _heads(do[:, 1024:1536], N_HEADS), p["rel_tab"],
        p["no_sink"], GQ=2, GK=2, P=C_PREV, kvoff=_kv_own, name=nm + "attn_c_bwd", comm=comms[2])
    d_rel = _rel_reduce(jnp.transpose(_unpair_table(dbias_c), (1, 0, 2)), nm + "rel_reduce")[:, :N_REL]
    dqkv = jnp.concatenate([dqa_t.T, _unheads(dka_h), _unheads(dva_h), dqb_t.T, _unheads(dkb_h), _unheads(dvb_h),
                            dqc_t.T, _unheads(dkc_h), _unheads(dvc_h)], axis=1)
    dgf = jnp.concatenate([dgates, dfb], axis=1)
    if ride is not None:
        ride.exchanged((got_a, got_b, got_c))
    dh1a = _mm(dqkv, p["wqkv"], mode="nt", out_dtype=F32, name="proj_qkv_dx", cap_k=1024)
    dh1b = _mm(dgf, p["wgf"], mode="nt", out_dtype=F32, name="proj_gf_dx", cap_k=640)
    d_wqkv = _mm(s["h1"], dqkv, mode="tn", out_dtype=F32, name="proj_qkv_dw", cap_m=1024, cap_k=2048)
    d_wgf = _mm(s["h1"], dgf, mode="tn", out_dtype=F32, name="proj_gf_dw", cap_m=1024, cap_n=640, cap_k=2048)
    dx, dsc_m, dsh_m, dgn_m = _norm_mod_bwd(s["x"], [dh1a, dh1b], dx1, p["norm_mix_g"], sc_m, nm + "norm_mix_bwd")
    d_mod = jnp.concatenate([dsh_m, dsc_m, dg_m, dsh_f, dsc_f, dg_f], axis=1)[0]
    grads = dict(w_in=_unpack_w_in(d_wqkv, d_wgf), w_branch=d_wb, w_out=d_wout.reshape(4, -1, D_MODEL),
                 w_ffn_in=d_wfi, w_ffn_out=d_wfo.reshape(4, -1, D_MODEL),
                 norm_mix_g=dgn_m[0], norm_ffn_g=dgn_f[0], b_forget=db_forget[0, :N_HEADS],
                 sinks=dsink[:, 0, 0], rel_bias=d_rel, d_mod=d_mod)
    return dx, grads


def kernel(x, c, norm_mix_g, norm_ffn_g, w_ada, b_ada, w_in, b_forget, sinks, rel_bias, w_branch, w_out, w_ffn_in, w_ffn_out, final_norm_g, loss_target, m_norm_mix_g, m_norm_ffn_g, m_w_ada, m_b_ada, m_w_in, m_b_forget, m_sinks, m_rel_bias, m_w_branch, m_w_out, m_w_ffn_in, m_w_ffn_out, m_final_norm_g, v_norm_mix_g, v_norm_ffn_g, v_w_ada, v_b_ada, v_w_in, v_b_forget, v_sinks, v_rel_bias, v_w_branch, v_w_out, v_w_ffn_in, v_w_ffn_out, v_final_norm_g):
    xi, yi, ci = _coords()
    chip = 2 * xi + yi
    dev = 2 * chip + ci
    xs = x[0]
    S = xs.shape[0]
    n_ada = w_ada.shape[2]

    big_names = ("w_in", "w_branch", "w_out", "w_ffn_in", "w_ffn_out")
    big_w = dict(w_in=w_in, w_branch=w_branch, w_out=w_out, w_ffn_in=w_ffn_in, w_ffn_out=w_ffn_out)
    big_m = dict(w_in=m_w_in, w_branch=m_w_branch, w_out=m_w_out, w_ffn_in=m_w_ffn_in, w_ffn_out=m_w_ffn_out)
    big_v = dict(w_in=v_w_in, w_branch=v_w_branch, w_out=v_w_out, w_ffn_in=v_w_ffn_in, w_ffn_out=v_w_ffn_out)
    flat2 = lambda a: a.reshape(-1, a.shape[-1])
    shards = [[flat2(big_w[n][l]).astype(BF16) for n in big_names] for l in range(DEPTH)]
    gw = [[None] * len(big_names) for _ in range(DEPTH)]
    gw[0][0] = _RowHalfGather([shards[0][0]]).run("weights_gather_w_in_l0")[0]
    host_g = ((1, 2, 4), (0,), (3,))

    class WeightRide:
        def __init__(self, l, plan):
            self.l, self.plan = l, plan

        def comm_for(self, name):
            if name not in self.plan:
                return None
            lay, idx = self.plan[name]
            return _RowHalfGather([shards[lay][i] for i in idx])

        def done(self, name, got):
            lay, idx = self.plan[name]
            for i, r in zip(idx, got):
                gw[lay][i] = r

        def late_weights(self):
            g = gw[self.l]
            return dict(wb=jnp.transpose(g[1], (1, 0, 2)).reshape(3 * BRANCH_W, D_MODEL),
                        wout=g[2].reshape(D_MODEL, D_MODEL),
                        wfi=jnp.transpose(g[3], (1, 0, 2)).reshape(D_MODEL, 2 * FFN_H),
                        wfo=g[4].reshape(FFN_H, D_MODEL))

    weight_plan = [
        {"proj_qkv": (0, (1, 2)), "attn_a": (0, (4,)), "attn_b": (0, (3,)), "attn_c": (1, (0,))},
        {"attn_a": (1, (1, 2)), "attn_b": (1, (3,)), "attn_c": (1, (4,))}]

    def hosted(arrs, split, reduce):
        return tuple(_LayerExchange([arrs[i] for i in idx], 1, reduce) for idx in split)

    def unsplit(got, split):
        out = [None] * len(big_names)
        for res, idx in zip(got, split):
            for r, i in zip(res, idx):
                out[i] = r
        return out

    c_all = _all_gather8(c.reshape(8, LANE), "gather_c").reshape(8, D_MODEL)
    b_sh = lax.dynamic_slice_in_dim(b_ada, chip * n_ada, n_ada, axis=1)[:, None, :]
    mod_sh = _ada_fwd(_pad_rows(c_all, 16), w_ada, b_sh, "ada_fwd")[:, :8, :]
    mod_all = _all_gather8(mod_sh.reshape(-1, LANE), "gather_mod").reshape(8, DEPTH, 8, n_ada)
    mod_mine = lax.dynamic_index_in_dim(mod_all[0::2], dev, axis=2, keepdims=False)
    mod = mod_mine.transpose(1, 0, 2).reshape(DEPTH, 6, D_MODEL)

    alibi = _pair_table(_alibi_table())
    no_sink = jnp.full((N_HEADS, 8, LANE), NEG_INF, F32)
    def make_params(l):
        wqkv, wgf = _pack_w_in(gw[l][0])
        rel_tab = _rel_expand(jnp.pad(rel_bias[l], ((0, 0), (0, N_REL_PAD - N_REL))), "l%d_rel_expand" % l)
        return dict(
            wqkv=wqkv, wgf=wgf, norm_mix_g=norm_mix_g[l][None], norm_ffn_g=norm_ffn_g[l][None],
            b_forget_pad=jnp.pad(b_forget[l], (0, LANE - N_HEADS))[None],
            sink_tab=jnp.broadcast_to(sinks[l][:, None, None], (N_HEADS, 8, LANE)),
            no_sink=no_sink, alibi=alibi, rel_tab=_pair_table(jnp.transpose(rel_tab, (1, 0, 2))))

    mods = [[mod[l, k][None] for k in range(6)] for l in range(DEPTH)]
    params, saved = [None] * DEPTH, [None] * DEPTH
    h = xs
    for l in range(DEPTH):
        h, saved[l], params[l] = _layer_fwd(h, mods[l], make_params(l), l, WeightRide(l, weight_plan[l]))
    loss_dev, dh, d_final = _final_loss(h, final_norm_g[None], loss_target[0], "final_loss")
    grads = [None] * DEPTH
    dh, grads[1] = _layer_bwd(dh, mods[1], params[1], saved[1], 1)

    class Layer1Ride:
        sends = {"ffn_out_dx": (4,), "ffn_out_dw": (1, 2), "ffn_in_dx": (3,), "ffn_in_dw": (0,)}
        hands = {"proj_qkv_dx": (0,), "proj_gf_dx": (3,), "proj_qkv_dw": (4,), "proj_gf_dw": (1, 2)}

        def __init__(self, g):
            self.g, self.t = g, [None] * len(g)
            self.parts, self.final = [None] * len(g), [None] * len(g)

        def comm_for(self, name):
            if name in self.sends:
                return _SiblingSend([self.g[i] for i in self.sends[name]], 0)
            if name in self.hands:
                return _Handoff([self.parts[i] for i in self.hands[name]], 1, (0, 1, 2, 3))
            return None

        def done(self, name, got):
            idx, dst = (self.sends[name], self.t) if name in self.sends else (self.hands[name], self.final)
            for i, r in zip(idx, got):
                dst[i] = r

        def exchanges(self):
            sums = [_add_cast_on(a, b, 1, "grads_chip_sum_l1_" + n) for n, a, b in zip(big_names, self.g, self.t)]
            return hosted(sums, host_g, True)

        def exchanged(self, got):
            self.parts = unsplit(got, host_g)

    ride = Layer1Ride([grads[1][n] for n in big_names])
    dh, grads[0] = _layer_bwd(dh, mods[0], params[0], saved[0], 0, ride)
    grad_x = dh[None]
    loss = lax.psum(loss_dev[0, 0], ("x", "y", "c"))
    parts1 = ride.final
    g0 = [grads[0][n] for n in big_names]
    t0 = _sibling_swap_rows(g0, "grads_swap_l0")
    sums0 = [_add_cast_rows(a, b, "grads_chip_sum_l0_" + n) for n, a, b in zip(big_names, g0, t0)]
    parts0 = _chip_exchange(sums0, reduce=True, name="grads_reduce_l0")
    big_out = {}
    for n, p0, p1 in zip(big_names, parts0, parts1):
        shp = big_w[n].shape
        as3 = lambda a: a.reshape(shp[0], -1, shp[-1])
        res = _adamw(as3(big_w[n]), as3(big_m[n]), as3(big_v[n]), [p0, p1], "adamw_" + n)
        big_out[n] = [r.reshape(shp) for r in res]

    small_names = ("norm_mix_g", "norm_ffn_g", "b_ada", "b_forget", "sinks", "rel_bias", "final_norm_g")
    small_w = dict(norm_mix_g=norm_mix_g, norm_ffn_g=norm_ffn_g, b_ada=b_ada, b_forget=b_forget, sinks=sinks,
                   rel_bias=rel_bias, final_norm_g=final_norm_g)
    small_m = dict(norm_mix_g=m_norm_mix_g, norm_ffn_g=m_norm_ffn_g, b_ada=m_b_ada, b_forget=m_b_forget,
                   sinks=m_sinks, rel_bias=m_rel_bias, final_norm_g=m_final_norm_g)
    small_v = dict(norm_mix_g=v_norm_mix_g, norm_ffn_g=v_norm_ffn_g, b_ada=v_b_ada, b_forget=v_b_forget,
                   sinks=v_sinks, rel_bias=v_rel_bias, final_norm_g=v_final_norm_g)
    small_g = dict(
        norm_mix_g=jnp.stack([grads[l]["norm_mix_g"] for l in range(DEPTH)]),
        norm_ffn_g=jnp.stack([grads[l]["norm_ffn_g"] for l in range(DEPTH)]),
        b_ada=jnp.stack([grads[l]["d_mod"] for l in range(DEPTH)]),
        b_forget=jnp.stack([grads[l]["b_forget"] for l in range(DEPTH)]),
        sinks=jnp.stack([grads[l]["sinks"] for l in range(DEPTH)]),
        rel_bias=jnp.stack([grads[l]["rel_bias"] for l in range(DEPTH)]),
        final_norm_g=d_final[0])
    shapes = [small_w[n].shape for n in small_names]
    g_all = _all_gather8(_small_pack([small_g[n] for n in small_names]), "gather_small_grads")
    res = _adamw(_small_pack([small_w[n] for n in small_names])[None], _small_pack([small_m[n] for n in small_names])[None],
                 _small_pack([small_v[n] for n in small_names])[None], g_all, "adamw_small")
    small_out = {n: [] for n in small_names}
    for r in res:
        for n, a in zip(small_names, _small_unpack(r[0], shapes)):
            small_out[n].append(a)
    off_b = sum(int(np.prod(s)) for s in shapes[:2])
    n_mod = DEPTH * 6 * D_MODEL
    dmod_all = g_all.reshape(8, -1)[:, off_b:off_b + n_mod].reshape(8, DEPTH, 6 * D_MODEL)
    dmod_sh = lax.dynamic_slice_in_dim(dmod_all, chip * n_ada, n_ada, axis=2).transpose(1, 0, 2)
    g_ada = _ada_bwd(c_all.T, dmod_sh, "ada_bwd")
    ada_out = _adamw(w_ada, m_w_ada, v_w_ada, flat2(g_ada)[None], "adamw_w_ada")

    order = ("norm_mix_g", "norm_ffn_g", "w_ada", "b_ada", "w_in", "b_forget", "sinks", "rel_bias", "w_branch",
             "w_out", "w_ffn_in", "w_ffn_out", "final_norm_g")

    def pick(n, k):
        if n == "w_ada":
            return ada_out[k]
        if n in big_out:
            return big_out[n][k]
        return small_out[n][k]

    outs = [loss, grad_x]
    for k in range(4):
        outs += [pick(n, k) for n in order]
    return tuple(outs)
```

```python
import functools

import numpy as np
import jax
import jax.numpy as jnp
from jax import lax
from jax.experimental import pallas as pl
from jax.experimental.pallas import tpu as pltpu

F32 = jnp.float32
BF16 = jnp.bfloat16
SDS = jax.ShapeDtypeStruct

D_MODEL = 1024
DEPTH = 2
CHUNK = 64
HEAD_DIM = 64
EPS = 1e-6
NEG_INF = -1e30
N_HEADS = 8
A_KV_HEADS = 2
A_PREV = 2
C_PREV = 8
REL_CLIP = 128
N_REL = 2 * REL_CLIP + 1
N_REL_PAD = 384
BRANCH_W = 512
FFN_H = 2816
FOX_BQ = 256
FOX_BK = 512
BAND_UNROLL_FWD = 4
BAND_UNROLL_BWD = 2
QKV_COLS = 3840
GF_COLS = 3200
N_IN_COLS = 6920
LANE = 128
VMEM_LIMIT = 48 * 1024 * 1024

ADAM_LR = 0.001
ADAM_B1 = 0.9
ADAM_B2 = 0.999
ADAM_EPS = 1e-08
ADAM_WD = 0.01
ADAM_STEP = 10

MESH = pl.DeviceIdType.MESH
ANY = pl.BlockSpec(memory_space=pl.ANY)
VMEM_SPEC = pl.BlockSpec(memory_space=pltpu.VMEM)


def _cparams(sem=None):
    return pltpu.CompilerParams(dimension_semantics=sem, vmem_limit_bytes=VMEM_LIMIT)


def _blk(n, cap):
    if n <= cap:
        return n
    best = None
    for m in range(LANE, cap + 1, LANE):
        if n % m == 0:
            best = m
    assert best is not None, (n, cap)
    return best


def _sigmoid(x):
    return 1.0 / (1.0 + jnp.exp(-x))


def _mm(a, b, *, mode, out_dtype, name, groups=1, cap_m=2048, cap_n=1024, cap_k=1408, col_quarters=False,
        comm=None):
    G = groups
    assert not col_quarters or mode == "tn"
    if mode == "nn":
        M, K, N = a.shape[0], a.shape[1] // G, b.shape[1]
        assert b.shape[0] == G * K
    elif mode == "nt":
        M, K, N = a.shape[0], a.shape[1] // G, b.shape[0] // G
        assert b.shape[1] == K
    else:
        K, M, N = a.shape[0], a.shape[1] // G, b.shape[1] // G
        assert b.shape[0] == K
    bm, bn, bk = _blk(M, cap_m), _blk(N // 4 if col_quarters else N, cap_n), _blk(K, cap_k)
    nm, nn, nk = M // bm, N // bn, K // bk
    if mode == "nn":
        a_spec = pl.BlockSpec((bm, bk), lambda g, i, j, k: (i, g * nk + k))
        b_spec = pl.BlockSpec((bk, bn), lambda g, i, j, k: (g * nk + k, j))
        o_spec = pl.BlockSpec((bm, bn), lambda g, i, j, k: (i, g * nn + j))
        dims = (((1,), (0,)), ((), ()))
        out_shape = (M, G * N)
    elif mode == "nt":
        a_spec = pl.BlockSpec((bm, bk), lambda g, i, j, k: (i, g * nk + k))
        b_spec = pl.BlockSpec((bn, bk), lambda g, i, j, k: (g * nn + j, k))
        o_spec = pl.BlockSpec((bm, bn), lambda g, i, j, k: (i, g * nn + j))
        dims = (((1,), (1,)), ((), ()))
        out_shape = (M, G * N)
    else:
        a_spec = pl.BlockSpec((bk, bm), lambda g, i, j, k: (k, g * nm + i))
        b_spec = pl.BlockSpec((bk, bn), lambda g, i, j, k: (k, g * nn + j))
        dims = (((0,), (0,)), ((), ()))
        if col_quarters:
            nq = nn // 4
            o_spec = pl.BlockSpec((1, bm, bn), lambda g, i, j, k: (j // nq, g * nm + i, j % nq))
            out_shape = (4, G * M, N // 4)
        else:
            o_spec = pl.BlockSpec((bm, bn), lambda g, i, j, k: (g * nm + i, j))
            out_shape = (G * M, N)

    def product(a_ref, b_ref):
        return lax.dot_general(a_ref[...].astype(BF16), b_ref[...].astype(BF16), dims, preferred_element_type=F32)

    def body_one(a_ref, b_ref, o_ref):
        o_ref[...] = product(a_ref, b_ref).astype(o_ref.dtype).reshape(o_ref.shape)

    def body_acc(a_ref, b_ref, o_ref, acc_ref):
        k = pl.program_id(3)

        @pl.when(k == 0)
        def _():
            acc_ref[...] = jnp.zeros_like(acc_ref)

        acc_ref[...] += product(a_ref, b_ref)

        @pl.when(k == nk - 1)
        def _():
            o_ref[...] = acc_ref[...].astype(o_ref.dtype).reshape(o_ref.shape)

    res, got = _call_hosting(
        body_one if nk == 1 else body_acc, comm=comm, grid=(G, nm, nn, nk), in_specs=[a_spec, b_spec],
        out_specs=[o_spec], out_shape=[SDS(out_shape, out_dtype)],
        scratch_shapes=[] if nk == 1 else [pltpu.VMEM((bm, bn), F32)], name=name, args=(a, b),
        semantics=("parallel", "parallel", "parallel", "arbitrary"))
    return res[0] if comm is None else (res[0], got)


def _rows(tm, n, col=0):
    return pl.BlockSpec((tm, n), lambda i: (i, col))


def _vec(n):
    return pl.BlockSpec((1, n), lambda i: (0, 0))


def _tm(S):
    return min(S, 256)


def _norm_mod_fwd(x, g, sc, sh, name):
    S, Dm = x.shape
    tm = _tm(S)

    def body(x_ref, g_ref, sc_ref, sh_ref, h_ref):
        xv = x_ref[...]
        r = lax.rsqrt(jnp.mean(xv * xv, axis=-1, keepdims=True) + EPS)
        h_ref[...] = ((xv * r) * g_ref[...] * (1.0 + sc_ref[...]) + sh_ref[...]).astype(h_ref.dtype)

    return pl.pallas_call(
        body, grid=(S // tm,), in_specs=[_rows(tm, Dm), _vec(Dm), _vec(Dm), _vec(Dm)],
        out_specs=_rows(tm, Dm), out_shape=SDS((S, Dm), BF16),
        compiler_params=_cparams(("parallel",)), name=name)(x, g, sc, sh)


def _norm_mod_bwd(x, dh_list, dres, g, sc, name):
    S, Dm = x.shape
    tm = _tm(S)
    nh = len(dh_list)

    def body(*refs):
        x_ref = refs[0]
        dh_refs = refs[1:1 + nh]
        dres_ref, g_ref, sc_ref, dx_ref, dsc_ref, dsh_ref, dg_ref = refs[1 + nh:]
        i = pl.program_id(0)

        @pl.when(i == 0)
        def _():
            dsc_ref[...] = jnp.zeros_like(dsc_ref)
            dsh_ref[...] = jnp.zeros_like(dsh_ref)
            dg_ref[...] = jnp.zeros_like(dg_ref)

        xv = x_ref[...]
        dh = dh_refs[0][...]
        for r_ in dh_refs[1:]:
            dh = dh + r_[...]
        gv = g_ref[...]
        r = lax.rsqrt(jnp.mean(xv * xv, axis=-1, keepdims=True) + EPS)
        xn = xv * r
        xg = xn * gv
        dsh_ref[...] += jnp.sum(dh, axis=0, keepdims=True)
        dsc_ref[...] += jnp.sum(dh * xg, axis=0, keepdims=True)
        dxg = dh * (1.0 + sc_ref[...])
        dg_ref[...] += jnp.sum(dxg * xn, axis=0, keepdims=True)
        dxn = dxg * gv
        dx_ref[...] = dres_ref[...] + r * (dxn - xn * jnp.mean(dxn * xn, axis=-1, keepdims=True))

    return pl.pallas_call(
        body, grid=(S // tm,),
        in_specs=[_rows(tm, Dm)] * (2 + nh) + [_vec(Dm), _vec(Dm)],
        out_specs=[_rows(tm, Dm), _vec(Dm), _vec(Dm), _vec(Dm)],
        out_shape=[SDS((S, Dm), F32), SDS((1, Dm), F32), SDS((1, Dm), F32), SDS((1, Dm), F32)],
        compiler_params=_cparams(("arbitrary",)), name=name)(x, *dh_list, dres, g, sc)


def _resid_fwd(x, val, g, name):
    S, Dm = x.shape
    tm = _tm(S)

    def body(x_ref, v_ref, g_ref, o_ref):
        o_ref[...] = x_ref[...] + g_ref[...] * v_ref[...]

    return pl.pallas_call(
        body, grid=(S // tm,), in_specs=[_rows(tm, Dm), _rows(tm, Dm), _vec(Dm)],
        out_specs=_rows(tm, Dm), out_shape=SDS((S, Dm), F32),
        compiler_params=_cparams(("parallel",)), name=name)(x, val, g)


def _resid_bwd(dx, val, g, name):
    S, Dm = dx.shape
    tm = _tm(S)

    def body(dx_ref, v_ref, g_ref, dg_ref, dv_ref):
        @pl.when(pl.program_id(0) == 0)
        def _():
            dg_ref[...] = jnp.zeros_like(dg_ref)

        dxv = dx_ref[...]
        dg_ref[...] += jnp.sum(dxv * v_ref[...], axis=0, keepdims=True)
        dv_ref[...] = (dxv * g_ref[...]).astype(dv_ref.dtype)

    return pl.pallas_call(
        body, grid=(S // tm,), in_specs=[_rows(tm, Dm), _rows(tm, Dm), _vec(Dm)],
        out_specs=[_vec(Dm), _rows(tm, Dm)], out_shape=[SDS((1, Dm), F32), SDS((S, Dm), BF16)],
        compiler_params=_cparams(("arbitrary",)), name=name)(dx, val, g)


def _merge_fwd(y, gf, name):
    S = y.shape[0]
    tm = _tm(S)
    W = 3 * D_MODEL

    def body(y_ref, g_ref, o_ref):
        acc = None
        for k in range(3):
            sl = slice(k * D_MODEL, (k + 1) * D_MODEL)
            t = _sigmoid(g_ref[:, sl]) * y_ref[:, sl]
            acc = t if acc is None else acc + t
        o_ref[...] = acc.astype(o_ref.dtype)

    return pl.pallas_call(
        body, grid=(S // tm,), in_specs=[_rows(tm, W), _rows(tm, W)],
        out_specs=_rows(tm, D_MODEL), out_shape=SDS((S, D_MODEL), BF16),
        compiler_params=_cparams(("parallel",)), name=name)(y, gf)


def _merge_bwd(dm, y, gf, name):
    S = y.shape[0]
    tm = _tm(S)
    W = 3 * D_MODEL

    def body(dm_ref, y_ref, g_ref, dy_ref, dg_ref):
        dmv = dm_ref[...]
        for k in range(3):
            sl = slice(k * D_MODEL, (k + 1) * D_MODEL)
            sg = _sigmoid(g_ref[:, sl])
            dy_ref[:, sl] = (dmv * sg).astype(dy_ref.dtype)
            dg_ref[:, sl] = (dmv * y_ref[:, sl] * (sg * (1.0 - sg))).astype(dg_ref.dtype)

    return pl.pallas_call(
        body, grid=(S // tm,), in_specs=[_rows(tm, D_MODEL), _rows(tm, W), _rows(tm, W)],
        out_specs=[_rows(tm, W), _rows(tm, W)], out_shape=[SDS((S, W), BF16), SDS((S, W), BF16)],
        compiler_params=_cparams(("parallel",)), name=name)(dm, y, gf)


def _swiglu_fwd(u, name):
    S = u.shape[0]
    tm = _tm(S)

    def body(g_ref, u_ref, a_ref):
        gv = g_ref[...]
        a_ref[...] = (gv * _sigmoid(gv) * u_ref[...]).astype(a_ref.dtype)

    return pl.pallas_call(
        body, grid=(S // tm,), in_specs=[_rows(tm, FFN_H, 0), _rows(tm, FFN_H, 1)],
        out_specs=_rows(tm, FFN_H), out_shape=SDS((S, FFN_H), BF16),
        compiler_params=_cparams(("parallel",)), name=name)(u, u)


def _swiglu_bwd(da, u, name):
    S = u.shape[0]
    tm = _tm(S)

    def body(da_ref, g_ref, u_ref, du_ref):
        dav = da_ref[...]
        gv = g_ref[...]
        sg = _sigmoid(gv)
        du_ref[:, 0:FFN_H] = (dav * u_ref[...] * (sg * (1.0 + gv * (1.0 - sg)))).astype(du_ref.dtype)
        du_ref[:, FFN_H:2 * FFN_H] = (dav * (gv * sg)).astype(du_ref.dtype)

    return pl.pallas_call(
        body, grid=(S // tm,), in_specs=[_rows(tm, FFN_H), _rows(tm, FFN_H, 0), _rows(tm, FFN_H, 1)],
        out_specs=_rows(tm, 2 * FFN_H), out_shape=SDS((S, 2 * FFN_H), BF16),
        compiler_params=_cparams(("parallel",)), name=name)(da, u, u)


def _final_loss(x, g, target, name):
    S, Dm = x.shape
    tm = _tm(S)

    def body(x_ref, g_ref, t_ref, loss_ref, dx_ref, dg_ref):
        @pl.when(pl.program_id(0) == 0)
        def _():
            loss_ref[...] = jnp.zeros_like(loss_ref)
            dg_ref[...] = jnp.zeros_like(dg_ref)

        xv = x_ref[...]
        gv = g_ref[...]
        r = lax.rsqrt(jnp.mean(xv * xv, axis=-1, keepdims=True) + EPS)
        xn = xv * r
        err = xn * gv - t_ref[...]
        row = jnp.mean(err * err, axis=-1, keepdims=True)
        loss_ref[...] += 0.5 * jnp.sum(row, axis=0, keepdims=True)
        dy = err * (1.0 / Dm)
        dg_ref[...] += jnp.sum(dy * xn, axis=0, keepdims=True)
        dxn = dy * gv
        dx_ref[...] = r * (dxn - xn * jnp.mean(dxn * xn, axis=-1, keepdims=True))

    return pl.pallas_call(
        body, grid=(S // tm,), in_specs=[_rows(tm, Dm), _vec(Dm), _rows(tm, Dm)],
        out_specs=[pl.BlockSpec((1, 1), lambda i: (0, 0)), _rows(tm, Dm), _vec(Dm)],
        out_shape=[SDS((1, 1), F32), SDS((S, Dm), F32), SDS((1, Dm), F32)],
        compiler_params=_cparams(("arbitrary",)), name=name)(x, g, target)


def _band_softmax(qg, kg, bias, sink, valid):
    s = lax.dot_general(qg, kg, (((1,), (1,)), ((), ())), preferred_element_type=F32)
    s = jnp.where(valid, s + bias, NEG_INF)
    m = jnp.maximum(jnp.max(s, axis=-1, keepdims=True), sink)
    e = jnp.exp(s - m)
    es = jnp.exp(sink - m)
    l = jnp.sum(e, axis=-1, keepdims=True) + es
    return e / l, es / l


def _band_attn_fwd(q, k, v, bias, sink, *, G, P, kvoff, name):
    S = q.shape[0]
    ng = q.shape[1] // (G * HEAD_DIM)
    band = (P + 1) * CHUNK
    pad = P * CHUNK
    nc = S // CHUNK

    def body(q_ref, k_ref, v_ref, b_ref, s_ref, o_ref, kp, vp):
        kp[0:pad, :] = jnp.zeros((pad, LANE), BF16)
        vp[0:pad, :] = jnp.zeros((pad, LANE), BF16)
        kp[pad:pad + S, :] = k_ref[...]
        vp[pad:pad + S, :] = v_ref[...]
        col = lax.broadcasted_iota(jnp.int32, (CHUNK, band), 1)

        def step(n, carry):
            r = pl.multiple_of(n * CHUNK, CHUNK)
            qn = q_ref[pl.ds(r, CHUNK), :]
            kb = kp[pl.ds(r, band), :]
            vb = vp[pl.ds(r, band), :]
            valid = col >= (P - n) * CHUNK
            for g in range(G):
                ko = kvoff(g) * HEAD_DIM
                qg = qn[:, g * HEAD_DIM:(g + 1) * HEAD_DIM] * 0.125
                p, _ = _band_softmax(qg, kb[:, ko:ko + HEAD_DIM], b_ref[g], s_ref[g, 0:1, 0:1], valid)
                og = jnp.dot(p.astype(BF16), vb[:, ko:ko + HEAD_DIM], preferred_element_type=F32)
                o_ref[pl.ds(r, CHUNK), g * HEAD_DIM:(g + 1) * HEAD_DIM] = og.astype(o_ref.dtype)
            return carry

        lax.fori_loop(0, nc, step, 0, unroll=min(BAND_UNROLL_FWD, nc))

    GW = G * HEAD_DIM
    return pl.pallas_call(
        body, grid=(ng,),
        in_specs=[pl.BlockSpec((S, GW), lambda i: (0, i)), pl.BlockSpec((S, LANE), lambda i: (0, i)),
                  pl.BlockSpec((S, LANE), lambda i: (0, i)),
                  pl.BlockSpec((G, CHUNK, band), lambda i: (i, 0, 0)),
                  pl.BlockSpec((G, 8, LANE), lambda i: (i, 0, 0))],
        out_specs=pl.BlockSpec((S, GW), lambda i: (0, i)),
        out_shape=SDS((S, ng * GW), BF16),
        scratch_shapes=[pltpu.VMEM((S + pad, LANE), BF16), pltpu.VMEM((S + pad, LANE), BF16)],
        compiler_params=_cparams(("parallel",)), name=name)(q, k, v, bias, sink)


def _band_attn_bwd(q, k, v, bias, sink, do, *, G, P, kvoff, name):
    S = q.shape[0]
    ng = q.shape[1] // (G * HEAD_DIM)
    band = (P + 1) * CHUNK
    pad = P * CHUNK
    nc = S // CHUNK
    TN = (((0,), (0,)), ((), ()))

    def body(q_ref, k_ref, v_ref, b_ref, s_ref, do_ref, dq_ref, dk_ref, dv_ref, db_ref, dsk_ref,
             kp, vp, dkp, dvp):
        kp[0:pad, :] = jnp.zeros((pad, LANE), BF16)
        vp[0:pad, :] = jnp.zeros((pad, LANE), BF16)
        kp[pad:pad + S, :] = k_ref[...]
        vp[pad:pad + S, :] = v_ref[...]
        dkp[...] = jnp.zeros_like(dkp)
        dvp[...] = jnp.zeros_like(dvp)
        db_ref[...] = jnp.zeros_like(db_ref)
        col = lax.broadcasted_iota(jnp.int32, (CHUNK, band), 1)

        def step(n, dsink):
            r = pl.multiple_of(n * CHUNK, CHUNK)
            qn = q_ref[pl.ds(r, CHUNK), :]
            don = do_ref[pl.ds(r, CHUNK), :]
            kb = kp[pl.ds(r, band), :]
            vb = vp[pl.ds(r, band), :]
            valid = col >= (P - n) * CHUNK
            new = []
            for g in range(G):
                ko = kvoff(g) * HEAD_DIM
                lanes = slice(g * HEAD_DIM, (g + 1) * HEAD_DIM)
                qg = qn[:, lanes] * 0.125
                kg = kb[:, ko:ko + HEAD_DIM]
                dog = don[:, lanes]
                p, ps = _band_softmax(qg, kg, b_ref[g], s_ref[g, 0:1, 0:1], valid)
                dp = lax.dot_general(dog, vb[:, ko:ko + HEAD_DIM], (((1,), (1,)), ((), ())),
                                     preferred_element_type=F32)
                delta = jnp.sum(p * dp, axis=-1, keepdims=True)
                ds = p * (dp - delta)
                new.append(dsink[g] - jnp.sum(ps * delta, axis=0, keepdims=True))
                db_ref[g] += ds
                dsb = ds.astype(BF16)
                dq = jnp.dot(dsb, kg, preferred_element_type=F32) * 0.125
                dq_ref[pl.ds(r, CHUNK), lanes] = dq.astype(dq_ref.dtype)
                dkp[pl.ds(r, band), ko:ko + HEAD_DIM] += lax.dot_general(
                    dsb, qg, TN, preferred_element_type=F32)
                dvp[pl.ds(r, band), ko:ko + HEAD_DIM] += lax.dot_general(
                    p.astype(BF16), dog, TN, preferred_element_type=F32)
            return tuple(new)

        dsink = lax.fori_loop(0, nc, step, tuple(jnp.zeros((1, 1), F32) for _ in range(G)),
                              unroll=min(BAND_UNROLL_BWD, nc))
        for g in range(G):
            dsk_ref[g] = jnp.broadcast_to(dsink[g], (8, LANE))
        dk_ref[...] = dkp[pad:pad + S, :].astype(dk_ref.dtype)
        dv_ref[...] = dvp[pad:pad + S, :].astype(dv_ref.dtype)

    GW = G * HEAD_DIM
    qs = pl.BlockSpec((S, GW), lambda i: (0, i))
    ks = pl.BlockSpec((S, LANE), lambda i: (0, i))
    bs = pl.BlockSpec((G, CHUNK, band), lambda i: (i, 0, 0))
    ss = pl.BlockSpec((G, 8, LANE), lambda i: (i, 0, 0))
    return pl.pallas_call(
        body, grid=(ng,), in_specs=[qs, ks, ks, bs, ss, qs],
        out_specs=[qs, ks, ks, bs, ss],
        out_shape=[SDS((S, ng * GW), BF16), SDS((S, ng * LANE), BF16), SDS((S, ng * LANE), BF16),
                   SDS((ng * G, CHUNK, band), F32), SDS((ng * G, 8, LANE), F32)],
        scratch_shapes=[pltpu.VMEM((S + pad, LANE), BF16), pltpu.VMEM((S + pad, LANE), BF16),
                        pltpu.VMEM((S + pad, LANE), F32), pltpu.VMEM((S + pad, LANE), F32)],
        compiler_params=_cparams(("parallel",)), name=name)(q, k, v, bias, sink, do)


PAIR = 2 * CHUNK


def _bandT_softmax(kg, qTg, bias, sink, valid):
    s = jnp.dot(kg, qTg, preferred_element_type=F32)
    s = jnp.where(valid, s + bias, NEG_INF)
    m = jnp.maximum(jnp.max(s, axis=0, keepdims=True), sink)
    e = jnp.exp(s - m)
    es = jnp.exp(sink - m)
    inv = 1.0 / (jnp.sum(e, axis=0, keepdims=True) + es)
    return e * inv, es * inv


def _pad_copy_rows(dst, src, pad, S):
    dst[:, 0:pad, :] = jnp.zeros((dst.shape[0], pad, dst.shape[2]), dst.dtype)
    dst[:, pad:pad + S, :] = src[...]


def _pad_copy_lanes(dst, src, pad, S):
    dst[:, 0:pad] = jnp.zeros((dst.shape[0], pad), dst.dtype)
    dst[:, pad:pad + S] = src[...]


def _fm(arg):
    return arg if isinstance(arg, tuple) else (arg, 0)


def _fm_spec(rows, S, row0):
    off, rem = divmod(row0, rows)
    assert rem == 0
    return pl.BlockSpec((rows, S), lambda i: (off + i, 0))


def _bandT_fwd(qT, k_h, vT, bias, sink, *, GQ, GK, P, kvoff, name, comm=None):
    (qT, q0), (vT, v0) = _fm(qT), _fm(vT)
    S = qT.shape[1]
    ng = bias.shape[0] // GQ
    BU = (P + 2) * CHUNK
    pad = P * CHUNK
    npair = S // PAIR

    def body(qT_ref, k_ref, vT_ref, b_ref, s_ref, oT_ref, kp, vTp):
        _pad_copy_rows(kp, k_ref, pad, S)
        _pad_copy_lanes(vTp, vT_ref, pad, S)
        rowi = lax.broadcasted_iota(jnp.int32, (BU, PAIR), 0)

        def step(n2, carry):
            r = pl.multiple_of(n2 * PAIR, PAIR)
            valid = rowi >= (P - 2 * n2) * CHUNK
            for g in range(GQ):
                kv = kvoff(g)
                hs = slice(g * HEAD_DIM, (g + 1) * HEAD_DIM)
                kvs = slice(kv * HEAD_DIM, (kv + 1) * HEAD_DIM)
                qTg = qT_ref[hs, pl.ds(r, PAIR)] * 0.125
                p, _ = _bandT_softmax(kp[kv, pl.ds(r, BU), :], qTg, b_ref[g], s_ref[g, 0:1, :], valid)
                oTg = jnp.dot(vTp[kvs, pl.ds(r, BU)], p.astype(BF16), preferred_element_type=F32)
                oT_ref[hs, pl.ds(r, PAIR)] = oTg.astype(oT_ref.dtype)
            return carry

        lax.fori_loop(0, npair, step, 0, unroll=min(2, npair))

    res, got = _call_hosting(
        body, comm=comm, grid=(ng,),
        in_specs=[_fm_spec(GQ * HEAD_DIM, S, q0),
                  pl.BlockSpec((GK, S, HEAD_DIM), lambda i: (i, 0, 0)),
                  _fm_spec(GK * HEAD_DIM, S, v0),
                  pl.BlockSpec((GQ, BU, PAIR), lambda i: (i, 0, 0)),
                  pl.BlockSpec((GQ, 8, LANE), lambda i: (i, 0, 0))],
        out_specs=[pl.BlockSpec((GQ * HEAD_DIM, S), lambda i: (i, 0))],
        out_shape=[SDS((ng * GQ * HEAD_DIM, S), BF16)],
        scratch_shapes=[pltpu.VMEM((GK, S + pad, HEAD_DIM), BF16), pltpu.VMEM((GK * HEAD_DIM, S + pad), BF16)],
        name=name, args=(qT, k_h, vT, bias, sink))
    return res[0], got


def _bandT_bwd(qT, q_h, k_h, kT, v_h, doT, do_h, bias, sink, *, GQ, GK, P, kvoff, name, comm=None):
    (qT, q0), (kT, k0), (doT, d0) = _fm(qT), _fm(kT), _fm(doT)
    S = qT.shape[1]
    ng = bias.shape[0] // GQ
    BU = (P + 2) * CHUNK
    pad = P * CHUNK
    npair = S // PAIR

    def body(qT_ref, q_ref, k_ref, kT_ref, v_ref, doT_ref, do_ref, b_ref, s_ref,
             dqT_ref, dk_ref, dv_ref, db_ref, dsk_ref, kp, kTp, vp, dkp, dvp):
        _pad_copy_rows(kp, k_ref, pad, S)
        _pad_copy_rows(vp, v_ref, pad, S)
        _pad_copy_lanes(kTp, kT_ref, pad, S)
        dkp[...] = jnp.zeros_like(dkp)
        dvp[...] = jnp.zeros_like(dvp)
        db_ref[...] = jnp.zeros_like(db_ref)
        rowi = lax.broadcasted_iota(jnp.int32, (BU, PAIR), 0)

        def step(n2, dsink):
            r = pl.multiple_of(n2 * PAIR, PAIR)
            valid = rowi >= (P - 2 * n2) * CHUNK
            new = []
            for g in range(GQ):
                kv = kvoff(g)
                hs = slice(g * HEAD_DIM, (g + 1) * HEAD_DIM)
                kvs = slice(kv * HEAD_DIM, (kv + 1) * HEAD_DIM)
                qTg = qT_ref[hs, pl.ds(r, PAIR)] * 0.125
                p, ps = _bandT_softmax(kp[kv, pl.ds(r, BU), :], qTg, b_ref[g], s_ref[g, 0:1, :], valid)
                dp = jnp.dot(vp[kv, pl.ds(r, BU), :], doT_ref[hs, pl.ds(r, PAIR)], preferred_element_type=F32)
                delta = jnp.sum(p * dp, axis=0, keepdims=True)
                ds = p * (dp - delta)
                new.append(dsink[g] - ps * delta)
                db_ref[g] += ds
                dsb = ds.astype(BF16)
                dq = jnp.dot(kTp[kvs, pl.ds(r, BU)], dsb, preferred_element_type=F32) * 0.125
                dqT_ref[hs, pl.ds(r, PAIR)] = dq.astype(dqT_ref.dtype)
                dkp[kv, pl.ds(r, BU), :] += jnp.dot(dsb, q_ref[g, pl.ds(r, PAIR), :] * 0.125,
                                                    preferred_element_type=F32)
                dvp[kv, pl.ds(r, BU), :] += jnp.dot(p.astype(BF16), do_ref[g, pl.ds(r, PAIR), :],
                                                    preferred_element_type=F32)
            return tuple(new)

        dsink = lax.fori_loop(0, npair, step, tuple(jnp.zeros((1, PAIR), F32) for _ in range(GQ)))
        for g in range(GQ):
            dsk_ref[g] = jnp.broadcast_to(jnp.sum(dsink[g], axis=1, keepdims=True), (8, LANE))
        dk_ref[...] = dkp[:, pad:pad + S, :].astype(dk_ref.dtype)
        dv_ref[...] = dvp[:, pad:pad + S, :].astype(dv_ref.dtype)

    qTs = pl.BlockSpec((GQ * HEAD_DIM, S), lambda i: (i, 0))
    qhs = pl.BlockSpec((GQ, S, HEAD_DIM), lambda i: (i, 0, 0))
    khs = pl.BlockSpec((GK, S, HEAD_DIM), lambda i: (i, 0, 0))
    kTs = pl.BlockSpec((GK * HEAD_DIM, S), lambda i: (i, 0))
    bs = pl.BlockSpec((GQ, BU, PAIR), lambda i: (i, 0, 0))
    ss = pl.BlockSpec((GQ, 8, LANE), lambda i: (i, 0, 0))
    nkv = ng * GK
    return _call_hosting(
        body, comm=comm, grid=(ng,),
        in_specs=[_fm_spec(GQ * HEAD_DIM, S, q0), qhs, khs, _fm_spec(GK * HEAD_DIM, S, k0), khs,
                  _fm_spec(GQ * HEAD_DIM, S, d0), qhs, bs, ss],
        out_specs=[qTs, khs, khs, bs, ss],
        out_shape=[SDS((ng * GQ * HEAD_DIM, S), BF16), SDS((nkv, S, HEAD_DIM), BF16), SDS((nkv, S, HEAD_DIM), BF16),
                   SDS((ng * GQ, BU, PAIR), F32), SDS((ng * GQ, 8, LANE), F32)],
        scratch_shapes=[pltpu.VMEM((GK, S + pad, HEAD_DIM), BF16), pltpu.VMEM((GK * HEAD_DIM, S + pad), BF16),
                        pltpu.VMEM((GK, S + pad, HEAD_DIM), BF16),
                        pltpu.VMEM((GK, S + pad, HEAD_DIM), F32), pltpu.VMEM((GK, S + pad, HEAD_DIM), F32)],
        name=name, args=(qT, q_h, k_h, kT, v_h, doT, do_h, bias, sink))


def _pair_table(tab):
    t = jnp.transpose(tab, (0, 2, 1))
    lo = jnp.pad(t, ((0, 0), (0, CHUNK), (0, 0)), constant_values=NEG_INF)
    hi = jnp.pad(t, ((0, 0), (CHUNK, 0), (0, 0)), constant_values=NEG_INF)
    return jnp.concatenate([lo, hi], axis=2)


def _unpair_table(d):
    band = d.shape[1] - CHUNK
    return jnp.transpose(d[:, 0:band, 0:CHUNK] + d[:, CHUNK:CHUNK + band, CHUNK:PAIR], (0, 2, 1))


def _heads(a, n):
    return jnp.transpose(a.reshape(a.shape[0], n, HEAD_DIM), (1, 0, 2))


def _unheads(a):
    return jnp.transpose(a, (1, 0, 2)).reshape(a.shape[1], a.shape[0] * HEAD_DIM)


def _fox_logits(qg, kj, cq, ck, r, c, row, col):
    s = lax.dot_general(qg, kj, (((1,), (1,)), ((), ())), preferred_element_type=F32)
    s = s + cq - ck
    return jnp.where(c + col <= r + row, s, NEG_INF)


def _fox_fwd(q, k, v, cc, cr, name):
    S = q.shape[0]
    npair = q.shape[1] // LANE
    BQ, BK = min(FOX_BQ, S), min(FOX_BK, S)
    nq = S // BQ
    heads = [slice(g * HEAD_DIM, (g + 1) * HEAD_DIM) for g in range(2)]

    def body(q_ref, k_ref, v_ref, cc_ref, cr_ref, o_ref, lse_ref):
        row = lax.broadcasted_iota(jnp.int32, (BQ, BK), 0)
        col = lax.broadcasted_iota(jnp.int32, (BQ, BK), 1)

        def qstep(i, carry):
            r = pl.multiple_of(i * BQ, BQ)
            qs = [q_ref[pl.ds(r, BQ), hl] * 0.125 for hl in heads]
            cqs = [cc_ref[g, pl.ds(r, BQ), :] for g in range(2)]

            def kstep(j, st):
                c = pl.multiple_of(j * BK, BK)
                new = []
                for g, hl in enumerate(heads):
                    m, l, acc = st[g]
                    s = _fox_logits(qs[g], k_ref[pl.ds(c, BK), hl], cqs[g], cr_ref[g, :, pl.ds(c, BK)],
                                    r, c, row, col)
                    mn = jnp.maximum(m, jnp.max(s, axis=-1, keepdims=True))
                    al = jnp.exp(m - mn)
                    e = jnp.exp(s - mn)
                    l = al * l + jnp.sum(e, axis=-1, keepdims=True)
                    acc = al * acc + jnp.dot(e.astype(BF16), v_ref[pl.ds(c, BK), hl],
                                             preferred_element_type=F32)
                    new.append((mn, l, acc))
                return tuple(new)

            init = (jnp.full((BQ, 1), NEG_INF, F32), jnp.zeros((BQ, 1), F32), jnp.zeros((BQ, HEAD_DIM), F32))
            st = lax.fori_loop(0, (r + BQ + BK - 1) // BK, kstep, (init, init))
            for g, hl in enumerate(heads):
                m, l, acc = st[g]
                o_ref[pl.ds(r, BQ), hl] = (acc / l).astype(o_ref.dtype)
                lse_ref[g, pl.ds(r, BQ), :] = m + jnp.log(l)
            return carry

        lax.fori_loop(0, nq, qstep, 0)

    blk = pl.BlockSpec((S, LANE), lambda i: (0, i))
    ccs = pl.BlockSpec((2, S, 1), lambda i: (i, 0, 0))
    crs = pl.BlockSpec((2, 1, S), lambda i: (i, 0, 0))
    return pl.pallas_call(
        body, grid=(npair,), in_specs=[blk, blk, blk, ccs, crs], out_specs=[blk, ccs],
        out_shape=[SDS((S, npair * LANE), BF16), SDS((2 * npair, S, 1), F32)],
        compiler_params=_cparams(("parallel",)), name=name)(q, k, v, cc, cr)


def _fox_bwd(q, k, v, cc, cr, o, do, lse, name):
    S = q.shape[0]
    npair = q.shape[1] // LANE
    BQ, BK = min(FOX_BQ, S), min(FOX_BK, S)
    nq = S // BQ
    heads = [slice(g * HEAD_DIM, (g + 1) * HEAD_DIM) for g in range(2)]
    TN = (((0,), (0,)), ((), ()))

    def body(q_ref, k_ref, v_ref, cc_ref, cr_ref, o_ref, do_ref, lse_ref,
             dq_ref, dk_ref, dv_ref, dcr_ref, dcc_ref, dka, dva):
        dka[...] = jnp.zeros_like(dka)
        dva[...] = jnp.zeros_like(dva)
        dcr_ref[...] = jnp.zeros_like(dcr_ref)
        row = lax.broadcasted_iota(jnp.int32, (BQ, BK), 0)
        col = lax.broadcasted_iota(jnp.int32, (BQ, BK), 1)

        def qstep(i, carry):
            r = pl.multiple_of(i * BQ, BQ)
            qs = [q_ref[pl.ds(r, BQ), hl] * 0.125 for hl in heads]
            dos = [do_ref[pl.ds(r, BQ), hl] for hl in heads]
            deltas = [jnp.sum(dos[g].astype(F32) * o_ref[pl.ds(r, BQ), hl].astype(F32), axis=-1, keepdims=True)
                      for g, hl in enumerate(heads)]
            cqs = [cc_ref[g, pl.ds(r, BQ), :] for g in range(2)]
            lses = [lse_ref[g, pl.ds(r, BQ), :] for g in range(2)]

            def kstep(j, st):
                c = pl.multiple_of(j * BK, BK)
                new = []
                for g, hl in enumerate(heads):
                    dq, rs = st[g]
                    kj = k_ref[pl.ds(c, BK), hl]
                    s = _fox_logits(qs[g], kj, cqs[g], cr_ref[g, :, pl.ds(c, BK)], r, c, row, col)
                    p = jnp.exp(s - lses[g])
                    dp = lax.dot_general(dos[g], v_ref[pl.ds(c, BK), hl], (((1,), (1,)), ((), ())),
                                         preferred_element_type=F32)
                    ds = p * (dp - deltas[g])
                    dcr_ref[g, :, pl.ds(c, BK)] -= jnp.sum(ds, axis=0, keepdims=True)
                    dsb = ds.astype(BF16)
                    dka[pl.ds(c, BK), hl] += lax.dot_general(dsb, qs[g], TN, preferred_element_type=F32)
                    dva[pl.ds(c, BK), hl] += lax.dot_general(p.astype(BF16), dos[g], TN,
                                                            preferred_element_type=F32)
                    new.append((dq + jnp.dot(dsb, kj, preferred_element_type=F32),
                                rs + jnp.sum(ds, axis=-1, keepdims=True)))
                return tuple(new)

            init = (jnp.zeros((BQ, HEAD_DIM), F32), jnp.zeros((BQ, 1), F32))
            st = lax.fori_loop(0, (r + BQ + BK - 1) // BK, kstep, (init, init))
            for g, hl in enumerate(heads):
                dq_ref[pl.ds(r, BQ), hl] = (st[g][0] * 0.125).astype(dq_ref.dtype)
                dcc_ref[g, pl.ds(r, BQ), :] = st[g][1]
            return carry

        lax.fori_loop(0, nq, qstep, 0)
        dk_ref[...] = dka[...].astype(dk_ref.dtype)
        dv_ref[...] = dva[...].astype(dv_ref.dtype)

    blk = pl.BlockSpec((S, LANE), lambda i: (0, i))
    ccs = pl.BlockSpec((2, S, 1), lambda i: (i, 0, 0))
    crs = pl.BlockSpec((2, 1, S), lambda i: (i, 0, 0))
    return pl.pallas_call(
        body, grid=(npair,), in_specs=[blk, blk, blk, ccs, crs, blk, blk, ccs],
        out_specs=[blk, blk, blk, crs, ccs],
        out_shape=[SDS((S, npair * LANE), BF16)] * 3 + [SDS((2 * npair, 1, S), F32), SDS((2 * npair, S, 1), F32)],
        scratch_shapes=[pltpu.VMEM((S, LANE), F32), pltpu.VMEM((S, LANE), F32)],
        compiler_params=_cparams(("parallel",)), name=name)(q, k, v, cc, cr, o, do, lse)


def _foxT_logits(kj, qTg, cq, ck, r, c, rowi, coli):
    s = jnp.dot(kj, qTg, preferred_element_type=F32)
    s = s + cq - ck
    return jnp.where(c + rowi <= r + coli, s, NEG_INF)


def _foxT_fwd(qT, k_h, vT, ck, cq, name, comm=None):
    (qT, q0), (vT, v0) = _fm(qT), _fm(vT)
    S = qT.shape[1]
    npair = k_h.shape[0] // 2
    BQ, BK = min(FOX_BQ, S), min(FOX_BK, S)
    nq = S // BQ
    heads = [slice(g * HEAD_DIM, (g + 1) * HEAD_DIM) for g in range(2)]

    def body(qT_ref, k_ref, vT_ref, ck_ref, cq_ref, oT_ref, lse_ref):
        rowi = lax.broadcasted_iota(jnp.int32, (BK, BQ), 0)
        coli = lax.broadcasted_iota(jnp.int32, (BK, BQ), 1)

        def qstep(i, carry):
            r = pl.multiple_of(i * BQ, BQ)
            qs = [qT_ref[hs, pl.ds(r, BQ)] * 0.125 for hs in heads]
            cqs = [cq_ref[g, :, pl.ds(r, BQ)] for g in range(2)]

            def kstep(j, st):
                c = pl.multiple_of(j * BK, BK)
                new = []
                for g, hs in enumerate(heads):
                    m, l, acc = st[g]
                    s = _foxT_logits(k_ref[g, pl.ds(c, BK), :], qs[g], cqs[g], ck_ref[g, pl.ds(c, BK), :],
                                     r, c, rowi, coli)
                    mn = jnp.maximum(m, jnp.max(s, axis=0, keepdims=True))
                    al = jnp.exp(m - mn)
                    e = jnp.exp(s - mn)
                    l = al * l + jnp.sum(e, axis=0, keepdims=True)
                    acc = al * acc + jnp.dot(vT_ref[hs, pl.ds(c, BK)], e.astype(BF16), preferred_element_type=F32)
                    new.append((mn, l, acc))
                return tuple(new)

            init = (jnp.full((1, BQ), NEG_INF, F32), jnp.zeros((1, BQ), F32), jnp.zeros((HEAD_DIM, BQ), F32))
            st = lax.fori_loop(0, (r + BQ + BK - 1) // BK, kstep, (init, init))
            for g, hs in enumerate(heads):
                m, l, acc = st[g]
                oT_ref[hs, pl.ds(r, BQ)] = (acc * (1.0 / l)).astype(oT_ref.dtype)
                lse_ref[g, :, pl.ds(r, BQ)] = m + jnp.log(l)
            return carry

        lax.fori_loop(0, nq, qstep, 0)

    fT = pl.BlockSpec((LANE, S), lambda i: (i, 0))
    hm = pl.BlockSpec((2, S, HEAD_DIM), lambda i: (i, 0, 0))
    col = pl.BlockSpec((2, S, 1), lambda i: (i, 0, 0))
    rw = pl.BlockSpec((2, 1, S), lambda i: (i, 0, 0))
    return _call_hosting(
        body, comm=comm, grid=(npair,), in_specs=[_fm_spec(LANE, S, q0), hm, _fm_spec(LANE, S, v0), col, rw],
        out_specs=[fT, rw],
        out_shape=[SDS((npair * LANE, S), BF16), SDS((2 * npair, 1, S), F32)], scratch_shapes=[],
        name=name, args=(qT, k_h, vT, ck, cq))


def _foxT_bwd(qT, q_h, k_h, kT, v_h, ck, cq, oT, doT, do_h, lse, name, comm=None):
    (qT, q0), (kT, k0), (doT, d0) = _fm(qT), _fm(kT), _fm(doT)
    S = qT.shape[1]
    npair = k_h.shape[0] // 2
    BQ, BK = min(FOX_BQ, S), min(FOX_BK, S)
    nq = S // BQ
    heads = [slice(g * HEAD_DIM, (g + 1) * HEAD_DIM) for g in range(2)]

    def body(qT_ref, q_ref, k_ref, kT_ref, v_ref, ck_ref, cq_ref, oT_ref, doT_ref, do_ref, lse_ref,
             dqT_ref, dk_ref, dv_ref, dck_ref, dcq_ref, dka, dva, qa_ref):
        qa_ref[:, :, 0:HEAD_DIM] = q_ref[...] * 0.125
        qa_ref[:, :, HEAD_DIM:LANE] = jnp.ones((2, S, LANE - HEAD_DIM), BF16)
        dka[...] = jnp.zeros_like(dka)
        dva[...] = jnp.zeros_like(dva)
        rowi = lax.broadcasted_iota(jnp.int32, (BK, BQ), 0)
        coli = lax.broadcasted_iota(jnp.int32, (BK, BQ), 1)

        def qstep(i, carry):
            r = pl.multiple_of(i * BQ, BQ)
            qs = [qT_ref[hs, pl.ds(r, BQ)] * 0.125 for hs in heads]
            dos = [doT_ref[hs, pl.ds(r, BQ)] for hs in heads]
            deltas = [jnp.sum(dos[g].astype(F32) * oT_ref[hs, pl.ds(r, BQ)].astype(F32), axis=0, keepdims=True)
                      for g, hs in enumerate(heads)]
            cqs = [cq_ref[g, :, pl.ds(r, BQ)] for g in range(2)]
            lses = [lse_ref[g, :, pl.ds(r, BQ)] for g in range(2)]

            def kstep(j, st):
                c = pl.multiple_of(j * BK, BK)
                new = []
                for g, hs in enumerate(heads):
                    dq, rs = st[g]
                    s = _foxT_logits(k_ref[g, pl.ds(c, BK), :], qs[g], cqs[g], ck_ref[g, pl.ds(c, BK), :],
                                     r, c, rowi, coli)
                    p = jnp.exp(s - lses[g])
                    dp = jnp.dot(v_ref[g, pl.ds(c, BK), :], dos[g], preferred_element_type=F32)
                    ds = p * (dp - deltas[g])
                    dsb = ds.astype(BF16)
                    dka[g, pl.ds(c, BK), :] += jnp.dot(dsb, qa_ref[g, pl.ds(r, BQ), :], preferred_element_type=F32)
                    dva[g, pl.ds(c, BK), :] += jnp.dot(p.astype(BF16), do_ref[g, pl.ds(r, BQ), :],
                                                      preferred_element_type=F32)
                    new.append((dq + jnp.dot(kT_ref[hs, pl.ds(c, BK)], dsb, preferred_element_type=F32),
                                rs + jnp.sum(dsb.astype(F32), axis=0, keepdims=True)))
                return tuple(new)

            init = (jnp.zeros((HEAD_DIM, BQ), F32), jnp.zeros((1, BQ), F32))
            st = lax.fori_loop(0, (r + BQ + BK - 1) // BK, kstep, (init, init))
            for g, hs in enumerate(heads):
                dqT_ref[hs, pl.ds(r, BQ)] = (st[g][0] * 0.125).astype(dqT_ref.dtype)
                dcq_ref[g, :, pl.ds(r, BQ)] = st[g][1]
            return carry

        lax.fori_loop(0, nq, qstep, 0)
        dk_ref[...] = dka[:, :, 0:HEAD_DIM].astype(dk_ref.dtype)
        dck_ref[...] = -dka[:, :, HEAD_DIM:HEAD_DIM + 1]
        dv_ref[...] = dva[...].astype(dv_ref.dtype)

    fT = pl.BlockSpec((LANE, S), lambda i: (i, 0))
    hm = pl.BlockSpec((2, S, HEAD_DIM), lambda i: (i, 0, 0))
    hma = pl.BlockSpec((2, S, LANE), lambda i: (i, 0, 0))
    col = pl.BlockSpec((2, S, 1), lambda i: (i, 0, 0))
    rw = pl.BlockSpec((2, 1, S), lambda i: (i, 0, 0))
    nh = 2 * npair
    return _call_hosting(
        body, comm=comm, grid=(npair,),
        in_specs=[_fm_spec(LANE, S, q0), hm, hm, _fm_spec(LANE, S, k0), hm, col, rw, fT, _fm_spec(LANE, S, d0), hm, rw],
        out_specs=[fT, hm, hm, col, rw],
        out_shape=[SDS((npair * LANE, S), BF16), SDS((nh, S, HEAD_DIM), BF16), SDS((nh, S, HEAD_DIM), BF16),
                   SDS((nh, S, 1), F32), SDS((nh, 1, S), F32)],
        scratch_shapes=[pltpu.VMEM((2, S, LANE), F32), pltpu.VMEM((2, S, HEAD_DIM), F32),
                        pltpu.VMEM((2, S, LANE), BF16)],
        name=name, args=(qT, q_h, k_h, kT, v_h, ck, cq, oT, doT, do_h, lse))


def _split3(x):
    hi = x.astype(BF16)
    r1 = x - hi.astype(F32)
    mid = r1.astype(BF16)
    lo = (r1 - mid.astype(F32)).astype(BF16)
    return hi, mid, lo


def _tri_dot(tri, x):
    hi, mid, lo = _split3(x)
    return (jnp.dot(tri, hi, preferred_element_type=F32) + jnp.dot(tri, mid, preferred_element_type=F32)
            + jnp.dot(tri, lo, preferred_element_type=F32))


def _fox_cum(gf, bfo, name):
    S = gf.shape[0]
    nb = S // LANE
    fcol = (GF_COLS - LANE) // LANE

    def body(f_ref, b_ref, cum_ref):
        row = lax.broadcasted_iota(jnp.int32, (LANE, LANE), 0)
        col = lax.broadcasted_iota(jnp.int32, (LANE, LANE), 1)
        tri = jnp.where(row >= col, 1.0, 0.0).astype(BF16)
        carry = jnp.zeros((1, LANE), F32)
        for t in range(nb):
            xl = f_ref[t * LANE:(t + 1) * LANE, :] + b_ref[...]
            lf = jnp.minimum(xl, 0.0) - jnp.log(1.0 + jnp.exp(-jnp.abs(xl)))
            cblk = _tri_dot(tri, lf) + carry
            cum_ref[t * LANE:(t + 1) * LANE, :] = cblk
            carry = cblk[LANE - 1:LANE, :]

    return pl.pallas_call(
        body, grid=(1,), in_specs=[pl.BlockSpec((S, LANE), lambda i: (0, fcol)), _vec(LANE)],
        out_specs=pl.BlockSpec((S, LANE), lambda i: (0, 0)), out_shape=SDS((S, LANE), F32),
        compiler_params=_cparams(("arbitrary",)), name=name)(gf, bfo)


def _fox_cum_bwd(gf, bfo, dcum, name):
    S = gf.shape[0]
    nb = S // LANE
    fcol = (GF_COLS - LANE) // LANE

    def body(f_ref, b_ref, dc_ref, df_ref, db_ref):
        row = lax.broadcasted_iota(jnp.int32, (LANE, LANE), 0)
        col = lax.broadcasted_iota(jnp.int32, (LANE, LANE), 1)
        tri = jnp.where(row <= col, 1.0, 0.0).astype(BF16)
        carry = jnp.zeros((1, LANE), F32)
        tot = jnp.zeros((1, LANE), F32)
        for t in range(nb - 1, -1, -1):
            rows = slice(t * LANE, (t + 1) * LANE)
            dlf = _tri_dot(tri, dc_ref[rows, :]) + carry
            carry = dlf[0:1, :]
            xl = f_ref[rows, :] + b_ref[...]
            dfl = dlf * (1.0 / (1.0 + jnp.exp(xl)))
            df_ref[rows, :] = dfl.astype(df_ref.dtype)
            tot = tot + jnp.sum(dfl, axis=0, keepdims=True)
        db_ref[...] = tot

    return pl.pallas_call(
        body, grid=(1,),
        in_specs=[pl.BlockSpec((S, LANE), lambda i: (0, fcol)), _vec(LANE), pl.BlockSpec((S, LANE), lambda i: (0, 0))],
        out_specs=[pl.BlockSpec((S, LANE), lambda i: (0, 0)), _vec(LANE)],
        out_shape=[SDS((S, LANE), BF16), SDS((1, LANE), F32)],
        compiler_params=_cparams(("arbitrary",)), name=name)(gf, bfo, dcum)


REL_FAR = C_PREV * CHUNK - REL_CLIP


def _rel_onehot(qi, band):
    w = band - REL_FAR
    r = lax.broadcasted_iota(jnp.int32, (N_REL_PAD, w), 0)
    j = lax.broadcasted_iota(jnp.int32, (N_REL_PAD, w), 1) + REL_FAR
    idx = jnp.clip(C_PREV * CHUNK + qi - j, -REL_CLIP, REL_CLIP) + REL_CLIP
    return jnp.where(r == idx, 1.0, 0.0).astype(BF16)


def _rel_expand(rel, name):
    band = (C_PREV + 1) * CHUNK

    def body(rel_ref, o_ref):
        hi, mid, lo = _split3(rel_ref[...])
        far = jnp.broadcast_to(rel_ref[:, 2 * REL_CLIP:2 * REL_CLIP + 1], (N_HEADS, REL_FAR))

        def row(qi, carry):
            oh = _rel_onehot(qi, band)
            o_ref[qi, :, 0:REL_FAR] = far
            o_ref[qi, :, REL_FAR:band] = (jnp.dot(hi, oh, preferred_element_type=F32)
                                          + jnp.dot(mid, oh, preferred_element_type=F32)
                                          + jnp.dot(lo, oh, preferred_element_type=F32))
            return carry

        lax.fori_loop(0, CHUNK, row, 0, unroll=2)

    return pl.pallas_call(
        body, grid=(1,), in_specs=[pl.BlockSpec((N_HEADS, N_REL_PAD), lambda i: (0, 0))],
        out_specs=pl.BlockSpec((CHUNK, N_HEADS, band), lambda i: (0, 0, 0)),
        out_shape=SDS((CHUNK, N_HEADS, band), F32),
        compiler_params=_cparams(("arbitrary",)), name=name)(rel)


def _tri_dot_rhs(x, oh):
    hi, mid, lo = _split3(x)
    return (jnp.dot(hi, oh, preferred_element_type=F32) + jnp.dot(mid, oh, preferred_element_type=F32)
            + jnp.dot(lo, oh, preferred_element_type=F32))


def _rel_reduce(dbias, name):
    band = (C_PREV + 1) * CHUNK
    NT = (((1,), (1,)), ((), ()))

    def body(d_ref, o_ref):
        def row(qi, st):
            acc, far = st
            oh = _rel_onehot(qi, band)
            hi, mid, lo = _split3(d_ref[qi, :, REL_FAR:band])
            acc = acc + (lax.dot_general(hi, oh, NT, preferred_element_type=F32)
                         + lax.dot_general(mid, oh, NT, preferred_element_type=F32)
                         + lax.dot_general(lo, oh, NT, preferred_element_type=F32))
            return acc, far + jnp.sum(d_ref[qi, :, 0:REL_FAR], axis=-1, keepdims=True)

        acc, far = lax.fori_loop(0, CHUNK, row, (jnp.zeros((N_HEADS, N_REL_PAD), F32), jnp.zeros((N_HEADS, 1), F32)),
                                 unroll=2)
        col = lax.broadcasted_iota(jnp.int32, (N_HEADS, N_REL_PAD), 1)
        o_ref[...] = acc + jnp.where(col == 2 * REL_CLIP, far, 0.0)

    return pl.pallas_call(
        body, grid=(1,), in_specs=[pl.BlockSpec((CHUNK, N_HEADS, band), lambda i: (0, 0, 0))],
        out_specs=pl.BlockSpec((N_HEADS, N_REL_PAD), lambda i: (0, 0)),
        out_shape=SDS((N_HEADS, N_REL_PAD), F32),
        compiler_params=_cparams(("arbitrary",)), name=name)(dbias)


def _alibi_table():
    qi = np.arange(CHUNK)[:, None]
    j = np.arange((A_PREV + 1) * CHUNK)[None, :]
    dist = np.abs(A_PREV * CHUNK + qi - j).astype(np.float32)
    slopes = np.exp2(-8.0 * np.arange(1, N_HEADS + 1, dtype=np.float32) / N_HEADS).astype(np.float32)
    return jnp.asarray(-slopes[:, None, None] * dist[None])


def _ada_fwd(c_all, w, b, name):
    n = w.shape[2]

    def body(c_ref, w_ref, b_ref, o_ref):
        cv = c_ref[...]
        cond = (cv * _sigmoid(cv)).astype(BF16)
        o_ref[0] = jnp.dot(cond, w_ref[0].astype(BF16), preferred_element_type=F32) + b_ref[0]

    return pl.pallas_call(
        body, grid=(DEPTH,),
        in_specs=[pl.BlockSpec((16, D_MODEL), lambda l: (0, 0)), pl.BlockSpec((1, D_MODEL, n), lambda l: (l, 0, 0)),
                  pl.BlockSpec((1, 1, n), lambda l: (l, 0, 0))],
        out_specs=pl.BlockSpec((1, 16, n), lambda l: (l, 0, 0)), out_shape=SDS((DEPTH, 16, n), F32),
        compiler_params=_cparams(("parallel",)), name=name)(c_all, w, b)


def _ada_bwd(c_t, dmod, name):
    n = dmod.shape[2]
    bn = _blk(n, 512)
    tr = 256

    def body(c_ref, d_ref, o_ref):
        cv = c_ref[...]
        cond = (cv * _sigmoid(cv)).astype(BF16).astype(F32)
        dm = d_ref[0].astype(BF16).astype(F32)
        acc = cond[:, 0:1] * dm[0:1, :]
        for b_ in range(1, 8):
            acc = acc + cond[:, b_:b_ + 1] * dm[b_:b_ + 1, :]
        o_ref[0] = acc

    return pl.pallas_call(
        body, grid=(DEPTH, D_MODEL // tr, n // bn),
        in_specs=[pl.BlockSpec((tr, 8), lambda l, i, j: (i, 0)), pl.BlockSpec((1, 8, bn), lambda l, i, j: (l, 0, j))],
        out_specs=pl.BlockSpec((1, tr, bn), lambda l, i, j: (l, i, j)), out_shape=SDS((DEPTH, D_MODEL, n), F32),
        compiler_params=_cparams(("parallel", "parallel", "parallel")), name=name)(c_t, dmod)


def _adamw(w, m, v, parts, name):
    L, R, C = w.shape
    per_layer = isinstance(parts, (list, tuple))
    plist = list(parts) if per_layer else [parts]
    P = plist[0].shape[0]
    tr = _blk_rows(R, max(16, (1 << 18) // C))
    nr = R // tr
    c1 = 1.0 - ADAM_B1 ** ADAM_STEP
    c2 = 1.0 - ADAM_B2 ** ADAM_STEP

    def total(p_ref):
        g = p_ref[0].astype(F32)
        for k in range(1, P):
            g = g + p_ref[k].astype(F32)
        return g

    def body(w_ref, m_ref, v_ref, *rest):
        p_refs, (g_ref, d_ref, nm_ref, nv_ref) = rest[:len(plist)], rest[len(plist):]
        g = total(p_refs[0])
        for k in range(1, len(plist)):
            g = jnp.where(pl.program_id(0) == k, total(p_refs[k]), g)
        mn = ADAM_B1 * m_ref[0] + (1.0 - ADAM_B1) * g
        vn = ADAM_B2 * v_ref[0] + (1.0 - ADAM_B2) * (g * g)
        m_hat = mn / c1
        v_hat = vn / c2
        g_ref[0] = g
        nm_ref[0] = mn
        nv_ref[0] = vn
        d_ref[0] = -ADAM_LR * (m_hat / (jnp.sqrt(v_hat) + ADAM_EPS) + ADAM_WD * w_ref[0])

    rs = pl.BlockSpec((1, tr, C), lambda l, i: (l, i, 0))
    if per_layer:
        pspecs = [pl.BlockSpec((P, tr, C), functools.partial(lambda l, i, k: (0, jnp.where(l == k, i, 0), 0), k=k))
                  for k in range(L)]
    else:
        pspecs = [pl.BlockSpec((P, tr, C), lambda l, i: (0, l * nr + i, 0))]
    return pl.pallas_call(
        body, grid=(L, nr), in_specs=[rs, rs, rs] + pspecs,
        out_specs=[rs, rs, rs, rs], out_shape=[SDS((L, R, C), F32)] * 4,
        compiler_params=_cparams(("parallel", "parallel")), name=name)(w, m, v, *plist)


def _blk_rows(R, cap):
    if R <= cap:
        return R
    best = None
    for t in range(16, cap + 1, 16):
        if R % t == 0:
            best = t
    assert best is not None, (R, cap)
    return best


def _add_cast_rows(g, t, name):
    Q, R, C = g.shape
    half = R // 2
    tr = _blk_rows(half, max(16, (1 << 19) // C))
    nb = half // tr

    def body(lo_ref, hi_ref, t_ref, o_ref):
        c = lax.axis_index("c")

        @pl.when(c == 0)
        def _():
            o_ref[...] = (lo_ref[...] + t_ref[...]).astype(o_ref.dtype)

        @pl.when(c == 1)
        def _():
            o_ref[...] = (hi_ref[...] + t_ref[...]).astype(o_ref.dtype)

    bs = pl.BlockSpec((1, tr, C), lambda q, i: (q, i, 0))
    hi = pl.BlockSpec((1, tr, C), lambda q, i: (q, nb + i, 0))
    return pl.pallas_call(
        body, grid=(Q, nb), in_specs=[bs, hi, bs], out_specs=bs, out_shape=SDS((Q, half, C), BF16),
        compiler_params=_cparams(("parallel", "parallel")), name=name)(g, g, t)


def _coords():
    return lax.axis_index("x"), lax.axis_index("y"), lax.axis_index("c")


def _flip(v, bit):
    return 1 - v if bit else v


def _all_gather8(v, name):
    R = v.shape[0]

    def body(v_ref, o_ref, send_sems, recv_sems):
        x, y, c = _coords()
        me = 4 * x + 2 * y + c
        o_ref[me] = v_ref[...]
        copies = []
        for k in range(1, 8):
            peer = (_flip(x, k & 4), _flip(y, k & 2), _flip(c, k & 1))
            cp = pltpu.make_async_remote_copy(
                src_ref=v_ref, dst_ref=o_ref.at[me], send_sem=send_sems.at[k - 1], recv_sem=recv_sems.at[k - 1],
                device_id=peer, device_id_type=MESH)
            cp.start()
            copies.append(cp)
        for cp in copies:
            cp.wait_recv()
        for cp in copies:
            cp.wait_send()

    return pl.pallas_call(
        body, in_specs=[VMEM_SPEC], out_specs=VMEM_SPEC, out_shape=SDS((8, R, LANE), v.dtype),
        scratch_shapes=[pltpu.SemaphoreType.DMA((7,)), pltpu.SemaphoreType.DMA((7,))],
        compiler_params=pltpu.CompilerParams(vmem_limit_bytes=VMEM_LIMIT), name=name)(v)


def _sibling_swap_rows(arrs, name):
    n = len(arrs)

    def body(*refs):
        in_refs, out_refs = refs[:n], refs[n:2 * n]
        send_sems, recv_sems = refs[2 * n:]
        x, y, c = _coords()
        copies = []
        for a in range(n):
            Q, R = in_refs[a].shape[0], in_refs[a].shape[1]
            half = R // 2
            src = in_refs[a].at[pl.ds(0, Q), pl.ds(pl.multiple_of((1 - c) * half, 16), half)]
            cp = pltpu.make_async_remote_copy(
                src_ref=src, dst_ref=out_refs[a], send_sem=send_sems.at[a], recv_sem=recv_sems.at[a],
                device_id=(x, y, 1 - c), device_id_type=MESH)
            cp.start()
            copies.append(cp)
        for cp in copies:
            cp.wait_recv()
        for cp in copies:
            cp.wait_send()

    return pl.pallas_call(
        body, in_specs=[ANY] * n, out_specs=[ANY] * n,
        out_shape=[SDS((a.shape[0], a.shape[1] // 2, a.shape[2]), a.dtype) for a in arrs],
        scratch_shapes=[pltpu.SemaphoreType.DMA((n,)), pltpu.SemaphoreType.DMA((n,))],
        name=name)(*arrs)


def _chip_exchange(arrs, *, reduce, name):
    n = len(arrs)

    def body(*refs):
        in_refs, out_refs = refs[:n], refs[n:2 * n]
        ici_send, ici_recv, d2d_send, d2d_recv, loc_sem = refs[2 * n:]
        x, y, c = _coords()
        p = 2 * x + y
        local, first, fwd = [], [], []
        for a in range(n):
            R = out_refs[a].shape[1] // 2
            half = pl.ds(pl.multiple_of(c * R, 16), R)
            if reduce:
                lc = pltpu.make_async_copy(in_refs[a].at[p], out_refs[a].at[p, half], loc_sem.at[a])
            else:
                lc = pltpu.make_async_copy(in_refs[a], out_refs[a].at[p], loc_sem.at[a])
            lc.start()
            local.append(lc)
            for k in range(1, 4):
                qx, qy = _flip(x, k & 2), _flip(y, k & 1)
                src = in_refs[a].at[2 * qx + qy] if reduce else in_refs[a].at[half]
                cp = pltpu.make_async_remote_copy(
                    src_ref=src, dst_ref=out_refs[a].at[p, half], send_sem=ici_send.at[a, k - 1],
                    recv_sem=ici_recv.at[a, k - 1], device_id=(qx, qy, c), device_id_type=MESH)
                cp.start()
                first.append(cp)
        for a in range(n):
            R = out_refs[a].shape[1] // 2
            half = pl.ds(pl.multiple_of(c * R, 16), R)
            for k in range(0 if reduce else 1, 4):
                qx, qy = _flip(x, k & 2), _flip(y, k & 1)
                slot = out_refs[a].at[2 * qx + qy, half]
                if k == 0:
                    local[a].wait()
                else:
                    first[a * 3 + k - 1].wait_recv()
                cp = pltpu.make_async_remote_copy(
                    src_ref=slot, dst_ref=slot, send_sem=d2d_send.at[a, k], recv_sem=d2d_recv.at[a, k],
                    device_id=(x, y, 1 - c), device_id_type=MESH)
                cp.start()
                fwd.append(cp)
        for cp in fwd:
            cp.wait_recv()
        for cp in first + fwd:
            cp.wait_send()
        if not reduce:
            for lc in local:
                lc.wait()

    if reduce:
        out_shape = [SDS((4, 2 * a.shape[1], a.shape[2]), a.dtype) for a in arrs]
    else:
        out_shape = [SDS((4,) + a.shape, a.dtype) for a in arrs]
    return pl.pallas_call(
        body, in_specs=[ANY] * n, out_specs=[ANY] * n, out_shape=out_shape,
        scratch_shapes=[pltpu.SemaphoreType.DMA((n, 3)), pltpu.SemaphoreType.DMA((n, 3)),
                        pltpu.SemaphoreType.DMA((n, 4)), pltpu.SemaphoreType.DMA((n, 4)),
                        pltpu.SemaphoreType.DMA((n,))],
        name=name)(*arrs)


class _LayerExchange:
    aliased = False

    def __init__(self, srcs, lay, reduce):
        self.srcs, self.lay, self.reduce = list(srcs), lay, reduce
        self.n = len(self.srcs)
        if reduce:
            self.out_shapes = [SDS(a.shape, a.dtype) for a in self.srcs]
        else:
            self.out_shapes = [SDS((4,) + a.shape, a.dtype) for a in self.srcs]
        self.sem_shapes = [pltpu.SemaphoreType.DMA((self.n, 3)), pltpu.SemaphoreType.DMA((self.n, 3)),
                           pltpu.SemaphoreType.DMA((self.n,))]

    def _copies(self, src_refs, dst_refs, sems):
        ici_send, ici_recv, loc_sem = sems
        x, y, c = _coords()
        p = 2 * x + y
        local, remote = [], []
        for a in range(self.n):
            src_own = src_refs[a].at[p] if self.reduce else src_refs[a]
            local.append(pltpu.make_async_copy(src_own, dst_refs[a].at[p], loc_sem.at[a]))
            for k in range(1, 4):
                qx, qy = _flip(x, k & 2), _flip(y, k & 1)
                src = src_refs[a].at[2 * qx + qy] if self.reduce else src_refs[a]
                remote.append(pltpu.make_async_remote_copy(
                    src_ref=src, dst_ref=dst_refs[a].at[p], send_sem=ici_send.at[a, k - 1],
                    recv_sem=ici_recv.at[a, k - 1], device_id=(qx, qy, self.lay), device_id_type=MESH))
        return c, local, remote

    def start(self, src_refs, dst_refs, sems):
        c, local, remote = self._copies(src_refs, dst_refs, sems)
        if self.reduce:
            @pl.when(c == self.lay)
            def _():
                for cp in local + remote:
                    cp.start()
        else:
            for cp in local:
                cp.start()

            @pl.when(c == self.lay)
            def _():
                for cp in remote:
                    cp.start()

    def finish(self, src_refs, dst_refs, sems):
        c, local, remote = self._copies(src_refs, dst_refs, sems)
        if self.reduce:
            @pl.when(c == self.lay)
            def _():
                for cp in remote:
                    cp.wait_recv()
                for cp in remote:
                    cp.wait_send()
                for cp in local:
                    cp.wait()
        else:
            @pl.when(c == self.lay)
            def _():
                for cp in remote:
                    cp.wait_recv()
                for cp in remote:
                    cp.wait_send()

            for cp in local:
                cp.wait()

    def run(self, name):
        n = self.n

        def body(*refs):
            src_refs, dst_refs, sems = refs[:n], refs[n:2 * n], refs[2 * n:]
            self.start(src_refs, dst_refs, sems)
            self.finish(src_refs, dst_refs, sems)

        return pl.pallas_call(body, in_specs=[ANY] * n, out_specs=[ANY] * n, out_shape=self.out_shapes,
                              scratch_shapes=self.sem_shapes, name=name)(*self.srcs)


def _call_hosting(body, *, comm, grid, in_specs, out_specs, out_shape, scratch_shapes, name, args, semantics=None):
    n_in, n_out, n_scr = len(args), len(out_shape), len(scratch_shapes)
    if comm is None:
        sem = semantics if semantics is not None else ("parallel",) * len(grid)
        res = pl.pallas_call(body, grid=grid, in_specs=in_specs, out_specs=out_specs, out_shape=out_shape,
                             scratch_shapes=scratch_shapes, compiler_params=_cparams(sem), name=name)(*args)
        return list(res), None
    k = comm.n

    def hosted(*refs):
        ins, cin = refs[:n_in], refs[n_in:n_in + k]
        outs = refs[n_in + k:n_in + k + n_out]
        cout = refs[n_in + k + n_out:n_in + 2 * k + n_out]
        scr = refs[n_in + 2 * k + n_out:n_in + 2 * k + n_out + n_scr]
        sems = refs[n_in + 2 * k + n_out + n_scr:]
        first = pl.program_id(0) == 0
        last = pl.program_id(0) == grid[0] - 1
        for d in range(1, len(grid)):
            first = jnp.logical_and(first, pl.program_id(d) == 0)
            last = jnp.logical_and(last, pl.program_id(d) == grid[d] - 1)

        @pl.when(first)
        def _():
            comm.start(cin, cout, sems)

        body(*ins, *outs, *scr)

        @pl.when(last)
        def _():
            comm.finish(cin, cout, sems)

    aliases = {n_in + j: n_out + j for j in range(k)} if comm.aliased else {}
    res = pl.pallas_call(
        hosted, grid=grid, in_specs=list(in_specs) + [ANY] * k, out_specs=list(out_specs) + [ANY] * k,
        out_shape=list(out_shape) + comm.out_shapes, scratch_shapes=list(scratch_shapes) + comm.sem_shapes,
        input_output_aliases=aliases, compiler_params=_cparams(("arbitrary",) * len(grid)),
        name=name)(*args, *comm.srcs)
    return list(res[:n_out]), list(res[n_out:])


class _RowHalfGather:
    aliased = False

    def __init__(self, srcs):
        self.srcs, self.n = list(srcs), len(srcs)
        self.out_shapes = [SDS((4,) + a.shape, a.dtype) for a in self.srcs]
        n = self.n
        self.sem_shapes = [pltpu.SemaphoreType.DMA((n, 3)), pltpu.SemaphoreType.DMA((n, 3)),
                           pltpu.SemaphoreType.DMA((n, 3)), pltpu.SemaphoreType.DMA((n, 3)),
                           pltpu.SemaphoreType.DMA((n,))]

    def _copies(self, src_refs, dst_refs, sems):
        ici_send, ici_recv, d2d_send, d2d_recv, loc_sem = sems
        x, y, c = _coords()
        p = 2 * x + y
        local, first, fwd = [], [], []
        for a in range(self.n):
            R = src_refs[a].shape[0] // 2
            half = pl.ds(pl.multiple_of(c * R, 16), R)
            local.append(pltpu.make_async_copy(src_refs[a], dst_refs[a].at[p], loc_sem.at[a]))
            for k in range(1, 4):
                qx, qy = _flip(x, k & 2), _flip(y, k & 1)
                first.append(pltpu.make_async_remote_copy(
                    src_ref=src_refs[a].at[half], dst_ref=dst_refs[a].at[p, half], send_sem=ici_send.at[a, k - 1],
                    recv_sem=ici_recv.at[a, k - 1], device_id=(qx, qy, c), device_id_type=MESH))
                slot = dst_refs[a].at[2 * qx + qy, half]
                fwd.append(pltpu.make_async_remote_copy(
                    src_ref=slot, dst_ref=slot, send_sem=d2d_send.at[a, k - 1], recv_sem=d2d_recv.at[a, k - 1],
                    device_id=(x, y, 1 - c), device_id_type=MESH))
        return local, first, fwd

    def start(self, src_refs, dst_refs, sems):
        local, first, _ = self._copies(src_refs, dst_refs, sems)
        for cp in local + first:
            cp.start()

    def finish(self, src_refs, dst_refs, sems):
        local, first, fwd = self._copies(src_refs, dst_refs, sems)
        for got, on in zip(first, fwd):
            got.wait_recv()
            on.start()
        for cp in fwd:
            cp.wait_recv()
        for cp in first + fwd:
            cp.wait_send()
        for cp in local:
            cp.wait()

    def run(self, name):
        n = self.n

        def body(*refs):
            src_refs, dst_refs, sems = refs[:n], refs[n:2 * n], refs[2 * n:]
            self.start(src_refs, dst_refs, sems)
            self.finish(src_refs, dst_refs, sems)

        return pl.pallas_call(body, in_specs=[ANY] * n, out_specs=[ANY] * n, out_shape=self.out_shapes,
                              scratch_shapes=self.sem_shapes, name=name)(*self.srcs)


class _SiblingSend:
    aliased = False

    def __init__(self, srcs, src_core):
        self.srcs, self.src_core, self.n = list(srcs), src_core, len(srcs)
        self.out_shapes = [SDS(a.shape, a.dtype) for a in self.srcs]
        self.sem_shapes = [pltpu.SemaphoreType.DMA((self.n,)), pltpu.SemaphoreType.DMA((self.n,))]

    def _copies(self, src_refs, dst_refs, sems):
        x, y, c = _coords()
        return c, [pltpu.make_async_remote_copy(
            src_ref=src_refs[a], dst_ref=dst_refs[a], send_sem=sems[0].at[a], recv_sem=sems[1].at[a],
            device_id=(x, y, 1 - c), device_id_type=MESH) for a in range(self.n)]

    def start(self, src_refs, dst_refs, sems):
        c, copies = self._copies(src_refs, dst_refs, sems)

        @pl.when(c == self.src_core)
        def _():
            for cp in copies:
                cp.start()

    def finish(self, src_refs, dst_refs, sems):
        c, copies = self._copies(src_refs, dst_refs, sems)

        @pl.when(c == self.src_core)
        def _():
            for cp in copies:
                cp.wait_send()

        @pl.when(c != self.src_core)
        def _():
            for cp in copies:
                cp.wait_recv()


class _Handoff:
    aliased = True

    def __init__(self, srcs, lay, slots):
        self.srcs, self.lay, self.slots, self.n = list(srcs), lay, tuple(slots), len(srcs)
        self.out_shapes = [SDS(a.shape, a.dtype) for a in self.srcs]
        ns = len(self.slots)
        self.sem_shapes = [pltpu.SemaphoreType.DMA((self.n, ns)), pltpu.SemaphoreType.DMA((self.n, ns))]

    def _copies(self, dst_refs, sems):
        x, y, c = _coords()
        copies = []
        for a in range(self.n):
            for j, k in enumerate(self.slots):
                slot = dst_refs[a].at[2 * _flip(x, k & 2) + _flip(y, k & 1)]
                copies.append(pltpu.make_async_remote_copy(
                    src_ref=slot, dst_ref=slot, send_sem=sems[0].at[a, j], recv_sem=sems[1].at[a, j],
                    device_id=(x, y, 1 - c), device_id_type=MESH))
        return c, copies

    def start(self, src_refs, dst_refs, sems):
        c, copies = self._copies(dst_refs, sems)

        @pl.when(c == self.lay)
        def _():
            for cp in copies:
                cp.start()

    def finish(self, src_refs, dst_refs, sems):
        c, copies = self._copies(dst_refs, sems)

        @pl.when(c == self.lay)
        def _():
            for cp in copies:
                cp.wait_send()

        @pl.when(c != self.lay)
        def _():
            for cp in copies:
                cp.wait_recv()


def _layer_handoff(bufs, lays, slots, name):
    flat = [b for group in bufs for b in group]
    n = len(flat)
    ns = len(slots)

    def body(*refs):
        out_refs = refs[n:2 * n]
        send_sems, recv_sems = refs[2 * n:]
        x, y, c = _coords()
        i = 0
        for group, lay in zip(bufs, lays):
            copies = []
            for _b in group:
                for j, k in enumerate(slots):
                    slot = out_refs[i].at[2 * _flip(x, k & 2) + _flip(y, k & 1)]
                    copies.append(pltpu.make_async_remote_copy(
                        src_ref=slot, dst_ref=slot, send_sem=send_sems.at[i, j], recv_sem=recv_sems.at[i, j],
                        device_id=(x, y, 1 - c), device_id_type=MESH))
                i += 1

            @pl.when(c == lay)
            def _(copies=copies):
                for cp in copies:
                    cp.start()
                for cp in copies:
                    cp.wait_send()

            @pl.when(c != lay)
            def _(copies=copies):
                for cp in copies:
                    cp.wait_recv()

    return pl.pallas_call(
        body, in_specs=[ANY] * n, out_specs=[ANY] * n, out_shape=[SDS(b.shape, b.dtype) for b in flat],
        input_output_aliases={i: i for i in range(n)},
        scratch_shapes=[pltpu.SemaphoreType.DMA((n, ns)), pltpu.SemaphoreType.DMA((n, ns))], name=name)(*flat)


def _sibling_send(arrs, src_core, name):
    n = len(arrs)

    def body(*refs):
        in_refs, out_refs = refs[:n], refs[n:2 * n]
        send_sems, recv_sems = refs[2 * n:]
        x, y, c = _coords()
        copies = [pltpu.make_async_remote_copy(
            src_ref=in_refs[a], dst_ref=out_refs[a], send_sem=send_sems.at[a], recv_sem=recv_sems.at[a],
            device_id=(x, y, 1 - c), device_id_type=MESH) for a in range(n)]

        @pl.when(c == src_core)
        def _():
            for cp in copies:
                cp.start()
            for cp in copies:
                cp.wait_send()

        @pl.when(c != src_core)
        def _():
            for cp in copies:
                cp.wait_recv()

    return pl.pallas_call(
        body, in_specs=[ANY] * n, out_specs=[ANY] * n, out_shape=[SDS(a.shape, a.dtype) for a in arrs],
        scratch_shapes=[pltpu.SemaphoreType.DMA((n,)), pltpu.SemaphoreType.DMA((n,))], name=name)(*arrs)


def _add_cast_on(a, b, lay, name):
    Q, R, C = b.shape
    tr = _blk_rows(R, max(16, (1 << 19) // C))

    def body(a_ref, b_ref, o_ref):
        @pl.when(lax.axis_index("c") == lay)
        def _():
            o_ref[...] = (a_ref[...] + b_ref[...]).astype(o_ref.dtype)

    bs = pl.BlockSpec((1, tr, C), lambda q, i: (q, i, 0))
    return pl.pallas_call(
        body, grid=(Q, R // tr), in_specs=[bs, bs], out_specs=bs, out_shape=SDS((Q, R, C), BF16),
        compiler_params=_cparams(("parallel", "parallel")), name=name)(a, b)


_IN_SIZES = (512, 128, 128, 512, 512, 512, 8, 512, 512, 512, 3072)
_IN_OFF = tuple(int(v) for v in np.cumsum((0,) + _IN_SIZES))
_IN_Q = N_IN_COLS // 4


def _pack_w_in(w):
    def cols(lo, hi):
        out = []
        while lo < hi:
            q, off = divmod(lo, _IN_Q)
            n = min(hi - lo, _IN_Q - off)
            out.append(w[q, :, off:off + n])
            lo += n
        return out

    fb0, fb1, g0 = _IN_OFF[6], _IN_OFF[7], _IN_OFF[10]
    wqkv = jnp.concatenate(cols(0, fb0) + cols(fb1, g0), axis=1)
    wgf = jnp.concatenate(cols(g0, N_IN_COLS) + cols(fb0, fb1) + [jnp.zeros((w.shape[1], LANE - 8), w.dtype)], axis=1)
    return wqkv, wgf


def _unpack_w_in(dqkv, dgf):
    fb0, fb1, g0 = _IN_OFF[6], _IN_OFF[7], _IN_OFF[10]

    def cols(lo, hi):
        out = []
        while lo < hi:
            if lo < fb0:
                n = min(hi, fb0) - lo
                out.append(dqkv[:, lo:lo + n])
            elif lo < fb1:
                n = min(hi, fb1) - lo
                out.append(dgf[:, 3072 + lo - fb0:3072 + lo - fb0 + n])
            elif lo < g0:
                n = min(hi, g0) - lo
                out.append(dqkv[:, lo - 8:lo - 8 + n])
            else:
                n = hi - lo
                out.append(dgf[:, lo - g0:lo - g0 + n])
            lo += n
        return out

    return jnp.stack([jnp.concatenate(cols(q * _IN_Q, (q + 1) * _IN_Q), axis=1) for q in range(4)])


def _pad_rows(a, rows):
    return jnp.pad(a, ((0, rows - a.shape[0]), (0, 0)))


def _small_pack(parts):
    flat = jnp.concatenate([p.reshape(-1) for p in parts])
    n = flat.shape[0]
    rows = -(-n // LANE)
    rows = -(-rows // 8) * 8
    return jnp.pad(flat, (0, rows * LANE - n)).reshape(rows, LANE)


def _small_unpack(block, shapes):
    flat = block.reshape(-1)
    out, off = [], 0
    for s in shapes:
        n = int(np.prod(s))
        out.append(flat[off:off + n].reshape(s))
        off += n
    return out


def _kv_same(g):
    return 0


def _kv_own(g):
    return g


_mm_plain = _mm


def _mm_hosting(a, b, *, comm, **kw):
    if comm is None:
        return _mm(a, b, **kw), None
    return _mm(a, b, comm=comm, **kw)


def _layer_fwd(x, mod, p, l, ride):
    sh_m, sc_m, g_m, sh_f, sc_f, g_f = mod
    nm = "l%d_" % l

    def carried(name, run):
        res, got = run(ride.comm_for(name))
        if got is not None:
            ride.done(name, got)
        return res

    h1 = _norm_mod_fwd(x, p["norm_mix_g"], sc_m, sh_m, nm + "norm_mix_fwd")
    qkv = carried("proj_qkv", lambda cm: _mm_hosting(h1, p["wqkv"], mode="nn", out_dtype=BF16,
                                                     name=nm + "proj_qkv", comm=cm))
    gf = _mm(h1, p["wgf"], mode="nn", out_dtype=F32, name=nm + "proj_gf", cap_n=640)
    qkv_t = qkv.T
    o_a_t = carried("attn_a", lambda cm: _bandT_fwd(
        (qkv_t, 0), _heads(qkv[:, 512:640], A_KV_HEADS), (qkv_t, 640), p["alibi"], p["sink_tab"],
        GQ=4, GK=1, P=A_PREV, kvoff=_kv_same, name=nm + "attn_a_fwd", comm=cm))
    cum = _fox_cum(gf, p["b_forget_pad"], nm + "fox_cum")
    cum_t = cum[:, :N_HEADS].T
    cc, cr = cum_t[:, :, None], cum_t[:, None, :]
    o_b_t, lse_b = carried("attn_b", lambda cm: _foxT_fwd(
        (qkv_t, 768), _heads(qkv[:, 1280:1792], N_HEADS), (qkv_t, 1792), cc, cr, nm + "attn_b_fwd", comm=cm))
    o_c_t = carried("attn_c", lambda cm: _bandT_fwd(
        (qkv_t, 2304), _heads(qkv[:, 2816:3328], N_HEADS), (qkv_t, 3328), p["rel_tab"], p["no_sink"],
        GQ=2, GK=2, P=C_PREV, kvoff=_kv_own, name=nm + "attn_c_fwd", comm=cm))
    p = dict(p, **ride.late_weights())
    o = jnp.concatenate([o_a_t, o_b_t, o_c_t], axis=0).T
    y = _mm(o, p["wb"], mode="nn", out_dtype=F32, groups=3, name=nm + "branch")
    merged = _merge_fwd(y, gf, nm + "merge_fwd")
    mix = _mm(merged, p["wout"], mode="nn", out_dtype=F32, name=nm + "out_proj")
    x1 = _resid_fwd(x, mix, g_m, nm + "resid_mix")
    h2 = _norm_mod_fwd(x1, p["norm_ffn_g"], sc_f, sh_f, nm + "norm_ffn_fwd")
    u = carried("ffn_in", lambda cm: _mm_hosting(h2, p["wfi"], mode="nn", out_dtype=F32, name=nm + "ffn_in",
                                                 cap_n=512, comm=cm))
    a = _swiglu_fwd(u, nm + "swiglu_fwd")
    f = _mm(a, p["wfo"], mode="nn", out_dtype=F32, name=nm + "ffn_out", cap_m=1024)
    x2 = _resid_fwd(x1, f, g_f, nm + "resid_ffn")
    saved = dict(x=x, h1=h1, qkv=qkv, qkv_t=qkv_t, gf=gf, cc=cc, cr=cr, o_b_t=o_b_t, lse_b=lse_b, o=o, y=y, merged=merged,
                 mix=mix, x1=x1, h2=h2, u=u, a=a, f=f)
    return x2, saved, p


def _layer_bwd(dx2, mod, p, s, l, ride=None):
    sh_m, sc_m, g_m, sh_f, sc_f, g_f = mod
    nm = "l%d_" % l

    def _mm(a, b, *, name, **kw):
        comm = ride.comm_for(name) if ride is not None else None
        if comm is None:
            return _mm_plain(a, b, name=nm + name, **kw)
        out, got = _mm_plain(a, b, name=nm + name, comm=comm, **kw)
        ride.done(name, got)
        return out

    dg_f, df = _resid_bwd(dx2, s["f"], g_f, nm + "resid_ffn_bwd")
    da = _mm(df, p["wfo"], mode="nt", out_dtype=F32, name="ffn_out_dx", cap_m=1024, cap_n=1408)
    d_wfo = _mm(s["a"], df, mode="tn", out_dtype=F32, name="ffn_out_dw", cap_m=1408, cap_k=2048)
    du = _swiglu_bwd(da, s["u"], nm + "swiglu_bwd")
    dh2 = _mm(du, p["wfi"], mode="nt", out_dtype=F32, name="ffn_in_dx", cap_m=1024)
    d_wfi = _mm(s["h2"], du, mode="tn", out_dtype=F32, name="ffn_in_dw", cap_m=1024, cap_n=1408, cap_k=2048,
                col_quarters=True)
    dx1, dsc_f, dsh_f, dgn_f = _norm_mod_bwd(s["x1"], [dh2], dx2, p["norm_ffn_g"], sc_f, nm + "norm_ffn_bwd")
    dg_m, dmix = _resid_bwd(dx1, s["mix"], g_m, nm + "resid_mix_bwd")
    dmerged = _mm(dmix, p["wout"], mode="nt", out_dtype=F32, name="out_proj_dx")
    d_wout = _mm(s["merged"], dmix, mode="tn", out_dtype=F32, name="out_proj_dw", cap_m=1024, cap_k=2048)
    dy, dgates = _merge_bwd(dmerged, s["y"], s["gf"], nm + "merge_bwd")
    do = _mm(dy, p["wb"], mode="nt", out_dtype=BF16, groups=3, name="branch_dx")
    d_wb = _mm(s["o"], dy, mode="tn", out_dtype=F32, groups=3, name="branch_dw", cap_k=2048,
               col_quarters=True)
    comms = ride.exchanges() if ride is not None else (None, None, None)
    qkv, qkv_t = s["qkv"], s["qkv_t"]
    do_t = do.T
    (dqa_t, dka_h, dva_h, _, dsink), got_a = _bandT_bwd(
        (qkv_t, 0), _heads(qkv[:, 0:512], N_HEADS), _heads(qkv[:, 512:640], A_KV_HEADS), (qkv_t, 512),
        _heads(qkv[:, 640:768], A_KV_HEADS), (do_t, 0), _heads(do[:, 0:512], N_HEADS), p["alibi"], p["sink_tab"],
        GQ=4, GK=1, P=A_PREV, kvoff=_kv_same, name=nm + "attn_a_bwd", comm=comms[0])
    (dqb_t, dkb_h, dvb_h, dck, dcq), got_b = _foxT_bwd(
        (qkv_t, 768), _heads(qkv[:, 768:1280], N_HEADS), _heads(qkv[:, 1280:1792], N_HEADS), (qkv_t, 1280),
        _heads(qkv[:, 1792:2304], N_HEADS), s["cc"], s["cr"], s["o_b_t"], (do_t, 512),
        _heads(do[:, 512:1024], N_HEADS), s["lse_b"], nm + "attn_b_bwd", comm=comms[1])
    dcum = jnp.pad((dck[:, :, 0] + dcq[:, 0, :]).T, ((0, 0), (0, LANE - N_HEADS)))
    dfb, db_forget = _fox_cum_bwd(s["gf"], p["b_forget_pad"], dcum, nm + "fox_cum_bwd")
    (dqc_t, dkc_h, dvc_h, dbias_c, _), got_c = _bandT_bwd(
        (qkv_t, 2304), _heads(qkv[:, 2304:2816], N_HEADS), _heads(qkv[:, 2816:3328], N_HEADS), (qkv_t, 2816),
        _heads(qkv[:, 3328:3840], N_HEADS), (do_t, 1024), _heads(do[:, 1024:1536], N_HEADS), p["rel_tab"],
        p["no_sink"], GQ=2, GK=2, P=C_PREV, kvoff=_kv_own, name=nm + "attn_c_bwd", comm=comms[2])
    d_rel = _rel_reduce(jnp.transpose(_unpair_table(dbias_c), (1, 0, 2)), nm + "rel_reduce")[:, :N_REL]
    dqkv = jnp.concatenate([dqa_t.T, _unheads(dka_h), _unheads(dva_h), dqb_t.T, _unheads(dkb_h), _unheads(dvb_h),
                            dqc_t.T, _unheads(dkc_h), _unheads(dvc_h)], axis=1)
    dgf = jnp.concatenate([dgates, dfb], axis=1)
    if ride is not None:
        ride.exchanged((got_a, got_b, got_c))
    dh1a = _mm(dqkv, p["wqkv"], mode="nt", out_dtype=F32, name="proj_qkv_dx", cap_k=1024)
    dh1b = _mm(dgf, p["wgf"], mode="nt", out_dtype=F32, name="proj_gf_dx", cap_k=640)
    d_wqkv = _mm(s["h1"], dqkv, mode="tn", out_dtype=F32, name="proj_qkv_dw", cap_m=1024, cap_k=2048)
    d_wgf = _mm(s["h1"], dgf, mode="tn", out_dtype=F32, name="proj_gf_dw", cap_m=1024, cap_n=640, cap_k=2048)
    dx, dsc_m, dsh_m, dgn_m = _norm_mod_bwd(s["x"], [dh1a, dh1b], dx1, p["norm_mix_g"], sc_m, nm + "norm_mix_bwd")
    d_mod = jnp.concatenate([dsh_m, dsc_m, dg_m, dsh_f, dsc_f, dg_f], axis=1)[0]
    grads = dict(w_in=_unpack_w_in(d_wqkv, d_wgf), w_branch=d_wb, w_out=d_wout.reshape(4, -1, D_MODEL),
                 w_ffn_in=d_wfi, w_ffn_out=d_wfo.reshape(4, -1, D_MODEL),
                 norm_mix_g=dgn_m[0], norm_ffn_g=dgn_f[0], b_forget=db_forget[0, :N_HEADS],
                 sinks=dsink[:, 0, 0], rel_bias=d_rel, d_mod=d_mod)
    return dx, grads


def kernel(x, c, norm_mix_g, norm_ffn_g, w_ada, b_ada, w_in, b_forget, sinks, rel_bias, w_branch, w_out, w_ffn_in, w_ffn_out, final_norm_g, loss_target, m_norm_mix_g, m_norm_ffn_g, m_w_ada, m_b_ada, m_w_in, m_b_forget, m_sinks, m_rel_bias, m_w_branch, m_w_out, m_w_ffn_in, m_w_ffn_out, m_final_norm_g, v_norm_mix_g, v_norm_ffn_g, v_w_ada, v_b_ada, v_w_in, v_b_forget, v_sinks, v_rel_bias, v_w_branch, v_w_out, v_w_ffn_in, v_w_ffn_out, v_final_norm_g):
    xi, yi, ci = _coords()
    chip = 2 * xi + yi
    dev = 2 * chip + ci
    xs = x[0]
    S = xs.shape[0]
    n_ada = w_ada.shape[2]

    big_names = ("w_in", "w_branch", "w_out", "w_ffn_in", "w_ffn_out")
    big_w = dict(w_in=w_in, w_branch=w_branch, w_out=w_out, w_ffn_in=w_ffn_in, w_ffn_out=w_ffn_out)
    big_m = dict(w_in=m_w_in, w_branch=m_w_branch, w_out=m_w_out, w_ffn_in=m_w_ffn_in, w_ffn_out=m_w_ffn_out)
    big_v = dict(w_in=v_w_in, w_branch=v_w_branch, w_out=v_w_out, w_ffn_in=v_w_ffn_in, w_ffn_out=v_w_ffn_out)
    flat2 = lambda a: a.reshape(-1, a.shape[-1])
    shards = [[flat2(big_w[n][l]).astype(BF16) for n in big_names] for l in range(DEPTH)]
    gw = [[None] * (len(big_names) + 2) for _ in range(DEPTH)]
    for l in range(DEPTH):
        shards[l] += [shards[l][0][:D_MODEL // 2], shards[l][0][D_MODEL // 2:]]
    gw[0][0] = _RowHalfGather([shards[0][0]]).run("weights_gather_w_in_l0")[0]
    host_g = ((1, 2, 4), (0,), (3,))

    class WeightRide:
        def __init__(self, l, plan):
            self.l, self.plan = l, plan

        def comm_for(self, name):
            if name not in self.plan:
                return None
            lay, idx = self.plan[name]
            return _RowHalfGather([shards[lay][i] for i in idx])

        def done(self, name, got):
            lay, idx = self.plan[name]
            for i, r in zip(idx, got):
                gw[lay][i] = r

        def late_weights(self):
            g = gw[self.l]
            return dict(wb=jnp.transpose(g[1], (1, 0, 2)).reshape(3 * BRANCH_W, D_MODEL),
                        wout=g[2].reshape(D_MODEL, D_MODEL),
                        wfi=jnp.transpose(g[3], (1, 0, 2)).reshape(D_MODEL, 2 * FFN_H),
                        wfo=g[4].reshape(FFN_H, D_MODEL))

    weight_plan = [
        {"proj_qkv": (0, (1, 2)), "attn_a": (0, (4,)), "attn_b": (0, (3,)), "attn_c": (1, (5,)), "ffn_in": (1, (6,))},
        {"attn_a": (1, (1, 2)), "attn_b": (1, (3,)), "attn_c": (1, (4,))}]

    def hosted(arrs, split, reduce):
        return tuple(_LayerExchange([arrs[i] for i in idx], 1, reduce) for idx in split)

    def unsplit(got, split):
        out = [None] * len(big_names)
        for res, idx in zip(got, split):
            for r, i in zip(res, idx):
                out[i] = r
        return out

    c_all = _all_gather8(c.reshape(8, LANE), "gather_c").reshape(8, D_MODEL)
    b_sh = lax.dynamic_slice_in_dim(b_ada, chip * n_ada, n_ada, axis=1)[:, None, :]
    mod_sh = _ada_fwd(_pad_rows(c_all, 16), w_ada, b_sh, "ada_fwd")[:, :8, :]
    mod_all = _all_gather8(mod_sh.reshape(-1, LANE), "gather_mod").reshape(8, DEPTH, 8, n_ada)
    mod_mine = lax.dynamic_index_in_dim(mod_all[0::2], dev, axis=2, keepdims=False)
    mod = mod_mine.transpose(1, 0, 2).reshape(DEPTH, 6, D_MODEL)

    alibi = _pair_table(_alibi_table())
    no_sink = jnp.full((N_HEADS, 8, LANE), NEG_INF, F32)
    def make_params(l):
        if gw[l][0] is None:
            gw[l][0] = jnp.concatenate([gw[l][5], gw[l][6]], axis=1)
        wqkv, wgf = _pack_w_in(gw[l][0])
        rel_tab = _rel_expand(jnp.pad(rel_bias[l], ((0, 0), (0, N_REL_PAD - N_REL))), "l%d_rel_expand" % l)
        return dict(
            wqkv=wqkv, wgf=wgf, norm_mix_g=norm_mix_g[l][None], norm_ffn_g=norm_ffn_g[l][None],
            b_forget_pad=jnp.pad(b_forget[l], (0, LANE - N_HEADS))[None],
            sink_tab=jnp.broadcast_to(sinks[l][:, None, None], (N_HEADS, 8, LANE)),
            no_sink=no_sink, alibi=alibi, rel_tab=_pair_table(jnp.transpose(rel_tab, (1, 0, 2))))

    mods = [[mod[l, k][None] for k in range(6)] for l in range(DEPTH)]
    params, saved = [None] * DEPTH, [None] * DEPTH
    h = xs
    for l in range(DEPTH):
        h, saved[l], params[l] = _layer_fwd(h, mods[l], make_params(l), l, WeightRide(l, weight_plan[l]))
    loss_dev, dh, d_final = _final_loss(h, final_norm_g[None], loss_target[0], "final_loss")
    grads = [None] * DEPTH
    dh, grads[1] = _layer_bwd(dh, mods[1], params[1], saved[1], 1)

    class Layer1Ride:
        sends = {"ffn_out_dx": (4,), "ffn_out_dw": (1, 2), "ffn_in_dx": (3,), "ffn_in_dw": (0,)}
        hands = {"proj_qkv_dx": (0,), "proj_gf_dx": (3,), "proj_qkv_dw": (4,), "proj_gf_dw": (1, 2)}

        def __init__(self, g):
            self.g, self.t = g, [None] * len(g)
            self.parts, self.final = [None] * len(g), [None] * len(g)

        def comm_for(self, name):
            if name in self.sends:
                return _SiblingSend([self.g[i] for i in self.sends[name]], 0)
            if name in self.hands:
                return _Handoff([self.parts[i] for i in self.hands[name]], 1, (0, 1, 2, 3))
            return None

        def done(self, name, got):
            idx, dst = (self.sends[name], self.t) if name in self.sends else (self.hands[name], self.final)
            for i, r in zip(idx, got):
                dst[i] = r

        def exchanges(self):
            sums = [_add_cast_on(a, b, 1, "grads_chip_sum_l1_" + n) for n, a, b in zip(big_names, self.g, self.t)]
            return hosted(sums, host_g, True)

        def exchanged(self, got):
            self.parts = unsplit(got, host_g)

    ride = Layer1Ride([grads[1][n] for n in big_names])
    dh, grads[0] = _layer_bwd(dh, mods[0], params[0], saved[0], 0, ride)
    grad_x = dh[None]
    loss = lax.psum(loss_dev[0, 0], ("x", "y", "c"))
    parts1 = ride.final
    g0 = [grads[0][n] for n in big_names]
    t0 = _sibling_swap_rows(g0, "grads_swap_l0")
    sums0 = [_add_cast_rows(a, b, "grads_chip_sum_l0_" + n) for n, a, b in zip(big_names, g0, t0)]
    parts0 = _chip_exchange(sums0, reduce=True, name="grads_reduce_l0")
    big_out = {}
    for n, p0, p1 in zip(big_names, parts0, parts1):
        shp = big_w[n].shape
        as3 = lambda a: a.reshape(shp[0], -1, shp[-1])
        res = _adamw(as3(big_w[n]), as3(big_m[n]), as3(big_v[n]), [p0, p1], "adamw_" + n)
        big_out[n] = [r.reshape(shp) for r in res]

    small_names = ("norm_mix_g", "norm_ffn_g", "b_ada", "b_forget", "sinks", "rel_bias", "final_norm_g")
    small_w = dict(norm_mix_g=norm_mix_g, norm_ffn_g=norm_ffn_g, b_ada=b_ada, b_forget=b_forget, sinks=sinks,
                   rel_bias=rel_bias, final_norm_g=final_norm_g)
    small_m = dict(norm_mix_g=m_norm_mix_g, norm_ffn_g=m_norm_ffn_g, b_ada=m_b_ada, b_forget=m_b_forget,
                   sinks=m_sinks, rel_bias=m_rel_bias, final_norm_g=m_final_norm_g)
    small_v = dict(norm_mix_g=v_norm_mix_g, norm_ffn_g=v_norm_ffn_g, b_ada=v_b_ada, b_forget=v_b_forget,
                   sinks=v_sinks, rel_bias=v_rel_bias, final_norm_g=v_final_norm_g)
    small_g = dict(
        norm_mix_g=jnp.stack([grads[l]["norm_mix_g"] for l in range(DEPTH)]),
        norm_ffn_g=jnp.stack([grads[l]["norm_ffn_g"] for l in range(DEPTH)]),
        b_ada=jnp.stack([grads[l]["d_mod"] for l in range(DEPTH)]),
        b_forget=jnp.stack([grads[l]["b_forget"] for l in range(DEPTH)]),
        sinks=jnp.stack([grads[l]["sinks"] for l in range(DEPTH)]),
        rel_bias=jnp.stack([grads[l]["rel_bias"] for l in range(DEPTH)]),
        final_norm_g=d_final[0])
    shapes = [small_w[n].shape for n in small_names]
    g_all = _all_gather8(_small_pack([small_g[n] for n in small_names]), "gather_small_grads")
    res = _adamw(_small_pack([small_w[n] for n in small_names])[None], _small_pack([small_m[n] for n in small_names])[None],
                 _small_pack([small_v[n] for n in small_names])[None], g_all, "adamw_small")
    small_out = {n: [] for n in small_names}
    for r in res:
        for n, a in zip(small_names, _small_unpack(r[0], shapes)):
            small_out[n].append(a)
    off_b = sum(int(np.prod(s)) for s in shapes[:2])
    n_mod = DEPTH * 6 * D_MODEL
    dmod_all = g_all.reshape(8, -1)[:, off_b:off_b + n_mod].reshape(8, DEPTH, 6 * D_MODEL)
    dmod_sh = lax.dynamic_slice_in_dim(dmod_all, chip * n_ada, n_ada, axis=2).transpose(1, 0, 2)
    g_ada = _ada_bwd(c_all.T, dmod_sh, "ada_bwd")
    ada_out = _adamw(w_ada, m_w_ada, v_w_ada, flat2(g_ada)[None], "adamw_w_ada")

    order = ("norm_mix_g", "norm_ffn_g", "w_ada", "b_ada", "w_in", "b_forget", "sinks", "rel_bias", "w_branch",
             "w_out", "w_ffn_in", "w_ffn_out", "final_norm_g")

    def pick(n, k):
        if n == "w_ada":
            return ada_out[k]
        if n in big_out:
            return big_out[n][k]
        return small_out[n][k]

    outs = [loss, grad_x]
    for k in range(4):
        outs += [pick(n, k) for n in order]
    return tuple(outs)
```

```python
import numpy as np
import jax
import jax.numpy as jnp
from jax import lax
from jax.experimental import pallas as pl
from jax.experimental.pallas import tpu as pltpu

F32 = jnp.float32
BF16 = jnp.bfloat16
SDS = jax.ShapeDtypeStruct

D_MODEL = 1024
DEPTH = 2
CHUNK = 64
HEAD_DIM = 64
EPS = 1e-6
NEG_INF = -1e30
N_HEADS = 8
A_KV_HEADS = 2
A_PREV = 2
C_PREV = 8
REL_CLIP = 128
N_REL = 2 * REL_CLIP + 1
N_REL_PAD = 384
BRANCH_W = 512
FFN_H = 2816
FOX_BQ = 256
FOX_BK = 512
GF_COLS = 3200
N_IN_COLS = 6920
LANE = 128
VMEM_LIMIT = 48 * 1024 * 1024

ADAM_LR = 0.001
ADAM_B1 = 0.9
ADAM_B2 = 0.999
ADAM_EPS = 1e-08
ADAM_WD = 0.01
ADAM_STEP = 10

MESH = pl.DeviceIdType.MESH
ANY = pl.BlockSpec(memory_space=pl.ANY)
VMEM_SPEC = pl.BlockSpec(memory_space=pltpu.VMEM)


def _cparams(sem=None):
    return pltpu.CompilerParams(dimension_semantics=sem, vmem_limit_bytes=VMEM_LIMIT)


def _blk(n, cap):
    if n <= cap:
        return n
    best = None
    for m in range(LANE, cap + 1, LANE):
        if n % m == 0:
            best = m
    assert best is not None, (n, cap)
    return best


def _sigmoid(x):
    return 1.0 / (1.0 + jnp.exp(-x))


def _mm(a, b, *, mode, out_dtype, name, groups=1, cap_m=2048, cap_n=1024, cap_k=1408, col_quarters=False,
        comm=None):
    G = groups
    assert not col_quarters or mode == "tn"
    if mode == "nn":
        M, K, N = a.shape[0], a.shape[1] // G, b.shape[1]
        assert b.shape[0] == G * K
    elif mode == "nt":
        M, K, N = a.shape[0], a.shape[1] // G, b.shape[0] // G
        assert b.shape[1] == K
    else:
        K, M, N = a.shape[0], a.shape[1] // G, b.shape[1] // G
        assert b.shape[0] == K
    bm, bn, bk = _blk(M, cap_m), _blk(N // 4 if col_quarters else N, cap_n), _blk(K, cap_k)
    nm, nn, nk = M // bm, N // bn, K // bk
    if mode == "nn":
        a_spec = pl.BlockSpec((bm, bk), lambda g, i, j, k: (i, g * nk + k))
        b_spec = pl.BlockSpec((bk, bn), lambda g, i, j, k: (g * nk + k, j))
        o_spec = pl.BlockSpec((bm, bn), lambda g, i, j, k: (i, g * nn + j))
        dims = (((1,), (0,)), ((), ()))
        out_shape = (M, G * N)
    elif mode == "nt":
        a_spec = pl.BlockSpec((bm, bk), lambda g, i, j, k: (i, g * nk + k))
        b_spec = pl.BlockSpec((bn, bk), lambda g, i, j, k: (g * nn + j, k))
        o_spec = pl.BlockSpec((bm, bn), lambda g, i, j, k: (i, g * nn + j))
        dims = (((1,), (1,)), ((), ()))
        out_shape = (M, G * N)
    else:
        a_spec = pl.BlockSpec((bk, bm), lambda g, i, j, k: (k, g * nm + i))
        b_spec = pl.BlockSpec((bk, bn), lambda g, i, j, k: (k, g * nn + j))
        dims = (((0,), (0,)), ((), ()))
        if col_quarters:
            nq = nn // 4
            o_spec = pl.BlockSpec((1, bm, bn), lambda g, i, j, k: (j // nq, g * nm + i, j % nq))
            out_shape = (4, G * M, N // 4)
        else:
            o_spec = pl.BlockSpec((bm, bn), lambda g, i, j, k: (g * nm + i, j))
            out_shape = (G * M, N)

    def product(a_ref, b_ref):
        return lax.dot_general(a_ref[...].astype(BF16), b_ref[...].astype(BF16), dims, preferred_element_type=F32)

    def body_one(a_ref, b_ref, o_ref):
        o_ref[...] = product(a_ref, b_ref).astype(o_ref.dtype).reshape(o_ref.shape)

    def body_acc(a_ref, b_ref, o_ref, acc_ref):
        k = pl.program_id(3)

        @pl.when(k == 0)
        def _():
            acc_ref[...] = jnp.zeros_like(acc_ref)

        acc_ref[...] += product(a_ref, b_ref)

        @pl.when(k == nk - 1)
        def _():
            o_ref[...] = acc_ref[...].astype(o_ref.dtype).reshape(o_ref.shape)

    res, got = _call_hosting(
        body_one if nk == 1 else body_acc, comm=comm, grid=(G, nm, nn, nk), in_specs=[a_spec, b_spec],
        out_specs=[o_spec], out_shape=[SDS(out_shape, out_dtype)],
        scratch_shapes=[] if nk == 1 else [pltpu.VMEM((bm, bn), F32)], name=name, args=(a, b),
        semantics=("parallel", "parallel", "parallel", "arbitrary"))
    return res[0] if comm is None else (res[0], got)


def _rows(tm, n, col=0):
    return pl.BlockSpec((tm, n), lambda i: (i, col))


def _vec(n):
    return pl.BlockSpec((1, n), lambda i: (0, 0))


def _tm(S):
    return min(S, 256)


def _norm_mod_fwd(x, g, sc, sh, name):
    S, Dm = x.shape
    tm = _tm(S)

    def body(x_ref, g_ref, sc_ref, sh_ref, h_ref):
        xv = x_ref[...]
        r = lax.rsqrt(jnp.mean(xv * xv, axis=-1, keepdims=True) + EPS)
        h_ref[...] = ((xv * r) * g_ref[...] * (1.0 + sc_ref[...]) + sh_ref[...]).astype(h_ref.dtype)

    return pl.pallas_call(
        body, grid=(S // tm,), in_specs=[_rows(tm, Dm), _vec(Dm), _vec(Dm), _vec(Dm)],
        out_specs=_rows(tm, Dm), out_shape=SDS((S, Dm), BF16),
        compiler_params=_cparams(("parallel",)), name=name)(x, g, sc, sh)


def _norm_mod_bwd(x, dh_list, dres, g, sc, name):
    S, Dm = x.shape
    tm = _tm(S)
    nh = len(dh_list)

    def body(*refs):
        x_ref = refs[0]
        dh_refs = refs[1:1 + nh]
        dres_ref, g_ref, sc_ref, dx_ref, dsc_ref, dsh_ref, dg_ref = refs[1 + nh:]
        i = pl.program_id(0)

        @pl.when(i == 0)
        def _():
            dsc_ref[...] = jnp.zeros_like(dsc_ref)
            dsh_ref[...] = jnp.zeros_like(dsh_ref)
            dg_ref[...] = jnp.zeros_like(dg_ref)

        xv = x_ref[...]
        dh = dh_refs[0][...]
        for r_ in dh_refs[1:]:
            dh = dh + r_[...]
        gv = g_ref[...]
        r = lax.rsqrt(jnp.mean(xv * xv, axis=-1, keepdims=True) + EPS)
        xn = xv * r
        xg = xn * gv
        dsh_ref[...] += jnp.sum(dh, axis=0, keepdims=True)
        dsc_ref[...] += jnp.sum(dh * xg, axis=0, keepdims=True)
        dxg = dh * (1.0 + sc_ref[...])
        dg_ref[...] += jnp.sum(dxg * xn, axis=0, keepdims=True)
        dxn = dxg * gv
        dx_ref[...] = dres_ref[...] + r * (dxn - xn * jnp.mean(dxn * xn, axis=-1, keepdims=True))

    return pl.pallas_call(
        body, grid=(S // tm,),
        in_specs=[_rows(tm, Dm)] * (2 + nh) + [_vec(Dm), _vec(Dm)],
        out_specs=[_rows(tm, Dm), _vec(Dm), _vec(Dm), _vec(Dm)],
        out_shape=[SDS((S, Dm), F32), SDS((1, Dm), F32), SDS((1, Dm), F32), SDS((1, Dm), F32)],
        compiler_params=_cparams(("arbitrary",)), name=name)(x, *dh_list, dres, g, sc)


def _resid_fwd(x, val, g, name):
    S, Dm = x.shape
    tm = _tm(S)

    def body(x_ref, v_ref, g_ref, o_ref):
        o_ref[...] = x_ref[...] + g_ref[...] * v_ref[...]

    return pl.pallas_call(
        body, grid=(S // tm,), in_specs=[_rows(tm, Dm), _rows(tm, Dm), _vec(Dm)],
        out_specs=_rows(tm, Dm), out_shape=SDS((S, Dm), F32),
        compiler_params=_cparams(("parallel",)), name=name)(x, val, g)


def _resid_bwd(dx, val, g, name):
    S, Dm = dx.shape
    tm = _tm(S)

    def body(dx_ref, v_ref, g_ref, dg_ref, dv_ref):
        @pl.when(pl.program_id(0) == 0)
        def _():
            dg_ref[...] = jnp.zeros_like(dg_ref)

        dxv = dx_ref[...]
        dg_ref[...] += jnp.sum(dxv * v_ref[...], axis=0, keepdims=True)
        dv_ref[...] = (dxv * g_ref[...]).astype(dv_ref.dtype)

    return pl.pallas_call(
        body, grid=(S // tm,), in_specs=[_rows(tm, Dm), _rows(tm, Dm), _vec(Dm)],
        out_specs=[_vec(Dm), _rows(tm, Dm)], out_shape=[SDS((1, Dm), F32), SDS((S, Dm), BF16)],
        compiler_params=_cparams(("arbitrary",)), name=name)(dx, val, g)


def _merge_fwd(y, gf, name):
    S = y.shape[0]
    tm = _tm(S)
    W = 3 * D_MODEL

    def body(y_ref, g_ref, o_ref):
        acc = None
        for k in range(3):
            sl = slice(k * D_MODEL, (k + 1) * D_MODEL)
            t = _sigmoid(g_ref[:, sl]) * y_ref[:, sl]
            acc = t if acc is None else acc + t
        o_ref[...] = acc.astype(o_ref.dtype)

    return pl.pallas_call(
        body, grid=(S // tm,), in_specs=[_rows(tm, W), _rows(tm, W)],
        out_specs=_rows(tm, D_MODEL), out_shape=SDS((S, D_MODEL), BF16),
        compiler_params=_cparams(("parallel",)), name=name)(y, gf)


def _merge_bwd(dm, y, gf, name):
    S = y.shape[0]
    tm = _tm(S)
    W = 3 * D_MODEL

    def body(dm_ref, y_ref, g_ref, dy_ref, dg_ref):
        dmv = dm_ref[...]
        for k in range(3):
            sl = slice(k * D_MODEL, (k + 1) * D_MODEL)
            sg = _sigmoid(g_ref[:, sl])
            dy_ref[:, sl] = (dmv * sg).astype(dy_ref.dtype)
            dg_ref[:, sl] = (dmv * y_ref[:, sl] * (sg * (1.0 - sg))).astype(dg_ref.dtype)

    return pl.pallas_call(
        body, grid=(S // tm,), in_specs=[_rows(tm, D_MODEL), _rows(tm, W), _rows(tm, W)],
        out_specs=[_rows(tm, W), _rows(tm, W)], out_shape=[SDS((S, W), BF16), SDS((S, W), BF16)],
        compiler_params=_cparams(("parallel",)), name=name)(dm, y, gf)


def _swiglu_fwd(u, name):
    S = u.shape[0]
    tm = _tm(S)

    def body(g_ref, u_ref, a_ref):
        gv = g_ref[...]
        a_ref[...] = (gv * _sigmoid(gv) * u_ref[...]).astype(a_ref.dtype)

    return pl.pallas_call(
        body, grid=(S // tm,), in_specs=[_rows(tm, FFN_H, 0), _rows(tm, FFN_H, 1)],
        out_specs=_rows(tm, FFN_H), out_shape=SDS((S, FFN_H), BF16),
        compiler_params=_cparams(("parallel",)), name=name)(u, u)


def _swiglu_bwd(da, u, name):
    S = u.shape[0]
    tm = _tm(S)

    def body(da_ref, g_ref, u_ref, du_ref):
        dav = da_ref[...]
        gv = g_ref[...]
        sg = _sigmoid(gv)
        du_ref[:, 0:FFN_H] = (dav * u_ref[...] * (sg * (1.0 + gv * (1.0 - sg)))).astype(du_ref.dtype)
        du_ref[:, FFN_H:2 * FFN_H] = (dav * (gv * sg)).astype(du_ref.dtype)

    return pl.pallas_call(
        body, grid=(S // tm,), in_specs=[_rows(tm, FFN_H), _rows(tm, FFN_H, 0), _rows(tm, FFN_H, 1)],
        out_specs=_rows(tm, 2 * FFN_H), out_shape=SDS((S, 2 * FFN_H), BF16),
        compiler_params=_cparams(("parallel",)), name=name)(da, u, u)


def _final_loss(x, g, target, name):
    S, Dm = x.shape
    tm = _tm(S)

    def body(x_ref, g_ref, t_ref, loss_ref, dx_ref, dg_ref):
        @pl.when(pl.program_id(0) == 0)
        def _():
            loss_ref[...] = jnp.zeros_like(loss_ref)
            dg_ref[...] = jnp.zeros_like(dg_ref)

        xv = x_ref[...]
        gv = g_ref[...]
        r = lax.rsqrt(jnp.mean(xv * xv, axis=-1, keepdims=True) + EPS)
        xn = xv * r
        err = xn * gv - t_ref[...]
        row = jnp.mean(err * err, axis=-1, keepdims=True)
        loss_ref[...] += 0.5 * jnp.sum(row, axis=0, keepdims=True)
        dy = err * (1.0 / Dm)
        dg_ref[...] += jnp.sum(dy * xn, axis=0, keepdims=True)
        dxn = dy * gv
        dx_ref[...] = r * (dxn - xn * jnp.mean(dxn * xn, axis=-1, keepdims=True))

    return pl.pallas_call(
        body, grid=(S // tm,), in_specs=[_rows(tm, Dm), _vec(Dm), _rows(tm, Dm)],
        out_specs=[pl.BlockSpec((1, 1), lambda i: (0, 0)), _rows(tm, Dm), _vec(Dm)],
        out_shape=[SDS((1, 1), F32), SDS((S, Dm), F32), SDS((1, Dm), F32)],
        compiler_params=_cparams(("arbitrary",)), name=name)(x, g, target)


PAIR = 2 * CHUNK


def _bandT_softmax(kg, qTg, bias, sink, valid):
    s = jnp.dot(kg, qTg, preferred_element_type=F32)
    s = jnp.where(valid, s + bias, NEG_INF)
    m = jnp.maximum(jnp.max(s, axis=0, keepdims=True), sink)
    e = jnp.exp(s - m)
    es = jnp.exp(sink - m)
    inv = 1.0 / (jnp.sum(e, axis=0, keepdims=True) + es)
    return e * inv, es * inv


def _pad_copy_rows(dst, src, pad, S):
    dst[:, 0:pad, :] = jnp.zeros((dst.shape[0], pad, dst.shape[2]), dst.dtype)
    dst[:, pad:pad + S, :] = src[...]


def _pad_copy_lanes(dst, src, pad, S):
    dst[:, 0:pad] = jnp.zeros((dst.shape[0], pad), dst.dtype)
    dst[:, pad:pad + S] = src[...]


def _fm(arg):
    return arg if isinstance(arg, tuple) else (arg, 0)


def _fm_spec(rows, S, row0):
    off, rem = divmod(row0, rows)
    assert rem == 0
    return pl.BlockSpec((rows, S), lambda i: (off + i, 0))


def _bandT_fwd(qT, k_h, vT, bias, sink, *, GQ, GK, P, kvoff, name, comm=None):
    (qT, q0), (vT, v0) = _fm(qT), _fm(vT)
    S = qT.shape[1]
    ng = bias.shape[0] // GQ
    BU = (P + 2) * CHUNK
    pad = P * CHUNK
    npair = S // PAIR

    def body(qT_ref, k_ref, vT_ref, b_ref, s_ref, oT_ref, kp, vTp):
        _pad_copy_rows(kp, k_ref, pad, S)
        _pad_copy_lanes(vTp, vT_ref, pad, S)
        rowi = lax.broadcasted_iota(jnp.int32, (BU, PAIR), 0)

        def step(n2, carry):
            r = pl.multiple_of(n2 * PAIR, PAIR)
            valid = rowi >= (P - 2 * n2) * CHUNK
            for g in range(GQ):
                kv = kvoff(g)
                hs = slice(g * HEAD_DIM, (g + 1) * HEAD_DIM)
                kvs = slice(kv * HEAD_DIM, (kv + 1) * HEAD_DIM)
                qTg = qT_ref[hs, pl.ds(r, PAIR)] * 0.125
                p, _ = _bandT_softmax(kp[kv, pl.ds(r, BU), :], qTg, b_ref[g], s_ref[g, 0:1, :], valid)
                oTg = jnp.dot(vTp[kvs, pl.ds(r, BU)], p.astype(BF16), preferred_element_type=F32)
                oT_ref[hs, pl.ds(r, PAIR)] = oTg.astype(oT_ref.dtype)
            return carry

        lax.fori_loop(0, npair, step, 0, unroll=min(2, npair))

    res, got = _call_hosting(
        body, comm=comm, grid=(ng,),
        in_specs=[_fm_spec(GQ * HEAD_DIM, S, q0),
                  pl.BlockSpec((GK, S, HEAD_DIM), lambda i: (i, 0, 0)),
                  _fm_spec(GK * HEAD_DIM, S, v0),
                  pl.BlockSpec((GQ, BU, PAIR), lambda i: (i, 0, 0)),
                  pl.BlockSpec((GQ, 8, LANE), lambda i: (i, 0, 0))],
        out_specs=[pl.BlockSpec((GQ * HEAD_DIM, S), lambda i: (i, 0))],
        out_shape=[SDS((ng * GQ * HEAD_DIM, S), BF16)],
        scratch_shapes=[pltpu.VMEM((GK, S + pad, HEAD_DIM), BF16), pltpu.VMEM((GK * HEAD_DIM, S + pad), BF16)],
        name=name, args=(qT, k_h, vT, bias, sink))
    return res[0], got


def _bandT_bwd(qT, q_h, k_h, kT, v_h, doT, do_h, bias, sink, *, GQ, GK, P, kvoff, name, comm=None):
    (qT, q0), (kT, k0), (doT, d0) = _fm(qT), _fm(kT), _fm(doT)
    S = qT.shape[1]
    ng = bias.shape[0] // GQ
    BU = (P + 2) * CHUNK
    pad = P * CHUNK
    npair = S // PAIR

    def body(qT_ref, q_ref, k_ref, kT_ref, v_ref, doT_ref, do_ref, b_ref, s_ref,
             dqT_ref, dk_ref, dv_ref, db_ref, dsk_ref, kp, kTp, vp, dkp, dvp):
        _pad_copy_rows(kp, k_ref, pad, S)
        _pad_copy_rows(vp, v_ref, pad, S)
        _pad_copy_lanes(kTp, kT_ref, pad, S)
        dkp[...] = jnp.zeros_like(dkp)
        dvp[...] = jnp.zeros_like(dvp)
        db_ref[...] = jnp.zeros_like(db_ref)
        rowi = lax.broadcasted_iota(jnp.int32, (BU, PAIR), 0)

        def step(n2, dsink):
            r = pl.multiple_of(n2 * PAIR, PAIR)
            valid = rowi >= (P - 2 * n2) * CHUNK
            new = []
            for g in range(GQ):
                kv = kvoff(g)
                hs = slice(g * HEAD_DIM, (g + 1) * HEAD_DIM)
                kvs = slice(kv * HEAD_DIM, (kv + 1) * HEAD_DIM)
                qTg = qT_ref[hs, pl.ds(r, PAIR)] * 0.125
                p, ps = _bandT_softmax(kp[kv, pl.ds(r, BU), :], qTg, b_ref[g], s_ref[g, 0:1, :], valid)
                dp = jnp.dot(vp[kv, pl.ds(r, BU), :], doT_ref[hs, pl.ds(r, PAIR)], preferred_element_type=F32)
                delta = jnp.sum(p * dp, axis=0, keepdims=True)
                ds = p * (dp - delta)
                new.append(dsink[g] - ps * delta)
                db_ref[g] += ds
                dsb = ds.astype(BF16)
                dq = jnp.dot(kTp[kvs, pl.ds(r, BU)], dsb, preferred_element_type=F32) * 0.125
                dqT_ref[hs, pl.ds(r, PAIR)] = dq.astype(dqT_ref.dtype)
                dkp[kv, pl.ds(r, BU), :] += jnp.dot(dsb, q_ref[g, pl.ds(r, PAIR), :] * 0.125,
                                                    preferred_element_type=F32)
                dvp[kv, pl.ds(r, BU), :] += jnp.dot(p.astype(BF16), do_ref[g, pl.ds(r, PAIR), :],
                                                    preferred_element_type=F32)
            return tuple(new)

        dsink = lax.fori_loop(0, npair, step, tuple(jnp.zeros((1, PAIR), F32) for _ in range(GQ)))
        for g in range(GQ):
            dsk_ref[g] = jnp.broadcast_to(jnp.sum(dsink[g], axis=1, keepdims=True), (8, LANE))
        dk_ref[...] = dkp[:, pad:pad + S, :].astype(dk_ref.dtype)
        dv_ref[...] = dvp[:, pad:pad + S, :].astype(dv_ref.dtype)

    qTs = pl.BlockSpec((GQ * HEAD_DIM, S), lambda i: (i, 0))
    qhs = pl.BlockSpec((GQ, S, HEAD_DIM), lambda i: (i, 0, 0))
    khs = pl.BlockSpec((GK, S, HEAD_DIM), lambda i: (i, 0, 0))
    bs = pl.BlockSpec((GQ, BU, PAIR), lambda i: (i, 0, 0))
    ss = pl.BlockSpec((GQ, 8, LANE), lambda i: (i, 0, 0))
    nkv = ng * GK
    return _call_hosting(
        body, comm=comm, grid=(ng,),
        in_specs=[_fm_spec(GQ * HEAD_DIM, S, q0), qhs, khs, _fm_spec(GK * HEAD_DIM, S, k0), khs,
                  _fm_spec(GQ * HEAD_DIM, S, d0), qhs, bs, ss],
        out_specs=[qTs, khs, khs, bs, ss],
        out_shape=[SDS((ng * GQ * HEAD_DIM, S), BF16), SDS((nkv, S, HEAD_DIM), BF16), SDS((nkv, S, HEAD_DIM), BF16),
                   SDS((ng * GQ, BU, PAIR), F32), SDS((ng * GQ, 8, LANE), F32)],
        scratch_shapes=[pltpu.VMEM((GK, S + pad, HEAD_DIM), BF16), pltpu.VMEM((GK * HEAD_DIM, S + pad), BF16),
                        pltpu.VMEM((GK, S + pad, HEAD_DIM), BF16),
                        pltpu.VMEM((GK, S + pad, HEAD_DIM), F32), pltpu.VMEM((GK, S + pad, HEAD_DIM), F32)],
        name=name, args=(qT, q_h, k_h, kT, v_h, doT, do_h, bias, sink))


def _pair_table(tab):
    t = jnp.transpose(tab, (0, 2, 1))
    lo = jnp.pad(t, ((0, 0), (0, CHUNK), (0, 0)), constant_values=NEG_INF)
    hi = jnp.pad(t, ((0, 0), (CHUNK, 0), (0, 0)), constant_values=NEG_INF)
    return jnp.concatenate([lo, hi], axis=2)


def _unpair_table(d):
    band = d.shape[1] - CHUNK
    return jnp.transpose(d[:, 0:band, 0:CHUNK] + d[:, CHUNK:CHUNK + band, CHUNK:PAIR], (0, 2, 1))


def _heads(a, n):
    return jnp.transpose(a.reshape(a.shape[0], n, HEAD_DIM), (1, 0, 2))


def _unheads(a):
    return jnp.transpose(a, (1, 0, 2)).reshape(a.shape[1], a.shape[0] * HEAD_DIM)


def _foxT_logits(kj, qTg, cq, ck, r, c, rowi, coli):
    s = jnp.dot(kj, qTg, preferred_element_type=F32)
    s = s + cq - ck
    return jnp.where(c + rowi <= r + coli, s, NEG_INF)


def _foxT_fwd(qT, k_h, vT, ck, cq, name, comm=None):
    (qT, q0), (vT, v0) = _fm(qT), _fm(vT)
    S = qT.shape[1]
    npair = k_h.shape[0] // 2
    BQ, BK = min(FOX_BQ, S), min(FOX_BK, S)
    nq = S // BQ
    heads = [slice(g * HEAD_DIM, (g + 1) * HEAD_DIM) for g in range(2)]

    def body(qT_ref, k_ref, vT_ref, ck_ref, cq_ref, oT_ref, lse_ref):
        rowi = lax.broadcasted_iota(jnp.int32, (BK, BQ), 0)
        coli = lax.broadcasted_iota(jnp.int32, (BK, BQ), 1)

        def qstep(i, carry):
            r = pl.multiple_of(i * BQ, BQ)
            qs = [qT_ref[hs, pl.ds(r, BQ)] * 0.125 for hs in heads]
            cqs = [cq_ref[g, :, pl.ds(r, BQ)] for g in range(2)]

            def kstep(j, st):
                c = pl.multiple_of(j * BK, BK)
                new = []
                for g, hs in enumerate(heads):
                    m, l, acc = st[g]
                    s = _foxT_logits(k_ref[g, pl.ds(c, BK), :], qs[g], cqs[g], ck_ref[g, pl.ds(c, BK), :],
                                     r, c, rowi, coli)
                    mn = jnp.maximum(m, jnp.max(s, axis=0, keepdims=True))
                    al = jnp.exp(m - mn)
                    e = jnp.exp(s - mn)
                    l = al * l + jnp.sum(e, axis=0, keepdims=True)
                    acc = al * acc + jnp.dot(vT_ref[hs, pl.ds(c, BK)], e.astype(BF16), preferred_element_type=F32)
                    new.append((mn, l, acc))
                return tuple(new)

            init = (jnp.full((1, BQ), NEG_INF, F32), jnp.zeros((1, BQ), F32), jnp.zeros((HEAD_DIM, BQ), F32))
            st = lax.fori_loop(0, (r + BQ + BK - 1) // BK, kstep, (init, init))
            for g, hs in enumerate(heads):
                m, l, acc = st[g]
                oT_ref[hs, pl.ds(r, BQ)] = (acc * (1.0 / l)).astype(oT_ref.dtype)
                lse_ref[g, :, pl.ds(r, BQ)] = m + jnp.log(l)
            return carry

        lax.fori_loop(0, nq, qstep, 0)

    fT = pl.BlockSpec((LANE, S), lambda i: (i, 0))
    hm = pl.BlockSpec((2, S, HEAD_DIM), lambda i: (i, 0, 0))
    col = pl.BlockSpec((2, S, 1), lambda i: (i, 0, 0))
    rw = pl.BlockSpec((2, 1, S), lambda i: (i, 0, 0))
    return _call_hosting(
        body, comm=comm, grid=(npair,), in_specs=[_fm_spec(LANE, S, q0), hm, _fm_spec(LANE, S, v0), col, rw],
        out_specs=[fT, rw],
        out_shape=[SDS((npair * LANE, S), BF16), SDS((2 * npair, 1, S), F32)], scratch_shapes=[],
        name=name, args=(qT, k_h, vT, ck, cq))


def _foxT_bwd(qT, q_h, k_h, kT, v_h, ck, cq, oT, doT, do_h, lse, name, comm=None):
    (qT, q0), (kT, k0), (doT, d0) = _fm(qT), _fm(kT), _fm(doT)
    S = qT.shape[1]
    npair = k_h.shape[0] // 2
    BQ, BK = min(FOX_BQ, S), min(FOX_BK, S)
    nq = S // BQ
    heads = [slice(g * HEAD_DIM, (g + 1) * HEAD_DIM) for g in range(2)]

    def body(qT_ref, q_ref, k_ref, kT_ref, v_ref, ck_ref, cq_ref, oT_ref, doT_ref, do_ref, lse_ref,
             dqT_ref, dk_ref, dv_ref, dck_ref, dcq_ref, dka, dva, qa_ref):
        qa_ref[:, :, 0:HEAD_DIM] = q_ref[...] * 0.125
        qa_ref[:, :, HEAD_DIM:LANE] = jnp.ones((2, S, LANE - HEAD_DIM), BF16)
        dka[...] = jnp.zeros_like(dka)
        dva[...] = jnp.zeros_like(dva)
        rowi = lax.broadcasted_iota(jnp.int32, (BK, BQ), 0)
        coli = lax.broadcasted_iota(jnp.int32, (BK, BQ), 1)

        def qstep(i, carry):
            r = pl.multiple_of(i * BQ, BQ)
            qs = [qT_ref[hs, pl.ds(r, BQ)] * 0.125 for hs in heads]
            dos = [doT_ref[hs, pl.ds(r, BQ)] for hs in heads]
            deltas = [jnp.sum(dos[g].astype(F32) * oT_ref[hs, pl.ds(r, BQ)].astype(F32), axis=0, keepdims=True)
                      for g, hs in enumerate(heads)]
            cqs = [cq_ref[g, :, pl.ds(r, BQ)] for g in range(2)]
            lses = [lse_ref[g, :, pl.ds(r, BQ)] for g in range(2)]

            def kstep(j, st):
                c = pl.multiple_of(j * BK, BK)
                new = []
                for g, hs in enumerate(heads):
                    dq, rs = st[g]
                    s = _foxT_logits(k_ref[g, pl.ds(c, BK), :], qs[g], cqs[g], ck_ref[g, pl.ds(c, BK), :],
                                     r, c, rowi, coli)
                    p = jnp.exp(s - lses[g])
                    dp = jnp.dot(v_ref[g, pl.ds(c, BK), :], dos[g], preferred_element_type=F32)
                    ds = p * (dp - deltas[g])
                    dsb = ds.astype(BF16)
                    dka[g, pl.ds(c, BK), :] += jnp.dot(dsb, qa_ref[g, pl.ds(r, BQ), :], preferred_element_type=F32)
                    dva[g, pl.ds(c, BK), :] += jnp.dot(p.astype(BF16), do_ref[g, pl.ds(r, BQ), :],
                                                      preferred_element_type=F32)
                    new.append((dq + jnp.dot(kT_ref[hs, pl.ds(c, BK)], dsb, preferred_element_type=F32),
                                rs + jnp.sum(dsb.astype(F32), axis=0, keepdims=True)))
                return tuple(new)

            init = (jnp.zeros((HEAD_DIM, BQ), F32), jnp.zeros((1, BQ), F32))
            st = lax.fori_loop(0, (r + BQ + BK - 1) // BK, kstep, (init, init))
            for g, hs in enumerate(heads):
                dqT_ref[hs, pl.ds(r, BQ)] = (st[g][0] * 0.125).astype(dqT_ref.dtype)
                dcq_ref[g, :, pl.ds(r, BQ)] = st[g][1]
            return carry

        lax.fori_loop(0, nq, qstep, 0)
        dk_ref[...] = dka[:, :, 0:HEAD_DIM].astype(dk_ref.dtype)
        dck_ref[...] = -dka[:, :, HEAD_DIM:HEAD_DIM + 1]
        dv_ref[...] = dva[...].astype(dv_ref.dtype)

    fT = pl.BlockSpec((LANE, S), lambda i: (i, 0))
    hm = pl.BlockSpec((2, S, HEAD_DIM), lambda i: (i, 0, 0))
    col = pl.BlockSpec((2, S, 1), lambda i: (i, 0, 0))
    rw = pl.BlockSpec((2, 1, S), lambda i: (i, 0, 0))
    nh = 2 * npair
    return _call_hosting(
        body, comm=comm, grid=(npair,),
        in_specs=[_fm_spec(LANE, S, q0), hm, hm, _fm_spec(LANE, S, k0), hm, col, rw, fT, _fm_spec(LANE, S, d0), hm, rw],
        out_specs=[fT, hm, hm, col, rw],
        out_shape=[SDS((npair * LANE, S), BF16), SDS((nh, S, HEAD_DIM), BF16), SDS((nh, S, HEAD_DIM), BF16),
                   SDS((nh, S, 1), F32), SDS((nh, 1, S), F32)],
        scratch_shapes=[pltpu.VMEM((2, S, LANE), F32), pltpu.VMEM((2, S, HEAD_DIM), F32),
                        pltpu.VMEM((2, S, LANE), BF16)],
        name=name, args=(qT, q_h, k_h, kT, v_h, ck, cq, oT, doT, do_h, lse))


def _split3(x):
    hi = x.astype(BF16)
    r1 = x - hi.astype(F32)
    mid = r1.astype(BF16)
    lo = (r1 - mid.astype(F32)).astype(BF16)
    return hi, mid, lo


def _tri_dot(tri, x):
    hi, mid, lo = _split3(x)
    return (jnp.dot(tri, hi, preferred_element_type=F32) + jnp.dot(tri, mid, preferred_element_type=F32)
            + jnp.dot(tri, lo, preferred_element_type=F32))


def _fox_cum(gf, bfo, name):
    S = gf.shape[0]
    nb = S // LANE
    fcol = (GF_COLS - LANE) // LANE

    def body(f_ref, b_ref, cum_ref):
        row = lax.broadcasted_iota(jnp.int32, (LANE, LANE), 0)
        col = lax.broadcasted_iota(jnp.int32, (LANE, LANE), 1)
        tri = jnp.where(row >= col, 1.0, 0.0).astype(BF16)
        carry = jnp.zeros((1, LANE), F32)
        for t in range(nb):
            xl = f_ref[t * LANE:(t + 1) * LANE, :] + b_ref[...]
            lf = jnp.minimum(xl, 0.0) - jnp.log(1.0 + jnp.exp(-jnp.abs(xl)))
            cblk = _tri_dot(tri, lf) + carry
            cum_ref[t * LANE:(t + 1) * LANE, :] = cblk
            carry = cblk[LANE - 1:LANE, :]

    return pl.pallas_call(
        body, grid=(1,), in_specs=[pl.BlockSpec((S, LANE), lambda i: (0, fcol)), _vec(LANE)],
        out_specs=pl.BlockSpec((S, LANE), lambda i: (0, 0)), out_shape=SDS((S, LANE), F32),
        compiler_params=_cparams(("arbitrary",)), name=name)(gf, bfo)


def _fox_cum_bwd(gf, bfo, dcum, name):
    S = gf.shape[0]
    nb = S // LANE
    fcol = (GF_COLS - LANE) // LANE

    def body(f_ref, b_ref, dc_ref, df_ref, db_ref):
        row = lax.broadcasted_iota(jnp.int32, (LANE, LANE), 0)
        col = lax.broadcasted_iota(jnp.int32, (LANE, LANE), 1)
        tri = jnp.where(row <= col, 1.0, 0.0).astype(BF16)
        carry = jnp.zeros((1, LANE), F32)
        tot = jnp.zeros((1, LANE), F32)
        for t in range(nb - 1, -1, -1):
            rows = slice(t * LANE, (t + 1) * LANE)
            dlf = _tri_dot(tri, dc_ref[rows, :]) + carry
            carry = dlf[0:1, :]
            xl = f_ref[rows, :] + b_ref[...]
            dfl = dlf * (1.0 / (1.0 + jnp.exp(xl)))
            df_ref[rows, :] = dfl.astype(df_ref.dtype)
            tot = tot + jnp.sum(dfl, axis=0, keepdims=True)
        db_ref[...] = tot

    return pl.pallas_call(
        body, grid=(1,),
        in_specs=[pl.BlockSpec((S, LANE), lambda i: (0, fcol)), _vec(LANE), pl.BlockSpec((S, LANE), lambda i: (0, 0))],
        out_specs=[pl.BlockSpec((S, LANE), lambda i: (0, 0)), _vec(LANE)],
        out_shape=[SDS((S, LANE), BF16), SDS((1, LANE), F32)],
        compiler_params=_cparams(("arbitrary",)), name=name)(gf, bfo, dcum)


REL_FAR = C_PREV * CHUNK - REL_CLIP


def _rel_onehot(qi, band):
    w = band - REL_FAR
    r = lax.broadcasted_iota(jnp.int32, (N_REL_PAD, w), 0)
    j = lax.broadcasted_iota(jnp.int32, (N_REL_PAD, w), 1) + REL_FAR
    idx = jnp.clip(C_PREV * CHUNK + qi - j, -REL_CLIP, REL_CLIP) + REL_CLIP
    return jnp.where(r == idx, 1.0, 0.0).astype(BF16)


def _rel_expand(rel, name):
    band = (C_PREV + 1) * CHUNK

    def body(rel_ref, o_ref):
        hi, mid, lo = _split3(rel_ref[...])
        far = jnp.broadcast_to(rel_ref[:, 2 * REL_CLIP:2 * REL_CLIP + 1], (N_HEADS, REL_FAR))

        def row(qi, carry):
            oh = _rel_onehot(qi, band)
            o_ref[qi, :, 0:REL_FAR] = far
            o_ref[qi, :, REL_FAR:band] = (jnp.dot(hi, oh, preferred_element_type=F32)
                                          + jnp.dot(mid, oh, preferred_element_type=F32)
                                          + jnp.dot(lo, oh, preferred_element_type=F32))
            return carry

        lax.fori_loop(0, CHUNK, row, 0, unroll=2)

    return pl.pallas_call(
        body, grid=(1,), in_specs=[pl.BlockSpec((N_HEADS, N_REL_PAD), lambda i: (0, 0))],
        out_specs=pl.BlockSpec((CHUNK, N_HEADS, band), lambda i: (0, 0, 0)),
        out_shape=SDS((CHUNK, N_HEADS, band), F32),
        compiler_params=_cparams(("arbitrary",)), name=name)(rel)


def _rel_reduce(dbias, name):
    band = (C_PREV + 1) * CHUNK
    NT = (((1,), (1,)), ((), ()))

    def body(d_ref, o_ref):
        def row(qi, st):
            acc, far = st
            oh = _rel_onehot(qi, band)
            hi, mid, lo = _split3(d_ref[qi, :, REL_FAR:band])
            acc = acc + (lax.dot_general(hi, oh, NT, preferred_element_type=F32)
                         + lax.dot_general(mid, oh, NT, preferred_element_type=F32)
                         + lax.dot_general(lo, oh, NT, preferred_element_type=F32))
            return acc, far + jnp.sum(d_ref[qi, :, 0:REL_FAR], axis=-1, keepdims=True)

        acc, far = lax.fori_loop(0, CHUNK, row, (jnp.zeros((N_HEADS, N_REL_PAD), F32), jnp.zeros((N_HEADS, 1), F32)),
                                 unroll=2)
        col = lax.broadcasted_iota(jnp.int32, (N_HEADS, N_REL_PAD), 1)
        o_ref[...] = acc + jnp.where(col == 2 * REL_CLIP, far, 0.0)

    return pl.pallas_call(
        body, grid=(1,), in_specs=[pl.BlockSpec((CHUNK, N_HEADS, band), lambda i: (0, 0, 0))],
        out_specs=pl.BlockSpec((N_HEADS, N_REL_PAD), lambda i: (0, 0)),
        out_shape=SDS((N_HEADS, N_REL_PAD), F32),
        compiler_params=_cparams(("arbitrary",)), name=name)(dbias)


def _alibi_table():
    qi = np.arange(CHUNK)[:, None]
    j = np.arange((A_PREV + 1) * CHUNK)[None, :]
    dist = np.abs(A_PREV * CHUNK + qi - j).astype(np.float32)
    slopes = np.exp2(-8.0 * np.arange(1, N_HEADS + 1, dtype=np.float32) / N_HEADS).astype(np.float32)
    return jnp.asarray(-slopes[:, None, None] * dist[None])


def _ada_fwd(c_all, w, b, name):
    n = w.shape[2]

    def body(c_ref, w_ref, b_ref, o_ref):
        cv = c_ref[...]
        cond = (cv * _sigmoid(cv)).astype(BF16)
        o_ref[0] = jnp.dot(cond, w_ref[0].astype(BF16), preferred_element_type=F32) + b_ref[0]

    return pl.pallas_call(
        body, grid=(DEPTH,),
        in_specs=[pl.BlockSpec((16, D_MODEL), lambda l: (0, 0)), pl.BlockSpec((1, D_MODEL, n), lambda l: (l, 0, 0)),
                  pl.BlockSpec((1, 1, n), lambda l: (l, 0, 0))],
        out_specs=pl.BlockSpec((1, 16, n), lambda l: (l, 0, 0)), out_shape=SDS((DEPTH, 16, n), F32),
        compiler_params=_cparams(("parallel",)), name=name)(c_all, w, b)


def _ada_bwd(c_t, dmod, name):
    n = dmod.shape[2]
    bn = _blk(n, 512)
    tr = 256

    def body(c_ref, d_ref, o_ref):
        cv = c_ref[...]
        cond = (cv * _sigmoid(cv)).astype(BF16).astype(F32)
        dm = d_ref[0].astype(BF16).astype(F32)
        acc = cond[:, 0:1] * dm[0:1, :]
        for b_ in range(1, 8):
            acc = acc + cond[:, b_:b_ + 1] * dm[b_:b_ + 1, :]
        o_ref[0] = acc

    return pl.pallas_call(
        body, grid=(DEPTH, D_MODEL // tr, n // bn),
        in_specs=[pl.BlockSpec((tr, 8), lambda l, i, j: (i, 0)), pl.BlockSpec((1, 8, bn), lambda l, i, j: (l, 0, j))],
        out_specs=pl.BlockSpec((1, tr, bn), lambda l, i, j: (l, i, j)), out_shape=SDS((DEPTH, D_MODEL, n), F32),
        compiler_params=_cparams(("parallel", "parallel", "parallel")), name=name)(c_t, dmod)


def _adamw(w, m, v, parts, name, layer=None, into=None, comm=None):
    L, R, C = w.shape
    P = parts.shape[0]
    tr = _blk_rows(R, max(16, (1 << 18) // C))
    nr = R // tr
    c1 = 1.0 - ADAM_B1 ** ADAM_STEP
    c2 = 1.0 - ADAM_B2 ** ADAM_STEP
    n_pass = 0 if into is None else 4

    def body(w_ref, m_ref, v_ref, p_ref, *rest):
        g_ref, d_ref, nm_ref, nv_ref = rest[n_pass:]
        g = p_ref[0].astype(F32)
        for k in range(1, P):
            g = g + p_ref[k].astype(F32)
        mn = ADAM_B1 * m_ref[0] + (1.0 - ADAM_B1) * g
        vn = ADAM_B2 * v_ref[0] + (1.0 - ADAM_B2) * (g * g)
        m_hat = mn / c1
        v_hat = vn / c2
        g_ref[0] = g
        nm_ref[0] = mn
        nv_ref[0] = vn
        d_ref[0] = -ADAM_LR * (m_hat / (jnp.sqrt(v_hat) + ADAM_EPS) + ADAM_WD * w_ref[0])

    if layer is None:
        grid = (L, nr)
        rs = pl.BlockSpec((1, tr, C), lambda l, i: (l, i, 0))
        ps = pl.BlockSpec((P, tr, C), lambda l, i: (0, l * nr + i, 0))
    else:
        grid = (1, nr)
        rs = pl.BlockSpec((1, tr, C), lambda l, i: (layer, i, 0))
        ps = pl.BlockSpec((P, tr, C), lambda l, i: (0, i, 0))
    passed = [] if into is None else list(into)
    return _call_hosting(
        body, comm=comm, grid=grid, in_specs=[rs, rs, rs, ps] + [ANY] * n_pass,
        out_specs=[rs, rs, rs, rs], out_shape=[SDS((L, R, C), F32)] * 4, scratch_shapes=[], name=name,
        args=(w, m, v, parts, *passed), io_aliases={4 + k: k for k in range(n_pass)})


def _blk_rows(R, cap):
    if R <= cap:
        return R
    best = None
    for t in range(16, cap + 1, 16):
        if R % t == 0:
            best = t
    assert best is not None, (R, cap)
    return best


def _add_cast_rows(g, t, name):
    Q, R, C = g.shape
    half = R // 2
    tr = _blk_rows(half, max(16, (1 << 19) // C))
    nb = half // tr

    def body(lo_ref, hi_ref, t_ref, o_ref):
        c = lax.axis_index("c")

        @pl.when(c == 0)
        def _():
            o_ref[...] = (lo_ref[...] + t_ref[...]).astype(o_ref.dtype)

        @pl.when(c == 1)
        def _():
            o_ref[...] = (hi_ref[...] + t_ref[...]).astype(o_ref.dtype)

    bs = pl.BlockSpec((1, tr, C), lambda q, i: (q, i, 0))
    hi = pl.BlockSpec((1, tr, C), lambda q, i: (q, nb + i, 0))
    return pl.pallas_call(
        body, grid=(Q, nb), in_specs=[bs, hi, bs], out_specs=bs, out_shape=SDS((Q, half, C), BF16),
        compiler_params=_cparams(("parallel", "parallel")), name=name)(g, g, t)


def _coords():
    return lax.axis_index("x"), lax.axis_index("y"), lax.axis_index("c")


def _flip(v, bit):
    return 1 - v if bit else v


def _all_gather8(v, name):
    R = v.shape[0]

    def body(v_ref, o_ref, send_sems, recv_sems):
        x, y, c = _coords()
        me = 4 * x + 2 * y + c
        o_ref[me] = v_ref[...]
        copies = []
        for k in range(1, 8):
            peer = (_flip(x, k & 4), _flip(y, k & 2), _flip(c, k & 1))
            cp = pltpu.make_async_remote_copy(
                src_ref=v_ref, dst_ref=o_ref.at[me], send_sem=send_sems.at[k - 1], recv_sem=recv_sems.at[k - 1],
                device_id=peer, device_id_type=MESH)
            cp.start()
            copies.append(cp)
        for cp in copies:
            cp.wait_recv()
        for cp in copies:
            cp.wait_send()

    return pl.pallas_call(
        body, in_specs=[VMEM_SPEC], out_specs=VMEM_SPEC, out_shape=SDS((8, R, LANE), v.dtype),
        scratch_shapes=[pltpu.SemaphoreType.DMA((7,)), pltpu.SemaphoreType.DMA((7,))],
        compiler_params=pltpu.CompilerParams(vmem_limit_bytes=VMEM_LIMIT), name=name)(v)


def _sibling_swap_rows(arrs, name):
    n = len(arrs)

    def body(*refs):
        in_refs, out_refs = refs[:n], refs[n:2 * n]
        send_sems, recv_sems = refs[2 * n:]
        x, y, c = _coords()
        copies = []
        for a in range(n):
            Q, R = in_refs[a].shape[0], in_refs[a].shape[1]
            half = R // 2
            src = in_refs[a].at[pl.ds(0, Q), pl.ds(pl.multiple_of((1 - c) * half, 16), half)]
            cp = pltpu.make_async_remote_copy(
                src_ref=src, dst_ref=out_refs[a], send_sem=send_sems.at[a], recv_sem=recv_sems.at[a],
                device_id=(x, y, 1 - c), device_id_type=MESH)
            cp.start()
            copies.append(cp)
        for cp in copies:
            cp.wait_recv()
        for cp in copies:
            cp.wait_send()

    return pl.pallas_call(
        body, in_specs=[ANY] * n, out_specs=[ANY] * n,
        out_shape=[SDS((a.shape[0], a.shape[1] // 2, a.shape[2]), a.dtype) for a in arrs],
        scratch_shapes=[pltpu.SemaphoreType.DMA((n,)), pltpu.SemaphoreType.DMA((n,))],
        name=name)(*arrs)


class _OwnerReduce:
    aliased = False

    def __init__(self, srcs, lay):
        self.srcs, self.lay, self.n = list(srcs), lay, len(srcs)
        self.out_shapes = [SDS(a.shape, a.dtype) for a in self.srcs]
        self.sem_shapes = [pltpu.SemaphoreType.DMA((self.n, 3)), pltpu.SemaphoreType.DMA((self.n, 3)),
                           pltpu.SemaphoreType.DMA((self.n,))]

    def _copies(self, src_refs, dst_refs, sems):
        ici_send, ici_recv, loc_sem = sems
        x, y, c = _coords()
        p = 2 * x + y
        local, remote = [], []
        for a in range(self.n):
            local.append(pltpu.make_async_copy(src_refs[a].at[p], dst_refs[a].at[p], loc_sem.at[a]))
            for k in range(1, 4):
                qx, qy = _flip(x, k & 2), _flip(y, k & 1)
                remote.append(pltpu.make_async_remote_copy(
                    src_ref=src_refs[a].at[2 * qx + qy], dst_ref=dst_refs[a].at[p], send_sem=ici_send.at[a, k - 1],
                    recv_sem=ici_recv.at[a, k - 1], device_id=(qx, qy, self.lay), device_id_type=MESH))
        return c, local, remote

    def start(self, src_refs, dst_refs, sems):
        c, local, remote = self._copies(src_refs, dst_refs, sems)

        @pl.when(c == self.lay)
        def _():
            for cp in local + remote:
                cp.start()

    def finish(self, src_refs, dst_refs, sems):
        c, local, remote = self._copies(src_refs, dst_refs, sems)

        @pl.when(c == self.lay)
        def _():
            for cp in remote:
                cp.wait_recv()
            for cp in remote:
                cp.wait_send()
            for cp in local:
                cp.wait()


def _call_hosting(body, *, comm, grid, in_specs, out_specs, out_shape, scratch_shapes, name, args, semantics=None,
                  io_aliases=None):
    n_in, n_out, n_scr = len(args), len(out_shape), len(scratch_shapes)
    if comm is None:
        sem = semantics if semantics is not None else ("parallel",) * len(grid)
        res = pl.pallas_call(body, grid=grid, in_specs=in_specs, out_specs=out_specs, out_shape=out_shape,
                             scratch_shapes=scratch_shapes, input_output_aliases=dict(io_aliases or {}),
                             compiler_params=_cparams(sem), name=name)(*args)
        return list(res), None
    k = comm.n

    def hosted(*refs):
        ins, cin = refs[:n_in], refs[n_in:n_in + k]
        outs = refs[n_in + k:n_in + k + n_out]
        cout = refs[n_in + k + n_out:n_in + 2 * k + n_out]
        scr = refs[n_in + 2 * k + n_out:n_in + 2 * k + n_out + n_scr]
        sems = refs[n_in + 2 * k + n_out + n_scr:]
        first = pl.program_id(0) == 0
        last = pl.program_id(0) == grid[0] - 1
        for d in range(1, len(grid)):
            first = jnp.logical_and(first, pl.program_id(d) == 0)
            last = jnp.logical_and(last, pl.program_id(d) == grid[d] - 1)

        @pl.when(first)
        def _():
            comm.start(cin, cout, sems)

        body(*ins, *outs, *scr)

        @pl.when(last)
        def _():
            comm.finish(cin, cout, sems)

    aliases = dict(io_aliases or {})
    if comm.aliased:
        aliases.update({n_in + j: n_out + j for j in range(k)})
    res = pl.pallas_call(
        hosted, grid=grid, in_specs=list(in_specs) + [ANY] * k, out_specs=list(out_specs) + [ANY] * k,
        out_shape=list(out_shape) + comm.out_shapes, scratch_shapes=list(scratch_shapes) + comm.sem_shapes,
        input_output_aliases=aliases, compiler_params=_cparams(("arbitrary",) * len(grid)),
        name=name)(*args, *comm.srcs)
    return list(res[:n_out]), list(res[n_out:])


class _RowHalfGather:
    aliased = False

    def __init__(self, srcs):
        self.srcs, self.n = list(srcs), len(srcs)
        self.out_shapes = [SDS((4,) + a.shape, a.dtype) for a in self.srcs]
        n = self.n
        self.sem_shapes = [pltpu.SemaphoreType.DMA((n, 3)), pltpu.SemaphoreType.DMA((n, 3)),
                           pltpu.SemaphoreType.DMA((n, 3)), pltpu.SemaphoreType.DMA((n, 3)),
                           pltpu.SemaphoreType.DMA((n,))]

    def _copies(self, src_refs, dst_refs, sems):
        ici_send, ici_recv, d2d_send, d2d_recv, loc_sem = sems
        x, y, c = _coords()
        p = 2 * x + y
        local, first, fwd = [], [], []
        for a in range(self.n):
            R = src_refs[a].shape[0] // 2
            half = pl.ds(pl.multiple_of(c * R, 16), R)
            local.append(pltpu.make_async_copy(src_refs[a], dst_refs[a].at[p], loc_sem.at[a]))
            for k in range(1, 4):
                qx, qy = _flip(x, k & 2), _flip(y, k & 1)
                first.append(pltpu.make_async_remote_copy(
                    src_ref=src_refs[a].at[half], dst_ref=dst_refs[a].at[p, half], send_sem=ici_send.at[a, k - 1],
                    recv_sem=ici_recv.at[a, k - 1], device_id=(qx, qy, c), device_id_type=MESH))
                slot = dst_refs[a].at[2 * qx + qy, half]
                fwd.append(pltpu.make_async_remote_copy(
                    src_ref=slot, dst_ref=slot, send_sem=d2d_send.at[a, k - 1], recv_sem=d2d_recv.at[a, k - 1],
                    device_id=(x, y, 1 - c), device_id_type=MESH))
        return local, first, fwd

    def start(self, src_refs, dst_refs, sems):
        local, first, _ = self._copies(src_refs, dst_refs, sems)
        for cp in local + first:
            cp.start()

    def finish(self, src_refs, dst_refs, sems):
        local, first, fwd = self._copies(src_refs, dst_refs, sems)
        for got, on in zip(first, fwd):
            got.wait_recv()
            on.start()
        for cp in fwd:
            cp.wait_recv()
        for cp in first + fwd:
            cp.wait_send()
        for cp in local:
            cp.wait()

    def run(self, name):
        n = self.n

        def body(*refs):
            src_refs, dst_refs, sems = refs[:n], refs[n:2 * n], refs[2 * n:]
            self.start(src_refs, dst_refs, sems)
            self.finish(src_refs, dst_refs, sems)

        return pl.pallas_call(body, in_specs=[ANY] * n, out_specs=[ANY] * n, out_shape=self.out_shapes,
                              scratch_shapes=self.sem_shapes, name=name)(*self.srcs)


class _RowHalfReduce:
    aliased = False

    def __init__(self, srcs):
        self.srcs, self.n = list(srcs), len(srcs)
        self.out_shapes = [SDS((4, 2 * a.shape[1], a.shape[2]), a.dtype) for a in self.srcs]
        n = self.n
        self.sem_shapes = [pltpu.SemaphoreType.DMA((n, 3)), pltpu.SemaphoreType.DMA((n, 3)),
                           pltpu.SemaphoreType.DMA((n, 4)), pltpu.SemaphoreType.DMA((n, 4)),
                           pltpu.SemaphoreType.DMA((n,))]

    def _copies(self, src_refs, dst_refs, sems):
        ici_send, ici_recv, d2d_send, d2d_recv, loc_sem = sems
        x, y, c = _coords()
        p = 2 * x + y
        local, first, fwd = [], [], []
        for a in range(self.n):
            R = src_refs[a].shape[1]
            half = pl.ds(pl.multiple_of(c * R, 16), R)
            local.append(pltpu.make_async_copy(src_refs[a].at[p], dst_refs[a].at[p, half], loc_sem.at[a]))
            for k in range(4):
                qx, qy = _flip(x, k & 2), _flip(y, k & 1)
                if k:
                    first.append(pltpu.make_async_remote_copy(
                        src_ref=src_refs[a].at[2 * qx + qy], dst_ref=dst_refs[a].at[p, half],
                        send_sem=ici_send.at[a, k - 1], recv_sem=ici_recv.at[a, k - 1], device_id=(qx, qy, c),
                        device_id_type=MESH))
                slot = dst_refs[a].at[2 * qx + qy, half]
                fwd.append(pltpu.make_async_remote_copy(
                    src_ref=slot, dst_ref=slot, send_sem=d2d_send.at[a, k], recv_sem=d2d_recv.at[a, k],
                    device_id=(x, y, 1 - c), device_id_type=MESH))
        return local, first, fwd

    def start(self, src_refs, dst_refs, sems):
        local, first, _ = self._copies(src_refs, dst_refs, sems)
        for cp in local + first:
            cp.start()

    def finish(self, src_refs, dst_refs, sems):
        local, first, fwd = self._copies(src_refs, dst_refs, sems)
        for a in range(self.n):
            local[a].wait()
            fwd[4 * a].start()
            for k in range(1, 4):
                first[3 * a + k - 1].wait_recv()
                fwd[4 * a + k].start()
        for cp in fwd:
            cp.wait_recv()
        for cp in first + fwd:
            cp.wait_send()


class _SiblingSend:
    aliased = False

    def __init__(self, srcs, src_core):
        self.srcs, self.src_core, self.n = list(srcs), src_core, len(srcs)
        self.out_shapes = [SDS(a.shape, a.dtype) for a in self.srcs]
        self.sem_shapes = [pltpu.SemaphoreType.DMA((self.n,)), pltpu.SemaphoreType.DMA((self.n,))]

    def _copies(self, src_refs, dst_refs, sems):
        x, y, c = _coords()
        return c, [pltpu.make_async_remote_copy(
            src_ref=src_refs[a], dst_ref=dst_refs[a], send_sem=sems[0].at[a], recv_sem=sems[1].at[a],
            device_id=(x, y, 1 - c), device_id_type=MESH) for a in range(self.n)]

    def start(self, src_refs, dst_refs, sems):
        c, copies = self._copies(src_refs, dst_refs, sems)

        @pl.when(c == self.src_core)
        def _():
            for cp in copies:
                cp.start()

    def finish(self, src_refs, dst_refs, sems):
        c, copies = self._copies(src_refs, dst_refs, sems)

        @pl.when(c == self.src_core)
        def _():
            for cp in copies:
                cp.wait_send()

        @pl.when(c != self.src_core)
        def _():
            for cp in copies:
                cp.wait_recv()


class _Handoff:
    aliased = True

    def __init__(self, srcs, lay, slots):
        self.srcs, self.lay, self.slots, self.n = list(srcs), lay, tuple(slots), len(srcs)
        self.out_shapes = [SDS(a.shape, a.dtype) for a in self.srcs]
        ns = len(self.slots)
        self.sem_shapes = [pltpu.SemaphoreType.DMA((self.n, ns)), pltpu.SemaphoreType.DMA((self.n, ns))]

    def _copies(self, dst_refs, sems):
        x, y, c = _coords()
        copies = []
        for a in range(self.n):
            for j, k in enumerate(self.slots):
                slot = dst_refs[a].at[2 * _flip(x, k & 2) + _flip(y, k & 1)]
                copies.append(pltpu.make_async_remote_copy(
                    src_ref=slot, dst_ref=slot, send_sem=sems[0].at[a, j], recv_sem=sems[1].at[a, j],
                    device_id=(x, y, 1 - c), device_id_type=MESH))
        return c, copies

    def start(self, src_refs, dst_refs, sems):
        c, copies = self._copies(dst_refs, sems)

        @pl.when(c == self.lay)
        def _():
            for cp in copies:
                cp.start()

    def finish(self, src_refs, dst_refs, sems):
        c, copies = self._copies(dst_refs, sems)

        @pl.when(c == self.lay)
        def _():
            for cp in copies:
                cp.wait_send()

        @pl.when(c != self.lay)
        def _():
            for cp in copies:
                cp.wait_recv()


def _add_cast_on(a, b, lay, name):
    Q, R, C = b.shape
    tr = _blk_rows(R, max(16, (1 << 19) // C))

    def body(a_ref, b_ref, o_ref):
        @pl.when(lax.axis_index("c") == lay)
        def _():
            o_ref[...] = (a_ref[...] + b_ref[...]).astype(o_ref.dtype)

    bs = pl.BlockSpec((1, tr, C), lambda q, i: (q, i, 0))
    return pl.pallas_call(
        body, grid=(Q, R // tr), in_specs=[bs, bs], out_specs=bs, out_shape=SDS((Q, R, C), BF16),
        compiler_params=_cparams(("parallel", "parallel")), name=name)(a, b)


_IN_SIZES = (512, 128, 128, 512, 512, 512, 8, 512, 512, 512, 3072)
_IN_OFF = tuple(int(v) for v in np.cumsum((0,) + _IN_SIZES))
_IN_Q = N_IN_COLS // 4


def _pack_w_in(w):
    def cols(lo, hi):
        out = []
        while lo < hi:
            q, off = divmod(lo, _IN_Q)
            n = min(hi - lo, _IN_Q - off)
            out.append(w[q, :, off:off + n])
            lo += n
        return out

    fb0, fb1, g0 = _IN_OFF[6], _IN_OFF[7], _IN_OFF[10]
    wqkv = jnp.concatenate(cols(0, fb0) + cols(fb1, g0), axis=1)
    wgf = jnp.concatenate(cols(g0, N_IN_COLS) + cols(fb0, fb1) + [jnp.zeros((w.shape[1], LANE - 8), w.dtype)], axis=1)
    return wqkv, wgf


def _unpack_w_in(dqkv, dgf):
    fb0, fb1, g0 = _IN_OFF[6], _IN_OFF[7], _IN_OFF[10]

    def cols(lo, hi):
        out = []
        while lo < hi:
            if lo < fb0:
                n = min(hi, fb0) - lo
                out.append(dqkv[:, lo:lo + n])
            elif lo < fb1:
                n = min(hi, fb1) - lo
                out.append(dgf[:, 3072 + lo - fb0:3072 + lo - fb0 + n])
            elif lo < g0:
                n = min(hi, g0) - lo
                out.append(dqkv[:, lo - 8:lo - 8 + n])
            else:
                n = hi - lo
                out.append(dgf[:, lo - g0:lo - g0 + n])
            lo += n
        return out

    return jnp.stack([jnp.concatenate(cols(q * _IN_Q, (q + 1) * _IN_Q), axis=1) for q in range(4)])


def _pad_rows(a, rows):
    return jnp.pad(a, ((0, rows - a.shape[0]), (0, 0)))


def _small_pack(parts):
    flat = jnp.concatenate([p.reshape(-1) for p in parts])
    n = flat.shape[0]
    rows = -(-n // LANE)
    rows = -(-rows // 8) * 8
    return jnp.pad(flat, (0, rows * LANE - n)).reshape(rows, LANE)


def _small_unpack(block, shapes):
    flat = block.reshape(-1)
    out, off = [], 0
    for s in shapes:
        n = int(np.prod(s))
        out.append(flat[off:off + n].reshape(s))
        off += n
    return out


def _kv_same(g):
    return 0


def _kv_own(g):
    return g


_mm_plain = _mm


def _mm_hosting(a, b, *, comm, **kw):
    if comm is None:
        return _mm(a, b, **kw), None
    return _mm(a, b, comm=comm, **kw)


def _layer_fwd(x, mod, p, l, ride):
    sh_m, sc_m, g_m, sh_f, sc_f, g_f = mod
    nm = "l%d_" % l

    def carried(name, run):
        res, got = run(ride.comm_for(name))
        if got is not None:
            ride.done(name, got)
        return res

    h1 = _norm_mod_fwd(x, p["norm_mix_g"], sc_m, sh_m, nm + "norm_mix_fwd")
    qkv = carried("proj_qkv", lambda cm: _mm_hosting(h1, p["wqkv"], mode="nn", out_dtype=BF16,
                                                     name=nm + "proj_qkv", comm=cm))
    gf = _mm(h1, p["wgf"], mode="nn", out_dtype=F32, name=nm + "proj_gf", cap_n=640)
    qkv_t = qkv.T
    o_a_t = carried("attn_a", lambda cm: _bandT_fwd(
        (qkv_t, 0), _heads(qkv[:, 512:640], A_KV_HEADS), (qkv_t, 640), p["alibi"], p["sink_tab"],
        GQ=4, GK=1, P=A_PREV, kvoff=_kv_same, name=nm + "attn_a_fwd", comm=cm))
    cum = _fox_cum(gf, p["b_forget_pad"], nm + "fox_cum")
    cum_t = cum[:, :N_HEADS].T
    cc, cr = cum_t[:, :, None], cum_t[:, None, :]
    o_b_t, lse_b = carried("attn_b", lambda cm: _foxT_fwd(
        (qkv_t, 768), _heads(qkv[:, 1280:1792], N_HEADS), (qkv_t, 1792), cc, cr, nm + "attn_b_fwd", comm=cm))
    o_c_t = carried("attn_c", lambda cm: _bandT_fwd(
        (qkv_t, 2304), _heads(qkv[:, 2816:3328], N_HEADS), (qkv_t, 3328), p["rel_tab"], p["no_sink"],
        GQ=2, GK=2, P=C_PREV, kvoff=_kv_own, name=nm + "attn_c_fwd", comm=cm))
    p = dict(p, **ride.late_weights())
    o = jnp.concatenate([o_a_t, o_b_t, o_c_t], axis=0).T
    y = _mm(o, p["wb"], mode="nn", out_dtype=F32, groups=3, name=nm + "branch")
    merged = _merge_fwd(y, gf, nm + "merge_fwd")
    mix = _mm(merged, p["wout"], mode="nn", out_dtype=F32, name=nm + "out_proj")
    x1 = _resid_fwd(x, mix, g_m, nm + "resid_mix")
    h2 = _norm_mod_fwd(x1, p["norm_ffn_g"], sc_f, sh_f, nm + "norm_ffn_fwd")
    u = carried("ffn_in", lambda cm: _mm_hosting(h2, p["wfi"], mode="nn", out_dtype=F32, name=nm + "ffn_in",
                                                 cap_n=512, comm=cm))
    a = _swiglu_fwd(u, nm + "swiglu_fwd")
    f = _mm(a, p["wfo"], mode="nn", out_dtype=F32, name=nm + "ffn_out", cap_m=1024)
    x2 = _resid_fwd(x1, f, g_f, nm + "resid_ffn")
    saved = dict(x=x, h1=h1, qkv=qkv, qkv_t=qkv_t, gf=gf, cc=cc, cr=cr, o_b_t=o_b_t, lse_b=lse_b, o=o, y=y, merged=merged,
                 mix=mix, x1=x1, h2=h2, u=u, a=a, f=f)
    return x2, saved, p


def _layer_bwd(dx2, mod, p, s, l, ride=None):
    sh_m, sc_m, g_m, sh_f, sc_f, g_f = mod
    nm = "l%d_" % l

    def _mm(a, b, *, name, **kw):
        comm = ride.comm_for(name) if ride is not None else None
        if comm is None:
            return _mm_plain(a, b, name=nm + name, **kw)
        out, got = _mm_plain(a, b, name=nm + name, comm=comm, **kw)
        ride.done(name, got)
        return out

    dg_f, df = _resid_bwd(dx2, s["f"], g_f, nm + "resid_ffn_bwd")
    da = _mm(df, p["wfo"], mode="nt", out_dtype=F32, name="ffn_out_dx", cap_m=1024, cap_n=1408)
    d_wfo = _mm(s["a"], df, mode="tn", out_dtype=F32, name="ffn_out_dw", cap_m=1408, cap_k=2048)
    du = _swiglu_bwd(da, s["u"], nm + "swiglu_bwd")
    dh2 = _mm(du, p["wfi"], mode="nt", out_dtype=F32, name="ffn_in_dx", cap_m=1024)
    d_wfi = _mm(s["h2"], du, mode="tn", out_dtype=F32, name="ffn_in_dw", cap_m=1024, cap_n=1408, cap_k=2048,
                col_quarters=True)
    dx1, dsc_f, dsh_f, dgn_f = _norm_mod_bwd(s["x1"], [dh2], dx2, p["norm_ffn_g"], sc_f, nm + "norm_ffn_bwd")
    dg_m, dmix = _resid_bwd(dx1, s["mix"], g_m, nm + "resid_mix_bwd")
    dmerged = _mm(dmix, p["wout"], mode="nt", out_dtype=F32, name="out_proj_dx")
    d_wout = _mm(s["merged"], dmix, mode="tn", out_dtype=F32, name="out_proj_dw", cap_m=1024, cap_k=2048)
    dy, dgates = _merge_bwd(dmerged, s["y"], s["gf"], nm + "merge_bwd")
    do = _mm(dy, p["wb"], mode="nt", out_dtype=BF16, groups=3, name="branch_dx")
    d_wb = _mm(s["o"], dy, mode="tn", out_dtype=F32, groups=3, name="branch_dw", cap_k=2048,
               col_quarters=True)
    comms = ride.exchanges() if ride is not None else (None, None, None)
    qkv, qkv_t = s["qkv"], s["qkv_t"]
    do_t = do.T
    (dqa_t, dka_h, dva_h, _, dsink), got_a = _bandT_bwd(
        (qkv_t, 0), _heads(qkv[:, 0:512], N_HEADS), _heads(qkv[:, 512:640], A_KV_HEADS), (qkv_t, 512),
        _heads(qkv[:, 640:768], A_KV_HEADS), (do_t, 0), _heads(do[:, 0:512], N_HEADS), p["alibi"], p["sink_tab"],
        GQ=4, GK=1, P=A_PREV, kvoff=_kv_same, name=nm + "attn_a_bwd", comm=comms[0])
    (dqb_t, dkb_h, dvb_h, dck, dcq), got_b = _foxT_bwd(
        (qkv_t, 768), _heads(qkv[:, 768:1280], N_HEADS), _heads(qkv[:, 1280:1792], N_HEADS), (qkv_t, 1280),
        _heads(qkv[:, 1792:2304], N_HEADS), s["cc"], s["cr"], s["o_b_t"], (do_t, 512),
        _heads(do[:, 512:1024], N_HEADS), s["lse_b"], nm + "attn_b_bwd", comm=comms[1])
    dcum = jnp.pad((dck[:, :, 0] + dcq[:, 0, :]).T, ((0, 0), (0, LANE - N_HEADS)))
    dfb, db_forget = _fox_cum_bwd(s["gf"], p["b_forget_pad"], dcum, nm + "fox_cum_bwd")
    (dqc_t, dkc_h, dvc_h, dbias_c, _), got_c = _bandT_bwd(
        (qkv_t, 2304), _heads(qkv[:, 2304:2816], N_HEADS), _heads(qkv[:, 2816:3328], N_HEADS), (qkv_t, 2816),
        _heads(qkv[:, 3328:3840], N_HEADS), (do_t, 1024), _heads(do[:, 1024:1536], N_HEADS), p["rel_tab"],
        p["no_sink"], GQ=2, GK=2, P=C_PREV, kvoff=_kv_own, name=nm + "attn_c_bwd", comm=comms[2])
    d_rel = _rel_reduce(jnp.transpose(_unpair_table(dbias_c), (1, 0, 2)), nm + "rel_reduce")[:, :N_REL]
    dqkv = jnp.concatenate([dqa_t.T, _unheads(dka_h), _unheads(dva_h), dqb_t.T, _unheads(dkb_h), _unheads(dvb_h),
                            dqc_t.T, _unheads(dkc_h), _unheads(dvc_h)], axis=1)
    dgf = jnp.concatenate([dgates, dfb], axis=1)
    if ride is not None:
        ride.exchanged((got_a, got_b, got_c))
    dh1a = _mm(dqkv, p["wqkv"], mode="nt", out_dtype=F32, name="proj_qkv_dx", cap_k=1024)
    dh1b = _mm(dgf, p["wgf"], mode="nt", out_dtype=F32, name="proj_gf_dx", cap_k=640)
    d_wqkv = _mm(s["h1"], dqkv, mode="tn", out_dtype=F32, name="proj_qkv_dw", cap_m=1024, cap_k=2048)
    d_wgf = _mm(s["h1"], dgf, mode="tn", out_dtype=F32, name="proj_gf_dw", cap_m=1024, cap_n=640, cap_k=2048)
    dx, dsc_m, dsh_m, dgn_m = _norm_mod_bwd(s["x"], [dh1a, dh1b], dx1, p["norm_mix_g"], sc_m, nm + "norm_mix_bwd")
    d_mod = jnp.concatenate([dsh_m, dsc_m, dg_m, dsh_f, dsc_f, dg_f], axis=1)[0]
    grads = dict(w_in=_unpack_w_in(d_wqkv, d_wgf), w_branch=d_wb, w_out=d_wout.reshape(4, -1, D_MODEL),
                 w_ffn_in=d_wfi, w_ffn_out=d_wfo.reshape(4, -1, D_MODEL),
                 norm_mix_g=dgn_m[0], norm_ffn_g=dgn_f[0], b_forget=db_forget[0, :N_HEADS],
                 sinks=dsink[:, 0, 0], rel_bias=d_rel, d_mod=d_mod)
    return dx, grads


def kernel(x, c, norm_mix_g, norm_ffn_g, w_ada, b_ada, w_in, b_forget, sinks, rel_bias, w_branch, w_out, w_ffn_in, w_ffn_out, final_norm_g, loss_target, m_norm_mix_g, m_norm_ffn_g, m_w_ada, m_b_ada, m_w_in, m_b_forget, m_sinks, m_rel_bias, m_w_branch, m_w_out, m_w_ffn_in, m_w_ffn_out, m_final_norm_g, v_norm_mix_g, v_norm_ffn_g, v_w_ada, v_b_ada, v_w_in, v_b_forget, v_sinks, v_rel_bias, v_w_branch, v_w_out, v_w_ffn_in, v_w_ffn_out, v_final_norm_g):
    xi, yi, ci = _coords()
    chip = 2 * xi + yi
    dev = 2 * chip + ci
    xs = x[0]
    S = xs.shape[0]
    n_ada = w_ada.shape[2]

    big_names = ("w_in", "w_branch", "w_out", "w_ffn_in", "w_ffn_out")
    big_w = dict(w_in=w_in, w_branch=w_branch, w_out=w_out, w_ffn_in=w_ffn_in, w_ffn_out=w_ffn_out)
    big_m = dict(w_in=m_w_in, w_branch=m_w_branch, w_out=m_w_out, w_ffn_in=m_w_ffn_in, w_ffn_out=m_w_ffn_out)
    big_v = dict(w_in=v_w_in, w_branch=v_w_branch, w_out=v_w_out, w_ffn_in=v_w_ffn_in, w_ffn_out=v_w_ffn_out)
    flat2 = lambda a: a.reshape(-1, a.shape[-1])
    shards = [[flat2(big_w[n][l]).astype(BF16) for n in big_names] for l in range(DEPTH)]
    gw = [[None] * (len(big_names) + 2) for _ in range(DEPTH)]
    for l in range(DEPTH):
        shards[l] += [shards[l][0][:D_MODEL // 2], shards[l][0][D_MODEL // 2:]]
    gw[0][0] = _RowHalfGather([shards[0][0]]).run("weights_gather_w_in_l0")[0]
    host_g = ((1, 2, 4), (0,), (3,))

    class WeightRide:
        def __init__(self, l, plan):
            self.l, self.plan = l, plan

        def comm_for(self, name):
            if name not in self.plan:
                return None
            lay, idx = self.plan[name]
            return _RowHalfGather([shards[lay][i] for i in idx])

        def done(self, name, got):
            lay, idx = self.plan[name]
            for i, r in zip(idx, got):
                gw[lay][i] = r

        def late_weights(self):
            g = gw[self.l]
            return dict(wb=jnp.transpose(g[1], (1, 0, 2)).reshape(3 * BRANCH_W, D_MODEL),
                        wout=g[2].reshape(D_MODEL, D_MODEL),
                        wfi=jnp.transpose(g[3], (1, 0, 2)).reshape(D_MODEL, 2 * FFN_H),
                        wfo=g[4].reshape(FFN_H, D_MODEL))

    weight_plan = [
        {"proj_qkv": (0, (1, 2)), "attn_a": (0, (4,)), "attn_b": (0, (3,)), "attn_c": (1, (5,)), "ffn_in": (1, (6,))},
        {"attn_a": (1, (1, 2)), "attn_b": (1, (3,)), "attn_c": (1, (4,))}]


    c_all = _all_gather8(c.reshape(8, LANE), "gather_c").reshape(8, D_MODEL)
    b_sh = lax.dynamic_slice_in_dim(b_ada, chip * n_ada, n_ada, axis=1)[:, None, :]
    mod_sh = _ada_fwd(_pad_rows(c_all, 16), w_ada, b_sh, "ada_fwd")[:, :8, :]
    mod_all = _all_gather8(mod_sh.reshape(-1, LANE), "gather_mod").reshape(8, DEPTH, 8, n_ada)
    mod_mine = lax.dynamic_index_in_dim(mod_all[0::2], dev, axis=2, keepdims=False)
    mod = mod_mine.transpose(1, 0, 2).reshape(DEPTH, 6, D_MODEL)

    alibi = _pair_table(_alibi_table())
    no_sink = jnp.full((N_HEADS, 8, LANE), NEG_INF, F32)
    def make_params(l):
        if gw[l][0] is None:
            gw[l][0] = jnp.concatenate([gw[l][5], gw[l][6]], axis=1)
        wqkv, wgf = _pack_w_in(gw[l][0])
        rel_tab = _rel_expand(jnp.pad(rel_bias[l], ((0, 0), (0, N_REL_PAD - N_REL))), "l%d_rel_expand" % l)
        return dict(
            wqkv=wqkv, wgf=wgf, norm_mix_g=norm_mix_g[l][None], norm_ffn_g=norm_ffn_g[l][None],
            b_forget_pad=jnp.pad(b_forget[l], (0, LANE - N_HEADS))[None],
            sink_tab=jnp.broadcast_to(sinks[l][:, None, None], (N_HEADS, 8, LANE)),
            no_sink=no_sink, alibi=alibi, rel_tab=_pair_table(jnp.transpose(rel_tab, (1, 0, 2))))

    mods = [[mod[l, k][None] for k in range(6)] for l in range(DEPTH)]
    params, saved = [None] * DEPTH, [None] * DEPTH
    h = xs
    for l in range(DEPTH):
        h, saved[l], params[l] = _layer_fwd(h, mods[l], make_params(l), l, WeightRide(l, weight_plan[l]))
    loss_dev, dh, d_final = _final_loss(h, final_norm_g[None], loss_target[0], "final_loss")
    grads = [None] * DEPTH
    dh, grads[1] = _layer_bwd(dh, mods[1], params[1], saved[1], 1)

    class Layer1Ride:
        sends = {"ffn_out_dx": (4,), "ffn_out_dw": (1, 2), "ffn_in_dx": (3,), "ffn_in_dw": (0,)}
        hands = {"proj_qkv_dx": (0,), "proj_gf_dx": (3,), "proj_qkv_dw": (4,), "proj_gf_dw": (1, 2)}

        def __init__(self, g):
            self.g, self.t = g, [None] * len(g)
            self.parts, self.final = [None] * len(g), [None] * len(g)

        def comm_for(self, name):
            if name in self.sends:
                return _SiblingSend([self.g[i] for i in self.sends[name]], 0)
            if name in self.hands:
                return _Handoff([self.parts[i] for i in self.hands[name]], 1, (0, 1, 2, 3))
            return None

        def done(self, name, got):
            idx, dst = (self.sends[name], self.t) if name in self.sends else (self.hands[name], self.final)
            for i, r in zip(idx, got):
                dst[i] = r

        def exchanges(self):
            sums = [_add_cast_on(a, b, 1, "grads_chip_sum_l1_" + n) for n, a, b in zip(big_names, self.g, self.t)]
            return tuple(_OwnerReduce([sums[i] for i in idx], 1) for idx in host_g)

        def exchanged(self, got):
            for res, idx in zip(got, host_g):
                for r, i in zip(res, idx):
                    self.parts[i] = r

    ride = Layer1Ride([grads[1][n] for n in big_names])
    dh, grads[0] = _layer_bwd(dh, mods[0], params[0], saved[0], 0, ride)
    grad_x = dh[None]
    loss = lax.psum(loss_dev[0, 0], ("x", "y", "c"))
    parts1 = ride.final
    g0 = [grads[0][n] for n in big_names]
    t0 = _sibling_swap_rows(g0, "grads_swap_l0")
    sums0 = [_add_cast_rows(a, b, "grads_chip_sum_l0_" + n) for n, a, b in zip(big_names, g0, t0)]
    carriers = {"w_ada": (0,), "w_in": (3,), "w_ffn_in": (4,), "w_ffn_out": (1, 2)}
    parts0 = [None] * len(big_names)

    def carry(key):
        return _RowHalfReduce([sums0[i] for i in carriers[key]]) if key in carriers else None

    def fetched(key, got):
        if got is not None:
            for i, r in zip(carriers[key], got):
                parts0[i] = r

    small_names = ("norm_mix_g", "norm_ffn_g", "b_ada", "b_forget", "sinks", "rel_bias", "final_norm_g")
    small_w = dict(norm_mix_g=norm_mix_g, norm_ffn_g=norm_ffn_g, b_ada=b_ada, b_forget=b_forget, sinks=sinks,
                   rel_bias=rel_bias, final_norm_g=final_norm_g)
    small_m = dict(norm_mix_g=m_norm_mix_g, norm_ffn_g=m_norm_ffn_g, b_ada=m_b_ada, b_forget=m_b_forget,
                   sinks=m_sinks, rel_bias=m_rel_bias, final_norm_g=m_final_norm_g)
    small_v = dict(norm_mix_g=v_norm_mix_g, norm_ffn_g=v_norm_ffn_g, b_ada=v_b_ada, b_forget=v_b_forget,
                   sinks=v_sinks, rel_bias=v_rel_bias, final_norm_g=v_final_norm_g)
    small_g = dict(
        norm_mix_g=jnp.stack([grads[l]["norm_mix_g"] for l in range(DEPTH)]),
        norm_ffn_g=jnp.stack([grads[l]["norm_ffn_g"] for l in range(DEPTH)]),
        b_ada=jnp.stack([grads[l]["d_mod"] for l in range(DEPTH)]),
        b_forget=jnp.stack([grads[l]["b_forget"] for l in range(DEPTH)]),
        sinks=jnp.stack([grads[l]["sinks"] for l in range(DEPTH)]),
        rel_bias=jnp.stack([grads[l]["rel_bias"] for l in range(DEPTH)]),
        final_norm_g=d_final[0])
    shapes = [small_w[n].shape for n in small_names]
    g_all = _all_gather8(_small_pack([small_g[n] for n in small_names]), "gather_small_grads")
    res, _ = _adamw(_small_pack([small_w[n] for n in small_names])[None],
                    _small_pack([small_m[n] for n in small_names])[None],
                    _small_pack([small_v[n] for n in small_names])[None], g_all, "adamw_small")
    small_out = {n: [] for n in small_names}
    for r in res:
        for n, a in zip(small_names, _small_unpack(r[0], shapes)):
            small_out[n].append(a)
    off_b = sum(int(np.prod(s)) for s in shapes[:2])
    n_mod = DEPTH * 6 * D_MODEL
    dmod_all = g_all.reshape(8, -1)[:, off_b:off_b + n_mod].reshape(8, DEPTH, 6 * D_MODEL)
    dmod_sh = lax.dynamic_slice_in_dim(dmod_all, chip * n_ada, n_ada, axis=2).transpose(1, 0, 2)
    g_ada = _ada_bwd(c_all.T, dmod_sh, "ada_bwd")
    ada_out, got = _adamw(w_ada, m_w_ada, v_w_ada, flat2(g_ada)[None], "adamw_w_ada", comm=carry("w_ada"))
    fetched("w_ada", got)

    big_out, layer1 = {}, {}
    as3 = lambda a: a.reshape(a.shape[0], -1, a.shape[-1])
    for n, p1 in zip(big_names, parts1):
        layer1[n], got = _adamw(as3(big_w[n]), as3(big_m[n]), as3(big_v[n]), p1, "adamw_l1_" + n, layer=1,
                                comm=carry(n))
        fetched(n, got)
    for n, p0 in zip(big_names, parts0):
        res, _ = _adamw(as3(big_w[n]), as3(big_m[n]), as3(big_v[n]), p0, "adamw_l0_" + n, layer=0, into=layer1[n])
        big_out[n] = [r.reshape(big_w[n].shape) for r in res]

    order = ("norm_mix_g", "norm_ffn_g", "w_ada", "b_ada", "w_in", "b_forget", "sinks", "rel_bias", "w_branch",
             "w_out", "w_ffn_in", "w_ffn_out", "final_norm_g")

    def pick(n, k):
        if n == "w_ada":
            return ada_out[k]
        if n in big_out:
            return big_out[n][k]
        return small_out[n][k]

    outs = [loss, grad_x]
    for k in range(4):
        outs += [pick(n, k) for n in order]
    return tuple(outs)
```

```python
import numpy as np
import jax
import jax.numpy as jnp
from jax import lax
from jax.experimental import pallas as pl
from jax.experimental.pallas import tpu as pltpu

F32 = jnp.float32
BF16 = jnp.bfloat16
SDS = jax.ShapeDtypeStruct

D_MODEL = 1024
DEPTH = 2
CHUNK = 64
HEAD_DIM = 64
EPS = 1e-6
NEG_INF = -1e30
N_HEADS = 8
A_KV_HEADS = 2
A_PREV = 2
C_PREV = 8
REL_CLIP = 128
N_REL = 2 * REL_CLIP + 1
N_REL_PAD = 384
BRANCH_W = 512
FFN_H = 2816
FOX_BQ = 256
FOX_BK = 512
GF_COLS = 3200
N_IN_COLS = 6920
LANE = 128
VMEM_LIMIT = 48 * 1024 * 1024

ADAM_LR = 0.001
ADAM_B1 = 0.9
ADAM_B2 = 0.999
ADAM_EPS = 1e-08
ADAM_WD = 0.01
ADAM_STEP = 10

MESH = pl.DeviceIdType.MESH
ANY = pl.BlockSpec(memory_space=pl.ANY)
VMEM_SPEC = pl.BlockSpec(memory_space=pltpu.VMEM)


def _cparams(sem=None):
    return pltpu.CompilerParams(dimension_semantics=sem, vmem_limit_bytes=VMEM_LIMIT)


def _blk(n, cap):
    if n <= cap:
        return n
    best = None
    for m in range(LANE, cap + 1, LANE):
        if n % m == 0:
            best = m
    assert best is not None, (n, cap)
    return best


def _sigmoid(x):
    return 1.0 / (1.0 + jnp.exp(-x))


def _mm(a, b, *, mode, out_dtype, name, groups=1, cap_m=2048, cap_n=1024, cap_k=1408, col_quarters=False,
        comm=None):
    G = groups
    assert not col_quarters or mode == "tn"
    if mode == "nn":
        M, K, N = a.shape[0], a.shape[1] // G, b.shape[1]
        assert b.shape[0] == G * K
    elif mode == "nt":
        M, K, N = a.shape[0], a.shape[1] // G, b.shape[0] // G
        assert b.shape[1] == K
    else:
        K, M, N = a.shape[0], a.shape[1] // G, b.shape[1] // G
        assert b.shape[0] == K
    bm, bn, bk = _blk(M, cap_m), _blk(N // 4 if col_quarters else N, cap_n), _blk(K, cap_k)
    nm, nn, nk = M // bm, N // bn, K // bk
    if mode == "nn":
        a_spec = pl.BlockSpec((bm, bk), lambda g, i, j, k: (i, g * nk + k))
        b_spec = pl.BlockSpec((bk, bn), lambda g, i, j, k: (g * nk + k, j))
        o_spec = pl.BlockSpec((bm, bn), lambda g, i, j, k: (i, g * nn + j))
        dims = (((1,), (0,)), ((), ()))
        out_shape = (M, G * N)
    elif mode == "nt":
        a_spec = pl.BlockSpec((bm, bk), lambda g, i, j, k: (i, g * nk + k))
        b_spec = pl.BlockSpec((bn, bk), lambda g, i, j, k: (g * nn + j, k))
        o_spec = pl.BlockSpec((bm, bn), lambda g, i, j, k: (i, g * nn + j))
        dims = (((1,), (1,)), ((), ()))
        out_shape = (M, G * N)
    else:
        a_spec = pl.BlockSpec((bk, bm), lambda g, i, j, k: (k, g * nm + i))
        b_spec = pl.BlockSpec((bk, bn), lambda g, i, j, k: (k, g * nn + j))
        dims = (((0,), (0,)), ((), ()))
        if col_quarters:
            nq = nn // 4
            o_spec = pl.BlockSpec((1, bm, bn), lambda g, i, j, k: (j // nq, g * nm + i, j % nq))
            out_shape = (4, G * M, N // 4)
        else:
            o_spec = pl.BlockSpec((bm, bn), lambda g, i, j, k: (g * nm + i, j))
            out_shape = (G * M, N)

    def product(a_ref, b_ref):
        return lax.dot_general(a_ref[...].astype(BF16), b_ref[...].astype(BF16), dims, preferred_element_type=F32)

    def body_one(a_ref, b_ref, o_ref):
        o_ref[...] = product(a_ref, b_ref).astype(o_ref.dtype).reshape(o_ref.shape)

    def body_acc(a_ref, b_ref, o_ref, acc_ref):
        k = pl.program_id(3)

        @pl.when(k == 0)
        def _():
            acc_ref[...] = jnp.zeros_like(acc_ref)

        acc_ref[...] += product(a_ref, b_ref)

        @pl.when(k == nk - 1)
        def _():
            o_ref[...] = acc_ref[...].astype(o_ref.dtype).reshape(o_ref.shape)

    res, got = _call_hosting(
        body_one if nk == 1 else body_acc, comm=comm, grid=(G, nm, nn, nk), in_specs=[a_spec, b_spec],
        out_specs=[o_spec], out_shape=[SDS(out_shape, out_dtype)],
        scratch_shapes=[] if nk == 1 else [pltpu.VMEM((bm, bn), F32)], name=name, args=(a, b),
        semantics=("parallel", "parallel", "parallel", "arbitrary"))
    return res[0] if comm is None else (res[0], got)


def _rows(tm, n, col=0):
    return pl.BlockSpec((tm, n), lambda i: (i, col))


def _vec(n):
    return pl.BlockSpec((1, n), lambda i: (0, 0))


def _tm(S):
    return min(S, 256)


def _norm_mod_fwd(x, g, sc, sh, name):
    S, Dm = x.shape
    tm = _tm(S)

    def body(x_ref, g_ref, sc_ref, sh_ref, h_ref):
        xv = x_ref[...]
        r = lax.rsqrt(jnp.mean(xv * xv, axis=-1, keepdims=True) + EPS)
        h_ref[...] = ((xv * r) * g_ref[...] * (1.0 + sc_ref[...]) + sh_ref[...]).astype(h_ref.dtype)

    return pl.pallas_call(
        body, grid=(S // tm,), in_specs=[_rows(tm, Dm), _vec(Dm), _vec(Dm), _vec(Dm)],
        out_specs=_rows(tm, Dm), out_shape=SDS((S, Dm), BF16),
        compiler_params=_cparams(("parallel",)), name=name)(x, g, sc, sh)


def _norm_mod_bwd(x, dh_list, dres, g, sc, name):
    S, Dm = x.shape
    tm = _tm(S)
    nh = len(dh_list)

    def body(*refs):
        x_ref = refs[0]
        dh_refs = refs[1:1 + nh]
        dres_ref, g_ref, sc_ref, dx_ref, dsc_ref, dsh_ref, dg_ref = refs[1 + nh:]
        i = pl.program_id(0)

        @pl.when(i == 0)
        def _():
            dsc_ref[...] = jnp.zeros_like(dsc_ref)
            dsh_ref[...] = jnp.zeros_like(dsh_ref)
            dg_ref[...] = jnp.zeros_like(dg_ref)

        xv = x_ref[...]
        dh = dh_refs[0][...]
        for r_ in dh_refs[1:]:
            dh = dh + r_[...]
        gv = g_ref[...]
        r = lax.rsqrt(jnp.mean(xv * xv, axis=-1, keepdims=True) + EPS)
        xn = xv * r
        xg = xn * gv
        dsh_ref[...] += jnp.sum(dh, axis=0, keepdims=True)
        dsc_ref[...] += jnp.sum(dh * xg, axis=0, keepdims=True)
        dxg = dh * (1.0 + sc_ref[...])
        dg_ref[...] += jnp.sum(dxg * xn, axis=0, keepdims=True)
        dxn = dxg * gv
        dx_ref[...] = dres_ref[...] + r * (dxn - xn * jnp.mean(dxn * xn, axis=-1, keepdims=True))

    return pl.pallas_call(
        body, grid=(S // tm,),
        in_specs=[_rows(tm, Dm)] * (2 + nh) + [_vec(Dm), _vec(Dm)],
        out_specs=[_rows(tm, Dm), _vec(Dm), _vec(Dm), _vec(Dm)],
        out_shape=[SDS((S, Dm), F32), SDS((1, Dm), F32), SDS((1, Dm), F32), SDS((1, Dm), F32)],
        compiler_params=_cparams(("arbitrary",)), name=name)(x, *dh_list, dres, g, sc)


def _resid_fwd(x, val, g, name):
    S, Dm = x.shape
    tm = _tm(S)

    def body(x_ref, v_ref, g_ref, o_ref):
        o_ref[...] = x_ref[...] + g_ref[...] * v_ref[...]

    return pl.pallas_call(
        body, grid=(S // tm,), in_specs=[_rows(tm, Dm), _rows(tm, Dm), _vec(Dm)],
        out_specs=_rows(tm, Dm), out_shape=SDS((S, Dm), F32),
        compiler_params=_cparams(("parallel",)), name=name)(x, val, g)


def _resid_bwd(dx, val, g, name):
    S, Dm = dx.shape
    tm = _tm(S)

    def body(dx_ref, v_ref, g_ref, dg_ref, dv_ref):
        @pl.when(pl.program_id(0) == 0)
        def _():
            dg_ref[...] = jnp.zeros_like(dg_ref)

        dxv = dx_ref[...]
        dg_ref[...] += jnp.sum(dxv * v_ref[...], axis=0, keepdims=True)
        dv_ref[...] = (dxv * g_ref[...]).astype(dv_ref.dtype)

    return pl.pallas_call(
        body, grid=(S // tm,), in_specs=[_rows(tm, Dm), _rows(tm, Dm), _vec(Dm)],
        out_specs=[_vec(Dm), _rows(tm, Dm)], out_shape=[SDS((1, Dm), F32), SDS((S, Dm), BF16)],
        compiler_params=_cparams(("arbitrary",)), name=name)(dx, val, g)


def _merge_fwd(y, gf, name):
    S = y.shape[0]
    tm = _tm(S)
    W = 3 * D_MODEL

    def body(y_ref, g_ref, o_ref):
        acc = None
        for k in range(3):
            sl = slice(k * D_MODEL, (k + 1) * D_MODEL)
            t = _sigmoid(g_ref[:, sl]) * y_ref[:, sl]
            acc = t if acc is None else acc + t
        o_ref[...] = acc.astype(o_ref.dtype)

    return pl.pallas_call(
        body, grid=(S // tm,), in_specs=[_rows(tm, W), _rows(tm, W)],
        out_specs=_rows(tm, D_MODEL), out_shape=SDS((S, D_MODEL), BF16),
        compiler_params=_cparams(("parallel",)), name=name)(y, gf)


def _merge_bwd(dm, y, gf, name):
    S = y.shape[0]
    tm = _tm(S)
    W = 3 * D_MODEL

    def body(dm_ref, y_ref, g_ref, dy_ref, dg_ref):
        dmv = dm_ref[...]
        for k in range(3):
            sl = slice(k * D_MODEL, (k + 1) * D_MODEL)
            sg = _sigmoid(g_ref[:, sl])
            dy_ref[:, sl] = (dmv * sg).astype(dy_ref.dtype)
            dg_ref[:, sl] = (dmv * y_ref[:, sl] * (sg * (1.0 - sg))).astype(dg_ref.dtype)

    return pl.pallas_call(
        body, grid=(S // tm,), in_specs=[_rows(tm, D_MODEL), _rows(tm, W), _rows(tm, W)],
        out_specs=[_rows(tm, W), _rows(tm, W)], out_shape=[SDS((S, W), BF16), SDS((S, W), BF16)],
        compiler_params=_cparams(("parallel",)), name=name)(dm, y, gf)


def _swiglu_fwd(u, name):
    S = u.shape[0]
    tm = _tm(S)

    def body(g_ref, u_ref, a_ref):
        gv = g_ref[...]
        a_ref[...] = (gv * _sigmoid(gv) * u_ref[...]).astype(a_ref.dtype)

    return pl.pallas_call(
        body, grid=(S // tm,), in_specs=[_rows(tm, FFN_H, 0), _rows(tm, FFN_H, 1)],
        out_specs=_rows(tm, FFN_H), out_shape=SDS((S, FFN_H), BF16),
        compiler_params=_cparams(("parallel",)), name=name)(u, u)


def _swiglu_bwd(da, u, name):
    S = u.shape[0]
    tm = _tm(S)

    def body(da_ref, g_ref, u_ref, du_ref):
        dav = da_ref[...]
        gv = g_ref[...]
        sg = _sigmoid(gv)
        du_ref[:, 0:FFN_H] = (dav * u_ref[...] * (sg * (1.0 + gv * (1.0 - sg)))).astype(du_ref.dtype)
        du_ref[:, FFN_H:2 * FFN_H] = (dav * (gv * sg)).astype(du_ref.dtype)

    return pl.pallas_call(
        body, grid=(S // tm,), in_specs=[_rows(tm, FFN_H), _rows(tm, FFN_H, 0), _rows(tm, FFN_H, 1)],
        out_specs=_rows(tm, 2 * FFN_H), out_shape=SDS((S, 2 * FFN_H), BF16),
        compiler_params=_cparams(("parallel",)), name=name)(da, u, u)


def _final_loss(x, g, target, name):
    S, Dm = x.shape
    tm = _tm(S)

    def body(x_ref, g_ref, t_ref, loss_ref, dx_ref, dg_ref):
        @pl.when(pl.program_id(0) == 0)
        def _():
            loss_ref[...] = jnp.zeros_like(loss_ref)
            dg_ref[...] = jnp.zeros_like(dg_ref)

        xv = x_ref[...]
        gv = g_ref[...]
        r = lax.rsqrt(jnp.mean(xv * xv, axis=-1, keepdims=True) + EPS)
        xn = xv * r
        err = xn * gv - t_ref[...]
        row = jnp.mean(err * err, axis=-1, keepdims=True)
        loss_ref[...] += 0.5 * jnp.sum(row, axis=0, keepdims=True)
        dy = err * (1.0 / Dm)
        dg_ref[...] += jnp.sum(dy * xn, axis=0, keepdims=True)
        dxn = dy * gv
        dx_ref[...] = r * (dxn - xn * jnp.mean(dxn * xn, axis=-1, keepdims=True))

    return pl.pallas_call(
        body, grid=(S // tm,), in_specs=[_rows(tm, Dm), _vec(Dm), _rows(tm, Dm)],
        out_specs=[pl.BlockSpec((1, 1), lambda i: (0, 0)), _rows(tm, Dm), _vec(Dm)],
        out_shape=[SDS((1, 1), F32), SDS((S, Dm), F32), SDS((1, Dm), F32)],
        compiler_params=_cparams(("arbitrary",)), name=name)(x, g, target)


PAIR = 2 * CHUNK


def _bandT_softmax(kg, qTg, bias, sink, valid):
    s = jnp.dot(kg, qTg, preferred_element_type=F32)
    s = jnp.where(valid, s + bias, NEG_INF)
    m = jnp.maximum(jnp.max(s, axis=0, keepdims=True), sink)
    e = jnp.exp(s - m)
    es = jnp.exp(sink - m)
    inv = 1.0 / (jnp.sum(e, axis=0, keepdims=True) + es)
    return e * inv, es * inv


def _pad_copy_rows(dst, src, pad, S):
    dst[:, 0:pad, :] = jnp.zeros((dst.shape[0], pad, dst.shape[2]), dst.dtype)
    dst[:, pad:pad + S, :] = src[...]


def _pad_copy_lanes(dst, src, pad, S):
    dst[:, 0:pad] = jnp.zeros((dst.shape[0], pad), dst.dtype)
    dst[:, pad:pad + S] = src[...]


def _fm(arg):
    return arg if isinstance(arg, tuple) else (arg, 0)


def _fm_spec(rows, S, row0):
    off, rem = divmod(row0, rows)
    assert rem == 0
    return pl.BlockSpec((rows, S), lambda i: (off + i, 0))


def _bandT_fwd(qT, k_h, vT, bias, sink, *, GQ, GK, P, kvoff, name, comm=None):
    (qT, q0), (vT, v0) = _fm(qT), _fm(vT)
    S = qT.shape[1]
    ng = bias.shape[0] // GQ
    BU = (P + 2) * CHUNK
    pad = P * CHUNK
    npair = S // PAIR

    def body(qT_ref, k_ref, vT_ref, b_ref, s_ref, oT_ref, kp, vTp):
        _pad_copy_rows(kp, k_ref, pad, S)
        _pad_copy_lanes(vTp, vT_ref, pad, S)
        rowi = lax.broadcasted_iota(jnp.int32, (BU, PAIR), 0)

        def step(n2, carry):
            r = pl.multiple_of(n2 * PAIR, PAIR)
            valid = rowi >= (P - 2 * n2) * CHUNK
            for g in range(GQ):
                kv = kvoff(g)
                hs = slice(g * HEAD_DIM, (g + 1) * HEAD_DIM)
                kvs = slice(kv * HEAD_DIM, (kv + 1) * HEAD_DIM)
                qTg = qT_ref[hs, pl.ds(r, PAIR)] * 0.125
                p, _ = _bandT_softmax(kp[kv, pl.ds(r, BU), :], qTg, b_ref[g], s_ref[g, 0:1, :], valid)
                oTg = jnp.dot(vTp[kvs, pl.ds(r, BU)], p.astype(BF16), preferred_element_type=F32)
                oT_ref[hs, pl.ds(r, PAIR)] = oTg.astype(oT_ref.dtype)
            return carry

        lax.fori_loop(0, npair, step, 0, unroll=min(2, npair))

    res, got = _call_hosting(
        body, comm=comm, grid=(ng,),
        in_specs=[_fm_spec(GQ * HEAD_DIM, S, q0),
                  pl.BlockSpec((GK, S, HEAD_DIM), lambda i: (i, 0, 0)),
                  _fm_spec(GK * HEAD_DIM, S, v0),
                  pl.BlockSpec((GQ, BU, PAIR), lambda i: (i, 0, 0)),
                  pl.BlockSpec((GQ, 8, LANE), lambda i: (i, 0, 0))],
        out_specs=[pl.BlockSpec((GQ * HEAD_DIM, S), lambda i: (i, 0))],
        out_shape=[SDS((ng * GQ * HEAD_DIM, S), BF16)],
        scratch_shapes=[pltpu.VMEM((GK, S + pad, HEAD_DIM), BF16), pltpu.VMEM((GK * HEAD_DIM, S + pad), BF16)],
        name=name, args=(qT, k_h, vT, bias, sink))
    return res[0], got


def _bandT_bwd(qT, q_h, k_h, kT, v_h, doT, do_h, bias, sink, *, GQ, GK, P, kvoff, name, comm=None):
    (qT, q0), (kT, k0), (doT, d0) = _fm(qT), _fm(kT), _fm(doT)
    S = qT.shape[1]
    ng = bias.shape[0] // GQ
    BU = (P + 2) * CHUNK
    pad = P * CHUNK
    npair = S // PAIR

    def body(qT_ref, q_ref, k_ref, kT_ref, v_ref, doT_ref, do_ref, b_ref, s_ref,
             dqT_ref, dk_ref, dv_ref, db_ref, dsk_ref, kp, kTp, vp, dkp, dvp):
        _pad_copy_rows(kp, k_ref, pad, S)
        _pad_copy_rows(vp, v_ref, pad, S)
        _pad_copy_lanes(kTp, kT_ref, pad, S)
        dkp[...] = jnp.zeros_like(dkp)
        dvp[...] = jnp.zeros_like(dvp)
        db_ref[...] = jnp.zeros_like(db_ref)
        rowi = lax.broadcasted_iota(jnp.int32, (BU, PAIR), 0)

        def step(n2, dsink):
            r = pl.multiple_of(n2 * PAIR, PAIR)
            valid = rowi >= (P - 2 * n2) * CHUNK
            new = []
            for g in range(GQ):
                kv = kvoff(g)
                hs = slice(g * HEAD_DIM, (g + 1) * HEAD_DIM)
                kvs = slice(kv * HEAD_DIM, (kv + 1) * HEAD_DIM)
                qTg = qT_ref[hs, pl.ds(r, PAIR)] * 0.125
                p, ps = _bandT_softmax(kp[kv, pl.ds(r, BU), :], qTg, b_ref[g], s_ref[g, 0:1, :], valid)
                dp = jnp.dot(vp[kv, pl.ds(r, BU), :], doT_ref[hs, pl.ds(r, PAIR)], preferred_element_type=F32)
                delta = jnp.sum(p * dp, axis=0, keepdims=True)
                ds = p * (dp - delta)
                new.append(dsink[g] - ps * delta)
                db_ref[g] += ds
                dsb = ds.astype(BF16)
                dq = jnp.dot(kTp[kvs, pl.ds(r, BU)], dsb, preferred_element_type=F32) * 0.125
                dqT_ref[hs, pl.ds(r, PAIR)] = dq.astype(dqT_ref.dtype)
                dkp[kv, pl.ds(r, BU), :] += jnp.dot(dsb, q_ref[g, pl.ds(r, PAIR), :] * 0.125,
                                                    preferred_element_type=F32)
                dvp[kv, pl.ds(r, BU), :] += jnp.dot(p.astype(BF16), do_ref[g, pl.ds(r, PAIR), :],
                                                    preferred_element_type=F32)
            return tuple(new)

        dsink = lax.fori_loop(0, npair, step, tuple(jnp.zeros((1, PAIR), F32) for _ in range(GQ)))
        for g in range(GQ):
            dsk_ref[g] = jnp.broadcast_to(jnp.sum(dsink[g], axis=1, keepdims=True), (8, LANE))
        dk_ref[...] = dkp[:, pad:pad + S, :].astype(dk_ref.dtype)
        dv_ref[...] = dvp[:, pad:pad + S, :].astype(dv_ref.dtype)

    qTs = pl.BlockSpec((GQ * HEAD_DIM, S), lambda i: (i, 0))
    qhs = pl.BlockSpec((GQ, S, HEAD_DIM), lambda i: (i, 0, 0))
    khs = pl.BlockSpec((GK, S, HEAD_DIM), lambda i: (i, 0, 0))
    bs = pl.BlockSpec((GQ, BU, PAIR), lambda i: (i, 0, 0))
    ss = pl.BlockSpec((GQ, 8, LANE), lambda i: (i, 0, 0))
    nkv = ng * GK
    return _call_hosting(
        body, comm=comm, grid=(ng,),
        in_specs=[_fm_spec(GQ * HEAD_DIM, S, q0), qhs, khs, _fm_spec(GK * HEAD_DIM, S, k0), khs,
                  _fm_spec(GQ * HEAD_DIM, S, d0), qhs, bs, ss],
        out_specs=[qTs, khs, khs, bs, ss],
        out_shape=[SDS((ng * GQ * HEAD_DIM, S), BF16), SDS((nkv, S, HEAD_DIM), BF16), SDS((nkv, S, HEAD_DIM), BF16),
                   SDS((ng * GQ, BU, PAIR), F32), SDS((ng * GQ, 8, LANE), F32)],
        scratch_shapes=[pltpu.VMEM((GK, S + pad, HEAD_DIM), BF16), pltpu.VMEM((GK * HEAD_DIM, S + pad), BF16),
                        pltpu.VMEM((GK, S + pad, HEAD_DIM), BF16),
                        pltpu.VMEM((GK, S + pad, HEAD_DIM), F32), pltpu.VMEM((GK, S + pad, HEAD_DIM), F32)],
        name=name, args=(qT, q_h, k_h, kT, v_h, doT, do_h, bias, sink))


def _pair_table(tab):
    t = jnp.transpose(tab, (0, 2, 1))
    lo = jnp.pad(t, ((0, 0), (0, CHUNK), (0, 0)), constant_values=NEG_INF)
    hi = jnp.pad(t, ((0, 0), (CHUNK, 0), (0, 0)), constant_values=NEG_INF)
    return jnp.concatenate([lo, hi], axis=2)


def _unpair_table(d):
    band = d.shape[1] - CHUNK
    return jnp.transpose(d[:, 0:band, 0:CHUNK] + d[:, CHUNK:CHUNK + band, CHUNK:PAIR], (0, 2, 1))


def _heads(a, n):
    return jnp.transpose(a.reshape(a.shape[0], n, HEAD_DIM), (1, 0, 2))


def _unheads(a):
    return jnp.transpose(a, (1, 0, 2)).reshape(a.shape[1], a.shape[0] * HEAD_DIM)


def _foxT_logits(kj, qTg, cq, ck, r, c, rowi, coli):
    s = jnp.dot(kj, qTg, preferred_element_type=F32)
    s = s + cq - ck
    return jnp.where(c + rowi <= r + coli, s, NEG_INF)


def _foxT_fwd(qT, k_h, vT, ck, cq, name, comm=None):
    (qT, q0), (vT, v0) = _fm(qT), _fm(vT)
    S = qT.shape[1]
    npair = k_h.shape[0] // 2
    BQ, BK = min(FOX_BQ, S), min(FOX_BK, S)
    nq = S // BQ
    heads = [slice(g * HEAD_DIM, (g + 1) * HEAD_DIM) for g in range(2)]

    def body(qT_ref, k_ref, vT_ref, ck_ref, cq_ref, oT_ref, lse_ref):
        rowi = lax.broadcasted_iota(jnp.int32, (BK, BQ), 0)
        coli = lax.broadcasted_iota(jnp.int32, (BK, BQ), 1)

        def qstep(i, carry):
            r = pl.multiple_of(i * BQ, BQ)
            qs = [qT_ref[hs, pl.ds(r, BQ)] * 0.125 for hs in heads]
            cqs = [cq_ref[g, :, pl.ds(r, BQ)] for g in range(2)]

            def kstep(j, st):
                c = pl.multiple_of(j * BK, BK)
                new = []
                for g, hs in enumerate(heads):
                    m, l, acc = st[g]
                    s = _foxT_logits(k_ref[g, pl.ds(c, BK), :], qs[g], cqs[g], ck_ref[g, pl.ds(c, BK), :],
                                     r, c, rowi, coli)
                    mn = jnp.maximum(m, jnp.max(s, axis=0, keepdims=True))
                    al = jnp.exp(m - mn)
                    e = jnp.exp(s - mn)
                    l = al * l + jnp.sum(e, axis=0, keepdims=True)
                    acc = al * acc + jnp.dot(vT_ref[hs, pl.ds(c, BK)], e.astype(BF16), preferred_element_type=F32)
                    new.append((mn, l, acc))
                return tuple(new)

            init = (jnp.full((1, BQ), NEG_INF, F32), jnp.zeros((1, BQ), F32), jnp.zeros((HEAD_DIM, BQ), F32))
            st = lax.fori_loop(0, (r + BQ + BK - 1) // BK, kstep, (init, init))
            for g, hs in enumerate(heads):
                m, l, acc = st[g]
                oT_ref[hs, pl.ds(r, BQ)] = (acc * (1.0 / l)).astype(oT_ref.dtype)
                lse_ref[g, :, pl.ds(r, BQ)] = m + jnp.log(l)
            return carry

        lax.fori_loop(0, nq, qstep, 0)

    fT = pl.BlockSpec((LANE, S), lambda i: (i, 0))
    hm = pl.BlockSpec((2, S, HEAD_DIM), lambda i: (i, 0, 0))
    col = pl.BlockSpec((2, S, 1), lambda i: (i, 0, 0))
    rw = pl.BlockSpec((2, 1, S), lambda i: (i, 0, 0))
    return _call_hosting(
        body, comm=comm, grid=(npair,), in_specs=[_fm_spec(LANE, S, q0), hm, _fm_spec(LANE, S, v0), col, rw],
        out_specs=[fT, rw],
        out_shape=[SDS((npair * LANE, S), BF16), SDS((2 * npair, 1, S), F32)], scratch_shapes=[],
        name=name, args=(qT, k_h, vT, ck, cq))


def _foxT_bwd(qT, q_h, k_h, kT, v_h, ck, cq, oT, doT, do_h, lse, name, comm=None):
    (qT, q0), (kT, k0), (doT, d0) = _fm(qT), _fm(kT), _fm(doT)
    S = qT.shape[1]
    npair = k_h.shape[0] // 2
    BQ, BK = min(FOX_BQ, S), min(FOX_BK, S)
    nq = S // BQ
    heads = [slice(g * HEAD_DIM, (g + 1) * HEAD_DIM) for g in range(2)]

    def body(qT_ref, q_ref, k_ref, kT_ref, v_ref, ck_ref, cq_ref, oT_ref, doT_ref, do_ref, lse_ref,
             dqT_ref, dk_ref, dv_ref, dck_ref, dcq_ref, dka, dva, qa_ref):
        qa_ref[:, :, 0:HEAD_DIM] = q_ref[...] * 0.125
        qa_ref[:, :, HEAD_DIM:LANE] = jnp.ones((2, S, LANE - HEAD_DIM), BF16)
        dka[...] = jnp.zeros_like(dka)
        dva[...] = jnp.zeros_like(dva)
        rowi = lax.broadcasted_iota(jnp.int32, (BK, BQ), 0)
        coli = lax.broadcasted_iota(jnp.int32, (BK, BQ), 1)

        def qstep(i, carry):
            r = pl.multiple_of(i * BQ, BQ)
            qs = [qT_ref[hs, pl.ds(r, BQ)] * 0.125 for hs in heads]
            dos = [doT_ref[hs, pl.ds(r, BQ)] for hs in heads]
            deltas = [jnp.sum(dos[g].astype(F32) * oT_ref[hs, pl.ds(r, BQ)].astype(F32), axis=0, keepdims=True)
                      for g, hs in enumerate(heads)]
            cqs = [cq_ref[g, :, pl.ds(r, BQ)] for g in range(2)]
            lses = [lse_ref[g, :, pl.ds(r, BQ)] for g in range(2)]

            def kstep(j, st):
                c = pl.multiple_of(j * BK, BK)
                new = []
                for g, hs in enumerate(heads):
                    dq, rs = st[g]
                    s = _foxT_logits(k_ref[g, pl.ds(c, BK), :], qs[g], cqs[g], ck_ref[g, pl.ds(c, BK), :],
                                     r, c, rowi, coli)
                    p = jnp.exp(s - lses[g])
                    dp = jnp.dot(v_ref[g, pl.ds(c, BK), :], dos[g], preferred_element_type=F32)
                    ds = p * (dp - deltas[g])
                    dsb = ds.astype(BF16)
                    dka[g, pl.ds(c, BK), :] += jnp.dot(dsb, qa_ref[g, pl.ds(r, BQ), :], preferred_element_type=F32)
                    dva[g, pl.ds(c, BK), :] += jnp.dot(p.astype(BF16), do_ref[g, pl.ds(r, BQ), :],
                                                      preferred_element_type=F32)
                    new.append((dq + jnp.dot(kT_ref[hs, pl.ds(c, BK)], dsb, preferred_element_type=F32),
                                rs + jnp.sum(dsb.astype(F32), axis=0, keepdims=True)))
                return tuple(new)

            init = (jnp.zeros((HEAD_DIM, BQ), F32), jnp.zeros((1, BQ), F32))
            st = lax.fori_loop(0, (r + BQ + BK - 1) // BK, kstep, (init, init))
            for g, hs in enumerate(heads):
                dqT_ref[hs, pl.ds(r, BQ)] = (st[g][0] * 0.125).astype(dqT_ref.dtype)
                dcq_ref[g, :, pl.ds(r, BQ)] = st[g][1]
            return carry

        lax.fori_loop(0, nq, qstep, 0)
        dk_ref[...] = dka[:, :, 0:HEAD_DIM].astype(dk_ref.dtype)
        dck_ref[...] = -dka[:, :, HEAD_DIM:HEAD_DIM + 1]
        dv_ref[...] = dva[...].astype(dv_ref.dtype)

    fT = pl.BlockSpec((LANE, S), lambda i: (i, 0))
    hm = pl.BlockSpec((2, S, HEAD_DIM), lambda i: (i, 0, 0))
    col = pl.BlockSpec((2, S, 1), lambda i: (i, 0, 0))
    rw = pl.BlockSpec((2, 1, S), lambda i: (i, 0, 0))
    nh = 2 * npair
    return _call_hosting(
        body, comm=comm, grid=(npair,),
        in_specs=[_fm_spec(LANE, S, q0), hm, hm, _fm_spec(LANE, S, k0), hm, col, rw, fT, _fm_spec(LANE, S, d0), hm, rw],
        out_specs=[fT, hm, hm, col, rw],
        out_shape=[SDS((npair * LANE, S), BF16), SDS((nh, S, HEAD_DIM), BF16), SDS((nh, S, HEAD_DIM), BF16),
                   SDS((nh, S, 1), F32), SDS((nh, 1, S), F32)],
        scratch_shapes=[pltpu.VMEM((2, S, LANE), F32), pltpu.VMEM((2, S, HEAD_DIM), F32),
                        pltpu.VMEM((2, S, LANE), BF16)],
        name=name, args=(qT, q_h, k_h, kT, v_h, ck, cq, oT, doT, do_h, lse))


def _split3(x):
    hi = x.astype(BF16)
    r1 = x - hi.astype(F32)
    mid = r1.astype(BF16)
    lo = (r1 - mid.astype(F32)).astype(BF16)
    return hi, mid, lo


def _tri_dot(tri, x):
    hi, mid, lo = _split3(x)
    return (jnp.dot(tri, hi, preferred_element_type=F32) + jnp.dot(tri, mid, preferred_element_type=F32)
            + jnp.dot(tri, lo, preferred_element_type=F32))


def _fox_cum(gf, bfo, name):
    S = gf.shape[0]
    nb = S // LANE
    fcol = (GF_COLS - LANE) // LANE

    def body(f_ref, b_ref, cum_ref):
        row = lax.broadcasted_iota(jnp.int32, (LANE, LANE), 0)
        col = lax.broadcasted_iota(jnp.int32, (LANE, LANE), 1)
        tri = jnp.where(row >= col, 1.0, 0.0).astype(BF16)
        carry = jnp.zeros((1, LANE), F32)
        for t in range(nb):
            xl = f_ref[t * LANE:(t + 1) * LANE, :] + b_ref[...]
            lf = jnp.minimum(xl, 0.0) - jnp.log(1.0 + jnp.exp(-jnp.abs(xl)))
            cblk = _tri_dot(tri, lf) + carry
            cum_ref[t * LANE:(t + 1) * LANE, :] = cblk
            carry = cblk[LANE - 1:LANE, :]

    return pl.pallas_call(
        body, grid=(1,), in_specs=[pl.BlockSpec((S, LANE), lambda i: (0, fcol)), _vec(LANE)],
        out_specs=pl.BlockSpec((S, LANE), lambda i: (0, 0)), out_shape=SDS((S, LANE), F32),
        compiler_params=_cparams(("arbitrary",)), name=name)(gf, bfo)


def _fox_cum_bwd(gf, bfo, dcum, name):
    S = gf.shape[0]
    nb = S // LANE
    fcol = (GF_COLS - LANE) // LANE

    def body(f_ref, b_ref, dc_ref, df_ref, db_ref):
        row = lax.broadcasted_iota(jnp.int32, (LANE, LANE), 0)
        col = lax.broadcasted_iota(jnp.int32, (LANE, LANE), 1)
        tri = jnp.where(row <= col, 1.0, 0.0).astype(BF16)
        carry = jnp.zeros((1, LANE), F32)
        tot = jnp.zeros((1, LANE), F32)
        for t in range(nb - 1, -1, -1):
            rows = slice(t * LANE, (t + 1) * LANE)
            dlf = _tri_dot(tri, dc_ref[rows, :]) + carry
            carry = dlf[0:1, :]
            xl = f_ref[rows, :] + b_ref[...]
            dfl = dlf * (1.0 / (1.0 + jnp.exp(xl)))
            df_ref[rows, :] = dfl.astype(df_ref.dtype)
            tot = tot + jnp.sum(dfl, axis=0, keepdims=True)
        db_ref[...] = tot

    return pl.pallas_call(
        body, grid=(1,),
        in_specs=[pl.BlockSpec((S, LANE), lambda i: (0, fcol)), _vec(LANE), pl.BlockSpec((S, LANE), lambda i: (0, 0))],
        out_specs=[pl.BlockSpec((S, LANE), lambda i: (0, 0)), _vec(LANE)],
        out_shape=[SDS((S, LANE), BF16), SDS((1, LANE), F32)],
        compiler_params=_cparams(("arbitrary",)), name=name)(gf, bfo, dcum)


REL_FAR = C_PREV * CHUNK - REL_CLIP


def _rel_onehot(qi, band):
    w = band - REL_FAR
    r = lax.broadcasted_iota(jnp.int32, (N_REL_PAD, w), 0)
    j = lax.broadcasted_iota(jnp.int32, (N_REL_PAD, w), 1) + REL_FAR
    idx = jnp.clip(C_PREV * CHUNK + qi - j, -REL_CLIP, REL_CLIP) + REL_CLIP
    return jnp.where(r == idx, 1.0, 0.0).astype(BF16)


def _rel_expand(rel, name):
    band = (C_PREV + 1) * CHUNK

    def body(rel_ref, o_ref):
        hi, mid, lo = _split3(rel_ref[...])
        far = jnp.broadcast_to(rel_ref[:, 2 * REL_CLIP:2 * REL_CLIP + 1], (N_HEADS, REL_FAR))

        def row(qi, carry):
            oh = _rel_onehot(qi, band)
            o_ref[qi, :, 0:REL_FAR] = far
            o_ref[qi, :, REL_FAR:band] = (jnp.dot(hi, oh, preferred_element_type=F32)
                                          + jnp.dot(mid, oh, preferred_element_type=F32)
                                          + jnp.dot(lo, oh, preferred_element_type=F32))
            return carry

        lax.fori_loop(0, CHUNK, row, 0, unroll=2)

    return pl.pallas_call(
        body, grid=(1,), in_specs=[pl.BlockSpec((N_HEADS, N_REL_PAD), lambda i: (0, 0))],
        out_specs=pl.BlockSpec((CHUNK, N_HEADS, band), lambda i: (0, 0, 0)),
        out_shape=SDS((CHUNK, N_HEADS, band), F32),
        compiler_params=_cparams(("arbitrary",)), name=name)(rel)


def _rel_reduce(dbias, name):
    band = (C_PREV + 1) * CHUNK
    NT = (((1,), (1,)), ((), ()))

    def body(d_ref, o_ref):
        def row(qi, st):
            acc, far = st
            oh = _rel_onehot(qi, band)
            hi, mid, lo = _split3(d_ref[qi, :, REL_FAR:band])
            acc = acc + (lax.dot_general(hi, oh, NT, preferred_element_type=F32)
                         + lax.dot_general(mid, oh, NT, preferred_element_type=F32)
                         + lax.dot_general(lo, oh, NT, preferred_element_type=F32))
            return acc, far + jnp.sum(d_ref[qi, :, 0:REL_FAR], axis=-1, keepdims=True)

        acc, far = lax.fori_loop(0, CHUNK, row, (jnp.zeros((N_HEADS, N_REL_PAD), F32), jnp.zeros((N_HEADS, 1), F32)),
                                 unroll=2)
        col = lax.broadcasted_iota(jnp.int32, (N_HEADS, N_REL_PAD), 1)
        o_ref[...] = acc + jnp.where(col == 2 * REL_CLIP, far, 0.0)

    return pl.pallas_call(
        body, grid=(1,), in_specs=[pl.BlockSpec((CHUNK, N_HEADS, band), lambda i: (0, 0, 0))],
        out_specs=pl.BlockSpec((N_HEADS, N_REL_PAD), lambda i: (0, 0)),
        out_shape=SDS((N_HEADS, N_REL_PAD), F32),
        compiler_params=_cparams(("arbitrary",)), name=name)(dbias)


def _alibi_table():
    qi = np.arange(CHUNK)[:, None]
    j = np.arange((A_PREV + 1) * CHUNK)[None, :]
    dist = np.abs(A_PREV * CHUNK + qi - j).astype(np.float32)
    slopes = np.exp2(-8.0 * np.arange(1, N_HEADS + 1, dtype=np.float32) / N_HEADS).astype(np.float32)
    return jnp.asarray(-slopes[:, None, None] * dist[None])


def _ada_fwd(c_all, w, b, name):
    n = w.shape[2]

    def body(c_ref, w_ref, b_ref, o_ref):
        cv = c_ref[...]
        cond = (cv * _sigmoid(cv)).astype(BF16)
        o_ref[0] = jnp.dot(cond, w_ref[0].astype(BF16), preferred_element_type=F32) + b_ref[0]

    return pl.pallas_call(
        body, grid=(DEPTH,),
        in_specs=[pl.BlockSpec((16, D_MODEL), lambda l: (0, 0)), pl.BlockSpec((1, D_MODEL, n), lambda l: (l, 0, 0)),
                  pl.BlockSpec((1, 1, n), lambda l: (l, 0, 0))],
        out_specs=pl.BlockSpec((1, 16, n), lambda l: (l, 0, 0)), out_shape=SDS((DEPTH, 16, n), F32),
        compiler_params=_cparams(("parallel",)), name=name)(c_all, w, b)


def _ada_bwd(c_t, dmod, name):
    n = dmod.shape[2]
    bn = _blk(n, 512)
    tr = 256

    def body(c_ref, d_ref, o_ref):
        cv = c_ref[...]
        cond = (cv * _sigmoid(cv)).astype(BF16).astype(F32)
        dm = d_ref[0].astype(BF16).astype(F32)
        acc = cond[:, 0:1] * dm[0:1, :]
        for b_ in range(1, 8):
            acc = acc + cond[:, b_:b_ + 1] * dm[b_:b_ + 1, :]
        o_ref[0] = acc

    return pl.pallas_call(
        body, grid=(DEPTH, D_MODEL // tr, n // bn),
        in_specs=[pl.BlockSpec((tr, 8), lambda l, i, j: (i, 0)), pl.BlockSpec((1, 8, bn), lambda l, i, j: (l, 0, j))],
        out_specs=pl.BlockSpec((1, tr, bn), lambda l, i, j: (l, i, j)), out_shape=SDS((DEPTH, D_MODEL, n), F32),
        compiler_params=_cparams(("parallel", "parallel", "parallel")), name=name)(c_t, dmod)


def _adamw(w, m, v, parts, name, comm=None):
    L, R, C = w.shape
    per_layer = isinstance(parts, (list, tuple))
    plist = list(parts) if per_layer else [parts]
    P = plist[0].shape[0]
    tr = _blk_rows(R, max(16, (1 << 18) // C))
    nr = R // tr
    c1 = 1.0 - ADAM_B1 ** ADAM_STEP
    c2 = 1.0 - ADAM_B2 ** ADAM_STEP

    def total(p_ref):
        g = p_ref[0].astype(F32)
        for k in range(1, P):
            g = g + p_ref[k].astype(F32)
        return g

    def body(w_ref, m_ref, v_ref, *rest):
        p_refs, (g_ref, d_ref, nm_ref, nv_ref) = rest[:len(plist)], rest[len(plist):]
        g = total(p_refs[0])
        for k in range(1, len(plist)):
            g = jnp.where(pl.program_id(0) == k, total(p_refs[k]), g)
        mn = ADAM_B1 * m_ref[0] + (1.0 - ADAM_B1) * g
        vn = ADAM_B2 * v_ref[0] + (1.0 - ADAM_B2) * (g * g)
        m_hat = mn / c1
        v_hat = vn / c2
        g_ref[0] = g
        nm_ref[0] = mn
        nv_ref[0] = vn
        d_ref[0] = -ADAM_LR * (m_hat / (jnp.sqrt(v_hat) + ADAM_EPS) + ADAM_WD * w_ref[0])

    rs = pl.BlockSpec((1, tr, C), lambda l, i: (l, i, 0))
    if per_layer:
        def layer_spec(k):
            return pl.BlockSpec((P, tr, C), lambda l, i: (0, jnp.where(l == k, i, 0), 0))
        pspecs = [layer_spec(k) for k in range(L)]
    else:
        pspecs = [pl.BlockSpec((P, tr, C), lambda l, i: (0, l * nr + i, 0))]
    return _call_hosting(
        body, comm=comm, grid=(L, nr), in_specs=[rs, rs, rs] + pspecs, out_specs=[rs, rs, rs, rs],
        out_shape=[SDS((L, R, C), F32)] * 4, scratch_shapes=[], name=name, args=(w, m, v, *plist))


def _blk_rows(R, cap):
    if R <= cap:
        return R
    best = None
    for t in range(16, cap + 1, 16):
        if R % t == 0:
            best = t
    assert best is not None, (R, cap)
    return best


def _add_cast_rows(g, t, name):
    Q, R, C = g.shape
    half = R // 2
    tr = _blk_rows(half, max(16, (1 << 19) // C))
    nb = half // tr

    def body(lo_ref, hi_ref, t_ref, o_ref):
        c = lax.axis_index("c")

        @pl.when(c == 0)
        def _():
            o_ref[...] = (lo_ref[...] + t_ref[...]).astype(o_ref.dtype)

        @pl.when(c == 1)
        def _():
            o_ref[...] = (hi_ref[...] + t_ref[...]).astype(o_ref.dtype)

    bs = pl.BlockSpec((1, tr, C), lambda q, i: (q, i, 0))
    hi = pl.BlockSpec((1, tr, C), lambda q, i: (q, nb + i, 0))
    return pl.pallas_call(
        body, grid=(Q, nb), in_specs=[bs, hi, bs], out_specs=bs, out_shape=SDS((Q, half, C), BF16),
        compiler_params=_cparams(("parallel", "parallel")), name=name)(g, g, t)


def _coords():
    return lax.axis_index("x"), lax.axis_index("y"), lax.axis_index("c")


def _flip(v, bit):
    return 1 - v if bit else v


def _all_gather8(v, name):
    R = v.shape[0]

    def body(v_ref, o_ref, send_sems, recv_sems):
        x, y, c = _coords()
        me = 4 * x + 2 * y + c
        o_ref[me] = v_ref[...]
        copies = []
        for k in range(1, 8):
            peer = (_flip(x, k & 4), _flip(y, k & 2), _flip(c, k & 1))
            cp = pltpu.make_async_remote_copy(
                src_ref=v_ref, dst_ref=o_ref.at[me], send_sem=send_sems.at[k - 1], recv_sem=recv_sems.at[k - 1],
                device_id=peer, device_id_type=MESH)
            cp.start()
            copies.append(cp)
        for cp in copies:
            cp.wait_recv()
        for cp in copies:
            cp.wait_send()

    return pl.pallas_call(
        body, in_specs=[VMEM_SPEC], out_specs=VMEM_SPEC, out_shape=SDS((8, R, LANE), v.dtype),
        scratch_shapes=[pltpu.SemaphoreType.DMA((7,)), pltpu.SemaphoreType.DMA((7,))],
        compiler_params=pltpu.CompilerParams(vmem_limit_bytes=VMEM_LIMIT), name=name)(v)


def _sibling_swap_rows(arrs, name):
    n = len(arrs)

    def body(*refs):
        in_refs, out_refs = refs[:n], refs[n:2 * n]
        send_sems, recv_sems = refs[2 * n:]
        x, y, c = _coords()
        copies = []
        for a in range(n):
            Q, R = in_refs[a].shape[0], in_refs[a].shape[1]
            half = R // 2
            src = in_refs[a].at[pl.ds(0, Q), pl.ds(pl.multiple_of((1 - c) * half, 16), half)]
            cp = pltpu.make_async_remote_copy(
                src_ref=src, dst_ref=out_refs[a], send_sem=send_sems.at[a], recv_sem=recv_sems.at[a],
                device_id=(x, y, 1 - c), device_id_type=MESH)
            cp.start()
            copies.append(cp)
        for cp in copies:
            cp.wait_recv()
        for cp in copies:
            cp.wait_send()

    return pl.pallas_call(
        body, in_specs=[ANY] * n, out_specs=[ANY] * n,
        out_shape=[SDS((a.shape[0], a.shape[1] // 2, a.shape[2]), a.dtype) for a in arrs],
        scratch_shapes=[pltpu.SemaphoreType.DMA((n,)), pltpu.SemaphoreType.DMA((n,))],
        name=name)(*arrs)


class _OwnerReduce:
    aliased = False

    def __init__(self, srcs, lay):
        self.srcs, self.lay, self.n = list(srcs), lay, len(srcs)
        self.out_shapes = [SDS(a.shape, a.dtype) for a in self.srcs]
        self.sem_shapes = [pltpu.SemaphoreType.DMA((self.n, 3)), pltpu.SemaphoreType.DMA((self.n, 3)),
                           pltpu.SemaphoreType.DMA((self.n,))]

    def _copies(self, src_refs, dst_refs, sems):
        ici_send, ici_recv, loc_sem = sems
        x, y, c = _coords()
        p = 2 * x + y
        local, remote = [], []
        for a in range(self.n):
            local.append(pltpu.make_async_copy(src_refs[a].at[p], dst_refs[a].at[p], loc_sem.at[a]))
            for k in range(1, 4):
                qx, qy = _flip(x, k & 2), _flip(y, k & 1)
                remote.append(pltpu.make_async_remote_copy(
                    src_ref=src_refs[a].at[2 * qx + qy], dst_ref=dst_refs[a].at[p], send_sem=ici_send.at[a, k - 1],
                    recv_sem=ici_recv.at[a, k - 1], device_id=(qx, qy, self.lay), device_id_type=MESH))
        return c, local, remote

    def start(self, src_refs, dst_refs, sems):
        c, local, remote = self._copies(src_refs, dst_refs, sems)

        @pl.when(c == self.lay)
        def _():
            for cp in local + remote:
                cp.start()

    def finish(self, src_refs, dst_refs, sems):
        c, local, remote = self._copies(src_refs, dst_refs, sems)

        @pl.when(c == self.lay)
        def _():
            for cp in remote:
                cp.wait_recv()
            for cp in remote:
                cp.wait_send()
            for cp in local:
                cp.wait()


def _call_hosting(body, *, comm, grid, in_specs, out_specs, out_shape, scratch_shapes, name, args, semantics=None):
    n_in, n_out, n_scr = len(args), len(out_shape), len(scratch_shapes)
    if comm is None:
        sem = semantics if semantics is not None else ("parallel",) * len(grid)
        res = pl.pallas_call(body, grid=grid, in_specs=in_specs, out_specs=out_specs, out_shape=out_shape,
                             scratch_shapes=scratch_shapes, compiler_params=_cparams(sem), name=name)(*args)
        return list(res), None
    k = comm.n

    def hosted(*refs):
        ins, cin = refs[:n_in], refs[n_in:n_in + k]
        outs = refs[n_in + k:n_in + k + n_out]
        cout = refs[n_in + k + n_out:n_in + 2 * k + n_out]
        scr = refs[n_in + 2 * k + n_out:n_in + 2 * k + n_out + n_scr]
        sems = refs[n_in + 2 * k + n_out + n_scr:]
        first = pl.program_id(0) == 0
        last = pl.program_id(0) == grid[0] - 1
        for d in range(1, len(grid)):
            first = jnp.logical_and(first, pl.program_id(d) == 0)
            last = jnp.logical_and(last, pl.program_id(d) == grid[d] - 1)

        @pl.when(first)
        def _():
            comm.start(cin, cout, sems)

        body(*ins, *outs, *scr)

        @pl.when(last)
        def _():
            comm.finish(cin, cout, sems)

    aliases = {n_in + j: n_out + j for j in range(k)} if comm.aliased else {}
    res = pl.pallas_call(
        hosted, grid=grid, in_specs=list(in_specs) + [ANY] * k, out_specs=list(out_specs) + [ANY] * k,
        out_shape=list(out_shape) + comm.out_shapes, scratch_shapes=list(scratch_shapes) + comm.sem_shapes,
        input_output_aliases=aliases, compiler_params=_cparams(("arbitrary",) * len(grid)),
        name=name)(*args, *comm.srcs)
    return list(res[:n_out]), list(res[n_out:])


class _RowHalfGather:
    aliased = False

    def __init__(self, srcs):
        self.srcs, self.n = list(srcs), len(srcs)
        self.out_shapes = [SDS((4,) + a.shape, a.dtype) for a in self.srcs]
        n = self.n
        self.sem_shapes = [pltpu.SemaphoreType.DMA((n, 3)), pltpu.SemaphoreType.DMA((n, 3)),
                           pltpu.SemaphoreType.DMA((n, 3)), pltpu.SemaphoreType.DMA((n, 3)),
                           pltpu.SemaphoreType.DMA((n,))]

    def _copies(self, src_refs, dst_refs, sems):
        ici_send, ici_recv, d2d_send, d2d_recv, loc_sem = sems
        x, y, c = _coords()
        p = 2 * x + y
        local, first, fwd = [], [], []
        for a in range(self.n):
            R = src_refs[a].shape[0] // 2
            half = pl.ds(pl.multiple_of(c * R, 16), R)
            local.append(pltpu.make_async_copy(src_refs[a], dst_refs[a].at[p], loc_sem.at[a]))
            for k in range(1, 4):
                qx, qy = _flip(x, k & 2), _flip(y, k & 1)
                first.append(pltpu.make_async_remote_copy(
                    src_ref=src_refs[a].at[half], dst_ref=dst_refs[a].at[p, half], send_sem=ici_send.at[a, k - 1],
                    recv_sem=ici_recv.at[a, k - 1], device_id=(qx, qy, c), device_id_type=MESH))
                slot = dst_refs[a].at[2 * qx + qy, half]
                fwd.append(pltpu.make_async_remote_copy(
                    src_ref=slot, dst_ref=slot, send_sem=d2d_send.at[a, k - 1], recv_sem=d2d_recv.at[a, k - 1],
                    device_id=(x, y, 1 - c), device_id_type=MESH))
        return local, first, fwd

    def start(self, src_refs, dst_refs, sems):
        local, first, _ = self._copies(src_refs, dst_refs, sems)
        for cp in local + first:
            cp.start()

    def finish(self, src_refs, dst_refs, sems):
        local, first, fwd = self._copies(src_refs, dst_refs, sems)
        for got, on in zip(first, fwd):
            got.wait_recv()
            on.start()
        for cp in fwd:
            cp.wait_recv()
        for cp in first + fwd:
            cp.wait_send()
        for cp in local:
            cp.wait()

    def run(self, name):
        return _run_exchange(self, name)


def _run_exchange(comm, name):
    n = comm.n

    def body(*refs):
        src_refs, dst_refs, sems = refs[:n], refs[n:2 * n], refs[2 * n:]
        comm.start(src_refs, dst_refs, sems)
        comm.finish(src_refs, dst_refs, sems)

    return pl.pallas_call(body, in_specs=[ANY] * n, out_specs=[ANY] * n, out_shape=comm.out_shapes,
                          scratch_shapes=comm.sem_shapes, name=name)(*comm.srcs)


class _RowHalfReduce:
    aliased = False

    def __init__(self, srcs):
        self.srcs, self.n = list(srcs), len(srcs)
        self.out_shapes = [SDS((4, 2 * a.shape[1], a.shape[2]), a.dtype) for a in self.srcs]
        n = self.n
        self.sem_shapes = [pltpu.SemaphoreType.DMA((n, 3)), pltpu.SemaphoreType.DMA((n, 3)),
                           pltpu.SemaphoreType.DMA((n, 4)), pltpu.SemaphoreType.DMA((n, 4)),
                           pltpu.SemaphoreType.DMA((n,))]

    def _copies(self, src_refs, dst_refs, sems):
        ici_send, ici_recv, d2d_send, d2d_recv, loc_sem = sems
        x, y, c = _coords()
        p = 2 * x + y
        local, first, fwd = [], [], []
        for a in range(self.n):
            R = src_refs[a].shape[1]
            half = pl.ds(pl.multiple_of(c * R, 16), R)
            local.append(pltpu.make_async_copy(src_refs[a].at[p], dst_refs[a].at[p, half], loc_sem.at[a]))
            for k in range(4):
                qx, qy = _flip(x, k & 2), _flip(y, k & 1)
                if k:
                    first.append(pltpu.make_async_remote_copy(
                        src_ref=src_refs[a].at[2 * qx + qy], dst_ref=dst_refs[a].at[p, half],
                        send_sem=ici_send.at[a, k - 1], recv_sem=ici_recv.at[a, k - 1], device_id=(qx, qy, c),
                        device_id_type=MESH))
                slot = dst_refs[a].at[2 * qx + qy, half]
                fwd.append(pltpu.make_async_remote_copy(
                    src_ref=slot, dst_ref=slot, send_sem=d2d_send.at[a, k], recv_sem=d2d_recv.at[a, k],
                    device_id=(x, y, 1 - c), device_id_type=MESH))
        return local, first, fwd

    def start(self, src_refs, dst_refs, sems):
        local, first, _ = self._copies(src_refs, dst_refs, sems)
        for cp in local + first:
            cp.start()

    def finish(self, src_refs, dst_refs, sems):
        local, first, fwd = self._copies(src_refs, dst_refs, sems)
        for a in range(self.n):
            local[a].wait()
            fwd[4 * a].start()
            for k in range(1, 4):
                first[3 * a + k - 1].wait_recv()
                fwd[4 * a + k].start()
        for cp in fwd:
            cp.wait_recv()
        for cp in first + fwd:
            cp.wait_send()

    def run(self, name):
        return _run_exchange(self, name)


class _SiblingSend:
    aliased = False

    def __init__(self, srcs, src_core):
        self.srcs, self.src_core, self.n = list(srcs), src_core, len(srcs)
        self.out_shapes = [SDS(a.shape, a.dtype) for a in self.srcs]
        self.sem_shapes = [pltpu.SemaphoreType.DMA((self.n,)), pltpu.SemaphoreType.DMA((self.n,))]

    def _copies(self, src_refs, dst_refs, sems):
        x, y, c = _coords()
        return c, [pltpu.make_async_remote_copy(
            src_ref=src_refs[a], dst_ref=dst_refs[a], send_sem=sems[0].at[a], recv_sem=sems[1].at[a],
            device_id=(x, y, 1 - c), device_id_type=MESH) for a in range(self.n)]

    def start(self, src_refs, dst_refs, sems):
        c, copies = self._copies(src_refs, dst_refs, sems)

        @pl.when(c == self.src_core)
        def _():
            for cp in copies:
                cp.start()

    def finish(self, src_refs, dst_refs, sems):
        c, copies = self._copies(src_refs, dst_refs, sems)

        @pl.when(c == self.src_core)
        def _():
            for cp in copies:
                cp.wait_send()

        @pl.when(c != self.src_core)
        def _():
            for cp in copies:
                cp.wait_recv()


class _Handoff:
    aliased = True

    def __init__(self, srcs, lay, slots):
        self.srcs, self.lay, self.slots, self.n = list(srcs), lay, tuple(slots), len(srcs)
        self.out_shapes = [SDS(a.shape, a.dtype) for a in self.srcs]
        ns = len(self.slots)
        self.sem_shapes = [pltpu.SemaphoreType.DMA((self.n, ns)), pltpu.SemaphoreType.DMA((self.n, ns))]

    def _copies(self, dst_refs, sems):
        x, y, c = _coords()
        copies = []
        for a in range(self.n):
            for j, k in enumerate(self.slots):
                slot = dst_refs[a].at[2 * _flip(x, k & 2) + _flip(y, k & 1)]
                copies.append(pltpu.make_async_remote_copy(
                    src_ref=slot, dst_ref=slot, send_sem=sems[0].at[a, j], recv_sem=sems[1].at[a, j],
                    device_id=(x, y, 1 - c), device_id_type=MESH))
        return c, copies

    def start(self, src_refs, dst_refs, sems):
        c, copies = self._copies(dst_refs, sems)

        @pl.when(c == self.lay)
        def _():
            for cp in copies:
                cp.start()

    def finish(self, src_refs, dst_refs, sems):
        c, copies = self._copies(dst_refs, sems)

        @pl.when(c == self.lay)
        def _():
            for cp in copies:
                cp.wait_send()

        @pl.when(c != self.lay)
        def _():
            for cp in copies:
                cp.wait_recv()


def _add_cast_on(a, b, lay, name):
    Q, R, C = b.shape
    tr = _blk_rows(R, max(16, (1 << 19) // C))

    def body(a_ref, b_ref, o_ref):
        @pl.when(lax.axis_index("c") == lay)
        def _():
            o_ref[...] = (a_ref[...] + b_ref[...]).astype(o_ref.dtype)

    bs = pl.BlockSpec((1, tr, C), lambda q, i: (q, i, 0))
    return pl.pallas_call(
        body, grid=(Q, R // tr), in_specs=[bs, bs], out_specs=bs, out_shape=SDS((Q, R, C), BF16),
        compiler_params=_cparams(("parallel", "parallel")), name=name)(a, b)


_IN_SIZES = (512, 128, 128, 512, 512, 512, 8, 512, 512, 512, 3072)
_IN_OFF = tuple(int(v) for v in np.cumsum((0,) + _IN_SIZES))
_IN_Q = N_IN_COLS // 4


def _pack_w_in(w):
    def cols(lo, hi):
        out = []
        while lo < hi:
            q, off = divmod(lo, _IN_Q)
            n = min(hi - lo, _IN_Q - off)
            out.append(w[q, :, off:off + n])
            lo += n
        return out

    fb0, fb1, g0 = _IN_OFF[6], _IN_OFF[7], _IN_OFF[10]
    wqkv = jnp.concatenate(cols(0, fb0) + cols(fb1, g0), axis=1)
    wgf = jnp.concatenate(cols(g0, N_IN_COLS) + cols(fb0, fb1) + [jnp.zeros((w.shape[1], LANE - 8), w.dtype)], axis=1)
    return wqkv, wgf


def _unpack_w_in(dqkv, dgf):
    fb0, fb1, g0 = _IN_OFF[6], _IN_OFF[7], _IN_OFF[10]

    def cols(lo, hi):
        out = []
        while lo < hi:
            if lo < fb0:
                n = min(hi, fb0) - lo
                out.append(dqkv[:, lo:lo + n])
            elif lo < fb1:
                n = min(hi, fb1) - lo
                out.append(dgf[:, 3072 + lo - fb0:3072 + lo - fb0 + n])
            elif lo < g0:
                n = min(hi, g0) - lo
                out.append(dqkv[:, lo - 8:lo - 8 + n])
            else:
                n = hi - lo
                out.append(dgf[:, lo - g0:lo - g0 + n])
            lo += n
        return out

    return jnp.stack([jnp.concatenate(cols(q * _IN_Q, (q + 1) * _IN_Q), axis=1) for q in range(4)])


def _pad_rows(a, rows):
    return jnp.pad(a, ((0, rows - a.shape[0]), (0, 0)))


def _small_pack(parts):
    flat = jnp.concatenate([p.reshape(-1) for p in parts])
    n = flat.shape[0]
    rows = -(-n // LANE)
    rows = -(-rows // 8) * 8
    return jnp.pad(flat, (0, rows * LANE - n)).reshape(rows, LANE)


def _small_unpack(block, shapes):
    flat = block.reshape(-1)
    out, off = [], 0
    for s in shapes:
        n = int(np.prod(s))
        out.append(flat[off:off + n].reshape(s))
        off += n
    return out


def _kv_same(g):
    return 0


def _kv_own(g):
    return g


_mm_plain = _mm


def _mm_hosting(a, b, *, comm, **kw):
    if comm is None:
        return _mm(a, b, **kw), None
    return _mm(a, b, comm=comm, **kw)


def _layer_fwd(x, mod, p, l, ride):
    sh_m, sc_m, g_m, sh_f, sc_f, g_f = mod
    nm = "l%d_" % l

    def carried(name, run):
        res, got = run(ride.comm_for(name))
        if got is not None:
            ride.done(name, got)
        return res

    h1 = _norm_mod_fwd(x, p["norm_mix_g"], sc_m, sh_m, nm + "norm_mix_fwd")
    qkv = carried("proj_qkv", lambda cm: _mm_hosting(h1, p["wqkv"], mode="nn", out_dtype=BF16,
                                                     name=nm + "proj_qkv", comm=cm))
    gf = _mm(h1, p["wgf"], mode="nn", out_dtype=F32, name=nm + "proj_gf", cap_n=640)
    qkv_t = qkv.T
    o_a_t = carried("attn_a", lambda cm: _bandT_fwd(
        (qkv_t, 0), _heads(qkv[:, 512:640], A_KV_HEADS), (qkv_t, 640), p["alibi"], p["sink_tab"],
        GQ=4, GK=1, P=A_PREV, kvoff=_kv_same, name=nm + "attn_a_fwd", comm=cm))
    cum = _fox_cum(gf, p["b_forget_pad"], nm + "fox_cum")
    cum_t = cum[:, :N_HEADS].T
    cc, cr = cum_t[:, :, None], cum_t[:, None, :]
    o_b_t, lse_b = carried("attn_b", lambda cm: _foxT_fwd(
        (qkv_t, 768), _heads(qkv[:, 1280:1792], N_HEADS), (qkv_t, 1792), cc, cr, nm + "attn_b_fwd", comm=cm))
    o_c_t = carried("attn_c", lambda cm: _bandT_fwd(
        (qkv_t, 2304), _heads(qkv[:, 2816:3328], N_HEADS), (qkv_t, 3328), p["rel_tab"], p["no_sink"],
        GQ=2, GK=2, P=C_PREV, kvoff=_kv_own, name=nm + "attn_c_fwd", comm=cm))
    p = dict(p, **ride.late_weights())
    o = jnp.concatenate([o_a_t, o_b_t, o_c_t], axis=0).T
    y = _mm(o, p["wb"], mode="nn", out_dtype=F32, groups=3, name=nm + "branch")
    merged = _merge_fwd(y, gf, nm + "merge_fwd")
    mix = _mm(merged, p["wout"], mode="nn", out_dtype=F32, name=nm + "out_proj")
    x1 = _resid_fwd(x, mix, g_m, nm + "resid_mix")
    h2 = _norm_mod_fwd(x1, p["norm_ffn_g"], sc_f, sh_f, nm + "norm_ffn_fwd")
    u = carried("ffn_in", lambda cm: _mm_hosting(h2, p["wfi"], mode="nn", out_dtype=F32, name=nm + "ffn_in",
                                                 cap_n=512, comm=cm))
    a = _swiglu_fwd(u, nm + "swiglu_fwd")
    f = _mm(a, p["wfo"], mode="nn", out_dtype=F32, name=nm + "ffn_out", cap_m=1024)
    x2 = _resid_fwd(x1, f, g_f, nm + "resid_ffn")
    saved = dict(x=x, h1=h1, qkv=qkv, qkv_t=qkv_t, gf=gf, cc=cc, cr=cr, o_b_t=o_b_t, lse_b=lse_b, o=o, y=y, merged=merged,
                 mix=mix, x1=x1, h2=h2, u=u, a=a, f=f)
    return x2, saved, p


def _layer_bwd(dx2, mod, p, s, l, ride=None):
    sh_m, sc_m, g_m, sh_f, sc_f, g_f = mod
    nm = "l%d_" % l

    def _mm(a, b, *, name, **kw):
        comm = ride.comm_for(name) if ride is not None else None
        if comm is None:
            return _mm_plain(a, b, name=nm + name, **kw)
        out, got = _mm_plain(a, b, name=nm + name, comm=comm, **kw)
        ride.done(name, got)
        return out

    dg_f, df = _resid_bwd(dx2, s["f"], g_f, nm + "resid_ffn_bwd")
    da = _mm(df, p["wfo"], mode="nt", out_dtype=F32, name="ffn_out_dx", cap_m=1024, cap_n=1408)
    d_wfo = _mm(s["a"], df, mode="tn", out_dtype=F32, name="ffn_out_dw", cap_m=1408, cap_k=2048)
    du = _swiglu_bwd(da, s["u"], nm + "swiglu_bwd")
    dh2 = _mm(du, p["wfi"], mode="nt", out_dtype=F32, name="ffn_in_dx", cap_m=1024)
    d_wfi = _mm(s["h2"], du, mode="tn", out_dtype=F32, name="ffn_in_dw", cap_m=1024, cap_n=1408, cap_k=2048,
                col_quarters=True)
    dx1, dsc_f, dsh_f, dgn_f = _norm_mod_bwd(s["x1"], [dh2], dx2, p["norm_ffn_g"], sc_f, nm + "norm_ffn_bwd")
    dg_m, dmix = _resid_bwd(dx1, s["mix"], g_m, nm + "resid_mix_bwd")
    dmerged = _mm(dmix, p["wout"], mode="nt", out_dtype=F32, name="out_proj_dx")
    d_wout = _mm(s["merged"], dmix, mode="tn", out_dtype=F32, name="out_proj_dw", cap_m=1024, cap_k=2048)
    dy, dgates = _merge_bwd(dmerged, s["y"], s["gf"], nm + "merge_bwd")
    do = _mm(dy, p["wb"], mode="nt", out_dtype=BF16, groups=3, name="branch_dx")
    d_wb = _mm(s["o"], dy, mode="tn", out_dtype=F32, groups=3, name="branch_dw", cap_k=2048,
               col_quarters=True)
    comms = ride.exchanges() if ride is not None else (None, None, None)
    qkv, qkv_t = s["qkv"], s["qkv_t"]
    do_t = do.T
    (dqa_t, dka_h, dva_h, _, dsink), got_a = _bandT_bwd(
        (qkv_t, 0), _heads(qkv[:, 0:512], N_HEADS), _heads(qkv[:, 512:640], A_KV_HEADS), (qkv_t, 512),
        _heads(qkv[:, 640:768], A_KV_HEADS), (do_t, 0), _heads(do[:, 0:512], N_HEADS), p["alibi"], p["sink_tab"],
        GQ=4, GK=1, P=A_PREV, kvoff=_kv_same, name=nm + "attn_a_bwd", comm=comms[0])
    (dqb_t, dkb_h, dvb_h, dck, dcq), got_b = _foxT_bwd(
        (qkv_t, 768), _heads(qkv[:, 768:1280], N_HEADS), _heads(qkv[:, 1280:1792], N_HEADS), (qkv_t, 1280),
        _heads(qkv[:, 1792:2304], N_HEADS), s["cc"], s["cr"], s["o_b_t"], (do_t, 512),
        _heads(do[:, 512:1024], N_HEADS), s["lse_b"], nm + "attn_b_bwd", comm=comms[1])
    dcum = jnp.pad((dck[:, :, 0] + dcq[:, 0, :]).T, ((0, 0), (0, LANE - N_HEADS)))
    dfb, db_forget = _fox_cum_bwd(s["gf"], p["b_forget_pad"], dcum, nm + "fox_cum_bwd")
    (dqc_t, dkc_h, dvc_h, dbias_c, _), got_c = _bandT_bwd(
        (qkv_t, 2304), _heads(qkv[:, 2304:2816], N_HEADS), _heads(qkv[:, 2816:3328], N_HEADS), (qkv_t, 2816),
        _heads(qkv[:, 3328:3840], N_HEADS), (do_t, 1024), _heads(do[:, 1024:1536], N_HEADS), p["rel_tab"],
        p["no_sink"], GQ=2, GK=2, P=C_PREV, kvoff=_kv_own, name=nm + "attn_c_bwd", comm=comms[2])
    d_rel = _rel_reduce(jnp.transpose(_unpair_table(dbias_c), (1, 0, 2)), nm + "rel_reduce")[:, :N_REL]
    dqkv = jnp.concatenate([dqa_t.T, _unheads(dka_h), _unheads(dva_h), dqb_t.T, _unheads(dkb_h), _unheads(dvb_h),
                            dqc_t.T, _unheads(dkc_h), _unheads(dvc_h)], axis=1)
    dgf = jnp.concatenate([dgates, dfb], axis=1)
    if ride is not None:
        ride.exchanged((got_a, got_b, got_c))
    dh1a = _mm(dqkv, p["wqkv"], mode="nt", out_dtype=F32, name="proj_qkv_dx", cap_k=1024)
    dh1b = _mm(dgf, p["wgf"], mode="nt", out_dtype=F32, name="proj_gf_dx", cap_k=640)
    d_wqkv = _mm(s["h1"], dqkv, mode="tn", out_dtype=F32, name="proj_qkv_dw", cap_m=1024, cap_k=2048)
    d_wgf = _mm(s["h1"], dgf, mode="tn", out_dtype=F32, name="proj_gf_dw", cap_m=1024, cap_n=640, cap_k=2048)
    dx, dsc_m, dsh_m, dgn_m = _norm_mod_bwd(s["x"], [dh1a, dh1b], dx1, p["norm_mix_g"], sc_m, nm + "norm_mix_bwd")
    d_mod = jnp.concatenate([dsh_m, dsc_m, dg_m, dsh_f, dsc_f, dg_f], axis=1)[0]
    grads = dict(w_in=_unpack_w_in(d_wqkv, d_wgf), w_branch=d_wb, w_out=d_wout.reshape(4, -1, D_MODEL),
                 w_ffn_in=d_wfi, w_ffn_out=d_wfo.reshape(4, -1, D_MODEL),
                 norm_mix_g=dgn_m[0], norm_ffn_g=dgn_f[0], b_forget=db_forget[0, :N_HEADS],
                 sinks=dsink[:, 0, 0], rel_bias=d_rel, d_mod=d_mod)
    return dx, grads


def kernel(x, c, norm_mix_g, norm_ffn_g, w_ada, b_ada, w_in, b_forget, sinks, rel_bias, w_branch, w_out, w_ffn_in, w_ffn_out, final_norm_g, loss_target, m_norm_mix_g, m_norm_ffn_g, m_w_ada, m_b_ada, m_w_in, m_b_forget, m_sinks, m_rel_bias, m_w_branch, m_w_out, m_w_ffn_in, m_w_ffn_out, m_final_norm_g, v_norm_mix_g, v_norm_ffn_g, v_w_ada, v_b_ada, v_w_in, v_b_forget, v_sinks, v_rel_bias, v_w_branch, v_w_out, v_w_ffn_in, v_w_ffn_out, v_final_norm_g):
    xi, yi, ci = _coords()
    chip = 2 * xi + yi
    dev = 2 * chip + ci
    xs = x[0]
    S = xs.shape[0]
    n_ada = w_ada.shape[2]

    big_names = ("w_in", "w_branch", "w_out", "w_ffn_in", "w_ffn_out")
    big_w = dict(w_in=w_in, w_branch=w_branch, w_out=w_out, w_ffn_in=w_ffn_in, w_ffn_out=w_ffn_out)
    big_m = dict(w_in=m_w_in, w_branch=m_w_branch, w_out=m_w_out, w_ffn_in=m_w_ffn_in, w_ffn_out=m_w_ffn_out)
    big_v = dict(w_in=v_w_in, w_branch=v_w_branch, w_out=v_w_out, w_ffn_in=v_w_ffn_in, w_ffn_out=v_w_ffn_out)
    flat2 = lambda a: a.reshape(-1, a.shape[-1])
    shards = [[flat2(big_w[n][l]).astype(BF16) for n in big_names] for l in range(DEPTH)]
    gw = [[None] * (len(big_names) + 2) for _ in range(DEPTH)]
    for l in range(DEPTH):
        shards[l] += [shards[l][0][:D_MODEL // 2], shards[l][0][D_MODEL // 2:]]
    gw[0][0] = _RowHalfGather([shards[0][0]]).run("weights_gather_w_in_l0")[0]
    host_g = ((1, 2, 4), (0,), (3,))

    class WeightRide:
        def __init__(self, l, plan):
            self.l, self.plan = l, plan

        def comm_for(self, name):
            if name not in self.plan:
                return None
            lay, idx = self.plan[name]
            return _RowHalfGather([shards[lay][i] for i in idx])

        def done(self, name, got):
            lay, idx = self.plan[name]
            for i, r in zip(idx, got):
                gw[lay][i] = r

        def late_weights(self):
            g = gw[self.l]
            return dict(wb=jnp.transpose(g[1], (1, 0, 2)).reshape(3 * BRANCH_W, D_MODEL),
                        wout=g[2].reshape(D_MODEL, D_MODEL),
                        wfi=jnp.transpose(g[3], (1, 0, 2)).reshape(D_MODEL, 2 * FFN_H),
                        wfo=g[4].reshape(FFN_H, D_MODEL))

    weight_plan = [
        {"proj_qkv": (0, (1, 2)), "attn_a": (0, (4,)), "attn_b": (0, (3,)), "attn_c": (1, (5,)), "ffn_in": (1, (6,))},
        {"attn_a": (1, (1, 2)), "attn_b": (1, (3,)), "attn_c": (1, (4,))}]


    c_all = _all_gather8(c.reshape(8, LANE), "gather_c").reshape(8, D_MODEL)
    b_sh = lax.dynamic_slice_in_dim(b_ada, chip * n_ada, n_ada, axis=1)[:, None, :]
    mod_sh = _ada_fwd(_pad_rows(c_all, 16), w_ada, b_sh, "ada_fwd")[:, :8, :]
    mod_all = _all_gather8(mod_sh.reshape(-1, LANE), "gather_mod").reshape(8, DEPTH, 8, n_ada)
    mod_mine = lax.dynamic_index_in_dim(mod_all[0::2], dev, axis=2, keepdims=False)
    mod = mod_mine.transpose(1, 0, 2).reshape(DEPTH, 6, D_MODEL)

    alibi = _pair_table(_alibi_table())
    no_sink = jnp.full((N_HEADS, 8, LANE), NEG_INF, F32)
    def make_params(l):
        if gw[l][0] is None:
            gw[l][0] = jnp.concatenate([gw[l][5], gw[l][6]], axis=1)
        wqkv, wgf = _pack_w_in(gw[l][0])
        rel_tab = _rel_expand(jnp.pad(rel_bias[l], ((0, 0), (0, N_REL_PAD - N_REL))), "l%d_rel_expand" % l)
        return dict(
            wqkv=wqkv, wgf=wgf, norm_mix_g=norm_mix_g[l][None], norm_ffn_g=norm_ffn_g[l][None],
            b_forget_pad=jnp.pad(b_forget[l], (0, LANE - N_HEADS))[None],
            sink_tab=jnp.broadcast_to(sinks[l][:, None, None], (N_HEADS, 8, LANE)),
            no_sink=no_sink, alibi=alibi, rel_tab=_pair_table(jnp.transpose(rel_tab, (1, 0, 2))))

    mods = [[mod[l, k][None] for k in range(6)] for l in range(DEPTH)]
    params, saved = [None] * DEPTH, [None] * DEPTH
    h = xs
    for l in range(DEPTH):
        h, saved[l], params[l] = _layer_fwd(h, mods[l], make_params(l), l, WeightRide(l, weight_plan[l]))
    loss_dev, dh, d_final = _final_loss(h, final_norm_g[None], loss_target[0], "final_loss")
    grads = [None] * DEPTH
    dh, grads[1] = _layer_bwd(dh, mods[1], params[1], saved[1], 1)

    class Layer1Ride:
        sends = {"ffn_out_dx": (4,), "ffn_out_dw": (1, 2), "ffn_in_dx": (3,), "ffn_in_dw": (0,)}
        hands = {"proj_qkv_dx": (0,), "proj_gf_dx": (3,), "proj_qkv_dw": (4,), "proj_gf_dw": (1, 2)}

        def __init__(self, g):
            self.g, self.t = g, [None] * len(g)
            self.parts, self.final = [None] * len(g), [None] * len(g)

        def comm_for(self, name):
            if name in self.sends:
                return _SiblingSend([self.g[i] for i in self.sends[name]], 0)
            if name in self.hands:
                return _Handoff([self.parts[i] for i in self.hands[name]], 1, (0, 1, 2, 3))
            return None

        def done(self, name, got):
            idx, dst = (self.sends[name], self.t) if name in self.sends else (self.hands[name], self.final)
            for i, r in zip(idx, got):
                dst[i] = r

        def exchanges(self):
            sums = [_add_cast_on(a, b, 1, "grads_chip_sum_l1_" + n) for n, a, b in zip(big_names, self.g, self.t)]
            return tuple(_OwnerReduce([sums[i] for i in idx], 1) for idx in host_g)

        def exchanged(self, got):
            for res, idx in zip(got, host_g):
                for r, i in zip(res, idx):
                    self.parts[i] = r

    ride = Layer1Ride([grads[1][n] for n in big_names])
    dh, grads[0] = _layer_bwd(dh, mods[0], params[0], saved[0], 0, ride)
    grad_x = dh[None]
    loss = lax.psum(loss_dev[0, 0], ("x", "y", "c"))
    parts1 = ride.final
    g0 = [grads[0][n] for n in big_names]
    t0 = _sibling_swap_rows(g0, "grads_swap_l0")
    sums0 = [_add_cast_rows(a, b, "grads_chip_sum_l0_" + n) for n, a, b in zip(big_names, g0, t0)]
    parts0 = [None] + list(_RowHalfReduce(sums0[1:]).run("grads_reduce_l0"))

    small_names = ("norm_mix_g", "norm_ffn_g", "b_ada", "b_forget", "sinks", "rel_bias", "final_norm_g")
    small_w = dict(norm_mix_g=norm_mix_g, norm_ffn_g=norm_ffn_g, b_ada=b_ada, b_forget=b_forget, sinks=sinks,
                   rel_bias=rel_bias, final_norm_g=final_norm_g)
    small_m = dict(norm_mix_g=m_norm_mix_g, norm_ffn_g=m_norm_ffn_g, b_ada=m_b_ada, b_forget=m_b_forget,
                   sinks=m_sinks, rel_bias=m_rel_bias, final_norm_g=m_final_norm_g)
    small_v = dict(norm_mix_g=v_norm_mix_g, norm_ffn_g=v_norm_ffn_g, b_ada=v_b_ada, b_forget=v_b_forget,
                   sinks=v_sinks, rel_bias=v_rel_bias, final_norm_g=v_final_norm_g)
    small_g = dict(
        norm_mix_g=jnp.stack([grads[l]["norm_mix_g"] for l in range(DEPTH)]),
        norm_ffn_g=jnp.stack([grads[l]["norm_ffn_g"] for l in range(DEPTH)]),
        b_ada=jnp.stack([grads[l]["d_mod"] for l in range(DEPTH)]),
        b_forget=jnp.stack([grads[l]["b_forget"] for l in range(DEPTH)]),
        sinks=jnp.stack([grads[l]["sinks"] for l in range(DEPTH)]),
        rel_bias=jnp.stack([grads[l]["rel_bias"] for l in range(DEPTH)]),
        final_norm_g=d_final[0])
    shapes = [small_w[n].shape for n in small_names]
    g_all = _all_gather8(_small_pack([small_g[n] for n in small_names]), "gather_small_grads")
    res, _ = _adamw(_small_pack([small_w[n] for n in small_names])[None],
                    _small_pack([small_m[n] for n in small_names])[None],
                    _small_pack([small_v[n] for n in small_names])[None], g_all, "adamw_small")
    small_out = {n: [] for n in small_names}
    for r in res:
        for n, a in zip(small_names, _small_unpack(r[0], shapes)):
            small_out[n].append(a)
    off_b = sum(int(np.prod(s)) for s in shapes[:2])
    n_mod = DEPTH * 6 * D_MODEL
    dmod_all = g_all.reshape(8, -1)[:, off_b:off_b + n_mod].reshape(8, DEPTH, 6 * D_MODEL)
    dmod_sh = lax.dynamic_slice_in_dim(dmod_all, chip * n_ada, n_ada, axis=2).transpose(1, 0, 2)
    g_ada = _ada_bwd(c_all.T, dmod_sh, "ada_bwd")
    ada_out, got = _adamw(w_ada, m_w_ada, v_w_ada, flat2(g_ada)[None], "adamw_w_ada",
                          comm=_RowHalfReduce(sums0[:1]))
    parts0[0] = got[0]

    big_out = {}
    as3 = lambda a: a.reshape(a.shape[0], -1, a.shape[-1])
    for n, p0, p1 in zip(big_names, parts0, parts1):
        res, _ = _adamw(as3(big_w[n]), as3(big_m[n]), as3(big_v[n]), [p0, p1], "adamw_" + n)
        big_out[n] = [r.reshape(big_w[n].shape) for r in res]

    order = ("norm_mix_g", "norm_ffn_g", "w_ada", "b_ada", "w_in", "b_forget", "sinks", "rel_bias", "w_branch",
             "w_out", "w_ffn_in", "w_ffn_out", "final_norm_g")

    def pick(n, k):
        if n == "w_ada":
            return ada_out[k]
        if n in big_out:
            return big_out[n][k]
        return small_out[n][k]

    outs = [loss, grad_x]
    for k in range(4):
        outs += [pick(n, k) for n in order]
    return tuple(outs)
```

```python
import numpy as np
import jax
import jax.numpy as jnp
from jax import lax
from jax.experimental import pallas as pl
from jax.experimental.pallas import tpu as pltpu

F32 = jnp.float32
BF16 = jnp.bfloat16
SDS = jax.ShapeDtypeStruct

D_MODEL = 1024
DEPTH = 2
CHUNK = 64
HEAD_DIM = 64
EPS = 1e-6
NEG_INF = -1e30
N_HEADS = 8
A_KV_HEADS = 2
A_PREV = 2
C_PREV = 8
REL_CLIP = 128
N_REL = 2 * REL_CLIP + 1
N_REL_PAD = 384
BRANCH_W = 512
FFN_H = 2816
FOX_BQ = 256
FOX_BK = 512
GF_COLS = 3200
N_IN_COLS = 6920
LANE = 128
VMEM_LIMIT = 48 * 1024 * 1024

ADAM_LR = 0.001
ADAM_B1 = 0.9
ADAM_B2 = 0.999
ADAM_EPS = 1e-08
ADAM_WD = 0.01
ADAM_STEP = 10

MESH = pl.DeviceIdType.MESH
ANY = pl.BlockSpec(memory_space=pl.ANY)
VMEM_SPEC = pl.BlockSpec(memory_space=pltpu.VMEM)


def _cparams(sem=None):
    return pltpu.CompilerParams(dimension_semantics=sem, vmem_limit_bytes=VMEM_LIMIT)


def _blk(n, cap):
    if n <= cap:
        return n
    best = None
    for m in range(LANE, cap + 1, LANE):
        if n % m == 0:
            best = m
    assert best is not None, (n, cap)
    return best


def _sigmoid(x):
    return 1.0 / (1.0 + jnp.exp(-x))


def _mm(a, b, *, mode, out_dtype, name, groups=1, cap_m=2048, cap_n=1024, cap_k=1408, col_quarters=False,
        comm=None):
    G = groups
    assert not col_quarters or mode == "tn"
    if mode == "nn":
        M, K, N = a.shape[0], a.shape[1] // G, b.shape[1]
        assert b.shape[0] == G * K
    elif mode == "nt":
        M, K, N = a.shape[0], a.shape[1] // G, b.shape[0] // G
        assert b.shape[1] == K
    else:
        K, M, N = a.shape[0], a.shape[1] // G, b.shape[1] // G
        assert b.shape[0] == K
    bm, bn, bk = _blk(M, cap_m), _blk(N // 4 if col_quarters else N, cap_n), _blk(K, cap_k)
    nm, nn, nk = M // bm, N // bn, K // bk
    if mode == "nn":
        a_spec = pl.BlockSpec((bm, bk), lambda g, i, j, k: (i, g * nk + k))
        b_spec = pl.BlockSpec((bk, bn), lambda g, i, j, k: (g * nk + k, j))
        o_spec = pl.BlockSpec((bm, bn), lambda g, i, j, k: (i, g * nn + j))
        dims = (((1,), (0,)), ((), ()))
        out_shape = (M, G * N)
    elif mode == "nt":
        a_spec = pl.BlockSpec((bm, bk), lambda g, i, j, k: (i, g * nk + k))
        b_spec = pl.BlockSpec((bn, bk), lambda g, i, j, k: (g * nn + j, k))
        o_spec = pl.BlockSpec((bm, bn), lambda g, i, j, k: (i, g * nn + j))
        dims = (((1,), (1,)), ((), ()))
        out_shape = (M, G * N)
    else:
        a_spec = pl.BlockSpec((bk, bm), lambda g, i, j, k: (k, g * nm + i))
        b_spec = pl.BlockSpec((bk, bn), lambda g, i, j, k: (k, g * nn + j))
        dims = (((0,), (0,)), ((), ()))
        if col_quarters:
            nq = nn // 4
            o_spec = pl.BlockSpec((1, bm, bn), lambda g, i, j, k: (j // nq, g * nm + i, j % nq))
            out_shape = (4, G * M, N // 4)
        else:
            o_spec = pl.BlockSpec((bm, bn), lambda g, i, j, k: (g * nm + i, j))
            out_shape = (G * M, N)

    def product(a_ref, b_ref):
        return lax.dot_general(a_ref[...].astype(BF16), b_ref[...].astype(BF16), dims, preferred_element_type=F32)

    def body_one(a_ref, b_ref, o_ref):
        o_ref[...] = product(a_ref, b_ref).astype(o_ref.dtype).reshape(o_ref.shape)

    def body_acc(a_ref, b_ref, o_ref, acc_ref):
        k = pl.program_id(3)

        @pl.when(k == 0)
        def _():
            acc_ref[...] = jnp.zeros_like(acc_ref)

        acc_ref[...] += product(a_ref, b_ref)

        @pl.when(k == nk - 1)
        def _():
            o_ref[...] = acc_ref[...].astype(o_ref.dtype).reshape(o_ref.shape)

    res, got = _call_hosting(
        body_one if nk == 1 else body_acc, comm=comm, grid=(G, nm, nn, nk), in_specs=[a_spec, b_spec],
        out_specs=[o_spec], out_shape=[SDS(out_shape, out_dtype)],
        scratch_shapes=[] if nk == 1 else [pltpu.VMEM((bm, bn), F32)], name=name, args=(a, b),
        semantics=("parallel", "parallel", "parallel", "arbitrary"))
    return res[0] if comm is None else (res[0], got)


def _rows(tm, n, col=0):
    return pl.BlockSpec((tm, n), lambda i: (i, col))


def _vec(n):
    return pl.BlockSpec((1, n), lambda i: (0, 0))


def _tm(S):
    return min(S, 256)


def _norm_mod_fwd(x, g, sc, sh, name):
    S, Dm = x.shape
    tm = _tm(S)

    def body(x_ref, g_ref, sc_ref, sh_ref, h_ref):
        xv = x_ref[...]
        r = lax.rsqrt(jnp.mean(xv * xv, axis=-1, keepdims=True) + EPS)
        h_ref[...] = ((xv * r) * g_ref[...] * (1.0 + sc_ref[...]) + sh_ref[...]).astype(h_ref.dtype)

    return pl.pallas_call(
        body, grid=(S // tm,), in_specs=[_rows(tm, Dm), _vec(Dm), _vec(Dm), _vec(Dm)],
        out_specs=_rows(tm, Dm), out_shape=SDS((S, Dm), BF16),
        compiler_params=_cparams(("parallel",)), name=name)(x, g, sc, sh)


def _norm_mod_bwd(x, dh_list, dres, g, sc, name):
    S, Dm = x.shape
    tm = _tm(S)
    nh = len(dh_list)

    def body(*refs):
        x_ref = refs[0]
        dh_refs = refs[1:1 + nh]
        dres_ref, g_ref, sc_ref, dx_ref, dsc_ref, dsh_ref, dg_ref = refs[1 + nh:]
        i = pl.program_id(0)

        @pl.when(i == 0)
        def _():
            dsc_ref[...] = jnp.zeros_like(dsc_ref)
            dsh_ref[...] = jnp.zeros_like(dsh_ref)
            dg_ref[...] = jnp.zeros_like(dg_ref)

        xv = x_ref[...]
        dh = dh_refs[0][...]
        for r_ in dh_refs[1:]:
            dh = dh + r_[...]
        gv = g_ref[...]
        r = lax.rsqrt(jnp.mean(xv * xv, axis=-1, keepdims=True) + EPS)
        xn = xv * r
        xg = xn * gv
        dsh_ref[...] += jnp.sum(dh, axis=0, keepdims=True)
        dsc_ref[...] += jnp.sum(dh * xg, axis=0, keepdims=True)
        dxg = dh * (1.0 + sc_ref[...])
        dg_ref[...] += jnp.sum(dxg * xn, axis=0, keepdims=True)
        dxn = dxg * gv
        dx_ref[...] = dres_ref[...] + r * (dxn - xn * jnp.mean(dxn * xn, axis=-1, keepdims=True))

    return pl.pallas_call(
        body, grid=(S // tm,),
        in_specs=[_rows(tm, Dm)] * (2 + nh) + [_vec(Dm), _vec(Dm)],
        out_specs=[_rows(tm, Dm), _vec(Dm), _vec(Dm), _vec(Dm)],
        out_shape=[SDS((S, Dm), F32), SDS((1, Dm), F32), SDS((1, Dm), F32), SDS((1, Dm), F32)],
        compiler_params=_cparams(("arbitrary",)), name=name)(x, *dh_list, dres, g, sc)


def _resid_fwd(x, val, g, name):
    S, Dm = x.shape
    tm = _tm(S)

    def body(x_ref, v_ref, g_ref, o_ref):
        o_ref[...] = x_ref[...] + g_ref[...] * v_ref[...]

    return pl.pallas_call(
        body, grid=(S // tm,), in_specs=[_rows(tm, Dm), _rows(tm, Dm), _vec(Dm)],
        out_specs=_rows(tm, Dm), out_shape=SDS((S, Dm), F32),
        compiler_params=_cparams(("parallel",)), name=name)(x, val, g)


def _resid_bwd(dx, val, g, name):
    S, Dm = dx.shape
    tm = _tm(S)

    def body(dx_ref, v_ref, g_ref, dg_ref, dv_ref):
        @pl.when(pl.program_id(0) == 0)
        def _():
            dg_ref[...] = jnp.zeros_like(dg_ref)

        dxv = dx_ref[...]
        dg_ref[...] += jnp.sum(dxv * v_ref[...], axis=0, keepdims=True)
        dv_ref[...] = (dxv * g_ref[...]).astype(dv_ref.dtype)

    return pl.pallas_call(
        body, grid=(S // tm,), in_specs=[_rows(tm, Dm), _rows(tm, Dm), _vec(Dm)],
        out_specs=[_vec(Dm), _rows(tm, Dm)], out_shape=[SDS((1, Dm), F32), SDS((S, Dm), BF16)],
        compiler_params=_cparams(("arbitrary",)), name=name)(dx, val, g)


def _merge_fwd(y, gf, name):
    S = y.shape[0]
    tm = _tm(S)
    W = 3 * D_MODEL

    def body(y_ref, g_ref, o_ref):
        acc = None
        for k in range(3):
            sl = slice(k * D_MODEL, (k + 1) * D_MODEL)
            t = _sigmoid(g_ref[:, sl]) * y_ref[:, sl]
            acc = t if acc is None else acc + t
        o_ref[...] = acc.astype(o_ref.dtype)

    return pl.pallas_call(
        body, grid=(S // tm,), in_specs=[_rows(tm, W), _rows(tm, W)],
        out_specs=_rows(tm, D_MODEL), out_shape=SDS((S, D_MODEL), BF16),
        compiler_params=_cparams(("parallel",)), name=name)(y, gf)


def _merge_bwd(dm, y, gf, name):
    S = y.shape[0]
    tm = _tm(S)
    W = 3 * D_MODEL

    def body(dm_ref, y_ref, g_ref, dy_ref, dg_ref):
        dmv = dm_ref[...]
        for k in range(3):
            sl = slice(k * D_MODEL, (k + 1) * D_MODEL)
            sg = _sigmoid(g_ref[:, sl])
            dy_ref[:, sl] = (dmv * sg).astype(dy_ref.dtype)
            dg_ref[:, sl] = (dmv * y_ref[:, sl] * (sg * (1.0 - sg))).astype(dg_ref.dtype)

    return pl.pallas_call(
        body, grid=(S // tm,), in_specs=[_rows(tm, D_MODEL), _rows(tm, W), _rows(tm, W)],
        out_specs=[_rows(tm, W), _rows(tm, W)], out_shape=[SDS((S, W), BF16), SDS((S, W), BF16)],
        compiler_params=_cparams(("parallel",)), name=name)(dm, y, gf)


def _swiglu_fwd(u, name):
    S = u.shape[0]
    tm = _tm(S)

    def body(g_ref, u_ref, a_ref):
        gv = g_ref[...]
        a_ref[...] = (gv * _sigmoid(gv) * u_ref[...]).astype(a_ref.dtype)

    return pl.pallas_call(
        body, grid=(S // tm,), in_specs=[_rows(tm, FFN_H, 0), _rows(tm, FFN_H, 1)],
        out_specs=_rows(tm, FFN_H), out_shape=SDS((S, FFN_H), BF16),
        compiler_params=_cparams(("parallel",)), name=name)(u, u)


def _swiglu_bwd(da, u, name):
    S = u.shape[0]
    tm = _tm(S)

    def body(da_ref, g_ref, u_ref, du_ref):
        dav = da_ref[...]
        gv = g_ref[...]
        sg = _sigmoid(gv)
        du_ref[:, 0:FFN_H] = (dav * u_ref[...] * (sg * (1.0 + gv * (1.0 - sg)))).astype(du_ref.dtype)
        du_ref[:, FFN_H:2 * FFN_H] = (dav * (gv * sg)).astype(du_ref.dtype)

    return pl.pallas_call(
        body, grid=(S // tm,), in_specs=[_rows(tm, FFN_H), _rows(tm, FFN_H, 0), _rows(tm, FFN_H, 1)],
        out_specs=_rows(tm, 2 * FFN_H), out_shape=SDS((S, 2 * FFN_H), BF16),
        compiler_params=_cparams(("parallel",)), name=name)(da, u, u)


def _final_loss(x, g, target, name):
    S, Dm = x.shape
    tm = _tm(S)

    def body(x_ref, g_ref, t_ref, loss_ref, dx_ref, dg_ref):
        @pl.when(pl.program_id(0) == 0)
        def _():
            loss_ref[...] = jnp.zeros_like(loss_ref)
            dg_ref[...] = jnp.zeros_like(dg_ref)

        xv = x_ref[...]
        gv = g_ref[...]
        r = lax.rsqrt(jnp.mean(xv * xv, axis=-1, keepdims=True) + EPS)
        xn = xv * r
        err = xn * gv - t_ref[...]
        row = jnp.mean(err * err, axis=-1, keepdims=True)
        loss_ref[...] += 0.5 * jnp.sum(row, axis=0, keepdims=True)
        dy = err * (1.0 / Dm)
        dg_ref[...] += jnp.sum(dy * xn, axis=0, keepdims=True)
        dxn = dy * gv
        dx_ref[...] = r * (dxn - xn * jnp.mean(dxn * xn, axis=-1, keepdims=True))

    return pl.pallas_call(
        body, grid=(S // tm,), in_specs=[_rows(tm, Dm), _vec(Dm), _rows(tm, Dm)],
        out_specs=[pl.BlockSpec((1, 1), lambda i: (0, 0)), _rows(tm, Dm), _vec(Dm)],
        out_shape=[SDS((1, 1), F32), SDS((S, Dm), F32), SDS((1, Dm), F32)],
        compiler_params=_cparams(("arbitrary",)), name=name)(x, g, target)


PAIR = 2 * CHUNK


def _bandT_softmax(kg, qTg, bias, sink, valid):
    s = jnp.dot(kg, qTg, preferred_element_type=F32)
    s = jnp.where(valid, s + bias, NEG_INF)
    m = jnp.maximum(jnp.max(s, axis=0, keepdims=True), sink)
    e = jnp.exp(s - m)
    es = jnp.exp(sink - m)
    inv = 1.0 / (jnp.sum(e, axis=0, keepdims=True) + es)
    return e * inv, es * inv


def _pad_copy_rows(dst, src, pad, S):
    dst[:, 0:pad, :] = jnp.zeros((dst.shape[0], pad, dst.shape[2]), dst.dtype)
    dst[:, pad:pad + S, :] = src[...]


def _pad_copy_lanes(dst, src, pad, S):
    dst[:, 0:pad] = jnp.zeros((dst.shape[0], pad), dst.dtype)
    dst[:, pad:pad + S] = src[...]


def _fm(arg):
    return arg if isinstance(arg, tuple) else (arg, 0)


def _fm_spec(rows, S, row0):
    off, rem = divmod(row0, rows)
    assert rem == 0
    return pl.BlockSpec((rows, S), lambda i: (off + i, 0))


def _bandT_fwd(qT, k_h, vT, bias, sink, *, GQ, GK, P, kvoff, name, comm=None):
    (qT, q0), (vT, v0) = _fm(qT), _fm(vT)
    S = qT.shape[1]
    ng = bias.shape[0] // GQ
    BU = (P + 2) * CHUNK
    pad = P * CHUNK
    npair = S // PAIR

    def body(qT_ref, k_ref, vT_ref, b_ref, s_ref, oT_ref, kp, vTp):
        _pad_copy_rows(kp, k_ref, pad, S)
        _pad_copy_lanes(vTp, vT_ref, pad, S)
        rowi = lax.broadcasted_iota(jnp.int32, (BU, PAIR), 0)

        def step(n2, carry):
            r = pl.multiple_of(n2 * PAIR, PAIR)
            valid = rowi >= (P - 2 * n2) * CHUNK
            for g in range(GQ):
                kv = kvoff(g)
                hs = slice(g * HEAD_DIM, (g + 1) * HEAD_DIM)
                kvs = slice(kv * HEAD_DIM, (kv + 1) * HEAD_DIM)
                qTg = qT_ref[hs, pl.ds(r, PAIR)] * 0.125
                p, _ = _bandT_softmax(kp[kv, pl.ds(r, BU), :], qTg, b_ref[g], s_ref[g, 0:1, :], valid)
                oTg = jnp.dot(vTp[kvs, pl.ds(r, BU)], p.astype(BF16), preferred_element_type=F32)
                oT_ref[hs, pl.ds(r, PAIR)] = oTg.astype(oT_ref.dtype)
            return carry

        lax.fori_loop(0, npair, step, 0, unroll=min(2, npair))

    res, got = _call_hosting(
        body, comm=comm, grid=(ng,),
        in_specs=[_fm_spec(GQ * HEAD_DIM, S, q0),
                  pl.BlockSpec((GK, S, HEAD_DIM), lambda i: (i, 0, 0)),
                  _fm_spec(GK * HEAD_DIM, S, v0),
                  pl.BlockSpec((GQ, BU, PAIR), lambda i: (i, 0, 0)),
                  pl.BlockSpec((GQ, 8, LANE), lambda i: (i, 0, 0))],
        out_specs=[pl.BlockSpec((GQ * HEAD_DIM, S), lambda i: (i, 0))],
        out_shape=[SDS((ng * GQ * HEAD_DIM, S), BF16)],
        scratch_shapes=[pltpu.VMEM((GK, S + pad, HEAD_DIM), BF16), pltpu.VMEM((GK * HEAD_DIM, S + pad), BF16)],
        name=name, args=(qT, k_h, vT, bias, sink))
    return res[0], got


def _bandT_bwd(qT, q_h, k_h, kT, v_h, doT, do_h, bias, sink, *, GQ, GK, P, kvoff, name, comm=None):
    (qT, q0), (kT, k0), (doT, d0) = _fm(qT), _fm(kT), _fm(doT)
    S = qT.shape[1]
    ng = bias.shape[0] // GQ
    BU = (P + 2) * CHUNK
    pad = P * CHUNK
    npair = S // PAIR

    def body(qT_ref, q_ref, k_ref, kT_ref, v_ref, doT_ref, do_ref, b_ref, s_ref,
             dqT_ref, dk_ref, dv_ref, db_ref, dsk_ref, kp, kTp, vp, dkp, dvp):
        _pad_copy_rows(kp, k_ref, pad, S)
        _pad_copy_rows(vp, v_ref, pad, S)
        _pad_copy_lanes(kTp, kT_ref, pad, S)
        dkp[...] = jnp.zeros_like(dkp)
        dvp[...] = jnp.zeros_like(dvp)
        db_ref[...] = jnp.zeros_like(db_ref)
        rowi = lax.broadcasted_iota(jnp.int32, (BU, PAIR), 0)

        def step(n2, dsink):
            r = pl.multiple_of(n2 * PAIR, PAIR)
            valid = rowi >= (P - 2 * n2) * CHUNK
            new = []
            for g in range(GQ):
                kv = kvoff(g)
                hs = slice(g * HEAD_DIM, (g + 1) * HEAD_DIM)
                kvs = slice(kv * HEAD_DIM, (kv + 1) * HEAD_DIM)
                qTg = qT_ref[hs, pl.ds(r, PAIR)] * 0.125
                p, ps = _bandT_softmax(kp[kv, pl.ds(r, BU), :], qTg, b_ref[g], s_ref[g, 0:1, :], valid)
                dp = jnp.dot(vp[kv, pl.ds(r, BU), :], doT_ref[hs, pl.ds(r, PAIR)], preferred_element_type=F32)
                delta = jnp.sum(p * dp, axis=0, keepdims=True)
                ds = p * (dp - delta)
                new.append(dsink[g] - ps * delta)
                db_ref[g] += ds
                dsb = ds.astype(BF16)
                dq = jnp.dot(kTp[kvs, pl.ds(r, BU)], dsb, preferred_element_type=F32) * 0.125
                dqT_ref[hs, pl.ds(r, PAIR)] = dq.astype(dqT_ref.dtype)
                dkp[kv, pl.ds(r, BU), :] += jnp.dot(dsb, q_ref[g, pl.ds(r, PAIR), :] * 0.125,
                                                    preferred_element_type=F32)
                dvp[kv, pl.ds(r, BU), :] += jnp.dot(p.astype(BF16), do_ref[g, pl.ds(r, PAIR), :],
                                                    preferred_element_type=F32)
            return tuple(new)

        dsink = lax.fori_loop(0, npair, step, tuple(jnp.zeros((1, PAIR), F32) for _ in range(GQ)))
        for g in range(GQ):
            dsk_ref[g] = jnp.broadcast_to(jnp.sum(dsink[g], axis=1, keepdims=True), (8, LANE))
        dk_ref[...] = dkp[:, pad:pad + S, :].astype(dk_ref.dtype)
        dv_ref[...] = dvp[:, pad:pad + S, :].astype(dv_ref.dtype)

    qTs = pl.BlockSpec((GQ * HEAD_DIM, S), lambda i: (i, 0))
    qhs = pl.BlockSpec((GQ, S, HEAD_DIM), lambda i: (i, 0, 0))
    khs = pl.BlockSpec((GK, S, HEAD_DIM), lambda i: (i, 0, 0))
    bs = pl.BlockSpec((GQ, BU, PAIR), lambda i: (i, 0, 0))
    ss = pl.BlockSpec((GQ, 8, LANE), lambda i: (i, 0, 0))
    nkv = ng * GK
    return _call_hosting(
        body, comm=comm, grid=(ng,),
        in_specs=[_fm_spec(GQ * HEAD_DIM, S, q0), qhs, khs, _fm_spec(GK * HEAD_DIM, S, k0), khs,
                  _fm_spec(GQ * HEAD_DIM, S, d0), qhs, bs, ss],
        out_specs=[qTs, khs, khs, bs, ss],
        out_shape=[SDS((ng * GQ * HEAD_DIM, S), BF16), SDS((nkv, S, HEAD_DIM), BF16), SDS((nkv, S, HEAD_DIM), BF16),
                   SDS((ng * GQ, BU, PAIR), F32), SDS((ng * GQ, 8, LANE), F32)],
        scratch_shapes=[pltpu.VMEM((GK, S + pad, HEAD_DIM), BF16), pltpu.VMEM((GK * HEAD_DIM, S + pad), BF16),
                        pltpu.VMEM((GK, S + pad, HEAD_DIM), BF16),
                        pltpu.VMEM((GK, S + pad, HEAD_DIM), F32), pltpu.VMEM((GK, S + pad, HEAD_DIM), F32)],
        name=name, args=(qT, q_h, k_h, kT, v_h, doT, do_h, bias, sink))


def _pair_table(tab):
    t = jnp.transpose(tab, (0, 2, 1))
    lo = jnp.pad(t, ((0, 0), (0, CHUNK), (0, 0)), constant_values=NEG_INF)
    hi = jnp.pad(t, ((0, 0), (CHUNK, 0), (0, 0)), constant_values=NEG_INF)
    return jnp.concatenate([lo, hi], axis=2)


def _unpair_table(d):
    band = d.shape[1] - CHUNK
    return jnp.transpose(d[:, 0:band, 0:CHUNK] + d[:, CHUNK:CHUNK + band, CHUNK:PAIR], (0, 2, 1))


def _heads(a, n):
    return jnp.transpose(a.reshape(a.shape[0], n, HEAD_DIM), (1, 0, 2))


def _unheads(a):
    return jnp.transpose(a, (1, 0, 2)).reshape(a.shape[1], a.shape[0] * HEAD_DIM)


def _foxT_logits(kj, qTg, cq, ck, r, c, rowi, coli):
    s = jnp.dot(kj, qTg, preferred_element_type=F32)
    s = s + cq - ck
    return jnp.where(c + rowi <= r + coli, s, NEG_INF)


def _foxT_fwd(qT, k_h, vT, ck, cq, name, comm=None):
    (qT, q0), (vT, v0) = _fm(qT), _fm(vT)
    S = qT.shape[1]
    npair = k_h.shape[0] // 2
    BQ, BK = min(FOX_BQ, S), min(FOX_BK, S)
    nq = S // BQ
    heads = [slice(g * HEAD_DIM, (g + 1) * HEAD_DIM) for g in range(2)]

    def body(qT_ref, k_ref, vT_ref, ck_ref, cq_ref, oT_ref, lse_ref):
        rowi = lax.broadcasted_iota(jnp.int32, (BK, BQ), 0)
        coli = lax.broadcasted_iota(jnp.int32, (BK, BQ), 1)

        def qstep(i, carry):
            r = pl.multiple_of(i * BQ, BQ)
            qs = [qT_ref[hs, pl.ds(r, BQ)] * 0.125 for hs in heads]
            cqs = [cq_ref[g, :, pl.ds(r, BQ)] for g in range(2)]

            def kstep(j, st):
                c = pl.multiple_of(j * BK, BK)
                new = []
                for g, hs in enumerate(heads):
                    m, l, acc = st[g]
                    s = _foxT_logits(k_ref[g, pl.ds(c, BK), :], qs[g], cqs[g], ck_ref[g, pl.ds(c, BK), :],
                                     r, c, rowi, coli)
                    mn = jnp.maximum(m, jnp.max(s, axis=0, keepdims=True))
                    al = jnp.exp(m - mn)
                    e = jnp.exp(s - mn)
                    l = al * l + jnp.sum(e, axis=0, keepdims=True)
                    acc = al * acc + jnp.dot(vT_ref[hs, pl.ds(c, BK)], e.astype(BF16), preferred_element_type=F32)
                    new.append((mn, l, acc))
                return tuple(new)

            init = (jnp.full((1, BQ), NEG_INF, F32), jnp.zeros((1, BQ), F32), jnp.zeros((HEAD_DIM, BQ), F32))
            st = lax.fori_loop(0, (r + BQ + BK - 1) // BK, kstep, (init, init))
            for g, hs in enumerate(heads):
                m, l, acc = st[g]
                oT_ref[hs, pl.ds(r, BQ)] = (acc * (1.0 / l)).astype(oT_ref.dtype)
                lse_ref[g, :, pl.ds(r, BQ)] = m + jnp.log(l)
            return carry

        lax.fori_loop(0, nq, qstep, 0)

    fT = pl.BlockSpec((LANE, S), lambda i: (i, 0))
    hm = pl.BlockSpec((2, S, HEAD_DIM), lambda i: (i, 0, 0))
    col = pl.BlockSpec((2, S, 1), lambda i: (i, 0, 0))
    rw = pl.BlockSpec((2, 1, S), lambda i: (i, 0, 0))
    return _call_hosting(
        body, comm=comm, grid=(npair,), in_specs=[_fm_spec(LANE, S, q0), hm, _fm_spec(LANE, S, v0), col, rw],
        out_specs=[fT, rw],
        out_shape=[SDS((npair * LANE, S), BF16), SDS((2 * npair, 1, S), F32)], scratch_shapes=[],
        name=name, args=(qT, k_h, vT, ck, cq))


def _foxT_bwd(qT, q_h, k_h, kT, v_h, ck, cq, oT, doT, do_h, lse, name, comm=None):
    (qT, q0), (kT, k0), (doT, d0) = _fm(qT), _fm(kT), _fm(doT)
    S = qT.shape[1]
    npair = k_h.shape[0] // 2
    BQ, BK = min(FOX_BQ, S), min(FOX_BK, S)
    nq = S // BQ
    heads = [slice(g * HEAD_DIM, (g + 1) * HEAD_DIM) for g in range(2)]

    def body(qT_ref, q_ref, k_ref, kT_ref, v_ref, ck_ref, cq_ref, oT_ref, doT_ref, do_ref, lse_ref,
             dqT_ref, dk_ref, dv_ref, dck_ref, dcq_ref, dka, dva, qa_ref):
        qa_ref[:, :, 0:HEAD_DIM] = q_ref[...] * 0.125
        qa_ref[:, :, HEAD_DIM:LANE] = jnp.ones((2, S, LANE - HEAD_DIM), BF16)
        dka[...] = jnp.zeros_like(dka)
        dva[...] = jnp.zeros_like(dva)
        rowi = lax.broadcasted_iota(jnp.int32, (BK, BQ), 0)
        coli = lax.broadcasted_iota(jnp.int32, (BK, BQ), 1)

        def qstep(i, carry):
            r = pl.multiple_of(i * BQ, BQ)
            qs = [qT_ref[hs, pl.ds(r, BQ)] * 0.125 for hs in heads]
            dos = [doT_ref[hs, pl.ds(r, BQ)] for hs in heads]
            deltas = [jnp.sum(dos[g].astype(F32) * oT_ref[hs, pl.ds(r, BQ)].astype(F32), axis=0, keepdims=True)
                      for g, hs in enumerate(heads)]
            cqs = [cq_ref[g, :, pl.ds(r, BQ)] for g in range(2)]
            lses = [lse_ref[g, :, pl.ds(r, BQ)] for g in range(2)]

            def kstep(j, st):
                c = pl.multiple_of(j * BK, BK)
                new = []
                for g, hs in enumerate(heads):
                    dq, rs = st[g]
                    s = _foxT_logits(k_ref[g, pl.ds(c, BK), :], qs[g], cqs[g], ck_ref[g, pl.ds(c, BK), :],
                                     r, c, rowi, coli)
                    p = jnp.exp(s - lses[g])
                    dp = jnp.dot(v_ref[g, pl.ds(c, BK), :], dos[g], preferred_element_type=F32)
                    ds = p * (dp - deltas[g])
                    dsb = ds.astype(BF16)
                    dka[g, pl.ds(c, BK), :] += jnp.dot(dsb, qa_ref[g, pl.ds(r, BQ), :], preferred_element_type=F32)
                    dva[g, pl.ds(c, BK), :] += jnp.dot(p.astype(BF16), do_ref[g, pl.ds(r, BQ), :],
                                                      preferred_element_type=F32)
                    new.append((dq + jnp.dot(kT_ref[hs, pl.ds(c, BK)], dsb, preferred_element_type=F32),
                                rs + jnp.sum(dsb.astype(F32), axis=0, keepdims=True)))
                return tuple(new)

            init = (jnp.zeros((HEAD_DIM, BQ), F32), jnp.zeros((1, BQ), F32))
            st = lax.fori_loop(0, (r + BQ + BK - 1) // BK, kstep, (init, init))
            for g, hs in enumerate(heads):
                dqT_ref[hs, pl.ds(r, BQ)] = (st[g][0] * 0.125).astype(dqT_ref.dtype)
                dcq_ref[g, :, pl.ds(r, BQ)] = st[g][1]
            return carry

        lax.fori_loop(0, nq, qstep, 0)
        dk_ref[...] = dka[:, :, 0:HEAD_DIM].astype(dk_ref.dtype)
        dck_ref[...] = -dka[:, :, HEAD_DIM:HEAD_DIM + 1]
        dv_ref[...] = dva[...].astype(dv_ref.dtype)

    fT = pl.BlockSpec((LANE, S), lambda i: (i, 0))
    hm = pl.BlockSpec((2, S, HEAD_DIM), lambda i: (i, 0, 0))
    col = pl.BlockSpec((2, S, 1), lambda i: (i, 0, 0))
    rw = pl.BlockSpec((2, 1, S), lambda i: (i, 0, 0))
    nh = 2 * npair
    return _call_hosting(
        body, comm=comm, grid=(npair,),
        in_specs=[_fm_spec(LANE, S, q0), hm, hm, _fm_spec(LANE, S, k0), hm, col, rw, fT, _fm_spec(LANE, S, d0), hm, rw],
        out_specs=[fT, hm, hm, col, rw],
        out_shape=[SDS((npair * LANE, S), BF16), SDS((nh, S, HEAD_DIM), BF16), SDS((nh, S, HEAD_DIM), BF16),
                   SDS((nh, S, 1), F32), SDS((nh, 1, S), F32)],
        scratch_shapes=[pltpu.VMEM((2, S, LANE), F32), pltpu.VMEM((2, S, HEAD_DIM), F32),
                        pltpu.VMEM((2, S, LANE), BF16)],
        name=name, args=(qT, q_h, k_h, kT, v_h, ck, cq, oT, doT, do_h, lse))


def _split3(x):
    hi = x.astype(BF16)
    r1 = x - hi.astype(F32)
    mid = r1.astype(BF16)
    lo = (r1 - mid.astype(F32)).astype(BF16)
    return hi, mid, lo


def _tri_dot(tri, x):
    hi, mid, lo = _split3(x)
    return (jnp.dot(tri, hi, preferred_element_type=F32) + jnp.dot(tri, mid, preferred_element_type=F32)
            + jnp.dot(tri, lo, preferred_element_type=F32))


def _fox_cum(gf, bfo, name):
    S = gf.shape[0]
    nb = S // LANE
    fcol = (GF_COLS - LANE) // LANE

    def body(f_ref, b_ref, cum_ref):
        row = lax.broadcasted_iota(jnp.int32, (LANE, LANE), 0)
        col = lax.broadcasted_iota(jnp.int32, (LANE, LANE), 1)
        tri = jnp.where(row >= col, 1.0, 0.0).astype(BF16)
        carry = jnp.zeros((1, LANE), F32)
        for t in range(nb):
            xl = f_ref[t * LANE:(t + 1) * LANE, :] + b_ref[...]
            lf = jnp.minimum(xl, 0.0) - jnp.log(1.0 + jnp.exp(-jnp.abs(xl)))
            cblk = _tri_dot(tri, lf) + carry
            cum_ref[t * LANE:(t + 1) * LANE, :] = cblk
            carry = cblk[LANE - 1:LANE, :]

    return pl.pallas_call(
        body, grid=(1,), in_specs=[pl.BlockSpec((S, LANE), lambda i: (0, fcol)), _vec(LANE)],
        out_specs=pl.BlockSpec((S, LANE), lambda i: (0, 0)), out_shape=SDS((S, LANE), F32),
        compiler_params=_cparams(("arbitrary",)), name=name)(gf, bfo)


def _fox_cum_bwd(gf, bfo, dcum, name):
    S = gf.shape[0]
    nb = S // LANE
    fcol = (GF_COLS - LANE) // LANE

    def body(f_ref, b_ref, dc_ref, df_ref, db_ref):
        row = lax.broadcasted_iota(jnp.int32, (LANE, LANE), 0)
        col = lax.broadcasted_iota(jnp.int32, (LANE, LANE), 1)
        tri = jnp.where(row <= col, 1.0, 0.0).astype(BF16)
        carry = jnp.zeros((1, LANE), F32)
        tot = jnp.zeros((1, LANE), F32)
        for t in range(nb - 1, -1, -1):
            rows = slice(t * LANE, (t + 1) * LANE)
            dlf = _tri_dot(tri, dc_ref[rows, :]) + carry
            carry = dlf[0:1, :]
            xl = f_ref[rows, :] + b_ref[...]
            dfl = dlf * (1.0 / (1.0 + jnp.exp(xl)))
            df_ref[rows, :] = dfl.astype(df_ref.dtype)
            tot = tot + jnp.sum(dfl, axis=0, keepdims=True)
        db_ref[...] = tot

    return pl.pallas_call(
        body, grid=(1,),
        in_specs=[pl.BlockSpec((S, LANE), lambda i: (0, fcol)), _vec(LANE), pl.BlockSpec((S, LANE), lambda i: (0, 0))],
        out_specs=[pl.BlockSpec((S, LANE), lambda i: (0, 0)), _vec(LANE)],
        out_shape=[SDS((S, LANE), BF16), SDS((1, LANE), F32)],
        compiler_params=_cparams(("arbitrary",)), name=name)(gf, bfo, dcum)


REL_FAR = C_PREV * CHUNK - REL_CLIP


def _rel_onehot(qi, band):
    w = band - REL_FAR
    r = lax.broadcasted_iota(jnp.int32, (N_REL_PAD, w), 0)
    j = lax.broadcasted_iota(jnp.int32, (N_REL_PAD, w), 1) + REL_FAR
    idx = jnp.clip(C_PREV * CHUNK + qi - j, -REL_CLIP, REL_CLIP) + REL_CLIP
    return jnp.where(r == idx, 1.0, 0.0).astype(BF16)


def _rel_expand(rel, name):
    band = (C_PREV + 1) * CHUNK

    def body(rel_ref, o_ref):
        hi, mid, lo = _split3(rel_ref[...])
        far = jnp.broadcast_to(rel_ref[:, 2 * REL_CLIP:2 * REL_CLIP + 1], (N_HEADS, REL_FAR))

        def row(qi, carry):
            oh = _rel_onehot(qi, band)
            o_ref[qi, :, 0:REL_FAR] = far
            o_ref[qi, :, REL_FAR:band] = (jnp.dot(hi, oh, preferred_element_type=F32)
                                          + jnp.dot(mid, oh, preferred_element_type=F32)
                                          + jnp.dot(lo, oh, preferred_element_type=F32))
            return carry

        lax.fori_loop(0, CHUNK, row, 0, unroll=2)

    return pl.pallas_call(
        body, grid=(1,), in_specs=[pl.BlockSpec((N_HEADS, N_REL_PAD), lambda i: (0, 0))],
        out_specs=pl.BlockSpec((CHUNK, N_HEADS, band), lambda i: (0, 0, 0)),
        out_shape=SDS((CHUNK, N_HEADS, band), F32),
        compiler_params=_cparams(("arbitrary",)), name=name)(rel)


def _rel_reduce(dbias, name):
    band = (C_PREV + 1) * CHUNK
    NT = (((1,), (1,)), ((), ()))

    def body(d_ref, o_ref):
        def row(qi, st):
            acc, far = st
            oh = _rel_onehot(qi, band)
            hi, mid, lo = _split3(d_ref[qi, :, REL_FAR:band])
            acc = acc + (lax.dot_general(hi, oh, NT, preferred_element_type=F32)
                         + lax.dot_general(mid, oh, NT, preferred_element_type=F32)
                         + lax.dot_general(lo, oh, NT, preferred_element_type=F32))
            return acc, far + jnp.sum(d_ref[qi, :, 0:REL_FAR], axis=-1, keepdims=True)

        acc, far = lax.fori_loop(0, CHUNK, row, (jnp.zeros((N_HEADS, N_REL_PAD), F32), jnp.zeros((N_HEADS, 1), F32)),
                                 unroll=2)
        col = lax.broadcasted_iota(jnp.int32, (N_HEADS, N_REL_PAD), 1)
        o_ref[...] = acc + jnp.where(col == 2 * REL_CLIP, far, 0.0)

    return pl.pallas_call(
        body, grid=(1,), in_specs=[pl.BlockSpec((CHUNK, N_HEADS, band), lambda i: (0, 0, 0))],
        out_specs=pl.BlockSpec((N_HEADS, N_REL_PAD), lambda i: (0, 0)),
        out_shape=SDS((N_HEADS, N_REL_PAD), F32),
        compiler_params=_cparams(("arbitrary",)), name=name)(dbias)


def _alibi_table():
    qi = np.arange(CHUNK)[:, None]
    j = np.arange((A_PREV + 1) * CHUNK)[None, :]
    dist = np.abs(A_PREV * CHUNK + qi - j).astype(np.float32)
    slopes = np.exp2(-8.0 * np.arange(1, N_HEADS + 1, dtype=np.float32) / N_HEADS).astype(np.float32)
    return jnp.asarray(-slopes[:, None, None] * dist[None])


def _ada_fwd(c_all, w, b, name):
    n = w.shape[2]

    def body(c_ref, w_ref, b_ref, o_ref):
        cv = c_ref[...]
        cond = (cv * _sigmoid(cv)).astype(BF16)
        o_ref[0] = jnp.dot(cond, w_ref[0].astype(BF16), preferred_element_type=F32) + b_ref[0]

    return pl.pallas_call(
        body, grid=(DEPTH,),
        in_specs=[pl.BlockSpec((16, D_MODEL), lambda l: (0, 0)), pl.BlockSpec((1, D_MODEL, n), lambda l: (l, 0, 0)),
                  pl.BlockSpec((1, 1, n), lambda l: (l, 0, 0))],
        out_specs=pl.BlockSpec((1, 16, n), lambda l: (l, 0, 0)), out_shape=SDS((DEPTH, 16, n), F32),
        compiler_params=_cparams(("parallel",)), name=name)(c_all, w, b)


def _ada_bwd(c_t, dmod, name):
    n = dmod.shape[2]
    bn = _blk(n, 512)
    tr = 256

    def body(c_ref, d_ref, o_ref):
        cv = c_ref[...]
        cond = (cv * _sigmoid(cv)).astype(BF16).astype(F32)
        dm = d_ref[0].astype(BF16).astype(F32)
        acc = cond[:, 0:1] * dm[0:1, :]
        for b_ in range(1, 8):
            acc = acc + cond[:, b_:b_ + 1] * dm[b_:b_ + 1, :]
        o_ref[0] = acc

    return pl.pallas_call(
        body, grid=(DEPTH, D_MODEL // tr, n // bn),
        in_specs=[pl.BlockSpec((tr, 8), lambda l, i, j: (i, 0)), pl.BlockSpec((1, 8, bn), lambda l, i, j: (l, 0, j))],
        out_specs=pl.BlockSpec((1, tr, bn), lambda l, i, j: (l, i, j)), out_shape=SDS((DEPTH, D_MODEL, n), F32),
        compiler_params=_cparams(("parallel", "parallel", "parallel")), name=name)(c_t, dmod)


def _adamw(w, m, v, parts, name, comm=None):
    L, R, C = w.shape
    per_layer = isinstance(parts, (list, tuple))
    plist = list(parts) if per_layer else [parts]
    P = plist[0].shape[0]
    tr = _blk_rows(R, max(16, (1 << 18) // C))
    nr = R // tr
    c1 = 1.0 - ADAM_B1 ** ADAM_STEP
    c2 = 1.0 - ADAM_B2 ** ADAM_STEP

    def total(p_ref):
        g = p_ref[0].astype(F32)
        for k in range(1, P):
            g = g + p_ref[k].astype(F32)
        return g

    def body(w_ref, m_ref, v_ref, *rest):
        p_refs, (g_ref, d_ref, nm_ref, nv_ref) = rest[:len(plist)], rest[len(plist):]
        g = total(p_refs[0])
        for k in range(1, len(plist)):
            g = jnp.where(pl.program_id(0) == k, total(p_refs[k]), g)
        mn = ADAM_B1 * m_ref[0] + (1.0 - ADAM_B1) * g
        vn = ADAM_B2 * v_ref[0] + (1.0 - ADAM_B2) * (g * g)
        m_hat = mn / c1
        v_hat = vn / c2
        g_ref[0] = g
        nm_ref[0] = mn
        nv_ref[0] = vn
        d_ref[0] = -ADAM_LR * (m_hat / (jnp.sqrt(v_hat) + ADAM_EPS) + ADAM_WD * w_ref[0])

    rs = pl.BlockSpec((1, tr, C), lambda l, i: (l, i, 0))
    if per_layer:
        def layer_spec(k):
            return pl.BlockSpec((P, tr, C), lambda l, i: (0, jnp.where(l == k, i, 0), 0))
        pspecs = [layer_spec(k) for k in range(L)]
    else:
        pspecs = [pl.BlockSpec((P, tr, C), lambda l, i: (0, l * nr + i, 0))]
    return _call_hosting(
        body, comm=comm, grid=(L, nr), in_specs=[rs, rs, rs] + pspecs, out_specs=[rs, rs, rs, rs],
        out_shape=[SDS((L, R, C), F32)] * 4, scratch_shapes=[], name=name, args=(w, m, v, *plist))


def _blk_rows(R, cap):
    if R <= cap:
        return R
    best = None
    for t in range(16, cap + 1, 16):
        if R % t == 0:
            best = t
    assert best is not None, (R, cap)
    return best


def _add_cast_rows(g, t, name):
    Q, R, C = g.shape
    half = R // 2
    tr = _blk_rows(half, max(16, (1 << 19) // C))
    nb = half // tr

    def body(lo_ref, hi_ref, t_ref, o_ref):
        c = lax.axis_index("c")

        @pl.when(c == 0)
        def _():
            o_ref[...] = (lo_ref[...].astype(F32) + t_ref[...].astype(F32)).astype(o_ref.dtype)

        @pl.when(c == 1)
        def _():
            o_ref[...] = (hi_ref[...].astype(F32) + t_ref[...].astype(F32)).astype(o_ref.dtype)

    bs = pl.BlockSpec((1, tr, C), lambda q, i: (q, i, 0))
    hi = pl.BlockSpec((1, tr, C), lambda q, i: (q, nb + i, 0))
    return pl.pallas_call(
        body, grid=(Q, nb), in_specs=[bs, hi, bs], out_specs=bs, out_shape=SDS((Q, half, C), BF16),
        compiler_params=_cparams(("parallel", "parallel")), name=name)(g, g, t)


def _coords():
    return lax.axis_index("x"), lax.axis_index("y"), lax.axis_index("c")


def _flip(v, bit):
    return 1 - v if bit else v


def _all_gather8(v, name):
    R = v.shape[0]

    def body(v_ref, o_ref, send_sems, recv_sems):
        x, y, c = _coords()
        me = 4 * x + 2 * y + c
        o_ref[me] = v_ref[...]
        copies = []
        for k in range(1, 8):
            peer = (_flip(x, k & 4), _flip(y, k & 2), _flip(c, k & 1))
            cp = pltpu.make_async_remote_copy(
                src_ref=v_ref, dst_ref=o_ref.at[me], send_sem=send_sems.at[k - 1], recv_sem=recv_sems.at[k - 1],
                device_id=peer, device_id_type=MESH)
            cp.start()
            copies.append(cp)
        for cp in copies:
            cp.wait_recv()
        for cp in copies:
            cp.wait_send()

    return pl.pallas_call(
        body, in_specs=[VMEM_SPEC], out_specs=VMEM_SPEC, out_shape=SDS((8, R, LANE), v.dtype),
        scratch_shapes=[pltpu.SemaphoreType.DMA((7,)), pltpu.SemaphoreType.DMA((7,))],
        compiler_params=pltpu.CompilerParams(vmem_limit_bytes=VMEM_LIMIT), name=name)(v)


def _sibling_swap_rows(arrs, name):
    n = len(arrs)

    def body(*refs):
        in_refs, out_refs = refs[:n], refs[n:2 * n]
        send_sems, recv_sems = refs[2 * n:]
        x, y, c = _coords()
        copies = []
        for a in range(n):
            Q, R = in_refs[a].shape[0], in_refs[a].shape[1]
            half = R // 2
            src = in_refs[a].at[pl.ds(0, Q), pl.ds(pl.multiple_of((1 - c) * half, 16), half)]
            cp = pltpu.make_async_remote_copy(
                src_ref=src, dst_ref=out_refs[a], send_sem=send_sems.at[a], recv_sem=recv_sems.at[a],
                device_id=(x, y, 1 - c), device_id_type=MESH)
            cp.start()
            copies.append(cp)
        for cp in copies:
            cp.wait_recv()
        for cp in copies:
            cp.wait_send()

    return pl.pallas_call(
        body, in_specs=[ANY] * n, out_specs=[ANY] * n,
        out_shape=[SDS((a.shape[0], a.shape[1] // 2, a.shape[2]), a.dtype) for a in arrs],
        scratch_shapes=[pltpu.SemaphoreType.DMA((n,)), pltpu.SemaphoreType.DMA((n,))],
        name=name)(*arrs)


class _OwnerReduce:
    aliased = False

    def __init__(self, srcs, lay):
        self.srcs, self.lay, self.n = list(srcs), lay, len(srcs)
        self.out_shapes = [SDS(a.shape, a.dtype) for a in self.srcs]
        self.sem_shapes = [pltpu.SemaphoreType.DMA((self.n, 3)), pltpu.SemaphoreType.DMA((self.n, 3)),
                           pltpu.SemaphoreType.DMA((self.n,))]

    def _copies(self, src_refs, dst_refs, sems):
        ici_send, ici_recv, loc_sem = sems
        x, y, c = _coords()
        p = 2 * x + y
        local, remote = [], []
        for a in range(self.n):
            local.append(pltpu.make_async_copy(src_refs[a].at[p], dst_refs[a].at[p], loc_sem.at[a]))
            for k in range(1, 4):
                qx, qy = _flip(x, k & 2), _flip(y, k & 1)
                remote.append(pltpu.make_async_remote_copy(
                    src_ref=src_refs[a].at[2 * qx + qy], dst_ref=dst_refs[a].at[p], send_sem=ici_send.at[a, k - 1],
                    recv_sem=ici_recv.at[a, k - 1], device_id=(qx, qy, self.lay), device_id_type=MESH))
        return c, local, remote

    def start(self, src_refs, dst_refs, sems):
        c, local, remote = self._copies(src_refs, dst_refs, sems)

        @pl.when(c == self.lay)
        def _():
            for cp in local + remote:
                cp.start()

    def finish(self, src_refs, dst_refs, sems):
        c, local, remote = self._copies(src_refs, dst_refs, sems)

        @pl.when(c == self.lay)
        def _():
            for cp in remote:
                cp.wait_recv()
            for cp in remote:
                cp.wait_send()
            for cp in local:
                cp.wait()


def _call_hosting(body, *, comm, grid, in_specs, out_specs, out_shape, scratch_shapes, name, args, semantics=None):
    n_in, n_out, n_scr = len(args), len(out_shape), len(scratch_shapes)
    if comm is None:
        sem = semantics if semantics is not None else ("parallel",) * len(grid)
        res = pl.pallas_call(body, grid=grid, in_specs=in_specs, out_specs=out_specs, out_shape=out_shape,
                             scratch_shapes=scratch_shapes, compiler_params=_cparams(sem), name=name)(*args)
        return list(res), None
    k = comm.n

    def hosted(*refs):
        ins, cin = refs[:n_in], refs[n_in:n_in + k]
        outs = refs[n_in + k:n_in + k + n_out]
        cout = refs[n_in + k + n_out:n_in + 2 * k + n_out]
        scr = refs[n_in + 2 * k + n_out:n_in + 2 * k + n_out + n_scr]
        sems = refs[n_in + 2 * k + n_out + n_scr:]
        first = pl.program_id(0) == 0
        last = pl.program_id(0) == grid[0] - 1
        for d in range(1, len(grid)):
            first = jnp.logical_and(first, pl.program_id(d) == 0)
            last = jnp.logical_and(last, pl.program_id(d) == grid[d] - 1)

        @pl.when(first)
        def _():
            comm.start(cin, cout, sems)

        body(*ins, *outs, *scr)

        @pl.when(last)
        def _():
            comm.finish(cin, cout, sems)

    aliases = {n_in + j: n_out + j for j in range(k)} if comm.aliased else {}
    res = pl.pallas_call(
        hosted, grid=grid, in_specs=list(in_specs) + [ANY] * k, out_specs=list(out_specs) + [ANY] * k,
        out_shape=list(out_shape) + comm.out_shapes, scratch_shapes=list(scratch_shapes) + comm.sem_shapes,
        input_output_aliases=aliases, compiler_params=_cparams(("arbitrary",) * len(grid)),
        name=name)(*args, *comm.srcs)
    return list(res[:n_out]), list(res[n_out:])


class _RowHalfGather:
    aliased = False

    def __init__(self, srcs):
        self.srcs, self.n = list(srcs), len(srcs)
        self.out_shapes = [SDS((4,) + a.shape, a.dtype) for a in self.srcs]
        n = self.n
        self.sem_shapes = [pltpu.SemaphoreType.DMA((n, 3)), pltpu.SemaphoreType.DMA((n, 3)),
                           pltpu.SemaphoreType.DMA((n, 3)), pltpu.SemaphoreType.DMA((n, 3)),
                           pltpu.SemaphoreType.DMA((n,))]

    def _copies(self, src_refs, dst_refs, sems):
        ici_send, ici_recv, d2d_send, d2d_recv, loc_sem = sems
        x, y, c = _coords()
        p = 2 * x + y
        local, first, fwd = [], [], []
        for a in range(self.n):
            R = src_refs[a].shape[0] // 2
            half = pl.ds(pl.multiple_of(c * R, 16), R)
            local.append(pltpu.make_async_copy(src_refs[a], dst_refs[a].at[p], loc_sem.at[a]))
            for k in range(1, 4):
                qx, qy = _flip(x, k & 2), _flip(y, k & 1)
                first.append(pltpu.make_async_remote_copy(
                    src_ref=src_refs[a].at[half], dst_ref=dst_refs[a].at[p, half], send_sem=ici_send.at[a, k - 1],
                    recv_sem=ici_recv.at[a, k - 1], device_id=(qx, qy, c), device_id_type=MESH))
                slot = dst_refs[a].at[2 * qx + qy, half]
                fwd.append(pltpu.make_async_remote_copy(
                    src_ref=slot, dst_ref=slot, send_sem=d2d_send.at[a, k - 1], recv_sem=d2d_recv.at[a, k - 1],
                    device_id=(x, y, 1 - c), device_id_type=MESH))
        return local, first, fwd

    def start(self, src_refs, dst_refs, sems):
        local, first, _ = self._copies(src_refs, dst_refs, sems)
        for cp in local + first:
            cp.start()

    def finish(self, src_refs, dst_refs, sems):
        local, first, fwd = self._copies(src_refs, dst_refs, sems)
        for got, on in zip(first, fwd):
            got.wait_recv()
            on.start()
        for cp in fwd:
            cp.wait_recv()
        for cp in first + fwd:
            cp.wait_send()
        for cp in local:
            cp.wait()

    def run(self, name):
        return _run_exchange(self, name)


def _run_exchange(comm, name):
    n = comm.n

    def body(*refs):
        src_refs, dst_refs, sems = refs[:n], refs[n:2 * n], refs[2 * n:]
        comm.start(src_refs, dst_refs, sems)
        comm.finish(src_refs, dst_refs, sems)

    return pl.pallas_call(body, in_specs=[ANY] * n, out_specs=[ANY] * n, out_shape=comm.out_shapes,
                          scratch_shapes=comm.sem_shapes, name=name)(*comm.srcs)


class _RowHalfReduce:
    aliased = False

    def __init__(self, srcs):
        self.srcs, self.n = list(srcs), len(srcs)
        self.out_shapes = [SDS((4, 2 * a.shape[1], a.shape[2]), a.dtype) for a in self.srcs]
        n = self.n
        self.sem_shapes = [pltpu.SemaphoreType.DMA((n, 3)), pltpu.SemaphoreType.DMA((n, 3)),
                           pltpu.SemaphoreType.DMA((n, 4)), pltpu.SemaphoreType.DMA((n, 4)),
                           pltpu.SemaphoreType.DMA((n,))]

    def _copies(self, src_refs, dst_refs, sems):
        ici_send, ici_recv, d2d_send, d2d_recv, loc_sem = sems
        x, y, c = _coords()
        p = 2 * x + y
        local, first, fwd = [], [], []
        for a in range(self.n):
            R = src_refs[a].shape[1]
            half = pl.ds(pl.multiple_of(c * R, 16), R)
            local.append(pltpu.make_async_copy(src_refs[a].at[p], dst_refs[a].at[p, half], loc_sem.at[a]))
            for k in range(4):
                qx, qy = _flip(x, k & 2), _flip(y, k & 1)
                if k:
                    first.append(pltpu.make_async_remote_copy(
                        src_ref=src_refs[a].at[2 * qx + qy], dst_ref=dst_refs[a].at[p, half],
                        send_sem=ici_send.at[a, k - 1], recv_sem=ici_recv.at[a, k - 1], device_id=(qx, qy, c),
                        device_id_type=MESH))
                slot = dst_refs[a].at[2 * qx + qy, half]
                fwd.append(pltpu.make_async_remote_copy(
                    src_ref=slot, dst_ref=slot, send_sem=d2d_send.at[a, k], recv_sem=d2d_recv.at[a, k],
                    device_id=(x, y, 1 - c), device_id_type=MESH))
        return local, first, fwd

    def start(self, src_refs, dst_refs, sems):
        local, first, _ = self._copies(src_refs, dst_refs, sems)
        for cp in local + first:
            cp.start()

    def finish(self, src_refs, dst_refs, sems):
        local, first, fwd = self._copies(src_refs, dst_refs, sems)
        for a in range(self.n):
            local[a].wait()
            fwd[4 * a].start()
            for k in range(1, 4):
                first[3 * a + k - 1].wait_recv()
                fwd[4 * a + k].start()
        for cp in fwd:
            cp.wait_recv()
        for cp in first + fwd:
            cp.wait_send()

    def run(self, name):
        return _run_exchange(self, name)


class _SiblingSend:
    aliased = False

    def __init__(self, srcs, src_core):
        self.srcs, self.src_core, self.n = list(srcs), src_core, len(srcs)
        self.out_shapes = [SDS(a.shape, a.dtype) for a in self.srcs]
        self.sem_shapes = [pltpu.SemaphoreType.DMA((self.n,)), pltpu.SemaphoreType.DMA((self.n,))]

    def _copies(self, src_refs, dst_refs, sems):
        x, y, c = _coords()
        return c, [pltpu.make_async_remote_copy(
            src_ref=src_refs[a], dst_ref=dst_refs[a], send_sem=sems[0].at[a], recv_sem=sems[1].at[a],
            device_id=(x, y, 1 - c), device_id_type=MESH) for a in range(self.n)]

    def start(self, src_refs, dst_refs, sems):
        c, copies = self._copies(src_refs, dst_refs, sems)

        @pl.when(c == self.src_core)
        def _():
            for cp in copies:
                cp.start()

    def finish(self, src_refs, dst_refs, sems):
        c, copies = self._copies(src_refs, dst_refs, sems)

        @pl.when(c == self.src_core)
        def _():
            for cp in copies:
                cp.wait_send()

        @pl.when(c != self.src_core)
        def _():
            for cp in copies:
                cp.wait_recv()


class _Handoff:
    aliased = True

    def __init__(self, srcs, lay, slots):
        self.srcs, self.lay, self.slots, self.n = list(srcs), lay, tuple(slots), len(srcs)
        self.out_shapes = [SDS(a.shape, a.dtype) for a in self.srcs]
        ns = len(self.slots)
        self.sem_shapes = [pltpu.SemaphoreType.DMA((self.n, ns)), pltpu.SemaphoreType.DMA((self.n, ns))]

    def _copies(self, dst_refs, sems):
        x, y, c = _coords()
        copies = []
        for a in range(self.n):
            for j, k in enumerate(self.slots):
                slot = dst_refs[a].at[2 * _flip(x, k & 2) + _flip(y, k & 1)]
                copies.append(pltpu.make_async_remote_copy(
                    src_ref=slot, dst_ref=slot, send_sem=sems[0].at[a, j], recv_sem=sems[1].at[a, j],
                    device_id=(x, y, 1 - c), device_id_type=MESH))
        return c, copies

    def start(self, src_refs, dst_refs, sems):
        c, copies = self._copies(dst_refs, sems)

        @pl.when(c == self.lay)
        def _():
            for cp in copies:
                cp.start()

    def finish(self, src_refs, dst_refs, sems):
        c, copies = self._copies(dst_refs, sems)

        @pl.when(c == self.lay)
        def _():
            for cp in copies:
                cp.wait_send()

        @pl.when(c != self.lay)
        def _():
            for cp in copies:
                cp.wait_recv()


def _add_cast_on(a, b, lay, name):
    Q, R, C = b.shape
    tr = _blk_rows(R, max(16, (1 << 19) // C))

    def body(a_ref, b_ref, o_ref):
        @pl.when(lax.axis_index("c") == lay)
        def _():
            o_ref[...] = (a_ref[...].astype(F32) + b_ref[...].astype(F32)).astype(o_ref.dtype)

    bs = pl.BlockSpec((1, tr, C), lambda q, i: (q, i, 0))
    return pl.pallas_call(
        body, grid=(Q, R // tr), in_specs=[bs, bs], out_specs=bs, out_shape=SDS((Q, R, C), BF16),
        compiler_params=_cparams(("parallel", "parallel")), name=name)(a, b)


_IN_SIZES = (512, 128, 128, 512, 512, 512, 8, 512, 512, 512, 3072)
_IN_OFF = tuple(int(v) for v in np.cumsum((0,) + _IN_SIZES))
_IN_Q = N_IN_COLS // 4


def _pack_w_in(w):
    def cols(lo, hi):
        out = []
        while lo < hi:
            q, off = divmod(lo, _IN_Q)
            n = min(hi - lo, _IN_Q - off)
            out.append(w[q, :, off:off + n])
            lo += n
        return out

    fb0, fb1, g0 = _IN_OFF[6], _IN_OFF[7], _IN_OFF[10]
    wqkv = jnp.concatenate(cols(0, fb0) + cols(fb1, g0), axis=1)
    wgf = jnp.concatenate(cols(g0, N_IN_COLS) + cols(fb0, fb1) + [jnp.zeros((w.shape[1], LANE - 8), w.dtype)], axis=1)
    return wqkv, wgf


def _unpack_w_in(dqkv, dgf):
    fb0, fb1, g0 = _IN_OFF[6], _IN_OFF[7], _IN_OFF[10]

    def cols(lo, hi):
        out = []
        while lo < hi:
            if lo < fb0:
                n = min(hi, fb0) - lo
                out.append(dqkv[:, lo:lo + n])
            elif lo < fb1:
                n = min(hi, fb1) - lo
                out.append(dgf[:, 3072 + lo - fb0:3072 + lo - fb0 + n])
            elif lo < g0:
                n = min(hi, g0) - lo
                out.append(dqkv[:, lo - 8:lo - 8 + n])
            else:
                n = hi - lo
                out.append(dgf[:, lo - g0:lo - g0 + n])
            lo += n
        return out

    return jnp.stack([jnp.concatenate(cols(q * _IN_Q, (q + 1) * _IN_Q), axis=1) for q in range(4)])


def _pad_rows(a, rows):
    return jnp.pad(a, ((0, rows - a.shape[0]), (0, 0)))


def _small_pack(parts):
    flat = jnp.concatenate([p.reshape(-1) for p in parts])
    n = flat.shape[0]
    rows = -(-n // LANE)
    rows = -(-rows // 8) * 8
    return jnp.pad(flat, (0, rows * LANE - n)).reshape(rows, LANE)


def _small_unpack(block, shapes):
    flat = block.reshape(-1)
    out, off = [], 0
    for s in shapes:
        n = int(np.prod(s))
        out.append(flat[off:off + n].reshape(s))
        off += n
    return out


def _kv_same(g):
    return 0


def _kv_own(g):
    return g


_mm_plain = _mm


def _mm_hosting(a, b, *, comm, **kw):
    if comm is None:
        return _mm(a, b, **kw), None
    return _mm(a, b, comm=comm, **kw)


def _layer_fwd(x, mod, p, l, ride):
    sh_m, sc_m, g_m, sh_f, sc_f, g_f = mod
    nm = "l%d_" % l

    def carried(name, run):
        res, got = run(ride.comm_for(name))
        if got is not None:
            ride.done(name, got)
        return res

    h1 = _norm_mod_fwd(x, p["norm_mix_g"], sc_m, sh_m, nm + "norm_mix_fwd")
    qkv = carried("proj_qkv", lambda cm: _mm_hosting(h1, p["wqkv"], mode="nn", out_dtype=BF16,
                                                     name=nm + "proj_qkv", comm=cm))
    gf = _mm(h1, p["wgf"], mode="nn", out_dtype=F32, name=nm + "proj_gf", cap_n=640)
    qkv_t = qkv.T
    o_a_t = carried("attn_a", lambda cm: _bandT_fwd(
        (qkv_t, 0), _heads(qkv[:, 512:640], A_KV_HEADS), (qkv_t, 640), p["alibi"], p["sink_tab"],
        GQ=4, GK=1, P=A_PREV, kvoff=_kv_same, name=nm + "attn_a_fwd", comm=cm))
    cum = _fox_cum(gf, p["b_forget_pad"], nm + "fox_cum")
    cum_t = cum[:, :N_HEADS].T
    cc, cr = cum_t[:, :, None], cum_t[:, None, :]
    o_b_t, lse_b = carried("attn_b", lambda cm: _foxT_fwd(
        (qkv_t, 768), _heads(qkv[:, 1280:1792], N_HEADS), (qkv_t, 1792), cc, cr, nm + "attn_b_fwd", comm=cm))
    o_c_t = carried("attn_c", lambda cm: _bandT_fwd(
        (qkv_t, 2304), _heads(qkv[:, 2816:3328], N_HEADS), (qkv_t, 3328), p["rel_tab"], p["no_sink"],
        GQ=2, GK=2, P=C_PREV, kvoff=_kv_own, name=nm + "attn_c_fwd", comm=cm))
    p = dict(p, **ride.late_weights())
    o = jnp.concatenate([o_a_t, o_b_t, o_c_t], axis=0).T
    y = _mm(o, p["wb"], mode="nn", out_dtype=F32, groups=3, name=nm + "branch")
    merged = _merge_fwd(y, gf, nm + "merge_fwd")
    mix = _mm(merged, p["wout"], mode="nn", out_dtype=F32, name=nm + "out_proj")
    x1 = _resid_fwd(x, mix, g_m, nm + "resid_mix")
    h2 = _norm_mod_fwd(x1, p["norm_ffn_g"], sc_f, sh_f, nm + "norm_ffn_fwd")
    u = carried("ffn_in", lambda cm: _mm_hosting(h2, p["wfi"], mode="nn", out_dtype=F32, name=nm + "ffn_in",
                                                 cap_n=512, comm=cm))
    a = _swiglu_fwd(u, nm + "swiglu_fwd")
    f = _mm(a, p["wfo"], mode="nn", out_dtype=F32, name=nm + "ffn_out", cap_m=1024)
    x2 = _resid_fwd(x1, f, g_f, nm + "resid_ffn")
    saved = dict(x=x, h1=h1, qkv=qkv, qkv_t=qkv_t, gf=gf, cc=cc, cr=cr, o_b_t=o_b_t, lse_b=lse_b, o=o, y=y, merged=merged,
                 mix=mix, x1=x1, h2=h2, u=u, a=a, f=f)
    return x2, saved, p


def _layer_bwd(dx2, mod, p, s, l, ride=None):
    sh_m, sc_m, g_m, sh_f, sc_f, g_f = mod
    nm = "l%d_" % l

    def _mm(a, b, *, name, **kw):
        comm = ride.comm_for(name) if ride is not None else None
        if comm is None:
            return _mm_plain(a, b, name=nm + name, **kw)
        out, got = _mm_plain(a, b, name=nm + name, comm=comm, **kw)
        ride.done(name, got)
        return out

    dg_f, df = _resid_bwd(dx2, s["f"], g_f, nm + "resid_ffn_bwd")
    da = _mm(df, p["wfo"], mode="nt", out_dtype=F32, name="ffn_out_dx", cap_m=1024, cap_n=1408)
    d_wfo = _mm(s["a"], df, mode="tn", out_dtype=BF16, name="ffn_out_dw", cap_m=1408, cap_k=2048)
    du = _swiglu_bwd(da, s["u"], nm + "swiglu_bwd")
    dh2 = _mm(du, p["wfi"], mode="nt", out_dtype=F32, name="ffn_in_dx", cap_m=1024)
    d_wfi = _mm(s["h2"], du, mode="tn", out_dtype=BF16, name="ffn_in_dw", cap_m=1024, cap_n=1408, cap_k=2048,
                col_quarters=True)
    dx1, dsc_f, dsh_f, dgn_f = _norm_mod_bwd(s["x1"], [dh2], dx2, p["norm_ffn_g"], sc_f, nm + "norm_ffn_bwd")
    dg_m, dmix = _resid_bwd(dx1, s["mix"], g_m, nm + "resid_mix_bwd")
    dmerged = _mm(dmix, p["wout"], mode="nt", out_dtype=F32, name="out_proj_dx")
    d_wout = _mm(s["merged"], dmix, mode="tn", out_dtype=BF16, name="out_proj_dw", cap_m=1024, cap_k=2048)
    dy, dgates = _merge_bwd(dmerged, s["y"], s["gf"], nm + "merge_bwd")
    do = _mm(dy, p["wb"], mode="nt", out_dtype=BF16, groups=3, name="branch_dx")
    d_wb = _mm(s["o"], dy, mode="tn", out_dtype=BF16, groups=3, name="branch_dw", cap_k=2048,
               col_quarters=True)
    comms = ride.exchanges() if ride is not None else (None, None, None)
    qkv, qkv_t = s["qkv"], s["qkv_t"]
    do_t = do.T
    (dqa_t, dka_h, dva_h, _, dsink), got_a = _bandT_bwd(
        (qkv_t, 0), _heads(qkv[:, 0:512], N_HEADS), _heads(qkv[:, 512:640], A_KV_HEADS), (qkv_t, 512),
        _heads(qkv[:, 640:768], A_KV_HEADS), (do_t, 0), _heads(do[:, 0:512], N_HEADS), p["alibi"], p["sink_tab"],
        GQ=4, GK=1, P=A_PREV, kvoff=_kv_same, name=nm + "attn_a_bwd", comm=comms[0])
    (dqb_t, dkb_h, dvb_h, dck, dcq), got_b = _foxT_bwd(
        (qkv_t, 768), _heads(qkv[:, 768:1280], N_HEADS), _heads(qkv[:, 1280:1792], N_HEADS), (qkv_t, 1280),
        _heads(qkv[:, 1792:2304], N_HEADS), s["cc"], s["cr"], s["o_b_t"], (do_t, 512),
        _heads(do[:, 512:1024], N_HEADS), s["lse_b"], nm + "attn_b_bwd", comm=comms[1])
    dcum = jnp.pad((dck[:, :, 0] + dcq[:, 0, :]).T, ((0, 0), (0, LANE - N_HEADS)))
    dfb, db_forget = _fox_cum_bwd(s["gf"], p["b_forget_pad"], dcum, nm + "fox_cum_bwd")
    (dqc_t, dkc_h, dvc_h, dbias_c, _), got_c = _bandT_bwd(
        (qkv_t, 2304), _heads(qkv[:, 2304:2816], N_HEADS), _heads(qkv[:, 2816:3328], N_HEADS), (qkv_t, 2816),
        _heads(qkv[:, 3328:3840], N_HEADS), (do_t, 1024), _heads(do[:, 1024:1536], N_HEADS), p["rel_tab"],
        p["no_sink"], GQ=2, GK=2, P=C_PREV, kvoff=_kv_own, name=nm + "attn_c_bwd", comm=comms[2])
    d_rel = _rel_reduce(jnp.transpose(_unpair_table(dbias_c), (1, 0, 2)), nm + "rel_reduce")[:, :N_REL]
    dqkv = jnp.concatenate([dqa_t.T, _unheads(dka_h), _unheads(dva_h), dqb_t.T, _unheads(dkb_h), _unheads(dvb_h),
                            dqc_t.T, _unheads(dkc_h), _unheads(dvc_h)], axis=1)
    dgf = jnp.concatenate([dgates, dfb], axis=1)
    if ride is not None:
        ride.exchanged((got_a, got_b, got_c))
    dh1a = _mm(dqkv, p["wqkv"], mode="nt", out_dtype=F32, name="proj_qkv_dx", cap_k=1024)
    dh1b = _mm(dgf, p["wgf"], mode="nt", out_dtype=F32, name="proj_gf_dx", cap_k=640)
    d_wqkv = _mm(s["h1"], dqkv, mode="tn", out_dtype=BF16, name="proj_qkv_dw", cap_m=1024, cap_k=2048)
    d_wgf = _mm(s["h1"], dgf, mode="tn", out_dtype=BF16, name="proj_gf_dw", cap_m=1024, cap_n=640, cap_k=2048)
    dx, dsc_m, dsh_m, dgn_m = _norm_mod_bwd(s["x"], [dh1a, dh1b], dx1, p["norm_mix_g"], sc_m, nm + "norm_mix_bwd")
    d_mod = jnp.concatenate([dsh_m, dsc_m, dg_m, dsh_f, dsc_f, dg_f], axis=1)[0]
    grads = dict(w_in=_unpack_w_in(d_wqkv, d_wgf), w_branch=d_wb, w_out=d_wout.reshape(4, -1, D_MODEL),
                 w_ffn_in=d_wfi, w_ffn_out=d_wfo.reshape(4, -1, D_MODEL),
                 norm_mix_g=dgn_m[0], norm_ffn_g=dgn_f[0], b_forget=db_forget[0, :N_HEADS],
                 sinks=dsink[:, 0, 0], rel_bias=d_rel, d_mod=d_mod)
    return dx, grads


def kernel(x, c, norm_mix_g, norm_ffn_g, w_ada, b_ada, w_in, b_forget, sinks, rel_bias, w_branch, w_out, w_ffn_in, w_ffn_out, final_norm_g, loss_target, m_norm_mix_g, m_norm_ffn_g, m_w_ada, m_b_ada, m_w_in, m_b_forget, m_sinks, m_rel_bias, m_w_branch, m_w_out, m_w_ffn_in, m_w_ffn_out, m_final_norm_g, v_norm_mix_g, v_norm_ffn_g, v_w_ada, v_b_ada, v_w_in, v_b_forget, v_sinks, v_rel_bias, v_w_branch, v_w_out, v_w_ffn_in, v_w_ffn_out, v_final_norm_g):
    xi, yi, ci = _coords()
    chip = 2 * xi + yi
    dev = 2 * chip + ci
    xs = x[0]
    S = xs.shape[0]
    n_ada = w_ada.shape[2]

    big_names = ("w_in", "w_branch", "w_out", "w_ffn_in", "w_ffn_out")
    big_w = dict(w_in=w_in, w_branch=w_branch, w_out=w_out, w_ffn_in=w_ffn_in, w_ffn_out=w_ffn_out)
    big_m = dict(w_in=m_w_in, w_branch=m_w_branch, w_out=m_w_out, w_ffn_in=m_w_ffn_in, w_ffn_out=m_w_ffn_out)
    big_v = dict(w_in=v_w_in, w_branch=v_w_branch, w_out=v_w_out, w_ffn_in=v_w_ffn_in, w_ffn_out=v_w_ffn_out)
    flat2 = lambda a: a.reshape(-1, a.shape[-1])
    shards = [[flat2(big_w[n][l]).astype(BF16) for n in big_names] for l in range(DEPTH)]
    gw = [[None] * (len(big_names) + 2) for _ in range(DEPTH)]
    for l in range(DEPTH):
        shards[l] += [shards[l][0][:D_MODEL // 2], shards[l][0][D_MODEL // 2:]]
    gw[0][0] = _RowHalfGather([shards[0][0]]).run("weights_gather_w_in_l0")[0]
    host_g = ((1, 2, 4), (0,), (3,))

    class WeightRide:
        def __init__(self, l, plan):
            self.l, self.plan = l, plan

        def comm_for(self, name):
            if name not in self.plan:
                return None
            lay, idx = self.plan[name]
            return _RowHalfGather([shards[lay][i] for i in idx])

        def done(self, name, got):
            lay, idx = self.plan[name]
            for i, r in zip(idx, got):
                gw[lay][i] = r

        def late_weights(self):
            g = gw[self.l]
            return dict(wb=jnp.transpose(g[1], (1, 0, 2)).reshape(3 * BRANCH_W, D_MODEL),
                        wout=g[2].reshape(D_MODEL, D_MODEL),
                        wfi=jnp.transpose(g[3], (1, 0, 2)).reshape(D_MODEL, 2 * FFN_H),
                        wfo=g[4].reshape(FFN_H, D_MODEL))

    weight_plan = [
        {"proj_qkv": (0, (1, 2)), "attn_a": (0, (4,)), "attn_b": (0, (3,)), "attn_c": (1, (5,)), "ffn_in": (1, (6,))},
        {"attn_a": (1, (1, 2)), "attn_b": (1, (3,)), "attn_c": (1, (4,))}]


    c_all = _all_gather8(c.reshape(8, LANE), "gather_c").reshape(8, D_MODEL)
    b_sh = lax.dynamic_slice_in_dim(b_ada, chip * n_ada, n_ada, axis=1)[:, None, :]
    mod_sh = _ada_fwd(_pad_rows(c_all, 16), w_ada, b_sh, "ada_fwd")[:, :8, :]
    mod_all = _all_gather8(mod_sh.reshape(-1, LANE), "gather_mod").reshape(8, DEPTH, 8, n_ada)
    mod_mine = lax.dynamic_index_in_dim(mod_all[0::2], dev, axis=2, keepdims=False)
    mod = mod_mine.transpose(1, 0, 2).reshape(DEPTH, 6, D_MODEL)

    alibi = _pair_table(_alibi_table())
    no_sink = jnp.full((N_HEADS, 8, LANE), NEG_INF, F32)
    def make_params(l):
        if gw[l][0] is None:
            gw[l][0] = jnp.concatenate([gw[l][5], gw[l][6]], axis=1)
        wqkv, wgf = _pack_w_in(gw[l][0])
        rel_tab = _rel_expand(jnp.pad(rel_bias[l], ((0, 0), (0, N_REL_PAD - N_REL))), "l%d_rel_expand" % l)
        return dict(
            wqkv=wqkv, wgf=wgf, norm_mix_g=norm_mix_g[l][None], norm_ffn_g=norm_ffn_g[l][None],
            b_forget_pad=jnp.pad(b_forget[l], (0, LANE - N_HEADS))[None],
            sink_tab=jnp.broadcast_to(sinks[l][:, None, None], (N_HEADS, 8, LANE)),
            no_sink=no_sink, alibi=alibi, rel_tab=_pair_table(jnp.transpose(rel_tab, (1, 0, 2))))

    mods = [[mod[l, k][None] for k in range(6)] for l in range(DEPTH)]
    params, saved = [None] * DEPTH, [None] * DEPTH
    h = xs
    for l in range(DEPTH):
        h, saved[l], params[l] = _layer_fwd(h, mods[l], make_params(l), l, WeightRide(l, weight_plan[l]))
    loss_dev, dh, d_final = _final_loss(h, final_norm_g[None], loss_target[0], "final_loss")
    grads = [None] * DEPTH
    dh, grads[1] = _layer_bwd(dh, mods[1], params[1], saved[1], 1)

    class Layer1Ride:
        sends = {"ffn_out_dx": (4,), "ffn_out_dw": (1, 2), "ffn_in_dx": (3,), "ffn_in_dw": (0,)}
        hands = {"proj_qkv_dx": (0,), "proj_gf_dx": (3,), "proj_qkv_dw": (4,), "proj_gf_dw": (1, 2)}

        def __init__(self, g):
            self.g, self.t = g, [None] * len(g)
            self.parts, self.final = [None] * len(g), [None] * len(g)

        def comm_for(self, name):
            if name in self.sends:
                return _SiblingSend([self.g[i] for i in self.sends[name]], 0)
            if name in self.hands:
                return _Handoff([self.parts[i] for i in self.hands[name]], 1, (0, 1, 2, 3))
            return None

        def done(self, name, got):
            idx, dst = (self.sends[name], self.t) if name in self.sends else (self.hands[name], self.final)
            for i, r in zip(idx, got):
                dst[i] = r

        def exchanges(self):
            sums = [_add_cast_on(a, b, 1, "grads_chip_sum_l1_" + n) for n, a, b in zip(big_names, self.g, self.t)]
            return tuple(_OwnerReduce([sums[i] for i in idx], 1) for idx in host_g)

        def exchanged(self, got):
            for res, idx in zip(got, host_g):
                for r, i in zip(res, idx):
                    self.parts[i] = r

    ride = Layer1Ride([grads[1][n] for n in big_names])
    dh, grads[0] = _layer_bwd(dh, mods[0], params[0], saved[0], 0, ride)
    grad_x = dh[None]
    loss = lax.psum(loss_dev[0, 0], ("x", "y", "c"))
    parts1 = ride.final
    g0 = [grads[0][n] for n in big_names]
    t0 = _sibling_swap_rows(g0, "grads_swap_l0")
    sums0 = [_add_cast_rows(a, b, "grads_chip_sum_l0_" + n) for n, a, b in zip(big_names, g0, t0)]
    parts0 = [None] + list(_RowHalfReduce(sums0[1:]).run("grads_reduce_l0"))

    small_names = ("norm_mix_g", "norm_ffn_g", "b_ada", "b_forget", "sinks", "rel_bias", "final_norm_g")
    small_w = dict(norm_mix_g=norm_mix_g, norm_ffn_g=norm_ffn_g, b_ada=b_ada, b_forget=b_forget, sinks=sinks,
                   rel_bias=rel_bias, final_norm_g=final_norm_g)
    small_m = dict(norm_mix_g=m_norm_mix_g, norm_ffn_g=m_norm_ffn_g, b_ada=m_b_ada, b_forget=m_b_forget,
                   sinks=m_sinks, rel_bias=m_rel_bias, final_norm_g=m_final_norm_g)
    small_v = dict(norm_mix_g=v_norm_mix_g, norm_ffn_g=v_norm_ffn_g, b_ada=v_b_ada, b_forget=v_b_forget,
                   sinks=v_sinks, rel_bias=v_rel_bias, final_norm_g=v_final_norm_g)
    small_g = dict(
        norm_mix_g=jnp.stack([grads[l]["norm_mix_g"] for l in range(DEPTH)]),
        norm_ffn_g=jnp.stack([grads[l]["norm_ffn_g"] for l in range(DEPTH)]),
        b_ada=jnp.stack([grads[l]["d_mod"] for l in range(DEPTH)]),
        b_forget=jnp.stack([grads[l]["b_forget"] for l in range(DEPTH)]),
        sinks=jnp.stack([grads[l]["sinks"] for l in range(DEPTH)]),
        rel_bias=jnp.stack([grads[l]["rel_bias"] for l in range(DEPTH)]),
        final_norm_g=d_final[0])
    shapes = [small_w[n].shape for n in small_names]
    g_all = _all_gather8(_small_pack([small_g[n] for n in small_names]), "gather_small_grads")
    res, _ = _adamw(_small_pack([small_w[n] for n in small_names])[None],
                    _small_pack([small_m[n] for n in small_names])[None],
                    _small_pack([small_v[n] for n in small_names])[None], g_all, "adamw_small")
    small_out = {n: [] for n in small_names}
    for r in res:
        for n, a in zip(small_names, _small_unpack(r[0], shapes)):
            small_out[n].append(a)
    off_b = sum(int(np.prod(s)) for s in shapes[:2])
    n_mod = DEPTH * 6 * D_MODEL
    dmod_all = g_all.reshape(8, -1)[:, off_b:off_b + n_mod].reshape(8, DEPTH, 6 * D_MODEL)
    dmod_sh = lax.dynamic_slice_in_dim(dmod_all, chip * n_ada, n_ada, axis=2).transpose(1, 0, 2)
    g_ada = _ada_bwd(c_all.T, dmod_sh, "ada_bwd")
    ada_out, got = _adamw(w_ada, m_w_ada, v_w_ada, flat2(g_ada)[None], "adamw_w_ada",
                          comm=_RowHalfReduce(sums0[:1]))
    parts0[0] = got[0]

    big_out = {}
    as3 = lambda a: a.reshape(a.shape[0], -1, a.shape[-1])
    for n, p0, p1 in zip(big_names, parts0, parts1):
        res, _ = _adamw(as3(big_w[n]), as3(big_m[n]), as3(big_v[n]), [p0, p1], "adamw_" + n)
        big_out[n] = [r.reshape(big_w[n].shape) for r in res]

    order = ("norm_mix_g", "norm_ffn_g", "w_ada", "b_ada", "w_in", "b_forget", "sinks", "rel_bias", "w_branch",
             "w_out", "w_ffn_in", "w_ffn_out", "final_norm_g")

    def pick(n, k):
        if n == "w_ada":
            return ada_out[k]
        if n in big_out:
            return big_out[n][k]
        return small_out[n][k]

    outs = [loss, grad_x]
    for k in range(4):
        outs += [pick(n, k) for n in order]
    return tuple(outs)
```

```python
import numpy as np
import jax
import jax.numpy as jnp
from jax import lax
from jax.experimental import pallas as pl
from jax.experimental.pallas import tpu as pltpu

F32 = jnp.float32
BF16 = jnp.bfloat16
SDS = jax.ShapeDtypeStruct

D_MODEL = 1024
DEPTH = 2
CHUNK = 64
HEAD_DIM = 64
EPS = 1e-6
NEG_INF = -1e30
N_HEADS = 8
A_KV_HEADS = 2
A_PREV = 2
C_PREV = 8
REL_CLIP = 128
N_REL = 2 * REL_CLIP + 1
N_REL_PAD = 384
BRANCH_W = 512
FFN_H = 2816
FOX_BQ = 256
FOX_BK = 512
GF_COLS = 3200
N_IN_COLS = 6920
LANE = 128
VMEM_LIMIT = 48 * 1024 * 1024

ADAM_LR = 0.001
ADAM_B1 = 0.9
ADAM_B2 = 0.999
ADAM_EPS = 1e-08
ADAM_WD = 0.01
ADAM_STEP = 10

MESH = pl.DeviceIdType.MESH
ANY = pl.BlockSpec(memory_space=pl.ANY)
VMEM_SPEC = pl.BlockSpec(memory_space=pltpu.VMEM)


def _cparams(sem=None):
    return pltpu.CompilerParams(dimension_semantics=sem, vmem_limit_bytes=VMEM_LIMIT)


def _blk(n, cap):
    if n <= cap:
        return n
    best = None
    for m in range(LANE, cap + 1, LANE):
        if n % m == 0:
            best = m
    assert best is not None, (n, cap)
    return best


def _sigmoid(x):
    return 1.0 / (1.0 + jnp.exp(-x))


def _mm(a, b, *, mode, out_dtype, name, groups=1, cap_m=2048, cap_n=1024, cap_k=1408, col_quarters=False,
        comm=None):
    G = groups
    assert not col_quarters or mode == "tn"
    if mode == "nn":
        M, K, N = a.shape[0], a.shape[1] // G, b.shape[1]
        assert b.shape[0] == G * K
    elif mode == "nt":
        M, K, N = a.shape[0], a.shape[1] // G, b.shape[0] // G
        assert b.shape[1] == K
    else:
        K, M, N = a.shape[0], a.shape[1] // G, b.shape[1] // G
        assert b.shape[0] == K
    bm, bn, bk = _blk(M, cap_m), _blk(N // 4 if col_quarters else N, cap_n), _blk(K, cap_k)
    nm, nn, nk = M // bm, N // bn, K // bk
    if mode == "nn":
        a_spec = pl.BlockSpec((bm, bk), lambda g, i, j, k: (i, g * nk + k))
        b_spec = pl.BlockSpec((bk, bn), lambda g, i, j, k: (g * nk + k, j))
        o_spec = pl.BlockSpec((bm, bn), lambda g, i, j, k: (i, g * nn + j))
        dims = (((1,), (0,)), ((), ()))
        out_shape = (M, G * N)
    elif mode == "nt":
        a_spec = pl.BlockSpec((bm, bk), lambda g, i, j, k: (i, g * nk + k))
        b_spec = pl.BlockSpec((bn, bk), lambda g, i, j, k: (g * nn + j, k))
        o_spec = pl.BlockSpec((bm, bn), lambda g, i, j, k: (i, g * nn + j))
        dims = (((1,), (1,)), ((), ()))
        out_shape = (M, G * N)
    else:
        a_spec = pl.BlockSpec((bk, bm), lambda g, i, j, k: (k, g * nm + i))
        b_spec = pl.BlockSpec((bk, bn), lambda g, i, j, k: (k, g * nn + j))
        dims = (((0,), (0,)), ((), ()))
        if col_quarters:
            nq = nn // 4
            o_spec = pl.BlockSpec((1, bm, bn), lambda g, i, j, k: (j // nq, g * nm + i, j % nq))
            out_shape = (4, G * M, N // 4)
        else:
            o_spec = pl.BlockSpec((bm, bn), lambda g, i, j, k: (g * nm + i, j))
            out_shape = (G * M, N)

    def product(a_ref, b_ref):
        return lax.dot_general(a_ref[...].astype(BF16), b_ref[...].astype(BF16), dims, preferred_element_type=F32)

    def body_one(a_ref, b_ref, o_ref):
        o_ref[...] = product(a_ref, b_ref).astype(o_ref.dtype).reshape(o_ref.shape)

    def body_acc(a_ref, b_ref, o_ref, acc_ref):
        k = pl.program_id(3)

        @pl.when(k == 0)
        def _():
            acc_ref[...] = jnp.zeros_like(acc_ref)

        acc_ref[...] += product(a_ref, b_ref)

        @pl.when(k == nk - 1)
        def _():
            o_ref[...] = acc_ref[...].astype(o_ref.dtype).reshape(o_ref.shape)

    res, got = _call_hosting(
        body_one if nk == 1 else body_acc, comm=comm, grid=(G, nm, nn, nk), in_specs=[a_spec, b_spec],
        out_specs=[o_spec], out_shape=[SDS(out_shape, out_dtype)],
        scratch_shapes=[] if nk == 1 else [pltpu.VMEM((bm, bn), F32)], name=name, args=(a, b),
        semantics=("parallel", "parallel", "parallel", "arbitrary"))
    return res[0] if comm is None else (res[0], got)


def _rows(tm, n, col=0):
    return pl.BlockSpec((tm, n), lambda i: (i, col))


def _vec(n):
    return pl.BlockSpec((1, n), lambda i: (0, 0))


def _tm(S):
    return min(S, 256)


def _norm_mod_fwd(x, g, sc, sh, name):
    S, Dm = x.shape
    tm = _tm(S)

    def body(x_ref, g_ref, sc_ref, sh_ref, h_ref):
        xv = x_ref[...]
        r = lax.rsqrt(jnp.mean(xv * xv, axis=-1, keepdims=True) + EPS)
        h_ref[...] = ((xv * r) * g_ref[...] * (1.0 + sc_ref[...]) + sh_ref[...]).astype(h_ref.dtype)

    return pl.pallas_call(
        body, grid=(S // tm,), in_specs=[_rows(tm, Dm), _vec(Dm), _vec(Dm), _vec(Dm)],
        out_specs=_rows(tm, Dm), out_shape=SDS((S, Dm), BF16),
        compiler_params=_cparams(("parallel",)), name=name)(x, g, sc, sh)


def _norm_mod_bwd(x, dh_list, dres, g, sc, name):
    S, Dm = x.shape
    tm = _tm(S)
    nh = len(dh_list)

    def body(*refs):
        x_ref = refs[0]
        dh_refs = refs[1:1 + nh]
        dres_ref, g_ref, sc_ref, dx_ref, dsc_ref, dsh_ref, dg_ref = refs[1 + nh:]
        i = pl.program_id(0)

        @pl.when(i == 0)
        def _():
            dsc_ref[...] = jnp.zeros_like(dsc_ref)
            dsh_ref[...] = jnp.zeros_like(dsh_ref)
            dg_ref[...] = jnp.zeros_like(dg_ref)

        xv = x_ref[...]
        dh = dh_refs[0][...]
        for r_ in dh_refs[1:]:
            dh = dh + r_[...]
        gv = g_ref[...]
        r = lax.rsqrt(jnp.mean(xv * xv, axis=-1, keepdims=True) + EPS)
        xn = xv * r
        xg = xn * gv
        dsh_ref[...] += jnp.sum(dh, axis=0, keepdims=True)
        dsc_ref[...] += jnp.sum(dh * xg, axis=0, keepdims=True)
        dxg = dh * (1.0 + sc_ref[...])
        dg_ref[...] += jnp.sum(dxg * xn, axis=0, keepdims=True)
        dxn = dxg * gv
        dx_ref[...] = dres_ref[...] + r * (dxn - xn * jnp.mean(dxn * xn, axis=-1, keepdims=True))

    return pl.pallas_call(
        body, grid=(S // tm,),
        in_specs=[_rows(tm, Dm)] * (2 + nh) + [_vec(Dm), _vec(Dm)],
        out_specs=[_rows(tm, Dm), _vec(Dm), _vec(Dm), _vec(Dm)],
        out_shape=[SDS((S, Dm), F32), SDS((1, Dm), F32), SDS((1, Dm), F32), SDS((1, Dm), F32)],
        compiler_params=_cparams(("arbitrary",)), name=name)(x, *dh_list, dres, g, sc)


def _resid_fwd(x, val, g, name):
    S, Dm = x.shape
    tm = _tm(S)

    def body(x_ref, v_ref, g_ref, o_ref):
        o_ref[...] = x_ref[...] + g_ref[...] * v_ref[...]

    return pl.pallas_call(
        body, grid=(S // tm,), in_specs=[_rows(tm, Dm), _rows(tm, Dm), _vec(Dm)],
        out_specs=_rows(tm, Dm), out_shape=SDS((S, Dm), F32),
        compiler_params=_cparams(("parallel",)), name=name)(x, val, g)


def _resid_bwd(dx, val, g, name):
    S, Dm = dx.shape
    tm = _tm(S)

    def body(dx_ref, v_ref, g_ref, dg_ref, dv_ref):
        @pl.when(pl.program_id(0) == 0)
        def _():
            dg_ref[...] = jnp.zeros_like(dg_ref)

        dxv = dx_ref[...]
        dg_ref[...] += jnp.sum(dxv * v_ref[...], axis=0, keepdims=True)
        dv_ref[...] = (dxv * g_ref[...]).astype(dv_ref.dtype)

    return pl.pallas_call(
        body, grid=(S // tm,), in_specs=[_rows(tm, Dm), _rows(tm, Dm), _vec(Dm)],
        out_specs=[_vec(Dm), _rows(tm, Dm)], out_shape=[SDS((1, Dm), F32), SDS((S, Dm), BF16)],
        compiler_params=_cparams(("arbitrary",)), name=name)(dx, val, g)


def _merge_fwd(y, gf, name):
    S = y.shape[0]
    tm = _tm(S)
    W = 3 * D_MODEL

    def body(y_ref, g_ref, o_ref):
        acc = None
        for k in range(3):
            sl = slice(k * D_MODEL, (k + 1) * D_MODEL)
            t = _sigmoid(g_ref[:, sl]) * y_ref[:, sl].astype(F32)
            acc = t if acc is None else acc + t
        o_ref[...] = acc.astype(o_ref.dtype)

    return pl.pallas_call(
        body, grid=(S // tm,), in_specs=[_rows(tm, W), _rows(tm, W)],
        out_specs=_rows(tm, D_MODEL), out_shape=SDS((S, D_MODEL), BF16),
        compiler_params=_cparams(("parallel",)), name=name)(y, gf)


def _merge_bwd(dm, y, gf, name):
    S = y.shape[0]
    tm = _tm(S)
    W = 3 * D_MODEL

    def body(dm_ref, y_ref, g_ref, dy_ref, dg_ref):
        dmv = dm_ref[...]
        for k in range(3):
            sl = slice(k * D_MODEL, (k + 1) * D_MODEL)
            sg = _sigmoid(g_ref[:, sl])
            dy_ref[:, sl] = (dmv * sg).astype(dy_ref.dtype)
            dg_ref[:, sl] = (dmv * y_ref[:, sl].astype(F32) * (sg * (1.0 - sg))).astype(dg_ref.dtype)

    return pl.pallas_call(
        body, grid=(S // tm,), in_specs=[_rows(tm, D_MODEL), _rows(tm, W), _rows(tm, W)],
        out_specs=[_rows(tm, W), _rows(tm, W)], out_shape=[SDS((S, W), BF16), SDS((S, W), BF16)],
        compiler_params=_cparams(("parallel",)), name=name)(dm, y, gf)


def _swiglu_fwd(u, name):
    S = u.shape[0]
    tm = _tm(S)

    def body(g_ref, u_ref, a_ref):
        gv = g_ref[...].astype(F32)
        a_ref[...] = (gv * _sigmoid(gv) * u_ref[...].astype(F32)).astype(a_ref.dtype)

    return pl.pallas_call(
        body, grid=(S // tm,), in_specs=[_rows(tm, FFN_H, 0), _rows(tm, FFN_H, 1)],
        out_specs=_rows(tm, FFN_H), out_shape=SDS((S, FFN_H), BF16),
        compiler_params=_cparams(("parallel",)), name=name)(u, u)


def _swiglu_bwd(da, u, name):
    S = u.shape[0]
    tm = _tm(S)

    def body(da_ref, g_ref, u_ref, du_ref):
        dav = da_ref[...].astype(F32)
        gv = g_ref[...].astype(F32)
        sg = _sigmoid(gv)
        du_ref[:, 0:FFN_H] = (dav * u_ref[...].astype(F32) * (sg * (1.0 + gv * (1.0 - sg)))).astype(du_ref.dtype)
        du_ref[:, FFN_H:2 * FFN_H] = (dav * (gv * sg)).astype(du_ref.dtype)

    return pl.pallas_call(
        body, grid=(S // tm,), in_specs=[_rows(tm, FFN_H), _rows(tm, FFN_H, 0), _rows(tm, FFN_H, 1)],
        out_specs=_rows(tm, 2 * FFN_H), out_shape=SDS((S, 2 * FFN_H), BF16),
        compiler_params=_cparams(("parallel",)), name=name)(da, u, u)


def _final_loss(x, g, target, name):
    S, Dm = x.shape
    tm = _tm(S)

    def body(x_ref, g_ref, t_ref, loss_ref, dx_ref, dg_ref):
        @pl.when(pl.program_id(0) == 0)
        def _():
            loss_ref[...] = jnp.zeros_like(loss_ref)
            dg_ref[...] = jnp.zeros_like(dg_ref)

        xv = x_ref[...]
        gv = g_ref[...]
        r = lax.rsqrt(jnp.mean(xv * xv, axis=-1, keepdims=True) + EPS)
        xn = xv * r
        err = xn * gv - t_ref[...]
        row = jnp.mean(err * err, axis=-1, keepdims=True)
        loss_ref[...] += 0.5 * jnp.sum(row, axis=0, keepdims=True)
        dy = err * (1.0 / Dm)
        dg_ref[...] += jnp.sum(dy * xn, axis=0, keepdims=True)
        dxn = dy * gv
        dx_ref[...] = r * (dxn - xn * jnp.mean(dxn * xn, axis=-1, keepdims=True))

    return pl.pallas_call(
        body, grid=(S // tm,), in_specs=[_rows(tm, Dm), _vec(Dm), _rows(tm, Dm)],
        out_specs=[pl.BlockSpec((1, 1), lambda i: (0, 0)), _rows(tm, Dm), _vec(Dm)],
        out_shape=[SDS((1, 1), F32), SDS((S, Dm), F32), SDS((1, Dm), F32)],
        compiler_params=_cparams(("arbitrary",)), name=name)(x, g, target)


PAIR = 2 * CHUNK


def _bandT_softmax(kg, qTg, bias, sink, valid):
    s = jnp.dot(kg, qTg, preferred_element_type=F32)
    s = jnp.where(valid, s + bias, NEG_INF)
    m = jnp.maximum(jnp.max(s, axis=0, keepdims=True), sink)
    e = jnp.exp(s - m)
    es = jnp.exp(sink - m)
    inv = 1.0 / (jnp.sum(e, axis=0, keepdims=True) + es)
    return e * inv, es * inv


def _pad_copy_rows(dst, src, pad, S):
    dst[:, 0:pad, :] = jnp.zeros((dst.shape[0], pad, dst.shape[2]), dst.dtype)
    dst[:, pad:pad + S, :] = src[...]


def _pad_copy_lanes(dst, src, pad, S):
    dst[:, 0:pad] = jnp.zeros((dst.shape[0], pad), dst.dtype)
    dst[:, pad:pad + S] = src[...]


def _fm(arg):
    return arg if isinstance(arg, tuple) else (arg, 0)


def _fm_spec(rows, S, row0):
    off, rem = divmod(row0, rows)
    assert rem == 0
    return pl.BlockSpec((rows, S), lambda i: (off + i, 0))


def _bandT_fwd(qT, k_h, vT, bias, sink, *, GQ, GK, P, kvoff, name, comm=None):
    (qT, q0), (vT, v0) = _fm(qT), _fm(vT)
    S = qT.shape[1]
    ng = bias.shape[0] // GQ
    BU = (P + 2) * CHUNK
    pad = P * CHUNK
    npair = S // PAIR

    def body(qT_ref, k_ref, vT_ref, b_ref, s_ref, oT_ref, kp, vTp):
        _pad_copy_rows(kp, k_ref, pad, S)
        _pad_copy_lanes(vTp, vT_ref, pad, S)
        rowi = lax.broadcasted_iota(jnp.int32, (BU, PAIR), 0)

        def step(n2, carry):
            r = pl.multiple_of(n2 * PAIR, PAIR)
            valid = rowi >= (P - 2 * n2) * CHUNK
            for g in range(GQ):
                kv = kvoff(g)
                hs = slice(g * HEAD_DIM, (g + 1) * HEAD_DIM)
                kvs = slice(kv * HEAD_DIM, (kv + 1) * HEAD_DIM)
                qTg = qT_ref[hs, pl.ds(r, PAIR)] * 0.125
                p, _ = _bandT_softmax(kp[kv, pl.ds(r, BU), :], qTg, b_ref[g], s_ref[g, 0:1, :], valid)
                oTg = jnp.dot(vTp[kvs, pl.ds(r, BU)], p.astype(BF16), preferred_element_type=F32)
                oT_ref[hs, pl.ds(r, PAIR)] = oTg.astype(oT_ref.dtype)
            return carry

        lax.fori_loop(0, npair, step, 0, unroll=min(2, npair))

    res, got = _call_hosting(
        body, comm=comm, grid=(ng,),
        in_specs=[_fm_spec(GQ * HEAD_DIM, S, q0),
                  pl.BlockSpec((GK, S, HEAD_DIM), lambda i: (i, 0, 0)),
                  _fm_spec(GK * HEAD_DIM, S, v0),
                  pl.BlockSpec((GQ, BU, PAIR), lambda i: (i, 0, 0)),
                  pl.BlockSpec((GQ, 8, LANE), lambda i: (i, 0, 0))],
        out_specs=[pl.BlockSpec((GQ * HEAD_DIM, S), lambda i: (i, 0))],
        out_shape=[SDS((ng * GQ * HEAD_DIM, S), BF16)],
        scratch_shapes=[pltpu.VMEM((GK, S + pad, HEAD_DIM), BF16), pltpu.VMEM((GK * HEAD_DIM, S + pad), BF16)],
        name=name, args=(qT, k_h, vT, bias, sink))
    return res[0], got


def _bandT_bwd(qT, q_h, k_h, kT, v_h, doT, do_h, bias, sink, *, GQ, GK, P, kvoff, name, comm=None):
    (qT, q0), (kT, k0), (doT, d0) = _fm(qT), _fm(kT), _fm(doT)
    S = qT.shape[1]
    ng = bias.shape[0] // GQ
    BU = (P + 2) * CHUNK
    pad = P * CHUNK
    npair = S // PAIR

    def body(qT_ref, q_ref, k_ref, kT_ref, v_ref, doT_ref, do_ref, b_ref, s_ref,
             dqT_ref, dk_ref, dv_ref, db_ref, dsk_ref, kp, kTp, vp, dkp, dvp):
        _pad_copy_rows(kp, k_ref, pad, S)
        _pad_copy_rows(vp, v_ref, pad, S)
        _pad_copy_lanes(kTp, kT_ref, pad, S)
        dkp[...] = jnp.zeros_like(dkp)
        dvp[...] = jnp.zeros_like(dvp)
        db_ref[...] = jnp.zeros_like(db_ref)
        rowi = lax.broadcasted_iota(jnp.int32, (BU, PAIR), 0)

        def step(n2, dsink):
            r = pl.multiple_of(n2 * PAIR, PAIR)
            valid = rowi >= (P - 2 * n2) * CHUNK
            new = []
            for g in range(GQ):
                kv = kvoff(g)
                hs = slice(g * HEAD_DIM, (g + 1) * HEAD_DIM)
                kvs = slice(kv * HEAD_DIM, (kv + 1) * HEAD_DIM)
                qTg = qT_ref[hs, pl.ds(r, PAIR)] * 0.125
                p, ps = _bandT_softmax(kp[kv, pl.ds(r, BU), :], qTg, b_ref[g], s_ref[g, 0:1, :], valid)
                dp = jnp.dot(vp[kv, pl.ds(r, BU), :], doT_ref[hs, pl.ds(r, PAIR)], preferred_element_type=F32)
                delta = jnp.sum(p * dp, axis=0, keepdims=True)
                ds = p * (dp - delta)
                new.append(dsink[g] - ps * delta)
                db_ref[g] += ds
                dsb = ds.astype(BF16)
                dq = jnp.dot(kTp[kvs, pl.ds(r, BU)], dsb, preferred_element_type=F32) * 0.125
                dqT_ref[hs, pl.ds(r, PAIR)] = dq.astype(dqT_ref.dtype)
                dkp[kv, pl.ds(r, BU), :] += jnp.dot(dsb, q_ref[g, pl.ds(r, PAIR), :] * 0.125,
                                                    preferred_element_type=F32)
                dvp[kv, pl.ds(r, BU), :] += jnp.dot(p.astype(BF16), do_ref[g, pl.ds(r, PAIR), :],
                                                    preferred_element_type=F32)
            return tuple(new)

        dsink = lax.fori_loop(0, npair, step, tuple(jnp.zeros((1, PAIR), F32) for _ in range(GQ)))
        for g in range(GQ):
            dsk_ref[g] = jnp.broadcast_to(jnp.sum(dsink[g], axis=1, keepdims=True), (8, LANE))
        dk_ref[...] = dkp[:, pad:pad + S, :].astype(dk_ref.dtype)
        dv_ref[...] = dvp[:, pad:pad + S, :].astype(dv_ref.dtype)

    qTs = pl.BlockSpec((GQ * HEAD_DIM, S), lambda i: (i, 0))
    qhs = pl.BlockSpec((GQ, S, HEAD_DIM), lambda i: (i, 0, 0))
    khs = pl.BlockSpec((GK, S, HEAD_DIM), lambda i: (i, 0, 0))
    bs = pl.BlockSpec((GQ, BU, PAIR), lambda i: (i, 0, 0))
    ss = pl.BlockSpec((GQ, 8, LANE), lambda i: (i, 0, 0))
    nkv = ng * GK
    return _call_hosting(
        body, comm=comm, grid=(ng,),
        in_specs=[_fm_spec(GQ * HEAD_DIM, S, q0), qhs, khs, _fm_spec(GK * HEAD_DIM, S, k0), khs,
                  _fm_spec(GQ * HEAD_DIM, S, d0), qhs, bs, ss],
        out_specs=[qTs, khs, khs, bs, ss],
        out_shape=[SDS((ng * GQ * HEAD_DIM, S), BF16), SDS((nkv, S, HEAD_DIM), BF16), SDS((nkv, S, HEAD_DIM), BF16),
                   SDS((ng * GQ, BU, PAIR), F32), SDS((ng * GQ, 8, LANE), F32)],
        scratch_shapes=[pltpu.VMEM((GK, S + pad, HEAD_DIM), BF16), pltpu.VMEM((GK * HEAD_DIM, S + pad), BF16),
                        pltpu.VMEM((GK, S + pad, HEAD_DIM), BF16),
                        pltpu.VMEM((GK, S + pad, HEAD_DIM), F32), pltpu.VMEM((GK, S + pad, HEAD_DIM), F32)],
        name=name, args=(qT, q_h, k_h, kT, v_h, doT, do_h, bias, sink))


def _pair_table(tab):
    t = jnp.transpose(tab, (0, 2, 1))
    lo = jnp.pad(t, ((0, 0), (0, CHUNK), (0, 0)), constant_values=NEG_INF)
    hi = jnp.pad(t, ((0, 0), (CHUNK, 0), (0, 0)), constant_values=NEG_INF)
    return jnp.concatenate([lo, hi], axis=2)


def _unpair_table(d):
    band = d.shape[1] - CHUNK
    return jnp.transpose(d[:, 0:band, 0:CHUNK] + d[:, CHUNK:CHUNK + band, CHUNK:PAIR], (0, 2, 1))


def _heads(a, n):
    return jnp.transpose(a.reshape(a.shape[0], n, HEAD_DIM), (1, 0, 2))


def _unheads(a):
    return jnp.transpose(a, (1, 0, 2)).reshape(a.shape[1], a.shape[0] * HEAD_DIM)


def _foxT_logits(kj, qTg, cq, ck, r, c, rowi, coli):
    s = jnp.dot(kj, qTg, preferred_element_type=F32)
    s = s + cq - ck
    return jnp.where(c + rowi <= r + coli, s, NEG_INF)


def _foxT_fwd(qT, k_h, vT, ck, cq, name, comm=None):
    (qT, q0), (vT, v0) = _fm(qT), _fm(vT)
    S = qT.shape[1]
    npair = k_h.shape[0] // 2
    BQ, BK = min(FOX_BQ, S), min(FOX_BK, S)
    nq = S // BQ
    heads = [slice(g * HEAD_DIM, (g + 1) * HEAD_DIM) for g in range(2)]

    def body(qT_ref, k_ref, vT_ref, ck_ref, cq_ref, oT_ref, lse_ref):
        rowi = lax.broadcasted_iota(jnp.int32, (BK, BQ), 0)
        coli = lax.broadcasted_iota(jnp.int32, (BK, BQ), 1)

        def qstep(i, carry):
            r = pl.multiple_of(i * BQ, BQ)
            qs = [qT_ref[hs, pl.ds(r, BQ)] * 0.125 for hs in heads]
            cqs = [cq_ref[g, :, pl.ds(r, BQ)] for g in range(2)]

            def kstep(j, st):
                c = pl.multiple_of(j * BK, BK)
                new = []
                for g, hs in enumerate(heads):
                    m, l, acc = st[g]
                    s = _foxT_logits(k_ref[g, pl.ds(c, BK), :], qs[g], cqs[g], ck_ref[g, pl.ds(c, BK), :],
                                     r, c, rowi, coli)
                    mn = jnp.maximum(m, jnp.max(s, axis=0, keepdims=True))
                    al = jnp.exp(m - mn)
                    e = jnp.exp(s - mn)
                    l = al * l + jnp.sum(e, axis=0, keepdims=True)
                    acc = al * acc + jnp.dot(vT_ref[hs, pl.ds(c, BK)], e.astype(BF16), preferred_element_type=F32)
                    new.append((mn, l, acc))
                return tuple(new)

            init = (jnp.full((1, BQ), NEG_INF, F32), jnp.zeros((1, BQ), F32), jnp.zeros((HEAD_DIM, BQ), F32))
            st = lax.fori_loop(0, (r + BQ + BK - 1) // BK, kstep, (init, init))
            for g, hs in enumerate(heads):
                m, l, acc = st[g]
                oT_ref[hs, pl.ds(r, BQ)] = (acc * (1.0 / l)).astype(oT_ref.dtype)
                lse_ref[g, :, pl.ds(r, BQ)] = m + jnp.log(l)
            return carry

        lax.fori_loop(0, nq, qstep, 0)

    fT = pl.BlockSpec((LANE, S), lambda i: (i, 0))
    hm = pl.BlockSpec((2, S, HEAD_DIM), lambda i: (i, 0, 0))
    col = pl.BlockSpec((2, S, 1), lambda i: (i, 0, 0))
    rw = pl.BlockSpec((2, 1, S), lambda i: (i, 0, 0))
    return _call_hosting(
        body, comm=comm, grid=(npair,), in_specs=[_fm_spec(LANE, S, q0), hm, _fm_spec(LANE, S, v0), col, rw],
        out_specs=[fT, rw],
        out_shape=[SDS((npair * LANE, S), BF16), SDS((2 * npair, 1, S), F32)], scratch_shapes=[],
        name=name, args=(qT, k_h, vT, ck, cq))


def _foxT_bwd(qT, q_h, k_h, kT, v_h, ck, cq, oT, doT, do_h, lse, name, comm=None):
    (qT, q0), (kT, k0), (doT, d0) = _fm(qT), _fm(kT), _fm(doT)
    S = qT.shape[1]
    npair = k_h.shape[0] // 2
    BQ, BK = min(FOX_BQ, S), min(FOX_BK, S)
    nq = S // BQ
    heads = [slice(g * HEAD_DIM, (g + 1) * HEAD_DIM) for g in range(2)]

    def body(qT_ref, q_ref, k_ref, kT_ref, v_ref, ck_ref, cq_ref, oT_ref, doT_ref, do_ref, lse_ref,
             dqT_ref, dk_ref, dv_ref, dck_ref, dcq_ref, dka, dva, qa_ref):
        qa_ref[:, :, 0:HEAD_DIM] = q_ref[...] * 0.125
        qa_ref[:, :, HEAD_DIM:LANE] = jnp.ones((2, S, LANE - HEAD_DIM), BF16)
        dka[...] = jnp.zeros_like(dka)
        dva[...] = jnp.zeros_like(dva)
        rowi = lax.broadcasted_iota(jnp.int32, (BK, BQ), 0)
        coli = lax.broadcasted_iota(jnp.int32, (BK, BQ), 1)

        def qstep(i, carry):
            r = pl.multiple_of(i * BQ, BQ)
            qs = [qT_ref[hs, pl.ds(r, BQ)] * 0.125 for hs in heads]
            dos = [doT_ref[hs, pl.ds(r, BQ)] for hs in heads]
            deltas = [jnp.sum(dos[g].astype(F32) * oT_ref[hs, pl.ds(r, BQ)].astype(F32), axis=0, keepdims=True)
                      for g, hs in enumerate(heads)]
            cqs = [cq_ref[g, :, pl.ds(r, BQ)] for g in range(2)]
            lses = [lse_ref[g, :, pl.ds(r, BQ)] for g in range(2)]

            def kstep(j, st):
                c = pl.multiple_of(j * BK, BK)
                new = []
                for g, hs in enumerate(heads):
                    dq, rs = st[g]
                    s = _foxT_logits(k_ref[g, pl.ds(c, BK), :], qs[g], cqs[g], ck_ref[g, pl.ds(c, BK), :],
                                     r, c, rowi, coli)
                    p = jnp.exp(s - lses[g])
                    dp = jnp.dot(v_ref[g, pl.ds(c, BK), :], dos[g], preferred_element_type=F32)
                    ds = p * (dp - deltas[g])
                    dsb = ds.astype(BF16)
                    dka[g, pl.ds(c, BK), :] += jnp.dot(dsb, qa_ref[g, pl.ds(r, BQ), :], preferred_element_type=F32)
                    dva[g, pl.ds(c, BK), :] += jnp.dot(p.astype(BF16), do_ref[g, pl.ds(r, BQ), :],
                                                      preferred_element_type=F32)
                    new.append((dq + jnp.dot(kT_ref[hs, pl.ds(c, BK)], dsb, preferred_element_type=F32),
                                rs + jnp.sum(dsb.astype(F32), axis=0, keepdims=True)))
                return tuple(new)

            init = (jnp.zeros((HEAD_DIM, BQ), F32), jnp.zeros((1, BQ), F32))
            st = lax.fori_loop(0, (r + BQ + BK - 1) // BK, kstep, (init, init))
            for g, hs in enumerate(heads):
                dqT_ref[hs, pl.ds(r, BQ)] = (st[g][0] * 0.125).astype(dqT_ref.dtype)
                dcq_ref[g, :, pl.ds(r, BQ)] = st[g][1]
            return carry

        lax.fori_loop(0, nq, qstep, 0)
        dk_ref[...] = dka[:, :, 0:HEAD_DIM].astype(dk_ref.dtype)
        dck_ref[...] = -dka[:, :, HEAD_DIM:HEAD_DIM + 1]
        dv_ref[...] = dva[...].astype(dv_ref.dtype)

    fT = pl.BlockSpec((LANE, S), lambda i: (i, 0))
    hm = pl.BlockSpec((2, S, HEAD_DIM), lambda i: (i, 0, 0))
    col = pl.BlockSpec((2, S, 1), lambda i: (i, 0, 0))
    rw = pl.BlockSpec((2, 1, S), lambda i: (i, 0, 0))
    nh = 2 * npair
    return _call_hosting(
        body, comm=comm, grid=(npair,),
        in_specs=[_fm_spec(LANE, S, q0), hm, hm, _fm_spec(LANE, S, k0), hm, col, rw, fT, _fm_spec(LANE, S, d0), hm, rw],
        out_specs=[fT, hm, hm, col, rw],
        out_shape=[SDS((npair * LANE, S), BF16), SDS((nh, S, HEAD_DIM), BF16), SDS((nh, S, HEAD_DIM), BF16),
                   SDS((nh, S, 1), F32), SDS((nh, 1, S), F32)],
        scratch_shapes=[pltpu.VMEM((2, S, LANE), F32), pltpu.VMEM((2, S, HEAD_DIM), F32),
                        pltpu.VMEM((2, S, LANE), BF16)],
        name=name, args=(qT, q_h, k_h, kT, v_h, ck, cq, oT, doT, do_h, lse))


def _split3(x):
    hi = x.astype(BF16)
    r1 = x - hi.astype(F32)
    mid = r1.astype(BF16)
    lo = (r1 - mid.astype(F32)).astype(BF16)
    return hi, mid, lo


def _tri_dot(tri, x):
    hi, mid, lo = _split3(x)
    return (jnp.dot(tri, hi, preferred_element_type=F32) + jnp.dot(tri, mid, preferred_element_type=F32)
            + jnp.dot(tri, lo, preferred_element_type=F32))


def _fox_cum(gf, bfo, name):
    S = gf.shape[0]
    nb = S // LANE
    fcol = (GF_COLS - LANE) // LANE

    def body(f_ref, b_ref, cum_ref):
        row = lax.broadcasted_iota(jnp.int32, (LANE, LANE), 0)
        col = lax.broadcasted_iota(jnp.int32, (LANE, LANE), 1)
        tri = jnp.where(row >= col, 1.0, 0.0).astype(BF16)
        carry = jnp.zeros((1, LANE), F32)
        for t in range(nb):
            xl = f_ref[t * LANE:(t + 1) * LANE, :] + b_ref[...]
            lf = jnp.minimum(xl, 0.0) - jnp.log(1.0 + jnp.exp(-jnp.abs(xl)))
            cblk = _tri_dot(tri, lf) + carry
            cum_ref[t * LANE:(t + 1) * LANE, :] = cblk
            carry = cblk[LANE - 1:LANE, :]

    return pl.pallas_call(
        body, grid=(1,), in_specs=[pl.BlockSpec((S, LANE), lambda i: (0, fcol)), _vec(LANE)],
        out_specs=pl.BlockSpec((S, LANE), lambda i: (0, 0)), out_shape=SDS((S, LANE), F32),
        compiler_params=_cparams(("arbitrary",)), name=name)(gf, bfo)


def _fox_cum_bwd(gf, bfo, dcum, name):
    S = gf.shape[0]
    nb = S // LANE
    fcol = (GF_COLS - LANE) // LANE

    def body(f_ref, b_ref, dc_ref, df_ref, db_ref):
        row = lax.broadcasted_iota(jnp.int32, (LANE, LANE), 0)
        col = lax.broadcasted_iota(jnp.int32, (LANE, LANE), 1)
        tri = jnp.where(row <= col, 1.0, 0.0).astype(BF16)
        carry = jnp.zeros((1, LANE), F32)
        tot = jnp.zeros((1, LANE), F32)
        for t in range(nb - 1, -1, -1):
            rows = slice(t * LANE, (t + 1) * LANE)
            dlf = _tri_dot(tri, dc_ref[rows, :]) + carry
            carry = dlf[0:1, :]
            xl = f_ref[rows, :] + b_ref[...]
            dfl = dlf * (1.0 / (1.0 + jnp.exp(xl)))
            df_ref[rows, :] = dfl.astype(df_ref.dtype)
            tot = tot + jnp.sum(dfl, axis=0, keepdims=True)
        db_ref[...] = tot

    return pl.pallas_call(
        body, grid=(1,),
        in_specs=[pl.BlockSpec((S, LANE), lambda i: (0, fcol)), _vec(LANE), pl.BlockSpec((S, LANE), lambda i: (0, 0))],
        out_specs=[pl.BlockSpec((S, LANE), lambda i: (0, 0)), _vec(LANE)],
        out_shape=[SDS((S, LANE), BF16), SDS((1, LANE), F32)],
        compiler_params=_cparams(("arbitrary",)), name=name)(gf, bfo, dcum)


REL_FAR = C_PREV * CHUNK - REL_CLIP


def _rel_onehot(qi, band):
    w = band - REL_FAR
    r = lax.broadcasted_iota(jnp.int32, (N_REL_PAD, w), 0)
    j = lax.broadcasted_iota(jnp.int32, (N_REL_PAD, w), 1) + REL_FAR
    idx = jnp.clip(C_PREV * CHUNK + qi - j, -REL_CLIP, REL_CLIP) + REL_CLIP
    return jnp.where(r == idx, 1.0, 0.0).astype(BF16)


def _rel_expand(rel, name):
    band = (C_PREV + 1) * CHUNK

    def body(rel_ref, o_ref):
        hi, mid, lo = _split3(rel_ref[...])
        far = jnp.broadcast_to(rel_ref[:, 2 * REL_CLIP:2 * REL_CLIP + 1], (N_HEADS, REL_FAR))

        def row(qi, carry):
            oh = _rel_onehot(qi, band)
            o_ref[qi, :, 0:REL_FAR] = far
            o_ref[qi, :, REL_FAR:band] = (jnp.dot(hi, oh, preferred_element_type=F32)
                                          + jnp.dot(mid, oh, preferred_element_type=F32)
                                          + jnp.dot(lo, oh, preferred_element_type=F32))
            return carry

        lax.fori_loop(0, CHUNK, row, 0, unroll=2)

    return pl.pallas_call(
        body, grid=(1,), in_specs=[pl.BlockSpec((N_HEADS, N_REL_PAD), lambda i: (0, 0))],
        out_specs=pl.BlockSpec((CHUNK, N_HEADS, band), lambda i: (0, 0, 0)),
        out_shape=SDS((CHUNK, N_HEADS, band), F32),
        compiler_params=_cparams(("arbitrary",)), name=name)(rel)


def _rel_reduce(dbias, name):
    band = (C_PREV + 1) * CHUNK
    NT = (((1,), (1,)), ((), ()))

    def body(d_ref, o_ref):
        def row(qi, st):
            acc, far = st
            oh = _rel_onehot(qi, band)
            hi, mid, lo = _split3(d_ref[qi, :, REL_FAR:band])
            acc = acc + (lax.dot_general(hi, oh, NT, preferred_element_type=F32)
                         + lax.dot_general(mid, oh, NT, preferred_element_type=F32)
                         + lax.dot_general(lo, oh, NT, preferred_element_type=F32))
            return acc, far + jnp.sum(d_ref[qi, :, 0:REL_FAR], axis=-1, keepdims=True)

        acc, far = lax.fori_loop(0, CHUNK, row, (jnp.zeros((N_HEADS, N_REL_PAD), F32), jnp.zeros((N_HEADS, 1), F32)),
                                 unroll=2)
        col = lax.broadcasted_iota(jnp.int32, (N_HEADS, N_REL_PAD), 1)
        o_ref[...] = acc + jnp.where(col == 2 * REL_CLIP, far, 0.0)

    return pl.pallas_call(
        body, grid=(1,), in_specs=[pl.BlockSpec((CHUNK, N_HEADS, band), lambda i: (0, 0, 0))],
        out_specs=pl.BlockSpec((N_HEADS, N_REL_PAD), lambda i: (0, 0)),
        out_shape=SDS((N_HEADS, N_REL_PAD), F32),
        compiler_params=_cparams(("arbitrary",)), name=name)(dbias)


def _alibi_table():
    qi = np.arange(CHUNK)[:, None]
    j = np.arange((A_PREV + 1) * CHUNK)[None, :]
    dist = np.abs(A_PREV * CHUNK + qi - j).astype(np.float32)
    slopes = np.exp2(-8.0 * np.arange(1, N_HEADS + 1, dtype=np.float32) / N_HEADS).astype(np.float32)
    return jnp.asarray(-slopes[:, None, None] * dist[None])


def _ada_fwd(c_all, w, b, name):
    n = w.shape[2]

    def body(c_ref, w_ref, b_ref, o_ref):
        cv = c_ref[...]
        cond = (cv * _sigmoid(cv)).astype(BF16)
        o_ref[0] = jnp.dot(cond, w_ref[0].astype(BF16), preferred_element_type=F32) + b_ref[0]

    return pl.pallas_call(
        body, grid=(DEPTH,),
        in_specs=[pl.BlockSpec((16, D_MODEL), lambda l: (0, 0)), pl.BlockSpec((1, D_MODEL, n), lambda l: (l, 0, 0)),
                  pl.BlockSpec((1, 1, n), lambda l: (l, 0, 0))],
        out_specs=pl.BlockSpec((1, 16, n), lambda l: (l, 0, 0)), out_shape=SDS((DEPTH, 16, n), F32),
        compiler_params=_cparams(("parallel",)), name=name)(c_all, w, b)


def _ada_bwd(c_t, dmod, name):
    n = dmod.shape[2]
    bn = _blk(n, 512)
    tr = 256

    def body(c_ref, d_ref, o_ref):
        cv = c_ref[...]
        cond = (cv * _sigmoid(cv)).astype(BF16).astype(F32)
        dm = d_ref[0].astype(BF16).astype(F32)
        acc = cond[:, 0:1] * dm[0:1, :]
        for b_ in range(1, 8):
            acc = acc + cond[:, b_:b_ + 1] * dm[b_:b_ + 1, :]
        o_ref[0] = acc

    return pl.pallas_call(
        body, grid=(DEPTH, D_MODEL // tr, n // bn),
        in_specs=[pl.BlockSpec((tr, 8), lambda l, i, j: (i, 0)), pl.BlockSpec((1, 8, bn), lambda l, i, j: (l, 0, j))],
        out_specs=pl.BlockSpec((1, tr, bn), lambda l, i, j: (l, i, j)), out_shape=SDS((DEPTH, D_MODEL, n), F32),
        compiler_params=_cparams(("parallel", "parallel", "parallel")), name=name)(c_t, dmod)


def _adamw(w, m, v, parts, name, comm=None):
    L, R, C = w.shape
    per_layer = isinstance(parts, (list, tuple))
    plist = list(parts) if per_layer else [parts]
    P = plist[0].shape[0]
    tr = _blk_rows(R, max(16, (1 << 18) // C))
    nr = R // tr
    c1 = 1.0 - ADAM_B1 ** ADAM_STEP
    c2 = 1.0 - ADAM_B2 ** ADAM_STEP

    def total(p_ref):
        g = p_ref[0].astype(F32)
        for k in range(1, P):
            g = g + p_ref[k].astype(F32)
        return g

    def body(w_ref, m_ref, v_ref, *rest):
        p_refs, (g_ref, d_ref, nm_ref, nv_ref) = rest[:len(plist)], rest[len(plist):]
        g = total(p_refs[0])
        for k in range(1, len(plist)):
            g = jnp.where(pl.program_id(0) == k, total(p_refs[k]), g)
        mn = ADAM_B1 * m_ref[0] + (1.0 - ADAM_B1) * g
        vn = ADAM_B2 * v_ref[0] + (1.0 - ADAM_B2) * (g * g)
        m_hat = mn / c1
        v_hat = vn / c2
        g_ref[0] = g
        nm_ref[0] = mn
        nv_ref[0] = vn
        d_ref[0] = -ADAM_LR * (m_hat / (jnp.sqrt(v_hat) + ADAM_EPS) + ADAM_WD * w_ref[0])

    rs = pl.BlockSpec((1, tr, C), lambda l, i: (l, i, 0))
    if per_layer:
        def layer_spec(k):
            return pl.BlockSpec((P, tr, C), lambda l, i: (0, jnp.where(l == k, i, 0), 0))
        pspecs = [layer_spec(k) for k in range(L)]
    else:
        pspecs = [pl.BlockSpec((P, tr, C), lambda l, i: (0, l * nr + i, 0))]
    return _call_hosting(
        body, comm=comm, grid=(L, nr), in_specs=[rs, rs, rs] + pspecs, out_specs=[rs, rs, rs, rs],
        out_shape=[SDS((L, R, C), F32)] * 4, scratch_shapes=[], name=name, args=(w, m, v, *plist))


def _blk_rows(R, cap):
    if R <= cap:
        return R
    best = None
    for t in range(16, cap + 1, 16):
        if R % t == 0:
            best = t
    assert best is not None, (R, cap)
    return best


def _add_cast_rows(g, t, name):
    Q, R, C = g.shape
    half = R // 2
    tr = _blk_rows(half, max(16, (1 << 19) // C))
    nb = half // tr

    def body(lo_ref, hi_ref, t_ref, o_ref):
        c = lax.axis_index("c")

        @pl.when(c == 0)
        def _():
            o_ref[...] = (lo_ref[...].astype(F32) + t_ref[...].astype(F32)).astype(o_ref.dtype)

        @pl.when(c == 1)
        def _():
            o_ref[...] = (hi_ref[...].astype(F32) + t_ref[...].astype(F32)).astype(o_ref.dtype)

    bs = pl.BlockSpec((1, tr, C), lambda q, i: (q, i, 0))
    hi = pl.BlockSpec((1, tr, C), lambda q, i: (q, nb + i, 0))
    return pl.pallas_call(
        body, grid=(Q, nb), in_specs=[bs, hi, bs], out_specs=bs, out_shape=SDS((Q, half, C), BF16),
        compiler_params=_cparams(("parallel", "parallel")), name=name)(g, g, t)


def _coords():
    return lax.axis_index("x"), lax.axis_index("y"), lax.axis_index("c")


def _flip(v, bit):
    return 1 - v if bit else v


def _all_gather8(v, name):
    R = v.shape[0]

    def body(v_ref, o_ref, send_sems, recv_sems):
        x, y, c = _coords()
        me = 4 * x + 2 * y + c
        o_ref[me] = v_ref[...]
        copies = []
        for k in range(1, 8):
            peer = (_flip(x, k & 4), _flip(y, k & 2), _flip(c, k & 1))
            cp = pltpu.make_async_remote_copy(
                src_ref=v_ref, dst_ref=o_ref.at[me], send_sem=send_sems.at[k - 1], recv_sem=recv_sems.at[k - 1],
                device_id=peer, device_id_type=MESH)
            cp.start()
            copies.append(cp)
        for cp in copies:
            cp.wait_recv()
        for cp in copies:
            cp.wait_send()

    return pl.pallas_call(
        body, in_specs=[VMEM_SPEC], out_specs=VMEM_SPEC, out_shape=SDS((8, R, LANE), v.dtype),
        scratch_shapes=[pltpu.SemaphoreType.DMA((7,)), pltpu.SemaphoreType.DMA((7,))],
        compiler_params=pltpu.CompilerParams(vmem_limit_bytes=VMEM_LIMIT), name=name)(v)


def _sibling_swap_rows(arrs, name):
    n = len(arrs)

    def body(*refs):
        in_refs, out_refs = refs[:n], refs[n:2 * n]
        send_sems, recv_sems = refs[2 * n:]
        x, y, c = _coords()
        copies = []
        for a in range(n):
            Q, R = in_refs[a].shape[0], in_refs[a].shape[1]
            half = R // 2
            src = in_refs[a].at[pl.ds(0, Q), pl.ds(pl.multiple_of((1 - c) * half, 16), half)]
            cp = pltpu.make_async_remote_copy(
                src_ref=src, dst_ref=out_refs[a], send_sem=send_sems.at[a], recv_sem=recv_sems.at[a],
                device_id=(x, y, 1 - c), device_id_type=MESH)
            cp.start()
            copies.append(cp)
        for cp in copies:
            cp.wait_recv()
        for cp in copies:
            cp.wait_send()

    return pl.pallas_call(
        body, in_specs=[ANY] * n, out_specs=[ANY] * n,
        out_shape=[SDS((a.shape[0], a.shape[1] // 2, a.shape[2]), a.dtype) for a in arrs],
        scratch_shapes=[pltpu.SemaphoreType.DMA((n,)), pltpu.SemaphoreType.DMA((n,))],
        name=name)(*arrs)


class _OwnerReduce:
    aliased = False

    def __init__(self, srcs, lay):
        self.srcs, self.lay, self.n = list(srcs), lay, len(srcs)
        self.out_shapes = [SDS(a.shape, a.dtype) for a in self.srcs]
        self.sem_shapes = [pltpu.SemaphoreType.DMA((self.n, 3)), pltpu.SemaphoreType.DMA((self.n, 3)),
                           pltpu.SemaphoreType.DMA((self.n,))]

    def _copies(self, src_refs, dst_refs, sems):
        ici_send, ici_recv, loc_sem = sems
        x, y, c = _coords()
        p = 2 * x + y
        local, remote = [], []
        for a in range(self.n):
            local.append(pltpu.make_async_copy(src_refs[a].at[p], dst_refs[a].at[p], loc_sem.at[a]))
            for k in range(1, 4):
                qx, qy = _flip(x, k & 2), _flip(y, k & 1)
                remote.append(pltpu.make_async_remote_copy(
                    src_ref=src_refs[a].at[2 * qx + qy], dst_ref=dst_refs[a].at[p], send_sem=ici_send.at[a, k - 1],
                    recv_sem=ici_recv.at[a, k - 1], device_id=(qx, qy, self.lay), device_id_type=MESH))
        return c, local, remote

    def start(self, src_refs, dst_refs, sems):
        c, local, remote = self._copies(src_refs, dst_refs, sems)

        @pl.when(c == self.lay)
        def _():
            for cp in local + remote:
                cp.start()

    def finish(self, src_refs, dst_refs, sems):
        c, local, remote = self._copies(src_refs, dst_refs, sems)

        @pl.when(c == self.lay)
        def _():
            for cp in remote:
                cp.wait_recv()
            for cp in remote:
                cp.wait_send()
            for cp in local:
                cp.wait()


def _call_hosting(body, *, comm, grid, in_specs, out_specs, out_shape, scratch_shapes, name, args, semantics=None):
    n_in, n_out, n_scr = len(args), len(out_shape), len(scratch_shapes)
    if comm is None:
        sem = semantics if semantics is not None else ("parallel",) * len(grid)
        res = pl.pallas_call(body, grid=grid, in_specs=in_specs, out_specs=out_specs, out_shape=out_shape,
                             scratch_shapes=scratch_shapes, compiler_params=_cparams(sem), name=name)(*args)
        return list(res), None
    k = comm.n

    def hosted(*refs):
        ins, cin = refs[:n_in], refs[n_in:n_in + k]
        outs = refs[n_in + k:n_in + k + n_out]
        cout = refs[n_in + k + n_out:n_in + 2 * k + n_out]
        scr = refs[n_in + 2 * k + n_out:n_in + 2 * k + n_out + n_scr]
        sems = refs[n_in + 2 * k + n_out + n_scr:]
        first = pl.program_id(0) == 0
        last = pl.program_id(0) == grid[0] - 1
        for d in range(1, len(grid)):
            first = jnp.logical_and(first, pl.program_id(d) == 0)
            last = jnp.logical_and(last, pl.program_id(d) == grid[d] - 1)

        @pl.when(first)
        def _():
            comm.start(cin, cout, sems)

        body(*ins, *outs, *scr)

        @pl.when(last)
        def _():
            comm.finish(cin, cout, sems)

    aliases = {n_in + j: n_out + j for j in range(k)} if comm.aliased else {}
    res = pl.pallas_call(
        hosted, grid=grid, in_specs=list(in_specs) + [ANY] * k, out_specs=list(out_specs) + [ANY] * k,
        out_shape=list(out_shape) + comm.out_shapes, scratch_shapes=list(scratch_shapes) + comm.sem_shapes,
        input_output_aliases=aliases, compiler_params=_cparams(("arbitrary",) * len(grid)),
        name=name)(*args, *comm.srcs)
    return list(res[:n_out]), list(res[n_out:])


class _RowHalfGather:
    aliased = False

    def __init__(self, srcs):
        self.srcs, self.n = list(srcs), len(srcs)
        self.out_shapes = [SDS((4,) + a.shape, a.dtype) for a in self.srcs]
        n = self.n
        self.sem_shapes = [pltpu.SemaphoreType.DMA((n, 3)), pltpu.SemaphoreType.DMA((n, 3)),
                           pltpu.SemaphoreType.DMA((n, 3)), pltpu.SemaphoreType.DMA((n, 3)),
                           pltpu.SemaphoreType.DMA((n,))]

    def _copies(self, src_refs, dst_refs, sems):
        ici_send, ici_recv, d2d_send, d2d_recv, loc_sem = sems
        x, y, c = _coords()
        p = 2 * x + y
        local, first, fwd = [], [], []
        for a in range(self.n):
            R = src_refs[a].shape[0] // 2
            half = pl.ds(pl.multiple_of(c * R, 16), R)
            local.append(pltpu.make_async_copy(src_refs[a], dst_refs[a].at[p], loc_sem.at[a]))
            for k in range(1, 4):
                qx, qy = _flip(x, k & 2), _flip(y, k & 1)
                first.append(pltpu.make_async_remote_copy(
                    src_ref=src_refs[a].at[half], dst_ref=dst_refs[a].at[p, half], send_sem=ici_send.at[a, k - 1],
                    recv_sem=ici_recv.at[a, k - 1], device_id=(qx, qy, c), device_id_type=MESH))
                slot = dst_refs[a].at[2 * qx + qy, half]
                fwd.append(pltpu.make_async_remote_copy(
                    src_ref=slot, dst_ref=slot, send_sem=d2d_send.at[a, k - 1], recv_sem=d2d_recv.at[a, k - 1],
                    device_id=(x, y, 1 - c), device_id_type=MESH))
        return local, first, fwd

    def start(self, src_refs, dst_refs, sems):
        local, first, _ = self._copies(src_refs, dst_refs, sems)
        for cp in local + first:
            cp.start()

    def finish(self, src_refs, dst_refs, sems):
        local, first, fwd = self._copies(src_refs, dst_refs, sems)
        for got, on in zip(first, fwd):
            got.wait_recv()
            on.start()
        for cp in fwd:
            cp.wait_recv()
        for cp in first + fwd:
            cp.wait_send()
        for cp in local:
            cp.wait()

    def run(self, name):
        return _run_exchange(self, name)


def _run_exchange(comm, name):
    n = comm.n

    def body(*refs):
        src_refs, dst_refs, sems = refs[:n], refs[n:2 * n], refs[2 * n:]
        comm.start(src_refs, dst_refs, sems)
        comm.finish(src_refs, dst_refs, sems)

    return pl.pallas_call(body, in_specs=[ANY] * n, out_specs=[ANY] * n, out_shape=comm.out_shapes,
                          scratch_shapes=comm.sem_shapes, name=name)(*comm.srcs)


class _RowHalfReduce:
    aliased = False

    def __init__(self, srcs):
        self.srcs, self.n = list(srcs), len(srcs)
        self.out_shapes = [SDS((4, 2 * a.shape[1], a.shape[2]), a.dtype) for a in self.srcs]
        n = self.n
        self.sem_shapes = [pltpu.SemaphoreType.DMA((n, 3)), pltpu.SemaphoreType.DMA((n, 3)),
                           pltpu.SemaphoreType.DMA((n, 4)), pltpu.SemaphoreType.DMA((n, 4)),
                           pltpu.SemaphoreType.DMA((n,))]

    def _copies(self, src_refs, dst_refs, sems):
        ici_send, ici_recv, d2d_send, d2d_recv, loc_sem = sems
        x, y, c = _coords()
        p = 2 * x + y
        local, first, fwd = [], [], []
        for a in range(self.n):
            R = src_refs[a].shape[1]
            half = pl.ds(pl.multiple_of(c * R, 16), R)
            local.append(pltpu.make_async_copy(src_refs[a].at[p], dst_refs[a].at[p, half], loc_sem.at[a]))
            for k in range(4):
                qx, qy = _flip(x, k & 2), _flip(y, k & 1)
                if k:
                    first.append(pltpu.make_async_remote_copy(
                        src_ref=src_refs[a].at[2 * qx + qy], dst_ref=dst_refs[a].at[p, half],
                        send_sem=ici_send.at[a, k - 1], recv_sem=ici_recv.at[a, k - 1], device_id=(qx, qy, c),
                        device_id_type=MESH))
                slot = dst_refs[a].at[2 * qx + qy, half]
                fwd.append(pltpu.make_async_remote_copy(
                    src_ref=slot, dst_ref=slot, send_sem=d2d_send.at[a, k], recv_sem=d2d_recv.at[a, k],
                    device_id=(x, y, 1 - c), device_id_type=MESH))
        return local, first, fwd

    def start(self, src_refs, dst_refs, sems):
        local, first, _ = self._copies(src_refs, dst_refs, sems)
        for cp in local + first:
            cp.start()

    def finish(self, src_refs, dst_refs, sems):
        local, first, fwd = self._copies(src_refs, dst_refs, sems)
        for a in range(self.n):
            local[a].wait()
            fwd[4 * a].start()
            for k in range(1, 4):
                first[3 * a + k - 1].wait_recv()
                fwd[4 * a + k].start()
        for cp in fwd:
            cp.wait_recv()
        for cp in first + fwd:
            cp.wait_send()

    def run(self, name):
        return _run_exchange(self, name)


class _SiblingSend:
    aliased = False

    def __init__(self, srcs, src_core):
        self.srcs, self.src_core, self.n = list(srcs), src_core, len(srcs)
        self.out_shapes = [SDS(a.shape, a.dtype) for a in self.srcs]
        self.sem_shapes = [pltpu.SemaphoreType.DMA((self.n,)), pltpu.SemaphoreType.DMA((self.n,))]

    def _copies(self, src_refs, dst_refs, sems):
        x, y, c = _coords()
        return c, [pltpu.make_async_remote_copy(
            src_ref=src_refs[a], dst_ref=dst_refs[a], send_sem=sems[0].at[a], recv_sem=sems[1].at[a],
            device_id=(x, y, 1 - c), device_id_type=MESH) for a in range(self.n)]

    def start(self, src_refs, dst_refs, sems):
        c, copies = self._copies(src_refs, dst_refs, sems)

        @pl.when(c == self.src_core)
        def _():
            for cp in copies:
                cp.start()

    def finish(self, src_refs, dst_refs, sems):
        c, copies = self._copies(src_refs, dst_refs, sems)

        @pl.when(c == self.src_core)
        def _():
            for cp in copies:
                cp.wait_send()

        @pl.when(c != self.src_core)
        def _():
            for cp in copies:
                cp.wait_recv()


class _Handoff:
    aliased = True

    def __init__(self, srcs, lay, slots):
        self.srcs, self.lay, self.slots, self.n = list(srcs), lay, tuple(slots), len(srcs)
        self.out_shapes = [SDS(a.shape, a.dtype) for a in self.srcs]
        ns = len(self.slots)
        self.sem_shapes = [pltpu.SemaphoreType.DMA((self.n, ns)), pltpu.SemaphoreType.DMA((self.n, ns))]

    def _copies(self, dst_refs, sems):
        x, y, c = _coords()
        copies = []
        for a in range(self.n):
            for j, k in enumerate(self.slots):
                slot = dst_refs[a].at[2 * _flip(x, k & 2) + _flip(y, k & 1)]
                copies.append(pltpu.make_async_remote_copy(
                    src_ref=slot, dst_ref=slot, send_sem=sems[0].at[a, j], recv_sem=sems[1].at[a, j],
                    device_id=(x, y, 1 - c), device_id_type=MESH))
        return c, copies

    def start(self, src_refs, dst_refs, sems):
        c, copies = self._copies(dst_refs, sems)

        @pl.when(c == self.lay)
        def _():
            for cp in copies:
                cp.start()

    def finish(self, src_refs, dst_refs, sems):
        c, copies = self._copies(dst_refs, sems)

        @pl.when(c == self.lay)
        def _():
            for cp in copies:
                cp.wait_send()

        @pl.when(c != self.lay)
        def _():
            for cp in copies:
                cp.wait_recv()


def _add_cast_on(a, b, lay, name):
    Q, R, C = b.shape
    tr = _blk_rows(R, max(16, (1 << 19) // C))

    def body(a_ref, b_ref, o_ref):
        @pl.when(lax.axis_index("c") == lay)
        def _():
            o_ref[...] = (a_ref[...].astype(F32) + b_ref[...].astype(F32)).astype(o_ref.dtype)

    bs = pl.BlockSpec((1, tr, C), lambda q, i: (q, i, 0))
    return pl.pallas_call(
        body, grid=(Q, R // tr), in_specs=[bs, bs], out_specs=bs, out_shape=SDS((Q, R, C), BF16),
        compiler_params=_cparams(("parallel", "parallel")), name=name)(a, b)


_IN_SIZES = (512, 128, 128, 512, 512, 512, 8, 512, 512, 512, 3072)
_IN_OFF = tuple(int(v) for v in np.cumsum((0,) + _IN_SIZES))
_IN_Q = N_IN_COLS // 4


def _pack_w_in(w):
    def cols(lo, hi):
        out = []
        while lo < hi:
            q, off = divmod(lo, _IN_Q)
            n = min(hi - lo, _IN_Q - off)
            out.append(w[q, :, off:off + n])
            lo += n
        return out

    fb0, fb1, g0 = _IN_OFF[6], _IN_OFF[7], _IN_OFF[10]
    wqkv = jnp.concatenate(cols(0, fb0) + cols(fb1, g0), axis=1)
    wgf = jnp.concatenate(cols(g0, N_IN_COLS) + cols(fb0, fb1) + [jnp.zeros((w.shape[1], LANE - 8), w.dtype)], axis=1)
    return wqkv, wgf


def _unpack_w_in(dqkv, dgf):
    fb0, fb1, g0 = _IN_OFF[6], _IN_OFF[7], _IN_OFF[10]

    def cols(lo, hi):
        out = []
        while lo < hi:
            if lo < fb0:
                n = min(hi, fb0) - lo
                out.append(dqkv[:, lo:lo + n])
            elif lo < fb1:
                n = min(hi, fb1) - lo
                out.append(dgf[:, 3072 + lo - fb0:3072 + lo - fb0 + n])
            elif lo < g0:
                n = min(hi, g0) - lo
                out.append(dqkv[:, lo - 8:lo - 8 + n])
            else:
                n = hi - lo
                out.append(dgf[:, lo - g0:lo - g0 + n])
            lo += n
        return out

    return jnp.stack([jnp.concatenate(cols(q * _IN_Q, (q + 1) * _IN_Q), axis=1) for q in range(4)])


def _pad_rows(a, rows):
    return jnp.pad(a, ((0, rows - a.shape[0]), (0, 0)))


def _small_pack(parts):
    flat = jnp.concatenate([p.reshape(-1) for p in parts])
    n = flat.shape[0]
    rows = -(-n // LANE)
    rows = -(-rows // 8) * 8
    return jnp.pad(flat, (0, rows * LANE - n)).reshape(rows, LANE)


def _small_unpack(block, shapes):
    flat = block.reshape(-1)
    out, off = [], 0
    for s in shapes:
        n = int(np.prod(s))
        out.append(flat[off:off + n].reshape(s))
        off += n
    return out


def _kv_same(g):
    return 0


def _kv_own(g):
    return g


_mm_plain = _mm


def _mm_hosting(a, b, *, comm, **kw):
    if comm is None:
        return _mm(a, b, **kw), None
    return _mm(a, b, comm=comm, **kw)


def _layer_fwd(x, mod, p, l, ride):
    sh_m, sc_m, g_m, sh_f, sc_f, g_f = mod
    nm = "l%d_" % l

    def carried(name, run):
        res, got = run(ride.comm_for(name))
        if got is not None:
            ride.done(name, got)
        return res

    h1 = _norm_mod_fwd(x, p["norm_mix_g"], sc_m, sh_m, nm + "norm_mix_fwd")
    qkv = carried("proj_qkv", lambda cm: _mm_hosting(h1, p["wqkv"], mode="nn", out_dtype=BF16,
                                                     name=nm + "proj_qkv", comm=cm))
    gf = _mm(h1, p["wgf"], mode="nn", out_dtype=F32, name=nm + "proj_gf", cap_n=640)
    qkv_t = qkv.T
    o_a_t = carried("attn_a", lambda cm: _bandT_fwd(
        (qkv_t, 0), _heads(qkv[:, 512:640], A_KV_HEADS), (qkv_t, 640), p["alibi"], p["sink_tab"],
        GQ=4, GK=1, P=A_PREV, kvoff=_kv_same, name=nm + "attn_a_fwd", comm=cm))
    cum = _fox_cum(gf, p["b_forget_pad"], nm + "fox_cum")
    cum_t = cum[:, :N_HEADS].T
    cc, cr = cum_t[:, :, None], cum_t[:, None, :]
    o_b_t, lse_b = carried("attn_b", lambda cm: _foxT_fwd(
        (qkv_t, 768), _heads(qkv[:, 1280:1792], N_HEADS), (qkv_t, 1792), cc, cr, nm + "attn_b_fwd", comm=cm))
    o_c_t = carried("attn_c", lambda cm: _bandT_fwd(
        (qkv_t, 2304), _heads(qkv[:, 2816:3328], N_HEADS), (qkv_t, 3328), p["rel_tab"], p["no_sink"],
        GQ=2, GK=2, P=C_PREV, kvoff=_kv_own, name=nm + "attn_c_fwd", comm=cm))
    p = dict(p, **ride.late_weights())
    o = jnp.concatenate([o_a_t, o_b_t, o_c_t], axis=0).T
    y = _mm(o, p["wb"], mode="nn", out_dtype=BF16, groups=3, name=nm + "branch")
    merged = _merge_fwd(y, gf, nm + "merge_fwd")
    mix = _mm(merged, p["wout"], mode="nn", out_dtype=F32, name=nm + "out_proj")
    x1 = _resid_fwd(x, mix, g_m, nm + "resid_mix")
    h2 = _norm_mod_fwd(x1, p["norm_ffn_g"], sc_f, sh_f, nm + "norm_ffn_fwd")
    u = carried("ffn_in", lambda cm: _mm_hosting(h2, p["wfi"], mode="nn", out_dtype=BF16, name=nm + "ffn_in",
                                                 cap_n=512, comm=cm))
    a = _swiglu_fwd(u, nm + "swiglu_fwd")
    f = _mm(a, p["wfo"], mode="nn", out_dtype=F32, name=nm + "ffn_out", cap_m=1024)
    x2 = _resid_fwd(x1, f, g_f, nm + "resid_ffn")
    saved = dict(x=x, h1=h1, qkv=qkv, qkv_t=qkv_t, gf=gf, cc=cc, cr=cr, o_b_t=o_b_t, lse_b=lse_b, o=o, y=y, merged=merged,
                 mix=mix, x1=x1, h2=h2, u=u, a=a, f=f)
    return x2, saved, p


def _layer_bwd(dx2, mod, p, s, l, ride=None):
    sh_m, sc_m, g_m, sh_f, sc_f, g_f = mod
    nm = "l%d_" % l

    def _mm(a, b, *, name, **kw):
        comm = ride.comm_for(name) if ride is not None else None
        if comm is None:
            return _mm_plain(a, b, name=nm + name, **kw)
        out, got = _mm_plain(a, b, name=nm + name, comm=comm, **kw)
        ride.done(name, got)
        return out

    dg_f, df = _resid_bwd(dx2, s["f"], g_f, nm + "resid_ffn_bwd")
    da = _mm(df, p["wfo"], mode="nt", out_dtype=BF16, name="ffn_out_dx", cap_m=1024, cap_n=1408)
    d_wfo = _mm(s["a"], df, mode="tn", out_dtype=BF16, name="ffn_out_dw", cap_m=1408, cap_k=2048)
    du = _swiglu_bwd(da, s["u"], nm + "swiglu_bwd")
    dh2 = _mm(du, p["wfi"], mode="nt", out_dtype=F32, name="ffn_in_dx", cap_m=1024)
    d_wfi = _mm(s["h2"], du, mode="tn", out_dtype=BF16, name="ffn_in_dw", cap_m=1024, cap_n=1408, cap_k=2048,
                col_quarters=True)
    dx1, dsc_f, dsh_f, dgn_f = _norm_mod_bwd(s["x1"], [dh2], dx2, p["norm_ffn_g"], sc_f, nm + "norm_ffn_bwd")
    dg_m, dmix = _resid_bwd(dx1, s["mix"], g_m, nm + "resid_mix_bwd")
    dmerged = _mm(dmix, p["wout"], mode="nt", out_dtype=F32, name="out_proj_dx")
    d_wout = _mm(s["merged"], dmix, mode="tn", out_dtype=BF16, name="out_proj_dw", cap_m=1024, cap_k=2048)
    dy, dgates = _merge_bwd(dmerged, s["y"], s["gf"], nm + "merge_bwd")
    do = _mm(dy, p["wb"], mode="nt", out_dtype=BF16, groups=3, name="branch_dx")
    d_wb = _mm(s["o"], dy, mode="tn", out_dtype=BF16, groups=3, name="branch_dw", cap_k=2048,
               col_quarters=True)
    comms = ride.exchanges() if ride is not None else (None, None, None)
    qkv, qkv_t = s["qkv"], s["qkv_t"]
    do_t = do.T
    (dqa_t, dka_h, dva_h, _, dsink), got_a = _bandT_bwd(
        (qkv_t, 0), _heads(qkv[:, 0:512], N_HEADS), _heads(qkv[:, 512:640], A_KV_HEADS), (qkv_t, 512),
        _heads(qkv[:, 640:768], A_KV_HEADS), (do_t, 0), _heads(do[:, 0:512], N_HEADS), p["alibi"], p["sink_tab"],
        GQ=4, GK=1, P=A_PREV, kvoff=_kv_same, name=nm + "attn_a_bwd", comm=comms[0])
    (dqb_t, dkb_h, dvb_h, dck, dcq), got_b = _foxT_bwd(
        (qkv_t, 768), _heads(qkv[:, 768:1280], N_HEADS), _heads(qkv[:, 1280:1792], N_HEADS), (qkv_t, 1280),
        _heads(qkv[:, 1792:2304], N_HEADS), s["cc"], s["cr"], s["o_b_t"], (do_t, 512),
        _heads(do[:, 512:1024], N_HEADS), s["lse_b"], nm + "attn_b_bwd", comm=comms[1])
    dcum = jnp.pad((dck[:, :, 0] + dcq[:, 0, :]).T, ((0, 0), (0, LANE - N_HEADS)))
    dfb, db_forget = _fox_cum_bwd(s["gf"], p["b_forget_pad"], dcum, nm + "fox_cum_bwd")
    (dqc_t, dkc_h, dvc_h, dbias_c, _), got_c = _bandT_bwd(
        (qkv_t, 2304), _heads(qkv[:, 2304:2816], N_HEADS), _heads(qkv[:, 2816:3328], N_HEADS), (qkv_t, 2816),
        _heads(qkv[:, 3328:3840], N_HEADS), (do_t, 1024), _heads(do[:, 1024:1536], N_HEADS), p["rel_tab"],
        p["no_sink"], GQ=2, GK=2, P=C_PREV, kvoff=_kv_own, name=nm + "attn_c_bwd", comm=comms[2])
    d_rel = _rel_reduce(jnp.transpose(_unpair_table(dbias_c), (1, 0, 2)), nm + "rel_reduce")[:, :N_REL]
    dqkv = jnp.concatenate([dqa_t.T, _unheads(dka_h), _unheads(dva_h), dqb_t.T, _unheads(dkb_h), _unheads(dvb_h),
                            dqc_t.T, _unheads(dkc_h), _unheads(dvc_h)], axis=1)
    dgf = jnp.concatenate([dgates, dfb], axis=1)
    if ride is not None:
        ride.exchanged((got_a, got_b, got_c))
    dh1a = _mm(dqkv, p["wqkv"], mode="nt", out_dtype=F32, name="proj_qkv_dx", cap_k=1024)
    dh1b = _mm(dgf, p["wgf"], mode="nt", out_dtype=F32, name="proj_gf_dx", cap_k=640)
    d_wqkv = _mm(s["h1"], dqkv, mode="tn", out_dtype=BF16, name="proj_qkv_dw", cap_m=1024, cap_k=2048)
    d_wgf = _mm(s["h1"], dgf, mode="tn", out_dtype=BF16, name="proj_gf_dw", cap_m=1024, cap_n=640, cap_k=2048)
    dx, dsc_m, dsh_m, dgn_m = _norm_mod_bwd(s["x"], [dh1a, dh1b], dx1, p["norm_mix_g"], sc_m, nm + "norm_mix_bwd")
    d_mod = jnp.concatenate([dsh_m, dsc_m, dg_m, dsh_f, dsc_f, dg_f], axis=1)[0]
    grads = dict(w_in=_unpack_w_in(d_wqkv, d_wgf), w_branch=d_wb, w_out=d_wout.reshape(4, -1, D_MODEL),
                 w_ffn_in=d_wfi, w_ffn_out=d_wfo.reshape(4, -1, D_MODEL),
                 norm_mix_g=dgn_m[0], norm_ffn_g=dgn_f[0], b_forget=db_forget[0, :N_HEADS],
                 sinks=dsink[:, 0, 0], rel_bias=d_rel, d_mod=d_mod)
    return dx, grads


def kernel(x, c, norm_mix_g, norm_ffn_g, w_ada, b_ada, w_in, b_forget, sinks, rel_bias, w_branch, w_out, w_ffn_in, w_ffn_out, final_norm_g, loss_target, m_norm_mix_g, m_norm_ffn_g, m_w_ada, m_b_ada, m_w_in, m_b_forget, m_sinks, m_rel_bias, m_w_branch, m_w_out, m_w_ffn_in, m_w_ffn_out, m_final_norm_g, v_norm_mix_g, v_norm_ffn_g, v_w_ada, v_b_ada, v_w_in, v_b_forget, v_sinks, v_rel_bias, v_w_branch, v_w_out, v_w_ffn_in, v_w_ffn_out, v_final_norm_g):
    xi, yi, ci = _coords()
    chip = 2 * xi + yi
    dev = 2 * chip + ci
    xs = x[0]
    S = xs.shape[0]
    n_ada = w_ada.shape[2]

    big_names = ("w_in", "w_branch", "w_out", "w_ffn_in", "w_ffn_out")
    big_w = dict(w_in=w_in, w_branch=w_branch, w_out=w_out, w_ffn_in=w_ffn_in, w_ffn_out=w_ffn_out)
    big_m = dict(w_in=m_w_in, w_branch=m_w_branch, w_out=m_w_out, w_ffn_in=m_w_ffn_in, w_ffn_out=m_w_ffn_out)
    big_v = dict(w_in=v_w_in, w_branch=v_w_branch, w_out=v_w_out, w_ffn_in=v_w_ffn_in, w_ffn_out=v_w_ffn_out)
    flat2 = lambda a: a.reshape(-1, a.shape[-1])
    shards = [[flat2(big_w[n][l]).astype(BF16) for n in big_names] for l in range(DEPTH)]
    gw = [[None] * (len(big_names) + 2) for _ in range(DEPTH)]
    for l in range(DEPTH):
        shards[l] += [shards[l][0][:D_MODEL // 2], shards[l][0][D_MODEL // 2:]]
    gw[0][0] = _RowHalfGather([shards[0][0]]).run("weights_gather_w_in_l0")[0]
    host_g = ((1, 2, 4), (0,), (3,))

    class WeightRide:
        def __init__(self, l, plan):
            self.l, self.plan = l, plan

        def comm_for(self, name):
            if name not in self.plan:
                return None
            lay, idx = self.plan[name]
            return _RowHalfGather([shards[lay][i] for i in idx])

        def done(self, name, got):
            lay, idx = self.plan[name]
            for i, r in zip(idx, got):
                gw[lay][i] = r

        def late_weights(self):
            g = gw[self.l]
            return dict(wb=jnp.transpose(g[1], (1, 0, 2)).reshape(3 * BRANCH_W, D_MODEL),
                        wout=g[2].reshape(D_MODEL, D_MODEL),
                        wfi=jnp.transpose(g[3], (1, 0, 2)).reshape(D_MODEL, 2 * FFN_H),
                        wfo=g[4].reshape(FFN_H, D_MODEL))

    weight_plan = [
        {"proj_qkv": (0, (1, 2)), "attn_a": (0, (4,)), "attn_b": (0, (3,)), "attn_c": (1, (5,)), "ffn_in": (1, (6,))},
        {"attn_a": (1, (1, 2)), "attn_b": (1, (3,)), "attn_c": (1, (4,))}]


    c_all = _all_gather8(c.reshape(8, LANE), "gather_c").reshape(8, D_MODEL)
    b_sh = lax.dynamic_slice_in_dim(b_ada, chip * n_ada, n_ada, axis=1)[:, None, :]
    mod_sh = _ada_fwd(_pad_rows(c_all, 16), w_ada, b_sh, "ada_fwd")[:, :8, :]
    mod_all = _all_gather8(mod_sh.reshape(-1, LANE), "gather_mod").reshape(8, DEPTH, 8, n_ada)
    mod_mine = lax.dynamic_index_in_dim(mod_all[0::2], dev, axis=2, keepdims=False)
    mod = mod_mine.transpose(1, 0, 2).reshape(DEPTH, 6, D_MODEL)

    alibi = _pair_table(_alibi_table())
    no_sink = jnp.full((N_HEADS, 8, LANE), NEG_INF, F32)
    def make_params(l):
        if gw[l][0] is None:
            gw[l][0] = jnp.concatenate([gw[l][5], gw[l][6]], axis=1)
        wqkv, wgf = _pack_w_in(gw[l][0])
        rel_tab = _rel_expand(jnp.pad(rel_bias[l], ((0, 0), (0, N_REL_PAD - N_REL))), "l%d_rel_expand" % l)
        return dict(
            wqkv=wqkv, wgf=wgf, norm_mix_g=norm_mix_g[l][None], norm_ffn_g=norm_ffn_g[l][None],
            b_forget_pad=jnp.pad(b_forget[l], (0, LANE - N_HEADS))[None],
            sink_tab=jnp.broadcast_to(sinks[l][:, None, None], (N_HEADS, 8, LANE)),
            no_sink=no_sink, alibi=alibi, rel_tab=_pair_table(jnp.transpose(rel_tab, (1, 0, 2))))

    mods = [[mod[l, k][None] for k in range(6)] for l in range(DEPTH)]
    params, saved = [None] * DEPTH, [None] * DEPTH
    h = xs
    for l in range(DEPTH):
        h, saved[l], params[l] = _layer_fwd(h, mods[l], make_params(l), l, WeightRide(l, weight_plan[l]))
    loss_dev, dh, d_final = _final_loss(h, final_norm_g[None], loss_target[0], "final_loss")
    grads = [None] * DEPTH
    dh, grads[1] = _layer_bwd(dh, mods[1], params[1], saved[1], 1)

    class Layer1Ride:
        sends = {"ffn_out_dx": (4,), "ffn_out_dw": (1, 2), "ffn_in_dx": (3,), "ffn_in_dw": (0,)}
        hands = {"proj_qkv_dx": (0,), "proj_gf_dx": (3,), "proj_qkv_dw": (4,), "proj_gf_dw": (1, 2)}

        def __init__(self, g):
            self.g, self.t = g, [None] * len(g)
            self.parts, self.final = [None] * len(g), [None] * len(g)

        def comm_for(self, name):
            if name in self.sends:
                return _SiblingSend([self.g[i] for i in self.sends[name]], 0)
            if name in self.hands:
                return _Handoff([self.parts[i] for i in self.hands[name]], 1, (0, 1, 2, 3))
            return None

        def done(self, name, got):
            idx, dst = (self.sends[name], self.t) if name in self.sends else (self.hands[name], self.final)
            for i, r in zip(idx, got):
                dst[i] = r

        def exchanges(self):
            sums = [_add_cast_on(a, b, 1, "grads_chip_sum_l1_" + n) for n, a, b in zip(big_names, self.g, self.t)]
            return tuple(_OwnerReduce([sums[i] for i in idx], 1) for idx in host_g)

        def exchanged(self, got):
            for res, idx in zip(got, host_g):
                for r, i in zip(res, idx):
                    self.parts[i] = r

    ride = Layer1Ride([grads[1][n] for n in big_names])
    dh, grads[0] = _layer_bwd(dh, mods[0], params[0], saved[0], 0, ride)
    grad_x = dh[None]
    loss = lax.psum(loss_dev[0, 0], ("x", "y", "c"))
    parts1 = ride.final
    g0 = [grads[0][n] for n in big_names]
    t0 = _sibling_swap_rows(g0, "grads_swap_l0")
    sums0 = [_add_cast_rows(a, b, "grads_chip_sum_l0_" + n) for n, a, b in zip(big_names, g0, t0)]
    parts0 = [None] + list(_RowHalfReduce(sums0[1:]).run("grads_reduce_l0"))

    small_names = ("norm_mix_g", "norm_ffn_g", "b_ada", "b_forget", "sinks", "rel_bias", "final_norm_g")
    small_w = dict(norm_mix_g=norm_mix_g, norm_ffn_g=norm_ffn_g, b_ada=b_ada, b_forget=b_forget, sinks=sinks,
                   rel_bias=rel_bias, final_norm_g=final_norm_g)
    small_m = dict(norm_mix_g=m_norm_mix_g, norm_ffn_g=m_norm_ffn_g, b_ada=m_b_ada, b_forget=m_b_forget,
                   sinks=m_sinks, rel_bias=m_rel_bias, final_norm_g=m_final_norm_g)
    small_v = dict(norm_mix_g=v_norm_mix_g, norm_ffn_g=v_norm_ffn_g, b_ada=v_b_ada, b_forget=v_b_forget,
                   sinks=v_sinks, rel_bias=v_rel_bias, final_norm_g=v_final_norm_g)
    small_g = dict(
        norm_mix_g=jnp.stack([grads[l]["norm_mix_g"] for l in range(DEPTH)]),
        norm_ffn_g=jnp.stack([grads[l]["norm_ffn_g"] for l in range(DEPTH)]),
        b_ada=jnp.stack([grads[l]["d_mod"] for l in range(DEPTH)]),
        b_forget=jnp.stack([grads[l]["b_forget"] for l in range(DEPTH)]),
        sinks=jnp.stack([grads[l]["sinks"] for l in range(DEPTH)]),
        rel_bias=jnp.stack([grads[l]["rel_bias"] for l in range(DEPTH)]),
        final_norm_g=d_final[0])
    shapes = [small_w[n].shape for n in small_names]
    g_all = _all_gather8(_small_pack([small_g[n] for n in small_names]), "gather_small_grads")
    res, _ = _adamw(_small_pack([small_w[n] for n in small_names])[None],
                    _small_pack([small_m[n] for n in small_names])[None],
                    _small_pack([small_v[n] for n in small_names])[None], g_all, "adamw_small")
    small_out = {n: [] for n in small_names}
    for r in res:
        for n, a in zip(small_names, _small_unpack(r[0], shapes)):
            small_out[n].append(a)
    off_b = sum(int(np.prod(s)) for s in shapes[:2])
    n_mod = DEPTH * 6 * D_MODEL
    dmod_all = g_all.reshape(8, -1)[:, off_b:off_b + n_mod].reshape(8, DEPTH, 6 * D_MODEL)
    dmod_sh = lax.dynamic_slice_in_dim(dmod_all, chip * n_ada, n_ada, axis=2).transpose(1, 0, 2)
    g_ada = _ada_bwd(c_all.T, dmod_sh, "ada_bwd")
    ada_out, got = _adamw(w_ada, m_w_ada, v_w_ada, flat2(g_ada)[None], "adamw_w_ada",
                          comm=_RowHalfReduce(sums0[:1]))
    parts0[0] = got[0]

    big_out = {}
    as3 = lambda a: a.reshape(a.shape[0], -1, a.shape[-1])
    for n, p0, p1 in zip(big_names, parts0, parts1):
        res, _ = _adamw(as3(big_w[n]), as3(big_m[n]), as3(big_v[n]), [p0, p1], "adamw_" + n)
        big_out[n] = [r.reshape(big_w[n].shape) for r in res]

    order = ("norm_mix_g", "norm_ffn_g", "w_ada", "b_ada", "w_in", "b_forget", "sinks", "rel_bias", "w_branch",
             "w_out", "w_ffn_in", "w_ffn_out", "final_norm_g")

    def pick(n, k):
        if n == "w_ada":
            return ada_out[k]
        if n in big_out:
            return big_out[n][k]
        return small_out[n][k]

    outs = [loss, grad_x]
    for k in range(4):
        outs += [pick(n, k) for n in order]
    return tuple(outs)
```

```python
import numpy as np
import jax
import jax.numpy as jnp
from jax import lax
from jax.experimental import pallas as pl
from jax.experimental.pallas import tpu as pltpu

F32 = jnp.float32
BF16 = jnp.bfloat16
SDS = jax.ShapeDtypeStruct

D_MODEL = 1024
DEPTH = 2
CHUNK = 64
HEAD_DIM = 64
EPS = 1e-6
NEG_INF = -1e30
N_HEADS = 8
A_KV_HEADS = 2
A_PREV = 2
C_PREV = 8
REL_CLIP = 128
N_REL = 2 * REL_CLIP + 1
N_REL_PAD = 384
BRANCH_W = 512
FFN_H = 2816
FOX_BQ = 256
FOX_BK = 512
GF_COLS = 3200
N_IN_COLS = 6920
LANE = 128
VMEM_LIMIT = 48 * 1024 * 1024

ADAM_LR = 0.001
ADAM_B1 = 0.9
ADAM_B2 = 0.999
ADAM_EPS = 1e-08
ADAM_WD = 0.01
ADAM_STEP = 10

MESH = pl.DeviceIdType.MESH
ANY = pl.BlockSpec(memory_space=pl.ANY)
VMEM_SPEC = pl.BlockSpec(memory_space=pltpu.VMEM)


def _cparams(sem=None):
    return pltpu.CompilerParams(dimension_semantics=sem, vmem_limit_bytes=VMEM_LIMIT)


def _blk(n, cap):
    if n <= cap:
        return n
    best = None
    for m in range(LANE, cap + 1, LANE):
        if n % m == 0:
            best = m
    assert best is not None, (n, cap)
    return best


def _sigmoid(x):
    return 1.0 / (1.0 + jnp.exp(-x))


def _mm(a, b, *, mode, out_dtype, name, groups=1, cap_m=2048, cap_n=1024, cap_k=1408, col_quarters=False,
        comm=None):
    G = groups
    assert not col_quarters or mode == "tn"
    if mode == "nn":
        M, K, N = a.shape[0], a.shape[1] // G, b.shape[1]
        assert b.shape[0] == G * K
    elif mode == "nt":
        M, K, N = a.shape[0], a.shape[1] // G, b.shape[0] // G
        assert b.shape[1] == K
    else:
        K, M, N = a.shape[0], a.shape[1] // G, b.shape[1] // G
        assert b.shape[0] == K
    bm, bn, bk = _blk(M, cap_m), _blk(N // 4 if col_quarters else N, cap_n), _blk(K, cap_k)
    nm, nn, nk = M // bm, N // bn, K // bk
    if mode == "nn":
        a_spec = pl.BlockSpec((bm, bk), lambda g, i, j, k: (i, g * nk + k))
        b_spec = pl.BlockSpec((bk, bn), lambda g, i, j, k: (g * nk + k, j))
        o_spec = pl.BlockSpec((bm, bn), lambda g, i, j, k: (i, g * nn + j))
        dims = (((1,), (0,)), ((), ()))
        out_shape = (M, G * N)
    elif mode == "nt":
        a_spec = pl.BlockSpec((bm, bk), lambda g, i, j, k: (i, g * nk + k))
        b_spec = pl.BlockSpec((bn, bk), lambda g, i, j, k: (g * nn + j, k))
        o_spec = pl.BlockSpec((bm, bn), lambda g, i, j, k: (i, g * nn + j))
        dims = (((1,), (1,)), ((), ()))
        out_shape = (M, G * N)
    else:
        a_spec = pl.BlockSpec((bk, bm), lambda g, i, j, k: (k, g * nm + i))
        b_spec = pl.BlockSpec((bk, bn), lambda g, i, j, k: (k, g * nn + j))
        dims = (((0,), (0,)), ((), ()))
        if col_quarters:
            nq = nn // 4
            o_spec = pl.BlockSpec((1, bm, bn), lambda g, i, j, k: (j // nq, g * nm + i, j % nq))
            out_shape = (4, G * M, N // 4)
        else:
            o_spec = pl.BlockSpec((bm, bn), lambda g, i, j, k: (g * nm + i, j))
            out_shape = (G * M, N)

    def product(a_ref, b_ref):
        return lax.dot_general(a_ref[...].astype(BF16), b_ref[...].astype(BF16), dims, preferred_element_type=F32)

    def body_one(a_ref, b_ref, o_ref):
        o_ref[...] = product(a_ref, b_ref).astype(o_ref.dtype).reshape(o_ref.shape)

    def body_acc(a_ref, b_ref, o_ref, acc_ref):
        k = pl.program_id(3)

        @pl.when(k == 0)
        def _():
            acc_ref[...] = jnp.zeros_like(acc_ref)

        acc_ref[...] += product(a_ref, b_ref)

        @pl.when(k == nk - 1)
        def _():
            o_ref[...] = acc_ref[...].astype(o_ref.dtype).reshape(o_ref.shape)

    res, got = _call_hosting(
        body_one if nk == 1 else body_acc, comm=comm, grid=(G, nm, nn, nk), in_specs=[a_spec, b_spec],
        out_specs=[o_spec], out_shape=[SDS(out_shape, out_dtype)],
        scratch_shapes=[] if nk == 1 else [pltpu.VMEM((bm, bn), F32)], name=name, args=(a, b),
        semantics=("parallel", "parallel", "parallel", "arbitrary"))
    return res[0] if comm is None else (res[0], got)


def _rows(tm, n, col=0):
    return pl.BlockSpec((tm, n), lambda i: (i, col))


def _vec(n):
    return pl.BlockSpec((1, n), lambda i: (0, 0))


def _tm(S):
    return min(S, 256)


def _norm_mod_fwd(x, g, sc, sh, name):
    S, Dm = x.shape
    tm = _tm(S)

    def body(x_ref, g_ref, sc_ref, sh_ref, h_ref):
        xv = x_ref[...]
        r = lax.rsqrt(jnp.mean(xv * xv, axis=-1, keepdims=True) + EPS)
        h_ref[...] = ((xv * r) * g_ref[...] * (1.0 + sc_ref[...]) + sh_ref[...]).astype(h_ref.dtype)

    return pl.pallas_call(
        body, grid=(S // tm,), in_specs=[_rows(tm, Dm), _vec(Dm), _vec(Dm), _vec(Dm)],
        out_specs=_rows(tm, Dm), out_shape=SDS((S, Dm), BF16),
        compiler_params=_cparams(("parallel",)), name=name)(x, g, sc, sh)


def _norm_mod_bwd(x, dh_list, dres, g, sc, name):
    S, Dm = x.shape
    tm = _tm(S)
    nh = len(dh_list)

    def body(*refs):
        x_ref = refs[0]
        dh_refs = refs[1:1 + nh]
        dres_ref, g_ref, sc_ref, dx_ref, dsc_ref, dsh_ref, dg_ref = refs[1 + nh:]
        i = pl.program_id(0)

        @pl.when(i == 0)
        def _():
            dsc_ref[...] = jnp.zeros_like(dsc_ref)
            dsh_ref[...] = jnp.zeros_like(dsh_ref)
            dg_ref[...] = jnp.zeros_like(dg_ref)

        xv = x_ref[...]
        dh = dh_refs[0][...]
        for r_ in dh_refs[1:]:
            dh = dh + r_[...]
        gv = g_ref[...]
        r = lax.rsqrt(jnp.mean(xv * xv, axis=-1, keepdims=True) + EPS)
        xn = xv * r
        xg = xn * gv
        dsh_ref[...] += jnp.sum(dh, axis=0, keepdims=True)
        dsc_ref[...] += jnp.sum(dh * xg, axis=0, keepdims=True)
        dxg = dh * (1.0 + sc_ref[...])
        dg_ref[...] += jnp.sum(dxg * xn, axis=0, keepdims=True)
        dxn = dxg * gv
        dx_ref[...] = dres_ref[...] + r * (dxn - xn * jnp.mean(dxn * xn, axis=-1, keepdims=True))

    return pl.pallas_call(
        body, grid=(S // tm,),
        in_specs=[_rows(tm, Dm)] * (2 + nh) + [_vec(Dm), _vec(Dm)],
        out_specs=[_rows(tm, Dm), _vec(Dm), _vec(Dm), _vec(Dm)],
        out_shape=[SDS((S, Dm), F32), SDS((1, Dm), F32), SDS((1, Dm), F32), SDS((1, Dm), F32)],
        compiler_params=_cparams(("arbitrary",)), name=name)(x, *dh_list, dres, g, sc)


def _resid_fwd(x, val, g, name):
    S, Dm = x.shape
    tm = _tm(S)

    def body(x_ref, v_ref, g_ref, o_ref):
        o_ref[...] = x_ref[...] + g_ref[...] * v_ref[...]

    return pl.pallas_call(
        body, grid=(S // tm,), in_specs=[_rows(tm, Dm), _rows(tm, Dm), _vec(Dm)],
        out_specs=_rows(tm, Dm), out_shape=SDS((S, Dm), F32),
        compiler_params=_cparams(("parallel",)), name=name)(x, val, g)


def _resid_bwd(dx, val, g, name):
    S, Dm = dx.shape
    tm = _tm(S)

    def body(dx_ref, v_ref, g_ref, dg_ref, dv_ref):
        @pl.when(pl.program_id(0) == 0)
        def _():
            dg_ref[...] = jnp.zeros_like(dg_ref)

        dxv = dx_ref[...]
        dg_ref[...] += jnp.sum(dxv * v_ref[...], axis=0, keepdims=True)
        dv_ref[...] = (dxv * g_ref[...]).astype(dv_ref.dtype)

    return pl.pallas_call(
        body, grid=(S // tm,), in_specs=[_rows(tm, Dm), _rows(tm, Dm), _vec(Dm)],
        out_specs=[_vec(Dm), _rows(tm, Dm)], out_shape=[SDS((1, Dm), F32), SDS((S, Dm), BF16)],
        compiler_params=_cparams(("arbitrary",)), name=name)(dx, val, g)


def _merge_fwd(y, gf, name):
    S = y.shape[0]
    tm = _tm(S)
    W = 3 * D_MODEL

    def body(y_ref, g_ref, o_ref):
        acc = None
        for k in range(3):
            sl = slice(k * D_MODEL, (k + 1) * D_MODEL)
            t = _sigmoid(g_ref[:, sl]) * y_ref[:, sl].astype(F32)
            acc = t if acc is None else acc + t
        o_ref[...] = acc.astype(o_ref.dtype)

    return pl.pallas_call(
        body, grid=(S // tm,), in_specs=[_rows(tm, W), _rows(tm, W)],
        out_specs=_rows(tm, D_MODEL), out_shape=SDS((S, D_MODEL), BF16),
        compiler_params=_cparams(("parallel",)), name=name)(y, gf)


def _merge_bwd(dm, y, gf, name):
    S = y.shape[0]
    tm = _tm(S)
    W = 3 * D_MODEL

    def body(dm_ref, y_ref, g_ref, dy_ref, dg_ref):
        dmv = dm_ref[...]
        for k in range(3):
            sl = slice(k * D_MODEL, (k + 1) * D_MODEL)
            sg = _sigmoid(g_ref[:, sl])
            dy_ref[:, sl] = (dmv * sg).astype(dy_ref.dtype)
            dg_ref[:, sl] = (dmv * y_ref[:, sl].astype(F32) * (sg * (1.0 - sg))).astype(dg_ref.dtype)

    return pl.pallas_call(
        body, grid=(S // tm,), in_specs=[_rows(tm, D_MODEL), _rows(tm, W), _rows(tm, W)],
        out_specs=[_rows(tm, W), _rows(tm, W)], out_shape=[SDS((S, W), BF16), SDS((S, W), BF16)],
        compiler_params=_cparams(("parallel",)), name=name)(dm, y, gf)


def _swiglu_fwd(u, name):
    S = u.shape[0]
    tm = _tm(S)

    def body(g_ref, u_ref, a_ref):
        gv = g_ref[...].astype(F32)
        a_ref[...] = (gv * _sigmoid(gv) * u_ref[...].astype(F32)).astype(a_ref.dtype)

    return pl.pallas_call(
        body, grid=(S // tm,), in_specs=[_rows(tm, FFN_H, 0), _rows(tm, FFN_H, 1)],
        out_specs=_rows(tm, FFN_H), out_shape=SDS((S, FFN_H), BF16),
        compiler_params=_cparams(("parallel",)), name=name)(u, u)


def _swiglu_bwd(da, u, name):
    S = u.shape[0]
    tm = _tm(S)

    def body(da_ref, g_ref, u_ref, du_ref):
        dav = da_ref[...].astype(F32)
        gv = g_ref[...].astype(F32)
        sg = _sigmoid(gv)
        du_ref[:, 0:FFN_H] = (dav * u_ref[...].astype(F32) * (sg * (1.0 + gv * (1.0 - sg)))).astype(du_ref.dtype)
        du_ref[:, FFN_H:2 * FFN_H] = (dav * (gv * sg)).astype(du_ref.dtype)

    return pl.pallas_call(
        body, grid=(S // tm,), in_specs=[_rows(tm, FFN_H), _rows(tm, FFN_H, 0), _rows(tm, FFN_H, 1)],
        out_specs=_rows(tm, 2 * FFN_H), out_shape=SDS((S, 2 * FFN_H), BF16),
        compiler_params=_cparams(("parallel",)), name=name)(da, u, u)


def _final_loss(x, g, target, name):
    S, Dm = x.shape
    tm = _tm(S)

    def body(x_ref, g_ref, t_ref, loss_ref, dx_ref, dg_ref):
        @pl.when(pl.program_id(0) == 0)
        def _():
            loss_ref[...] = jnp.zeros_like(loss_ref)
            dg_ref[...] = jnp.zeros_like(dg_ref)

        xv = x_ref[...]
        gv = g_ref[...]
        r = lax.rsqrt(jnp.mean(xv * xv, axis=-1, keepdims=True) + EPS)
        xn = xv * r
        err = xn * gv - t_ref[...]
        row = jnp.mean(err * err, axis=-1, keepdims=True)
        loss_ref[...] += 0.5 * jnp.sum(row, axis=0, keepdims=True)
        dy = err * (1.0 / Dm)
        dg_ref[...] += jnp.sum(dy * xn, axis=0, keepdims=True)
        dxn = dy * gv
        dx_ref[...] = r * (dxn - xn * jnp.mean(dxn * xn, axis=-1, keepdims=True))

    return pl.pallas_call(
        body, grid=(S // tm,), in_specs=[_rows(tm, Dm), _vec(Dm), _rows(tm, Dm)],
        out_specs=[pl.BlockSpec((1, 1), lambda i: (0, 0)), _rows(tm, Dm), _vec(Dm)],
        out_shape=[SDS((1, 1), F32), SDS((S, Dm), F32), SDS((1, Dm), F32)],
        compiler_params=_cparams(("arbitrary",)), name=name)(x, g, target)


PAIR = 2 * CHUNK


def _bandT_softmax(kg, qTg, bias, sink, valid):
    s = jnp.dot(kg, qTg, preferred_element_type=F32)
    s = jnp.where(valid, s + bias, NEG_INF)
    m = jnp.maximum(jnp.max(s, axis=0, keepdims=True), sink)
    e = jnp.exp(s - m)
    es = jnp.exp(sink - m)
    inv = 1.0 / (jnp.sum(e, axis=0, keepdims=True) + es)
    return e * inv, es * inv


def _pad_copy_rows(dst, src, pad, S):
    dst[:, 0:pad, :] = jnp.zeros((dst.shape[0], pad, dst.shape[2]), dst.dtype)
    dst[:, pad:pad + S, :] = src[...]


def _pad_copy_lanes(dst, src, pad, S):
    dst[:, 0:pad] = jnp.zeros((dst.shape[0], pad), dst.dtype)
    dst[:, pad:pad + S] = src[...]


def _fm(arg):
    return arg if isinstance(arg, tuple) else (arg, 0)


def _fm_spec(rows, S, row0):
    off, rem = divmod(row0, rows)
    assert rem == 0
    return pl.BlockSpec((rows, S), lambda i: (off + i, 0))


def _bandT_fwd(qT, k_h, vT, bias, sink, *, GQ, GK, P, kvoff, name, comm=None):
    (qT, q0), (vT, v0) = _fm(qT), _fm(vT)
    S = qT.shape[1]
    ng = bias.shape[0] // GQ
    BU = (P + 2) * CHUNK
    pad = P * CHUNK
    npair = S // PAIR

    def body(qT_ref, k_ref, vT_ref, b_ref, s_ref, oT_ref, kp, vTp):
        _pad_copy_rows(kp, k_ref, pad, S)
        _pad_copy_lanes(vTp, vT_ref, pad, S)
        rowi = lax.broadcasted_iota(jnp.int32, (BU, PAIR), 0)

        def step(n2, carry):
            r = pl.multiple_of(n2 * PAIR, PAIR)
            valid = rowi >= (P - 2 * n2) * CHUNK
            for g in range(GQ):
                kv = kvoff(g)
                hs = slice(g * HEAD_DIM, (g + 1) * HEAD_DIM)
                kvs = slice(kv * HEAD_DIM, (kv + 1) * HEAD_DIM)
                qTg = qT_ref[hs, pl.ds(r, PAIR)] * 0.125
                p, _ = _bandT_softmax(kp[kv, pl.ds(r, BU), :], qTg, b_ref[g], s_ref[g, 0:1, :], valid)
                oTg = jnp.dot(vTp[kvs, pl.ds(r, BU)], p.astype(BF16), preferred_element_type=F32)
                oT_ref[hs, pl.ds(r, PAIR)] = oTg.astype(oT_ref.dtype)
            return carry

        lax.fori_loop(0, npair, step, 0, unroll=min(2, npair))

    res, got = _call_hosting(
        body, comm=comm, grid=(ng,),
        in_specs=[_fm_spec(GQ * HEAD_DIM, S, q0),
                  pl.BlockSpec((GK, S, HEAD_DIM), lambda i: (i, 0, 0)),
                  _fm_spec(GK * HEAD_DIM, S, v0),
                  pl.BlockSpec((GQ, BU, PAIR), lambda i: (i, 0, 0)),
                  pl.BlockSpec((GQ, 8, LANE), lambda i: (i, 0, 0))],
        out_specs=[pl.BlockSpec((GQ * HEAD_DIM, S), lambda i: (i, 0))],
        out_shape=[SDS((ng * GQ * HEAD_DIM, S), BF16)],
        scratch_shapes=[pltpu.VMEM((GK, S + pad, HEAD_DIM), BF16), pltpu.VMEM((GK * HEAD_DIM, S + pad), BF16)],
        name=name, args=(qT, k_h, vT, bias, sink))
    return res[0], got


def _bandT_bwd(qT, q_h, k_h, kT, v_h, doT, do_h, bias, sink, *, GQ, GK, P, kvoff, name, comm=None):
    (qT, q0), (kT, k0), (doT, d0) = _fm(qT), _fm(kT), _fm(doT)
    S = qT.shape[1]
    ng = bias.shape[0] // GQ
    BU = (P + 2) * CHUNK
    pad = P * CHUNK
    npair = S // PAIR

    def body(qT_ref, q_ref, k_ref, kT_ref, v_ref, doT_ref, do_ref, b_ref, s_ref,
             dqT_ref, dk_ref, dv_ref, db_ref, dsk_ref, kp, kTp, vp, dkp, dvp):
        _pad_copy_rows(kp, k_ref, pad, S)
        _pad_copy_rows(vp, v_ref, pad, S)
        _pad_copy_lanes(kTp, kT_ref, pad, S)
        dkp[...] = jnp.zeros_like(dkp)
        dvp[...] = jnp.zeros_like(dvp)
        db_ref[...] = jnp.zeros_like(db_ref)
        rowi = lax.broadcasted_iota(jnp.int32, (BU, PAIR), 0)

        def step(n2, dsink):
            r = pl.multiple_of(n2 * PAIR, PAIR)
            valid = rowi >= (P - 2 * n2) * CHUNK
            new = []
            for g in range(GQ):
                kv = kvoff(g)
                hs = slice(g * HEAD_DIM, (g + 1) * HEAD_DIM)
                kvs = slice(kv * HEAD_DIM, (kv + 1) * HEAD_DIM)
                qTg = qT_ref[hs, pl.ds(r, PAIR)] * 0.125
                p, ps = _bandT_softmax(kp[kv, pl.ds(r, BU), :], qTg, b_ref[g], s_ref[g, 0:1, :], valid)
                dp = jnp.dot(vp[kv, pl.ds(r, BU), :], doT_ref[hs, pl.ds(r, PAIR)], preferred_element_type=F32)
                delta = jnp.sum(p * dp, axis=0, keepdims=True)
                ds = p * (dp - delta)
                new.append(dsink[g] - ps * delta)
                db_ref[g] += ds
                dsb = ds.astype(BF16)
                dq = jnp.dot(kTp[kvs, pl.ds(r, BU)], dsb, preferred_element_type=F32) * 0.125
                dqT_ref[hs, pl.ds(r, PAIR)] = dq.astype(dqT_ref.dtype)
                dkp[kv, pl.ds(r, BU), :] += jnp.dot(dsb, q_ref[g, pl.ds(r, PAIR), :] * 0.125,
                                                    preferred_element_type=F32)
                dvp[kv, pl.ds(r, BU), :] += jnp.dot(p.astype(BF16), do_ref[g, pl.ds(r, PAIR), :],
                                                    preferred_element_type=F32)
            return tuple(new)

        dsink = lax.fori_loop(0, npair, step, tuple(jnp.zeros((1, PAIR), F32) for _ in range(GQ)))
        for g in range(GQ):
            dsk_ref[g] = jnp.broadcast_to(jnp.sum(dsink[g], axis=1, keepdims=True), (8, LANE))
        dk_ref[...] = dkp[:, pad:pad + S, :].astype(dk_ref.dtype)
        dv_ref[...] = dvp[:, pad:pad + S, :].astype(dv_ref.dtype)

    qTs = pl.BlockSpec((GQ * HEAD_DIM, S), lambda i: (i, 0))
    qhs = pl.BlockSpec((GQ, S, HEAD_DIM), lambda i: (i, 0, 0))
    khs = pl.BlockSpec((GK, S, HEAD_DIM), lambda i: (i, 0, 0))
    bs = pl.BlockSpec((GQ, BU, PAIR), lambda i: (i, 0, 0))
    ss = pl.BlockSpec((GQ, 8, LANE), lambda i: (i, 0, 0))
    nkv = ng * GK
    return _call_hosting(
        body, comm=comm, grid=(ng,),
        in_specs=[_fm_spec(GQ * HEAD_DIM, S, q0), qhs, khs, _fm_spec(GK * HEAD_DIM, S, k0), khs,
                  _fm_spec(GQ * HEAD_DIM, S, d0), qhs, bs, ss],
        out_specs=[qTs, khs, khs, bs, ss],
        out_shape=[SDS((ng * GQ * HEAD_DIM, S), BF16), SDS((nkv, S, HEAD_DIM), BF16), SDS((nkv, S, HEAD_DIM), BF16),
                   SDS((ng * GQ, BU, PAIR), F32), SDS((ng * GQ, 8, LANE), F32)],
        scratch_shapes=[pltpu.VMEM((GK, S + pad, HEAD_DIM), BF16), pltpu.VMEM((GK * HEAD_DIM, S + pad), BF16),
                        pltpu.VMEM((GK, S + pad, HEAD_DIM), BF16),
                        pltpu.VMEM((GK, S + pad, HEAD_DIM), F32), pltpu.VMEM((GK, S + pad, HEAD_DIM), F32)],
        name=name, args=(qT, q_h, k_h, kT, v_h, doT, do_h, bias, sink))


def _pair_table(tab):
    t = jnp.transpose(tab, (0, 2, 1))
    lo = jnp.pad(t, ((0, 0), (0, CHUNK), (0, 0)), constant_values=NEG_INF)
    hi = jnp.pad(t, ((0, 0), (CHUNK, 0), (0, 0)), constant_values=NEG_INF)
    return jnp.concatenate([lo, hi], axis=2)


def _unpair_table(d):
    band = d.shape[1] - CHUNK
    return jnp.transpose(d[:, 0:band, 0:CHUNK] + d[:, CHUNK:CHUNK + band, CHUNK:PAIR], (0, 2, 1))


def _heads(a, n):
    return jnp.transpose(a.reshape(a.shape[0], n, HEAD_DIM), (1, 0, 2))


def _unheads(a):
    return jnp.transpose(a, (1, 0, 2)).reshape(a.shape[1], a.shape[0] * HEAD_DIM)


def _foxT_logits(kj, qTg, cq, ck, r, c, rowi, coli):
    s = jnp.dot(kj, qTg, preferred_element_type=F32)
    s = s + cq - ck
    return jnp.where(c + rowi <= r + coli, s, NEG_INF)


def _foxT_fwd(qT, k_h, vT, ck, cq, name, comm=None):
    (qT, q0), (vT, v0) = _fm(qT), _fm(vT)
    S = qT.shape[1]
    npair = k_h.shape[0] // 2
    BQ, BK = min(FOX_BQ, S), min(FOX_BK, S)
    nq = S // BQ
    heads = [slice(g * HEAD_DIM, (g + 1) * HEAD_DIM) for g in range(2)]

    def body(qT_ref, k_ref, vT_ref, ck_ref, cq_ref, oT_ref, lse_ref):
        rowi = lax.broadcasted_iota(jnp.int32, (BK, BQ), 0)
        coli = lax.broadcasted_iota(jnp.int32, (BK, BQ), 1)

        def qstep(i, carry):
            r = pl.multiple_of(i * BQ, BQ)
            qs = [qT_ref[hs, pl.ds(r, BQ)] * 0.125 for hs in heads]
            cqs = [cq_ref[g, :, pl.ds(r, BQ)] for g in range(2)]

            def kstep(j, st):
                c = pl.multiple_of(j * BK, BK)
                new = []
                for g, hs in enumerate(heads):
                    m, l, acc = st[g]
                    s = _foxT_logits(k_ref[g, pl.ds(c, BK), :], qs[g], cqs[g], ck_ref[g, pl.ds(c, BK), :],
                                     r, c, rowi, coli)
                    mn = jnp.maximum(m, jnp.max(s, axis=0, keepdims=True))
                    al = jnp.exp(m - mn)
                    e = jnp.exp(s - mn)
                    l = al * l + jnp.sum(e, axis=0, keepdims=True)
                    acc = al * acc + jnp.dot(vT_ref[hs, pl.ds(c, BK)], e.astype(BF16), preferred_element_type=F32)
                    new.append((mn, l, acc))
                return tuple(new)

            init = (jnp.full((1, BQ), NEG_INF, F32), jnp.zeros((1, BQ), F32), jnp.zeros((HEAD_DIM, BQ), F32))
            st = lax.fori_loop(0, (r + BQ + BK - 1) // BK, kstep, (init, init))
            for g, hs in enumerate(heads):
                m, l, acc = st[g]
                oT_ref[hs, pl.ds(r, BQ)] = (acc * (1.0 / l)).astype(oT_ref.dtype)
                lse_ref[g, :, pl.ds(r, BQ)] = m + jnp.log(l)
            return carry

        lax.fori_loop(0, nq, qstep, 0)

    fT = pl.BlockSpec((LANE, S), lambda i: (i, 0))
    hm = pl.BlockSpec((2, S, HEAD_DIM), lambda i: (i, 0, 0))
    col = pl.BlockSpec((2, S, 1), lambda i: (i, 0, 0))
    rw = pl.BlockSpec((2, 1, S), lambda i: (i, 0, 0))
    return _call_hosting(
        body, comm=comm, grid=(npair,), in_specs=[_fm_spec(LANE, S, q0), hm, _fm_spec(LANE, S, v0), col, rw],
        out_specs=[fT, rw],
        out_shape=[SDS((npair * LANE, S), BF16), SDS((2 * npair, 1, S), F32)], scratch_shapes=[],
        name=name, args=(qT, k_h, vT, ck, cq))


def _foxT_bwd(qT, q_h, k_h, kT, v_h, ck, cq, oT, doT, do_h, lse, name, comm=None):
    (qT, q0), (kT, k0), (doT, d0) = _fm(qT), _fm(kT), _fm(doT)
    S = qT.shape[1]
    npair = k_h.shape[0] // 2
    BQ, BK = min(FOX_BQ, S), min(FOX_BK, S)
    nq = S // BQ
    heads = [slice(g * HEAD_DIM, (g + 1) * HEAD_DIM) for g in range(2)]

    def body(qT_ref, q_ref, k_ref, kT_ref, v_ref, ck_ref, cq_ref, oT_ref, doT_ref, do_ref, lse_ref,
             dqT_ref, dk_ref, dv_ref, dck_ref, dcq_ref, dka, dva, qa_ref):
        qa_ref[:, :, 0:HEAD_DIM] = q_ref[...] * 0.125
        qa_ref[:, :, HEAD_DIM:LANE] = jnp.ones((2, S, LANE - HEAD_DIM), BF16)
        dka[...] = jnp.zeros_like(dka)
        dva[...] = jnp.zeros_like(dva)
        rowi = lax.broadcasted_iota(jnp.int32, (BK, BQ), 0)
        coli = lax.broadcasted_iota(jnp.int32, (BK, BQ), 1)

        def qstep(i, carry):
            r = pl.multiple_of(i * BQ, BQ)
            qs = [qT_ref[hs, pl.ds(r, BQ)] * 0.125 for hs in heads]
            dos = [doT_ref[hs, pl.ds(r, BQ)] for hs in heads]
            deltas = [jnp.sum(dos[g].astype(F32) * oT_ref[hs, pl.ds(r, BQ)].astype(F32), axis=0, keepdims=True)
                      for g, hs in enumerate(heads)]
            cqs = [cq_ref[g, :, pl.ds(r, BQ)] for g in range(2)]
            lses = [lse_ref[g, :, pl.ds(r, BQ)] for g in range(2)]

            def kstep(j, st):
                c = pl.multiple_of(j * BK, BK)
                new = []
                for g, hs in enumerate(heads):
                    dq, rs = st[g]
                    s = _foxT_logits(k_ref[g, pl.ds(c, BK), :], qs[g], cqs[g], ck_ref[g, pl.ds(c, BK), :],
                                     r, c, rowi, coli)
                    p = jnp.exp(s - lses[g])
                    dp = jnp.dot(v_ref[g, pl.ds(c, BK), :], dos[g], preferred_element_type=F32)
                    ds = p * (dp - deltas[g])
                    dsb = ds.astype(BF16)
                    dka[g, pl.ds(c, BK), :] += jnp.dot(dsb, qa_ref[g, pl.ds(r, BQ), :], preferred_element_type=F32)
                    dva[g, pl.ds(c, BK), :] += jnp.dot(p.astype(BF16), do_ref[g, pl.ds(r, BQ), :],
                                                      preferred_element_type=F32)
                    new.append((dq + jnp.dot(kT_ref[hs, pl.ds(c, BK)], dsb, preferred_element_type=F32),
                                rs + jnp.sum(dsb.astype(F32), axis=0, keepdims=True)))
                return tuple(new)

            init = (jnp.zeros((HEAD_DIM, BQ), F32), jnp.zeros((1, BQ), F32))
            st = lax.fori_loop(0, (r + BQ + BK - 1) // BK, kstep, (init, init))
            for g, hs in enumerate(heads):
                dqT_ref[hs, pl.ds(r, BQ)] = (st[g][0] * 0.125).astype(dqT_ref.dtype)
                dcq_ref[g, :, pl.ds(r, BQ)] = st[g][1]
            return carry

        lax.fori_loop(0, nq, qstep, 0)
        dk_ref[...] = dka[:, :, 0:HEAD_DIM].astype(dk_ref.dtype)
        dck_ref[...] = -dka[:, :, HEAD_DIM:HEAD_DIM + 1]
        dv_ref[...] = dva[...].astype(dv_ref.dtype)

    fT = pl.BlockSpec((LANE, S), lambda i: (i, 0))
    hm = pl.BlockSpec((2, S, HEAD_DIM), lambda i: (i, 0, 0))
    col = pl.BlockSpec((2, S, 1), lambda i: (i, 0, 0))
    rw = pl.BlockSpec((2, 1, S), lambda i: (i, 0, 0))
    nh = 2 * npair
    return _call_hosting(
        body, comm=comm, grid=(npair,),
        in_specs=[_fm_spec(LANE, S, q0), hm, hm, _fm_spec(LANE, S, k0), hm, col, rw, fT, _fm_spec(LANE, S, d0), hm, rw],
        out_specs=[fT, hm, hm, col, rw],
        out_shape=[SDS((npair * LANE, S), BF16), SDS((nh, S, HEAD_DIM), BF16), SDS((nh, S, HEAD_DIM), BF16),
                   SDS((nh, S, 1), F32), SDS((nh, 1, S), F32)],
        scratch_shapes=[pltpu.VMEM((2, S, LANE), F32), pltpu.VMEM((2, S, HEAD_DIM), F32),
                        pltpu.VMEM((2, S, LANE), BF16)],
        name=name, args=(qT, q_h, k_h, kT, v_h, ck, cq, oT, doT, do_h, lse))


def _split3(x):
    hi = x.astype(BF16)
    r1 = x - hi.astype(F32)
    mid = r1.astype(BF16)
    lo = (r1 - mid.astype(F32)).astype(BF16)
    return hi, mid, lo


def _tri_dot(tri, x):
    hi, mid, lo = _split3(x)
    return (jnp.dot(tri, hi, preferred_element_type=F32) + jnp.dot(tri, mid, preferred_element_type=F32)
            + jnp.dot(tri, lo, preferred_element_type=F32))


def _fox_cum(gf, bfo, name):
    S = gf.shape[0]
    nb = S // LANE
    fcol = (GF_COLS - LANE) // LANE

    def body(f_ref, b_ref, cum_ref):
        row = lax.broadcasted_iota(jnp.int32, (LANE, LANE), 0)
        col = lax.broadcasted_iota(jnp.int32, (LANE, LANE), 1)
        tri = jnp.where(row >= col, 1.0, 0.0).astype(BF16)
        carry = jnp.zeros((1, LANE), F32)
        for t in range(nb):
            xl = f_ref[t * LANE:(t + 1) * LANE, :] + b_ref[...]
            lf = jnp.minimum(xl, 0.0) - jnp.log(1.0 + jnp.exp(-jnp.abs(xl)))
            cblk = _tri_dot(tri, lf) + carry
            cum_ref[t * LANE:(t + 1) * LANE, :] = cblk
            carry = cblk[LANE - 1:LANE, :]

    return pl.pallas_call(
        body, grid=(1,), in_specs=[pl.BlockSpec((S, LANE), lambda i: (0, fcol)), _vec(LANE)],
        out_specs=pl.BlockSpec((S, LANE), lambda i: (0, 0)), out_shape=SDS((S, LANE), F32),
        compiler_params=_cparams(("arbitrary",)), name=name)(gf, bfo)


def _fox_cum_bwd(gf, bfo, dcum, name):
    S = gf.shape[0]
    nb = S // LANE
    fcol = (GF_COLS - LANE) // LANE

    def body(f_ref, b_ref, dc_ref, df_ref, db_ref):
        row = lax.broadcasted_iota(jnp.int32, (LANE, LANE), 0)
        col = lax.broadcasted_iota(jnp.int32, (LANE, LANE), 1)
        tri = jnp.where(row <= col, 1.0, 0.0).astype(BF16)
        carry = jnp.zeros((1, LANE), F32)
        tot = jnp.zeros((1, LANE), F32)
        for t in range(nb - 1, -1, -1):
            rows = slice(t * LANE, (t + 1) * LANE)
            dlf = _tri_dot(tri, dc_ref[rows, :]) + carry
            carry = dlf[0:1, :]
            xl = f_ref[rows, :] + b_ref[...]
            dfl = dlf * (1.0 / (1.0 + jnp.exp(xl)))
            df_ref[rows, :] = dfl.astype(df_ref.dtype)
            tot = tot + jnp.sum(dfl, axis=0, keepdims=True)
        db_ref[...] = tot

    return pl.pallas_call(
        body, grid=(1,),
        in_specs=[pl.BlockSpec((S, LANE), lambda i: (0, fcol)), _vec(LANE), pl.BlockSpec((S, LANE), lambda i: (0, 0))],
        out_specs=[pl.BlockSpec((S, LANE), lambda i: (0, 0)), _vec(LANE)],
        out_shape=[SDS((S, LANE), BF16), SDS((1, LANE), F32)],
        compiler_params=_cparams(("arbitrary",)), name=name)(gf, bfo, dcum)


REL_FAR = C_PREV * CHUNK - REL_CLIP


def _rel_onehot(qi, band):
    w = band - REL_FAR
    r = lax.broadcasted_iota(jnp.int32, (N_REL_PAD, w), 0)
    j = lax.broadcasted_iota(jnp.int32, (N_REL_PAD, w), 1) + REL_FAR
    idx = jnp.clip(C_PREV * CHUNK + qi - j, -REL_CLIP, REL_CLIP) + REL_CLIP
    return jnp.where(r == idx, 1.0, 0.0).astype(BF16)


def _rel_expand(rel, name):
    band = (C_PREV + 1) * CHUNK

    def body(rel_ref, o_ref):
        hi, mid, lo = _split3(rel_ref[...])
        far = jnp.broadcast_to(rel_ref[:, 2 * REL_CLIP:2 * REL_CLIP + 1], (N_HEADS, REL_FAR))

        def row(qi, carry):
            oh = _rel_onehot(qi, band)
            o_ref[qi, :, 0:REL_FAR] = far
            o_ref[qi, :, REL_FAR:band] = (jnp.dot(hi, oh, preferred_element_type=F32)
                                          + jnp.dot(mid, oh, preferred_element_type=F32)
                                          + jnp.dot(lo, oh, preferred_element_type=F32))
            return carry

        lax.fori_loop(0, CHUNK, row, 0, unroll=2)

    return pl.pallas_call(
        body, grid=(1,), in_specs=[pl.BlockSpec((N_HEADS, N_REL_PAD), lambda i: (0, 0))],
        out_specs=pl.BlockSpec((CHUNK, N_HEADS, band), lambda i: (0, 0, 0)),
        out_shape=SDS((CHUNK, N_HEADS, band), F32),
        compiler_params=_cparams(("arbitrary",)), name=name)(rel)


def _rel_reduce(dbias, name):
    band = (C_PREV + 1) * CHUNK
    NT = (((1,), (1,)), ((), ()))

    def body(d_ref, o_ref):
        def row(qi, st):
            acc, far = st
            oh = _rel_onehot(qi, band)
            hi, mid, lo = _split3(d_ref[qi, :, REL_FAR:band])
            acc = acc + (lax.dot_general(hi, oh, NT, preferred_element_type=F32)
                         + lax.dot_general(mid, oh, NT, preferred_element_type=F32)
                         + lax.dot_general(lo, oh, NT, preferred_element_type=F32))
            return acc, far + jnp.sum(d_ref[qi, :, 0:REL_FAR], axis=-1, keepdims=True)

        acc, far = lax.fori_loop(0, CHUNK, row, (jnp.zeros((N_HEADS, N_REL_PAD), F32), jnp.zeros((N_HEADS, 1), F32)),
                                 unroll=2)
        col = lax.broadcasted_iota(jnp.int32, (N_HEADS, N_REL_PAD), 1)
        o_ref[...] = acc + jnp.where(col == 2 * REL_CLIP, far, 0.0)

    return pl.pallas_call(
        body, grid=(1,), in_specs=[pl.BlockSpec((CHUNK, N_HEADS, band), lambda i: (0, 0, 0))],
        out_specs=pl.BlockSpec((N_HEADS, N_REL_PAD), lambda i: (0, 0)),
        out_shape=SDS((N_HEADS, N_REL_PAD), F32),
        compiler_params=_cparams(("arbitrary",)), name=name)(dbias)


def _alibi_table():
    qi = np.arange(CHUNK)[:, None]
    j = np.arange((A_PREV + 1) * CHUNK)[None, :]
    dist = np.abs(A_PREV * CHUNK + qi - j).astype(np.float32)
    slopes = np.exp2(-8.0 * np.arange(1, N_HEADS + 1, dtype=np.float32) / N_HEADS).astype(np.float32)
    return jnp.asarray(-slopes[:, None, None] * dist[None])


def _ada_fwd(c_all, w, b, name):
    n = w.shape[2]

    def body(c_ref, w_ref, b_ref, o_ref):
        cv = c_ref[...]
        cond = (cv * _sigmoid(cv)).astype(BF16)
        o_ref[0] = jnp.dot(cond, w_ref[0].astype(BF16), preferred_element_type=F32) + b_ref[0]

    return pl.pallas_call(
        body, grid=(DEPTH,),
        in_specs=[pl.BlockSpec((16, D_MODEL), lambda l: (0, 0)), pl.BlockSpec((1, D_MODEL, n), lambda l: (l, 0, 0)),
                  pl.BlockSpec((1, 1, n), lambda l: (l, 0, 0))],
        out_specs=pl.BlockSpec((1, 16, n), lambda l: (l, 0, 0)), out_shape=SDS((DEPTH, 16, n), F32),
        compiler_params=_cparams(("parallel",)), name=name)(c_all, w, b)


def _ada_bwd(c_t, dmod, name, comm=None):
    n = dmod.shape[2]
    bn = _blk(n, 512)
    tr = 256

    def body(c_ref, d_ref, o_ref):
        cv = c_ref[...]
        cond = (cv * _sigmoid(cv)).astype(BF16).astype(F32)
        dm = d_ref[0].astype(BF16).astype(F32)
        acc = cond[:, 0:1] * dm[0:1, :]
        for b_ in range(1, 8):
            acc = acc + cond[:, b_:b_ + 1] * dm[b_:b_ + 1, :]
        o_ref[0] = acc

    res, got = _call_hosting(
        body, comm=comm, grid=(DEPTH, D_MODEL // tr, n // bn),
        in_specs=[pl.BlockSpec((tr, 8), lambda l, i, j: (i, 0)), pl.BlockSpec((1, 8, bn), lambda l, i, j: (l, 0, j))],
        out_specs=[pl.BlockSpec((1, tr, bn), lambda l, i, j: (l, i, j))], out_shape=[SDS((DEPTH, D_MODEL, n), F32)],
        scratch_shapes=[], name=name, args=(c_t, dmod))
    return res[0], got


def _adamw(w, m, v, parts, name):
    L, R, C = w.shape
    per_layer = isinstance(parts, (list, tuple))
    plist = list(parts) if per_layer else [parts]
    P = plist[0].shape[0]
    tr = _blk_rows(R, max(16, (1 << 18) // C))
    nr = R // tr
    c1 = 1.0 - ADAM_B1 ** ADAM_STEP
    c2 = 1.0 - ADAM_B2 ** ADAM_STEP

    def total(p_ref):
        g = p_ref[0].astype(F32)
        for k in range(1, P):
            g = g + p_ref[k].astype(F32)
        return g

    def body(w_ref, m_ref, v_ref, *rest):
        p_refs, (g_ref, d_ref, nm_ref, nv_ref) = rest[:len(plist)], rest[len(plist):]
        g = total(p_refs[0])
        for k in range(1, len(plist)):
            g = jnp.where(pl.program_id(0) == k, total(p_refs[k]), g)
        mn = ADAM_B1 * m_ref[0] + (1.0 - ADAM_B1) * g
        vn = ADAM_B2 * v_ref[0] + (1.0 - ADAM_B2) * (g * g)
        m_hat = mn / c1
        v_hat = vn / c2
        g_ref[0] = g
        nm_ref[0] = mn
        nv_ref[0] = vn
        d_ref[0] = -ADAM_LR * (m_hat / (jnp.sqrt(v_hat) + ADAM_EPS) + ADAM_WD * w_ref[0])

    rs = pl.BlockSpec((1, tr, C), lambda l, i: (l, i, 0))
    if per_layer:
        def layer_spec(k):
            return pl.BlockSpec((P, tr, C), lambda l, i: (0, jnp.where(l == k, i, 0), 0))
        pspecs = [layer_spec(k) for k in range(L)]
    else:
        pspecs = [pl.BlockSpec((P, tr, C), lambda l, i: (0, l * nr + i, 0))]
    return pl.pallas_call(
        body, grid=(L, nr), in_specs=[rs, rs, rs] + pspecs, out_specs=[rs, rs, rs, rs],
        out_shape=[SDS((L, R, C), F32)] * 4, compiler_params=_cparams(("parallel", "parallel")),
        name=name)(w, m, v, *plist)


def _blk_rows(R, cap):
    if R <= cap:
        return R
    best = None
    for t in range(16, cap + 1, 16):
        if R % t == 0:
            best = t
    assert best is not None, (R, cap)
    return best


def _add_cast_rows(g, t, name):
    Q, R, C = g.shape
    half = R // 2
    tr = _blk_rows(half, max(16, (1 << 19) // C))
    nb = half // tr

    def body(lo_ref, hi_ref, t_ref, o_ref):
        c = lax.axis_index("c")

        @pl.when(c == 0)
        def _():
            o_ref[...] = (lo_ref[...].astype(F32) + t_ref[...].astype(F32)).astype(o_ref.dtype)

        @pl.when(c == 1)
        def _():
            o_ref[...] = (hi_ref[...].astype(F32) + t_ref[...].astype(F32)).astype(o_ref.dtype)

    bs = pl.BlockSpec((1, tr, C), lambda q, i: (q, i, 0))
    hi = pl.BlockSpec((1, tr, C), lambda q, i: (q, nb + i, 0))
    return pl.pallas_call(
        body, grid=(Q, nb), in_specs=[bs, hi, bs], out_specs=bs, out_shape=SDS((Q, half, C), BF16),
        compiler_params=_cparams(("parallel", "parallel")), name=name)(g, g, t)


def _coords():
    return lax.axis_index("x"), lax.axis_index("y"), lax.axis_index("c")


def _flip(v, bit):
    return 1 - v if bit else v


def _all_gather8(v, name):
    R = v.shape[0]

    def body(v_ref, o_ref, send_sems, recv_sems):
        x, y, c = _coords()
        me = 4 * x + 2 * y + c
        o_ref[me] = v_ref[...]
        copies = []
        for k in range(1, 8):
            peer = (_flip(x, k & 4), _flip(y, k & 2), _flip(c, k & 1))
            cp = pltpu.make_async_remote_copy(
                src_ref=v_ref, dst_ref=o_ref.at[me], send_sem=send_sems.at[k - 1], recv_sem=recv_sems.at[k - 1],
                device_id=peer, device_id_type=MESH)
            cp.start()
            copies.append(cp)
        for cp in copies:
            cp.wait_recv()
        for cp in copies:
            cp.wait_send()

    return pl.pallas_call(
        body, in_specs=[VMEM_SPEC], out_specs=VMEM_SPEC, out_shape=SDS((8, R, LANE), v.dtype),
        scratch_shapes=[pltpu.SemaphoreType.DMA((7,)), pltpu.SemaphoreType.DMA((7,))],
        compiler_params=pltpu.CompilerParams(vmem_limit_bytes=VMEM_LIMIT), name=name)(v)


def _sibling_swap_rows(arrs, name):
    n = len(arrs)

    def body(*refs):
        in_refs, out_refs = refs[:n], refs[n:2 * n]
        send_sems, recv_sems = refs[2 * n:]
        x, y, c = _coords()
        copies = []
        for a in range(n):
            Q, R = in_refs[a].shape[0], in_refs[a].shape[1]
            half = R // 2
            src = in_refs[a].at[pl.ds(0, Q), pl.ds(pl.multiple_of((1 - c) * half, 16), half)]
            cp = pltpu.make_async_remote_copy(
                src_ref=src, dst_ref=out_refs[a], send_sem=send_sems.at[a], recv_sem=recv_sems.at[a],
                device_id=(x, y, 1 - c), device_id_type=MESH)
            cp.start()
            copies.append(cp)
        for cp in copies:
            cp.wait_recv()
        for cp in copies:
            cp.wait_send()

    return pl.pallas_call(
        body, in_specs=[ANY] * n, out_specs=[ANY] * n,
        out_shape=[SDS((a.shape[0], a.shape[1] // 2, a.shape[2]), a.dtype) for a in arrs],
        scratch_shapes=[pltpu.SemaphoreType.DMA((n,)), pltpu.SemaphoreType.DMA((n,))],
        name=name)(*arrs)


class _OwnerReduce:
    aliased = False

    def __init__(self, srcs, lay):
        self.srcs, self.lay, self.n = list(srcs), lay, len(srcs)
        self.out_shapes = [SDS(a.shape, a.dtype) for a in self.srcs]
        self.sem_shapes = [pltpu.SemaphoreType.DMA((self.n, 3)), pltpu.SemaphoreType.DMA((self.n, 3)),
                           pltpu.SemaphoreType.DMA((self.n,))]

    def _copies(self, src_refs, dst_refs, sems):
        ici_send, ici_recv, loc_sem = sems
        x, y, c = _coords()
        p = 2 * x + y
        local, remote = [], []
        for a in range(self.n):
            local.append(pltpu.make_async_copy(src_refs[a].at[p], dst_refs[a].at[p], loc_sem.at[a]))
            for k in range(1, 4):
                qx, qy = _flip(x, k & 2), _flip(y, k & 1)
                remote.append(pltpu.make_async_remote_copy(
                    src_ref=src_refs[a].at[2 * qx + qy], dst_ref=dst_refs[a].at[p], send_sem=ici_send.at[a, k - 1],
                    recv_sem=ici_recv.at[a, k - 1], device_id=(qx, qy, self.lay), device_id_type=MESH))
        return c, local, remote

    def start(self, src_refs, dst_refs, sems):
        c, local, remote = self._copies(src_refs, dst_refs, sems)

        @pl.when(c == self.lay)
        def _():
            for cp in local + remote:
                cp.start()

    def finish(self, src_refs, dst_refs, sems):
        c, local, remote = self._copies(src_refs, dst_refs, sems)

        @pl.when(c == self.lay)
        def _():
            for cp in remote:
                cp.wait_recv()
            for cp in remote:
                cp.wait_send()
            for cp in local:
                cp.wait()


def _call_hosting(body, *, comm, grid, in_specs, out_specs, out_shape, scratch_shapes, name, args, semantics=None):
    n_in, n_out, n_scr = len(args), len(out_shape), len(scratch_shapes)
    if comm is None:
        sem = semantics if semantics is not None else ("parallel",) * len(grid)
        res = pl.pallas_call(body, grid=grid, in_specs=in_specs, out_specs=out_specs, out_shape=out_shape,
                             scratch_shapes=scratch_shapes, compiler_params=_cparams(sem), name=name)(*args)
        return list(res), None
    k = comm.n

    def hosted(*refs):
        ins, cin = refs[:n_in], refs[n_in:n_in + k]
        outs = refs[n_in + k:n_in + k + n_out]
        cout = refs[n_in + k + n_out:n_in + 2 * k + n_out]
        scr = refs[n_in + 2 * k + n_out:n_in + 2 * k + n_out + n_scr]
        sems = refs[n_in + 2 * k + n_out + n_scr:]
        first = pl.program_id(0) == 0
        last = pl.program_id(0) == grid[0] - 1
        for d in range(1, len(grid)):
            first = jnp.logical_and(first, pl.program_id(d) == 0)
            last = jnp.logical_and(last, pl.program_id(d) == grid[d] - 1)

        @pl.when(first)
        def _():
            comm.start(cin, cout, sems)

        body(*ins, *outs, *scr)

        @pl.when(last)
        def _():
            comm.finish(cin, cout, sems)

    aliases = {n_in + j: n_out + j for j in range(k)} if comm.aliased else {}
    res = pl.pallas_call(
        hosted, grid=grid, in_specs=list(in_specs) + [ANY] * k, out_specs=list(out_specs) + [ANY] * k,
        out_shape=list(out_shape) + comm.out_shapes, scratch_shapes=list(scratch_shapes) + comm.sem_shapes,
        input_output_aliases=aliases, compiler_params=_cparams(("arbitrary",) * len(grid)),
        name=name)(*args, *comm.srcs)
    return list(res[:n_out]), list(res[n_out:])


class _RowHalfGather:
    aliased = False

    def __init__(self, srcs):
        self.srcs, self.n = list(srcs), len(srcs)
        self.out_shapes = [SDS((4,) + a.shape, a.dtype) for a in self.srcs]
        n = self.n
        self.sem_shapes = [pltpu.SemaphoreType.DMA((n, 3)), pltpu.SemaphoreType.DMA((n, 3)),
                           pltpu.SemaphoreType.DMA((n, 3)), pltpu.SemaphoreType.DMA((n, 3)),
                           pltpu.SemaphoreType.DMA((n,))]

    def _copies(self, src_refs, dst_refs, sems):
        ici_send, ici_recv, d2d_send, d2d_recv, loc_sem = sems
        x, y, c = _coords()
        p = 2 * x + y
        local, first, fwd = [], [], []
        for a in range(self.n):
            R = src_refs[a].shape[0] // 2
            half = pl.ds(pl.multiple_of(c * R, 16), R)
            local.append(pltpu.make_async_copy(src_refs[a], dst_refs[a].at[p], loc_sem.at[a]))
            for k in range(1, 4):
                qx, qy = _flip(x, k & 2), _flip(y, k & 1)
                first.append(pltpu.make_async_remote_copy(
                    src_ref=src_refs[a].at[half], dst_ref=dst_refs[a].at[p, half], send_sem=ici_send.at[a, k - 1],
                    recv_sem=ici_recv.at[a, k - 1], device_id=(qx, qy, c), device_id_type=MESH))
                slot = dst_refs[a].at[2 * qx + qy, half]
                fwd.append(pltpu.make_async_remote_copy(
                    src_ref=slot, dst_ref=slot, send_sem=d2d_send.at[a, k - 1], recv_sem=d2d_recv.at[a, k - 1],
                    device_id=(x, y, 1 - c), device_id_type=MESH))
        return local, first, fwd

    def start(self, src_refs, dst_refs, sems):
        local, first, _ = self._copies(src_refs, dst_refs, sems)
        for cp in local + first:
            cp.start()

    def finish(self, src_refs, dst_refs, sems):
        local, first, fwd = self._copies(src_refs, dst_refs, sems)
        for got, on in zip(first, fwd):
            got.wait_recv()
            on.start()
        for cp in fwd:
            cp.wait_recv()
        for cp in first + fwd:
            cp.wait_send()
        for cp in local:
            cp.wait()

    def run(self, name):
        return _run_exchange(self, name)


def _run_exchange(comm, name):
    n = comm.n

    def body(*refs):
        src_refs, dst_refs, sems = refs[:n], refs[n:2 * n], refs[2 * n:]
        comm.start(src_refs, dst_refs, sems)
        comm.finish(src_refs, dst_refs, sems)

    return pl.pallas_call(body, in_specs=[ANY] * n, out_specs=[ANY] * n, out_shape=comm.out_shapes,
                          scratch_shapes=comm.sem_shapes, name=name)(*comm.srcs)


class _RowHalfReduce:
    aliased = False

    def __init__(self, srcs):
        self.srcs, self.n = list(srcs), len(srcs)
        self.out_shapes = [SDS((4, 2 * a.shape[1], a.shape[2]), a.dtype) for a in self.srcs]
        n = self.n
        self.sem_shapes = [pltpu.SemaphoreType.DMA((n, 3)), pltpu.SemaphoreType.DMA((n, 3)),
                           pltpu.SemaphoreType.DMA((n, 4)), pltpu.SemaphoreType.DMA((n, 4)),
                           pltpu.SemaphoreType.DMA((n,))]

    def _copies(self, src_refs, dst_refs, sems):
        ici_send, ici_recv, d2d_send, d2d_recv, loc_sem = sems
        x, y, c = _coords()
        p = 2 * x + y
        local, first, fwd = [], [], []
        for a in range(self.n):
            R = src_refs[a].shape[1]
            half = pl.ds(pl.multiple_of(c * R, 16), R)
            local.append(pltpu.make_async_copy(src_refs[a].at[p], dst_refs[a].at[p, half], loc_sem.at[a]))
            for k in range(4):
                qx, qy = _flip(x, k & 2), _flip(y, k & 1)
                if k:
                    first.append(pltpu.make_async_remote_copy(
                        src_ref=src_refs[a].at[2 * qx + qy], dst_ref=dst_refs[a].at[p, half],
                        send_sem=ici_send.at[a, k - 1], recv_sem=ici_recv.at[a, k - 1], device_id=(qx, qy, c),
                        device_id_type=MESH))
                slot = dst_refs[a].at[2 * qx + qy, half]
                fwd.append(pltpu.make_async_remote_copy(
                    src_ref=slot, dst_ref=slot, send_sem=d2d_send.at[a, k], recv_sem=d2d_recv.at[a, k],
                    device_id=(x, y, 1 - c), device_id_type=MESH))
        return local, first, fwd

    def start(self, src_refs, dst_refs, sems):
        local, first, _ = self._copies(src_refs, dst_refs, sems)
        for cp in local + first:
            cp.start()

    def finish(self, src_refs, dst_refs, sems):
        local, first, fwd = self._copies(src_refs, dst_refs, sems)
        for a in range(self.n):
            local[a].wait()
            fwd[4 * a].start()
            for k in range(1, 4):
                first[3 * a + k - 1].wait_recv()
                fwd[4 * a + k].start()
        for cp in fwd:
            cp.wait_recv()
        for cp in first + fwd:
            cp.wait_send()

    def run(self, name):
        return _run_exchange(self, name)


class _SiblingSend:
    aliased = False

    def __init__(self, srcs, src_core):
        self.srcs, self.src_core, self.n = list(srcs), src_core, len(srcs)
        self.out_shapes = [SDS(a.shape, a.dtype) for a in self.srcs]
        self.sem_shapes = [pltpu.SemaphoreType.DMA((self.n,)), pltpu.SemaphoreType.DMA((self.n,))]

    def _copies(self, src_refs, dst_refs, sems):
        x, y, c = _coords()
        return c, [pltpu.make_async_remote_copy(
            src_ref=src_refs[a], dst_ref=dst_refs[a], send_sem=sems[0].at[a], recv_sem=sems[1].at[a],
            device_id=(x, y, 1 - c), device_id_type=MESH) for a in range(self.n)]

    def start(self, src_refs, dst_refs, sems):
        c, copies = self._copies(src_refs, dst_refs, sems)

        @pl.when(c == self.src_core)
        def _():
            for cp in copies:
                cp.start()

    def finish(self, src_refs, dst_refs, sems):
        c, copies = self._copies(src_refs, dst_refs, sems)

        @pl.when(c == self.src_core)
        def _():
            for cp in copies:
                cp.wait_send()

        @pl.when(c != self.src_core)
        def _():
            for cp in copies:
                cp.wait_recv()


class _Handoff:
    aliased = True

    def __init__(self, srcs, lay, slots):
        self.srcs, self.lay, self.slots, self.n = list(srcs), lay, tuple(slots), len(srcs)
        self.out_shapes = [SDS(a.shape, a.dtype) for a in self.srcs]
        ns = len(self.slots)
        self.sem_shapes = [pltpu.SemaphoreType.DMA((self.n, ns)), pltpu.SemaphoreType.DMA((self.n, ns))]

    def _copies(self, dst_refs, sems):
        x, y, c = _coords()
        copies = []
        for a in range(self.n):
            for j, k in enumerate(self.slots):
                slot = dst_refs[a].at[2 * _flip(x, k & 2) + _flip(y, k & 1)]
                copies.append(pltpu.make_async_remote_copy(
                    src_ref=slot, dst_ref=slot, send_sem=sems[0].at[a, j], recv_sem=sems[1].at[a, j],
                    device_id=(x, y, 1 - c), device_id_type=MESH))
        return c, copies

    def start(self, src_refs, dst_refs, sems):
        c, copies = self._copies(dst_refs, sems)

        @pl.when(c == self.lay)
        def _():
            for cp in copies:
                cp.start()

    def finish(self, src_refs, dst_refs, sems):
        c, copies = self._copies(dst_refs, sems)

        @pl.when(c == self.lay)
        def _():
            for cp in copies:
                cp.wait_send()

        @pl.when(c != self.lay)
        def _():
            for cp in copies:
                cp.wait_recv()


def _add_cast_on(a, b, lay, name):
    Q, R, C = b.shape
    tr = _blk_rows(R, max(16, (1 << 19) // C))

    def body(a_ref, b_ref, o_ref):
        @pl.when(lax.axis_index("c") == lay)
        def _():
            o_ref[...] = (a_ref[...].astype(F32) + b_ref[...].astype(F32)).astype(o_ref.dtype)

    bs = pl.BlockSpec((1, tr, C), lambda q, i: (q, i, 0))
    return pl.pallas_call(
        body, grid=(Q, R // tr), in_specs=[bs, bs], out_specs=bs, out_shape=SDS((Q, R, C), BF16),
        compiler_params=_cparams(("parallel", "parallel")), name=name)(a, b)


_IN_SIZES = (512, 128, 128, 512, 512, 512, 8, 512, 512, 512, 3072)
_IN_OFF = tuple(int(v) for v in np.cumsum((0,) + _IN_SIZES))
_IN_Q = N_IN_COLS // 4


def _pack_w_in(w):
    def cols(lo, hi):
        out = []
        while lo < hi:
            q, off = divmod(lo, _IN_Q)
            n = min(hi - lo, _IN_Q - off)
            out.append(w[q, :, off:off + n])
            lo += n
        return out

    fb0, fb1, g0 = _IN_OFF[6], _IN_OFF[7], _IN_OFF[10]
    wqkv = jnp.concatenate(cols(0, fb0) + cols(fb1, g0), axis=1)
    wgf = jnp.concatenate(cols(g0, N_IN_COLS) + cols(fb0, fb1) + [jnp.zeros((w.shape[1], LANE - 8), w.dtype)], axis=1)
    return wqkv, wgf


def _unpack_w_in(dqkv, dgf):
    fb0, fb1, g0 = _IN_OFF[6], _IN_OFF[7], _IN_OFF[10]

    def cols(lo, hi):
        out = []
        while lo < hi:
            if lo < fb0:
                n = min(hi, fb0) - lo
                out.append(dqkv[:, lo:lo + n])
            elif lo < fb1:
                n = min(hi, fb1) - lo
                out.append(dgf[:, 3072 + lo - fb0:3072 + lo - fb0 + n])
            elif lo < g0:
                n = min(hi, g0) - lo
                out.append(dqkv[:, lo - 8:lo - 8 + n])
            else:
                n = hi - lo
                out.append(dgf[:, lo - g0:lo - g0 + n])
            lo += n
        return out

    return jnp.stack([jnp.concatenate(cols(q * _IN_Q, (q + 1) * _IN_Q), axis=1) for q in range(4)])


def _pad_rows(a, rows):
    return jnp.pad(a, ((0, rows - a.shape[0]), (0, 0)))


def _small_pack(parts):
    flat = jnp.concatenate([p.reshape(-1) for p in parts])
    n = flat.shape[0]
    rows = -(-n // LANE)
    rows = -(-rows // 8) * 8
    return jnp.pad(flat, (0, rows * LANE - n)).reshape(rows, LANE)


def _small_unpack(block, shapes):
    flat = block.reshape(-1)
    out, off = [], 0
    for s in shapes:
        n = int(np.prod(s))
        out.append(flat[off:off + n].reshape(s))
        off += n
    return out


def _kv_same(g):
    return 0


def _kv_own(g):
    return g


_mm_plain = _mm


def _mm_hosting(a, b, *, comm, **kw):
    if comm is None:
        return _mm(a, b, **kw), None
    return _mm(a, b, comm=comm, **kw)


def _layer_fwd(x, mod, p, l, ride):
    sh_m, sc_m, g_m, sh_f, sc_f, g_f = mod
    nm = "l%d_" % l

    def carried(name, run):
        res, got = run(ride.comm_for(name))
        if got is not None:
            ride.done(name, got)
        return res

    h1 = _norm_mod_fwd(x, p["norm_mix_g"], sc_m, sh_m, nm + "norm_mix_fwd")
    qkv = carried("proj_qkv", lambda cm: _mm_hosting(h1, p["wqkv"], mode="nn", out_dtype=BF16,
                                                     name=nm + "proj_qkv", comm=cm))
    gf = _mm(h1, p["wgf"], mode="nn", out_dtype=F32, name=nm + "proj_gf", cap_n=640)
    qkv_t = qkv.T
    o_a_t = carried("attn_a", lambda cm: _bandT_fwd(
        (qkv_t, 0), _heads(qkv[:, 512:640], A_KV_HEADS), (qkv_t, 640), p["alibi"], p["sink_tab"],
        GQ=4, GK=1, P=A_PREV, kvoff=_kv_same, name=nm + "attn_a_fwd", comm=cm))
    cum = _fox_cum(gf, p["b_forget_pad"], nm + "fox_cum")
    cum_t = cum[:, :N_HEADS].T
    cc, cr = cum_t[:, :, None], cum_t[:, None, :]
    o_b_t, lse_b = carried("attn_b", lambda cm: _foxT_fwd(
        (qkv_t, 768), _heads(qkv[:, 1280:1792], N_HEADS), (qkv_t, 1792), cc, cr, nm + "attn_b_fwd", comm=cm))
    o_c_t = carried("attn_c", lambda cm: _bandT_fwd(
        (qkv_t, 2304), _heads(qkv[:, 2816:3328], N_HEADS), (qkv_t, 3328), p["rel_tab"], p["no_sink"],
        GQ=2, GK=2, P=C_PREV, kvoff=_kv_own, name=nm + "attn_c_fwd", comm=cm))
    p = dict(p, **ride.late_weights())
    o = jnp.concatenate([o_a_t, o_b_t, o_c_t], axis=0).T
    y = _mm(o, p["wb"], mode="nn", out_dtype=BF16, groups=3, name=nm + "branch")
    merged = _merge_fwd(y, gf, nm + "merge_fwd")
    mix = _mm(merged, p["wout"], mode="nn", out_dtype=F32, name=nm + "out_proj")
    x1 = _resid_fwd(x, mix, g_m, nm + "resid_mix")
    h2 = _norm_mod_fwd(x1, p["norm_ffn_g"], sc_f, sh_f, nm + "norm_ffn_fwd")
    u = carried("ffn_in", lambda cm: _mm_hosting(h2, p["wfi"], mode="nn", out_dtype=BF16, name=nm + "ffn_in",
                                                 cap_n=512, comm=cm))
    a = _swiglu_fwd(u, nm + "swiglu_fwd")
    f = _mm(a, p["wfo"], mode="nn", out_dtype=F32, name=nm + "ffn_out", cap_m=1024)
    x2 = _resid_fwd(x1, f, g_f, nm + "resid_ffn")
    saved = dict(x=x, h1=h1, qkv=qkv, qkv_t=qkv_t, gf=gf, cc=cc, cr=cr, o_b_t=o_b_t, lse_b=lse_b, o=o, y=y, merged=merged,
                 mix=mix, x1=x1, h2=h2, u=u, a=a, f=f)
    return x2, saved, p


def _layer_bwd(dx2, mod, p, s, l, ride=None):
    sh_m, sc_m, g_m, sh_f, sc_f, g_f = mod
    nm = "l%d_" % l

    def _mm(a, b, *, name, **kw):
        comm = ride.comm_for(name) if ride is not None else None
        if comm is None:
            return _mm_plain(a, b, name=nm + name, **kw)
        out, got = _mm_plain(a, b, name=nm + name, comm=comm, **kw)
        ride.done(name, got)
        return out

    dg_f, df = _resid_bwd(dx2, s["f"], g_f, nm + "resid_ffn_bwd")
    da = _mm(df, p["wfo"], mode="nt", out_dtype=BF16, name="ffn_out_dx", cap_m=1024, cap_n=1408)
    d_wfo = _mm(s["a"], df, mode="tn", out_dtype=BF16, name="ffn_out_dw", cap_m=1408, cap_k=2048)
    du = _swiglu_bwd(da, s["u"], nm + "swiglu_bwd")
    dh2 = _mm(du, p["wfi"], mode="nt", out_dtype=F32, name="ffn_in_dx", cap_m=1024)
    d_wfi = _mm(s["h2"], du, mode="tn", out_dtype=BF16, name="ffn_in_dw", cap_m=1024, cap_n=1408, cap_k=2048,
                col_quarters=True)
    dx1, dsc_f, dsh_f, dgn_f = _norm_mod_bwd(s["x1"], [dh2], dx2, p["norm_ffn_g"], sc_f, nm + "norm_ffn_bwd")
    dg_m, dmix = _resid_bwd(dx1, s["mix"], g_m, nm + "resid_mix_bwd")
    dmerged = _mm(dmix, p["wout"], mode="nt", out_dtype=F32, name="out_proj_dx")
    d_wout = _mm(s["merged"], dmix, mode="tn", out_dtype=BF16, name="out_proj_dw", cap_m=1024, cap_k=2048)
    dy, dgates = _merge_bwd(dmerged, s["y"], s["gf"], nm + "merge_bwd")
    do = _mm(dy, p["wb"], mode="nt", out_dtype=BF16, groups=3, name="branch_dx")
    d_wb = _mm(s["o"], dy, mode="tn", out_dtype=BF16, groups=3, name="branch_dw", cap_k=2048,
               col_quarters=True)
    comms = ride.exchanges() if ride is not None else (None, None, None)
    qkv, qkv_t = s["qkv"], s["qkv_t"]
    do_t = do.T
    (dqa_t, dka_h, dva_h, _, dsink), got_a = _bandT_bwd(
        (qkv_t, 0), _heads(qkv[:, 0:512], N_HEADS), _heads(qkv[:, 512:640], A_KV_HEADS), (qkv_t, 512),
        _heads(qkv[:, 640:768], A_KV_HEADS), (do_t, 0), _heads(do[:, 0:512], N_HEADS), p["alibi"], p["sink_tab"],
        GQ=4, GK=1, P=A_PREV, kvoff=_kv_same, name=nm + "attn_a_bwd", comm=comms[0])
    (dqb_t, dkb_h, dvb_h, dck, dcq), got_b = _foxT_bwd(
        (qkv_t, 768), _heads(qkv[:, 768:1280], N_HEADS), _heads(qkv[:, 1280:1792], N_HEADS), (qkv_t, 1280),
        _heads(qkv[:, 1792:2304], N_HEADS), s["cc"], s["cr"], s["o_b_t"], (do_t, 512),
        _heads(do[:, 512:1024], N_HEADS), s["lse_b"], nm + "attn_b_bwd", comm=comms[1])
    dcum = jnp.pad((dck[:, :, 0] + dcq[:, 0, :]).T, ((0, 0), (0, LANE - N_HEADS)))
    dfb, db_forget = _fox_cum_bwd(s["gf"], p["b_forget_pad"], dcum, nm + "fox_cum_bwd")
    (dqc_t, dkc_h, dvc_h, dbias_c, _), got_c = _bandT_bwd(
        (qkv_t, 2304), _heads(qkv[:, 2304:2816], N_HEADS), _heads(qkv[:, 2816:3328], N_HEADS), (qkv_t, 2816),
        _heads(qkv[:, 3328:3840], N_HEADS), (do_t, 1024), _heads(do[:, 1024:1536], N_HEADS), p["rel_tab"],
        p["no_sink"], GQ=2, GK=2, P=C_PREV, kvoff=_kv_own, name=nm + "attn_c_bwd", comm=comms[2])
    d_rel = _rel_reduce(jnp.transpose(_unpair_table(dbias_c), (1, 0, 2)), nm + "rel_reduce")[:, :N_REL]
    dqkv = jnp.concatenate([dqa_t.T, _unheads(dka_h), _unheads(dva_h), dqb_t.T, _unheads(dkb_h), _unheads(dvb_h),
                            dqc_t.T, _unheads(dkc_h), _unheads(dvc_h)], axis=1)
    dgf = jnp.concatenate([dgates, dfb], axis=1)
    if ride is not None:
        ride.exchanged((got_a, got_b, got_c))
    dh1a = _mm(dqkv, p["wqkv"], mode="nt", out_dtype=F32, name="proj_qkv_dx", cap_k=1024)
    dh1b = _mm(dgf, p["wgf"], mode="nt", out_dtype=F32, name="proj_gf_dx", cap_k=640)
    d_wqkv = _mm(s["h1"], dqkv, mode="tn", out_dtype=BF16, name="proj_qkv_dw", cap_m=1024, cap_k=2048)
    d_wgf = _mm(s["h1"], dgf, mode="tn", out_dtype=BF16, name="proj_gf_dw", cap_m=1024, cap_n=640, cap_k=2048)
    dx, dsc_m, dsh_m, dgn_m = _norm_mod_bwd(s["x"], [dh1a, dh1b], dx1, p["norm_mix_g"], sc_m, nm + "norm_mix_bwd")
    d_mod = jnp.concatenate([dsh_m, dsc_m, dg_m, dsh_f, dsc_f, dg_f], axis=1)[0]
    grads = dict(w_in=_unpack_w_in(d_wqkv, d_wgf), w_branch=d_wb, w_out=d_wout.reshape(4, -1, D_MODEL),
                 w_ffn_in=d_wfi, w_ffn_out=d_wfo.reshape(4, -1, D_MODEL),
                 norm_mix_g=dgn_m[0], norm_ffn_g=dgn_f[0], b_forget=db_forget[0, :N_HEADS],
                 sinks=dsink[:, 0, 0], rel_bias=d_rel, d_mod=d_mod)
    return dx, grads


def kernel(x, c, norm_mix_g, norm_ffn_g, w_ada, b_ada, w_in, b_forget, sinks, rel_bias, w_branch, w_out, w_ffn_in, w_ffn_out, final_norm_g, loss_target, m_norm_mix_g, m_norm_ffn_g, m_w_ada, m_b_ada, m_w_in, m_b_forget, m_sinks, m_rel_bias, m_w_branch, m_w_out, m_w_ffn_in, m_w_ffn_out, m_final_norm_g, v_norm_mix_g, v_norm_ffn_g, v_w_ada, v_b_ada, v_w_in, v_b_forget, v_sinks, v_rel_bias, v_w_branch, v_w_out, v_w_ffn_in, v_w_ffn_out, v_final_norm_g):
    xi, yi, ci = _coords()
    chip = 2 * xi + yi
    dev = 2 * chip + ci
    xs = x[0]
    S = xs.shape[0]
    n_ada = w_ada.shape[2]

    big_names = ("w_in", "w_branch", "w_out", "w_ffn_in", "w_ffn_out")
    big_w = dict(w_in=w_in, w_branch=w_branch, w_out=w_out, w_ffn_in=w_ffn_in, w_ffn_out=w_ffn_out)
    big_m = dict(w_in=m_w_in, w_branch=m_w_branch, w_out=m_w_out, w_ffn_in=m_w_ffn_in, w_ffn_out=m_w_ffn_out)
    big_v = dict(w_in=v_w_in, w_branch=v_w_branch, w_out=v_w_out, w_ffn_in=v_w_ffn_in, w_ffn_out=v_w_ffn_out)
    flat2 = lambda a: a.reshape(-1, a.shape[-1])
    shards = [[flat2(big_w[n][l]).astype(BF16) for n in big_names] for l in range(DEPTH)]
    gw = [[None] * (len(big_names) + 2) for _ in range(DEPTH)]
    for l in range(DEPTH):
        shards[l] += [shards[l][0][:D_MODEL // 2], shards[l][0][D_MODEL // 2:]]
    gw[0][0] = _RowHalfGather([shards[0][0]]).run("weights_gather_w_in_l0")[0]
    host_g = ((1, 2, 4), (0,), (3,))

    class WeightRide:
        def __init__(self, l, plan):
            self.l, self.plan = l, plan

        def comm_for(self, name):
            if name not in self.plan:
                return None
            lay, idx = self.plan[name]
            return _RowHalfGather([shards[lay][i] for i in idx])

        def done(self, name, got):
            lay, idx = self.plan[name]
            for i, r in zip(idx, got):
                gw[lay][i] = r

        def late_weights(self):
            g = gw[self.l]
            return dict(wb=jnp.transpose(g[1], (1, 0, 2)).reshape(3 * BRANCH_W, D_MODEL),
                        wout=g[2].reshape(D_MODEL, D_MODEL),
                        wfi=jnp.transpose(g[3], (1, 0, 2)).reshape(D_MODEL, 2 * FFN_H),
                        wfo=g[4].reshape(FFN_H, D_MODEL))

    weight_plan = [
        {"proj_qkv": (0, (1, 2)), "attn_a": (0, (4,)), "attn_b": (0, (3,)), "attn_c": (1, (5,)), "ffn_in": (1, (6,))},
        {"attn_a": (1, (1, 2)), "attn_b": (1, (3,)), "attn_c": (1, (4,))}]


    c_all = _all_gather8(c.reshape(8, LANE), "gather_c").reshape(8, D_MODEL)
    b_sh = lax.dynamic_slice_in_dim(b_ada, chip * n_ada, n_ada, axis=1)[:, None, :]
    mod_sh = _ada_fwd(_pad_rows(c_all, 16), w_ada, b_sh, "ada_fwd")[:, :8, :]
    mod_all = _all_gather8(mod_sh.reshape(-1, LANE), "gather_mod").reshape(8, DEPTH, 8, n_ada)
    mod_mine = lax.dynamic_index_in_dim(mod_all[0::2], dev, axis=2, keepdims=False)
    mod = mod_mine.transpose(1, 0, 2).reshape(DEPTH, 6, D_MODEL)

    alibi = _pair_table(_alibi_table())
    no_sink = jnp.full((N_HEADS, 8, LANE), NEG_INF, F32)
    def make_params(l):
        if gw[l][0] is None:
            gw[l][0] = jnp.concatenate([gw[l][5], gw[l][6]], axis=1)
        wqkv, wgf = _pack_w_in(gw[l][0])
        rel_tab = _rel_expand(jnp.pad(rel_bias[l], ((0, 0), (0, N_REL_PAD - N_REL))), "l%d_rel_expand" % l)
        return dict(
            wqkv=wqkv, wgf=wgf, norm_mix_g=norm_mix_g[l][None], norm_ffn_g=norm_ffn_g[l][None],
            b_forget_pad=jnp.pad(b_forget[l], (0, LANE - N_HEADS))[None],
            sink_tab=jnp.broadcast_to(sinks[l][:, None, None], (N_HEADS, 8, LANE)),
            no_sink=no_sink, alibi=alibi, rel_tab=_pair_table(jnp.transpose(rel_tab, (1, 0, 2))))

    mods = [[mod[l, k][None] for k in range(6)] for l in range(DEPTH)]
    params, saved = [None] * DEPTH, [None] * DEPTH
    h = xs
    for l in range(DEPTH):
        h, saved[l], params[l] = _layer_fwd(h, mods[l], make_params(l), l, WeightRide(l, weight_plan[l]))
    loss_dev, dh, d_final = _final_loss(h, final_norm_g[None], loss_target[0], "final_loss")
    grads = [None] * DEPTH
    dh, grads[1] = _layer_bwd(dh, mods[1], params[1], saved[1], 1)

    class Layer1Ride:
        sends = {"ffn_out_dx": (4,), "ffn_out_dw": (1, 2), "ffn_in_dx": (3,), "ffn_in_dw": (0,)}
        hands = {"proj_qkv_dx": (0,), "proj_gf_dx": (3,), "proj_qkv_dw": (4,), "proj_gf_dw": (1, 2)}

        def __init__(self, g):
            self.g, self.t = g, [None] * len(g)
            self.parts, self.final = [None] * len(g), [None] * len(g)

        def comm_for(self, name):
            if name in self.sends:
                return _SiblingSend([self.g[i] for i in self.sends[name]], 0)
            if name in self.hands:
                return _Handoff([self.parts[i] for i in self.hands[name]], 1, (0, 1, 2, 3))
            return None

        def done(self, name, got):
            idx, dst = (self.sends[name], self.t) if name in self.sends else (self.hands[name], self.final)
            for i, r in zip(idx, got):
                dst[i] = r

        def exchanges(self):
            sums = [_add_cast_on(a, b, 1, "grads_chip_sum_l1_" + n) for n, a, b in zip(big_names, self.g, self.t)]
            return tuple(_OwnerReduce([sums[i] for i in idx], 1) for idx in host_g)

        def exchanged(self, got):
            for res, idx in zip(got, host_g):
                for r, i in zip(res, idx):
                    self.parts[i] = r

    ride = Layer1Ride([grads[1][n] for n in big_names])
    dh, grads[0] = _layer_bwd(dh, mods[0], params[0], saved[0], 0, ride)
    grad_x = dh[None]
    loss = lax.psum(loss_dev[0, 0], ("x", "y", "c"))
    parts1 = ride.final
    g0 = [grads[0][n] for n in big_names]
    t0 = _sibling_swap_rows(g0, "grads_swap_l0")
    sums0 = [_add_cast_rows(a, b, "grads_chip_sum_l0_" + n) for n, a, b in zip(big_names, g0, t0)]
    parts0 = [None] + list(_RowHalfReduce(sums0[1:]).run("grads_reduce_l0"))

    small_names = ("norm_mix_g", "norm_ffn_g", "b_ada", "b_forget", "sinks", "rel_bias", "final_norm_g")
    small_w = dict(norm_mix_g=norm_mix_g, norm_ffn_g=norm_ffn_g, b_ada=b_ada, b_forget=b_forget, sinks=sinks,
                   rel_bias=rel_bias, final_norm_g=final_norm_g)
    small_m = dict(norm_mix_g=m_norm_mix_g, norm_ffn_g=m_norm_ffn_g, b_ada=m_b_ada, b_forget=m_b_forget,
                   sinks=m_sinks, rel_bias=m_rel_bias, final_norm_g=m_final_norm_g)
    small_v = dict(norm_mix_g=v_norm_mix_g, norm_ffn_g=v_norm_ffn_g, b_ada=v_b_ada, b_forget=v_b_forget,
                   sinks=v_sinks, rel_bias=v_rel_bias, final_norm_g=v_final_norm_g)
    small_g = dict(
        norm_mix_g=jnp.stack([grads[l]["norm_mix_g"] for l in range(DEPTH)]),
        norm_ffn_g=jnp.stack([grads[l]["norm_ffn_g"] for l in range(DEPTH)]),
        b_ada=jnp.stack([grads[l]["d_mod"] for l in range(DEPTH)]),
        b_forget=jnp.stack([grads[l]["b_forget"] for l in range(DEPTH)]),
        sinks=jnp.stack([grads[l]["sinks"] for l in range(DEPTH)]),
        rel_bias=jnp.stack([grads[l]["rel_bias"] for l in range(DEPTH)]),
        final_norm_g=d_final[0])
    shapes = [small_w[n].shape for n in small_names]
    g_all = _all_gather8(_small_pack([small_g[n] for n in small_names]), "gather_small_grads")
    res = _adamw(_small_pack([small_w[n] for n in small_names])[None],
                    _small_pack([small_m[n] for n in small_names])[None],
                    _small_pack([small_v[n] for n in small_names])[None], g_all, "adamw_small")
    small_out = {n: [] for n in small_names}
    for r in res:
        for n, a in zip(small_names, _small_unpack(r[0], shapes)):
            small_out[n].append(a)
    off_b = sum(int(np.prod(s)) for s in shapes[:2])
    n_mod = DEPTH * 6 * D_MODEL
    dmod_all = g_all.reshape(8, -1)[:, off_b:off_b + n_mod].reshape(8, DEPTH, 6 * D_MODEL)
    dmod_sh = lax.dynamic_slice_in_dim(dmod_all, chip * n_ada, n_ada, axis=2).transpose(1, 0, 2)
    g_ada, got = _ada_bwd(c_all.T, dmod_sh, "ada_bwd", comm=_RowHalfReduce(sums0[:1]))
    parts0[0] = got[0]
    ada_out = _adamw(w_ada, m_w_ada, v_w_ada, flat2(g_ada)[None], "adamw_w_ada")

    big_out = {}
    as3 = lambda a: a.reshape(a.shape[0], -1, a.shape[-1])
    for n, p0, p1 in zip(big_names, parts0, parts1):
        res = _adamw(as3(big_w[n]), as3(big_m[n]), as3(big_v[n]), [p0, p1], "adamw_" + n)
        big_out[n] = [r.reshape(big_w[n].shape) for r in res]

    order = ("norm_mix_g", "norm_ffn_g", "w_ada", "b_ada", "w_in", "b_forget", "sinks", "rel_bias", "w_branch",
             "w_out", "w_ffn_in", "w_ffn_out", "final_norm_g")

    def pick(n, k):
        if n == "w_ada":
            return ada_out[k]
        if n in big_out:
            return big_out[n][k]
        return small_out[n][k]

    outs = [loss, grad_x]
    for k in range(4):
        outs += [pick(n, k) for n in order]
    return tuple(outs)
```

```python
import numpy as np
import jax
import jax.numpy as jnp
from jax import lax
from jax.experimental import pallas as pl
from jax.experimental.pallas import tpu as pltpu

F32 = jnp.float32
BF16 = jnp.bfloat16
SDS = jax.ShapeDtypeStruct

D_MODEL = 1024
DEPTH = 2
CHUNK = 64
HEAD_DIM = 64
EPS = 1e-6
NEG_INF = -1e30
N_HEADS = 8
A_KV_HEADS = 2
A_PREV = 2
C_PREV = 8
REL_CLIP = 128
N_REL = 2 * REL_CLIP + 1
N_REL_PAD = 384
BRANCH_W = 512
FFN_H = 2816
FOX_BQ = 256
FOX_BK = 512
GF_COLS = 3200
N_IN_COLS = 6920
LANE = 128
VMEM_LIMIT = 48 * 1024 * 1024

ADAM_LR = 0.001
ADAM_B1 = 0.9
ADAM_B2 = 0.999
ADAM_EPS = 1e-08
ADAM_WD = 0.01
ADAM_STEP = 10

MESH = pl.DeviceIdType.MESH
ANY = pl.BlockSpec(memory_space=pl.ANY)
VMEM_SPEC = pl.BlockSpec(memory_space=pltpu.VMEM)


def _cparams(sem=None):
    return pltpu.CompilerParams(dimension_semantics=sem, vmem_limit_bytes=VMEM_LIMIT)


def _blk(n, cap):
    if n <= cap:
        return n
    best = None
    for m in range(LANE, cap + 1, LANE):
        if n % m == 0:
            best = m
    assert best is not None, (n, cap)
    return best


def _sigmoid(x):
    return 1.0 / (1.0 + jnp.exp(-x))


def _mm(a, b, *, mode, out_dtype, name, groups=1, cap_m=2048, cap_n=1024, cap_k=1408, col_quarters=False,
        comm=None):
    G = groups
    assert not col_quarters or mode == "tn"
    if mode == "nn":
        M, K, N = a.shape[0], a.shape[1] // G, b.shape[1]
        assert b.shape[0] == G * K
    elif mode == "nt":
        M, K, N = a.shape[0], a.shape[1] // G, b.shape[0] // G
        assert b.shape[1] == K
    else:
        K, M, N = a.shape[0], a.shape[1] // G, b.shape[1] // G
        assert b.shape[0] == K
    bm, bn, bk = _blk(M, cap_m), _blk(N // 4 if col_quarters else N, cap_n), _blk(K, cap_k)
    nm, nn, nk = M // bm, N // bn, K // bk
    if mode == "nn":
        a_spec = pl.BlockSpec((bm, bk), lambda g, i, j, k: (i, g * nk + k))
        b_spec = pl.BlockSpec((bk, bn), lambda g, i, j, k: (g * nk + k, j))
        o_spec = pl.BlockSpec((bm, bn), lambda g, i, j, k: (i, g * nn + j))
        dims = (((1,), (0,)), ((), ()))
        out_shape = (M, G * N)
    elif mode == "nt":
        a_spec = pl.BlockSpec((bm, bk), lambda g, i, j, k: (i, g * nk + k))
        b_spec = pl.BlockSpec((bn, bk), lambda g, i, j, k: (g * nn + j, k))
        o_spec = pl.BlockSpec((bm, bn), lambda g, i, j, k: (i, g * nn + j))
        dims = (((1,), (1,)), ((), ()))
        out_shape = (M, G * N)
    else:
        a_spec = pl.BlockSpec((bk, bm), lambda g, i, j, k: (k, g * nm + i))
        b_spec = pl.BlockSpec((bk, bn), lambda g, i, j, k: (k, g * nn + j))
        dims = (((0,), (0,)), ((), ()))
        if col_quarters:
            nq = nn // 4
            o_spec = pl.BlockSpec((1, bm, bn), lambda g, i, j, k: (j // nq, g * nm + i, j % nq))
            out_shape = (4, G * M, N // 4)
        else:
            o_spec = pl.BlockSpec((bm, bn), lambda g, i, j, k: (g * nm + i, j))
            out_shape = (G * M, N)

    def product(a_ref, b_ref):
        return lax.dot_general(a_ref[...].astype(BF16), b_ref[...].astype(BF16), dims, preferred_element_type=F32)

    def body_one(a_ref, b_ref, o_ref):
        o_ref[...] = product(a_ref, b_ref).astype(o_ref.dtype).reshape(o_ref.shape)

    def body_acc(a_ref, b_ref, o_ref, acc_ref):
        k = pl.program_id(3)

        @pl.when(k == 0)
        def _():
            acc_ref[...] = jnp.zeros_like(acc_ref)

        acc_ref[...] += product(a_ref, b_ref)

        @pl.when(k == nk - 1)
        def _():
            o_ref[...] = acc_ref[...].astype(o_ref.dtype).reshape(o_ref.shape)

    res, got = _call_hosting(
        body_one if nk == 1 else body_acc, comm=comm, grid=(G, nm, nn, nk), in_specs=[a_spec, b_spec],
        out_specs=[o_spec], out_shape=[SDS(out_shape, out_dtype)],
        scratch_shapes=[] if nk == 1 else [pltpu.VMEM((bm, bn), F32)], name=name, args=(a, b),
        semantics=("parallel", "parallel", "parallel", "arbitrary"))
    return res[0] if comm is None else (res[0], got)


def _rows(tm, n, col=0):
    return pl.BlockSpec((tm, n), lambda i: (i, col))


def _vec(n):
    return pl.BlockSpec((1, n), lambda i: (0, 0))


def _tm(S):
    return min(S, 256)


def _norm_mod_fwd(x, g, sc, sh, name):
    S, Dm = x.shape
    tm = _tm(S)

    def body(x_ref, g_ref, sc_ref, sh_ref, h_ref):
        xv = x_ref[...]
        r = lax.rsqrt(jnp.mean(xv * xv, axis=-1, keepdims=True) + EPS)
        h_ref[...] = ((xv * r) * g_ref[...] * (1.0 + sc_ref[...]) + sh_ref[...]).astype(h_ref.dtype)

    return pl.pallas_call(
        body, grid=(S // tm,), in_specs=[_rows(tm, Dm), _vec(Dm), _vec(Dm), _vec(Dm)],
        out_specs=_rows(tm, Dm), out_shape=SDS((S, Dm), BF16),
        compiler_params=_cparams(("parallel",)), name=name)(x, g, sc, sh)


def _norm_mod_bwd(x, dh_list, dres, g, sc, name):
    S, Dm = x.shape
    tm = _tm(S)
    nh = len(dh_list)

    def body(*refs):
        x_ref = refs[0]
        dh_refs = refs[1:1 + nh]
        dres_ref, g_ref, sc_ref, dx_ref, dsc_ref, dsh_ref, dg_ref = refs[1 + nh:]
        i = pl.program_id(0)

        @pl.when(i == 0)
        def _():
            dsc_ref[...] = jnp.zeros_like(dsc_ref)
            dsh_ref[...] = jnp.zeros_like(dsh_ref)
            dg_ref[...] = jnp.zeros_like(dg_ref)

        xv = x_ref[...]
        dh = dh_refs[0][...]
        for r_ in dh_refs[1:]:
            dh = dh + r_[...]
        gv = g_ref[...]
        r = lax.rsqrt(jnp.mean(xv * xv, axis=-1, keepdims=True) + EPS)
        xn = xv * r
        xg = xn * gv
        dsh_ref[...] += jnp.sum(dh, axis=0, keepdims=True)
        dsc_ref[...] += jnp.sum(dh * xg, axis=0, keepdims=True)
        dxg = dh * (1.0 + sc_ref[...])
        dg_ref[...] += jnp.sum(dxg * xn, axis=0, keepdims=True)
        dxn = dxg * gv
        dx_ref[...] = dres_ref[...] + r * (dxn - xn * jnp.mean(dxn * xn, axis=-1, keepdims=True))

    return pl.pallas_call(
        body, grid=(S // tm,),
        in_specs=[_rows(tm, Dm)] * (2 + nh) + [_vec(Dm), _vec(Dm)],
        out_specs=[_rows(tm, Dm), _vec(Dm), _vec(Dm), _vec(Dm)],
        out_shape=[SDS((S, Dm), F32), SDS((1, Dm), F32), SDS((1, Dm), F32), SDS((1, Dm), F32)],
        compiler_params=_cparams(("arbitrary",)), name=name)(x, *dh_list, dres, g, sc)


def _resid_fwd(x, val, g, name):
    S, Dm = x.shape
    tm = _tm(S)

    def body(x_ref, v_ref, g_ref, o_ref):
        o_ref[...] = x_ref[...] + g_ref[...] * v_ref[...]

    return pl.pallas_call(
        body, grid=(S // tm,), in_specs=[_rows(tm, Dm), _rows(tm, Dm), _vec(Dm)],
        out_specs=_rows(tm, Dm), out_shape=SDS((S, Dm), F32),
        compiler_params=_cparams(("parallel",)), name=name)(x, val, g)


def _resid_bwd(dx, val, g, name):
    S, Dm = dx.shape
    tm = _tm(S)

    def body(dx_ref, v_ref, g_ref, dg_ref, dv_ref):
        @pl.when(pl.program_id(0) == 0)
        def _():
            dg_ref[...] = jnp.zeros_like(dg_ref)

        dxv = dx_ref[...]
        dg_ref[...] += jnp.sum(dxv * v_ref[...], axis=0, keepdims=True)
        dv_ref[...] = (dxv * g_ref[...]).astype(dv_ref.dtype)

    return pl.pallas_call(
        body, grid=(S // tm,), in_specs=[_rows(tm, Dm), _rows(tm, Dm), _vec(Dm)],
        out_specs=[_vec(Dm), _rows(tm, Dm)], out_shape=[SDS((1, Dm), F32), SDS((S, Dm), BF16)],
        compiler_params=_cparams(("arbitrary",)), name=name)(dx, val, g)


def _merge_fwd(y, gf, name):
    S = y.shape[0]
    tm = _tm(S)
    W = 3 * D_MODEL

    def body(y_ref, g_ref, o_ref):
        acc = None
        for k in range(3):
            sl = slice(k * D_MODEL, (k + 1) * D_MODEL)
            t = _sigmoid(g_ref[:, sl]) * y_ref[:, sl].astype(F32)
            acc = t if acc is None else acc + t
        o_ref[...] = acc.astype(o_ref.dtype)

    return pl.pallas_call(
        body, grid=(S // tm,), in_specs=[_rows(tm, W), _rows(tm, W)],
        out_specs=_rows(tm, D_MODEL), out_shape=SDS((S, D_MODEL), BF16),
        compiler_params=_cparams(("parallel",)), name=name)(y, gf)


def _merge_bwd(dm, y, gf, name):
    S = y.shape[0]
    tm = _tm(S)
    W = 3 * D_MODEL

    def body(dm_ref, y_ref, g_ref, dy_ref, dg_ref):
        dmv = dm_ref[...]
        for k in range(3):
            sl = slice(k * D_MODEL, (k + 1) * D_MODEL)
            sg = _sigmoid(g_ref[:, sl])
            dy_ref[:, sl] = (dmv * sg).astype(dy_ref.dtype)
            dg_ref[:, sl] = (dmv * y_ref[:, sl].astype(F32) * (sg * (1.0 - sg))).astype(dg_ref.dtype)

    return pl.pallas_call(
        body, grid=(S // tm,), in_specs=[_rows(tm, D_MODEL), _rows(tm, W), _rows(tm, W)],
        out_specs=[_rows(tm, W), _rows(tm, W)], out_shape=[SDS((S, W), BF16), SDS((S, W), BF16)],
        compiler_params=_cparams(("parallel",)), name=name)(dm, y, gf)


def _swiglu_fwd(u, name):
    S = u.shape[0]
    tm = _tm(S)

    def body(g_ref, u_ref, a_ref):
        gv = g_ref[...].astype(F32)
        a_ref[...] = (gv * _sigmoid(gv) * u_ref[...].astype(F32)).astype(a_ref.dtype)

    return pl.pallas_call(
        body, grid=(S // tm,), in_specs=[_rows(tm, FFN_H, 0), _rows(tm, FFN_H, 1)],
        out_specs=_rows(tm, FFN_H), out_shape=SDS((S, FFN_H), BF16),
        compiler_params=_cparams(("parallel",)), name=name)(u, u)


def _swiglu_bwd(da, u, name):
    S = u.shape[0]
    tm = _tm(S)

    def body(da_ref, g_ref, u_ref, du_ref):
        dav = da_ref[...].astype(F32)
        gv = g_ref[...].astype(F32)
        sg = _sigmoid(gv)
        du_ref[:, 0:FFN_H] = (dav * u_ref[...].astype(F32) * (sg * (1.0 + gv * (1.0 - sg)))).astype(du_ref.dtype)
        du_ref[:, FFN_H:2 * FFN_H] = (dav * (gv * sg)).astype(du_ref.dtype)

    return pl.pallas_call(
        body, grid=(S // tm,), in_specs=[_rows(tm, FFN_H), _rows(tm, FFN_H, 0), _rows(tm, FFN_H, 1)],
        out_specs=_rows(tm, 2 * FFN_H), out_shape=SDS((S, 2 * FFN_H), BF16),
        compiler_params=_cparams(("parallel",)), name=name)(da, u, u)


def _final_loss(x, g, target, name):
    S, Dm = x.shape
    tm = _tm(S)

    def body(x_ref, g_ref, t_ref, loss_ref, dx_ref, dg_ref):
        @pl.when(pl.program_id(0) == 0)
        def _():
            loss_ref[...] = jnp.zeros_like(loss_ref)
            dg_ref[...] = jnp.zeros_like(dg_ref)

        xv = x_ref[...]
        gv = g_ref[...]
        r = lax.rsqrt(jnp.mean(xv * xv, axis=-1, keepdims=True) + EPS)
        xn = xv * r
        err = xn * gv - t_ref[...]
        row = jnp.mean(err * err, axis=-1, keepdims=True)
        loss_ref[...] += 0.5 * jnp.sum(row, axis=0, keepdims=True)
        dy = err * (1.0 / Dm)
        dg_ref[...] += jnp.sum(dy * xn, axis=0, keepdims=True)
        dxn = dy * gv
        dx_ref[...] = r * (dxn - xn * jnp.mean(dxn * xn, axis=-1, keepdims=True))

    return pl.pallas_call(
        body, grid=(S // tm,), in_specs=[_rows(tm, Dm), _vec(Dm), _rows(tm, Dm)],
        out_specs=[pl.BlockSpec((1, 1), lambda i: (0, 0)), _rows(tm, Dm), _vec(Dm)],
        out_shape=[SDS((1, 1), F32), SDS((S, Dm), F32), SDS((1, Dm), F32)],
        compiler_params=_cparams(("arbitrary",)), name=name)(x, g, target)


PAIR = 2 * CHUNK


def _bandT_softmax(kg, qTg, bias, sink, valid):
    s = jnp.dot(kg, qTg, preferred_element_type=F32)
    s = jnp.where(valid, s + bias, NEG_INF)
    m = jnp.maximum(jnp.max(s, axis=0, keepdims=True), sink)
    e = jnp.exp(s - m)
    es = jnp.exp(sink - m)
    inv = 1.0 / (jnp.sum(e, axis=0, keepdims=True) + es)
    return e * inv, es * inv


def _pad_copy_rows(dst, src, pad, S):
    dst[:, 0:pad, :] = jnp.zeros((dst.shape[0], pad, dst.shape[2]), dst.dtype)
    dst[:, pad:pad + S, :] = src[...]


def _pad_copy_lanes(dst, src, pad, S):
    dst[:, 0:pad] = jnp.zeros((dst.shape[0], pad), dst.dtype)
    dst[:, pad:pad + S] = src[...]


def _fm(arg):
    return arg if isinstance(arg, tuple) else (arg, 0)


def _fm_spec(rows, S, row0):
    off, rem = divmod(row0, rows)
    assert rem == 0
    return pl.BlockSpec((rows, S), lambda i: (off + i, 0))


def _bandT_fwd(qT, k_h, vT, bias, sink, *, GQ, GK, P, kvoff, name, comm=None):
    (qT, q0), (vT, v0) = _fm(qT), _fm(vT)
    S = qT.shape[1]
    ng = bias.shape[0] // GQ
    BU = (P + 2) * CHUNK
    pad = P * CHUNK
    npair = S // PAIR

    def body(qT_ref, k_ref, vT_ref, b_ref, s_ref, oT_ref, kp, vTp):
        _pad_copy_rows(kp, k_ref, pad, S)
        _pad_copy_lanes(vTp, vT_ref, pad, S)
        rowi = lax.broadcasted_iota(jnp.int32, (BU, PAIR), 0)

        def step(n2, carry):
            r = pl.multiple_of(n2 * PAIR, PAIR)
            valid = rowi >= (P - 2 * n2) * CHUNK
            for g in range(GQ):
                kv = kvoff(g)
                hs = slice(g * HEAD_DIM, (g + 1) * HEAD_DIM)
                kvs = slice(kv * HEAD_DIM, (kv + 1) * HEAD_DIM)
                qTg = qT_ref[hs, pl.ds(r, PAIR)] * 0.125
                p, _ = _bandT_softmax(kp[kv, pl.ds(r, BU), :], qTg, b_ref[g], s_ref[g, 0:1, :], valid)
                oTg = jnp.dot(vTp[kvs, pl.ds(r, BU)], p.astype(BF16), preferred_element_type=F32)
                oT_ref[hs, pl.ds(r, PAIR)] = oTg.astype(oT_ref.dtype)
            return carry

        lax.fori_loop(0, npair, step, 0, unroll=min(2, npair))

    res, got = _call_hosting(
        body, comm=comm, grid=(ng,),
        in_specs=[_fm_spec(GQ * HEAD_DIM, S, q0),
                  pl.BlockSpec((GK, S, HEAD_DIM), lambda i: (i, 0, 0)),
                  _fm_spec(GK * HEAD_DIM, S, v0),
                  pl.BlockSpec((GQ, BU, PAIR), lambda i: (i, 0, 0)),
                  pl.BlockSpec((GQ, 8, LANE), lambda i: (i, 0, 0))],
        out_specs=[pl.BlockSpec((GQ * HEAD_DIM, S), lambda i: (i, 0))],
        out_shape=[SDS((ng * GQ * HEAD_DIM, S), BF16)],
        scratch_shapes=[pltpu.VMEM((GK, S + pad, HEAD_DIM), BF16), pltpu.VMEM((GK * HEAD_DIM, S + pad), BF16)],
        name=name, args=(qT, k_h, vT, bias, sink))
    return res[0], got


def _bandT_bwd(qT, q_h, k_h, kT, v_h, doT, do_h, bias, sink, *, GQ, GK, P, kvoff, name, comm=None):
    (qT, q0), (kT, k0), (doT, d0) = _fm(qT), _fm(kT), _fm(doT)
    S = qT.shape[1]
    ng = bias.shape[0] // GQ
    BU = (P + 2) * CHUNK
    pad = P * CHUNK
    npair = S // PAIR

    def body(qT_ref, q_ref, k_ref, kT_ref, v_ref, doT_ref, do_ref, b_ref, s_ref,
             dqT_ref, dk_ref, dv_ref, db_ref, dsk_ref, kp, kTp, vp, dkp, dvp):
        _pad_copy_rows(kp, k_ref, pad, S)
        _pad_copy_rows(vp, v_ref, pad, S)
        _pad_copy_lanes(kTp, kT_ref, pad, S)
        dkp[...] = jnp.zeros_like(dkp)
        dvp[...] = jnp.zeros_like(dvp)
        db_ref[...] = jnp.zeros_like(db_ref)
        rowi = lax.broadcasted_iota(jnp.int32, (BU, PAIR), 0)

        def step(n2, dsink):
            r = pl.multiple_of(n2 * PAIR, PAIR)
            valid = rowi >= (P - 2 * n2) * CHUNK
            new = []
            for g in range(GQ):
                kv = kvoff(g)
                hs = slice(g * HEAD_DIM, (g + 1) * HEAD_DIM)
                kvs = slice(kv * HEAD_DIM, (kv + 1) * HEAD_DIM)
                qTg = qT_ref[hs, pl.ds(r, PAIR)] * 0.125
                p, ps = _bandT_softmax(kp[kv, pl.ds(r, BU), :], qTg, b_ref[g], s_ref[g, 0:1, :], valid)
                dp = jnp.dot(vp[kv, pl.ds(r, BU), :], doT_ref[hs, pl.ds(r, PAIR)], preferred_element_type=F32)
                delta = jnp.sum(p * dp, axis=0, keepdims=True)
                ds = p * (dp - delta)
                new.append(dsink[g] - ps * delta)
                db_ref[g] += ds
                dsb = ds.astype(BF16)
                dq = jnp.dot(kTp[kvs, pl.ds(r, BU)], dsb, preferred_element_type=F32) * 0.125
                dqT_ref[hs, pl.ds(r, PAIR)] = dq.astype(dqT_ref.dtype)
                dkp[kv, pl.ds(r, BU), :] += jnp.dot(dsb, q_ref[g, pl.ds(r, PAIR), :] * 0.125,
                                                    preferred_element_type=F32)
                dvp[kv, pl.ds(r, BU), :] += jnp.dot(p.astype(BF16), do_ref[g, pl.ds(r, PAIR), :],
                                                    preferred_element_type=F32)
            return tuple(new)

        dsink = lax.fori_loop(0, npair, step, tuple(jnp.zeros((1, PAIR), F32) for _ in range(GQ)))
        for g in range(GQ):
            dsk_ref[g] = jnp.broadcast_to(jnp.sum(dsink[g], axis=1, keepdims=True), (8, LANE))
        dk_ref[...] = dkp[:, pad:pad + S, :].astype(dk_ref.dtype)
        dv_ref[...] = dvp[:, pad:pad + S, :].astype(dv_ref.dtype)

    qTs = pl.BlockSpec((GQ * HEAD_DIM, S), lambda i: (i, 0))
    qhs = pl.BlockSpec((GQ, S, HEAD_DIM), lambda i: (i, 0, 0))
    khs = pl.BlockSpec((GK, S, HEAD_DIM), lambda i: (i, 0, 0))
    bs = pl.BlockSpec((GQ, BU, PAIR), lambda i: (i, 0, 0))
    ss = pl.BlockSpec((GQ, 8, LANE), lambda i: (i, 0, 0))
    nkv = ng * GK
    return _call_hosting(
        body, comm=comm, grid=(ng,),
        in_specs=[_fm_spec(GQ * HEAD_DIM, S, q0), qhs, khs, _fm_spec(GK * HEAD_DIM, S, k0), khs,
                  _fm_spec(GQ * HEAD_DIM, S, d0), qhs, bs, ss],
        out_specs=[qTs, khs, khs, bs, ss],
        out_shape=[SDS((ng * GQ * HEAD_DIM, S), BF16), SDS((nkv, S, HEAD_DIM), BF16), SDS((nkv, S, HEAD_DIM), BF16),
                   SDS((ng * GQ, BU, PAIR), F32), SDS((ng * GQ, 8, LANE), F32)],
        scratch_shapes=[pltpu.VMEM((GK, S + pad, HEAD_DIM), BF16), pltpu.VMEM((GK * HEAD_DIM, S + pad), BF16),
                        pltpu.VMEM((GK, S + pad, HEAD_DIM), BF16),
                        pltpu.VMEM((GK, S + pad, HEAD_DIM), F32), pltpu.VMEM((GK, S + pad, HEAD_DIM), F32)],
        name=name, args=(qT, q_h, k_h, kT, v_h, doT, do_h, bias, sink))


def _pair_table(tab):
    t = jnp.transpose(tab, (0, 2, 1))
    lo = jnp.pad(t, ((0, 0), (0, CHUNK), (0, 0)), constant_values=NEG_INF)
    hi = jnp.pad(t, ((0, 0), (CHUNK, 0), (0, 0)), constant_values=NEG_INF)
    return jnp.concatenate([lo, hi], axis=2)


def _unpair_table(d):
    band = d.shape[1] - CHUNK
    return jnp.transpose(d[:, 0:band, 0:CHUNK] + d[:, CHUNK:CHUNK + band, CHUNK:PAIR], (0, 2, 1))


def _heads(a, n):
    return jnp.transpose(a.reshape(a.shape[0], n, HEAD_DIM), (1, 0, 2))


def _unheads(a):
    return jnp.transpose(a, (1, 0, 2)).reshape(a.shape[1], a.shape[0] * HEAD_DIM)


def _foxT_logits(kj, qTg, cq, ck, r, c, rowi, coli):
    s = jnp.dot(kj, qTg, preferred_element_type=F32)
    s = s + cq - ck
    return jnp.where(c + rowi <= r + coli, s, NEG_INF)


def _foxT_fwd(qT, k_h, vT, ck, cq, name, comm=None):
    (qT, q0), (vT, v0) = _fm(qT), _fm(vT)
    S = qT.shape[1]
    npair = k_h.shape[0] // 2
    BQ, BK = min(FOX_BQ, S), min(FOX_BK, S)
    nq = S // BQ
    heads = [slice(g * HEAD_DIM, (g + 1) * HEAD_DIM) for g in range(2)]

    def body(qT_ref, k_ref, vT_ref, ck_ref, cq_ref, oT_ref, lse_ref):
        rowi = lax.broadcasted_iota(jnp.int32, (BK, BQ), 0)
        coli = lax.broadcasted_iota(jnp.int32, (BK, BQ), 1)

        def qstep(i, carry):
            r = pl.multiple_of(i * BQ, BQ)
            qs = [qT_ref[hs, pl.ds(r, BQ)] * 0.125 for hs in heads]
            cqs = [cq_ref[g, :, pl.ds(r, BQ)] for g in range(2)]

            def kstep(j, st):
                c = pl.multiple_of(j * BK, BK)
                new = []
                for g, hs in enumerate(heads):
                    m, l, acc = st[g]
                    s = _foxT_logits(k_ref[g, pl.ds(c, BK), :], qs[g], cqs[g], ck_ref[g, pl.ds(c, BK), :],
                                     r, c, rowi, coli)
                    mn = jnp.maximum(m, jnp.max(s, axis=0, keepdims=True))
                    al = jnp.exp(m - mn)
                    e = jnp.exp(s - mn)
                    l = al * l + jnp.sum(e, axis=0, keepdims=True)
                    acc = al * acc + jnp.dot(vT_ref[hs, pl.ds(c, BK)], e.astype(BF16), preferred_element_type=F32)
                    new.append((mn, l, acc))
                return tuple(new)

            init = (jnp.full((1, BQ), NEG_INF, F32), jnp.zeros((1, BQ), F32), jnp.zeros((HEAD_DIM, BQ), F32))
            st = lax.fori_loop(0, (r + BQ + BK - 1) // BK, kstep, (init, init))
            for g, hs in enumerate(heads):
                m, l, acc = st[g]
                oT_ref[hs, pl.ds(r, BQ)] = (acc * (1.0 / l)).astype(oT_ref.dtype)
                lse_ref[g, :, pl.ds(r, BQ)] = m + jnp.log(l)
            return carry

        lax.fori_loop(0, nq, qstep, 0)

    fT = pl.BlockSpec((LANE, S), lambda i: (i, 0))
    hm = pl.BlockSpec((2, S, HEAD_DIM), lambda i: (i, 0, 0))
    col = pl.BlockSpec((2, S, 1), lambda i: (i, 0, 0))
    rw = pl.BlockSpec((2, 1, S), lambda i: (i, 0, 0))
    return _call_hosting(
        body, comm=comm, grid=(npair,), in_specs=[_fm_spec(LANE, S, q0), hm, _fm_spec(LANE, S, v0), col, rw],
        out_specs=[fT, rw],
        out_shape=[SDS((npair * LANE, S), BF16), SDS((2 * npair, 1, S), F32)], scratch_shapes=[],
        name=name, args=(qT, k_h, vT, ck, cq))


def _foxT_bwd(qT, q_h, k_h, kT, v_h, ck, cq, oT, doT, do_h, lse, name, comm=None):
    (qT, q0), (kT, k0), (doT, d0) = _fm(qT), _fm(kT), _fm(doT)
    S = qT.shape[1]
    npair = k_h.shape[0] // 2
    BQ, BK = min(FOX_BQ, S), min(FOX_BK, S)
    nq = S // BQ
    heads = [slice(g * HEAD_DIM, (g + 1) * HEAD_DIM) for g in range(2)]

    def body(qT_ref, q_ref, k_ref, kT_ref, v_ref, ck_ref, cq_ref, oT_ref, doT_ref, do_ref, lse_ref,
             dqT_ref, dk_ref, dv_ref, dck_ref, dcq_ref, dka, dva, qa_ref):
        qa_ref[:, :, 0:HEAD_DIM] = q_ref[...] * 0.125
        qa_ref[:, :, HEAD_DIM:LANE] = jnp.ones((2, S, LANE - HEAD_DIM), BF16)
        dka[...] = jnp.zeros_like(dka)
        dva[...] = jnp.zeros_like(dva)
        rowi = lax.broadcasted_iota(jnp.int32, (BK, BQ), 0)
        coli = lax.broadcasted_iota(jnp.int32, (BK, BQ), 1)

        def qstep(i, carry):
            r = pl.multiple_of(i * BQ, BQ)
            qs = [qT_ref[hs, pl.ds(r, BQ)] * 0.125 for hs in heads]
            dos = [doT_ref[hs, pl.ds(r, BQ)] for hs in heads]
            deltas = [jnp.sum(dos[g].astype(F32) * oT_ref[hs, pl.ds(r, BQ)].astype(F32), axis=0, keepdims=True)
                      for g, hs in enumerate(heads)]
            cqs = [cq_ref[g, :, pl.ds(r, BQ)] for g in range(2)]
            lses = [lse_ref[g, :, pl.ds(r, BQ)] for g in range(2)]

            def kstep(j, st):
                c = pl.multiple_of(j * BK, BK)
                new = []
                for g, hs in enumerate(heads):
                    dq, rs = st[g]
                    s = _foxT_logits(k_ref[g, pl.ds(c, BK), :], qs[g], cqs[g], ck_ref[g, pl.ds(c, BK), :],
                                     r, c, rowi, coli)
                    p = jnp.exp(s - lses[g])
                    dp = jnp.dot(v_ref[g, pl.ds(c, BK), :], dos[g], preferred_element_type=F32)
                    ds = p * (dp - deltas[g])
                    dsb = ds.astype(BF16)
                    dka[g, pl.ds(c, BK), :] += jnp.dot(dsb, qa_ref[g, pl.ds(r, BQ), :], preferred_element_type=F32)
                    dva[g, pl.ds(c, BK), :] += jnp.dot(p.astype(BF16), do_ref[g, pl.ds(r, BQ), :],
                                                      preferred_element_type=F32)
                    new.append((dq + jnp.dot(kT_ref[hs, pl.ds(c, BK)], dsb, preferred_element_type=F32),
                                rs + jnp.sum(dsb.astype(F32), axis=0, keepdims=True)))
                return tuple(new)

            init = (jnp.zeros((HEAD_DIM, BQ), F32), jnp.zeros((1, BQ), F32))
            st = lax.fori_loop(0, (r + BQ + BK - 1) // BK, kstep, (init, init))
            for g, hs in enumerate(heads):
                dqT_ref[hs, pl.ds(r, BQ)] = (st[g][0] * 0.125).astype(dqT_ref.dtype)
                dcq_ref[g, :, pl.ds(r, BQ)] = st[g][1]
            return carry

        lax.fori_loop(0, nq, qstep, 0)
        dk_ref[...] = dka[:, :, 0:HEAD_DIM].astype(dk_ref.dtype)
        dck_ref[...] = -dka[:, :, HEAD_DIM:HEAD_DIM + 1]
        dv_ref[...] = dva[...].astype(dv_ref.dtype)

    fT = pl.BlockSpec((LANE, S), lambda i: (i, 0))
    hm = pl.BlockSpec((2, S, HEAD_DIM), lambda i: (i, 0, 0))
    col = pl.BlockSpec((2, S, 1), lambda i: (i, 0, 0))
    rw = pl.BlockSpec((2, 1, S), lambda i: (i, 0, 0))
    nh = 2 * npair
    return _call_hosting(
        body, comm=comm, grid=(npair,),
        in_specs=[_fm_spec(LANE, S, q0), hm, hm, _fm_spec(LANE, S, k0), hm, col, rw, fT, _fm_spec(LANE, S, d0), hm, rw],
        out_specs=[fT, hm, hm, col, rw],
        out_shape=[SDS((npair * LANE, S), BF16), SDS((nh, S, HEAD_DIM), BF16), SDS((nh, S, HEAD_DIM), BF16),
                   SDS((nh, S, 1), F32), SDS((nh, 1, S), F32)],
        scratch_shapes=[pltpu.VMEM((2, S, LANE), F32), pltpu.VMEM((2, S, HEAD_DIM), F32),
                        pltpu.VMEM((2, S, LANE), BF16)],
        name=name, args=(qT, q_h, k_h, kT, v_h, ck, cq, oT, doT, do_h, lse))


def _split3(x):
    hi = x.astype(BF16)
    r1 = x - hi.astype(F32)
    mid = r1.astype(BF16)
    lo = (r1 - mid.astype(F32)).astype(BF16)
    return hi, mid, lo


def _tri_dot(tri, x):
    hi, mid, lo = _split3(x)
    return (jnp.dot(tri, hi, preferred_element_type=F32) + jnp.dot(tri, mid, preferred_element_type=F32)
            + jnp.dot(tri, lo, preferred_element_type=F32))


def _fox_cum(gf, bfo, name):
    S = gf.shape[0]
    nb = S // LANE
    fcol = (GF_COLS - LANE) // LANE

    def body(f_ref, b_ref, cum_ref):
        row = lax.broadcasted_iota(jnp.int32, (LANE, LANE), 0)
        col = lax.broadcasted_iota(jnp.int32, (LANE, LANE), 1)
        tri = jnp.where(row >= col, 1.0, 0.0).astype(BF16)
        carry = jnp.zeros((1, LANE), F32)
        for t in range(nb):
            xl = f_ref[t * LANE:(t + 1) * LANE, :] + b_ref[...]
            lf = jnp.minimum(xl, 0.0) - jnp.log(1.0 + jnp.exp(-jnp.abs(xl)))
            cblk = _tri_dot(tri, lf) + carry
            cum_ref[t * LANE:(t + 1) * LANE, :] = cblk
            carry = cblk[LANE - 1:LANE, :]

    return pl.pallas_call(
        body, grid=(1,), in_specs=[pl.BlockSpec((S, LANE), lambda i: (0, fcol)), _vec(LANE)],
        out_specs=pl.BlockSpec((S, LANE), lambda i: (0, 0)), out_shape=SDS((S, LANE), F32),
        compiler_params=_cparams(("arbitrary",)), name=name)(gf, bfo)


def _fox_cum_bwd(gf, bfo, dcum, name):
    S = gf.shape[0]
    nb = S // LANE
    fcol = (GF_COLS - LANE) // LANE

    def body(f_ref, b_ref, dc_ref, df_ref, db_ref):
        row = lax.broadcasted_iota(jnp.int32, (LANE, LANE), 0)
        col = lax.broadcasted_iota(jnp.int32, (LANE, LANE), 1)
        tri = jnp.where(row <= col, 1.0, 0.0).astype(BF16)
        carry = jnp.zeros((1, LANE), F32)
        tot = jnp.zeros((1, LANE), F32)
        for t in range(nb - 1, -1, -1):
            rows = slice(t * LANE, (t + 1) * LANE)
            dlf = _tri_dot(tri, dc_ref[rows, :]) + carry
            carry = dlf[0:1, :]
            xl = f_ref[rows, :] + b_ref[...]
            dfl = dlf * (1.0 / (1.0 + jnp.exp(xl)))
            df_ref[rows, :] = dfl.astype(df_ref.dtype)
            tot = tot + jnp.sum(dfl, axis=0, keepdims=True)
        db_ref[...] = tot

    return pl.pallas_call(
        body, grid=(1,),
        in_specs=[pl.BlockSpec((S, LANE), lambda i: (0, fcol)), _vec(LANE), pl.BlockSpec((S, LANE), lambda i: (0, 0))],
        out_specs=[pl.BlockSpec((S, LANE), lambda i: (0, 0)), _vec(LANE)],
        out_shape=[SDS((S, LANE), BF16), SDS((1, LANE), F32)],
        compiler_params=_cparams(("arbitrary",)), name=name)(gf, bfo, dcum)


REL_FAR = C_PREV * CHUNK - REL_CLIP


def _rel_onehot(qi, band):
    w = band - REL_FAR
    r = lax.broadcasted_iota(jnp.int32, (N_REL_PAD, w), 0)
    j = lax.broadcasted_iota(jnp.int32, (N_REL_PAD, w), 1) + REL_FAR
    idx = jnp.clip(C_PREV * CHUNK + qi - j, -REL_CLIP, REL_CLIP) + REL_CLIP
    return jnp.where(r == idx, 1.0, 0.0).astype(BF16)


def _rel_expand(rel, name):
    band = (C_PREV + 1) * CHUNK

    def body(rel_ref, o_ref):
        hi, mid, lo = _split3(rel_ref[...])
        far = jnp.broadcast_to(rel_ref[:, 2 * REL_CLIP:2 * REL_CLIP + 1], (N_HEADS, REL_FAR))

        def row(qi, carry):
            oh = _rel_onehot(qi, band)
            o_ref[qi, :, 0:REL_FAR] = far
            o_ref[qi, :, REL_FAR:band] = (jnp.dot(hi, oh, preferred_element_type=F32)
                                          + jnp.dot(mid, oh, preferred_element_type=F32)
                                          + jnp.dot(lo, oh, preferred_element_type=F32))
            return carry

        lax.fori_loop(0, CHUNK, row, 0, unroll=2)

    return pl.pallas_call(
        body, grid=(1,), in_specs=[pl.BlockSpec((N_HEADS, N_REL_PAD), lambda i: (0, 0))],
        out_specs=pl.BlockSpec((CHUNK, N_HEADS, band), lambda i: (0, 0, 0)),
        out_shape=SDS((CHUNK, N_HEADS, band), F32),
        compiler_params=_cparams(("arbitrary",)), name=name)(rel)


def _rel_reduce(dbias, name):
    band = (C_PREV + 1) * CHUNK
    NT = (((1,), (1,)), ((), ()))

    def body(d_ref, o_ref):
        def row(qi, st):
            acc, far = st
            oh = _rel_onehot(qi, band)
            hi, mid, lo = _split3(d_ref[qi, :, REL_FAR:band])
            acc = acc + (lax.dot_general(hi, oh, NT, preferred_element_type=F32)
                         + lax.dot_general(mid, oh, NT, preferred_element_type=F32)
                         + lax.dot_general(lo, oh, NT, preferred_element_type=F32))
            return acc, far + jnp.sum(d_ref[qi, :, 0:REL_FAR], axis=-1, keepdims=True)

        acc, far = lax.fori_loop(0, CHUNK, row, (jnp.zeros((N_HEADS, N_REL_PAD), F32), jnp.zeros((N_HEADS, 1), F32)),
                                 unroll=2)
        col = lax.broadcasted_iota(jnp.int32, (N_HEADS, N_REL_PAD), 1)
        o_ref[...] = acc + jnp.where(col == 2 * REL_CLIP, far, 0.0)

    return pl.pallas_call(
        body, grid=(1,), in_specs=[pl.BlockSpec((CHUNK, N_HEADS, band), lambda i: (0, 0, 0))],
        out_specs=pl.BlockSpec((N_HEADS, N_REL_PAD), lambda i: (0, 0)),
        out_shape=SDS((N_HEADS, N_REL_PAD), F32),
        compiler_params=_cparams(("arbitrary",)), name=name)(dbias)


def _alibi_table():
    qi = np.arange(CHUNK)[:, None]
    j = np.arange((A_PREV + 1) * CHUNK)[None, :]
    dist = np.abs(A_PREV * CHUNK + qi - j).astype(np.float32)
    slopes = np.exp2(-8.0 * np.arange(1, N_HEADS + 1, dtype=np.float32) / N_HEADS).astype(np.float32)
    return jnp.asarray(-slopes[:, None, None] * dist[None])


def _ada_fwd(c_all, w, b, name):
    n = w.shape[2]

    def body(c_ref, w_ref, b_ref, o_ref):
        cv = c_ref[...]
        cond = (cv * _sigmoid(cv)).astype(BF16)
        o_ref[0] = jnp.dot(cond, w_ref[0].astype(BF16), preferred_element_type=F32) + b_ref[0]

    return pl.pallas_call(
        body, grid=(DEPTH,),
        in_specs=[pl.BlockSpec((16, D_MODEL), lambda l: (0, 0)), pl.BlockSpec((1, D_MODEL, n), lambda l: (l, 0, 0)),
                  pl.BlockSpec((1, 1, n), lambda l: (l, 0, 0))],
        out_specs=pl.BlockSpec((1, 16, n), lambda l: (l, 0, 0)), out_shape=SDS((DEPTH, 16, n), F32),
        compiler_params=_cparams(("parallel",)), name=name)(c_all, w, b)


def _ada_bwd(c_t, dmod, name, comm=None):
    n = dmod.shape[2]
    bn = _blk(n, 512)
    tr = 256

    def body(c_ref, d_ref, o_ref):
        cv = c_ref[...]
        cond = (cv * _sigmoid(cv)).astype(BF16).astype(F32)
        dm = d_ref[0].astype(BF16).astype(F32)
        acc = cond[:, 0:1] * dm[0:1, :]
        for b_ in range(1, 8):
            acc = acc + cond[:, b_:b_ + 1] * dm[b_:b_ + 1, :]
        o_ref[0] = acc

    res, got = _call_hosting(
        body, comm=comm, grid=(DEPTH, D_MODEL // tr, n // bn),
        in_specs=[pl.BlockSpec((tr, 8), lambda l, i, j: (i, 0)), pl.BlockSpec((1, 8, bn), lambda l, i, j: (l, 0, j))],
        out_specs=[pl.BlockSpec((1, tr, bn), lambda l, i, j: (l, i, j))], out_shape=[SDS((DEPTH, D_MODEL, n), F32)],
        scratch_shapes=[], name=name, args=(c_t, dmod))
    return res[0], got


def _adamw(w, m, v, parts, name):
    L, R, C = w.shape
    per_layer = isinstance(parts, (list, tuple))
    plist = list(parts) if per_layer else [parts]
    P = plist[0].shape[0]
    tr = _blk_rows(R, max(16, (1 << 18) // C))
    nr = R // tr
    c1 = 1.0 - ADAM_B1 ** ADAM_STEP
    c2 = 1.0 - ADAM_B2 ** ADAM_STEP

    def total(p_ref):
        g = p_ref[0].astype(F32)
        for k in range(1, P):
            g = g + p_ref[k].astype(F32)
        return g

    def body(w_ref, m_ref, v_ref, *rest):
        p_refs, (g_ref, d_ref, nm_ref, nv_ref) = rest[:len(plist)], rest[len(plist):]
        g = total(p_refs[0])
        for k in range(1, len(plist)):
            g = jnp.where(pl.program_id(0) == k, total(p_refs[k]), g)
        mn = ADAM_B1 * m_ref[0] + (1.0 - ADAM_B1) * g
        vn = ADAM_B2 * v_ref[0] + (1.0 - ADAM_B2) * (g * g)
        m_hat = mn / c1
        v_hat = vn / c2
        g_ref[0] = g
        nm_ref[0] = mn
        nv_ref[0] = vn
        d_ref[0] = -ADAM_LR * (m_hat / (jnp.sqrt(v_hat) + ADAM_EPS) + ADAM_WD * w_ref[0])

    rs = pl.BlockSpec((1, tr, C), lambda l, i: (l, i, 0))
    if per_layer:
        def layer_spec(k):
            return pl.BlockSpec((P, tr, C), lambda l, i: (0, jnp.where(l == k, i, 0), 0))
        pspecs = [layer_spec(k) for k in range(L)]
    else:
        pspecs = [pl.BlockSpec((P, tr, C), lambda l, i: (0, l * nr + i, 0))]
    return pl.pallas_call(
        body, grid=(L, nr), in_specs=[rs, rs, rs] + pspecs, out_specs=[rs, rs, rs, rs],
        out_shape=[SDS((L, R, C), F32)] * 4, compiler_params=_cparams(("parallel", "parallel")),
        name=name)(w, m, v, *plist)


def _blk_rows(R, cap):
    if R <= cap:
        return R
    best = None
    for t in range(16, cap + 1, 16):
        if R % t == 0:
            best = t
    assert best is not None, (R, cap)
    return best


def _add_cast_rows(g, t, name):
    Q, R, C = g.shape
    half = R // 2
    tr = _blk_rows(half, max(16, (1 << 19) // C))
    nb = half // tr

    def body(lo_ref, hi_ref, t_ref, o_ref):
        c = lax.axis_index("c")

        @pl.when(c == 0)
        def _():
            o_ref[...] = (lo_ref[...].astype(F32) + t_ref[...].astype(F32)).astype(o_ref.dtype)

        @pl.when(c == 1)
        def _():
            o_ref[...] = (hi_ref[...].astype(F32) + t_ref[...].astype(F32)).astype(o_ref.dtype)

    bs = pl.BlockSpec((1, tr, C), lambda q, i: (q, i, 0))
    hi = pl.BlockSpec((1, tr, C), lambda q, i: (q, nb + i, 0))
    return pl.pallas_call(
        body, grid=(Q, nb), in_specs=[bs, hi, bs], out_specs=bs, out_shape=SDS((Q, half, C), BF16),
        compiler_params=_cparams(("parallel", "parallel")), name=name)(g, g, t)


def _coords():
    return lax.axis_index("x"), lax.axis_index("y"), lax.axis_index("c")


def _flip(v, bit):
    return 1 - v if bit else v


def _all_gather8(v, name):
    R = v.shape[0]

    def body(v_ref, o_ref, send_sems, recv_sems):
        x, y, c = _coords()
        me = 4 * x + 2 * y + c
        o_ref[me] = v_ref[...]
        copies = []
        for k in range(1, 8):
            peer = (_flip(x, k & 4), _flip(y, k & 2), _flip(c, k & 1))
            cp = pltpu.make_async_remote_copy(
                src_ref=v_ref, dst_ref=o_ref.at[me], send_sem=send_sems.at[k - 1], recv_sem=recv_sems.at[k - 1],
                device_id=peer, device_id_type=MESH)
            cp.start()
            copies.append(cp)
        for cp in copies:
            cp.wait_recv()
        for cp in copies:
            cp.wait_send()

    return pl.pallas_call(
        body, in_specs=[VMEM_SPEC], out_specs=VMEM_SPEC, out_shape=SDS((8, R, LANE), v.dtype),
        scratch_shapes=[pltpu.SemaphoreType.DMA((7,)), pltpu.SemaphoreType.DMA((7,))],
        compiler_params=pltpu.CompilerParams(vmem_limit_bytes=VMEM_LIMIT), name=name)(v)


def _sibling_swap_rows(arrs, name):
    n = len(arrs)

    def body(*refs):
        in_refs, out_refs = refs[:n], refs[n:2 * n]
        send_sems, recv_sems = refs[2 * n:]
        x, y, c = _coords()
        copies = []
        for a in range(n):
            Q, R = in_refs[a].shape[0], in_refs[a].shape[1]
            half = R // 2
            src = in_refs[a].at[pl.ds(0, Q), pl.ds(pl.multiple_of((1 - c) * half, 16), half)]
            cp = pltpu.make_async_remote_copy(
                src_ref=src, dst_ref=out_refs[a], send_sem=send_sems.at[a], recv_sem=recv_sems.at[a],
                device_id=(x, y, 1 - c), device_id_type=MESH)
            cp.start()
            copies.append(cp)
        for cp in copies:
            cp.wait_recv()
        for cp in copies:
            cp.wait_send()

    return pl.pallas_call(
        body, in_specs=[ANY] * n, out_specs=[ANY] * n,
        out_shape=[SDS((a.shape[0], a.shape[1] // 2, a.shape[2]), a.dtype) for a in arrs],
        scratch_shapes=[pltpu.SemaphoreType.DMA((n,)), pltpu.SemaphoreType.DMA((n,))],
        name=name)(*arrs)


class _OwnerReduce:
    aliased = False

    def __init__(self, srcs, lay):
        self.srcs, self.lay, self.n = list(srcs), lay, len(srcs)
        self.out_shapes = [SDS(a.shape, a.dtype) for a in self.srcs]
        self.sem_shapes = [pltpu.SemaphoreType.DMA((self.n, 3)), pltpu.SemaphoreType.DMA((self.n, 3)),
                           pltpu.SemaphoreType.DMA((self.n,))]

    def _copies(self, src_refs, dst_refs, sems):
        ici_send, ici_recv, loc_sem = sems
        x, y, c = _coords()
        p = 2 * x + y
        local, remote = [], []
        for a in range(self.n):
            local.append(pltpu.make_async_copy(src_refs[a].at[p], dst_refs[a].at[p], loc_sem.at[a]))
            for k in range(1, 4):
                qx, qy = _flip(x, k & 2), _flip(y, k & 1)
                remote.append(pltpu.make_async_remote_copy(
                    src_ref=src_refs[a].at[2 * qx + qy], dst_ref=dst_refs[a].at[p], send_sem=ici_send.at[a, k - 1],
                    recv_sem=ici_recv.at[a, k - 1], device_id=(qx, qy, self.lay), device_id_type=MESH))
        return c, local, remote

    def start(self, src_refs, dst_refs, sems):
        c, local, remote = self._copies(src_refs, dst_refs, sems)

        @pl.when(c == self.lay)
        def _():
            for cp in local + remote:
                cp.start()

    def finish(self, src_refs, dst_refs, sems):
        c, local, remote = self._copies(src_refs, dst_refs, sems)

        @pl.when(c == self.lay)
        def _():
            for cp in remote:
                cp.wait_recv()
            for cp in remote:
                cp.wait_send()
            for cp in local:
                cp.wait()


def _call_hosting(body, *, comm, grid, in_specs, out_specs, out_shape, scratch_shapes, name, args, semantics=None):
    n_in, n_out, n_scr = len(args), len(out_shape), len(scratch_shapes)
    if comm is None:
        sem = semantics if semantics is not None else ("parallel",) * len(grid)
        res = pl.pallas_call(body, grid=grid, in_specs=in_specs, out_specs=out_specs, out_shape=out_shape,
                             scratch_shapes=scratch_shapes, compiler_params=_cparams(sem), name=name)(*args)
        return list(res), None
    k = comm.n

    def hosted(*refs):
        ins, cin = refs[:n_in], refs[n_in:n_in + k]
        outs = refs[n_in + k:n_in + k + n_out]
        cout = refs[n_in + k + n_out:n_in + 2 * k + n_out]
        scr = refs[n_in + 2 * k + n_out:n_in + 2 * k + n_out + n_scr]
        sems = refs[n_in + 2 * k + n_out + n_scr:]
        first = pl.program_id(0) == 0
        last = pl.program_id(0) == grid[0] - 1
        for d in range(1, len(grid)):
            first = jnp.logical_and(first, pl.program_id(d) == 0)
            last = jnp.logical_and(last, pl.program_id(d) == grid[d] - 1)

        @pl.when(first)
        def _():
            comm.start(cin, cout, sems)

        body(*ins, *outs, *scr)

        @pl.when(last)
        def _():
            comm.finish(cin, cout, sems)

    aliases = {n_in + j: n_out + j for j in range(k)} if comm.aliased else {}
    res = pl.pallas_call(
        hosted, grid=grid, in_specs=list(in_specs) + [ANY] * k, out_specs=list(out_specs) + [ANY] * k,
        out_shape=list(out_shape) + comm.out_shapes, scratch_shapes=list(scratch_shapes) + comm.sem_shapes,
        input_output_aliases=aliases, compiler_params=_cparams(("arbitrary",) * len(grid)),
        name=name)(*args, *comm.srcs)
    return list(res[:n_out]), list(res[n_out:])


class _RowHalfGather:
    aliased = False

    def __init__(self, srcs):
        self.srcs, self.n = list(srcs), len(srcs)
        self.out_shapes = [SDS((4,) + a.shape, a.dtype) for a in self.srcs]
        n = self.n
        self.sem_shapes = [pltpu.SemaphoreType.DMA((n, 3)), pltpu.SemaphoreType.DMA((n, 3)),
                           pltpu.SemaphoreType.DMA((n, 3)), pltpu.SemaphoreType.DMA((n, 3)),
                           pltpu.SemaphoreType.DMA((n,))]

    def _copies(self, src_refs, dst_refs, sems):
        ici_send, ici_recv, d2d_send, d2d_recv, loc_sem = sems
        x, y, c = _coords()
        p = 2 * x + y
        local, first, fwd = [], [], []
        for a in range(self.n):
            R = src_refs[a].shape[0] // 2
            half = pl.ds(pl.multiple_of(c * R, 16), R)
            local.append(pltpu.make_async_copy(src_refs[a], dst_refs[a].at[p], loc_sem.at[a]))
            for k in range(1, 4):
                qx, qy = _flip(x, k & 2), _flip(y, k & 1)
                first.append(pltpu.make_async_remote_copy(
                    src_ref=src_refs[a].at[half], dst_ref=dst_refs[a].at[p, half], send_sem=ici_send.at[a, k - 1],
                    recv_sem=ici_recv.at[a, k - 1], device_id=(qx, qy, c), device_id_type=MESH))
                slot = dst_refs[a].at[2 * qx + qy, half]
                fwd.append(pltpu.make_async_remote_copy(
                    src_ref=slot, dst_ref=slot, send_sem=d2d_send.at[a, k - 1], recv_sem=d2d_recv.at[a, k - 1],
                    device_id=(x, y, 1 - c), device_id_type=MESH))
        return local, first, fwd

    def start(self, src_refs, dst_refs, sems):
        local, first, _ = self._copies(src_refs, dst_refs, sems)
        for cp in local + first:
            cp.start()

    def finish(self, src_refs, dst_refs, sems):
        local, first, fwd = self._copies(src_refs, dst_refs, sems)
        for got, on in zip(first, fwd):
            got.wait_recv()
            on.start()
        for cp in fwd:
            cp.wait_recv()
        for cp in first + fwd:
            cp.wait_send()
        for cp in local:
            cp.wait()

    def run(self, name):
        return _run_exchange(self, name)


def _run_exchange(comm, name):
    n = comm.n

    def body(*refs):
        src_refs, dst_refs, sems = refs[:n], refs[n:2 * n], refs[2 * n:]
        comm.start(src_refs, dst_refs, sems)
        comm.finish(src_refs, dst_refs, sems)

    return pl.pallas_call(body, in_specs=[ANY] * n, out_specs=[ANY] * n, out_shape=comm.out_shapes,
                          scratch_shapes=comm.sem_shapes, name=name)(*comm.srcs)


class _RowHalfReduce:
    aliased = False

    def __init__(self, srcs):
        self.srcs, self.n = list(srcs), len(srcs)
        self.out_shapes = [SDS((4, 2 * a.shape[1], a.shape[2]), a.dtype) for a in self.srcs]
        n = self.n
        self.sem_shapes = [pltpu.SemaphoreType.DMA((n, 3)), pltpu.SemaphoreType.DMA((n, 3)),
                           pltpu.SemaphoreType.DMA((n, 4)), pltpu.SemaphoreType.DMA((n, 4)),
                           pltpu.SemaphoreType.DMA((n,))]

    def _copies(self, src_refs, dst_refs, sems):
        ici_send, ici_recv, d2d_send, d2d_recv, loc_sem = sems
        x, y, c = _coords()
        p = 2 * x + y
        local, first, fwd = [], [], []
        for a in range(self.n):
            R = src_refs[a].shape[1]
            half = pl.ds(pl.multiple_of(c * R, 16), R)
            local.append(pltpu.make_async_copy(src_refs[a].at[p], dst_refs[a].at[p, half], loc_sem.at[a]))
            for k in range(4):
                qx, qy = _flip(x, k & 2), _flip(y, k & 1)
                if k:
                    first.append(pltpu.make_async_remote_copy(
                        src_ref=src_refs[a].at[2 * qx + qy], dst_ref=dst_refs[a].at[p, half],
                        send_sem=ici_send.at[a, k - 1], recv_sem=ici_recv.at[a, k - 1], device_id=(qx, qy, c),
                        device_id_type=MESH))
                slot = dst_refs[a].at[2 * qx + qy, half]
                fwd.append(pltpu.make_async_remote_copy(
                    src_ref=slot, dst_ref=slot, send_sem=d2d_send.at[a, k], recv_sem=d2d_recv.at[a, k],
                    device_id=(x, y, 1 - c), device_id_type=MESH))
        return local, first, fwd

    def start(self, src_refs, dst_refs, sems):
        local, first, _ = self._copies(src_refs, dst_refs, sems)
        for cp in local + first:
            cp.start()

    def finish(self, src_refs, dst_refs, sems):
        local, first, fwd = self._copies(src_refs, dst_refs, sems)
        for a in range(self.n):
            local[a].wait()
            fwd[4 * a].start()
            for k in range(1, 4):
                first[3 * a + k - 1].wait_recv()
                fwd[4 * a + k].start()
        for cp in fwd:
            cp.wait_recv()
        for cp in first + fwd:
            cp.wait_send()

    def run(self, name):
        return _run_exchange(self, name)


class _SiblingSend:
    aliased = False

    def __init__(self, srcs, src_core):
        self.srcs, self.src_core, self.n = list(srcs), src_core, len(srcs)
        self.out_shapes = [SDS(a.shape, a.dtype) for a in self.srcs]
        self.sem_shapes = [pltpu.SemaphoreType.DMA((self.n,)), pltpu.SemaphoreType.DMA((self.n,))]

    def _copies(self, src_refs, dst_refs, sems):
        x, y, c = _coords()
        return c, [pltpu.make_async_remote_copy(
            src_ref=src_refs[a], dst_ref=dst_refs[a], send_sem=sems[0].at[a], recv_sem=sems[1].at[a],
            device_id=(x, y, 1 - c), device_id_type=MESH) for a in range(self.n)]

    def start(self, src_refs, dst_refs, sems):
        c, copies = self._copies(src_refs, dst_refs, sems)

        @pl.when(c == self.src_core)
        def _():
            for cp in copies:
                cp.start()

    def finish(self, src_refs, dst_refs, sems):
        c, copies = self._copies(src_refs, dst_refs, sems)

        @pl.when(c == self.src_core)
        def _():
            for cp in copies:
                cp.wait_send()

        @pl.when(c != self.src_core)
        def _():
            for cp in copies:
                cp.wait_recv()


class _Handoff:
    aliased = True

    def __init__(self, srcs, lay, slots):
        self.srcs, self.lay, self.slots, self.n = list(srcs), lay, tuple(slots), len(srcs)
        self.out_shapes = [SDS(a.shape, a.dtype) for a in self.srcs]
        ns = len(self.slots)
        self.sem_shapes = [pltpu.SemaphoreType.DMA((self.n, ns)), pltpu.SemaphoreType.DMA((self.n, ns))]

    def _copies(self, dst_refs, sems):
        x, y, c = _coords()
        copies = []
        for a in range(self.n):
            for j, k in enumerate(self.slots):
                slot = dst_refs[a].at[2 * _flip(x, k & 2) + _flip(y, k & 1)]
                copies.append(pltpu.make_async_remote_copy(
                    src_ref=slot, dst_ref=slot, send_sem=sems[0].at[a, j], recv_sem=sems[1].at[a, j],
                    device_id=(x, y, 1 - c), device_id_type=MESH))
        return c, copies

    def start(self, src_refs, dst_refs, sems):
        c, copies = self._copies(dst_refs, sems)

        @pl.when(c == self.lay)
        def _():
            for cp in copies:
                cp.start()

    def finish(self, src_refs, dst_refs, sems):
        c, copies = self._copies(dst_refs, sems)

        @pl.when(c == self.lay)
        def _():
            for cp in copies:
                cp.wait_send()

        @pl.when(c != self.lay)
        def _():
            for cp in copies:
                cp.wait_recv()


def _add_cast_on(a, b, lay, name):
    Q, R, C = b.shape
    tr = _blk_rows(R, max(16, (1 << 19) // C))

    def body(a_ref, b_ref, o_ref):
        @pl.when(lax.axis_index("c") == lay)
        def _():
            o_ref[...] = (a_ref[...].astype(F32) + b_ref[...].astype(F32)).astype(o_ref.dtype)

    bs = pl.BlockSpec((1, tr, C), lambda q, i: (q, i, 0))
    return pl.pallas_call(
        body, grid=(Q, R // tr), in_specs=[bs, bs], out_specs=bs, out_shape=SDS((Q, R, C), BF16),
        compiler_params=_cparams(("parallel", "parallel")), name=name)(a, b)


_IN_SIZES = (512, 128, 128, 512, 512, 512, 8, 512, 512, 512, 3072)
_IN_OFF = tuple(int(v) for v in np.cumsum((0,) + _IN_SIZES))
_IN_Q = N_IN_COLS // 4


def _pack_w_in(w):
    def cols(lo, hi):
        out = []
        while lo < hi:
            q, off = divmod(lo, _IN_Q)
            n = min(hi - lo, _IN_Q - off)
            out.append(w[q, :, off:off + n])
            lo += n
        return out

    fb0, fb1, g0 = _IN_OFF[6], _IN_OFF[7], _IN_OFF[10]
    wqkv = jnp.concatenate(cols(0, fb0) + cols(fb1, g0), axis=1)
    wgf = jnp.concatenate(cols(g0, N_IN_COLS) + cols(fb0, fb1) + [jnp.zeros((w.shape[1], LANE - 8), w.dtype)], axis=1)
    return wqkv, wgf


def _unpack_w_in(dqkv, dgf):
    fb0, fb1, g0 = _IN_OFF[6], _IN_OFF[7], _IN_OFF[10]

    def cols(lo, hi):
        out = []
        while lo < hi:
            if lo < fb0:
                n = min(hi, fb0) - lo
                out.append(dqkv[:, lo:lo + n])
            elif lo < fb1:
                n = min(hi, fb1) - lo
                out.append(dgf[:, 3072 + lo - fb0:3072 + lo - fb0 + n])
            elif lo < g0:
                n = min(hi, g0) - lo
                out.append(dqkv[:, lo - 8:lo - 8 + n])
            else:
                n = hi - lo
                out.append(dgf[:, lo - g0:lo - g0 + n])
            lo += n
        return out

    return jnp.stack([jnp.concatenate(cols(q * _IN_Q, (q + 1) * _IN_Q), axis=1) for q in range(4)])


def _pad_rows(a, rows):
    return jnp.pad(a, ((0, rows - a.shape[0]), (0, 0)))


def _small_pack(parts):
    flat = jnp.concatenate([p.reshape(-1) for p in parts])
    n = flat.shape[0]
    rows = -(-n // LANE)
    rows = -(-rows // 8) * 8
    return jnp.pad(flat, (0, rows * LANE - n)).reshape(rows, LANE)


def _small_unpack(block, shapes):
    flat = block.reshape(-1)
    out, off = [], 0
    for s in shapes:
        n = int(np.prod(s))
        out.append(flat[off:off + n].reshape(s))
        off += n
    return out


def _kv_same(g):
    return 0


def _kv_own(g):
    return g


_mm_plain = _mm


def _mm_hosting(a, b, *, comm, **kw):
    if comm is None:
        return _mm(a, b, **kw), None
    return _mm(a, b, comm=comm, **kw)


def _layer_fwd(x, mod, p, l, ride):
    sh_m, sc_m, g_m, sh_f, sc_f, g_f = mod
    nm = "l%d_" % l

    def carried(name, run):
        res, got = run(ride.comm_for(name))
        if got is not None:
            ride.done(name, got)
        return res

    h1 = _norm_mod_fwd(x, p["norm_mix_g"], sc_m, sh_m, nm + "norm_mix_fwd")
    qkv = carried("proj_qkv", lambda cm: _mm_hosting(h1, p["wqkv"], mode="nn", out_dtype=BF16,
                                                     name=nm + "proj_qkv", comm=cm))
    gf = _mm(h1, p["wgf"], mode="nn", out_dtype=F32, name=nm + "proj_gf", cap_n=640)
    qkv_t = qkv.T
    o_a_t = carried("attn_a", lambda cm: _bandT_fwd(
        (qkv_t, 0), _heads(qkv[:, 512:640], A_KV_HEADS), (qkv_t, 640), p["alibi"], p["sink_tab"],
        GQ=4, GK=1, P=A_PREV, kvoff=_kv_same, name=nm + "attn_a_fwd", comm=cm))
    cum = _fox_cum(gf, p["b_forget_pad"], nm + "fox_cum")
    cum_t = cum[:, :N_HEADS].T
    cc, cr = cum_t[:, :, None], cum_t[:, None, :]
    o_b_t, lse_b = carried("attn_b", lambda cm: _foxT_fwd(
        (qkv_t, 768), _heads(qkv[:, 1280:1792], N_HEADS), (qkv_t, 1792), cc, cr, nm + "attn_b_fwd", comm=cm))
    o_c_t = carried("attn_c", lambda cm: _bandT_fwd(
        (qkv_t, 2304), _heads(qkv[:, 2816:3328], N_HEADS), (qkv_t, 3328), p["rel_tab"], p["no_sink"],
        GQ=2, GK=2, P=C_PREV, kvoff=_kv_own, name=nm + "attn_c_fwd", comm=cm))
    p = dict(p, **ride.late_weights())
    o = jnp.concatenate([o_a_t, o_b_t, o_c_t], axis=0).T
    y = _mm(o, p["wb"], mode="nn", out_dtype=BF16, groups=3, name=nm + "branch")
    merged = _merge_fwd(y, gf, nm + "merge_fwd")
    mix = _mm(merged, p["wout"], mode="nn", out_dtype=F32, name=nm + "out_proj")
    x1 = _resid_fwd(x, mix, g_m, nm + "resid_mix")
    h2 = _norm_mod_fwd(x1, p["norm_ffn_g"], sc_f, sh_f, nm + "norm_ffn_fwd")
    u = carried("ffn_in", lambda cm: _mm_hosting(h2, p["wfi"], mode="nn", out_dtype=BF16, name=nm + "ffn_in",
                                                 cap_n=512, comm=cm))
    a = _swiglu_fwd(u, nm + "swiglu_fwd")
    f = _mm(a, p["wfo"], mode="nn", out_dtype=F32, name=nm + "ffn_out", cap_m=1024)
    x2 = _resid_fwd(x1, f, g_f, nm + "resid_ffn")
    saved = dict(x=x, h1=h1, qkv=qkv, qkv_t=qkv_t, gf=gf, cc=cc, cr=cr, o_b_t=o_b_t, lse_b=lse_b, o=o, y=y, merged=merged,
                 mix=mix, x1=x1, h2=h2, u=u, a=a, f=f)
    return x2, saved, p


def _layer_bwd(dx2, mod, p, s, l, ride=None):
    sh_m, sc_m, g_m, sh_f, sc_f, g_f = mod
    nm = "l%d_" % l

    def _mm(a, b, *, name, **kw):
        comm = ride.comm_for(name) if ride is not None else None
        if comm is None:
            return _mm_plain(a, b, name=nm + name, **kw)
        out, got = _mm_plain(a, b, name=nm + name, comm=comm, **kw)
        ride.done(name, got)
        return out

    dg_f, df = _resid_bwd(dx2, s["f"], g_f, nm + "resid_ffn_bwd")
    da = _mm(df, p["wfo"], mode="nt", out_dtype=BF16, name="ffn_out_dx", cap_m=1024, cap_n=1408)
    d_wfo = _mm(s["a"], df, mode="tn", out_dtype=BF16, name="ffn_out_dw", cap_m=1408, cap_k=2048)
    du = _swiglu_bwd(da, s["u"], nm + "swiglu_bwd")
    dh2 = _mm(du, p["wfi"], mode="nt", out_dtype=F32, name="ffn_in_dx", cap_m=1024)
    d_wfi = _mm(s["h2"], du, mode="tn", out_dtype=BF16, name="ffn_in_dw", cap_m=1024, cap_n=1408, cap_k=2048,
                col_quarters=True)
    dx1, dsc_f, dsh_f, dgn_f = _norm_mod_bwd(s["x1"], [dh2], dx2, p["norm_ffn_g"], sc_f, nm + "norm_ffn_bwd")
    dg_m, dmix = _resid_bwd(dx1, s["mix"], g_m, nm + "resid_mix_bwd")
    dmerged = _mm(dmix, p["wout"], mode="nt", out_dtype=F32, name="out_proj_dx")
    d_wout = _mm(s["merged"], dmix, mode="tn", out_dtype=BF16, name="out_proj_dw", cap_m=1024, cap_k=2048)
    dy, dgates = _merge_bwd(dmerged, s["y"], s["gf"], nm + "merge_bwd")
    do = _mm(dy, p["wb"], mode="nt", out_dtype=BF16, groups=3, name="branch_dx")
    d_wb = _mm(s["o"], dy, mode="tn", out_dtype=BF16, groups=3, name="branch_dw", cap_k=2048,
               col_quarters=True)
    comms = ride.exchanges() if ride is not None else (None, None, None)
    qkv, qkv_t = s["qkv"], s["qkv_t"]
    do_t = do.T
    (dqa_t, dka_h, dva_h, _, dsink), got_a = _bandT_bwd(
        (qkv_t, 0), _heads(qkv[:, 0:512], N_HEADS), _heads(qkv[:, 512:640], A_KV_HEADS), (qkv_t, 512),
        _heads(qkv[:, 640:768], A_KV_HEADS), (do_t, 0), _heads(do[:, 0:512], N_HEADS), p["alibi"], p["sink_tab"],
        GQ=4, GK=1, P=A_PREV, kvoff=_kv_same, name=nm + "attn_a_bwd", comm=comms[0])
    (dqb_t, dkb_h, dvb_h, dck, dcq), got_b = _foxT_bwd(
        (qkv_t, 768), _heads(qkv[:, 768:1280], N_HEADS), _heads(qkv[:, 1280:1792], N_HEADS), (qkv_t, 1280),
        _heads(qkv[:, 1792:2304], N_HEADS), s["cc"], s["cr"], s["o_b_t"], (do_t, 512),
        _heads(do[:, 512:1024], N_HEADS), s["lse_b"], nm + "attn_b_bwd", comm=comms[1])
    dcum = jnp.pad((dck[:, :, 0] + dcq[:, 0, :]).T, ((0, 0), (0, LANE - N_HEADS)))
    dfb, db_forget = _fox_cum_bwd(s["gf"], p["b_forget_pad"], dcum, nm + "fox_cum_bwd")
    (dqc_t, dkc_h, dvc_h, dbias_c, _), got_c = _bandT_bwd(
        (qkv_t, 2304), _heads(qkv[:, 2304:2816], N_HEADS), _heads(qkv[:, 2816:3328], N_HEADS), (qkv_t, 2816),
        _heads(qkv[:, 3328:3840], N_HEADS), (do_t, 1024), _heads(do[:, 1024:1536], N_HEADS), p["rel_tab"],
        p["no_sink"], GQ=2, GK=2, P=C_PREV, kvoff=_kv_own, name=nm + "attn_c_bwd", comm=comms[2])
    d_rel = _rel_reduce(jnp.transpose(_unpair_table(dbias_c), (1, 0, 2)), nm + "rel_reduce")[:, :N_REL]
    dqkv = jnp.concatenate([dqa_t.T, _unheads(dka_h), _unheads(dva_h), dqb_t.T, _unheads(dkb_h), _unheads(dvb_h),
                            dqc_t.T, _unheads(dkc_h), _unheads(dvc_h)], axis=1)
    dgf = jnp.concatenate([dgates, dfb], axis=1)
    if ride is not None:
        ride.exchanged((got_a, got_b, got_c))
    dh1a = _mm(dqkv, p["wqkv"], mode="nt", out_dtype=F32, name="proj_qkv_dx", cap_k=1024)
    dh1b = _mm(dgf, p["wgf"], mode="nt", out_dtype=F32, name="proj_gf_dx", cap_k=640)
    d_wqkv = _mm(s["h1"], dqkv, mode="tn", out_dtype=BF16, name="proj_qkv_dw", cap_m=1024, cap_k=2048)
    d_wgf = _mm(s["h1"], dgf, mode="tn", out_dtype=BF16, name="proj_gf_dw", cap_m=1024, cap_n=640, cap_k=2048)
    dx, dsc_m, dsh_m, dgn_m = _norm_mod_bwd(s["x"], [dh1a, dh1b], dx1, p["norm_mix_g"], sc_m, nm + "norm_mix_bwd")
    d_mod = jnp.concatenate([dsh_m, dsc_m, dg_m, dsh_f, dsc_f, dg_f], axis=1)[0]
    grads = dict(w_in=_unpack_w_in(d_wqkv, d_wgf), w_branch=d_wb, w_out=d_wout.reshape(4, -1, D_MODEL),
                 w_ffn_in=d_wfi, w_ffn_out=d_wfo.reshape(4, -1, D_MODEL),
                 norm_mix_g=dgn_m[0], norm_ffn_g=dgn_f[0], b_forget=db_forget[0, :N_HEADS],
                 sinks=dsink[:, 0, 0], rel_bias=d_rel, d_mod=d_mod)
    return dx, grads


def kernel(x, c, norm_mix_g, norm_ffn_g, w_ada, b_ada, w_in, b_forget, sinks, rel_bias, w_branch, w_out, w_ffn_in, w_ffn_out, final_norm_g, loss_target, m_norm_mix_g, m_norm_ffn_g, m_w_ada, m_b_ada, m_w_in, m_b_forget, m_sinks, m_rel_bias, m_w_branch, m_w_out, m_w_ffn_in, m_w_ffn_out, m_final_norm_g, v_norm_mix_g, v_norm_ffn_g, v_w_ada, v_b_ada, v_w_in, v_b_forget, v_sinks, v_rel_bias, v_w_branch, v_w_out, v_w_ffn_in, v_w_ffn_out, v_final_norm_g):
    xi, yi, ci = _coords()
    chip = 2 * xi + yi
    dev = 2 * chip + ci
    xs = x[0]
    S = xs.shape[0]
    n_ada = w_ada.shape[2]

    big_names = ("w_in", "w_branch", "w_out", "w_ffn_in", "w_ffn_out")
    big_w = dict(w_in=w_in, w_branch=w_branch, w_out=w_out, w_ffn_in=w_ffn_in, w_ffn_out=w_ffn_out)
    big_m = dict(w_in=m_w_in, w_branch=m_w_branch, w_out=m_w_out, w_ffn_in=m_w_ffn_in, w_ffn_out=m_w_ffn_out)
    big_v = dict(w_in=v_w_in, w_branch=v_w_branch, w_out=v_w_out, w_ffn_in=v_w_ffn_in, w_ffn_out=v_w_ffn_out)
    flat2 = lambda a: a.reshape(-1, a.shape[-1])
    shards = [[flat2(big_w[n][l]).astype(BF16) for n in big_names] for l in range(DEPTH)]
    gw = [[None] * (len(big_names) + 2) for _ in range(DEPTH)]
    for l in range(DEPTH):
        shards[l] += [shards[l][0][:D_MODEL // 2], shards[l][0][D_MODEL // 2:]]
    gw[0][0] = _RowHalfGather([shards[0][0]]).run("weights_gather_w_in_l0")[0]
    host_g = ((1, 2, 4), (0,), (3,))

    class WeightRide:
        def __init__(self, l, plan):
            self.l, self.plan = l, plan

        def comm_for(self, name):
            if name not in self.plan:
                return None
            lay, idx = self.plan[name]
            return _RowHalfGather([shards[lay][i] for i in idx])

        def done(self, name, got):
            lay, idx = self.plan[name]
            for i, r in zip(idx, got):
                gw[lay][i] = r

        def late_weights(self):
            g = gw[self.l]
            return dict(wb=jnp.transpose(g[1], (1, 0, 2)).reshape(3 * BRANCH_W, D_MODEL),
                        wout=g[2].reshape(D_MODEL, D_MODEL),
                        wfi=jnp.transpose(g[3], (1, 0, 2)).reshape(D_MODEL, 2 * FFN_H),
                        wfo=g[4].reshape(FFN_H, D_MODEL))

    weight_plan = [
        {"proj_qkv": (0, (1, 2)), "attn_a": (0, (4,)), "attn_b": (0, (3,)), "attn_c": (1, (5,)), "ffn_in": (1, (6,))},
        {"attn_a": (1, (1, 2)), "attn_b": (1, (3,)), "attn_c": (1, (4,))}]


    c_all = _all_gather8(c.reshape(8, LANE), "gather_c").reshape(8, D_MODEL)
    b_sh = lax.dynamic_slice_in_dim(b_ada, chip * n_ada, n_ada, axis=1)[:, None, :]
    mod_sh = _ada_fwd(_pad_rows(c_all, 16), w_ada, b_sh, "ada_fwd")[:, :8, :]
    mod_all = _all_gather8(mod_sh.reshape(-1, LANE), "gather_mod").reshape(8, DEPTH, 8, n_ada)
    mod_mine = lax.dynamic_index_in_dim(mod_all[0::2], dev, axis=2, keepdims=False)
    mod = mod_mine.transpose(1, 0, 2).reshape(DEPTH, 6, D_MODEL)

    alibi = _pair_table(_alibi_table())
    no_sink = jnp.full((N_HEADS, 8, LANE), NEG_INF, F32)
    def make_params(l):
        if gw[l][0] is None:
            gw[l][0] = jnp.concatenate([gw[l][5], gw[l][6]], axis=1)
        wqkv, wgf = _pack_w_in(gw[l][0])
        rel_tab = _rel_expand(jnp.pad(rel_bias[l], ((0, 0), (0, N_REL_PAD - N_REL))), "l%d_rel_expand" % l)
        return dict(
            wqkv=wqkv, wgf=wgf, norm_mix_g=norm_mix_g[l][None], norm_ffn_g=norm_ffn_g[l][None],
            b_forget_pad=jnp.pad(b_forget[l], (0, LANE - N_HEADS))[None],
            sink_tab=jnp.broadcast_to(sinks[l][:, None, None], (N_HEADS, 8, LANE)),
            no_sink=no_sink, alibi=alibi, rel_tab=_pair_table(jnp.transpose(rel_tab, (1, 0, 2))))

    mods = [[mod[l, k][None] for k in range(6)] for l in range(DEPTH)]
    params, saved = [None] * DEPTH, [None] * DEPTH
    h = xs
    for l in range(DEPTH):
        h, saved[l], params[l] = _layer_fwd(h, mods[l], make_params(l), l, WeightRide(l, weight_plan[l]))
    loss_dev, dh, d_final = _final_loss(h, final_norm_g[None], loss_target[0], "final_loss")
    grads = [None] * DEPTH
    dh, grads[1] = _layer_bwd(dh, mods[1], params[1], saved[1], 1)

    class Layer1Ride:
        sends = {"ffn_out_dx": (4,), "ffn_in_dx": (3, 1, 2), "ffn_in_dw": (0,)}
        hands = {"proj_qkv_dx": (0,), "proj_gf_dx": (3,), "proj_gf_dw": (4, 1, 2)}

        def __init__(self, g):
            self.g, self.t = g, [None] * len(g)
            self.parts, self.final = [None] * len(g), [None] * len(g)

        def comm_for(self, name):
            if name in self.sends:
                return _SiblingSend([self.g[i] for i in self.sends[name]], 0)
            if name in self.hands:
                return _Handoff([self.parts[i] for i in self.hands[name]], 1, (0, 1, 2, 3))
            return None

        def done(self, name, got):
            idx, dst = (self.sends[name], self.t) if name in self.sends else (self.hands[name], self.final)
            for i, r in zip(idx, got):
                dst[i] = r

        def exchanges(self):
            sums = [_add_cast_on(a, b, 1, "grads_chip_sum_l1_" + n) for n, a, b in zip(big_names, self.g, self.t)]
            return tuple(_OwnerReduce([sums[i] for i in idx], 1) for idx in host_g)

        def exchanged(self, got):
            for res, idx in zip(got, host_g):
                for r, i in zip(res, idx):
                    self.parts[i] = r

    ride = Layer1Ride([grads[1][n] for n in big_names])
    dh, grads[0] = _layer_bwd(dh, mods[0], params[0], saved[0], 0, ride)
    grad_x = dh[None]
    loss = lax.psum(loss_dev[0, 0], ("x", "y", "c"))
    parts1 = ride.final
    g0 = [grads[0][n] for n in big_names]
    t0 = _sibling_swap_rows(g0, "grads_swap_l0")
    sums0 = [_add_cast_rows(a, b, "grads_chip_sum_l0_" + n) for n, a, b in zip(big_names, g0, t0)]
    parts0 = [None] + list(_RowHalfReduce(sums0[1:]).run("grads_reduce_l0"))

    small_names = ("norm_mix_g", "norm_ffn_g", "b_ada", "b_forget", "sinks", "rel_bias", "final_norm_g")
    small_w = dict(norm_mix_g=norm_mix_g, norm_ffn_g=norm_ffn_g, b_ada=b_ada, b_forget=b_forget, sinks=sinks,
                   rel_bias=rel_bias, final_norm_g=final_norm_g)
    small_m = dict(norm_mix_g=m_norm_mix_g, norm_ffn_g=m_norm_ffn_g, b_ada=m_b_ada, b_forget=m_b_forget,
                   sinks=m_sinks, rel_bias=m_rel_bias, final_norm_g=m_final_norm_g)
    small_v = dict(norm_mix_g=v_norm_mix_g, norm_ffn_g=v_norm_ffn_g, b_ada=v_b_ada, b_forget=v_b_forget,
                   sinks=v_sinks, rel_bias=v_rel_bias, final_norm_g=v_final_norm_g)
    small_g = dict(
        norm_mix_g=jnp.stack([grads[l]["norm_mix_g"] for l in range(DEPTH)]),
        norm_ffn_g=jnp.stack([grads[l]["norm_ffn_g"] for l in range(DEPTH)]),
        b_ada=jnp.stack([grads[l]["d_mod"] for l in range(DEPTH)]),
        b_forget=jnp.stack([grads[l]["b_forget"] for l in range(DEPTH)]),
        sinks=jnp.stack([grads[l]["sinks"] for l in range(DEPTH)]),
        rel_bias=jnp.stack([grads[l]["rel_bias"] for l in range(DEPTH)]),
        final_norm_g=d_final[0])
    shapes = [small_w[n].shape for n in small_names]
    g_all = _all_gather8(_small_pack([small_g[n] for n in small_names]), "gather_small_grads")
    res = _adamw(_small_pack([small_w[n] for n in small_names])[None],
                    _small_pack([small_m[n] for n in small_names])[None],
                    _small_pack([small_v[n] for n in small_names])[None], g_all, "adamw_small")
    small_out = {n: [] for n in small_names}
    for r in res:
        for n, a in zip(small_names, _small_unpack(r[0], shapes)):
            small_out[n].append(a)
    off_b = sum(int(np.prod(s)) for s in shapes[:2])
    n_mod = DEPTH * 6 * D_MODEL
    dmod_all = g_all.reshape(8, -1)[:, off_b:off_b + n_mod].reshape(8, DEPTH, 6 * D_MODEL)
    dmod_sh = lax.dynamic_slice_in_dim(dmod_all, chip * n_ada, n_ada, axis=2).transpose(1, 0, 2)
    g_ada, got = _ada_bwd(c_all.T, dmod_sh, "ada_bwd", comm=_RowHalfReduce(sums0[:1]))
    parts0[0] = got[0]
    ada_out = _adamw(w_ada, m_w_ada, v_w_ada, flat2(g_ada)[None], "adamw_w_ada")

    big_out = {}
    as3 = lambda a: a.reshape(a.shape[0], -1, a.shape[-1])
    for n, p0, p1 in zip(big_names, parts0, parts1):
        res = _adamw(as3(big_w[n]), as3(big_m[n]), as3(big_v[n]), [p0, p1], "adamw_" + n)
        big_out[n] = [r.reshape(big_w[n].shape) for r in res]

    order = ("norm_mix_g", "norm_ffn_g", "w_ada", "b_ada", "w_in", "b_forget", "sinks", "rel_bias", "w_branch",
             "w_out", "w_ffn_in", "w_ffn_out", "final_norm_g")

    def pick(n, k):
        if n == "w_ada":
            return ada_out[k]
        if n in big_out:
            return big_out[n][k]
        return small_out[n][k]

    outs = [loss, grad_x]
    for k in range(4):
        outs += [pick(n, k) for n in order]
    return tuple(outs)
```

```python
import numpy as np
import jax
import jax.numpy as jnp
from jax import lax
from jax.experimental import pallas as pl
from jax.experimental.pallas import tpu as pltpu

F32 = jnp.float32
BF16 = jnp.bfloat16
SDS = jax.ShapeDtypeStruct

D_MODEL = 1024
DEPTH = 2
CHUNK = 64
HEAD_DIM = 64
EPS = 1e-6
NEG_INF = -1e30
N_HEADS = 8
A_KV_HEADS = 2
A_PREV = 2
C_PREV = 8
REL_CLIP = 128
N_REL = 2 * REL_CLIP + 1
N_REL_PAD = 384
BRANCH_W = 512
FFN_H = 2816
FOX_BQ = 512
FOX_BK = 512
GF_COLS = 3200
N_IN_COLS = 6920
LANE = 128
VMEM_LIMIT = 48 * 1024 * 1024

ADAM_LR = 0.001
ADAM_B1 = 0.9
ADAM_B2 = 0.999
ADAM_EPS = 1e-08
ADAM_WD = 0.01
ADAM_STEP = 10

MESH = pl.DeviceIdType.MESH
ANY = pl.BlockSpec(memory_space=pl.ANY)
VMEM_SPEC = pl.BlockSpec(memory_space=pltpu.VMEM)


def _cparams(sem=None):
    return pltpu.CompilerParams(dimension_semantics=sem, vmem_limit_bytes=VMEM_LIMIT)


def _blk(n, cap):
    if n <= cap:
        return n
    best = None
    for m in range(LANE, cap + 1, LANE):
        if n % m == 0:
            best = m
    assert best is not None, (n, cap)
    return best


def _sigmoid(x):
    return 1.0 / (1.0 + jnp.exp(-x))


def _mm(a, b, *, mode, out_dtype, name, groups=1, cap_m=2048, cap_n=1024, cap_k=1408, col_quarters=False,
        comm=None):
    G = groups
    assert not col_quarters or mode == "tn"
    if mode == "nn":
        M, K, N = a.shape[0], a.shape[1] // G, b.shape[1]
        assert b.shape[0] == G * K
    elif mode == "nt":
        M, K, N = a.shape[0], a.shape[1] // G, b.shape[0] // G
        assert b.shape[1] == K
    else:
        K, M, N = a.shape[0], a.shape[1] // G, b.shape[1] // G
        assert b.shape[0] == K
    bm, bn, bk = _blk(M, cap_m), _blk(N // 4 if col_quarters else N, cap_n), _blk(K, cap_k)
    nm, nn, nk = M // bm, N // bn, K // bk
    if mode == "nn":
        a_spec = pl.BlockSpec((bm, bk), lambda g, i, j, k: (i, g * nk + k))
        b_spec = pl.BlockSpec((bk, bn), lambda g, i, j, k: (g * nk + k, j))
        o_spec = pl.BlockSpec((bm, bn), lambda g, i, j, k: (i, g * nn + j))
        dims = (((1,), (0,)), ((), ()))
        out_shape = (M, G * N)
    elif mode == "nt":
        a_spec = pl.BlockSpec((bm, bk), lambda g, i, j, k: (i, g * nk + k))
        b_spec = pl.BlockSpec((bn, bk), lambda g, i, j, k: (g * nn + j, k))
        o_spec = pl.BlockSpec((bm, bn), lambda g, i, j, k: (i, g * nn + j))
        dims = (((1,), (1,)), ((), ()))
        out_shape = (M, G * N)
    else:
        a_spec = pl.BlockSpec((bk, bm), lambda g, i, j, k: (k, g * nm + i))
        b_spec = pl.BlockSpec((bk, bn), lambda g, i, j, k: (k, g * nn + j))
        dims = (((0,), (0,)), ((), ()))
        if col_quarters:
            nq = nn // 4
            o_spec = pl.BlockSpec((1, bm, bn), lambda g, i, j, k: (j // nq, g * nm + i, j % nq))
            out_shape = (4, G * M, N // 4)
        else:
            o_spec = pl.BlockSpec((bm, bn), lambda g, i, j, k: (g * nm + i, j))
            out_shape = (G * M, N)

    def product(a_ref, b_ref):
        return lax.dot_general(a_ref[...].astype(BF16), b_ref[...].astype(BF16), dims, preferred_element_type=F32)

    def body_one(a_ref, b_ref, o_ref):
        o_ref[...] = product(a_ref, b_ref).astype(o_ref.dtype).reshape(o_ref.shape)

    def body_acc(a_ref, b_ref, o_ref, acc_ref):
        k = pl.program_id(3)

        @pl.when(k == 0)
        def _():
            acc_ref[...] = jnp.zeros_like(acc_ref)

        acc_ref[...] += product(a_ref, b_ref)

        @pl.when(k == nk - 1)
        def _():
            o_ref[...] = acc_ref[...].astype(o_ref.dtype).reshape(o_ref.shape)

    res, got = _call_hosting(
        body_one if nk == 1 else body_acc, comm=comm, grid=(G, nm, nn, nk), in_specs=[a_spec, b_spec],
        out_specs=[o_spec], out_shape=[SDS(out_shape, out_dtype)],
        scratch_shapes=[] if nk == 1 else [pltpu.VMEM((bm, bn), F32)], name=name, args=(a, b),
        semantics=("parallel", "parallel", "parallel", "arbitrary"))
    return res[0] if comm is None else (res[0], got)


def _rows(tm, n, col=0):
    return pl.BlockSpec((tm, n), lambda i: (i, col))


def _vec(n):
    return pl.BlockSpec((1, n), lambda i: (0, 0))


def _tm(S):
    return min(S, 256)


def _norm_mod_fwd(x, g, sc, sh, name):
    S, Dm = x.shape
    tm = _tm(S)

    def body(x_ref, g_ref, sc_ref, sh_ref, h_ref):
        xv = x_ref[...]
        r = lax.rsqrt(jnp.mean(xv * xv, axis=-1, keepdims=True) + EPS)
        h_ref[...] = ((xv * r) * g_ref[...] * (1.0 + sc_ref[...]) + sh_ref[...]).astype(h_ref.dtype)

    return pl.pallas_call(
        body, grid=(S // tm,), in_specs=[_rows(tm, Dm), _vec(Dm), _vec(Dm), _vec(Dm)],
        out_specs=_rows(tm, Dm), out_shape=SDS((S, Dm), BF16),
        compiler_params=_cparams(("parallel",)), name=name)(x, g, sc, sh)


def _norm_mod_bwd(x, dh_list, dres, g, sc, name):
    S, Dm = x.shape
    tm = _tm(S)
    nh = len(dh_list)

    def body(*refs):
        x_ref = refs[0]
        dh_refs = refs[1:1 + nh]
        dres_ref, g_ref, sc_ref, dx_ref, dsc_ref, dsh_ref, dg_ref = refs[1 + nh:]
        i = pl.program_id(0)

        @pl.when(i == 0)
        def _():
            dsc_ref[...] = jnp.zeros_like(dsc_ref)
            dsh_ref[...] = jnp.zeros_like(dsh_ref)
            dg_ref[...] = jnp.zeros_like(dg_ref)

        xv = x_ref[...]
        dh = dh_refs[0][...]
        for r_ in dh_refs[1:]:
            dh = dh + r_[...]
        gv = g_ref[...]
        r = lax.rsqrt(jnp.mean(xv * xv, axis=-1, keepdims=True) + EPS)
        xn = xv * r
        xg = xn * gv
        dsh_ref[...] += jnp.sum(dh, axis=0, keepdims=True)
        dsc_ref[...] += jnp.sum(dh * xg, axis=0, keepdims=True)
        dxg = dh * (1.0 + sc_ref[...])
        dg_ref[...] += jnp.sum(dxg * xn, axis=0, keepdims=True)
        dxn = dxg * gv
        dx_ref[...] = dres_ref[...] + r * (dxn - xn * jnp.mean(dxn * xn, axis=-1, keepdims=True))

    return pl.pallas_call(
        body, grid=(S // tm,),
        in_specs=[_rows(tm, Dm)] * (2 + nh) + [_vec(Dm), _vec(Dm)],
        out_specs=[_rows(tm, Dm), _vec(Dm), _vec(Dm), _vec(Dm)],
        out_shape=[SDS((S, Dm), F32), SDS((1, Dm), F32), SDS((1, Dm), F32), SDS((1, Dm), F32)],
        compiler_params=_cparams(("arbitrary",)), name=name)(x, *dh_list, dres, g, sc)


def _resid_fwd(x, val, g, name):
    S, Dm = x.shape
    tm = _tm(S)

    def body(x_ref, v_ref, g_ref, o_ref):
        o_ref[...] = x_ref[...] + g_ref[...] * v_ref[...]

    return pl.pallas_call(
        body, grid=(S // tm,), in_specs=[_rows(tm, Dm), _rows(tm, Dm), _vec(Dm)],
        out_specs=_rows(tm, Dm), out_shape=SDS((S, Dm), F32),
        compiler_params=_cparams(("parallel",)), name=name)(x, val, g)


def _resid_bwd(dx, val, g, name):
    S, Dm = dx.shape
    tm = _tm(S)

    def body(dx_ref, v_ref, g_ref, dg_ref, dv_ref):
        @pl.when(pl.program_id(0) == 0)
        def _():
            dg_ref[...] = jnp.zeros_like(dg_ref)

        dxv = dx_ref[...]
        dg_ref[...] += jnp.sum(dxv * v_ref[...], axis=0, keepdims=True)
        dv_ref[...] = (dxv * g_ref[...]).astype(dv_ref.dtype)

    return pl.pallas_call(
        body, grid=(S // tm,), in_specs=[_rows(tm, Dm), _rows(tm, Dm), _vec(Dm)],
        out_specs=[_vec(Dm), _rows(tm, Dm)], out_shape=[SDS((1, Dm), F32), SDS((S, Dm), BF16)],
        compiler_params=_cparams(("arbitrary",)), name=name)(dx, val, g)


def _merge_fwd(y, gf, name):
    S = y.shape[0]
    tm = _tm(S)
    W = 3 * D_MODEL

    def body(y_ref, g_ref, o_ref):
        acc = None
        for k in range(3):
            sl = slice(k * D_MODEL, (k + 1) * D_MODEL)
            t = _sigmoid(g_ref[:, sl]) * y_ref[:, sl].astype(F32)
            acc = t if acc is None else acc + t
        o_ref[...] = acc.astype(o_ref.dtype)

    return pl.pallas_call(
        body, grid=(S // tm,), in_specs=[_rows(tm, W), _rows(tm, W)],
        out_specs=_rows(tm, D_MODEL), out_shape=SDS((S, D_MODEL), BF16),
        compiler_params=_cparams(("parallel",)), name=name)(y, gf)


def _merge_bwd(dm, y, gf, name):
    S = y.shape[0]
    tm = _tm(S)
    W = 3 * D_MODEL

    def body(dm_ref, y_ref, g_ref, dy_ref, dg_ref):
        dmv = dm_ref[...]
        for k in range(3):
            sl = slice(k * D_MODEL, (k + 1) * D_MODEL)
            sg = _sigmoid(g_ref[:, sl])
            dy_ref[:, sl] = (dmv * sg).astype(dy_ref.dtype)
            dg_ref[:, sl] = (dmv * y_ref[:, sl].astype(F32) * (sg * (1.0 - sg))).astype(dg_ref.dtype)

    return pl.pallas_call(
        body, grid=(S // tm,), in_specs=[_rows(tm, D_MODEL), _rows(tm, W), _rows(tm, W)],
        out_specs=[_rows(tm, W), _rows(tm, W)], out_shape=[SDS((S, W), BF16), SDS((S, W), BF16)],
        compiler_params=_cparams(("parallel",)), name=name)(dm, y, gf)


def _swiglu_fwd(u, name):
    S = u.shape[0]
    tm = _tm(S)

    def body(g_ref, u_ref, a_ref):
        gv = g_ref[...].astype(F32)
        a_ref[...] = (gv * _sigmoid(gv) * u_ref[...].astype(F32)).astype(a_ref.dtype)

    return pl.pallas_call(
        body, grid=(S // tm,), in_specs=[_rows(tm, FFN_H, 0), _rows(tm, FFN_H, 1)],
        out_specs=_rows(tm, FFN_H), out_shape=SDS((S, FFN_H), BF16),
        compiler_params=_cparams(("parallel",)), name=name)(u, u)


def _swiglu_bwd(da, u, name):
    S = u.shape[0]
    tm = _tm(S)

    def body(da_ref, g_ref, u_ref, du_ref):
        dav = da_ref[...].astype(F32)
        gv = g_ref[...].astype(F32)
        sg = _sigmoid(gv)
        du_ref[:, 0:FFN_H] = (dav * u_ref[...].astype(F32) * (sg * (1.0 + gv * (1.0 - sg)))).astype(du_ref.dtype)
        du_ref[:, FFN_H:2 * FFN_H] = (dav * (gv * sg)).astype(du_ref.dtype)

    return pl.pallas_call(
        body, grid=(S // tm,), in_specs=[_rows(tm, FFN_H), _rows(tm, FFN_H, 0), _rows(tm, FFN_H, 1)],
        out_specs=_rows(tm, 2 * FFN_H), out_shape=SDS((S, 2 * FFN_H), BF16),
        compiler_params=_cparams(("parallel",)), name=name)(da, u, u)


def _final_loss(x, g, target, name):
    S, Dm = x.shape
    tm = _tm(S)

    def body(x_ref, g_ref, t_ref, loss_ref, dx_ref, dg_ref):
        @pl.when(pl.program_id(0) == 0)
        def _():
            loss_ref[...] = jnp.zeros_like(loss_ref)
            dg_ref[...] = jnp.zeros_like(dg_ref)

        xv = x_ref[...]
        gv = g_ref[...]
        r = lax.rsqrt(jnp.mean(xv * xv, axis=-1, keepdims=True) + EPS)
        xn = xv * r
        err = xn * gv - t_ref[...]
        row = jnp.mean(err * err, axis=-1, keepdims=True)
        loss_ref[...] += 0.5 * jnp.sum(row, axis=0, keepdims=True)
        dy = err * (1.0 / Dm)
        dg_ref[...] += jnp.sum(dy * xn, axis=0, keepdims=True)
        dxn = dy * gv
        dx_ref[...] = r * (dxn - xn * jnp.mean(dxn * xn, axis=-1, keepdims=True))

    return pl.pallas_call(
        body, grid=(S // tm,), in_specs=[_rows(tm, Dm), _vec(Dm), _rows(tm, Dm)],
        out_specs=[pl.BlockSpec((1, 1), lambda i: (0, 0)), _rows(tm, Dm), _vec(Dm)],
        out_shape=[SDS((1, 1), F32), SDS((S, Dm), F32), SDS((1, Dm), F32)],
        compiler_params=_cparams(("arbitrary",)), name=name)(x, g, target)


PAIR = 2 * CHUNK


def _bandT_softmax(kg, qTg, bias, sink, valid):
    s = jnp.dot(kg, qTg, preferred_element_type=F32)
    s = jnp.where(valid, s + bias, NEG_INF)
    m = jnp.maximum(jnp.max(s, axis=0, keepdims=True), sink)
    e = jnp.exp(s - m)
    es = jnp.exp(sink - m)
    inv = 1.0 / (jnp.sum(e, axis=0, keepdims=True) + es)
    return e * inv, es * inv


def _pad_copy_rows(dst, src, pad, S):
    dst[:, 0:pad, :] = jnp.zeros((dst.shape[0], pad, dst.shape[2]), dst.dtype)
    dst[:, pad:pad + S, :] = src[...]


def _pad_copy_lanes(dst, src, pad, S):
    dst[:, 0:pad] = jnp.zeros((dst.shape[0], pad), dst.dtype)
    dst[:, pad:pad + S] = src[...]


def _fm(arg):
    return arg if isinstance(arg, tuple) else (arg, 0)


def _fm_spec(rows, S, row0):
    off, rem = divmod(row0, rows)
    assert rem == 0
    return pl.BlockSpec((rows, S), lambda i: (off + i, 0))


def _bandT_fwd(qT, k_h, vT, bias, sink, *, GQ, GK, P, kvoff, name, comm=None):
    (qT, q0), (vT, v0) = _fm(qT), _fm(vT)
    S = qT.shape[1]
    ng = bias.shape[0] // GQ
    BU = (P + 2) * CHUNK
    pad = P * CHUNK
    npair = S // PAIR

    def body(qT_ref, k_ref, vT_ref, b_ref, s_ref, oT_ref, kp, vTp):
        _pad_copy_rows(kp, k_ref, pad, S)
        _pad_copy_lanes(vTp, vT_ref, pad, S)
        rowi = lax.broadcasted_iota(jnp.int32, (BU, PAIR), 0)

        def step(n2, carry):
            r = pl.multiple_of(n2 * PAIR, PAIR)
            valid = rowi >= (P - 2 * n2) * CHUNK
            for g in range(GQ):
                kv = kvoff(g)
                hs = slice(g * HEAD_DIM, (g + 1) * HEAD_DIM)
                kvs = slice(kv * HEAD_DIM, (kv + 1) * HEAD_DIM)
                qTg = qT_ref[hs, pl.ds(r, PAIR)] * 0.125
                p, _ = _bandT_softmax(kp[kv, pl.ds(r, BU), :], qTg, b_ref[g], s_ref[g, 0:1, :], valid)
                oTg = jnp.dot(vTp[kvs, pl.ds(r, BU)], p.astype(BF16), preferred_element_type=F32)
                oT_ref[hs, pl.ds(r, PAIR)] = oTg.astype(oT_ref.dtype)
            return carry

        lax.fori_loop(0, npair, step, 0, unroll=min(2, npair))

    res, got = _call_hosting(
        body, comm=comm, grid=(ng,),
        in_specs=[_fm_spec(GQ * HEAD_DIM, S, q0),
                  pl.BlockSpec((GK, S, HEAD_DIM), lambda i: (i, 0, 0)),
                  _fm_spec(GK * HEAD_DIM, S, v0),
                  pl.BlockSpec((GQ, BU, PAIR), lambda i: (i, 0, 0)),
                  pl.BlockSpec((GQ, 8, LANE), lambda i: (i, 0, 0))],
        out_specs=[pl.BlockSpec((GQ * HEAD_DIM, S), lambda i: (i, 0))],
        out_shape=[SDS((ng * GQ * HEAD_DIM, S), BF16)],
        scratch_shapes=[pltpu.VMEM((GK, S + pad, HEAD_DIM), BF16), pltpu.VMEM((GK * HEAD_DIM, S + pad), BF16)],
        name=name, args=(qT, k_h, vT, bias, sink))
    return res[0], got


def _bandT_bwd(qT, q_h, k_h, kT, v_h, doT, do_h, bias, sink, *, GQ, GK, P, kvoff, name, comm=None):
    (qT, q0), (kT, k0), (doT, d0) = _fm(qT), _fm(kT), _fm(doT)
    S = qT.shape[1]
    ng = bias.shape[0] // GQ
    BU = (P + 2) * CHUNK
    pad = P * CHUNK
    npair = S // PAIR

    def body(qT_ref, q_ref, k_ref, kT_ref, v_ref, doT_ref, do_ref, b_ref, s_ref,
             dqT_ref, dk_ref, dv_ref, db_ref, dsk_ref, kp, kTp, vp, dkp, dvp):
        _pad_copy_rows(kp, k_ref, pad, S)
        _pad_copy_rows(vp, v_ref, pad, S)
        _pad_copy_lanes(kTp, kT_ref, pad, S)
        dkp[...] = jnp.zeros_like(dkp)
        dvp[...] = jnp.zeros_like(dvp)
        db_ref[...] = jnp.zeros_like(db_ref)
        rowi = lax.broadcasted_iota(jnp.int32, (BU, PAIR), 0)

        def step(n2, dsink):
            r = pl.multiple_of(n2 * PAIR, PAIR)
            valid = rowi >= (P - 2 * n2) * CHUNK
            new = []
            for g in range(GQ):
                kv = kvoff(g)
                hs = slice(g * HEAD_DIM, (g + 1) * HEAD_DIM)
                kvs = slice(kv * HEAD_DIM, (kv + 1) * HEAD_DIM)
                qTg = qT_ref[hs, pl.ds(r, PAIR)] * 0.125
                p, ps = _bandT_softmax(kp[kv, pl.ds(r, BU), :], qTg, b_ref[g], s_ref[g, 0:1, :], valid)
                dp = jnp.dot(vp[kv, pl.ds(r, BU), :], doT_ref[hs, pl.ds(r, PAIR)], preferred_element_type=F32)
                delta = jnp.sum(p * dp, axis=0, keepdims=True)
                ds = p * (dp - delta)
                new.append(dsink[g] - ps * delta)
                db_ref[g] += ds
                dsb = ds.astype(BF16)
                dq = jnp.dot(kTp[kvs, pl.ds(r, BU)], dsb, preferred_element_type=F32) * 0.125
                dqT_ref[hs, pl.ds(r, PAIR)] = dq.astype(dqT_ref.dtype)
                dkp[kv, pl.ds(r, BU), :] += jnp.dot(dsb, q_ref[g, pl.ds(r, PAIR), :] * 0.125,
                                                    preferred_element_type=F32)
                dvp[kv, pl.ds(r, BU), :] += jnp.dot(p.astype(BF16), do_ref[g, pl.ds(r, PAIR), :],
                                                    preferred_element_type=F32)
            return tuple(new)

        dsink = lax.fori_loop(0, npair, step, tuple(jnp.zeros((1, PAIR), F32) for _ in range(GQ)))
        for g in range(GQ):
            dsk_ref[g] = jnp.broadcast_to(jnp.sum(dsink[g], axis=1, keepdims=True), (8, LANE))
        dk_ref[...] = dkp[:, pad:pad + S, :].astype(dk_ref.dtype)
        dv_ref[...] = dvp[:, pad:pad + S, :].astype(dv_ref.dtype)

    qTs = pl.BlockSpec((GQ * HEAD_DIM, S), lambda i: (i, 0))
    qhs = pl.BlockSpec((GQ, S, HEAD_DIM), lambda i: (i, 0, 0))
    khs = pl.BlockSpec((GK, S, HEAD_DIM), lambda i: (i, 0, 0))
    bs = pl.BlockSpec((GQ, BU, PAIR), lambda i: (i, 0, 0))
    ss = pl.BlockSpec((GQ, 8, LANE), lambda i: (i, 0, 0))
    nkv = ng * GK
    return _call_hosting(
        body, comm=comm, grid=(ng,),
        in_specs=[_fm_spec(GQ * HEAD_DIM, S, q0), qhs, khs, _fm_spec(GK * HEAD_DIM, S, k0), khs,
                  _fm_spec(GQ * HEAD_DIM, S, d0), qhs, bs, ss],
        out_specs=[qTs, khs, khs, bs, ss],
        out_shape=[SDS((ng * GQ * HEAD_DIM, S), BF16), SDS((nkv, S, HEAD_DIM), BF16), SDS((nkv, S, HEAD_DIM), BF16),
                   SDS((ng * GQ, BU, PAIR), F32), SDS((ng * GQ, 8, LANE), F32)],
        scratch_shapes=[pltpu.VMEM((GK, S + pad, HEAD_DIM), BF16), pltpu.VMEM((GK * HEAD_DIM, S + pad), BF16),
                        pltpu.VMEM((GK, S + pad, HEAD_DIM), BF16),
                        pltpu.VMEM((GK, S + pad, HEAD_DIM), F32), pltpu.VMEM((GK, S + pad, HEAD_DIM), F32)],
        name=name, args=(qT, q_h, k_h, kT, v_h, doT, do_h, bias, sink))


def _pair_table(tab):
    t = jnp.transpose(tab, (0, 2, 1))
    lo = jnp.pad(t, ((0, 0), (0, CHUNK), (0, 0)), constant_values=NEG_INF)
    hi = jnp.pad(t, ((0, 0), (CHUNK, 0), (0, 0)), constant_values=NEG_INF)
    return jnp.concatenate([lo, hi], axis=2)


def _unpair_table(d):
    band = d.shape[1] - CHUNK
    return jnp.transpose(d[:, 0:band, 0:CHUNK] + d[:, CHUNK:CHUNK + band, CHUNK:PAIR], (0, 2, 1))


def _heads(a, n):
    return jnp.transpose(a.reshape(a.shape[0], n, HEAD_DIM), (1, 0, 2))


def _unheads(a):
    return jnp.transpose(a, (1, 0, 2)).reshape(a.shape[1], a.shape[0] * HEAD_DIM)


def _foxT_logits(kj, qTg, cq, ck, r, c, rowi, coli):
    s = jnp.dot(kj, qTg, preferred_element_type=F32)
    s = s + cq - ck
    return jnp.where(c + rowi <= r + coli, s, NEG_INF)


def _foxT_fwd(qT, k_h, vT, ck, cq, name, comm=None):
    (qT, q0), (vT, v0) = _fm(qT), _fm(vT)
    S = qT.shape[1]
    npair = k_h.shape[0] // 2
    BQ, BK = min(FOX_BQ, S), min(FOX_BK, S)
    nq = S // BQ
    heads = [slice(g * HEAD_DIM, (g + 1) * HEAD_DIM) for g in range(2)]

    def body(qT_ref, k_ref, vT_ref, ck_ref, cq_ref, oT_ref, lse_ref):
        rowi = lax.broadcasted_iota(jnp.int32, (BK, BQ), 0)
        coli = lax.broadcasted_iota(jnp.int32, (BK, BQ), 1)

        def qstep(i, carry):
            r = pl.multiple_of(i * BQ, BQ)
            qs = [qT_ref[hs, pl.ds(r, BQ)] * 0.125 for hs in heads]
            cqs = [cq_ref[g, :, pl.ds(r, BQ)] for g in range(2)]

            def kstep(j, st):
                c = pl.multiple_of(j * BK, BK)
                new = []
                for g, hs in enumerate(heads):
                    m, l, acc = st[g]
                    s = _foxT_logits(k_ref[g, pl.ds(c, BK), :], qs[g], cqs[g], ck_ref[g, pl.ds(c, BK), :],
                                     r, c, rowi, coli)
                    mn = jnp.maximum(m, jnp.max(s, axis=0, keepdims=True))
                    al = jnp.exp(m - mn)
                    e = jnp.exp(s - mn)
                    l = al * l + jnp.sum(e, axis=0, keepdims=True)
                    acc = al * acc + jnp.dot(vT_ref[hs, pl.ds(c, BK)], e.astype(BF16), preferred_element_type=F32)
                    new.append((mn, l, acc))
                return tuple(new)

            init = (jnp.full((1, BQ), NEG_INF, F32), jnp.zeros((1, BQ), F32), jnp.zeros((HEAD_DIM, BQ), F32))
            st = lax.fori_loop(0, (r + BQ + BK - 1) // BK, kstep, (init, init))
            for g, hs in enumerate(heads):
                m, l, acc = st[g]
                oT_ref[hs, pl.ds(r, BQ)] = (acc * (1.0 / l)).astype(oT_ref.dtype)
                lse_ref[g, :, pl.ds(r, BQ)] = m + jnp.log(l)
            return carry

        lax.fori_loop(0, nq, qstep, 0)

    fT = pl.BlockSpec((LANE, S), lambda i: (i, 0))
    hm = pl.BlockSpec((2, S, HEAD_DIM), lambda i: (i, 0, 0))
    col = pl.BlockSpec((2, S, 1), lambda i: (i, 0, 0))
    rw = pl.BlockSpec((2, 1, S), lambda i: (i, 0, 0))
    return _call_hosting(
        body, comm=comm, grid=(npair,), in_specs=[_fm_spec(LANE, S, q0), hm, _fm_spec(LANE, S, v0), col, rw],
        out_specs=[fT, rw],
        out_shape=[SDS((npair * LANE, S), BF16), SDS((2 * npair, 1, S), F32)], scratch_shapes=[],
        name=name, args=(qT, k_h, vT, ck, cq))


def _foxT_bwd(qT, q_h, k_h, kT, v_h, ck, cq, oT, doT, do_h, lse, name, comm=None):
    (qT, q0), (kT, k0), (doT, d0) = _fm(qT), _fm(kT), _fm(doT)
    S = qT.shape[1]
    npair = k_h.shape[0] // 2
    BQ, BK = min(FOX_BQ, S), min(FOX_BK, S)
    nq = S // BQ
    heads = [slice(g * HEAD_DIM, (g + 1) * HEAD_DIM) for g in range(2)]

    def body(qT_ref, q_ref, k_ref, kT_ref, v_ref, ck_ref, cq_ref, oT_ref, doT_ref, do_ref, lse_ref,
             dqT_ref, dk_ref, dv_ref, dck_ref, dcq_ref, dka, dva, qa_ref):
        qa_ref[:, :, 0:HEAD_DIM] = q_ref[...] * 0.125
        qa_ref[:, :, HEAD_DIM:LANE] = jnp.ones((2, S, LANE - HEAD_DIM), BF16)
        dka[...] = jnp.zeros_like(dka)
        dva[...] = jnp.zeros_like(dva)
        rowi = lax.broadcasted_iota(jnp.int32, (BK, BQ), 0)
        coli = lax.broadcasted_iota(jnp.int32, (BK, BQ), 1)

        def qstep(i, carry):
            r = pl.multiple_of(i * BQ, BQ)
            qs = [qT_ref[hs, pl.ds(r, BQ)] * 0.125 for hs in heads]
            dos = [doT_ref[hs, pl.ds(r, BQ)] for hs in heads]
            deltas = [jnp.sum(dos[g].astype(F32) * oT_ref[hs, pl.ds(r, BQ)].astype(F32), axis=0, keepdims=True)
                      for g, hs in enumerate(heads)]
            cqs = [cq_ref[g, :, pl.ds(r, BQ)] for g in range(2)]
            lses = [lse_ref[g, :, pl.ds(r, BQ)] for g in range(2)]

            def kstep(j, st):
                c = pl.multiple_of(j * BK, BK)
                new = []
                for g, hs in enumerate(heads):
                    dq, rs = st[g]
                    s = _foxT_logits(k_ref[g, pl.ds(c, BK), :], qs[g], cqs[g], ck_ref[g, pl.ds(c, BK), :],
                                     r, c, rowi, coli)
                    p = jnp.exp(s - lses[g])
                    dp = jnp.dot(v_ref[g, pl.ds(c, BK), :], dos[g], preferred_element_type=F32)
                    ds = p * (dp - deltas[g])
                    dsb = ds.astype(BF16)
                    dka[g, pl.ds(c, BK), :] += jnp.dot(dsb, qa_ref[g, pl.ds(r, BQ), :], preferred_element_type=F32)
                    dva[g, pl.ds(c, BK), :] += jnp.dot(p.astype(BF16), do_ref[g, pl.ds(r, BQ), :],
                                                      preferred_element_type=F32)
                    new.append((dq + jnp.dot(kT_ref[hs, pl.ds(c, BK)], dsb, preferred_element_type=F32),
                                rs + jnp.sum(dsb.astype(F32), axis=0, keepdims=True)))
                return tuple(new)

            init = (jnp.zeros((HEAD_DIM, BQ), F32), jnp.zeros((1, BQ), F32))
            st = lax.fori_loop(0, (r + BQ + BK - 1) // BK, kstep, (init, init))
            for g, hs in enumerate(heads):
                dqT_ref[hs, pl.ds(r, BQ)] = (st[g][0] * 0.125).astype(dqT_ref.dtype)
                dcq_ref[g, :, pl.ds(r, BQ)] = st[g][1]
            return carry

        lax.fori_loop(0, nq, qstep, 0)
        dk_ref[...] = dka[:, :, 0:HEAD_DIM].astype(dk_ref.dtype)
        dck_ref[...] = -dka[:, :, HEAD_DIM:HEAD_DIM + 1]
        dv_ref[...] = dva[...].astype(dv_ref.dtype)

    fT = pl.BlockSpec((LANE, S), lambda i: (i, 0))
    hm = pl.BlockSpec((2, S, HEAD_DIM), lambda i: (i, 0, 0))
    col = pl.BlockSpec((2, S, 1), lambda i: (i, 0, 0))
    rw = pl.BlockSpec((2, 1, S), lambda i: (i, 0, 0))
    nh = 2 * npair
    return _call_hosting(
        body, comm=comm, grid=(npair,),
        in_specs=[_fm_spec(LANE, S, q0), hm, hm, _fm_spec(LANE, S, k0), hm, col, rw, fT, _fm_spec(LANE, S, d0), hm, rw],
        out_specs=[fT, hm, hm, col, rw],
        out_shape=[SDS((npair * LANE, S), BF16), SDS((nh, S, HEAD_DIM), BF16), SDS((nh, S, HEAD_DIM), BF16),
                   SDS((nh, S, 1), F32), SDS((nh, 1, S), F32)],
        scratch_shapes=[pltpu.VMEM((2, S, LANE), F32), pltpu.VMEM((2, S, HEAD_DIM), F32),
                        pltpu.VMEM((2, S, LANE), BF16)],
        name=name, args=(qT, q_h, k_h, kT, v_h, ck, cq, oT, doT, do_h, lse))


def _split3(x):
    hi = x.astype(BF16)
    r1 = x - hi.astype(F32)
    mid = r1.astype(BF16)
    lo = (r1 - mid.astype(F32)).astype(BF16)
    return hi, mid, lo


def _tri_dot(tri, x):
    hi, mid, lo = _split3(x)
    return (jnp.dot(tri, hi, preferred_element_type=F32) + jnp.dot(tri, mid, preferred_element_type=F32)
            + jnp.dot(tri, lo, preferred_element_type=F32))


def _fox_cum(gf, bfo, name):
    S = gf.shape[0]
    nb = S // LANE
    fcol = (GF_COLS - LANE) // LANE

    def body(f_ref, b_ref, cum_ref):
        row = lax.broadcasted_iota(jnp.int32, (LANE, LANE), 0)
        col = lax.broadcasted_iota(jnp.int32, (LANE, LANE), 1)
        tri = jnp.where(row >= col, 1.0, 0.0).astype(BF16)
        carry = jnp.zeros((1, LANE), F32)
        for t in range(nb):
            xl = f_ref[t * LANE:(t + 1) * LANE, :] + b_ref[...]
            lf = jnp.minimum(xl, 0.0) - jnp.log(1.0 + jnp.exp(-jnp.abs(xl)))
            cblk = _tri_dot(tri, lf) + carry
            cum_ref[t * LANE:(t + 1) * LANE, :] = cblk
            carry = cblk[LANE - 1:LANE, :]

    return pl.pallas_call(
        body, grid=(1,), in_specs=[pl.BlockSpec((S, LANE), lambda i: (0, fcol)), _vec(LANE)],
        out_specs=pl.BlockSpec((S, LANE), lambda i: (0, 0)), out_shape=SDS((S, LANE), F32),
        compiler_params=_cparams(("arbitrary",)), name=name)(gf, bfo)


def _fox_cum_bwd(gf, bfo, dcum, name):
    S = gf.shape[0]
    nb = S // LANE
    fcol = (GF_COLS - LANE) // LANE

    def body(f_ref, b_ref, dc_ref, df_ref, db_ref):
        row = lax.broadcasted_iota(jnp.int32, (LANE, LANE), 0)
        col = lax.broadcasted_iota(jnp.int32, (LANE, LANE), 1)
        tri = jnp.where(row <= col, 1.0, 0.0).astype(BF16)
        carry = jnp.zeros((1, LANE), F32)
        tot = jnp.zeros((1, LANE), F32)
        for t in range(nb - 1, -1, -1):
            rows = slice(t * LANE, (t + 1) * LANE)
            dlf = _tri_dot(tri, dc_ref[rows, :]) + carry
            carry = dlf[0:1, :]
            xl = f_ref[rows, :] + b_ref[...]
            dfl = dlf * (1.0 / (1.0 + jnp.exp(xl)))
            df_ref[rows, :] = dfl.astype(df_ref.dtype)
            tot = tot + jnp.sum(dfl, axis=0, keepdims=True)
        db_ref[...] = tot

    return pl.pallas_call(
        body, grid=(1,),
        in_specs=[pl.BlockSpec((S, LANE), lambda i: (0, fcol)), _vec(LANE), pl.BlockSpec((S, LANE), lambda i: (0, 0))],
        out_specs=[pl.BlockSpec((S, LANE), lambda i: (0, 0)), _vec(LANE)],
        out_shape=[SDS((S, LANE), BF16), SDS((1, LANE), F32)],
        compiler_params=_cparams(("arbitrary",)), name=name)(gf, bfo, dcum)


REL_FAR = C_PREV * CHUNK - REL_CLIP


def _rel_onehot(qi, band):
    w = band - REL_FAR
    r = lax.broadcasted_iota(jnp.int32, (N_REL_PAD, w), 0)
    j = lax.broadcasted_iota(jnp.int32, (N_REL_PAD, w), 1) + REL_FAR
    idx = jnp.clip(C_PREV * CHUNK + qi - j, -REL_CLIP, REL_CLIP) + REL_CLIP
    return jnp.where(r == idx, 1.0, 0.0).astype(BF16)


def _rel_expand(rel, name):
    band = (C_PREV + 1) * CHUNK

    def body(rel_ref, o_ref):
        hi, mid, lo = _split3(rel_ref[...])
        far = jnp.broadcast_to(rel_ref[:, 2 * REL_CLIP:2 * REL_CLIP + 1], (N_HEADS, REL_FAR))

        def row(qi, carry):
            oh = _rel_onehot(qi, band)
            o_ref[qi, :, 0:REL_FAR] = far
            o_ref[qi, :, REL_FAR:band] = (jnp.dot(hi, oh, preferred_element_type=F32)
                                          + jnp.dot(mid, oh, preferred_element_type=F32)
                                          + jnp.dot(lo, oh, preferred_element_type=F32))
            return carry

        lax.fori_loop(0, CHUNK, row, 0, unroll=2)

    return pl.pallas_call(
        body, grid=(1,), in_specs=[pl.BlockSpec((N_HEADS, N_REL_PAD), lambda i: (0, 0))],
        out_specs=pl.BlockSpec((CHUNK, N_HEADS, band), lambda i: (0, 0, 0)),
        out_shape=SDS((CHUNK, N_HEADS, band), F32),
        compiler_params=_cparams(("arbitrary",)), name=name)(rel)


def _rel_reduce(dbias, name):
    band = (C_PREV + 1) * CHUNK
    NT = (((1,), (1,)), ((), ()))

    def body(d_ref, o_ref):
        def row(qi, st):
            acc, far = st
            oh = _rel_onehot(qi, band)
            hi, mid, lo = _split3(d_ref[qi, :, REL_FAR:band])
            acc = acc + (lax.dot_general(hi, oh, NT, preferred_element_type=F32)
                         + lax.dot_general(mid, oh, NT, preferred_element_type=F32)
                         + lax.dot_general(lo, oh, NT, preferred_element_type=F32))
            return acc, far + jnp.sum(d_ref[qi, :, 0:REL_FAR], axis=-1, keepdims=True)

        acc, far = lax.fori_loop(0, CHUNK, row, (jnp.zeros((N_HEADS, N_REL_PAD), F32), jnp.zeros((N_HEADS, 1), F32)),
                                 unroll=2)
        col = lax.broadcasted_iota(jnp.int32, (N_HEADS, N_REL_PAD), 1)
        o_ref[...] = acc + jnp.where(col == 2 * REL_CLIP, far, 0.0)

    return pl.pallas_call(
        body, grid=(1,), in_specs=[pl.BlockSpec((CHUNK, N_HEADS, band), lambda i: (0, 0, 0))],
        out_specs=pl.BlockSpec((N_HEADS, N_REL_PAD), lambda i: (0, 0)),
        out_shape=SDS((N_HEADS, N_REL_PAD), F32),
        compiler_params=_cparams(("arbitrary",)), name=name)(dbias)


def _alibi_table():
    qi = np.arange(CHUNK)[:, None]
    j = np.arange((A_PREV + 1) * CHUNK)[None, :]
    dist = np.abs(A_PREV * CHUNK + qi - j).astype(np.float32)
    slopes = np.exp2(-8.0 * np.arange(1, N_HEADS + 1, dtype=np.float32) / N_HEADS).astype(np.float32)
    return jnp.asarray(-slopes[:, None, None] * dist[None])


def _ada_fwd(c_all, w, b, name):
    n = w.shape[2]

    def body(c_ref, w_ref, b_ref, o_ref):
        cv = c_ref[...]
        cond = (cv * _sigmoid(cv)).astype(BF16)
        o_ref[0] = jnp.dot(cond, w_ref[0].astype(BF16), preferred_element_type=F32) + b_ref[0]

    return pl.pallas_call(
        body, grid=(DEPTH,),
        in_specs=[pl.BlockSpec((16, D_MODEL), lambda l: (0, 0)), pl.BlockSpec((1, D_MODEL, n), lambda l: (l, 0, 0)),
                  pl.BlockSpec((1, 1, n), lambda l: (l, 0, 0))],
        out_specs=pl.BlockSpec((1, 16, n), lambda l: (l, 0, 0)), out_shape=SDS((DEPTH, 16, n), F32),
        compiler_params=_cparams(("parallel",)), name=name)(c_all, w, b)


def _ada_bwd(c_t, dmod, name, comm=None):
    n = dmod.shape[2]
    bn = _blk(n, 512)
    tr = 256

    def body(c_ref, d_ref, o_ref):
        cv = c_ref[...]
        cond = (cv * _sigmoid(cv)).astype(BF16).astype(F32)
        dm = d_ref[0].astype(BF16).astype(F32)
        acc = cond[:, 0:1] * dm[0:1, :]
        for b_ in range(1, 8):
            acc = acc + cond[:, b_:b_ + 1] * dm[b_:b_ + 1, :]
        o_ref[0] = acc

    res, got = _call_hosting(
        body, comm=comm, grid=(DEPTH, D_MODEL // tr, n // bn),
        in_specs=[pl.BlockSpec((tr, 8), lambda l, i, j: (i, 0)), pl.BlockSpec((1, 8, bn), lambda l, i, j: (l, 0, j))],
        out_specs=[pl.BlockSpec((1, tr, bn), lambda l, i, j: (l, i, j))], out_shape=[SDS((DEPTH, D_MODEL, n), F32)],
        scratch_shapes=[], name=name, args=(c_t, dmod))
    return res[0], got


def _adamw(w, m, v, parts, name):
    L, R, C = w.shape
    per_layer = isinstance(parts, (list, tuple))
    plist = list(parts) if per_layer else [parts]
    P = plist[0].shape[0]
    tr = _blk_rows(R, max(16, (1 << 18) // C))
    nr = R // tr
    c1 = 1.0 - ADAM_B1 ** ADAM_STEP
    c2 = 1.0 - ADAM_B2 ** ADAM_STEP

    def total(p_ref):
        g = p_ref[0].astype(F32)
        for k in range(1, P):
            g = g + p_ref[k].astype(F32)
        return g

    def body(w_ref, m_ref, v_ref, *rest):
        p_refs, (g_ref, d_ref, nm_ref, nv_ref) = rest[:len(plist)], rest[len(plist):]
        g = total(p_refs[0])
        for k in range(1, len(plist)):
            g = jnp.where(pl.program_id(0) == k, total(p_refs[k]), g)
        mn = ADAM_B1 * m_ref[0] + (1.0 - ADAM_B1) * g
        vn = ADAM_B2 * v_ref[0] + (1.0 - ADAM_B2) * (g * g)
        m_hat = mn / c1
        v_hat = vn / c2
        g_ref[0] = g
        nm_ref[0] = mn
        nv_ref[0] = vn
        d_ref[0] = -ADAM_LR * (m_hat / (jnp.sqrt(v_hat) + ADAM_EPS) + ADAM_WD * w_ref[0])

    rs = pl.BlockSpec((1, tr, C), lambda l, i: (l, i, 0))
    if per_layer:
        def layer_spec(k):
            return pl.BlockSpec((P, tr, C), lambda l, i: (0, jnp.where(l == k, i, 0), 0))
        pspecs = [layer_spec(k) for k in range(L)]
    else:
        pspecs = [pl.BlockSpec((P, tr, C), lambda l, i: (0, l * nr + i, 0))]
    return pl.pallas_call(
        body, grid=(L, nr), in_specs=[rs, rs, rs] + pspecs, out_specs=[rs, rs, rs, rs],
        out_shape=[SDS((L, R, C), F32)] * 4, compiler_params=_cparams(("parallel", "parallel")),
        name=name)(w, m, v, *plist)


def _blk_rows(R, cap):
    if R <= cap:
        return R
    best = None
    for t in range(16, cap + 1, 16):
        if R % t == 0:
            best = t
    assert best is not None, (R, cap)
    return best


def _add_cast_rows(g, t, name):
    Q, R, C = g.shape
    half = R // 2
    tr = _blk_rows(half, max(16, (1 << 19) // C))
    nb = half // tr

    def body(lo_ref, hi_ref, t_ref, o_ref):
        c = lax.axis_index("c")

        @pl.when(c == 0)
        def _():
            o_ref[...] = (lo_ref[...].astype(F32) + t_ref[...].astype(F32)).astype(o_ref.dtype)

        @pl.when(c == 1)
        def _():
            o_ref[...] = (hi_ref[...].astype(F32) + t_ref[...].astype(F32)).astype(o_ref.dtype)

    bs = pl.BlockSpec((1, tr, C), lambda q, i: (q, i, 0))
    hi = pl.BlockSpec((1, tr, C), lambda q, i: (q, nb + i, 0))
    return pl.pallas_call(
        body, grid=(Q, nb), in_specs=[bs, hi, bs], out_specs=bs, out_shape=SDS((Q, half, C), BF16),
        compiler_params=_cparams(("parallel", "parallel")), name=name)(g, g, t)


def _coords():
    return lax.axis_index("x"), lax.axis_index("y"), lax.axis_index("c")


def _flip(v, bit):
    return 1 - v if bit else v


def _all_gather8(v, name):
    R = v.shape[0]

    def body(v_ref, o_ref, send_sems, recv_sems):
        x, y, c = _coords()
        me = 4 * x + 2 * y + c
        o_ref[me] = v_ref[...]
        copies = []
        for k in range(1, 8):
            peer = (_flip(x, k & 4), _flip(y, k & 2), _flip(c, k & 1))
            cp = pltpu.make_async_remote_copy(
                src_ref=v_ref, dst_ref=o_ref.at[me], send_sem=send_sems.at[k - 1], recv_sem=recv_sems.at[k - 1],
                device_id=peer, device_id_type=MESH)
            cp.start()
            copies.append(cp)
        for cp in copies:
            cp.wait_recv()
        for cp in copies:
            cp.wait_send()

    return pl.pallas_call(
        body, in_specs=[VMEM_SPEC], out_specs=VMEM_SPEC, out_shape=SDS((8, R, LANE), v.dtype),
        scratch_shapes=[pltpu.SemaphoreType.DMA((7,)), pltpu.SemaphoreType.DMA((7,))],
        compiler_params=pltpu.CompilerParams(vmem_limit_bytes=VMEM_LIMIT), name=name)(v)


def _sibling_swap_rows(arrs, name):
    n = len(arrs)

    def body(*refs):
        in_refs, out_refs = refs[:n], refs[n:2 * n]
        send_sems, recv_sems = refs[2 * n:]
        x, y, c = _coords()
        copies = []
        for a in range(n):
            Q, R = in_refs[a].shape[0], in_refs[a].shape[1]
            half = R // 2
            src = in_refs[a].at[pl.ds(0, Q), pl.ds(pl.multiple_of((1 - c) * half, 16), half)]
            cp = pltpu.make_async_remote_copy(
                src_ref=src, dst_ref=out_refs[a], send_sem=send_sems.at[a], recv_sem=recv_sems.at[a],
                device_id=(x, y, 1 - c), device_id_type=MESH)
            cp.start()
            copies.append(cp)
        for cp in copies:
            cp.wait_recv()
        for cp in copies:
            cp.wait_send()

    return pl.pallas_call(
        body, in_specs=[ANY] * n, out_specs=[ANY] * n,
        out_shape=[SDS((a.shape[0], a.shape[1] // 2, a.shape[2]), a.dtype) for a in arrs],
        scratch_shapes=[pltpu.SemaphoreType.DMA((n,)), pltpu.SemaphoreType.DMA((n,))],
        name=name)(*arrs)


class _OwnerReduce:
    aliased = False

    def __init__(self, srcs, lay):
        self.srcs, self.lay, self.n = list(srcs), lay, len(srcs)
        self.out_shapes = [SDS(a.shape, a.dtype) for a in self.srcs]
        self.sem_shapes = [pltpu.SemaphoreType.DMA((self.n, 3)), pltpu.SemaphoreType.DMA((self.n, 3)),
                           pltpu.SemaphoreType.DMA((self.n,))]

    def _copies(self, src_refs, dst_refs, sems):
        ici_send, ici_recv, loc_sem = sems
        x, y, c = _coords()
        p = 2 * x + y
        local, remote = [], []
        for a in range(self.n):
            local.append(pltpu.make_async_copy(src_refs[a].at[p], dst_refs[a].at[p], loc_sem.at[a]))
            for k in range(1, 4):
                qx, qy = _flip(x, k & 2), _flip(y, k & 1)
                remote.append(pltpu.make_async_remote_copy(
                    src_ref=src_refs[a].at[2 * qx + qy], dst_ref=dst_refs[a].at[p], send_sem=ici_send.at[a, k - 1],
                    recv_sem=ici_recv.at[a, k - 1], device_id=(qx, qy, self.lay), device_id_type=MESH))
        return c, local, remote

    def start(self, src_refs, dst_refs, sems):
        c, local, remote = self._copies(src_refs, dst_refs, sems)

        @pl.when(c == self.lay)
        def _():
            for cp in local + remote:
                cp.start()

    def finish(self, src_refs, dst_refs, sems):
        c, local, remote = self._copies(src_refs, dst_refs, sems)

        @pl.when(c == self.lay)
        def _():
            for cp in remote:
                cp.wait_recv()
            for cp in remote:
                cp.wait_send()
            for cp in local:
                cp.wait()


def _call_hosting(body, *, comm, grid, in_specs, out_specs, out_shape, scratch_shapes, name, args, semantics=None):
    n_in, n_out, n_scr = len(args), len(out_shape), len(scratch_shapes)
    if comm is None:
        sem = semantics if semantics is not None else ("parallel",) * len(grid)
        res = pl.pallas_call(body, grid=grid, in_specs=in_specs, out_specs=out_specs, out_shape=out_shape,
                             scratch_shapes=scratch_shapes, compiler_params=_cparams(sem), name=name)(*args)
        return list(res), None
    k = comm.n

    def hosted(*refs):
        ins, cin = refs[:n_in], refs[n_in:n_in + k]
        outs = refs[n_in + k:n_in + k + n_out]
        cout = refs[n_in + k + n_out:n_in + 2 * k + n_out]
        scr = refs[n_in + 2 * k + n_out:n_in + 2 * k + n_out + n_scr]
        sems = refs[n_in + 2 * k + n_out + n_scr:]
        first = pl.program_id(0) == 0
        last = pl.program_id(0) == grid[0] - 1
        for d in range(1, len(grid)):
            first = jnp.logical_and(first, pl.program_id(d) == 0)
            last = jnp.logical_and(last, pl.program_id(d) == grid[d] - 1)

        @pl.when(first)
        def _():
            comm.start(cin, cout, sems)

        body(*ins, *outs, *scr)

        @pl.when(last)
        def _():
            comm.finish(cin, cout, sems)

    aliases = {n_in + j: n_out + j for j in range(k)} if comm.aliased else {}
    res = pl.pallas_call(
        hosted, grid=grid, in_specs=list(in_specs) + [ANY] * k, out_specs=list(out_specs) + [ANY] * k,
        out_shape=list(out_shape) + comm.out_shapes, scratch_shapes=list(scratch_shapes) + comm.sem_shapes,
        input_output_aliases=aliases, compiler_params=_cparams(("arbitrary",) * len(grid)),
        name=name)(*args, *comm.srcs)
    return list(res[:n_out]), list(res[n_out:])


class _RowHalfGather:
    aliased = False

    def __init__(self, srcs):
        self.srcs, self.n = list(srcs), len(srcs)
        self.out_shapes = [SDS((4,) + a.shape, a.dtype) for a in self.srcs]
        n = self.n
        self.sem_shapes = [pltpu.SemaphoreType.DMA((n, 3)), pltpu.SemaphoreType.DMA((n, 3)),
                           pltpu.SemaphoreType.DMA((n, 3)), pltpu.SemaphoreType.DMA((n, 3)),
                           pltpu.SemaphoreType.DMA((n,))]

    def _copies(self, src_refs, dst_refs, sems):
        ici_send, ici_recv, d2d_send, d2d_recv, loc_sem = sems
        x, y, c = _coords()
        p = 2 * x + y
        local, first, fwd = [], [], []
        for a in range(self.n):
            R = src_refs[a].shape[0] // 2
            half = pl.ds(pl.multiple_of(c * R, 16), R)
            local.append(pltpu.make_async_copy(src_refs[a], dst_refs[a].at[p], loc_sem.at[a]))
            for k in range(1, 4):
                qx, qy = _flip(x, k & 2), _flip(y, k & 1)
                first.append(pltpu.make_async_remote_copy(
                    src_ref=src_refs[a].at[half], dst_ref=dst_refs[a].at[p, half], send_sem=ici_send.at[a, k - 1],
                    recv_sem=ici_recv.at[a, k - 1], device_id=(qx, qy, c), device_id_type=MESH))
                slot = dst_refs[a].at[2 * qx + qy, half]
                fwd.append(pltpu.make_async_remote_copy(
                    src_ref=slot, dst_ref=slot, send_sem=d2d_send.at[a, k - 1], recv_sem=d2d_recv.at[a, k - 1],
                    device_id=(x, y, 1 - c), device_id_type=MESH))
        return local, first, fwd

    def start(self, src_refs, dst_refs, sems):
        local, first, _ = self._copies(src_refs, dst_refs, sems)
        for cp in local + first:
            cp.start()

    def finish(self, src_refs, dst_refs, sems):
        local, first, fwd = self._copies(src_refs, dst_refs, sems)
        for got, on in zip(first, fwd):
            got.wait_recv()
            on.start()
        for cp in fwd:
            cp.wait_recv()
        for cp in first + fwd:
            cp.wait_send()
        for cp in local:
            cp.wait()

    def run(self, name):
        return _run_exchange(self, name)


def _run_exchange(comm, name):
    n = comm.n

    def body(*refs):
        src_refs, dst_refs, sems = refs[:n], refs[n:2 * n], refs[2 * n:]
        comm.start(src_refs, dst_refs, sems)
        comm.finish(src_refs, dst_refs, sems)

    return pl.pallas_call(body, in_specs=[ANY] * n, out_specs=[ANY] * n, out_shape=comm.out_shapes,
                          scratch_shapes=comm.sem_shapes, name=name)(*comm.srcs)


class _RowHalfReduce:
    aliased = False

    def __init__(self, srcs):
        self.srcs, self.n = list(srcs), len(srcs)
        self.out_shapes = [SDS((4, 2 * a.shape[1], a.shape[2]), a.dtype) for a in self.srcs]
        n = self.n
        self.sem_shapes = [pltpu.SemaphoreType.DMA((n, 3)), pltpu.SemaphoreType.DMA((n, 3)),
                           pltpu.SemaphoreType.DMA((n, 4)), pltpu.SemaphoreType.DMA((n, 4)),
                           pltpu.SemaphoreType.DMA((n,))]

    def _copies(self, src_refs, dst_refs, sems):
        ici_send, ici_recv, d2d_send, d2d_recv, loc_sem = sems
        x, y, c = _coords()
        p = 2 * x + y
        local, first, fwd = [], [], []
        for a in range(self.n):
            R = src_refs[a].shape[1]
            half = pl.ds(pl.multiple_of(c * R, 16), R)
            local.append(pltpu.make_async_copy(src_refs[a].at[p], dst_refs[a].at[p, half], loc_sem.at[a]))
            for k in range(4):
                qx, qy = _flip(x, k & 2), _flip(y, k & 1)
                if k:
                    first.append(pltpu.make_async_remote_copy(
                        src_ref=src_refs[a].at[2 * qx + qy], dst_ref=dst_refs[a].at[p, half],
                        send_sem=ici_send.at[a, k - 1], recv_sem=ici_recv.at[a, k - 1], device_id=(qx, qy, c),
                        device_id_type=MESH))
                slot = dst_refs[a].at[2 * qx + qy, half]
                fwd.append(pltpu.make_async_remote_copy(
                    src_ref=slot, dst_ref=slot, send_sem=d2d_send.at[a, k], recv_sem=d2d_recv.at[a, k],
                    device_id=(x, y, 1 - c), device_id_type=MESH))
        return local, first, fwd

    def start(self, src_refs, dst_refs, sems):
        local, first, _ = self._copies(src_refs, dst_refs, sems)
        for cp in local + first:
            cp.start()

    def finish(self, src_refs, dst_refs, sems):
        local, first, fwd = self._copies(src_refs, dst_refs, sems)
        for a in range(self.n):
            local[a].wait()
            fwd[4 * a].start()
            for k in range(1, 4):
                first[3 * a + k - 1].wait_recv()
                fwd[4 * a + k].start()
        for cp in fwd:
            cp.wait_recv()
        for cp in first + fwd:
            cp.wait_send()

    def run(self, name):
        return _run_exchange(self, name)


class _SiblingSend:
    aliased = False

    def __init__(self, srcs, src_core):
        self.srcs, self.src_core, self.n = list(srcs), src_core, len(srcs)
        self.out_shapes = [SDS(a.shape, a.dtype) for a in self.srcs]
        self.sem_shapes = [pltpu.SemaphoreType.DMA((self.n,)), pltpu.SemaphoreType.DMA((self.n,))]

    def _copies(self, src_refs, dst_refs, sems):
        x, y, c = _coords()
        return c, [pltpu.make_async_remote_copy(
            src_ref=src_refs[a], dst_ref=dst_refs[a], send_sem=sems[0].at[a], recv_sem=sems[1].at[a],
            device_id=(x, y, 1 - c), device_id_type=MESH) for a in range(self.n)]

    def start(self, src_refs, dst_refs, sems):
        c, copies = self._copies(src_refs, dst_refs, sems)

        @pl.when(c == self.src_core)
        def _():
            for cp in copies:
                cp.start()

    def finish(self, src_refs, dst_refs, sems):
        c, copies = self._copies(src_refs, dst_refs, sems)

        @pl.when(c == self.src_core)
        def _():
            for cp in copies:
                cp.wait_send()

        @pl.when(c != self.src_core)
        def _():
            for cp in copies:
                cp.wait_recv()


class _Handoff:
    aliased = True

    def __init__(self, srcs, lay, slots):
        self.srcs, self.lay, self.slots, self.n = list(srcs), lay, tuple(slots), len(srcs)
        self.out_shapes = [SDS(a.shape, a.dtype) for a in self.srcs]
        ns = len(self.slots)
        self.sem_shapes = [pltpu.SemaphoreType.DMA((self.n, ns)), pltpu.SemaphoreType.DMA((self.n, ns))]

    def _copies(self, dst_refs, sems):
        x, y, c = _coords()
        copies = []
        for a in range(self.n):
            for j, k in enumerate(self.slots):
                slot = dst_refs[a].at[2 * _flip(x, k & 2) + _flip(y, k & 1)]
                copies.append(pltpu.make_async_remote_copy(
                    src_ref=slot, dst_ref=slot, send_sem=sems[0].at[a, j], recv_sem=sems[1].at[a, j],
                    device_id=(x, y, 1 - c), device_id_type=MESH))
        return c, copies

    def start(self, src_refs, dst_refs, sems):
        c, copies = self._copies(dst_refs, sems)

        @pl.when(c == self.lay)
        def _():
            for cp in copies:
                cp.start()

    def finish(self, src_refs, dst_refs, sems):
        c, copies = self._copies(dst_refs, sems)

        @pl.when(c == self.lay)
        def _():
            for cp in copies:
                cp.wait_send()

        @pl.when(c != self.lay)
        def _():
            for cp in copies:
                cp.wait_recv()


def _add_cast_on(a, b, lay, name):
    Q, R, C = b.shape
    tr = _blk_rows(R, max(16, (1 << 19) // C))

    def body(a_ref, b_ref, o_ref):
        @pl.when(lax.axis_index("c") == lay)
        def _():
            o_ref[...] = (a_ref[...].astype(F32) + b_ref[...].astype(F32)).astype(o_ref.dtype)

    bs = pl.BlockSpec((1, tr, C), lambda q, i: (q, i, 0))
    return pl.pallas_call(
        body, grid=(Q, R // tr), in_specs=[bs, bs], out_specs=bs, out_shape=SDS((Q, R, C), BF16),
        compiler_params=_cparams(("parallel", "parallel")), name=name)(a, b)


_IN_SIZES = (512, 128, 128, 512, 512, 512, 8, 512, 512, 512, 3072)
_IN_OFF = tuple(int(v) for v in np.cumsum((0,) + _IN_SIZES))
_IN_Q = N_IN_COLS // 4


def _pack_w_in(w):
    def cols(lo, hi):
        out = []
        while lo < hi:
            q, off = divmod(lo, _IN_Q)
            n = min(hi - lo, _IN_Q - off)
            out.append(w[q, :, off:off + n])
            lo += n
        return out

    fb0, fb1, g0 = _IN_OFF[6], _IN_OFF[7], _IN_OFF[10]
    wqkv = jnp.concatenate(cols(0, fb0) + cols(fb1, g0), axis=1)
    wgf = jnp.concatenate(cols(g0, N_IN_COLS) + cols(fb0, fb1) + [jnp.zeros((w.shape[1], LANE - 8), w.dtype)], axis=1)
    return wqkv, wgf


def _unpack_w_in(dqkv, dgf):
    fb0, fb1, g0 = _IN_OFF[6], _IN_OFF[7], _IN_OFF[10]

    def cols(lo, hi):
        out = []
        while lo < hi:
            if lo < fb0:
                n = min(hi, fb0) - lo
                out.append(dqkv[:, lo:lo + n])
            elif lo < fb1:
                n = min(hi, fb1) - lo
                out.append(dgf[:, 3072 + lo - fb0:3072 + lo - fb0 + n])
            elif lo < g0:
                n = min(hi, g0) - lo
                out.append(dqkv[:, lo - 8:lo - 8 + n])
            else:
                n = hi - lo
                out.append(dgf[:, lo - g0:lo - g0 + n])
            lo += n
        return out

    return jnp.stack([jnp.concatenate(cols(q * _IN_Q, (q + 1) * _IN_Q), axis=1) for q in range(4)])


def _pad_rows(a, rows):
    return jnp.pad(a, ((0, rows - a.shape[0]), (0, 0)))


def _small_pack(parts):
    flat = jnp.concatenate([p.reshape(-1) for p in parts])
    n = flat.shape[0]
    rows = -(-n // LANE)
    rows = -(-rows // 8) * 8
    return jnp.pad(flat, (0, rows * LANE - n)).reshape(rows, LANE)


def _small_unpack(block, shapes):
    flat = block.reshape(-1)
    out, off = [], 0
    for s in shapes:
        n = int(np.prod(s))
        out.append(flat[off:off + n].reshape(s))
        off += n
    return out


def _kv_same(g):
    return 0


def _kv_own(g):
    return g


_mm_plain = _mm


def _mm_hosting(a, b, *, comm, **kw):
    if comm is None:
        return _mm(a, b, **kw), None
    return _mm(a, b, comm=comm, **kw)


def _layer_fwd(x, mod, p, l, ride):
    sh_m, sc_m, g_m, sh_f, sc_f, g_f = mod
    nm = "l%d_" % l

    def carried(name, run):
        res, got = run(ride.comm_for(name))
        if got is not None:
            ride.done(name, got)
        return res

    h1 = _norm_mod_fwd(x, p["norm_mix_g"], sc_m, sh_m, nm + "norm_mix_fwd")
    qkv = carried("proj_qkv", lambda cm: _mm_hosting(h1, p["wqkv"], mode="nn", out_dtype=BF16,
                                                     name=nm + "proj_qkv", comm=cm))
    gf = carried("proj_gf", lambda cm: _mm_hosting(h1, p["wgf"], mode="nn", out_dtype=F32, name=nm + "proj_gf",
                                                  cap_n=640, comm=cm))
    qkv_t = qkv.T
    o_a_t = carried("attn_a", lambda cm: _bandT_fwd(
        (qkv_t, 0), _heads(qkv[:, 512:640], A_KV_HEADS), (qkv_t, 640), p["alibi"], p["sink_tab"],
        GQ=4, GK=1, P=A_PREV, kvoff=_kv_same, name=nm + "attn_a_fwd", comm=cm))
    cum = _fox_cum(gf, p["b_forget_pad"], nm + "fox_cum")
    cum_t = cum[:, :N_HEADS].T
    cc, cr = cum_t[:, :, None], cum_t[:, None, :]
    o_b_t, lse_b = carried("attn_b", lambda cm: _foxT_fwd(
        (qkv_t, 768), _heads(qkv[:, 1280:1792], N_HEADS), (qkv_t, 1792), cc, cr, nm + "attn_b_fwd", comm=cm))
    o_c_t = carried("attn_c", lambda cm: _bandT_fwd(
        (qkv_t, 2304), _heads(qkv[:, 2816:3328], N_HEADS), (qkv_t, 3328), p["rel_tab"], p["no_sink"],
        GQ=2, GK=2, P=C_PREV, kvoff=_kv_own, name=nm + "attn_c_fwd", comm=cm))
    p = dict(p, **ride.late_weights())
    o = jnp.concatenate([o_a_t, o_b_t, o_c_t], axis=0).T
    y = _mm(o, p["wb"], mode="nn", out_dtype=BF16, groups=3, name=nm + "branch")
    merged = _merge_fwd(y, gf, nm + "merge_fwd")
    mix = _mm(merged, p["wout"], mode="nn", out_dtype=F32, name=nm + "out_proj")
    x1 = _resid_fwd(x, mix, g_m, nm + "resid_mix")
    h2 = _norm_mod_fwd(x1, p["norm_ffn_g"], sc_f, sh_f, nm + "norm_ffn_fwd")
    u = carried("ffn_in", lambda cm: _mm_hosting(h2, p["wfi"], mode="nn", out_dtype=BF16, name=nm + "ffn_in",
                                                 cap_n=512, comm=cm))
    a = _swiglu_fwd(u, nm + "swiglu_fwd")
    f = _mm(a, p["wfo"], mode="nn", out_dtype=F32, name=nm + "ffn_out", cap_m=1024)
    x2 = _resid_fwd(x1, f, g_f, nm + "resid_ffn")
    saved = dict(x=x, h1=h1, qkv=qkv, qkv_t=qkv_t, gf=gf, cc=cc, cr=cr, o_b_t=o_b_t, lse_b=lse_b, o=o, y=y, merged=merged,
                 mix=mix, x1=x1, h2=h2, u=u, a=a, f=f)
    return x2, saved, p


def _layer_bwd(dx2, mod, p, s, l, ride=None):
    sh_m, sc_m, g_m, sh_f, sc_f, g_f = mod
    nm = "l%d_" % l

    def _mm(a, b, *, name, **kw):
        comm = ride.comm_for(name) if ride is not None else None
        if comm is None:
            return _mm_plain(a, b, name=nm + name, **kw)
        out, got = _mm_plain(a, b, name=nm + name, comm=comm, **kw)
        ride.done(name, got)
        return out

    dg_f, df = _resid_bwd(dx2, s["f"], g_f, nm + "resid_ffn_bwd")
    da = _mm(df, p["wfo"], mode="nt", out_dtype=BF16, name="ffn_out_dx", cap_m=1024, cap_n=1408)
    d_wfo = _mm(s["a"], df, mode="tn", out_dtype=BF16, name="ffn_out_dw", cap_m=1408, cap_k=2048)
    du = _swiglu_bwd(da, s["u"], nm + "swiglu_bwd")
    dh2 = _mm(du, p["wfi"], mode="nt", out_dtype=F32, name="ffn_in_dx", cap_m=1024)
    d_wfi = _mm(s["h2"], du, mode="tn", out_dtype=BF16, name="ffn_in_dw", cap_m=1024, cap_n=1408, cap_k=2048,
                col_quarters=True)
    dx1, dsc_f, dsh_f, dgn_f = _norm_mod_bwd(s["x1"], [dh2], dx2, p["norm_ffn_g"], sc_f, nm + "norm_ffn_bwd")
    dg_m, dmix = _resid_bwd(dx1, s["mix"], g_m, nm + "resid_mix_bwd")
    dmerged = _mm(dmix, p["wout"], mode="nt", out_dtype=F32, name="out_proj_dx")
    d_wout = _mm(s["merged"], dmix, mode="tn", out_dtype=BF16, name="out_proj_dw", cap_m=1024, cap_k=2048)
    dy, dgates = _merge_bwd(dmerged, s["y"], s["gf"], nm + "merge_bwd")
    do = _mm(dy, p["wb"], mode="nt", out_dtype=BF16, groups=3, name="branch_dx")
    d_wb = _mm(s["o"], dy, mode="tn", out_dtype=BF16, groups=3, name="branch_dw", cap_k=2048,
               col_quarters=True)
    comms = ride.exchanges() if ride is not None else (None, None, None)
    qkv, qkv_t = s["qkv"], s["qkv_t"]
    do_t = do.T
    (dqa_t, dka_h, dva_h, _, dsink), got_a = _bandT_bwd(
        (qkv_t, 0), _heads(qkv[:, 0:512], N_HEADS), _heads(qkv[:, 512:640], A_KV_HEADS), (qkv_t, 512),
        _heads(qkv[:, 640:768], A_KV_HEADS), (do_t, 0), _heads(do[:, 0:512], N_HEADS), p["alibi"], p["sink_tab"],
        GQ=4, GK=1, P=A_PREV, kvoff=_kv_same, name=nm + "attn_a_bwd", comm=comms[0])
    (dqb_t, dkb_h, dvb_h, dck, dcq), got_b = _foxT_bwd(
        (qkv_t, 768), _heads(qkv[:, 768:1280], N_HEADS), _heads(qkv[:, 1280:1792], N_HEADS), (qkv_t, 1280),
        _heads(qkv[:, 1792:2304], N_HEADS), s["cc"], s["cr"], s["o_b_t"], (do_t, 512),
        _heads(do[:, 512:1024], N_HEADS), s["lse_b"], nm + "attn_b_bwd", comm=comms[1])
    dcum = jnp.pad((dck[:, :, 0] + dcq[:, 0, :]).T, ((0, 0), (0, LANE - N_HEADS)))
    dfb, db_forget = _fox_cum_bwd(s["gf"], p["b_forget_pad"], dcum, nm + "fox_cum_bwd")
    (dqc_t, dkc_h, dvc_h, dbias_c, _), got_c = _bandT_bwd(
        (qkv_t, 2304), _heads(qkv[:, 2304:2816], N_HEADS), _heads(qkv[:, 2816:3328], N_HEADS), (qkv_t, 2816),
        _heads(qkv[:, 3328:3840], N_HEADS), (do_t, 1024), _heads(do[:, 1024:1536], N_HEADS), p["rel_tab"],
        p["no_sink"], GQ=2, GK=2, P=C_PREV, kvoff=_kv_own, name=nm + "attn_c_bwd", comm=comms[2])
    d_rel = _rel_reduce(jnp.transpose(_unpair_table(dbias_c), (1, 0, 2)), nm + "rel_reduce")[:, :N_REL]
    dqkv = jnp.concatenate([dqa_t.T, _unheads(dka_h), _unheads(dva_h), dqb_t.T, _unheads(dkb_h), _unheads(dvb_h),
                            dqc_t.T, _unheads(dkc_h), _unheads(dvc_h)], axis=1)
    dgf = jnp.concatenate([dgates, dfb], axis=1)
    if ride is not None:
        ride.exchanged((got_a, got_b, got_c))
    dh1a = _mm(dqkv, p["wqkv"], mode="nt", out_dtype=F32, name="proj_qkv_dx", cap_k=1024)
    dh1b = _mm(dgf, p["wgf"], mode="nt", out_dtype=F32, name="proj_gf_dx", cap_k=640)
    d_wqkv = _mm(s["h1"], dqkv, mode="tn", out_dtype=BF16, name="proj_qkv_dw", cap_m=1024, cap_k=2048)
    d_wgf = _mm(s["h1"], dgf, mode="tn", out_dtype=BF16, name="proj_gf_dw", cap_m=1024, cap_n=640, cap_k=2048)
    dx, dsc_m, dsh_m, dgn_m = _norm_mod_bwd(s["x"], [dh1a, dh1b], dx1, p["norm_mix_g"], sc_m, nm + "norm_mix_bwd")
    d_mod = jnp.concatenate([dsh_m, dsc_m, dg_m, dsh_f, dsc_f, dg_f], axis=1)[0]
    grads = dict(w_in=_unpack_w_in(d_wqkv, d_wgf), w_branch=d_wb, w_out=d_wout.reshape(4, -1, D_MODEL),
                 w_ffn_in=d_wfi, w_ffn_out=d_wfo.reshape(4, -1, D_MODEL),
                 norm_mix_g=dgn_m[0], norm_ffn_g=dgn_f[0], b_forget=db_forget[0, :N_HEADS],
                 sinks=dsink[:, 0, 0], rel_bias=d_rel, d_mod=d_mod)
    return dx, grads


def kernel(x, c, norm_mix_g, norm_ffn_g, w_ada, b_ada, w_in, b_forget, sinks, rel_bias, w_branch, w_out, w_ffn_in, w_ffn_out, final_norm_g, loss_target, m_norm_mix_g, m_norm_ffn_g, m_w_ada, m_b_ada, m_w_in, m_b_forget, m_sinks, m_rel_bias, m_w_branch, m_w_out, m_w_ffn_in, m_w_ffn_out, m_final_norm_g, v_norm_mix_g, v_norm_ffn_g, v_w_ada, v_b_ada, v_w_in, v_b_forget, v_sinks, v_rel_bias, v_w_branch, v_w_out, v_w_ffn_in, v_w_ffn_out, v_final_norm_g):
    xi, yi, ci = _coords()
    chip = 2 * xi + yi
    dev = 2 * chip + ci
    xs = x[0]
    S = xs.shape[0]
    n_ada = w_ada.shape[2]

    big_names = ("w_in", "w_branch", "w_out", "w_ffn_in", "w_ffn_out")
    big_w = dict(w_in=w_in, w_branch=w_branch, w_out=w_out, w_ffn_in=w_ffn_in, w_ffn_out=w_ffn_out)
    big_m = dict(w_in=m_w_in, w_branch=m_w_branch, w_out=m_w_out, w_ffn_in=m_w_ffn_in, w_ffn_out=m_w_ffn_out)
    big_v = dict(w_in=v_w_in, w_branch=v_w_branch, w_out=v_w_out, w_ffn_in=v_w_ffn_in, w_ffn_out=v_w_ffn_out)
    flat2 = lambda a: a.reshape(-1, a.shape[-1])
    shards = [[flat2(big_w[n][l]).astype(BF16) for n in big_names] for l in range(DEPTH)]
    gw = [[None] * (len(big_names) + 2) for _ in range(DEPTH)]
    for l in range(DEPTH):
        shards[l] += [shards[l][0][:D_MODEL // 2], shards[l][0][D_MODEL // 2:]]
    gw[0][0] = _RowHalfGather([shards[0][0]]).run("weights_gather_w_in_l0")[0]
    host_g = ((1, 2, 4), (0,), (3,))

    class WeightRide:
        def __init__(self, l, plan):
            self.l, self.plan = l, plan

        def comm_for(self, name):
            if name not in self.plan:
                return None
            lay, idx = self.plan[name]
            return _RowHalfGather([shards[lay][i] for i in idx])

        def done(self, name, got):
            lay, idx = self.plan[name]
            for i, r in zip(idx, got):
                gw[lay][i] = r

        def late_weights(self):
            g = gw[self.l]
            return dict(wb=jnp.transpose(g[1], (1, 0, 2)).reshape(3 * BRANCH_W, D_MODEL),
                        wout=g[2].reshape(D_MODEL, D_MODEL),
                        wfi=jnp.transpose(g[3], (1, 0, 2)).reshape(D_MODEL, 2 * FFN_H),
                        wfo=g[4].reshape(FFN_H, D_MODEL))

    weight_plan = [
        {"proj_qkv": (0, (1,)), "proj_gf": (0, (2,)), "attn_a": (0, (4,)), "attn_b": (0, (3,)), "attn_c": (1, (5,)),
         "ffn_in": (1, (6,))},
        {"attn_a": (1, (1, 2)), "attn_b": (1, (3,)), "attn_c": (1, (4,))}]


    c_all = _all_gather8(c.reshape(8, LANE), "gather_c").reshape(8, D_MODEL)
    b_sh = lax.dynamic_slice_in_dim(b_ada, chip * n_ada, n_ada, axis=1)[:, None, :]
    mod_sh = _ada_fwd(_pad_rows(c_all, 16), w_ada, b_sh, "ada_fwd")[:, :8, :]
    mod_all = _all_gather8(mod_sh.reshape(-1, LANE), "gather_mod").reshape(8, DEPTH, 8, n_ada)
    mod_mine = lax.dynamic_index_in_dim(mod_all[0::2], dev, axis=2, keepdims=False)
    mod = mod_mine.transpose(1, 0, 2).reshape(DEPTH, 6, D_MODEL)

    alibi = _pair_table(_alibi_table())
    no_sink = jnp.full((N_HEADS, 8, LANE), NEG_INF, F32)
    def make_params(l):
        if gw[l][0] is None:
            gw[l][0] = jnp.concatenate([gw[l][5], gw[l][6]], axis=1)
        wqkv, wgf = _pack_w_in(gw[l][0])
        rel_tab = _rel_expand(jnp.pad(rel_bias[l], ((0, 0), (0, N_REL_PAD - N_REL))), "l%d_rel_expand" % l)
        return dict(
            wqkv=wqkv, wgf=wgf, norm_mix_g=norm_mix_g[l][None], norm_ffn_g=norm_ffn_g[l][None],
            b_forget_pad=jnp.pad(b_forget[l], (0, LANE - N_HEADS))[None],
            sink_tab=jnp.broadcast_to(sinks[l][:, None, None], (N_HEADS, 8, LANE)),
            no_sink=no_sink, alibi=alibi, rel_tab=_pair_table(jnp.transpose(rel_tab, (1, 0, 2))))

    mods = [[mod[l, k][None] for k in range(6)] for l in range(DEPTH)]
    params, saved = [None] * DEPTH, [None] * DEPTH
    h = xs
    for l in range(DEPTH):
        h, saved[l], params[l] = _layer_fwd(h, mods[l], make_params(l), l, WeightRide(l, weight_plan[l]))
    loss_dev, dh, d_final = _final_loss(h, final_norm_g[None], loss_target[0], "final_loss")
    grads = [None] * DEPTH
    dh, grads[1] = _layer_bwd(dh, mods[1], params[1], saved[1], 1)

    class Layer1Ride:
        sends = {"ffn_out_dx": (4,), "ffn_in_dx": (3, 1, 2), "ffn_in_dw": (0,)}
        hands = {"proj_qkv_dx": (0,), "proj_gf_dx": (3,), "proj_gf_dw": (4, 1, 2)}

        def __init__(self, g):
            self.g, self.t = g, [None] * len(g)
            self.parts, self.final = [None] * len(g), [None] * len(g)

        def comm_for(self, name):
            if name in self.sends:
                return _SiblingSend([self.g[i] for i in self.sends[name]], 0)
            if name in self.hands:
                return _Handoff([self.parts[i] for i in self.hands[name]], 1, (0, 1, 2, 3))
            return None

        def done(self, name, got):
            idx, dst = (self.sends[name], self.t) if name in self.sends else (self.hands[name], self.final)
            for i, r in zip(idx, got):
                dst[i] = r

        def exchanges(self):
            sums = [_add_cast_on(a, b, 1, "grads_chip_sum_l1_" + n) for n, a, b in zip(big_names, self.g, self.t)]
            return tuple(_OwnerReduce([sums[i] for i in idx], 1) for idx in host_g)

        def exchanged(self, got):
            for res, idx in zip(got, host_g):
                for r, i in zip(res, idx):
                    self.parts[i] = r

    ride = Layer1Ride([grads[1][n] for n in big_names])
    dh, grads[0] = _layer_bwd(dh, mods[0], params[0], saved[0], 0, ride)
    grad_x = dh[None]
    loss = lax.psum(loss_dev[0, 0], ("x", "y", "c"))
    parts1 = ride.final
    g0 = [grads[0][n] for n in big_names]
    t0 = _sibling_swap_rows(g0, "grads_swap_l0")
    sums0 = [_add_cast_rows(a, b, "grads_chip_sum_l0_" + n) for n, a, b in zip(big_names, g0, t0)]
    parts0 = [None] + list(_RowHalfReduce(sums0[1:]).run("grads_reduce_l0"))

    small_names = ("norm_mix_g", "norm_ffn_g", "b_ada", "b_forget", "sinks", "rel_bias", "final_norm_g")
    small_w = dict(norm_mix_g=norm_mix_g, norm_ffn_g=norm_ffn_g, b_ada=b_ada, b_forget=b_forget, sinks=sinks,
                   rel_bias=rel_bias, final_norm_g=final_norm_g)
    small_m = dict(norm_mix_g=m_norm_mix_g, norm_ffn_g=m_norm_ffn_g, b_ada=m_b_ada, b_forget=m_b_forget,
                   sinks=m_sinks, rel_bias=m_rel_bias, final_norm_g=m_final_norm_g)
    small_v = dict(norm_mix_g=v_norm_mix_g, norm_ffn_g=v_norm_ffn_g, b_ada=v_b_ada, b_forget=v_b_forget,
                   sinks=v_sinks, rel_bias=v_rel_bias, final_norm_g=v_final_norm_g)
    small_g = dict(
        norm_mix_g=jnp.stack([grads[l]["norm_mix_g"] for l in range(DEPTH)]),
        norm_ffn_g=jnp.stack([grads[l]["norm_ffn_g"] for l in range(DEPTH)]),
        b_ada=jnp.stack([grads[l]["d_mod"] for l in range(DEPTH)]),
        b_forget=jnp.stack([grads[l]["b_forget"] for l in range(DEPTH)]),
        sinks=jnp.stack([grads[l]["sinks"] for l in range(DEPTH)]),
        rel_bias=jnp.stack([grads[l]["rel_bias"] for l in range(DEPTH)]),
        final_norm_g=d_final[0])
    shapes = [small_w[n].shape for n in small_names]
    g_all = _all_gather8(_small_pack([small_g[n] for n in small_names]), "gather_small_grads")
    res = _adamw(_small_pack([small_w[n] for n in small_names])[None],
                    _small_pack([small_m[n] for n in small_names])[None],
                    _small_pack([small_v[n] for n in small_names])[None], g_all, "adamw_small")
    small_out = {n: [] for n in small_names}
    for r in res:
        for n, a in zip(small_names, _small_unpack(r[0], shapes)):
            small_out[n].append(a)
    off_b = sum(int(np.prod(s)) for s in shapes[:2])
    n_mod = DEPTH * 6 * D_MODEL
    dmod_all = g_all.reshape(8, -1)[:, off_b:off_b + n_mod].reshape(8, DEPTH, 6 * D_MODEL)
    dmod_sh = lax.dynamic_slice_in_dim(dmod_all, chip * n_ada, n_ada, axis=2).transpose(1, 0, 2)
    g_ada, got = _ada_bwd(c_all.T, dmod_sh, "ada_bwd", comm=_RowHalfReduce(sums0[:1]))
    parts0[0] = got[0]
    ada_out = _adamw(w_ada, m_w_ada, v_w_ada, flat2(g_ada)[None], "adamw_w_ada")

    big_out = {}
    as3 = lambda a: a.reshape(a.shape[0], -1, a.shape[-1])
    for n, p0, p1 in zip(big_names, parts0, parts1):
        res = _adamw(as3(big_w[n]), as3(big_m[n]), as3(big_v[n]), [p0, p1], "adamw_" + n)
        big_out[n] = [r.reshape(big_w[n].shape) for r in res]

    order = ("norm_mix_g", "norm_ffn_g", "w_ada", "b_ada", "w_in", "b_forget", "sinks", "rel_bias", "w_branch",
             "w_out", "w_ffn_in", "w_ffn_out", "final_norm_g")

    def pick(n, k):
        if n == "w_ada":
            return ada_out[k]
        if n in big_out:
            return big_out[n][k]
        return small_out[n][k]

    outs = [loss, grad_x]
    for k in range(4):
        outs += [pick(n, k) for n in order]
    return tuple(outs)
```

```python
import numpy as np
import jax
import jax.numpy as jnp
from jax import lax
from jax.experimental import pallas as pl
from jax.experimental.pallas import tpu as pltpu

F32 = jnp.float32
BF16 = jnp.bfloat16
SDS = jax.ShapeDtypeStruct

D_MODEL = 1024
DEPTH = 2
CHUNK = 64
HEAD_DIM = 64
EPS = 1e-6
NEG_INF = -1e30
N_HEADS = 8
A_KV_HEADS = 2
A_PREV = 2
C_PREV = 8
REL_CLIP = 128
N_REL = 2 * REL_CLIP + 1
N_REL_PAD = 384
BRANCH_W = 512
FFN_H = 2816
FOX_BQ = 512
FOX_BK = 512
GF_COLS = 3200
N_IN_COLS = 6920
LANE = 128
VMEM_LIMIT = 48 * 1024 * 1024

ADAM_LR = 0.001
ADAM_B1 = 0.9
ADAM_B2 = 0.999
ADAM_EPS = 1e-08
ADAM_WD = 0.01
ADAM_STEP = 10

MESH = pl.DeviceIdType.MESH
ANY = pl.BlockSpec(memory_space=pl.ANY)
VMEM_SPEC = pl.BlockSpec(memory_space=pltpu.VMEM)


def _cparams(sem=None):
    return pltpu.CompilerParams(dimension_semantics=sem, vmem_limit_bytes=VMEM_LIMIT)


def _blk(n, cap):
    if n <= cap:
        return n
    best = None
    for m in range(LANE, cap + 1, LANE):
        if n % m == 0:
            best = m
    assert best is not None, (n, cap)
    return best


def _sigmoid(x):
    return 1.0 / (1.0 + jnp.exp(-x))


def _mm(a, b, *, mode, out_dtype, name, groups=1, cap_m=2048, cap_n=1024, cap_k=1408, col_quarters=False,
        comm=None):
    G = groups
    assert not col_quarters or mode == "tn"
    if mode == "nn":
        M, K, N = a.shape[0], a.shape[1] // G, b.shape[1]
        assert b.shape[0] == G * K
    elif mode == "nt":
        M, K, N = a.shape[0], a.shape[1] // G, b.shape[0] // G
        assert b.shape[1] == K
    else:
        K, M, N = a.shape[0], a.shape[1] // G, b.shape[1] // G
        assert b.shape[0] == K
    bm, bn, bk = _blk(M, cap_m), _blk(N // 4 if col_quarters else N, cap_n), _blk(K, cap_k)
    nm, nn, nk = M // bm, N // bn, K // bk
    if mode == "nn":
        a_spec = pl.BlockSpec((bm, bk), lambda g, i, j, k: (i, g * nk + k))
        b_spec = pl.BlockSpec((bk, bn), lambda g, i, j, k: (g * nk + k, j))
        o_spec = pl.BlockSpec((bm, bn), lambda g, i, j, k: (i, g * nn + j))
        dims = (((1,), (0,)), ((), ()))
        out_shape = (M, G * N)
    elif mode == "nt":
        a_spec = pl.BlockSpec((bm, bk), lambda g, i, j, k: (i, g * nk + k))
        b_spec = pl.BlockSpec((bn, bk), lambda g, i, j, k: (g * nn + j, k))
        o_spec = pl.BlockSpec((bm, bn), lambda g, i, j, k: (i, g * nn + j))
        dims = (((1,), (1,)), ((), ()))
        out_shape = (M, G * N)
    else:
        a_spec = pl.BlockSpec((bk, bm), lambda g, i, j, k: (k, g * nm + i))
        b_spec = pl.BlockSpec((bk, bn), lambda g, i, j, k: (k, g * nn + j))
        dims = (((0,), (0,)), ((), ()))
        if col_quarters:
            nq = nn // 4
            o_spec = pl.BlockSpec((1, bm, bn), lambda g, i, j, k: (j // nq, g * nm + i, j % nq))
            out_shape = (4, G * M, N // 4)
        else:
            o_spec = pl.BlockSpec((bm, bn), lambda g, i, j, k: (g * nm + i, j))
            out_shape = (G * M, N)

    def product(a_ref, b_ref):
        return lax.dot_general(a_ref[...].astype(BF16), b_ref[...].astype(BF16), dims, preferred_element_type=F32)

    def body_one(a_ref, b_ref, o_ref):
        o_ref[...] = product(a_ref, b_ref).astype(o_ref.dtype).reshape(o_ref.shape)

    def body_acc(a_ref, b_ref, o_ref, acc_ref):
        k = pl.program_id(3)

        @pl.when(k == 0)
        def _():
            acc_ref[...] = jnp.zeros_like(acc_ref)

        acc_ref[...] += product(a_ref, b_ref)

        @pl.when(k == nk - 1)
        def _():
            o_ref[...] = acc_ref[...].astype(o_ref.dtype).reshape(o_ref.shape)

    res, got = _call_hosting(
        body_one if nk == 1 else body_acc, comm=comm, grid=(G, nm, nn, nk), in_specs=[a_spec, b_spec],
        out_specs=[o_spec], out_shape=[SDS(out_shape, out_dtype)],
        scratch_shapes=[] if nk == 1 else [pltpu.VMEM((bm, bn), F32)], name=name, args=(a, b),
        semantics=("parallel", "parallel", "parallel", "arbitrary"))
    return res[0] if comm is None else (res[0], got)


def _rows(tm, n, col=0):
    return pl.BlockSpec((tm, n), lambda i: (i, col))


def _vec(n):
    return pl.BlockSpec((1, n), lambda i: (0, 0))


def _tm(S):
    return min(S, 256)


def _norm_mod_fwd(x, g, sc, sh, name):
    S, Dm = x.shape
    tm = _tm(S)

    def body(x_ref, g_ref, sc_ref, sh_ref, h_ref):
        xv = x_ref[...]
        r = lax.rsqrt(jnp.mean(xv * xv, axis=-1, keepdims=True) + EPS)
        h_ref[...] = ((xv * r) * g_ref[...] * (1.0 + sc_ref[...]) + sh_ref[...]).astype(h_ref.dtype)

    return pl.pallas_call(
        body, grid=(S // tm,), in_specs=[_rows(tm, Dm), _vec(Dm), _vec(Dm), _vec(Dm)],
        out_specs=_rows(tm, Dm), out_shape=SDS((S, Dm), BF16),
        compiler_params=_cparams(("parallel",)), name=name)(x, g, sc, sh)


def _norm_mod_bwd(x, dh_list, dres, g, sc, name):
    S, Dm = x.shape
    tm = _tm(S)
    nh = len(dh_list)

    def body(*refs):
        x_ref = refs[0]
        dh_refs = refs[1:1 + nh]
        dres_ref, g_ref, sc_ref, dx_ref, dsc_ref, dsh_ref, dg_ref = refs[1 + nh:]
        i = pl.program_id(0)

        @pl.when(i == 0)
        def _():
            dsc_ref[...] = jnp.zeros_like(dsc_ref)
            dsh_ref[...] = jnp.zeros_like(dsh_ref)
            dg_ref[...] = jnp.zeros_like(dg_ref)

        xv = x_ref[...]
        dh = dh_refs[0][...].astype(F32)
        for r_ in dh_refs[1:]:
            dh = dh + r_[...].astype(F32)
        gv = g_ref[...]
        r = lax.rsqrt(jnp.mean(xv * xv, axis=-1, keepdims=True) + EPS)
        xn = xv * r
        xg = xn * gv
        dsh_ref[...] += jnp.sum(dh, axis=0, keepdims=True)
        dsc_ref[...] += jnp.sum(dh * xg, axis=0, keepdims=True)
        dxg = dh * (1.0 + sc_ref[...])
        dg_ref[...] += jnp.sum(dxg * xn, axis=0, keepdims=True)
        dxn = dxg * gv
        dx_ref[...] = dres_ref[...] + r * (dxn - xn * jnp.mean(dxn * xn, axis=-1, keepdims=True))

    return pl.pallas_call(
        body, grid=(S // tm,),
        in_specs=[_rows(tm, Dm)] * (2 + nh) + [_vec(Dm), _vec(Dm)],
        out_specs=[_rows(tm, Dm), _vec(Dm), _vec(Dm), _vec(Dm)],
        out_shape=[SDS((S, Dm), F32), SDS((1, Dm), F32), SDS((1, Dm), F32), SDS((1, Dm), F32)],
        compiler_params=_cparams(("arbitrary",)), name=name)(x, *dh_list, dres, g, sc)


def _resid_fwd(x, val, g, name):
    S, Dm = x.shape
    tm = _tm(S)

    def body(x_ref, v_ref, g_ref, o_ref):
        o_ref[...] = x_ref[...] + g_ref[...] * v_ref[...].astype(F32)

    return pl.pallas_call(
        body, grid=(S // tm,), in_specs=[_rows(tm, Dm), _rows(tm, Dm), _vec(Dm)],
        out_specs=_rows(tm, Dm), out_shape=SDS((S, Dm), F32),
        compiler_params=_cparams(("parallel",)), name=name)(x, val, g)


def _resid_bwd(dx, val, g, name):
    S, Dm = dx.shape
    tm = _tm(S)

    def body(dx_ref, v_ref, g_ref, dg_ref, dv_ref):
        @pl.when(pl.program_id(0) == 0)
        def _():
            dg_ref[...] = jnp.zeros_like(dg_ref)

        dxv = dx_ref[...]
        dg_ref[...] += jnp.sum(dxv * v_ref[...].astype(F32), axis=0, keepdims=True)
        dv_ref[...] = (dxv * g_ref[...]).astype(dv_ref.dtype)

    return pl.pallas_call(
        body, grid=(S // tm,), in_specs=[_rows(tm, Dm), _rows(tm, Dm), _vec(Dm)],
        out_specs=[_vec(Dm), _rows(tm, Dm)], out_shape=[SDS((1, Dm), F32), SDS((S, Dm), BF16)],
        compiler_params=_cparams(("arbitrary",)), name=name)(dx, val, g)


def _merge_fwd(y, gf, name):
    S = y.shape[0]
    tm = _tm(S)
    W = 3 * D_MODEL

    def body(y_ref, g_ref, o_ref):
        acc = None
        for k in range(3):
            sl = slice(k * D_MODEL, (k + 1) * D_MODEL)
            t = _sigmoid(g_ref[:, sl]) * y_ref[:, sl].astype(F32)
            acc = t if acc is None else acc + t
        o_ref[...] = acc.astype(o_ref.dtype)

    return pl.pallas_call(
        body, grid=(S // tm,), in_specs=[_rows(tm, W), _rows(tm, W)],
        out_specs=_rows(tm, D_MODEL), out_shape=SDS((S, D_MODEL), BF16),
        compiler_params=_cparams(("parallel",)), name=name)(y, gf)


def _merge_bwd(dm, y, gf, name):
    S = y.shape[0]
    tm = _tm(S)
    W = 3 * D_MODEL

    def body(dm_ref, y_ref, g_ref, dy_ref, dg_ref):
        dmv = dm_ref[...].astype(F32)
        for k in range(3):
            sl = slice(k * D_MODEL, (k + 1) * D_MODEL)
            sg = _sigmoid(g_ref[:, sl])
            dy_ref[:, sl] = (dmv * sg).astype(dy_ref.dtype)
            dg_ref[:, sl] = (dmv * y_ref[:, sl].astype(F32) * (sg * (1.0 - sg))).astype(dg_ref.dtype)

    return pl.pallas_call(
        body, grid=(S // tm,), in_specs=[_rows(tm, D_MODEL), _rows(tm, W), _rows(tm, W)],
        out_specs=[_rows(tm, W), _rows(tm, W)], out_shape=[SDS((S, W), BF16), SDS((S, W), BF16)],
        compiler_params=_cparams(("parallel",)), name=name)(dm, y, gf)


def _swiglu_fwd(u, name):
    S = u.shape[0]
    tm = _tm(S)

    def body(g_ref, u_ref, a_ref):
        gv = g_ref[...].astype(F32)
        a_ref[...] = (gv * _sigmoid(gv) * u_ref[...].astype(F32)).astype(a_ref.dtype)

    return pl.pallas_call(
        body, grid=(S // tm,), in_specs=[_rows(tm, FFN_H, 0), _rows(tm, FFN_H, 1)],
        out_specs=_rows(tm, FFN_H), out_shape=SDS((S, FFN_H), BF16),
        compiler_params=_cparams(("parallel",)), name=name)(u, u)


def _swiglu_bwd(da, u, name):
    S = u.shape[0]
    tm = _tm(S)

    def body(da_ref, g_ref, u_ref, du_ref):
        dav = da_ref[...].astype(F32)
        gv = g_ref[...].astype(F32)
        sg = _sigmoid(gv)
        du_ref[:, 0:FFN_H] = (dav * u_ref[...].astype(F32) * (sg * (1.0 + gv * (1.0 - sg)))).astype(du_ref.dtype)
        du_ref[:, FFN_H:2 * FFN_H] = (dav * (gv * sg)).astype(du_ref.dtype)

    return pl.pallas_call(
        body, grid=(S // tm,), in_specs=[_rows(tm, FFN_H), _rows(tm, FFN_H, 0), _rows(tm, FFN_H, 1)],
        out_specs=_rows(tm, 2 * FFN_H), out_shape=SDS((S, 2 * FFN_H), BF16),
        compiler_params=_cparams(("parallel",)), name=name)(da, u, u)


def _final_loss(x, g, target, name):
    S, Dm = x.shape
    tm = _tm(S)

    def body(x_ref, g_ref, t_ref, loss_ref, dx_ref, dg_ref):
        @pl.when(pl.program_id(0) == 0)
        def _():
            loss_ref[...] = jnp.zeros_like(loss_ref)
            dg_ref[...] = jnp.zeros_like(dg_ref)

        xv = x_ref[...]
        gv = g_ref[...]
        r = lax.rsqrt(jnp.mean(xv * xv, axis=-1, keepdims=True) + EPS)
        xn = xv * r
        err = xn * gv - t_ref[...]
        row = jnp.mean(err * err, axis=-1, keepdims=True)
        loss_ref[...] += 0.5 * jnp.sum(row, axis=0, keepdims=True)
        dy = err * (1.0 / Dm)
        dg_ref[...] += jnp.sum(dy * xn, axis=0, keepdims=True)
        dxn = dy * gv
        dx_ref[...] = r * (dxn - xn * jnp.mean(dxn * xn, axis=-1, keepdims=True))

    return pl.pallas_call(
        body, grid=(S // tm,), in_specs=[_rows(tm, Dm), _vec(Dm), _rows(tm, Dm)],
        out_specs=[pl.BlockSpec((1, 1), lambda i: (0, 0)), _rows(tm, Dm), _vec(Dm)],
        out_shape=[SDS((1, 1), F32), SDS((S, Dm), F32), SDS((1, Dm), F32)],
        compiler_params=_cparams(("arbitrary",)), name=name)(x, g, target)


PAIR = 2 * CHUNK


def _bandT_softmax(kg, qTg, bias, sink, valid):
    s = jnp.dot(kg, qTg, preferred_element_type=F32)
    s = jnp.where(valid, s + bias, NEG_INF)
    m = jnp.maximum(jnp.max(s, axis=0, keepdims=True), sink)
    e = jnp.exp(s - m)
    es = jnp.exp(sink - m)
    inv = 1.0 / (jnp.sum(e, axis=0, keepdims=True) + es)
    return e * inv, es * inv


def _pad_copy_rows(dst, src, pad, S):
    dst[:, 0:pad, :] = jnp.zeros((dst.shape[0], pad, dst.shape[2]), dst.dtype)
    dst[:, pad:pad + S, :] = src[...]


def _pad_copy_lanes(dst, src, pad, S):
    dst[:, 0:pad] = jnp.zeros((dst.shape[0], pad), dst.dtype)
    dst[:, pad:pad + S] = src[...]


def _fm(arg):
    return arg if isinstance(arg, tuple) else (arg, 0)


def _fm_spec(rows, S, row0):
    off, rem = divmod(row0, rows)
    assert rem == 0
    return pl.BlockSpec((rows, S), lambda i: (off + i, 0))


def _bandT_fwd(qT, k_h, vT, bias, sink, *, GQ, GK, P, kvoff, name, comm=None):
    (qT, q0), (vT, v0) = _fm(qT), _fm(vT)
    S = qT.shape[1]
    ng = bias.shape[0] // GQ
    BU = (P + 2) * CHUNK
    pad = P * CHUNK
    npair = S // PAIR

    def body(qT_ref, k_ref, vT_ref, b_ref, s_ref, oT_ref, kp, vTp):
        _pad_copy_rows(kp, k_ref, pad, S)
        _pad_copy_lanes(vTp, vT_ref, pad, S)
        rowi = lax.broadcasted_iota(jnp.int32, (BU, PAIR), 0)

        def step(n2, carry):
            r = pl.multiple_of(n2 * PAIR, PAIR)
            valid = rowi >= (P - 2 * n2) * CHUNK
            for g in range(GQ):
                kv = kvoff(g)
                hs = slice(g * HEAD_DIM, (g + 1) * HEAD_DIM)
                kvs = slice(kv * HEAD_DIM, (kv + 1) * HEAD_DIM)
                qTg = qT_ref[hs, pl.ds(r, PAIR)] * 0.125
                p, _ = _bandT_softmax(kp[kv, pl.ds(r, BU), :], qTg, b_ref[g], s_ref[g, 0:1, :], valid)
                oTg = jnp.dot(vTp[kvs, pl.ds(r, BU)], p.astype(BF16), preferred_element_type=F32)
                oT_ref[hs, pl.ds(r, PAIR)] = oTg.astype(oT_ref.dtype)
            return carry

        lax.fori_loop(0, npair, step, 0, unroll=min(2, npair))

    res, got = _call_hosting(
        body, comm=comm, grid=(ng,),
        in_specs=[_fm_spec(GQ * HEAD_DIM, S, q0),
                  pl.BlockSpec((GK, S, HEAD_DIM), lambda i: (i, 0, 0)),
                  _fm_spec(GK * HEAD_DIM, S, v0),
                  pl.BlockSpec((GQ, BU, PAIR), lambda i: (i, 0, 0)),
                  pl.BlockSpec((GQ, 8, LANE), lambda i: (i, 0, 0))],
        out_specs=[pl.BlockSpec((GQ * HEAD_DIM, S), lambda i: (i, 0))],
        out_shape=[SDS((ng * GQ * HEAD_DIM, S), BF16)],
        scratch_shapes=[pltpu.VMEM((GK, S + pad, HEAD_DIM), BF16), pltpu.VMEM((GK * HEAD_DIM, S + pad), BF16)],
        name=name, args=(qT, k_h, vT, bias, sink))
    return res[0], got


def _bandT_bwd(qT, q_h, k_h, kT, v_h, doT, do_h, bias, sink, *, GQ, GK, P, kvoff, name, comm=None):
    (qT, q0), (kT, k0), (doT, d0) = _fm(qT), _fm(kT), _fm(doT)
    S = qT.shape[1]
    ng = bias.shape[0] // GQ
    BU = (P + 2) * CHUNK
    pad = P * CHUNK
    npair = S // PAIR

    def body(qT_ref, q_ref, k_ref, kT_ref, v_ref, doT_ref, do_ref, b_ref, s_ref,
             dqT_ref, dk_ref, dv_ref, db_ref, dsk_ref, kp, kTp, vp, dkp, dvp):
        _pad_copy_rows(kp, k_ref, pad, S)
        _pad_copy_rows(vp, v_ref, pad, S)
        _pad_copy_lanes(kTp, kT_ref, pad, S)
        dkp[...] = jnp.zeros_like(dkp)
        dvp[...] = jnp.zeros_like(dvp)
        db_ref[...] = jnp.zeros_like(db_ref)
        rowi = lax.broadcasted_iota(jnp.int32, (BU, PAIR), 0)

        def step(n2, dsink):
            r = pl.multiple_of(n2 * PAIR, PAIR)
            valid = rowi >= (P - 2 * n2) * CHUNK
            new = []
            for g in range(GQ):
                kv = kvoff(g)
                hs = slice(g * HEAD_DIM, (g + 1) * HEAD_DIM)
                kvs = slice(kv * HEAD_DIM, (kv + 1) * HEAD_DIM)
                qTg = qT_ref[hs, pl.ds(r, PAIR)] * 0.125
                p, ps = _bandT_softmax(kp[kv, pl.ds(r, BU), :], qTg, b_ref[g], s_ref[g, 0:1, :], valid)
                dp = jnp.dot(vp[kv, pl.ds(r, BU), :], doT_ref[hs, pl.ds(r, PAIR)], preferred_element_type=F32)
                delta = jnp.sum(p * dp, axis=0, keepdims=True)
                ds = p * (dp - delta)
                new.append(dsink[g] - ps * delta)
                db_ref[g] += ds
                dsb = ds.astype(BF16)
                dq = jnp.dot(kTp[kvs, pl.ds(r, BU)], dsb, preferred_element_type=F32) * 0.125
                dqT_ref[hs, pl.ds(r, PAIR)] = dq.astype(dqT_ref.dtype)
                dkp[kv, pl.ds(r, BU), :] += jnp.dot(dsb, q_ref[g, pl.ds(r, PAIR), :] * 0.125,
                                                    preferred_element_type=F32)
                dvp[kv, pl.ds(r, BU), :] += jnp.dot(p.astype(BF16), do_ref[g, pl.ds(r, PAIR), :],
                                                    preferred_element_type=F32)
            return tuple(new)

        dsink = lax.fori_loop(0, npair, step, tuple(jnp.zeros((1, PAIR), F32) for _ in range(GQ)))
        for g in range(GQ):
            dsk_ref[g] = jnp.broadcast_to(jnp.sum(dsink[g], axis=1, keepdims=True), (8, LANE))
        dk_ref[...] = dkp[:, pad:pad + S, :].astype(dk_ref.dtype)
        dv_ref[...] = dvp[:, pad:pad + S, :].astype(dv_ref.dtype)

    qTs = pl.BlockSpec((GQ * HEAD_DIM, S), lambda i: (i, 0))
    qhs = pl.BlockSpec((GQ, S, HEAD_DIM), lambda i: (i, 0, 0))
    khs = pl.BlockSpec((GK, S, HEAD_DIM), lambda i: (i, 0, 0))
    bs = pl.BlockSpec((GQ, BU, PAIR), lambda i: (i, 0, 0))
    ss = pl.BlockSpec((GQ, 8, LANE), lambda i: (i, 0, 0))
    nkv = ng * GK
    return _call_hosting(
        body, comm=comm, grid=(ng,),
        in_specs=[_fm_spec(GQ * HEAD_DIM, S, q0), qhs, khs, _fm_spec(GK * HEAD_DIM, S, k0), khs,
                  _fm_spec(GQ * HEAD_DIM, S, d0), qhs, bs, ss],
        out_specs=[qTs, khs, khs, bs, ss],
        out_shape=[SDS((ng * GQ * HEAD_DIM, S), BF16), SDS((nkv, S, HEAD_DIM), BF16), SDS((nkv, S, HEAD_DIM), BF16),
                   SDS((ng * GQ, BU, PAIR), F32), SDS((ng * GQ, 8, LANE), F32)],
        scratch_shapes=[pltpu.VMEM((GK, S + pad, HEAD_DIM), BF16), pltpu.VMEM((GK * HEAD_DIM, S + pad), BF16),
                        pltpu.VMEM((GK, S + pad, HEAD_DIM), BF16),
                        pltpu.VMEM((GK, S + pad, HEAD_DIM), F32), pltpu.VMEM((GK, S + pad, HEAD_DIM), F32)],
        name=name, args=(qT, q_h, k_h, kT, v_h, doT, do_h, bias, sink))


def _pair_table(tab):
    t = jnp.transpose(tab, (0, 2, 1))
    lo = jnp.pad(t, ((0, 0), (0, CHUNK), (0, 0)), constant_values=NEG_INF)
    hi = jnp.pad(t, ((0, 0), (CHUNK, 0), (0, 0)), constant_values=NEG_INF)
    return jnp.concatenate([lo, hi], axis=2)


def _unpair_table(d):
    band = d.shape[1] - CHUNK
    return jnp.transpose(d[:, 0:band, 0:CHUNK] + d[:, CHUNK:CHUNK + band, CHUNK:PAIR], (0, 2, 1))


def _heads(a, n):
    return jnp.transpose(a.reshape(a.shape[0], n, HEAD_DIM), (1, 0, 2))


def _unheads(a):
    return jnp.transpose(a, (1, 0, 2)).reshape(a.shape[1], a.shape[0] * HEAD_DIM)


def _foxT_logits(kj, qTg, cq, ck, r, c, rowi, coli):
    s = jnp.dot(kj, qTg, preferred_element_type=F32)
    s = s + cq - ck
    return jnp.where(c + rowi <= r + coli, s, NEG_INF)


def _foxT_fwd(qT, k_h, vT, ck, cq, name, comm=None):
    (qT, q0), (vT, v0) = _fm(qT), _fm(vT)
    S = qT.shape[1]
    npair = k_h.shape[0] // 2
    BQ, BK = min(FOX_BQ, S), min(FOX_BK, S)
    nq = S // BQ
    heads = [slice(g * HEAD_DIM, (g + 1) * HEAD_DIM) for g in range(2)]

    def body(qT_ref, k_ref, vT_ref, ck_ref, cq_ref, oT_ref, lse_ref):
        rowi = lax.broadcasted_iota(jnp.int32, (BK, BQ), 0)
        coli = lax.broadcasted_iota(jnp.int32, (BK, BQ), 1)

        def qstep(i, carry):
            r = pl.multiple_of(i * BQ, BQ)
            qs = [qT_ref[hs, pl.ds(r, BQ)] * 0.125 for hs in heads]
            cqs = [cq_ref[g, :, pl.ds(r, BQ)] for g in range(2)]

            def kstep(j, st):
                c = pl.multiple_of(j * BK, BK)
                new = []
                for g, hs in enumerate(heads):
                    m, l, acc = st[g]
                    s = _foxT_logits(k_ref[g, pl.ds(c, BK), :], qs[g], cqs[g], ck_ref[g, pl.ds(c, BK), :],
                                     r, c, rowi, coli)
                    mn = jnp.maximum(m, jnp.max(s, axis=0, keepdims=True))
                    al = jnp.exp(m - mn)
                    e = jnp.exp(s - mn)
                    l = al * l + jnp.sum(e, axis=0, keepdims=True)
                    acc = al * acc + jnp.dot(vT_ref[hs, pl.ds(c, BK)], e.astype(BF16), preferred_element_type=F32)
                    new.append((mn, l, acc))
                return tuple(new)

            init = (jnp.full((1, BQ), NEG_INF, F32), jnp.zeros((1, BQ), F32), jnp.zeros((HEAD_DIM, BQ), F32))
            st = lax.fori_loop(0, (r + BQ + BK - 1) // BK, kstep, (init, init))
            for g, hs in enumerate(heads):
                m, l, acc = st[g]
                oT_ref[hs, pl.ds(r, BQ)] = (acc * (1.0 / l)).astype(oT_ref.dtype)
                lse_ref[g, :, pl.ds(r, BQ)] = m + jnp.log(l)
            return carry

        lax.fori_loop(0, nq, qstep, 0)

    fT = pl.BlockSpec((LANE, S), lambda i: (i, 0))
    hm = pl.BlockSpec((2, S, HEAD_DIM), lambda i: (i, 0, 0))
    col = pl.BlockSpec((2, S, 1), lambda i: (i, 0, 0))
    rw = pl.BlockSpec((2, 1, S), lambda i: (i, 0, 0))
    return _call_hosting(
        body, comm=comm, grid=(npair,), in_specs=[_fm_spec(LANE, S, q0), hm, _fm_spec(LANE, S, v0), col, rw],
        out_specs=[fT, rw],
        out_shape=[SDS((npair * LANE, S), BF16), SDS((2 * npair, 1, S), F32)], scratch_shapes=[],
        name=name, args=(qT, k_h, vT, ck, cq))


def _foxT_bwd(qT, q_h, k_h, kT, v_h, ck, cq, oT, doT, do_h, lse, name, comm=None):
    (qT, q0), (kT, k0), (doT, d0) = _fm(qT), _fm(kT), _fm(doT)
    S = qT.shape[1]
    npair = k_h.shape[0] // 2
    BQ, BK = min(FOX_BQ, S), min(FOX_BK, S)
    nq = S // BQ
    heads = [slice(g * HEAD_DIM, (g + 1) * HEAD_DIM) for g in range(2)]

    def body(qT_ref, q_ref, k_ref, kT_ref, v_ref, ck_ref, cq_ref, oT_ref, doT_ref, do_ref, lse_ref,
             dqT_ref, dk_ref, dv_ref, dck_ref, dcq_ref, dka, dva, qa_ref):
        qa_ref[:, :, 0:HEAD_DIM] = q_ref[...] * 0.125
        qa_ref[:, :, HEAD_DIM:LANE] = jnp.ones((2, S, LANE - HEAD_DIM), BF16)
        dka[...] = jnp.zeros_like(dka)
        dva[...] = jnp.zeros_like(dva)
        rowi = lax.broadcasted_iota(jnp.int32, (BK, BQ), 0)
        coli = lax.broadcasted_iota(jnp.int32, (BK, BQ), 1)

        def qstep(i, carry):
            r = pl.multiple_of(i * BQ, BQ)
            qs = [qT_ref[hs, pl.ds(r, BQ)] * 0.125 for hs in heads]
            dos = [doT_ref[hs, pl.ds(r, BQ)] for hs in heads]
            deltas = [jnp.sum(dos[g].astype(F32) * oT_ref[hs, pl.ds(r, BQ)].astype(F32), axis=0, keepdims=True)
                      for g, hs in enumerate(heads)]
            cqs = [cq_ref[g, :, pl.ds(r, BQ)] for g in range(2)]
            lses = [lse_ref[g, :, pl.ds(r, BQ)] for g in range(2)]

            def kstep(j, st):
                c = pl.multiple_of(j * BK, BK)
                new = []
                for g, hs in enumerate(heads):
                    dq, rs = st[g]
                    s = _foxT_logits(k_ref[g, pl.ds(c, BK), :], qs[g], cqs[g], ck_ref[g, pl.ds(c, BK), :],
                                     r, c, rowi, coli)
                    p = jnp.exp(s - lses[g])
                    dp = jnp.dot(v_ref[g, pl.ds(c, BK), :], dos[g], preferred_element_type=F32)
                    ds = p * (dp - deltas[g])
                    dsb = ds.astype(BF16)
                    dka[g, pl.ds(c, BK), :] += jnp.dot(dsb, qa_ref[g, pl.ds(r, BQ), :], preferred_element_type=F32)
                    dva[g, pl.ds(c, BK), :] += jnp.dot(p.astype(BF16), do_ref[g, pl.ds(r, BQ), :],
                                                      preferred_element_type=F32)
                    new.append((dq + jnp.dot(kT_ref[hs, pl.ds(c, BK)], dsb, preferred_element_type=F32),
                                rs + jnp.sum(dsb.astype(F32), axis=0, keepdims=True)))
                return tuple(new)

            init = (jnp.zeros((HEAD_DIM, BQ), F32), jnp.zeros((1, BQ), F32))
            st = lax.fori_loop(0, (r + BQ + BK - 1) // BK, kstep, (init, init))
            for g, hs in enumerate(heads):
                dqT_ref[hs, pl.ds(r, BQ)] = (st[g][0] * 0.125).astype(dqT_ref.dtype)
                dcq_ref[g, :, pl.ds(r, BQ)] = st[g][1]
            return carry

        lax.fori_loop(0, nq, qstep, 0)
        dk_ref[...] = dka[:, :, 0:HEAD_DIM].astype(dk_ref.dtype)
        dck_ref[...] = -dka[:, :, HEAD_DIM:HEAD_DIM + 1]
        dv_ref[...] = dva[...].astype(dv_ref.dtype)

    fT = pl.BlockSpec((LANE, S), lambda i: (i, 0))
    hm = pl.BlockSpec((2, S, HEAD_DIM), lambda i: (i, 0, 0))
    col = pl.BlockSpec((2, S, 1), lambda i: (i, 0, 0))
    rw = pl.BlockSpec((2, 1, S), lambda i: (i, 0, 0))
    nh = 2 * npair
    return _call_hosting(
        body, comm=comm, grid=(npair,),
        in_specs=[_fm_spec(LANE, S, q0), hm, hm, _fm_spec(LANE, S, k0), hm, col, rw, fT, _fm_spec(LANE, S, d0), hm, rw],
        out_specs=[fT, hm, hm, col, rw],
        out_shape=[SDS((npair * LANE, S), BF16), SDS((nh, S, HEAD_DIM), BF16), SDS((nh, S, HEAD_DIM), BF16),
                   SDS((nh, S, 1), F32), SDS((nh, 1, S), F32)],
        scratch_shapes=[pltpu.VMEM((2, S, LANE), F32), pltpu.VMEM((2, S, HEAD_DIM), F32),
                        pltpu.VMEM((2, S, LANE), BF16)],
        name=name, args=(qT, q_h, k_h, kT, v_h, ck, cq, oT, doT, do_h, lse))


def _split3(x):
    hi = x.astype(BF16)
    r1 = x - hi.astype(F32)
    mid = r1.astype(BF16)
    lo = (r1 - mid.astype(F32)).astype(BF16)
    return hi, mid, lo


def _tri_dot(tri, x):
    hi, mid, lo = _split3(x)
    return (jnp.dot(tri, hi, preferred_element_type=F32) + jnp.dot(tri, mid, preferred_element_type=F32)
            + jnp.dot(tri, lo, preferred_element_type=F32))


def _fox_cum(gf, bfo, name):
    S = gf.shape[0]
    nb = S // LANE
    fcol = (GF_COLS - LANE) // LANE

    def body(f_ref, b_ref, cum_ref):
        row = lax.broadcasted_iota(jnp.int32, (LANE, LANE), 0)
        col = lax.broadcasted_iota(jnp.int32, (LANE, LANE), 1)
        tri = jnp.where(row >= col, 1.0, 0.0).astype(BF16)
        carry = jnp.zeros((1, LANE), F32)
        for t in range(nb):
            xl = f_ref[t * LANE:(t + 1) * LANE, :] + b_ref[...]
            lf = jnp.minimum(xl, 0.0) - jnp.log(1.0 + jnp.exp(-jnp.abs(xl)))
            cblk = _tri_dot(tri, lf) + carry
            cum_ref[t * LANE:(t + 1) * LANE, :] = cblk
            carry = cblk[LANE - 1:LANE, :]

    return pl.pallas_call(
        body, grid=(1,), in_specs=[pl.BlockSpec((S, LANE), lambda i: (0, fcol)), _vec(LANE)],
        out_specs=pl.BlockSpec((S, LANE), lambda i: (0, 0)), out_shape=SDS((S, LANE), F32),
        compiler_params=_cparams(("arbitrary",)), name=name)(gf, bfo)


def _fox_cum_bwd(gf, bfo, dcum, name):
    S = gf.shape[0]
    nb = S // LANE
    fcol = (GF_COLS - LANE) // LANE

    def body(f_ref, b_ref, dc_ref, df_ref, db_ref):
        row = lax.broadcasted_iota(jnp.int32, (LANE, LANE), 0)
        col = lax.broadcasted_iota(jnp.int32, (LANE, LANE), 1)
        tri = jnp.where(row <= col, 1.0, 0.0).astype(BF16)
        carry = jnp.zeros((1, LANE), F32)
        tot = jnp.zeros((1, LANE), F32)
        for t in range(nb - 1, -1, -1):
            rows = slice(t * LANE, (t + 1) * LANE)
            dlf = _tri_dot(tri, dc_ref[rows, :]) + carry
            carry = dlf[0:1, :]
            xl = f_ref[rows, :] + b_ref[...]
            dfl = dlf * (1.0 / (1.0 + jnp.exp(xl)))
            df_ref[rows, :] = dfl.astype(df_ref.dtype)
            tot = tot + jnp.sum(dfl, axis=0, keepdims=True)
        db_ref[...] = tot

    return pl.pallas_call(
        body, grid=(1,),
        in_specs=[pl.BlockSpec((S, LANE), lambda i: (0, fcol)), _vec(LANE), pl.BlockSpec((S, LANE), lambda i: (0, 0))],
        out_specs=[pl.BlockSpec((S, LANE), lambda i: (0, 0)), _vec(LANE)],
        out_shape=[SDS((S, LANE), BF16), SDS((1, LANE), F32)],
        compiler_params=_cparams(("arbitrary",)), name=name)(gf, bfo, dcum)


REL_FAR = C_PREV * CHUNK - REL_CLIP


def _rel_onehot(qi, band):
    w = band - REL_FAR
    r = lax.broadcasted_iota(jnp.int32, (N_REL_PAD, w), 0)
    j = lax.broadcasted_iota(jnp.int32, (N_REL_PAD, w), 1) + REL_FAR
    idx = jnp.clip(C_PREV * CHUNK + qi - j, -REL_CLIP, REL_CLIP) + REL_CLIP
    return jnp.where(r == idx, 1.0, 0.0).astype(BF16)


def _rel_expand(rel, name):
    band = (C_PREV + 1) * CHUNK

    def body(rel_ref, o_ref):
        hi, mid, lo = _split3(rel_ref[...])
        far = jnp.broadcast_to(rel_ref[:, 2 * REL_CLIP:2 * REL_CLIP + 1], (N_HEADS, REL_FAR))

        def row(qi, carry):
            oh = _rel_onehot(qi, band)
            o_ref[qi, :, 0:REL_FAR] = far
            o_ref[qi, :, REL_FAR:band] = (jnp.dot(hi, oh, preferred_element_type=F32)
                                          + jnp.dot(mid, oh, preferred_element_type=F32)
                                          + jnp.dot(lo, oh, preferred_element_type=F32))
            return carry

        lax.fori_loop(0, CHUNK, row, 0, unroll=2)

    return pl.pallas_call(
        body, grid=(1,), in_specs=[pl.BlockSpec((N_HEADS, N_REL_PAD), lambda i: (0, 0))],
        out_specs=pl.BlockSpec((CHUNK, N_HEADS, band), lambda i: (0, 0, 0)),
        out_shape=SDS((CHUNK, N_HEADS, band), F32),
        compiler_params=_cparams(("arbitrary",)), name=name)(rel)


def _rel_reduce(dbias, name):
    band = (C_PREV + 1) * CHUNK
    NT = (((1,), (1,)), ((), ()))

    def body(d_ref, o_ref):
        def row(qi, st):
            acc, far = st
            oh = _rel_onehot(qi, band)
            hi, mid, lo = _split3(d_ref[qi, :, REL_FAR:band])
            acc = acc + (lax.dot_general(hi, oh, NT, preferred_element_type=F32)
                         + lax.dot_general(mid, oh, NT, preferred_element_type=F32)
                         + lax.dot_general(lo, oh, NT, preferred_element_type=F32))
            return acc, far + jnp.sum(d_ref[qi, :, 0:REL_FAR], axis=-1, keepdims=True)

        acc, far = lax.fori_loop(0, CHUNK, row, (jnp.zeros((N_HEADS, N_REL_PAD), F32), jnp.zeros((N_HEADS, 1), F32)),
                                 unroll=2)
        col = lax.broadcasted_iota(jnp.int32, (N_HEADS, N_REL_PAD), 1)
        o_ref[...] = acc + jnp.where(col == 2 * REL_CLIP, far, 0.0)

    return pl.pallas_call(
        body, grid=(1,), in_specs=[pl.BlockSpec((CHUNK, N_HEADS, band), lambda i: (0, 0, 0))],
        out_specs=pl.BlockSpec((N_HEADS, N_REL_PAD), lambda i: (0, 0)),
        out_shape=SDS((N_HEADS, N_REL_PAD), F32),
        compiler_params=_cparams(("arbitrary",)), name=name)(dbias)


def _alibi_table():
    qi = np.arange(CHUNK)[:, None]
    j = np.arange((A_PREV + 1) * CHUNK)[None, :]
    dist = np.abs(A_PREV * CHUNK + qi - j).astype(np.float32)
    slopes = np.exp2(-8.0 * np.arange(1, N_HEADS + 1, dtype=np.float32) / N_HEADS).astype(np.float32)
    return jnp.asarray(-slopes[:, None, None] * dist[None])


def _ada_fwd(c_all, w, b, name):
    n = w.shape[2]

    def body(c_ref, w_ref, b_ref, o_ref):
        cv = c_ref[...]
        cond = (cv * _sigmoid(cv)).astype(BF16)
        o_ref[0] = jnp.dot(cond, w_ref[0].astype(BF16), preferred_element_type=F32) + b_ref[0]

    return pl.pallas_call(
        body, grid=(DEPTH,),
        in_specs=[pl.BlockSpec((16, D_MODEL), lambda l: (0, 0)), pl.BlockSpec((1, D_MODEL, n), lambda l: (l, 0, 0)),
                  pl.BlockSpec((1, 1, n), lambda l: (l, 0, 0))],
        out_specs=pl.BlockSpec((1, 16, n), lambda l: (l, 0, 0)), out_shape=SDS((DEPTH, 16, n), F32),
        compiler_params=_cparams(("parallel",)), name=name)(c_all, w, b)


def _ada_bwd(c_t, dmod, name, comm=None):
    n = dmod.shape[2]
    bn = _blk(n, 512)
    tr = 256

    def body(c_ref, d_ref, o_ref):
        cv = c_ref[...]
        cond = (cv * _sigmoid(cv)).astype(BF16).astype(F32)
        dm = d_ref[0].astype(BF16).astype(F32)
        acc = cond[:, 0:1] * dm[0:1, :]
        for b_ in range(1, 8):
            acc = acc + cond[:, b_:b_ + 1] * dm[b_:b_ + 1, :]
        o_ref[0] = acc

    res, got = _call_hosting(
        body, comm=comm, grid=(DEPTH, D_MODEL // tr, n // bn),
        in_specs=[pl.BlockSpec((tr, 8), lambda l, i, j: (i, 0)), pl.BlockSpec((1, 8, bn), lambda l, i, j: (l, 0, j))],
        out_specs=[pl.BlockSpec((1, tr, bn), lambda l, i, j: (l, i, j))], out_shape=[SDS((DEPTH, D_MODEL, n), F32)],
        scratch_shapes=[], name=name, args=(c_t, dmod))
    return res[0], got


def _adamw(w, m, v, parts, name):
    L, R, C = w.shape
    per_layer = isinstance(parts, (list, tuple))
    plist = list(parts) if per_layer else [parts]
    P = plist[0].shape[0]
    tr = _blk_rows(R, max(16, (1 << 18) // C))
    nr = R // tr
    c1 = 1.0 - ADAM_B1 ** ADAM_STEP
    c2 = 1.0 - ADAM_B2 ** ADAM_STEP

    def total(p_ref):
        g = p_ref[0].astype(F32)
        for k in range(1, P):
            g = g + p_ref[k].astype(F32)
        return g

    def body(w_ref, m_ref, v_ref, *rest):
        p_refs, (g_ref, d_ref, nm_ref, nv_ref) = rest[:len(plist)], rest[len(plist):]
        g = total(p_refs[0])
        for k in range(1, len(plist)):
            g = jnp.where(pl.program_id(0) == k, total(p_refs[k]), g)
        mn = ADAM_B1 * m_ref[0] + (1.0 - ADAM_B1) * g
        vn = ADAM_B2 * v_ref[0] + (1.0 - ADAM_B2) * (g * g)
        m_hat = mn / c1
        v_hat = vn / c2
        g_ref[0] = g
        nm_ref[0] = mn
        nv_ref[0] = vn
        d_ref[0] = -ADAM_LR * (m_hat / (jnp.sqrt(v_hat) + ADAM_EPS) + ADAM_WD * w_ref[0])

    rs = pl.BlockSpec((1, tr, C), lambda l, i: (l, i, 0))
    if per_layer:
        def layer_spec(k):
            return pl.BlockSpec((P, tr, C), lambda l, i: (0, jnp.where(l == k, i, 0), 0))
        pspecs = [layer_spec(k) for k in range(L)]
    else:
        pspecs = [pl.BlockSpec((P, tr, C), lambda l, i: (0, l * nr + i, 0))]
    return pl.pallas_call(
        body, grid=(L, nr), in_specs=[rs, rs, rs] + pspecs, out_specs=[rs, rs, rs, rs],
        out_shape=[SDS((L, R, C), F32)] * 4, compiler_params=_cparams(("parallel", "parallel")),
        name=name)(w, m, v, *plist)


def _blk_rows(R, cap):
    if R <= cap:
        return R
    best = None
    for t in range(16, cap + 1, 16):
        if R % t == 0:
            best = t
    assert best is not None, (R, cap)
    return best


def _add_cast_rows(g, t, name):
    Q, R, C = g.shape
    half = R // 2
    tr = _blk_rows(half, max(16, (1 << 19) // C))
    nb = half // tr

    def body(lo_ref, hi_ref, t_ref, o_ref):
        c = lax.axis_index("c")

        @pl.when(c == 0)
        def _():
            o_ref[...] = (lo_ref[...].astype(F32) + t_ref[...].astype(F32)).astype(o_ref.dtype)

        @pl.when(c == 1)
        def _():
            o_ref[...] = (hi_ref[...].astype(F32) + t_ref[...].astype(F32)).astype(o_ref.dtype)

    bs = pl.BlockSpec((1, tr, C), lambda q, i: (q, i, 0))
    hi = pl.BlockSpec((1, tr, C), lambda q, i: (q, nb + i, 0))
    return pl.pallas_call(
        body, grid=(Q, nb), in_specs=[bs, hi, bs], out_specs=bs, out_shape=SDS((Q, half, C), BF16),
        compiler_params=_cparams(("parallel", "parallel")), name=name)(g, g, t)


def _coords():
    return lax.axis_index("x"), lax.axis_index("y"), lax.axis_index("c")


def _flip(v, bit):
    return 1 - v if bit else v


def _all_gather8(v, name):
    R = v.shape[0]

    def body(v_ref, o_ref, send_sems, recv_sems):
        x, y, c = _coords()
        me = 4 * x + 2 * y + c
        o_ref[me] = v_ref[...]
        copies = []
        for k in range(1, 8):
            peer = (_flip(x, k & 4), _flip(y, k & 2), _flip(c, k & 1))
            cp = pltpu.make_async_remote_copy(
                src_ref=v_ref, dst_ref=o_ref.at[me], send_sem=send_sems.at[k - 1], recv_sem=recv_sems.at[k - 1],
                device_id=peer, device_id_type=MESH)
            cp.start()
            copies.append(cp)
        for cp in copies:
            cp.wait_recv()
        for cp in copies:
            cp.wait_send()

    return pl.pallas_call(
        body, in_specs=[VMEM_SPEC], out_specs=VMEM_SPEC, out_shape=SDS((8, R, LANE), v.dtype),
        scratch_shapes=[pltpu.SemaphoreType.DMA((7,)), pltpu.SemaphoreType.DMA((7,))],
        compiler_params=pltpu.CompilerParams(vmem_limit_bytes=VMEM_LIMIT), name=name)(v)


def _sibling_swap_rows(arrs, name):
    n = len(arrs)

    def body(*refs):
        in_refs, out_refs = refs[:n], refs[n:2 * n]
        send_sems, recv_sems = refs[2 * n:]
        x, y, c = _coords()
        copies = []
        for a in range(n):
            Q, R = in_refs[a].shape[0], in_refs[a].shape[1]
            half = R // 2
            src = in_refs[a].at[pl.ds(0, Q), pl.ds(pl.multiple_of((1 - c) * half, 16), half)]
            cp = pltpu.make_async_remote_copy(
                src_ref=src, dst_ref=out_refs[a], send_sem=send_sems.at[a], recv_sem=recv_sems.at[a],
                device_id=(x, y, 1 - c), device_id_type=MESH)
            cp.start()
            copies.append(cp)
        for cp in copies:
            cp.wait_recv()
        for cp in copies:
            cp.wait_send()

    return pl.pallas_call(
        body, in_specs=[ANY] * n, out_specs=[ANY] * n,
        out_shape=[SDS((a.shape[0], a.shape[1] // 2, a.shape[2]), a.dtype) for a in arrs],
        scratch_shapes=[pltpu.SemaphoreType.DMA((n,)), pltpu.SemaphoreType.DMA((n,))],
        name=name)(*arrs)


class _OwnerReduce:
    aliased = False

    def __init__(self, srcs, lay):
        self.srcs, self.lay, self.n = list(srcs), lay, len(srcs)
        self.out_shapes = [SDS(a.shape, a.dtype) for a in self.srcs]
        self.sem_shapes = [pltpu.SemaphoreType.DMA((self.n, 3)), pltpu.SemaphoreType.DMA((self.n, 3)),
                           pltpu.SemaphoreType.DMA((self.n,))]

    def _copies(self, src_refs, dst_refs, sems):
        ici_send, ici_recv, loc_sem = sems
        x, y, c = _coords()
        p = 2 * x + y
        local, remote = [], []
        for a in range(self.n):
            local.append(pltpu.make_async_copy(src_refs[a].at[p], dst_refs[a].at[p], loc_sem.at[a]))
            for k in range(1, 4):
                qx, qy = _flip(x, k & 2), _flip(y, k & 1)
                remote.append(pltpu.make_async_remote_copy(
                    src_ref=src_refs[a].at[2 * qx + qy], dst_ref=dst_refs[a].at[p], send_sem=ici_send.at[a, k - 1],
                    recv_sem=ici_recv.at[a, k - 1], device_id=(qx, qy, self.lay), device_id_type=MESH))
        return c, local, remote

    def start(self, src_refs, dst_refs, sems):
        c, local, remote = self._copies(src_refs, dst_refs, sems)

        @pl.when(c == self.lay)
        def _():
            for cp in local + remote:
                cp.start()

    def finish(self, src_refs, dst_refs, sems):
        c, local, remote = self._copies(src_refs, dst_refs, sems)

        @pl.when(c == self.lay)
        def _():
            for cp in remote:
                cp.wait_recv()
            for cp in remote:
                cp.wait_send()
            for cp in local:
                cp.wait()


def _call_hosting(body, *, comm, grid, in_specs, out_specs, out_shape, scratch_shapes, name, args, semantics=None):
    n_in, n_out, n_scr = len(args), len(out_shape), len(scratch_shapes)
    if comm is None:
        sem = semantics if semantics is not None else ("parallel",) * len(grid)
        res = pl.pallas_call(body, grid=grid, in_specs=in_specs, out_specs=out_specs, out_shape=out_shape,
                             scratch_shapes=scratch_shapes, compiler_params=_cparams(sem), name=name)(*args)
        return list(res), None
    k = comm.n

    def hosted(*refs):
        ins, cin = refs[:n_in], refs[n_in:n_in + k]
        outs = refs[n_in + k:n_in + k + n_out]
        cout = refs[n_in + k + n_out:n_in + 2 * k + n_out]
        scr = refs[n_in + 2 * k + n_out:n_in + 2 * k + n_out + n_scr]
        sems = refs[n_in + 2 * k + n_out + n_scr:]
        first = pl.program_id(0) == 0
        last = pl.program_id(0) == grid[0] - 1
        for d in range(1, len(grid)):
            first = jnp.logical_and(first, pl.program_id(d) == 0)
            last = jnp.logical_and(last, pl.program_id(d) == grid[d] - 1)

        @pl.when(first)
        def _():
            comm.start(cin, cout, sems)

        body(*ins, *outs, *scr)

        @pl.when(last)
        def _():
            comm.finish(cin, cout, sems)

    aliases = {n_in + j: n_out + j for j in range(k)} if comm.aliased else {}
    res = pl.pallas_call(
        hosted, grid=grid, in_specs=list(in_specs) + [ANY] * k, out_specs=list(out_specs) + [ANY] * k,
        out_shape=list(out_shape) + comm.out_shapes, scratch_shapes=list(scratch_shapes) + comm.sem_shapes,
        input_output_aliases=aliases, compiler_params=_cparams(("arbitrary",) * len(grid)),
        name=name)(*args, *comm.srcs)
    return list(res[:n_out]), list(res[n_out:])


class _RowHalfGather:
    aliased = False

    def __init__(self, srcs):
        self.srcs, self.n = list(srcs), len(srcs)
        self.out_shapes = [SDS((4,) + a.shape, a.dtype) for a in self.srcs]
        n = self.n
        self.sem_shapes = [pltpu.SemaphoreType.DMA((n, 3)), pltpu.SemaphoreType.DMA((n, 3)),
                           pltpu.SemaphoreType.DMA((n, 3)), pltpu.SemaphoreType.DMA((n, 3)),
                           pltpu.SemaphoreType.DMA((n,))]

    def _copies(self, src_refs, dst_refs, sems):
        ici_send, ici_recv, d2d_send, d2d_recv, loc_sem = sems
        x, y, c = _coords()
        p = 2 * x + y
        local, first, fwd = [], [], []
        for a in range(self.n):
            R = src_refs[a].shape[0] // 2
            half = pl.ds(pl.multiple_of(c * R, 16), R)
            local.append(pltpu.make_async_copy(src_refs[a], dst_refs[a].at[p], loc_sem.at[a]))
            for k in range(1, 4):
                qx, qy = _flip(x, k & 2), _flip(y, k & 1)
                first.append(pltpu.make_async_remote_copy(
                    src_ref=src_refs[a].at[half], dst_ref=dst_refs[a].at[p, half], send_sem=ici_send.at[a, k - 1],
                    recv_sem=ici_recv.at[a, k - 1], device_id=(qx, qy, c), device_id_type=MESH))
                slot = dst_refs[a].at[2 * qx + qy, half]
                fwd.append(pltpu.make_async_remote_copy(
                    src_ref=slot, dst_ref=slot, send_sem=d2d_send.at[a, k - 1], recv_sem=d2d_recv.at[a, k - 1],
                    device_id=(x, y, 1 - c), device_id_type=MESH))
        return local, first, fwd

    def start(self, src_refs, dst_refs, sems):
        local, first, _ = self._copies(src_refs, dst_refs, sems)
        for cp in local + first:
            cp.start()

    def finish(self, src_refs, dst_refs, sems):
        local, first, fwd = self._copies(src_refs, dst_refs, sems)
        for got, on in zip(first, fwd):
            got.wait_recv()
            on.start()
        for cp in fwd:
            cp.wait_recv()
        for cp in first + fwd:
            cp.wait_send()
        for cp in local:
            cp.wait()

    def run(self, name):
        return _run_exchange(self, name)


def _run_exchange(comm, name):
    n = comm.n

    def body(*refs):
        src_refs, dst_refs, sems = refs[:n], refs[n:2 * n], refs[2 * n:]
        comm.start(src_refs, dst_refs, sems)
        comm.finish(src_refs, dst_refs, sems)

    return pl.pallas_call(body, in_specs=[ANY] * n, out_specs=[ANY] * n, out_shape=comm.out_shapes,
                          scratch_shapes=comm.sem_shapes, name=name)(*comm.srcs)


class _RowHalfReduce:
    aliased = False

    def __init__(self, srcs):
        self.srcs, self.n = list(srcs), len(srcs)
        self.out_shapes = [SDS((4, 2 * a.shape[1], a.shape[2]), a.dtype) for a in self.srcs]
        n = self.n
        self.sem_shapes = [pltpu.SemaphoreType.DMA((n, 3)), pltpu.SemaphoreType.DMA((n, 3)),
                           pltpu.SemaphoreType.DMA((n, 4)), pltpu.SemaphoreType.DMA((n, 4)),
                           pltpu.SemaphoreType.DMA((n,))]

    def _copies(self, src_refs, dst_refs, sems):
        ici_send, ici_recv, d2d_send, d2d_recv, loc_sem = sems
        x, y, c = _coords()
        p = 2 * x + y
        local, first, fwd = [], [], []
        for a in range(self.n):
            R = src_refs[a].shape[1]
            half = pl.ds(pl.multiple_of(c * R, 16), R)
            local.append(pltpu.make_async_copy(src_refs[a].at[p], dst_refs[a].at[p, half], loc_sem.at[a]))
            for k in range(4):
                qx, qy = _flip(x, k & 2), _flip(y, k & 1)
                if k:
                    first.append(pltpu.make_async_remote_copy(
                        src_ref=src_refs[a].at[2 * qx + qy], dst_ref=dst_refs[a].at[p, half],
                        send_sem=ici_send.at[a, k - 1], recv_sem=ici_recv.at[a, k - 1], device_id=(qx, qy, c),
                        device_id_type=MESH))
                slot = dst_refs[a].at[2 * qx + qy, half]
                fwd.append(pltpu.make_async_remote_copy(
                    src_ref=slot, dst_ref=slot, send_sem=d2d_send.at[a, k], recv_sem=d2d_recv.at[a, k],
                    device_id=(x, y, 1 - c), device_id_type=MESH))
        return local, first, fwd

    def start(self, src_refs, dst_refs, sems):
        local, first, _ = self._copies(src_refs, dst_refs, sems)
        for cp in local + first:
            cp.start()

    def finish(self, src_refs, dst_refs, sems):
        local, first, fwd = self._copies(src_refs, dst_refs, sems)
        for a in range(self.n):
            local[a].wait()
            fwd[4 * a].start()
            for k in range(1, 4):
                first[3 * a + k - 1].wait_recv()
                fwd[4 * a + k].start()
        for cp in fwd:
            cp.wait_recv()
        for cp in first + fwd:
            cp.wait_send()

    def run(self, name):
        return _run_exchange(self, name)


class _SiblingSend:
    aliased = False

    def __init__(self, srcs, src_core):
        self.srcs, self.src_core, self.n = list(srcs), src_core, len(srcs)
        self.out_shapes = [SDS(a.shape, a.dtype) for a in self.srcs]
        self.sem_shapes = [pltpu.SemaphoreType.DMA((self.n,)), pltpu.SemaphoreType.DMA((self.n,))]

    def _copies(self, src_refs, dst_refs, sems):
        x, y, c = _coords()
        return c, [pltpu.make_async_remote_copy(
            src_ref=src_refs[a], dst_ref=dst_refs[a], send_sem=sems[0].at[a], recv_sem=sems[1].at[a],
            device_id=(x, y, 1 - c), device_id_type=MESH) for a in range(self.n)]

    def start(self, src_refs, dst_refs, sems):
        c, copies = self._copies(src_refs, dst_refs, sems)

        @pl.when(c == self.src_core)
        def _():
            for cp in copies:
                cp.start()

    def finish(self, src_refs, dst_refs, sems):
        c, copies = self._copies(src_refs, dst_refs, sems)

        @pl.when(c == self.src_core)
        def _():
            for cp in copies:
                cp.wait_send()

        @pl.when(c != self.src_core)
        def _():
            for cp in copies:
                cp.wait_recv()


class _Handoff:
    aliased = True

    def __init__(self, srcs, lay, slots):
        self.srcs, self.lay, self.slots, self.n = list(srcs), lay, tuple(slots), len(srcs)
        self.out_shapes = [SDS(a.shape, a.dtype) for a in self.srcs]
        ns = len(self.slots)
        self.sem_shapes = [pltpu.SemaphoreType.DMA((self.n, ns)), pltpu.SemaphoreType.DMA((self.n, ns))]

    def _copies(self, dst_refs, sems):
        x, y, c = _coords()
        copies = []
        for a in range(self.n):
            for j, k in enumerate(self.slots):
                slot = dst_refs[a].at[2 * _flip(x, k & 2) + _flip(y, k & 1)]
                copies.append(pltpu.make_async_remote_copy(
                    src_ref=slot, dst_ref=slot, send_sem=sems[0].at[a, j], recv_sem=sems[1].at[a, j],
                    device_id=(x, y, 1 - c), device_id_type=MESH))
        return c, copies

    def start(self, src_refs, dst_refs, sems):
        c, copies = self._copies(dst_refs, sems)

        @pl.when(c == self.lay)
        def _():
            for cp in copies:
                cp.start()

    def finish(self, src_refs, dst_refs, sems):
        c, copies = self._copies(dst_refs, sems)

        @pl.when(c == self.lay)
        def _():
            for cp in copies:
                cp.wait_send()

        @pl.when(c != self.lay)
        def _():
            for cp in copies:
                cp.wait_recv()


def _add_cast_on(a, b, lay, name):
    Q, R, C = b.shape
    tr = _blk_rows(R, max(16, (1 << 19) // C))

    def body(a_ref, b_ref, o_ref):
        @pl.when(lax.axis_index("c") == lay)
        def _():
            o_ref[...] = (a_ref[...].astype(F32) + b_ref[...].astype(F32)).astype(o_ref.dtype)

    bs = pl.BlockSpec((1, tr, C), lambda q, i: (q, i, 0))
    return pl.pallas_call(
        body, grid=(Q, R // tr), in_specs=[bs, bs], out_specs=bs, out_shape=SDS((Q, R, C), BF16),
        compiler_params=_cparams(("parallel", "parallel")), name=name)(a, b)


_IN_SIZES = (512, 128, 128, 512, 512, 512, 8, 512, 512, 512, 3072)
_IN_OFF = tuple(int(v) for v in np.cumsum((0,) + _IN_SIZES))
_IN_Q = N_IN_COLS // 4


def _pack_w_in(w):
    def cols(lo, hi):
        out = []
        while lo < hi:
            q, off = divmod(lo, _IN_Q)
            n = min(hi - lo, _IN_Q - off)
            out.append(w[q, :, off:off + n])
            lo += n
        return out

    fb0, fb1, g0 = _IN_OFF[6], _IN_OFF[7], _IN_OFF[10]
    wqkv = jnp.concatenate(cols(0, fb0) + cols(fb1, g0), axis=1)
    wgf = jnp.concatenate(cols(g0, N_IN_COLS) + cols(fb0, fb1) + [jnp.zeros((w.shape[1], LANE - 8), w.dtype)], axis=1)
    return wqkv, wgf


def _unpack_w_in(dqkv, dgf):
    fb0, fb1, g0 = _IN_OFF[6], _IN_OFF[7], _IN_OFF[10]

    def cols(lo, hi):
        out = []
        while lo < hi:
            if lo < fb0:
                n = min(hi, fb0) - lo
                out.append(dqkv[:, lo:lo + n])
            elif lo < fb1:
                n = min(hi, fb1) - lo
                out.append(dgf[:, 3072 + lo - fb0:3072 + lo - fb0 + n])
            elif lo < g0:
                n = min(hi, g0) - lo
                out.append(dqkv[:, lo - 8:lo - 8 + n])
            else:
                n = hi - lo
                out.append(dgf[:, lo - g0:lo - g0 + n])
            lo += n
        return out

    return jnp.stack([jnp.concatenate(cols(q * _IN_Q, (q + 1) * _IN_Q), axis=1) for q in range(4)])


def _pad_rows(a, rows):
    return jnp.pad(a, ((0, rows - a.shape[0]), (0, 0)))


def _small_pack(parts):
    flat = jnp.concatenate([p.reshape(-1) for p in parts])
    n = flat.shape[0]
    rows = -(-n // LANE)
    rows = -(-rows // 8) * 8
    return jnp.pad(flat, (0, rows * LANE - n)).reshape(rows, LANE)


def _small_unpack(block, shapes):
    flat = block.reshape(-1)
    out, off = [], 0
    for s in shapes:
        n = int(np.prod(s))
        out.append(flat[off:off + n].reshape(s))
        off += n
    return out


def _kv_same(g):
    return 0


def _kv_own(g):
    return g


_mm_plain = _mm


def _mm_hosting(a, b, *, comm, **kw):
    if comm is None:
        return _mm(a, b, **kw), None
    return _mm(a, b, comm=comm, **kw)


def _layer_fwd(x, mod, p, l, ride):
    sh_m, sc_m, g_m, sh_f, sc_f, g_f = mod
    nm = "l%d_" % l

    def carried(name, run):
        res, got = run(ride.comm_for(name))
        if got is not None:
            ride.done(name, got)
        return res

    h1 = _norm_mod_fwd(x, p["norm_mix_g"], sc_m, sh_m, nm + "norm_mix_fwd")
    qkv = carried("proj_qkv", lambda cm: _mm_hosting(h1, p["wqkv"], mode="nn", out_dtype=BF16,
                                                     name=nm + "proj_qkv", comm=cm))
    gf = carried("proj_gf", lambda cm: _mm_hosting(h1, p["wgf"], mode="nn", out_dtype=F32, name=nm + "proj_gf",
                                                  cap_n=640, comm=cm))
    qkv_t = qkv.T
    o_a_t = carried("attn_a", lambda cm: _bandT_fwd(
        (qkv_t, 0), _heads(qkv[:, 512:640], A_KV_HEADS), (qkv_t, 640), p["alibi"], p["sink_tab"],
        GQ=4, GK=1, P=A_PREV, kvoff=_kv_same, name=nm + "attn_a_fwd", comm=cm))
    cum = _fox_cum(gf, p["b_forget_pad"], nm + "fox_cum")
    cum_t = cum[:, :N_HEADS].T
    cc, cr = cum_t[:, :, None], cum_t[:, None, :]
    o_b_t, lse_b = carried("attn_b", lambda cm: _foxT_fwd(
        (qkv_t, 768), _heads(qkv[:, 1280:1792], N_HEADS), (qkv_t, 1792), cc, cr, nm + "attn_b_fwd", comm=cm))
    o_c_t = carried("attn_c", lambda cm: _bandT_fwd(
        (qkv_t, 2304), _heads(qkv[:, 2816:3328], N_HEADS), (qkv_t, 3328), p["rel_tab"], p["no_sink"],
        GQ=2, GK=2, P=C_PREV, kvoff=_kv_own, name=nm + "attn_c_fwd", comm=cm))
    p = dict(p, **ride.late_weights())
    o = jnp.concatenate([o_a_t, o_b_t, o_c_t], axis=0).T
    y = _mm(o, p["wb"], mode="nn", out_dtype=BF16, groups=3, name=nm + "branch")
    merged = _merge_fwd(y, gf, nm + "merge_fwd")
    mix = _mm(merged, p["wout"], mode="nn", out_dtype=BF16, name=nm + "out_proj")
    x1 = _resid_fwd(x, mix, g_m, nm + "resid_mix")
    h2 = _norm_mod_fwd(x1, p["norm_ffn_g"], sc_f, sh_f, nm + "norm_ffn_fwd")
    u = carried("ffn_in", lambda cm: _mm_hosting(h2, p["wfi"], mode="nn", out_dtype=BF16, name=nm + "ffn_in",
                                                 cap_n=512, comm=cm))
    a = _swiglu_fwd(u, nm + "swiglu_fwd")
    f = _mm(a, p["wfo"], mode="nn", out_dtype=BF16, name=nm + "ffn_out", cap_m=1024)
    x2 = _resid_fwd(x1, f, g_f, nm + "resid_ffn")
    saved = dict(x=x, h1=h1, qkv=qkv, qkv_t=qkv_t, gf=gf, cc=cc, cr=cr, o_b_t=o_b_t, lse_b=lse_b, o=o, y=y, merged=merged,
                 mix=mix, x1=x1, h2=h2, u=u, a=a, f=f)
    return x2, saved, p


def _layer_bwd(dx2, mod, p, s, l, ride=None):
    sh_m, sc_m, g_m, sh_f, sc_f, g_f = mod
    nm = "l%d_" % l

    def _mm(a, b, *, name, **kw):
        comm = ride.comm_for(name) if ride is not None else None
        if comm is None:
            return _mm_plain(a, b, name=nm + name, **kw)
        out, got = _mm_plain(a, b, name=nm + name, comm=comm, **kw)
        ride.done(name, got)
        return out

    dg_f, df = _resid_bwd(dx2, s["f"], g_f, nm + "resid_ffn_bwd")
    da = _mm(df, p["wfo"], mode="nt", out_dtype=BF16, name="ffn_out_dx", cap_m=1024, cap_n=1408)
    d_wfo = _mm(s["a"], df, mode="tn", out_dtype=BF16, name="ffn_out_dw", cap_m=1408, cap_k=2048)
    du = _swiglu_bwd(da, s["u"], nm + "swiglu_bwd")
    dh2 = _mm(du, p["wfi"], mode="nt", out_dtype=BF16, name="ffn_in_dx", cap_m=1024)
    d_wfi = _mm(s["h2"], du, mode="tn", out_dtype=BF16, name="ffn_in_dw", cap_m=1024, cap_n=1408, cap_k=2048,
                col_quarters=True)
    dx1, dsc_f, dsh_f, dgn_f = _norm_mod_bwd(s["x1"], [dh2], dx2, p["norm_ffn_g"], sc_f, nm + "norm_ffn_bwd")
    dg_m, dmix = _resid_bwd(dx1, s["mix"], g_m, nm + "resid_mix_bwd")
    dmerged = _mm(dmix, p["wout"], mode="nt", out_dtype=BF16, name="out_proj_dx")
    d_wout = _mm(s["merged"], dmix, mode="tn", out_dtype=BF16, name="out_proj_dw", cap_m=1024, cap_k=2048)
    dy, dgates = _merge_bwd(dmerged, s["y"], s["gf"], nm + "merge_bwd")
    do = _mm(dy, p["wb"], mode="nt", out_dtype=BF16, groups=3, name="branch_dx")
    d_wb = _mm(s["o"], dy, mode="tn", out_dtype=BF16, groups=3, name="branch_dw", cap_k=2048,
               col_quarters=True)
    comms = ride.exchanges() if ride is not None else (None, None, None)
    qkv, qkv_t = s["qkv"], s["qkv_t"]
    do_t = do.T
    (dqa_t, dka_h, dva_h, _, dsink), got_a = _bandT_bwd(
        (qkv_t, 0), _heads(qkv[:, 0:512], N_HEADS), _heads(qkv[:, 512:640], A_KV_HEADS), (qkv_t, 512),
        _heads(qkv[:, 640:768], A_KV_HEADS), (do_t, 0), _heads(do[:, 0:512], N_HEADS), p["alibi"], p["sink_tab"],
        GQ=4, GK=1, P=A_PREV, kvoff=_kv_same, name=nm + "attn_a_bwd", comm=comms[0])
    (dqb_t, dkb_h, dvb_h, dck, dcq), got_b = _foxT_bwd(
        (qkv_t, 768), _heads(qkv[:, 768:1280], N_HEADS), _heads(qkv[:, 1280:1792], N_HEADS), (qkv_t, 1280),
        _heads(qkv[:, 1792:2304], N_HEADS), s["cc"], s["cr"], s["o_b_t"], (do_t, 512),
        _heads(do[:, 512:1024], N_HEADS), s["lse_b"], nm + "attn_b_bwd", comm=comms[1])
    dcum = jnp.pad((dck[:, :, 0] + dcq[:, 0, :]).T, ((0, 0), (0, LANE - N_HEADS)))
    dfb, db_forget = _fox_cum_bwd(s["gf"], p["b_forget_pad"], dcum, nm + "fox_cum_bwd")
    (dqc_t, dkc_h, dvc_h, dbias_c, _), got_c = _bandT_bwd(
        (qkv_t, 2304), _heads(qkv[:, 2304:2816], N_HEADS), _heads(qkv[:, 2816:3328], N_HEADS), (qkv_t, 2816),
        _heads(qkv[:, 3328:3840], N_HEADS), (do_t, 1024), _heads(do[:, 1024:1536], N_HEADS), p["rel_tab"],
        p["no_sink"], GQ=2, GK=2, P=C_PREV, kvoff=_kv_own, name=nm + "attn_c_bwd", comm=comms[2])
    d_rel = _rel_reduce(jnp.transpose(_unpair_table(dbias_c), (1, 0, 2)), nm + "rel_reduce")[:, :N_REL]
    dqkv = jnp.concatenate([dqa_t.T, _unheads(dka_h), _unheads(dva_h), dqb_t.T, _unheads(dkb_h), _unheads(dvb_h),
                            dqc_t.T, _unheads(dkc_h), _unheads(dvc_h)], axis=1)
    dgf = jnp.concatenate([dgates, dfb], axis=1)
    if ride is not None:
        ride.exchanged((got_a, got_b, got_c))
    dh1a = _mm(dqkv, p["wqkv"], mode="nt", out_dtype=BF16, name="proj_qkv_dx", cap_k=1024)
    dh1b = _mm(dgf, p["wgf"], mode="nt", out_dtype=BF16, name="proj_gf_dx", cap_k=640)
    d_wqkv = _mm(s["h1"], dqkv, mode="tn", out_dtype=BF16, name="proj_qkv_dw", cap_m=1024, cap_k=2048)
    d_wgf = _mm(s["h1"], dgf, mode="tn", out_dtype=BF16, name="proj_gf_dw", cap_m=1024, cap_n=640, cap_k=2048)
    dx, dsc_m, dsh_m, dgn_m = _norm_mod_bwd(s["x"], [dh1a, dh1b], dx1, p["norm_mix_g"], sc_m, nm + "norm_mix_bwd")
    d_mod = jnp.concatenate([dsh_m, dsc_m, dg_m, dsh_f, dsc_f, dg_f], axis=1)[0]
    grads = dict(w_in=_unpack_w_in(d_wqkv, d_wgf), w_branch=d_wb, w_out=d_wout.reshape(4, -1, D_MODEL),
                 w_ffn_in=d_wfi, w_ffn_out=d_wfo.reshape(4, -1, D_MODEL),
                 norm_mix_g=dgn_m[0], norm_ffn_g=dgn_f[0], b_forget=db_forget[0, :N_HEADS],
                 sinks=dsink[:, 0, 0], rel_bias=d_rel, d_mod=d_mod)
    return dx, grads


def kernel(x, c, norm_mix_g, norm_ffn_g, w_ada, b_ada, w_in, b_forget, sinks, rel_bias, w_branch, w_out, w_ffn_in, w_ffn_out, final_norm_g, loss_target, m_norm_mix_g, m_norm_ffn_g, m_w_ada, m_b_ada, m_w_in, m_b_forget, m_sinks, m_rel_bias, m_w_branch, m_w_out, m_w_ffn_in, m_w_ffn_out, m_final_norm_g, v_norm_mix_g, v_norm_ffn_g, v_w_ada, v_b_ada, v_w_in, v_b_forget, v_sinks, v_rel_bias, v_w_branch, v_w_out, v_w_ffn_in, v_w_ffn_out, v_final_norm_g):
    xi, yi, ci = _coords()
    chip = 2 * xi + yi
    dev = 2 * chip + ci
    xs = x[0]
    S = xs.shape[0]
    n_ada = w_ada.shape[2]

    big_names = ("w_in", "w_branch", "w_out", "w_ffn_in", "w_ffn_out")
    big_w = dict(w_in=w_in, w_branch=w_branch, w_out=w_out, w_ffn_in=w_ffn_in, w_ffn_out=w_ffn_out)
    big_m = dict(w_in=m_w_in, w_branch=m_w_branch, w_out=m_w_out, w_ffn_in=m_w_ffn_in, w_ffn_out=m_w_ffn_out)
    big_v = dict(w_in=v_w_in, w_branch=v_w_branch, w_out=v_w_out, w_ffn_in=v_w_ffn_in, w_ffn_out=v_w_ffn_out)
    flat2 = lambda a: a.reshape(-1, a.shape[-1])
    shards = [[flat2(big_w[n][l]).astype(BF16) for n in big_names] for l in range(DEPTH)]
    gw = [[None] * (len(big_names) + 2) for _ in range(DEPTH)]
    for l in range(DEPTH):
        shards[l] += [shards[l][0][:D_MODEL // 2], shards[l][0][D_MODEL // 2:]]
    gw[0][0] = _RowHalfGather([shards[0][0]]).run("weights_gather_w_in_l0")[0]
    host_g = ((1, 2, 4), (0,), (3,))

    class WeightRide:
        def __init__(self, l, plan):
            self.l, self.plan = l, plan

        def comm_for(self, name):
            if name not in self.plan:
                return None
            lay, idx = self.plan[name]
            return _RowHalfGather([shards[lay][i] for i in idx])

        def done(self, name, got):
            lay, idx = self.plan[name]
            for i, r in zip(idx, got):
                gw[lay][i] = r

        def late_weights(self):
            g = gw[self.l]
            return dict(wb=jnp.transpose(g[1], (1, 0, 2)).reshape(3 * BRANCH_W, D_MODEL),
                        wout=g[2].reshape(D_MODEL, D_MODEL),
                        wfi=jnp.transpose(g[3], (1, 0, 2)).reshape(D_MODEL, 2 * FFN_H),
                        wfo=g[4].reshape(FFN_H, D_MODEL))

    weight_plan = [
        {"proj_qkv": (0, (1,)), "proj_gf": (0, (2,)), "attn_a": (0, (4,)), "attn_b": (0, (3,)), "attn_c": (1, (5,)),
         "ffn_in": (1, (6,))},
        {"attn_a": (1, (1, 2)), "attn_b": (1, (3,)), "attn_c": (1, (4,))}]


    c_all = _all_gather8(c.reshape(8, LANE), "gather_c").reshape(8, D_MODEL)
    b_sh = lax.dynamic_slice_in_dim(b_ada, chip * n_ada, n_ada, axis=1)[:, None, :]
    mod_sh = _ada_fwd(_pad_rows(c_all, 16), w_ada, b_sh, "ada_fwd")[:, :8, :]
    mod_all = _all_gather8(mod_sh.reshape(-1, LANE), "gather_mod").reshape(8, DEPTH, 8, n_ada)
    mod_mine = lax.dynamic_index_in_dim(mod_all[0::2], dev, axis=2, keepdims=False)
    mod = mod_mine.transpose(1, 0, 2).reshape(DEPTH, 6, D_MODEL)

    alibi = _pair_table(_alibi_table())
    no_sink = jnp.full((N_HEADS, 8, LANE), NEG_INF, F32)
    def make_params(l):
        if gw[l][0] is None:
            gw[l][0] = jnp.concatenate([gw[l][5], gw[l][6]], axis=1)
        wqkv, wgf = _pack_w_in(gw[l][0])
        rel_tab = _rel_expand(jnp.pad(rel_bias[l], ((0, 0), (0, N_REL_PAD - N_REL))), "l%d_rel_expand" % l)
        return dict(
            wqkv=wqkv, wgf=wgf, norm_mix_g=norm_mix_g[l][None], norm_ffn_g=norm_ffn_g[l][None],
            b_forget_pad=jnp.pad(b_forget[l], (0, LANE - N_HEADS))[None],
            sink_tab=jnp.broadcast_to(sinks[l][:, None, None], (N_HEADS, 8, LANE)),
            no_sink=no_sink, alibi=alibi, rel_tab=_pair_table(jnp.transpose(rel_tab, (1, 0, 2))))

    mods = [[mod[l, k][None] for k in range(6)] for l in range(DEPTH)]
    params, saved = [None] * DEPTH, [None] * DEPTH
    h = xs
    for l in range(DEPTH):
        h, saved[l], params[l] = _layer_fwd(h, mods[l], make_params(l), l, WeightRide(l, weight_plan[l]))
    loss_dev, dh, d_final = _final_loss(h, final_norm_g[None], loss_target[0], "final_loss")
    grads = [None] * DEPTH
    dh, grads[1] = _layer_bwd(dh, mods[1], params[1], saved[1], 1)

    class Layer1Ride:
        sends = {"ffn_out_dx": (4,), "ffn_in_dx": (3, 1, 2), "ffn_in_dw": (0,)}
        hands = {"proj_qkv_dx": (0,), "proj_gf_dx": (3,), "proj_gf_dw": (4, 1, 2)}

        def __init__(self, g):
            self.g, self.t = g, [None] * len(g)
            self.parts, self.final = [None] * len(g), [None] * len(g)

        def comm_for(self, name):
            if name in self.sends:
                return _SiblingSend([self.g[i] for i in self.sends[name]], 0)
            if name in self.hands:
                return _Handoff([self.parts[i] for i in self.hands[name]], 1, (0, 1, 2, 3))
            return None

        def done(self, name, got):
            idx, dst = (self.sends[name], self.t) if name in self.sends else (self.hands[name], self.final)
            for i, r in zip(idx, got):
                dst[i] = r

        def exchanges(self):
            sums = [_add_cast_on(a, b, 1, "grads_chip_sum_l1_" + n) for n, a, b in zip(big_names, self.g, self.t)]
            return tuple(_OwnerReduce([sums[i] for i in idx], 1) for idx in host_g)

        def exchanged(self, got):
            for res, idx in zip(got, host_g):
                for r, i in zip(res, idx):
                    self.parts[i] = r

    ride = Layer1Ride([grads[1][n] for n in big_names])
    dh, grads[0] = _layer_bwd(dh, mods[0], params[0], saved[0], 0, ride)
    grad_x = dh[None]
    loss = lax.psum(loss_dev[0, 0], ("x", "y", "c"))
    parts1 = ride.final
    g0 = [grads[0][n] for n in big_names]
    t0 = _sibling_swap_rows(g0, "grads_swap_l0")
    sums0 = [_add_cast_rows(a, b, "grads_chip_sum_l0_" + n) for n, a, b in zip(big_names, g0, t0)]
    parts0 = [None] + list(_RowHalfReduce(sums0[1:]).run("grads_reduce_l0"))

    small_names = ("norm_mix_g", "norm_ffn_g", "b_ada", "b_forget", "sinks", "rel_bias", "final_norm_g")
    small_w = dict(norm_mix_g=norm_mix_g, norm_ffn_g=norm_ffn_g, b_ada=b_ada, b_forget=b_forget, sinks=sinks,
                   rel_bias=rel_bias, final_norm_g=final_norm_g)
    small_m = dict(norm_mix_g=m_norm_mix_g, norm_ffn_g=m_norm_ffn_g, b_ada=m_b_ada, b_forget=m_b_forget,
                   sinks=m_sinks, rel_bias=m_rel_bias, final_norm_g=m_final_norm_g)
    small_v = dict(norm_mix_g=v_norm_mix_g, norm_ffn_g=v_norm_ffn_g, b_ada=v_b_ada, b_forget=v_b_forget,
                   sinks=v_sinks, rel_bias=v_rel_bias, final_norm_g=v_final_norm_g)
    small_g = dict(
        norm_mix_g=jnp.stack([grads[l]["norm_mix_g"] for l in range(DEPTH)]),
        norm_ffn_g=jnp.stack([grads[l]["norm_ffn_g"] for l in range(DEPTH)]),
        b_ada=jnp.stack([grads[l]["d_mod"] for l in range(DEPTH)]),
        b_forget=jnp.stack([grads[l]["b_forget"] for l in range(DEPTH)]),
        sinks=jnp.stack([grads[l]["sinks"] for l in range(DEPTH)]),
        rel_bias=jnp.stack([grads[l]["rel_bias"] for l in range(DEPTH)]),
        final_norm_g=d_final[0])
    shapes = [small_w[n].shape for n in small_names]
    g_all = _all_gather8(_small_pack([small_g[n] for n in small_names]), "gather_small_grads")
    res = _adamw(_small_pack([small_w[n] for n in small_names])[None],
                    _small_pack([small_m[n] for n in small_names])[None],
                    _small_pack([small_v[n] for n in small_names])[None], g_all, "adamw_small")
    small_out = {n: [] for n in small_names}
    for r in res:
        for n, a in zip(small_names, _small_unpack(r[0], shapes)):
            small_out[n].append(a)
    off_b = sum(int(np.prod(s)) for s in shapes[:2])
    n_mod = DEPTH * 6 * D_MODEL
    dmod_all = g_all.reshape(8, -1)[:, off_b:off_b + n_mod].reshape(8, DEPTH, 6 * D_MODEL)
    dmod_sh = lax.dynamic_slice_in_dim(dmod_all, chip * n_ada, n_ada, axis=2).transpose(1, 0, 2)
    g_ada, got = _ada_bwd(c_all.T, dmod_sh, "ada_bwd", comm=_RowHalfReduce(sums0[:1]))
    parts0[0] = got[0]
    ada_out = _adamw(w_ada, m_w_ada, v_w_ada, flat2(g_ada)[None], "adamw_w_ada")

    big_out = {}
    as3 = lambda a: a.reshape(a.shape[0], -1, a.shape[-1])
    for n, p0, p1 in zip(big_names, parts0, parts1):
        res = _adamw(as3(big_w[n]), as3(big_m[n]), as3(big_v[n]), [p0, p1], "adamw_" + n)
        big_out[n] = [r.reshape(big_w[n].shape) for r in res]

    order = ("norm_mix_g", "norm_ffn_g", "w_ada", "b_ada", "w_in", "b_forget", "sinks", "rel_bias", "w_branch",
             "w_out", "w_ffn_in", "w_ffn_out", "final_norm_g")

    def pick(n, k):
        if n == "w_ada":
            return ada_out[k]
        if n in big_out:
            return big_out[n][k]
        return small_out[n][k]

    outs = [loss, grad_x]
    for k in range(4):
        outs += [pick(n, k) for n in order]
    return tuple(outs)
```

```python
import numpy as np
import jax
import jax.numpy as jnp
from jax import lax
from jax.experimental import pallas as pl
from jax.experimental.pallas import tpu as pltpu

F32 = jnp.float32
BF16 = jnp.bfloat16
SDS = jax.ShapeDtypeStruct

D_MODEL = 1024
DEPTH = 2
CHUNK = 64
HEAD_DIM = 64
EPS = 1e-6
NEG_INF = -1e30
N_HEADS = 8
A_KV_HEADS = 2
A_PREV = 2
C_PREV = 8
REL_CLIP = 128
N_REL = 2 * REL_CLIP + 1
N_REL_PAD = 384
BRANCH_W = 512
FFN_H = 2816
FOX_BQ = 512
FOX_BK = 512
GF_COLS = 3200
N_IN_COLS = 6920
LANE = 128
VMEM_LIMIT = 48 * 1024 * 1024

ADAM_LR = 0.001
ADAM_B1 = 0.9
ADAM_B2 = 0.999
ADAM_EPS = 1e-08
ADAM_WD = 0.01
ADAM_STEP = 10

MESH = pl.DeviceIdType.MESH
ANY = pl.BlockSpec(memory_space=pl.ANY)
VMEM_SPEC = pl.BlockSpec(memory_space=pltpu.VMEM)


def _cparams(sem=None):
    return pltpu.CompilerParams(dimension_semantics=sem, vmem_limit_bytes=VMEM_LIMIT)


def _blk(n, cap):
    if n <= cap:
        return n
    best = None
    for m in range(LANE, cap + 1, LANE):
        if n % m == 0:
            best = m
    assert best is not None, (n, cap)
    return best


def _sigmoid(x):
    return 1.0 / (1.0 + jnp.exp(-x))


def _mm(a, b, *, mode, out_dtype, name, groups=1, cap_m=2048, cap_n=1024, cap_k=1408, col_quarters=False,
        comm=None):
    G = groups
    assert not col_quarters or mode == "tn"
    if mode == "nn":
        M, K, N = a.shape[0], a.shape[1] // G, b.shape[1]
        assert b.shape[0] == G * K
    elif mode == "nt":
        M, K, N = a.shape[0], a.shape[1] // G, b.shape[0] // G
        assert b.shape[1] == K
    else:
        K, M, N = a.shape[0], a.shape[1] // G, b.shape[1] // G
        assert b.shape[0] == K
    bm, bn, bk = _blk(M, cap_m), _blk(N // 4 if col_quarters else N, cap_n), _blk(K, cap_k)
    nm, nn, nk = M // bm, N // bn, K // bk
    if mode == "nn":
        a_spec = pl.BlockSpec((bm, bk), lambda g, i, j, k: (i, g * nk + k))
        b_spec = pl.BlockSpec((bk, bn), lambda g, i, j, k: (g * nk + k, j))
        o_spec = pl.BlockSpec((bm, bn), lambda g, i, j, k: (i, g * nn + j))
        dims = (((1,), (0,)), ((), ()))
        out_shape = (M, G * N)
    elif mode == "nt":
        a_spec = pl.BlockSpec((bm, bk), lambda g, i, j, k: (i, g * nk + k))
        b_spec = pl.BlockSpec((bn, bk), lambda g, i, j, k: (g * nn + j, k))
        o_spec = pl.BlockSpec((bm, bn), lambda g, i, j, k: (i, g * nn + j))
        dims = (((1,), (1,)), ((), ()))
        out_shape = (M, G * N)
    else:
        a_spec = pl.BlockSpec((bk, bm), lambda g, i, j, k: (k, g * nm + i))
        b_spec = pl.BlockSpec((bk, bn), lambda g, i, j, k: (k, g * nn + j))
        dims = (((0,), (0,)), ((), ()))
        if col_quarters:
            nq = nn // 4
            o_spec = pl.BlockSpec((1, bm, bn), lambda g, i, j, k: (j // nq, g * nm + i, j % nq))
            out_shape = (4, G * M, N // 4)
        else:
            o_spec = pl.BlockSpec((bm, bn), lambda g, i, j, k: (g * nm + i, j))
            out_shape = (G * M, N)

    def product(a_ref, b_ref):
        return lax.dot_general(a_ref[...].astype(BF16), b_ref[...].astype(BF16), dims, preferred_element_type=F32)

    def body_one(a_ref, b_ref, o_ref):
        o_ref[...] = product(a_ref, b_ref).astype(o_ref.dtype).reshape(o_ref.shape)

    def body_acc(a_ref, b_ref, o_ref, acc_ref):
        k = pl.program_id(3)

        @pl.when(k == 0)
        def _():
            acc_ref[...] = jnp.zeros_like(acc_ref)

        acc_ref[...] += product(a_ref, b_ref)

        @pl.when(k == nk - 1)
        def _():
            o_ref[...] = acc_ref[...].astype(o_ref.dtype).reshape(o_ref.shape)

    res, got = _call_hosting(
        body_one if nk == 1 else body_acc, comm=comm, grid=(G, nm, nn, nk), in_specs=[a_spec, b_spec],
        out_specs=[o_spec], out_shape=[SDS(out_shape, out_dtype)],
        scratch_shapes=[] if nk == 1 else [pltpu.VMEM((bm, bn), F32)], name=name, args=(a, b),
        semantics=("parallel", "parallel", "parallel", "arbitrary"))
    return res[0] if comm is None else (res[0], got)


def _rows(tm, n, col=0):
    return pl.BlockSpec((tm, n), lambda i: (i, col))


def _vec(n):
    return pl.BlockSpec((1, n), lambda i: (0, 0))


def _tm(S):
    return min(S, 512)


def _norm_mod_fwd(x, g, sc, sh, name):
    S, Dm = x.shape
    tm = _tm(S)

    def body(x_ref, g_ref, sc_ref, sh_ref, h_ref):
        xv = x_ref[...]
        r = lax.rsqrt(jnp.mean(xv * xv, axis=-1, keepdims=True) + EPS)
        h_ref[...] = ((xv * r) * g_ref[...] * (1.0 + sc_ref[...]) + sh_ref[...]).astype(h_ref.dtype)

    return pl.pallas_call(
        body, grid=(S // tm,), in_specs=[_rows(tm, Dm), _vec(Dm), _vec(Dm), _vec(Dm)],
        out_specs=_rows(tm, Dm), out_shape=SDS((S, Dm), BF16),
        compiler_params=_cparams(("parallel",)), name=name)(x, g, sc, sh)


def _norm_mod_bwd(x, dh_list, dres, g, sc, name):
    S, Dm = x.shape
    tm = _tm(S)
    nh = len(dh_list)

    def body(*refs):
        x_ref = refs[0]
        dh_refs = refs[1:1 + nh]
        dres_ref, g_ref, sc_ref, dx_ref, dsc_ref, dsh_ref, dg_ref = refs[1 + nh:]
        i = pl.program_id(0)

        @pl.when(i == 0)
        def _():
            dsc_ref[...] = jnp.zeros_like(dsc_ref)
            dsh_ref[...] = jnp.zeros_like(dsh_ref)
            dg_ref[...] = jnp.zeros_like(dg_ref)

        xv = x_ref[...]
        dh = dh_refs[0][...].astype(F32)
        for r_ in dh_refs[1:]:
            dh = dh + r_[...].astype(F32)
        gv = g_ref[...]
        r = lax.rsqrt(jnp.mean(xv * xv, axis=-1, keepdims=True) + EPS)
        xn = xv * r
        xg = xn * gv
        dsh_ref[...] += jnp.sum(dh, axis=0, keepdims=True)
        dsc_ref[...] += jnp.sum(dh * xg, axis=0, keepdims=True)
        dxg = dh * (1.0 + sc_ref[...])
        dg_ref[...] += jnp.sum(dxg * xn, axis=0, keepdims=True)
        dxn = dxg * gv
        dx_ref[...] = dres_ref[...] + r * (dxn - xn * jnp.mean(dxn * xn, axis=-1, keepdims=True))

    return pl.pallas_call(
        body, grid=(S // tm,),
        in_specs=[_rows(tm, Dm)] * (2 + nh) + [_vec(Dm), _vec(Dm)],
        out_specs=[_rows(tm, Dm), _vec(Dm), _vec(Dm), _vec(Dm)],
        out_shape=[SDS((S, Dm), F32), SDS((1, Dm), F32), SDS((1, Dm), F32), SDS((1, Dm), F32)],
        compiler_params=_cparams(("arbitrary",)), name=name)(x, *dh_list, dres, g, sc)


def _resid_fwd(x, val, g, name):
    S, Dm = x.shape
    tm = _tm(S)

    def body(x_ref, v_ref, g_ref, o_ref):
        o_ref[...] = x_ref[...] + g_ref[...] * v_ref[...].astype(F32)

    return pl.pallas_call(
        body, grid=(S // tm,), in_specs=[_rows(tm, Dm), _rows(tm, Dm), _vec(Dm)],
        out_specs=_rows(tm, Dm), out_shape=SDS((S, Dm), F32),
        compiler_params=_cparams(("parallel",)), name=name)(x, val, g)


def _resid_bwd(dx, val, g, name):
    S, Dm = dx.shape
    tm = _tm(S)

    def body(dx_ref, v_ref, g_ref, dg_ref, dv_ref):
        @pl.when(pl.program_id(0) == 0)
        def _():
            dg_ref[...] = jnp.zeros_like(dg_ref)

        dxv = dx_ref[...]
        dg_ref[...] += jnp.sum(dxv * v_ref[...].astype(F32), axis=0, keepdims=True)
        dv_ref[...] = (dxv * g_ref[...]).astype(dv_ref.dtype)

    return pl.pallas_call(
        body, grid=(S // tm,), in_specs=[_rows(tm, Dm), _rows(tm, Dm), _vec(Dm)],
        out_specs=[_vec(Dm), _rows(tm, Dm)], out_shape=[SDS((1, Dm), F32), SDS((S, Dm), BF16)],
        compiler_params=_cparams(("arbitrary",)), name=name)(dx, val, g)


def _merge_fwd(y, gf, name):
    S = y.shape[0]
    tm = _tm(S)
    W = 3 * D_MODEL

    def body(y_ref, g_ref, o_ref):
        acc = None
        for k in range(3):
            sl = slice(k * D_MODEL, (k + 1) * D_MODEL)
            t = _sigmoid(g_ref[:, sl]) * y_ref[:, sl].astype(F32)
            acc = t if acc is None else acc + t
        o_ref[...] = acc.astype(o_ref.dtype)

    return pl.pallas_call(
        body, grid=(S // tm,), in_specs=[_rows(tm, W), _rows(tm, W)],
        out_specs=_rows(tm, D_MODEL), out_shape=SDS((S, D_MODEL), BF16),
        compiler_params=_cparams(("parallel",)), name=name)(y, gf)


def _merge_bwd(dm, y, gf, name):
    S = y.shape[0]
    tm = _tm(S)
    W = 3 * D_MODEL

    def body(dm_ref, y_ref, g_ref, dy_ref, dg_ref):
        dmv = dm_ref[...].astype(F32)
        for k in range(3):
            sl = slice(k * D_MODEL, (k + 1) * D_MODEL)
            sg = _sigmoid(g_ref[:, sl])
            dy_ref[:, sl] = (dmv * sg).astype(dy_ref.dtype)
            dg_ref[:, sl] = (dmv * y_ref[:, sl].astype(F32) * (sg * (1.0 - sg))).astype(dg_ref.dtype)

    return pl.pallas_call(
        body, grid=(S // tm,), in_specs=[_rows(tm, D_MODEL), _rows(tm, W), _rows(tm, W)],
        out_specs=[_rows(tm, W), _rows(tm, W)], out_shape=[SDS((S, W), BF16), SDS((S, W), BF16)],
        compiler_params=_cparams(("parallel",)), name=name)(dm, y, gf)


def _swiglu_fwd(u, name):
    S = u.shape[0]
    tm = _tm(S)

    def body(g_ref, u_ref, a_ref):
        gv = g_ref[...].astype(F32)
        a_ref[...] = (gv * _sigmoid(gv) * u_ref[...].astype(F32)).astype(a_ref.dtype)

    return pl.pallas_call(
        body, grid=(S // tm,), in_specs=[_rows(tm, FFN_H, 0), _rows(tm, FFN_H, 1)],
        out_specs=_rows(tm, FFN_H), out_shape=SDS((S, FFN_H), BF16),
        compiler_params=_cparams(("parallel",)), name=name)(u, u)


def _swiglu_bwd(da, u, name):
    S = u.shape[0]
    tm = _tm(S)

    def body(da_ref, g_ref, u_ref, du_ref):
        dav = da_ref[...].astype(F32)
        gv = g_ref[...].astype(F32)
        sg = _sigmoid(gv)
        du_ref[:, 0:FFN_H] = (dav * u_ref[...].astype(F32) * (sg * (1.0 + gv * (1.0 - sg)))).astype(du_ref.dtype)
        du_ref[:, FFN_H:2 * FFN_H] = (dav * (gv * sg)).astype(du_ref.dtype)

    return pl.pallas_call(
        body, grid=(S // tm,), in_specs=[_rows(tm, FFN_H), _rows(tm, FFN_H, 0), _rows(tm, FFN_H, 1)],
        out_specs=_rows(tm, 2 * FFN_H), out_shape=SDS((S, 2 * FFN_H), BF16),
        compiler_params=_cparams(("parallel",)), name=name)(da, u, u)


def _final_loss(x, g, target, name):
    S, Dm = x.shape
    tm = _tm(S)

    def body(x_ref, g_ref, t_ref, loss_ref, dx_ref, dg_ref):
        @pl.when(pl.program_id(0) == 0)
        def _():
            loss_ref[...] = jnp.zeros_like(loss_ref)
            dg_ref[...] = jnp.zeros_like(dg_ref)

        xv = x_ref[...]
        gv = g_ref[...]
        r = lax.rsqrt(jnp.mean(xv * xv, axis=-1, keepdims=True) + EPS)
        xn = xv * r
        err = xn * gv - t_ref[...]
        row = jnp.mean(err * err, axis=-1, keepdims=True)
        loss_ref[...] += 0.5 * jnp.sum(row, axis=0, keepdims=True)
        dy = err * (1.0 / Dm)
        dg_ref[...] += jnp.sum(dy * xn, axis=0, keepdims=True)
        dxn = dy * gv
        dx_ref[...] = r * (dxn - xn * jnp.mean(dxn * xn, axis=-1, keepdims=True))

    return pl.pallas_call(
        body, grid=(S // tm,), in_specs=[_rows(tm, Dm), _vec(Dm), _rows(tm, Dm)],
        out_specs=[pl.BlockSpec((1, 1), lambda i: (0, 0)), _rows(tm, Dm), _vec(Dm)],
        out_shape=[SDS((1, 1), F32), SDS((S, Dm), F32), SDS((1, Dm), F32)],
        compiler_params=_cparams(("arbitrary",)), name=name)(x, g, target)


PAIR = 2 * CHUNK


def _bandT_softmax(kg, qTg, bias, sink, valid):
    s = jnp.dot(kg, qTg, preferred_element_type=F32)
    s = jnp.where(valid, s + bias, NEG_INF)
    m = jnp.maximum(jnp.max(s, axis=0, keepdims=True), sink)
    e = jnp.exp(s - m)
    es = jnp.exp(sink - m)
    inv = 1.0 / (jnp.sum(e, axis=0, keepdims=True) + es)
    return e * inv, es * inv


def _pad_copy_rows(dst, src, pad, S):
    dst[:, 0:pad, :] = jnp.zeros((dst.shape[0], pad, dst.shape[2]), dst.dtype)
    dst[:, pad:pad + S, :] = src[...]


def _pad_copy_lanes(dst, src, pad, S):
    dst[:, 0:pad] = jnp.zeros((dst.shape[0], pad), dst.dtype)
    dst[:, pad:pad + S] = src[...]


def _fm(arg):
    return arg if isinstance(arg, tuple) else (arg, 0)


def _fm_spec(rows, S, row0):
    off, rem = divmod(row0, rows)
    assert rem == 0
    return pl.BlockSpec((rows, S), lambda i: (off + i, 0))


def _bandT_fwd(qT, k_h, vT, bias, sink, *, GQ, GK, P, kvoff, name, comm=None):
    (qT, q0), (vT, v0) = _fm(qT), _fm(vT)
    S = qT.shape[1]
    ng = bias.shape[0] // GQ
    BU = (P + 2) * CHUNK
    pad = P * CHUNK
    npair = S // PAIR

    def body(qT_ref, k_ref, vT_ref, b_ref, s_ref, oT_ref, kp, vTp):
        _pad_copy_rows(kp, k_ref, pad, S)
        _pad_copy_lanes(vTp, vT_ref, pad, S)
        rowi = lax.broadcasted_iota(jnp.int32, (BU, PAIR), 0)

        def step(n2, carry):
            r = pl.multiple_of(n2 * PAIR, PAIR)
            valid = rowi >= (P - 2 * n2) * CHUNK
            for g in range(GQ):
                kv = kvoff(g)
                hs = slice(g * HEAD_DIM, (g + 1) * HEAD_DIM)
                kvs = slice(kv * HEAD_DIM, (kv + 1) * HEAD_DIM)
                qTg = qT_ref[hs, pl.ds(r, PAIR)] * 0.125
                p, _ = _bandT_softmax(kp[kv, pl.ds(r, BU), :], qTg, b_ref[g], s_ref[g, 0:1, :], valid)
                oTg = jnp.dot(vTp[kvs, pl.ds(r, BU)], p.astype(BF16), preferred_element_type=F32)
                oT_ref[hs, pl.ds(r, PAIR)] = oTg.astype(oT_ref.dtype)
            return carry

        lax.fori_loop(0, npair, step, 0, unroll=min(2, npair))

    res, got = _call_hosting(
        body, comm=comm, grid=(ng,),
        in_specs=[_fm_spec(GQ * HEAD_DIM, S, q0),
                  pl.BlockSpec((GK, S, HEAD_DIM), lambda i: (i, 0, 0)),
                  _fm_spec(GK * HEAD_DIM, S, v0),
                  pl.BlockSpec((GQ, BU, PAIR), lambda i: (i, 0, 0)),
                  pl.BlockSpec((GQ, 8, LANE), lambda i: (i, 0, 0))],
        out_specs=[pl.BlockSpec((GQ * HEAD_DIM, S), lambda i: (i, 0))],
        out_shape=[SDS((ng * GQ * HEAD_DIM, S), BF16)],
        scratch_shapes=[pltpu.VMEM((GK, S + pad, HEAD_DIM), BF16), pltpu.VMEM((GK * HEAD_DIM, S + pad), BF16)],
        name=name, args=(qT, k_h, vT, bias, sink))
    return res[0], got


def _bandT_bwd(qT, q_h, k_h, kT, v_h, doT, do_h, bias, sink, *, GQ, GK, P, kvoff, name, comm=None):
    (qT, q0), (kT, k0), (doT, d0) = _fm(qT), _fm(kT), _fm(doT)
    S = qT.shape[1]
    ng = bias.shape[0] // GQ
    BU = (P + 2) * CHUNK
    pad = P * CHUNK
    npair = S // PAIR

    def body(qT_ref, q_ref, k_ref, kT_ref, v_ref, doT_ref, do_ref, b_ref, s_ref,
             dqT_ref, dk_ref, dv_ref, db_ref, dsk_ref, kp, kTp, vp, dkp, dvp):
        _pad_copy_rows(kp, k_ref, pad, S)
        _pad_copy_rows(vp, v_ref, pad, S)
        _pad_copy_lanes(kTp, kT_ref, pad, S)
        dkp[...] = jnp.zeros_like(dkp)
        dvp[...] = jnp.zeros_like(dvp)
        db_ref[...] = jnp.zeros_like(db_ref)
        rowi = lax.broadcasted_iota(jnp.int32, (BU, PAIR), 0)

        def step(n2, dsink):
            r = pl.multiple_of(n2 * PAIR, PAIR)
            valid = rowi >= (P - 2 * n2) * CHUNK
            new = []
            for g in range(GQ):
                kv = kvoff(g)
                hs = slice(g * HEAD_DIM, (g + 1) * HEAD_DIM)
                kvs = slice(kv * HEAD_DIM, (kv + 1) * HEAD_DIM)
                qTg = qT_ref[hs, pl.ds(r, PAIR)] * 0.125
                p, ps = _bandT_softmax(kp[kv, pl.ds(r, BU), :], qTg, b_ref[g], s_ref[g, 0:1, :], valid)
                dp = jnp.dot(vp[kv, pl.ds(r, BU), :], doT_ref[hs, pl.ds(r, PAIR)], preferred_element_type=F32)
                delta = jnp.sum(p * dp, axis=0, keepdims=True)
                ds = p * (dp - delta)
                new.append(dsink[g] - ps * delta)
                db_ref[g] += ds
                dsb = ds.astype(BF16)
                dq = jnp.dot(kTp[kvs, pl.ds(r, BU)], dsb, preferred_element_type=F32) * 0.125
                dqT_ref[hs, pl.ds(r, PAIR)] = dq.astype(dqT_ref.dtype)
                dkp[kv, pl.ds(r, BU), :] += jnp.dot(dsb, q_ref[g, pl.ds(r, PAIR), :] * 0.125,
                                                    preferred_element_type=F32)
                dvp[kv, pl.ds(r, BU), :] += jnp.dot(p.astype(BF16), do_ref[g, pl.ds(r, PAIR), :],
                                                    preferred_element_type=F32)
            return tuple(new)

        dsink = lax.fori_loop(0, npair, step, tuple(jnp.zeros((1, PAIR), F32) for _ in range(GQ)))
        for g in range(GQ):
            dsk_ref[g] = jnp.broadcast_to(jnp.sum(dsink[g], axis=1, keepdims=True), (8, LANE))
        dk_ref[...] = dkp[:, pad:pad + S, :].astype(dk_ref.dtype)
        dv_ref[...] = dvp[:, pad:pad + S, :].astype(dv_ref.dtype)

    qTs = pl.BlockSpec((GQ * HEAD_DIM, S), lambda i: (i, 0))
    qhs = pl.BlockSpec((GQ, S, HEAD_DIM), lambda i: (i, 0, 0))
    khs = pl.BlockSpec((GK, S, HEAD_DIM), lambda i: (i, 0, 0))
    bs = pl.BlockSpec((GQ, BU, PAIR), lambda i: (i, 0, 0))
    ss = pl.BlockSpec((GQ, 8, LANE), lambda i: (i, 0, 0))
    nkv = ng * GK
    return _call_hosting(
        body, comm=comm, grid=(ng,),
        in_specs=[_fm_spec(GQ * HEAD_DIM, S, q0), qhs, khs, _fm_spec(GK * HEAD_DIM, S, k0), khs,
                  _fm_spec(GQ * HEAD_DIM, S, d0), qhs, bs, ss],
        out_specs=[qTs, khs, khs, bs, ss],
        out_shape=[SDS((ng * GQ * HEAD_DIM, S), BF16), SDS((nkv, S, HEAD_DIM), BF16), SDS((nkv, S, HEAD_DIM), BF16),
                   SDS((ng * GQ, BU, PAIR), F32), SDS((ng * GQ, 8, LANE), F32)],
        scratch_shapes=[pltpu.VMEM((GK, S + pad, HEAD_DIM), BF16), pltpu.VMEM((GK * HEAD_DIM, S + pad), BF16),
                        pltpu.VMEM((GK, S + pad, HEAD_DIM), BF16),
                        pltpu.VMEM((GK, S + pad, HEAD_DIM), F32), pltpu.VMEM((GK, S + pad, HEAD_DIM), F32)],
        name=name, args=(qT, q_h, k_h, kT, v_h, doT, do_h, bias, sink))


def _pair_table(tab):
    t = jnp.transpose(tab, (0, 2, 1))
    lo = jnp.pad(t, ((0, 0), (0, CHUNK), (0, 0)), constant_values=NEG_INF)
    hi = jnp.pad(t, ((0, 0), (CHUNK, 0), (0, 0)), constant_values=NEG_INF)
    return jnp.concatenate([lo, hi], axis=2)


def _unpair_table(d):
    band = d.shape[1] - CHUNK
    return jnp.transpose(d[:, 0:band, 0:CHUNK] + d[:, CHUNK:CHUNK + band, CHUNK:PAIR], (0, 2, 1))


def _heads(a, n):
    return jnp.transpose(a.reshape(a.shape[0], n, HEAD_DIM), (1, 0, 2))


def _unheads(a):
    return jnp.transpose(a, (1, 0, 2)).reshape(a.shape[1], a.shape[0] * HEAD_DIM)


def _foxT_logits(kj, qTg, cq, ck, r, c, rowi, coli):
    s = jnp.dot(kj, qTg, preferred_element_type=F32)
    s = s + cq - ck
    return jnp.where(c + rowi <= r + coli, s, NEG_INF)


def _foxT_fwd(qT, k_h, vT, ck, cq, name, comm=None):
    (qT, q0), (vT, v0) = _fm(qT), _fm(vT)
    S = qT.shape[1]
    npair = k_h.shape[0] // 2
    BQ, BK = min(FOX_BQ, S), min(FOX_BK, S)
    nq = S // BQ
    heads = [slice(g * HEAD_DIM, (g + 1) * HEAD_DIM) for g in range(2)]

    def body(qT_ref, k_ref, vT_ref, ck_ref, cq_ref, oT_ref, lse_ref):
        rowi = lax.broadcasted_iota(jnp.int32, (BK, BQ), 0)
        coli = lax.broadcasted_iota(jnp.int32, (BK, BQ), 1)

        def qstep(i, carry):
            r = pl.multiple_of(i * BQ, BQ)
            qs = [qT_ref[hs, pl.ds(r, BQ)] * 0.125 for hs in heads]
            cqs = [cq_ref[g, :, pl.ds(r, BQ)] for g in range(2)]

            def kstep(j, st):
                c = pl.multiple_of(j * BK, BK)
                new = []
                for g, hs in enumerate(heads):
                    m, l, acc = st[g]
                    s = _foxT_logits(k_ref[g, pl.ds(c, BK), :], qs[g], cqs[g], ck_ref[g, pl.ds(c, BK), :],
                                     r, c, rowi, coli)
                    mn = jnp.maximum(m, jnp.max(s, axis=0, keepdims=True))
                    al = jnp.exp(m - mn)
                    e = jnp.exp(s - mn)
                    l = al * l + jnp.sum(e, axis=0, keepdims=True)
                    acc = al * acc + jnp.dot(vT_ref[hs, pl.ds(c, BK)], e.astype(BF16), preferred_element_type=F32)
                    new.append((mn, l, acc))
                return tuple(new)

            init = (jnp.full((1, BQ), NEG_INF, F32), jnp.zeros((1, BQ), F32), jnp.zeros((HEAD_DIM, BQ), F32))
            st = lax.fori_loop(0, (r + BQ + BK - 1) // BK, kstep, (init, init))
            for g, hs in enumerate(heads):
                m, l, acc = st[g]
                oT_ref[hs, pl.ds(r, BQ)] = (acc * (1.0 / l)).astype(oT_ref.dtype)
                lse_ref[g, :, pl.ds(r, BQ)] = m + jnp.log(l)
            return carry

        lax.fori_loop(0, nq, qstep, 0)

    fT = pl.BlockSpec((LANE, S), lambda i: (i, 0))
    hm = pl.BlockSpec((2, S, HEAD_DIM), lambda i: (i, 0, 0))
    col = pl.BlockSpec((2, S, 1), lambda i: (i, 0, 0))
    rw = pl.BlockSpec((2, 1, S), lambda i: (i, 0, 0))
    return _call_hosting(
        body, comm=comm, grid=(npair,), in_specs=[_fm_spec(LANE, S, q0), hm, _fm_spec(LANE, S, v0), col, rw],
        out_specs=[fT, rw],
        out_shape=[SDS((npair * LANE, S), BF16), SDS((2 * npair, 1, S), F32)], scratch_shapes=[],
        name=name, args=(qT, k_h, vT, ck, cq))


def _foxT_bwd(qT, q_h, k_h, kT, v_h, ck, cq, oT, doT, do_h, lse, name, comm=None):
    (qT, q0), (kT, k0), (doT, d0) = _fm(qT), _fm(kT), _fm(doT)
    S = qT.shape[1]
    npair = k_h.shape[0] // 2
    BQ, BK = min(FOX_BQ, S), min(FOX_BK, S)
    nq = S // BQ
    heads = [slice(g * HEAD_DIM, (g + 1) * HEAD_DIM) for g in range(2)]

    def body(qT_ref, q_ref, k_ref, kT_ref, v_ref, ck_ref, cq_ref, oT_ref, doT_ref, do_ref, lse_ref,
             dqT_ref, dk_ref, dv_ref, dck_ref, dcq_ref, dka, dva, qa_ref):
        qa_ref[:, :, 0:HEAD_DIM] = q_ref[...] * 0.125
        qa_ref[:, :, HEAD_DIM:LANE] = jnp.ones((2, S, LANE - HEAD_DIM), BF16)
        dka[...] = jnp.zeros_like(dka)
        dva[...] = jnp.zeros_like(dva)
        rowi = lax.broadcasted_iota(jnp.int32, (BK, BQ), 0)
        coli = lax.broadcasted_iota(jnp.int32, (BK, BQ), 1)

        def qstep(i, carry):
            r = pl.multiple_of(i * BQ, BQ)
            qs = [qT_ref[hs, pl.ds(r, BQ)] * 0.125 for hs in heads]
            dos = [doT_ref[hs, pl.ds(r, BQ)] for hs in heads]
            deltas = [jnp.sum(dos[g].astype(F32) * oT_ref[hs, pl.ds(r, BQ)].astype(F32), axis=0, keepdims=True)
                      for g, hs in enumerate(heads)]
            cqs = [cq_ref[g, :, pl.ds(r, BQ)] for g in range(2)]
            lses = [lse_ref[g, :, pl.ds(r, BQ)] for g in range(2)]

            def kstep(j, st):
                c = pl.multiple_of(j * BK, BK)
                new = []
                for g, hs in enumerate(heads):
                    dq, rs = st[g]
                    s = _foxT_logits(k_ref[g, pl.ds(c, BK), :], qs[g], cqs[g], ck_ref[g, pl.ds(c, BK), :],
                                     r, c, rowi, coli)
                    p = jnp.exp(s - lses[g])
                    dp = jnp.dot(v_ref[g, pl.ds(c, BK), :], dos[g], preferred_element_type=F32)
                    ds = p * (dp - deltas[g])
                    dsb = ds.astype(BF16)
                    dka[g, pl.ds(c, BK), :] += jnp.dot(dsb, qa_ref[g, pl.ds(r, BQ), :], preferred_element_type=F32)
                    dva[g, pl.ds(c, BK), :] += jnp.dot(p.astype(BF16), do_ref[g, pl.ds(r, BQ), :],
                                                      preferred_element_type=F32)
                    new.append((dq + jnp.dot(kT_ref[hs, pl.ds(c, BK)], dsb, preferred_element_type=F32),
                                rs + jnp.sum(dsb.astype(F32), axis=0, keepdims=True)))
                return tuple(new)

            init = (jnp.zeros((HEAD_DIM, BQ), F32), jnp.zeros((1, BQ), F32))
            st = lax.fori_loop(0, (r + BQ + BK - 1) // BK, kstep, (init, init))
            for g, hs in enumerate(heads):
                dqT_ref[hs, pl.ds(r, BQ)] = (st[g][0] * 0.125).astype(dqT_ref.dtype)
                dcq_ref[g, :, pl.ds(r, BQ)] = st[g][1]
            return carry

        lax.fori_loop(0, nq, qstep, 0)
        dk_ref[...] = dka[:, :, 0:HEAD_DIM].astype(dk_ref.dtype)
        dck_ref[...] = -dka[:, :, HEAD_DIM:HEAD_DIM + 1]
        dv_ref[...] = dva[...].astype(dv_ref.dtype)

    fT = pl.BlockSpec((LANE, S), lambda i: (i, 0))
    hm = pl.BlockSpec((2, S, HEAD_DIM), lambda i: (i, 0, 0))
    col = pl.BlockSpec((2, S, 1), lambda i: (i, 0, 0))
    rw = pl.BlockSpec((2, 1, S), lambda i: (i, 0, 0))
    nh = 2 * npair
    return _call_hosting(
        body, comm=comm, grid=(npair,),
        in_specs=[_fm_spec(LANE, S, q0), hm, hm, _fm_spec(LANE, S, k0), hm, col, rw, fT, _fm_spec(LANE, S, d0), hm, rw],
        out_specs=[fT, hm, hm, col, rw],
        out_shape=[SDS((npair * LANE, S), BF16), SDS((nh, S, HEAD_DIM), BF16), SDS((nh, S, HEAD_DIM), BF16),
                   SDS((nh, S, 1), F32), SDS((nh, 1, S), F32)],
        scratch_shapes=[pltpu.VMEM((2, S, LANE), F32), pltpu.VMEM((2, S, HEAD_DIM), F32),
                        pltpu.VMEM((2, S, LANE), BF16)],
        name=name, args=(qT, q_h, k_h, kT, v_h, ck, cq, oT, doT, do_h, lse))


def _split3(x):
    hi = x.astype(BF16)
    r1 = x - hi.astype(F32)
    mid = r1.astype(BF16)
    lo = (r1 - mid.astype(F32)).astype(BF16)
    return hi, mid, lo


def _tri_dot(tri, x):
    hi, mid, lo = _split3(x)
    return (jnp.dot(tri, hi, preferred_element_type=F32) + jnp.dot(tri, mid, preferred_element_type=F32)
            + jnp.dot(tri, lo, preferred_element_type=F32))


def _fox_cum(gf, bfo, name):
    S = gf.shape[0]
    nb = S // LANE
    fcol = (GF_COLS - LANE) // LANE

    def body(f_ref, b_ref, cum_ref):
        row = lax.broadcasted_iota(jnp.int32, (LANE, LANE), 0)
        col = lax.broadcasted_iota(jnp.int32, (LANE, LANE), 1)
        tri = jnp.where(row >= col, 1.0, 0.0).astype(BF16)
        carry = jnp.zeros((1, LANE), F32)
        for t in range(nb):
            xl = f_ref[t * LANE:(t + 1) * LANE, :] + b_ref[...]
            lf = jnp.minimum(xl, 0.0) - jnp.log(1.0 + jnp.exp(-jnp.abs(xl)))
            cblk = _tri_dot(tri, lf) + carry
            cum_ref[t * LANE:(t + 1) * LANE, :] = cblk
            carry = cblk[LANE - 1:LANE, :]

    return pl.pallas_call(
        body, grid=(1,), in_specs=[pl.BlockSpec((S, LANE), lambda i: (0, fcol)), _vec(LANE)],
        out_specs=pl.BlockSpec((S, LANE), lambda i: (0, 0)), out_shape=SDS((S, LANE), F32),
        compiler_params=_cparams(("arbitrary",)), name=name)(gf, bfo)


def _fox_cum_bwd(gf, bfo, dcum, name):
    S = gf.shape[0]
    nb = S // LANE
    fcol = (GF_COLS - LANE) // LANE

    def body(f_ref, b_ref, dc_ref, df_ref, db_ref):
        row = lax.broadcasted_iota(jnp.int32, (LANE, LANE), 0)
        col = lax.broadcasted_iota(jnp.int32, (LANE, LANE), 1)
        tri = jnp.where(row <= col, 1.0, 0.0).astype(BF16)
        carry = jnp.zeros((1, LANE), F32)
        tot = jnp.zeros((1, LANE), F32)
        for t in range(nb - 1, -1, -1):
            rows = slice(t * LANE, (t + 1) * LANE)
            dlf = _tri_dot(tri, dc_ref[rows, :]) + carry
            carry = dlf[0:1, :]
            xl = f_ref[rows, :] + b_ref[...]
            dfl = dlf * (1.0 / (1.0 + jnp.exp(xl)))
            df_ref[rows, :] = dfl.astype(df_ref.dtype)
            tot = tot + jnp.sum(dfl, axis=0, keepdims=True)
        db_ref[...] = tot

    return pl.pallas_call(
        body, grid=(1,),
        in_specs=[pl.BlockSpec((S, LANE), lambda i: (0, fcol)), _vec(LANE), pl.BlockSpec((S, LANE), lambda i: (0, 0))],
        out_specs=[pl.BlockSpec((S, LANE), lambda i: (0, 0)), _vec(LANE)],
        out_shape=[SDS((S, LANE), BF16), SDS((1, LANE), F32)],
        compiler_params=_cparams(("arbitrary",)), name=name)(gf, bfo, dcum)


REL_FAR = C_PREV * CHUNK - REL_CLIP


def _rel_onehot(qi, band):
    w = band - REL_FAR
    r = lax.broadcasted_iota(jnp.int32, (N_REL_PAD, w), 0)
    j = lax.broadcasted_iota(jnp.int32, (N_REL_PAD, w), 1) + REL_FAR
    idx = jnp.clip(C_PREV * CHUNK + qi - j, -REL_CLIP, REL_CLIP) + REL_CLIP
    return jnp.where(r == idx, 1.0, 0.0).astype(BF16)


def _rel_expand(rel, name):
    band = (C_PREV + 1) * CHUNK

    def body(rel_ref, o_ref):
        hi, mid, lo = _split3(rel_ref[...])
        far = jnp.broadcast_to(rel_ref[:, 2 * REL_CLIP:2 * REL_CLIP + 1], (N_HEADS, REL_FAR))

        def row(qi, carry):
            oh = _rel_onehot(qi, band)
            o_ref[qi, :, 0:REL_FAR] = far
            o_ref[qi, :, REL_FAR:band] = (jnp.dot(hi, oh, preferred_element_type=F32)
                                          + jnp.dot(mid, oh, preferred_element_type=F32)
                                          + jnp.dot(lo, oh, preferred_element_type=F32))
            return carry

        lax.fori_loop(0, CHUNK, row, 0, unroll=2)

    return pl.pallas_call(
        body, grid=(1,), in_specs=[pl.BlockSpec((N_HEADS, N_REL_PAD), lambda i: (0, 0))],
        out_specs=pl.BlockSpec((CHUNK, N_HEADS, band), lambda i: (0, 0, 0)),
        out_shape=SDS((CHUNK, N_HEADS, band), F32),
        compiler_params=_cparams(("arbitrary",)), name=name)(rel)


def _rel_reduce(dbias, name):
    band = (C_PREV + 1) * CHUNK
    NT = (((1,), (1,)), ((), ()))

    def body(d_ref, o_ref):
        def row(qi, st):
            acc, far = st
            oh = _rel_onehot(qi, band)
            hi, mid, lo = _split3(d_ref[qi, :, REL_FAR:band])
            acc = acc + (lax.dot_general(hi, oh, NT, preferred_element_type=F32)
                         + lax.dot_general(mid, oh, NT, preferred_element_type=F32)
                         + lax.dot_general(lo, oh, NT, preferred_element_type=F32))
            return acc, far + jnp.sum(d_ref[qi, :, 0:REL_FAR], axis=-1, keepdims=True)

        acc, far = lax.fori_loop(0, CHUNK, row, (jnp.zeros((N_HEADS, N_REL_PAD), F32), jnp.zeros((N_HEADS, 1), F32)),
                                 unroll=2)
        col = lax.broadcasted_iota(jnp.int32, (N_HEADS, N_REL_PAD), 1)
        o_ref[...] = acc + jnp.where(col == 2 * REL_CLIP, far, 0.0)

    return pl.pallas_call(
        body, grid=(1,), in_specs=[pl.BlockSpec((CHUNK, N_HEADS, band), lambda i: (0, 0, 0))],
        out_specs=pl.BlockSpec((N_HEADS, N_REL_PAD), lambda i: (0, 0)),
        out_shape=SDS((N_HEADS, N_REL_PAD), F32),
        compiler_params=_cparams(("arbitrary",)), name=name)(dbias)


def _alibi_table():
    qi = np.arange(CHUNK)[:, None]
    j = np.arange((A_PREV + 1) * CHUNK)[None, :]
    dist = np.abs(A_PREV * CHUNK + qi - j).astype(np.float32)
    slopes = np.exp2(-8.0 * np.arange(1, N_HEADS + 1, dtype=np.float32) / N_HEADS).astype(np.float32)
    return jnp.asarray(-slopes[:, None, None] * dist[None])


def _ada_fwd(c_all, w, b, name):
    n = w.shape[2]

    def body(c_ref, w_ref, b_ref, o_ref):
        cv = c_ref[...]
        cond = (cv * _sigmoid(cv)).astype(BF16)
        o_ref[0] = jnp.dot(cond, w_ref[0].astype(BF16), preferred_element_type=F32) + b_ref[0]

    return pl.pallas_call(
        body, grid=(DEPTH,),
        in_specs=[pl.BlockSpec((16, D_MODEL), lambda l: (0, 0)), pl.BlockSpec((1, D_MODEL, n), lambda l: (l, 0, 0)),
                  pl.BlockSpec((1, 1, n), lambda l: (l, 0, 0))],
        out_specs=pl.BlockSpec((1, 16, n), lambda l: (l, 0, 0)), out_shape=SDS((DEPTH, 16, n), F32),
        compiler_params=_cparams(("parallel",)), name=name)(c_all, w, b)


def _ada_bwd(c_t, dmod, name, comm=None):
    n = dmod.shape[2]
    bn = _blk(n, 512)
    tr = 256

    def body(c_ref, d_ref, o_ref):
        cv = c_ref[...]
        cond = (cv * _sigmoid(cv)).astype(BF16).astype(F32)
        dm = d_ref[0].astype(BF16).astype(F32)
        acc = cond[:, 0:1] * dm[0:1, :]
        for b_ in range(1, 8):
            acc = acc + cond[:, b_:b_ + 1] * dm[b_:b_ + 1, :]
        o_ref[0] = acc

    res, got = _call_hosting(
        body, comm=comm, grid=(DEPTH, D_MODEL // tr, n // bn),
        in_specs=[pl.BlockSpec((tr, 8), lambda l, i, j: (i, 0)), pl.BlockSpec((1, 8, bn), lambda l, i, j: (l, 0, j))],
        out_specs=[pl.BlockSpec((1, tr, bn), lambda l, i, j: (l, i, j))], out_shape=[SDS((DEPTH, D_MODEL, n), F32)],
        scratch_shapes=[], name=name, args=(c_t, dmod))
    return res[0], got


def _adamw(w, m, v, parts, name):
    L, R, C = w.shape
    per_layer = isinstance(parts, (list, tuple))
    plist = list(parts) if per_layer else [parts]
    P = plist[0].shape[0]
    tr = _blk_rows(R, max(16, (1 << 18) // C))
    nr = R // tr
    c1 = 1.0 - ADAM_B1 ** ADAM_STEP
    c2 = 1.0 - ADAM_B2 ** ADAM_STEP

    def total(p_ref):
        g = p_ref[0].astype(F32)
        for k in range(1, P):
            g = g + p_ref[k].astype(F32)
        return g

    def body(w_ref, m_ref, v_ref, *rest):
        p_refs, (g_ref, d_ref, nm_ref, nv_ref) = rest[:len(plist)], rest[len(plist):]
        g = total(p_refs[0])
        for k in range(1, len(plist)):
            g = jnp.where(pl.program_id(0) == k, total(p_refs[k]), g)
        mn = ADAM_B1 * m_ref[0] + (1.0 - ADAM_B1) * g
        vn = ADAM_B2 * v_ref[0] + (1.0 - ADAM_B2) * (g * g)
        m_hat = mn / c1
        v_hat = vn / c2
        g_ref[0] = g
        nm_ref[0] = mn
        nv_ref[0] = vn
        d_ref[0] = -ADAM_LR * (m_hat / (jnp.sqrt(v_hat) + ADAM_EPS) + ADAM_WD * w_ref[0])

    rs = pl.BlockSpec((1, tr, C), lambda l, i: (l, i, 0))
    if per_layer:
        def layer_spec(k):
            return pl.BlockSpec((P, tr, C), lambda l, i: (0, jnp.where(l == k, i, 0), 0))
        pspecs = [layer_spec(k) for k in range(L)]
    else:
        pspecs = [pl.BlockSpec((P, tr, C), lambda l, i: (0, l * nr + i, 0))]
    return pl.pallas_call(
        body, grid=(L, nr), in_specs=[rs, rs, rs] + pspecs, out_specs=[rs, rs, rs, rs],
        out_shape=[SDS((L, R, C), F32)] * 4, compiler_params=_cparams(("parallel", "parallel")),
        name=name)(w, m, v, *plist)


def _blk_rows(R, cap):
    if R <= cap:
        return R
    best = None
    for t in range(16, cap + 1, 16):
        if R % t == 0:
            best = t
    assert best is not None, (R, cap)
    return best


def _add_cast_rows(g, t, name):
    Q, R, C = g.shape
    half = R // 2
    tr = _blk_rows(half, max(16, (1 << 19) // C))
    nb = half // tr

    def body(lo_ref, hi_ref, t_ref, o_ref):
        c = lax.axis_index("c")

        @pl.when(c == 0)
        def _():
            o_ref[...] = (lo_ref[...].astype(F32) + t_ref[...].astype(F32)).astype(o_ref.dtype)

        @pl.when(c == 1)
        def _():
            o_ref[...] = (hi_ref[...].astype(F32) + t_ref[...].astype(F32)).astype(o_ref.dtype)

    bs = pl.BlockSpec((1, tr, C), lambda q, i: (q, i, 0))
    hi = pl.BlockSpec((1, tr, C), lambda q, i: (q, nb + i, 0))
    return pl.pallas_call(
        body, grid=(Q, nb), in_specs=[bs, hi, bs], out_specs=bs, out_shape=SDS((Q, half, C), BF16),
        compiler_params=_cparams(("parallel", "parallel")), name=name)(g, g, t)


def _coords():
    return lax.axis_index("x"), lax.axis_index("y"), lax.axis_index("c")


def _flip(v, bit):
    return 1 - v if bit else v


def _all_gather8(v, name):
    R = v.shape[0]

    def body(v_ref, o_ref, send_sems, recv_sems):
        x, y, c = _coords()
        me = 4 * x + 2 * y + c
        o_ref[me] = v_ref[...]
        copies = []
        for k in range(1, 8):
            peer = (_flip(x, k & 4), _flip(y, k & 2), _flip(c, k & 1))
            cp = pltpu.make_async_remote_copy(
                src_ref=v_ref, dst_ref=o_ref.at[me], send_sem=send_sems.at[k - 1], recv_sem=recv_sems.at[k - 1],
                device_id=peer, device_id_type=MESH)
            cp.start()
            copies.append(cp)
        for cp in copies:
            cp.wait_recv()
        for cp in copies:
            cp.wait_send()

    return pl.pallas_call(
        body, in_specs=[VMEM_SPEC], out_specs=VMEM_SPEC, out_shape=SDS((8, R, LANE), v.dtype),
        scratch_shapes=[pltpu.SemaphoreType.DMA((7,)), pltpu.SemaphoreType.DMA((7,))],
        compiler_params=pltpu.CompilerParams(vmem_limit_bytes=VMEM_LIMIT), name=name)(v)


def _sibling_swap_rows(arrs, name):
    n = len(arrs)

    def body(*refs):
        in_refs, out_refs = refs[:n], refs[n:2 * n]
        send_sems, recv_sems = refs[2 * n:]
        x, y, c = _coords()
        copies = []
        for a in range(n):
            Q, R = in_refs[a].shape[0], in_refs[a].shape[1]
            half = R // 2
            src = in_refs[a].at[pl.ds(0, Q), pl.ds(pl.multiple_of((1 - c) * half, 16), half)]
            cp = pltpu.make_async_remote_copy(
                src_ref=src, dst_ref=out_refs[a], send_sem=send_sems.at[a], recv_sem=recv_sems.at[a],
                device_id=(x, y, 1 - c), device_id_type=MESH)
            cp.start()
            copies.append(cp)
        for cp in copies:
            cp.wait_recv()
        for cp in copies:
            cp.wait_send()

    return pl.pallas_call(
        body, in_specs=[ANY] * n, out_specs=[ANY] * n,
        out_shape=[SDS((a.shape[0], a.shape[1] // 2, a.shape[2]), a.dtype) for a in arrs],
        scratch_shapes=[pltpu.SemaphoreType.DMA((n,)), pltpu.SemaphoreType.DMA((n,))],
        name=name)(*arrs)


class _OwnerReduce:
    aliased = False

    def __init__(self, srcs, lay):
        self.srcs, self.lay, self.n = list(srcs), lay, len(srcs)
        self.out_shapes = [SDS(a.shape, a.dtype) for a in self.srcs]
        self.sem_shapes = [pltpu.SemaphoreType.DMA((self.n, 3)), pltpu.SemaphoreType.DMA((self.n, 3)),
                           pltpu.SemaphoreType.DMA((self.n,))]

    def _copies(self, src_refs, dst_refs, sems):
        ici_send, ici_recv, loc_sem = sems
        x, y, c = _coords()
        p = 2 * x + y
        local, remote = [], []
        for a in range(self.n):
            local.append(pltpu.make_async_copy(src_refs[a].at[p], dst_refs[a].at[p], loc_sem.at[a]))
            for k in range(1, 4):
                qx, qy = _flip(x, k & 2), _flip(y, k & 1)
                remote.append(pltpu.make_async_remote_copy(
                    src_ref=src_refs[a].at[2 * qx + qy], dst_ref=dst_refs[a].at[p], send_sem=ici_send.at[a, k - 1],
                    recv_sem=ici_recv.at[a, k - 1], device_id=(qx, qy, self.lay), device_id_type=MESH))
        return c, local, remote

    def start(self, src_refs, dst_refs, sems):
        c, local, remote = self._copies(src_refs, dst_refs, sems)

        @pl.when(c == self.lay)
        def _():
            for cp in local + remote:
                cp.start()

    def finish(self, src_refs, dst_refs, sems):
        c, local, remote = self._copies(src_refs, dst_refs, sems)

        @pl.when(c == self.lay)
        def _():
            for cp in remote:
                cp.wait_recv()
            for cp in remote:
                cp.wait_send()
            for cp in local:
                cp.wait()


def _call_hosting(body, *, comm, grid, in_specs, out_specs, out_shape, scratch_shapes, name, args, semantics=None):
    n_in, n_out, n_scr = len(args), len(out_shape), len(scratch_shapes)
    if comm is None:
        sem = semantics if semantics is not None else ("parallel",) * len(grid)
        res = pl.pallas_call(body, grid=grid, in_specs=in_specs, out_specs=out_specs, out_shape=out_shape,
                             scratch_shapes=scratch_shapes, compiler_params=_cparams(sem), name=name)(*args)
        return list(res), None
    k = comm.n

    def hosted(*refs):
        ins, cin = refs[:n_in], refs[n_in:n_in + k]
        outs = refs[n_in + k:n_in + k + n_out]
        cout = refs[n_in + k + n_out:n_in + 2 * k + n_out]
        scr = refs[n_in + 2 * k + n_out:n_in + 2 * k + n_out + n_scr]
        sems = refs[n_in + 2 * k + n_out + n_scr:]
        first = pl.program_id(0) == 0
        last = pl.program_id(0) == grid[0] - 1
        for d in range(1, len(grid)):
            first = jnp.logical_and(first, pl.program_id(d) == 0)
            last = jnp.logical_and(last, pl.program_id(d) == grid[d] - 1)

        @pl.when(first)
        def _():
            comm.start(cin, cout, sems)

        body(*ins, *outs, *scr)

        @pl.when(last)
        def _():
            comm.finish(cin, cout, sems)

    aliases = {n_in + j: n_out + j for j in range(k)} if comm.aliased else {}
    res = pl.pallas_call(
        hosted, grid=grid, in_specs=list(in_specs) + [ANY] * k, out_specs=list(out_specs) + [ANY] * k,
        out_shape=list(out_shape) + comm.out_shapes, scratch_shapes=list(scratch_shapes) + comm.sem_shapes,
        input_output_aliases=aliases, compiler_params=_cparams(("arbitrary",) * len(grid)),
        name=name)(*args, *comm.srcs)
    return list(res[:n_out]), list(res[n_out:])


class _RowHalfGather:
    aliased = False

    def __init__(self, srcs):
        self.srcs, self.n = list(srcs), len(srcs)
        self.out_shapes = [SDS((4,) + a.shape, a.dtype) for a in self.srcs]
        n = self.n
        self.sem_shapes = [pltpu.SemaphoreType.DMA((n, 3)), pltpu.SemaphoreType.DMA((n, 3)),
                           pltpu.SemaphoreType.DMA((n, 3)), pltpu.SemaphoreType.DMA((n, 3)),
                           pltpu.SemaphoreType.DMA((n,))]

    def _copies(self, src_refs, dst_refs, sems):
        ici_send, ici_recv, d2d_send, d2d_recv, loc_sem = sems
        x, y, c = _coords()
        p = 2 * x + y
        local, first, fwd = [], [], []
        for a in range(self.n):
            R = src_refs[a].shape[0] // 2
            half = pl.ds(pl.multiple_of(c * R, 16), R)
            local.append(pltpu.make_async_copy(src_refs[a], dst_refs[a].at[p], loc_sem.at[a]))
            for k in range(1, 4):
                qx, qy = _flip(x, k & 2), _flip(y, k & 1)
                first.append(pltpu.make_async_remote_copy(
                    src_ref=src_refs[a].at[half], dst_ref=dst_refs[a].at[p, half], send_sem=ici_send.at[a, k - 1],
                    recv_sem=ici_recv.at[a, k - 1], device_id=(qx, qy, c), device_id_type=MESH))
                slot = dst_refs[a].at[2 * qx + qy, half]
                fwd.append(pltpu.make_async_remote_copy(
                    src_ref=slot, dst_ref=slot, send_sem=d2d_send.at[a, k - 1], recv_sem=d2d_recv.at[a, k - 1],
                    device_id=(x, y, 1 - c), device_id_type=MESH))
        return local, first, fwd

    def start(self, src_refs, dst_refs, sems):
        local, first, _ = self._copies(src_refs, dst_refs, sems)
        for cp in local + first:
            cp.start()

    def finish(self, src_refs, dst_refs, sems):
        local, first, fwd = self._copies(src_refs, dst_refs, sems)
        for got, on in zip(first, fwd):
            got.wait_recv()
            on.start()
        for cp in fwd:
            cp.wait_recv()
        for cp in first + fwd:
            cp.wait_send()
        for cp in local:
            cp.wait()

    def run(self, name):
        return _run_exchange(self, name)


def _run_exchange(comm, name):
    n = comm.n

    def body(*refs):
        src_refs, dst_refs, sems = refs[:n], refs[n:2 * n], refs[2 * n:]
        comm.start(src_refs, dst_refs, sems)
        comm.finish(src_refs, dst_refs, sems)

    return pl.pallas_call(body, in_specs=[ANY] * n, out_specs=[ANY] * n, out_shape=comm.out_shapes,
                          scratch_shapes=comm.sem_shapes, name=name)(*comm.srcs)


class _RowHalfReduce:
    aliased = False

    def __init__(self, srcs):
        self.srcs, self.n = list(srcs), len(srcs)
        self.out_shapes = [SDS((4, 2 * a.shape[1], a.shape[2]), a.dtype) for a in self.srcs]
        n = self.n
        self.sem_shapes = [pltpu.SemaphoreType.DMA((n, 3)), pltpu.SemaphoreType.DMA((n, 3)),
                           pltpu.SemaphoreType.DMA((n, 4)), pltpu.SemaphoreType.DMA((n, 4)),
                           pltpu.SemaphoreType.DMA((n,))]

    def _copies(self, src_refs, dst_refs, sems):
        ici_send, ici_recv, d2d_send, d2d_recv, loc_sem = sems
        x, y, c = _coords()
        p = 2 * x + y
        local, first, fwd = [], [], []
        for a in range(self.n):
            R = src_refs[a].shape[1]
            half = pl.ds(pl.multiple_of(c * R, 16), R)
            local.append(pltpu.make_async_copy(src_refs[a].at[p], dst_refs[a].at[p, half], loc_sem.at[a]))
            for k in range(4):
                qx, qy = _flip(x, k & 2), _flip(y, k & 1)
                if k:
                    first.append(pltpu.make_async_remote_copy(
                        src_ref=src_refs[a].at[2 * qx + qy], dst_ref=dst_refs[a].at[p, half],
                        send_sem=ici_send.at[a, k - 1], recv_sem=ici_recv.at[a, k - 1], device_id=(qx, qy, c),
                        device_id_type=MESH))
                slot = dst_refs[a].at[2 * qx + qy, half]
                fwd.append(pltpu.make_async_remote_copy(
                    src_ref=slot, dst_ref=slot, send_sem=d2d_send.at[a, k], recv_sem=d2d_recv.at[a, k],
                    device_id=(x, y, 1 - c), device_id_type=MESH))
        return local, first, fwd

    def start(self, src_refs, dst_refs, sems):
        local, first, _ = self._copies(src_refs, dst_refs, sems)
        for cp in local + first:
            cp.start()

    def finish(self, src_refs, dst_refs, sems):
        local, first, fwd = self._copies(src_refs, dst_refs, sems)
        for a in range(self.n):
            local[a].wait()
            fwd[4 * a].start()
            for k in range(1, 4):
                first[3 * a + k - 1].wait_recv()
                fwd[4 * a + k].start()
        for cp in fwd:
            cp.wait_recv()
        for cp in first + fwd:
            cp.wait_send()

    def run(self, name):
        return _run_exchange(self, name)


class _SiblingSend:
    aliased = False

    def __init__(self, srcs, src_core):
        self.srcs, self.src_core, self.n = list(srcs), src_core, len(srcs)
        self.out_shapes = [SDS(a.shape, a.dtype) for a in self.srcs]
        self.sem_shapes = [pltpu.SemaphoreType.DMA((self.n,)), pltpu.SemaphoreType.DMA((self.n,))]

    def _copies(self, src_refs, dst_refs, sems):
        x, y, c = _coords()
        return c, [pltpu.make_async_remote_copy(
            src_ref=src_refs[a], dst_ref=dst_refs[a], send_sem=sems[0].at[a], recv_sem=sems[1].at[a],
            device_id=(x, y, 1 - c), device_id_type=MESH) for a in range(self.n)]

    def start(self, src_refs, dst_refs, sems):
        c, copies = self._copies(src_refs, dst_refs, sems)

        @pl.when(c == self.src_core)
        def _():
            for cp in copies:
                cp.start()

    def finish(self, src_refs, dst_refs, sems):
        c, copies = self._copies(src_refs, dst_refs, sems)

        @pl.when(c == self.src_core)
        def _():
            for cp in copies:
                cp.wait_send()

        @pl.when(c != self.src_core)
        def _():
            for cp in copies:
                cp.wait_recv()


class _Handoff:
    aliased = True

    def __init__(self, srcs, lay, slots):
        self.srcs, self.lay, self.slots, self.n = list(srcs), lay, tuple(slots), len(srcs)
        self.out_shapes = [SDS(a.shape, a.dtype) for a in self.srcs]
        ns = len(self.slots)
        self.sem_shapes = [pltpu.SemaphoreType.DMA((self.n, ns)), pltpu.SemaphoreType.DMA((self.n, ns))]

    def _copies(self, dst_refs, sems):
        x, y, c = _coords()
        copies = []
        for a in range(self.n):
            for j, k in enumerate(self.slots):
                slot = dst_refs[a].at[2 * _flip(x, k & 2) + _flip(y, k & 1)]
                copies.append(pltpu.make_async_remote_copy(
                    src_ref=slot, dst_ref=slot, send_sem=sems[0].at[a, j], recv_sem=sems[1].at[a, j],
                    device_id=(x, y, 1 - c), device_id_type=MESH))
        return c, copies

    def start(self, src_refs, dst_refs, sems):
        c, copies = self._copies(dst_refs, sems)

        @pl.when(c == self.lay)
        def _():
            for cp in copies:
                cp.start()

    def finish(self, src_refs, dst_refs, sems):
        c, copies = self._copies(dst_refs, sems)

        @pl.when(c == self.lay)
        def _():
            for cp in copies:
                cp.wait_send()

        @pl.when(c != self.lay)
        def _():
            for cp in copies:
                cp.wait_recv()


def _add_cast_on(a, b, lay, name):
    Q, R, C = b.shape
    tr = _blk_rows(R, max(16, (1 << 19) // C))

    def body(a_ref, b_ref, o_ref):
        @pl.when(lax.axis_index("c") == lay)
        def _():
            o_ref[...] = (a_ref[...].astype(F32) + b_ref[...].astype(F32)).astype(o_ref.dtype)

    bs = pl.BlockSpec((1, tr, C), lambda q, i: (q, i, 0))
    return pl.pallas_call(
        body, grid=(Q, R // tr), in_specs=[bs, bs], out_specs=bs, out_shape=SDS((Q, R, C), BF16),
        compiler_params=_cparams(("parallel", "parallel")), name=name)(a, b)


_IN_SIZES = (512, 128, 128, 512, 512, 512, 8, 512, 512, 512, 3072)
_IN_OFF = tuple(int(v) for v in np.cumsum((0,) + _IN_SIZES))
_IN_Q = N_IN_COLS // 4


def _pack_w_in(w):
    def cols(lo, hi):
        out = []
        while lo < hi:
            q, off = divmod(lo, _IN_Q)
            n = min(hi - lo, _IN_Q - off)
            out.append(w[q, :, off:off + n])
            lo += n
        return out

    fb0, fb1, g0 = _IN_OFF[6], _IN_OFF[7], _IN_OFF[10]
    wqkv = jnp.concatenate(cols(0, fb0) + cols(fb1, g0), axis=1)
    wgf = jnp.concatenate(cols(g0, N_IN_COLS) + cols(fb0, fb1) + [jnp.zeros((w.shape[1], LANE - 8), w.dtype)], axis=1)
    return wqkv, wgf


def _unpack_w_in(dqkv, dgf):
    fb0, fb1, g0 = _IN_OFF[6], _IN_OFF[7], _IN_OFF[10]

    def cols(lo, hi):
        out = []
        while lo < hi:
            if lo < fb0:
                n = min(hi, fb0) - lo
                out.append(dqkv[:, lo:lo + n])
            elif lo < fb1:
                n = min(hi, fb1) - lo
                out.append(dgf[:, 3072 + lo - fb0:3072 + lo - fb0 + n])
            elif lo < g0:
                n = min(hi, g0) - lo
                out.append(dqkv[:, lo - 8:lo - 8 + n])
            else:
                n = hi - lo
                out.append(dgf[:, lo - g0:lo - g0 + n])
            lo += n
        return out

    return jnp.stack([jnp.concatenate(cols(q * _IN_Q, (q + 1) * _IN_Q), axis=1) for q in range(4)])


def _pad_rows(a, rows):
    return jnp.pad(a, ((0, rows - a.shape[0]), (0, 0)))


def _small_pack(parts):
    flat = jnp.concatenate([p.reshape(-1) for p in parts])
    n = flat.shape[0]
    rows = -(-n // LANE)
    rows = -(-rows // 8) * 8
    return jnp.pad(flat, (0, rows * LANE - n)).reshape(rows, LANE)


def _small_unpack(block, shapes):
    flat = block.reshape(-1)
    out, off = [], 0
    for s in shapes:
        n = int(np.prod(s))
        out.append(flat[off:off + n].reshape(s))
        off += n
    return out


def _kv_same(g):
    return 0


def _kv_own(g):
    return g


_mm_plain = _mm


def _mm_hosting(a, b, *, comm, **kw):
    if comm is None:
        return _mm(a, b, **kw), None
    return _mm(a, b, comm=comm, **kw)


def _layer_fwd(x, mod, p, l, ride):
    sh_m, sc_m, g_m, sh_f, sc_f, g_f = mod
    nm = "l%d_" % l

    def carried(name, run):
        res, got = run(ride.comm_for(name))
        if got is not None:
            ride.done(name, got)
        return res

    h1 = _norm_mod_fwd(x, p["norm_mix_g"], sc_m, sh_m, nm + "norm_mix_fwd")
    qkv = carried("proj_qkv", lambda cm: _mm_hosting(h1, p["wqkv"], mode="nn", out_dtype=BF16,
                                                     name=nm + "proj_qkv", comm=cm))
    gf = carried("proj_gf", lambda cm: _mm_hosting(h1, p["wgf"], mode="nn", out_dtype=F32, name=nm + "proj_gf",
                                                  cap_n=640, comm=cm))
    qkv_t = qkv.T
    o_a_t = carried("attn_a", lambda cm: _bandT_fwd(
        (qkv_t, 0), _heads(qkv[:, 512:640], A_KV_HEADS), (qkv_t, 640), p["alibi"], p["sink_tab"],
        GQ=4, GK=1, P=A_PREV, kvoff=_kv_same, name=nm + "attn_a_fwd", comm=cm))
    cum = _fox_cum(gf, p["b_forget_pad"], nm + "fox_cum")
    cum_t = cum[:, :N_HEADS].T
    cc, cr = cum_t[:, :, None], cum_t[:, None, :]
    o_b_t, lse_b = carried("attn_b", lambda cm: _foxT_fwd(
        (qkv_t, 768), _heads(qkv[:, 1280:1792], N_HEADS), (qkv_t, 1792), cc, cr, nm + "attn_b_fwd", comm=cm))
    o_c_t = carried("attn_c", lambda cm: _bandT_fwd(
        (qkv_t, 2304), _heads(qkv[:, 2816:3328], N_HEADS), (qkv_t, 3328), p["rel_tab"], p["no_sink"],
        GQ=2, GK=2, P=C_PREV, kvoff=_kv_own, name=nm + "attn_c_fwd", comm=cm))
    p = dict(p, **ride.late_weights())
    o = jnp.concatenate([o_a_t, o_b_t, o_c_t], axis=0).T
    y = _mm(o, p["wb"], mode="nn", out_dtype=BF16, groups=3, name=nm + "branch")
    merged = _merge_fwd(y, gf, nm + "merge_fwd")
    mix = _mm(merged, p["wout"], mode="nn", out_dtype=BF16, name=nm + "out_proj")
    x1 = _resid_fwd(x, mix, g_m, nm + "resid_mix")
    h2 = _norm_mod_fwd(x1, p["norm_ffn_g"], sc_f, sh_f, nm + "norm_ffn_fwd")
    u = carried("ffn_in", lambda cm: _mm_hosting(h2, p["wfi"], mode="nn", out_dtype=BF16, name=nm + "ffn_in",
                                                 cap_n=512, comm=cm))
    a = _swiglu_fwd(u, nm + "swiglu_fwd")
    f = _mm(a, p["wfo"], mode="nn", out_dtype=BF16, name=nm + "ffn_out", cap_m=1024)
    x2 = _resid_fwd(x1, f, g_f, nm + "resid_ffn")
    saved = dict(x=x, h1=h1, qkv=qkv, qkv_t=qkv_t, gf=gf, cc=cc, cr=cr, o_b_t=o_b_t, lse_b=lse_b, o=o, y=y, merged=merged,
                 mix=mix, x1=x1, h2=h2, u=u, a=a, f=f)
    return x2, saved, p


def _layer_bwd(dx2, mod, p, s, l, ride=None):
    sh_m, sc_m, g_m, sh_f, sc_f, g_f = mod
    nm = "l%d_" % l

    def _mm(a, b, *, name, **kw):
        comm = ride.comm_for(name) if ride is not None else None
        if comm is None:
            return _mm_plain(a, b, name=nm + name, **kw)
        out, got = _mm_plain(a, b, name=nm + name, comm=comm, **kw)
        ride.done(name, got)
        return out

    dg_f, df = _resid_bwd(dx2, s["f"], g_f, nm + "resid_ffn_bwd")
    da = _mm(df, p["wfo"], mode="nt", out_dtype=BF16, name="ffn_out_dx", cap_m=1024, cap_n=1408)
    d_wfo = _mm(s["a"], df, mode="tn", out_dtype=BF16, name="ffn_out_dw", cap_m=1408, cap_k=2048)
    du = _swiglu_bwd(da, s["u"], nm + "swiglu_bwd")
    dh2 = _mm(du, p["wfi"], mode="nt", out_dtype=BF16, name="ffn_in_dx", cap_m=1024)
    d_wfi = _mm(s["h2"], du, mode="tn", out_dtype=BF16, name="ffn_in_dw", cap_m=1024, cap_n=1408, cap_k=2048,
                col_quarters=True)
    dx1, dsc_f, dsh_f, dgn_f = _norm_mod_bwd(s["x1"], [dh2], dx2, p["norm_ffn_g"], sc_f, nm + "norm_ffn_bwd")
    dg_m, dmix = _resid_bwd(dx1, s["mix"], g_m, nm + "resid_mix_bwd")
    dmerged = _mm(dmix, p["wout"], mode="nt", out_dtype=BF16, name="out_proj_dx")
    d_wout = _mm(s["merged"], dmix, mode="tn", out_dtype=BF16, name="out_proj_dw", cap_m=1024, cap_k=2048)
    dy, dgates = _merge_bwd(dmerged, s["y"], s["gf"], nm + "merge_bwd")
    do = _mm(dy, p["wb"], mode="nt", out_dtype=BF16, groups=3, name="branch_dx")
    d_wb = _mm(s["o"], dy, mode="tn", out_dtype=BF16, groups=3, name="branch_dw", cap_k=2048,
               col_quarters=True)
    comms = ride.exchanges() if ride is not None else (None, None, None)
    qkv, qkv_t = s["qkv"], s["qkv_t"]
    do_t = do.T
    (dqa_t, dka_h, dva_h, _, dsink), got_a = _bandT_bwd(
        (qkv_t, 0), _heads(qkv[:, 0:512], N_HEADS), _heads(qkv[:, 512:640], A_KV_HEADS), (qkv_t, 512),
        _heads(qkv[:, 640:768], A_KV_HEADS), (do_t, 0), _heads(do[:, 0:512], N_HEADS), p["alibi"], p["sink_tab"],
        GQ=4, GK=1, P=A_PREV, kvoff=_kv_same, name=nm + "attn_a_bwd", comm=comms[0])
    (dqb_t, dkb_h, dvb_h, dck, dcq), got_b = _foxT_bwd(
        (qkv_t, 768), _heads(qkv[:, 768:1280], N_HEADS), _heads(qkv[:, 1280:1792], N_HEADS), (qkv_t, 1280),
        _heads(qkv[:, 1792:2304], N_HEADS), s["cc"], s["cr"], s["o_b_t"], (do_t, 512),
        _heads(do[:, 512:1024], N_HEADS), s["lse_b"], nm + "attn_b_bwd", comm=comms[1])
    dcum = jnp.pad((dck[:, :, 0] + dcq[:, 0, :]).T, ((0, 0), (0, LANE - N_HEADS)))
    dfb, db_forget = _fox_cum_bwd(s["gf"], p["b_forget_pad"], dcum, nm + "fox_cum_bwd")
    (dqc_t, dkc_h, dvc_h, dbias_c, _), got_c = _bandT_bwd(
        (qkv_t, 2304), _heads(qkv[:, 2304:2816], N_HEADS), _heads(qkv[:, 2816:3328], N_HEADS), (qkv_t, 2816),
        _heads(qkv[:, 3328:3840], N_HEADS), (do_t, 1024), _heads(do[:, 1024:1536], N_HEADS), p["rel_tab"],
        p["no_sink"], GQ=2, GK=2, P=C_PREV, kvoff=_kv_own, name=nm + "attn_c_bwd", comm=comms[2])
    d_rel = _rel_reduce(jnp.transpose(_unpair_table(dbias_c), (1, 0, 2)), nm + "rel_reduce")[:, :N_REL]
    dqkv = jnp.concatenate([dqa_t.T, _unheads(dka_h), _unheads(dva_h), dqb_t.T, _unheads(dkb_h), _unheads(dvb_h),
                            dqc_t.T, _unheads(dkc_h), _unheads(dvc_h)], axis=1)
    dgf = jnp.concatenate([dgates, dfb], axis=1)
    if ride is not None:
        ride.exchanged((got_a, got_b, got_c))
    dh1a = _mm(dqkv, p["wqkv"], mode="nt", out_dtype=BF16, name="proj_qkv_dx", cap_k=1024)
    dh1b = _mm(dgf, p["wgf"], mode="nt", out_dtype=BF16, name="proj_gf_dx", cap_k=640)
    d_wqkv = _mm(s["h1"], dqkv, mode="tn", out_dtype=BF16, name="proj_qkv_dw", cap_m=1024, cap_k=2048)
    d_wgf = _mm(s["h1"], dgf, mode="tn", out_dtype=BF16, name="proj_gf_dw", cap_m=1024, cap_n=640, cap_k=2048)
    dx, dsc_m, dsh_m, dgn_m = _norm_mod_bwd(s["x"], [dh1a, dh1b], dx1, p["norm_mix_g"], sc_m, nm + "norm_mix_bwd")
    d_mod = jnp.concatenate([dsh_m, dsc_m, dg_m, dsh_f, dsc_f, dg_f], axis=1)[0]
    grads = dict(w_in=_unpack_w_in(d_wqkv, d_wgf), w_branch=d_wb, w_out=d_wout.reshape(4, -1, D_MODEL),
                 w_ffn_in=d_wfi, w_ffn_out=d_wfo.reshape(4, -1, D_MODEL),
                 norm_mix_g=dgn_m[0], norm_ffn_g=dgn_f[0], b_forget=db_forget[0, :N_HEADS],
                 sinks=dsink[:, 0, 0], rel_bias=d_rel, d_mod=d_mod)
    return dx, grads


def kernel(x, c, norm_mix_g, norm_ffn_g, w_ada, b_ada, w_in, b_forget, sinks, rel_bias, w_branch, w_out, w_ffn_in, w_ffn_out, final_norm_g, loss_target, m_norm_mix_g, m_norm_ffn_g, m_w_ada, m_b_ada, m_w_in, m_b_forget, m_sinks, m_rel_bias, m_w_branch, m_w_out, m_w_ffn_in, m_w_ffn_out, m_final_norm_g, v_norm_mix_g, v_norm_ffn_g, v_w_ada, v_b_ada, v_w_in, v_b_forget, v_sinks, v_rel_bias, v_w_branch, v_w_out, v_w_ffn_in, v_w_ffn_out, v_final_norm_g):
    xi, yi, ci = _coords()
    chip = 2 * xi + yi
    dev = 2 * chip + ci
    xs = x[0]
    S = xs.shape[0]
    n_ada = w_ada.shape[2]

    big_names = ("w_in", "w_branch", "w_out", "w_ffn_in", "w_ffn_out")
    big_w = dict(w_in=w_in, w_branch=w_branch, w_out=w_out, w_ffn_in=w_ffn_in, w_ffn_out=w_ffn_out)
    big_m = dict(w_in=m_w_in, w_branch=m_w_branch, w_out=m_w_out, w_ffn_in=m_w_ffn_in, w_ffn_out=m_w_ffn_out)
    big_v = dict(w_in=v_w_in, w_branch=v_w_branch, w_out=v_w_out, w_ffn_in=v_w_ffn_in, w_ffn_out=v_w_ffn_out)
    flat2 = lambda a: a.reshape(-1, a.shape[-1])
    shards = [[flat2(big_w[n][l]).astype(BF16) for n in big_names] for l in range(DEPTH)]
    gw = [[None] * (len(big_names) + 2) for _ in range(DEPTH)]
    for l in range(DEPTH):
        shards[l] += [shards[l][0][:D_MODEL // 2], shards[l][0][D_MODEL // 2:]]
    gw[0][0] = _RowHalfGather([shards[0][0]]).run("weights_gather_w_in_l0")[0]
    host_g = ((1, 2, 4), (0,), (3,))

    class WeightRide:
        def __init__(self, l, plan):
            self.l, self.plan = l, plan

        def comm_for(self, name):
            if name not in self.plan:
                return None
            lay, idx = self.plan[name]
            return _RowHalfGather([shards[lay][i] for i in idx])

        def done(self, name, got):
            lay, idx = self.plan[name]
            for i, r in zip(idx, got):
                gw[lay][i] = r

        def late_weights(self):
            g = gw[self.l]
            return dict(wb=jnp.transpose(g[1], (1, 0, 2)).reshape(3 * BRANCH_W, D_MODEL),
                        wout=g[2].reshape(D_MODEL, D_MODEL),
                        wfi=jnp.transpose(g[3], (1, 0, 2)).reshape(D_MODEL, 2 * FFN_H),
                        wfo=g[4].reshape(FFN_H, D_MODEL))

    weight_plan = [
        {"proj_qkv": (0, (1,)), "proj_gf": (0, (2,)), "attn_a": (0, (4,)), "attn_b": (0, (3,)), "attn_c": (1, (5,)),
         "ffn_in": (1, (6,))},
        {"attn_a": (1, (1, 2)), "attn_b": (1, (3,)), "attn_c": (1, (4,))}]


    c_all = _all_gather8(c.reshape(8, LANE), "gather_c").reshape(8, D_MODEL)
    b_sh = lax.dynamic_slice_in_dim(b_ada, chip * n_ada, n_ada, axis=1)[:, None, :]
    mod_sh = _ada_fwd(_pad_rows(c_all, 16), w_ada, b_sh, "ada_fwd")[:, :8, :]
    mod_all = _all_gather8(mod_sh.reshape(-1, LANE), "gather_mod").reshape(8, DEPTH, 8, n_ada)
    mod_mine = lax.dynamic_index_in_dim(mod_all[0::2], dev, axis=2, keepdims=False)
    mod = mod_mine.transpose(1, 0, 2).reshape(DEPTH, 6, D_MODEL)

    alibi = _pair_table(_alibi_table())
    no_sink = jnp.full((N_HEADS, 8, LANE), NEG_INF, F32)
    def make_params(l):
        if gw[l][0] is None:
            gw[l][0] = jnp.concatenate([gw[l][5], gw[l][6]], axis=1)
        wqkv, wgf = _pack_w_in(gw[l][0])
        rel_tab = _rel_expand(jnp.pad(rel_bias[l], ((0, 0), (0, N_REL_PAD - N_REL))), "l%d_rel_expand" % l)
        return dict(
            wqkv=wqkv, wgf=wgf, norm_mix_g=norm_mix_g[l][None], norm_ffn_g=norm_ffn_g[l][None],
            b_forget_pad=jnp.pad(b_forget[l], (0, LANE - N_HEADS))[None],
            sink_tab=jnp.broadcast_to(sinks[l][:, None, None], (N_HEADS, 8, LANE)),
            no_sink=no_sink, alibi=alibi, rel_tab=_pair_table(jnp.transpose(rel_tab, (1, 0, 2))))

    mods = [[mod[l, k][None] for k in range(6)] for l in range(DEPTH)]
    params, saved = [None] * DEPTH, [None] * DEPTH
    h = xs
    for l in range(DEPTH):
        h, saved[l], params[l] = _layer_fwd(h, mods[l], make_params(l), l, WeightRide(l, weight_plan[l]))
    loss_dev, dh, d_final = _final_loss(h, final_norm_g[None], loss_target[0], "final_loss")
    grads = [None] * DEPTH
    dh, grads[1] = _layer_bwd(dh, mods[1], params[1], saved[1], 1)

    class Layer1Ride:
        sends = {"ffn_out_dx": (4,), "ffn_in_dx": (3, 1, 2), "ffn_in_dw": (0,)}
        hands = {"proj_qkv_dx": (0,), "proj_gf_dx": (3,), "proj_gf_dw": (4, 1, 2)}

        def __init__(self, g):
            self.g, self.t = g, [None] * len(g)
            self.parts, self.final = [None] * len(g), [None] * len(g)

        def comm_for(self, name):
            if name in self.sends:
                return _SiblingSend([self.g[i] for i in self.sends[name]], 0)
            if name in self.hands:
                return _Handoff([self.parts[i] for i in self.hands[name]], 1, (0, 1, 2, 3))
            return None

        def done(self, name, got):
            idx, dst = (self.sends[name], self.t) if name in self.sends else (self.hands[name], self.final)
            for i, r in zip(idx, got):
                dst[i] = r

        def exchanges(self):
            sums = [_add_cast_on(a, b, 1, "grads_chip_sum_l1_" + n) for n, a, b in zip(big_names, self.g, self.t)]
            return tuple(_OwnerReduce([sums[i] for i in idx], 1) for idx in host_g)

        def exchanged(self, got):
            for res, idx in zip(got, host_g):
                for r, i in zip(res, idx):
                    self.parts[i] = r

    ride = Layer1Ride([grads[1][n] for n in big_names])
    dh, grads[0] = _layer_bwd(dh, mods[0], params[0], saved[0], 0, ride)
    grad_x = dh[None]
    loss = lax.psum(loss_dev[0, 0], ("x", "y", "c"))
    parts1 = ride.final
    g0 = [grads[0][n] for n in big_names]
    t0 = _sibling_swap_rows(g0, "grads_swap_l0")
    sums0 = [_add_cast_rows(a, b, "grads_chip_sum_l0_" + n) for n, a, b in zip(big_names, g0, t0)]
    parts0 = [None] + list(_RowHalfReduce(sums0[1:]).run("grads_reduce_l0"))

    small_names = ("norm_mix_g", "norm_ffn_g", "b_ada", "b_forget", "sinks", "rel_bias", "final_norm_g")
    small_w = dict(norm_mix_g=norm_mix_g, norm_ffn_g=norm_ffn_g, b_ada=b_ada, b_forget=b_forget, sinks=sinks,
                   rel_bias=rel_bias, final_norm_g=final_norm_g)
    small_m = dict(norm_mix_g=m_norm_mix_g, norm_ffn_g=m_norm_ffn_g, b_ada=m_b_ada, b_forget=m_b_forget,
                   sinks=m_sinks, rel_bias=m_rel_bias, final_norm_g=m_final_norm_g)
    small_v = dict(norm_mix_g=v_norm_mix_g, norm_ffn_g=v_norm_ffn_g, b_ada=v_b_ada, b_forget=v_b_forget,
                   sinks=v_sinks, rel_bias=v_rel_bias, final_norm_g=v_final_norm_g)
    small_g = dict(
        norm_mix_g=jnp.stack([grads[l]["norm_mix_g"] for l in range(DEPTH)]),
        norm_ffn_g=jnp.stack([grads[l]["norm_ffn_g"] for l in range(DEPTH)]),
        b_ada=jnp.stack([grads[l]["d_mod"] for l in range(DEPTH)]),
        b_forget=jnp.stack([grads[l]["b_forget"] for l in range(DEPTH)]),
        sinks=jnp.stack([grads[l]["sinks"] for l in range(DEPTH)]),
        rel_bias=jnp.stack([grads[l]["rel_bias"] for l in range(DEPTH)]),
        final_norm_g=d_final[0])
    shapes = [small_w[n].shape for n in small_names]
    g_all = _all_gather8(_small_pack([small_g[n] for n in small_names]), "gather_small_grads")
    res = _adamw(_small_pack([small_w[n] for n in small_names])[None],
                    _small_pack([small_m[n] for n in small_names])[None],
                    _small_pack([small_v[n] for n in small_names])[None], g_all, "adamw_small")
    small_out = {n: [] for n in small_names}
    for r in res:
        for n, a in zip(small_names, _small_unpack(r[0], shapes)):
            small_out[n].append(a)
    off_b = sum(int(np.prod(s)) for s in shapes[:2])
    n_mod = DEPTH * 6 * D_MODEL
    dmod_all = g_all.reshape(8, -1)[:, off_b:off_b + n_mod].reshape(8, DEPTH, 6 * D_MODEL)
    dmod_sh = lax.dynamic_slice_in_dim(dmod_all, chip * n_ada, n_ada, axis=2).transpose(1, 0, 2)
    g_ada, got = _ada_bwd(c_all.T, dmod_sh, "ada_bwd", comm=_RowHalfReduce(sums0[:1]))
    parts0[0] = got[0]
    ada_out = _adamw(w_ada, m_w_ada, v_w_ada, flat2(g_ada)[None], "adamw_w_ada")

    big_out = {}
    as3 = lambda a: a.reshape(a.shape[0], -1, a.shape[-1])
    for n, p0, p1 in zip(big_names, parts0, parts1):
        res = _adamw(as3(big_w[n]), as3(big_m[n]), as3(big_v[n]), [p0, p1], "adamw_" + n)
        big_out[n] = [r.reshape(big_w[n].shape) for r in res]

    order = ("norm_mix_g", "norm_ffn_g", "w_ada", "b_ada", "w_in", "b_forget", "sinks", "rel_bias", "w_branch",
             "w_out", "w_ffn_in", "w_ffn_out", "final_norm_g")

    def pick(n, k):
        if n == "w_ada":
            return ada_out[k]
        if n in big_out:
            return big_out[n][k]
        return small_out[n][k]

    outs = [loss, grad_x]
    for k in range(4):
        outs += [pick(n, k) for n in order]
    return tuple(outs)
```

```python
import numpy as np
import jax
import jax.numpy as jnp
from jax import lax
from jax.experimental import pallas as pl
from jax.experimental.pallas import tpu as pltpu

F32 = jnp.float32
BF16 = jnp.bfloat16
SDS = jax.ShapeDtypeStruct

D_MODEL = 1024
DEPTH = 2
CHUNK = 64
HEAD_DIM = 64
EPS = 1e-6
NEG_INF = -1e30
N_HEADS = 8
A_KV_HEADS = 2
A_PREV = 2
C_PREV = 8
REL_CLIP = 128
N_REL = 2 * REL_CLIP + 1
N_REL_PAD = 384
BRANCH_W = 512
FFN_H = 2816
FOX_BQ = 512
FOX_BK = 512
GF_COLS = 3200
N_IN_COLS = 6920
LANE = 128
VMEM_LIMIT = 48 * 1024 * 1024

ADAM_LR = 0.001
ADAM_B1 = 0.9
ADAM_B2 = 0.999
ADAM_EPS = 1e-08
ADAM_WD = 0.01
ADAM_STEP = 10

MESH = pl.DeviceIdType.MESH
ANY = pl.BlockSpec(memory_space=pl.ANY)
VMEM_SPEC = pl.BlockSpec(memory_space=pltpu.VMEM)


def _cparams(sem=None):
    return pltpu.CompilerParams(dimension_semantics=sem, vmem_limit_bytes=VMEM_LIMIT)


def _blk(n, cap):
    if n <= cap:
        return n
    best = None
    for m in range(LANE, cap + 1, LANE):
        if n % m == 0:
            best = m
    assert best is not None, (n, cap)
    return best


def _sigmoid(x):
    return 1.0 / (1.0 + jnp.exp(-x))


def _mm(a, b, *, mode, out_dtype, name, groups=1, cap_m=2048, cap_n=1024, cap_k=1408, col_quarters=False,
        comm=None):
    G = groups
    assert not col_quarters or mode == "tn"
    if mode == "nn":
        M, K, N = a.shape[0], a.shape[1] // G, b.shape[1]
        assert b.shape[0] == G * K
    elif mode == "nt":
        M, K, N = a.shape[0], a.shape[1] // G, b.shape[0] // G
        assert b.shape[1] == K
    else:
        K, M, N = a.shape[0], a.shape[1] // G, b.shape[1] // G
        assert b.shape[0] == K
    bm, bn, bk = _blk(M, cap_m), _blk(N // 4 if col_quarters else N, cap_n), _blk(K, cap_k)
    nm, nn, nk = M // bm, N // bn, K // bk
    if mode == "nn":
        a_spec = pl.BlockSpec((bm, bk), lambda g, i, j, k: (i, g * nk + k))
        b_spec = pl.BlockSpec((bk, bn), lambda g, i, j, k: (g * nk + k, j))
        o_spec = pl.BlockSpec((bm, bn), lambda g, i, j, k: (i, g * nn + j))
        dims = (((1,), (0,)), ((), ()))
        out_shape = (M, G * N)
    elif mode == "nt":
        a_spec = pl.BlockSpec((bm, bk), lambda g, i, j, k: (i, g * nk + k))
        b_spec = pl.BlockSpec((bn, bk), lambda g, i, j, k: (g * nn + j, k))
        o_spec = pl.BlockSpec((bm, bn), lambda g, i, j, k: (i, g * nn + j))
        dims = (((1,), (1,)), ((), ()))
        out_shape = (M, G * N)
    else:
        a_spec = pl.BlockSpec((bk, bm), lambda g, i, j, k: (k, g * nm + i))
        b_spec = pl.BlockSpec((bk, bn), lambda g, i, j, k: (k, g * nn + j))
        dims = (((0,), (0,)), ((), ()))
        if col_quarters:
            nq = nn // 4
            o_spec = pl.BlockSpec((1, bm, bn), lambda g, i, j, k: (j // nq, g * nm + i, j % nq))
            out_shape = (4, G * M, N // 4)
        else:
            o_spec = pl.BlockSpec((bm, bn), lambda g, i, j, k: (g * nm + i, j))
            out_shape = (G * M, N)

    def product(a_ref, b_ref):
        return lax.dot_general(a_ref[...].astype(BF16), b_ref[...].astype(BF16), dims, preferred_element_type=F32)

    def body_one(a_ref, b_ref, o_ref):
        o_ref[...] = product(a_ref, b_ref).astype(o_ref.dtype).reshape(o_ref.shape)

    def body_acc(a_ref, b_ref, o_ref, acc_ref):
        k = pl.program_id(3)

        @pl.when(k == 0)
        def _():
            acc_ref[...] = jnp.zeros_like(acc_ref)

        acc_ref[...] += product(a_ref, b_ref)

        @pl.when(k == nk - 1)
        def _():
            o_ref[...] = acc_ref[...].astype(o_ref.dtype).reshape(o_ref.shape)

    res, got = _call_hosting(
        body_one if nk == 1 else body_acc, comm=comm, grid=(G, nm, nn, nk), in_specs=[a_spec, b_spec],
        out_specs=[o_spec], out_shape=[SDS(out_shape, out_dtype)],
        scratch_shapes=[] if nk == 1 else [pltpu.VMEM((bm, bn), F32)], name=name, args=(a, b),
        semantics=("parallel", "parallel", "parallel", "arbitrary"))
    return res[0] if comm is None else (res[0], got)


def _rows(tm, n, col=0):
    return pl.BlockSpec((tm, n), lambda i: (i, col))


def _vec(n):
    return pl.BlockSpec((1, n), lambda i: (0, 0))


def _tm(S, cap=512):
    return min(S, cap)


def _norm_mod_fwd(x, g, sc, sh, name):
    S, Dm = x.shape
    tm = _tm(S, 1024)

    def body(x_ref, g_ref, sc_ref, sh_ref, h_ref):
        xv = x_ref[...]
        r = lax.rsqrt(jnp.mean(xv * xv, axis=-1, keepdims=True) + EPS)
        h_ref[...] = ((xv * r) * g_ref[...] * (1.0 + sc_ref[...]) + sh_ref[...]).astype(h_ref.dtype)

    return pl.pallas_call(
        body, grid=(S // tm,), in_specs=[_rows(tm, Dm), _vec(Dm), _vec(Dm), _vec(Dm)],
        out_specs=_rows(tm, Dm), out_shape=SDS((S, Dm), BF16),
        compiler_params=_cparams(("parallel",)), name=name)(x, g, sc, sh)


def _norm_mod_bwd(x, dh_list, dres, g, sc, name):
    S, Dm = x.shape
    tm = _tm(S, 1024)
    nh = len(dh_list)

    def body(*refs):
        x_ref = refs[0]
        dh_refs = refs[1:1 + nh]
        dres_ref, g_ref, sc_ref, dx_ref, dsc_ref, dsh_ref, dg_ref = refs[1 + nh:]
        i = pl.program_id(0)

        @pl.when(i == 0)
        def _():
            dsc_ref[...] = jnp.zeros_like(dsc_ref)
            dsh_ref[...] = jnp.zeros_like(dsh_ref)
            dg_ref[...] = jnp.zeros_like(dg_ref)

        xv = x_ref[...]
        dh = dh_refs[0][...].astype(F32)
        for r_ in dh_refs[1:]:
            dh = dh + r_[...].astype(F32)
        gv = g_ref[...]
        r = lax.rsqrt(jnp.mean(xv * xv, axis=-1, keepdims=True) + EPS)
        xn = xv * r
        xg = xn * gv
        dsh_ref[...] += jnp.sum(dh, axis=0, keepdims=True)
        dsc_ref[...] += jnp.sum(dh * xg, axis=0, keepdims=True)
        dxg = dh * (1.0 + sc_ref[...])
        dg_ref[...] += jnp.sum(dxg * xn, axis=0, keepdims=True)
        dxn = dxg * gv
        dx_ref[...] = dres_ref[...] + r * (dxn - xn * jnp.mean(dxn * xn, axis=-1, keepdims=True))

    return pl.pallas_call(
        body, grid=(S // tm,),
        in_specs=[_rows(tm, Dm)] * (2 + nh) + [_vec(Dm), _vec(Dm)],
        out_specs=[_rows(tm, Dm), _vec(Dm), _vec(Dm), _vec(Dm)],
        out_shape=[SDS((S, Dm), F32), SDS((1, Dm), F32), SDS((1, Dm), F32), SDS((1, Dm), F32)],
        compiler_params=_cparams(("arbitrary",)), name=name)(x, *dh_list, dres, g, sc)


def _resid_fwd(x, val, g, name):
    S, Dm = x.shape
    tm = _tm(S, 1024)

    def body(x_ref, v_ref, g_ref, o_ref):
        o_ref[...] = x_ref[...] + g_ref[...] * v_ref[...].astype(F32)

    return pl.pallas_call(
        body, grid=(S // tm,), in_specs=[_rows(tm, Dm), _rows(tm, Dm), _vec(Dm)],
        out_specs=_rows(tm, Dm), out_shape=SDS((S, Dm), F32),
        compiler_params=_cparams(("parallel",)), name=name)(x, val, g)


def _resid_bwd(dx, val, g, name):
    S, Dm = dx.shape
    tm = _tm(S, 1024)

    def body(dx_ref, v_ref, g_ref, dg_ref, dv_ref):
        @pl.when(pl.program_id(0) == 0)
        def _():
            dg_ref[...] = jnp.zeros_like(dg_ref)

        dxv = dx_ref[...]
        dg_ref[...] += jnp.sum(dxv * v_ref[...].astype(F32), axis=0, keepdims=True)
        dv_ref[...] = (dxv * g_ref[...]).astype(dv_ref.dtype)

    return pl.pallas_call(
        body, grid=(S // tm,), in_specs=[_rows(tm, Dm), _rows(tm, Dm), _vec(Dm)],
        out_specs=[_vec(Dm), _rows(tm, Dm)], out_shape=[SDS((1, Dm), F32), SDS((S, Dm), BF16)],
        compiler_params=_cparams(("arbitrary",)), name=name)(dx, val, g)


def _merge_fwd(y, gf, name):
    S = y.shape[0]
    tm = _tm(S)
    W = 3 * D_MODEL

    def body(y_ref, g_ref, o_ref):
        acc = None
        for k in range(3):
            sl = slice(k * D_MODEL, (k + 1) * D_MODEL)
            t = _sigmoid(g_ref[:, sl]) * y_ref[:, sl].astype(F32)
            acc = t if acc is None else acc + t
        o_ref[...] = acc.astype(o_ref.dtype)

    return pl.pallas_call(
        body, grid=(S // tm,), in_specs=[_rows(tm, W), _rows(tm, W)],
        out_specs=_rows(tm, D_MODEL), out_shape=SDS((S, D_MODEL), BF16),
        compiler_params=_cparams(("parallel",)), name=name)(y, gf)


def _merge_bwd(dm, y, gf, name):
    S = y.shape[0]
    tm = _tm(S)
    W = 3 * D_MODEL

    def body(dm_ref, y_ref, g_ref, dy_ref, dg_ref):
        dmv = dm_ref[...].astype(F32)
        for k in range(3):
            sl = slice(k * D_MODEL, (k + 1) * D_MODEL)
            sg = _sigmoid(g_ref[:, sl])
            dy_ref[:, sl] = (dmv * sg).astype(dy_ref.dtype)
            dg_ref[:, sl] = (dmv * y_ref[:, sl].astype(F32) * (sg * (1.0 - sg))).astype(dg_ref.dtype)

    return pl.pallas_call(
        body, grid=(S // tm,), in_specs=[_rows(tm, D_MODEL), _rows(tm, W), _rows(tm, W)],
        out_specs=[_rows(tm, W), _rows(tm, W)], out_shape=[SDS((S, W), BF16), SDS((S, W), BF16)],
        compiler_params=_cparams(("parallel",)), name=name)(dm, y, gf)


def _swiglu_fwd(u, name):
    S = u.shape[0]
    tm = _tm(S)

    def body(g_ref, u_ref, a_ref):
        gv = g_ref[...].astype(F32)
        a_ref[...] = (gv * _sigmoid(gv) * u_ref[...].astype(F32)).astype(a_ref.dtype)

    return pl.pallas_call(
        body, grid=(S // tm,), in_specs=[_rows(tm, FFN_H, 0), _rows(tm, FFN_H, 1)],
        out_specs=_rows(tm, FFN_H), out_shape=SDS((S, FFN_H), BF16),
        compiler_params=_cparams(("parallel",)), name=name)(u, u)


def _swiglu_bwd(da, u, name):
    S = u.shape[0]
    tm = _tm(S)

    def body(da_ref, g_ref, u_ref, du_ref):
        dav = da_ref[...].astype(F32)
        gv = g_ref[...].astype(F32)
        sg = _sigmoid(gv)
        du_ref[:, 0:FFN_H] = (dav * u_ref[...].astype(F32) * (sg * (1.0 + gv * (1.0 - sg)))).astype(du_ref.dtype)
        du_ref[:, FFN_H:2 * FFN_H] = (dav * (gv * sg)).astype(du_ref.dtype)

    return pl.pallas_call(
        body, grid=(S // tm,), in_specs=[_rows(tm, FFN_H), _rows(tm, FFN_H, 0), _rows(tm, FFN_H, 1)],
        out_specs=_rows(tm, 2 * FFN_H), out_shape=SDS((S, 2 * FFN_H), BF16),
        compiler_params=_cparams(("parallel",)), name=name)(da, u, u)


def _final_loss(x, g, target, name):
    S, Dm = x.shape
    tm = _tm(S, 1024)

    def body(x_ref, g_ref, t_ref, loss_ref, dx_ref, dg_ref):
        @pl.when(pl.program_id(0) == 0)
        def _():
            loss_ref[...] = jnp.zeros_like(loss_ref)
            dg_ref[...] = jnp.zeros_like(dg_ref)

        xv = x_ref[...]
        gv = g_ref[...]
        r = lax.rsqrt(jnp.mean(xv * xv, axis=-1, keepdims=True) + EPS)
        xn = xv * r
        err = xn * gv - t_ref[...]
        row = jnp.mean(err * err, axis=-1, keepdims=True)
        loss_ref[...] += 0.5 * jnp.sum(row, axis=0, keepdims=True)
        dy = err * (1.0 / Dm)
        dg_ref[...] += jnp.sum(dy * xn, axis=0, keepdims=True)
        dxn = dy * gv
        dx_ref[...] = r * (dxn - xn * jnp.mean(dxn * xn, axis=-1, keepdims=True))

    return pl.pallas_call(
        body, grid=(S // tm,), in_specs=[_rows(tm, Dm), _vec(Dm), _rows(tm, Dm)],
        out_specs=[pl.BlockSpec((1, 1), lambda i: (0, 0)), _rows(tm, Dm), _vec(Dm)],
        out_shape=[SDS((1, 1), F32), SDS((S, Dm), F32), SDS((1, Dm), F32)],
        compiler_params=_cparams(("arbitrary",)), name=name)(x, g, target)


PAIR = 2 * CHUNK


def _bandT_softmax(kg, qTg, bias, sink, valid):
    s = jnp.dot(kg, qTg, preferred_element_type=F32)
    s = jnp.where(valid, s + bias, NEG_INF)
    m = jnp.maximum(jnp.max(s, axis=0, keepdims=True), sink)
    e = jnp.exp(s - m)
    es = jnp.exp(sink - m)
    inv = 1.0 / (jnp.sum(e, axis=0, keepdims=True) + es)
    return e * inv, es * inv


def _pad_copy_rows(dst, src, pad, S):
    dst[:, 0:pad, :] = jnp.zeros((dst.shape[0], pad, dst.shape[2]), dst.dtype)
    dst[:, pad:pad + S, :] = src[...]


def _pad_copy_lanes(dst, src, pad, S):
    dst[:, 0:pad] = jnp.zeros((dst.shape[0], pad), dst.dtype)
    dst[:, pad:pad + S] = src[...]


def _fm(arg):
    return arg if isinstance(arg, tuple) else (arg, 0)


def _fm_spec(rows, S, row0):
    off, rem = divmod(row0, rows)
    assert rem == 0
    return pl.BlockSpec((rows, S), lambda i: (off + i, 0))


def _bandT_fwd(qT, k_h, vT, bias, sink, *, GQ, GK, P, kvoff, name, comm=None):
    (qT, q0), (vT, v0) = _fm(qT), _fm(vT)
    S = qT.shape[1]
    ng = bias.shape[0] // GQ
    BU = (P + 2) * CHUNK
    pad = P * CHUNK
    npair = S // PAIR

    def body(qT_ref, k_ref, vT_ref, b_ref, s_ref, oT_ref, kp, vTp):
        _pad_copy_rows(kp, k_ref, pad, S)
        _pad_copy_lanes(vTp, vT_ref, pad, S)
        rowi = lax.broadcasted_iota(jnp.int32, (BU, PAIR), 0)

        def step(n2, carry):
            r = pl.multiple_of(n2 * PAIR, PAIR)
            valid = rowi >= (P - 2 * n2) * CHUNK
            for g in range(GQ):
                kv = kvoff(g)
                hs = slice(g * HEAD_DIM, (g + 1) * HEAD_DIM)
                kvs = slice(kv * HEAD_DIM, (kv + 1) * HEAD_DIM)
                qTg = qT_ref[hs, pl.ds(r, PAIR)] * 0.125
                p, _ = _bandT_softmax(kp[kv, pl.ds(r, BU), :], qTg, b_ref[g], s_ref[g, 0:1, :], valid)
                oTg = jnp.dot(vTp[kvs, pl.ds(r, BU)], p.astype(BF16), preferred_element_type=F32)
                oT_ref[hs, pl.ds(r, PAIR)] = oTg.astype(oT_ref.dtype)
            return carry

        lax.fori_loop(0, npair, step, 0, unroll=min(2, npair))

    res, got = _call_hosting(
        body, comm=comm, grid=(ng,),
        in_specs=[_fm_spec(GQ * HEAD_DIM, S, q0),
                  pl.BlockSpec((GK, S, HEAD_DIM), lambda i: (i, 0, 0)),
                  _fm_spec(GK * HEAD_DIM, S, v0),
                  pl.BlockSpec((GQ, BU, PAIR), lambda i: (i, 0, 0)),
                  pl.BlockSpec((GQ, 8, LANE), lambda i: (i, 0, 0))],
        out_specs=[pl.BlockSpec((GQ * HEAD_DIM, S), lambda i: (i, 0))],
        out_shape=[SDS((ng * GQ * HEAD_DIM, S), BF16)],
        scratch_shapes=[pltpu.VMEM((GK, S + pad, HEAD_DIM), BF16), pltpu.VMEM((GK * HEAD_DIM, S + pad), BF16)],
        name=name, args=(qT, k_h, vT, bias, sink))
    return res[0], got


def _bandT_bwd(qT, q_h, k_h, kT, v_h, doT, do_h, bias, sink, *, GQ, GK, P, kvoff, name, comm=None):
    (qT, q0), (kT, k0), (doT, d0) = _fm(qT), _fm(kT), _fm(doT)
    S = qT.shape[1]
    ng = bias.shape[0] // GQ
    BU = (P + 2) * CHUNK
    pad = P * CHUNK
    npair = S // PAIR

    def body(qT_ref, q_ref, k_ref, kT_ref, v_ref, doT_ref, do_ref, b_ref, s_ref,
             dqT_ref, dk_ref, dv_ref, db_ref, dsk_ref, kp, kTp, vp, dkp, dvp):
        _pad_copy_rows(kp, k_ref, pad, S)
        _pad_copy_rows(vp, v_ref, pad, S)
        _pad_copy_lanes(kTp, kT_ref, pad, S)
        dkp[...] = jnp.zeros_like(dkp)
        dvp[...] = jnp.zeros_like(dvp)
        db_ref[...] = jnp.zeros_like(db_ref)
        rowi = lax.broadcasted_iota(jnp.int32, (BU, PAIR), 0)

        def step(n2, dsink):
            r = pl.multiple_of(n2 * PAIR, PAIR)
            valid = rowi >= (P - 2 * n2) * CHUNK
            new = []
            for g in range(GQ):
                kv = kvoff(g)
                hs = slice(g * HEAD_DIM, (g + 1) * HEAD_DIM)
                kvs = slice(kv * HEAD_DIM, (kv + 1) * HEAD_DIM)
                qTg = qT_ref[hs, pl.ds(r, PAIR)] * 0.125
                p, ps = _bandT_softmax(kp[kv, pl.ds(r, BU), :], qTg, b_ref[g], s_ref[g, 0:1, :], valid)
                dp = jnp.dot(vp[kv, pl.ds(r, BU), :], doT_ref[hs, pl.ds(r, PAIR)], preferred_element_type=F32)
                delta = jnp.sum(p * dp, axis=0, keepdims=True)
                ds = p * (dp - delta)
                new.append(dsink[g] - ps * delta)
                db_ref[g] += ds
                dsb = ds.astype(BF16)
                dq = jnp.dot(kTp[kvs, pl.ds(r, BU)], dsb, preferred_element_type=F32) * 0.125
                dqT_ref[hs, pl.ds(r, PAIR)] = dq.astype(dqT_ref.dtype)
                dkp[kv, pl.ds(r, BU), :] += jnp.dot(dsb, q_ref[g, pl.ds(r, PAIR), :] * 0.125,
                                                    preferred_element_type=F32)
                dvp[kv, pl.ds(r, BU), :] += jnp.dot(p.astype(BF16), do_ref[g, pl.ds(r, PAIR), :],
                                                    preferred_element_type=F32)
            return tuple(new)

        dsink = lax.fori_loop(0, npair, step, tuple(jnp.zeros((1, PAIR), F32) for _ in range(GQ)))
        for g in range(GQ):
            dsk_ref[g] = jnp.broadcast_to(jnp.sum(dsink[g], axis=1, keepdims=True), (8, LANE))
        dk_ref[...] = dkp[:, pad:pad + S, :].astype(dk_ref.dtype)
        dv_ref[...] = dvp[:, pad:pad + S, :].astype(dv_ref.dtype)

    qTs = pl.BlockSpec((GQ * HEAD_DIM, S), lambda i: (i, 0))
    qhs = pl.BlockSpec((GQ, S, HEAD_DIM), lambda i: (i, 0, 0))
    khs = pl.BlockSpec((GK, S, HEAD_DIM), lambda i: (i, 0, 0))
    bs = pl.BlockSpec((GQ, BU, PAIR), lambda i: (i, 0, 0))
    ss = pl.BlockSpec((GQ, 8, LANE), lambda i: (i, 0, 0))
    nkv = ng * GK
    return _call_hosting(
        body, comm=comm, grid=(ng,),
        in_specs=[_fm_spec(GQ * HEAD_DIM, S, q0), qhs, khs, _fm_spec(GK * HEAD_DIM, S, k0), khs,
                  _fm_spec(GQ * HEAD_DIM, S, d0), qhs, bs, ss],
        out_specs=[qTs, khs, khs, bs, ss],
        out_shape=[SDS((ng * GQ * HEAD_DIM, S), BF16), SDS((nkv, S, HEAD_DIM), BF16), SDS((nkv, S, HEAD_DIM), BF16),
                   SDS((ng * GQ, BU, PAIR), F32), SDS((ng * GQ, 8, LANE), F32)],
        scratch_shapes=[pltpu.VMEM((GK, S + pad, HEAD_DIM), BF16), pltpu.VMEM((GK * HEAD_DIM, S + pad), BF16),
                        pltpu.VMEM((GK, S + pad, HEAD_DIM), BF16),
                        pltpu.VMEM((GK, S + pad, HEAD_DIM), F32), pltpu.VMEM((GK, S + pad, HEAD_DIM), F32)],
        name=name, args=(qT, q_h, k_h, kT, v_h, doT, do_h, bias, sink))


def _pair_table(tab):
    t = jnp.transpose(tab, (0, 2, 1))
    lo = jnp.pad(t, ((0, 0), (0, CHUNK), (0, 0)), constant_values=NEG_INF)
    hi = jnp.pad(t, ((0, 0), (CHUNK, 0), (0, 0)), constant_values=NEG_INF)
    return jnp.concatenate([lo, hi], axis=2)


def _unpair_table(d):
    band = d.shape[1] - CHUNK
    return jnp.transpose(d[:, 0:band, 0:CHUNK] + d[:, CHUNK:CHUNK + band, CHUNK:PAIR], (0, 2, 1))


def _heads(a, n):
    return jnp.transpose(a.reshape(a.shape[0], n, HEAD_DIM), (1, 0, 2))


def _unheads(a):
    return jnp.transpose(a, (1, 0, 2)).reshape(a.shape[1], a.shape[0] * HEAD_DIM)


def _foxT_logits(kj, qTg, cq, ck, r, c, rowi, coli):
    s = jnp.dot(kj, qTg, preferred_element_type=F32)
    s = s + cq - ck
    return jnp.where(c + rowi <= r + coli, s, NEG_INF)


def _foxT_fwd(qT, k_h, vT, ck, cq, name, comm=None):
    (qT, q0), (vT, v0) = _fm(qT), _fm(vT)
    S = qT.shape[1]
    npair = k_h.shape[0] // 2
    BQ, BK = min(FOX_BQ, S), min(FOX_BK, S)
    nq = S // BQ
    heads = [slice(g * HEAD_DIM, (g + 1) * HEAD_DIM) for g in range(2)]

    def body(qT_ref, k_ref, vT_ref, ck_ref, cq_ref, oT_ref, lse_ref):
        rowi = lax.broadcasted_iota(jnp.int32, (BK, BQ), 0)
        coli = lax.broadcasted_iota(jnp.int32, (BK, BQ), 1)

        def qstep(i, carry):
            r = pl.multiple_of(i * BQ, BQ)
            qs = [qT_ref[hs, pl.ds(r, BQ)] * 0.125 for hs in heads]
            cqs = [cq_ref[g, :, pl.ds(r, BQ)] for g in range(2)]

            def kstep(j, st):
                c = pl.multiple_of(j * BK, BK)
                new = []
                for g, hs in enumerate(heads):
                    m, l, acc = st[g]
                    s = _foxT_logits(k_ref[g, pl.ds(c, BK), :], qs[g], cqs[g], ck_ref[g, pl.ds(c, BK), :],
                                     r, c, rowi, coli)
                    mn = jnp.maximum(m, jnp.max(s, axis=0, keepdims=True))
                    al = jnp.exp(m - mn)
                    e = jnp.exp(s - mn)
                    l = al * l + jnp.sum(e, axis=0, keepdims=True)
                    acc = al * acc + jnp.dot(vT_ref[hs, pl.ds(c, BK)], e.astype(BF16), preferred_element_type=F32)
                    new.append((mn, l, acc))
                return tuple(new)

            init = (jnp.full((1, BQ), NEG_INF, F32), jnp.zeros((1, BQ), F32), jnp.zeros((HEAD_DIM, BQ), F32))
            st = lax.fori_loop(0, (r + BQ + BK - 1) // BK, kstep, (init, init))
            for g, hs in enumerate(heads):
                m, l, acc = st[g]
                oT_ref[hs, pl.ds(r, BQ)] = (acc * (1.0 / l)).astype(oT_ref.dtype)
                lse_ref[g, :, pl.ds(r, BQ)] = m + jnp.log(l)
            return carry

        lax.fori_loop(0, nq, qstep, 0)

    fT = pl.BlockSpec((LANE, S), lambda i: (i, 0))
    hm = pl.BlockSpec((2, S, HEAD_DIM), lambda i: (i, 0, 0))
    col = pl.BlockSpec((2, S, 1), lambda i: (i, 0, 0))
    rw = pl.BlockSpec((2, 1, S), lambda i: (i, 0, 0))
    return _call_hosting(
        body, comm=comm, grid=(npair,), in_specs=[_fm_spec(LANE, S, q0), hm, _fm_spec(LANE, S, v0), col, rw],
        out_specs=[fT, rw],
        out_shape=[SDS((npair * LANE, S), BF16), SDS((2 * npair, 1, S), F32)], scratch_shapes=[],
        name=name, args=(qT, k_h, vT, ck, cq))


def _foxT_bwd(qT, q_h, k_h, kT, v_h, ck, cq, oT, doT, do_h, lse, name, comm=None):
    (qT, q0), (kT, k0), (doT, d0) = _fm(qT), _fm(kT), _fm(doT)
    S = qT.shape[1]
    npair = k_h.shape[0] // 2
    BQ, BK = min(FOX_BQ, S), min(FOX_BK, S)
    nq = S // BQ
    heads = [slice(g * HEAD_DIM, (g + 1) * HEAD_DIM) for g in range(2)]

    def body(qT_ref, q_ref, k_ref, kT_ref, v_ref, ck_ref, cq_ref, oT_ref, doT_ref, do_ref, lse_ref,
             dqT_ref, dk_ref, dv_ref, dck_ref, dcq_ref, dka, dva, qa_ref):
        qa_ref[:, :, 0:HEAD_DIM] = q_ref[...] * 0.125
        qa_ref[:, :, HEAD_DIM:LANE] = jnp.ones((2, S, LANE - HEAD_DIM), BF16)
        dka[...] = jnp.zeros_like(dka)
        dva[...] = jnp.zeros_like(dva)
        rowi = lax.broadcasted_iota(jnp.int32, (BK, BQ), 0)
        coli = lax.broadcasted_iota(jnp.int32, (BK, BQ), 1)

        def qstep(i, carry):
            r = pl.multiple_of(i * BQ, BQ)
            qs = [qT_ref[hs, pl.ds(r, BQ)] * 0.125 for hs in heads]
            dos = [doT_ref[hs, pl.ds(r, BQ)] for hs in heads]
            deltas = [jnp.sum(dos[g].astype(F32) * oT_ref[hs, pl.ds(r, BQ)].astype(F32), axis=0, keepdims=True)
                      for g, hs in enumerate(heads)]
            cqs = [cq_ref[g, :, pl.ds(r, BQ)] for g in range(2)]
            lses = [lse_ref[g, :, pl.ds(r, BQ)] for g in range(2)]

            def kstep(j, st):
                c = pl.multiple_of(j * BK, BK)
                new = []
                for g, hs in enumerate(heads):
                    dq, rs = st[g]
                    s = _foxT_logits(k_ref[g, pl.ds(c, BK), :], qs[g], cqs[g], ck_ref[g, pl.ds(c, BK), :],
                                     r, c, rowi, coli)
                    p = jnp.exp(s - lses[g])
                    dp = jnp.dot(v_ref[g, pl.ds(c, BK), :], dos[g], preferred_element_type=F32)
                    ds = p * (dp - deltas[g])
                    dsb = ds.astype(BF16)
                    dka[g, pl.ds(c, BK), :] += jnp.dot(dsb, qa_ref[g, pl.ds(r, BQ), :], preferred_element_type=F32)
                    dva[g, pl.ds(c, BK), :] += jnp.dot(p.astype(BF16), do_ref[g, pl.ds(r, BQ), :],
                                                      preferred_element_type=F32)
                    new.append((dq + jnp.dot(kT_ref[hs, pl.ds(c, BK)], dsb, preferred_element_type=F32),
                                rs + jnp.sum(dsb.astype(F32), axis=0, keepdims=True)))
                return tuple(new)

            init = (jnp.zeros((HEAD_DIM, BQ), F32), jnp.zeros((1, BQ), F32))
            st = lax.fori_loop(0, (r + BQ + BK - 1) // BK, kstep, (init, init))
            for g, hs in enumerate(heads):
                dqT_ref[hs, pl.ds(r, BQ)] = (st[g][0] * 0.125).astype(dqT_ref.dtype)
                dcq_ref[g, :, pl.ds(r, BQ)] = st[g][1]
            return carry

        lax.fori_loop(0, nq, qstep, 0)
        dk_ref[...] = dka[:, :, 0:HEAD_DIM].astype(dk_ref.dtype)
        dck_ref[...] = -dka[:, :, HEAD_DIM:HEAD_DIM + 1]
        dv_ref[...] = dva[...].astype(dv_ref.dtype)

    fT = pl.BlockSpec((LANE, S), lambda i: (i, 0))
    hm = pl.BlockSpec((2, S, HEAD_DIM), lambda i: (i, 0, 0))
    col = pl.BlockSpec((2, S, 1), lambda i: (i, 0, 0))
    rw = pl.BlockSpec((2, 1, S), lambda i: (i, 0, 0))
    nh = 2 * npair
    return _call_hosting(
        body, comm=comm, grid=(npair,),
        in_specs=[_fm_spec(LANE, S, q0), hm, hm, _fm_spec(LANE, S, k0), hm, col, rw, fT, _fm_spec(LANE, S, d0), hm, rw],
        out_specs=[fT, hm, hm, col, rw],
        out_shape=[SDS((npair * LANE, S), BF16), SDS((nh, S, HEAD_DIM), BF16), SDS((nh, S, HEAD_DIM), BF16),
                   SDS((nh, S, 1), F32), SDS((nh, 1, S), F32)],
        scratch_shapes=[pltpu.VMEM((2, S, LANE), F32), pltpu.VMEM((2, S, HEAD_DIM), F32),
                        pltpu.VMEM((2, S, LANE), BF16)],
        name=name, args=(qT, q_h, k_h, kT, v_h, ck, cq, oT, doT, do_h, lse))


def _split3(x):
    hi = x.astype(BF16)
    r1 = x - hi.astype(F32)
    mid = r1.astype(BF16)
    lo = (r1 - mid.astype(F32)).astype(BF16)
    return hi, mid, lo


def _tri_dot(tri, x):
    hi, mid, lo = _split3(x)
    return (jnp.dot(tri, hi, preferred_element_type=F32) + jnp.dot(tri, mid, preferred_element_type=F32)
            + jnp.dot(tri, lo, preferred_element_type=F32))


def _fox_cum(gf, bfo, name):
    S = gf.shape[0]
    nb = S // LANE
    fcol = (GF_COLS - LANE) // LANE

    def body(f_ref, b_ref, cum_ref):
        row = lax.broadcasted_iota(jnp.int32, (LANE, LANE), 0)
        col = lax.broadcasted_iota(jnp.int32, (LANE, LANE), 1)
        tri = jnp.where(row >= col, 1.0, 0.0).astype(BF16)
        carry = jnp.zeros((1, LANE), F32)
        for t in range(nb):
            xl = f_ref[t * LANE:(t + 1) * LANE, :] + b_ref[...]
            lf = jnp.minimum(xl, 0.0) - jnp.log(1.0 + jnp.exp(-jnp.abs(xl)))
            cblk = _tri_dot(tri, lf) + carry
            cum_ref[t * LANE:(t + 1) * LANE, :] = cblk
            carry = cblk[LANE - 1:LANE, :]

    return pl.pallas_call(
        body, grid=(1,), in_specs=[pl.BlockSpec((S, LANE), lambda i: (0, fcol)), _vec(LANE)],
        out_specs=pl.BlockSpec((S, LANE), lambda i: (0, 0)), out_shape=SDS((S, LANE), F32),
        compiler_params=_cparams(("arbitrary",)), name=name)(gf, bfo)


def _fox_cum_bwd(gf, bfo, dcum, name):
    S = gf.shape[0]
    nb = S // LANE
    fcol = (GF_COLS - LANE) // LANE

    def body(f_ref, b_ref, dc_ref, df_ref, db_ref):
        row = lax.broadcasted_iota(jnp.int32, (LANE, LANE), 0)
        col = lax.broadcasted_iota(jnp.int32, (LANE, LANE), 1)
        tri = jnp.where(row <= col, 1.0, 0.0).astype(BF16)
        carry = jnp.zeros((1, LANE), F32)
        tot = jnp.zeros((1, LANE), F32)
        for t in range(nb - 1, -1, -1):
            rows = slice(t * LANE, (t + 1) * LANE)
            dlf = _tri_dot(tri, dc_ref[rows, :]) + carry
            carry = dlf[0:1, :]
            xl = f_ref[rows, :] + b_ref[...]
            dfl = dlf * (1.0 / (1.0 + jnp.exp(xl)))
            df_ref[rows, :] = dfl.astype(df_ref.dtype)
            tot = tot + jnp.sum(dfl, axis=0, keepdims=True)
        db_ref[...] = tot

    return pl.pallas_call(
        body, grid=(1,),
        in_specs=[pl.BlockSpec((S, LANE), lambda i: (0, fcol)), _vec(LANE), pl.BlockSpec((S, LANE), lambda i: (0, 0))],
        out_specs=[pl.BlockSpec((S, LANE), lambda i: (0, 0)), _vec(LANE)],
        out_shape=[SDS((S, LANE), BF16), SDS((1, LANE), F32)],
        compiler_params=_cparams(("arbitrary",)), name=name)(gf, bfo, dcum)


REL_FAR = C_PREV * CHUNK - REL_CLIP


def _rel_onehot(qi, band):
    w = band - REL_FAR
    r = lax.broadcasted_iota(jnp.int32, (N_REL_PAD, w), 0)
    j = lax.broadcasted_iota(jnp.int32, (N_REL_PAD, w), 1) + REL_FAR
    idx = jnp.clip(C_PREV * CHUNK + qi - j, -REL_CLIP, REL_CLIP) + REL_CLIP
    return jnp.where(r == idx, 1.0, 0.0).astype(BF16)


def _rel_expand(rel, name):
    band = (C_PREV + 1) * CHUNK

    def body(rel_ref, o_ref):
        hi, mid, lo = _split3(rel_ref[...])
        far = jnp.broadcast_to(rel_ref[:, 2 * REL_CLIP:2 * REL_CLIP + 1], (N_HEADS, REL_FAR))

        def row(qi, carry):
            oh = _rel_onehot(qi, band)
            o_ref[qi, :, 0:REL_FAR] = far
            o_ref[qi, :, REL_FAR:band] = (jnp.dot(hi, oh, preferred_element_type=F32)
                                          + jnp.dot(mid, oh, preferred_element_type=F32)
                                          + jnp.dot(lo, oh, preferred_element_type=F32))
            return carry

        lax.fori_loop(0, CHUNK, row, 0, unroll=2)

    return pl.pallas_call(
        body, grid=(1,), in_specs=[pl.BlockSpec((N_HEADS, N_REL_PAD), lambda i: (0, 0))],
        out_specs=pl.BlockSpec((CHUNK, N_HEADS, band), lambda i: (0, 0, 0)),
        out_shape=SDS((CHUNK, N_HEADS, band), F32),
        compiler_params=_cparams(("arbitrary",)), name=name)(rel)


def _rel_reduce(dbias, name):
    band = (C_PREV + 1) * CHUNK
    NT = (((1,), (1,)), ((), ()))

    def body(d_ref, o_ref):
        def row(qi, st):
            acc, far = st
            oh = _rel_onehot(qi, band)
            hi, mid, lo = _split3(d_ref[qi, :, REL_FAR:band])
            acc = acc + (lax.dot_general(hi, oh, NT, preferred_element_type=F32)
                         + lax.dot_general(mid, oh, NT, preferred_element_type=F32)
                         + lax.dot_general(lo, oh, NT, preferred_element_type=F32))
            return acc, far + jnp.sum(d_ref[qi, :, 0:REL_FAR], axis=-1, keepdims=True)

        acc, far = lax.fori_loop(0, CHUNK, row, (jnp.zeros((N_HEADS, N_REL_PAD), F32), jnp.zeros((N_HEADS, 1), F32)),
                                 unroll=2)
        col = lax.broadcasted_iota(jnp.int32, (N_HEADS, N_REL_PAD), 1)
        o_ref[...] = acc + jnp.where(col == 2 * REL_CLIP, far, 0.0)

    return pl.pallas_call(
        body, grid=(1,), in_specs=[pl.BlockSpec((CHUNK, N_HEADS, band), lambda i: (0, 0, 0))],
        out_specs=pl.BlockSpec((N_HEADS, N_REL_PAD), lambda i: (0, 0)),
        out_shape=SDS((N_HEADS, N_REL_PAD), F32),
        compiler_params=_cparams(("arbitrary",)), name=name)(dbias)


def _alibi_table():
    qi = np.arange(CHUNK)[:, None]
    j = np.arange((A_PREV + 1) * CHUNK)[None, :]
    dist = np.abs(A_PREV * CHUNK + qi - j).astype(np.float32)
    slopes = np.exp2(-8.0 * np.arange(1, N_HEADS + 1, dtype=np.float32) / N_HEADS).astype(np.float32)
    return jnp.asarray(-slopes[:, None, None] * dist[None])


def _ada_fwd(c_all, w, b, name):
    n = w.shape[2]

    def body(c_ref, w_ref, b_ref, o_ref):
        cv = c_ref[...]
        cond = (cv * _sigmoid(cv)).astype(BF16)
        o_ref[0] = jnp.dot(cond, w_ref[0].astype(BF16), preferred_element_type=F32) + b_ref[0]

    return pl.pallas_call(
        body, grid=(DEPTH,),
        in_specs=[pl.BlockSpec((16, D_MODEL), lambda l: (0, 0)), pl.BlockSpec((1, D_MODEL, n), lambda l: (l, 0, 0)),
                  pl.BlockSpec((1, 1, n), lambda l: (l, 0, 0))],
        out_specs=pl.BlockSpec((1, 16, n), lambda l: (l, 0, 0)), out_shape=SDS((DEPTH, 16, n), F32),
        compiler_params=_cparams(("parallel",)), name=name)(c_all, w, b)


def _ada_bwd(c_t, dmod, name, comm=None):
    n = dmod.shape[2]
    bn = _blk(n, 512)
    tr = 256

    def body(c_ref, d_ref, o_ref):
        cv = c_ref[...]
        cond = (cv * _sigmoid(cv)).astype(BF16).astype(F32)
        dm = d_ref[0].astype(BF16).astype(F32)
        acc = cond[:, 0:1] * dm[0:1, :]
        for b_ in range(1, 8):
            acc = acc + cond[:, b_:b_ + 1] * dm[b_:b_ + 1, :]
        o_ref[0] = acc

    res, got = _call_hosting(
        body, comm=comm, grid=(DEPTH, D_MODEL // tr, n // bn),
        in_specs=[pl.BlockSpec((tr, 8), lambda l, i, j: (i, 0)), pl.BlockSpec((1, 8, bn), lambda l, i, j: (l, 0, j))],
        out_specs=[pl.BlockSpec((1, tr, bn), lambda l, i, j: (l, i, j))], out_shape=[SDS((DEPTH, D_MODEL, n), F32)],
        scratch_shapes=[], name=name, args=(c_t, dmod))
    return res[0], got


def _adamw(w, m, v, parts, name):
    L, R, C = w.shape
    per_layer = isinstance(parts, (list, tuple))
    plist = list(parts) if per_layer else [parts]
    P = plist[0].shape[0]
    tr = _blk_rows(R, max(16, (1 << 18) // C))
    nr = R // tr
    c1 = 1.0 - ADAM_B1 ** ADAM_STEP
    c2 = 1.0 - ADAM_B2 ** ADAM_STEP

    def total(p_ref):
        g = p_ref[0].astype(F32)
        for k in range(1, P):
            g = g + p_ref[k].astype(F32)
        return g

    def body(w_ref, m_ref, v_ref, *rest):
        p_refs, (g_ref, d_ref, nm_ref, nv_ref) = rest[:len(plist)], rest[len(plist):]
        g = total(p_refs[0])
        for k in range(1, len(plist)):
            g = jnp.where(pl.program_id(0) == k, total(p_refs[k]), g)
        mn = ADAM_B1 * m_ref[0] + (1.0 - ADAM_B1) * g
        vn = ADAM_B2 * v_ref[0] + (1.0 - ADAM_B2) * (g * g)
        m_hat = mn / c1
        v_hat = vn / c2
        g_ref[0] = g
        nm_ref[0] = mn
        nv_ref[0] = vn
        d_ref[0] = -ADAM_LR * (m_hat / (jnp.sqrt(v_hat) + ADAM_EPS) + ADAM_WD * w_ref[0])

    rs = pl.BlockSpec((1, tr, C), lambda l, i: (l, i, 0))
    if per_layer:
        def layer_spec(k):
            return pl.BlockSpec((P, tr, C), lambda l, i: (0, jnp.where(l == k, i, 0), 0))
        pspecs = [layer_spec(k) for k in range(L)]
    else:
        pspecs = [pl.BlockSpec((P, tr, C), lambda l, i: (0, l * nr + i, 0))]
    return pl.pallas_call(
        body, grid=(L, nr), in_specs=[rs, rs, rs] + pspecs, out_specs=[rs, rs, rs, rs],
        out_shape=[SDS((L, R, C), F32)] * 4, compiler_params=_cparams(("parallel", "parallel")),
        name=name)(w, m, v, *plist)


def _blk_rows(R, cap):
    if R <= cap:
        return R
    best = None
    for t in range(16, cap + 1, 16):
        if R % t == 0:
            best = t
    assert best is not None, (R, cap)
    return best


def _add_cast_rows(g, t, name):
    Q, R, C = g.shape
    half = R // 2
    tr = _blk_rows(half, max(16, (1 << 19) // C))
    nb = half // tr

    def body(lo_ref, hi_ref, t_ref, o_ref):
        c = lax.axis_index("c")

        @pl.when(c == 0)
        def _():
            o_ref[...] = (lo_ref[...].astype(F32) + t_ref[...].astype(F32)).astype(o_ref.dtype)

        @pl.when(c == 1)
        def _():
            o_ref[...] = (hi_ref[...].astype(F32) + t_ref[...].astype(F32)).astype(o_ref.dtype)

    bs = pl.BlockSpec((1, tr, C), lambda q, i: (q, i, 0))
    hi = pl.BlockSpec((1, tr, C), lambda q, i: (q, nb + i, 0))
    return pl.pallas_call(
        body, grid=(Q, nb), in_specs=[bs, hi, bs], out_specs=bs, out_shape=SDS((Q, half, C), BF16),
        compiler_params=_cparams(("parallel", "parallel")), name=name)(g, g, t)


def _coords():
    return lax.axis_index("x"), lax.axis_index("y"), lax.axis_index("c")


def _flip(v, bit):
    return 1 - v if bit else v


def _all_gather8(v, name):
    R = v.shape[0]

    def body(v_ref, o_ref, send_sems, recv_sems):
        x, y, c = _coords()
        me = 4 * x + 2 * y + c
        o_ref[me] = v_ref[...]
        copies = []
        for k in range(1, 8):
            peer = (_flip(x, k & 4), _flip(y, k & 2), _flip(c, k & 1))
            cp = pltpu.make_async_remote_copy(
                src_ref=v_ref, dst_ref=o_ref.at[me], send_sem=send_sems.at[k - 1], recv_sem=recv_sems.at[k - 1],
                device_id=peer, device_id_type=MESH)
            cp.start()
            copies.append(cp)
        for cp in copies:
            cp.wait_recv()
        for cp in copies:
            cp.wait_send()

    return pl.pallas_call(
        body, in_specs=[VMEM_SPEC], out_specs=VMEM_SPEC, out_shape=SDS((8, R, LANE), v.dtype),
        scratch_shapes=[pltpu.SemaphoreType.DMA((7,)), pltpu.SemaphoreType.DMA((7,))],
        compiler_params=pltpu.CompilerParams(vmem_limit_bytes=VMEM_LIMIT), name=name)(v)


def _sibling_swap_rows(arrs, name):
    n = len(arrs)

    def body(*refs):
        in_refs, out_refs = refs[:n], refs[n:2 * n]
        send_sems, recv_sems = refs[2 * n:]
        x, y, c = _coords()
        copies = []
        for a in range(n):
            Q, R = in_refs[a].shape[0], in_refs[a].shape[1]
            half = R // 2
            src = in_refs[a].at[pl.ds(0, Q), pl.ds(pl.multiple_of((1 - c) * half, 16), half)]
            cp = pltpu.make_async_remote_copy(
                src_ref=src, dst_ref=out_refs[a], send_sem=send_sems.at[a], recv_sem=recv_sems.at[a],
                device_id=(x, y, 1 - c), device_id_type=MESH)
            cp.start()
            copies.append(cp)
        for cp in copies:
            cp.wait_recv()
        for cp in copies:
            cp.wait_send()

    return pl.pallas_call(
        body, in_specs=[ANY] * n, out_specs=[ANY] * n,
        out_shape=[SDS((a.shape[0], a.shape[1] // 2, a.shape[2]), a.dtype) for a in arrs],
        scratch_shapes=[pltpu.SemaphoreType.DMA((n,)), pltpu.SemaphoreType.DMA((n,))],
        name=name)(*arrs)


class _OwnerReduce:
    aliased = False

    def __init__(self, srcs, lay):
        self.srcs, self.lay, self.n = list(srcs), lay, len(srcs)
        self.out_shapes = [SDS(a.shape, a.dtype) for a in self.srcs]
        self.sem_shapes = [pltpu.SemaphoreType.DMA((self.n, 3)), pltpu.SemaphoreType.DMA((self.n, 3)),
                           pltpu.SemaphoreType.DMA((self.n,))]

    def _copies(self, src_refs, dst_refs, sems):
        ici_send, ici_recv, loc_sem = sems
        x, y, c = _coords()
        p = 2 * x + y
        local, remote = [], []
        for a in range(self.n):
            local.append(pltpu.make_async_copy(src_refs[a].at[p], dst_refs[a].at[p], loc_sem.at[a]))
            for k in range(1, 4):
                qx, qy = _flip(x, k & 2), _flip(y, k & 1)
                remote.append(pltpu.make_async_remote_copy(
                    src_ref=src_refs[a].at[2 * qx + qy], dst_ref=dst_refs[a].at[p], send_sem=ici_send.at[a, k - 1],
                    recv_sem=ici_recv.at[a, k - 1], device_id=(qx, qy, self.lay), device_id_type=MESH))
        return c, local, remote

    def start(self, src_refs, dst_refs, sems):
        c, local, remote = self._copies(src_refs, dst_refs, sems)

        @pl.when(c == self.lay)
        def _():
            for cp in local + remote:
                cp.start()

    def finish(self, src_refs, dst_refs, sems):
        c, local, remote = self._copies(src_refs, dst_refs, sems)

        @pl.when(c == self.lay)
        def _():
            for cp in remote:
                cp.wait_recv()
            for cp in remote:
                cp.wait_send()
            for cp in local:
                cp.wait()


def _call_hosting(body, *, comm, grid, in_specs, out_specs, out_shape, scratch_shapes, name, args, semantics=None):
    n_in, n_out, n_scr = len(args), len(out_shape), len(scratch_shapes)
    if comm is None:
        sem = semantics if semantics is not None else ("parallel",) * len(grid)
        res = pl.pallas_call(body, grid=grid, in_specs=in_specs, out_specs=out_specs, out_shape=out_shape,
                             scratch_shapes=scratch_shapes, compiler_params=_cparams(sem), name=name)(*args)
        return list(res), None
    k = comm.n

    def hosted(*refs):
        ins, cin = refs[:n_in], refs[n_in:n_in + k]
        outs = refs[n_in + k:n_in + k + n_out]
        cout = refs[n_in + k + n_out:n_in + 2 * k + n_out]
        scr = refs[n_in + 2 * k + n_out:n_in + 2 * k + n_out + n_scr]
        sems = refs[n_in + 2 * k + n_out + n_scr:]
        first = pl.program_id(0) == 0
        last = pl.program_id(0) == grid[0] - 1
        for d in range(1, len(grid)):
            first = jnp.logical_and(first, pl.program_id(d) == 0)
            last = jnp.logical_and(last, pl.program_id(d) == grid[d] - 1)

        @pl.when(first)
        def _():
            comm.start(cin, cout, sems)

        body(*ins, *outs, *scr)

        @pl.when(last)
        def _():
            comm.finish(cin, cout, sems)

    aliases = {n_in + j: n_out + j for j in range(k)} if comm.aliased else {}
    res = pl.pallas_call(
        hosted, grid=grid, in_specs=list(in_specs) + [ANY] * k, out_specs=list(out_specs) + [ANY] * k,
        out_shape=list(out_shape) + comm.out_shapes, scratch_shapes=list(scratch_shapes) + comm.sem_shapes,
        input_output_aliases=aliases, compiler_params=_cparams(("arbitrary",) * len(grid)),
        name=name)(*args, *comm.srcs)
    return list(res[:n_out]), list(res[n_out:])


class _RowHalfGather:
    aliased = False

    def __init__(self, srcs):
        self.srcs, self.n = list(srcs), len(srcs)
        self.out_shapes = [SDS((4,) + a.shape, a.dtype) for a in self.srcs]
        n = self.n
        self.sem_shapes = [pltpu.SemaphoreType.DMA((n, 3)), pltpu.SemaphoreType.DMA((n, 3)),
                           pltpu.SemaphoreType.DMA((n, 3)), pltpu.SemaphoreType.DMA((n, 3)),
                           pltpu.SemaphoreType.DMA((n,))]

    def _copies(self, src_refs, dst_refs, sems):
        ici_send, ici_recv, d2d_send, d2d_recv, loc_sem = sems
        x, y, c = _coords()
        p = 2 * x + y
        local, first, fwd = [], [], []
        for a in range(self.n):
            R = src_refs[a].shape[0] // 2
            half = pl.ds(pl.multiple_of(c * R, 16), R)
            local.append(pltpu.make_async_copy(src_refs[a], dst_refs[a].at[p], loc_sem.at[a]))
            for k in range(1, 4):
                qx, qy = _flip(x, k & 2), _flip(y, k & 1)
                first.append(pltpu.make_async_remote_copy(
                    src_ref=src_refs[a].at[half], dst_ref=dst_refs[a].at[p, half], send_sem=ici_send.at[a, k - 1],
                    recv_sem=ici_recv.at[a, k - 1], device_id=(qx, qy, c), device_id_type=MESH))
                slot = dst_refs[a].at[2 * qx + qy, half]
                fwd.append(pltpu.make_async_remote_copy(
                    src_ref=slot, dst_ref=slot, send_sem=d2d_send.at[a, k - 1], recv_sem=d2d_recv.at[a, k - 1],
                    device_id=(x, y, 1 - c), device_id_type=MESH))
        return local, first, fwd

    def start(self, src_refs, dst_refs, sems):
        local, first, _ = self._copies(src_refs, dst_refs, sems)
        for cp in local + first:
            cp.start()

    def finish(self, src_refs, dst_refs, sems):
        local, first, fwd = self._copies(src_refs, dst_refs, sems)
        for got, on in zip(first, fwd):
            got.wait_recv()
            on.start()
        for cp in fwd:
            cp.wait_recv()
        for cp in first + fwd:
            cp.wait_send()
        for cp in local:
            cp.wait()

    def run(self, name):
        return _run_exchange(self, name)


def _run_exchange(comm, name):
    n = comm.n

    def body(*refs):
        src_refs, dst_refs, sems = refs[:n], refs[n:2 * n], refs[2 * n:]
        comm.start(src_refs, dst_refs, sems)
        comm.finish(src_refs, dst_refs, sems)

    return pl.pallas_call(body, in_specs=[ANY] * n, out_specs=[ANY] * n, out_shape=comm.out_shapes,
                          scratch_shapes=comm.sem_shapes, name=name)(*comm.srcs)


class _RowHalfReduce:
    aliased = False

    def __init__(self, srcs):
        self.srcs, self.n = list(srcs), len(srcs)
        self.out_shapes = [SDS((4, 2 * a.shape[1], a.shape[2]), a.dtype) for a in self.srcs]
        n = self.n
        self.sem_shapes = [pltpu.SemaphoreType.DMA((n, 3)), pltpu.SemaphoreType.DMA((n, 3)),
                           pltpu.SemaphoreType.DMA((n, 4)), pltpu.SemaphoreType.DMA((n, 4)),
                           pltpu.SemaphoreType.DMA((n,))]

    def _copies(self, src_refs, dst_refs, sems):
        ici_send, ici_recv, d2d_send, d2d_recv, loc_sem = sems
        x, y, c = _coords()
        p = 2 * x + y
        local, first, fwd = [], [], []
        for a in range(self.n):
            R = src_refs[a].shape[1]
            half = pl.ds(pl.multiple_of(c * R, 16), R)
            local.append(pltpu.make_async_copy(src_refs[a].at[p], dst_refs[a].at[p, half], loc_sem.at[a]))
            for k in range(4):
                qx, qy = _flip(x, k & 2), _flip(y, k & 1)
                if k:
                    first.append(pltpu.make_async_remote_copy(
                        src_ref=src_refs[a].at[2 * qx + qy], dst_ref=dst_refs[a].at[p, half],
                        send_sem=ici_send.at[a, k - 1], recv_sem=ici_recv.at[a, k - 1], device_id=(qx, qy, c),
                        device_id_type=MESH))
                slot = dst_refs[a].at[2 * qx + qy, half]
                fwd.append(pltpu.make_async_remote_copy(
                    src_ref=slot, dst_ref=slot, send_sem=d2d_send.at[a, k], recv_sem=d2d_recv.at[a, k],
                    device_id=(x, y, 1 - c), device_id_type=MESH))
        return local, first, fwd

    def start(self, src_refs, dst_refs, sems):
        local, first, _ = self._copies(src_refs, dst_refs, sems)
        for cp in local + first:
            cp.start()

    def finish(self, src_refs, dst_refs, sems):
        local, first, fwd = self._copies(src_refs, dst_refs, sems)
        for a in range(self.n):
            local[a].wait()
            fwd[4 * a].start()
            for k in range(1, 4):
                first[3 * a + k - 1].wait_recv()
                fwd[4 * a + k].start()
        for cp in fwd:
            cp.wait_recv()
        for cp in first + fwd:
            cp.wait_send()

    def run(self, name):
        return _run_exchange(self, name)


class _SiblingSend:
    aliased = False

    def __init__(self, srcs, src_core):
        self.srcs, self.src_core, self.n = list(srcs), src_core, len(srcs)
        self.out_shapes = [SDS(a.shape, a.dtype) for a in self.srcs]
        self.sem_shapes = [pltpu.SemaphoreType.DMA((self.n,)), pltpu.SemaphoreType.DMA((self.n,))]

    def _copies(self, src_refs, dst_refs, sems):
        x, y, c = _coords()
        return c, [pltpu.make_async_remote_copy(
            src_ref=src_refs[a], dst_ref=dst_refs[a], send_sem=sems[0].at[a], recv_sem=sems[1].at[a],
            device_id=(x, y, 1 - c), device_id_type=MESH) for a in range(self.n)]

    def start(self, src_refs, dst_refs, sems):
        c, copies = self._copies(src_refs, dst_refs, sems)

        @pl.when(c == self.src_core)
        def _():
            for cp in copies:
                cp.start()

    def finish(self, src_refs, dst_refs, sems):
        c, copies = self._copies(src_refs, dst_refs, sems)

        @pl.when(c == self.src_core)
        def _():
            for cp in copies:
                cp.wait_send()

        @pl.when(c != self.src_core)
        def _():
            for cp in copies:
                cp.wait_recv()


class _Handoff:
    aliased = True

    def __init__(self, srcs, lay, slots):
        self.srcs, self.lay, self.slots, self.n = list(srcs), lay, tuple(slots), len(srcs)
        self.out_shapes = [SDS(a.shape, a.dtype) for a in self.srcs]
        ns = len(self.slots)
        self.sem_shapes = [pltpu.SemaphoreType.DMA((self.n, ns)), pltpu.SemaphoreType.DMA((self.n, ns))]

    def _copies(self, dst_refs, sems):
        x, y, c = _coords()
        copies = []
        for a in range(self.n):
            for j, k in enumerate(self.slots):
                slot = dst_refs[a].at[2 * _flip(x, k & 2) + _flip(y, k & 1)]
                copies.append(pltpu.make_async_remote_copy(
                    src_ref=slot, dst_ref=slot, send_sem=sems[0].at[a, j], recv_sem=sems[1].at[a, j],
                    device_id=(x, y, 1 - c), device_id_type=MESH))
        return c, copies

    def start(self, src_refs, dst_refs, sems):
        c, copies = self._copies(dst_refs, sems)

        @pl.when(c == self.lay)
        def _():
            for cp in copies:
                cp.start()

    def finish(self, src_refs, dst_refs, sems):
        c, copies = self._copies(dst_refs, sems)

        @pl.when(c == self.lay)
        def _():
            for cp in copies:
                cp.wait_send()

        @pl.when(c != self.lay)
        def _():
            for cp in copies:
                cp.wait_recv()


def _add_cast_on(a, b, lay, name):
    Q, R, C = b.shape
    tr = _blk_rows(R, max(16, (1 << 19) // C))

    def body(a_ref, b_ref, o_ref):
        @pl.when(lax.axis_index("c") == lay)
        def _():
            o_ref[...] = (a_ref[...].astype(F32) + b_ref[...].astype(F32)).astype(o_ref.dtype)

    bs = pl.BlockSpec((1, tr, C), lambda q, i: (q, i, 0))
    return pl.pallas_call(
        body, grid=(Q, R // tr), in_specs=[bs, bs], out_specs=bs, out_shape=SDS((Q, R, C), BF16),
        compiler_params=_cparams(("parallel", "parallel")), name=name)(a, b)


_IN_SIZES = (512, 128, 128, 512, 512, 512, 8, 512, 512, 512, 3072)
_IN_OFF = tuple(int(v) for v in np.cumsum((0,) + _IN_SIZES))
_IN_Q = N_IN_COLS // 4


def _pack_w_in(w):
    def cols(lo, hi):
        out = []
        while lo < hi:
            q, off = divmod(lo, _IN_Q)
            n = min(hi - lo, _IN_Q - off)
            out.append(w[q, :, off:off + n])
            lo += n
        return out

    fb0, fb1, g0 = _IN_OFF[6], _IN_OFF[7], _IN_OFF[10]
    wqkv = jnp.concatenate(cols(0, fb0) + cols(fb1, g0), axis=1)
    wgf = jnp.concatenate(cols(g0, N_IN_COLS) + cols(fb0, fb1) + [jnp.zeros((w.shape[1], LANE - 8), w.dtype)], axis=1)
    return wqkv, wgf


def _unpack_w_in(dqkv, dgf):
    fb0, fb1, g0 = _IN_OFF[6], _IN_OFF[7], _IN_OFF[10]

    def cols(lo, hi):
        out = []
        while lo < hi:
            if lo < fb0:
                n = min(hi, fb0) - lo
                out.append(dqkv[:, lo:lo + n])
            elif lo < fb1:
                n = min(hi, fb1) - lo
                out.append(dgf[:, 3072 + lo - fb0:3072 + lo - fb0 + n])
            elif lo < g0:
                n = min(hi, g0) - lo
                out.append(dqkv[:, lo - 8:lo - 8 + n])
            else:
                n = hi - lo
                out.append(dgf[:, lo - g0:lo - g0 + n])
            lo += n
        return out

    return jnp.stack([jnp.concatenate(cols(q * _IN_Q, (q + 1) * _IN_Q), axis=1) for q in range(4)])


def _pad_rows(a, rows):
    return jnp.pad(a, ((0, rows - a.shape[0]), (0, 0)))


def _small_pack(parts):
    flat = jnp.concatenate([p.reshape(-1) for p in parts])
    n = flat.shape[0]
    rows = -(-n // LANE)
    rows = -(-rows // 8) * 8
    return jnp.pad(flat, (0, rows * LANE - n)).reshape(rows, LANE)


def _small_unpack(block, shapes):
    flat = block.reshape(-1)
    out, off = [], 0
    for s in shapes:
        n = int(np.prod(s))
        out.append(flat[off:off + n].reshape(s))
        off += n
    return out


def _kv_same(g):
    return 0


def _kv_own(g):
    return g


_mm_plain = _mm


def _mm_hosting(a, b, *, comm, **kw):
    if comm is None:
        return _mm(a, b, **kw), None
    return _mm(a, b, comm=comm, **kw)


def _layer_fwd(x, mod, p, l, ride):
    sh_m, sc_m, g_m, sh_f, sc_f, g_f = mod
    nm = "l%d_" % l

    def carried(name, run):
        res, got = run(ride.comm_for(name))
        if got is not None:
            ride.done(name, got)
        return res

    h1 = _norm_mod_fwd(x, p["norm_mix_g"], sc_m, sh_m, nm + "norm_mix_fwd")
    qkv = carried("proj_qkv", lambda cm: _mm_hosting(h1, p["wqkv"], mode="nn", out_dtype=BF16,
                                                     name=nm + "proj_qkv", comm=cm))
    gf = carried("proj_gf", lambda cm: _mm_hosting(h1, p["wgf"], mode="nn", out_dtype=F32, name=nm + "proj_gf",
                                                  cap_n=640, comm=cm))
    qkv_t = qkv.T
    o_a_t = carried("attn_a", lambda cm: _bandT_fwd(
        (qkv_t, 0), _heads(qkv[:, 512:640], A_KV_HEADS), (qkv_t, 640), p["alibi"], p["sink_tab"],
        GQ=4, GK=1, P=A_PREV, kvoff=_kv_same, name=nm + "attn_a_fwd", comm=cm))
    cum = _fox_cum(gf, p["b_forget_pad"], nm + "fox_cum")
    cum_t = cum[:, :N_HEADS].T
    cc, cr = cum_t[:, :, None], cum_t[:, None, :]
    o_b_t, lse_b = carried("attn_b", lambda cm: _foxT_fwd(
        (qkv_t, 768), _heads(qkv[:, 1280:1792], N_HEADS), (qkv_t, 1792), cc, cr, nm + "attn_b_fwd", comm=cm))
    o_c_t = carried("attn_c", lambda cm: _bandT_fwd(
        (qkv_t, 2304), _heads(qkv[:, 2816:3328], N_HEADS), (qkv_t, 3328), p["rel_tab"], p["no_sink"],
        GQ=2, GK=2, P=C_PREV, kvoff=_kv_own, name=nm + "attn_c_fwd", comm=cm))
    p = dict(p, **ride.late_weights())
    o = jnp.concatenate([o_a_t, o_b_t, o_c_t], axis=0).T
    y = _mm(o, p["wb"], mode="nn", out_dtype=BF16, groups=3, name=nm + "branch")
    merged = _merge_fwd(y, gf, nm + "merge_fwd")
    mix = _mm(merged, p["wout"], mode="nn", out_dtype=BF16, name=nm + "out_proj")
    x1 = _resid_fwd(x, mix, g_m, nm + "resid_mix")
    h2 = _norm_mod_fwd(x1, p["norm_ffn_g"], sc_f, sh_f, nm + "norm_ffn_fwd")
    u = carried("ffn_in", lambda cm: _mm_hosting(h2, p["wfi"], mode="nn", out_dtype=BF16, name=nm + "ffn_in",
                                                 cap_n=512, comm=cm))
    a = _swiglu_fwd(u, nm + "swiglu_fwd")
    f = _mm(a, p["wfo"], mode="nn", out_dtype=BF16, name=nm + "ffn_out", cap_m=1024)
    x2 = _resid_fwd(x1, f, g_f, nm + "resid_ffn")
    saved = dict(x=x, h1=h1, qkv=qkv, qkv_t=qkv_t, gf=gf, cc=cc, cr=cr, o_b_t=o_b_t, lse_b=lse_b, o=o, y=y, merged=merged,
                 mix=mix, x1=x1, h2=h2, u=u, a=a, f=f)
    return x2, saved, p


def _layer_bwd(dx2, mod, p, s, l, ride=None):
    sh_m, sc_m, g_m, sh_f, sc_f, g_f = mod
    nm = "l%d_" % l

    def _mm(a, b, *, name, **kw):
        comm = ride.comm_for(name) if ride is not None else None
        if comm is None:
            return _mm_plain(a, b, name=nm + name, **kw)
        out, got = _mm_plain(a, b, name=nm + name, comm=comm, **kw)
        ride.done(name, got)
        return out

    dg_f, df = _resid_bwd(dx2, s["f"], g_f, nm + "resid_ffn_bwd")
    da = _mm(df, p["wfo"], mode="nt", out_dtype=BF16, name="ffn_out_dx", cap_m=1024, cap_n=1408)
    d_wfo = _mm(s["a"], df, mode="tn", out_dtype=BF16, name="ffn_out_dw", cap_m=1408, cap_k=2048)
    du = _swiglu_bwd(da, s["u"], nm + "swiglu_bwd")
    dh2 = _mm(du, p["wfi"], mode="nt", out_dtype=BF16, name="ffn_in_dx", cap_m=1024)
    d_wfi = _mm(s["h2"], du, mode="tn", out_dtype=BF16, name="ffn_in_dw", cap_m=1024, cap_n=1408, cap_k=2048,
                col_quarters=True)
    dx1, dsc_f, dsh_f, dgn_f = _norm_mod_bwd(s["x1"], [dh2], dx2, p["norm_ffn_g"], sc_f, nm + "norm_ffn_bwd")
    dg_m, dmix = _resid_bwd(dx1, s["mix"], g_m, nm + "resid_mix_bwd")
    dmerged = _mm(dmix, p["wout"], mode="nt", out_dtype=BF16, name="out_proj_dx")
    d_wout = _mm(s["merged"], dmix, mode="tn", out_dtype=BF16, name="out_proj_dw", cap_m=1024, cap_k=2048)
    dy, dgates = _merge_bwd(dmerged, s["y"], s["gf"], nm + "merge_bwd")
    do = _mm(dy, p["wb"], mode="nt", out_dtype=BF16, groups=3, name="branch_dx")
    d_wb = _mm(s["o"], dy, mode="tn", out_dtype=BF16, groups=3, name="branch_dw", cap_k=2048,
               col_quarters=True)
    comms = ride.exchanges() if ride is not None else (None, None, None)
    qkv, qkv_t = s["qkv"], s["qkv_t"]
    do_t = do.T
    (dqa_t, dka_h, dva_h, _, dsink), got_a = _bandT_bwd(
        (qkv_t, 0), _heads(qkv[:, 0:512], N_HEADS), _heads(qkv[:, 512:640], A_KV_HEADS), (qkv_t, 512),
        _heads(qkv[:, 640:768], A_KV_HEADS), (do_t, 0), _heads(do[:, 0:512], N_HEADS), p["alibi"], p["sink_tab"],
        GQ=4, GK=1, P=A_PREV, kvoff=_kv_same, name=nm + "attn_a_bwd", comm=comms[0])
    (dqb_t, dkb_h, dvb_h, dck, dcq), got_b = _foxT_bwd(
        (qkv_t, 768), _heads(qkv[:, 768:1280], N_HEADS), _heads(qkv[:, 1280:1792], N_HEADS), (qkv_t, 1280),
        _heads(qkv[:, 1792:2304], N_HEADS), s["cc"], s["cr"], s["o_b_t"], (do_t, 512),
        _heads(do[:, 512:1024], N_HEADS), s["lse_b"], nm + "attn_b_bwd", comm=comms[1])
    dcum = jnp.pad((dck[:, :, 0] + dcq[:, 0, :]).T, ((0, 0), (0, LANE - N_HEADS)))
    dfb, db_forget = _fox_cum_bwd(s["gf"], p["b_forget_pad"], dcum, nm + "fox_cum_bwd")
    (dqc_t, dkc_h, dvc_h, dbias_c, _), got_c = _bandT_bwd(
        (qkv_t, 2304), _heads(qkv[:, 2304:2816], N_HEADS), _heads(qkv[:, 2816:3328], N_HEADS), (qkv_t, 2816),
        _heads(qkv[:, 3328:3840], N_HEADS), (do_t, 1024), _heads(do[:, 1024:1536], N_HEADS), p["rel_tab"],
        p["no_sink"], GQ=2, GK=2, P=C_PREV, kvoff=_kv_own, name=nm + "attn_c_bwd", comm=comms[2])
    d_rel = _rel_reduce(jnp.transpose(_unpair_table(dbias_c), (1, 0, 2)), nm + "rel_reduce")[:, :N_REL]
    dqkv = jnp.concatenate([dqa_t.T, _unheads(dka_h), _unheads(dva_h), dqb_t.T, _unheads(dkb_h), _unheads(dvb_h),
                            dqc_t.T, _unheads(dkc_h), _unheads(dvc_h)], axis=1)
    dgf = jnp.concatenate([dgates, dfb], axis=1)
    if ride is not None:
        ride.exchanged((got_a, got_b, got_c))
    dh1a = _mm(dqkv, p["wqkv"], mode="nt", out_dtype=BF16, name="proj_qkv_dx", cap_k=1024)
    dh1b = _mm(dgf, p["wgf"], mode="nt", out_dtype=BF16, name="proj_gf_dx", cap_k=640)
    d_wqkv = _mm(s["h1"], dqkv, mode="tn", out_dtype=BF16, name="proj_qkv_dw", cap_m=1024, cap_k=2048)
    d_wgf = _mm(s["h1"], dgf, mode="tn", out_dtype=BF16, name="proj_gf_dw", cap_m=1024, cap_n=640, cap_k=2048)
    dx, dsc_m, dsh_m, dgn_m = _norm_mod_bwd(s["x"], [dh1a, dh1b], dx1, p["norm_mix_g"], sc_m, nm + "norm_mix_bwd")
    d_mod = jnp.concatenate([dsh_m, dsc_m, dg_m, dsh_f, dsc_f, dg_f], axis=1)[0]
    grads = dict(w_in=_unpack_w_in(d_wqkv, d_wgf), w_branch=d_wb, w_out=d_wout.reshape(4, -1, D_MODEL),
                 w_ffn_in=d_wfi, w_ffn_out=d_wfo.reshape(4, -1, D_MODEL),
                 norm_mix_g=dgn_m[0], norm_ffn_g=dgn_f[0], b_forget=db_forget[0, :N_HEADS],
                 sinks=dsink[:, 0, 0], rel_bias=d_rel, d_mod=d_mod)
    return dx, grads


def kernel(x, c, norm_mix_g, norm_ffn_g, w_ada, b_ada, w_in, b_forget, sinks, rel_bias, w_branch, w_out, w_ffn_in, w_ffn_out, final_norm_g, loss_target, m_norm_mix_g, m_norm_ffn_g, m_w_ada, m_b_ada, m_w_in, m_b_forget, m_sinks, m_rel_bias, m_w_branch, m_w_out, m_w_ffn_in, m_w_ffn_out, m_final_norm_g, v_norm_mix_g, v_norm_ffn_g, v_w_ada, v_b_ada, v_w_in, v_b_forget, v_sinks, v_rel_bias, v_w_branch, v_w_out, v_w_ffn_in, v_w_ffn_out, v_final_norm_g):
    xi, yi, ci = _coords()
    chip = 2 * xi + yi
    dev = 2 * chip + ci
    xs = x[0]
    S = xs.shape[0]
    n_ada = w_ada.shape[2]

    big_names = ("w_in", "w_branch", "w_out", "w_ffn_in", "w_ffn_out")
    big_w = dict(w_in=w_in, w_branch=w_branch, w_out=w_out, w_ffn_in=w_ffn_in, w_ffn_out=w_ffn_out)
    big_m = dict(w_in=m_w_in, w_branch=m_w_branch, w_out=m_w_out, w_ffn_in=m_w_ffn_in, w_ffn_out=m_w_ffn_out)
    big_v = dict(w_in=v_w_in, w_branch=v_w_branch, w_out=v_w_out, w_ffn_in=v_w_ffn_in, w_ffn_out=v_w_ffn_out)
    flat2 = lambda a: a.reshape(-1, a.shape[-1])
    shards = [[flat2(big_w[n][l]).astype(BF16) for n in big_names] for l in range(DEPTH)]
    gw = [[None] * (len(big_names) + 2) for _ in range(DEPTH)]
    for l in range(DEPTH):
        shards[l] += [shards[l][0][:D_MODEL // 2], shards[l][0][D_MODEL // 2:]]
    gw[0][0] = _RowHalfGather([shards[0][0]]).run("weights_gather_w_in_l0")[0]
    host_g = ((1, 2, 4), (0,), (3,))

    class WeightRide:
        def __init__(self, l, plan):
            self.l, self.plan = l, plan

        def comm_for(self, name):
            if name not in self.plan:
                return None
            lay, idx = self.plan[name]
            return _RowHalfGather([shards[lay][i] for i in idx])

        def done(self, name, got):
            lay, idx = self.plan[name]
            for i, r in zip(idx, got):
                gw[lay][i] = r

        def late_weights(self):
            g = gw[self.l]
            return dict(wb=jnp.transpose(g[1], (1, 0, 2)).reshape(3 * BRANCH_W, D_MODEL),
                        wout=g[2].reshape(D_MODEL, D_MODEL),
                        wfi=jnp.transpose(g[3], (1, 0, 2)).reshape(D_MODEL, 2 * FFN_H),
                        wfo=g[4].reshape(FFN_H, D_MODEL))

    weight_plan = [
        {"proj_qkv": (0, (1,)), "proj_gf": (0, (2,)), "attn_a": (0, (4,)), "attn_b": (0, (3,)), "attn_c": (1, (5,)),
         "ffn_in": (1, (6,))},
        {"attn_a": (1, (1, 2)), "attn_b": (1, (3,)), "attn_c": (1, (4,))}]


    c_all = _all_gather8(c.reshape(8, LANE), "gather_c").reshape(8, D_MODEL)
    b_sh = lax.dynamic_slice_in_dim(b_ada, chip * n_ada, n_ada, axis=1)[:, None, :]
    mod_sh = _ada_fwd(_pad_rows(c_all, 16), w_ada, b_sh, "ada_fwd")[:, :8, :]
    mod_all = _all_gather8(mod_sh.reshape(-1, LANE), "gather_mod").reshape(8, DEPTH, 8, n_ada)
    mod_mine = lax.dynamic_index_in_dim(mod_all[0::2], dev, axis=2, keepdims=False)
    mod = mod_mine.transpose(1, 0, 2).reshape(DEPTH, 6, D_MODEL)

    alibi = _pair_table(_alibi_table())
    no_sink = jnp.full((N_HEADS, 8, LANE), NEG_INF, F32)
    def make_params(l):
        if gw[l][0] is None:
            gw[l][0] = jnp.concatenate([gw[l][5], gw[l][6]], axis=1)
        wqkv, wgf = _pack_w_in(gw[l][0])
        rel_tab = _rel_expand(jnp.pad(rel_bias[l], ((0, 0), (0, N_REL_PAD - N_REL))), "l%d_rel_expand" % l)
        return dict(
            wqkv=wqkv, wgf=wgf, norm_mix_g=norm_mix_g[l][None], norm_ffn_g=norm_ffn_g[l][None],
            b_forget_pad=jnp.pad(b_forget[l], (0, LANE - N_HEADS))[None],
            sink_tab=jnp.broadcast_to(sinks[l][:, None, None], (N_HEADS, 8, LANE)),
            no_sink=no_sink, alibi=alibi, rel_tab=_pair_table(jnp.transpose(rel_tab, (1, 0, 2))))

    mods = [[mod[l, k][None] for k in range(6)] for l in range(DEPTH)]
    params, saved = [None] * DEPTH, [None] * DEPTH
    h = xs
    for l in range(DEPTH):
        h, saved[l], params[l] = _layer_fwd(h, mods[l], make_params(l), l, WeightRide(l, weight_plan[l]))
    loss_dev, dh, d_final = _final_loss(h, final_norm_g[None], loss_target[0], "final_loss")
    grads = [None] * DEPTH
    dh, grads[1] = _layer_bwd(dh, mods[1], params[1], saved[1], 1)

    class Layer1Ride:
        sends = {"ffn_out_dx": (4,), "ffn_in_dx": (3, 1, 2), "ffn_in_dw": (0,)}
        hands = {"proj_qkv_dx": (0,), "proj_gf_dx": (3,), "proj_gf_dw": (4, 1, 2)}

        def __init__(self, g):
            self.g, self.t = g, [None] * len(g)
            self.parts, self.final = [None] * len(g), [None] * len(g)

        def comm_for(self, name):
            if name in self.sends:
                return _SiblingSend([self.g[i] for i in self.sends[name]], 0)
            if name in self.hands:
                return _Handoff([self.parts[i] for i in self.hands[name]], 1, (0, 1, 2, 3))
            return None

        def done(self, name, got):
            idx, dst = (self.sends[name], self.t) if name in self.sends else (self.hands[name], self.final)
            for i, r in zip(idx, got):
                dst[i] = r

        def exchanges(self):
            sums = [_add_cast_on(a, b, 1, "grads_chip_sum_l1_" + n) for n, a, b in zip(big_names, self.g, self.t)]
            return tuple(_OwnerReduce([sums[i] for i in idx], 1) for idx in host_g)

        def exchanged(self, got):
            for res, idx in zip(got, host_g):
                for r, i in zip(res, idx):
                    self.parts[i] = r

    ride = Layer1Ride([grads[1][n] for n in big_names])
    dh, grads[0] = _layer_bwd(dh, mods[0], params[0], saved[0], 0, ride)
    grad_x = dh[None]
    loss = lax.psum(loss_dev[0, 0], ("x", "y", "c"))
    parts1 = ride.final
    g0 = [grads[0][n] for n in big_names]
    t0 = _sibling_swap_rows(g0, "grads_swap_l0")
    sums0 = [_add_cast_rows(a, b, "grads_chip_sum_l0_" + n) for n, a, b in zip(big_names, g0, t0)]
    parts0 = [None] + list(_RowHalfReduce(sums0[1:]).run("grads_reduce_l0"))

    small_names = ("norm_mix_g", "norm_ffn_g", "b_ada", "b_forget", "sinks", "rel_bias", "final_norm_g")
    small_w = dict(norm_mix_g=norm_mix_g, norm_ffn_g=norm_ffn_g, b_ada=b_ada, b_forget=b_forget, sinks=sinks,
                   rel_bias=rel_bias, final_norm_g=final_norm_g)
    small_m = dict(norm_mix_g=m_norm_mix_g, norm_ffn_g=m_norm_ffn_g, b_ada=m_b_ada, b_forget=m_b_forget,
                   sinks=m_sinks, rel_bias=m_rel_bias, final_norm_g=m_final_norm_g)
    small_v = dict(norm_mix_g=v_norm_mix_g, norm_ffn_g=v_norm_ffn_g, b_ada=v_b_ada, b_forget=v_b_forget,
                   sinks=v_sinks, rel_bias=v_rel_bias, final_norm_g=v_final_norm_g)
    small_g = dict(
        norm_mix_g=jnp.stack([grads[l]["norm_mix_g"] for l in range(DEPTH)]),
        norm_ffn_g=jnp.stack([grads[l]["norm_ffn_g"] for l in range(DEPTH)]),
        b_ada=jnp.stack([grads[l]["d_mod"] for l in range(DEPTH)]),
        b_forget=jnp.stack([grads[l]["b_forget"] for l in range(DEPTH)]),
        sinks=jnp.stack([grads[l]["sinks"] for l in range(DEPTH)]),
        rel_bias=jnp.stack([grads[l]["rel_bias"] for l in range(DEPTH)]),
        final_norm_g=d_final[0])
    shapes = [small_w[n].shape for n in small_names]
    g_all = _all_gather8(_small_pack([small_g[n] for n in small_names]), "gather_small_grads")
    res = _adamw(_small_pack([small_w[n] for n in small_names])[None],
                    _small_pack([small_m[n] for n in small_names])[None],
                    _small_pack([small_v[n] for n in small_names])[None], g_all, "adamw_small")
    small_out = {n: [] for n in small_names}
    for r in res:
        for n, a in zip(small_names, _small_unpack(r[0], shapes)):
            small_out[n].append(a)
    off_b = sum(int(np.prod(s)) for s in shapes[:2])
    n_mod = DEPTH * 6 * D_MODEL
    dmod_all = g_all.reshape(8, -1)[:, off_b:off_b + n_mod].reshape(8, DEPTH, 6 * D_MODEL)
    dmod_sh = lax.dynamic_slice_in_dim(dmod_all, chip * n_ada, n_ada, axis=2).transpose(1, 0, 2)
    g_ada, got = _ada_bwd(c_all.T, dmod_sh, "ada_bwd", comm=_RowHalfReduce(sums0[:1]))
    parts0[0] = got[0]
    ada_out = _adamw(w_ada, m_w_ada, v_w_ada, flat2(g_ada)[None], "adamw_w_ada")

    big_out = {}
    as3 = lambda a: a.reshape(a.shape[0], -1, a.shape[-1])
    for n, p0, p1 in zip(big_names, parts0, parts1):
        res = _adamw(as3(big_w[n]), as3(big_m[n]), as3(big_v[n]), [p0, p1], "adamw_" + n)
        big_out[n] = [r.reshape(big_w[n].shape) for r in res]

    order = ("norm_mix_g", "norm_ffn_g", "w_ada", "b_ada", "w_in", "b_forget", "sinks", "rel_bias", "w_branch",
             "w_out", "w_ffn_in", "w_ffn_out", "final_norm_g")

    def pick(n, k):
        if n == "w_ada":
            return ada_out[k]
        if n in big_out:
            return big_out[n][k]
        return small_out[n][k]

    outs = [loss, grad_x]
    for k in range(4):
        outs += [pick(n, k) for n in order]
    return tuple(outs)
```

```python
import numpy as np
import jax
import jax.numpy as jnp
from jax import lax
from jax.experimental import pallas as pl
from jax.experimental.pallas import tpu as pltpu

F32 = jnp.float32
BF16 = jnp.bfloat16
SDS = jax.ShapeDtypeStruct

D_MODEL = 1024
DEPTH = 2
CHUNK = 64
HEAD_DIM = 64
EPS = 1e-6
NEG_INF = -1e30
N_HEADS = 8
A_KV_HEADS = 2
A_PREV = 2
C_PREV = 8
REL_CLIP = 128
N_REL = 2 * REL_CLIP + 1
N_REL_PAD = 384
BRANCH_W = 512
FFN_H = 2816
FOX_BQ = 512
FOX_BK = 512
GF_COLS = 3200
N_IN_COLS = 6920
LANE = 128
VMEM_LIMIT = 48 * 1024 * 1024

ADAM_LR = 0.001
ADAM_B1 = 0.9
ADAM_B2 = 0.999
ADAM_EPS = 1e-08
ADAM_WD = 0.01
ADAM_STEP = 10

MESH = pl.DeviceIdType.MESH
ANY = pl.BlockSpec(memory_space=pl.ANY)
VMEM_SPEC = pl.BlockSpec(memory_space=pltpu.VMEM)


def _cparams(sem=None):
    return pltpu.CompilerParams(dimension_semantics=sem, vmem_limit_bytes=VMEM_LIMIT)


def _blk(n, cap):
    if n <= cap:
        return n
    best = None
    for m in range(LANE, cap + 1, LANE):
        if n % m == 0:
            best = m
    assert best is not None, (n, cap)
    return best


def _sigmoid(x):
    return 1.0 / (1.0 + jnp.exp(-x))


def _mm(a, b, *, mode, out_dtype, name, groups=1, cap_m=2048, cap_n=1024, cap_k=1408, col_quarters=False,
        comm=None):
    G = groups
    assert not col_quarters or mode == "tn"
    if mode == "nn":
        M, K, N = a.shape[0], a.shape[1] // G, b.shape[1]
        assert b.shape[0] == G * K
    elif mode == "nt":
        M, K, N = a.shape[0], a.shape[1] // G, b.shape[0] // G
        assert b.shape[1] == K
    else:
        K, M, N = a.shape[0], a.shape[1] // G, b.shape[1] // G
        assert b.shape[0] == K
    bm, bn, bk = _blk(M, cap_m), _blk(N // 4 if col_quarters else N, cap_n), _blk(K, cap_k)
    nm, nn, nk = M // bm, N // bn, K // bk
    if mode == "nn":
        a_spec = pl.BlockSpec((bm, bk), lambda g, i, j, k: (i, g * nk + k))
        b_spec = pl.BlockSpec((bk, bn), lambda g, i, j, k: (g * nk + k, j))
        o_spec = pl.BlockSpec((bm, bn), lambda g, i, j, k: (i, g * nn + j))
        dims = (((1,), (0,)), ((), ()))
        out_shape = (M, G * N)
    elif mode == "nt":
        a_spec = pl.BlockSpec((bm, bk), lambda g, i, j, k: (i, g * nk + k))
        b_spec = pl.BlockSpec((bn, bk), lambda g, i, j, k: (g * nn + j, k))
        o_spec = pl.BlockSpec((bm, bn), lambda g, i, j, k: (i, g * nn + j))
        dims = (((1,), (1,)), ((), ()))
        out_shape = (M, G * N)
    else:
        a_spec = pl.BlockSpec((bk, bm), lambda g, i, j, k: (k, g * nm + i))
        b_spec = pl.BlockSpec((bk, bn), lambda g, i, j, k: (k, g * nn + j))
        dims = (((0,), (0,)), ((), ()))
        if col_quarters:
            nq = nn // 4
            o_spec = pl.BlockSpec((1, bm, bn), lambda g, i, j, k: (j // nq, g * nm + i, j % nq))
            out_shape = (4, G * M, N // 4)
        else:
            o_spec = pl.BlockSpec((bm, bn), lambda g, i, j, k: (g * nm + i, j))
            out_shape = (G * M, N)

    def product(a_ref, b_ref):
        return lax.dot_general(a_ref[...].astype(BF16), b_ref[...].astype(BF16), dims, preferred_element_type=F32)

    def body_one(a_ref, b_ref, o_ref):
        o_ref[...] = product(a_ref, b_ref).astype(o_ref.dtype).reshape(o_ref.shape)

    def body_acc(a_ref, b_ref, o_ref, acc_ref):
        k = pl.program_id(3)

        @pl.when(k == 0)
        def _():
            acc_ref[...] = jnp.zeros_like(acc_ref)

        acc_ref[...] += product(a_ref, b_ref)

        @pl.when(k == nk - 1)
        def _():
            o_ref[...] = acc_ref[...].astype(o_ref.dtype).reshape(o_ref.shape)

    res, got = _call_hosting(
        body_one if nk == 1 else body_acc, comm=comm, grid=(G, nm, nn, nk), in_specs=[a_spec, b_spec],
        out_specs=[o_spec], out_shape=[SDS(out_shape, out_dtype)],
        scratch_shapes=[] if nk == 1 else [pltpu.VMEM((bm, bn), F32)], name=name, args=(a, b),
        semantics=("parallel", "parallel", "parallel", "arbitrary"))
    return res[0] if comm is None else (res[0], got)


def _rows(tm, n, col=0):
    return pl.BlockSpec((tm, n), lambda i: (i, col))


def _vec(n):
    return pl.BlockSpec((1, n), lambda i: (0, 0))


def _tm(S, cap=512):
    return min(S, cap)


def _norm_mod_fwd(x, g, sc, sh, name):
    S, Dm = x.shape
    tm = _tm(S, 1024)

    def body(x_ref, g_ref, sc_ref, sh_ref, h_ref):
        xv = x_ref[...]
        r = lax.rsqrt(jnp.mean(xv * xv, axis=-1, keepdims=True) + EPS)
        h_ref[...] = ((xv * r) * g_ref[...] * (1.0 + sc_ref[...]) + sh_ref[...]).astype(h_ref.dtype)

    return pl.pallas_call(
        body, grid=(S // tm,), in_specs=[_rows(tm, Dm), _vec(Dm), _vec(Dm), _vec(Dm)],
        out_specs=_rows(tm, Dm), out_shape=SDS((S, Dm), BF16),
        compiler_params=_cparams(("parallel",)), name=name)(x, g, sc, sh)


def _norm_mod_bwd(x, dh_list, dres, g, sc, name):
    S, Dm = x.shape
    tm = _tm(S, 1024)
    nh = len(dh_list)

    def body(*refs):
        x_ref = refs[0]
        dh_refs = refs[1:1 + nh]
        dres_ref, g_ref, sc_ref, dx_ref, dsc_ref, dsh_ref, dg_ref = refs[1 + nh:]
        i = pl.program_id(0)

        @pl.when(i == 0)
        def _():
            dsc_ref[...] = jnp.zeros_like(dsc_ref)
            dsh_ref[...] = jnp.zeros_like(dsh_ref)
            dg_ref[...] = jnp.zeros_like(dg_ref)

        xv = x_ref[...]
        dh = dh_refs[0][...].astype(F32)
        for r_ in dh_refs[1:]:
            dh = dh + r_[...].astype(F32)
        gv = g_ref[...]
        r = lax.rsqrt(jnp.mean(xv * xv, axis=-1, keepdims=True) + EPS)
        xn = xv * r
        xg = xn * gv
        dsh_ref[...] += jnp.sum(dh, axis=0, keepdims=True)
        dsc_ref[...] += jnp.sum(dh * xg, axis=0, keepdims=True)
        dxg = dh * (1.0 + sc_ref[...])
        dg_ref[...] += jnp.sum(dxg * xn, axis=0, keepdims=True)
        dxn = dxg * gv
        dx_ref[...] = dres_ref[...] + r * (dxn - xn * jnp.mean(dxn * xn, axis=-1, keepdims=True))

    return pl.pallas_call(
        body, grid=(S // tm,),
        in_specs=[_rows(tm, Dm)] * (2 + nh) + [_vec(Dm), _vec(Dm)],
        out_specs=[_rows(tm, Dm), _vec(Dm), _vec(Dm), _vec(Dm)],
        out_shape=[SDS((S, Dm), F32), SDS((1, Dm), F32), SDS((1, Dm), F32), SDS((1, Dm), F32)],
        compiler_params=_cparams(("arbitrary",)), name=name)(x, *dh_list, dres, g, sc)


def _resid_fwd(x, val, g, name):
    S, Dm = x.shape
    tm = _tm(S, 1024)

    def body(x_ref, v_ref, g_ref, o_ref):
        o_ref[...] = x_ref[...] + g_ref[...] * v_ref[...].astype(F32)

    return pl.pallas_call(
        body, grid=(S // tm,), in_specs=[_rows(tm, Dm), _rows(tm, Dm), _vec(Dm)],
        out_specs=_rows(tm, Dm), out_shape=SDS((S, Dm), F32),
        compiler_params=_cparams(("parallel",)), name=name)(x, val, g)


def _resid_bwd(dx, val, g, name):
    S, Dm = dx.shape
    tm = _tm(S, 1024)

    def body(dx_ref, v_ref, g_ref, dg_ref, dv_ref):
        @pl.when(pl.program_id(0) == 0)
        def _():
            dg_ref[...] = jnp.zeros_like(dg_ref)

        dxv = dx_ref[...]
        dg_ref[...] += jnp.sum(dxv * v_ref[...].astype(F32), axis=0, keepdims=True)
        dv_ref[...] = (dxv * g_ref[...]).astype(dv_ref.dtype)

    return pl.pallas_call(
        body, grid=(S // tm,), in_specs=[_rows(tm, Dm), _rows(tm, Dm), _vec(Dm)],
        out_specs=[_vec(Dm), _rows(tm, Dm)], out_shape=[SDS((1, Dm), F32), SDS((S, Dm), BF16)],
        compiler_params=_cparams(("arbitrary",)), name=name)(dx, val, g)


def _merge_fwd(y, gf, name):
    S = y.shape[0]
    tm = _tm(S)
    W = 3 * D_MODEL

    def body(y_ref, g_ref, o_ref):
        acc = None
        for k in range(3):
            sl = slice(k * D_MODEL, (k + 1) * D_MODEL)
            t = _sigmoid(g_ref[:, sl]) * y_ref[:, sl].astype(F32)
            acc = t if acc is None else acc + t
        o_ref[...] = acc.astype(o_ref.dtype)

    return pl.pallas_call(
        body, grid=(S // tm,), in_specs=[_rows(tm, W), _rows(tm, W)],
        out_specs=_rows(tm, D_MODEL), out_shape=SDS((S, D_MODEL), BF16),
        compiler_params=_cparams(("parallel",)), name=name)(y, gf)


def _merge_bwd(dm, y, gf, name):
    S = y.shape[0]
    tm = _tm(S)
    W = 3 * D_MODEL

    def body(dm_ref, y_ref, g_ref, dy_ref, dg_ref):
        dmv = dm_ref[...].astype(F32)
        for k in range(3):
            sl = slice(k * D_MODEL, (k + 1) * D_MODEL)
            sg = _sigmoid(g_ref[:, sl])
            dy_ref[:, sl] = (dmv * sg).astype(dy_ref.dtype)
            dg_ref[:, sl] = (dmv * y_ref[:, sl].astype(F32) * (sg * (1.0 - sg))).astype(dg_ref.dtype)

    return pl.pallas_call(
        body, grid=(S // tm,), in_specs=[_rows(tm, D_MODEL), _rows(tm, W), _rows(tm, W)],
        out_specs=[_rows(tm, W), _rows(tm, W)], out_shape=[SDS((S, W), BF16), SDS((S, W), BF16)],
        compiler_params=_cparams(("parallel",)), name=name)(dm, y, gf)


def _swiglu_fwd(u, name):
    S = u.shape[0]
    tm = _tm(S)

    def body(g_ref, u_ref, a_ref):
        gv = g_ref[...].astype(F32)
        a_ref[...] = (gv * _sigmoid(gv) * u_ref[...].astype(F32)).astype(a_ref.dtype)

    return pl.pallas_call(
        body, grid=(S // tm,), in_specs=[_rows(tm, FFN_H, 0), _rows(tm, FFN_H, 1)],
        out_specs=_rows(tm, FFN_H), out_shape=SDS((S, FFN_H), BF16),
        compiler_params=_cparams(("parallel",)), name=name)(u, u)


def _swiglu_bwd(da, u, name):
    S = u.shape[0]
    tm = _tm(S)

    def body(da_ref, g_ref, u_ref, du_ref):
        dav = da_ref[...].astype(F32)
        gv = g_ref[...].astype(F32)
        sg = _sigmoid(gv)
        du_ref[:, 0:FFN_H] = (dav * u_ref[...].astype(F32) * (sg * (1.0 + gv * (1.0 - sg)))).astype(du_ref.dtype)
        du_ref[:, FFN_H:2 * FFN_H] = (dav * (gv * sg)).astype(du_ref.dtype)

    return pl.pallas_call(
        body, grid=(S // tm,), in_specs=[_rows(tm, FFN_H), _rows(tm, FFN_H, 0), _rows(tm, FFN_H, 1)],
        out_specs=_rows(tm, 2 * FFN_H), out_shape=SDS((S, 2 * FFN_H), BF16),
        compiler_params=_cparams(("parallel",)), name=name)(da, u, u)


def _final_loss(x, g, target, name):
    S, Dm = x.shape
    tm = _tm(S, 1024)

    def body(x_ref, g_ref, t_ref, loss_ref, dx_ref, dg_ref):
        @pl.when(pl.program_id(0) == 0)
        def _():
            loss_ref[...] = jnp.zeros_like(loss_ref)
            dg_ref[...] = jnp.zeros_like(dg_ref)

        xv = x_ref[...]
        gv = g_ref[...]
        r = lax.rsqrt(jnp.mean(xv * xv, axis=-1, keepdims=True) + EPS)
        xn = xv * r
        err = xn * gv - t_ref[...]
        row = jnp.mean(err * err, axis=-1, keepdims=True)
        loss_ref[...] += 0.5 * jnp.sum(row, axis=0, keepdims=True)
        dy = err * (1.0 / Dm)
        dg_ref[...] += jnp.sum(dy * xn, axis=0, keepdims=True)
        dxn = dy * gv
        dx_ref[...] = r * (dxn - xn * jnp.mean(dxn * xn, axis=-1, keepdims=True))

    return pl.pallas_call(
        body, grid=(S // tm,), in_specs=[_rows(tm, Dm), _vec(Dm), _rows(tm, Dm)],
        out_specs=[pl.BlockSpec((1, 1), lambda i: (0, 0)), _rows(tm, Dm), _vec(Dm)],
        out_shape=[SDS((1, 1), F32), SDS((S, Dm), F32), SDS((1, Dm), F32)],
        compiler_params=_cparams(("arbitrary",)), name=name)(x, g, target)


PAIR = 2 * CHUNK


def _bandT_softmax(kg, qTg, bias, sink, valid):
    s = jnp.dot(kg, qTg, preferred_element_type=F32)
    s = jnp.where(valid, s + bias, NEG_INF)
    m = jnp.maximum(jnp.max(s, axis=0, keepdims=True), sink)
    e = jnp.exp(s - m)
    es = jnp.exp(sink - m)
    inv = 1.0 / (jnp.sum(e, axis=0, keepdims=True) + es)
    return e * inv, es * inv


def _pad_copy_rows(dst, src, pad, S):
    dst[:, 0:pad, :] = jnp.zeros((dst.shape[0], pad, dst.shape[2]), dst.dtype)
    dst[:, pad:pad + S, :] = src[...]


def _pad_copy_lanes(dst, src, pad, S):
    dst[:, 0:pad] = jnp.zeros((dst.shape[0], pad), dst.dtype)
    dst[:, pad:pad + S] = src[...]


def _fm(arg):
    return arg if isinstance(arg, tuple) else (arg, 0)


def _fm_spec(rows, S, row0):
    off, rem = divmod(row0, rows)
    assert rem == 0
    return pl.BlockSpec((rows, S), lambda i: (off + i, 0))


def _bandT_fwd(qT, k_h, vT, bias, sink, *, GQ, GK, P, kvoff, name, comm=None):
    (qT, q0), (vT, v0) = _fm(qT), _fm(vT)
    S = qT.shape[1]
    ng = bias.shape[0] // GQ
    BU = (P + 2) * CHUNK
    pad = P * CHUNK
    npair = S // PAIR

    def body(qT_ref, k_ref, vT_ref, b_ref, s_ref, oT_ref, kp, vTp):
        _pad_copy_rows(kp, k_ref, pad, S)
        _pad_copy_lanes(vTp, vT_ref, pad, S)
        rowi = lax.broadcasted_iota(jnp.int32, (BU, PAIR), 0)

        def step(n2, carry):
            r = pl.multiple_of(n2 * PAIR, PAIR)
            valid = rowi >= (P - 2 * n2) * CHUNK
            for g in range(GQ):
                kv = kvoff(g)
                hs = slice(g * HEAD_DIM, (g + 1) * HEAD_DIM)
                kvs = slice(kv * HEAD_DIM, (kv + 1) * HEAD_DIM)
                qTg = qT_ref[hs, pl.ds(r, PAIR)] * 0.125
                p, _ = _bandT_softmax(kp[kv, pl.ds(r, BU), :], qTg, b_ref[g], s_ref[g, 0:1, :], valid)
                oTg = jnp.dot(vTp[kvs, pl.ds(r, BU)], p.astype(BF16), preferred_element_type=F32)
                oT_ref[hs, pl.ds(r, PAIR)] = oTg.astype(oT_ref.dtype)
            return carry

        lax.fori_loop(0, npair, step, 0, unroll=min(2, npair))

    res, got = _call_hosting(
        body, comm=comm, grid=(ng,),
        in_specs=[_fm_spec(GQ * HEAD_DIM, S, q0),
                  pl.BlockSpec((GK, S, HEAD_DIM), lambda i: (i, 0, 0)),
                  _fm_spec(GK * HEAD_DIM, S, v0),
                  pl.BlockSpec((GQ, BU, PAIR), lambda i: (i, 0, 0)),
                  pl.BlockSpec((GQ, 8, LANE), lambda i: (i, 0, 0))],
        out_specs=[pl.BlockSpec((GQ * HEAD_DIM, S), lambda i: (i, 0))],
        out_shape=[SDS((ng * GQ * HEAD_DIM, S), BF16)],
        scratch_shapes=[pltpu.VMEM((GK, S + pad, HEAD_DIM), BF16), pltpu.VMEM((GK * HEAD_DIM, S + pad), BF16)],
        name=name, args=(qT, k_h, vT, bias, sink))
    return res[0], got


def _bandT_bwd(qT, q_h, k_h, kT, v_h, doT, do_h, bias, sink, *, GQ, GK, P, kvoff, name, comm=None):
    (qT, q0), (kT, k0), (doT, d0) = _fm(qT), _fm(kT), _fm(doT)
    S = qT.shape[1]
    ng = bias.shape[0] // GQ
    BU = (P + 2) * CHUNK
    pad = P * CHUNK
    npair = S // PAIR

    def body(qT_ref, q_ref, k_ref, kT_ref, v_ref, doT_ref, do_ref, b_ref, s_ref,
             dqT_ref, dk_ref, dv_ref, db_ref, dsk_ref, kp, kTp, vp, dkp, dvp):
        _pad_copy_rows(kp, k_ref, pad, S)
        _pad_copy_rows(vp, v_ref, pad, S)
        _pad_copy_lanes(kTp, kT_ref, pad, S)
        dkp[...] = jnp.zeros_like(dkp)
        dvp[...] = jnp.zeros_like(dvp)
        db_ref[...] = jnp.zeros_like(db_ref)
        rowi = lax.broadcasted_iota(jnp.int32, (BU, PAIR), 0)

        def step(n2, dsink):
            r = pl.multiple_of(n2 * PAIR, PAIR)
            valid = rowi >= (P - 2 * n2) * CHUNK
            new = []
            for g in range(GQ):
                kv = kvoff(g)
                hs = slice(g * HEAD_DIM, (g + 1) * HEAD_DIM)
                kvs = slice(kv * HEAD_DIM, (kv + 1) * HEAD_DIM)
                qTg = qT_ref[hs, pl.ds(r, PAIR)] * 0.125
                p, ps = _bandT_softmax(kp[kv, pl.ds(r, BU), :], qTg, b_ref[g], s_ref[g, 0:1, :], valid)
                dp = jnp.dot(vp[kv, pl.ds(r, BU), :], doT_ref[hs, pl.ds(r, PAIR)], preferred_element_type=F32)
                delta = jnp.sum(p * dp, axis=0, keepdims=True)
                ds = p * (dp - delta)
                new.append(dsink[g] - ps * delta)
                db_ref[g] += ds
                dsb = ds.astype(BF16)
                dq = jnp.dot(kTp[kvs, pl.ds(r, BU)], dsb, preferred_element_type=F32) * 0.125
                dqT_ref[hs, pl.ds(r, PAIR)] = dq.astype(dqT_ref.dtype)
                dkp[kv, pl.ds(r, BU), :] += jnp.dot(dsb, q_ref[g, pl.ds(r, PAIR), :] * 0.125,
                                                    preferred_element_type=F32)
                dvp[kv, pl.ds(r, BU), :] += jnp.dot(p.astype(BF16), do_ref[g, pl.ds(r, PAIR), :],
                                                    preferred_element_type=F32)
            return tuple(new)

        dsink = lax.fori_loop(0, npair, step, tuple(jnp.zeros((1, PAIR), F32) for _ in range(GQ)))
        for g in range(GQ):
            dsk_ref[g] = jnp.broadcast_to(jnp.sum(dsink[g], axis=1, keepdims=True), (8, LANE))
        dk_ref[...] = dkp[:, pad:pad + S, :].astype(dk_ref.dtype)
        dv_ref[...] = dvp[:, pad:pad + S, :].astype(dv_ref.dtype)

    qTs = pl.BlockSpec((GQ * HEAD_DIM, S), lambda i: (i, 0))
    qhs = pl.BlockSpec((GQ, S, HEAD_DIM), lambda i: (i, 0, 0))
    khs = pl.BlockSpec((GK, S, HEAD_DIM), lambda i: (i, 0, 0))
    bs = pl.BlockSpec((GQ, BU, PAIR), lambda i: (i, 0, 0))
    ss = pl.BlockSpec((GQ, 8, LANE), lambda i: (i, 0, 0))
    nkv = ng * GK
    return _call_hosting(
        body, comm=comm, grid=(ng,),
        in_specs=[_fm_spec(GQ * HEAD_DIM, S, q0), qhs, khs, _fm_spec(GK * HEAD_DIM, S, k0), khs,
                  _fm_spec(GQ * HEAD_DIM, S, d0), qhs, bs, ss],
        out_specs=[qTs, khs, khs, bs, ss],
        out_shape=[SDS((ng * GQ * HEAD_DIM, S), BF16), SDS((nkv, S, HEAD_DIM), BF16), SDS((nkv, S, HEAD_DIM), BF16),
                   SDS((ng * GQ, BU, PAIR), F32), SDS((ng * GQ, 8, LANE), F32)],
        scratch_shapes=[pltpu.VMEM((GK, S + pad, HEAD_DIM), BF16), pltpu.VMEM((GK * HEAD_DIM, S + pad), BF16),
                        pltpu.VMEM((GK, S + pad, HEAD_DIM), BF16),
                        pltpu.VMEM((GK, S + pad, HEAD_DIM), F32), pltpu.VMEM((GK, S + pad, HEAD_DIM), F32)],
        name=name, args=(qT, q_h, k_h, kT, v_h, doT, do_h, bias, sink))


def _pair_table(tab):
    t = jnp.transpose(tab, (0, 2, 1))
    lo = jnp.pad(t, ((0, 0), (0, CHUNK), (0, 0)), constant_values=NEG_INF)
    hi = jnp.pad(t, ((0, 0), (CHUNK, 0), (0, 0)), constant_values=NEG_INF)
    return jnp.concatenate([lo, hi], axis=2)


def _unpair_table(d):
    band = d.shape[1] - CHUNK
    return jnp.transpose(d[:, 0:band, 0:CHUNK] + d[:, CHUNK:CHUNK + band, CHUNK:PAIR], (0, 2, 1))


def _heads(a, n):
    return jnp.transpose(a.reshape(a.shape[0], n, HEAD_DIM), (1, 0, 2))


def _unheads(a):
    return jnp.transpose(a, (1, 0, 2)).reshape(a.shape[1], a.shape[0] * HEAD_DIM)


def _foxT_logits(kj, qTg, cq, ck, r, c, rowi, coli):
    s = jnp.dot(kj, qTg, preferred_element_type=F32)
    s = s + cq - ck
    return jnp.where(c + rowi <= r + coli, s, NEG_INF)


def _foxT_fwd(qT, k_h, vT, ck, cq, name, comm=None):
    (qT, q0), (vT, v0) = _fm(qT), _fm(vT)
    S = qT.shape[1]
    npair = k_h.shape[0] // 2
    BQ, BK = min(FOX_BQ, S), min(FOX_BK, S)
    nq = S // BQ
    heads = [slice(g * HEAD_DIM, (g + 1) * HEAD_DIM) for g in range(2)]

    def body(qT_ref, k_ref, vT_ref, ck_ref, cq_ref, oT_ref, lse_ref):
        rowi = lax.broadcasted_iota(jnp.int32, (BK, BQ), 0)
        coli = lax.broadcasted_iota(jnp.int32, (BK, BQ), 1)

        def qstep(i, carry):
            r = pl.multiple_of(i * BQ, BQ)
            qs = [qT_ref[hs, pl.ds(r, BQ)] * 0.125 for hs in heads]
            cqs = [cq_ref[g, :, pl.ds(r, BQ)] for g in range(2)]

            def kstep(j, st):
                c = pl.multiple_of(j * BK, BK)
                new = []
                for g, hs in enumerate(heads):
                    m, l, acc = st[g]
                    s = _foxT_logits(k_ref[g, pl.ds(c, BK), :], qs[g], cqs[g], ck_ref[g, pl.ds(c, BK), :],
                                     r, c, rowi, coli)
                    mn = jnp.maximum(m, jnp.max(s, axis=0, keepdims=True))
                    al = jnp.exp(m - mn)
                    e = jnp.exp(s - mn)
                    l = al * l + jnp.sum(e, axis=0, keepdims=True)
                    acc = al * acc + jnp.dot(vT_ref[hs, pl.ds(c, BK)], e.astype(BF16), preferred_element_type=F32)
                    new.append((mn, l, acc))
                return tuple(new)

            init = (jnp.full((1, BQ), NEG_INF, F32), jnp.zeros((1, BQ), F32), jnp.zeros((HEAD_DIM, BQ), F32))
            st = lax.fori_loop(0, (r + BQ + BK - 1) // BK, kstep, (init, init))
            for g, hs in enumerate(heads):
                m, l, acc = st[g]
                oT_ref[hs, pl.ds(r, BQ)] = (acc * (1.0 / l)).astype(oT_ref.dtype)
                lse_ref[g, :, pl.ds(r, BQ)] = m + jnp.log(l)
            return carry

        lax.fori_loop(0, nq, qstep, 0)

    fT = pl.BlockSpec((LANE, S), lambda i: (i, 0))
    hm = pl.BlockSpec((2, S, HEAD_DIM), lambda i: (i, 0, 0))
    col = pl.BlockSpec((2, S, 1), lambda i: (i, 0, 0))
    rw = pl.BlockSpec((2, 1, S), lambda i: (i, 0, 0))
    return _call_hosting(
        body, comm=comm, grid=(npair,), in_specs=[_fm_spec(LANE, S, q0), hm, _fm_spec(LANE, S, v0), col, rw],
        out_specs=[fT, rw],
        out_shape=[SDS((npair * LANE, S), BF16), SDS((2 * npair, 1, S), F32)], scratch_shapes=[],
        name=name, args=(qT, k_h, vT, ck, cq))


def _foxT_bwd(qT, q_h, k_h, kT, v_h, ck, cq, oT, doT, do_h, lse, name, comm=None):
    (qT, q0), (kT, k0), (doT, d0) = _fm(qT), _fm(kT), _fm(doT)
    S = qT.shape[1]
    npair = k_h.shape[0] // 2
    BQ, BK = min(FOX_BQ, S), min(FOX_BK, S)
    nq = S // BQ
    heads = [slice(g * HEAD_DIM, (g + 1) * HEAD_DIM) for g in range(2)]

    def body(qT_ref, q_ref, k_ref, kT_ref, v_ref, ck_ref, cq_ref, oT_ref, doT_ref, do_ref, lse_ref,
             dqT_ref, dk_ref, dv_ref, dck_ref, dcq_ref, dka, dva, qa_ref):
        qa_ref[:, :, 0:HEAD_DIM] = q_ref[...] * 0.125
        qa_ref[:, :, HEAD_DIM:LANE] = jnp.ones((2, S, LANE - HEAD_DIM), BF16)
        dka[...] = jnp.zeros_like(dka)
        dva[...] = jnp.zeros_like(dva)
        rowi = lax.broadcasted_iota(jnp.int32, (BK, BQ), 0)
        coli = lax.broadcasted_iota(jnp.int32, (BK, BQ), 1)

        def qstep(i, carry):
            r = pl.multiple_of(i * BQ, BQ)
            qs = [qT_ref[hs, pl.ds(r, BQ)] * 0.125 for hs in heads]
            dos = [doT_ref[hs, pl.ds(r, BQ)] for hs in heads]
            deltas = [jnp.sum(dos[g].astype(F32) * oT_ref[hs, pl.ds(r, BQ)].astype(F32), axis=0, keepdims=True)
                      for g, hs in enumerate(heads)]
            cqs = [cq_ref[g, :, pl.ds(r, BQ)] for g in range(2)]
            lses = [lse_ref[g, :, pl.ds(r, BQ)] for g in range(2)]

            def kstep(j, st):
                c = pl.multiple_of(j * BK, BK)
                new = []
                for g, hs in enumerate(heads):
                    dq, rs = st[g]
                    s = _foxT_logits(k_ref[g, pl.ds(c, BK), :], qs[g], cqs[g], ck_ref[g, pl.ds(c, BK), :],
                                     r, c, rowi, coli)
                    p = jnp.exp(s - lses[g])
                    dp = jnp.dot(v_ref[g, pl.ds(c, BK), :], dos[g], preferred_element_type=F32)
                    ds = p * (dp - deltas[g])
                    dsb = ds.astype(BF16)
                    dka[g, pl.ds(c, BK), :] += jnp.dot(dsb, qa_ref[g, pl.ds(r, BQ), :], preferred_element_type=F32)
                    dva[g, pl.ds(c, BK), :] += jnp.dot(p.astype(BF16), do_ref[g, pl.ds(r, BQ), :],
                                                      preferred_element_type=F32)
                    new.append((dq + jnp.dot(kT_ref[hs, pl.ds(c, BK)], dsb, preferred_element_type=F32),
                                rs + jnp.sum(dsb.astype(F32), axis=0, keepdims=True)))
                return tuple(new)

            init = (jnp.zeros((HEAD_DIM, BQ), F32), jnp.zeros((1, BQ), F32))
            st = lax.fori_loop(0, (r + BQ + BK - 1) // BK, kstep, (init, init))
            for g, hs in enumerate(heads):
                dqT_ref[hs, pl.ds(r, BQ)] = (st[g][0] * 0.125).astype(dqT_ref.dtype)
                dcq_ref[g, :, pl.ds(r, BQ)] = st[g][1]
            return carry

        lax.fori_loop(0, nq, qstep, 0)
        dk_ref[...] = dka[:, :, 0:HEAD_DIM].astype(dk_ref.dtype)
        dck_ref[...] = -dka[:, :, HEAD_DIM:HEAD_DIM + 1]
        dv_ref[...] = dva[...].astype(dv_ref.dtype)

    fT = pl.BlockSpec((LANE, S), lambda i: (i, 0))
    hm = pl.BlockSpec((2, S, HEAD_DIM), lambda i: (i, 0, 0))
    col = pl.BlockSpec((2, S, 1), lambda i: (i, 0, 0))
    rw = pl.BlockSpec((2, 1, S), lambda i: (i, 0, 0))
    nh = 2 * npair
    return _call_hosting(
        body, comm=comm, grid=(npair,),
        in_specs=[_fm_spec(LANE, S, q0), hm, hm, _fm_spec(LANE, S, k0), hm, col, rw, fT, _fm_spec(LANE, S, d0), hm, rw],
        out_specs=[fT, hm, hm, col, rw],
        out_shape=[SDS((npair * LANE, S), BF16), SDS((nh, S, HEAD_DIM), BF16), SDS((nh, S, HEAD_DIM), BF16),
                   SDS((nh, S, 1), F32), SDS((nh, 1, S), F32)],
        scratch_shapes=[pltpu.VMEM((2, S, LANE), F32), pltpu.VMEM((2, S, HEAD_DIM), F32),
                        pltpu.VMEM((2, S, LANE), BF16)],
        name=name, args=(qT, q_h, k_h, kT, v_h, ck, cq, oT, doT, do_h, lse))


def _split3(x):
    hi = x.astype(BF16)
    r1 = x - hi.astype(F32)
    mid = r1.astype(BF16)
    lo = (r1 - mid.astype(F32)).astype(BF16)
    return hi, mid, lo


def _tri_dot(tri, x):
    hi, mid, lo = _split3(x)
    return (jnp.dot(tri, hi, preferred_element_type=F32) + jnp.dot(tri, mid, preferred_element_type=F32)
            + jnp.dot(tri, lo, preferred_element_type=F32))


def _fox_cum(gf, bfo, name):
    S = gf.shape[0]
    nb = S // LANE
    fcol = (GF_COLS - LANE) // LANE

    def body(f_ref, b_ref, cum_ref):
        row = lax.broadcasted_iota(jnp.int32, (LANE, LANE), 0)
        col = lax.broadcasted_iota(jnp.int32, (LANE, LANE), 1)
        tri = jnp.where(row >= col, 1.0, 0.0).astype(BF16)
        carry = jnp.zeros((1, LANE), F32)
        for t in range(nb):
            xl = f_ref[t * LANE:(t + 1) * LANE, :] + b_ref[...]
            lf = jnp.minimum(xl, 0.0) - jnp.log(1.0 + jnp.exp(-jnp.abs(xl)))
            cblk = _tri_dot(tri, lf) + carry
            cum_ref[t * LANE:(t + 1) * LANE, :] = cblk
            carry = cblk[LANE - 1:LANE, :]

    return pl.pallas_call(
        body, grid=(1,), in_specs=[pl.BlockSpec((S, LANE), lambda i: (0, fcol)), _vec(LANE)],
        out_specs=pl.BlockSpec((S, LANE), lambda i: (0, 0)), out_shape=SDS((S, LANE), F32),
        compiler_params=_cparams(("arbitrary",)), name=name)(gf, bfo)


def _fox_cum_bwd(gf, bfo, dcum, name):
    S = gf.shape[0]
    nb = S // LANE
    fcol = (GF_COLS - LANE) // LANE

    def body(f_ref, b_ref, dc_ref, df_ref, db_ref):
        row = lax.broadcasted_iota(jnp.int32, (LANE, LANE), 0)
        col = lax.broadcasted_iota(jnp.int32, (LANE, LANE), 1)
        tri = jnp.where(row <= col, 1.0, 0.0).astype(BF16)
        carry = jnp.zeros((1, LANE), F32)
        tot = jnp.zeros((1, LANE), F32)
        for t in range(nb - 1, -1, -1):
            rows = slice(t * LANE, (t + 1) * LANE)
            dlf = _tri_dot(tri, dc_ref[rows, :]) + carry
            carry = dlf[0:1, :]
            xl = f_ref[rows, :] + b_ref[...]
            dfl = dlf * (1.0 / (1.0 + jnp.exp(xl)))
            df_ref[rows, :] = dfl.astype(df_ref.dtype)
            tot = tot + jnp.sum(dfl, axis=0, keepdims=True)
        db_ref[...] = tot

    return pl.pallas_call(
        body, grid=(1,),
        in_specs=[pl.BlockSpec((S, LANE), lambda i: (0, fcol)), _vec(LANE), pl.BlockSpec((S, LANE), lambda i: (0, 0))],
        out_specs=[pl.BlockSpec((S, LANE), lambda i: (0, 0)), _vec(LANE)],
        out_shape=[SDS((S, LANE), BF16), SDS((1, LANE), F32)],
        compiler_params=_cparams(("arbitrary",)), name=name)(gf, bfo, dcum)


REL_FAR = C_PREV * CHUNK - REL_CLIP


def _rel_onehot(qi, band):
    w = band - REL_FAR
    r = lax.broadcasted_iota(jnp.int32, (N_REL_PAD, w), 0)
    j = lax.broadcasted_iota(jnp.int32, (N_REL_PAD, w), 1) + REL_FAR
    idx = jnp.clip(C_PREV * CHUNK + qi - j, -REL_CLIP, REL_CLIP) + REL_CLIP
    return jnp.where(r == idx, 1.0, 0.0).astype(BF16)


def _rel_expand(rel, name):
    band = (C_PREV + 1) * CHUNK

    def body(rel_ref, o_ref):
        hi, mid, lo = _split3(rel_ref[...])
        far = jnp.broadcast_to(rel_ref[:, 2 * REL_CLIP:2 * REL_CLIP + 1], (N_HEADS, REL_FAR))

        def row(qi, carry):
            oh = _rel_onehot(qi, band)
            o_ref[qi, :, 0:REL_FAR] = far
            o_ref[qi, :, REL_FAR:band] = (jnp.dot(hi, oh, preferred_element_type=F32)
                                          + jnp.dot(mid, oh, preferred_element_type=F32)
                                          + jnp.dot(lo, oh, preferred_element_type=F32))
            return carry

        lax.fori_loop(0, CHUNK, row, 0, unroll=2)

    return pl.pallas_call(
        body, grid=(1,), in_specs=[pl.BlockSpec((N_HEADS, N_REL_PAD), lambda i: (0, 0))],
        out_specs=pl.BlockSpec((CHUNK, N_HEADS, band), lambda i: (0, 0, 0)),
        out_shape=SDS((CHUNK, N_HEADS, band), F32),
        compiler_params=_cparams(("arbitrary",)), name=name)(rel)


def _rel_reduce(dbias, name):
    band = (C_PREV + 1) * CHUNK
    NT = (((1,), (1,)), ((), ()))

    def body(d_ref, o_ref):
        def row(qi, st):
            acc, far = st
            oh = _rel_onehot(qi, band)
            hi, mid, lo = _split3(d_ref[qi, :, REL_FAR:band])
            acc = acc + (lax.dot_general(hi, oh, NT, preferred_element_type=F32)
                         + lax.dot_general(mid, oh, NT, preferred_element_type=F32)
                         + lax.dot_general(lo, oh, NT, preferred_element_type=F32))
            return acc, far + jnp.sum(d_ref[qi, :, 0:REL_FAR], axis=-1, keepdims=True)

        acc, far = lax.fori_loop(0, CHUNK, row, (jnp.zeros((N_HEADS, N_REL_PAD), F32), jnp.zeros((N_HEADS, 1), F32)),
                                 unroll=2)
        col = lax.broadcasted_iota(jnp.int32, (N_HEADS, N_REL_PAD), 1)
        o_ref[...] = acc + jnp.where(col == 2 * REL_CLIP, far, 0.0)

    return pl.pallas_call(
        body, grid=(1,), in_specs=[pl.BlockSpec((CHUNK, N_HEADS, band), lambda i: (0, 0, 0))],
        out_specs=pl.BlockSpec((N_HEADS, N_REL_PAD), lambda i: (0, 0)),
        out_shape=SDS((N_HEADS, N_REL_PAD), F32),
        compiler_params=_cparams(("arbitrary",)), name=name)(dbias)


def _alibi_table():
    qi = np.arange(CHUNK)[:, None]
    j = np.arange((A_PREV + 1) * CHUNK)[None, :]
    dist = np.abs(A_PREV * CHUNK + qi - j).astype(np.float32)
    slopes = np.exp2(-8.0 * np.arange(1, N_HEADS + 1, dtype=np.float32) / N_HEADS).astype(np.float32)
    return jnp.asarray(-slopes[:, None, None] * dist[None])


def _ada_fwd(c_all, w, b, name):
    n = w.shape[2]

    def body(c_ref, w_ref, b_ref, o_ref):
        cv = c_ref[...]
        cond = (cv * _sigmoid(cv)).astype(BF16)
        o_ref[0] = jnp.dot(cond, w_ref[0].astype(BF16), preferred_element_type=F32) + b_ref[0]

    return pl.pallas_call(
        body, grid=(DEPTH,),
        in_specs=[pl.BlockSpec((16, D_MODEL), lambda l: (0, 0)), pl.BlockSpec((1, D_MODEL, n), lambda l: (l, 0, 0)),
                  pl.BlockSpec((1, 1, n), lambda l: (l, 0, 0))],
        out_specs=pl.BlockSpec((1, 16, n), lambda l: (l, 0, 0)), out_shape=SDS((DEPTH, 16, n), F32),
        compiler_params=_cparams(("parallel",)), name=name)(c_all, w, b)


def _ada_bwd(c_t, dmod, name, comm=None):
    n = dmod.shape[2]
    bn = _blk(n, 512)
    tr = 256

    def body(c_ref, d_ref, o_ref):
        cv = c_ref[...]
        cond = (cv * _sigmoid(cv)).astype(BF16).astype(F32)
        dm = d_ref[0].astype(BF16).astype(F32)
        acc = cond[:, 0:1] * dm[0:1, :]
        for b_ in range(1, 8):
            acc = acc + cond[:, b_:b_ + 1] * dm[b_:b_ + 1, :]
        o_ref[0] = acc

    res, got = _call_hosting(
        body, comm=comm, grid=(DEPTH, D_MODEL // tr, n // bn),
        in_specs=[pl.BlockSpec((tr, 8), lambda l, i, j: (i, 0)), pl.BlockSpec((1, 8, bn), lambda l, i, j: (l, 0, j))],
        out_specs=[pl.BlockSpec((1, tr, bn), lambda l, i, j: (l, i, j))], out_shape=[SDS((DEPTH, D_MODEL, n), F32)],
        scratch_shapes=[], name=name, args=(c_t, dmod))
    return res[0], got


def _adamw(w, m, v, parts, name):
    L, R, C = w.shape
    per_layer = isinstance(parts, (list, tuple))
    plist = list(parts) if per_layer else [parts]
    P = plist[0].shape[0]
    tr = _blk_rows(R, max(16, (1 << 18) // C))
    nr = R // tr
    c1 = 1.0 - ADAM_B1 ** ADAM_STEP
    c2 = 1.0 - ADAM_B2 ** ADAM_STEP

    def total(p_ref):
        g = p_ref[0].astype(F32)
        for k in range(1, P):
            g = g + p_ref[k].astype(F32)
        return g

    def body(w_ref, m_ref, v_ref, *rest):
        p_refs, (g_ref, d_ref, nm_ref, nv_ref) = rest[:len(plist)], rest[len(plist):]
        g = total(p_refs[0])
        for k in range(1, len(plist)):
            g = jnp.where(pl.program_id(0) == k, total(p_refs[k]), g)
        mn = ADAM_B1 * m_ref[0] + (1.0 - ADAM_B1) * g
        vn = ADAM_B2 * v_ref[0] + (1.0 - ADAM_B2) * (g * g)
        m_hat = mn / c1
        v_hat = vn / c2
        g_ref[0] = g
        nm_ref[0] = mn
        nv_ref[0] = vn
        d_ref[0] = -ADAM_LR * (m_hat / (jnp.sqrt(v_hat) + ADAM_EPS) + ADAM_WD * w_ref[0])

    rs = pl.BlockSpec((1, tr, C), lambda l, i: (l, i, 0))
    if per_layer:
        def layer_spec(k):
            return pl.BlockSpec((P, tr, C), lambda l, i: (0, jnp.where(l == k, i, 0), 0))
        pspecs = [layer_spec(k) for k in range(L)]
    else:
        pspecs = [pl.BlockSpec((P, tr, C), lambda l, i: (0, l * nr + i, 0))]
    return pl.pallas_call(
        body, grid=(L, nr), in_specs=[rs, rs, rs] + pspecs, out_specs=[rs, rs, rs, rs],
        out_shape=[SDS((L, R, C), F32)] * 4, compiler_params=_cparams(("parallel", "parallel")),
        name=name)(w, m, v, *plist)


def _blk_rows(R, cap):
    if R <= cap:
        return R
    best = None
    for t in range(16, cap + 1, 16):
        if R % t == 0:
            best = t
    assert best is not None, (R, cap)
    return best


def _add_cast_rows(g, t, name):
    Q, R, C = g.shape
    half = R // 2
    tr = _blk_rows(half, max(16, (1 << 19) // C))
    nb = half // tr

    def body(lo_ref, hi_ref, t_ref, o_ref):
        c = lax.axis_index("c")

        @pl.when(c == 0)
        def _():
            o_ref[...] = (lo_ref[...].astype(F32) + t_ref[...].astype(F32)).astype(o_ref.dtype)

        @pl.when(c == 1)
        def _():
            o_ref[...] = (hi_ref[...].astype(F32) + t_ref[...].astype(F32)).astype(o_ref.dtype)

    bs = pl.BlockSpec((1, tr, C), lambda q, i: (q, i, 0))
    hi = pl.BlockSpec((1, tr, C), lambda q, i: (q, nb + i, 0))
    return pl.pallas_call(
        body, grid=(Q, nb), in_specs=[bs, hi, bs], out_specs=bs, out_shape=SDS((Q, half, C), BF16),
        compiler_params=_cparams(("parallel", "parallel")), name=name)(g, g, t)


def _coords():
    return lax.axis_index("x"), lax.axis_index("y"), lax.axis_index("c")


def _flip(v, bit):
    return 1 - v if bit else v


def _all_gather8(v, name):
    R = v.shape[0]

    def body(v_ref, o_ref, send_sems, recv_sems):
        x, y, c = _coords()
        me = 4 * x + 2 * y + c
        o_ref[me] = v_ref[...]
        copies = []
        for k in range(1, 8):
            peer = (_flip(x, k & 4), _flip(y, k & 2), _flip(c, k & 1))
            cp = pltpu.make_async_remote_copy(
                src_ref=v_ref, dst_ref=o_ref.at[me], send_sem=send_sems.at[k - 1], recv_sem=recv_sems.at[k - 1],
                device_id=peer, device_id_type=MESH)
            cp.start()
            copies.append(cp)
        for cp in copies:
            cp.wait_recv()
        for cp in copies:
            cp.wait_send()

    return pl.pallas_call(
        body, in_specs=[VMEM_SPEC], out_specs=VMEM_SPEC, out_shape=SDS((8, R, LANE), v.dtype),
        scratch_shapes=[pltpu.SemaphoreType.DMA((7,)), pltpu.SemaphoreType.DMA((7,))],
        compiler_params=pltpu.CompilerParams(vmem_limit_bytes=VMEM_LIMIT), name=name)(v)


def _sibling_swap_rows(arrs, name):
    n = len(arrs)

    def body(*refs):
        in_refs, out_refs = refs[:n], refs[n:2 * n]
        send_sems, recv_sems = refs[2 * n:]
        x, y, c = _coords()
        copies = []
        for a in range(n):
            Q, R = in_refs[a].shape[0], in_refs[a].shape[1]
            half = R // 2
            src = in_refs[a].at[pl.ds(0, Q), pl.ds(pl.multiple_of((1 - c) * half, 16), half)]
            cp = pltpu.make_async_remote_copy(
                src_ref=src, dst_ref=out_refs[a], send_sem=send_sems.at[a], recv_sem=recv_sems.at[a],
                device_id=(x, y, 1 - c), device_id_type=MESH)
            cp.start()
            copies.append(cp)
        for cp in copies:
            cp.wait_recv()
        for cp in copies:
            cp.wait_send()

    return pl.pallas_call(
        body, in_specs=[ANY] * n, out_specs=[ANY] * n,
        out_shape=[SDS((a.shape[0], a.shape[1] // 2, a.shape[2]), a.dtype) for a in arrs],
        scratch_shapes=[pltpu.SemaphoreType.DMA((n,)), pltpu.SemaphoreType.DMA((n,))],
        name=name)(*arrs)


class _OwnerReduce:
    aliased = False

    def __init__(self, srcs, lay):
        self.srcs, self.lay, self.n = list(srcs), lay, len(srcs)
        self.out_shapes = [SDS(a.shape, a.dtype) for a in self.srcs]
        self.sem_shapes = [pltpu.SemaphoreType.DMA((self.n, 3)), pltpu.SemaphoreType.DMA((self.n, 3)),
                           pltpu.SemaphoreType.DMA((self.n,))]

    def _copies(self, src_refs, dst_refs, sems):
        ici_send, ici_recv, loc_sem = sems
        x, y, c = _coords()
        p = 2 * x + y
        local, remote = [], []
        for a in range(self.n):
            local.append(pltpu.make_async_copy(src_refs[a].at[p], dst_refs[a].at[p], loc_sem.at[a]))
            for k in range(1, 4):
                qx, qy = _flip(x, k & 2), _flip(y, k & 1)
                remote.append(pltpu.make_async_remote_copy(
                    src_ref=src_refs[a].at[2 * qx + qy], dst_ref=dst_refs[a].at[p], send_sem=ici_send.at[a, k - 1],
                    recv_sem=ici_recv.at[a, k - 1], device_id=(qx, qy, self.lay), device_id_type=MESH))
        return c, local, remote

    def start(self, src_refs, dst_refs, sems):
        c, local, remote = self._copies(src_refs, dst_refs, sems)

        @pl.when(c == self.lay)
        def _():
            for cp in local + remote:
                cp.start()

    def finish(self, src_refs, dst_refs, sems):
        c, local, remote = self._copies(src_refs, dst_refs, sems)

        @pl.when(c == self.lay)
        def _():
            for cp in remote:
                cp.wait_recv()
            for cp in remote:
                cp.wait_send()
            for cp in local:
                cp.wait()


def _call_hosting(body, *, comm, grid, in_specs, out_specs, out_shape, scratch_shapes, name, args, semantics=None):
    n_in, n_out, n_scr = len(args), len(out_shape), len(scratch_shapes)
    if comm is None:
        sem = semantics if semantics is not None else ("parallel",) * len(grid)
        res = pl.pallas_call(body, grid=grid, in_specs=in_specs, out_specs=out_specs, out_shape=out_shape,
                             scratch_shapes=scratch_shapes, compiler_params=_cparams(sem), name=name)(*args)
        return list(res), None
    k = comm.n

    def hosted(*refs):
        ins, cin = refs[:n_in], refs[n_in:n_in + k]
        outs = refs[n_in + k:n_in + k + n_out]
        cout = refs[n_in + k + n_out:n_in + 2 * k + n_out]
        scr = refs[n_in + 2 * k + n_out:n_in + 2 * k + n_out + n_scr]
        sems = refs[n_in + 2 * k + n_out + n_scr:]
        first = pl.program_id(0) == 0
        last = pl.program_id(0) == grid[0] - 1
        for d in range(1, len(grid)):
            first = jnp.logical_and(first, pl.program_id(d) == 0)
            last = jnp.logical_and(last, pl.program_id(d) == grid[d] - 1)

        @pl.when(first)
        def _():
            comm.start(cin, cout, sems)

        body(*ins, *outs, *scr)

        @pl.when(last)
        def _():
            comm.finish(cin, cout, sems)

    aliases = {n_in + j: n_out + j for j in range(k)} if comm.aliased else {}
    res = pl.pallas_call(
        hosted, grid=grid, in_specs=list(in_specs) + [ANY] * k, out_specs=list(out_specs) + [ANY] * k,
        out_shape=list(out_shape) + comm.out_shapes, scratch_shapes=list(scratch_shapes) + comm.sem_shapes,
        input_output_aliases=aliases, compiler_params=_cparams(("arbitrary",) * len(grid)),
        name=name)(*args, *comm.srcs)
    return list(res[:n_out]), list(res[n_out:])


class _RowHalfGather:
    aliased = False

    def __init__(self, srcs):
        self.srcs, self.n = list(srcs), len(srcs)
        self.out_shapes = [SDS((4,) + a.shape, a.dtype) for a in self.srcs]
        n = self.n
        self.sem_shapes = [pltpu.SemaphoreType.DMA((n, 3)), pltpu.SemaphoreType.DMA((n, 3)),
                           pltpu.SemaphoreType.DMA((n, 3)), pltpu.SemaphoreType.DMA((n, 3)),
                           pltpu.SemaphoreType.DMA((n,))]

    def _copies(self, src_refs, dst_refs, sems):
        ici_send, ici_recv, d2d_send, d2d_recv, loc_sem = sems
        x, y, c = _coords()
        p = 2 * x + y
        local, first, fwd = [], [], []
        for a in range(self.n):
            R = src_refs[a].shape[0] // 2
            half = pl.ds(pl.multiple_of(c * R, 16), R)
            local.append(pltpu.make_async_copy(src_refs[a], dst_refs[a].at[p], loc_sem.at[a]))
            for k in range(1, 4):
                qx, qy = _flip(x, k & 2), _flip(y, k & 1)
                first.append(pltpu.make_async_remote_copy(
                    src_ref=src_refs[a].at[half], dst_ref=dst_refs[a].at[p, half], send_sem=ici_send.at[a, k - 1],
                    recv_sem=ici_recv.at[a, k - 1], device_id=(qx, qy, c), device_id_type=MESH))
                slot = dst_refs[a].at[2 * qx + qy, half]
                fwd.append(pltpu.make_async_remote_copy(
                    src_ref=slot, dst_ref=slot, send_sem=d2d_send.at[a, k - 1], recv_sem=d2d_recv.at[a, k - 1],
                    device_id=(x, y, 1 - c), device_id_type=MESH))
        return local, first, fwd

    def start(self, src_refs, dst_refs, sems):
        local, first, _ = self._copies(src_refs, dst_refs, sems)
        for cp in local + first:
            cp.start()

    def finish(self, src_refs, dst_refs, sems):
        local, first, fwd = self._copies(src_refs, dst_refs, sems)
        for got, on in zip(first, fwd):
            got.wait_recv()
            on.start()
        for cp in fwd:
            cp.wait_recv()
        for cp in first + fwd:
            cp.wait_send()
        for cp in local:
            cp.wait()

    def run(self, name):
        return _run_exchange(self, name)


def _run_exchange(comm, name):
    n = comm.n

    def body(*refs):
        src_refs, dst_refs, sems = refs[:n], refs[n:2 * n], refs[2 * n:]
        comm.start(src_refs, dst_refs, sems)
        comm.finish(src_refs, dst_refs, sems)

    return pl.pallas_call(body, in_specs=[ANY] * n, out_specs=[ANY] * n, out_shape=comm.out_shapes,
                          scratch_shapes=comm.sem_shapes, name=name)(*comm.srcs)


class _RowHalfReduce:
    aliased = False

    def __init__(self, srcs):
        self.srcs, self.n = list(srcs), len(srcs)
        self.out_shapes = [SDS((4, 2 * a.shape[1], a.shape[2]), a.dtype) for a in self.srcs]
        n = self.n
        self.sem_shapes = [pltpu.SemaphoreType.DMA((n, 3)), pltpu.SemaphoreType.DMA((n, 3)),
                           pltpu.SemaphoreType.DMA((n, 4)), pltpu.SemaphoreType.DMA((n, 4)),
                           pltpu.SemaphoreType.DMA((n,))]

    def _copies(self, src_refs, dst_refs, sems):
        ici_send, ici_recv, d2d_send, d2d_recv, loc_sem = sems
        x, y, c = _coords()
        p = 2 * x + y
        local, first, fwd = [], [], []
        for a in range(self.n):
            R = src_refs[a].shape[1]
            half = pl.ds(pl.multiple_of(c * R, 16), R)
            local.append(pltpu.make_async_copy(src_refs[a].at[p], dst_refs[a].at[p, half], loc_sem.at[a]))
            for k in range(4):
                qx, qy = _flip(x, k & 2), _flip(y, k & 1)
                if k:
                    first.append(pltpu.make_async_remote_copy(
                        src_ref=src_refs[a].at[2 * qx + qy], dst_ref=dst_refs[a].at[p, half],
                        send_sem=ici_send.at[a, k - 1], recv_sem=ici_recv.at[a, k - 1], device_id=(qx, qy, c),
                        device_id_type=MESH))
                slot = dst_refs[a].at[2 * qx + qy, half]
                fwd.append(pltpu.make_async_remote_copy(
                    src_ref=slot, dst_ref=slot, send_sem=d2d_send.at[a, k], recv_sem=d2d_recv.at[a, k],
                    device_id=(x, y, 1 - c), device_id_type=MESH))
        return local, first, fwd

    def start(self, src_refs, dst_refs, sems):
        local, first, _ = self._copies(src_refs, dst_refs, sems)
        for cp in local + first:
            cp.start()

    def finish(self, src_refs, dst_refs, sems):
        local, first, fwd = self._copies(src_refs, dst_refs, sems)
        for a in range(self.n):
            local[a].wait()
            fwd[4 * a].start()
            for k in range(1, 4):
                first[3 * a + k - 1].wait_recv()
                fwd[4 * a + k].start()
        for cp in fwd:
            cp.wait_recv()
        for cp in first + fwd:
            cp.wait_send()

    def run(self, name):
        return _run_exchange(self, name)


class _SiblingSend:
    aliased = False

    def __init__(self, srcs, src_core):
        self.srcs, self.src_core, self.n = list(srcs), src_core, len(srcs)
        self.out_shapes = [SDS(a.shape, a.dtype) for a in self.srcs]
        self.sem_shapes = [pltpu.SemaphoreType.DMA((self.n,)), pltpu.SemaphoreType.DMA((self.n,))]

    def _copies(self, src_refs, dst_refs, sems):
        x, y, c = _coords()
        return c, [pltpu.make_async_remote_copy(
            src_ref=src_refs[a], dst_ref=dst_refs[a], send_sem=sems[0].at[a], recv_sem=sems[1].at[a],
            device_id=(x, y, 1 - c), device_id_type=MESH) for a in range(self.n)]

    def start(self, src_refs, dst_refs, sems):
        c, copies = self._copies(src_refs, dst_refs, sems)

        @pl.when(c == self.src_core)
        def _():
            for cp in copies:
                cp.start()

    def finish(self, src_refs, dst_refs, sems):
        c, copies = self._copies(src_refs, dst_refs, sems)

        @pl.when(c == self.src_core)
        def _():
            for cp in copies:
                cp.wait_send()

        @pl.when(c != self.src_core)
        def _():
            for cp in copies:
                cp.wait_recv()


class _Handoff:
    aliased = True

    def __init__(self, srcs, lay, slots):
        self.srcs, self.lay, self.slots, self.n = list(srcs), lay, tuple(slots), len(srcs)
        self.out_shapes = [SDS(a.shape, a.dtype) for a in self.srcs]
        ns = len(self.slots)
        self.sem_shapes = [pltpu.SemaphoreType.DMA((self.n, ns)), pltpu.SemaphoreType.DMA((self.n, ns))]

    def _copies(self, dst_refs, sems):
        x, y, c = _coords()
        copies = []
        for a in range(self.n):
            for j, k in enumerate(self.slots):
                slot = dst_refs[a].at[2 * _flip(x, k & 2) + _flip(y, k & 1)]
                copies.append(pltpu.make_async_remote_copy(
                    src_ref=slot, dst_ref=slot, send_sem=sems[0].at[a, j], recv_sem=sems[1].at[a, j],
                    device_id=(x, y, 1 - c), device_id_type=MESH))
        return c, copies

    def start(self, src_refs, dst_refs, sems):
        c, copies = self._copies(dst_refs, sems)

        @pl.when(c == self.lay)
        def _():
            for cp in copies:
                cp.start()

    def finish(self, src_refs, dst_refs, sems):
        c, copies = self._copies(dst_refs, sems)

        @pl.when(c == self.lay)
        def _():
            for cp in copies:
                cp.wait_send()

        @pl.when(c != self.lay)
        def _():
            for cp in copies:
                cp.wait_recv()


def _add_cast_on(a, b, lay, name):
    Q, R, C = b.shape
    tr = _blk_rows(R, max(16, (1 << 19) // C))

    def body(a_ref, b_ref, o_ref):
        @pl.when(lax.axis_index("c") == lay)
        def _():
            o_ref[...] = (a_ref[...].astype(F32) + b_ref[...].astype(F32)).astype(o_ref.dtype)

    bs = pl.BlockSpec((1, tr, C), lambda q, i: (q, i, 0))
    return pl.pallas_call(
        body, grid=(Q, R // tr), in_specs=[bs, bs], out_specs=bs, out_shape=SDS((Q, R, C), BF16),
        compiler_params=_cparams(("parallel", "parallel")), name=name)(a, b)


_IN_SIZES = (512, 128, 128, 512, 512, 512, 8, 512, 512, 512, 3072)
_IN_OFF = tuple(int(v) for v in np.cumsum((0,) + _IN_SIZES))
_IN_Q = N_IN_COLS // 4


def _pack_w_in(w):
    def cols(lo, hi):
        out = []
        while lo < hi:
            q, off = divmod(lo, _IN_Q)
            n = min(hi - lo, _IN_Q - off)
            out.append(w[q, :, off:off + n])
            lo += n
        return out

    fb0, fb1, g0 = _IN_OFF[6], _IN_OFF[7], _IN_OFF[10]
    wqkv = jnp.concatenate(cols(0, fb0) + cols(fb1, g0), axis=1)
    wgf = jnp.concatenate(cols(g0, N_IN_COLS) + cols(fb0, fb1) + [jnp.zeros((w.shape[1], LANE - 8), w.dtype)], axis=1)
    return wqkv, wgf


def _unpack_w_in(dqkv, dgf):
    fb0, fb1, g0 = _IN_OFF[6], _IN_OFF[7], _IN_OFF[10]

    def cols(lo, hi):
        out = []
        while lo < hi:
            if lo < fb0:
                n = min(hi, fb0) - lo
                out.append(dqkv[:, lo:lo + n])
            elif lo < fb1:
                n = min(hi, fb1) - lo
                out.append(dgf[:, 3072 + lo - fb0:3072 + lo - fb0 + n])
            elif lo < g0:
                n = min(hi, g0) - lo
                out.append(dqkv[:, lo - 8:lo - 8 + n])
            else:
                n = hi - lo
                out.append(dgf[:, lo - g0:lo - g0 + n])
            lo += n
        return out

    return jnp.stack([jnp.concatenate(cols(q * _IN_Q, (q + 1) * _IN_Q), axis=1) for q in range(4)])


def _pad_rows(a, rows):
    return jnp.pad(a, ((0, rows - a.shape[0]), (0, 0)))


def _small_pack(parts):
    flat = jnp.concatenate([p.reshape(-1) for p in parts])
    n = flat.shape[0]
    rows = -(-n // LANE)
    rows = -(-rows // 8) * 8
    return jnp.pad(flat, (0, rows * LANE - n)).reshape(rows, LANE)


def _small_unpack(block, shapes):
    flat = block.reshape(-1)
    out, off = [], 0
    for s in shapes:
        n = int(np.prod(s))
        out.append(flat[off:off + n].reshape(s))
        off += n
    return out


def _kv_same(g):
    return 0


def _kv_own(g):
    return g


_mm_plain = _mm


def _mm_hosting(a, b, *, comm, **kw):
    if comm is None:
        return _mm(a, b, **kw), None
    return _mm(a, b, comm=comm, **kw)


def _layer_fwd(x, mod, p, l, ride):
    sh_m, sc_m, g_m, sh_f, sc_f, g_f = mod
    nm = "l%d_" % l

    def carried(name, run):
        res, got = run(ride.comm_for(name))
        if got is not None:
            ride.done(name, got)
        return res

    h1 = _norm_mod_fwd(x, p["norm_mix_g"], sc_m, sh_m, nm + "norm_mix_fwd")
    qkv = carried("proj_qkv", lambda cm: _mm_hosting(h1, p["wqkv"], mode="nn", out_dtype=BF16,
                                                     name=nm + "proj_qkv", comm=cm))
    gf = carried("proj_gf", lambda cm: _mm_hosting(h1, p["wgf"], mode="nn", out_dtype=F32, name=nm + "proj_gf",
                                                  cap_n=640, comm=cm))
    qkv_t = qkv.T
    o_a_t = carried("attn_a", lambda cm: _bandT_fwd(
        (qkv_t, 0), _heads(qkv[:, 512:640], A_KV_HEADS), (qkv_t, 640), p["alibi"], p["sink_tab"],
        GQ=4, GK=1, P=A_PREV, kvoff=_kv_same, name=nm + "attn_a_fwd", comm=cm))
    cum = _fox_cum(gf, p["b_forget_pad"], nm + "fox_cum")
    cum_t = cum[:, :N_HEADS].T
    cc, cr = cum_t[:, :, None], cum_t[:, None, :]
    o_b_t, lse_b = carried("attn_b", lambda cm: _foxT_fwd(
        (qkv_t, 768), _heads(qkv[:, 1280:1792], N_HEADS), (qkv_t, 1792), cc, cr, nm + "attn_b_fwd", comm=cm))
    o_c_t = carried("attn_c", lambda cm: _bandT_fwd(
        (qkv_t, 2304), _heads(qkv[:, 2816:3328], N_HEADS), (qkv_t, 3328), p["rel_tab"], p["no_sink"],
        GQ=2, GK=2, P=C_PREV, kvoff=_kv_own, name=nm + "attn_c_fwd", comm=cm))
    p = dict(p, **ride.late_weights())
    o = jnp.concatenate([o_a_t, o_b_t, o_c_t], axis=0).T
    y = _mm(o, p["wb"], mode="nn", out_dtype=BF16, groups=3, name=nm + "branch")
    merged = _merge_fwd(y, gf, nm + "merge_fwd")
    mix = _mm(merged, p["wout"], mode="nn", out_dtype=BF16, name=nm + "out_proj")
    x1 = _resid_fwd(x, mix, g_m, nm + "resid_mix")
    h2 = _norm_mod_fwd(x1, p["norm_ffn_g"], sc_f, sh_f, nm + "norm_ffn_fwd")
    u = carried("ffn_in", lambda cm: _mm_hosting(h2, p["wfi"], mode="nn", out_dtype=BF16, name=nm + "ffn_in",
                                                 cap_n=1408, comm=cm))
    a = _swiglu_fwd(u, nm + "swiglu_fwd")
    f = _mm(a, p["wfo"], mode="nn", out_dtype=BF16, name=nm + "ffn_out")
    x2 = _resid_fwd(x1, f, g_f, nm + "resid_ffn")
    saved = dict(x=x, h1=h1, qkv=qkv, qkv_t=qkv_t, gf=gf, cc=cc, cr=cr, o_b_t=o_b_t, lse_b=lse_b, o=o, y=y, merged=merged,
                 mix=mix, x1=x1, h2=h2, u=u, a=a, f=f)
    return x2, saved, p


def _layer_bwd(dx2, mod, p, s, l, ride=None):
    sh_m, sc_m, g_m, sh_f, sc_f, g_f = mod
    nm = "l%d_" % l

    def _mm(a, b, *, name, **kw):
        comm = ride.comm_for(name) if ride is not None else None
        if comm is None:
            return _mm_plain(a, b, name=nm + name, **kw)
        out, got = _mm_plain(a, b, name=nm + name, comm=comm, **kw)
        ride.done(name, got)
        return out

    dg_f, df = _resid_bwd(dx2, s["f"], g_f, nm + "resid_ffn_bwd")
    da = _mm(df, p["wfo"], mode="nt", out_dtype=BF16, name="ffn_out_dx", cap_n=1408)
    d_wfo = _mm(s["a"], df, mode="tn", out_dtype=BF16, name="ffn_out_dw", cap_m=1408, cap_k=2048)
    du = _swiglu_bwd(da, s["u"], nm + "swiglu_bwd")
    dh2 = _mm(du, p["wfi"], mode="nt", out_dtype=BF16, name="ffn_in_dx")
    d_wfi = _mm(s["h2"], du, mode="tn", out_dtype=BF16, name="ffn_in_dw", cap_m=1024, cap_n=1408, cap_k=2048,
                col_quarters=True)
    dx1, dsc_f, dsh_f, dgn_f = _norm_mod_bwd(s["x1"], [dh2], dx2, p["norm_ffn_g"], sc_f, nm + "norm_ffn_bwd")
    dg_m, dmix = _resid_bwd(dx1, s["mix"], g_m, nm + "resid_mix_bwd")
    dmerged = _mm(dmix, p["wout"], mode="nt", out_dtype=BF16, name="out_proj_dx")
    d_wout = _mm(s["merged"], dmix, mode="tn", out_dtype=BF16, name="out_proj_dw", cap_m=1024, cap_k=2048)
    dy, dgates = _merge_bwd(dmerged, s["y"], s["gf"], nm + "merge_bwd")
    do = _mm(dy, p["wb"], mode="nt", out_dtype=BF16, groups=3, name="branch_dx")
    d_wb = _mm(s["o"], dy, mode="tn", out_dtype=BF16, groups=3, name="branch_dw", cap_k=2048,
               col_quarters=True)
    comms = ride.exchanges() if ride is not None else (None, None, None)
    qkv, qkv_t = s["qkv"], s["qkv_t"]
    do_t = do.T
    (dqa_t, dka_h, dva_h, _, dsink), got_a = _bandT_bwd(
        (qkv_t, 0), _heads(qkv[:, 0:512], N_HEADS), _heads(qkv[:, 512:640], A_KV_HEADS), (qkv_t, 512),
        _heads(qkv[:, 640:768], A_KV_HEADS), (do_t, 0), _heads(do[:, 0:512], N_HEADS), p["alibi"], p["sink_tab"],
        GQ=4, GK=1, P=A_PREV, kvoff=_kv_same, name=nm + "attn_a_bwd", comm=comms[0])
    (dqb_t, dkb_h, dvb_h, dck, dcq), got_b = _foxT_bwd(
        (qkv_t, 768), _heads(qkv[:, 768:1280], N_HEADS), _heads(qkv[:, 1280:1792], N_HEADS), (qkv_t, 1280),
        _heads(qkv[:, 1792:2304], N_HEADS), s["cc"], s["cr"], s["o_b_t"], (do_t, 512),
        _heads(do[:, 512:1024], N_HEADS), s["lse_b"], nm + "attn_b_bwd", comm=comms[1])
    dcum = jnp.pad((dck[:, :, 0] + dcq[:, 0, :]).T, ((0, 0), (0, LANE - N_HEADS)))
    dfb, db_forget = _fox_cum_bwd(s["gf"], p["b_forget_pad"], dcum, nm + "fox_cum_bwd")
    (dqc_t, dkc_h, dvc_h, dbias_c, _), got_c = _bandT_bwd(
        (qkv_t, 2304), _heads(qkv[:, 2304:2816], N_HEADS), _heads(qkv[:, 2816:3328], N_HEADS), (qkv_t, 2816),
        _heads(qkv[:, 3328:3840], N_HEADS), (do_t, 1024), _heads(do[:, 1024:1536], N_HEADS), p["rel_tab"],
        p["no_sink"], GQ=2, GK=2, P=C_PREV, kvoff=_kv_own, name=nm + "attn_c_bwd", comm=comms[2])
    d_rel = _rel_reduce(jnp.transpose(_unpair_table(dbias_c), (1, 0, 2)), nm + "rel_reduce")[:, :N_REL]
    dqkv = jnp.concatenate([dqa_t.T, _unheads(dka_h), _unheads(dva_h), dqb_t.T, _unheads(dkb_h), _unheads(dvb_h),
                            dqc_t.T, _unheads(dkc_h), _unheads(dvc_h)], axis=1)
    dgf = jnp.concatenate([dgates, dfb], axis=1)
    if ride is not None:
        ride.exchanged((got_a, got_b, got_c))
    dh1a = _mm(dqkv, p["wqkv"], mode="nt", out_dtype=BF16, name="proj_qkv_dx", cap_k=1024)
    dh1b = _mm(dgf, p["wgf"], mode="nt", out_dtype=BF16, name="proj_gf_dx", cap_k=640)
    d_wqkv = _mm(s["h1"], dqkv, mode="tn", out_dtype=BF16, name="proj_qkv_dw", cap_m=1024, cap_k=2048)
    d_wgf = _mm(s["h1"], dgf, mode="tn", out_dtype=BF16, name="proj_gf_dw", cap_m=1024, cap_n=640, cap_k=2048)
    dx, dsc_m, dsh_m, dgn_m = _norm_mod_bwd(s["x"], [dh1a, dh1b], dx1, p["norm_mix_g"], sc_m, nm + "norm_mix_bwd")
    d_mod = jnp.concatenate([dsh_m, dsc_m, dg_m, dsh_f, dsc_f, dg_f], axis=1)[0]
    grads = dict(w_in=_unpack_w_in(d_wqkv, d_wgf), w_branch=d_wb, w_out=d_wout.reshape(4, -1, D_MODEL),
                 w_ffn_in=d_wfi, w_ffn_out=d_wfo.reshape(4, -1, D_MODEL),
                 norm_mix_g=dgn_m[0], norm_ffn_g=dgn_f[0], b_forget=db_forget[0, :N_HEADS],
                 sinks=dsink[:, 0, 0], rel_bias=d_rel, d_mod=d_mod)
    return dx, grads


def kernel(x, c, norm_mix_g, norm_ffn_g, w_ada, b_ada, w_in, b_forget, sinks, rel_bias, w_branch, w_out, w_ffn_in, w_ffn_out, final_norm_g, loss_target, m_norm_mix_g, m_norm_ffn_g, m_w_ada, m_b_ada, m_w_in, m_b_forget, m_sinks, m_rel_bias, m_w_branch, m_w_out, m_w_ffn_in, m_w_ffn_out, m_final_norm_g, v_norm_mix_g, v_norm_ffn_g, v_w_ada, v_b_ada, v_w_in, v_b_forget, v_sinks, v_rel_bias, v_w_branch, v_w_out, v_w_ffn_in, v_w_ffn_out, v_final_norm_g):
    xi, yi, ci = _coords()
    chip = 2 * xi + yi
    dev = 2 * chip + ci
    xs = x[0]
    S = xs.shape[0]
    n_ada = w_ada.shape[2]

    big_names = ("w_in", "w_branch", "w_out", "w_ffn_in", "w_ffn_out")
    big_w = dict(w_in=w_in, w_branch=w_branch, w_out=w_out, w_ffn_in=w_ffn_in, w_ffn_out=w_ffn_out)
    big_m = dict(w_in=m_w_in, w_branch=m_w_branch, w_out=m_w_out, w_ffn_in=m_w_ffn_in, w_ffn_out=m_w_ffn_out)
    big_v = dict(w_in=v_w_in, w_branch=v_w_branch, w_out=v_w_out, w_ffn_in=v_w_ffn_in, w_ffn_out=v_w_ffn_out)
    flat2 = lambda a: a.reshape(-1, a.shape[-1])
    shards = [[flat2(big_w[n][l]).astype(BF16) for n in big_names] for l in range(DEPTH)]
    gw = [[None] * (len(big_names) + 2) for _ in range(DEPTH)]
    for l in range(DEPTH):
        shards[l] += [shards[l][0][:D_MODEL // 2], shards[l][0][D_MODEL // 2:]]
    gw[0][0] = _RowHalfGather([shards[0][0]]).run("weights_gather_w_in_l0")[0]
    host_g = ((1, 2, 4), (0,), (3,))

    class WeightRide:
        def __init__(self, l, plan):
            self.l, self.plan = l, plan

        def comm_for(self, name):
            if name not in self.plan:
                return None
            lay, idx = self.plan[name]
            return _RowHalfGather([shards[lay][i] for i in idx])

        def done(self, name, got):
            lay, idx = self.plan[name]
            for i, r in zip(idx, got):
                gw[lay][i] = r

        def late_weights(self):
            g = gw[self.l]
            return dict(wb=jnp.transpose(g[1], (1, 0, 2)).reshape(3 * BRANCH_W, D_MODEL),
                        wout=g[2].reshape(D_MODEL, D_MODEL),
                        wfi=jnp.transpose(g[3], (1, 0, 2)).reshape(D_MODEL, 2 * FFN_H),
                        wfo=g[4].reshape(FFN_H, D_MODEL))

    weight_plan = [
        {"proj_qkv": (0, (1,)), "proj_gf": (0, (2,)), "attn_a": (0, (4,)), "attn_b": (0, (3,)), "attn_c": (1, (5,)),
         "ffn_in": (1, (6,))},
        {"attn_a": (1, (1, 2)), "attn_b": (1, (3,)), "attn_c": (1, (4,))}]


    c_all = _all_gather8(c.reshape(8, LANE), "gather_c").reshape(8, D_MODEL)
    b_sh = lax.dynamic_slice_in_dim(b_ada, chip * n_ada, n_ada, axis=1)[:, None, :]
    mod_sh = _ada_fwd(_pad_rows(c_all, 16), w_ada, b_sh, "ada_fwd")[:, :8, :]
    mod_all = _all_gather8(mod_sh.reshape(-1, LANE), "gather_mod").reshape(8, DEPTH, 8, n_ada)
    mod_mine = lax.dynamic_index_in_dim(mod_all[0::2], dev, axis=2, keepdims=False)
    mod = mod_mine.transpose(1, 0, 2).reshape(DEPTH, 6, D_MODEL)

    alibi = _pair_table(_alibi_table())
    no_sink = jnp.full((N_HEADS, 8, LANE), NEG_INF, F32)
    def make_params(l):
        if gw[l][0] is None:
            gw[l][0] = jnp.concatenate([gw[l][5], gw[l][6]], axis=1)
        wqkv, wgf = _pack_w_in(gw[l][0])
        rel_tab = _rel_expand(jnp.pad(rel_bias[l], ((0, 0), (0, N_REL_PAD - N_REL))), "l%d_rel_expand" % l)
        return dict(
            wqkv=wqkv, wgf=wgf, norm_mix_g=norm_mix_g[l][None], norm_ffn_g=norm_ffn_g[l][None],
            b_forget_pad=jnp.pad(b_forget[l], (0, LANE - N_HEADS))[None],
            sink_tab=jnp.broadcast_to(sinks[l][:, None, None], (N_HEADS, 8, LANE)),
            no_sink=no_sink, alibi=alibi, rel_tab=_pair_table(jnp.transpose(rel_tab, (1, 0, 2))))

    mods = [[mod[l, k][None] for k in range(6)] for l in range(DEPTH)]
    params, saved = [None] * DEPTH, [None] * DEPTH
    h = xs
    for l in range(DEPTH):
        h, saved[l], params[l] = _layer_fwd(h, mods[l], make_params(l), l, WeightRide(l, weight_plan[l]))
    loss_dev, dh, d_final = _final_loss(h, final_norm_g[None], loss_target[0], "final_loss")
    grads = [None] * DEPTH
    dh, grads[1] = _layer_bwd(dh, mods[1], params[1], saved[1], 1)

    class Layer1Ride:
        sends = {"ffn_out_dx": (4,), "ffn_in_dx": (3, 1, 2), "ffn_in_dw": (0,)}
        hands = {"proj_qkv_dx": (0,), "proj_gf_dx": (3,), "proj_gf_dw": (4, 1, 2)}

        def __init__(self, g):
            self.g, self.t = g, [None] * len(g)
            self.parts, self.final = [None] * len(g), [None] * len(g)

        def comm_for(self, name):
            if name in self.sends:
                return _SiblingSend([self.g[i] for i in self.sends[name]], 0)
            if name in self.hands:
                return _Handoff([self.parts[i] for i in self.hands[name]], 1, (0, 1, 2, 3))
            return None

        def done(self, name, got):
            idx, dst = (self.sends[name], self.t) if name in self.sends else (self.hands[name], self.final)
            for i, r in zip(idx, got):
                dst[i] = r

        def exchanges(self):
            sums = [_add_cast_on(a, b, 1, "grads_chip_sum_l1_" + n) for n, a, b in zip(big_names, self.g, self.t)]
            return tuple(_OwnerReduce([sums[i] for i in idx], 1) for idx in host_g)

        def exchanged(self, got):
            for res, idx in zip(got, host_g):
                for r, i in zip(res, idx):
                    self.parts[i] = r

    ride = Layer1Ride([grads[1][n] for n in big_names])
    dh, grads[0] = _layer_bwd(dh, mods[0], params[0], saved[0], 0, ride)
    grad_x = dh[None]
    loss = lax.psum(loss_dev[0, 0], ("x", "y", "c"))
    parts1 = ride.final
    g0 = [grads[0][n] for n in big_names]
    t0 = _sibling_swap_rows(g0, "grads_swap_l0")
    sums0 = [_add_cast_rows(a, b, "grads_chip_sum_l0_" + n) for n, a, b in zip(big_names, g0, t0)]
    parts0 = [None] + list(_RowHalfReduce(sums0[1:]).run("grads_reduce_l0"))

    small_names = ("norm_mix_g", "norm_ffn_g", "b_ada", "b_forget", "sinks", "rel_bias", "final_norm_g")
    small_w = dict(norm_mix_g=norm_mix_g, norm_ffn_g=norm_ffn_g, b_ada=b_ada, b_forget=b_forget, sinks=sinks,
                   rel_bias=rel_bias, final_norm_g=final_norm_g)
    small_m = dict(norm_mix_g=m_norm_mix_g, norm_ffn_g=m_norm_ffn_g, b_ada=m_b_ada, b_forget=m_b_forget,
                   sinks=m_sinks, rel_bias=m_rel_bias, final_norm_g=m_final_norm_g)
    small_v = dict(norm_mix_g=v_norm_mix_g, norm_ffn_g=v_norm_ffn_g, b_ada=v_b_ada, b_forget=v_b_forget,
                   sinks=v_sinks, rel_bias=v_rel_bias, final_norm_g=v_final_norm_g)
    small_g = dict(
        norm_mix_g=jnp.stack([grads[l]["norm_mix_g"] for l in range(DEPTH)]),
        norm_ffn_g=jnp.stack([grads[l]["norm_ffn_g"] for l in range(DEPTH)]),
        b_ada=jnp.stack([grads[l]["d_mod"] for l in range(DEPTH)]),
        b_forget=jnp.stack([grads[l]["b_forget"] for l in range(DEPTH)]),
        sinks=jnp.stack([grads[l]["sinks"] for l in range(DEPTH)]),
        rel_bias=jnp.stack([grads[l]["rel_bias"] for l in range(DEPTH)]),
        final_norm_g=d_final[0])
    shapes = [small_w[n].shape for n in small_names]
    g_all = _all_gather8(_small_pack([small_g[n] for n in small_names]), "gather_small_grads")
    res = _adamw(_small_pack([small_w[n] for n in small_names])[None],
                    _small_pack([small_m[n] for n in small_names])[None],
                    _small_pack([small_v[n] for n in small_names])[None], g_all, "adamw_small")
    small_out = {n: [] for n in small_names}
    for r in res:
        for n, a in zip(small_names, _small_unpack(r[0], shapes)):
            small_out[n].append(a)
    off_b = sum(int(np.prod(s)) for s in shapes[:2])
    n_mod = DEPTH * 6 * D_MODEL
    dmod_all = g_all.reshape(8, -1)[:, off_b:off_b + n_mod].reshape(8, DEPTH, 6 * D_MODEL)
    dmod_sh = lax.dynamic_slice_in_dim(dmod_all, chip * n_ada, n_ada, axis=2).transpose(1, 0, 2)
    g_ada, got = _ada_bwd(c_all.T, dmod_sh, "ada_bwd", comm=_RowHalfReduce(sums0[:1]))
    parts0[0] = got[0]
    ada_out = _adamw(w_ada, m_w_ada, v_w_ada, flat2(g_ada)[None], "adamw_w_ada")

    big_out = {}
    as3 = lambda a: a.reshape(a.shape[0], -1, a.shape[-1])
    for n, p0, p1 in zip(big_names, parts0, parts1):
        res = _adamw(as3(big_w[n]), as3(big_m[n]), as3(big_v[n]), [p0, p1], "adamw_" + n)
        big_out[n] = [r.reshape(big_w[n].shape) for r in res]

    order = ("norm_mix_g", "norm_ffn_g", "w_ada", "b_ada", "w_in", "b_forget", "sinks", "rel_bias", "w_branch",
             "w_out", "w_ffn_in", "w_ffn_out", "final_norm_g")

    def pick(n, k):
        if n == "w_ada":
            return ada_out[k]
        if n in big_out:
            return big_out[n][k]
        return small_out[n][k]

    outs = [loss, grad_x]
    for k in range(4):
        outs += [pick(n, k) for n in order]
    return tuple(outs)
```
